```python
import math
import jax, jax.numpy as jnp
from jax import lax
import numpy as np

D_MODEL = 1024
BATCH = 8
SEQ = 8192
DEPTH = 2

N_META = 16
BLOCK = 128
PAD = BLOCK - N_META

D_RNN = D_MODEL
LRU_BLOCKS = 8
LRU_BS = D_RNN // LRU_BLOCKS
LRU_C = 8.0
CONV_A = 4

N_Q_HEADS = 16
N_KV_HEADS = 2
HEAD_DIM = 64
Q_PER_KV = N_Q_HEADS // N_KV_HEADS
WINDOW = 128
Q_DIM = N_Q_HEADS * HEAD_DIM
KV_DIM = N_KV_HEADS * HEAD_DIM

EVEN_IN = 2 * D_RNN + Q_DIM + 2 * KV_DIM
EVEN_MIX = D_RNN + Q_DIM

D_SSM = 2 * D_MODEL
SSD_HEADDIM = 64
SSD_HEADS = D_SSM // SSD_HEADDIM
SSD_GROUPS = 8
SSD_HPG = SSD_HEADS // SSD_GROUPS
SSD_STATE = 128
CONV_C = 4
SSD_CONV_DIM = D_SSM + 2 * SSD_GROUPS * SSD_STATE
ODD_IN = D_SSM + SSD_CONV_DIM + SSD_HEADS

D_FF = 2816
CONV_F = 3

EPS = 1e-6

kernel_name = "hybrid_rglru_swa_sink_ssd_convffn"


def rms_norm(x, w):
    x32 = x.astype(jnp.float32)
    y = x32 * lax.rsqrt(jnp.mean(x32 * x32, axis=-1, keepdims=True) + EPS)
    return (y * w.astype(jnp.float32)).astype(x.dtype)


def causal_dwconv(x, w, b):
    k = w.shape[0]
    y = lax.conv_general_dilated(
        x, w[:, None, :].astype(x.dtype), window_strides=(1,), padding=[(k - 1, 0)],
        dimension_numbers=("NWC", "WIO", "NWC"), feature_group_count=x.shape[-1])
    return y + b.astype(x.dtype)


def alibi_slopes(n_heads):
    return 2.0 ** (-8.0 * jnp.arange(1, n_heads + 1, dtype=jnp.float32) / n_heads)


def rg_lru(x, w_a, b_a, w_x, b_x, lam):
    bsz, L, _ = x.shape
    x32 = x.astype(jnp.float32)
    xb = x32.reshape(bsz, L, LRU_BLOCKS, LRU_BS)
    r = jax.nn.sigmoid(jnp.einsum("blni,nij->blnj", xb, w_a.astype(jnp.float32)).reshape(bsz, L, D_RNN) + b_a)
    i = jax.nn.sigmoid(jnp.einsum("blni,nij->blnj", xb, w_x.astype(jnp.float32)).reshape(bsz, L, D_RNN) + b_x)
    log_a = -LRU_C * r * jax.nn.softplus(-lam.astype(jnp.float32))
    a = jnp.exp(log_a)
    u = jnp.sqrt(-jnp.expm1(2.0 * log_a)) * (i * x32)

    def combine(c1, c2):
        a1, b1 = c1
        a2, b2 = c2
        return a1 * a2, a2 * b1 + b2

    _, h = lax.associative_scan(combine, (a, u), axis=1)
    return h.astype(x.dtype)


def swa_sink_alibi(q, k, v, sinks):
    bsz, L = q.shape[:2]
    Lp = L + PAD
    nblk = Lp // BLOCK
    q = q.astype(jnp.float32)
    k = k.astype(jnp.float32)
    v = v.astype(jnp.float32)
    padw = ((0, 0), (PAD, 0), (0, 0), (0, 0))
    qb = jnp.pad(q, padw).reshape(bsz, nblk, BLOCK, N_KV_HEADS, Q_PER_KV, HEAD_DIM)
    kb = jnp.pad(k, padw).reshape(bsz, nblk, BLOCK, N_KV_HEADS, HEAD_DIM)
    vb = jnp.pad(v, padw).reshape(bsz, nblk, BLOCK, N_KV_HEADS, HEAD_DIM)
    shift = ((0, 0), (1, 0), (0, 0), (0, 0), (0, 0))
    k_band = jnp.concatenate([jnp.pad(kb, shift)[:, :-1], kb], axis=2)
    v_band = jnp.concatenate([jnp.pad(vb, shift)[:, :-1], vb], axis=2)
    k_meta = k[:, :N_META]
    v_meta = v[:, :N_META]
    scale = HEAD_DIM ** -0.5
    s_band = jnp.einsum("bnqkgd,bnskd->bnkgqs", qb, k_band) * scale
    s_meta = jnp.einsum("bnqkgd,bmkd->bnkgqm", qb, k_meta) * scale

    blk = jnp.arange(nblk)
    t = blk[:, None] * BLOCK + jnp.arange(BLOCK)[None, :] - PAD
    s = (blk[:, None] - 1) * BLOCK + jnp.arange(2 * BLOCK)[None, :] - PAD
    dist_band = t[:, :, None] - s[:, None, :]
    band_ok = (s[:, None, :] >= N_META) & (dist_band >= 0) & (dist_band < WINDOW)
    dist_meta = t[:, :, None] - jnp.arange(N_META)[None, None, :]
    meta_ok = dist_meta >= 0

    slopes = alibi_slopes(N_Q_HEADS).reshape(N_KV_HEADS, Q_PER_KV)[:, :, None, None]
    pen_band = slopes * dist_band[:, None, None].astype(jnp.float32)
    pen_meta = slopes * jnp.minimum(dist_meta, WINDOW)[:, None, None].astype(jnp.float32)
    s_band = jnp.where(band_ok[:, None, None], s_band - pen_band, -jnp.inf)
    s_meta = jnp.where(meta_ok[:, None, None], s_meta - pen_meta, -jnp.inf)

    sink = sinks.astype(jnp.float32).reshape(N_KV_HEADS, Q_PER_KV)[:, :, None, None]
    mx = jnp.maximum(jnp.maximum(s_band.max(-1, keepdims=True), s_meta.max(-1, keepdims=True)), sink)
    p_band = jnp.exp(s_band - mx)
    p_meta = jnp.exp(s_meta - mx)
    denom = p_band.sum(-1, keepdims=True) + p_meta.sum(-1, keepdims=True) + jnp.exp(sink - mx)
    p_band = p_band / denom
    p_meta = p_meta / denom
    o = (jnp.einsum("bnkgqs,bnskd->bnqkgd", p_band, v_band)
         + jnp.einsum("bnkgqm,bmkd->bnqkgd", p_meta, v_meta))
    return o.reshape(bsz, Lp, Q_DIM)[:, PAD:]


def griffin_swa_mixer(u, w_in, conv_w, conv_b, w_a, b_a, w_x, b_x, lam, sinks, w_out):
    bsz, L, _ = u.shape
    proj = u @ w_in
    gate, xr, q, k, v = jnp.split(
        proj, [D_RNN, 2 * D_RNN, 2 * D_RNN + Q_DIM, 2 * D_RNN + Q_DIM + KV_DIM], axis=-1)
    xr = causal_dwconv(xr, conv_w, conv_b)
    y_a = jax.nn.gelu(gate, approximate=True) * rg_lru(xr, w_a, b_a, w_x, b_x, lam)
    y_b = swa_sink_alibi(q.reshape(bsz, L, N_Q_HEADS, HEAD_DIM),
                         k.reshape(bsz, L, N_KV_HEADS, HEAD_DIM),
                         v.reshape(bsz, L, N_KV_HEADS, HEAD_DIM), sinks).astype(u.dtype)
    return jnp.concatenate([y_a, y_b], axis=-1) @ w_out


def ssd_chunked(x, dt, a, b_in, c_in):
    bsz, Lp = x.shape[:2]
    nc = Lp // BLOCK
    x = x.reshape(bsz, nc, BLOCK, SSD_GROUPS, SSD_HPG, SSD_HEADDIM)
    dt = dt.reshape(bsz, nc, BLOCK, SSD_GROUPS, SSD_HPG)
    bc = b_in.reshape(bsz, nc, BLOCK, SSD_GROUPS, SSD_STATE)
    cc = c_in.reshape(bsz, nc, BLOCK, SSD_GROUPS, SSD_STATE)
    cs = jnp.cumsum(dt * a.reshape(SSD_GROUPS, SSD_HPG), axis=2)
    xdt = x * dt[..., None]
    cs_t = jnp.moveaxis(cs, 2, -1)
    seg = cs_t[..., :, None] - cs_t[..., None, :]
    tril = jnp.tril(jnp.ones((BLOCK, BLOCK), dtype=bool))
    decay_in = jnp.exp(jnp.where(tril, seg, -jnp.inf))
    cb = jnp.einsum("bclgn,bcsgn->bcgls", cc, bc)
    y_diag = jnp.einsum("bcgls,bcghls,bcsghp->bclghp", cb, decay_in, xdt)
    cs_last = cs[:, :, -1:]
    chunk_states = jnp.einsum("bclgn,bclgh,bclghp->bcghpn", bc, jnp.exp(cs_last - cs), xdt)
    chunk_decay = jnp.exp(cs_last[:, :, 0])

    def step(state, inp):
        dec, st = inp
        return state * dec[..., None, None] + st, state

    init = jnp.zeros_like(chunk_states[:, 0])
    _, prev = lax.scan(step, init, (jnp.moveaxis(chunk_decay, 1, 0), jnp.moveaxis(chunk_states, 1, 0)))
    prev = jnp.moveaxis(prev, 0, 1)
    y_off = jnp.einsum("bclgn,bcghpn,bclgh->bclghp", cc, prev, jnp.exp(cs))
    return (y_diag + y_off).reshape(bsz, Lp, SSD_HEADS, SSD_HEADDIM)


def mamba2_mixer(u, w_in, conv_w, conv_b, dt_bias, a_log, d_skip, gate_norm, w_out):
    bsz, L, _ = u.shape
    proj = u @ w_in
    z, xbc, dt = jnp.split(proj, [D_SSM, D_SSM + SSD_CONV_DIM], axis=-1)
    xbc = jax.nn.silu(causal_dwconv(xbc, conv_w, conv_b))
    xs, bs, cs = jnp.split(xbc, [D_SSM, D_SSM + SSD_GROUPS * SSD_STATE], axis=-1)
    xs = xs.reshape(bsz, L, SSD_HEADS, SSD_HEADDIM).astype(jnp.float32)
    bs = bs.reshape(bsz, L, SSD_GROUPS, SSD_STATE).astype(jnp.float32)
    cs = cs.reshape(bsz, L, SSD_GROUPS, SSD_STATE).astype(jnp.float32)
    dt = jax.nn.softplus(dt.astype(jnp.float32) + dt_bias.astype(jnp.float32))
    a = -jnp.exp(a_log.astype(jnp.float32))

    def front_pad(t):
        return jnp.pad(t, ((0, 0), (PAD, 0)) + ((0, 0),) * (t.ndim - 2))

    y = ssd_chunked(front_pad(xs), front_pad(dt), a, front_pad(bs), front_pad(cs))[:, PAD:]
    y = y + d_skip.astype(jnp.float32)[:, None] * xs
    y = y.reshape(bsz, L, D_SSM) * jax.nn.silu(z.astype(jnp.float32))
    yg = y.reshape(bsz, L, SSD_GROUPS, D_SSM // SSD_GROUPS)
    yg = yg * lax.rsqrt(jnp.mean(yg * yg, axis=-1, keepdims=True) + EPS)
    y = yg.reshape(bsz, L, D_SSM) * gate_norm.astype(jnp.float32)
    return y.astype(u.dtype) @ w_out


def conv_ffn(u, w_up, conv_w, conv_b, w_down):
    h = causal_dwconv(u @ w_up, conv_w, conv_b)
    g, up = jnp.split(h, 2, axis=-1)
    return (jax.nn.gelu(g, approximate=True) * up) @ w_down


def _fwd_setup_inputs(seed: int = 0) -> dict:
    key = jax.random.key(seed)
    k = jax.random.split(key, 36)

    def normal(kk, shape, scale):
        return jax.random.normal(kk, shape, jnp.float32) * scale

    def gain(kk, n):
        return 1.0 + 0.05 * jax.random.normal(kk, (n,), jnp.float32)

    u = jax.random.uniform(k[11], (D_RNN,), jnp.float32, 0.9, 0.999)
    a0 = u ** (1.0 / LRU_C)
    lru_lambda = jnp.log(a0) - jnp.log1p(-a0)
    dt0 = jnp.exp(jax.random.uniform(k[25], (SSD_HEADS,), jnp.float32, math.log(1e-3), math.log(1e-1)))
    dt_bias = dt0 + jnp.log(-jnp.expm1(-dt0))
    a_log = jnp.log(jax.random.uniform(k[26], (SSD_HEADS,), jnp.float32, 1.0, 16.0))

    return {
        "x": normal(k[0], (BATCH, SEQ, D_MODEL), 1.0),
        "meta_tokens": normal(k[1], (N_META, D_MODEL), 1.0),
        "l0_mix_pre_norm": gain(k[2], D_MODEL),
        "l0_mix_post_norm": gain(k[3], D_MODEL),
        "l0_w_in": normal(k[4], (D_MODEL, EVEN_IN), D_MODEL ** -0.5),
        "l0_lru_conv_w": normal(k[5], (CONV_A, D_RNN), CONV_A ** -0.5),
        "l0_lru_conv_b": normal(k[6], (D_RNN,), 0.01),
        "l0_lru_w_a": normal(k[7], (LRU_BLOCKS, LRU_BS, LRU_BS), LRU_BS ** -0.5),
        "l0_lru_b_a": normal(k[8], (D_RNN,), 0.01),
        "l0_lru_w_x": normal(k[9], (LRU_BLOCKS, LRU_BS, LRU_BS), LRU_BS ** -0.5),
        "l0_lru_b_x": normal(k[10], (D_RNN,), 0.01),
        "l0_lru_lambda": lru_lambda,
        "l0_attn_sinks": normal(k[12], (N_Q_HEADS,), 0.5),
        "l0_w_out": normal(k[13], (EVEN_MIX, D_MODEL), EVEN_MIX ** -0.5),
        "l0_ffn_pre_norm": gain(k[14], D_MODEL),
        "l0_ffn_post_norm": gain(k[15], D_MODEL),
        "l0_ffn_w_up": normal(k[16], (D_MODEL, 2 * D_FF), D_MODEL ** -0.5),
        "l0_ffn_conv_w": normal(k[17], (CONV_F, 2 * D_FF), CONV_F ** -0.5),
        "l0_ffn_conv_b": normal(k[18], (2 * D_FF,), 0.01),
        "l0_ffn_w_down": normal(k[19], (D_FF, D_MODEL), D_FF ** -0.5),
        "l1_mix_pre_norm": gain(k[20], D_MODEL),
        "l1_mix_post_norm": gain(k[21], D_MODEL),
        "l1_w_in": normal(k[22], (D_MODEL, ODD_IN), D_MODEL ** -0.5),
        "l1_ssm_conv_w": normal(k[23], (CONV_C, SSD_CONV_DIM), CONV_C ** -0.5),
        "l1_ssm_conv_b": normal(k[24], (SSD_CONV_DIM,), 0.01),
        "l1_dt_bias": dt_bias,
        "l1_a_log": a_log,
        "l1_d_skip": 1.0 + 0.1 * jax.random.normal(k[27], (SSD_HEADS,), jnp.float32),
        "l1_gate_norm": gain(k[28], D_SSM),
        "l1_w_out": normal(k[29], (D_SSM, D_MODEL), D_SSM ** -0.5),
        "l1_ffn_pre_norm": gain(k[30], D_MODEL),
        "l1_ffn_post_norm": gain(k[31], D_MODEL),
        "l1_ffn_w_up": normal(k[32], (D_MODEL, 2 * D_FF), D_MODEL ** -0.5),
        "l1_ffn_conv_w": normal(k[33], (CONV_F, 2 * D_FF), CONV_F ** -0.5),
        "l1_ffn_conv_b": normal(k[34], (2 * D_FF,), 0.01),
        "l1_ffn_w_down": normal(k[35], (D_FF, D_MODEL), D_FF ** -0.5),
    }


def _fwd_reference(x, meta_tokens,
              l0_mix_pre_norm, l0_mix_post_norm, l0_w_in, l0_lru_conv_w, l0_lru_conv_b,
              l0_lru_w_a, l0_lru_b_a, l0_lru_w_x, l0_lru_b_x, l0_lru_lambda, l0_attn_sinks, l0_w_out,
              l0_ffn_pre_norm, l0_ffn_post_norm, l0_ffn_w_up, l0_ffn_conv_w, l0_ffn_conv_b, l0_ffn_w_down,
              l1_mix_pre_norm, l1_mix_post_norm, l1_w_in, l1_ssm_conv_w, l1_ssm_conv_b,
              l1_dt_bias, l1_a_log, l1_d_skip, l1_gate_norm, l1_w_out,
              l1_ffn_pre_norm, l1_ffn_post_norm, l1_ffn_w_up, l1_ffn_conv_w, l1_ffn_conv_b, l1_ffn_w_down):
    bsz = x.shape[0]
    meta = jnp.broadcast_to(meta_tokens.astype(x.dtype)[None], (bsz, N_META, D_MODEL))
    h = jnp.concatenate([meta, x], axis=1)
    layers = [
        (l0_mix_pre_norm, l0_mix_post_norm,
         (l0_w_in, l0_lru_conv_w, l0_lru_conv_b, l0_lru_w_a, l0_lru_b_a, l0_lru_w_x, l0_lru_b_x,
          l0_lru_lambda, l0_attn_sinks, l0_w_out),
         (l0_ffn_pre_norm, l0_ffn_post_norm, l0_ffn_w_up, l0_ffn_conv_w, l0_ffn_conv_b, l0_ffn_w_down)),
        (l1_mix_pre_norm, l1_mix_post_norm,
         (l1_w_in, l1_ssm_conv_w, l1_ssm_conv_b, l1_dt_bias, l1_a_log, l1_d_skip, l1_gate_norm, l1_w_out),
         (l1_ffn_pre_norm, l1_ffn_post_norm, l1_ffn_w_up, l1_ffn_conv_w, l1_ffn_conv_b, l1_ffn_w_down)),
    ]
    for i in range(DEPTH):
        pre, post, mix, ffn = layers[i]
        mixer = griffin_swa_mixer if i % 2 == 0 else mamba2_mixer
        h = h + rms_norm(mixer(rms_norm(h, pre), *mix), post)
        f_pre, f_post, w_up, c_w, c_b, w_down = ffn
        h = h + rms_norm(conv_ffn(rms_norm(h, f_pre), w_up, c_w, c_b, w_down), f_post)
    return h[:, N_META:]


import jax as _jax
import jax.numpy as _jnp

TWIN_FORMAT = 'train_step'
FWD_PARAMS = ['x', 'meta_tokens', 'l0_mix_pre_norm', 'l0_mix_post_norm', 'l0_w_in', 'l0_lru_conv_w', 'l0_lru_conv_b', 'l0_lru_w_a', 'l0_lru_b_a', 'l0_lru_w_x', 'l0_lru_b_x', 'l0_lru_lambda', 'l0_attn_sinks', 'l0_w_out', 'l0_ffn_pre_norm', 'l0_ffn_post_norm', 'l0_ffn_w_up', 'l0_ffn_conv_w', 'l0_ffn_conv_b', 'l0_ffn_w_down', 'l1_mix_pre_norm', 'l1_mix_post_norm', 'l1_w_in', 'l1_ssm_conv_w', 'l1_ssm_conv_b', 'l1_dt_bias', 'l1_a_log', 'l1_d_skip', 'l1_gate_norm', 'l1_w_out', 'l1_ffn_pre_norm', 'l1_ffn_post_norm', 'l1_ffn_w_up', 'l1_ffn_conv_w', 'l1_ffn_conv_b', 'l1_ffn_w_down']
TWIN_WEIGHTS = ['meta_tokens', 'l0_mix_pre_norm', 'l0_mix_post_norm', 'l0_w_in', 'l0_lru_conv_w', 'l0_lru_conv_b', 'l0_lru_w_a', 'l0_lru_b_a', 'l0_lru_w_x', 'l0_lru_b_x', 'l0_lru_lambda', 'l0_attn_sinks', 'l0_w_out', 'l0_ffn_pre_norm', 'l0_ffn_post_norm', 'l0_ffn_w_up', 'l0_ffn_conv_w', 'l0_ffn_conv_b', 'l0_ffn_w_down', 'l1_mix_pre_norm', 'l1_mix_post_norm', 'l1_w_in', 'l1_ssm_conv_w', 'l1_ssm_conv_b', 'l1_dt_bias', 'l1_a_log', 'l1_d_skip', 'l1_gate_norm', 'l1_w_out', 'l1_ffn_pre_norm', 'l1_ffn_post_norm', 'l1_ffn_w_up', 'l1_ffn_conv_w', 'l1_ffn_conv_b', 'l1_ffn_w_down']
TWIN_DIFF_INPUT = 'x'
TWIN_INPUTS = ['x', 'meta_tokens', 'l0_mix_pre_norm', 'l0_mix_post_norm', 'l0_w_in', 'l0_lru_conv_w', 'l0_lru_conv_b', 'l0_lru_w_a', 'l0_lru_b_a', 'l0_lru_w_x', 'l0_lru_b_x', 'l0_lru_lambda', 'l0_attn_sinks', 'l0_w_out', 'l0_ffn_pre_norm', 'l0_ffn_post_norm', 'l0_ffn_w_up', 'l0_ffn_conv_w', 'l0_ffn_conv_b', 'l0_ffn_w_down', 'l1_mix_pre_norm', 'l1_mix_post_norm', 'l1_w_in', 'l1_ssm_conv_w', 'l1_ssm_conv_b', 'l1_dt_bias', 'l1_a_log', 'l1_d_skip', 'l1_gate_norm', 'l1_w_out', 'l1_ffn_pre_norm', 'l1_ffn_post_norm', 'l1_ffn_w_up', 'l1_ffn_conv_w', 'l1_ffn_conv_b', 'l1_ffn_w_down', 'loss_target', 'm_meta_tokens', 'm_l0_mix_pre_norm', 'm_l0_mix_post_norm', 'm_l0_w_in', 'm_l0_lru_conv_w', 'm_l0_lru_conv_b', 'm_l0_lru_w_a', 'm_l0_lru_b_a', 'm_l0_lru_w_x', 'm_l0_lru_b_x', 'm_l0_lru_lambda', 'm_l0_attn_sinks', 'm_l0_w_out', 'm_l0_ffn_pre_norm', 'm_l0_ffn_post_norm', 'm_l0_ffn_w_up', 'm_l0_ffn_conv_w', 'm_l0_ffn_conv_b', 'm_l0_ffn_w_down', 'm_l1_mix_pre_norm', 'm_l1_mix_post_norm', 'm_l1_w_in', 'm_l1_ssm_conv_w', 'm_l1_ssm_conv_b', 'm_l1_dt_bias', 'm_l1_a_log', 'm_l1_d_skip', 'm_l1_gate_norm', 'm_l1_w_out', 'm_l1_ffn_pre_norm', 'm_l1_ffn_post_norm', 'm_l1_ffn_w_up', 'm_l1_ffn_conv_w', 'm_l1_ffn_conv_b', 'm_l1_ffn_w_down', 'v_meta_tokens', 'v_l0_mix_pre_norm', 'v_l0_mix_post_norm', 'v_l0_w_in', 'v_l0_lru_conv_w', 'v_l0_lru_conv_b', 'v_l0_lru_w_a', 'v_l0_lru_b_a', 'v_l0_lru_w_x', 'v_l0_lru_b_x', 'v_l0_lru_lambda', 'v_l0_attn_sinks', 'v_l0_w_out', 'v_l0_ffn_pre_norm', 'v_l0_ffn_post_norm', 'v_l0_ffn_w_up', 'v_l0_ffn_conv_w', 'v_l0_ffn_conv_b', 'v_l0_ffn_w_down', 'v_l1_mix_pre_norm', 'v_l1_mix_post_norm', 'v_l1_w_in', 'v_l1_ssm_conv_w', 'v_l1_ssm_conv_b', 'v_l1_dt_bias', 'v_l1_a_log', 'v_l1_d_skip', 'v_l1_gate_norm', 'v_l1_w_out', 'v_l1_ffn_pre_norm', 'v_l1_ffn_post_norm', 'v_l1_ffn_w_up', 'v_l1_ffn_conv_w', 'v_l1_ffn_conv_b', 'v_l1_ffn_w_down']
TWIN_OUTPUTS = ['loss', 'grad_x', 'grad_meta_tokens', 'grad_l0_mix_pre_norm', 'grad_l0_mix_post_norm', 'grad_l0_w_in', 'grad_l0_lru_conv_w', 'grad_l0_lru_conv_b', 'grad_l0_lru_w_a', 'grad_l0_lru_b_a', 'grad_l0_lru_w_x', 'grad_l0_lru_b_x', 'grad_l0_lru_lambda', 'grad_l0_attn_sinks', 'grad_l0_w_out', 'grad_l0_ffn_pre_norm', 'grad_l0_ffn_post_norm', 'grad_l0_ffn_w_up', 'grad_l0_ffn_conv_w', 'grad_l0_ffn_conv_b', 'grad_l0_ffn_w_down', 'grad_l1_mix_pre_norm', 'grad_l1_mix_post_norm', 'grad_l1_w_in', 'grad_l1_ssm_conv_w', 'grad_l1_ssm_conv_b', 'grad_l1_dt_bias', 'grad_l1_a_log', 'grad_l1_d_skip', 'grad_l1_gate_norm', 'grad_l1_w_out', 'grad_l1_ffn_pre_norm', 'grad_l1_ffn_post_norm', 'grad_l1_ffn_w_up', 'grad_l1_ffn_conv_w', 'grad_l1_ffn_conv_b', 'grad_l1_ffn_w_down', 'delta_meta_tokens', 'delta_l0_mix_pre_norm', 'delta_l0_mix_post_norm', 'delta_l0_w_in', 'delta_l0_lru_conv_w', 'delta_l0_lru_conv_b', 'delta_l0_lru_w_a', 'delta_l0_lru_b_a', 'delta_l0_lru_w_x', 'delta_l0_lru_b_x', 'delta_l0_lru_lambda', 'delta_l0_attn_sinks', 'delta_l0_w_out', 'delta_l0_ffn_pre_norm', 'delta_l0_ffn_post_norm', 'delta_l0_ffn_w_up', 'delta_l0_ffn_conv_w', 'delta_l0_ffn_conv_b', 'delta_l0_ffn_w_down', 'delta_l1_mix_pre_norm', 'delta_l1_mix_post_norm', 'delta_l1_w_in', 'delta_l1_ssm_conv_w', 'delta_l1_ssm_conv_b', 'delta_l1_dt_bias', 'delta_l1_a_log', 'delta_l1_d_skip', 'delta_l1_gate_norm', 'delta_l1_w_out', 'delta_l1_ffn_pre_norm', 'delta_l1_ffn_post_norm', 'delta_l1_ffn_w_up', 'delta_l1_ffn_conv_w', 'delta_l1_ffn_conv_b', 'delta_l1_ffn_w_down', 'new_m_meta_tokens', 'new_m_l0_mix_pre_norm', 'new_m_l0_mix_post_norm', 'new_m_l0_w_in', 'new_m_l0_lru_conv_w', 'new_m_l0_lru_conv_b', 'new_m_l0_lru_w_a', 'new_m_l0_lru_b_a', 'new_m_l0_lru_w_x', 'new_m_l0_lru_b_x', 'new_m_l0_lru_lambda', 'new_m_l0_attn_sinks', 'new_m_l0_w_out', 'new_m_l0_ffn_pre_norm', 'new_m_l0_ffn_post_norm', 'new_m_l0_ffn_w_up', 'new_m_l0_ffn_conv_w', 'new_m_l0_ffn_conv_b', 'new_m_l0_ffn_w_down', 'new_m_l1_mix_pre_norm', 'new_m_l1_mix_post_norm', 'new_m_l1_w_in', 'new_m_l1_ssm_conv_w', 'new_m_l1_ssm_conv_b', 'new_m_l1_dt_bias', 'new_m_l1_a_log', 'new_m_l1_d_skip', 'new_m_l1_gate_norm', 'new_m_l1_w_out', 'new_m_l1_ffn_pre_norm', 'new_m_l1_ffn_post_norm', 'new_m_l1_ffn_w_up', 'new_m_l1_ffn_conv_w', 'new_m_l1_ffn_conv_b', 'new_m_l1_ffn_w_down', 'new_v_meta_tokens', 'new_v_l0_mix_pre_norm', 'new_v_l0_mix_post_norm', 'new_v_l0_w_in', 'new_v_l0_lru_conv_w', 'new_v_l0_lru_conv_b', 'new_v_l0_lru_w_a', 'new_v_l0_lru_b_a', 'new_v_l0_lru_w_x', 'new_v_l0_lru_b_x', 'new_v_l0_lru_lambda', 'new_v_l0_attn_sinks', 'new_v_l0_w_out', 'new_v_l0_ffn_pre_norm', 'new_v_l0_ffn_post_norm', 'new_v_l0_ffn_w_up', 'new_v_l0_ffn_conv_w', 'new_v_l0_ffn_conv_b', 'new_v_l0_ffn_w_down', 'new_v_l1_mix_pre_norm', 'new_v_l1_mix_post_norm', 'new_v_l1_w_in', 'new_v_l1_ssm_conv_w', 'new_v_l1_ssm_conv_b', 'new_v_l1_dt_bias', 'new_v_l1_a_log', 'new_v_l1_d_skip', 'new_v_l1_gate_norm', 'new_v_l1_w_out', 'new_v_l1_ffn_pre_norm', 'new_v_l1_ffn_post_norm', 'new_v_l1_ffn_w_up', 'new_v_l1_ffn_conv_w', 'new_v_l1_ffn_conv_b', 'new_v_l1_ffn_w_down']
TWIN_LEAF_KINDS = {'loss': 'loss', 'grad_x': 'grad_x', 'grad_meta_tokens': 'grad_w', 'grad_l0_mix_pre_norm': 'grad_w', 'grad_l0_mix_post_norm': 'grad_w', 'grad_l0_w_in': 'grad_w', 'grad_l0_lru_conv_w': 'grad_w', 'grad_l0_lru_conv_b': 'grad_w', 'grad_l0_lru_w_a': 'grad_w', 'grad_l0_lru_b_a': 'grad_w', 'grad_l0_lru_w_x': 'grad_w', 'grad_l0_lru_b_x': 'grad_w', 'grad_l0_lru_lambda': 'grad_w', 'grad_l0_attn_sinks': 'grad_w', 'grad_l0_w_out': 'grad_w', 'grad_l0_ffn_pre_norm': 'grad_w', 'grad_l0_ffn_post_norm': 'grad_w', 'grad_l0_ffn_w_up': 'grad_w', 'grad_l0_ffn_conv_w': 'grad_w', 'grad_l0_ffn_conv_b': 'grad_w', 'grad_l0_ffn_w_down': 'grad_w', 'grad_l1_mix_pre_norm': 'grad_w', 'grad_l1_mix_post_norm': 'grad_w', 'grad_l1_w_in': 'grad_w', 'grad_l1_ssm_conv_w': 'grad_w', 'grad_l1_ssm_conv_b': 'grad_w', 'grad_l1_dt_bias': 'grad_w', 'grad_l1_a_log': 'grad_w', 'grad_l1_d_skip': 'grad_w', 'grad_l1_gate_norm': 'grad_w', 'grad_l1_w_out': 'grad_w', 'grad_l1_ffn_pre_norm': 'grad_w', 'grad_l1_ffn_post_norm': 'grad_w', 'grad_l1_ffn_w_up': 'grad_w', 'grad_l1_ffn_conv_w': 'grad_w', 'grad_l1_ffn_conv_b': 'grad_w', 'grad_l1_ffn_w_down': 'grad_w', 'delta_meta_tokens': 'delta_w', 'delta_l0_mix_pre_norm': 'delta_w', 'delta_l0_mix_post_norm': 'delta_w', 'delta_l0_w_in': 'delta_w', 'delta_l0_lru_conv_w': 'delta_w', 'delta_l0_lru_conv_b': 'delta_w', 'delta_l0_lru_w_a': 'delta_w', 'delta_l0_lru_b_a': 'delta_w', 'delta_l0_lru_w_x': 'delta_w', 'delta_l0_lru_b_x': 'delta_w', 'delta_l0_lru_lambda': 'delta_w', 'delta_l0_attn_sinks': 'delta_w', 'delta_l0_w_out': 'delta_w', 'delta_l0_ffn_pre_norm': 'delta_w', 'delta_l0_ffn_post_norm': 'delta_w', 'delta_l0_ffn_w_up': 'delta_w', 'delta_l0_ffn_conv_w': 'delta_w', 'delta_l0_ffn_conv_b': 'delta_w', 'delta_l0_ffn_w_down': 'delta_w', 'delta_l1_mix_pre_norm': 'delta_w', 'delta_l1_mix_post_norm': 'delta_w', 'delta_l1_w_in': 'delta_w', 'delta_l1_ssm_conv_w': 'delta_w', 'delta_l1_ssm_conv_b': 'delta_w', 'delta_l1_dt_bias': 'delta_w', 'delta_l1_a_log': 'delta_w', 'delta_l1_d_skip': 'delta_w', 'delta_l1_gate_norm': 'delta_w', 'delta_l1_w_out': 'delta_w', 'delta_l1_ffn_pre_norm': 'delta_w', 'delta_l1_ffn_post_norm': 'delta_w', 'delta_l1_ffn_w_up': 'delta_w', 'delta_l1_ffn_conv_w': 'delta_w', 'delta_l1_ffn_conv_b': 'delta_w', 'delta_l1_ffn_w_down': 'delta_w', 'new_m_meta_tokens': 'new_m', 'new_m_l0_mix_pre_norm': 'new_m', 'new_m_l0_mix_post_norm': 'new_m', 'new_m_l0_w_in': 'new_m', 'new_m_l0_lru_conv_w': 'new_m', 'new_m_l0_lru_conv_b': 'new_m', 'new_m_l0_lru_w_a': 'new_m', 'new_m_l0_lru_b_a': 'new_m', 'new_m_l0_lru_w_x': 'new_m', 'new_m_l0_lru_b_x': 'new_m', 'new_m_l0_lru_lambda': 'new_m', 'new_m_l0_attn_sinks': 'new_m', 'new_m_l0_w_out': 'new_m', 'new_m_l0_ffn_pre_norm': 'new_m', 'new_m_l0_ffn_post_norm': 'new_m', 'new_m_l0_ffn_w_up': 'new_m', 'new_m_l0_ffn_conv_w': 'new_m', 'new_m_l0_ffn_conv_b': 'new_m', 'new_m_l0_ffn_w_down': 'new_m', 'new_m_l1_mix_pre_norm': 'new_m', 'new_m_l1_mix_post_norm': 'new_m', 'new_m_l1_w_in': 'new_m', 'new_m_l1_ssm_conv_w': 'new_m', 'new_m_l1_ssm_conv_b': 'new_m', 'new_m_l1_dt_bias': 'new_m', 'new_m_l1_a_log': 'new_m', 'new_m_l1_d_skip': 'new_m', 'new_m_l1_gate_norm': 'new_m', 'new_m_l1_w_out': 'new_m', 'new_m_l1_ffn_pre_norm': 'new_m', 'new_m_l1_ffn_post_norm': 'new_m', 'new_m_l1_ffn_w_up': 'new_m', 'new_m_l1_ffn_conv_w': 'new_m', 'new_m_l1_ffn_conv_b': 'new_m', 'new_m_l1_ffn_w_down': 'new_m', 'new_v_meta_tokens': 'new_v', 'new_v_l0_mix_pre_norm': 'new_v', 'new_v_l0_mix_post_norm': 'new_v', 'new_v_l0_w_in': 'new_v', 'new_v_l0_lru_conv_w': 'new_v', 'new_v_l0_lru_conv_b': 'new_v', 'new_v_l0_lru_w_a': 'new_v', 'new_v_l0_lru_b_a': 'new_v', 'new_v_l0_lru_w_x': 'new_v', 'new_v_l0_lru_b_x': 'new_v', 'new_v_l0_lru_lambda': 'new_v', 'new_v_l0_attn_sinks': 'new_v', 'new_v_l0_w_out': 'new_v', 'new_v_l0_ffn_pre_norm': 'new_v', 'new_v_l0_ffn_post_norm': 'new_v', 'new_v_l0_ffn_w_up': 'new_v', 'new_v_l0_ffn_conv_w': 'new_v', 'new_v_l0_ffn_conv_b': 'new_v', 'new_v_l0_ffn_w_down': 'new_v', 'new_v_l1_mix_pre_norm': 'new_v', 'new_v_l1_mix_post_norm': 'new_v', 'new_v_l1_w_in': 'new_v', 'new_v_l1_ssm_conv_w': 'new_v', 'new_v_l1_ssm_conv_b': 'new_v', 'new_v_l1_dt_bias': 'new_v', 'new_v_l1_a_log': 'new_v', 'new_v_l1_d_skip': 'new_v', 'new_v_l1_gate_norm': 'new_v', 'new_v_l1_w_out': 'new_v', 'new_v_l1_ffn_pre_norm': 'new_v', 'new_v_l1_ffn_post_norm': 'new_v', 'new_v_l1_ffn_w_up': 'new_v', 'new_v_l1_ffn_conv_w': 'new_v', 'new_v_l1_ffn_conv_b': 'new_v', 'new_v_l1_ffn_w_down': 'new_v'}


def _forward(args):
    return _fwd_reference(*[args[k] for k in FWD_PARAMS])


def _output_shape():
    def fwd():
        inp = _fwd_setup_inputs(0)
        return _fwd_reference(*[inp[k] for k in FWD_PARAMS])
    out = _jax.eval_shape(fwd)
    return out.shape, out.dtype

N_MICROBATCH = 1
ADAM_LR = 0.001
ADAM_B1 = 0.9
ADAM_B2 = 0.999
ADAM_EPS = 1e-08
ADAM_WD = 0.01
ADAM_STEP = 10
PER_EXAMPLE_BATCH_AXIS = {'x': 0, 'loss_target': 0}
SHARED_INPUTS = []
_WEIGHT_DTYPES = {'meta_tokens': _jnp.float32, 'l0_mix_pre_norm': _jnp.float32, 'l0_mix_post_norm': _jnp.float32, 'l0_w_in': _jnp.float32, 'l0_lru_conv_w': _jnp.float32, 'l0_lru_conv_b': _jnp.float32, 'l0_lru_w_a': _jnp.float32, 'l0_lru_b_a': _jnp.float32, 'l0_lru_w_x': _jnp.float32, 'l0_lru_b_x': _jnp.float32, 'l0_lru_lambda': _jnp.float32, 'l0_attn_sinks': _jnp.float32, 'l0_w_out': _jnp.float32, 'l0_ffn_pre_norm': _jnp.float32, 'l0_ffn_post_norm': _jnp.float32, 'l0_ffn_w_up': _jnp.float32, 'l0_ffn_conv_w': _jnp.float32, 'l0_ffn_conv_b': _jnp.float32, 'l0_ffn_w_down': _jnp.float32, 'l1_mix_pre_norm': _jnp.float32, 'l1_mix_post_norm': _jnp.float32, 'l1_w_in': _jnp.float32, 'l1_ssm_conv_w': _jnp.float32, 'l1_ssm_conv_b': _jnp.float32, 'l1_dt_bias': _jnp.float32, 'l1_a_log': _jnp.float32, 'l1_d_skip': _jnp.float32, 'l1_gate_norm': _jnp.float32, 'l1_w_out': _jnp.float32, 'l1_ffn_pre_norm': _jnp.float32, 'l1_ffn_post_norm': _jnp.float32, 'l1_ffn_w_up': _jnp.float32, 'l1_ffn_conv_w': _jnp.float32, 'l1_ffn_conv_b': _jnp.float32, 'l1_ffn_w_down': _jnp.float32}
MOMENT_SCALE = {'meta_tokens': 1.221874e-01, 'l0_mix_pre_norm': 2.051896e+00, 'l0_mix_post_norm': 6.364628e+01, 'l0_w_in': 1.113446e+00, 'l0_lru_conv_w': 2.104356e+00, 'l0_lru_conv_b': 5.812575e+01, 'l0_lru_w_a': 1.258406e+00, 'l0_lru_b_a': 6.216973e-01, 'l0_lru_w_x': 2.308551e+00, 'l0_lru_b_x': 6.430986e-01, 'l0_lru_lambda': 9.495997e-01, 'l0_attn_sinks': 1.450360e+00, 'l0_w_out': 2.600747e+00, 'l0_ffn_pre_norm': 1.740444e+00, 'l0_ffn_post_norm': 6.384878e+01, 'l0_ffn_w_up': 6.936578e-01, 'l0_ffn_conv_w': 7.980646e-01, 'l0_ffn_conv_b': 3.672250e+00, 'l0_ffn_w_down': 1.454455e+00, 'l1_mix_pre_norm': 1.772059e+00, 'l1_mix_post_norm': 6.457534e+01, 'l1_w_in': 6.927851e-01, 'l1_ssm_conv_w': 8.237722e-01, 'l1_ssm_conv_b': 2.981076e+00, 'l1_dt_bias': 9.812302e-01, 'l1_a_log': 3.227137e+00, 'l1_d_skip': 2.940914e+00, 'l1_gate_norm': 1.658835e+00, 'l1_w_out': 2.639670e+00, 'l1_ffn_pre_norm': 1.480671e+00, 'l1_ffn_post_norm': 6.429417e+01, 'l1_ffn_w_up': 5.930196e-01, 'l1_ffn_conv_w': 7.127024e-01, 'l1_ffn_conv_b': 3.709661e+00, 'l1_ffn_w_down': 1.295223e+00}


def _to_microbatches(a, axis):
    t = _jnp.moveaxis(a, axis, 0)
    t = t.reshape((N_MICROBATCH, t.shape[0] // N_MICROBATCH) + t.shape[1:])
    return _jnp.moveaxis(t, 1, axis + 1)


def setup_inputs(seed: int = 0) -> dict:
    inp = _fwd_setup_inputs(seed)
    key = _jax.random.fold_in(_jax.random.key(seed), 7919)
    shape, _ = _output_shape()
    out = dict(inp)
    out["loss_target"] = _jax.random.normal(_jax.random.fold_in(key, 0), shape, _jnp.float32)
    for i, name in enumerate(TWIN_WEIGHTS):
        w = inp[name].astype(_jnp.float32)
        if MOMENT_SCALE is None:
            s = _jnp.sqrt(_jnp.mean(_jnp.square(w)) + 1e-30)
        else:
            s = MOMENT_SCALE[name]
        km, kv = _jax.random.split(_jax.random.fold_in(key, i + 1))
        out[name] = w
        out["m_" + name] = s * _jax.random.normal(km, w.shape, _jnp.float32)
        out["v_" + name] = (s * s) * _jax.random.uniform(kv, w.shape, _jnp.float32, 0.5, 1.5)
    if N_MICROBATCH > 1:
        for name, axis in PER_EXAMPLE_BATCH_AXIS.items():
            out[name] = _to_microbatches(out[name], axis)
    return {'x': out['x'], 'meta_tokens': out['meta_tokens'], 'l0_mix_pre_norm': out['l0_mix_pre_norm'], 'l0_mix_post_norm': out['l0_mix_post_norm'], 'l0_w_in': out['l0_w_in'], 'l0_lru_conv_w': out['l0_lru_conv_w'], 'l0_lru_conv_b': out['l0_lru_conv_b'], 'l0_lru_w_a': out['l0_lru_w_a'], 'l0_lru_b_a': out['l0_lru_b_a'], 'l0_lru_w_x': out['l0_lru_w_x'], 'l0_lru_b_x': out['l0_lru_b_x'], 'l0_lru_lambda': out['l0_lru_lambda'], 'l0_attn_sinks': out['l0_attn_sinks'], 'l0_w_out': out['l0_w_out'], 'l0_ffn_pre_norm': out['l0_ffn_pre_norm'], 'l0_ffn_post_norm': out['l0_ffn_post_norm'], 'l0_ffn_w_up': out['l0_ffn_w_up'], 'l0_ffn_conv_w': out['l0_ffn_conv_w'], 'l0_ffn_conv_b': out['l0_ffn_conv_b'], 'l0_ffn_w_down': out['l0_ffn_w_down'], 'l1_mix_pre_norm': out['l1_mix_pre_norm'], 'l1_mix_post_norm': out['l1_mix_post_norm'], 'l1_w_in': out['l1_w_in'], 'l1_ssm_conv_w': out['l1_ssm_conv_w'], 'l1_ssm_conv_b': out['l1_ssm_conv_b'], 'l1_dt_bias': out['l1_dt_bias'], 'l1_a_log': out['l1_a_log'], 'l1_d_skip': out['l1_d_skip'], 'l1_gate_norm': out['l1_gate_norm'], 'l1_w_out': out['l1_w_out'], 'l1_ffn_pre_norm': out['l1_ffn_pre_norm'], 'l1_ffn_post_norm': out['l1_ffn_post_norm'], 'l1_ffn_w_up': out['l1_ffn_w_up'], 'l1_ffn_conv_w': out['l1_ffn_conv_w'], 'l1_ffn_conv_b': out['l1_ffn_conv_b'], 'l1_ffn_w_down': out['l1_ffn_w_down'], 'loss_target': out['loss_target'], 'm_meta_tokens': out['m_meta_tokens'], 'm_l0_mix_pre_norm': out['m_l0_mix_pre_norm'], 'm_l0_mix_post_norm': out['m_l0_mix_post_norm'], 'm_l0_w_in': out['m_l0_w_in'], 'm_l0_lru_conv_w': out['m_l0_lru_conv_w'], 'm_l0_lru_conv_b': out['m_l0_lru_conv_b'], 'm_l0_lru_w_a': out['m_l0_lru_w_a'], 'm_l0_lru_b_a': out['m_l0_lru_b_a'], 'm_l0_lru_w_x': out['m_l0_lru_w_x'], 'm_l0_lru_b_x': out['m_l0_lru_b_x'], 'm_l0_lru_lambda': out['m_l0_lru_lambda'], 'm_l0_attn_sinks': out['m_l0_attn_sinks'], 'm_l0_w_out': out['m_l0_w_out'], 'm_l0_ffn_pre_norm': out['m_l0_ffn_pre_norm'], 'm_l0_ffn_post_norm': out['m_l0_ffn_post_norm'], 'm_l0_ffn_w_up': out['m_l0_ffn_w_up'], 'm_l0_ffn_conv_w': out['m_l0_ffn_conv_w'], 'm_l0_ffn_conv_b': out['m_l0_ffn_conv_b'], 'm_l0_ffn_w_down': out['m_l0_ffn_w_down'], 'm_l1_mix_pre_norm': out['m_l1_mix_pre_norm'], 'm_l1_mix_post_norm': out['m_l1_mix_post_norm'], 'm_l1_w_in': out['m_l1_w_in'], 'm_l1_ssm_conv_w': out['m_l1_ssm_conv_w'], 'm_l1_ssm_conv_b': out['m_l1_ssm_conv_b'], 'm_l1_dt_bias': out['m_l1_dt_bias'], 'm_l1_a_log': out['m_l1_a_log'], 'm_l1_d_skip': out['m_l1_d_skip'], 'm_l1_gate_norm': out['m_l1_gate_norm'], 'm_l1_w_out': out['m_l1_w_out'], 'm_l1_ffn_pre_norm': out['m_l1_ffn_pre_norm'], 'm_l1_ffn_post_norm': out['m_l1_ffn_post_norm'], 'm_l1_ffn_w_up': out['m_l1_ffn_w_up'], 'm_l1_ffn_conv_w': out['m_l1_ffn_conv_w'], 'm_l1_ffn_conv_b': out['m_l1_ffn_conv_b'], 'm_l1_ffn_w_down': out['m_l1_ffn_w_down'], 'v_meta_tokens': out['v_meta_tokens'], 'v_l0_mix_pre_norm': out['v_l0_mix_pre_norm'], 'v_l0_mix_post_norm': out['v_l0_mix_post_norm'], 'v_l0_w_in': out['v_l0_w_in'], 'v_l0_lru_conv_w': out['v_l0_lru_conv_w'], 'v_l0_lru_conv_b': out['v_l0_lru_conv_b'], 'v_l0_lru_w_a': out['v_l0_lru_w_a'], 'v_l0_lru_b_a': out['v_l0_lru_b_a'], 'v_l0_lru_w_x': out['v_l0_lru_w_x'], 'v_l0_lru_b_x': out['v_l0_lru_b_x'], 'v_l0_lru_lambda': out['v_l0_lru_lambda'], 'v_l0_attn_sinks': out['v_l0_attn_sinks'], 'v_l0_w_out': out['v_l0_w_out'], 'v_l0_ffn_pre_norm': out['v_l0_ffn_pre_norm'], 'v_l0_ffn_post_norm': out['v_l0_ffn_post_norm'], 'v_l0_ffn_w_up': out['v_l0_ffn_w_up'], 'v_l0_ffn_conv_w': out['v_l0_ffn_conv_w'], 'v_l0_ffn_conv_b': out['v_l0_ffn_conv_b'], 'v_l0_ffn_w_down': out['v_l0_ffn_w_down'], 'v_l1_mix_pre_norm': out['v_l1_mix_pre_norm'], 'v_l1_mix_post_norm': out['v_l1_mix_post_norm'], 'v_l1_w_in': out['v_l1_w_in'], 'v_l1_ssm_conv_w': out['v_l1_ssm_conv_w'], 'v_l1_ssm_conv_b': out['v_l1_ssm_conv_b'], 'v_l1_dt_bias': out['v_l1_dt_bias'], 'v_l1_a_log': out['v_l1_a_log'], 'v_l1_d_skip': out['v_l1_d_skip'], 'v_l1_gate_norm': out['v_l1_gate_norm'], 'v_l1_w_out': out['v_l1_w_out'], 'v_l1_ffn_pre_norm': out['v_l1_ffn_pre_norm'], 'v_l1_ffn_post_norm': out['v_l1_ffn_post_norm'], 'v_l1_ffn_w_up': out['v_l1_ffn_w_up'], 'v_l1_ffn_conv_w': out['v_l1_ffn_conv_w'], 'v_l1_ffn_conv_b': out['v_l1_ffn_conv_b'], 'v_l1_ffn_w_down': out['v_l1_ffn_w_down']}


def _loss(weights, diff, rest, loss_target):
    with _jax.named_scope("forward"):
        args = {**rest, TWIN_DIFF_INPUT: diff, **{k: w.astype(_WEIGHT_DTYPES[k]) for k, w in weights.items()}}
        y = _forward(args)
    with _jax.named_scope("loss_head"):
        err = _jnp.square(y.astype(_jnp.float32) - loss_target)
        return 0.5 * _jnp.sum(_jnp.mean(err, axis=-1)) if err.ndim else 0.5 * err


def _adamw(w, g, m, v):
    m = ADAM_B1 * m + (1.0 - ADAM_B1) * g
    v = ADAM_B2 * v + (1.0 - ADAM_B2) * _jnp.square(g)
    m_hat = m / (1.0 - ADAM_B1 ** ADAM_STEP)
    v_hat = v / (1.0 - ADAM_B2 ** ADAM_STEP)
    delta = -ADAM_LR * (m_hat / (_jnp.sqrt(v_hat) + ADAM_EPS) + ADAM_WD * w)
    return delta, m, v


def reference(x, meta_tokens, l0_mix_pre_norm, l0_mix_post_norm, l0_w_in, l0_lru_conv_w, l0_lru_conv_b, l0_lru_w_a, l0_lru_b_a, l0_lru_w_x, l0_lru_b_x, l0_lru_lambda, l0_attn_sinks, l0_w_out, l0_ffn_pre_norm, l0_ffn_post_norm, l0_ffn_w_up, l0_ffn_conv_w, l0_ffn_conv_b, l0_ffn_w_down, l1_mix_pre_norm, l1_mix_post_norm, l1_w_in, l1_ssm_conv_w, l1_ssm_conv_b, l1_dt_bias, l1_a_log, l1_d_skip, l1_gate_norm, l1_w_out, l1_ffn_pre_norm, l1_ffn_post_norm, l1_ffn_w_up, l1_ffn_conv_w, l1_ffn_conv_b, l1_ffn_w_down, loss_target, m_meta_tokens, m_l0_mix_pre_norm, m_l0_mix_post_norm, m_l0_w_in, m_l0_lru_conv_w, m_l0_lru_conv_b, m_l0_lru_w_a, m_l0_lru_b_a, m_l0_lru_w_x, m_l0_lru_b_x, m_l0_lru_lambda, m_l0_attn_sinks, m_l0_w_out, m_l0_ffn_pre_norm, m_l0_ffn_post_norm, m_l0_ffn_w_up, m_l0_ffn_conv_w, m_l0_ffn_conv_b, m_l0_ffn_w_down, m_l1_mix_pre_norm, m_l1_mix_post_norm, m_l1_w_in, m_l1_ssm_conv_w, m_l1_ssm_conv_b, m_l1_dt_bias, m_l1_a_log, m_l1_d_skip, m_l1_gate_norm, m_l1_w_out, m_l1_ffn_pre_norm, m_l1_ffn_post_norm, m_l1_ffn_w_up, m_l1_ffn_conv_w, m_l1_ffn_conv_b, m_l1_ffn_w_down, v_meta_tokens, v_l0_mix_pre_norm, v_l0_mix_post_norm, v_l0_w_in, v_l0_lru_conv_w, v_l0_lru_conv_b, v_l0_lru_w_a, v_l0_lru_b_a, v_l0_lru_w_x, v_l0_lru_b_x, v_l0_lru_lambda, v_l0_attn_sinks, v_l0_w_out, v_l0_ffn_pre_norm, v_l0_ffn_post_norm, v_l0_ffn_w_up, v_l0_ffn_conv_w, v_l0_ffn_conv_b, v_l0_ffn_w_down, v_l1_mix_pre_norm, v_l1_mix_post_norm, v_l1_w_in, v_l1_ssm_conv_w, v_l1_ssm_conv_b, v_l1_dt_bias, v_l1_a_log, v_l1_d_skip, v_l1_gate_norm, v_l1_w_out, v_l1_ffn_pre_norm, v_l1_ffn_post_norm, v_l1_ffn_w_up, v_l1_ffn_conv_w, v_l1_ffn_conv_b, v_l1_ffn_w_down):
    given = dict(x=x, meta_tokens=meta_tokens, l0_mix_pre_norm=l0_mix_pre_norm, l0_mix_post_norm=l0_mix_post_norm, l0_w_in=l0_w_in, l0_lru_conv_w=l0_lru_conv_w, l0_lru_conv_b=l0_lru_conv_b, l0_lru_w_a=l0_lru_w_a, l0_lru_b_a=l0_lru_b_a, l0_lru_w_x=l0_lru_w_x, l0_lru_b_x=l0_lru_b_x, l0_lru_lambda=l0_lru_lambda, l0_attn_sinks=l0_attn_sinks, l0_w_out=l0_w_out, l0_ffn_pre_norm=l0_ffn_pre_norm, l0_ffn_post_norm=l0_ffn_post_norm, l0_ffn_w_up=l0_ffn_w_up, l0_ffn_conv_w=l0_ffn_conv_w, l0_ffn_conv_b=l0_ffn_conv_b, l0_ffn_w_down=l0_ffn_w_down, l1_mix_pre_norm=l1_mix_pre_norm, l1_mix_post_norm=l1_mix_post_norm, l1_w_in=l1_w_in, l1_ssm_conv_w=l1_ssm_conv_w, l1_ssm_conv_b=l1_ssm_conv_b, l1_dt_bias=l1_dt_bias, l1_a_log=l1_a_log, l1_d_skip=l1_d_skip, l1_gate_norm=l1_gate_norm, l1_w_out=l1_w_out, l1_ffn_pre_norm=l1_ffn_pre_norm, l1_ffn_post_norm=l1_ffn_post_norm, l1_ffn_w_up=l1_ffn_w_up, l1_ffn_conv_w=l1_ffn_conv_w, l1_ffn_conv_b=l1_ffn_conv_b, l1_ffn_w_down=l1_ffn_w_down, loss_target=loss_target, m_meta_tokens=m_meta_tokens, m_l0_mix_pre_norm=m_l0_mix_pre_norm, m_l0_mix_post_norm=m_l0_mix_post_norm, m_l0_w_in=m_l0_w_in, m_l0_lru_conv_w=m_l0_lru_conv_w, m_l0_lru_conv_b=m_l0_lru_conv_b, m_l0_lru_w_a=m_l0_lru_w_a, m_l0_lru_b_a=m_l0_lru_b_a, m_l0_lru_w_x=m_l0_lru_w_x, m_l0_lru_b_x=m_l0_lru_b_x, m_l0_lru_lambda=m_l0_lru_lambda, m_l0_attn_sinks=m_l0_attn_sinks, m_l0_w_out=m_l0_w_out, m_l0_ffn_pre_norm=m_l0_ffn_pre_norm, m_l0_ffn_post_norm=m_l0_ffn_post_norm, m_l0_ffn_w_up=m_l0_ffn_w_up, m_l0_ffn_conv_w=m_l0_ffn_conv_w, m_l0_ffn_conv_b=m_l0_ffn_conv_b, m_l0_ffn_w_down=m_l0_ffn_w_down, m_l1_mix_pre_norm=m_l1_mix_pre_norm, m_l1_mix_post_norm=m_l1_mix_post_norm, m_l1_w_in=m_l1_w_in, m_l1_ssm_conv_w=m_l1_ssm_conv_w, m_l1_ssm_conv_b=m_l1_ssm_conv_b, m_l1_dt_bias=m_l1_dt_bias, m_l1_a_log=m_l1_a_log, m_l1_d_skip=m_l1_d_skip, m_l1_gate_norm=m_l1_gate_norm, m_l1_w_out=m_l1_w_out, m_l1_ffn_pre_norm=m_l1_ffn_pre_norm, m_l1_ffn_post_norm=m_l1_ffn_post_norm, m_l1_ffn_w_up=m_l1_ffn_w_up, m_l1_ffn_conv_w=m_l1_ffn_conv_w, m_l1_ffn_conv_b=m_l1_ffn_conv_b, m_l1_ffn_w_down=m_l1_ffn_w_down, v_meta_tokens=v_meta_tokens, v_l0_mix_pre_norm=v_l0_mix_pre_norm, v_l0_mix_post_norm=v_l0_mix_post_norm, v_l0_w_in=v_l0_w_in, v_l0_lru_conv_w=v_l0_lru_conv_w, v_l0_lru_conv_b=v_l0_lru_conv_b, v_l0_lru_w_a=v_l0_lru_w_a, v_l0_lru_b_a=v_l0_lru_b_a, v_l0_lru_w_x=v_l0_lru_w_x, v_l0_lru_b_x=v_l0_lru_b_x, v_l0_lru_lambda=v_l0_lru_lambda, v_l0_attn_sinks=v_l0_attn_sinks, v_l0_w_out=v_l0_w_out, v_l0_ffn_pre_norm=v_l0_ffn_pre_norm, v_l0_ffn_post_norm=v_l0_ffn_post_norm, v_l0_ffn_w_up=v_l0_ffn_w_up, v_l0_ffn_conv_w=v_l0_ffn_conv_w, v_l0_ffn_conv_b=v_l0_ffn_conv_b, v_l0_ffn_w_down=v_l0_ffn_w_down, v_l1_mix_pre_norm=v_l1_mix_pre_norm, v_l1_mix_post_norm=v_l1_mix_post_norm, v_l1_w_in=v_l1_w_in, v_l1_ssm_conv_w=v_l1_ssm_conv_w, v_l1_ssm_conv_b=v_l1_ssm_conv_b, v_l1_dt_bias=v_l1_dt_bias, v_l1_a_log=v_l1_a_log, v_l1_d_skip=v_l1_d_skip, v_l1_gate_norm=v_l1_gate_norm, v_l1_w_out=v_l1_w_out, v_l1_ffn_pre_norm=v_l1_ffn_pre_norm, v_l1_ffn_post_norm=v_l1_ffn_post_norm, v_l1_ffn_w_up=v_l1_ffn_w_up, v_l1_ffn_conv_w=v_l1_ffn_conv_w, v_l1_ffn_conv_b=v_l1_ffn_conv_b, v_l1_ffn_w_down=v_l1_ffn_w_down)
    weights = {n: given[n] for n in TWIN_WEIGHTS}
    shared = {n: given[n] for n in SHARED_INPUTS}
    per_example = {n: given[n] for n in ['x']}
    grad_fn = _jax.value_and_grad(_loss, argnums=(0, 1))

    def one_microbatch(ex, loss_target):
        ex = dict(ex)
        diff = ex.pop(TWIN_DIFF_INPUT)
        return grad_fn(weights, diff, {**shared, **ex}, loss_target)

    if N_MICROBATCH == 1:
        loss, (grad_w, grad_x) = one_microbatch(per_example, given["loss_target"])
    else:
        def body(carry, xs):
            loss_sum, grad_sum = carry
            l_k, (gw_k, gx_k) = one_microbatch(xs[0], xs[1])
            with _jax.named_scope("update"):
                return (loss_sum + l_k, _jax.tree.map(_jnp.add, grad_sum, gw_k)), gx_k

        init = (_jnp.zeros((), _jnp.float32), _jax.tree.map(_jnp.zeros_like, weights))
        (loss, grad_w), grad_x = _jax.lax.scan(body, init, (per_example, given["loss_target"]))
    with _jax.named_scope("update"):
        delta_w, new_m, new_v = {}, {}, {}
        for n in TWIN_WEIGHTS:
            delta_w[n], new_m[n], new_v[n] = _adamw(weights[n], grad_w[n], given["m_" + n], given["v_" + n])
    return (loss, grad_x, *[grad_w[n] for n in TWIN_WEIGHTS], *[delta_w[n] for n in TWIN_WEIGHTS],
            *[new_m[n] for n in TWIN_WEIGHTS], *[new_v[n] for n in TWIN_WEIGHTS])
```

```python
import functools
import math

import jax
import jax.numpy as jnp
from jax import lax
from jax.experimental import pallas as pl
from jax.experimental.pallas import tpu as pltpu

F32 = jnp.float32
BF16 = jnp.bfloat16

D_MODEL = 1024
N_META = 16
BLOCK = 128
PAD = BLOCK - N_META
EPS = 1e-6
LRU_BLOCKS = 8
LRU_BS = 128
LRU_C = 8.0
N_Q_HEADS = 16
N_KV_HEADS = 2
HEAD_DIM = 64
Q_PER_KV = 8
WINDOW = 128
D_SSM = 2048
SSD_HEADS = 32
SSD_GROUPS = 8
SSD_HPG = 4
SSD_P = 64
SSD_N = 128
D_FF = 2816
NEG = -1e30
LANES = 128
SUBLANES = 8

ADAM_LR = 0.001
ADAM_B1 = 0.9
ADAM_B2 = 0.999
ADAM_EPS = 1e-08
ADAM_WD = 0.01
ADAM_STEP = 10

MESH = pl.DeviceIdType.MESH
N_CHIPS = 4
N_DEV = 8


def _pick(n, cands):
    for c in cands:
        if n % c == 0:
            return c
    raise ValueError(f"no tile for {n} in {cands}")


def _col_tile(n, limit=1792):
    best = None
    for t in range(LANES, min(n, limit) + 1, LANES):
        if n % t == 0:
            best = t
    if best is None:
        raise ValueError(f"no lane tile for {n}")
    return best


def _sigmoid(x):
    return 1.0 / (1.0 + jnp.exp(-x))


def _log1p(e):
    u = 1.0 + e
    return jnp.where(u == 1.0, e, jnp.log(u) * (e / jnp.where(u == 1.0, 1.0, u - 1.0)))


def _softplus(x):
    return jnp.maximum(x, 0.0) + _log1p(jnp.exp(-jnp.abs(x)))


def _neg_expm1(x):
    poly = x * (1.0 + x * (0.5 + x * (1.0 / 6.0 + x * (1.0 / 24.0 + x * (1.0 / 120.0)))))
    return -jnp.where(x > -0.05, poly, jnp.exp(x) - 1.0)


_GELU_C = math.sqrt(2.0 / math.pi)


def _gelu(x):
    t = jnp.tanh(_GELU_C * (x + 0.044715 * x * x * x))
    return 0.5 * x * (1.0 + t)


def _gelu_and_grad(x):
    x2 = x * x
    t = jnp.tanh(_GELU_C * (x + 0.044715 * x * x2))
    g = 0.5 * x * (1.0 + t)
    dg = 0.5 * (1.0 + t) + 0.5 * x * (1.0 - t * t) * _GELU_C * (1.0 + 3.0 * 0.044715 * x2)
    return g, dg


def _silu_and_grad(x):
    s = _sigmoid(x)
    return x * s, s * (1.0 + x * (1.0 - s))


def _dot(a, b):
    return jnp.dot(a, b, preferred_element_type=F32)


def _dot_nt(a, b):
    return lax.dot_general(a, b, (((1,), (1,)), ((), ())), preferred_element_type=F32)


def _dot_tn(a, b):
    return lax.dot_general(a, b, (((0,), (0,)), ((), ())), preferred_element_type=F32)


def _row_iota(t):
    return lax.broadcasted_iota(jnp.int32, (t, 1), 0)


def _scan_fwd(a, u, t):
    row = _row_iota(t)
    d = 1
    while d < t:
        m = row >= d
        u_sh = jnp.where(m, pltpu.roll(u, d, 0), 0.0)
        a_sh = jnp.where(m, pltpu.roll(a, d, 0), 1.0)
        u = u + a * u_sh
        a = a * a_sh
        d *= 2
    return a, u


def _scan_rev(c, x, t):
    row = _row_iota(t)
    d = 1
    while d < t:
        m = row < t - d
        x_sh = jnp.where(m, pltpu.roll(x, t - d, 0), 0.0)
        c_sh = jnp.where(m, pltpu.roll(c, t - d, 0), 1.0)
        x = x + c * x_sh
        c = c * c_sh
        d *= 2
    return c, x


def _cumsum_rows(x, t):
    row = _row_iota(t)
    d = 1
    while d < t:
        x = x + jnp.where(row >= d, pltpu.roll(x, d, 0), 0.0)
        d *= 2
    return x


def _rev_cumsum_rows(x, t):
    row = _row_iota(t)
    d = 1
    while d < t:
        x = x + jnp.where(row < t - d, pltpu.roll(x, t - d, 0), 0.0)
        d *= 2
    return x


def _rms_bwd(x, g, dy):
    rs = lax.rsqrt(jnp.mean(x * x, axis=-1, keepdims=True) + EPS)
    gy = dy * g
    dx = rs * gy - x * (rs * rs * rs) * jnp.mean(x * gy, axis=-1, keepdims=True)
    return dx, dy * x * rs


def _mm_nn(a, w, out_dtype, name):
    m, k = a.shape
    n = w.shape[1]
    tm = _pick(m, (640, 512, 256, 128))
    tn = _col_tile(n)

    def body(a_ref, w_ref, o_ref):
        o_ref[...] = _dot(a_ref[...].astype(BF16), w_ref[...]).astype(o_ref.dtype)

    return pl.pallas_call(
        body, name=name, grid=(n // tn, m // tm),
        in_specs=[pl.BlockSpec((tm, k), lambda j, i: (i, 0)),
                  pl.BlockSpec((k, tn), lambda j, i: (0, j))],
        out_specs=pl.BlockSpec((tm, tn), lambda j, i: (i, j)),
        out_shape=jax.ShapeDtypeStruct((m, n), out_dtype),
        compiler_params=pltpu.CompilerParams(dimension_semantics=("parallel", "parallel")),
    )(a, w)


def _mm_nt(dy, w, name):
    m, n = dy.shape
    k = w.shape[0]
    wide = n > 3328
    tm = _pick(m, (320, 256, 128)) if wide else _pick(m, (640, 512, 256, 128))
    tk = _col_tile(k, 512 if wide else 1408)

    def body(dy_ref, w_ref, o_ref):
        o_ref[...] = _dot_nt(dy_ref[...].astype(BF16), w_ref[...])

    return pl.pallas_call(
        body, name=name, grid=(k // tk, m // tm),
        in_specs=[pl.BlockSpec((tm, n), lambda j, i: (i, 0)),
                  pl.BlockSpec((tk, n), lambda j, i: (j, 0))],
        out_specs=pl.BlockSpec((tm, tk), lambda j, i: (i, j)),
        out_shape=jax.ShapeDtypeStruct((m, k), F32),
        compiler_params=pltpu.CompilerParams(dimension_semantics=("parallel", "parallel")),
    )(dy, w)


def _mm_tn(a, dy, name):
    m, k = a.shape
    n = dy.shape[1]
    tm = _pick(m, (640, 512, 256, 128))
    tk = _col_tile(k, 1408)
    tn = _col_tile(n, 1664)

    def body(a_ref, dy_ref, o_ref):
        @pl.when(pl.program_id(2) == 0)
        def _():
            o_ref[...] = jnp.zeros_like(o_ref)

        o_ref[...] += _dot_tn(a_ref[...].astype(BF16), dy_ref[...].astype(BF16))

    return pl.pallas_call(
        body, name=name, grid=(k // tk, n // tn, m // tm),
        in_specs=[pl.BlockSpec((tm, tk), lambda kk, j, i: (i, kk)),
                  pl.BlockSpec((tm, tn), lambda kk, j, i: (i, j))],
        out_specs=pl.BlockSpec((tk, tn), lambda kk, j, i: (kk, j)),
        out_shape=jax.ShapeDtypeStruct((k, n), F32),
        compiler_params=pltpu.CompilerParams(
            dimension_semantics=("parallel", "parallel", "arbitrary")),
    )(a, dy)


def _rowcall(name, body, lp, tm, rows=(), prevs=(), vecs=(), outs=(), accs=(), scratch=(),
             reverse=False, seq=False):
    nt = lp // tm
    hb = tm // SUBLANES

    def ri(i):
        return nt - 1 - i if reverse else i

    in_specs, args = [], []
    for arr, w, cb in rows:
        in_specs.append(pl.BlockSpec((tm, w), lambda i, cb=cb: (ri(i), cb)))
        args.append(arr)
    for arr, w, cb in prevs:
        in_specs.append(pl.BlockSpec((SUBLANES, w), lambda i, cb=cb: (jnp.maximum(ri(i) * hb - 1, 0), cb)))
        args.append(arr)
    for arr in vecs:
        in_specs.append(pl.BlockSpec(arr.shape, lambda i, nd=arr.ndim: (0,) * nd))
        args.append(arr)
    out_shape, out_specs = [], []
    for w, dt in outs:
        out_shape.append(jax.ShapeDtypeStruct((lp, w), dt))
        out_specs.append(pl.BlockSpec((tm, w), lambda i: (ri(i), 0)))
    for shp, dt in accs:
        out_shape.append(jax.ShapeDtypeStruct(shp, dt))
        out_specs.append(pl.BlockSpec(shp, lambda i, nd=len(shp): (0,) * nd))

    def kern(*refs):
        i = pl.program_id(0)
        body(ri(i), i == 0, *refs)

    sem = ("arbitrary",) if (seq or accs) else ("parallel",)
    res = pl.pallas_call(
        kern, name=name, grid=(nt,), in_specs=in_specs, out_specs=out_specs,
        out_shape=out_shape, scratch_shapes=list(scratch),
        compiler_params=pltpu.CompilerParams(dimension_semantics=sem),
    )(*args)
    return res


def _acc_add(first, ref, val):
    @pl.when(first)
    def _():
        ref[...] = jnp.zeros_like(ref)

    ref[...] += val


def _real_rows(r, tm):
    return (r * tm + _row_iota(tm)) >= PAD


def _rmsnorm_fwd(h, g, name):
    lp, d = h.shape
    tm = _pick(lp, (640, 512, 256, 128))

    def body(r, first, h_ref, g_ref, u_ref):
        x = h_ref[...]
        rs = lax.rsqrt(jnp.mean(x * x, axis=-1, keepdims=True) + EPS)
        u_ref[...] = (x * rs * g_ref[...]).astype(u_ref.dtype)

    return _rowcall(name, body, lp, tm, rows=[(h, d, 0)], vecs=[g], outs=[(d, BF16)])[0]


def _postnorm_res_fwd(h, o, g, name):
    lp, d = h.shape
    tm = _pick(lp, (640, 512, 256, 128))

    def body(r, first, h_ref, o_ref, g_ref, out_ref):
        x = o_ref[...]
        rs = lax.rsqrt(jnp.mean(x * x, axis=-1, keepdims=True) + EPS)
        out_ref[...] = jnp.where(_real_rows(r, tm), h_ref[...] + x * rs * g_ref[...], 0.0)

    return _rowcall(name, body, lp, tm, rows=[(h, d, 0), (o, d, 0)], vecs=[g], outs=[(d, F32)])[0]


def _postnorm_bwd(o, g, dh, name):
    lp, d = o.shape
    tm = _pick(lp, (640, 512, 256, 128))

    def body(r, first, o_ref, dh_ref, g_ref, do_ref, dg_ref):
        dx, dgt = _rms_bwd(o_ref[...], g_ref[...], dh_ref[...])
        do_ref[...] = dx.astype(do_ref.dtype)
        _acc_add(first, dg_ref, jnp.sum(dgt, axis=0, keepdims=True))

    return _rowcall(name, body, lp, tm, rows=[(o, d, 0), (dh, d, 0)], vecs=[g],
                    outs=[(d, BF16)], accs=[((1, d), F32)])


def _prenorm_bwd(h, g, du, dh_res, name):
    lp, d = h.shape
    tm = _pick(lp, (640, 512, 256, 128))

    def body(r, first, h_ref, du_ref, dres_ref, g_ref, dh_ref, dg_ref):
        dx, dgt = _rms_bwd(h_ref[...], g_ref[...], du_ref[...])
        dh_ref[...] = jnp.where(_real_rows(r, tm), dres_ref[...] + dx, 0.0)
        _acc_add(first, dg_ref, jnp.sum(dgt, axis=0, keepdims=True))

    return _rowcall(name, body, lp, tm, rows=[(h, d, 0), (du, d, 0), (dh_res, d, 0)], vecs=[g],
                    outs=[(d, F32)], accs=[((1, d), F32)])


def _loss_fwd_bwd(h, tgt, name):
    lp, d = h.shape
    tm = _pick(lp, (640, 512, 256, 128))

    def body(r, first, h_ref, t_ref, dh_ref, ls_ref):
        tok = (r * tm + _row_iota(tm)) >= BLOCK
        e = jnp.where(tok, h_ref[...] - t_ref[...], 0.0)
        dh_ref[...] = e * (1.0 / d)
        _acc_add(first, ls_ref, jnp.sum(e * e, axis=0, keepdims=True))

    return _rowcall(name, body, lp, tm, rows=[(h, d, 0), (tgt, d, 0)],
                    outs=[(d, F32)], accs=[((1, d), F32)])


def _conv_tiles(lp, width):
    wc = _col_tile(width, 1408)
    tm = _pick(lp, (320, 256, 128))
    return tm, wc


def _conv_fwd(x, col_off, width, w, b, name):
    lp = x.shape[0]
    kk = w.shape[0]
    tm, wc = _conv_tiles(lp, width)
    offb = col_off // wc
    assert col_off % wc == 0
    hb = tm // SUBLANES

    def body(x_ref, xp_ref, w_ref, b_ref, y_ref):
        i = pl.program_id(1)
        xv = x_ref[...]
        halo = jnp.where(i > 0, xp_ref[...], 0.0)
        xx = jnp.concatenate([halo, xv], axis=0)
        acc = b_ref[...] + w_ref[kk - 1:kk, :] * xv
        for j in range(1, kk):
            acc = acc + w_ref[kk - 1 - j:kk - j, :] * pltpu.roll(xx, j, 0)[SUBLANES:, :]
        y_ref[...] = acc

    return pl.pallas_call(
        body, name=name, grid=(width // wc, lp // tm),
        in_specs=[pl.BlockSpec((tm, wc), lambda j, i: (i, offb + j)),
                  pl.BlockSpec((SUBLANES, wc), lambda j, i: (jnp.maximum(i * hb - 1, 0), offb + j)),
                  pl.BlockSpec((kk, wc), lambda j, i: (0, j)),
                  pl.BlockSpec((1, wc), lambda j, i: (0, j))],
        out_specs=pl.BlockSpec((tm, wc), lambda j, i: (i, j)),
        out_shape=jax.ShapeDtypeStruct((lp, width), F32),
        compiler_params=pltpu.CompilerParams(dimension_semantics=("parallel", "parallel")),
    )(x, x, w, b)


def _conv_bwd(x, col_off, width, dy, w, name):
    lp = x.shape[0]
    kk = w.shape[0]
    tm, wc = _conv_tiles(lp, width)
    offb = col_off // wc
    assert col_off % wc == 0
    hb = tm // SUBLANES
    nb8 = lp // SUBLANES

    def body(x_ref, xp_ref, dy_ref, dn_ref, w_ref, dx_ref, dw_ref, db_ref):
        i = pl.program_id(1)
        last = pl.num_programs(1) - 1
        xv = x_ref[...]
        dyv = dy_ref[...]
        xx = jnp.concatenate([jnp.where(i > 0, xp_ref[...], 0.0), xv], axis=0)
        dd = jnp.concatenate([dyv, jnp.where(i < last, dn_ref[...], 0.0)], axis=0)
        dx = w_ref[kk - 1:kk, :] * dyv
        rows = [jnp.sum(dyv * xv, axis=0, keepdims=True)]
        for m in range(1, kk):
            dx = dx + w_ref[kk - 1 - m:kk - m, :] * pltpu.roll(dd, tm + SUBLANES - m, 0)[:tm, :]
            rows.append(jnp.sum(dyv * pltpu.roll(xx, m, 0)[SUBLANES:, :], axis=0, keepdims=True))
        dx_ref[...] = dx.astype(dx_ref.dtype)
        dwp = jnp.concatenate(rows[::-1] + [jnp.zeros((SUBLANES - kk, wc), F32)], axis=0)

        @pl.when(i == 0)
        def _():
            dw_ref[...] = jnp.zeros_like(dw_ref)
            db_ref[...] = jnp.zeros_like(db_ref)

        dw_ref[...] += dwp
        db_ref[...] += jnp.sum(dyv, axis=0, keepdims=True)

    return pl.pallas_call(
        body, name=name, grid=(width // wc, lp // tm),
        in_specs=[pl.BlockSpec((tm, wc), lambda j, i: (i, offb + j)),
                  pl.BlockSpec((SUBLANES, wc), lambda j, i: (jnp.maximum(i * hb - 1, 0), offb + j)),
                  pl.BlockSpec((tm, wc), lambda j, i: (i, j)),
                  pl.BlockSpec((SUBLANES, wc), lambda j, i: (jnp.minimum((i + 1) * hb, nb8 - 1), j)),
                  pl.BlockSpec((kk, wc), lambda j, i: (0, j))],
        out_specs=[pl.BlockSpec((tm, wc), lambda j, i: (i, j)),
                   pl.BlockSpec((SUBLANES, wc), lambda j, i: (0, j)),
                   pl.BlockSpec((1, wc), lambda j, i: (0, j))],
        out_shape=[jax.ShapeDtypeStruct((lp, width), BF16),
                   jax.ShapeDtypeStruct((SUBLANES, width), F32),
                   jax.ShapeDtypeStruct((1, width), F32)],
        compiler_params=pltpu.CompilerParams(dimension_semantics=("parallel", "arbitrary")),
    )(x, x, dy, dy, w)


def _ffn_act_fwd(hc, name):
    lp = hc.shape[0]
    tm = _pick(lp, (320, 256, 128))

    def body(r, first, g_ref, u_ref, a_ref):
        a_ref[...] = (_gelu(g_ref[...]) * u_ref[...]).astype(a_ref.dtype)

    return _rowcall(name, body, lp, tm, rows=[(hc, D_FF, 0), (hc, D_FF, 1)], outs=[(D_FF, BF16)])[0]


def _ffn_act_bwd(hc, dact, name):
    lp = hc.shape[0]
    tm = BLOCK

    def body(r, first, g_ref, u_ref, da_ref, dh_ref):
        gl, dgl = _gelu_and_grad(g_ref[...])
        da = da_ref[...]
        dh_ref[:, :D_FF] = da * u_ref[...] * dgl
        dh_ref[:, D_FF:] = da * gl

    return _rowcall(name, body, lp, tm, rows=[(hc, D_FF, 0), (hc, D_FF, 1), (dact, D_FF, 0)],
                    outs=[(2 * D_FF, F32)])[0]


def _lru_gates(x, wa_ref, wx_ref, ba, bx, lam):
    xb = x.astype(BF16)
    za, zx = [], []
    for n in range(LRU_BLOCKS):
        xs = xb[:, n * LRU_BS:(n + 1) * LRU_BS]
        za.append(_dot(xs, wa_ref[n]))
        zx.append(_dot(xs, wx_ref[n]))
    r = _sigmoid(jnp.concatenate(za, axis=1) + ba)
    ig = _sigmoid(jnp.concatenate(zx, axis=1) + bx)
    sp = _softplus(-lam)
    log_a = -LRU_C * r * sp
    a = jnp.exp(log_a)
    om = _neg_expm1(2.0 * log_a)
    mult = jnp.sqrt(om)
    return xb, r, ig, sp, a, om, mult


def _lru_fwd(proj, xrc, wa, wx, ba, bx, lam, name):
    lp, d = xrc.shape
    tm = BLOCK

    def body(r_idx, first, gate_ref, x_ref, wa_ref, wx_ref, ba_ref, bx_ref, lam_ref,
             y_ref, h_ref, carry):
        @pl.when(first)
        def _():
            carry[...] = jnp.zeros_like(carry)

        x = x_ref[...]
        _, _, ig, _, a, _, mult = _lru_gates(x, wa_ref, wx_ref, ba_ref[...], bx_ref[...], lam_ref[...])
        u = jnp.where(_real_rows(r_idx, tm), mult * ig * x, 0.0)
        acum, hloc = _scan_fwd(a, u, tm)
        h = hloc + acum * carry[0:1, :]
        h_ref[...] = h
        carry[0:1, :] = h[tm - 1:tm, :]
        y_ref[...] = (_gelu(gate_ref[...]) * h).astype(y_ref.dtype)

    return _rowcall(name, body, lp, tm, rows=[(proj, d, 0), (xrc, d, 0)],
                    vecs=[wa, wx, ba, bx, lam], outs=[(d, BF16), (d, F32)],
                    scratch=[pltpu.VMEM((SUBLANES, d), F32)], seq=True)


def _lru_bwd(proj, xrc, hl, dmix, wa, wx, ba, bx, lam, name):
    lp, d = xrc.shape
    tm = BLOCK

    def body(r_idx, first, gate_ref, x_ref, h_ref, dy_ref, hp_ref, wa_ref, wx_ref, ba_ref, bx_ref,
             lam_ref, dgate_ref, dx_ref, dwa_ref, dwx_ref, dba_ref, dbx_ref, dlam_ref, carry):
        @pl.when(first)
        def _():
            carry[...] = jnp.zeros_like(carry)
            dwa_ref[...] = jnp.zeros_like(dwa_ref)
            dwx_ref[...] = jnp.zeros_like(dwx_ref)
            dba_ref[...] = jnp.zeros_like(dba_ref)
            dbx_ref[...] = jnp.zeros_like(dbx_ref)
            dlam_ref[...] = jnp.zeros_like(dlam_ref)

        x = x_ref[...]
        lam = lam_ref[...]
        xb, r, ig, sp, a, om, mult = _lru_gates(x, wa_ref, wx_ref, ba_ref[...], bx_ref[...], lam)
        h = h_ref[...]
        dy = dy_ref[...]
        gl, dgl = _gelu_and_grad(gate_ref[...])
        dgate_ref[...] = (dy * h * dgl).astype(dgate_ref.dtype)
        row = _row_iota(tm)
        lastrow = row == tm - 1
        xg = dy * gl + jnp.where(lastrow, carry[0:1, :], 0.0)
        c = jnp.where(lastrow, 1.0, pltpu.roll(a, tm - 1, 0))
        _, g = _scan_rev(c, xg, tm)
        carry[0:1, :] = a[0:1, :] * g[0:1, :]
        hprev_in = jnp.where(r_idx > 0, hp_ref[SUBLANES - 1:SUBLANES, :], 0.0)
        hprev = jnp.where(row == 0, hprev_in, pltpu.roll(h, 1, 0))
        du = jnp.where(_real_rows(r_idx, tm), g, 0.0)
        da = g * hprev
        dmult = du * ig * x
        dig = du * mult * x
        dxv = du * mult * ig
        e2 = 1.0 - om
        dlog_a = da * a - dmult * e2 / mult
        dr = dlog_a * (-LRU_C) * sp
        dsp = jnp.sum(dlog_a * (-LRU_C) * r, axis=0, keepdims=True)
        dlam_ref[...] += -dsp * _sigmoid(-lam)
        dza = dr * r * (1.0 - r)
        dzx = dig * ig * (1.0 - ig)
        dba_ref[...] += jnp.sum(dza, axis=0, keepdims=True)
        dbx_ref[...] += jnp.sum(dzx, axis=0, keepdims=True)
        dzab = dza.astype(BF16)
        dzxb = dzx.astype(BF16)
        parts = []
        for n in range(LRU_BLOCKS):
            sl = slice(n * LRU_BS, (n + 1) * LRU_BS)
            dwa_ref[n] += _dot_tn(xb[:, sl], dzab[:, sl])
            dwx_ref[n] += _dot_tn(xb[:, sl], dzxb[:, sl])
            parts.append(_dot_nt(dzab[:, sl], wa_ref[n]) + _dot_nt(dzxb[:, sl], wx_ref[n]))
        dx_ref[...] = dxv + jnp.concatenate(parts, axis=1)

    return _rowcall(name, body, lp, tm,
                    rows=[(proj, d, 0), (xrc, d, 0), (hl, d, 0), (dmix, d, 0)],
                    prevs=[(hl, d, 0)], vecs=[wa, wx, ba, bx, lam],
                    outs=[(d, BF16), (d, F32)],
                    accs=[((LRU_BLOCKS, LRU_BS, LRU_BS), F32), ((LRU_BLOCKS, LRU_BS, LRU_BS), F32),
                          ((1, d), F32), ((1, d), F32), ((1, d), F32)],
                    scratch=[pltpu.VMEM((SUBLANES, d), F32)], reverse=True, seq=True)


_SLOPES = [2.0 ** (-8.0 * (h + 1) / N_Q_HEADS) for h in range(N_Q_HEADS)]
_QK_SCALE = HEAD_DIM ** -0.5
_QCOL = 2 * D_MODEL // D_MODEL
_KCOL = (3 * D_MODEL) // LANES
_VCOL = _KCOL + 1


def _attn_masks(n):
    start = pl.multiple_of(jnp.maximum(n - 1, 0) * BLOCK, BLOCK)
    qi = n * BLOCK + lax.broadcasted_iota(jnp.int32, (BLOCK, 2 * BLOCK), 0)
    kj = start + lax.broadcasted_iota(jnp.int32, (BLOCK, 2 * BLOCK), 1)
    dist = qi - kj
    ok = (kj >= BLOCK) & (dist >= 0) & (dist < WINDOW)
    dm = (n * BLOCK - PAD + lax.broadcasted_iota(jnp.int32, (BLOCK, N_META), 0)
          - lax.broadcasted_iota(jnp.int32, (BLOCK, N_META), 1))
    okm = dm >= 0
    return start, ok, dist.astype(F32), okm, jnp.minimum(dm, WINDOW).astype(F32)


def _attn_probs(qh, kg, kmg, sink, slope, ok, distf, okm, dmf):
    s = jnp.where(ok, _dot_nt(qh, kg) * _QK_SCALE - slope * distf, NEG)
    sm = jnp.where(okm, _dot_nt(qh, kmg) * _QK_SCALE - slope * dmf, NEG)
    mx = jnp.maximum(jnp.maximum(jnp.max(s, axis=-1, keepdims=True),
                                 jnp.max(sm, axis=-1, keepdims=True)), sink)
    p = jnp.exp(s - mx)
    pm = jnp.exp(sm - mx)
    ps = jnp.exp(sink - mx)
    inv = 1.0 / (jnp.sum(p, axis=-1, keepdims=True) + jnp.sum(pm, axis=-1, keepdims=True) + ps)
    return p * inv, pm * inv, ps * inv


def _attn_fwd(proj, sinks, name):
    lp = proj.shape[0]
    nblk = lp // BLOCK

    def body(q_ref, k_ref, v_ref, sink_ref, o_ref):
        n = pl.program_id(0)
        start, ok, distf, okm, dmf = _attn_masks(n)
        kb = k_ref[pl.ds(start, 2 * BLOCK), :].astype(BF16)
        vb = v_ref[pl.ds(start, 2 * BLOCK), :].astype(BF16)
        km = k_ref[PAD:BLOCK, :].astype(BF16)
        vm = v_ref[PAD:BLOCK, :].astype(BF16)
        for h in range(N_Q_HEADS):
            g = h // Q_PER_KV
            gs = slice(g * HEAD_DIM, (g + 1) * HEAD_DIM)
            qh = q_ref[:, h * HEAD_DIM:(h + 1) * HEAD_DIM].astype(BF16)
            pn, pmn, _ = _attn_probs(qh, kb[:, gs], km[:, gs], sink_ref[0:1, h:h + 1], _SLOPES[h],
                                     ok, distf, okm, dmf)
            o = _dot(pn.astype(BF16), vb[:, gs]) + _dot(pmn.astype(BF16), vm[:, gs])
            o_ref[:, h * HEAD_DIM:(h + 1) * HEAD_DIM] = o.astype(o_ref.dtype)

    return pl.pallas_call(
        body, name=name, grid=(nblk,),
        in_specs=[pl.BlockSpec((BLOCK, D_MODEL), lambda n: (n, _QCOL)),
                  pl.BlockSpec((lp, LANES), lambda n: (0, _KCOL)),
                  pl.BlockSpec((lp, LANES), lambda n: (0, _VCOL)),
                  pl.BlockSpec(sinks.shape, lambda n: (0, 0))],
        out_specs=pl.BlockSpec((BLOCK, D_MODEL), lambda n: (n, 0)),
        out_shape=jax.ShapeDtypeStruct((lp, D_MODEL), BF16),
        compiler_params=pltpu.CompilerParams(dimension_semantics=("parallel",)),
    )(proj, proj, proj, sinks)


def _attn_bwd(proj, sinks, dmix, name):
    lp = proj.shape[0]
    nblk = lp // BLOCK

    def body(q_ref, k_ref, v_ref, sink_ref, dy_ref, dq_ref, dk_ref, dv_ref, ds_ref):
        n = pl.program_id(0)

        @pl.when(n == 0)
        def _():
            dk_ref[...] = jnp.zeros_like(dk_ref)
            dv_ref[...] = jnp.zeros_like(dv_ref)
            ds_ref[...] = jnp.zeros_like(ds_ref)

        start, ok, distf, okm, dmf = _attn_masks(n)
        kb = k_ref[pl.ds(start, 2 * BLOCK), :].astype(BF16)
        vb = v_ref[pl.ds(start, 2 * BLOCK), :].astype(BF16)
        km = k_ref[PAD:BLOCK, :].astype(BF16)
        vm = v_ref[PAD:BLOCK, :].astype(BF16)
        lane16 = lax.broadcasted_iota(jnp.int32, (1, N_Q_HEADS), 1)
        dsink = jnp.zeros((1, N_Q_HEADS), F32)
        for g in range(N_KV_HEADS):
            gs = slice(g * HEAD_DIM, (g + 1) * HEAD_DIM)
            dk = jnp.zeros((2 * BLOCK, HEAD_DIM), F32)
            dv = jnp.zeros((2 * BLOCK, HEAD_DIM), F32)
            dkm = jnp.zeros((N_META, HEAD_DIM), F32)
            dvm = jnp.zeros((N_META, HEAD_DIM), F32)
            for hh in range(Q_PER_KV):
                h = g * Q_PER_KV + hh
                hs = slice(h * HEAD_DIM, (h + 1) * HEAD_DIM)
                qh = q_ref[:, hs].astype(BF16)
                doh = dy_ref[:, hs].astype(BF16)
                pn, pmn, psn = _attn_probs(qh, kb[:, gs], km[:, gs], sink_ref[0:1, h:h + 1],
                                           _SLOPES[h], ok, distf, okm, dmf)
                dp = _dot_nt(doh, vb[:, gs])
                dpm = _dot_nt(doh, vm[:, gs])
                delta = (jnp.sum(pn * dp, axis=-1, keepdims=True)
                         + jnp.sum(pmn * dpm, axis=-1, keepdims=True))
                dsb = (pn * (dp - delta)).astype(BF16)
                dsm = (pmn * (dpm - delta)).astype(BF16)
                dsink = dsink - jnp.where(lane16 == h, jnp.sum(psn * delta, axis=0, keepdims=True), 0.0)
                dq = (_dot(dsb, kb[:, gs]) + _dot(dsm, km[:, gs])) * _QK_SCALE
                dq_ref[:, hs] = dq.astype(dq_ref.dtype)
                dk = dk + _dot_tn(dsb, qh)
                dkm = dkm + _dot_tn(dsm, qh)
                dv = dv + _dot_tn(pn.astype(BF16), doh)
                dvm = dvm + _dot_tn(pmn.astype(BF16), doh)
            dk_ref[pl.ds(start, 2 * BLOCK), gs] += dk * _QK_SCALE
            dv_ref[pl.ds(start, 2 * BLOCK), gs] += dv
            dk_ref[PAD:BLOCK, gs] += dkm * _QK_SCALE
            dv_ref[PAD:BLOCK, gs] += dvm
        ds_ref[...] += dsink

    return pl.pallas_call(
        body, name=name, grid=(nblk,),
        in_specs=[pl.BlockSpec((BLOCK, D_MODEL), lambda n: (n, _QCOL)),
                  pl.BlockSpec((lp, LANES), lambda n: (0, _KCOL)),
                  pl.BlockSpec((lp, LANES), lambda n: (0, _VCOL)),
                  pl.BlockSpec(sinks.shape, lambda n: (0, 0)),
                  pl.BlockSpec((BLOCK, D_MODEL), lambda n: (n, 1))],
        out_specs=[pl.BlockSpec((BLOCK, D_MODEL), lambda n: (n, 0)),
                   pl.BlockSpec((lp, LANES), lambda n: (0, 0)),
                   pl.BlockSpec((lp, LANES), lambda n: (0, 0)),
                   pl.BlockSpec((1, N_Q_HEADS), lambda n: (0, 0))],
        out_shape=[jax.ShapeDtypeStruct((lp, D_MODEL), BF16),
                   jax.ShapeDtypeStruct((lp, LANES), F32),
                   jax.ShapeDtypeStruct((lp, LANES), F32),
                   jax.ShapeDtypeStruct((1, N_Q_HEADS), F32)],
        compiler_params=pltpu.CompilerParams(dimension_semantics=("arbitrary",)),
    )(proj, proj, proj, sinks, dmix)


_ZW = D_SSM
_XBC_W = D_SSM + 2 * SSD_GROUPS * SSD_N
_DT_COL = (_ZW + _XBC_W) // LANES
ODD_IN = _ZW + _XBC_W + SSD_HEADS
ODD_IN_PAD = _ZW + _XBC_W + LANES


def _ssm_prep_fwd(xc, proj, dt_bias, name):
    lp = xc.shape[0]
    tm = _pick(lp, (320, 256, 128))

    def body(r, first, xc_ref, dtr_ref, b_ref, act_ref, dt_ref):
        real = _real_rows(r, tm)
        act, _ = _silu_and_grad(xc_ref[...])
        act_ref[...] = jnp.where(real, act, 0.0)
        dt_ref[...] = jnp.where(real, _softplus(dtr_ref[...] + b_ref[...]), 0.0)

    return _rowcall(name, body, lp, tm, rows=[(xc, _XBC_W, 0), (proj, LANES, _DT_COL)],
                    vecs=[dt_bias], outs=[(_XBC_W, F32), (LANES, F32)])


def _ssm_prep_bwd(xc, proj, dt_bias, dxs, dxskip, db, dc, ddt, name):
    lp = xc.shape[0]
    tm = BLOCK

    def body(r, first, xc_ref, dtr_ref, dxs_ref, dsk_ref, db_ref, dc_ref, ddt_ref, b_ref,
             dxc_ref, ddtr_ref, dbias_ref):
        real = _real_rows(r, tm)
        _, ds = _silu_and_grad(xc_ref[...])
        dxc_ref[:, :D_SSM] = jnp.where(real, (dxs_ref[...] + dsk_ref[...]) * ds[:, :D_SSM], 0.0)
        dxc_ref[:, D_SSM:D_SSM + 1024] = jnp.where(real, db_ref[...] * ds[:, D_SSM:D_SSM + 1024], 0.0)
        dxc_ref[:, D_SSM + 1024:] = jnp.where(real, dc_ref[...] * ds[:, D_SSM + 1024:], 0.0)
        dd = jnp.where(real, ddt_ref[...] * _sigmoid(dtr_ref[...] + b_ref[...]), 0.0)
        ddtr_ref[...] = dd.astype(ddtr_ref.dtype)
        _acc_add(first, dbias_ref, jnp.sum(dd, axis=0, keepdims=True))

    return _rowcall(name, body, lp, tm,
                    rows=[(xc, _XBC_W, 0), (proj, LANES, _DT_COL), (dxs, D_SSM, 0), (dxskip, D_SSM, 0),
                          (db, 1024, 0), (dc, 1024, 0), (ddt, LANES, 0)],
                    vecs=[dt_bias], outs=[(_XBC_W, F32), (LANES, BF16)], accs=[((1, LANES), F32)])


def _ssd_common(dt, alog):
    a = -jnp.exp(alog)
    cs = _cumsum_rows(dt * a, BLOCK)
    cst = cs.T
    cl = cs[BLOCK - 1:BLOCK, :]
    tril = (lax.broadcasted_iota(jnp.int32, (BLOCK, BLOCK), 0)
            >= lax.broadcasted_iota(jnp.int32, (BLOCK, BLOCK), 1))
    return a, cs, cst, cl, jnp.exp(cs), jnp.exp(cl - cs), jnp.exp(cl), tril


def _head_cols(ecl, g):
    lane = lax.broadcasted_iota(jnp.int32, (1, SSD_HPG * SSD_P), 1)
    e = [ecl[:, SSD_HPG * g + hh:SSD_HPG * g + hh + 1] for hh in range(SSD_HPG)]
    return jnp.where(lane < SSD_P, e[0], jnp.where(lane < 2 * SSD_P, e[1],
                                                   jnp.where(lane < 3 * SSD_P, e[2], e[3])))


def _ssd_fwd(xbc, dt, alog, name):
    lp = xbc.shape[0]
    nc = lp // BLOCK
    gw = SSD_HPG * SSD_P

    def body(xs_ref, b_ref, c_ref, dt_ref, alog_ref, y_ref, so_ref, st, fx):
        n = pl.program_id(0)

        @pl.when(n == 0)
        def _():
            st[...] = jnp.zeros_like(st)

        dtv = dt_ref[...]
        _, cs, cst, cl, e, f, ecl, tril = _ssd_common(dtv, alog_ref[...])
        for g in range(SSD_GROUPS):
            bg = b_ref[:, g * SSD_N:(g + 1) * SSD_N].astype(BF16)
            cg = c_ref[:, g * SSD_N:(g + 1) * SSD_N].astype(BF16)
            gm = _dot_nt(cg, bg)
            stg = st[g]
            so_ref[0, g] = stg
            yoff = _dot(cg, stg.astype(BF16))
            for hh in range(SSD_HPG):
                h = SSD_HPG * g + hh
                hs = slice(h * SSD_P, (h + 1) * SSD_P)
                seg = cs[:, h:h + 1] - cst[h:h + 1, :]
                m = gm * jnp.exp(jnp.where(tril, seg, NEG))
                xdt = xs_ref[:, hs] * dtv[:, h:h + 1]
                y_ref[:, hs] = (_dot(m.astype(BF16), xdt.astype(BF16))
                                + e[:, h:h + 1] * yoff[:, hh * SSD_P:(hh + 1) * SSD_P])
                fx[:, hh * SSD_P:(hh + 1) * SSD_P] = f[:, h:h + 1] * xdt
            st[g] = stg * _head_cols(ecl, g) + _dot_tn(bg, fx[...].astype(BF16))

    return pl.pallas_call(
        body, name=name, grid=(nc,),
        in_specs=[pl.BlockSpec((BLOCK, D_SSM), lambda n: (n, 0)),
                  pl.BlockSpec((BLOCK, 1024), lambda n: (n, 2)),
                  pl.BlockSpec((BLOCK, 1024), lambda n: (n, 3)),
                  pl.BlockSpec((BLOCK, LANES), lambda n: (n, 0)),
                  pl.BlockSpec((1, LANES), lambda n: (0, 0))],
        out_specs=[pl.BlockSpec((BLOCK, D_SSM), lambda n: (n, 0)),
                   pl.BlockSpec((1, SSD_GROUPS, SSD_N, gw), lambda n: (n, 0, 0, 0))],
        out_shape=[jax.ShapeDtypeStruct((lp, D_SSM), F32),
                   jax.ShapeDtypeStruct((nc, SSD_GROUPS, SSD_N, gw), F32)],
        scratch_shapes=[pltpu.VMEM((SSD_GROUPS, SSD_N, gw), F32), pltpu.VMEM((BLOCK, gw), F32)],
        compiler_params=pltpu.CompilerParams(dimension_semantics=("arbitrary",)),
    )(xbc, xbc, xbc, dt, alog)


def _ssd_bwd(xbc, dt, alog, states, dy, name):
    lp = xbc.shape[0]
    nc = lp // BLOCK
    gw = SSD_HPG * SSD_P

    def body(xs_ref, b_ref, c_ref, dt_ref, alog_ref, dy_ref, st_ref,
             dxs_ref, db_ref, dc_ref, ddt_ref, dalog_ref, dst, edy, fx):
        i = pl.program_id(0)

        @pl.when(i == 0)
        def _():
            dst[...] = jnp.zeros_like(dst)
            dalog_ref[...] = jnp.zeros_like(dalog_ref)

        dtv = dt_ref[...]
        a, cs, cst, cl, e, f, ecl, tril = _ssd_common(dtv, alog_ref[...])
        lane = lax.broadcasted_iota(jnp.int32, (1, LANES), 1)
        sub = _row_iota(BLOCK)
        dcs = jnp.zeros((BLOCK, LANES), F32)
        dcst = jnp.zeros((LANES, BLOCK), F32)
        dcl = jnp.zeros((1, LANES), F32)
        ddtx = jnp.zeros((BLOCK, LANES), F32)
        for g in range(SSD_GROUPS):
            bg = b_ref[:, g * SSD_N:(g + 1) * SSD_N].astype(BF16)
            cg = c_ref[:, g * SSD_N:(g + 1) * SSD_N].astype(BF16)
            gm = _dot_nt(cg, bg)
            stg = st_ref[0, g]
            stb = stg.astype(BF16)
            dso = dst[g]
            dsob = dso.astype(BF16)
            yraw = _dot(cg, stb)
            dfx = _dot(bg, dsob)
            prodsum = jnp.sum(dso * stg, axis=0, keepdims=True)
            dgm = jnp.zeros((BLOCK, BLOCK), F32)
            for hh in range(SSD_HPG):
                h = SSD_HPG * g + hh
                hs = slice(h * SSD_P, (h + 1) * SSD_P)
                ls = slice(hh * SSD_P, (hh + 1) * SSD_P)
                eh = e[:, h:h + 1]
                fh = f[:, h:h + 1]
                dth = dtv[:, h:h + 1]
                xh = xs_ref[:, hs]
                xdt = xh * dth
                dyh = dy_ref[:, hs]
                dyb = dyh.astype(BF16)
                lam = jnp.exp(jnp.where(tril, cs[:, h:h + 1] - cst[h:h + 1, :], NEG))
                m = gm * lam
                dm = _dot_nt(dyb, xdt.astype(BF16))
                dxdt = _dot_tn(m.astype(BF16), dyb) + fh * dfx[:, ls]
                w = dm * m
                dgm = dgm + dm * lam
                dff = jnp.sum(dfx[:, ls] * xdt, axis=1, keepdims=True) * fh
                col = (jnp.sum(w, axis=1, keepdims=True)
                       + jnp.sum(dyh * yraw[:, ls], axis=1, keepdims=True) * eh - dff)
                onl = (lane == h).astype(F32)
                dcs = dcs + col * onl
                dcst = dcst - (sub == h).astype(F32) * jnp.sum(w, axis=0, keepdims=True)
                dclh = (jnp.sum(dff, axis=0, keepdims=True)
                        + ecl[:, h:h + 1] * jnp.sum(prodsum[:, ls], axis=1, keepdims=True))
                dcl = dcl + dclh * onl
                ddtx = ddtx + jnp.sum(dxdt * xh, axis=1, keepdims=True) * onl
                dxs_ref[:, hs] = dxdt * dth
                edy[:, ls] = eh * dyh
                fx[:, ls] = fh * xdt
            edyb = edy[...].astype(BF16)
            fxb = fx[...].astype(BF16)
            dgb = dgm.astype(BF16)
            dc_ref[:, g * SSD_N:(g + 1) * SSD_N] = _dot_nt(edyb, stb) + _dot(dgb, bg)
            db_ref[:, g * SSD_N:(g + 1) * SSD_N] = _dot_nt(fxb, dsob) + _dot_tn(dgb, cg)
            dst[g] = dso * _head_cols(ecl, g) + _dot_tn(cg, edyb)
        dcs = dcs + dcst.T + jnp.where(sub == BLOCK - 1, dcl, 0.0)
        dda = _rev_cumsum_rows(dcs, BLOCK)
        ddt_ref[...] = ddtx + dda * a
        dalog_ref[...] += jnp.sum(dda * dtv, axis=0, keepdims=True) * a

    rev = lambda i: nc - 1 - i
    return pl.pallas_call(
        body, name=name, grid=(nc,),
        in_specs=[pl.BlockSpec((BLOCK, D_SSM), lambda i: (rev(i), 0)),
                  pl.BlockSpec((BLOCK, 1024), lambda i: (rev(i), 2)),
                  pl.BlockSpec((BLOCK, 1024), lambda i: (rev(i), 3)),
                  pl.BlockSpec((BLOCK, LANES), lambda i: (rev(i), 0)),
                  pl.BlockSpec((1, LANES), lambda i: (0, 0)),
                  pl.BlockSpec((BLOCK, D_SSM), lambda i: (rev(i), 0)),
                  pl.BlockSpec((1, SSD_GROUPS, SSD_N, gw), lambda i: (rev(i), 0, 0, 0))],
        out_specs=[pl.BlockSpec((BLOCK, D_SSM), lambda i: (rev(i), 0)),
                   pl.BlockSpec((BLOCK, 1024), lambda i: (rev(i), 0)),
                   pl.BlockSpec((BLOCK, 1024), lambda i: (rev(i), 0)),
                   pl.BlockSpec((BLOCK, LANES), lambda i: (rev(i), 0)),
                   pl.BlockSpec((1, LANES), lambda i: (0, 0))],
        out_shape=[jax.ShapeDtypeStruct((lp, D_SSM), F32),
                   jax.ShapeDtypeStruct((lp, 1024), F32),
                   jax.ShapeDtypeStruct((lp, 1024), F32),
                   jax.ShapeDtypeStruct((lp, LANES), F32),
                   jax.ShapeDtypeStruct((1, LANES), F32)],
        scratch_shapes=[pltpu.VMEM((SSD_GROUPS, SSD_N, gw), F32),
                        pltpu.VMEM((BLOCK, gw), F32), pltpu.VMEM((BLOCK, gw), F32)],
        compiler_params=pltpu.CompilerParams(dimension_semantics=("arbitrary",)),
    )(xbc, xbc, xbc, dt, alog, dy, states)


_GN_GROUPS = 8
_GN_W = D_SSM // _GN_GROUPS


def _ssm_gate_fwd(yssd, xbc, proj, dskip, gnorm, name):
    lp = yssd.shape[0]
    tm = _pick(lp, (320, 256, 128))

    def body(r, first, y_ref, x_ref, z_ref, d_ref, g_ref, o_ref):
        sz, _ = _silu_and_grad(z_ref[...])
        y2 = (y_ref[...] + d_ref[...] * x_ref[...]) * sz
        for k in range(_GN_GROUPS):
            sl = slice(k * _GN_W, (k + 1) * _GN_W)
            yk = y2[:, sl]
            rs = lax.rsqrt(jnp.mean(yk * yk, axis=-1, keepdims=True) + EPS)
            o_ref[:, sl] = (yk * rs * g_ref[:, sl]).astype(o_ref.dtype)

    return _rowcall(name, body, lp, tm, rows=[(yssd, D_SSM, 0), (xbc, D_SSM, 0), (proj, D_SSM, 0)],
                    vecs=[dskip, gnorm], outs=[(D_SSM, BF16)])[0]


def _ssm_gate_bwd(yssd, xbc, proj, dskip, gnorm, dyn, name):
    lp = yssd.shape[0]
    tm = BLOCK

    def body(r, first, y_ref, x_ref, z_ref, dyn_ref, d_ref, g_ref,
             dy_ref, dx_ref, dz_ref, dd_ref, dg_ref):
        z = z_ref[...]
        sz, dsz = _silu_and_grad(z)
        xs = x_ref[...]
        y1 = y_ref[...] + d_ref[...] * xs
        y2 = y1 * sz
        dyn = dyn_ref[...]
        for k in range(_GN_GROUPS):
            sl = slice(k * _GN_W, (k + 1) * _GN_W)
            dx, dgt = _rms_bwd(y2[:, sl], g_ref[:, sl], dyn[:, sl])
            dy1 = dx * sz[:, sl]
            dy_ref[:, sl] = dy1
            dx_ref[:, sl] = dy1 * d_ref[:, sl]
            dz_ref[:, sl] = (dx * y1[:, sl] * dsz[:, sl]).astype(dz_ref.dtype)

            @pl.when(first)
            def _():
                dd_ref[:, sl] = jnp.zeros((1, _GN_W), F32)
                dg_ref[:, sl] = jnp.zeros((1, _GN_W), F32)

            dd_ref[:, sl] += jnp.sum(dy1 * xs[:, sl], axis=0, keepdims=True)
            dg_ref[:, sl] += jnp.sum(dgt, axis=0, keepdims=True)

    return _rowcall(name, body, lp, tm,
                    rows=[(yssd, D_SSM, 0), (xbc, D_SSM, 0), (proj, D_SSM, 0), (dyn, D_SSM, 0)],
                    vecs=[dskip, gnorm], outs=[(D_SSM, F32), (D_SSM, F32), (D_SSM, BF16)],
                    accs=[((1, D_SSM), F32), ((1, D_SSM), F32)])


def _adamw(w, g, m, v, name):
    r, c = w.shape
    tm = r if r <= 512 else _pick(r, (512, 352, 256, 128, 64, 32, 16, 8))
    c1 = 1.0 / (1.0 - ADAM_B1 ** ADAM_STEP)
    c2 = 1.0 / (1.0 - ADAM_B2 ** ADAM_STEP)

    def body(w_ref, g_ref, m_ref, v_ref, d_ref, nm_ref, nv_ref):
        gv = g_ref[...]
        nm = ADAM_B1 * m_ref[...] + (1.0 - ADAM_B1) * gv
        nv = ADAM_B2 * v_ref[...] + (1.0 - ADAM_B2) * (gv * gv)
        nm_ref[...] = nm
        nv_ref[...] = nv
        d_ref[...] = -ADAM_LR * ((nm * c1) / (jnp.sqrt(nv * c2) + ADAM_EPS) + ADAM_WD * w_ref[...])

    spec = pl.BlockSpec((tm, c), lambda i: (i, 0))
    return pl.pallas_call(
        body, name=name, grid=(r // tm,), in_specs=[spec] * 4, out_specs=[spec] * 3,
        out_shape=[jax.ShapeDtypeStruct((r, c), F32)] * 3,
        compiler_params=pltpu.CompilerParams(dimension_semantics=("parallel",)),
    )(w, g, m, v)


def _place():
    return lax.axis_index("x"), lax.axis_index("y"), lax.axis_index("c")


def _other_chips(x, y):
    return [(1 - x, y), (x, 1 - y), (1 - x, 1 - y)]


_ANY = pl.BlockSpec(memory_space=pl.ANY)


def _gather_chips(packs, name):
    n = len(packs)

    def body(*refs):
        ins, outs = refs[:n], refs[n:2 * n]
        send_sems, recv_sems, local_sems = refs[2 * n:]
        x, y, c = _place()
        me = 2 * x + y
        copies = []
        for p in range(n):
            loc = pltpu.make_async_copy(ins[p], outs[p].at[me], local_sems.at[p])
            loc.start()
            copies.append(loc)
        sends = []
        for p in range(n):
            for k, (px, py) in enumerate(_other_chips(x, y)):
                cp = pltpu.make_async_remote_copy(
                    src_ref=ins[p], dst_ref=outs[p].at[me],
                    send_sem=send_sems.at[3 * p + k], recv_sem=recv_sems.at[3 * p + k],
                    device_id=(px, py, c), device_id_type=MESH)
                cp.start()
                sends.append(cp)
        for p in range(n):
            for k, (px, py) in enumerate(_other_chips(x, y)):
                pltpu.make_async_remote_copy(
                    src_ref=ins[p], dst_ref=outs[p].at[2 * px + py],
                    send_sem=send_sems.at[3 * p + k], recv_sem=recv_sems.at[3 * p + k],
                    device_id=(px, py, c), device_id_type=MESH).wait_recv()
        for cp in sends:
            cp.wait_send()
        for cp in copies:
            cp.wait()

    return pl.pallas_call(
        body, name=name,
        in_specs=[_ANY] * n, out_specs=[_ANY] * n,
        out_shape=[jax.ShapeDtypeStruct((N_CHIPS,) + p.shape, p.dtype) for p in packs],
        scratch_shapes=[pltpu.SemaphoreType.DMA((3 * n,)), pltpu.SemaphoreType.DMA((3 * n,)),
                        pltpu.SemaphoreType.DMA((n,))],
    )(*packs)


def _sibling_half_swap(g, name):
    s, r, l = g.shape
    hr = r // 2

    def body(g_ref, o_ref, send_sem, recv_sem):
        x, y, c = _place()
        src = g_ref.at[:, pl.ds(pl.multiple_of((1 - c) * hr, SUBLANES), hr), :]
        cp = pltpu.make_async_remote_copy(src_ref=src, dst_ref=o_ref, send_sem=send_sem, recv_sem=recv_sem,
                                          device_id=(x, y, 1 - c), device_id_type=MESH)
        cp.start()
        cp.wait()

    return pl.pallas_call(
        body, name=name, in_specs=[_ANY], out_specs=_ANY,
        out_shape=jax.ShapeDtypeStruct((s, hr, l), g.dtype),
        scratch_shapes=[pltpu.SemaphoreType.DMA, pltpu.SemaphoreType.DMA],
    )(g)


def _chip_scatter(gc, name):
    s, h, l = gc.shape

    def body(g_ref, o_ref, send_sems, recv_sems):
        x, y, c = _place()
        sends = []
        for k, (px, py) in enumerate(_other_chips(x, y)):
            cp = pltpu.make_async_remote_copy(
                src_ref=g_ref.at[2 * px + py], dst_ref=o_ref.at[k],
                send_sem=send_sems.at[k], recv_sem=recv_sems.at[k],
                device_id=(px, py, c), device_id_type=MESH)
            cp.start()
            sends.append(cp)
        for cp in sends:
            cp.wait_recv()
        for cp in sends:
            cp.wait_send()

    return pl.pallas_call(
        body, name=name, in_specs=[_ANY], out_specs=_ANY,
        out_shape=jax.ShapeDtypeStruct((3, h, l), gc.dtype),
        scratch_shapes=[pltpu.SemaphoreType.DMA((3,)), pltpu.SemaphoreType.DMA((3,))],
    )(gc)


def _sibling_concat(t, name):
    h, l = t.shape

    def body(t_ref, o_ref, send_sem, recv_sem, local_sem):
        x, y, c = _place()
        loc = pltpu.make_async_copy(t_ref, o_ref.at[c], local_sem)
        loc.start()
        cp = pltpu.make_async_remote_copy(src_ref=t_ref, dst_ref=o_ref.at[c], send_sem=send_sem,
                                          recv_sem=recv_sem, device_id=(x, y, 1 - c), device_id_type=MESH)
        cp.start()
        pltpu.make_async_remote_copy(src_ref=t_ref, dst_ref=o_ref.at[1 - c], send_sem=send_sem,
                                     recv_sem=recv_sem, device_id=(x, y, 1 - c), device_id_type=MESH).wait_recv()
        cp.wait_send()
        loc.wait()

    return pl.pallas_call(
        body, name=name, in_specs=[_ANY], out_specs=_ANY,
        out_shape=jax.ShapeDtypeStruct((2, h, l), t.dtype),
        scratch_shapes=[pltpu.SemaphoreType.DMA, pltpu.SemaphoreType.DMA, pltpu.SemaphoreType.DMA],
    )(t)


def _allreduce_small(pack, name):
    r, l = pack.shape

    def body(p_ref, o_ref, land, send_sems, recv_sems):
        x, y, c = _place()
        me = 4 * x + 2 * y + c
        land[me] = p_ref[...]
        rel = [(fx, fy, fc) for fx in (0, 1) for fy in (0, 1) for fc in (0, 1)][1:]
        sends = []
        for k, (fx, fy, fc) in enumerate(rel):
            peer = (x ^ fx, y ^ fy, c ^ fc)
            cp = pltpu.make_async_remote_copy(
                src_ref=p_ref, dst_ref=land.at[me], send_sem=send_sems.at[k], recv_sem=recv_sems.at[k],
                device_id=peer, device_id_type=MESH)
            cp.start()
            sends.append(cp)
        for k, (fx, fy, fc) in enumerate(rel):
            src = 4 * (x ^ fx) + 2 * (y ^ fy) + (c ^ fc)
            pltpu.make_async_remote_copy(
                src_ref=p_ref, dst_ref=land.at[src], send_sem=send_sems.at[k], recv_sem=recv_sems.at[k],
                device_id=(x ^ fx, y ^ fy, c ^ fc), device_id_type=MESH).wait_recv()
        for cp in sends:
            cp.wait_send()
        acc = land[0]
        for d in range(1, N_DEV):
            acc = acc + land[d]
        o_ref[...] = acc

    vm = pl.BlockSpec(memory_space=pltpu.VMEM)
    return pl.pallas_call(
        body, name=name, in_specs=[vm], out_specs=vm,
        out_shape=jax.ShapeDtypeStruct((r, l), F32),
        scratch_shapes=[pltpu.VMEM((N_DEV, r, l), F32),
                        pltpu.SemaphoreType.DMA((N_DEV - 1,)), pltpu.SemaphoreType.DMA((N_DEV - 1,))],
    )(pack)


def _add_rows(terms, name):
    r, l = terms[0].shape
    tm = _pick(r, (2048, 1024, 512, 256, 128, 64, 32, 16, 8))

    def body(*refs):
        acc = refs[0][...]
        for t in refs[1:-1]:
            acc = acc + t[...]
        refs[-1][...] = acc

    spec = pl.BlockSpec((tm, l), lambda i: (i, 0))
    return pl.pallas_call(
        body, name=name, grid=(r // tm,), in_specs=[spec] * len(terms), out_specs=spec,
        out_shape=jax.ShapeDtypeStruct((r, l), F32),
        compiler_params=pltpu.CompilerParams(dimension_semantics=("parallel",)),
    )(*terms)


def _flat_rows(a, mult=SUBLANES * LANES):
    f = a.reshape(-1)
    padn = (-f.shape[0]) % mult
    if padn:
        f = jnp.concatenate([f, jnp.zeros((padn,), f.dtype)])
    return f


def _pack(arrs, mult=SUBLANES * LANES):
    flat = [_flat_rows(a, mult) for a in arrs]
    sizes = [f.shape[0] for f in flat]
    return jnp.concatenate(flat).reshape(-1, LANES), sizes


def _unpack(pack, shapes, sizes, lead=()):
    flat = pack.reshape(lead + (-1,))
    out, off = [], 0
    for shp, sz in zip(shapes, sizes):
        n = math.prod(shp)
        out.append(flat[..., off:off + n].reshape(lead + tuple(shp)))
        off += sz
    return out


def _cols_from_shards(g):
    s, k, n = g.shape
    return jnp.transpose(g, (1, 0, 2)).reshape(k, s * n)


def _cols_to_shards(w, s=N_CHIPS):
    k, n = w.shape
    return jnp.transpose(w.reshape(k, s, n // s), (1, 0, 2))


def _ffn_fwd(h, pre, post, w_up, cw, cb, w_down, tag):
    u = _rmsnorm_fwd(h, pre, f"{tag}_prenorm")
    hp = _mm_nn(u, w_up, F32, f"{tag}_up")
    hc = _conv_fwd(hp, 0, 2 * D_FF, cw, cb, f"{tag}_conv")
    act = _ffn_act_fwd(hc, f"{tag}_act")
    o = _mm_nn(act, w_down, F32, f"{tag}_down")
    hn = _postnorm_res_fwd(h, o, post, f"{tag}_postnorm")
    return hn, (h, u, hp, hc, act, o)


def _ffn_bwd(dh, saved, pre, post, w_up, cw, w_down, tag):
    h, u, hp, hc, act, o = saved
    do, dpost = _postnorm_bwd(o, post, dh, f"{tag}_postnorm_bwd")
    dact = _mm_nt(do, w_down, f"{tag}_down_dx")
    dw_down = _mm_tn(act, do, f"{tag}_down_dw")
    dhc = _ffn_act_bwd(hc, dact, f"{tag}_act_bwd")
    dhp, dcw, dcb = _conv_bwd(hp, 0, 2 * D_FF, dhc, cw, f"{tag}_conv_bwd")
    du = _mm_nt(dhp, w_up, f"{tag}_up_dx")
    dw_up = _mm_tn(u, dhp, f"{tag}_up_dw")
    dhn, dpre = _prenorm_bwd(h, pre, du, dh, f"{tag}_prenorm_bwd")
    return dhn, dict(pre=dpre, post=dpost, w_up=dw_up, conv_w=dcw[:3], conv_b=dcb, w_down=dw_down)


def _local_step(x, tgt, meta, P):
    seq, d = x.shape
    lp = seq + BLOCK
    h0 = jnp.concatenate([jnp.zeros((PAD, d), F32), meta, x], axis=0)
    tgt_p = jnp.concatenate([jnp.zeros((BLOCK, d), F32), tgt], axis=0)

    u0 = _rmsnorm_fwd(h0, P["l0_mix_pre_norm"], "l0_mix_prenorm")
    proj0 = _mm_nn(u0, P["l0_w_in"], F32, "l0_in")
    xrc = _conv_fwd(proj0, D_MODEL, D_MODEL, P["l0_lru_conv_w"], P["l0_lru_conv_b"], "l0_lru_conv")
    lru_args = (P["l0_lru_w_a"], P["l0_lru_w_x"], P["l0_lru_b_a"], P["l0_lru_b_x"], P["l0_lru_lambda"])
    ya, hl = _lru_fwd(proj0, xrc, *lru_args, "l0_lru")
    yb = _attn_fwd(proj0, P["l0_attn_sinks"], "l0_attn")
    mix0 = jnp.concatenate([ya, yb], axis=1)
    o0 = _mm_nn(mix0, P["l0_w_out"], F32, "l0_out")
    h1 = _postnorm_res_fwd(h0, o0, P["l0_mix_post_norm"], "l0_mix_postnorm")
    h2, ffn0 = _ffn_fwd(h1, P["l0_ffn_pre_norm"], P["l0_ffn_post_norm"], P["l0_ffn_w_up"],
                        P["l0_ffn_conv_w"], P["l0_ffn_conv_b"], P["l0_ffn_w_down"], "l0_ffn")
    u2 = _rmsnorm_fwd(h2, P["l1_mix_pre_norm"], "l1_mix_prenorm")
    proj1 = _mm_nn(u2, P["l1_w_in"], F32, "l1_in")
    xc1 = _conv_fwd(proj1, _ZW, _XBC_W, P["l1_ssm_conv_w"], P["l1_ssm_conv_b"], "l1_ssm_conv")
    xbc, dt = _ssm_prep_fwd(xc1, proj1, P["l1_dt_bias"], "l1_ssm_prep")
    yssd, states = _ssd_fwd(xbc, dt, P["l1_a_log"], "l1_ssd")
    yn = _ssm_gate_fwd(yssd, xbc, proj1, P["l1_d_skip"], P["l1_gate_norm"], "l1_ssm_gate")
    o1 = _mm_nn(yn, P["l1_w_out"], F32, "l1_out")
    h3 = _postnorm_res_fwd(h2, o1, P["l1_mix_post_norm"], "l1_mix_postnorm")
    h4, ffn1 = _ffn_fwd(h3, P["l1_ffn_pre_norm"], P["l1_ffn_post_norm"], P["l1_ffn_w_up"],
                        P["l1_ffn_conv_w"], P["l1_ffn_conv_b"], P["l1_ffn_w_down"], "l1_ffn")
    dh4, loss_cols = _loss_fwd_bwd(h4, tgt_p, "loss")

    G = {}
    dh3, g = _ffn_bwd(dh4, ffn1, P["l1_ffn_pre_norm"], P["l1_ffn_post_norm"], P["l1_ffn_w_up"],
                      P["l1_ffn_conv_w"], P["l1_ffn_w_down"], "l1_ffn")
    for k, v in g.items():
        G["l1_ffn_" + (k + "_norm" if k in ("pre", "post") else k)] = v
    do1, G["l1_mix_post_norm"] = _postnorm_bwd(o1, P["l1_mix_post_norm"], dh3, "l1_mix_postnorm_bwd")
    dyn = _mm_nt(do1, P["l1_w_out"], "l1_out_dx")
    G["l1_w_out"] = _mm_tn(yn, do1, "l1_out_dw")
    dyssd, dxskip, dz, dd_cols, G["l1_gate_norm"] = _ssm_gate_bwd(
        yssd, xbc, proj1, P["l1_d_skip"], P["l1_gate_norm"], dyn, "l1_ssm_gate_bwd")
    G["l1_d_skip"] = dd_cols.reshape(SSD_HEADS, SSD_P).sum(axis=1)
    dxs, dbm, dcm, ddt, dalog = _ssd_bwd(xbc, dt, P["l1_a_log"], states, dyssd, "l1_ssd_bwd")
    G["l1_a_log"] = dalog[0, :SSD_HEADS]
    dxc, ddtr, dbias = _ssm_prep_bwd(xc1, proj1, P["l1_dt_bias"], dxs, dxskip, dbm, dcm, ddt,
                                     "l1_ssm_prep_bwd")
    G["l1_dt_bias"] = dbias[0, :SSD_HEADS]
    dxbc, dcw, dcb = _conv_bwd(proj1, _ZW, _XBC_W, dxc, P["l1_ssm_conv_w"], "l1_ssm_conv_bwd")
    G["l1_ssm_conv_w"] = dcw[:4]
    G["l1_ssm_conv_b"] = dcb
    dproj1 = jnp.concatenate([dz, dxbc, ddtr], axis=1)
    du2 = _mm_nt(dproj1, P["l1_w_in"], "l1_in_dx")
    G["l1_w_in"] = _mm_tn(u2, dproj1, "l1_in_dw")
    dh2, G["l1_mix_pre_norm"] = _prenorm_bwd(h2, P["l1_mix_pre_norm"], du2, dh3, "l1_mix_prenorm_bwd")
    dh1, g = _ffn_bwd(dh2, ffn0, P["l0_ffn_pre_norm"], P["l0_ffn_post_norm"], P["l0_ffn_w_up"],
                      P["l0_ffn_conv_w"], P["l0_ffn_w_down"], "l0_ffn")
    for k, v in g.items():
        G["l0_ffn_" + (k + "_norm" if k in ("pre", "post") else k)] = v
    do0, G["l0_mix_post_norm"] = _postnorm_bwd(o0, P["l0_mix_post_norm"], dh1, "l0_mix_postnorm_bwd")
    dmix = _mm_nt(do0, P["l0_w_out"], "l0_out_dx")
    G["l0_w_out"] = _mm_tn(mix0, do0, "l0_out_dw")
    (dgate, dxrc, G["l0_lru_w_a"], G["l0_lru_w_x"], G["l0_lru_b_a"], G["l0_lru_b_x"],
     G["l0_lru_lambda"]) = _lru_bwd(proj0, xrc, hl, dmix, *lru_args, "l0_lru_bwd")
    dxr, dcw, dcb = _conv_bwd(proj0, D_MODEL, D_MODEL, dxrc, P["l0_lru_conv_w"], "l0_lru_conv_bwd")
    G["l0_lru_conv_w"] = dcw[:4]
    G["l0_lru_conv_b"] = dcb
    dq, dk, dv, G["l0_attn_sinks"] = _attn_bwd(proj0, P["l0_attn_sinks"], dmix, "l0_attn_bwd")
    dproj0 = jnp.concatenate([dgate, dxr, dq, dk.astype(BF16), dv.astype(BF16)], axis=1)
    du0 = _mm_nt(dproj0, P["l0_w_in"], "l0_in_dx")
    G["l0_w_in"] = _mm_tn(u0, dproj0, "l0_in_dw")
    dh0, G["l0_mix_pre_norm"] = _prenorm_bwd(h0, P["l0_mix_pre_norm"], du0, dh1, "l0_mix_prenorm_bwd")
    return loss_cols, dh0[BLOCK:], dh0[PAD:BLOCK], G


_BIG_COL = ("l0_w_in", "l0_ffn_w_up", "l1_w_in", "l1_ffn_w_up")
_BIG_ROW = ("l0_w_out", "l0_ffn_w_down", "l1_w_out", "l1_ffn_w_down")
_BIG = ("l0_w_in", "l0_w_out", "l0_ffn_w_up", "l0_ffn_w_down",
        "l1_w_in", "l1_w_out", "l1_ffn_w_up", "l1_ffn_w_down")
_SMALL_SHARDED = ("meta_tokens", "l0_lru_conv_w", "l0_ffn_conv_w", "l1_ssm_conv_w", "l1_ffn_conv_w")
_WEIGHTS = ("meta_tokens", "l0_mix_pre_norm", "l0_mix_post_norm", "l0_w_in", "l0_lru_conv_w",
            "l0_lru_conv_b", "l0_lru_w_a", "l0_lru_b_a", "l0_lru_w_x", "l0_lru_b_x", "l0_lru_lambda",
            "l0_attn_sinks", "l0_w_out", "l0_ffn_pre_norm", "l0_ffn_post_norm", "l0_ffn_w_up",
            "l0_ffn_conv_w", "l0_ffn_conv_b", "l0_ffn_w_down", "l1_mix_pre_norm", "l1_mix_post_norm",
            "l1_w_in", "l1_ssm_conv_w", "l1_ssm_conv_b", "l1_dt_bias", "l1_a_log", "l1_d_skip",
            "l1_gate_norm", "l1_w_out", "l1_ffn_pre_norm", "l1_ffn_post_norm", "l1_ffn_w_up",
            "l1_ffn_conv_w", "l1_ffn_conv_b", "l1_ffn_w_down")
_REPL = tuple(n for n in _WEIGHTS if n not in _BIG and n not in _SMALL_SHARDED)


def _pad_lanes(v, n=LANES):
    return jnp.concatenate([v, jnp.zeros((n - v.shape[0],), v.dtype)]).reshape(1, n)


def _step(x, tgt, W, M, V):
    cx, cy, cc = _place()
    chip = 2 * cx + cy

    big_pack, big_sizes = _pack([W[n].astype(BF16) for n in _BIG], 2 * SUBLANES * LANES)
    small_pack, small_sizes = _pack([W[n] for n in _SMALL_SHARDED])
    big_all, small_all = _gather_chips([big_pack, small_pack], "gather_weights")
    big_full = _unpack(big_all, [W[n].shape for n in _BIG], big_sizes, lead=(N_CHIPS,))
    small_full = _unpack(small_all, [W[n].shape for n in _SMALL_SHARDED], small_sizes, lead=(N_CHIPS,))

    P = {}
    for n, g in zip(_BIG, big_full):
        P[n] = _cols_from_shards(g) if n in _BIG_COL else g.reshape(-1, g.shape[-1])
    P["l1_w_in"] = jnp.concatenate(
        [P["l1_w_in"], jnp.zeros((D_MODEL, ODD_IN_PAD - ODD_IN), BF16)], axis=1)
    for n, g in zip(_SMALL_SHARDED, small_full):
        P[n] = _cols_from_shards(g)
    for n in _REPL:
        v = W[n]
        P[n] = v.reshape(1, -1) if v.ndim == 1 else v
    P["l0_lru_w_a"] = W["l0_lru_w_a"].astype(BF16)
    P["l0_lru_w_x"] = W["l0_lru_w_x"].astype(BF16)
    P["l1_dt_bias"] = _pad_lanes(W["l1_dt_bias"])
    P["l1_a_log"] = _pad_lanes(W["l1_a_log"])
    P["l1_d_skip"] = jnp.repeat(W["l1_d_skip"], SSD_P).reshape(1, D_SSM)
    meta = P.pop("meta_tokens")

    loss_cols, grad_x, grad_meta, G = _local_step(x, tgt, meta, P)
    G["meta_tokens"] = grad_meta
    G["l1_w_in"] = G["l1_w_in"][:, :ODD_IN]

    slabs = []
    for n in _BIG:
        g = G[n]
        slabs.append(_cols_to_shards(g) if n in _BIG_COL else g.reshape(N_CHIPS, -1, g.shape[-1]))
    gflat = [s.reshape(N_CHIPS, -1) for s in slabs]
    gsizes = [f.shape[1] for f in gflat]
    gpack = jnp.concatenate(gflat, axis=1)
    padn = (-gpack.shape[1]) % (2 * SUBLANES * LANES)
    if padn:
        gpack = jnp.concatenate([gpack, jnp.zeros((N_CHIPS, padn), F32)], axis=1)
    gpack = gpack.reshape(N_CHIPS, -1, LANES)
    r = gpack.shape[1]
    hr = r // 2
    from_sib = _sibling_half_swap(gpack, "grad_sibling_swap")
    mine = lax.dynamic_slice_in_dim(gpack, cc * hr, hr, axis=1)
    chip_sum = _add_rows([mine.reshape(-1, LANES), from_sib.reshape(-1, LANES)],
                         "grad_chip_sum").reshape(N_CHIPS, hr, LANES)
    from_chips = _chip_scatter(chip_sum, "grad_chip_scatter")
    own = lax.dynamic_index_in_dim(chip_sum, chip, axis=0, keepdims=False)
    total_half = _add_rows([own, from_chips[0], from_chips[1], from_chips[2]], "grad_total_sum")
    total = _sibling_concat(total_half, "grad_sibling_concat").reshape(r * LANES)
    big_grads, off = {}, 0
    for n, sz in zip(_BIG, gsizes):
        big_grads[n] = total[off:off + sz].reshape(W[n].shape)
        off += sz

    small_names = list(_REPL) + list(_SMALL_SHARDED)
    small_list = [G[n] for n in small_names] + [loss_cols]
    spack, ssizes = _pack(small_list)
    sred = _allreduce_small(spack, "small_allreduce")
    sfull = _unpack(sred, [a.shape for a in small_list], ssizes)
    loss = 0.5 / D_MODEL * jnp.sum(sfull[-1])
    small_grads = {}
    for n, g in zip(small_names, sfull[:-1]):
        if n in _SMALL_SHARDED:
            wcols = W[n].shape[1]
            g = lax.dynamic_slice_in_dim(g, chip * wcols, wcols, axis=1)
        small_grads[n] = g.reshape(W[n].shape)

    grads, delta, new_m, new_v = {}, {}, {}, {}
    for n in _BIG:
        grads[n] = big_grads[n]
        delta[n], new_m[n], new_v[n] = _adamw(W[n], grads[n], M[n], V[n], f"adamw_{n}")
    s_names = [n for n in _WEIGHTS if n not in _BIG]
    wp, wsz = _pack([W[n] for n in s_names])
    gp, _ = _pack([small_grads[n] for n in s_names])
    mp, _ = _pack([M[n] for n in s_names])
    vp, _ = _pack([V[n] for n in s_names])
    dp, nmp, nvp = _adamw(wp, gp, mp, vp, "adamw_small")
    shapes = [W[n].shape for n in s_names]
    for n, a, b, c_ in zip(s_names, _unpack(dp, shapes, wsz), _unpack(nmp, shapes, wsz),
                           _unpack(nvp, shapes, wsz)):
        grads[n] = small_grads[n]
        delta[n], new_m[n], new_v[n] = a, b, c_
    return loss, grad_x, grads, delta, new_m, new_v


def kernel(x, meta_tokens, l0_mix_pre_norm, l0_mix_post_norm, l0_w_in, l0_lru_conv_w, l0_lru_conv_b, l0_lru_w_a, l0_lru_b_a, l0_lru_w_x, l0_lru_b_x, l0_lru_lambda, l0_attn_sinks, l0_w_out, l0_ffn_pre_norm, l0_ffn_post_norm, l0_ffn_w_up, l0_ffn_conv_w, l0_ffn_conv_b, l0_ffn_w_down, l1_mix_pre_norm, l1_mix_post_norm, l1_w_in, l1_ssm_conv_w, l1_ssm_conv_b, l1_dt_bias, l1_a_log, l1_d_skip, l1_gate_norm, l1_w_out, l1_ffn_pre_norm, l1_ffn_post_norm, l1_ffn_w_up, l1_ffn_conv_w, l1_ffn_conv_b, l1_ffn_w_down, loss_target, m_meta_tokens, m_l0_mix_pre_norm, m_l0_mix_post_norm, m_l0_w_in, m_l0_lru_conv_w, m_l0_lru_conv_b, m_l0_lru_w_a, m_l0_lru_b_a, m_l0_lru_w_x, m_l0_lru_b_x, m_l0_lru_lambda, m_l0_attn_sinks, m_l0_w_out, m_l0_ffn_pre_norm, m_l0_ffn_post_norm, m_l0_ffn_w_up, m_l0_ffn_conv_w, m_l0_ffn_conv_b, m_l0_ffn_w_down, m_l1_mix_pre_norm, m_l1_mix_post_norm, m_l1_w_in, m_l1_ssm_conv_w, m_l1_ssm_conv_b, m_l1_dt_bias, m_l1_a_log, m_l1_d_skip, m_l1_gate_norm, m_l1_w_out, m_l1_ffn_pre_norm, m_l1_ffn_post_norm, m_l1_ffn_w_up, m_l1_ffn_conv_w, m_l1_ffn_conv_b, m_l1_ffn_w_down, v_meta_tokens, v_l0_mix_pre_norm, v_l0_mix_post_norm, v_l0_w_in, v_l0_lru_conv_w, v_l0_lru_conv_b, v_l0_lru_w_a, v_l0_lru_b_a, v_l0_lru_w_x, v_l0_lru_b_x, v_l0_lru_lambda, v_l0_attn_sinks, v_l0_w_out, v_l0_ffn_pre_norm, v_l0_ffn_post_norm, v_l0_ffn_w_up, v_l0_ffn_conv_w, v_l0_ffn_conv_b, v_l0_ffn_w_down, v_l1_mix_pre_norm, v_l1_mix_post_norm, v_l1_w_in, v_l1_ssm_conv_w, v_l1_ssm_conv_b, v_l1_dt_bias, v_l1_a_log, v_l1_d_skip, v_l1_gate_norm, v_l1_w_out, v_l1_ffn_pre_norm, v_l1_ffn_post_norm, v_l1_ffn_w_up, v_l1_ffn_conv_w, v_l1_ffn_conv_b, v_l1_ffn_w_down):
    args = locals()
    W = {n: args[n] for n in _WEIGHTS}
    M = {n: args["m_" + n] for n in _WEIGHTS}
    V = {n: args["v_" + n] for n in _WEIGHTS}
    loss, grad_x, grads, delta, new_m, new_v = _step(x[0], loss_target[0], W, M, V)
    return (loss, grad_x[None], *[grads[n] for n in _WEIGHTS], *[delta[n] for n in _WEIGHTS],
            *[new_m[n] for n in _WEIGHTS], *[new_v[n] for n in _WEIGHTS])
```

```python
import functools
import math

import jax
import jax.numpy as jnp
from jax import lax
from jax.experimental import pallas as pl
from jax.experimental.pallas import tpu as pltpu

F32 = jnp.float32
BF16 = jnp.bfloat16

D_MODEL = 1024
N_META = 16
BLOCK = 128
PAD = BLOCK - N_META
EPS = 1e-6
LRU_BLOCKS = 8
LRU_BS = 128
LRU_C = 8.0
N_Q_HEADS = 16
N_KV_HEADS = 2
HEAD_DIM = 64
Q_PER_KV = 8
WINDOW = 128
D_SSM = 2048
SSD_HEADS = 32
SSD_GROUPS = 8
SSD_HPG = 4
SSD_P = 64
SSD_N = 128
D_FF = 2816
NEG = -1e30
LANES = 128
SUBLANES = 8

ADAM_LR = 0.001
ADAM_B1 = 0.9
ADAM_B2 = 0.999
ADAM_EPS = 1e-08
ADAM_WD = 0.01
ADAM_STEP = 10

MESH = pl.DeviceIdType.MESH
N_CHIPS = 4
N_DEV = 8


def _pick(n, cands):
    for c in cands:
        if n % c == 0:
            return c
    raise ValueError(f"no tile for {n} in {cands}")


def _col_tile(n, limit=1792):
    best = None
    for t in range(LANES, min(n, limit) + 1, LANES):
        if n % t == 0:
            best = t
    if best is None:
        raise ValueError(f"no lane tile for {n}")
    return best


def _sigmoid(x):
    return 1.0 / (1.0 + jnp.exp(-x))


def _log1p(e):
    u = 1.0 + e
    return jnp.where(u == 1.0, e, jnp.log(u) * (e / jnp.where(u == 1.0, 1.0, u - 1.0)))


def _softplus(x):
    return jnp.maximum(x, 0.0) + _log1p(jnp.exp(-jnp.abs(x)))


def _neg_expm1(x):
    poly = x * (1.0 + x * (0.5 + x * (1.0 / 6.0 + x * (1.0 / 24.0 + x * (1.0 / 120.0)))))
    return -jnp.where(x > -0.05, poly, jnp.exp(x) - 1.0)


_GELU_C = math.sqrt(2.0 / math.pi)


def _gelu(x):
    t = jnp.tanh(_GELU_C * (x + 0.044715 * x * x * x))
    return 0.5 * x * (1.0 + t)


def _gelu_and_grad(x):
    x2 = x * x
    t = jnp.tanh(_GELU_C * (x + 0.044715 * x * x2))
    g = 0.5 * x * (1.0 + t)
    dg = 0.5 * (1.0 + t) + 0.5 * x * (1.0 - t * t) * _GELU_C * (1.0 + 3.0 * 0.044715 * x2)
    return g, dg


def _silu_and_grad(x):
    s = _sigmoid(x)
    return x * s, s * (1.0 + x * (1.0 - s))


def _dot(a, b):
    return jnp.dot(a, b, preferred_element_type=F32)


def _dot_nt(a, b):
    return lax.dot_general(a, b, (((1,), (1,)), ((), ())), preferred_element_type=F32)


def _dot_tn(a, b):
    return lax.dot_general(a, b, (((0,), (0,)), ((), ())), preferred_element_type=F32)


def _row_iota(t):
    return lax.broadcasted_iota(jnp.int32, (t, 1), 0)


def _scan_fwd(a, u, t):
    row = _row_iota(t)
    d = 1
    while d < t:
        m = row >= d
        u_sh = jnp.where(m, pltpu.roll(u, d, 0), 0.0)
        a_sh = jnp.where(m, pltpu.roll(a, d, 0), 1.0)
        u = u + a * u_sh
        a = a * a_sh
        d *= 2
    return a, u


def _scan_rev(c, x, t):
    row = _row_iota(t)
    d = 1
    while d < t:
        m = row < t - d
        x_sh = jnp.where(m, pltpu.roll(x, t - d, 0), 0.0)
        c_sh = jnp.where(m, pltpu.roll(c, t - d, 0), 1.0)
        x = x + c * x_sh
        c = c * c_sh
        d *= 2
    return c, x


def _cumsum_rows(x, t):
    row = _row_iota(t)
    d = 1
    while d < t:
        x = x + jnp.where(row >= d, pltpu.roll(x, d, 0), 0.0)
        d *= 2
    return x


def _rev_cumsum_rows(x, t):
    row = _row_iota(t)
    d = 1
    while d < t:
        x = x + jnp.where(row < t - d, pltpu.roll(x, t - d, 0), 0.0)
        d *= 2
    return x


def _rms_bwd(x, g, dy):
    rs = lax.rsqrt(jnp.mean(x * x, axis=-1, keepdims=True) + EPS)
    gy = dy * g
    dx = rs * gy - x * (rs * rs * rs) * jnp.mean(x * gy, axis=-1, keepdims=True)
    return dx, dy * x * rs


def _mm_nn(a, w, out_dtype, name):
    m, k = a.shape
    n = w.shape[1]
    tm = _pick(m, (640, 512, 256, 128))
    tn = _col_tile(n)

    def body(a_ref, w_ref, o_ref):
        o_ref[...] = _dot(a_ref[...].astype(BF16), w_ref[...]).astype(o_ref.dtype)

    return pl.pallas_call(
        body, name=name, grid=(n // tn, m // tm),
        in_specs=[pl.BlockSpec((tm, k), lambda j, i: (i, 0)),
                  pl.BlockSpec((k, tn), lambda j, i: (0, j))],
        out_specs=pl.BlockSpec((tm, tn), lambda j, i: (i, j)),
        out_shape=jax.ShapeDtypeStruct((m, n), out_dtype),
        compiler_params=pltpu.CompilerParams(dimension_semantics=("parallel", "parallel")),
    )(a, w)


def _mm_nt(dy, w, name):
    m, n = dy.shape
    k = w.shape[0]
    wide = n > 3328
    tm = _pick(m, (320, 256, 128)) if wide else _pick(m, (640, 512, 256, 128))
    tk = _col_tile(k, 512 if wide else 1408)

    def body(dy_ref, w_ref, o_ref):
        o_ref[...] = _dot_nt(dy_ref[...].astype(BF16), w_ref[...])

    return pl.pallas_call(
        body, name=name, grid=(k // tk, m // tm),
        in_specs=[pl.BlockSpec((tm, n), lambda j, i: (i, 0)),
                  pl.BlockSpec((tk, n), lambda j, i: (j, 0))],
        out_specs=pl.BlockSpec((tm, tk), lambda j, i: (i, j)),
        out_shape=jax.ShapeDtypeStruct((m, k), F32),
        compiler_params=pltpu.CompilerParams(dimension_semantics=("parallel", "parallel")),
    )(dy, w)


def _mm_tn(a, dy, name):
    m, k = a.shape
    n = dy.shape[1]
    tm = _pick(m, (640, 512, 256, 128))
    tk = _col_tile(k, 1408)
    tn = _col_tile(n, 1664)

    def body(a_ref, dy_ref, o_ref):
        @pl.when(pl.program_id(2) == 0)
        def _():
            o_ref[...] = jnp.zeros_like(o_ref)

        o_ref[...] += _dot_tn(a_ref[...].astype(BF16), dy_ref[...].astype(BF16))

    return pl.pallas_call(
        body, name=name, grid=(k // tk, n // tn, m // tm),
        in_specs=[pl.BlockSpec((tm, tk), lambda kk, j, i: (i, kk)),
                  pl.BlockSpec((tm, tn), lambda kk, j, i: (i, j))],
        out_specs=pl.BlockSpec((tk, tn), lambda kk, j, i: (kk, j)),
        out_shape=jax.ShapeDtypeStruct((k, n), F32),
        compiler_params=pltpu.CompilerParams(
            dimension_semantics=("parallel", "parallel", "arbitrary")),
    )(a, dy)


def _mm_nn_sh(a, w4, n_out, name):
    m, k = a.shape
    s, _, n = w4.shape
    tm = _pick(m, (320, 256, 128))

    def body(a_ref, w_ref, o_ref):
        av = a_ref[...].astype(BF16)
        for j in range(s):
            o_ref[:, j * n:(j + 1) * n] = _dot(av, w_ref[j])
        if n_out > s * n:
            o_ref[:, s * n:] = jnp.zeros((tm, n_out - s * n), F32)

    return pl.pallas_call(
        body, name=name, grid=(m // tm,),
        in_specs=[pl.BlockSpec((tm, k), lambda i: (i, 0)),
                  pl.BlockSpec((s, k, n), lambda i: (0, 0, 0))],
        out_specs=pl.BlockSpec((tm, n_out), lambda i: (i, 0)),
        out_shape=jax.ShapeDtypeStruct((m, n_out), F32),
        compiler_params=pltpu.CompilerParams(dimension_semantics=("parallel",)),
    )(a, w4)


def _mm_nt_sh(dy, w4, name):
    m, nn = dy.shape
    s, k, n = w4.shape
    tm = _pick(m, (320, 256, 128))
    tk = _col_tile(k, 256)

    def body(dy_ref, w_ref, o_ref):
        acc = _dot_nt(dy_ref[:, 0:n].astype(BF16), w_ref[0])
        for j in range(1, s):
            acc = acc + _dot_nt(dy_ref[:, j * n:(j + 1) * n].astype(BF16), w_ref[j])
        o_ref[...] = acc

    return pl.pallas_call(
        body, name=name, grid=(k // tk, m // tm),
        in_specs=[pl.BlockSpec((tm, nn), lambda j, i: (i, 0)),
                  pl.BlockSpec((s, tk, n), lambda j, i: (0, j, 0))],
        out_specs=pl.BlockSpec((tm, tk), lambda j, i: (i, j)),
        out_shape=jax.ShapeDtypeStruct((m, k), F32),
        compiler_params=pltpu.CompilerParams(dimension_semantics=("parallel", "parallel")),
    )(dy, w4)


def _mm_tn_sh(a, dy, n, name):
    m, k = a.shape
    nn = dy.shape[1]
    s = N_CHIPS
    tm = _pick(m, (640, 512, 256, 128))
    tk = _col_tile(k, 256)

    def body(a_ref, dy_ref, o_ref):
        @pl.when(pl.program_id(1) == 0)
        def _():
            o_ref[...] = jnp.zeros_like(o_ref)

        av = a_ref[...].astype(BF16)
        for j in range(s):
            o_ref[j] += _dot_tn(av, dy_ref[:, j * n:(j + 1) * n].astype(BF16))

    return pl.pallas_call(
        body, name=name, grid=(k // tk, m // tm),
        in_specs=[pl.BlockSpec((tm, tk), lambda kk, i: (i, kk)),
                  pl.BlockSpec((tm, nn), lambda kk, i: (i, 0))],
        out_specs=pl.BlockSpec((s, tk, n), lambda kk, i: (0, kk, 0)),
        out_shape=jax.ShapeDtypeStruct((s, k, n), F32),
        compiler_params=pltpu.CompilerParams(dimension_semantics=("parallel", "arbitrary")),
    )(a, dy)


def _rowcall(name, body, lp, tm, rows=(), prevs=(), vecs=(), outs=(), accs=(), scratch=(),
             reverse=False, seq=False):
    nt = lp // tm
    hb = tm // SUBLANES

    def ri(i):
        return nt - 1 - i if reverse else i

    in_specs, args = [], []
    for arr, w, cb in rows:
        in_specs.append(pl.BlockSpec((tm, w), lambda i, cb=cb: (ri(i), cb)))
        args.append(arr)
    for arr, w, cb in prevs:
        in_specs.append(pl.BlockSpec((SUBLANES, w), lambda i, cb=cb: (jnp.maximum(ri(i) * hb - 1, 0), cb)))
        args.append(arr)
    for arr in vecs:
        in_specs.append(pl.BlockSpec(arr.shape, lambda i, nd=arr.ndim: (0,) * nd))
        args.append(arr)
    out_shape, out_specs = [], []
    for w, dt in outs:
        out_shape.append(jax.ShapeDtypeStruct((lp, w), dt))
        out_specs.append(pl.BlockSpec((tm, w), lambda i: (ri(i), 0)))
    for shp, dt in accs:
        out_shape.append(jax.ShapeDtypeStruct(shp, dt))
        out_specs.append(pl.BlockSpec(shp, lambda i, nd=len(shp): (0,) * nd))

    def kern(*refs):
        i = pl.program_id(0)
        body(ri(i), i == 0, *refs)

    sem = ("arbitrary",) if (seq or accs) else ("parallel",)
    res = pl.pallas_call(
        kern, name=name, grid=(nt,), in_specs=in_specs, out_specs=out_specs,
        out_shape=out_shape, scratch_shapes=list(scratch),
        compiler_params=pltpu.CompilerParams(dimension_semantics=sem),
    )(*args)
    return res


def _acc_add(first, ref, val):
    @pl.when(first)
    def _():
        ref[...] = jnp.zeros_like(ref)

    ref[...] += val


def _real_rows(r, tm):
    return (r * tm + _row_iota(tm)) >= PAD


def _rmsnorm_fwd(h, g, name):
    lp, d = h.shape
    tm = _pick(lp, (640, 512, 256, 128))

    def body(r, first, h_ref, g_ref, u_ref):
        x = h_ref[...]
        rs = lax.rsqrt(jnp.mean(x * x, axis=-1, keepdims=True) + EPS)
        u_ref[...] = (x * rs * g_ref[...]).astype(u_ref.dtype)

    return _rowcall(name, body, lp, tm, rows=[(h, d, 0)], vecs=[g], outs=[(d, BF16)])[0]


def _postnorm_res_fwd(h, o, g, name):
    lp, d = h.shape
    tm = _pick(lp, (640, 512, 256, 128))

    def body(r, first, h_ref, o_ref, g_ref, out_ref):
        x = o_ref[...]
        rs = lax.rsqrt(jnp.mean(x * x, axis=-1, keepdims=True) + EPS)
        out_ref[...] = jnp.where(_real_rows(r, tm), h_ref[...] + x * rs * g_ref[...], 0.0)

    return _rowcall(name, body, lp, tm, rows=[(h, d, 0), (o, d, 0)], vecs=[g], outs=[(d, F32)])[0]


def _postnorm_bwd(o, g, dh, name):
    lp, d = o.shape
    tm = _pick(lp, (640, 512, 256, 128))

    def body(r, first, o_ref, dh_ref, g_ref, do_ref, dg_ref):
        dx, dgt = _rms_bwd(o_ref[...], g_ref[...], dh_ref[...])
        do_ref[...] = dx.astype(do_ref.dtype)
        _acc_add(first, dg_ref, jnp.sum(dgt, axis=0, keepdims=True))

    return _rowcall(name, body, lp, tm, rows=[(o, d, 0), (dh, d, 0)], vecs=[g],
                    outs=[(d, BF16)], accs=[((1, d), F32)])


def _prenorm_bwd(h, g, du, dh_res, name):
    lp, d = h.shape
    tm = _pick(lp, (640, 512, 256, 128))

    def body(r, first, h_ref, du_ref, dres_ref, g_ref, dh_ref, dg_ref):
        dx, dgt = _rms_bwd(h_ref[...], g_ref[...], du_ref[...])
        dh_ref[...] = jnp.where(_real_rows(r, tm), dres_ref[...] + dx, 0.0)
        _acc_add(first, dg_ref, jnp.sum(dgt, axis=0, keepdims=True))

    return _rowcall(name, body, lp, tm, rows=[(h, d, 0), (du, d, 0), (dh_res, d, 0)], vecs=[g],
                    outs=[(d, F32)], accs=[((1, d), F32)])


def _loss_fwd_bwd(h, tgt, name):
    lp, d = h.shape
    tm = _pick(lp, (640, 512, 256, 128))

    def body(r, first, h_ref, t_ref, dh_ref, ls_ref):
        tok = (r * tm + _row_iota(tm)) >= BLOCK
        e = jnp.where(tok, h_ref[...] - t_ref[...], 0.0)
        dh_ref[...] = e * (1.0 / d)
        _acc_add(first, ls_ref, jnp.sum(e * e, axis=0, keepdims=True))

    return _rowcall(name, body, lp, tm, rows=[(h, d, 0), (tgt, d, 0)],
                    outs=[(d, F32)], accs=[((1, d), F32)])


def _conv_tiles(lp, width):
    wc = _col_tile(width, 1408)
    tm = _pick(lp, (320, 256, 128))
    return tm, wc


def _conv_fwd(x, col_off, width, w, b, name):
    lp = x.shape[0]
    kk = w.shape[0]
    tm, wc = _conv_tiles(lp, width)
    offb = col_off // wc
    assert col_off % wc == 0
    hb = tm // SUBLANES

    def body(x_ref, xp_ref, w_ref, b_ref, y_ref):
        i = pl.program_id(1)
        xv = x_ref[...]
        halo = jnp.where(i > 0, xp_ref[...], 0.0)
        xx = jnp.concatenate([halo, xv], axis=0)
        acc = b_ref[...] + w_ref[kk - 1:kk, :] * xv
        for j in range(1, kk):
            acc = acc + w_ref[kk - 1 - j:kk - j, :] * pltpu.roll(xx, j, 0)[SUBLANES:, :]
        y_ref[...] = acc

    return pl.pallas_call(
        body, name=name, grid=(width // wc, lp // tm),
        in_specs=[pl.BlockSpec((tm, wc), lambda j, i: (i, offb + j)),
                  pl.BlockSpec((SUBLANES, wc), lambda j, i: (jnp.maximum(i * hb - 1, 0), offb + j)),
                  pl.BlockSpec((kk, wc), lambda j, i: (0, j)),
                  pl.BlockSpec((1, wc), lambda j, i: (0, j))],
        out_specs=pl.BlockSpec((tm, wc), lambda j, i: (i, j)),
        out_shape=jax.ShapeDtypeStruct((lp, width), F32),
        compiler_params=pltpu.CompilerParams(dimension_semantics=("parallel", "parallel")),
    )(x, x, w, b)


def _conv_bwd(x, col_off, width, dy, w, name):
    lp = x.shape[0]
    kk = w.shape[0]
    tm, wc = _conv_tiles(lp, width)
    offb = col_off // wc
    assert col_off % wc == 0
    hb = tm // SUBLANES
    nb8 = lp // SUBLANES

    def body(x_ref, xp_ref, dy_ref, dn_ref, w_ref, dx_ref, dw_ref, db_ref):
        i = pl.program_id(1)
        last = pl.num_programs(1) - 1
        xv = x_ref[...]
        dyv = dy_ref[...]
        xx = jnp.concatenate([jnp.where(i > 0, xp_ref[...], 0.0), xv], axis=0)
        dd = jnp.concatenate([dyv, jnp.where(i < last, dn_ref[...], 0.0)], axis=0)
        dx = w_ref[kk - 1:kk, :] * dyv
        rows = [jnp.sum(dyv * xv, axis=0, keepdims=True)]
        for m in range(1, kk):
            dx = dx + w_ref[kk - 1 - m:kk - m, :] * pltpu.roll(dd, tm + SUBLANES - m, 0)[:tm, :]
            rows.append(jnp.sum(dyv * pltpu.roll(xx, m, 0)[SUBLANES:, :], axis=0, keepdims=True))
        dx_ref[...] = dx.astype(dx_ref.dtype)
        dwp = jnp.concatenate(rows[::-1] + [jnp.zeros((SUBLANES - kk, wc), F32)], axis=0)

        @pl.when(i == 0)
        def _():
            dw_ref[...] = jnp.zeros_like(dw_ref)
            db_ref[...] = jnp.zeros_like(db_ref)

        dw_ref[...] += dwp
        db_ref[...] += jnp.sum(dyv, axis=0, keepdims=True)

    return pl.pallas_call(
        body, name=name, grid=(width // wc, lp // tm),
        in_specs=[pl.BlockSpec((tm, wc), lambda j, i: (i, offb + j)),
                  pl.BlockSpec((SUBLANES, wc), lambda j, i: (jnp.maximum(i * hb - 1, 0), offb + j)),
                  pl.BlockSpec((tm, wc), lambda j, i: (i, j)),
                  pl.BlockSpec((SUBLANES, wc), lambda j, i: (jnp.minimum((i + 1) * hb, nb8 - 1), j)),
                  pl.BlockSpec((kk, wc), lambda j, i: (0, j))],
        out_specs=[pl.BlockSpec((tm, wc), lambda j, i: (i, j)),
                   pl.BlockSpec((SUBLANES, wc), lambda j, i: (0, j)),
                   pl.BlockSpec((1, wc), lambda j, i: (0, j))],
        out_shape=[jax.ShapeDtypeStruct((lp, width), BF16),
                   jax.ShapeDtypeStruct((SUBLANES, width), F32),
                   jax.ShapeDtypeStruct((1, width), F32)],
        compiler_params=pltpu.CompilerParams(dimension_semantics=("parallel", "arbitrary")),
    )(x, x, dy, dy, w)


def _ffn_act_fwd(hc, name):
    lp = hc.shape[0]
    tm = _pick(lp, (320, 256, 128))

    def body(r, first, g_ref, u_ref, a_ref):
        a_ref[...] = (_gelu(g_ref[...]) * u_ref[...]).astype(a_ref.dtype)

    return _rowcall(name, body, lp, tm, rows=[(hc, D_FF, 0), (hc, D_FF, 1)], outs=[(D_FF, BF16)])[0]


def _ffn_act_bwd(hc, dact, name):
    lp = hc.shape[0]
    tm = BLOCK

    def body(r, first, g_ref, u_ref, da_ref, dh_ref):
        gl, dgl = _gelu_and_grad(g_ref[...])
        da = da_ref[...]
        dh_ref[:, :D_FF] = da * u_ref[...] * dgl
        dh_ref[:, D_FF:] = da * gl

    return _rowcall(name, body, lp, tm, rows=[(hc, D_FF, 0), (hc, D_FF, 1), (dact, D_FF, 0)],
                    outs=[(2 * D_FF, F32)])[0]


def _lru_gates(x, wa_ref, wx_ref, ba, bx, lam):
    xb = x.astype(BF16)
    za, zx = [], []
    for n in range(LRU_BLOCKS):
        xs = xb[:, n * LRU_BS:(n + 1) * LRU_BS]
        za.append(_dot(xs, wa_ref[n]))
        zx.append(_dot(xs, wx_ref[n]))
    r = _sigmoid(jnp.concatenate(za, axis=1) + ba)
    ig = _sigmoid(jnp.concatenate(zx, axis=1) + bx)
    sp = _softplus(-lam)
    log_a = -LRU_C * r * sp
    a = jnp.exp(log_a)
    om = _neg_expm1(2.0 * log_a)
    mult = jnp.sqrt(om)
    return xb, r, ig, sp, a, om, mult


def _lru_fwd(proj, xrc, wa, wx, ba, bx, lam, name):
    lp, d = xrc.shape
    tm = BLOCK

    def body(r_idx, first, gate_ref, x_ref, wa_ref, wx_ref, ba_ref, bx_ref, lam_ref,
             y_ref, h_ref, carry):
        @pl.when(first)
        def _():
            carry[...] = jnp.zeros_like(carry)

        x = x_ref[...]
        _, _, ig, _, a, _, mult = _lru_gates(x, wa_ref, wx_ref, ba_ref[...], bx_ref[...], lam_ref[...])
        u = jnp.where(_real_rows(r_idx, tm), mult * ig * x, 0.0)
        acum, hloc = _scan_fwd(a, u, tm)
        h = hloc + acum * carry[0:1, :]
        h_ref[...] = h
        carry[0:1, :] = h[tm - 1:tm, :]
        y_ref[...] = (_gelu(gate_ref[...]) * h).astype(y_ref.dtype)

    return _rowcall(name, body, lp, tm, rows=[(proj, d, 0), (xrc, d, 0)],
                    vecs=[wa, wx, ba, bx, lam], outs=[(d, BF16), (d, F32)],
                    scratch=[pltpu.VMEM((SUBLANES, d), F32)], seq=True)


def _lru_bwd(proj, xrc, hl, dmix, wa, wx, ba, bx, lam, name):
    lp, d = xrc.shape
    tm = BLOCK

    def body(r_idx, first, gate_ref, x_ref, h_ref, dy_ref, hp_ref, wa_ref, wx_ref, ba_ref, bx_ref,
             lam_ref, dgate_ref, dx_ref, dwa_ref, dwx_ref, dba_ref, dbx_ref, dlam_ref, carry):
        @pl.when(first)
        def _():
            carry[...] = jnp.zeros_like(carry)
            dwa_ref[...] = jnp.zeros_like(dwa_ref)
            dwx_ref[...] = jnp.zeros_like(dwx_ref)
            dba_ref[...] = jnp.zeros_like(dba_ref)
            dbx_ref[...] = jnp.zeros_like(dbx_ref)
            dlam_ref[...] = jnp.zeros_like(dlam_ref)

        x = x_ref[...]
        lam = lam_ref[...]
        xb, r, ig, sp, a, om, mult = _lru_gates(x, wa_ref, wx_ref, ba_ref[...], bx_ref[...], lam)
        h = h_ref[...]
        dy = dy_ref[...]
        gl, dgl = _gelu_and_grad(gate_ref[...])
        dgate_ref[...] = (dy * h * dgl).astype(dgate_ref.dtype)
        row = _row_iota(tm)
        lastrow = row == tm - 1
        xg = dy * gl + jnp.where(lastrow, carry[0:1, :], 0.0)
        c = jnp.where(lastrow, 1.0, pltpu.roll(a, tm - 1, 0))
        _, g = _scan_rev(c, xg, tm)
        carry[0:1, :] = a[0:1, :] * g[0:1, :]
        hprev_in = jnp.where(r_idx > 0, hp_ref[SUBLANES - 1:SUBLANES, :], 0.0)
        hprev = jnp.where(row == 0, hprev_in, pltpu.roll(h, 1, 0))
        du = jnp.where(_real_rows(r_idx, tm), g, 0.0)
        da = g * hprev
        dmult = du * ig * x
        dig = du * mult * x
        dxv = du * mult * ig
        e2 = 1.0 - om
        dlog_a = da * a - dmult * e2 / mult
        dr = dlog_a * (-LRU_C) * sp
        dsp = jnp.sum(dlog_a * (-LRU_C) * r, axis=0, keepdims=True)
        dlam_ref[...] += -dsp * _sigmoid(-lam)
        dza = dr * r * (1.0 - r)
        dzx = dig * ig * (1.0 - ig)
        dba_ref[...] += jnp.sum(dza, axis=0, keepdims=True)
        dbx_ref[...] += jnp.sum(dzx, axis=0, keepdims=True)
        dzab = dza.astype(BF16)
        dzxb = dzx.astype(BF16)
        parts = []
        for n in range(LRU_BLOCKS):
            sl = slice(n * LRU_BS, (n + 1) * LRU_BS)
            dwa_ref[n] += _dot_tn(xb[:, sl], dzab[:, sl])
            dwx_ref[n] += _dot_tn(xb[:, sl], dzxb[:, sl])
            parts.append(_dot_nt(dzab[:, sl], wa_ref[n]) + _dot_nt(dzxb[:, sl], wx_ref[n]))
        dx_ref[...] = dxv + jnp.concatenate(parts, axis=1)

    return _rowcall(name, body, lp, tm,
                    rows=[(proj, d, 0), (xrc, d, 0), (hl, d, 0), (dmix, d, 0)],
                    prevs=[(hl, d, 0)], vecs=[wa, wx, ba, bx, lam],
                    outs=[(d, BF16), (d, F32)],
                    accs=[((LRU_BLOCKS, LRU_BS, LRU_BS), F32), ((LRU_BLOCKS, LRU_BS, LRU_BS), F32),
                          ((1, d), F32), ((1, d), F32), ((1, d), F32)],
                    scratch=[pltpu.VMEM((SUBLANES, d), F32)], reverse=True, seq=True)


_SLOPES = [2.0 ** (-8.0 * (h + 1) / N_Q_HEADS) for h in range(N_Q_HEADS)]
_QK_SCALE = HEAD_DIM ** -0.5
_QCOL = 2 * D_MODEL // D_MODEL
_KCOL = (3 * D_MODEL) // LANES
_VCOL = _KCOL + 1


def _attn_masks(n):
    start = pl.multiple_of(jnp.maximum(n - 1, 0) * BLOCK, BLOCK)
    qi = n * BLOCK + lax.broadcasted_iota(jnp.int32, (BLOCK, 2 * BLOCK), 0)
    kj = start + lax.broadcasted_iota(jnp.int32, (BLOCK, 2 * BLOCK), 1)
    dist = qi - kj
    ok = (kj >= BLOCK) & (dist >= 0) & (dist < WINDOW)
    dm = (n * BLOCK - PAD + lax.broadcasted_iota(jnp.int32, (BLOCK, N_META), 0)
          - lax.broadcasted_iota(jnp.int32, (BLOCK, N_META), 1))
    okm = dm >= 0
    return start, ok, dist.astype(F32), okm, jnp.minimum(dm, WINDOW).astype(F32)


def _attn_probs(qh, kg, kmg, sink, slope, ok, distf, okm, dmf):
    s = jnp.where(ok, _dot_nt(qh, kg) * _QK_SCALE - slope * distf, NEG)
    sm = jnp.where(okm, _dot_nt(qh, kmg) * _QK_SCALE - slope * dmf, NEG)
    mx = jnp.maximum(jnp.maximum(jnp.max(s, axis=-1, keepdims=True),
                                 jnp.max(sm, axis=-1, keepdims=True)), sink)
    p = jnp.exp(s - mx)
    pm = jnp.exp(sm - mx)
    ps = jnp.exp(sink - mx)
    inv = 1.0 / (jnp.sum(p, axis=-1, keepdims=True) + jnp.sum(pm, axis=-1, keepdims=True) + ps)
    return p * inv, pm * inv, ps * inv


def _attn_fwd(proj, sinks, name):
    lp = proj.shape[0]
    nblk = lp // BLOCK

    def body(q_ref, k_ref, v_ref, sink_ref, o_ref):
        n = pl.program_id(0)
        start, ok, distf, okm, dmf = _attn_masks(n)
        kb = k_ref[pl.ds(start, 2 * BLOCK), :].astype(BF16)
        vb = v_ref[pl.ds(start, 2 * BLOCK), :].astype(BF16)
        km = k_ref[PAD:BLOCK, :].astype(BF16)
        vm = v_ref[PAD:BLOCK, :].astype(BF16)
        for h in range(N_Q_HEADS):
            g = h // Q_PER_KV
            gs = slice(g * HEAD_DIM, (g + 1) * HEAD_DIM)
            qh = q_ref[:, h * HEAD_DIM:(h + 1) * HEAD_DIM].astype(BF16)
            pn, pmn, _ = _attn_probs(qh, kb[:, gs], km[:, gs], sink_ref[0:1, h:h + 1], _SLOPES[h],
                                     ok, distf, okm, dmf)
            o = _dot(pn.astype(BF16), vb[:, gs]) + _dot(pmn.astype(BF16), vm[:, gs])
            o_ref[:, h * HEAD_DIM:(h + 1) * HEAD_DIM] = o.astype(o_ref.dtype)

    return pl.pallas_call(
        body, name=name, grid=(nblk,),
        in_specs=[pl.BlockSpec((BLOCK, D_MODEL), lambda n: (n, _QCOL)),
                  pl.BlockSpec((lp, LANES), lambda n: (0, _KCOL)),
                  pl.BlockSpec((lp, LANES), lambda n: (0, _VCOL)),
                  pl.BlockSpec(sinks.shape, lambda n: (0, 0))],
        out_specs=pl.BlockSpec((BLOCK, D_MODEL), lambda n: (n, 0)),
        out_shape=jax.ShapeDtypeStruct((lp, D_MODEL), BF16),
        compiler_params=pltpu.CompilerParams(dimension_semantics=("parallel",)),
    )(proj, proj, proj, sinks)


def _attn_bwd(proj, sinks, dmix, name):
    lp = proj.shape[0]
    nblk = lp // BLOCK

    def body(q_ref, k_ref, v_ref, sink_ref, dy_ref, dq_ref, dk_ref, dv_ref, ds_ref):
        n = pl.program_id(0)

        @pl.when(n == 0)
        def _():
            dk_ref[...] = jnp.zeros_like(dk_ref)
            dv_ref[...] = jnp.zeros_like(dv_ref)
            ds_ref[...] = jnp.zeros_like(ds_ref)

        start, ok, distf, okm, dmf = _attn_masks(n)
        kb = k_ref[pl.ds(start, 2 * BLOCK), :].astype(BF16)
        vb = v_ref[pl.ds(start, 2 * BLOCK), :].astype(BF16)
        km = k_ref[PAD:BLOCK, :].astype(BF16)
        vm = v_ref[PAD:BLOCK, :].astype(BF16)
        lane16 = lax.broadcasted_iota(jnp.int32, (1, N_Q_HEADS), 1)
        dsink = jnp.zeros((1, N_Q_HEADS), F32)
        for g in range(N_KV_HEADS):
            gs = slice(g * HEAD_DIM, (g + 1) * HEAD_DIM)
            dk = jnp.zeros((2 * BLOCK, HEAD_DIM), F32)
            dv = jnp.zeros((2 * BLOCK, HEAD_DIM), F32)
            dkm = jnp.zeros((N_META, HEAD_DIM), F32)
            dvm = jnp.zeros((N_META, HEAD_DIM), F32)
            for hh in range(Q_PER_KV):
                h = g * Q_PER_KV + hh
                hs = slice(h * HEAD_DIM, (h + 1) * HEAD_DIM)
                qh = q_ref[:, hs].astype(BF16)
                doh = dy_ref[:, hs].astype(BF16)
                pn, pmn, psn = _attn_probs(qh, kb[:, gs], km[:, gs], sink_ref[0:1, h:h + 1],
                                           _SLOPES[h], ok, distf, okm, dmf)
                dp = _dot_nt(doh, vb[:, gs])
                dpm = _dot_nt(doh, vm[:, gs])
                delta = (jnp.sum(pn * dp, axis=-1, keepdims=True)
                         + jnp.sum(pmn * dpm, axis=-1, keepdims=True))
                dsb = (pn * (dp - delta)).astype(BF16)
                dsm = (pmn * (dpm - delta)).astype(BF16)
                dsink = dsink - jnp.where(lane16 == h, jnp.sum(psn * delta, axis=0, keepdims=True), 0.0)
                dq = (_dot(dsb, kb[:, gs]) + _dot(dsm, km[:, gs])) * _QK_SCALE
                dq_ref[:, hs] = dq.astype(dq_ref.dtype)
                dk = dk + _dot_tn(dsb, qh)
                dkm = dkm + _dot_tn(dsm, qh)
                dv = dv + _dot_tn(pn.astype(BF16), doh)
                dvm = dvm + _dot_tn(pmn.astype(BF16), doh)
            dk_ref[pl.ds(start, 2 * BLOCK), gs] += dk * _QK_SCALE
            dv_ref[pl.ds(start, 2 * BLOCK), gs] += dv
            dk_ref[PAD:BLOCK, gs] += dkm * _QK_SCALE
            dv_ref[PAD:BLOCK, gs] += dvm
        ds_ref[...] += dsink

    return pl.pallas_call(
        body, name=name, grid=(nblk,),
        in_specs=[pl.BlockSpec((BLOCK, D_MODEL), lambda n: (n, _QCOL)),
                  pl.BlockSpec((lp, LANES), lambda n: (0, _KCOL)),
                  pl.BlockSpec((lp, LANES), lambda n: (0, _VCOL)),
                  pl.BlockSpec(sinks.shape, lambda n: (0, 0)),
                  pl.BlockSpec((BLOCK, D_MODEL), lambda n: (n, 1))],
        out_specs=[pl.BlockSpec((BLOCK, D_MODEL), lambda n: (n, 0)),
                   pl.BlockSpec((lp, LANES), lambda n: (0, 0)),
                   pl.BlockSpec((lp, LANES), lambda n: (0, 0)),
                   pl.BlockSpec((1, N_Q_HEADS), lambda n: (0, 0))],
        out_shape=[jax.ShapeDtypeStruct((lp, D_MODEL), BF16),
                   jax.ShapeDtypeStruct((lp, LANES), F32),
                   jax.ShapeDtypeStruct((lp, LANES), F32),
                   jax.ShapeDtypeStruct((1, N_Q_HEADS), F32)],
        compiler_params=pltpu.CompilerParams(dimension_semantics=("arbitrary",)),
    )(proj, proj, proj, sinks, dmix)


_ZW = D_SSM
_XBC_W = D_SSM + 2 * SSD_GROUPS * SSD_N
_DT_COL = (_ZW + _XBC_W) // LANES
EVEN_IN = 3 * D_MODEL + 2 * LANES
ODD_IN = _ZW + _XBC_W + SSD_HEADS
ODD_IN_PAD = _ZW + _XBC_W + LANES


def _ssm_prep_fwd(xc, proj, dt_bias, name):
    lp = xc.shape[0]
    tm = _pick(lp, (320, 256, 128))

    def body(r, first, xc_ref, dtr_ref, b_ref, act_ref, dt_ref):
        real = _real_rows(r, tm)
        act, _ = _silu_and_grad(xc_ref[...])
        act_ref[...] = jnp.where(real, act, 0.0)
        dt_ref[...] = jnp.where(real, _softplus(dtr_ref[...] + b_ref[...]), 0.0)

    return _rowcall(name, body, lp, tm, rows=[(xc, _XBC_W, 0), (proj, LANES, _DT_COL)],
                    vecs=[dt_bias], outs=[(_XBC_W, F32), (LANES, F32)])


def _ssm_prep_bwd(xc, proj, dt_bias, dxs, dxskip, db, dc, ddt, name):
    lp = xc.shape[0]
    tm = BLOCK

    def body(r, first, xc_ref, dtr_ref, dxs_ref, dsk_ref, db_ref, dc_ref, ddt_ref, b_ref,
             dxc_ref, ddtr_ref, dbias_ref):
        real = _real_rows(r, tm)
        _, ds = _silu_and_grad(xc_ref[...])
        dxc_ref[:, :D_SSM] = jnp.where(real, (dxs_ref[...] + dsk_ref[...]) * ds[:, :D_SSM], 0.0)
        dxc_ref[:, D_SSM:D_SSM + 1024] = jnp.where(real, db_ref[...] * ds[:, D_SSM:D_SSM + 1024], 0.0)
        dxc_ref[:, D_SSM + 1024:] = jnp.where(real, dc_ref[...] * ds[:, D_SSM + 1024:], 0.0)
        dd = jnp.where(real, ddt_ref[...] * _sigmoid(dtr_ref[...] + b_ref[...]), 0.0)
        ddtr_ref[...] = dd.astype(ddtr_ref.dtype)
        _acc_add(first, dbias_ref, jnp.sum(dd, axis=0, keepdims=True))

    return _rowcall(name, body, lp, tm,
                    rows=[(xc, _XBC_W, 0), (proj, LANES, _DT_COL), (dxs, D_SSM, 0), (dxskip, D_SSM, 0),
                          (db, 1024, 0), (dc, 1024, 0), (ddt, LANES, 0)],
                    vecs=[dt_bias], outs=[(_XBC_W, F32), (LANES, BF16)], accs=[((1, LANES), F32)])


def _ssd_common(dt, alog):
    a = -jnp.exp(alog)
    cs = _cumsum_rows(dt * a, BLOCK)
    cst = cs.T
    cl = cs[BLOCK - 1:BLOCK, :]
    tril = (lax.broadcasted_iota(jnp.int32, (BLOCK, BLOCK), 0)
            >= lax.broadcasted_iota(jnp.int32, (BLOCK, BLOCK), 1))
    return a, cs, cst, cl, jnp.exp(cs), jnp.exp(cl - cs), jnp.exp(cl), tril


def _head_cols(ecl, g):
    lane = lax.broadcasted_iota(jnp.int32, (1, SSD_HPG * SSD_P), 1)
    e = [ecl[:, SSD_HPG * g + hh:SSD_HPG * g + hh + 1] for hh in range(SSD_HPG)]
    return jnp.where(lane < SSD_P, e[0], jnp.where(lane < 2 * SSD_P, e[1],
                                                   jnp.where(lane < 3 * SSD_P, e[2], e[3])))


def _ssd_fwd(xbc, dt, alog, name):
    lp = xbc.shape[0]
    nc = lp // BLOCK
    gw = SSD_HPG * SSD_P

    def body(xs_ref, b_ref, c_ref, dt_ref, alog_ref, y_ref, so_ref, st, fx):
        n = pl.program_id(0)

        @pl.when(n == 0)
        def _():
            st[...] = jnp.zeros_like(st)

        dtv = dt_ref[...]
        _, cs, cst, cl, e, f, ecl, tril = _ssd_common(dtv, alog_ref[...])
        for g in range(SSD_GROUPS):
            bg = b_ref[:, g * SSD_N:(g + 1) * SSD_N].astype(BF16)
            cg = c_ref[:, g * SSD_N:(g + 1) * SSD_N].astype(BF16)
            gm = _dot_nt(cg, bg)
            stg = st[g]
            so_ref[0, g] = stg
            yoff = _dot(cg, stg.astype(BF16))
            for hh in range(SSD_HPG):
                h = SSD_HPG * g + hh
                hs = slice(h * SSD_P, (h + 1) * SSD_P)
                seg = cs[:, h:h + 1] - cst[h:h + 1, :]
                m = gm * jnp.exp(jnp.where(tril, seg, NEG))
                xdt = xs_ref[:, hs] * dtv[:, h:h + 1]
                y_ref[:, hs] = (_dot(m.astype(BF16), xdt.astype(BF16))
                                + e[:, h:h + 1] * yoff[:, hh * SSD_P:(hh + 1) * SSD_P])
                fx[:, hh * SSD_P:(hh + 1) * SSD_P] = f[:, h:h + 1] * xdt
            st[g] = stg * _head_cols(ecl, g) + _dot_tn(bg, fx[...].astype(BF16))

    return pl.pallas_call(
        body, name=name, grid=(nc,),
        in_specs=[pl.BlockSpec((BLOCK, D_SSM), lambda n: (n, 0)),
                  pl.BlockSpec((BLOCK, 1024), lambda n: (n, 2)),
                  pl.BlockSpec((BLOCK, 1024), lambda n: (n, 3)),
                  pl.BlockSpec((BLOCK, LANES), lambda n: (n, 0)),
                  pl.BlockSpec((1, LANES), lambda n: (0, 0))],
        out_specs=[pl.BlockSpec((BLOCK, D_SSM), lambda n: (n, 0)),
                   pl.BlockSpec((1, SSD_GROUPS, SSD_N, gw), lambda n: (n, 0, 0, 0))],
        out_shape=[jax.ShapeDtypeStruct((lp, D_SSM), F32),
                   jax.ShapeDtypeStruct((nc, SSD_GROUPS, SSD_N, gw), F32)],
        scratch_shapes=[pltpu.VMEM((SSD_GROUPS, SSD_N, gw), F32), pltpu.VMEM((BLOCK, gw), F32)],
        compiler_params=pltpu.CompilerParams(dimension_semantics=("arbitrary",)),
    )(xbc, xbc, xbc, dt, alog)


def _ssd_bwd(xbc, dt, alog, states, dy, name):
    lp = xbc.shape[0]
    nc = lp // BLOCK
    gw = SSD_HPG * SSD_P

    def body(xs_ref, b_ref, c_ref, dt_ref, alog_ref, dy_ref, st_ref,
             dxs_ref, db_ref, dc_ref, ddt_ref, dalog_ref, dst, edy, fx):
        i = pl.program_id(0)

        @pl.when(i == 0)
        def _():
            dst[...] = jnp.zeros_like(dst)
            dalog_ref[...] = jnp.zeros_like(dalog_ref)

        dtv = dt_ref[...]
        a, cs, cst, cl, e, f, ecl, tril = _ssd_common(dtv, alog_ref[...])
        lane = lax.broadcasted_iota(jnp.int32, (1, LANES), 1)
        sub = _row_iota(BLOCK)
        dcs = jnp.zeros((BLOCK, LANES), F32)
        dcst = jnp.zeros((LANES, BLOCK), F32)
        dcl = jnp.zeros((1, LANES), F32)
        ddtx = jnp.zeros((BLOCK, LANES), F32)
        for g in range(SSD_GROUPS):
            bg = b_ref[:, g * SSD_N:(g + 1) * SSD_N].astype(BF16)
            cg = c_ref[:, g * SSD_N:(g + 1) * SSD_N].astype(BF16)
            gm = _dot_nt(cg, bg)
            stg = st_ref[0, g]
            stb = stg.astype(BF16)
            dso = dst[g]
            dsob = dso.astype(BF16)
            yraw = _dot(cg, stb)
            dfx = _dot(bg, dsob)
            prodsum = jnp.sum(dso * stg, axis=0, keepdims=True)
            dgm = jnp.zeros((BLOCK, BLOCK), F32)
            for hh in range(SSD_HPG):
                h = SSD_HPG * g + hh
                hs = slice(h * SSD_P, (h + 1) * SSD_P)
                ls = slice(hh * SSD_P, (hh + 1) * SSD_P)
                eh = e[:, h:h + 1]
                fh = f[:, h:h + 1]
                dth = dtv[:, h:h + 1]
                xh = xs_ref[:, hs]
                xdt = xh * dth
                dyh = dy_ref[:, hs]
                dyb = dyh.astype(BF16)
                lam = jnp.exp(jnp.where(tril, cs[:, h:h + 1] - cst[h:h + 1, :], NEG))
                m = gm * lam
                dm = _dot_nt(dyb, xdt.astype(BF16))
                dxdt = _dot_tn(m.astype(BF16), dyb) + fh * dfx[:, ls]
                w = dm * m
                dgm = dgm + dm * lam
                dff = jnp.sum(dfx[:, ls] * xdt, axis=1, keepdims=True) * fh
                col = (jnp.sum(w, axis=1, keepdims=True)
                       + jnp.sum(dyh * yraw[:, ls], axis=1, keepdims=True) * eh - dff)
                onl = (lane == h).astype(F32)
                dcs = dcs + col * onl
                dcst = dcst - (sub == h).astype(F32) * jnp.sum(w, axis=0, keepdims=True)
                dclh = (jnp.sum(dff, axis=0, keepdims=True)
                        + ecl[:, h:h + 1] * jnp.sum(prodsum[:, ls], axis=1, keepdims=True))
                dcl = dcl + dclh * onl
                ddtx = ddtx + jnp.sum(dxdt * xh, axis=1, keepdims=True) * onl
                dxs_ref[:, hs] = dxdt * dth
                edy[:, ls] = eh * dyh
                fx[:, ls] = fh * xdt
            edyb = edy[...].astype(BF16)
            fxb = fx[...].astype(BF16)
            dgb = dgm.astype(BF16)
            dc_ref[:, g * SSD_N:(g + 1) * SSD_N] = _dot_nt(edyb, stb) + _dot(dgb, bg)
            db_ref[:, g * SSD_N:(g + 1) * SSD_N] = _dot_nt(fxb, dsob) + _dot_tn(dgb, cg)
            dst[g] = dso * _head_cols(ecl, g) + _dot_tn(cg, edyb)
        dcs = dcs + dcst.T + jnp.where(sub == BLOCK - 1, dcl, 0.0)
        dda = _rev_cumsum_rows(dcs, BLOCK)
        ddt_ref[...] = ddtx + dda * a
        dalog_ref[...] += jnp.sum(dda * dtv, axis=0, keepdims=True) * a

    rev = lambda i: nc - 1 - i
    return pl.pallas_call(
        body, name=name, grid=(nc,),
        in_specs=[pl.BlockSpec((BLOCK, D_SSM), lambda i: (rev(i), 0)),
                  pl.BlockSpec((BLOCK, 1024), lambda i: (rev(i), 2)),
                  pl.BlockSpec((BLOCK, 1024), lambda i: (rev(i), 3)),
                  pl.BlockSpec((BLOCK, LANES), lambda i: (rev(i), 0)),
                  pl.BlockSpec((1, LANES), lambda i: (0, 0)),
                  pl.BlockSpec((BLOCK, D_SSM), lambda i: (rev(i), 0)),
                  pl.BlockSpec((1, SSD_GROUPS, SSD_N, gw), lambda i: (rev(i), 0, 0, 0))],
        out_specs=[pl.BlockSpec((BLOCK, D_SSM), lambda i: (rev(i), 0)),
                   pl.BlockSpec((BLOCK, 1024), lambda i: (rev(i), 0)),
                   pl.BlockSpec((BLOCK, 1024), lambda i: (rev(i), 0)),
                   pl.BlockSpec((BLOCK, LANES), lambda i: (rev(i), 0)),
                   pl.BlockSpec((1, LANES), lambda i: (0, 0))],
        out_shape=[jax.ShapeDtypeStruct((lp, D_SSM), F32),
                   jax.ShapeDtypeStruct((lp, 1024), F32),
                   jax.ShapeDtypeStruct((lp, 1024), F32),
                   jax.ShapeDtypeStruct((lp, LANES), F32),
                   jax.ShapeDtypeStruct((1, LANES), F32)],
        scratch_shapes=[pltpu.VMEM((SSD_GROUPS, SSD_N, gw), F32),
                        pltpu.VMEM((BLOCK, gw), F32), pltpu.VMEM((BLOCK, gw), F32)],
        compiler_params=pltpu.CompilerParams(dimension_semantics=("arbitrary",)),
    )(xbc, xbc, xbc, dt, alog, dy, states)


_GN_GROUPS = 8
_GN_W = D_SSM // _GN_GROUPS


def _ssm_gate_fwd(yssd, xbc, proj, dskip, gnorm, name):
    lp = yssd.shape[0]
    tm = _pick(lp, (320, 256, 128))

    def body(r, first, y_ref, x_ref, z_ref, d_ref, g_ref, o_ref):
        sz, _ = _silu_and_grad(z_ref[...])
        y2 = (y_ref[...] + d_ref[...] * x_ref[...]) * sz
        for k in range(_GN_GROUPS):
            sl = slice(k * _GN_W, (k + 1) * _GN_W)
            yk = y2[:, sl]
            rs = lax.rsqrt(jnp.mean(yk * yk, axis=-1, keepdims=True) + EPS)
            o_ref[:, sl] = (yk * rs * g_ref[:, sl]).astype(o_ref.dtype)

    return _rowcall(name, body, lp, tm, rows=[(yssd, D_SSM, 0), (xbc, D_SSM, 0), (proj, D_SSM, 0)],
                    vecs=[dskip, gnorm], outs=[(D_SSM, BF16)])[0]


def _ssm_gate_bwd(yssd, xbc, proj, dskip, gnorm, dyn, name):
    lp = yssd.shape[0]
    tm = BLOCK

    def body(r, first, y_ref, x_ref, z_ref, dyn_ref, d_ref, g_ref,
             dy_ref, dx_ref, dz_ref, dd_ref, dg_ref):
        z = z_ref[...]
        sz, dsz = _silu_and_grad(z)
        xs = x_ref[...]
        y1 = y_ref[...] + d_ref[...] * xs
        y2 = y1 * sz
        dyn = dyn_ref[...]
        for k in range(_GN_GROUPS):
            sl = slice(k * _GN_W, (k + 1) * _GN_W)
            dx, dgt = _rms_bwd(y2[:, sl], g_ref[:, sl], dyn[:, sl])
            dy1 = dx * sz[:, sl]
            dy_ref[:, sl] = dy1
            dx_ref[:, sl] = dy1 * d_ref[:, sl]
            dz_ref[:, sl] = (dx * y1[:, sl] * dsz[:, sl]).astype(dz_ref.dtype)

            @pl.when(first)
            def _():
                dd_ref[:, sl] = jnp.zeros((1, _GN_W), F32)
                dg_ref[:, sl] = jnp.zeros((1, _GN_W), F32)

            dd_ref[:, sl] += jnp.sum(dy1 * xs[:, sl], axis=0, keepdims=True)
            dg_ref[:, sl] += jnp.sum(dgt, axis=0, keepdims=True)

    return _rowcall(name, body, lp, tm,
                    rows=[(yssd, D_SSM, 0), (xbc, D_SSM, 0), (proj, D_SSM, 0), (dyn, D_SSM, 0)],
                    vecs=[dskip, gnorm], outs=[(D_SSM, F32), (D_SSM, F32), (D_SSM, BF16)],
                    accs=[((1, D_SSM), F32), ((1, D_SSM), F32)])


def _adamw(w, g, m, v, name):
    r, c = w.shape
    tm = r if r <= 512 else _pick(r, (512, 352, 256, 128, 64, 32, 16, 8))
    c1 = 1.0 / (1.0 - ADAM_B1 ** ADAM_STEP)
    c2 = 1.0 / (1.0 - ADAM_B2 ** ADAM_STEP)

    def body(w_ref, g_ref, m_ref, v_ref, d_ref, nm_ref, nv_ref):
        gv = g_ref[...]
        nm = ADAM_B1 * m_ref[...] + (1.0 - ADAM_B1) * gv
        nv = ADAM_B2 * v_ref[...] + (1.0 - ADAM_B2) * (gv * gv)
        nm_ref[...] = nm
        nv_ref[...] = nv
        d_ref[...] = -ADAM_LR * ((nm * c1) / (jnp.sqrt(nv * c2) + ADAM_EPS) + ADAM_WD * w_ref[...])

    spec = pl.BlockSpec((tm, c), lambda i: (i, 0))
    return pl.pallas_call(
        body, name=name, grid=(r // tm,), in_specs=[spec] * 4, out_specs=[spec] * 3,
        out_shape=[jax.ShapeDtypeStruct((r, c), F32)] * 3,
        compiler_params=pltpu.CompilerParams(dimension_semantics=("parallel",)),
    )(w, g, m, v)


def _place():
    return lax.axis_index("x"), lax.axis_index("y"), lax.axis_index("c")


def _other_chips(x, y):
    return [(1 - x, y), (x, 1 - y), (1 - x, 1 - y)]


_ANY = pl.BlockSpec(memory_space=pl.ANY)


def _gather_chips(packs, name):
    n = len(packs)

    def body(*refs):
        ins, outs = refs[:n], refs[n:2 * n]
        send_sems, recv_sems, local_sems = refs[2 * n:]
        x, y, c = _place()
        me = 2 * x + y
        copies = []
        for p in range(n):
            loc = pltpu.make_async_copy(ins[p], outs[p].at[me], local_sems.at[p])
            loc.start()
            copies.append(loc)
        sends = []
        for p in range(n):
            for k, (px, py) in enumerate(_other_chips(x, y)):
                cp = pltpu.make_async_remote_copy(
                    src_ref=ins[p], dst_ref=outs[p].at[me],
                    send_sem=send_sems.at[3 * p + k], recv_sem=recv_sems.at[3 * p + k],
                    device_id=(px, py, c), device_id_type=MESH)
                cp.start()
                sends.append(cp)
        for p in range(n):
            for k, (px, py) in enumerate(_other_chips(x, y)):
                pltpu.make_async_remote_copy(
                    src_ref=ins[p], dst_ref=outs[p].at[2 * px + py],
                    send_sem=send_sems.at[3 * p + k], recv_sem=recv_sems.at[3 * p + k],
                    device_id=(px, py, c), device_id_type=MESH).wait_recv()
        for cp in sends:
            cp.wait_send()
        for cp in copies:
            cp.wait()

    return pl.pallas_call(
        body, name=name,
        in_specs=[_ANY] * n, out_specs=[_ANY] * n,
        out_shape=[jax.ShapeDtypeStruct((N_CHIPS,) + p.shape, p.dtype) for p in packs],
        scratch_shapes=[pltpu.SemaphoreType.DMA((3 * n,)), pltpu.SemaphoreType.DMA((3 * n,)),
                        pltpu.SemaphoreType.DMA((n,))],
    )(*packs)


def _sibling_half_swap(gs, name):
    n = len(gs)

    def body(*refs):
        ins, outs = refs[:n], refs[n:2 * n]
        send_sems, recv_sems = refs[2 * n:]
        x, y, c = _place()
        cps = []
        for p in range(n):
            hr = gs[p].shape[1] // 2
            src = ins[p].at[:, pl.ds(pl.multiple_of((1 - c) * hr, SUBLANES), hr), :]
            cp = pltpu.make_async_remote_copy(src_ref=src, dst_ref=outs[p], send_sem=send_sems.at[p],
                                              recv_sem=recv_sems.at[p], device_id=(x, y, 1 - c),
                                              device_id_type=MESH)
            cp.start()
            cps.append(cp)
        for cp in cps:
            cp.wait()

    return pl.pallas_call(
        body, name=name, in_specs=[_ANY] * n, out_specs=[_ANY] * n,
        out_shape=[jax.ShapeDtypeStruct((g.shape[0], g.shape[1] // 2, g.shape[2]), g.dtype) for g in gs],
        scratch_shapes=[pltpu.SemaphoreType.DMA((n,)), pltpu.SemaphoreType.DMA((n,))],
    )(*gs)


def _chip_scatter(gcs, name):
    n = len(gcs)

    def body(*refs):
        ins, outs = refs[:n], refs[n:2 * n]
        send_sems, recv_sems = refs[2 * n:]
        x, y, c = _place()
        sends = []
        for p in range(n):
            for k, (px, py) in enumerate(_other_chips(x, y)):
                cp = pltpu.make_async_remote_copy(
                    src_ref=ins[p].at[2 * px + py], dst_ref=outs[p].at[k],
                    send_sem=send_sems.at[3 * p + k], recv_sem=recv_sems.at[3 * p + k],
                    device_id=(px, py, c), device_id_type=MESH)
                cp.start()
                sends.append(cp)
        for cp in sends:
            cp.wait_recv()
        for cp in sends:
            cp.wait_send()

    return pl.pallas_call(
        body, name=name, in_specs=[_ANY] * n, out_specs=[_ANY] * n,
        out_shape=[jax.ShapeDtypeStruct((3,) + g.shape[1:], g.dtype) for g in gcs],
        scratch_shapes=[pltpu.SemaphoreType.DMA((3 * n,)), pltpu.SemaphoreType.DMA((3 * n,))],
    )(*gcs)


def _sibling_concat(ts, name):
    n = len(ts)

    def body(*refs):
        ins, outs = refs[:n], refs[n:2 * n]
        send_sems, recv_sems, local_sems = refs[2 * n:]
        x, y, c = _place()
        locs, cps = [], []
        for p in range(n):
            loc = pltpu.make_async_copy(ins[p], outs[p].at[c], local_sems.at[p])
            loc.start()
            locs.append(loc)
            cp = pltpu.make_async_remote_copy(src_ref=ins[p], dst_ref=outs[p].at[c], send_sem=send_sems.at[p],
                                              recv_sem=recv_sems.at[p], device_id=(x, y, 1 - c),
                                              device_id_type=MESH)
            cp.start()
            cps.append(cp)
        for p in range(n):
            pltpu.make_async_remote_copy(src_ref=ins[p], dst_ref=outs[p].at[1 - c], send_sem=send_sems.at[p],
                                         recv_sem=recv_sems.at[p], device_id=(x, y, 1 - c),
                                         device_id_type=MESH).wait_recv()
        for cp in cps:
            cp.wait_send()
        for loc in locs:
            loc.wait()

    return pl.pallas_call(
        body, name=name, in_specs=[_ANY] * n, out_specs=[_ANY] * n,
        out_shape=[jax.ShapeDtypeStruct((2,) + t.shape, t.dtype) for t in ts],
        scratch_shapes=[pltpu.SemaphoreType.DMA((n,)), pltpu.SemaphoreType.DMA((n,)),
                        pltpu.SemaphoreType.DMA((n,))],
    )(*ts)


def _add_half(g, recv, core, name):
    s, r, n = g.shape
    hr = r // 2

    def body(core_ref, g_ref, r_ref, o_ref):
        o_ref[...] = g_ref[...] + r_ref[...]

    return pl.pallas_call(
        body, name=name,
        grid_spec=pltpu.PrefetchScalarGridSpec(
            num_scalar_prefetch=1, grid=(s,),
            in_specs=[pl.BlockSpec((1, hr, n), lambda i, cr: (i, cr[0], 0)),
                      pl.BlockSpec((1, hr, n), lambda i, cr: (i, 0, 0))],
            out_specs=pl.BlockSpec((1, hr, n), lambda i, cr: (i, 0, 0))),
        out_shape=jax.ShapeDtypeStruct((s, hr, n), F32),
        compiler_params=pltpu.CompilerParams(dimension_semantics=("parallel",)),
    )(core, g, recv)


def _add_total(gc, recv, chip, name):
    s, h, n = gc.shape
    th = h // 2 if (h // 2) % SUBLANES == 0 else h

    def body(chip_ref, g_ref, r_ref, o_ref):
        o_ref[...] = ((g_ref[0] + r_ref[0]) + r_ref[1]) + r_ref[2]

    return pl.pallas_call(
        body, name=name,
        grid_spec=pltpu.PrefetchScalarGridSpec(
            num_scalar_prefetch=1, grid=(h // th,),
            in_specs=[pl.BlockSpec((1, th, n), lambda i, cr: (cr[0], i, 0)),
                      pl.BlockSpec((3, th, n), lambda i, cr: (0, i, 0))],
            out_specs=pl.BlockSpec((th, n), lambda i, cr: (i, 0))),
        out_shape=jax.ShapeDtypeStruct((h, n), F32),
        compiler_params=pltpu.CompilerParams(dimension_semantics=("parallel",)),
    )(chip, gc, recv)


def _allreduce_small(pack, name):
    r, l = pack.shape

    def body(p_ref, o_ref, land, send_sems, recv_sems):
        x, y, c = _place()
        me = 4 * x + 2 * y + c
        land[me] = p_ref[...]
        rel = [(fx, fy, fc) for fx in (0, 1) for fy in (0, 1) for fc in (0, 1)][1:]
        sends = []
        for k, (fx, fy, fc) in enumerate(rel):
            peer = (x ^ fx, y ^ fy, c ^ fc)
            cp = pltpu.make_async_remote_copy(
                src_ref=p_ref, dst_ref=land.at[me], send_sem=send_sems.at[k], recv_sem=recv_sems.at[k],
                device_id=peer, device_id_type=MESH)
            cp.start()
            sends.append(cp)
        for k, (fx, fy, fc) in enumerate(rel):
            src = 4 * (x ^ fx) + 2 * (y ^ fy) + (c ^ fc)
            pltpu.make_async_remote_copy(
                src_ref=p_ref, dst_ref=land.at[src], send_sem=send_sems.at[k], recv_sem=recv_sems.at[k],
                device_id=(x ^ fx, y ^ fy, c ^ fc), device_id_type=MESH).wait_recv()
        for cp in sends:
            cp.wait_send()
        acc = land[0]
        for d in range(1, N_DEV):
            acc = acc + land[d]
        o_ref[...] = acc

    vm = pl.BlockSpec(memory_space=pltpu.VMEM)
    return pl.pallas_call(
        body, name=name, in_specs=[vm], out_specs=vm,
        out_shape=jax.ShapeDtypeStruct((r, l), F32),
        scratch_shapes=[pltpu.VMEM((N_DEV, r, l), F32),
                        pltpu.SemaphoreType.DMA((N_DEV - 1,)), pltpu.SemaphoreType.DMA((N_DEV - 1,))],
    )(pack)


def _flat_rows(a, mult=SUBLANES * LANES):
    f = a.reshape(-1)
    padn = (-f.shape[0]) % mult
    if padn:
        f = jnp.concatenate([f, jnp.zeros((padn,), f.dtype)])
    return f


def _pack(arrs, mult=SUBLANES * LANES, total_mult=None):
    flat = [_flat_rows(a, mult) for a in arrs]
    sizes = [f.shape[0] for f in flat]
    if total_mult is not None:
        padn = (-sum(sizes)) % total_mult
        if padn:
            flat.append(jnp.zeros((padn,), flat[0].dtype))
    return jnp.concatenate(flat).reshape(-1, LANES), sizes


def _unpack(pack, shapes, sizes, lead=()):
    flat = pack.reshape(lead + (-1,))
    out, off = [], 0
    for shp, sz in zip(shapes, sizes):
        n = math.prod(shp)
        out.append(flat[..., off:off + n].reshape(lead + tuple(shp)))
        off += sz
    return out


def _cols_from_shards(g):
    s, k, n = g.shape
    return jnp.transpose(g, (1, 0, 2)).reshape(k, s * n)


def _cols_to_shards(w, s=N_CHIPS):
    k, n = w.shape
    return jnp.transpose(w.reshape(k, s, n // s), (1, 0, 2))


def _ffn_fwd(h, pre, post, w_up, cw, cb, w_down, tag):
    u = _rmsnorm_fwd(h, pre, f"{tag}_prenorm")
    hp = _mm_nn_sh(u, w_up, 2 * D_FF, f"{tag}_up")
    hc = _conv_fwd(hp, 0, 2 * D_FF, cw, cb, f"{tag}_conv")
    act = _ffn_act_fwd(hc, f"{tag}_act")
    o = _mm_nn(act, w_down, F32, f"{tag}_down")
    hn = _postnorm_res_fwd(h, o, post, f"{tag}_postnorm")
    return hn, (h, u, hp, hc, act, o)


def _ffn_bwd(dh, saved, pre, post, w_up, cw, w_down, tag):
    h, u, hp, hc, act, o = saved
    do, dpost = _postnorm_bwd(o, post, dh, f"{tag}_postnorm_bwd")
    dact = _mm_nt(do, w_down, f"{tag}_down_dx")
    dw_down = _mm_tn(act, do, f"{tag}_down_dw")
    dhc = _ffn_act_bwd(hc, dact, f"{tag}_act_bwd")
    dhp, dcw, dcb = _conv_bwd(hp, 0, 2 * D_FF, dhc, cw, f"{tag}_conv_bwd")
    du = _mm_nt_sh(dhp, w_up, f"{tag}_up_dx")
    dw_up = _mm_tn_sh(u, dhp, w_up.shape[2], f"{tag}_up_dw")
    dhn, dpre = _prenorm_bwd(h, pre, du, dh, f"{tag}_prenorm_bwd")
    return dhn, dict(pre=dpre, post=dpost, w_up=dw_up, conv_w=dcw[:3], conv_b=dcb, w_down=dw_down)


def _local_step(x, tgt, meta, P):
    seq, d = x.shape
    lp = seq + BLOCK
    h0 = jnp.concatenate([jnp.zeros((PAD, d), F32), meta, x], axis=0)
    tgt_p = jnp.concatenate([jnp.zeros((BLOCK, d), F32), tgt], axis=0)

    u0 = _rmsnorm_fwd(h0, P["l0_mix_pre_norm"], "l0_mix_prenorm")
    proj0 = _mm_nn_sh(u0, P["l0_w_in"], EVEN_IN, "l0_in")
    xrc = _conv_fwd(proj0, D_MODEL, D_MODEL, P["l0_lru_conv_w"], P["l0_lru_conv_b"], "l0_lru_conv")
    lru_args = (P["l0_lru_w_a"], P["l0_lru_w_x"], P["l0_lru_b_a"], P["l0_lru_b_x"], P["l0_lru_lambda"])
    ya, hl = _lru_fwd(proj0, xrc, *lru_args, "l0_lru")
    yb = _attn_fwd(proj0, P["l0_attn_sinks"], "l0_attn")
    mix0 = jnp.concatenate([ya, yb], axis=1)
    o0 = _mm_nn(mix0, P["l0_w_out"], F32, "l0_out")
    h1 = _postnorm_res_fwd(h0, o0, P["l0_mix_post_norm"], "l0_mix_postnorm")
    h2, ffn0 = _ffn_fwd(h1, P["l0_ffn_pre_norm"], P["l0_ffn_post_norm"], P["l0_ffn_w_up"],
                        P["l0_ffn_conv_w"], P["l0_ffn_conv_b"], P["l0_ffn_w_down"], "l0_ffn")
    u2 = _rmsnorm_fwd(h2, P["l1_mix_pre_norm"], "l1_mix_prenorm")
    proj1 = _mm_nn_sh(u2, P["l1_w_in"], ODD_IN_PAD, "l1_in")
    xc1 = _conv_fwd(proj1, _ZW, _XBC_W, P["l1_ssm_conv_w"], P["l1_ssm_conv_b"], "l1_ssm_conv")
    xbc, dt = _ssm_prep_fwd(xc1, proj1, P["l1_dt_bias"], "l1_ssm_prep")
    yssd, states = _ssd_fwd(xbc, dt, P["l1_a_log"], "l1_ssd")
    yn = _ssm_gate_fwd(yssd, xbc, proj1, P["l1_d_skip"], P["l1_gate_norm"], "l1_ssm_gate")
    o1 = _mm_nn(yn, P["l1_w_out"], F32, "l1_out")
    h3 = _postnorm_res_fwd(h2, o1, P["l1_mix_post_norm"], "l1_mix_postnorm")
    h4, ffn1 = _ffn_fwd(h3, P["l1_ffn_pre_norm"], P["l1_ffn_post_norm"], P["l1_ffn_w_up"],
                        P["l1_ffn_conv_w"], P["l1_ffn_conv_b"], P["l1_ffn_w_down"], "l1_ffn")
    dh4, loss_cols = _loss_fwd_bwd(h4, tgt_p, "loss")

    G = {}
    dh3, g = _ffn_bwd(dh4, ffn1, P["l1_ffn_pre_norm"], P["l1_ffn_post_norm"], P["l1_ffn_w_up"],
                      P["l1_ffn_conv_w"], P["l1_ffn_w_down"], "l1_ffn")
    for k, v in g.items():
        G["l1_ffn_" + (k + "_norm" if k in ("pre", "post") else k)] = v
    do1, G["l1_mix_post_norm"] = _postnorm_bwd(o1, P["l1_mix_post_norm"], dh3, "l1_mix_postnorm_bwd")
    dyn = _mm_nt(do1, P["l1_w_out"], "l1_out_dx")
    G["l1_w_out"] = _mm_tn(yn, do1, "l1_out_dw")
    dyssd, dxskip, dz, dd_cols, G["l1_gate_norm"] = _ssm_gate_bwd(
        yssd, xbc, proj1, P["l1_d_skip"], P["l1_gate_norm"], dyn, "l1_ssm_gate_bwd")
    G["l1_d_skip"] = dd_cols.reshape(SSD_HEADS, SSD_P).sum(axis=1)
    dxs, dbm, dcm, ddt, dalog = _ssd_bwd(xbc, dt, P["l1_a_log"], states, dyssd, "l1_ssd_bwd")
    G["l1_a_log"] = dalog[0, :SSD_HEADS]
    dxc, ddtr, dbias = _ssm_prep_bwd(xc1, proj1, P["l1_dt_bias"], dxs, dxskip, dbm, dcm, ddt,
                                     "l1_ssm_prep_bwd")
    G["l1_dt_bias"] = dbias[0, :SSD_HEADS]
    dxbc, dcw, dcb = _conv_bwd(proj1, _ZW, _XBC_W, dxc, P["l1_ssm_conv_w"], "l1_ssm_conv_bwd")
    G["l1_ssm_conv_w"] = dcw[:4]
    G["l1_ssm_conv_b"] = dcb
    dproj1 = jnp.concatenate([dz, dxbc, ddtr], axis=1)
    du2 = _mm_nt_sh(dproj1, P["l1_w_in"], "l1_in_dx")
    G["l1_w_in"] = _mm_tn_sh(u2, dproj1, ODD_IN // N_CHIPS, "l1_in_dw")
    dh2, G["l1_mix_pre_norm"] = _prenorm_bwd(h2, P["l1_mix_pre_norm"], du2, dh3, "l1_mix_prenorm_bwd")
    dh1, g = _ffn_bwd(dh2, ffn0, P["l0_ffn_pre_norm"], P["l0_ffn_post_norm"], P["l0_ffn_w_up"],
                      P["l0_ffn_conv_w"], P["l0_ffn_w_down"], "l0_ffn")
    for k, v in g.items():
        G["l0_ffn_" + (k + "_norm" if k in ("pre", "post") else k)] = v
    do0, G["l0_mix_post_norm"] = _postnorm_bwd(o0, P["l0_mix_post_norm"], dh1, "l0_mix_postnorm_bwd")
    dmix = _mm_nt(do0, P["l0_w_out"], "l0_out_dx")
    G["l0_w_out"] = _mm_tn(mix0, do0, "l0_out_dw")
    (dgate, dxrc, G["l0_lru_w_a"], G["l0_lru_w_x"], G["l0_lru_b_a"], G["l0_lru_b_x"],
     G["l0_lru_lambda"]) = _lru_bwd(proj0, xrc, hl, dmix, *lru_args, "l0_lru_bwd")
    dxr, dcw, dcb = _conv_bwd(proj0, D_MODEL, D_MODEL, dxrc, P["l0_lru_conv_w"], "l0_lru_conv_bwd")
    G["l0_lru_conv_w"] = dcw[:4]
    G["l0_lru_conv_b"] = dcb
    dq, dk, dv, G["l0_attn_sinks"] = _attn_bwd(proj0, P["l0_attn_sinks"], dmix, "l0_attn_bwd")
    dproj0 = jnp.concatenate([dgate, dxr, dq, dk.astype(BF16), dv.astype(BF16)], axis=1)
    du0 = _mm_nt_sh(dproj0, P["l0_w_in"], "l0_in_dx")
    G["l0_w_in"] = _mm_tn_sh(u0, dproj0, EVEN_IN // N_CHIPS, "l0_in_dw")
    dh0, G["l0_mix_pre_norm"] = _prenorm_bwd(h0, P["l0_mix_pre_norm"], du0, dh1, "l0_mix_prenorm_bwd")
    return loss_cols, dh0[BLOCK:], dh0[PAD:BLOCK], G


_BIG_COL = ("l0_w_in", "l0_ffn_w_up", "l1_w_in", "l1_ffn_w_up")
_BIG_ROW = ("l0_w_out", "l0_ffn_w_down", "l1_w_out", "l1_ffn_w_down")
_BIG = ("l0_w_in", "l0_w_out", "l0_ffn_w_up", "l0_ffn_w_down",
        "l1_w_in", "l1_w_out", "l1_ffn_w_up", "l1_ffn_w_down")
_SMALL_SHARDED = ("meta_tokens", "l0_lru_conv_w", "l0_ffn_conv_w", "l1_ssm_conv_w", "l1_ffn_conv_w")
_WEIGHTS = ("meta_tokens", "l0_mix_pre_norm", "l0_mix_post_norm", "l0_w_in", "l0_lru_conv_w",
            "l0_lru_conv_b", "l0_lru_w_a", "l0_lru_b_a", "l0_lru_w_x", "l0_lru_b_x", "l0_lru_lambda",
            "l0_attn_sinks", "l0_w_out", "l0_ffn_pre_norm", "l0_ffn_post_norm", "l0_ffn_w_up",
            "l0_ffn_conv_w", "l0_ffn_conv_b", "l0_ffn_w_down", "l1_mix_pre_norm", "l1_mix_post_norm",
            "l1_w_in", "l1_ssm_conv_w", "l1_ssm_conv_b", "l1_dt_bias", "l1_a_log", "l1_d_skip",
            "l1_gate_norm", "l1_w_out", "l1_ffn_pre_norm", "l1_ffn_post_norm", "l1_ffn_w_up",
            "l1_ffn_conv_w", "l1_ffn_conv_b", "l1_ffn_w_down")
_REPL = tuple(n for n in _WEIGHTS if n not in _BIG and n not in _SMALL_SHARDED)


def _pad_lanes(v, n=LANES):
    return jnp.concatenate([v, jnp.zeros((n - v.shape[0],), v.dtype)]).reshape(1, n)


def _step(x, tgt, W, M, V):
    cx, cy, cc = _place()
    chip = 2 * cx + cy

    small_pack, small_sizes = _pack([W[n] for n in _SMALL_SHARDED])
    gathered = _gather_chips([W[n].astype(BF16) for n in _BIG] + [small_pack], "gather_weights")
    small_full = _unpack(gathered[-1], [W[n].shape for n in _SMALL_SHARDED], small_sizes, lead=(N_CHIPS,))

    P = {}
    for n, g in zip(_BIG, gathered[:-1]):
        P[n] = g if n in _BIG_COL else g.reshape(-1, g.shape[-1])
    for n, g in zip(_SMALL_SHARDED, small_full):
        P[n] = _cols_from_shards(g)
    for n in _REPL:
        v = W[n]
        P[n] = v.reshape(1, -1) if v.ndim == 1 else v
    P["l0_lru_w_a"] = W["l0_lru_w_a"].astype(BF16)
    P["l0_lru_w_x"] = W["l0_lru_w_x"].astype(BF16)
    P["l1_dt_bias"] = _pad_lanes(W["l1_dt_bias"])
    P["l1_a_log"] = _pad_lanes(W["l1_a_log"])
    P["l1_d_skip"] = jnp.repeat(W["l1_d_skip"], SSD_P).reshape(1, D_SSM)
    meta = P.pop("meta_tokens")

    loss_cols, grad_x, grad_meta, G = _local_step(x, tgt, meta, P)
    G["meta_tokens"] = grad_meta

    core_idx = cc.astype(jnp.int32).reshape(1)
    chip_idx = chip.astype(jnp.int32).reshape(1)
    slabs = [G[n] if n in _BIG_COL else G[n].reshape(N_CHIPS, -1, G[n].shape[-1]) for n in _BIG]
    from_sib = _sibling_half_swap(slabs, "grad_sibling_swap")
    chip_sum = [_add_half(g, r, core_idx, f"grad_chip_sum_{n}") for n, g, r in zip(_BIG, slabs, from_sib)]
    from_chips = _chip_scatter(chip_sum, "grad_chip_scatter")
    total_half = [_add_total(g, r, chip_idx, f"grad_total_{n}") for n, g, r in zip(_BIG, chip_sum, from_chips)]
    total = _sibling_concat(total_half, "grad_sibling_concat")
    big_grads = {n: t.reshape(W[n].shape) for n, t in zip(_BIG, total)}

    small_names = list(_REPL) + list(_SMALL_SHARDED)
    small_list = [G[n] for n in small_names] + [loss_cols]
    spack, ssizes = _pack(small_list)
    sred = _allreduce_small(spack, "small_allreduce")
    sfull = _unpack(sred, [a.shape for a in small_list], ssizes)
    loss = 0.5 / D_MODEL * jnp.sum(sfull[-1])
    small_grads = {}
    for n, g in zip(small_names, sfull[:-1]):
        if n in _SMALL_SHARDED:
            wcols = W[n].shape[1]
            g = lax.dynamic_slice_in_dim(g, chip * wcols, wcols, axis=1)
        small_grads[n] = g.reshape(W[n].shape)

    grads, delta, new_m, new_v = {}, {}, {}, {}
    for n in _BIG:
        grads[n] = big_grads[n]
        delta[n], new_m[n], new_v[n] = _adamw(W[n], grads[n], M[n], V[n], f"adamw_{n}")
    s_names = [n for n in _WEIGHTS if n not in _BIG]
    tile_elems = 512 * LANES
    wp, wsz = _pack([W[n] for n in s_names], total_mult=tile_elems)
    gp, _ = _pack([small_grads[n] for n in s_names], total_mult=tile_elems)
    mp, _ = _pack([M[n] for n in s_names], total_mult=tile_elems)
    vp, _ = _pack([V[n] for n in s_names], total_mult=tile_elems)
    dp, nmp, nvp = _adamw(wp, gp, mp, vp, "adamw_small")
    shapes = [W[n].shape for n in s_names]
    for n, a, b, c_ in zip(s_names, _unpack(dp, shapes, wsz), _unpack(nmp, shapes, wsz),
                           _unpack(nvp, shapes, wsz)):
        grads[n] = small_grads[n]
        delta[n], new_m[n], new_v[n] = a, b, c_
    return loss, grad_x, grads, delta, new_m, new_v


def kernel(x, meta_tokens, l0_mix_pre_norm, l0_mix_post_norm, l0_w_in, l0_lru_conv_w, l0_lru_conv_b, l0_lru_w_a, l0_lru_b_a, l0_lru_w_x, l0_lru_b_x, l0_lru_lambda, l0_attn_sinks, l0_w_out, l0_ffn_pre_norm, l0_ffn_post_norm, l0_ffn_w_up, l0_ffn_conv_w, l0_ffn_conv_b, l0_ffn_w_down, l1_mix_pre_norm, l1_mix_post_norm, l1_w_in, l1_ssm_conv_w, l1_ssm_conv_b, l1_dt_bias, l1_a_log, l1_d_skip, l1_gate_norm, l1_w_out, l1_ffn_pre_norm, l1_ffn_post_norm, l1_ffn_w_up, l1_ffn_conv_w, l1_ffn_conv_b, l1_ffn_w_down, loss_target, m_meta_tokens, m_l0_mix_pre_norm, m_l0_mix_post_norm, m_l0_w_in, m_l0_lru_conv_w, m_l0_lru_conv_b, m_l0_lru_w_a, m_l0_lru_b_a, m_l0_lru_w_x, m_l0_lru_b_x, m_l0_lru_lambda, m_l0_attn_sinks, m_l0_w_out, m_l0_ffn_pre_norm, m_l0_ffn_post_norm, m_l0_ffn_w_up, m_l0_ffn_conv_w, m_l0_ffn_conv_b, m_l0_ffn_w_down, m_l1_mix_pre_norm, m_l1_mix_post_norm, m_l1_w_in, m_l1_ssm_conv_w, m_l1_ssm_conv_b, m_l1_dt_bias, m_l1_a_log, m_l1_d_skip, m_l1_gate_norm, m_l1_w_out, m_l1_ffn_pre_norm, m_l1_ffn_post_norm, m_l1_ffn_w_up, m_l1_ffn_conv_w, m_l1_ffn_conv_b, m_l1_ffn_w_down, v_meta_tokens, v_l0_mix_pre_norm, v_l0_mix_post_norm, v_l0_w_in, v_l0_lru_conv_w, v_l0_lru_conv_b, v_l0_lru_w_a, v_l0_lru_b_a, v_l0_lru_w_x, v_l0_lru_b_x, v_l0_lru_lambda, v_l0_attn_sinks, v_l0_w_out, v_l0_ffn_pre_norm, v_l0_ffn_post_norm, v_l0_ffn_w_up, v_l0_ffn_conv_w, v_l0_ffn_conv_b, v_l0_ffn_w_down, v_l1_mix_pre_norm, v_l1_mix_post_norm, v_l1_w_in, v_l1_ssm_conv_w, v_l1_ssm_conv_b, v_l1_dt_bias, v_l1_a_log, v_l1_d_skip, v_l1_gate_norm, v_l1_w_out, v_l1_ffn_pre_norm, v_l1_ffn_post_norm, v_l1_ffn_w_up, v_l1_ffn_conv_w, v_l1_ffn_conv_b, v_l1_ffn_w_down):
    args = locals()
    W = {n: args[n] for n in _WEIGHTS}
    M = {n: args["m_" + n] for n in _WEIGHTS}
    V = {n: args["v_" + n] for n in _WEIGHTS}
    loss, grad_x, grads, delta, new_m, new_v = _step(x[0], loss_target[0], W, M, V)
    return (loss, grad_x[None], *[grads[n] for n in _WEIGHTS], *[delta[n] for n in _WEIGHTS],
            *[new_m[n] for n in _WEIGHTS], *[new_v[n] for n in _WEIGHTS])
```

```python
import functools
import math

import jax
import jax.numpy as jnp
from jax import lax
from jax.experimental import pallas as pl
from jax.experimental.pallas import tpu as pltpu

F32 = jnp.float32
BF16 = jnp.bfloat16

D_MODEL = 1024
N_META = 16
BLOCK = 128
PAD = BLOCK - N_META
EPS = 1e-6
LRU_BLOCKS = 8
LRU_BS = 128
LRU_C = 8.0
N_Q_HEADS = 16
N_KV_HEADS = 2
HEAD_DIM = 64
Q_PER_KV = 8
WINDOW = 128
D_SSM = 2048
SSD_HEADS = 32
SSD_GROUPS = 8
SSD_HPG = 4
SSD_P = 64
SSD_N = 128
D_FF = 2816
NEG = -1e30
LANES = 128
SUBLANES = 8

ADAM_LR = 0.001
ADAM_B1 = 0.9
ADAM_B2 = 0.999
ADAM_EPS = 1e-08
ADAM_WD = 0.01
ADAM_STEP = 10

MESH = pl.DeviceIdType.MESH
N_CHIPS = 4
N_DEV = 8


def _pick(n, cands):
    for c in cands:
        if n % c == 0:
            return c
    raise ValueError(f"no tile for {n} in {cands}")


def _col_tile(n, limit=1792):
    best = None
    for t in range(LANES, min(n, limit) + 1, LANES):
        if n % t == 0:
            best = t
    if best is None:
        raise ValueError(f"no lane tile for {n}")
    return best


def _sigmoid(x):
    return 1.0 / (1.0 + jnp.exp(-x))


def _log1p(e):
    u = 1.0 + e
    return jnp.where(u == 1.0, e, jnp.log(u) * (e / jnp.where(u == 1.0, 1.0, u - 1.0)))


def _softplus(x):
    return jnp.maximum(x, 0.0) + _log1p(jnp.exp(-jnp.abs(x)))


def _neg_expm1(x):
    poly = x * (1.0 + x * (0.5 + x * (1.0 / 6.0 + x * (1.0 / 24.0 + x * (1.0 / 120.0)))))
    return -jnp.where(x > -0.05, poly, jnp.exp(x) - 1.0)


_GELU_C = math.sqrt(2.0 / math.pi)


def _gelu(x):
    t = jnp.tanh(_GELU_C * (x + 0.044715 * x * x * x))
    return 0.5 * x * (1.0 + t)


def _gelu_and_grad(x):
    x2 = x * x
    t = jnp.tanh(_GELU_C * (x + 0.044715 * x * x2))
    g = 0.5 * x * (1.0 + t)
    dg = 0.5 * (1.0 + t) + 0.5 * x * (1.0 - t * t) * _GELU_C * (1.0 + 3.0 * 0.044715 * x2)
    return g, dg


def _silu_and_grad(x):
    s = _sigmoid(x)
    return x * s, s * (1.0 + x * (1.0 - s))


def _dot(a, b):
    return jnp.dot(a, b, preferred_element_type=F32)


def _dot_nt(a, b):
    return lax.dot_general(a, b, (((1,), (1,)), ((), ())), preferred_element_type=F32)


def _dot_tn(a, b):
    return lax.dot_general(a, b, (((0,), (0,)), ((), ())), preferred_element_type=F32)


def _row_iota(t):
    return lax.broadcasted_iota(jnp.int32, (t, 1), 0)


def _scan_fwd(a, u, t):
    row = _row_iota(t)
    d = 1
    while d < t:
        m = row >= d
        u_sh = jnp.where(m, pltpu.roll(u, d, 0), 0.0)
        a_sh = jnp.where(m, pltpu.roll(a, d, 0), 1.0)
        u = u + a * u_sh
        a = a * a_sh
        d *= 2
    return a, u


def _scan_rev(c, x, t):
    row = _row_iota(t)
    d = 1
    while d < t:
        m = row < t - d
        x_sh = jnp.where(m, pltpu.roll(x, t - d, 0), 0.0)
        c_sh = jnp.where(m, pltpu.roll(c, t - d, 0), 1.0)
        x = x + c * x_sh
        c = c * c_sh
        d *= 2
    return c, x


def _cumsum_rows(x, t):
    row = _row_iota(t)
    d = 1
    while d < t:
        x = x + jnp.where(row >= d, pltpu.roll(x, d, 0), 0.0)
        d *= 2
    return x


def _rev_cumsum_rows(x, t):
    row = _row_iota(t)
    d = 1
    while d < t:
        x = x + jnp.where(row < t - d, pltpu.roll(x, t - d, 0), 0.0)
        d *= 2
    return x


def _rms_bwd(x, g, dy):
    rs = lax.rsqrt(jnp.mean(x * x, axis=-1, keepdims=True) + EPS)
    gy = dy * g
    dx = rs * gy - x * (rs * rs * rs) * jnp.mean(x * gy, axis=-1, keepdims=True)
    return dx, dy * x * rs


def _mm_nn(a, w, out_dtype, name):
    m, k = a.shape
    n = w.shape[1]
    tm = _pick(m, (640, 512, 256, 128))
    tn = _col_tile(n)

    def body(a_ref, w_ref, o_ref):
        o_ref[...] = _dot(a_ref[...].astype(BF16), w_ref[...]).astype(o_ref.dtype)

    return pl.pallas_call(
        body, name=name, grid=(n // tn, m // tm),
        in_specs=[pl.BlockSpec((tm, k), lambda j, i: (i, 0)),
                  pl.BlockSpec((k, tn), lambda j, i: (0, j))],
        out_specs=pl.BlockSpec((tm, tn), lambda j, i: (i, j)),
        out_shape=jax.ShapeDtypeStruct((m, n), out_dtype),
        compiler_params=pltpu.CompilerParams(dimension_semantics=("parallel", "parallel")),
    )(a, w)


def _mm_nt(dy, w, name):
    m, n = dy.shape
    k = w.shape[0]
    wide = n > 3328
    tm = _pick(m, (320, 256, 128)) if wide else _pick(m, (640, 512, 256, 128))
    tk = _col_tile(k, 512 if wide else 1408)

    def body(dy_ref, w_ref, o_ref):
        o_ref[...] = _dot_nt(dy_ref[...].astype(BF16), w_ref[...])

    return pl.pallas_call(
        body, name=name, grid=(k // tk, m // tm),
        in_specs=[pl.BlockSpec((tm, n), lambda j, i: (i, 0)),
                  pl.BlockSpec((tk, n), lambda j, i: (j, 0))],
        out_specs=pl.BlockSpec((tm, tk), lambda j, i: (i, j)),
        out_shape=jax.ShapeDtypeStruct((m, k), F32),
        compiler_params=pltpu.CompilerParams(dimension_semantics=("parallel", "parallel")),
    )(dy, w)


def _mm_tn(a, dy, name):
    m, k = a.shape
    n = dy.shape[1]
    tm = _pick(m, (640, 512, 256, 128))
    tk = _col_tile(k, 1408)
    tn = _col_tile(n, 1664)

    def body(a_ref, dy_ref, o_ref):
        @pl.when(pl.program_id(2) == 0)
        def _():
            o_ref[...] = jnp.zeros_like(o_ref)

        o_ref[...] += _dot_tn(a_ref[...].astype(BF16), dy_ref[...].astype(BF16))

    return pl.pallas_call(
        body, name=name, grid=(k // tk, n // tn, m // tm),
        in_specs=[pl.BlockSpec((tm, tk), lambda kk, j, i: (i, kk)),
                  pl.BlockSpec((tm, tn), lambda kk, j, i: (i, j))],
        out_specs=pl.BlockSpec((tk, tn), lambda kk, j, i: (kk, j)),
        out_shape=jax.ShapeDtypeStruct((k, n), F32),
        compiler_params=pltpu.CompilerParams(
            dimension_semantics=("parallel", "parallel", "arbitrary")),
    )(a, dy)


def _mm_nn_sh(a, w4, n_out, name):
    m, k = a.shape
    s, _, n = w4.shape
    tm = _pick(m, (320, 256, 128))

    def body(a_ref, w_ref, o_ref):
        av = a_ref[...].astype(BF16)
        for j in range(s):
            o_ref[:, j * n:(j + 1) * n] = _dot(av, w_ref[j])
        if n_out > s * n:
            o_ref[:, s * n:] = jnp.zeros((tm, n_out - s * n), F32)

    return pl.pallas_call(
        body, name=name, grid=(m // tm,),
        in_specs=[pl.BlockSpec((tm, k), lambda i: (i, 0)),
                  pl.BlockSpec((s, k, n), lambda i: (0, 0, 0))],
        out_specs=pl.BlockSpec((tm, n_out), lambda i: (i, 0)),
        out_shape=jax.ShapeDtypeStruct((m, n_out), F32),
        compiler_params=pltpu.CompilerParams(dimension_semantics=("parallel",)),
    )(a, w4)


def _mm_nt_sh(dy, w4, name):
    m, nn = dy.shape
    s, k, n = w4.shape
    tm = _pick(m, (320, 256, 128))
    tk = _col_tile(k, 256)

    def body(dy_ref, w_ref, o_ref):
        acc = _dot_nt(dy_ref[:, 0:n].astype(BF16), w_ref[0])
        for j in range(1, s):
            acc = acc + _dot_nt(dy_ref[:, j * n:(j + 1) * n].astype(BF16), w_ref[j])
        o_ref[...] = acc

    return pl.pallas_call(
        body, name=name, grid=(k // tk, m // tm),
        in_specs=[pl.BlockSpec((tm, nn), lambda j, i: (i, 0)),
                  pl.BlockSpec((s, tk, n), lambda j, i: (0, j, 0))],
        out_specs=pl.BlockSpec((tm, tk), lambda j, i: (i, j)),
        out_shape=jax.ShapeDtypeStruct((m, k), F32),
        compiler_params=pltpu.CompilerParams(dimension_semantics=("parallel", "parallel")),
    )(dy, w4)


def _mm_tn_sh(a, dy, n, name):
    m, k = a.shape
    nn = dy.shape[1]
    s = N_CHIPS
    tm = _pick(m, (640, 512, 256, 128))
    tk = _col_tile(k, 256)

    def body(a_ref, dy_ref, o_ref):
        @pl.when(pl.program_id(1) == 0)
        def _():
            o_ref[...] = jnp.zeros_like(o_ref)

        av = a_ref[...].astype(BF16)
        for j in range(s):
            o_ref[j] += _dot_tn(av, dy_ref[:, j * n:(j + 1) * n].astype(BF16))

    return pl.pallas_call(
        body, name=name, grid=(k // tk, m // tm),
        in_specs=[pl.BlockSpec((tm, tk), lambda kk, i: (i, kk)),
                  pl.BlockSpec((tm, nn), lambda kk, i: (i, 0))],
        out_specs=pl.BlockSpec((s, tk, n), lambda kk, i: (0, kk, 0)),
        out_shape=jax.ShapeDtypeStruct((s, k, n), F32),
        compiler_params=pltpu.CompilerParams(dimension_semantics=("parallel", "arbitrary")),
    )(a, dy)


def _rowcall(name, body, lp, tm, rows=(), prevs=(), vecs=(), outs=(), accs=(), scratch=(),
             reverse=False, seq=False):
    nt = lp // tm
    hb = tm // SUBLANES

    def ri(i):
        return nt - 1 - i if reverse else i

    in_specs, args = [], []
    for arr, w, cb in rows:
        in_specs.append(pl.BlockSpec((tm, w), lambda i, cb=cb: (ri(i), cb)))
        args.append(arr)
    for arr, w, cb in prevs:
        in_specs.append(pl.BlockSpec((SUBLANES, w), lambda i, cb=cb: (jnp.maximum(ri(i) * hb - 1, 0), cb)))
        args.append(arr)
    for arr in vecs:
        in_specs.append(pl.BlockSpec(arr.shape, lambda i, nd=arr.ndim: (0,) * nd))
        args.append(arr)
    out_shape, out_specs = [], []
    for w, dt in outs:
        out_shape.append(jax.ShapeDtypeStruct((lp, w), dt))
        out_specs.append(pl.BlockSpec((tm, w), lambda i: (ri(i), 0)))
    for shp, dt in accs:
        out_shape.append(jax.ShapeDtypeStruct(shp, dt))
        out_specs.append(pl.BlockSpec(shp, lambda i, nd=len(shp): (0,) * nd))

    def kern(*refs):
        i = pl.program_id(0)
        body(ri(i), i == 0, *refs)

    sem = ("arbitrary",) if (seq or accs) else ("parallel",)
    res = pl.pallas_call(
        kern, name=name, grid=(nt,), in_specs=in_specs, out_specs=out_specs,
        out_shape=out_shape, scratch_shapes=list(scratch),
        compiler_params=pltpu.CompilerParams(dimension_semantics=sem),
    )(*args)
    return res


def _acc_add(first, ref, val):
    @pl.when(first)
    def _():
        ref[...] = jnp.zeros_like(ref)

    ref[...] += val


def _real_rows(r, tm):
    return (r * tm + _row_iota(tm)) >= PAD


def _rmsnorm_fwd(h, g, name):
    lp, d = h.shape
    tm = _pick(lp, (640, 512, 256, 128))

    def body(r, first, h_ref, g_ref, u_ref):
        x = h_ref[...]
        rs = lax.rsqrt(jnp.mean(x * x, axis=-1, keepdims=True) + EPS)
        u_ref[...] = (x * rs * g_ref[...]).astype(u_ref.dtype)

    return _rowcall(name, body, lp, tm, rows=[(h, d, 0)], vecs=[g], outs=[(d, BF16)])[0]


def _postnorm_res_fwd(h, o, g, name):
    lp, d = h.shape
    tm = _pick(lp, (640, 512, 256, 128))

    def body(r, first, h_ref, o_ref, g_ref, out_ref):
        x = o_ref[...]
        rs = lax.rsqrt(jnp.mean(x * x, axis=-1, keepdims=True) + EPS)
        out_ref[...] = jnp.where(_real_rows(r, tm), h_ref[...] + x * rs * g_ref[...], 0.0)

    return _rowcall(name, body, lp, tm, rows=[(h, d, 0), (o, d, 0)], vecs=[g], outs=[(d, F32)])[0]


def _postnorm_bwd(o, g, dh, name):
    lp, d = o.shape
    tm = _pick(lp, (640, 512, 256, 128))

    def body(r, first, o_ref, dh_ref, g_ref, do_ref, dg_ref):
        dx, dgt = _rms_bwd(o_ref[...], g_ref[...], dh_ref[...])
        do_ref[...] = dx.astype(do_ref.dtype)
        _acc_add(first, dg_ref, jnp.sum(dgt, axis=0, keepdims=True))

    return _rowcall(name, body, lp, tm, rows=[(o, d, 0), (dh, d, 0)], vecs=[g],
                    outs=[(d, BF16)], accs=[((1, d), F32)])


def _prenorm_bwd(h, g, du, dh_res, name):
    lp, d = h.shape
    tm = _pick(lp, (640, 512, 256, 128))

    def body(r, first, h_ref, du_ref, dres_ref, g_ref, dh_ref, dg_ref):
        dx, dgt = _rms_bwd(h_ref[...], g_ref[...], du_ref[...])
        dh_ref[...] = jnp.where(_real_rows(r, tm), dres_ref[...] + dx, 0.0)
        _acc_add(first, dg_ref, jnp.sum(dgt, axis=0, keepdims=True))

    return _rowcall(name, body, lp, tm, rows=[(h, d, 0), (du, d, 0), (dh_res, d, 0)], vecs=[g],
                    outs=[(d, F32)], accs=[((1, d), F32)])


def _loss_fwd_bwd(h, tgt, name):
    lp, d = h.shape
    tm = _pick(lp, (640, 512, 256, 128))

    def body(r, first, h_ref, t_ref, dh_ref, ls_ref):
        tok = (r * tm + _row_iota(tm)) >= BLOCK
        e = jnp.where(tok, h_ref[...] - t_ref[...], 0.0)
        dh_ref[...] = e * (1.0 / d)
        _acc_add(first, ls_ref, jnp.sum(e * e, axis=0, keepdims=True))

    return _rowcall(name, body, lp, tm, rows=[(h, d, 0), (tgt, d, 0)],
                    outs=[(d, F32)], accs=[((1, d), F32)])


def _conv_tiles(lp, width):
    wc = _col_tile(width, 1408)
    tm = _pick(lp, (320, 256, 128))
    return tm, wc


def _conv_fwd(x, col_off, width, w, b, name):
    lp = x.shape[0]
    kk = w.shape[0]
    tm, wc = _conv_tiles(lp, width)
    offb = col_off // wc
    assert col_off % wc == 0
    hb = tm // SUBLANES

    def body(x_ref, xp_ref, w_ref, b_ref, y_ref):
        i = pl.program_id(1)
        xv = x_ref[...]
        halo = jnp.where(i > 0, xp_ref[...], 0.0)
        xx = jnp.concatenate([halo, xv], axis=0)
        acc = b_ref[...] + w_ref[kk - 1:kk, :] * xv
        for j in range(1, kk):
            acc = acc + w_ref[kk - 1 - j:kk - j, :] * pltpu.roll(xx, j, 0)[SUBLANES:, :]
        y_ref[...] = acc

    return pl.pallas_call(
        body, name=name, grid=(width // wc, lp // tm),
        in_specs=[pl.BlockSpec((tm, wc), lambda j, i: (i, offb + j)),
                  pl.BlockSpec((SUBLANES, wc), lambda j, i: (jnp.maximum(i * hb - 1, 0), offb + j)),
                  pl.BlockSpec((kk, wc), lambda j, i: (0, j)),
                  pl.BlockSpec((1, wc), lambda j, i: (0, j))],
        out_specs=pl.BlockSpec((tm, wc), lambda j, i: (i, j)),
        out_shape=jax.ShapeDtypeStruct((lp, width), F32),
        compiler_params=pltpu.CompilerParams(dimension_semantics=("parallel", "parallel")),
    )(x, x, w, b)


def _conv_bwd(x, col_off, width, dy, w, name):
    lp = x.shape[0]
    kk = w.shape[0]
    tm, wc = _conv_tiles(lp, width)
    offb = col_off // wc
    assert col_off % wc == 0
    hb = tm // SUBLANES
    nb8 = lp // SUBLANES

    def body(x_ref, xp_ref, dy_ref, dn_ref, w_ref, dx_ref, dw_ref, db_ref):
        i = pl.program_id(1)
        last = pl.num_programs(1) - 1
        xv = x_ref[...]
        dyv = dy_ref[...]
        xx = jnp.concatenate([jnp.where(i > 0, xp_ref[...], 0.0), xv], axis=0)
        dd = jnp.concatenate([dyv, jnp.where(i < last, dn_ref[...], 0.0)], axis=0)
        dx = w_ref[kk - 1:kk, :] * dyv
        rows = [jnp.sum(dyv * xv, axis=0, keepdims=True)]
        for m in range(1, kk):
            dx = dx + w_ref[kk - 1 - m:kk - m, :] * pltpu.roll(dd, tm + SUBLANES - m, 0)[:tm, :]
            rows.append(jnp.sum(dyv * pltpu.roll(xx, m, 0)[SUBLANES:, :], axis=0, keepdims=True))
        dx_ref[...] = dx.astype(dx_ref.dtype)
        dwp = jnp.concatenate(rows[::-1] + [jnp.zeros((SUBLANES - kk, wc), F32)], axis=0)

        @pl.when(i == 0)
        def _():
            dw_ref[...] = jnp.zeros_like(dw_ref)
            db_ref[...] = jnp.zeros_like(db_ref)

        dw_ref[...] += dwp
        db_ref[...] += jnp.sum(dyv, axis=0, keepdims=True)

    return pl.pallas_call(
        body, name=name, grid=(width // wc, lp // tm),
        in_specs=[pl.BlockSpec((tm, wc), lambda j, i: (i, offb + j)),
                  pl.BlockSpec((SUBLANES, wc), lambda j, i: (jnp.maximum(i * hb - 1, 0), offb + j)),
                  pl.BlockSpec((tm, wc), lambda j, i: (i, j)),
                  pl.BlockSpec((SUBLANES, wc), lambda j, i: (jnp.minimum((i + 1) * hb, nb8 - 1), j)),
                  pl.BlockSpec((kk, wc), lambda j, i: (0, j))],
        out_specs=[pl.BlockSpec((tm, wc), lambda j, i: (i, j)),
                   pl.BlockSpec((SUBLANES, wc), lambda j, i: (0, j)),
                   pl.BlockSpec((1, wc), lambda j, i: (0, j))],
        out_shape=[jax.ShapeDtypeStruct((lp, width), BF16),
                   jax.ShapeDtypeStruct((SUBLANES, width), F32),
                   jax.ShapeDtypeStruct((1, width), F32)],
        compiler_params=pltpu.CompilerParams(dimension_semantics=("parallel", "arbitrary")),
    )(x, x, dy, dy, w)


def _ffn_act_fwd(hc, name):
    lp = hc.shape[0]
    tm = _pick(lp, (320, 256, 128))

    def body(r, first, g_ref, u_ref, a_ref):
        a_ref[...] = (_gelu(g_ref[...]) * u_ref[...]).astype(a_ref.dtype)

    return _rowcall(name, body, lp, tm, rows=[(hc, D_FF, 0), (hc, D_FF, 1)], outs=[(D_FF, BF16)])[0]


def _ffn_act_bwd(hc, dact, name):
    lp = hc.shape[0]
    tm = BLOCK

    def body(r, first, g_ref, u_ref, da_ref, dh_ref):
        gl, dgl = _gelu_and_grad(g_ref[...])
        da = da_ref[...]
        dh_ref[:, :D_FF] = da * u_ref[...] * dgl
        dh_ref[:, D_FF:] = da * gl

    return _rowcall(name, body, lp, tm, rows=[(hc, D_FF, 0), (hc, D_FF, 1), (dact, D_FF, 0)],
                    outs=[(2 * D_FF, F32)])[0]


def _lru_gates(x, wa_ref, wx_ref, ba, bx, lam):
    xb = x.astype(BF16)
    za, zx = [], []
    for n in range(LRU_BLOCKS):
        xs = xb[:, n * LRU_BS:(n + 1) * LRU_BS]
        za.append(_dot(xs, wa_ref[n]))
        zx.append(_dot(xs, wx_ref[n]))
    r = _sigmoid(jnp.concatenate(za, axis=1) + ba)
    ig = _sigmoid(jnp.concatenate(zx, axis=1) + bx)
    sp = _softplus(-lam)
    log_a = -LRU_C * r * sp
    a = jnp.exp(log_a)
    om = _neg_expm1(2.0 * log_a)
    mult = jnp.sqrt(om)
    return xb, r, ig, sp, a, om, mult


def _lru_fwd(proj, xrc, wa, wx, ba, bx, lam, name):
    lp, d = xrc.shape
    tm = BLOCK

    def body(r_idx, first, gate_ref, x_ref, wa_ref, wx_ref, ba_ref, bx_ref, lam_ref,
             y_ref, h_ref, carry):
        @pl.when(first)
        def _():
            carry[...] = jnp.zeros_like(carry)

        x = x_ref[...]
        _, _, ig, _, a, _, mult = _lru_gates(x, wa_ref, wx_ref, ba_ref[...], bx_ref[...], lam_ref[...])
        u = jnp.where(_real_rows(r_idx, tm), mult * ig * x, 0.0)
        acum, hloc = _scan_fwd(a, u, tm)
        h = hloc + acum * carry[0:1, :]
        h_ref[...] = h
        carry[0:1, :] = h[tm - 1:tm, :]
        y_ref[...] = (_gelu(gate_ref[...]) * h).astype(y_ref.dtype)

    return _rowcall(name, body, lp, tm, rows=[(proj, d, 0), (xrc, d, 0)],
                    vecs=[wa, wx, ba, bx, lam], outs=[(d, BF16), (d, F32)],
                    scratch=[pltpu.VMEM((SUBLANES, d), F32)], seq=True)


def _lru_bwd(proj, xrc, hl, dmix, wa, wx, ba, bx, lam, name):
    lp, d = xrc.shape
    tm = BLOCK

    def body(r_idx, first, gate_ref, x_ref, h_ref, dy_ref, hp_ref, wa_ref, wx_ref, ba_ref, bx_ref,
             lam_ref, dgate_ref, dx_ref, dwa_ref, dwx_ref, dba_ref, dbx_ref, dlam_ref, carry):
        @pl.when(first)
        def _():
            carry[...] = jnp.zeros_like(carry)
            dwa_ref[...] = jnp.zeros_like(dwa_ref)
            dwx_ref[...] = jnp.zeros_like(dwx_ref)
            dba_ref[...] = jnp.zeros_like(dba_ref)
            dbx_ref[...] = jnp.zeros_like(dbx_ref)
            dlam_ref[...] = jnp.zeros_like(dlam_ref)

        x = x_ref[...]
        lam = lam_ref[...]
        xb, r, ig, sp, a, om, mult = _lru_gates(x, wa_ref, wx_ref, ba_ref[...], bx_ref[...], lam)
        h = h_ref[...]
        dy = dy_ref[...]
        gl, dgl = _gelu_and_grad(gate_ref[...])
        dgate_ref[...] = (dy * h * dgl).astype(dgate_ref.dtype)
        row = _row_iota(tm)
        lastrow = row == tm - 1
        xg = dy * gl + jnp.where(lastrow, carry[0:1, :], 0.0)
        c = jnp.where(lastrow, 1.0, pltpu.roll(a, tm - 1, 0))
        _, g = _scan_rev(c, xg, tm)
        carry[0:1, :] = a[0:1, :] * g[0:1, :]
        hprev_in = jnp.where(r_idx > 0, hp_ref[SUBLANES - 1:SUBLANES, :], 0.0)
        hprev = jnp.where(row == 0, hprev_in, pltpu.roll(h, 1, 0))
        du = jnp.where(_real_rows(r_idx, tm), g, 0.0)
        da = g * hprev
        dmult = du * ig * x
        dig = du * mult * x
        dxv = du * mult * ig
        e2 = 1.0 - om
        dlog_a = da * a - dmult * e2 / mult
        dr = dlog_a * (-LRU_C) * sp
        dsp = jnp.sum(dlog_a * (-LRU_C) * r, axis=0, keepdims=True)
        dlam_ref[...] += -dsp * _sigmoid(-lam)
        dza = dr * r * (1.0 - r)
        dzx = dig * ig * (1.0 - ig)
        dba_ref[...] += jnp.sum(dza, axis=0, keepdims=True)
        dbx_ref[...] += jnp.sum(dzx, axis=0, keepdims=True)
        dzab = dza.astype(BF16)
        dzxb = dzx.astype(BF16)
        parts = []
        for n in range(LRU_BLOCKS):
            sl = slice(n * LRU_BS, (n + 1) * LRU_BS)
            dwa_ref[n] += _dot_tn(xb[:, sl], dzab[:, sl])
            dwx_ref[n] += _dot_tn(xb[:, sl], dzxb[:, sl])
            parts.append(_dot_nt(dzab[:, sl], wa_ref[n]) + _dot_nt(dzxb[:, sl], wx_ref[n]))
        dx_ref[...] = dxv + jnp.concatenate(parts, axis=1)

    return _rowcall(name, body, lp, tm,
                    rows=[(proj, d, 0), (xrc, d, 0), (hl, d, 0), (dmix, d, 0)],
                    prevs=[(hl, d, 0)], vecs=[wa, wx, ba, bx, lam],
                    outs=[(d, BF16), (d, F32)],
                    accs=[((LRU_BLOCKS, LRU_BS, LRU_BS), F32), ((LRU_BLOCKS, LRU_BS, LRU_BS), F32),
                          ((1, d), F32), ((1, d), F32), ((1, d), F32)],
                    scratch=[pltpu.VMEM((SUBLANES, d), F32)], reverse=True, seq=True)


_SLOPES = [2.0 ** (-8.0 * (h + 1) / N_Q_HEADS) for h in range(N_Q_HEADS)]
_QK_SCALE = HEAD_DIM ** -0.5
_QCOL = 2 * D_MODEL // D_MODEL
_KCOL = (3 * D_MODEL) // LANES
_VCOL = _KCOL + 1


def _attn_masks(n):
    start = pl.multiple_of(jnp.maximum(n - 1, 0) * BLOCK, BLOCK)
    qi = n * BLOCK + lax.broadcasted_iota(jnp.int32, (BLOCK, 2 * BLOCK), 0)
    kj = start + lax.broadcasted_iota(jnp.int32, (BLOCK, 2 * BLOCK), 1)
    dist = qi - kj
    ok = (kj >= BLOCK) & (dist >= 0) & (dist < WINDOW)
    dm = (n * BLOCK - PAD + lax.broadcasted_iota(jnp.int32, (BLOCK, N_META), 0)
          - lax.broadcasted_iota(jnp.int32, (BLOCK, N_META), 1))
    okm = dm >= 0
    return start, ok, dist.astype(F32), okm, jnp.minimum(dm, WINDOW).astype(F32)


def _attn_probs(qh, kg, kmg, sink, slope, ok, distf, okm, dmf):
    s = jnp.where(ok, _dot_nt(qh, kg) * _QK_SCALE - slope * distf, NEG)
    sm = jnp.where(okm, _dot_nt(qh, kmg) * _QK_SCALE - slope * dmf, NEG)
    mx = jnp.maximum(jnp.maximum(jnp.max(s, axis=-1, keepdims=True),
                                 jnp.max(sm, axis=-1, keepdims=True)), sink)
    p = jnp.exp(s - mx)
    pm = jnp.exp(sm - mx)
    ps = jnp.exp(sink - mx)
    inv = 1.0 / (jnp.sum(p, axis=-1, keepdims=True) + jnp.sum(pm, axis=-1, keepdims=True) + ps)
    return p * inv, pm * inv, ps * inv


def _attn_fwd(proj, sinks, name, plan=None):
    lp = proj.shape[0]
    nblk = lp // BLOCK
    p_in, p_shapes, p_out, p_scr = _plan_parts(plan)

    def body(*refs):
        q_ref, k_ref, v_ref, sink_ref = refs[:4]
        cins = refs[4:4 + len(p_in)]
        o_ref = refs[4 + len(p_in)]
        couts = refs[5 + len(p_in):5 + len(p_in) + len(p_out)]
        sems = refs[5 + len(p_in) + len(p_out):]
        n = pl.program_id(0)
        if plan is not None:
            @pl.when(n == 0)
            def _():
                plan.start(cins, couts, sems)

        start, ok, distf, okm, dmf = _attn_masks(n)
        kb = k_ref[pl.ds(start, 2 * BLOCK), :].astype(BF16)
        vb = v_ref[pl.ds(start, 2 * BLOCK), :].astype(BF16)
        km = k_ref[PAD:BLOCK, :].astype(BF16)
        vm = v_ref[PAD:BLOCK, :].astype(BF16)
        for h in range(N_Q_HEADS):
            g = h // Q_PER_KV
            gs = slice(g * HEAD_DIM, (g + 1) * HEAD_DIM)
            qh = q_ref[:, h * HEAD_DIM:(h + 1) * HEAD_DIM].astype(BF16)
            pn, pmn, _ = _attn_probs(qh, kb[:, gs], km[:, gs], sink_ref[0:1, h:h + 1], _SLOPES[h],
                                     ok, distf, okm, dmf)
            o = _dot(pn.astype(BF16), vb[:, gs]) + _dot(pmn.astype(BF16), vm[:, gs])
            o_ref[:, h * HEAD_DIM:(h + 1) * HEAD_DIM] = o.astype(o_ref.dtype)
        if plan is not None:
            @pl.when(n == nblk - 1)
            def _():
                plan.wait(cins, couts, sems)

    res = pl.pallas_call(
        body, name=name, grid=(nblk,),
        in_specs=[pl.BlockSpec((BLOCK, D_MODEL), lambda n: (n, _QCOL)),
                  pl.BlockSpec((lp, LANES), lambda n: (0, _KCOL)),
                  pl.BlockSpec((lp, LANES), lambda n: (0, _VCOL)),
                  pl.BlockSpec(sinks.shape, lambda n: (0, 0))] + p_in,
        out_specs=[pl.BlockSpec((BLOCK, D_MODEL), lambda n: (n, 0))] + p_out,
        out_shape=[jax.ShapeDtypeStruct((lp, D_MODEL), BF16)] + p_shapes,
        scratch_shapes=p_scr,
        compiler_params=pltpu.CompilerParams(dimension_semantics=("arbitrary",)),
    )(proj, proj, proj, sinks, *(plan.ins if plan is not None else []))
    return res[0], res[1:]


def _attn_bwd(proj, sinks, dmix, name, plan=None):
    lp = proj.shape[0]
    nblk = lp // BLOCK

    p_in, p_shapes, p_out, p_scr = _plan_parts(plan)

    def body(*refs):
        q_ref, k_ref, v_ref, sink_ref, dy_ref = refs[:5]
        cins = refs[5:5 + len(p_in)]
        dq_ref, dk_ref, dv_ref, ds_ref = refs[5 + len(p_in):9 + len(p_in)]
        couts = refs[9 + len(p_in):9 + len(p_in) + len(p_out)]
        sems = refs[9 + len(p_in) + len(p_out):]
        n = pl.program_id(0)

        @pl.when(n == 0)
        def _():
            dk_ref[...] = jnp.zeros_like(dk_ref)
            dv_ref[...] = jnp.zeros_like(dv_ref)
            ds_ref[...] = jnp.zeros_like(ds_ref)
            if plan is not None:
                plan.start(cins, couts, sems)

        start, ok, distf, okm, dmf = _attn_masks(n)
        kb = k_ref[pl.ds(start, 2 * BLOCK), :].astype(BF16)
        vb = v_ref[pl.ds(start, 2 * BLOCK), :].astype(BF16)
        km = k_ref[PAD:BLOCK, :].astype(BF16)
        vm = v_ref[PAD:BLOCK, :].astype(BF16)
        lane16 = lax.broadcasted_iota(jnp.int32, (1, N_Q_HEADS), 1)
        dsink = jnp.zeros((1, N_Q_HEADS), F32)
        for g in range(N_KV_HEADS):
            gs = slice(g * HEAD_DIM, (g + 1) * HEAD_DIM)
            dk = jnp.zeros((2 * BLOCK, HEAD_DIM), F32)
            dv = jnp.zeros((2 * BLOCK, HEAD_DIM), F32)
            dkm = jnp.zeros((N_META, HEAD_DIM), F32)
            dvm = jnp.zeros((N_META, HEAD_DIM), F32)
            for hh in range(Q_PER_KV):
                h = g * Q_PER_KV + hh
                hs = slice(h * HEAD_DIM, (h + 1) * HEAD_DIM)
                qh = q_ref[:, hs].astype(BF16)
                doh = dy_ref[:, hs].astype(BF16)
                pn, pmn, psn = _attn_probs(qh, kb[:, gs], km[:, gs], sink_ref[0:1, h:h + 1],
                                           _SLOPES[h], ok, distf, okm, dmf)
                dp = _dot_nt(doh, vb[:, gs])
                dpm = _dot_nt(doh, vm[:, gs])
                delta = (jnp.sum(pn * dp, axis=-1, keepdims=True)
                         + jnp.sum(pmn * dpm, axis=-1, keepdims=True))
                dsb = (pn * (dp - delta)).astype(BF16)
                dsm = (pmn * (dpm - delta)).astype(BF16)
                dsink = dsink - jnp.where(lane16 == h, jnp.sum(psn * delta, axis=0, keepdims=True), 0.0)
                dq = (_dot(dsb, kb[:, gs]) + _dot(dsm, km[:, gs])) * _QK_SCALE
                dq_ref[:, hs] = dq.astype(dq_ref.dtype)
                dk = dk + _dot_tn(dsb, qh)
                dkm = dkm + _dot_tn(dsm, qh)
                dv = dv + _dot_tn(pn.astype(BF16), doh)
                dvm = dvm + _dot_tn(pmn.astype(BF16), doh)
            dk_ref[pl.ds(start, 2 * BLOCK), gs] += dk * _QK_SCALE
            dv_ref[pl.ds(start, 2 * BLOCK), gs] += dv
            dk_ref[PAD:BLOCK, gs] += dkm * _QK_SCALE
            dv_ref[PAD:BLOCK, gs] += dvm
        ds_ref[...] += dsink
        if plan is not None:
            @pl.when(n == nblk - 1)
            def _():
                plan.wait(cins, couts, sems)

    res = pl.pallas_call(
        body, name=name, grid=(nblk,),
        in_specs=[pl.BlockSpec((BLOCK, D_MODEL), lambda n: (n, _QCOL)),
                  pl.BlockSpec((lp, LANES), lambda n: (0, _KCOL)),
                  pl.BlockSpec((lp, LANES), lambda n: (0, _VCOL)),
                  pl.BlockSpec(sinks.shape, lambda n: (0, 0)),
                  pl.BlockSpec((BLOCK, D_MODEL), lambda n: (n, 1))] + p_in,
        out_specs=[pl.BlockSpec((BLOCK, D_MODEL), lambda n: (n, 0)),
                   pl.BlockSpec((lp, LANES), lambda n: (0, 0)),
                   pl.BlockSpec((lp, LANES), lambda n: (0, 0)),
                   pl.BlockSpec((1, N_Q_HEADS), lambda n: (0, 0))] + p_out,
        out_shape=[jax.ShapeDtypeStruct((lp, D_MODEL), BF16),
                   jax.ShapeDtypeStruct((lp, LANES), F32),
                   jax.ShapeDtypeStruct((lp, LANES), F32),
                   jax.ShapeDtypeStruct((1, N_Q_HEADS), F32)] + p_shapes,
        scratch_shapes=p_scr,
        compiler_params=pltpu.CompilerParams(dimension_semantics=("arbitrary",)),
    )(proj, proj, proj, sinks, dmix, *(plan.ins if plan is not None else []))
    return res[:4], res[4:]


_ZW = D_SSM
_XBC_W = D_SSM + 2 * SSD_GROUPS * SSD_N
_DT_COL = (_ZW + _XBC_W) // LANES
EVEN_IN = 3 * D_MODEL + 2 * LANES
ODD_IN = _ZW + _XBC_W + SSD_HEADS
ODD_IN_PAD = _ZW + _XBC_W + LANES


def _ssm_prep_fwd(xc, proj, dt_bias, name):
    lp = xc.shape[0]
    tm = _pick(lp, (320, 256, 128))

    def body(r, first, xc_ref, dtr_ref, b_ref, act_ref, dt_ref):
        real = _real_rows(r, tm)
        act, _ = _silu_and_grad(xc_ref[...])
        act_ref[...] = jnp.where(real, act, 0.0)
        dt_ref[...] = jnp.where(real, _softplus(dtr_ref[...] + b_ref[...]), 0.0)

    return _rowcall(name, body, lp, tm, rows=[(xc, _XBC_W, 0), (proj, LANES, _DT_COL)],
                    vecs=[dt_bias], outs=[(_XBC_W, F32), (LANES, F32)])


def _ssm_prep_bwd(xc, proj, dt_bias, dxs, dxskip, db, dc, ddt, name):
    lp = xc.shape[0]
    tm = BLOCK

    def body(r, first, xc_ref, dtr_ref, dxs_ref, dsk_ref, db_ref, dc_ref, ddt_ref, b_ref,
             dxc_ref, ddtr_ref, dbias_ref):
        real = _real_rows(r, tm)
        _, ds = _silu_and_grad(xc_ref[...])
        dxc_ref[:, :D_SSM] = jnp.where(real, (dxs_ref[...] + dsk_ref[...]) * ds[:, :D_SSM], 0.0)
        dxc_ref[:, D_SSM:D_SSM + 1024] = jnp.where(real, db_ref[...] * ds[:, D_SSM:D_SSM + 1024], 0.0)
        dxc_ref[:, D_SSM + 1024:] = jnp.where(real, dc_ref[...] * ds[:, D_SSM + 1024:], 0.0)
        dd = jnp.where(real, ddt_ref[...] * _sigmoid(dtr_ref[...] + b_ref[...]), 0.0)
        ddtr_ref[...] = dd.astype(ddtr_ref.dtype)
        _acc_add(first, dbias_ref, jnp.sum(dd, axis=0, keepdims=True))

    return _rowcall(name, body, lp, tm,
                    rows=[(xc, _XBC_W, 0), (proj, LANES, _DT_COL), (dxs, D_SSM, 0), (dxskip, D_SSM, 0),
                          (db, 1024, 0), (dc, 1024, 0), (ddt, LANES, 0)],
                    vecs=[dt_bias], outs=[(_XBC_W, F32), (LANES, BF16)], accs=[((1, LANES), F32)])


def _ssd_common(dt, alog):
    a = -jnp.exp(alog)
    cs = _cumsum_rows(dt * a, BLOCK)
    cst = cs.T
    cl = cs[BLOCK - 1:BLOCK, :]
    tril = (lax.broadcasted_iota(jnp.int32, (BLOCK, BLOCK), 0)
            >= lax.broadcasted_iota(jnp.int32, (BLOCK, BLOCK), 1))
    return a, cs, cst, cl, jnp.exp(cs), jnp.exp(cl - cs), jnp.exp(cl), tril


def _head_cols(ecl, g):
    lane = lax.broadcasted_iota(jnp.int32, (1, SSD_HPG * SSD_P), 1)
    e = [ecl[:, SSD_HPG * g + hh:SSD_HPG * g + hh + 1] for hh in range(SSD_HPG)]
    return jnp.where(lane < SSD_P, e[0], jnp.where(lane < 2 * SSD_P, e[1],
                                                   jnp.where(lane < 3 * SSD_P, e[2], e[3])))


def _ssd_fwd(xbc, dt, alog, name):
    lp = xbc.shape[0]
    nc = lp // BLOCK
    gw = SSD_HPG * SSD_P

    def body(xs_ref, b_ref, c_ref, dt_ref, alog_ref, y_ref, so_ref, st, fx):
        n = pl.program_id(0)

        @pl.when(n == 0)
        def _():
            st[...] = jnp.zeros_like(st)

        dtv = dt_ref[...]
        _, cs, cst, cl, e, f, ecl, tril = _ssd_common(dtv, alog_ref[...])
        for g in range(SSD_GROUPS):
            bg = b_ref[:, g * SSD_N:(g + 1) * SSD_N].astype(BF16)
            cg = c_ref[:, g * SSD_N:(g + 1) * SSD_N].astype(BF16)
            gm = _dot_nt(cg, bg)
            stg = st[g]
            so_ref[0, g] = stg
            yoff = _dot(cg, stg.astype(BF16))
            for hh in range(SSD_HPG):
                h = SSD_HPG * g + hh
                hs = slice(h * SSD_P, (h + 1) * SSD_P)
                seg = cs[:, h:h + 1] - cst[h:h + 1, :]
                m = gm * jnp.exp(jnp.where(tril, seg, NEG))
                xdt = xs_ref[:, hs] * dtv[:, h:h + 1]
                y_ref[:, hs] = (_dot(m.astype(BF16), xdt.astype(BF16))
                                + e[:, h:h + 1] * yoff[:, hh * SSD_P:(hh + 1) * SSD_P])
                fx[:, hh * SSD_P:(hh + 1) * SSD_P] = f[:, h:h + 1] * xdt
            st[g] = stg * _head_cols(ecl, g) + _dot_tn(bg, fx[...].astype(BF16))

    return pl.pallas_call(
        body, name=name, grid=(nc,),
        in_specs=[pl.BlockSpec((BLOCK, D_SSM), lambda n: (n, 0)),
                  pl.BlockSpec((BLOCK, 1024), lambda n: (n, 2)),
                  pl.BlockSpec((BLOCK, 1024), lambda n: (n, 3)),
                  pl.BlockSpec((BLOCK, LANES), lambda n: (n, 0)),
                  pl.BlockSpec((1, LANES), lambda n: (0, 0))],
        out_specs=[pl.BlockSpec((BLOCK, D_SSM), lambda n: (n, 0)),
                   pl.BlockSpec((1, SSD_GROUPS, SSD_N, gw), lambda n: (n, 0, 0, 0))],
        out_shape=[jax.ShapeDtypeStruct((lp, D_SSM), F32),
                   jax.ShapeDtypeStruct((nc, SSD_GROUPS, SSD_N, gw), F32)],
        scratch_shapes=[pltpu.VMEM((SSD_GROUPS, SSD_N, gw), F32), pltpu.VMEM((BLOCK, gw), F32)],
        compiler_params=pltpu.CompilerParams(dimension_semantics=("arbitrary",)),
    )(xbc, xbc, xbc, dt, alog)


def _ssd_bwd(xbc, dt, alog, states, dy, name, plan=None):
    lp = xbc.shape[0]
    nc = lp // BLOCK
    gw = SSD_HPG * SSD_P
    p_in, p_shapes, p_out, p_scr = _plan_parts(plan)

    def body(*refs):
        xs_ref, b_ref, c_ref, dt_ref, alog_ref, dy_ref, st_ref = refs[:7]
        cins = refs[7:7 + len(p_in)]
        dxs_ref, db_ref, dc_ref, ddt_ref, dalog_ref = refs[7 + len(p_in):12 + len(p_in)]
        couts = refs[12 + len(p_in):12 + len(p_in) + len(p_out)]
        dst, edy, fx = refs[12 + len(p_in) + len(p_out):15 + len(p_in) + len(p_out)]
        sems = refs[15 + len(p_in) + len(p_out):]
        i = pl.program_id(0)

        @pl.when(i == 0)
        def _():
            dst[...] = jnp.zeros_like(dst)
            dalog_ref[...] = jnp.zeros_like(dalog_ref)
            if plan is not None:
                plan.start(cins, couts, sems)

        dtv = dt_ref[...]
        a, cs, cst, cl, e, f, ecl, tril = _ssd_common(dtv, alog_ref[...])
        lane = lax.broadcasted_iota(jnp.int32, (1, LANES), 1)
        sub = _row_iota(BLOCK)
        dcs = jnp.zeros((BLOCK, LANES), F32)
        dcst = jnp.zeros((LANES, BLOCK), F32)
        dcl = jnp.zeros((1, LANES), F32)
        ddtx = jnp.zeros((BLOCK, LANES), F32)
        for g in range(SSD_GROUPS):
            bg = b_ref[:, g * SSD_N:(g + 1) * SSD_N].astype(BF16)
            cg = c_ref[:, g * SSD_N:(g + 1) * SSD_N].astype(BF16)
            gm = _dot_nt(cg, bg)
            stg = st_ref[0, g]
            stb = stg.astype(BF16)
            dso = dst[g]
            dsob = dso.astype(BF16)
            yraw = _dot(cg, stb)
            dfx = _dot(bg, dsob)
            prodsum = jnp.sum(dso * stg, axis=0, keepdims=True)
            dgm = jnp.zeros((BLOCK, BLOCK), F32)
            for hh in range(SSD_HPG):
                h = SSD_HPG * g + hh
                hs = slice(h * SSD_P, (h + 1) * SSD_P)
                ls = slice(hh * SSD_P, (hh + 1) * SSD_P)
                eh = e[:, h:h + 1]
                fh = f[:, h:h + 1]
                dth = dtv[:, h:h + 1]
                xh = xs_ref[:, hs]
                xdt = xh * dth
                dyh = dy_ref[:, hs]
                dyb = dyh.astype(BF16)
                lam = jnp.exp(jnp.where(tril, cs[:, h:h + 1] - cst[h:h + 1, :], NEG))
                m = gm * lam
                dm = _dot_nt(dyb, xdt.astype(BF16))
                dxdt = _dot_tn(m.astype(BF16), dyb) + fh * dfx[:, ls]
                w = dm * m
                dgm = dgm + dm * lam
                dff = jnp.sum(dfx[:, ls] * xdt, axis=1, keepdims=True) * fh
                col = (jnp.sum(w, axis=1, keepdims=True)
                       + jnp.sum(dyh * yraw[:, ls], axis=1, keepdims=True) * eh - dff)
                onl = (lane == h).astype(F32)
                dcs = dcs + col * onl
                dcst = dcst - (sub == h).astype(F32) * jnp.sum(w, axis=0, keepdims=True)
                dclh = (jnp.sum(dff, axis=0, keepdims=True)
                        + ecl[:, h:h + 1] * jnp.sum(prodsum[:, ls], axis=1, keepdims=True))
                dcl = dcl + dclh * onl
                ddtx = ddtx + jnp.sum(dxdt * xh, axis=1, keepdims=True) * onl
                dxs_ref[:, hs] = dxdt * dth
                edy[:, ls] = eh * dyh
                fx[:, ls] = fh * xdt
            edyb = edy[...].astype(BF16)
            fxb = fx[...].astype(BF16)
            dgb = dgm.astype(BF16)
            dc_ref[:, g * SSD_N:(g + 1) * SSD_N] = _dot_nt(edyb, stb) + _dot(dgb, bg)
            db_ref[:, g * SSD_N:(g + 1) * SSD_N] = _dot_nt(fxb, dsob) + _dot_tn(dgb, cg)
            dst[g] = dso * _head_cols(ecl, g) + _dot_tn(cg, edyb)
        dcs = dcs + dcst.T + jnp.where(sub == BLOCK - 1, dcl, 0.0)
        dda = _rev_cumsum_rows(dcs, BLOCK)
        ddt_ref[...] = ddtx + dda * a
        dalog_ref[...] += jnp.sum(dda * dtv, axis=0, keepdims=True) * a
        if plan is not None:
            @pl.when(i == nc - 1)
            def _():
                plan.wait(cins, couts, sems)

    rev = lambda i: nc - 1 - i
    res = pl.pallas_call(
        body, name=name, grid=(nc,),
        in_specs=[pl.BlockSpec((BLOCK, D_SSM), lambda i: (rev(i), 0)),
                  pl.BlockSpec((BLOCK, 1024), lambda i: (rev(i), 2)),
                  pl.BlockSpec((BLOCK, 1024), lambda i: (rev(i), 3)),
                  pl.BlockSpec((BLOCK, LANES), lambda i: (rev(i), 0)),
                  pl.BlockSpec((1, LANES), lambda i: (0, 0)),
                  pl.BlockSpec((BLOCK, D_SSM), lambda i: (rev(i), 0)),
                  pl.BlockSpec((1, SSD_GROUPS, SSD_N, gw), lambda i: (rev(i), 0, 0, 0))] + p_in,
        out_specs=[pl.BlockSpec((BLOCK, D_SSM), lambda i: (rev(i), 0)),
                   pl.BlockSpec((BLOCK, 1024), lambda i: (rev(i), 0)),
                   pl.BlockSpec((BLOCK, 1024), lambda i: (rev(i), 0)),
                   pl.BlockSpec((BLOCK, LANES), lambda i: (rev(i), 0)),
                   pl.BlockSpec((1, LANES), lambda i: (0, 0))] + p_out,
        out_shape=[jax.ShapeDtypeStruct((lp, D_SSM), F32),
                   jax.ShapeDtypeStruct((lp, 1024), F32),
                   jax.ShapeDtypeStruct((lp, 1024), F32),
                   jax.ShapeDtypeStruct((lp, LANES), F32),
                   jax.ShapeDtypeStruct((1, LANES), F32)] + p_shapes,
        scratch_shapes=[pltpu.VMEM((SSD_GROUPS, SSD_N, gw), F32),
                        pltpu.VMEM((BLOCK, gw), F32), pltpu.VMEM((BLOCK, gw), F32)] + p_scr,
        compiler_params=pltpu.CompilerParams(dimension_semantics=("arbitrary",)),
    )(xbc, xbc, xbc, dt, alog, dy, states, *(plan.ins if plan is not None else []))
    return res[:5], res[5:]


_GN_GROUPS = 8
_GN_W = D_SSM // _GN_GROUPS


def _ssm_gate_fwd(yssd, xbc, proj, dskip, gnorm, name):
    lp = yssd.shape[0]
    tm = _pick(lp, (320, 256, 128))

    def body(r, first, y_ref, x_ref, z_ref, d_ref, g_ref, o_ref):
        sz, _ = _silu_and_grad(z_ref[...])
        y2 = (y_ref[...] + d_ref[...] * x_ref[...]) * sz
        for k in range(_GN_GROUPS):
            sl = slice(k * _GN_W, (k + 1) * _GN_W)
            yk = y2[:, sl]
            rs = lax.rsqrt(jnp.mean(yk * yk, axis=-1, keepdims=True) + EPS)
            o_ref[:, sl] = (yk * rs * g_ref[:, sl]).astype(o_ref.dtype)

    return _rowcall(name, body, lp, tm, rows=[(yssd, D_SSM, 0), (xbc, D_SSM, 0), (proj, D_SSM, 0)],
                    vecs=[dskip, gnorm], outs=[(D_SSM, BF16)])[0]


def _ssm_gate_bwd(yssd, xbc, proj, dskip, gnorm, dyn, name):
    lp = yssd.shape[0]
    tm = BLOCK

    def body(r, first, y_ref, x_ref, z_ref, dyn_ref, d_ref, g_ref,
             dy_ref, dx_ref, dz_ref, dd_ref, dg_ref):
        z = z_ref[...]
        sz, dsz = _silu_and_grad(z)
        xs = x_ref[...]
        y1 = y_ref[...] + d_ref[...] * xs
        y2 = y1 * sz
        dyn = dyn_ref[...]
        for k in range(_GN_GROUPS):
            sl = slice(k * _GN_W, (k + 1) * _GN_W)
            dx, dgt = _rms_bwd(y2[:, sl], g_ref[:, sl], dyn[:, sl])
            dy1 = dx * sz[:, sl]
            dy_ref[:, sl] = dy1
            dx_ref[:, sl] = dy1 * d_ref[:, sl]
            dz_ref[:, sl] = (dx * y1[:, sl] * dsz[:, sl]).astype(dz_ref.dtype)

            @pl.when(first)
            def _():
                dd_ref[:, sl] = jnp.zeros((1, _GN_W), F32)
                dg_ref[:, sl] = jnp.zeros((1, _GN_W), F32)

            dd_ref[:, sl] += jnp.sum(dy1 * xs[:, sl], axis=0, keepdims=True)
            dg_ref[:, sl] += jnp.sum(dgt, axis=0, keepdims=True)

    return _rowcall(name, body, lp, tm,
                    rows=[(yssd, D_SSM, 0), (xbc, D_SSM, 0), (proj, D_SSM, 0), (dyn, D_SSM, 0)],
                    vecs=[dskip, gnorm], outs=[(D_SSM, F32), (D_SSM, F32), (D_SSM, BF16)],
                    accs=[((1, D_SSM), F32), ((1, D_SSM), F32)])


def _adamw(w, g, m, v, name):
    r, c = w.shape
    tm = r if r <= 512 else _pick(r, (512, 352, 256, 128, 64, 32, 16, 8))
    c1 = 1.0 / (1.0 - ADAM_B1 ** ADAM_STEP)
    c2 = 1.0 / (1.0 - ADAM_B2 ** ADAM_STEP)

    def body(w_ref, g_ref, m_ref, v_ref, d_ref, nm_ref, nv_ref):
        gv = g_ref[...]
        nm = ADAM_B1 * m_ref[...] + (1.0 - ADAM_B1) * gv
        nv = ADAM_B2 * v_ref[...] + (1.0 - ADAM_B2) * (gv * gv)
        nm_ref[...] = nm
        nv_ref[...] = nv
        d_ref[...] = -ADAM_LR * ((nm * c1) / (jnp.sqrt(nv * c2) + ADAM_EPS) + ADAM_WD * w_ref[...])

    spec = pl.BlockSpec((tm, c), lambda i: (i, 0))
    return pl.pallas_call(
        body, name=name, grid=(r // tm,), in_specs=[spec] * 4, out_specs=[spec] * 3,
        out_shape=[jax.ShapeDtypeStruct((r, c), F32)] * 3,
        compiler_params=pltpu.CompilerParams(dimension_semantics=("parallel",)),
    )(w, g, m, v)


def _place():
    return lax.axis_index("x"), lax.axis_index("y"), lax.axis_index("c")


def _other_chips(x, y):
    return [(1 - x, y), (x, 1 - y), (1 - x, 1 - y)]


_ANY = pl.BlockSpec(memory_space=pl.ANY)


class _Plan:
    def __init__(self, ins, out_shapes, n_remote, n_local, issue):
        self.ins = list(ins)
        self.out_shapes = list(out_shapes)
        self.issue = issue
        self.scratch = [pltpu.SemaphoreType.DMA((max(n_remote, 1),)),
                        pltpu.SemaphoreType.DMA((max(n_remote, 1),)),
                        pltpu.SemaphoreType.DMA((max(n_local, 1),))]

    def start(self, ins, outs, sems):
        sends, _, locs = self.issue(ins, outs, *sems)
        for cp in locs + sends:
            cp.start()

    def wait(self, ins, outs, sems):
        sends, recvs, locs = self.issue(ins, outs, *sems)
        for make in recvs:
            make().wait_recv()
        for cp in sends:
            cp.wait_send()
        for cp in locs:
            cp.wait()


def _plan_parts(plan):
    if plan is None:
        return [], [], [], []
    return ([_ANY] * len(plan.ins), plan.out_shapes, [_ANY] * len(plan.out_shapes), plan.scratch)


def _run_plan(plan, name):
    n_in, n_out = len(plan.ins), len(plan.out_shapes)

    def body(*refs):
        ins, outs, sems = refs[:n_in], refs[n_in:n_in + n_out], refs[n_in + n_out:]
        plan.start(ins, outs, sems)
        plan.wait(ins, outs, sems)

    return pl.pallas_call(
        body, name=name, in_specs=[_ANY] * n_in, out_specs=[_ANY] * n_out,
        out_shape=plan.out_shapes, scratch_shapes=plan.scratch,
    )(*plan.ins)


def _gather_plan(shards):
    n = len(shards)

    def issue(ins, outs, send_sems, recv_sems, local_sems):
        x, y, c = _place()
        me = 2 * x + y
        sends, recvs, locs = [], [], []
        for p in range(n):
            locs.append(pltpu.make_async_copy(ins[p], outs[p].at[me], local_sems.at[p]))
            for k, (px, py) in enumerate(_other_chips(x, y)):
                sems = dict(send_sem=send_sems.at[3 * p + k], recv_sem=recv_sems.at[3 * p + k],
                            device_id=(px, py, c), device_id_type=MESH)
                sends.append(pltpu.make_async_remote_copy(src_ref=ins[p], dst_ref=outs[p].at[me], **sems))
                recvs.append(functools.partial(pltpu.make_async_remote_copy, src_ref=ins[p],
                                               dst_ref=outs[p].at[2 * px + py], **sems))
        return sends, recvs, locs

    return _Plan(shards, [jax.ShapeDtypeStruct((N_CHIPS,) + s.shape, s.dtype) for s in shards], 3 * n, n, issue)


_REL7 = [(fx, fy, fc) for fx in (0, 1) for fy in (0, 1) for fc in (0, 1)][1:]


def _scatter8_plan(gs):
    n = len(gs)

    def issue(ins, outs, send_sems, recv_sems, local_sems):
        x, y, c = _place()
        sends = []
        for p in range(n):
            hr = gs[p].shape[1] // 2
            for k, (fx, fy, fc) in enumerate(_REL7):
                tx, ty, tc = x ^ fx, y ^ fy, c ^ fc
                src = ins[p].at[2 * tx + ty, pl.ds(pl.multiple_of(tc * hr, SUBLANES), hr), :]
                sends.append(pltpu.make_async_remote_copy(
                    src_ref=src, dst_ref=outs[p].at[k],
                    send_sem=send_sems.at[7 * p + k], recv_sem=recv_sems.at[7 * p + k],
                    device_id=(tx, ty, tc), device_id_type=MESH))
        return sends, [functools.partial(lambda cp: cp, cp) for cp in sends], []

    shapes = [jax.ShapeDtypeStruct((7, g.shape[1] // 2, g.shape[2]), g.dtype) for g in gs]
    return _Plan(gs, shapes, 7 * n, 0, issue)


def _sibling_plan(ts):
    n = len(ts)

    def issue(ins, outs, send_sems, recv_sems, local_sems):
        x, y, c = _place()
        sends = [pltpu.make_async_remote_copy(
            src_ref=ins[p], dst_ref=outs[p], send_sem=send_sems.at[p], recv_sem=recv_sems.at[p],
            device_id=(x, y, 1 - c), device_id_type=MESH) for p in range(n)]
        return sends, [functools.partial(lambda cp: cp, cp) for cp in sends], []

    return _Plan(ts, [jax.ShapeDtypeStruct(t.shape, t.dtype) for t in ts], n, 0, issue)


def _add8(g, recv, chip, core, name):
    s, r, n = g.shape
    hr = r // 2
    th = hr // 2 if (hr // 2) % SUBLANES == 0 else hr
    nt = hr // th

    def body(chip_ref, core_ref, g_ref, r_ref, o_ref):
        acc = g_ref[0]
        for k in range(7):
            acc = acc + r_ref[k]
        o_ref[...] = acc

    return pl.pallas_call(
        body, name=name,
        grid_spec=pltpu.PrefetchScalarGridSpec(
            num_scalar_prefetch=2, grid=(nt,),
            in_specs=[pl.BlockSpec((1, th, n), lambda i, ch, co: (ch[0], co[0] * nt + i, 0)),
                      pl.BlockSpec((7, th, n), lambda i, ch, co: (0, i, 0))],
            out_specs=pl.BlockSpec((th, n), lambda i, ch, co: (i, 0))),
        out_shape=jax.ShapeDtypeStruct((hr, n), F32),
        compiler_params=pltpu.CompilerParams(dimension_semantics=("parallel",)),
    )(chip, core, g, recv)


def _adamw_halves(w, own, other, m, v, core, name):
    r, n = w.shape
    hr = r // 2
    th = hr // 2 if (hr // 2) % SUBLANES == 0 else hr
    tph = hr // th
    c1 = 1.0 / (1.0 - ADAM_B1 ** ADAM_STEP)
    c2 = 1.0 / (1.0 - ADAM_B2 ** ADAM_STEP)

    def body(core_ref, w_ref, a_ref, b_ref, m_ref, v_ref, g_ref, d_ref, nm_ref, nv_ref):
        half = pl.program_id(0) // tph
        gv = jnp.where(half == core_ref[0], a_ref[...], b_ref[...])
        nm = ADAM_B1 * m_ref[...] + (1.0 - ADAM_B1) * gv
        nv = ADAM_B2 * v_ref[...] + (1.0 - ADAM_B2) * (gv * gv)
        g_ref[...] = gv
        nm_ref[...] = nm
        nv_ref[...] = nv
        d_ref[...] = -ADAM_LR * ((nm * c1) / (jnp.sqrt(nv * c2) + ADAM_EPS) + ADAM_WD * w_ref[...])

    full = pl.BlockSpec((th, n), lambda i, co: (i, 0))
    part = pl.BlockSpec((th, n), lambda i, co: (i % tph, 0))
    return pl.pallas_call(
        body, name=name,
        grid_spec=pltpu.PrefetchScalarGridSpec(
            num_scalar_prefetch=1, grid=(2 * tph,),
            in_specs=[full, part, part, full, full], out_specs=[full] * 4),
        out_shape=[jax.ShapeDtypeStruct((r, n), F32)] * 4,
        compiler_params=pltpu.CompilerParams(dimension_semantics=("parallel",)),
    )(core, w, own, other, m, v)


def _allreduce_small(pack, name):
    r, l = pack.shape

    def body(p_ref, o_ref, land, send_sems, recv_sems):
        x, y, c = _place()
        me = 4 * x + 2 * y + c
        land[me] = p_ref[...]
        rel = [(fx, fy, fc) for fx in (0, 1) for fy in (0, 1) for fc in (0, 1)][1:]
        sends = []
        for k, (fx, fy, fc) in enumerate(rel):
            peer = (x ^ fx, y ^ fy, c ^ fc)
            cp = pltpu.make_async_remote_copy(
                src_ref=p_ref, dst_ref=land.at[me], send_sem=send_sems.at[k], recv_sem=recv_sems.at[k],
                device_id=peer, device_id_type=MESH)
            cp.start()
            sends.append(cp)
        for k, (fx, fy, fc) in enumerate(rel):
            src = 4 * (x ^ fx) + 2 * (y ^ fy) + (c ^ fc)
            pltpu.make_async_remote_copy(
                src_ref=p_ref, dst_ref=land.at[src], send_sem=send_sems.at[k], recv_sem=recv_sems.at[k],
                device_id=(x ^ fx, y ^ fy, c ^ fc), device_id_type=MESH).wait_recv()
        for cp in sends:
            cp.wait_send()
        acc = land[0]
        for d in range(1, N_DEV):
            acc = acc + land[d]
        o_ref[...] = acc

    vm = pl.BlockSpec(memory_space=pltpu.VMEM)
    return pl.pallas_call(
        body, name=name, in_specs=[vm], out_specs=vm,
        out_shape=jax.ShapeDtypeStruct((r, l), F32),
        scratch_shapes=[pltpu.VMEM((N_DEV, r, l), F32),
                        pltpu.SemaphoreType.DMA((N_DEV - 1,)), pltpu.SemaphoreType.DMA((N_DEV - 1,))],
    )(pack)


def _flat_rows(a, mult=SUBLANES * LANES):
    f = a.reshape(-1)
    padn = (-f.shape[0]) % mult
    if padn:
        f = jnp.concatenate([f, jnp.zeros((padn,), f.dtype)])
    return f


def _pack(arrs, mult=SUBLANES * LANES, total_mult=None):
    flat = [_flat_rows(a, mult) for a in arrs]
    sizes = [f.shape[0] for f in flat]
    if total_mult is not None:
        padn = (-sum(sizes)) % total_mult
        if padn:
            flat.append(jnp.zeros((padn,), flat[0].dtype))
    return jnp.concatenate(flat).reshape(-1, LANES), sizes


def _unpack(pack, shapes, sizes, lead=()):
    flat = pack.reshape(lead + (-1,))
    out, off = [], 0
    for shp, sz in zip(shapes, sizes):
        n = math.prod(shp)
        out.append(flat[..., off:off + n].reshape(lead + tuple(shp)))
        off += sz
    return out


def _cols_from_shards(g):
    s, k, n = g.shape
    return jnp.transpose(g, (1, 0, 2)).reshape(k, s * n)


def _cols_to_shards(w, s=N_CHIPS):
    k, n = w.shape
    return jnp.transpose(w.reshape(k, s, n // s), (1, 0, 2))


def _ffn_fwd(h, pre, post, w_up, cw, cb, w_down, tag):
    u = _rmsnorm_fwd(h, pre, f"{tag}_prenorm")
    hp = _mm_nn_sh(u, w_up, 2 * D_FF, f"{tag}_up")
    hc = _conv_fwd(hp, 0, 2 * D_FF, cw, cb, f"{tag}_conv")
    act = _ffn_act_fwd(hc, f"{tag}_act")
    o = _mm_nn(act, w_down, F32, f"{tag}_down")
    hn = _postnorm_res_fwd(h, o, post, f"{tag}_postnorm")
    return hn, (h, u, hp, hc, act, o)


def _ffn_bwd(dh, saved, pre, post, w_up, cw, w_down, tag):
    h, u, hp, hc, act, o = saved
    do, dpost = _postnorm_bwd(o, post, dh, f"{tag}_postnorm_bwd")
    dact = _mm_nt(do, w_down, f"{tag}_down_dx")
    dw_down = _mm_tn(act, do, f"{tag}_down_dw")
    dhc = _ffn_act_bwd(hc, dact, f"{tag}_act_bwd")
    dhp, dcw, dcb = _conv_bwd(hp, 0, 2 * D_FF, dhc, cw, f"{tag}_conv_bwd")
    du = _mm_nt_sh(dhp, w_up, f"{tag}_up_dx")
    dw_up = _mm_tn_sh(u, dhp, w_up.shape[2], f"{tag}_up_dw")
    dhn, dpre = _prenorm_bwd(h, pre, du, dh, f"{tag}_prenorm_bwd")
    return dhn, dict(pre=dpre, post=dpost, w_up=dw_up, conv_w=dcw[:3], conv_b=dcb, w_down=dw_down)


class _Exchange:
    AFTER_L1_FFN = ("l1_ffn_w_up", "l1_ffn_w_down")
    AFTER_L0_OUT = ("l1_w_out", "l1_w_in", "l0_ffn_w_up", "l0_ffn_w_down", "l0_w_out")
    LAST = ("l0_w_in",)

    def __init__(self, late_shards):
        self.late = dict(late_shards)
        self.slabs = {}
        self.recv = {}

    def gather_plan(self):
        return _gather_plan(list(self.late.values()))

    def gathered(self, outs):
        return {n: (g if n in _BIG_COL else g.reshape(-1, g.shape[-1])) for n, g in zip(self.late, outs)}

    def scatter_plan(self, grads, names):
        for n in names:
            g = grads[n]
            self.slabs[n] = g if n in _BIG_COL else g.reshape(N_CHIPS, -1, g.shape[-1])
        return _scatter8_plan([self.slabs[n] for n in names])

    def scattered(self, names, outs):
        self.recv.update(zip(names, outs))


def _local_step(x, tgt, meta, P, ex=None):
    seq, d = x.shape
    lp = seq + BLOCK
    h0 = jnp.concatenate([jnp.zeros((PAD, d), F32), meta, x], axis=0)
    tgt_p = jnp.concatenate([jnp.zeros((BLOCK, d), F32), tgt], axis=0)

    u0 = _rmsnorm_fwd(h0, P["l0_mix_pre_norm"], "l0_mix_prenorm")
    proj0 = _mm_nn_sh(u0, P["l0_w_in"], EVEN_IN, "l0_in")
    xrc = _conv_fwd(proj0, D_MODEL, D_MODEL, P["l0_lru_conv_w"], P["l0_lru_conv_b"], "l0_lru_conv")
    lru_args = (P["l0_lru_w_a"], P["l0_lru_w_x"], P["l0_lru_b_a"], P["l0_lru_b_x"], P["l0_lru_lambda"])
    ya, hl = _lru_fwd(proj0, xrc, *lru_args, "l0_lru")
    yb, outs = _attn_fwd(proj0, P["l0_attn_sinks"], "l0_attn", ex.gather_plan() if ex else None)
    if ex:
        P = {**P, **ex.gathered(outs)}
    mix0 = jnp.concatenate([ya, yb], axis=1)
    o0 = _mm_nn(mix0, P["l0_w_out"], F32, "l0_out")
    h1 = _postnorm_res_fwd(h0, o0, P["l0_mix_post_norm"], "l0_mix_postnorm")
    h2, ffn0 = _ffn_fwd(h1, P["l0_ffn_pre_norm"], P["l0_ffn_post_norm"], P["l0_ffn_w_up"],
                        P["l0_ffn_conv_w"], P["l0_ffn_conv_b"], P["l0_ffn_w_down"], "l0_ffn")
    u2 = _rmsnorm_fwd(h2, P["l1_mix_pre_norm"], "l1_mix_prenorm")
    proj1 = _mm_nn_sh(u2, P["l1_w_in"], ODD_IN_PAD, "l1_in")
    xc1 = _conv_fwd(proj1, _ZW, _XBC_W, P["l1_ssm_conv_w"], P["l1_ssm_conv_b"], "l1_ssm_conv")
    xbc, dt = _ssm_prep_fwd(xc1, proj1, P["l1_dt_bias"], "l1_ssm_prep")
    yssd, states = _ssd_fwd(xbc, dt, P["l1_a_log"], "l1_ssd")
    yn = _ssm_gate_fwd(yssd, xbc, proj1, P["l1_d_skip"], P["l1_gate_norm"], "l1_ssm_gate")
    o1 = _mm_nn(yn, P["l1_w_out"], F32, "l1_out")
    h3 = _postnorm_res_fwd(h2, o1, P["l1_mix_post_norm"], "l1_mix_postnorm")
    h4, ffn1 = _ffn_fwd(h3, P["l1_ffn_pre_norm"], P["l1_ffn_post_norm"], P["l1_ffn_w_up"],
                        P["l1_ffn_conv_w"], P["l1_ffn_conv_b"], P["l1_ffn_w_down"], "l1_ffn")
    dh4, loss_cols = _loss_fwd_bwd(h4, tgt_p, "loss")

    G = {}
    dh3, g = _ffn_bwd(dh4, ffn1, P["l1_ffn_pre_norm"], P["l1_ffn_post_norm"], P["l1_ffn_w_up"],
                      P["l1_ffn_conv_w"], P["l1_ffn_w_down"], "l1_ffn")
    for k, v in g.items():
        G["l1_ffn_" + (k + "_norm" if k in ("pre", "post") else k)] = v
    do1, G["l1_mix_post_norm"] = _postnorm_bwd(o1, P["l1_mix_post_norm"], dh3, "l1_mix_postnorm_bwd")
    dyn = _mm_nt(do1, P["l1_w_out"], "l1_out_dx")
    G["l1_w_out"] = _mm_tn(yn, do1, "l1_out_dw")
    dyssd, dxskip, dz, dd_cols, G["l1_gate_norm"] = _ssm_gate_bwd(
        yssd, xbc, proj1, P["l1_d_skip"], P["l1_gate_norm"], dyn, "l1_ssm_gate_bwd")
    G["l1_d_skip"] = dd_cols.reshape(SSD_HEADS, SSD_P).sum(axis=1)
    (dxs, dbm, dcm, ddt, dalog), outs = _ssd_bwd(
        xbc, dt, P["l1_a_log"], states, dyssd, "l1_ssd_bwd",
        ex.scatter_plan(G, ex.AFTER_L1_FFN) if ex else None)
    if ex:
        ex.scattered(ex.AFTER_L1_FFN, outs)
    G["l1_a_log"] = dalog[0, :SSD_HEADS]
    dxc, ddtr, dbias = _ssm_prep_bwd(xc1, proj1, P["l1_dt_bias"], dxs, dxskip, dbm, dcm, ddt,
                                     "l1_ssm_prep_bwd")
    G["l1_dt_bias"] = dbias[0, :SSD_HEADS]
    dxbc, dcw, dcb = _conv_bwd(proj1, _ZW, _XBC_W, dxc, P["l1_ssm_conv_w"], "l1_ssm_conv_bwd")
    G["l1_ssm_conv_w"] = dcw[:4]
    G["l1_ssm_conv_b"] = dcb
    dproj1 = jnp.concatenate([dz, dxbc, ddtr], axis=1)
    du2 = _mm_nt_sh(dproj1, P["l1_w_in"], "l1_in_dx")
    G["l1_w_in"] = _mm_tn_sh(u2, dproj1, ODD_IN // N_CHIPS, "l1_in_dw")
    dh2, G["l1_mix_pre_norm"] = _prenorm_bwd(h2, P["l1_mix_pre_norm"], du2, dh3, "l1_mix_prenorm_bwd")
    dh1, g = _ffn_bwd(dh2, ffn0, P["l0_ffn_pre_norm"], P["l0_ffn_post_norm"], P["l0_ffn_w_up"],
                      P["l0_ffn_conv_w"], P["l0_ffn_w_down"], "l0_ffn")
    for k, v in g.items():
        G["l0_ffn_" + (k + "_norm" if k in ("pre", "post") else k)] = v
    do0, G["l0_mix_post_norm"] = _postnorm_bwd(o0, P["l0_mix_post_norm"], dh1, "l0_mix_postnorm_bwd")
    dmix = _mm_nt(do0, P["l0_w_out"], "l0_out_dx")
    G["l0_w_out"] = _mm_tn(mix0, do0, "l0_out_dw")
    (dgate, dxrc, G["l0_lru_w_a"], G["l0_lru_w_x"], G["l0_lru_b_a"], G["l0_lru_b_x"],
     G["l0_lru_lambda"]) = _lru_bwd(proj0, xrc, hl, dmix, *lru_args, "l0_lru_bwd")
    dxr, dcw, dcb = _conv_bwd(proj0, D_MODEL, D_MODEL, dxrc, P["l0_lru_conv_w"], "l0_lru_conv_bwd")
    G["l0_lru_conv_w"] = dcw[:4]
    G["l0_lru_conv_b"] = dcb
    (dq, dk, dv, G["l0_attn_sinks"]), outs = _attn_bwd(
        proj0, P["l0_attn_sinks"], dmix, "l0_attn_bwd",
        ex.scatter_plan(G, ex.AFTER_L0_OUT) if ex else None)
    if ex:
        ex.scattered(ex.AFTER_L0_OUT, outs)
    dproj0 = jnp.concatenate([dgate, dxr, dq, dk.astype(BF16), dv.astype(BF16)], axis=1)
    du0 = _mm_nt_sh(dproj0, P["l0_w_in"], "l0_in_dx")
    G["l0_w_in"] = _mm_tn_sh(u0, dproj0, EVEN_IN // N_CHIPS, "l0_in_dw")
    dh0, G["l0_mix_pre_norm"] = _prenorm_bwd(h0, P["l0_mix_pre_norm"], du0, dh1, "l0_mix_prenorm_bwd")
    return loss_cols, dh0[BLOCK:], dh0[PAD:BLOCK], G


_BIG_COL = ("l0_w_in", "l0_ffn_w_up", "l1_w_in", "l1_ffn_w_up")
_BIG_ROW = ("l0_w_out", "l0_ffn_w_down", "l1_w_out", "l1_ffn_w_down")
_BIG = ("l0_w_in", "l0_w_out", "l0_ffn_w_up", "l0_ffn_w_down",
        "l1_w_in", "l1_w_out", "l1_ffn_w_up", "l1_ffn_w_down")
_SMALL_SHARDED = ("meta_tokens", "l0_lru_conv_w", "l0_ffn_conv_w", "l1_ssm_conv_w", "l1_ffn_conv_w")
_WEIGHTS = ("meta_tokens", "l0_mix_pre_norm", "l0_mix_post_norm", "l0_w_in", "l0_lru_conv_w",
            "l0_lru_conv_b", "l0_lru_w_a", "l0_lru_b_a", "l0_lru_w_x", "l0_lru_b_x", "l0_lru_lambda",
            "l0_attn_sinks", "l0_w_out", "l0_ffn_pre_norm", "l0_ffn_post_norm", "l0_ffn_w_up",
            "l0_ffn_conv_w", "l0_ffn_conv_b", "l0_ffn_w_down", "l1_mix_pre_norm", "l1_mix_post_norm",
            "l1_w_in", "l1_ssm_conv_w", "l1_ssm_conv_b", "l1_dt_bias", "l1_a_log", "l1_d_skip",
            "l1_gate_norm", "l1_w_out", "l1_ffn_pre_norm", "l1_ffn_post_norm", "l1_ffn_w_up",
            "l1_ffn_conv_w", "l1_ffn_conv_b", "l1_ffn_w_down")
_REPL = tuple(n for n in _WEIGHTS if n not in _BIG and n not in _SMALL_SHARDED)


def _pad_lanes(v, n=LANES):
    return jnp.concatenate([v, jnp.zeros((n - v.shape[0],), v.dtype)]).reshape(1, n)


def _step(x, tgt, W, M, V):
    cx, cy, cc = _place()
    chip = 2 * cx + cy

    small_pack, small_sizes = _pack([W[n] for n in _SMALL_SHARDED])
    first = _run_plan(_gather_plan([W["l0_w_in"].astype(BF16), small_pack]), "gather_first")
    small_full = _unpack(first[1], [W[n].shape for n in _SMALL_SHARDED], small_sizes, lead=(N_CHIPS,))
    ex = _Exchange({n: W[n].astype(BF16) for n in _BIG if n != "l0_w_in"})

    P = {"l0_w_in": first[0]}
    for n, g in zip(_SMALL_SHARDED, small_full):
        P[n] = _cols_from_shards(g)
    for n in _REPL:
        v = W[n]
        P[n] = v.reshape(1, -1) if v.ndim == 1 else v
    P["l0_lru_w_a"] = W["l0_lru_w_a"].astype(BF16)
    P["l0_lru_w_x"] = W["l0_lru_w_x"].astype(BF16)
    P["l1_dt_bias"] = _pad_lanes(W["l1_dt_bias"])
    P["l1_a_log"] = _pad_lanes(W["l1_a_log"])
    P["l1_d_skip"] = jnp.repeat(W["l1_d_skip"], SSD_P).reshape(1, D_SSM)
    meta = P.pop("meta_tokens")

    loss_cols, grad_x, grad_meta, G = _local_step(x, tgt, meta, P, ex)
    G["meta_tokens"] = grad_meta

    ex.scattered(ex.LAST, _run_plan(ex.scatter_plan(G, ex.LAST), "grad_scatter_last"))
    core_idx = cc.astype(jnp.int32).reshape(1)
    chip_idx = chip.astype(jnp.int32).reshape(1)
    own_half = [_add8(ex.slabs[n], ex.recv[n], chip_idx, core_idx, f"grad_sum_{n}") for n in _BIG]
    other_half = _run_plan(_sibling_plan(own_half), "grad_sibling_swap")
    small_names = list(_REPL) + list(_SMALL_SHARDED)
    small_list = [G[n] for n in small_names] + [loss_cols]
    spack, ssizes = _pack(small_list)
    sred = _allreduce_small(spack, "small_allreduce")
    sfull = _unpack(sred, [a.shape for a in small_list], ssizes)
    loss = 0.5 / D_MODEL * jnp.sum(sfull[-1])
    small_grads = {}
    for n, g in zip(small_names, sfull[:-1]):
        if n in _SMALL_SHARDED:
            wcols = W[n].shape[1]
            g = lax.dynamic_slice_in_dim(g, chip * wcols, wcols, axis=1)
        small_grads[n] = g.reshape(W[n].shape)

    grads, delta, new_m, new_v = {}, {}, {}, {}
    for n, own, other in zip(_BIG, own_half, other_half):
        grads[n], delta[n], new_m[n], new_v[n] = _adamw_halves(
            W[n], own, other, M[n], V[n], core_idx, f"adamw_{n}")
    s_names = [n for n in _WEIGHTS if n not in _BIG]
    tile_elems = 512 * LANES
    wp, wsz = _pack([W[n] for n in s_names], total_mult=tile_elems)
    gp, _ = _pack([small_grads[n] for n in s_names], total_mult=tile_elems)
    mp, _ = _pack([M[n] for n in s_names], total_mult=tile_elems)
    vp, _ = _pack([V[n] for n in s_names], total_mult=tile_elems)
    dp, nmp, nvp = _adamw(wp, gp, mp, vp, "adamw_small")
    shapes = [W[n].shape for n in s_names]
    for n, a, b, c_ in zip(s_names, _unpack(dp, shapes, wsz), _unpack(nmp, shapes, wsz),
                           _unpack(nvp, shapes, wsz)):
        grads[n] = small_grads[n]
        delta[n], new_m[n], new_v[n] = a, b, c_
    return loss, grad_x, grads, delta, new_m, new_v


def kernel(x, meta_tokens, l0_mix_pre_norm, l0_mix_post_norm, l0_w_in, l0_lru_conv_w, l0_lru_conv_b, l0_lru_w_a, l0_lru_b_a, l0_lru_w_x, l0_lru_b_x, l0_lru_lambda, l0_attn_sinks, l0_w_out, l0_ffn_pre_norm, l0_ffn_post_norm, l0_ffn_w_up, l0_ffn_conv_w, l0_ffn_conv_b, l0_ffn_w_down, l1_mix_pre_norm, l1_mix_post_norm, l1_w_in, l1_ssm_conv_w, l1_ssm_conv_b, l1_dt_bias, l1_a_log, l1_d_skip, l1_gate_norm, l1_w_out, l1_ffn_pre_norm, l1_ffn_post_norm, l1_ffn_w_up, l1_ffn_conv_w, l1_ffn_conv_b, l1_ffn_w_down, loss_target, m_meta_tokens, m_l0_mix_pre_norm, m_l0_mix_post_norm, m_l0_w_in, m_l0_lru_conv_w, m_l0_lru_conv_b, m_l0_lru_w_a, m_l0_lru_b_a, m_l0_lru_w_x, m_l0_lru_b_x, m_l0_lru_lambda, m_l0_attn_sinks, m_l0_w_out, m_l0_ffn_pre_norm, m_l0_ffn_post_norm, m_l0_ffn_w_up, m_l0_ffn_conv_w, m_l0_ffn_conv_b, m_l0_ffn_w_down, m_l1_mix_pre_norm, m_l1_mix_post_norm, m_l1_w_in, m_l1_ssm_conv_w, m_l1_ssm_conv_b, m_l1_dt_bias, m_l1_a_log, m_l1_d_skip, m_l1_gate_norm, m_l1_w_out, m_l1_ffn_pre_norm, m_l1_ffn_post_norm, m_l1_ffn_w_up, m_l1_ffn_conv_w, m_l1_ffn_conv_b, m_l1_ffn_w_down, v_meta_tokens, v_l0_mix_pre_norm, v_l0_mix_post_norm, v_l0_w_in, v_l0_lru_conv_w, v_l0_lru_conv_b, v_l0_lru_w_a, v_l0_lru_b_a, v_l0_lru_w_x, v_l0_lru_b_x, v_l0_lru_lambda, v_l0_attn_sinks, v_l0_w_out, v_l0_ffn_pre_norm, v_l0_ffn_post_norm, v_l0_ffn_w_up, v_l0_ffn_conv_w, v_l0_ffn_conv_b, v_l0_ffn_w_down, v_l1_mix_pre_norm, v_l1_mix_post_norm, v_l1_w_in, v_l1_ssm_conv_w, v_l1_ssm_conv_b, v_l1_dt_bias, v_l1_a_log, v_l1_d_skip, v_l1_gate_norm, v_l1_w_out, v_l1_ffn_pre_norm, v_l1_ffn_post_norm, v_l1_ffn_w_up, v_l1_ffn_conv_w, v_l1_ffn_conv_b, v_l1_ffn_w_down):
    args = locals()
    W = {n: args[n] for n in _WEIGHTS}
    M = {n: args["m_" + n] for n in _WEIGHTS}
    V = {n: args["v_" + n] for n in _WEIGHTS}
    loss, grad_x, grads, delta, new_m, new_v = _step(x[0], loss_target[0], W, M, V)
    return (loss, grad_x[None], *[grads[n] for n in _WEIGHTS], *[delta[n] for n in _WEIGHTS],
            *[new_m[n] for n in _WEIGHTS], *[new_v[n] for n in _WEIGHTS])
```

```python
import functools
import math

import jax
import jax.numpy as jnp
from jax import lax
from jax.experimental import pallas as pl
from jax.experimental.pallas import tpu as pltpu

F32 = jnp.float32
BF16 = jnp.bfloat16

D_MODEL = 1024
N_META = 16
BLOCK = 128
PAD = BLOCK - N_META
EPS = 1e-6
LRU_BLOCKS = 8
LRU_BS = 128
LRU_C = 8.0
N_Q_HEADS = 16
N_KV_HEADS = 2
HEAD_DIM = 64
Q_PER_KV = 8
WINDOW = 128
D_SSM = 2048
SSD_HEADS = 32
SSD_GROUPS = 8
SSD_HPG = 4
SSD_P = 64
SSD_N = 128
D_FF = 2816
NEG = -1e30
LANES = 128
SUBLANES = 8

ADAM_LR = 0.001
ADAM_B1 = 0.9
ADAM_B2 = 0.999
ADAM_EPS = 1e-08
ADAM_WD = 0.01
ADAM_STEP = 10

MESH = pl.DeviceIdType.MESH
N_CHIPS = 4
N_DEV = 8


def _pick(n, cands):
    for c in cands:
        if n % c == 0:
            return c
    raise ValueError(f"no tile for {n} in {cands}")


def _col_tile(n, limit=1792):
    best = None
    for t in range(LANES, min(n, limit) + 1, LANES):
        if n % t == 0:
            best = t
    if best is None:
        raise ValueError(f"no lane tile for {n}")
    return best


def _sigmoid(x):
    return 1.0 / (1.0 + jnp.exp(-x))


def _log1p(e):
    u = 1.0 + e
    return jnp.where(u == 1.0, e, jnp.log(u) * (e / jnp.where(u == 1.0, 1.0, u - 1.0)))


def _softplus(x):
    return jnp.maximum(x, 0.0) + _log1p(jnp.exp(-jnp.abs(x)))


def _neg_expm1(x):
    poly = x * (1.0 + x * (0.5 + x * (1.0 / 6.0 + x * (1.0 / 24.0 + x * (1.0 / 120.0)))))
    return -jnp.where(x > -0.05, poly, jnp.exp(x) - 1.0)


_GELU_C = math.sqrt(2.0 / math.pi)


def _gelu(x):
    t = jnp.tanh(_GELU_C * (x + 0.044715 * x * x * x))
    return 0.5 * x * (1.0 + t)


def _gelu_and_grad(x):
    x2 = x * x
    t = jnp.tanh(_GELU_C * (x + 0.044715 * x * x2))
    g = 0.5 * x * (1.0 + t)
    dg = 0.5 * (1.0 + t) + 0.5 * x * (1.0 - t * t) * _GELU_C * (1.0 + 3.0 * 0.044715 * x2)
    return g, dg


def _silu_and_grad(x):
    s = _sigmoid(x)
    return x * s, s * (1.0 + x * (1.0 - s))


def _dot(a, b):
    return jnp.dot(a, b, preferred_element_type=F32)


def _dot_nt(a, b):
    return lax.dot_general(a, b, (((1,), (1,)), ((), ())), preferred_element_type=F32)


def _dot_tn(a, b):
    return lax.dot_general(a, b, (((0,), (0,)), ((), ())), preferred_element_type=F32)


def _row_iota(t):
    return lax.broadcasted_iota(jnp.int32, (t, 1), 0)


def _scan_fwd(a, u, t):
    row = _row_iota(t)
    d = 1
    while d < t:
        m = row >= d
        u_sh = jnp.where(m, pltpu.roll(u, d, 0), 0.0)
        a_sh = jnp.where(m, pltpu.roll(a, d, 0), 1.0)
        u = u + a * u_sh
        a = a * a_sh
        d *= 2
    return a, u


def _scan_rev(c, x, t):
    row = _row_iota(t)
    d = 1
    while d < t:
        m = row < t - d
        x_sh = jnp.where(m, pltpu.roll(x, t - d, 0), 0.0)
        c_sh = jnp.where(m, pltpu.roll(c, t - d, 0), 1.0)
        x = x + c * x_sh
        c = c * c_sh
        d *= 2
    return c, x


def _cumsum_rows(x, t):
    row = _row_iota(t)
    d = 1
    while d < t:
        x = x + jnp.where(row >= d, pltpu.roll(x, d, 0), 0.0)
        d *= 2
    return x


def _rev_cumsum_rows(x, t):
    row = _row_iota(t)
    d = 1
    while d < t:
        x = x + jnp.where(row < t - d, pltpu.roll(x, t - d, 0), 0.0)
        d *= 2
    return x


def _rms_bwd(x, g, dy):
    rs = lax.rsqrt(jnp.mean(x * x, axis=-1, keepdims=True) + EPS)
    gy = dy * g
    dx = rs * gy - x * (rs * rs * rs) * jnp.mean(x * gy, axis=-1, keepdims=True)
    return dx, dy * x * rs


def _mm_nn(a, w, out_dtype, name):
    m, k = a.shape
    n = w.shape[1]
    tm = _pick(m, (640, 512, 256, 128))
    tn = _col_tile(n)

    def body(a_ref, w_ref, o_ref):
        o_ref[...] = _dot(a_ref[...].astype(BF16), w_ref[...]).astype(o_ref.dtype)

    return pl.pallas_call(
        body, name=name, grid=(n // tn, m // tm),
        in_specs=[pl.BlockSpec((tm, k), lambda j, i: (i, 0)),
                  pl.BlockSpec((k, tn), lambda j, i: (0, j))],
        out_specs=pl.BlockSpec((tm, tn), lambda j, i: (i, j)),
        out_shape=jax.ShapeDtypeStruct((m, n), out_dtype),
        compiler_params=pltpu.CompilerParams(dimension_semantics=("parallel", "parallel")),
    )(a, w)


def _mm_nt(dy, w, name):
    m, n = dy.shape
    k = w.shape[0]
    wide = n > 3328
    tm = _pick(m, (320, 256, 128)) if wide else _pick(m, (640, 512, 256, 128))
    tk = _col_tile(k, 512 if wide else 1408)

    def body(dy_ref, w_ref, o_ref):
        o_ref[...] = _dot_nt(dy_ref[...].astype(BF16), w_ref[...])

    return pl.pallas_call(
        body, name=name, grid=(k // tk, m // tm),
        in_specs=[pl.BlockSpec((tm, n), lambda j, i: (i, 0)),
                  pl.BlockSpec((tk, n), lambda j, i: (j, 0))],
        out_specs=pl.BlockSpec((tm, tk), lambda j, i: (i, j)),
        out_shape=jax.ShapeDtypeStruct((m, k), F32),
        compiler_params=pltpu.CompilerParams(dimension_semantics=("parallel", "parallel")),
    )(dy, w)


def _mm_tn(a, dy, name):
    m, k = a.shape
    n = dy.shape[1]
    tm = _pick(m, (640, 512, 256, 128))
    tk = _col_tile(k, 1408)
    tn = _col_tile(n, 1664)

    def body(a_ref, dy_ref, o_ref):
        @pl.when(pl.program_id(2) == 0)
        def _():
            o_ref[...] = jnp.zeros_like(o_ref)

        o_ref[...] += _dot_tn(a_ref[...].astype(BF16), dy_ref[...].astype(BF16))

    return pl.pallas_call(
        body, name=name, grid=(k // tk, n // tn, m // tm),
        in_specs=[pl.BlockSpec((tm, tk), lambda kk, j, i: (i, kk)),
                  pl.BlockSpec((tm, tn), lambda kk, j, i: (i, j))],
        out_specs=pl.BlockSpec((tk, tn), lambda kk, j, i: (kk, j)),
        out_shape=jax.ShapeDtypeStruct((k, n), F32),
        compiler_params=pltpu.CompilerParams(
            dimension_semantics=("parallel", "parallel", "arbitrary")),
    )(a, dy)


def _mm_nn_sh(a, w4, n_out, name):
    m, k = a.shape
    s, _, n = w4.shape
    tm = _pick(m, (320, 256, 128))

    def body(a_ref, w_ref, o_ref):
        av = a_ref[...].astype(BF16)
        for j in range(s):
            o_ref[:, j * n:(j + 1) * n] = _dot(av, w_ref[j])
        if n_out > s * n:
            o_ref[:, s * n:] = jnp.zeros((tm, n_out - s * n), F32)

    return pl.pallas_call(
        body, name=name, grid=(m // tm,),
        in_specs=[pl.BlockSpec((tm, k), lambda i: (i, 0)),
                  pl.BlockSpec((s, k, n), lambda i: (0, 0, 0))],
        out_specs=pl.BlockSpec((tm, n_out), lambda i: (i, 0)),
        out_shape=jax.ShapeDtypeStruct((m, n_out), F32),
        compiler_params=pltpu.CompilerParams(dimension_semantics=("parallel",)),
    )(a, w4)


def _mm_nt_sh(dy, w4, name):
    m, nn = dy.shape
    s, k, n = w4.shape
    tm = _pick(m, (640, 512, 256, 128))
    tk = _col_tile(k, 256)

    def body(dy_ref, w_ref, o_ref):
        acc = _dot_nt(dy_ref[:, 0:n].astype(BF16), w_ref[0])
        for j in range(1, s):
            acc = acc + _dot_nt(dy_ref[:, j * n:(j + 1) * n].astype(BF16), w_ref[j])
        o_ref[...] = acc

    return pl.pallas_call(
        body, name=name, grid=(k // tk, m // tm),
        in_specs=[pl.BlockSpec((tm, nn), lambda j, i: (i, 0)),
                  pl.BlockSpec((s, tk, n), lambda j, i: (0, j, 0))],
        out_specs=pl.BlockSpec((tm, tk), lambda j, i: (i, j)),
        out_shape=jax.ShapeDtypeStruct((m, k), F32),
        compiler_params=pltpu.CompilerParams(dimension_semantics=("parallel", "parallel")),
    )(dy, w4)


def _mm_tn_sh(a, dy, n, name):
    m, k = a.shape
    nn = dy.shape[1]
    s = N_CHIPS
    tm = _pick(m, (320, 256, 128)) if 1024 < n <= 1408 else _pick(m, (640, 512, 256, 128))
    tk = _col_tile(k, 512 if n <= 1408 else 256)

    def body(a_ref, dy_ref, o_ref):
        @pl.when(pl.program_id(1) == 0)
        def _():
            o_ref[...] = jnp.zeros_like(o_ref)

        av = a_ref[...].astype(BF16)
        for j in range(s):
            o_ref[j] += _dot_tn(av, dy_ref[:, j * n:(j + 1) * n].astype(BF16))

    return pl.pallas_call(
        body, name=name, grid=(k // tk, m // tm),
        in_specs=[pl.BlockSpec((tm, tk), lambda kk, i: (i, kk)),
                  pl.BlockSpec((tm, nn), lambda kk, i: (i, 0))],
        out_specs=pl.BlockSpec((s, tk, n), lambda kk, i: (0, kk, 0)),
        out_shape=jax.ShapeDtypeStruct((s, k, n), F32),
        compiler_params=pltpu.CompilerParams(dimension_semantics=("parallel", "arbitrary")),
    )(a, dy)


def _rowcall(name, body, lp, tm, rows=(), prevs=(), vecs=(), outs=(), accs=(), scratch=(),
             reverse=False, seq=False):
    nt = lp // tm
    hb = tm // SUBLANES

    def ri(i):
        return nt - 1 - i if reverse else i

    in_specs, args = [], []
    for arr, w, cb in rows:
        in_specs.append(pl.BlockSpec((tm, w), lambda i, cb=cb: (ri(i), cb)))
        args.append(arr)
    for arr, w, cb in prevs:
        in_specs.append(pl.BlockSpec((SUBLANES, w), lambda i, cb=cb: (jnp.maximum(ri(i) * hb - 1, 0), cb)))
        args.append(arr)
    for arr in vecs:
        in_specs.append(pl.BlockSpec(arr.shape, lambda i, nd=arr.ndim: (0,) * nd))
        args.append(arr)
    out_shape, out_specs = [], []
    for w, dt in outs:
        out_shape.append(jax.ShapeDtypeStruct((lp, w), dt))
        out_specs.append(pl.BlockSpec((tm, w), lambda i: (ri(i), 0)))
    for shp, dt in accs:
        out_shape.append(jax.ShapeDtypeStruct(shp, dt))
        out_specs.append(pl.BlockSpec(shp, lambda i, nd=len(shp): (0,) * nd))

    def kern(*refs):
        i = pl.program_id(0)
        body(ri(i), i == 0, *refs)

    sem = ("arbitrary",) if (seq or accs) else ("parallel",)
    res = pl.pallas_call(
        kern, name=name, grid=(nt,), in_specs=in_specs, out_specs=out_specs,
        out_shape=out_shape, scratch_shapes=list(scratch),
        compiler_params=pltpu.CompilerParams(dimension_semantics=sem),
    )(*args)
    return res


def _acc_add(first, ref, val):
    @pl.when(first)
    def _():
        ref[...] = jnp.zeros_like(ref)

    ref[...] += val


def _real_rows(r, tm):
    return (r * tm + _row_iota(tm)) >= PAD


def _rmsnorm_fwd(h, g, name):
    lp, d = h.shape
    tm = _pick(lp, (640, 512, 256, 128))

    def body(r, first, h_ref, g_ref, u_ref):
        x = h_ref[...]
        rs = lax.rsqrt(jnp.mean(x * x, axis=-1, keepdims=True) + EPS)
        u_ref[...] = (x * rs * g_ref[...]).astype(u_ref.dtype)

    return _rowcall(name, body, lp, tm, rows=[(h, d, 0)], vecs=[g], outs=[(d, BF16)])[0]


def _postnorm_res_fwd(h, o, g, name):
    lp, d = h.shape
    tm = _pick(lp, (640, 512, 256, 128))

    def body(r, first, h_ref, o_ref, g_ref, out_ref):
        x = o_ref[...]
        rs = lax.rsqrt(jnp.mean(x * x, axis=-1, keepdims=True) + EPS)
        out_ref[...] = jnp.where(_real_rows(r, tm), h_ref[...] + x * rs * g_ref[...], 0.0)

    return _rowcall(name, body, lp, tm, rows=[(h, d, 0), (o, d, 0)], vecs=[g], outs=[(d, F32)])[0]


def _postnorm_bwd(o, g, dh, name):
    lp, d = o.shape
    tm = _pick(lp, (640, 512, 256, 128))

    def body(r, first, o_ref, dh_ref, g_ref, do_ref, dg_ref):
        dx, dgt = _rms_bwd(o_ref[...], g_ref[...], dh_ref[...])
        do_ref[...] = dx.astype(do_ref.dtype)
        _acc_add(first, dg_ref, jnp.sum(dgt, axis=0, keepdims=True))

    return _rowcall(name, body, lp, tm, rows=[(o, d, 0), (dh, d, 0)], vecs=[g],
                    outs=[(d, BF16)], accs=[((1, d), F32)])


def _prenorm_bwd(h, g, du, dh_res, name):
    lp, d = h.shape
    tm = _pick(lp, (640, 512, 256, 128))

    def body(r, first, h_ref, du_ref, dres_ref, g_ref, dh_ref, dg_ref):
        dx, dgt = _rms_bwd(h_ref[...], g_ref[...], du_ref[...])
        dh_ref[...] = jnp.where(_real_rows(r, tm), dres_ref[...] + dx, 0.0)
        _acc_add(first, dg_ref, jnp.sum(dgt, axis=0, keepdims=True))

    return _rowcall(name, body, lp, tm, rows=[(h, d, 0), (du, d, 0), (dh_res, d, 0)], vecs=[g],
                    outs=[(d, F32)], accs=[((1, d), F32)])


def _loss_fwd_bwd(h, tgt, name):
    lp, d = h.shape
    tm = _pick(lp, (640, 512, 256, 128))

    def body(r, first, h_ref, t_ref, dh_ref, ls_ref):
        tok = (r * tm + _row_iota(tm)) >= BLOCK
        e = jnp.where(tok, h_ref[...] - t_ref[...], 0.0)
        dh_ref[...] = e * (1.0 / d)
        _acc_add(first, ls_ref, jnp.sum(e * e, axis=0, keepdims=True))

    return _rowcall(name, body, lp, tm, rows=[(h, d, 0), (tgt, d, 0)],
                    outs=[(d, F32)], accs=[((1, d), F32)])


def _conv_tiles(lp, width):
    wc = _col_tile(width, 1408)
    tm = _pick(lp, (320, 256, 128))
    return tm, wc


def _conv_fwd(x, col_off, width, w, b, name):
    lp = x.shape[0]
    kk = w.shape[0]
    tm, wc = _conv_tiles(lp, width)
    offb = col_off // wc
    assert col_off % wc == 0
    hb = tm // SUBLANES

    def body(x_ref, xp_ref, w_ref, b_ref, y_ref):
        i = pl.program_id(1)
        xv = x_ref[...]
        halo = jnp.where(i > 0, xp_ref[...], 0.0)
        xx = jnp.concatenate([halo, xv], axis=0)
        acc = b_ref[...] + w_ref[kk - 1:kk, :] * xv
        for j in range(1, kk):
            acc = acc + w_ref[kk - 1 - j:kk - j, :] * pltpu.roll(xx, j, 0)[SUBLANES:, :]
        y_ref[...] = acc

    return pl.pallas_call(
        body, name=name, grid=(width // wc, lp // tm),
        in_specs=[pl.BlockSpec((tm, wc), lambda j, i: (i, offb + j)),
                  pl.BlockSpec((SUBLANES, wc), lambda j, i: (jnp.maximum(i * hb - 1, 0), offb + j)),
                  pl.BlockSpec((kk, wc), lambda j, i: (0, j)),
                  pl.BlockSpec((1, wc), lambda j, i: (0, j))],
        out_specs=pl.BlockSpec((tm, wc), lambda j, i: (i, j)),
        out_shape=jax.ShapeDtypeStruct((lp, width), F32),
        compiler_params=pltpu.CompilerParams(dimension_semantics=("parallel", "parallel")),
    )(x, x, w, b)


def _conv_bwd(x, col_off, width, dy, w, name):
    lp = x.shape[0]
    kk = w.shape[0]
    tm, wc = _conv_tiles(lp, width)
    offb = col_off // wc
    assert col_off % wc == 0
    hb = tm // SUBLANES
    nb8 = lp // SUBLANES

    def body(x_ref, xp_ref, dy_ref, dn_ref, w_ref, dx_ref, dw_ref, db_ref):
        i = pl.program_id(1)
        last = pl.num_programs(1) - 1
        xv = x_ref[...]
        dyv = dy_ref[...]
        xx = jnp.concatenate([jnp.where(i > 0, xp_ref[...], 0.0), xv], axis=0)
        dd = jnp.concatenate([dyv, jnp.where(i < last, dn_ref[...], 0.0)], axis=0)
        dx = w_ref[kk - 1:kk, :] * dyv
        rows = [jnp.sum(dyv * xv, axis=0, keepdims=True)]
        for m in range(1, kk):
            dx = dx + w_ref[kk - 1 - m:kk - m, :] * pltpu.roll(dd, tm + SUBLANES - m, 0)[:tm, :]
            rows.append(jnp.sum(dyv * pltpu.roll(xx, m, 0)[SUBLANES:, :], axis=0, keepdims=True))
        dx_ref[...] = dx.astype(dx_ref.dtype)
        dwp = jnp.concatenate(rows[::-1] + [jnp.zeros((SUBLANES - kk, wc), F32)], axis=0)

        @pl.when(i == 0)
        def _():
            dw_ref[...] = jnp.zeros_like(dw_ref)
            db_ref[...] = jnp.zeros_like(db_ref)

        dw_ref[...] += dwp
        db_ref[...] += jnp.sum(dyv, axis=0, keepdims=True)

    return pl.pallas_call(
        body, name=name, grid=(width // wc, lp // tm),
        in_specs=[pl.BlockSpec((tm, wc), lambda j, i: (i, offb + j)),
                  pl.BlockSpec((SUBLANES, wc), lambda j, i: (jnp.maximum(i * hb - 1, 0), offb + j)),
                  pl.BlockSpec((tm, wc), lambda j, i: (i, j)),
                  pl.BlockSpec((SUBLANES, wc), lambda j, i: (jnp.minimum((i + 1) * hb, nb8 - 1), j)),
                  pl.BlockSpec((kk, wc), lambda j, i: (0, j))],
        out_specs=[pl.BlockSpec((tm, wc), lambda j, i: (i, j)),
                   pl.BlockSpec((SUBLANES, wc), lambda j, i: (0, j)),
                   pl.BlockSpec((1, wc), lambda j, i: (0, j))],
        out_shape=[jax.ShapeDtypeStruct((lp, width), BF16),
                   jax.ShapeDtypeStruct((SUBLANES, width), F32),
                   jax.ShapeDtypeStruct((1, width), F32)],
        compiler_params=pltpu.CompilerParams(dimension_semantics=("parallel", "arbitrary")),
    )(x, x, dy, dy, w)


def _ffn_act_fwd(hc, name):
    lp = hc.shape[0]
    tm = _pick(lp, (320, 256, 128))

    def body(r, first, g_ref, u_ref, a_ref):
        a_ref[...] = (_gelu(g_ref[...]) * u_ref[...]).astype(a_ref.dtype)

    return _rowcall(name, body, lp, tm, rows=[(hc, D_FF, 0), (hc, D_FF, 1)], outs=[(D_FF, BF16)])[0]


def _ffn_act_bwd(hc, dact, name):
    lp = hc.shape[0]
    tm = BLOCK

    def body(r, first, g_ref, u_ref, da_ref, dh_ref):
        gl, dgl = _gelu_and_grad(g_ref[...])
        da = da_ref[...]
        dh_ref[:, :D_FF] = da * u_ref[...] * dgl
        dh_ref[:, D_FF:] = da * gl

    return _rowcall(name, body, lp, tm, rows=[(hc, D_FF, 0), (hc, D_FF, 1), (dact, D_FF, 0)],
                    outs=[(2 * D_FF, F32)])[0]


def _lru_gates(x, wa_ref, wx_ref, ba, bx, lam):
    xb = x.astype(BF16)
    za, zx = [], []
    for n in range(LRU_BLOCKS):
        xs = xb[:, n * LRU_BS:(n + 1) * LRU_BS]
        za.append(_dot(xs, wa_ref[n]))
        zx.append(_dot(xs, wx_ref[n]))
    r = _sigmoid(jnp.concatenate(za, axis=1) + ba)
    ig = _sigmoid(jnp.concatenate(zx, axis=1) + bx)
    sp = _softplus(-lam)
    log_a = -LRU_C * r * sp
    a = jnp.exp(log_a)
    om = _neg_expm1(2.0 * log_a)
    mult = jnp.sqrt(om)
    return xb, r, ig, sp, a, om, mult


def _lru_fwd(proj, xrc, wa, wx, ba, bx, lam, name):
    lp, d = xrc.shape
    tm = BLOCK

    def body(r_idx, first, gate_ref, x_ref, wa_ref, wx_ref, ba_ref, bx_ref, lam_ref,
             y_ref, h_ref, carry):
        @pl.when(first)
        def _():
            carry[...] = jnp.zeros_like(carry)

        x = x_ref[...]
        _, _, ig, _, a, _, mult = _lru_gates(x, wa_ref, wx_ref, ba_ref[...], bx_ref[...], lam_ref[...])
        u = jnp.where(_real_rows(r_idx, tm), mult * ig * x, 0.0)
        acum, hloc = _scan_fwd(a, u, tm)
        h = hloc + acum * carry[0:1, :]
        h_ref[...] = h
        carry[0:1, :] = h[tm - 1:tm, :]
        y_ref[...] = (_gelu(gate_ref[...]) * h).astype(y_ref.dtype)

    return _rowcall(name, body, lp, tm, rows=[(proj, d, 0), (xrc, d, 0)],
                    vecs=[wa, wx, ba, bx, lam], outs=[(d, BF16), (d, F32)],
                    scratch=[pltpu.VMEM((SUBLANES, d), F32)], seq=True)


def _lru_bwd(proj, xrc, hl, dmix, wa, wx, ba, bx, lam, name):
    lp, d = xrc.shape
    tm = BLOCK

    def body(r_idx, first, gate_ref, x_ref, h_ref, dy_ref, hp_ref, wa_ref, wx_ref, ba_ref, bx_ref,
             lam_ref, dgate_ref, dx_ref, dwa_ref, dwx_ref, dba_ref, dbx_ref, dlam_ref, carry):
        @pl.when(first)
        def _():
            carry[...] = jnp.zeros_like(carry)
            dwa_ref[...] = jnp.zeros_like(dwa_ref)
            dwx_ref[...] = jnp.zeros_like(dwx_ref)
            dba_ref[...] = jnp.zeros_like(dba_ref)
            dbx_ref[...] = jnp.zeros_like(dbx_ref)
            dlam_ref[...] = jnp.zeros_like(dlam_ref)

        x = x_ref[...]
        lam = lam_ref[...]
        xb, r, ig, sp, a, om, mult = _lru_gates(x, wa_ref, wx_ref, ba_ref[...], bx_ref[...], lam)
        h = h_ref[...]
        dy = dy_ref[...]
        gl, dgl = _gelu_and_grad(gate_ref[...])
        dgate_ref[...] = (dy * h * dgl).astype(dgate_ref.dtype)
        row = _row_iota(tm)
        lastrow = row == tm - 1
        xg = dy * gl + jnp.where(lastrow, carry[0:1, :], 0.0)
        c = jnp.where(lastrow, 1.0, pltpu.roll(a, tm - 1, 0))
        _, g = _scan_rev(c, xg, tm)
        carry[0:1, :] = a[0:1, :] * g[0:1, :]
        hprev_in = jnp.where(r_idx > 0, hp_ref[SUBLANES - 1:SUBLANES, :], 0.0)
        hprev = jnp.where(row == 0, hprev_in, pltpu.roll(h, 1, 0))
        du = jnp.where(_real_rows(r_idx, tm), g, 0.0)
        da = g * hprev
        dmult = du * ig * x
        dig = du * mult * x
        dxv = du * mult * ig
        e2 = 1.0 - om
        dlog_a = da * a - dmult * e2 / mult
        dr = dlog_a * (-LRU_C) * sp
        dsp = jnp.sum(dlog_a * (-LRU_C) * r, axis=0, keepdims=True)
        dlam_ref[...] += -dsp * _sigmoid(-lam)
        dza = dr * r * (1.0 - r)
        dzx = dig * ig * (1.0 - ig)
        dba_ref[...] += jnp.sum(dza, axis=0, keepdims=True)
        dbx_ref[...] += jnp.sum(dzx, axis=0, keepdims=True)
        dzab = dza.astype(BF16)
        dzxb = dzx.astype(BF16)
        parts = []
        for n in range(LRU_BLOCKS):
            sl = slice(n * LRU_BS, (n + 1) * LRU_BS)
            dwa_ref[n] += _dot_tn(xb[:, sl], dzab[:, sl])
            dwx_ref[n] += _dot_tn(xb[:, sl], dzxb[:, sl])
            parts.append(_dot_nt(dzab[:, sl], wa_ref[n]) + _dot_nt(dzxb[:, sl], wx_ref[n]))
        dx_ref[...] = dxv + jnp.concatenate(parts, axis=1)

    return _rowcall(name, body, lp, tm,
                    rows=[(proj, d, 0), (xrc, d, 0), (hl, d, 0), (dmix, d, 0)],
                    prevs=[(hl, d, 0)], vecs=[wa, wx, ba, bx, lam],
                    outs=[(d, BF16), (d, F32)],
                    accs=[((LRU_BLOCKS, LRU_BS, LRU_BS), F32), ((LRU_BLOCKS, LRU_BS, LRU_BS), F32),
                          ((1, d), F32), ((1, d), F32), ((1, d), F32)],
                    scratch=[pltpu.VMEM((SUBLANES, d), F32)], reverse=True, seq=True)


_SLOPES = [2.0 ** (-8.0 * (h + 1) / N_Q_HEADS) for h in range(N_Q_HEADS)]
_QK_SCALE = HEAD_DIM ** -0.5
_QCOL = 2 * D_MODEL // D_MODEL
_KCOL = (3 * D_MODEL) // LANES
_VCOL = _KCOL + 1


def _attn_masks(n):
    start = pl.multiple_of(jnp.maximum(n - 1, 0) * BLOCK, BLOCK)
    qi = n * BLOCK + lax.broadcasted_iota(jnp.int32, (BLOCK, 2 * BLOCK), 0)
    kj = start + lax.broadcasted_iota(jnp.int32, (BLOCK, 2 * BLOCK), 1)
    dist = qi - kj
    ok = (kj >= BLOCK) & (dist >= 0) & (dist < WINDOW)
    dm = (n * BLOCK - PAD + lax.broadcasted_iota(jnp.int32, (BLOCK, N_META), 0)
          - lax.broadcasted_iota(jnp.int32, (BLOCK, N_META), 1))
    okm = dm >= 0
    return start, ok, dist.astype(F32), okm, jnp.minimum(dm, WINDOW).astype(F32)


def _group_rows(ref, g):
    return jnp.concatenate(
        [ref[:, (g * Q_PER_KV + hh) * HEAD_DIM:(g * Q_PER_KV + hh + 1) * HEAD_DIM] for hh in range(Q_PER_KV)],
        axis=0).astype(BF16)


def _attn_probs(qg, kg, kmg, sink_ref, g, ok, distf, okm, dmf):
    slope = jnp.stack([jnp.full((1, 1), _SLOPES[g * Q_PER_KV + hh], F32) for hh in range(Q_PER_KV)])
    sink = jnp.stack([sink_ref[0:1, g * Q_PER_KV + hh:g * Q_PER_KV + hh + 1] for hh in range(Q_PER_KV)])
    s = (_dot_nt(qg, kg) * _QK_SCALE).reshape(Q_PER_KV, BLOCK, 2 * BLOCK)
    sm = (_dot_nt(qg, kmg) * _QK_SCALE).reshape(Q_PER_KV, BLOCK, N_META)
    s = jnp.where(ok[None], s - slope * distf[None], NEG)
    sm = jnp.where(okm[None], sm - slope * dmf[None], NEG)
    mx = jnp.maximum(jnp.maximum(jnp.max(s, axis=-1, keepdims=True),
                                 jnp.max(sm, axis=-1, keepdims=True)), sink)
    p = jnp.exp(s - mx)
    pm = jnp.exp(sm - mx)
    ps = jnp.exp(sink - mx)
    inv = 1.0 / (jnp.sum(p, axis=-1, keepdims=True) + jnp.sum(pm, axis=-1, keepdims=True) + ps)
    return p * inv, pm * inv, ps * inv


def _attn_fwd(proj, sinks, name, plan=None):
    lp = proj.shape[0]
    nblk = lp // BLOCK
    p_in, p_shapes, p_out, p_scr = _plan_parts(plan)

    def body(*refs):
        q_ref, k_ref, v_ref, sink_ref = refs[:4]
        cins = refs[4:4 + len(p_in)]
        o_ref = refs[4 + len(p_in)]
        couts = refs[5 + len(p_in):5 + len(p_in) + len(p_out)]
        sems = refs[5 + len(p_in) + len(p_out):]
        n = pl.program_id(0)
        if plan is not None:
            @pl.when(n == 0)
            def _():
                plan.start(cins, couts, sems)

        start, ok, distf, okm, dmf = _attn_masks(n)
        kb = k_ref[pl.ds(start, 2 * BLOCK), :].astype(BF16)
        vb = v_ref[pl.ds(start, 2 * BLOCK), :].astype(BF16)
        km = k_ref[PAD:BLOCK, :].astype(BF16)
        vm = v_ref[PAD:BLOCK, :].astype(BF16)
        for g in range(N_KV_HEADS):
            gs = slice(g * HEAD_DIM, (g + 1) * HEAD_DIM)
            pn, pmn, _ = _attn_probs(_group_rows(q_ref, g), kb[:, gs], km[:, gs], sink_ref, g,
                                     ok, distf, okm, dmf)
            o = (_dot(pn.astype(BF16).reshape(Q_PER_KV * BLOCK, 2 * BLOCK), vb[:, gs])
                 + _dot(pmn.astype(BF16).reshape(Q_PER_KV * BLOCK, N_META), vm[:, gs]))
            for hh in range(Q_PER_KV):
                h = g * Q_PER_KV + hh
                o_ref[:, h * HEAD_DIM:(h + 1) * HEAD_DIM] = o[hh * BLOCK:(hh + 1) * BLOCK, :].astype(o_ref.dtype)
        if plan is not None:
            @pl.when(n == nblk - 1)
            def _():
                plan.wait(cins, couts, sems)

    res = pl.pallas_call(
        body, name=name, grid=(nblk,),
        in_specs=[pl.BlockSpec((BLOCK, D_MODEL), lambda n: (n, _QCOL)),
                  pl.BlockSpec((lp, LANES), lambda n: (0, _KCOL)),
                  pl.BlockSpec((lp, LANES), lambda n: (0, _VCOL)),
                  pl.BlockSpec(sinks.shape, lambda n: (0, 0))] + p_in,
        out_specs=[pl.BlockSpec((BLOCK, D_MODEL), lambda n: (n, 0))] + p_out,
        out_shape=[jax.ShapeDtypeStruct((lp, D_MODEL), BF16)] + p_shapes,
        scratch_shapes=p_scr,
        compiler_params=pltpu.CompilerParams(dimension_semantics=("arbitrary",)),
    )(proj, proj, proj, sinks, *(plan.ins if plan is not None else []))
    return res[0], res[1:]


def _attn_bwd(proj, sinks, dmix, name, plan=None):
    lp = proj.shape[0]
    nblk = lp // BLOCK

    p_in, p_shapes, p_out, p_scr = _plan_parts(plan)

    def body(*refs):
        q_ref, k_ref, v_ref, sink_ref, dy_ref = refs[:5]
        cins = refs[5:5 + len(p_in)]
        dq_ref, dk_ref, dv_ref, ds_ref = refs[5 + len(p_in):9 + len(p_in)]
        couts = refs[9 + len(p_in):9 + len(p_in) + len(p_out)]
        sems = refs[9 + len(p_in) + len(p_out):]
        n = pl.program_id(0)

        @pl.when(n == 0)
        def _():
            dk_ref[...] = jnp.zeros_like(dk_ref)
            dv_ref[...] = jnp.zeros_like(dv_ref)
            ds_ref[...] = jnp.zeros_like(ds_ref)
            if plan is not None:
                plan.start(cins, couts, sems)

        start, ok, distf, okm, dmf = _attn_masks(n)
        kb = k_ref[pl.ds(start, 2 * BLOCK), :].astype(BF16)
        vb = v_ref[pl.ds(start, 2 * BLOCK), :].astype(BF16)
        km = k_ref[PAD:BLOCK, :].astype(BF16)
        vm = v_ref[PAD:BLOCK, :].astype(BF16)
        lane16 = lax.broadcasted_iota(jnp.int32, (1, N_Q_HEADS), 1)
        dsink = jnp.zeros((1, N_Q_HEADS), F32)
        rows = Q_PER_KV * BLOCK
        for g in range(N_KV_HEADS):
            gs = slice(g * HEAD_DIM, (g + 1) * HEAD_DIM)
            qg = _group_rows(q_ref, g)
            dog = _group_rows(dy_ref, g)
            pn, pmn, psn = _attn_probs(qg, kb[:, gs], km[:, gs], sink_ref, g, ok, distf, okm, dmf)
            dp = _dot_nt(dog, vb[:, gs]).reshape(Q_PER_KV, BLOCK, 2 * BLOCK)
            dpm = _dot_nt(dog, vm[:, gs]).reshape(Q_PER_KV, BLOCK, N_META)
            delta = (jnp.sum(pn * dp, axis=-1, keepdims=True)
                     + jnp.sum(pmn * dpm, axis=-1, keepdims=True))
            dsb = (pn * (dp - delta)).astype(BF16).reshape(rows, 2 * BLOCK)
            dsm = (pmn * (dpm - delta)).astype(BF16).reshape(rows, N_META)
            dsk = jnp.sum(psn * delta, axis=1, keepdims=True)
            for hh in range(Q_PER_KV):
                dsink = dsink - jnp.where(lane16 == g * Q_PER_KV + hh, dsk[hh], 0.0)
            dq = (_dot(dsb, kb[:, gs]) + _dot(dsm, km[:, gs])) * _QK_SCALE
            for hh in range(Q_PER_KV):
                h = g * Q_PER_KV + hh
                dq_ref[:, h * HEAD_DIM:(h + 1) * HEAD_DIM] = dq[hh * BLOCK:(hh + 1) * BLOCK, :].astype(dq_ref.dtype)
            pnb = pn.astype(BF16).reshape(rows, 2 * BLOCK)
            pmnb = pmn.astype(BF16).reshape(rows, N_META)
            dk_ref[pl.ds(start, 2 * BLOCK), gs] += _dot_tn(dsb, qg) * _QK_SCALE
            dv_ref[pl.ds(start, 2 * BLOCK), gs] += _dot_tn(pnb, dog)
            dk_ref[PAD:BLOCK, gs] += _dot_tn(dsm, qg) * _QK_SCALE
            dv_ref[PAD:BLOCK, gs] += _dot_tn(pmnb, dog)
        ds_ref[...] += dsink
        if plan is not None:
            @pl.when(n == nblk - 1)
            def _():
                plan.wait(cins, couts, sems)

    res = pl.pallas_call(
        body, name=name, grid=(nblk,),
        in_specs=[pl.BlockSpec((BLOCK, D_MODEL), lambda n: (n, _QCOL)),
                  pl.BlockSpec((lp, LANES), lambda n: (0, _KCOL)),
                  pl.BlockSpec((lp, LANES), lambda n: (0, _VCOL)),
                  pl.BlockSpec(sinks.shape, lambda n: (0, 0)),
                  pl.BlockSpec((BLOCK, D_MODEL), lambda n: (n, 1))] + p_in,
        out_specs=[pl.BlockSpec((BLOCK, D_MODEL), lambda n: (n, 0)),
                   pl.BlockSpec((lp, LANES), lambda n: (0, 0)),
                   pl.BlockSpec((lp, LANES), lambda n: (0, 0)),
                   pl.BlockSpec((1, N_Q_HEADS), lambda n: (0, 0))] + p_out,
        out_shape=[jax.ShapeDtypeStruct((lp, D_MODEL), BF16),
                   jax.ShapeDtypeStruct((lp, LANES), F32),
                   jax.ShapeDtypeStruct((lp, LANES), F32),
                   jax.ShapeDtypeStruct((1, N_Q_HEADS), F32)] + p_shapes,
        scratch_shapes=p_scr,
        compiler_params=pltpu.CompilerParams(dimension_semantics=("arbitrary",)),
    )(proj, proj, proj, sinks, dmix, *(plan.ins if plan is not None else []))
    return res[:4], res[4:]


_ZW = D_SSM
_XBC_W = D_SSM + 2 * SSD_GROUPS * SSD_N
_DT_COL = (_ZW + _XBC_W) // LANES
EVEN_IN = 3 * D_MODEL + 2 * LANES
ODD_IN = _ZW + _XBC_W + SSD_HEADS
ODD_IN_PAD = _ZW + _XBC_W + LANES


def _ssm_prep_fwd(xc, proj, dt_bias, name):
    lp = xc.shape[0]
    tm = _pick(lp, (320, 256, 128))

    def body(r, first, xc_ref, dtr_ref, b_ref, act_ref, dt_ref):
        real = _real_rows(r, tm)
        act, _ = _silu_and_grad(xc_ref[...])
        act_ref[...] = jnp.where(real, act, 0.0)
        dt_ref[...] = jnp.where(real, _softplus(dtr_ref[...] + b_ref[...]), 0.0)

    return _rowcall(name, body, lp, tm, rows=[(xc, _XBC_W, 0), (proj, LANES, _DT_COL)],
                    vecs=[dt_bias], outs=[(_XBC_W, F32), (LANES, F32)])


def _ssm_prep_bwd(xc, proj, dt_bias, dxs, dxskip, db, dc, ddt, name):
    lp = xc.shape[0]
    tm = BLOCK

    def body(r, first, xc_ref, dtr_ref, dxs_ref, dsk_ref, db_ref, dc_ref, ddt_ref, b_ref,
             dxc_ref, ddtr_ref, dbias_ref):
        real = _real_rows(r, tm)
        _, ds = _silu_and_grad(xc_ref[...])
        dxc_ref[:, :D_SSM] = jnp.where(real, (dxs_ref[...] + dsk_ref[...]) * ds[:, :D_SSM], 0.0)
        dxc_ref[:, D_SSM:D_SSM + 1024] = jnp.where(real, db_ref[...] * ds[:, D_SSM:D_SSM + 1024], 0.0)
        dxc_ref[:, D_SSM + 1024:] = jnp.where(real, dc_ref[...] * ds[:, D_SSM + 1024:], 0.0)
        dd = jnp.where(real, ddt_ref[...] * _sigmoid(dtr_ref[...] + b_ref[...]), 0.0)
        ddtr_ref[...] = dd.astype(ddtr_ref.dtype)
        _acc_add(first, dbias_ref, jnp.sum(dd, axis=0, keepdims=True))

    return _rowcall(name, body, lp, tm,
                    rows=[(xc, _XBC_W, 0), (proj, LANES, _DT_COL), (dxs, D_SSM, 0), (dxskip, D_SSM, 0),
                          (db, 1024, 0), (dc, 1024, 0), (ddt, LANES, 0)],
                    vecs=[dt_bias], outs=[(_XBC_W, F32), (LANES, BF16)], accs=[((1, LANES), F32)])


def _ssd_common(dt, alog):
    a = -jnp.exp(alog)
    cs = _cumsum_rows(dt * a, BLOCK)
    cst = cs.T
    cl = cs[BLOCK - 1:BLOCK, :]
    tril = (lax.broadcasted_iota(jnp.int32, (BLOCK, BLOCK), 0)
            >= lax.broadcasted_iota(jnp.int32, (BLOCK, BLOCK), 1))
    return a, cs, cst, cl, jnp.exp(cs), jnp.exp(cl - cs), jnp.exp(cl), tril


def _head_cols(ecl, g):
    lane = lax.broadcasted_iota(jnp.int32, (1, SSD_HPG * SSD_P), 1)
    e = [ecl[:, SSD_HPG * g + hh:SSD_HPG * g + hh + 1] for hh in range(SSD_HPG)]
    return jnp.where(lane < SSD_P, e[0], jnp.where(lane < 2 * SSD_P, e[1],
                                                   jnp.where(lane < 3 * SSD_P, e[2], e[3])))


def _ssd_fwd(xbc, dt, alog, name):
    lp = xbc.shape[0]
    nc = lp // BLOCK
    gw = SSD_HPG * SSD_P

    def body(xs_ref, b_ref, c_ref, dt_ref, alog_ref, y_ref, so_ref, st, fx):
        n = pl.program_id(0)

        @pl.when(n == 0)
        def _():
            st[...] = jnp.zeros_like(st)

        dtv = dt_ref[...]
        _, cs, cst, cl, e, f, ecl, tril = _ssd_common(dtv, alog_ref[...])
        for g in range(SSD_GROUPS):
            bg = b_ref[:, g * SSD_N:(g + 1) * SSD_N].astype(BF16)
            cg = c_ref[:, g * SSD_N:(g + 1) * SSD_N].astype(BF16)
            gm = _dot_nt(cg, bg)
            stg = st[g]
            so_ref[0, g] = stg
            yoff = _dot(cg, stg.astype(BF16))
            for hh in range(SSD_HPG):
                h = SSD_HPG * g + hh
                hs = slice(h * SSD_P, (h + 1) * SSD_P)
                seg = cs[:, h:h + 1] - cst[h:h + 1, :]
                m = gm * jnp.exp(jnp.where(tril, seg, NEG))
                xdt = xs_ref[:, hs] * dtv[:, h:h + 1]
                y_ref[:, hs] = (_dot(m.astype(BF16), xdt.astype(BF16))
                                + e[:, h:h + 1] * yoff[:, hh * SSD_P:(hh + 1) * SSD_P])
                fx[:, hh * SSD_P:(hh + 1) * SSD_P] = f[:, h:h + 1] * xdt
            st[g] = stg * _head_cols(ecl, g) + _dot_tn(bg, fx[...].astype(BF16))

    return pl.pallas_call(
        body, name=name, grid=(nc,),
        in_specs=[pl.BlockSpec((BLOCK, D_SSM), lambda n: (n, 0)),
                  pl.BlockSpec((BLOCK, 1024), lambda n: (n, 2)),
                  pl.BlockSpec((BLOCK, 1024), lambda n: (n, 3)),
                  pl.BlockSpec((BLOCK, LANES), lambda n: (n, 0)),
                  pl.BlockSpec((1, LANES), lambda n: (0, 0))],
        out_specs=[pl.BlockSpec((BLOCK, D_SSM), lambda n: (n, 0)),
                   pl.BlockSpec((1, SSD_GROUPS, SSD_N, gw), lambda n: (n, 0, 0, 0))],
        out_shape=[jax.ShapeDtypeStruct((lp, D_SSM), F32),
                   jax.ShapeDtypeStruct((nc, SSD_GROUPS, SSD_N, gw), F32)],
        scratch_shapes=[pltpu.VMEM((SSD_GROUPS, SSD_N, gw), F32), pltpu.VMEM((BLOCK, gw), F32)],
        compiler_params=pltpu.CompilerParams(dimension_semantics=("arbitrary",)),
    )(xbc, xbc, xbc, dt, alog)


def _ssd_bwd(xbc, dt, alog, states, dy, name, plan=None):
    lp = xbc.shape[0]
    nc = lp // BLOCK
    gw = SSD_HPG * SSD_P
    p_in, p_shapes, p_out, p_scr = _plan_parts(plan)

    def body(*refs):
        xs_ref, b_ref, c_ref, dt_ref, alog_ref, dy_ref, st_ref = refs[:7]
        cins = refs[7:7 + len(p_in)]
        dxs_ref, db_ref, dc_ref, ddt_ref, dalog_ref = refs[7 + len(p_in):12 + len(p_in)]
        couts = refs[12 + len(p_in):12 + len(p_in) + len(p_out)]
        dst, edy, fx = refs[12 + len(p_in) + len(p_out):15 + len(p_in) + len(p_out)]
        sems = refs[15 + len(p_in) + len(p_out):]
        i = pl.program_id(0)

        @pl.when(i == 0)
        def _():
            dst[...] = jnp.zeros_like(dst)
            dalog_ref[...] = jnp.zeros_like(dalog_ref)
            if plan is not None:
                plan.start(cins, couts, sems)

        dtv = dt_ref[...]
        a, cs, cst, cl, e, f, ecl, tril = _ssd_common(dtv, alog_ref[...])
        lane = lax.broadcasted_iota(jnp.int32, (1, LANES), 1)
        sub = _row_iota(BLOCK)
        dcs = jnp.zeros((BLOCK, LANES), F32)
        dcst = jnp.zeros((LANES, BLOCK), F32)
        dcl = jnp.zeros((1, LANES), F32)
        ddtx = jnp.zeros((BLOCK, LANES), F32)
        for g in range(SSD_GROUPS):
            bg = b_ref[:, g * SSD_N:(g + 1) * SSD_N].astype(BF16)
            cg = c_ref[:, g * SSD_N:(g + 1) * SSD_N].astype(BF16)
            gm = _dot_nt(cg, bg)
            stg = st_ref[0, g]
            stb = stg.astype(BF16)
            dso = dst[g]
            dsob = dso.astype(BF16)
            yraw = _dot(cg, stb)
            dfx = _dot(bg, dsob)
            prodsum = jnp.sum(dso * stg, axis=0, keepdims=True)
            dgm = jnp.zeros((BLOCK, BLOCK), F32)
            for hh in range(SSD_HPG):
                h = SSD_HPG * g + hh
                hs = slice(h * SSD_P, (h + 1) * SSD_P)
                ls = slice(hh * SSD_P, (hh + 1) * SSD_P)
                eh = e[:, h:h + 1]
                fh = f[:, h:h + 1]
                dth = dtv[:, h:h + 1]
                xh = xs_ref[:, hs]
                xdt = xh * dth
                dyh = dy_ref[:, hs]
                dyb = dyh.astype(BF16)
                lam = jnp.exp(jnp.where(tril, cs[:, h:h + 1] - cst[h:h + 1, :], NEG))
                m = gm * lam
                dm = _dot_nt(dyb, xdt.astype(BF16))
                dxdt = _dot_tn(m.astype(BF16), dyb) + fh * dfx[:, ls]
                w = dm * m
                dgm = dgm + dm * lam
                dff = jnp.sum(dfx[:, ls] * xdt, axis=1, keepdims=True) * fh
                col = (jnp.sum(w, axis=1, keepdims=True)
                       + jnp.sum(dyh * yraw[:, ls], axis=1, keepdims=True) * eh - dff)
                onl = (lane == h).astype(F32)
                dcs = dcs + col * onl
                dcst = dcst - (sub == h).astype(F32) * jnp.sum(w, axis=0, keepdims=True)
                dclh = (jnp.sum(dff, axis=0, keepdims=True)
                        + ecl[:, h:h + 1] * jnp.sum(prodsum[:, ls], axis=1, keepdims=True))
                dcl = dcl + dclh * onl
                ddtx = ddtx + jnp.sum(dxdt * xh, axis=1, keepdims=True) * onl
                dxs_ref[:, hs] = dxdt * dth
                edy[:, ls] = eh * dyh
                fx[:, ls] = fh * xdt
            edyb = edy[...].astype(BF16)
            fxb = fx[...].astype(BF16)
            dgb = dgm.astype(BF16)
            dc_ref[:, g * SSD_N:(g + 1) * SSD_N] = _dot_nt(edyb, stb) + _dot(dgb, bg)
            db_ref[:, g * SSD_N:(g + 1) * SSD_N] = _dot_nt(fxb, dsob) + _dot_tn(dgb, cg)
            dst[g] = dso * _head_cols(ecl, g) + _dot_tn(cg, edyb)
        dcs = dcs + dcst.T + jnp.where(sub == BLOCK - 1, dcl, 0.0)
        dda = _rev_cumsum_rows(dcs, BLOCK)
        ddt_ref[...] = ddtx + dda * a
        dalog_ref[...] += jnp.sum(dda * dtv, axis=0, keepdims=True) * a
        if plan is not None:
            @pl.when(i == nc - 1)
            def _():
                plan.wait(cins, couts, sems)

    rev = lambda i: nc - 1 - i
    res = pl.pallas_call(
        body, name=name, grid=(nc,),
        in_specs=[pl.BlockSpec((BLOCK, D_SSM), lambda i: (rev(i), 0)),
                  pl.BlockSpec((BLOCK, 1024), lambda i: (rev(i), 2)),
                  pl.BlockSpec((BLOCK, 1024), lambda i: (rev(i), 3)),
                  pl.BlockSpec((BLOCK, LANES), lambda i: (rev(i), 0)),
                  pl.BlockSpec((1, LANES), lambda i: (0, 0)),
                  pl.BlockSpec((BLOCK, D_SSM), lambda i: (rev(i), 0)),
                  pl.BlockSpec((1, SSD_GROUPS, SSD_N, gw), lambda i: (rev(i), 0, 0, 0))] + p_in,
        out_specs=[pl.BlockSpec((BLOCK, D_SSM), lambda i: (rev(i), 0)),
                   pl.BlockSpec((BLOCK, 1024), lambda i: (rev(i), 0)),
                   pl.BlockSpec((BLOCK, 1024), lambda i: (rev(i), 0)),
                   pl.BlockSpec((BLOCK, LANES), lambda i: (rev(i), 0)),
                   pl.BlockSpec((1, LANES), lambda i: (0, 0))] + p_out,
        out_shape=[jax.ShapeDtypeStruct((lp, D_SSM), F32),
                   jax.ShapeDtypeStruct((lp, 1024), F32),
                   jax.ShapeDtypeStruct((lp, 1024), F32),
                   jax.ShapeDtypeStruct((lp, LANES), F32),
                   jax.ShapeDtypeStruct((1, LANES), F32)] + p_shapes,
        scratch_shapes=[pltpu.VMEM((SSD_GROUPS, SSD_N, gw), F32),
                        pltpu.VMEM((BLOCK, gw), F32), pltpu.VMEM((BLOCK, gw), F32)] + p_scr,
        compiler_params=pltpu.CompilerParams(dimension_semantics=("arbitrary",)),
    )(xbc, xbc, xbc, dt, alog, dy, states, *(plan.ins if plan is not None else []))
    return res[:5], res[5:]


_GN_GROUPS = 8
_GN_W = D_SSM // _GN_GROUPS


def _ssm_gate_fwd(yssd, xbc, proj, dskip, gnorm, name):
    lp = yssd.shape[0]
    tm = _pick(lp, (320, 256, 128))

    def body(r, first, y_ref, x_ref, z_ref, d_ref, g_ref, o_ref):
        sz, _ = _silu_and_grad(z_ref[...])
        y2 = (y_ref[...] + d_ref[...] * x_ref[...]) * sz
        for k in range(_GN_GROUPS):
            sl = slice(k * _GN_W, (k + 1) * _GN_W)
            yk = y2[:, sl]
            rs = lax.rsqrt(jnp.mean(yk * yk, axis=-1, keepdims=True) + EPS)
            o_ref[:, sl] = (yk * rs * g_ref[:, sl]).astype(o_ref.dtype)

    return _rowcall(name, body, lp, tm, rows=[(yssd, D_SSM, 0), (xbc, D_SSM, 0), (proj, D_SSM, 0)],
                    vecs=[dskip, gnorm], outs=[(D_SSM, BF16)])[0]


def _ssm_gate_bwd(yssd, xbc, proj, dskip, gnorm, dyn, name):
    lp = yssd.shape[0]
    tm = BLOCK

    def body(r, first, y_ref, x_ref, z_ref, dyn_ref, d_ref, g_ref,
             dy_ref, dx_ref, dz_ref, dd_ref, dg_ref):
        z = z_ref[...]
        sz, dsz = _silu_and_grad(z)
        xs = x_ref[...]
        y1 = y_ref[...] + d_ref[...] * xs
        y2 = y1 * sz
        dyn = dyn_ref[...]
        for k in range(_GN_GROUPS):
            sl = slice(k * _GN_W, (k + 1) * _GN_W)
            dx, dgt = _rms_bwd(y2[:, sl], g_ref[:, sl], dyn[:, sl])
            dy1 = dx * sz[:, sl]
            dy_ref[:, sl] = dy1
            dx_ref[:, sl] = dy1 * d_ref[:, sl]
            dz_ref[:, sl] = (dx * y1[:, sl] * dsz[:, sl]).astype(dz_ref.dtype)

            @pl.when(first)
            def _():
                dd_ref[:, sl] = jnp.zeros((1, _GN_W), F32)
                dg_ref[:, sl] = jnp.zeros((1, _GN_W), F32)

            dd_ref[:, sl] += jnp.sum(dy1 * xs[:, sl], axis=0, keepdims=True)
            dg_ref[:, sl] += jnp.sum(dgt, axis=0, keepdims=True)

    return _rowcall(name, body, lp, tm,
                    rows=[(yssd, D_SSM, 0), (xbc, D_SSM, 0), (proj, D_SSM, 0), (dyn, D_SSM, 0)],
                    vecs=[dskip, gnorm], outs=[(D_SSM, F32), (D_SSM, F32), (D_SSM, BF16)],
                    accs=[((1, D_SSM), F32), ((1, D_SSM), F32)])


def _adamw(w, g, m, v, name):
    r, c = w.shape
    tm = r if r <= 512 else _pick(r, (512, 352, 256, 128, 64, 32, 16, 8))
    c1 = 1.0 / (1.0 - ADAM_B1 ** ADAM_STEP)
    c2 = 1.0 / (1.0 - ADAM_B2 ** ADAM_STEP)

    def body(w_ref, g_ref, m_ref, v_ref, d_ref, nm_ref, nv_ref):
        gv = g_ref[...]
        nm = ADAM_B1 * m_ref[...] + (1.0 - ADAM_B1) * gv
        nv = ADAM_B2 * v_ref[...] + (1.0 - ADAM_B2) * (gv * gv)
        nm_ref[...] = nm
        nv_ref[...] = nv
        d_ref[...] = -ADAM_LR * ((nm * c1) / (jnp.sqrt(nv * c2) + ADAM_EPS) + ADAM_WD * w_ref[...])

    spec = pl.BlockSpec((tm, c), lambda i: (i, 0))
    return pl.pallas_call(
        body, name=name, grid=(r // tm,), in_specs=[spec] * 4, out_specs=[spec] * 3,
        out_shape=[jax.ShapeDtypeStruct((r, c), F32)] * 3,
        compiler_params=pltpu.CompilerParams(dimension_semantics=("parallel",)),
    )(w, g, m, v)


def _place():
    return lax.axis_index("x"), lax.axis_index("y"), lax.axis_index("c")


def _other_chips(x, y):
    return [(1 - x, y), (x, 1 - y), (1 - x, 1 - y)]


_ANY = pl.BlockSpec(memory_space=pl.ANY)


class _Plan:
    def __init__(self, ins, out_shapes, n_remote, n_local, issue):
        self.ins = list(ins)
        self.out_shapes = list(out_shapes)
        self.issue = issue
        self.scratch = [pltpu.SemaphoreType.DMA((max(n_remote, 1),)),
                        pltpu.SemaphoreType.DMA((max(n_remote, 1),)),
                        pltpu.SemaphoreType.DMA((max(n_local, 1),))]

    def start(self, ins, outs, sems):
        sends, _, locs = self.issue(ins, outs, *sems)
        for cp in locs + sends:
            cp.start()

    def wait(self, ins, outs, sems):
        sends, recvs, locs = self.issue(ins, outs, *sems)
        for make in recvs:
            make().wait_recv()
        for cp in sends:
            cp.wait_send()
        for cp in locs:
            cp.wait()


def _plan_parts(plan):
    if plan is None:
        return [], [], [], []
    return ([_ANY] * len(plan.ins), plan.out_shapes, [_ANY] * len(plan.out_shapes), plan.scratch)


def _run_plan(plan, name):
    n_in, n_out = len(plan.ins), len(plan.out_shapes)

    def body(*refs):
        ins, outs, sems = refs[:n_in], refs[n_in:n_in + n_out], refs[n_in + n_out:]
        plan.start(ins, outs, sems)
        plan.wait(ins, outs, sems)

    return pl.pallas_call(
        body, name=name, in_specs=[_ANY] * n_in, out_specs=[_ANY] * n_out,
        out_shape=plan.out_shapes, scratch_shapes=plan.scratch,
    )(*plan.ins)


def _gather_plan(shards):
    n = len(shards)

    def issue(ins, outs, send_sems, recv_sems, local_sems):
        x, y, c = _place()
        me = 2 * x + y
        sends, recvs, locs = [], [], []
        for p in range(n):
            locs.append(pltpu.make_async_copy(ins[p], outs[p].at[me], local_sems.at[p]))
            for k, (px, py) in enumerate(_other_chips(x, y)):
                sems = dict(send_sem=send_sems.at[3 * p + k], recv_sem=recv_sems.at[3 * p + k],
                            device_id=(px, py, c), device_id_type=MESH)
                sends.append(pltpu.make_async_remote_copy(src_ref=ins[p], dst_ref=outs[p].at[me], **sems))
                recvs.append(functools.partial(pltpu.make_async_remote_copy, src_ref=ins[p],
                                               dst_ref=outs[p].at[2 * px + py], **sems))
        return sends, recvs, locs

    return _Plan(shards, [jax.ShapeDtypeStruct((N_CHIPS,) + s.shape, s.dtype) for s in shards], 3 * n, n, issue)


_REL7 = [(fx, fy, fc) for fx in (0, 1) for fy in (0, 1) for fc in (0, 1)][1:]


def _scatter8_plan(gs):
    n = len(gs)

    def issue(ins, outs, send_sems, recv_sems, local_sems):
        x, y, c = _place()
        sends = []
        for p in range(n):
            hr = gs[p].shape[1] // 2
            for k, (fx, fy, fc) in enumerate(_REL7):
                tx, ty, tc = x ^ fx, y ^ fy, c ^ fc
                src = ins[p].at[2 * tx + ty, pl.ds(pl.multiple_of(tc * hr, SUBLANES), hr), :]
                sends.append(pltpu.make_async_remote_copy(
                    src_ref=src, dst_ref=outs[p].at[k],
                    send_sem=send_sems.at[7 * p + k], recv_sem=recv_sems.at[7 * p + k],
                    device_id=(tx, ty, tc), device_id_type=MESH))
        return sends, [functools.partial(lambda cp: cp, cp) for cp in sends], []

    shapes = [jax.ShapeDtypeStruct((7, g.shape[1] // 2, g.shape[2]), g.dtype) for g in gs]
    return _Plan(gs, shapes, 7 * n, 0, issue)


def _sibling_plan(ts):
    n = len(ts)

    def issue(ins, outs, send_sems, recv_sems, local_sems):
        x, y, c = _place()
        sends = [pltpu.make_async_remote_copy(
            src_ref=ins[p], dst_ref=outs[p], send_sem=send_sems.at[p], recv_sem=recv_sems.at[p],
            device_id=(x, y, 1 - c), device_id_type=MESH) for p in range(n)]
        return sends, [functools.partial(lambda cp: cp, cp) for cp in sends], []

    return _Plan(ts, [jax.ShapeDtypeStruct(t.shape, t.dtype) for t in ts], n, 0, issue)


def _add8(g, recv, chip, core, name):
    s, r, n = g.shape
    hr = r // 2
    th = hr // 2 if (hr // 2) % SUBLANES == 0 else hr
    nt = hr // th

    def body(chip_ref, core_ref, g_ref, r_ref, o_ref):
        acc = g_ref[0]
        for k in range(7):
            acc = acc + r_ref[k]
        o_ref[...] = acc

    return pl.pallas_call(
        body, name=name,
        grid_spec=pltpu.PrefetchScalarGridSpec(
            num_scalar_prefetch=2, grid=(nt,),
            in_specs=[pl.BlockSpec((1, th, n), lambda i, ch, co: (ch[0], co[0] * nt + i, 0)),
                      pl.BlockSpec((7, th, n), lambda i, ch, co: (0, i, 0))],
            out_specs=pl.BlockSpec((th, n), lambda i, ch, co: (i, 0))),
        out_shape=jax.ShapeDtypeStruct((hr, n), F32),
        compiler_params=pltpu.CompilerParams(dimension_semantics=("parallel",)),
    )(chip, core, g, recv)


def _adamw_halves(w, own, other, m, v, core, name):
    r, n = w.shape
    hr = r // 2
    th = hr // 2 if (hr // 2) % SUBLANES == 0 else hr
    tph = hr // th
    c1 = 1.0 / (1.0 - ADAM_B1 ** ADAM_STEP)
    c2 = 1.0 / (1.0 - ADAM_B2 ** ADAM_STEP)

    def body(core_ref, w_ref, a_ref, b_ref, m_ref, v_ref, g_ref, d_ref, nm_ref, nv_ref):
        half = pl.program_id(0) // tph
        gv = jnp.where(half == core_ref[0], a_ref[...], b_ref[...])
        nm = ADAM_B1 * m_ref[...] + (1.0 - ADAM_B1) * gv
        nv = ADAM_B2 * v_ref[...] + (1.0 - ADAM_B2) * (gv * gv)
        g_ref[...] = gv
        nm_ref[...] = nm
        nv_ref[...] = nv
        d_ref[...] = -ADAM_LR * ((nm * c1) / (jnp.sqrt(nv * c2) + ADAM_EPS) + ADAM_WD * w_ref[...])

    full = pl.BlockSpec((th, n), lambda i, co: (i, 0))
    part = pl.BlockSpec((th, n), lambda i, co: (i % tph, 0))
    return pl.pallas_call(
        body, name=name,
        grid_spec=pltpu.PrefetchScalarGridSpec(
            num_scalar_prefetch=1, grid=(2 * tph,),
            in_specs=[full, part, part, full, full], out_specs=[full] * 4),
        out_shape=[jax.ShapeDtypeStruct((r, n), F32)] * 4,
        compiler_params=pltpu.CompilerParams(dimension_semantics=("parallel",)),
    )(core, w, own, other, m, v)


def _allreduce_small(pack, name):
    r, l = pack.shape

    def body(p_ref, o_ref, land, send_sems, recv_sems):
        x, y, c = _place()
        me = 4 * x + 2 * y + c
        land[me] = p_ref[...]
        rel = [(fx, fy, fc) for fx in (0, 1) for fy in (0, 1) for fc in (0, 1)][1:]
        sends = []
        for k, (fx, fy, fc) in enumerate(rel):
            peer = (x ^ fx, y ^ fy, c ^ fc)
            cp = pltpu.make_async_remote_copy(
                src_ref=p_ref, dst_ref=land.at[me], send_sem=send_sems.at[k], recv_sem=recv_sems.at[k],
                device_id=peer, device_id_type=MESH)
            cp.start()
            sends.append(cp)
        for k, (fx, fy, fc) in enumerate(rel):
            src = 4 * (x ^ fx) + 2 * (y ^ fy) + (c ^ fc)
            pltpu.make_async_remote_copy(
                src_ref=p_ref, dst_ref=land.at[src], send_sem=send_sems.at[k], recv_sem=recv_sems.at[k],
                device_id=(x ^ fx, y ^ fy, c ^ fc), device_id_type=MESH).wait_recv()
        for cp in sends:
            cp.wait_send()
        acc = land[0]
        for d in range(1, N_DEV):
            acc = acc + land[d]
        o_ref[...] = acc

    vm = pl.BlockSpec(memory_space=pltpu.VMEM)
    return pl.pallas_call(
        body, name=name, in_specs=[vm], out_specs=vm,
        out_shape=jax.ShapeDtypeStruct((r, l), F32),
        scratch_shapes=[pltpu.VMEM((N_DEV, r, l), F32),
                        pltpu.SemaphoreType.DMA((N_DEV - 1,)), pltpu.SemaphoreType.DMA((N_DEV - 1,))],
    )(pack)


def _flat_rows(a, mult=SUBLANES * LANES):
    f = a.reshape(-1)
    padn = (-f.shape[0]) % mult
    if padn:
        f = jnp.concatenate([f, jnp.zeros((padn,), f.dtype)])
    return f


def _pack(arrs, mult=SUBLANES * LANES, total_mult=None):
    flat = [_flat_rows(a, mult) for a in arrs]
    sizes = [f.shape[0] for f in flat]
    if total_mult is not None:
        padn = (-sum(sizes)) % total_mult
        if padn:
            flat.append(jnp.zeros((padn,), flat[0].dtype))
    return jnp.concatenate(flat).reshape(-1, LANES), sizes


def _unpack(pack, shapes, sizes, lead=()):
    flat = pack.reshape(lead + (-1,))
    out, off = [], 0
    for shp, sz in zip(shapes, sizes):
        n = math.prod(shp)
        out.append(flat[..., off:off + n].reshape(lead + tuple(shp)))
        off += sz
    return out


def _cols_from_shards(g):
    s, k, n = g.shape
    return jnp.transpose(g, (1, 0, 2)).reshape(k, s * n)


def _cols_to_shards(w, s=N_CHIPS):
    k, n = w.shape
    return jnp.transpose(w.reshape(k, s, n // s), (1, 0, 2))


def _ffn_fwd(h, pre, post, w_up, cw, cb, w_down, tag):
    u = _rmsnorm_fwd(h, pre, f"{tag}_prenorm")
    hp = _mm_nn_sh(u, w_up, 2 * D_FF, f"{tag}_up")
    hc = _conv_fwd(hp, 0, 2 * D_FF, cw, cb, f"{tag}_conv")
    act = _ffn_act_fwd(hc, f"{tag}_act")
    o = _mm_nn(act, w_down, F32, f"{tag}_down")
    hn = _postnorm_res_fwd(h, o, post, f"{tag}_postnorm")
    return hn, (h, u, hp, hc, act, o)


def _ffn_bwd(dh, saved, pre, post, w_up, cw, w_down, tag):
    h, u, hp, hc, act, o = saved
    do, dpost = _postnorm_bwd(o, post, dh, f"{tag}_postnorm_bwd")
    dact = _mm_nt(do, w_down, f"{tag}_down_dx")
    dw_down = _mm_tn(act, do, f"{tag}_down_dw")
    dhc = _ffn_act_bwd(hc, dact, f"{tag}_act_bwd")
    dhp, dcw, dcb = _conv_bwd(hp, 0, 2 * D_FF, dhc, cw, f"{tag}_conv_bwd")
    du = _mm_nt_sh(dhp, w_up, f"{tag}_up_dx")
    dw_up = _mm_tn_sh(u, dhp, w_up.shape[2], f"{tag}_up_dw")
    dhn, dpre = _prenorm_bwd(h, pre, du, dh, f"{tag}_prenorm_bwd")
    return dhn, dict(pre=dpre, post=dpost, w_up=dw_up, conv_w=dcw[:3], conv_b=dcb, w_down=dw_down)


class _Exchange:
    AFTER_L1_FFN = ("l1_ffn_w_up", "l1_ffn_w_down")
    AFTER_L0_OUT = ("l1_w_out", "l1_w_in", "l0_ffn_w_up", "l0_ffn_w_down", "l0_w_out")
    LAST = ("l0_w_in",)

    def __init__(self, late_shards):
        self.late = dict(late_shards)
        self.slabs = {}
        self.recv = {}

    def gather_plan(self):
        return _gather_plan(list(self.late.values()))

    def gathered(self, outs):
        return {n: (g if n in _BIG_COL else g.reshape(-1, g.shape[-1])) for n, g in zip(self.late, outs)}

    def scatter_plan(self, grads, names):
        for n in names:
            g = grads[n]
            self.slabs[n] = g if n in _BIG_COL else g.reshape(N_CHIPS, -1, g.shape[-1])
        return _scatter8_plan([self.slabs[n] for n in names])

    def scattered(self, names, outs):
        self.recv.update(zip(names, outs))


def _local_step(x, tgt, meta, P, ex=None):
    seq, d = x.shape
    lp = seq + BLOCK
    h0 = jnp.concatenate([jnp.zeros((PAD, d), F32), meta, x], axis=0)
    tgt_p = jnp.concatenate([jnp.zeros((BLOCK, d), F32), tgt], axis=0)

    u0 = _rmsnorm_fwd(h0, P["l0_mix_pre_norm"], "l0_mix_prenorm")
    proj0 = _mm_nn_sh(u0, P["l0_w_in"], EVEN_IN, "l0_in")
    xrc = _conv_fwd(proj0, D_MODEL, D_MODEL, P["l0_lru_conv_w"], P["l0_lru_conv_b"], "l0_lru_conv")
    lru_args = (P["l0_lru_w_a"], P["l0_lru_w_x"], P["l0_lru_b_a"], P["l0_lru_b_x"], P["l0_lru_lambda"])
    ya, hl = _lru_fwd(proj0, xrc, *lru_args, "l0_lru")
    yb, outs = _attn_fwd(proj0, P["l0_attn_sinks"], "l0_attn", ex.gather_plan() if ex else None)
    if ex:
        P = {**P, **ex.gathered(outs)}
    mix0 = jnp.concatenate([ya, yb], axis=1)
    o0 = _mm_nn(mix0, P["l0_w_out"], F32, "l0_out")
    h1 = _postnorm_res_fwd(h0, o0, P["l0_mix_post_norm"], "l0_mix_postnorm")
    h2, ffn0 = _ffn_fwd(h1, P["l0_ffn_pre_norm"], P["l0_ffn_post_norm"], P["l0_ffn_w_up"],
                        P["l0_ffn_conv_w"], P["l0_ffn_conv_b"], P["l0_ffn_w_down"], "l0_ffn")
    u2 = _rmsnorm_fwd(h2, P["l1_mix_pre_norm"], "l1_mix_prenorm")
    proj1 = _mm_nn_sh(u2, P["l1_w_in"], ODD_IN_PAD, "l1_in")
    xc1 = _conv_fwd(proj1, _ZW, _XBC_W, P["l1_ssm_conv_w"], P["l1_ssm_conv_b"], "l1_ssm_conv")
    xbc, dt = _ssm_prep_fwd(xc1, proj1, P["l1_dt_bias"], "l1_ssm_prep")
    yssd, states = _ssd_fwd(xbc, dt, P["l1_a_log"], "l1_ssd")
    yn = _ssm_gate_fwd(yssd, xbc, proj1, P["l1_d_skip"], P["l1_gate_norm"], "l1_ssm_gate")
    o1 = _mm_nn(yn, P["l1_w_out"], F32, "l1_out")
    h3 = _postnorm_res_fwd(h2, o1, P["l1_mix_post_norm"], "l1_mix_postnorm")
    h4, ffn1 = _ffn_fwd(h3, P["l1_ffn_pre_norm"], P["l1_ffn_post_norm"], P["l1_ffn_w_up"],
                        P["l1_ffn_conv_w"], P["l1_ffn_conv_b"], P["l1_ffn_w_down"], "l1_ffn")
    dh4, loss_cols = _loss_fwd_bwd(h4, tgt_p, "loss")

    G = {}
    dh3, g = _ffn_bwd(dh4, ffn1, P["l1_ffn_pre_norm"], P["l1_ffn_post_norm"], P["l1_ffn_w_up"],
                      P["l1_ffn_conv_w"], P["l1_ffn_w_down"], "l1_ffn")
    for k, v in g.items():
        G["l1_ffn_" + (k + "_norm" if k in ("pre", "post") else k)] = v
    do1, G["l1_mix_post_norm"] = _postnorm_bwd(o1, P["l1_mix_post_norm"], dh3, "l1_mix_postnorm_bwd")
    dyn = _mm_nt(do1, P["l1_w_out"], "l1_out_dx")
    G["l1_w_out"] = _mm_tn(yn, do1, "l1_out_dw")
    dyssd, dxskip, dz, dd_cols, G["l1_gate_norm"] = _ssm_gate_bwd(
        yssd, xbc, proj1, P["l1_d_skip"], P["l1_gate_norm"], dyn, "l1_ssm_gate_bwd")
    G["l1_d_skip"] = dd_cols.reshape(SSD_HEADS, SSD_P).sum(axis=1)
    (dxs, dbm, dcm, ddt, dalog), outs = _ssd_bwd(
        xbc, dt, P["l1_a_log"], states, dyssd, "l1_ssd_bwd",
        ex.scatter_plan(G, ex.AFTER_L1_FFN) if ex else None)
    if ex:
        ex.scattered(ex.AFTER_L1_FFN, outs)
    G["l1_a_log"] = dalog[0, :SSD_HEADS]
    dxc, ddtr, dbias = _ssm_prep_bwd(xc1, proj1, P["l1_dt_bias"], dxs, dxskip, dbm, dcm, ddt,
                                     "l1_ssm_prep_bwd")
    G["l1_dt_bias"] = dbias[0, :SSD_HEADS]
    dxbc, dcw, dcb = _conv_bwd(proj1, _ZW, _XBC_W, dxc, P["l1_ssm_conv_w"], "l1_ssm_conv_bwd")
    G["l1_ssm_conv_w"] = dcw[:4]
    G["l1_ssm_conv_b"] = dcb
    dproj1 = jnp.concatenate([dz, dxbc, ddtr], axis=1)
    du2 = _mm_nt_sh(dproj1, P["l1_w_in"], "l1_in_dx")
    G["l1_w_in"] = _mm_tn_sh(u2, dproj1, ODD_IN // N_CHIPS, "l1_in_dw")
    dh2, G["l1_mix_pre_norm"] = _prenorm_bwd(h2, P["l1_mix_pre_norm"], du2, dh3, "l1_mix_prenorm_bwd")
    dh1, g = _ffn_bwd(dh2, ffn0, P["l0_ffn_pre_norm"], P["l0_ffn_post_norm"], P["l0_ffn_w_up"],
                      P["l0_ffn_conv_w"], P["l0_ffn_w_down"], "l0_ffn")
    for k, v in g.items():
        G["l0_ffn_" + (k + "_norm" if k in ("pre", "post") else k)] = v
    do0, G["l0_mix_post_norm"] = _postnorm_bwd(o0, P["l0_mix_post_norm"], dh1, "l0_mix_postnorm_bwd")
    dmix = _mm_nt(do0, P["l0_w_out"], "l0_out_dx")
    G["l0_w_out"] = _mm_tn(mix0, do0, "l0_out_dw")
    (dgate, dxrc, G["l0_lru_w_a"], G["l0_lru_w_x"], G["l0_lru_b_a"], G["l0_lru_b_x"],
     G["l0_lru_lambda"]) = _lru_bwd(proj0, xrc, hl, dmix, *lru_args, "l0_lru_bwd")
    dxr, dcw, dcb = _conv_bwd(proj0, D_MODEL, D_MODEL, dxrc, P["l0_lru_conv_w"], "l0_lru_conv_bwd")
    G["l0_lru_conv_w"] = dcw[:4]
    G["l0_lru_conv_b"] = dcb
    (dq, dk, dv, G["l0_attn_sinks"]), outs = _attn_bwd(
        proj0, P["l0_attn_sinks"], dmix, "l0_attn_bwd",
        ex.scatter_plan(G, ex.AFTER_L0_OUT) if ex else None)
    if ex:
        ex.scattered(ex.AFTER_L0_OUT, outs)
    dproj0 = jnp.concatenate([dgate, dxr, dq, dk.astype(BF16), dv.astype(BF16)], axis=1)
    du0 = _mm_nt_sh(dproj0, P["l0_w_in"], "l0_in_dx")
    G["l0_w_in"] = _mm_tn_sh(u0, dproj0, EVEN_IN // N_CHIPS, "l0_in_dw")
    dh0, G["l0_mix_pre_norm"] = _prenorm_bwd(h0, P["l0_mix_pre_norm"], du0, dh1, "l0_mix_prenorm_bwd")
    return loss_cols, dh0[BLOCK:], dh0[PAD:BLOCK], G


_BIG_COL = ("l0_w_in", "l0_ffn_w_up", "l1_w_in", "l1_ffn_w_up")
_BIG_ROW = ("l0_w_out", "l0_ffn_w_down", "l1_w_out", "l1_ffn_w_down")
_BIG = ("l0_w_in", "l0_w_out", "l0_ffn_w_up", "l0_ffn_w_down",
        "l1_w_in", "l1_w_out", "l1_ffn_w_up", "l1_ffn_w_down")
_SMALL_SHARDED = ("meta_tokens", "l0_lru_conv_w", "l0_ffn_conv_w", "l1_ssm_conv_w", "l1_ffn_conv_w")
_WEIGHTS = ("meta_tokens", "l0_mix_pre_norm", "l0_mix_post_norm", "l0_w_in", "l0_lru_conv_w",
            "l0_lru_conv_b", "l0_lru_w_a", "l0_lru_b_a", "l0_lru_w_x", "l0_lru_b_x", "l0_lru_lambda",
            "l0_attn_sinks", "l0_w_out", "l0_ffn_pre_norm", "l0_ffn_post_norm", "l0_ffn_w_up",
            "l0_ffn_conv_w", "l0_ffn_conv_b", "l0_ffn_w_down", "l1_mix_pre_norm", "l1_mix_post_norm",
            "l1_w_in", "l1_ssm_conv_w", "l1_ssm_conv_b", "l1_dt_bias", "l1_a_log", "l1_d_skip",
            "l1_gate_norm", "l1_w_out", "l1_ffn_pre_norm", "l1_ffn_post_norm", "l1_ffn_w_up",
            "l1_ffn_conv_w", "l1_ffn_conv_b", "l1_ffn_w_down")
_REPL = tuple(n for n in _WEIGHTS if n not in _BIG and n not in _SMALL_SHARDED)


def _pad_lanes(v, n=LANES):
    return jnp.concatenate([v, jnp.zeros((n - v.shape[0],), v.dtype)]).reshape(1, n)


def _step(x, tgt, W, M, V):
    cx, cy, cc = _place()
    chip = 2 * cx + cy

    small_pack, small_sizes = _pack([W[n] for n in _SMALL_SHARDED])
    first = _run_plan(_gather_plan([W["l0_w_in"].astype(BF16), small_pack]), "gather_first")
    small_full = _unpack(first[1], [W[n].shape for n in _SMALL_SHARDED], small_sizes, lead=(N_CHIPS,))
    ex = _Exchange({n: W[n].astype(BF16) for n in _BIG if n != "l0_w_in"})

    P = {"l0_w_in": first[0]}
    for n, g in zip(_SMALL_SHARDED, small_full):
        P[n] = _cols_from_shards(g)
    for n in _REPL:
        v = W[n]
        P[n] = v.reshape(1, -1) if v.ndim == 1 else v
    P["l0_lru_w_a"] = W["l0_lru_w_a"].astype(BF16)
    P["l0_lru_w_x"] = W["l0_lru_w_x"].astype(BF16)
    P["l1_dt_bias"] = _pad_lanes(W["l1_dt_bias"])
    P["l1_a_log"] = _pad_lanes(W["l1_a_log"])
    P["l1_d_skip"] = jnp.repeat(W["l1_d_skip"], SSD_P).reshape(1, D_SSM)
    meta = P.pop("meta_tokens")

    loss_cols, grad_x, grad_meta, G = _local_step(x, tgt, meta, P, ex)
    G["meta_tokens"] = grad_meta

    ex.scattered(ex.LAST, _run_plan(ex.scatter_plan(G, ex.LAST), "grad_scatter_last"))
    core_idx = cc.astype(jnp.int32).reshape(1)
    chip_idx = chip.astype(jnp.int32).reshape(1)
    own_half = [_add8(ex.slabs[n], ex.recv[n], chip_idx, core_idx, f"grad_sum_{n}") for n in _BIG]
    other_half = _run_plan(_sibling_plan(own_half), "grad_sibling_swap")
    small_names = list(_REPL) + list(_SMALL_SHARDED)
    small_list = [G[n] for n in small_names] + [loss_cols]
    spack, ssizes = _pack(small_list)
    sred = _allreduce_small(spack, "small_allreduce")
    sfull = _unpack(sred, [a.shape for a in small_list], ssizes)
    loss = 0.5 / D_MODEL * jnp.sum(sfull[-1])
    small_grads = {}
    for n, g in zip(small_names, sfull[:-1]):
        if n in _SMALL_SHARDED:
            wcols = W[n].shape[1]
            g = lax.dynamic_slice_in_dim(g, chip * wcols, wcols, axis=1)
        small_grads[n] = g.reshape(W[n].shape)

    grads, delta, new_m, new_v = {}, {}, {}, {}
    for n, own, other in zip(_BIG, own_half, other_half):
        grads[n], delta[n], new_m[n], new_v[n] = _adamw_halves(
            W[n], own, other, M[n], V[n], core_idx, f"adamw_{n}")
    s_names = [n for n in _WEIGHTS if n not in _BIG]
    tile_elems = 512 * LANES
    wp, wsz = _pack([W[n] for n in s_names], total_mult=tile_elems)
    gp, _ = _pack([small_grads[n] for n in s_names], total_mult=tile_elems)
    mp, _ = _pack([M[n] for n in s_names], total_mult=tile_elems)
    vp, _ = _pack([V[n] for n in s_names], total_mult=tile_elems)
    dp, nmp, nvp = _adamw(wp, gp, mp, vp, "adamw_small")
    shapes = [W[n].shape for n in s_names]
    for n, a, b, c_ in zip(s_names, _unpack(dp, shapes, wsz), _unpack(nmp, shapes, wsz),
                           _unpack(nvp, shapes, wsz)):
        grads[n] = small_grads[n]
        delta[n], new_m[n], new_v[n] = a, b, c_
    return loss, grad_x, grads, delta, new_m, new_v


def kernel(x, meta_tokens, l0_mix_pre_norm, l0_mix_post_norm, l0_w_in, l0_lru_conv_w, l0_lru_conv_b, l0_lru_w_a, l0_lru_b_a, l0_lru_w_x, l0_lru_b_x, l0_lru_lambda, l0_attn_sinks, l0_w_out, l0_ffn_pre_norm, l0_ffn_post_norm, l0_ffn_w_up, l0_ffn_conv_w, l0_ffn_conv_b, l0_ffn_w_down, l1_mix_pre_norm, l1_mix_post_norm, l1_w_in, l1_ssm_conv_w, l1_ssm_conv_b, l1_dt_bias, l1_a_log, l1_d_skip, l1_gate_norm, l1_w_out, l1_ffn_pre_norm, l1_ffn_post_norm, l1_ffn_w_up, l1_ffn_conv_w, l1_ffn_conv_b, l1_ffn_w_down, loss_target, m_meta_tokens, m_l0_mix_pre_norm, m_l0_mix_post_norm, m_l0_w_in, m_l0_lru_conv_w, m_l0_lru_conv_b, m_l0_lru_w_a, m_l0_lru_b_a, m_l0_lru_w_x, m_l0_lru_b_x, m_l0_lru_lambda, m_l0_attn_sinks, m_l0_w_out, m_l0_ffn_pre_norm, m_l0_ffn_post_norm, m_l0_ffn_w_up, m_l0_ffn_conv_w, m_l0_ffn_conv_b, m_l0_ffn_w_down, m_l1_mix_pre_norm, m_l1_mix_post_norm, m_l1_w_in, m_l1_ssm_conv_w, m_l1_ssm_conv_b, m_l1_dt_bias, m_l1_a_log, m_l1_d_skip, m_l1_gate_norm, m_l1_w_out, m_l1_ffn_pre_norm, m_l1_ffn_post_norm, m_l1_ffn_w_up, m_l1_ffn_conv_w, m_l1_ffn_conv_b, m_l1_ffn_w_down, v_meta_tokens, v_l0_mix_pre_norm, v_l0_mix_post_norm, v_l0_w_in, v_l0_lru_conv_w, v_l0_lru_conv_b, v_l0_lru_w_a, v_l0_lru_b_a, v_l0_lru_w_x, v_l0_lru_b_x, v_l0_lru_lambda, v_l0_attn_sinks, v_l0_w_out, v_l0_ffn_pre_norm, v_l0_ffn_post_norm, v_l0_ffn_w_up, v_l0_ffn_conv_w, v_l0_ffn_conv_b, v_l0_ffn_w_down, v_l1_mix_pre_norm, v_l1_mix_post_norm, v_l1_w_in, v_l1_ssm_conv_w, v_l1_ssm_conv_b, v_l1_dt_bias, v_l1_a_log, v_l1_d_skip, v_l1_gate_norm, v_l1_w_out, v_l1_ffn_pre_norm, v_l1_ffn_post_norm, v_l1_ffn_w_up, v_l1_ffn_conv_w, v_l1_ffn_conv_b, v_l1_ffn_w_down):
    args = locals()
    W = {n: args[n] for n in _WEIGHTS}
    M = {n: args["m_" + n] for n in _WEIGHTS}
    V = {n: args["v_" + n] for n in _WEIGHTS}
    loss, grad_x, grads, delta, new_m, new_v = _step(x[0], loss_target[0], W, M, V)
    return (loss, grad_x[None], *[grads[n] for n in _WEIGHTS], *[delta[n] for n in _WEIGHTS],
            *[new_m[n] for n in _WEIGHTS], *[new_v[n] for n in _WEIGHTS])
```

```python
import functools
import math

import jax
import jax.numpy as jnp
from jax import lax
from jax.experimental import pallas as pl
from jax.experimental.pallas import tpu as pltpu

F32 = jnp.float32
BF16 = jnp.bfloat16

D_MODEL = 1024
N_META = 16
BLOCK = 128
PAD = BLOCK - N_META
EPS = 1e-6
LRU_BLOCKS = 8
LRU_BS = 128
LRU_C = 8.0
N_Q_HEADS = 16
N_KV_HEADS = 2
HEAD_DIM = 64
Q_PER_KV = 8
WINDOW = 128
D_SSM = 2048
SSD_HEADS = 32
SSD_GROUPS = 8
SSD_HPG = 4
SSD_P = 64
SSD_N = 128
D_FF = 2816
NEG = -1e30
LANES = 128
SUBLANES = 8

ADAM_LR = 0.001
ADAM_B1 = 0.9
ADAM_B2 = 0.999
ADAM_EPS = 1e-08
ADAM_WD = 0.01
ADAM_STEP = 10

MESH = pl.DeviceIdType.MESH
N_CHIPS = 4
N_DEV = 8


def _pick(n, cands):
    for c in cands:
        if n % c == 0:
            return c
    raise ValueError(f"no tile for {n} in {cands}")


def _col_tile(n, limit=1792):
    best = None
    for t in range(LANES, min(n, limit) + 1, LANES):
        if n % t == 0:
            best = t
    if best is None:
        raise ValueError(f"no lane tile for {n}")
    return best


def _sigmoid(x):
    return 1.0 / (1.0 + jnp.exp(-x))


def _log1p(e):
    u = 1.0 + e
    return jnp.where(u == 1.0, e, jnp.log(u) * (e / jnp.where(u == 1.0, 1.0, u - 1.0)))


def _softplus(x):
    return jnp.maximum(x, 0.0) + _log1p(jnp.exp(-jnp.abs(x)))


def _neg_expm1(x):
    poly = x * (1.0 + x * (0.5 + x * (1.0 / 6.0 + x * (1.0 / 24.0 + x * (1.0 / 120.0)))))
    return -jnp.where(x > -0.05, poly, jnp.exp(x) - 1.0)


_GELU_C = math.sqrt(2.0 / math.pi)


def _gelu(x):
    t = jnp.tanh(_GELU_C * (x + 0.044715 * x * x * x))
    return 0.5 * x * (1.0 + t)


def _gelu_and_grad(x):
    x2 = x * x
    t = jnp.tanh(_GELU_C * (x + 0.044715 * x * x2))
    g = 0.5 * x * (1.0 + t)
    dg = 0.5 * (1.0 + t) + 0.5 * x * (1.0 - t * t) * _GELU_C * (1.0 + 3.0 * 0.044715 * x2)
    return g, dg


def _silu_and_grad(x):
    s = _sigmoid(x)
    return x * s, s * (1.0 + x * (1.0 - s))


def _dot(a, b):
    return jnp.dot(a, b, preferred_element_type=F32)


def _dot_nt(a, b):
    return lax.dot_general(a, b, (((1,), (1,)), ((), ())), preferred_element_type=F32)


def _dot_tn(a, b):
    return lax.dot_general(a, b, (((0,), (0,)), ((), ())), preferred_element_type=F32)


def _row_iota(t):
    return lax.broadcasted_iota(jnp.int32, (t, 1), 0)


def _scan_fwd(a, u, t):
    row = _row_iota(t)
    d = 1
    while d < t:
        m = row >= d
        u_sh = jnp.where(m, pltpu.roll(u, d, 0), 0.0)
        a_sh = jnp.where(m, pltpu.roll(a, d, 0), 1.0)
        u = u + a * u_sh
        a = a * a_sh
        d *= 2
    return a, u


def _scan_rev(c, x, t):
    row = _row_iota(t)
    d = 1
    while d < t:
        m = row < t - d
        x_sh = jnp.where(m, pltpu.roll(x, t - d, 0), 0.0)
        c_sh = jnp.where(m, pltpu.roll(c, t - d, 0), 1.0)
        x = x + c * x_sh
        c = c * c_sh
        d *= 2
    return c, x


def _cumsum_rows(x, t):
    row = _row_iota(t)
    d = 1
    while d < t:
        x = x + jnp.where(row >= d, pltpu.roll(x, d, 0), 0.0)
        d *= 2
    return x


def _rev_cumsum_rows(x, t):
    row = _row_iota(t)
    d = 1
    while d < t:
        x = x + jnp.where(row < t - d, pltpu.roll(x, t - d, 0), 0.0)
        d *= 2
    return x


def _rms_bwd(x, g, dy):
    rs = lax.rsqrt(jnp.mean(x * x, axis=-1, keepdims=True) + EPS)
    gy = dy * g
    dx = rs * gy - x * (rs * rs * rs) * jnp.mean(x * gy, axis=-1, keepdims=True)
    return dx, dy * x * rs


def _mm_nn(a, w, out_dtype, name):
    m, k = a.shape
    n = w.shape[1]
    tm = _pick(m, (640, 512, 256, 128))
    tn = _col_tile(n)

    def body(a_ref, w_ref, o_ref):
        o_ref[...] = _dot(a_ref[...].astype(BF16), w_ref[...]).astype(o_ref.dtype)

    return pl.pallas_call(
        body, name=name, grid=(n // tn, m // tm),
        in_specs=[pl.BlockSpec((tm, k), lambda j, i: (i, 0)),
                  pl.BlockSpec((k, tn), lambda j, i: (0, j))],
        out_specs=pl.BlockSpec((tm, tn), lambda j, i: (i, j)),
        out_shape=jax.ShapeDtypeStruct((m, n), out_dtype),
        compiler_params=pltpu.CompilerParams(dimension_semantics=("parallel", "parallel")),
    )(a, w)


def _mm_nt(dy, w, name):
    m, n = dy.shape
    k = w.shape[0]
    wide = n > 3328
    tm = _pick(m, (320, 256, 128)) if wide else _pick(m, (640, 512, 256, 128))
    tk = _col_tile(k, 512 if wide else 1408)

    def body(dy_ref, w_ref, o_ref):
        o_ref[...] = _dot_nt(dy_ref[...].astype(BF16), w_ref[...])

    return pl.pallas_call(
        body, name=name, grid=(k // tk, m // tm),
        in_specs=[pl.BlockSpec((tm, n), lambda j, i: (i, 0)),
                  pl.BlockSpec((tk, n), lambda j, i: (j, 0))],
        out_specs=pl.BlockSpec((tm, tk), lambda j, i: (i, j)),
        out_shape=jax.ShapeDtypeStruct((m, k), F32),
        compiler_params=pltpu.CompilerParams(dimension_semantics=("parallel", "parallel")),
    )(dy, w)


def _mm_tn(a, dy, name):
    m, k = a.shape
    n = dy.shape[1]
    tm = _pick(m, (640, 512, 256, 128))
    tk = _col_tile(k, 1408)
    tn = _col_tile(n, 1664)
    nsteps = m // tm

    def body(a_ref, dy_ref, o_ref, acc):
        @pl.when(pl.program_id(2) == 0)
        def _():
            acc[...] = jnp.zeros_like(acc)

        acc[...] += _dot_tn(a_ref[...].astype(BF16), dy_ref[...].astype(BF16))

        @pl.when(pl.program_id(2) == nsteps - 1)
        def _():
            o_ref[...] = acc[...].astype(o_ref.dtype)

    return pl.pallas_call(
        body, name=name, grid=(k // tk, n // tn, nsteps),
        in_specs=[pl.BlockSpec((tm, tk), lambda kk, j, i: (i, kk)),
                  pl.BlockSpec((tm, tn), lambda kk, j, i: (i, j))],
        out_specs=pl.BlockSpec((tk, tn), lambda kk, j, i: (kk, j)),
        out_shape=jax.ShapeDtypeStruct((k, n), BF16),
        scratch_shapes=[pltpu.VMEM((tk, tn), F32)],
        compiler_params=pltpu.CompilerParams(
            dimension_semantics=("parallel", "parallel", "arbitrary")),
    )(a, dy)


def _mm_nn_sh(a, w4, n_out, name):
    m, k = a.shape
    s, _, n = w4.shape
    tm = _pick(m, (320, 256, 128))

    def body(a_ref, w_ref, o_ref):
        av = a_ref[...].astype(BF16)
        for j in range(s):
            o_ref[:, j * n:(j + 1) * n] = _dot(av, w_ref[j])
        if n_out > s * n:
            o_ref[:, s * n:] = jnp.zeros((tm, n_out - s * n), F32)

    return pl.pallas_call(
        body, name=name, grid=(m // tm,),
        in_specs=[pl.BlockSpec((tm, k), lambda i: (i, 0)),
                  pl.BlockSpec((s, k, n), lambda i: (0, 0, 0))],
        out_specs=pl.BlockSpec((tm, n_out), lambda i: (i, 0)),
        out_shape=jax.ShapeDtypeStruct((m, n_out), F32),
        compiler_params=pltpu.CompilerParams(dimension_semantics=("parallel",)),
    )(a, w4)


def _mm_nt_sh(dy, w4, name):
    m, nn = dy.shape
    s, k, n = w4.shape
    tm = _pick(m, (640, 512, 256, 128))
    tk = _col_tile(k, 256)

    def body(dy_ref, w_ref, o_ref):
        acc = _dot_nt(dy_ref[:, 0:n].astype(BF16), w_ref[0])
        for j in range(1, s):
            acc = acc + _dot_nt(dy_ref[:, j * n:(j + 1) * n].astype(BF16), w_ref[j])
        o_ref[...] = acc

    return pl.pallas_call(
        body, name=name, grid=(k // tk, m // tm),
        in_specs=[pl.BlockSpec((tm, nn), lambda j, i: (i, 0)),
                  pl.BlockSpec((s, tk, n), lambda j, i: (0, j, 0))],
        out_specs=pl.BlockSpec((tm, tk), lambda j, i: (i, j)),
        out_shape=jax.ShapeDtypeStruct((m, k), F32),
        compiler_params=pltpu.CompilerParams(dimension_semantics=("parallel", "parallel")),
    )(dy, w4)


def _mm_tn_sh(a, dy, n, name):
    m, k = a.shape
    nn = dy.shape[1]
    s = N_CHIPS
    tm = _pick(m, (640, 512, 256, 128))
    tk = _col_tile(k, 512 if n <= 1024 else 256)
    nsteps = m // tm

    def body(a_ref, dy_ref, o_ref, acc):
        @pl.when(pl.program_id(1) == 0)
        def _():
            acc[...] = jnp.zeros_like(acc)

        av = a_ref[...].astype(BF16)
        for j in range(s):
            acc[j] += _dot_tn(av, dy_ref[:, j * n:(j + 1) * n].astype(BF16))

        @pl.when(pl.program_id(1) == nsteps - 1)
        def _():
            o_ref[...] = acc[...].astype(o_ref.dtype)

    return pl.pallas_call(
        body, name=name, grid=(k // tk, nsteps),
        in_specs=[pl.BlockSpec((tm, tk), lambda kk, i: (i, kk)),
                  pl.BlockSpec((tm, nn), lambda kk, i: (i, 0))],
        out_specs=pl.BlockSpec((s, tk, n), lambda kk, i: (0, kk, 0)),
        out_shape=jax.ShapeDtypeStruct((s, k, n), BF16),
        scratch_shapes=[pltpu.VMEM((s, tk, n), F32)],
        compiler_params=pltpu.CompilerParams(dimension_semantics=("parallel", "arbitrary")),
    )(a, dy)


def _rowcall(name, body, lp, tm, rows=(), prevs=(), vecs=(), outs=(), accs=(), scratch=(),
             reverse=False, seq=False):
    nt = lp // tm
    hb = tm // SUBLANES

    def ri(i):
        return nt - 1 - i if reverse else i

    in_specs, args = [], []
    for arr, w, cb in rows:
        in_specs.append(pl.BlockSpec((tm, w), lambda i, cb=cb: (ri(i), cb)))
        args.append(arr)
    for arr, w, cb in prevs:
        in_specs.append(pl.BlockSpec((SUBLANES, w), lambda i, cb=cb: (jnp.maximum(ri(i) * hb - 1, 0), cb)))
        args.append(arr)
    for arr in vecs:
        in_specs.append(pl.BlockSpec(arr.shape, lambda i, nd=arr.ndim: (0,) * nd))
        args.append(arr)
    out_shape, out_specs = [], []
    for w, dt in outs:
        out_shape.append(jax.ShapeDtypeStruct((lp, w), dt))
        out_specs.append(pl.BlockSpec((tm, w), lambda i: (ri(i), 0)))
    for shp, dt in accs:
        out_shape.append(jax.ShapeDtypeStruct(shp, dt))
        out_specs.append(pl.BlockSpec(shp, lambda i, nd=len(shp): (0,) * nd))

    def kern(*refs):
        i = pl.program_id(0)
        body(ri(i), i == 0, *refs)

    sem = ("arbitrary",) if (seq or accs) else ("parallel",)
    res = pl.pallas_call(
        kern, name=name, grid=(nt,), in_specs=in_specs, out_specs=out_specs,
        out_shape=out_shape, scratch_shapes=list(scratch),
        compiler_params=pltpu.CompilerParams(dimension_semantics=sem),
    )(*args)
    return res


def _acc_add(first, ref, val):
    @pl.when(first)
    def _():
        ref[...] = jnp.zeros_like(ref)

    ref[...] += val


def _real_rows(r, tm):
    return (r * tm + _row_iota(tm)) >= PAD


def _rmsnorm_fwd(h, g, name):
    lp, d = h.shape
    tm = _pick(lp, (640, 512, 256, 128))

    def body(r, first, h_ref, g_ref, u_ref):
        x = h_ref[...]
        rs = lax.rsqrt(jnp.mean(x * x, axis=-1, keepdims=True) + EPS)
        u_ref[...] = (x * rs * g_ref[...]).astype(u_ref.dtype)

    return _rowcall(name, body, lp, tm, rows=[(h, d, 0)], vecs=[g], outs=[(d, BF16)])[0]


def _postnorm_res_fwd(h, o, g, name):
    lp, d = h.shape
    tm = _pick(lp, (640, 512, 256, 128))

    def body(r, first, h_ref, o_ref, g_ref, out_ref):
        x = o_ref[...]
        rs = lax.rsqrt(jnp.mean(x * x, axis=-1, keepdims=True) + EPS)
        out_ref[...] = jnp.where(_real_rows(r, tm), h_ref[...] + x * rs * g_ref[...], 0.0)

    return _rowcall(name, body, lp, tm, rows=[(h, d, 0), (o, d, 0)], vecs=[g], outs=[(d, F32)])[0]


def _postnorm_bwd(o, g, dh, name):
    lp, d = o.shape
    tm = _pick(lp, (640, 512, 256, 128))

    def body(r, first, o_ref, dh_ref, g_ref, do_ref, dg_ref):
        dx, dgt = _rms_bwd(o_ref[...], g_ref[...], dh_ref[...])
        do_ref[...] = dx.astype(do_ref.dtype)
        _acc_add(first, dg_ref, jnp.sum(dgt, axis=0, keepdims=True))

    return _rowcall(name, body, lp, tm, rows=[(o, d, 0), (dh, d, 0)], vecs=[g],
                    outs=[(d, BF16)], accs=[((1, d), F32)])


def _prenorm_bwd(h, g, du, dh_res, name):
    lp, d = h.shape
    tm = _pick(lp, (640, 512, 256, 128))

    def body(r, first, h_ref, du_ref, dres_ref, g_ref, dh_ref, dg_ref):
        dx, dgt = _rms_bwd(h_ref[...], g_ref[...], du_ref[...])
        dh_ref[...] = jnp.where(_real_rows(r, tm), dres_ref[...] + dx, 0.0)
        _acc_add(first, dg_ref, jnp.sum(dgt, axis=0, keepdims=True))

    return _rowcall(name, body, lp, tm, rows=[(h, d, 0), (du, d, 0), (dh_res, d, 0)], vecs=[g],
                    outs=[(d, F32)], accs=[((1, d), F32)])


def _loss_fwd_bwd(h, tgt, name):
    lp, d = h.shape
    tm = _pick(lp, (640, 512, 256, 128))

    def body(r, first, h_ref, t_ref, dh_ref, ls_ref):
        tok = (r * tm + _row_iota(tm)) >= BLOCK
        e = jnp.where(tok, h_ref[...] - t_ref[...], 0.0)
        dh_ref[...] = e * (1.0 / d)
        _acc_add(first, ls_ref, jnp.sum(e * e, axis=0, keepdims=True))

    return _rowcall(name, body, lp, tm, rows=[(h, d, 0), (tgt, d, 0)],
                    outs=[(d, F32)], accs=[((1, d), F32)])


def _conv_tiles(lp, width):
    wc = _col_tile(width, 1408)
    tm = _pick(lp, (320, 256, 128))
    return tm, wc


def _conv_fwd(x, col_off, width, w, b, name):
    lp = x.shape[0]
    kk = w.shape[0]
    tm, wc = _conv_tiles(lp, width)
    offb = col_off // wc
    assert col_off % wc == 0
    hb = tm // SUBLANES

    def body(x_ref, xp_ref, w_ref, b_ref, y_ref):
        i = pl.program_id(1)
        xv = x_ref[...]
        halo = jnp.where(i > 0, xp_ref[...], 0.0)
        xx = jnp.concatenate([halo, xv], axis=0)
        acc = b_ref[...] + w_ref[kk - 1:kk, :] * xv
        for j in range(1, kk):
            acc = acc + w_ref[kk - 1 - j:kk - j, :] * pltpu.roll(xx, j, 0)[SUBLANES:, :]
        y_ref[...] = acc

    return pl.pallas_call(
        body, name=name, grid=(width // wc, lp // tm),
        in_specs=[pl.BlockSpec((tm, wc), lambda j, i: (i, offb + j)),
                  pl.BlockSpec((SUBLANES, wc), lambda j, i: (jnp.maximum(i * hb - 1, 0), offb + j)),
                  pl.BlockSpec((kk, wc), lambda j, i: (0, j)),
                  pl.BlockSpec((1, wc), lambda j, i: (0, j))],
        out_specs=pl.BlockSpec((tm, wc), lambda j, i: (i, j)),
        out_shape=jax.ShapeDtypeStruct((lp, width), F32),
        compiler_params=pltpu.CompilerParams(dimension_semantics=("parallel", "parallel")),
    )(x, x, w, b)


def _conv_bwd(x, col_off, width, dy, w, name):
    lp = x.shape[0]
    kk = w.shape[0]
    tm, wc = _conv_tiles(lp, width)
    offb = col_off // wc
    assert col_off % wc == 0
    hb = tm // SUBLANES
    nb8 = lp // SUBLANES

    def body(x_ref, xp_ref, dy_ref, dn_ref, w_ref, dx_ref, dw_ref, db_ref):
        i = pl.program_id(1)
        last = pl.num_programs(1) - 1
        xv = x_ref[...]
        dyv = dy_ref[...]
        xx = jnp.concatenate([jnp.where(i > 0, xp_ref[...], 0.0), xv], axis=0)
        dd = jnp.concatenate([dyv, jnp.where(i < last, dn_ref[...], 0.0)], axis=0)
        dx = w_ref[kk - 1:kk, :] * dyv
        rows = [jnp.sum(dyv * xv, axis=0, keepdims=True)]
        for m in range(1, kk):
            dx = dx + w_ref[kk - 1 - m:kk - m, :] * pltpu.roll(dd, tm + SUBLANES - m, 0)[:tm, :]
            rows.append(jnp.sum(dyv * pltpu.roll(xx, m, 0)[SUBLANES:, :], axis=0, keepdims=True))
        dx_ref[...] = dx.astype(dx_ref.dtype)
        dwp = jnp.concatenate(rows[::-1] + [jnp.zeros((SUBLANES - kk, wc), F32)], axis=0)

        @pl.when(i == 0)
        def _():
            dw_ref[...] = jnp.zeros_like(dw_ref)
            db_ref[...] = jnp.zeros_like(db_ref)

        dw_ref[...] += dwp
        db_ref[...] += jnp.sum(dyv, axis=0, keepdims=True)

    return pl.pallas_call(
        body, name=name, grid=(width // wc, lp // tm),
        in_specs=[pl.BlockSpec((tm, wc), lambda j, i: (i, offb + j)),
                  pl.BlockSpec((SUBLANES, wc), lambda j, i: (jnp.maximum(i * hb - 1, 0), offb + j)),
                  pl.BlockSpec((tm, wc), lambda j, i: (i, j)),
                  pl.BlockSpec((SUBLANES, wc), lambda j, i: (jnp.minimum((i + 1) * hb, nb8 - 1), j)),
                  pl.BlockSpec((kk, wc), lambda j, i: (0, j))],
        out_specs=[pl.BlockSpec((tm, wc), lambda j, i: (i, j)),
                   pl.BlockSpec((SUBLANES, wc), lambda j, i: (0, j)),
                   pl.BlockSpec((1, wc), lambda j, i: (0, j))],
        out_shape=[jax.ShapeDtypeStruct((lp, width), BF16),
                   jax.ShapeDtypeStruct((SUBLANES, width), F32),
                   jax.ShapeDtypeStruct((1, width), F32)],
        compiler_params=pltpu.CompilerParams(dimension_semantics=("parallel", "arbitrary")),
    )(x, x, dy, dy, w)


def _ffn_act_fwd(hc, name):
    lp = hc.shape[0]
    tm = _pick(lp, (320, 256, 128))

    def body(r, first, g_ref, u_ref, a_ref):
        a_ref[...] = (_gelu(g_ref[...]) * u_ref[...]).astype(a_ref.dtype)

    return _rowcall(name, body, lp, tm, rows=[(hc, D_FF, 0), (hc, D_FF, 1)], outs=[(D_FF, BF16)])[0]


def _ffn_act_bwd(hc, dact, name):
    lp = hc.shape[0]
    tm = BLOCK

    def body(r, first, g_ref, u_ref, da_ref, dh_ref):
        gl, dgl = _gelu_and_grad(g_ref[...])
        da = da_ref[...]
        dh_ref[:, :D_FF] = da * u_ref[...] * dgl
        dh_ref[:, D_FF:] = da * gl

    return _rowcall(name, body, lp, tm, rows=[(hc, D_FF, 0), (hc, D_FF, 1), (dact, D_FF, 0)],
                    outs=[(2 * D_FF, F32)])[0]


def _lru_gates(x, wa_ref, wx_ref, ba, bx, lam):
    xb = x.astype(BF16)
    za, zx = [], []
    for n in range(LRU_BLOCKS):
        xs = xb[:, n * LRU_BS:(n + 1) * LRU_BS]
        za.append(_dot(xs, wa_ref[n]))
        zx.append(_dot(xs, wx_ref[n]))
    r = _sigmoid(jnp.concatenate(za, axis=1) + ba)
    ig = _sigmoid(jnp.concatenate(zx, axis=1) + bx)
    sp = _softplus(-lam)
    log_a = -LRU_C * r * sp
    a = jnp.exp(log_a)
    om = _neg_expm1(2.0 * log_a)
    mult = jnp.sqrt(om)
    return xb, r, ig, sp, a, om, mult


def _lru_fwd(proj, xrc, wa, wx, ba, bx, lam, name):
    lp, d = xrc.shape
    tm = BLOCK

    def body(r_idx, first, gate_ref, x_ref, wa_ref, wx_ref, ba_ref, bx_ref, lam_ref,
             y_ref, h_ref, carry):
        @pl.when(first)
        def _():
            carry[...] = jnp.zeros_like(carry)

        x = x_ref[...]
        _, _, ig, _, a, _, mult = _lru_gates(x, wa_ref, wx_ref, ba_ref[...], bx_ref[...], lam_ref[...])
        u = jnp.where(_real_rows(r_idx, tm), mult * ig * x, 0.0)
        acum, hloc = _scan_fwd(a, u, tm)
        h = hloc + acum * carry[0:1, :]
        h_ref[...] = h
        carry[0:1, :] = h[tm - 1:tm, :]
        y_ref[...] = (_gelu(gate_ref[...]) * h).astype(y_ref.dtype)

    return _rowcall(name, body, lp, tm, rows=[(proj, d, 0), (xrc, d, 0)],
                    vecs=[wa, wx, ba, bx, lam], outs=[(d, BF16), (d, F32)],
                    scratch=[pltpu.VMEM((SUBLANES, d), F32)], seq=True)


def _lru_bwd(proj, xrc, hl, dmix, wa, wx, ba, bx, lam, name):
    lp, d = xrc.shape
    tm = BLOCK

    def body(r_idx, first, gate_ref, x_ref, h_ref, dy_ref, hp_ref, wa_ref, wx_ref, ba_ref, bx_ref,
             lam_ref, dgate_ref, dx_ref, dwa_ref, dwx_ref, dba_ref, dbx_ref, dlam_ref, carry):
        @pl.when(first)
        def _():
            carry[...] = jnp.zeros_like(carry)
            dwa_ref[...] = jnp.zeros_like(dwa_ref)
            dwx_ref[...] = jnp.zeros_like(dwx_ref)
            dba_ref[...] = jnp.zeros_like(dba_ref)
            dbx_ref[...] = jnp.zeros_like(dbx_ref)
            dlam_ref[...] = jnp.zeros_like(dlam_ref)

        x = x_ref[...]
        lam = lam_ref[...]
        xb, r, ig, sp, a, om, mult = _lru_gates(x, wa_ref, wx_ref, ba_ref[...], bx_ref[...], lam)
        h = h_ref[...]
        dy = dy_ref[...]
        gl, dgl = _gelu_and_grad(gate_ref[...])
        dgate_ref[...] = (dy * h * dgl).astype(dgate_ref.dtype)
        row = _row_iota(tm)
        lastrow = row == tm - 1
        xg = dy * gl + jnp.where(lastrow, carry[0:1, :], 0.0)
        c = jnp.where(lastrow, 1.0, pltpu.roll(a, tm - 1, 0))
        _, g = _scan_rev(c, xg, tm)
        carry[0:1, :] = a[0:1, :] * g[0:1, :]
        hprev_in = jnp.where(r_idx > 0, hp_ref[SUBLANES - 1:SUBLANES, :], 0.0)
        hprev = jnp.where(row == 0, hprev_in, pltpu.roll(h, 1, 0))
        du = jnp.where(_real_rows(r_idx, tm), g, 0.0)
        da = g * hprev
        dmult = du * ig * x
        dig = du * mult * x
        dxv = du * mult * ig
        e2 = 1.0 - om
        dlog_a = da * a - dmult * e2 / mult
        dr = dlog_a * (-LRU_C) * sp
        dsp = jnp.sum(dlog_a * (-LRU_C) * r, axis=0, keepdims=True)
        dlam_ref[...] += -dsp * _sigmoid(-lam)
        dza = dr * r * (1.0 - r)
        dzx = dig * ig * (1.0 - ig)
        dba_ref[...] += jnp.sum(dza, axis=0, keepdims=True)
        dbx_ref[...] += jnp.sum(dzx, axis=0, keepdims=True)
        dzab = dza.astype(BF16)
        dzxb = dzx.astype(BF16)
        parts = []
        for n in range(LRU_BLOCKS):
            sl = slice(n * LRU_BS, (n + 1) * LRU_BS)
            dwa_ref[n] += _dot_tn(xb[:, sl], dzab[:, sl])
            dwx_ref[n] += _dot_tn(xb[:, sl], dzxb[:, sl])
            parts.append(_dot_nt(dzab[:, sl], wa_ref[n]) + _dot_nt(dzxb[:, sl], wx_ref[n]))
        dx_ref[...] = dxv + jnp.concatenate(parts, axis=1)

    return _rowcall(name, body, lp, tm,
                    rows=[(proj, d, 0), (xrc, d, 0), (hl, d, 0), (dmix, d, 0)],
                    prevs=[(hl, d, 0)], vecs=[wa, wx, ba, bx, lam],
                    outs=[(d, BF16), (d, F32)],
                    accs=[((LRU_BLOCKS, LRU_BS, LRU_BS), F32), ((LRU_BLOCKS, LRU_BS, LRU_BS), F32),
                          ((1, d), F32), ((1, d), F32), ((1, d), F32)],
                    scratch=[pltpu.VMEM((SUBLANES, d), F32)], reverse=True, seq=True)


_SLOPES = [2.0 ** (-8.0 * (h + 1) / N_Q_HEADS) for h in range(N_Q_HEADS)]
_QK_SCALE = HEAD_DIM ** -0.5
_QCOL = 2 * D_MODEL // D_MODEL
_KCOL = (3 * D_MODEL) // LANES
_VCOL = _KCOL + 1


def _attn_masks(n):
    start = pl.multiple_of(jnp.maximum(n - 1, 0) * BLOCK, BLOCK)
    qi = n * BLOCK + lax.broadcasted_iota(jnp.int32, (BLOCK, 2 * BLOCK), 0)
    kj = start + lax.broadcasted_iota(jnp.int32, (BLOCK, 2 * BLOCK), 1)
    dist = qi - kj
    ok = (kj >= BLOCK) & (dist >= 0) & (dist < WINDOW)
    dm = (n * BLOCK - PAD + lax.broadcasted_iota(jnp.int32, (BLOCK, N_META), 0)
          - lax.broadcasted_iota(jnp.int32, (BLOCK, N_META), 1))
    okm = dm >= 0
    return start, ok, dist.astype(F32), okm, jnp.minimum(dm, WINDOW).astype(F32)


def _group_rows(ref, g):
    return jnp.concatenate(
        [ref[:, (g * Q_PER_KV + hh) * HEAD_DIM:(g * Q_PER_KV + hh + 1) * HEAD_DIM] for hh in range(Q_PER_KV)],
        axis=0).astype(BF16)


def _attn_probs(qg, kg, kmg, sink_ref, g, ok, distf, okm, dmf):
    slope = jnp.stack([jnp.full((1, 1), _SLOPES[g * Q_PER_KV + hh], F32) for hh in range(Q_PER_KV)])
    sink = jnp.stack([sink_ref[0:1, g * Q_PER_KV + hh:g * Q_PER_KV + hh + 1] for hh in range(Q_PER_KV)])
    s = (_dot_nt(qg, kg) * _QK_SCALE).reshape(Q_PER_KV, BLOCK, 2 * BLOCK)
    sm = (_dot_nt(qg, kmg) * _QK_SCALE).reshape(Q_PER_KV, BLOCK, N_META)
    s = jnp.where(ok[None], s - slope * distf[None], NEG)
    sm = jnp.where(okm[None], sm - slope * dmf[None], NEG)
    mx = jnp.maximum(jnp.maximum(jnp.max(s, axis=-1, keepdims=True),
                                 jnp.max(sm, axis=-1, keepdims=True)), sink)
    p = jnp.exp(s - mx)
    pm = jnp.exp(sm - mx)
    ps = jnp.exp(sink - mx)
    inv = 1.0 / (jnp.sum(p, axis=-1, keepdims=True) + jnp.sum(pm, axis=-1, keepdims=True) + ps)
    return p * inv, pm * inv, ps * inv


def _attn_fwd(proj, sinks, name, plan=None):
    lp = proj.shape[0]
    nblk = lp // BLOCK
    p_in, p_shapes, p_out, p_scr = _plan_parts(plan)

    def body(*refs):
        q_ref, k_ref, v_ref, sink_ref = refs[:4]
        cins = refs[4:4 + len(p_in)]
        o_ref = refs[4 + len(p_in)]
        couts = refs[5 + len(p_in):5 + len(p_in) + len(p_out)]
        sems = refs[5 + len(p_in) + len(p_out):]
        n = pl.program_id(0)
        if plan is not None:
            @pl.when(n == 0)
            def _():
                plan.start(cins, couts, sems)

        start, ok, distf, okm, dmf = _attn_masks(n)
        kb = k_ref[pl.ds(start, 2 * BLOCK), :].astype(BF16)
        vb = v_ref[pl.ds(start, 2 * BLOCK), :].astype(BF16)
        km = k_ref[PAD:BLOCK, :].astype(BF16)
        vm = v_ref[PAD:BLOCK, :].astype(BF16)
        for g in range(N_KV_HEADS):
            gs = slice(g * HEAD_DIM, (g + 1) * HEAD_DIM)
            pn, pmn, _ = _attn_probs(_group_rows(q_ref, g), kb[:, gs], km[:, gs], sink_ref, g,
                                     ok, distf, okm, dmf)
            o = (_dot(pn.astype(BF16).reshape(Q_PER_KV * BLOCK, 2 * BLOCK), vb[:, gs])
                 + _dot(pmn.astype(BF16).reshape(Q_PER_KV * BLOCK, N_META), vm[:, gs]))
            for hh in range(Q_PER_KV):
                h = g * Q_PER_KV + hh
                o_ref[:, h * HEAD_DIM:(h + 1) * HEAD_DIM] = o[hh * BLOCK:(hh + 1) * BLOCK, :].astype(o_ref.dtype)
        if plan is not None:
            @pl.when(n == nblk - 1)
            def _():
                plan.wait(cins, couts, sems)

    res = pl.pallas_call(
        body, name=name, grid=(nblk,),
        in_specs=[pl.BlockSpec((BLOCK, D_MODEL), lambda n: (n, _QCOL)),
                  pl.BlockSpec((lp, LANES), lambda n: (0, _KCOL)),
                  pl.BlockSpec((lp, LANES), lambda n: (0, _VCOL)),
                  pl.BlockSpec(sinks.shape, lambda n: (0, 0))] + p_in,
        out_specs=[pl.BlockSpec((BLOCK, D_MODEL), lambda n: (n, 0))] + p_out,
        out_shape=[jax.ShapeDtypeStruct((lp, D_MODEL), BF16)] + p_shapes,
        scratch_shapes=p_scr,
        compiler_params=pltpu.CompilerParams(dimension_semantics=("arbitrary",)),
    )(proj, proj, proj, sinks, *(plan.ins if plan is not None else []))
    return res[0], res[1:]


def _attn_bwd(proj, sinks, dmix, name, plan=None):
    lp = proj.shape[0]
    nblk = lp // BLOCK

    p_in, p_shapes, p_out, p_scr = _plan_parts(plan)

    def body(*refs):
        q_ref, k_ref, v_ref, sink_ref, dy_ref = refs[:5]
        cins = refs[5:5 + len(p_in)]
        dq_ref, dk_ref, dv_ref, ds_ref = refs[5 + len(p_in):9 + len(p_in)]
        couts = refs[9 + len(p_in):9 + len(p_in) + len(p_out)]
        sems = refs[9 + len(p_in) + len(p_out):]
        n = pl.program_id(0)

        @pl.when(n == 0)
        def _():
            dk_ref[...] = jnp.zeros_like(dk_ref)
            dv_ref[...] = jnp.zeros_like(dv_ref)
            ds_ref[...] = jnp.zeros_like(ds_ref)
            if plan is not None:
                plan.start(cins, couts, sems)

        start, ok, distf, okm, dmf = _attn_masks(n)
        kb = k_ref[pl.ds(start, 2 * BLOCK), :].astype(BF16)
        vb = v_ref[pl.ds(start, 2 * BLOCK), :].astype(BF16)
        km = k_ref[PAD:BLOCK, :].astype(BF16)
        vm = v_ref[PAD:BLOCK, :].astype(BF16)
        lane16 = lax.broadcasted_iota(jnp.int32, (1, N_Q_HEADS), 1)
        dsink = jnp.zeros((1, N_Q_HEADS), F32)
        rows = Q_PER_KV * BLOCK
        for g in range(N_KV_HEADS):
            gs = slice(g * HEAD_DIM, (g + 1) * HEAD_DIM)
            qg = _group_rows(q_ref, g)
            dog = _group_rows(dy_ref, g)
            pn, pmn, psn = _attn_probs(qg, kb[:, gs], km[:, gs], sink_ref, g, ok, distf, okm, dmf)
            dp = _dot_nt(dog, vb[:, gs]).reshape(Q_PER_KV, BLOCK, 2 * BLOCK)
            dpm = _dot_nt(dog, vm[:, gs]).reshape(Q_PER_KV, BLOCK, N_META)
            delta = (jnp.sum(pn * dp, axis=-1, keepdims=True)
                     + jnp.sum(pmn * dpm, axis=-1, keepdims=True))
            dsb = (pn * (dp - delta)).astype(BF16).reshape(rows, 2 * BLOCK)
            dsm = (pmn * (dpm - delta)).astype(BF16).reshape(rows, N_META)
            dsk = jnp.sum(psn * delta, axis=1, keepdims=True)
            for hh in range(Q_PER_KV):
                dsink = dsink - jnp.where(lane16 == g * Q_PER_KV + hh, dsk[hh], 0.0)
            dq = (_dot(dsb, kb[:, gs]) + _dot(dsm, km[:, gs])) * _QK_SCALE
            for hh in range(Q_PER_KV):
                h = g * Q_PER_KV + hh
                dq_ref[:, h * HEAD_DIM:(h + 1) * HEAD_DIM] = dq[hh * BLOCK:(hh + 1) * BLOCK, :].astype(dq_ref.dtype)
            pnb = pn.astype(BF16).reshape(rows, 2 * BLOCK)
            pmnb = pmn.astype(BF16).reshape(rows, N_META)
            dk_ref[pl.ds(start, 2 * BLOCK), gs] += _dot_tn(dsb, qg) * _QK_SCALE
            dv_ref[pl.ds(start, 2 * BLOCK), gs] += _dot_tn(pnb, dog)
            dk_ref[PAD:BLOCK, gs] += _dot_tn(dsm, qg) * _QK_SCALE
            dv_ref[PAD:BLOCK, gs] += _dot_tn(pmnb, dog)
        ds_ref[...] += dsink
        if plan is not None:
            @pl.when(n == nblk - 1)
            def _():
                plan.wait(cins, couts, sems)

    res = pl.pallas_call(
        body, name=name, grid=(nblk,),
        in_specs=[pl.BlockSpec((BLOCK, D_MODEL), lambda n: (n, _QCOL)),
                  pl.BlockSpec((lp, LANES), lambda n: (0, _KCOL)),
                  pl.BlockSpec((lp, LANES), lambda n: (0, _VCOL)),
                  pl.BlockSpec(sinks.shape, lambda n: (0, 0)),
                  pl.BlockSpec((BLOCK, D_MODEL), lambda n: (n, 1))] + p_in,
        out_specs=[pl.BlockSpec((BLOCK, D_MODEL), lambda n: (n, 0)),
                   pl.BlockSpec((lp, LANES), lambda n: (0, 0)),
                   pl.BlockSpec((lp, LANES), lambda n: (0, 0)),
                   pl.BlockSpec((1, N_Q_HEADS), lambda n: (0, 0))] + p_out,
        out_shape=[jax.ShapeDtypeStruct((lp, D_MODEL), BF16),
                   jax.ShapeDtypeStruct((lp, LANES), F32),
                   jax.ShapeDtypeStruct((lp, LANES), F32),
                   jax.ShapeDtypeStruct((1, N_Q_HEADS), F32)] + p_shapes,
        scratch_shapes=p_scr,
        compiler_params=pltpu.CompilerParams(dimension_semantics=("arbitrary",)),
    )(proj, proj, proj, sinks, dmix, *(plan.ins if plan is not None else []))
    return res[:4], res[4:]


_ZW = D_SSM
_XBC_W = D_SSM + 2 * SSD_GROUPS * SSD_N
_DT_COL = (_ZW + _XBC_W) // LANES
EVEN_IN = 3 * D_MODEL + 2 * LANES
ODD_IN = _ZW + _XBC_W + SSD_HEADS
ODD_IN_PAD = _ZW + _XBC_W + LANES


def _ssm_prep_fwd(xc, proj, dt_bias, name):
    lp = xc.shape[0]
    tm = _pick(lp, (320, 256, 128))

    def body(r, first, xc_ref, dtr_ref, b_ref, act_ref, dt_ref):
        real = _real_rows(r, tm)
        act, _ = _silu_and_grad(xc_ref[...])
        act_ref[...] = jnp.where(real, act, 0.0)
        dt_ref[...] = jnp.where(real, _softplus(dtr_ref[...] + b_ref[...]), 0.0)

    return _rowcall(name, body, lp, tm, rows=[(xc, _XBC_W, 0), (proj, LANES, _DT_COL)],
                    vecs=[dt_bias], outs=[(_XBC_W, F32), (LANES, F32)])


def _ssm_prep_bwd(xc, proj, dt_bias, dxs, dxskip, db, dc, ddt, name):
    lp = xc.shape[0]
    tm = BLOCK

    def body(r, first, xc_ref, dtr_ref, dxs_ref, dsk_ref, db_ref, dc_ref, ddt_ref, b_ref,
             dxc_ref, ddtr_ref, dbias_ref):
        real = _real_rows(r, tm)
        _, ds = _silu_and_grad(xc_ref[...])
        dxc_ref[:, :D_SSM] = jnp.where(real, (dxs_ref[...] + dsk_ref[...]) * ds[:, :D_SSM], 0.0)
        dxc_ref[:, D_SSM:D_SSM + 1024] = jnp.where(real, db_ref[...] * ds[:, D_SSM:D_SSM + 1024], 0.0)
        dxc_ref[:, D_SSM + 1024:] = jnp.where(real, dc_ref[...] * ds[:, D_SSM + 1024:], 0.0)
        dd = jnp.where(real, ddt_ref[...] * _sigmoid(dtr_ref[...] + b_ref[...]), 0.0)
        ddtr_ref[...] = dd.astype(ddtr_ref.dtype)
        _acc_add(first, dbias_ref, jnp.sum(dd, axis=0, keepdims=True))

    return _rowcall(name, body, lp, tm,
                    rows=[(xc, _XBC_W, 0), (proj, LANES, _DT_COL), (dxs, D_SSM, 0), (dxskip, D_SSM, 0),
                          (db, 1024, 0), (dc, 1024, 0), (ddt, LANES, 0)],
                    vecs=[dt_bias], outs=[(_XBC_W, F32), (LANES, BF16)], accs=[((1, LANES), F32)])


def _ssd_common(dt, alog):
    a = -jnp.exp(alog)
    cs = _cumsum_rows(dt * a, BLOCK)
    cst = cs.T
    cl = cs[BLOCK - 1:BLOCK, :]
    tril = (lax.broadcasted_iota(jnp.int32, (BLOCK, BLOCK), 0)
            >= lax.broadcasted_iota(jnp.int32, (BLOCK, BLOCK), 1))
    return a, cs, cst, cl, jnp.exp(cs), jnp.exp(cl - cs), jnp.exp(cl), tril


def _head_cols(ecl, g):
    lane = lax.broadcasted_iota(jnp.int32, (1, SSD_HPG * SSD_P), 1)
    e = [ecl[:, SSD_HPG * g + hh:SSD_HPG * g + hh + 1] for hh in range(SSD_HPG)]
    return jnp.where(lane < SSD_P, e[0], jnp.where(lane < 2 * SSD_P, e[1],
                                                   jnp.where(lane < 3 * SSD_P, e[2], e[3])))


def _ssd_fwd(xbc, dt, alog, name, plan=None):
    lp = xbc.shape[0]
    nc = lp // BLOCK
    gw = SSD_HPG * SSD_P
    p_in, p_shapes, p_out, p_scr = _plan_parts(plan)

    def body(*refs):
        xs_ref, b_ref, c_ref, dt_ref, alog_ref = refs[:5]
        cins = refs[5:5 + len(p_in)]
        y_ref, so_ref = refs[5 + len(p_in):7 + len(p_in)]
        couts = refs[7 + len(p_in):7 + len(p_in) + len(p_out)]
        st, fx = refs[7 + len(p_in) + len(p_out):9 + len(p_in) + len(p_out)]
        sems = refs[9 + len(p_in) + len(p_out):]
        n = pl.program_id(0)

        @pl.when(n == 0)
        def _():
            st[...] = jnp.zeros_like(st)
            if plan is not None:
                plan.start(cins, couts, sems)

        dtv = dt_ref[...]
        _, cs, cst, cl, e, f, ecl, tril = _ssd_common(dtv, alog_ref[...])
        for g in range(SSD_GROUPS):
            bg = b_ref[:, g * SSD_N:(g + 1) * SSD_N].astype(BF16)
            cg = c_ref[:, g * SSD_N:(g + 1) * SSD_N].astype(BF16)
            gm = _dot_nt(cg, bg)
            stg = st[g]
            so_ref[0, g] = stg
            yoff = _dot(cg, stg.astype(BF16))
            for hh in range(SSD_HPG):
                h = SSD_HPG * g + hh
                hs = slice(h * SSD_P, (h + 1) * SSD_P)
                seg = cs[:, h:h + 1] - cst[h:h + 1, :]
                m = gm * jnp.exp(jnp.where(tril, seg, NEG))
                xdt = xs_ref[:, hs] * dtv[:, h:h + 1]
                y_ref[:, hs] = (_dot(m.astype(BF16), xdt.astype(BF16))
                                + e[:, h:h + 1] * yoff[:, hh * SSD_P:(hh + 1) * SSD_P])
                fx[:, hh * SSD_P:(hh + 1) * SSD_P] = f[:, h:h + 1] * xdt
            st[g] = stg * _head_cols(ecl, g) + _dot_tn(bg, fx[...].astype(BF16))
        if plan is not None:
            @pl.when(n == nc - 1)
            def _():
                plan.wait(cins, couts, sems)

    res = pl.pallas_call(
        body, name=name, grid=(nc,),
        in_specs=[pl.BlockSpec((BLOCK, D_SSM), lambda n: (n, 0)),
                  pl.BlockSpec((BLOCK, 1024), lambda n: (n, 2)),
                  pl.BlockSpec((BLOCK, 1024), lambda n: (n, 3)),
                  pl.BlockSpec((BLOCK, LANES), lambda n: (n, 0)),
                  pl.BlockSpec((1, LANES), lambda n: (0, 0))] + p_in,
        out_specs=[pl.BlockSpec((BLOCK, D_SSM), lambda n: (n, 0)),
                   pl.BlockSpec((1, SSD_GROUPS, SSD_N, gw), lambda n: (n, 0, 0, 0))] + p_out,
        out_shape=[jax.ShapeDtypeStruct((lp, D_SSM), F32),
                   jax.ShapeDtypeStruct((nc, SSD_GROUPS, SSD_N, gw), F32)] + p_shapes,
        scratch_shapes=[pltpu.VMEM((SSD_GROUPS, SSD_N, gw), F32), pltpu.VMEM((BLOCK, gw), F32)] + p_scr,
        compiler_params=pltpu.CompilerParams(dimension_semantics=("arbitrary",)),
    )(xbc, xbc, xbc, dt, alog, *(plan.ins if plan is not None else []))
    return res[:2], res[2:]


def _ssd_bwd(xbc, dt, alog, states, dy, name, plan=None):
    lp = xbc.shape[0]
    nc = lp // BLOCK
    gw = SSD_HPG * SSD_P
    p_in, p_shapes, p_out, p_scr = _plan_parts(plan)

    def body(*refs):
        xs_ref, b_ref, c_ref, dt_ref, alog_ref, dy_ref, st_ref = refs[:7]
        cins = refs[7:7 + len(p_in)]
        dxs_ref, db_ref, dc_ref, ddt_ref, dalog_ref = refs[7 + len(p_in):12 + len(p_in)]
        couts = refs[12 + len(p_in):12 + len(p_in) + len(p_out)]
        dst, edy, fx = refs[12 + len(p_in) + len(p_out):15 + len(p_in) + len(p_out)]
        sems = refs[15 + len(p_in) + len(p_out):]
        i = pl.program_id(0)

        @pl.when(i == 0)
        def _():
            dst[...] = jnp.zeros_like(dst)
            dalog_ref[...] = jnp.zeros_like(dalog_ref)
            if plan is not None:
                plan.start(cins, couts, sems)

        dtv = dt_ref[...]
        a, cs, cst, cl, e, f, ecl, tril = _ssd_common(dtv, alog_ref[...])
        lane = lax.broadcasted_iota(jnp.int32, (1, LANES), 1)
        sub = _row_iota(BLOCK)
        dcs = jnp.zeros((BLOCK, LANES), F32)
        dcst = jnp.zeros((LANES, BLOCK), F32)
        dcl = jnp.zeros((1, LANES), F32)
        ddtx = jnp.zeros((BLOCK, LANES), F32)
        for g in range(SSD_GROUPS):
            bg = b_ref[:, g * SSD_N:(g + 1) * SSD_N].astype(BF16)
            cg = c_ref[:, g * SSD_N:(g + 1) * SSD_N].astype(BF16)
            gm = _dot_nt(cg, bg)
            stg = st_ref[0, g]
            stb = stg.astype(BF16)
            dso = dst[g]
            dsob = dso.astype(BF16)
            yraw = _dot(cg, stb)
            dfx = _dot(bg, dsob)
            prodsum = jnp.sum(dso * stg, axis=0, keepdims=True)
            dgm = jnp.zeros((BLOCK, BLOCK), F32)
            for hh in range(SSD_HPG):
                h = SSD_HPG * g + hh
                hs = slice(h * SSD_P, (h + 1) * SSD_P)
                ls = slice(hh * SSD_P, (hh + 1) * SSD_P)
                eh = e[:, h:h + 1]
                fh = f[:, h:h + 1]
                dth = dtv[:, h:h + 1]
                xh = xs_ref[:, hs]
                xdt = xh * dth
                dyh = dy_ref[:, hs]
                dyb = dyh.astype(BF16)
                lam = jnp.exp(jnp.where(tril, cs[:, h:h + 1] - cst[h:h + 1, :], NEG))
                m = gm * lam
                dm = _dot_nt(dyb, xdt.astype(BF16))
                dxdt = _dot_tn(m.astype(BF16), dyb) + fh * dfx[:, ls]
                w = dm * m
                dgm = dgm + dm * lam
                dff = jnp.sum(dfx[:, ls] * xdt, axis=1, keepdims=True) * fh
                col = (jnp.sum(w, axis=1, keepdims=True)
                       + jnp.sum(dyh * yraw[:, ls], axis=1, keepdims=True) * eh - dff)
                onl = (lane == h).astype(F32)
                dcs = dcs + col * onl
                dcst = dcst - (sub == h).astype(F32) * jnp.sum(w, axis=0, keepdims=True)
                dclh = (jnp.sum(dff, axis=0, keepdims=True)
                        + ecl[:, h:h + 1] * jnp.sum(prodsum[:, ls], axis=1, keepdims=True))
                dcl = dcl + dclh * onl
                ddtx = ddtx + jnp.sum(dxdt * xh, axis=1, keepdims=True) * onl
                dxs_ref[:, hs] = dxdt * dth
                edy[:, ls] = eh * dyh
                fx[:, ls] = fh * xdt
            edyb = edy[...].astype(BF16)
            fxb = fx[...].astype(BF16)
            dgb = dgm.astype(BF16)
            dc_ref[:, g * SSD_N:(g + 1) * SSD_N] = _dot_nt(edyb, stb) + _dot(dgb, bg)
            db_ref[:, g * SSD_N:(g + 1) * SSD_N] = _dot_nt(fxb, dsob) + _dot_tn(dgb, cg)
            dst[g] = dso * _head_cols(ecl, g) + _dot_tn(cg, edyb)
        dcs = dcs + dcst.T + jnp.where(sub == BLOCK - 1, dcl, 0.0)
        dda = _rev_cumsum_rows(dcs, BLOCK)
        ddt_ref[...] = ddtx + dda * a
        dalog_ref[...] += jnp.sum(dda * dtv, axis=0, keepdims=True) * a
        if plan is not None:
            @pl.when(i == nc - 1)
            def _():
                plan.wait(cins, couts, sems)

    rev = lambda i: nc - 1 - i
    res = pl.pallas_call(
        body, name=name, grid=(nc,),
        in_specs=[pl.BlockSpec((BLOCK, D_SSM), lambda i: (rev(i), 0)),
                  pl.BlockSpec((BLOCK, 1024), lambda i: (rev(i), 2)),
                  pl.BlockSpec((BLOCK, 1024), lambda i: (rev(i), 3)),
                  pl.BlockSpec((BLOCK, LANES), lambda i: (rev(i), 0)),
                  pl.BlockSpec((1, LANES), lambda i: (0, 0)),
                  pl.BlockSpec((BLOCK, D_SSM), lambda i: (rev(i), 0)),
                  pl.BlockSpec((1, SSD_GROUPS, SSD_N, gw), lambda i: (rev(i), 0, 0, 0))] + p_in,
        out_specs=[pl.BlockSpec((BLOCK, D_SSM), lambda i: (rev(i), 0)),
                   pl.BlockSpec((BLOCK, 1024), lambda i: (rev(i), 0)),
                   pl.BlockSpec((BLOCK, 1024), lambda i: (rev(i), 0)),
                   pl.BlockSpec((BLOCK, LANES), lambda i: (rev(i), 0)),
                   pl.BlockSpec((1, LANES), lambda i: (0, 0))] + p_out,
        out_shape=[jax.ShapeDtypeStruct((lp, D_SSM), F32),
                   jax.ShapeDtypeStruct((lp, 1024), F32),
                   jax.ShapeDtypeStruct((lp, 1024), F32),
                   jax.ShapeDtypeStruct((lp, LANES), F32),
                   jax.ShapeDtypeStruct((1, LANES), F32)] + p_shapes,
        scratch_shapes=[pltpu.VMEM((SSD_GROUPS, SSD_N, gw), F32),
                        pltpu.VMEM((BLOCK, gw), F32), pltpu.VMEM((BLOCK, gw), F32)] + p_scr,
        compiler_params=pltpu.CompilerParams(dimension_semantics=("arbitrary",)),
    )(xbc, xbc, xbc, dt, alog, dy, states, *(plan.ins if plan is not None else []))
    return res[:5], res[5:]


_GN_GROUPS = 8
_GN_W = D_SSM // _GN_GROUPS


def _ssm_gate_fwd(yssd, xbc, proj, dskip, gnorm, name):
    lp = yssd.shape[0]
    tm = _pick(lp, (320, 256, 128))

    def body(r, first, y_ref, x_ref, z_ref, d_ref, g_ref, o_ref):
        sz, _ = _silu_and_grad(z_ref[...])
        y2 = (y_ref[...] + d_ref[...] * x_ref[...]) * sz
        for k in range(_GN_GROUPS):
            sl = slice(k * _GN_W, (k + 1) * _GN_W)
            yk = y2[:, sl]
            rs = lax.rsqrt(jnp.mean(yk * yk, axis=-1, keepdims=True) + EPS)
            o_ref[:, sl] = (yk * rs * g_ref[:, sl]).astype(o_ref.dtype)

    return _rowcall(name, body, lp, tm, rows=[(yssd, D_SSM, 0), (xbc, D_SSM, 0), (proj, D_SSM, 0)],
                    vecs=[dskip, gnorm], outs=[(D_SSM, BF16)])[0]


def _ssm_gate_bwd(yssd, xbc, proj, dskip, gnorm, dyn, name):
    lp = yssd.shape[0]
    tm = BLOCK

    def body(r, first, y_ref, x_ref, z_ref, dyn_ref, d_ref, g_ref,
             dy_ref, dx_ref, dz_ref, dd_ref, dg_ref):
        z = z_ref[...]
        sz, dsz = _silu_and_grad(z)
        xs = x_ref[...]
        y1 = y_ref[...] + d_ref[...] * xs
        y2 = y1 * sz
        dyn = dyn_ref[...]
        for k in range(_GN_GROUPS):
            sl = slice(k * _GN_W, (k + 1) * _GN_W)
            dx, dgt = _rms_bwd(y2[:, sl], g_ref[:, sl], dyn[:, sl])
            dy1 = dx * sz[:, sl]
            dy_ref[:, sl] = dy1
            dx_ref[:, sl] = dy1 * d_ref[:, sl]
            dz_ref[:, sl] = (dx * y1[:, sl] * dsz[:, sl]).astype(dz_ref.dtype)

            @pl.when(first)
            def _():
                dd_ref[:, sl] = jnp.zeros((1, _GN_W), F32)
                dg_ref[:, sl] = jnp.zeros((1, _GN_W), F32)

            dd_ref[:, sl] += jnp.sum(dy1 * xs[:, sl], axis=0, keepdims=True)
            dg_ref[:, sl] += jnp.sum(dgt, axis=0, keepdims=True)

    return _rowcall(name, body, lp, tm,
                    rows=[(yssd, D_SSM, 0), (xbc, D_SSM, 0), (proj, D_SSM, 0), (dyn, D_SSM, 0)],
                    vecs=[dskip, gnorm], outs=[(D_SSM, F32), (D_SSM, F32), (D_SSM, BF16)],
                    accs=[((1, D_SSM), F32), ((1, D_SSM), F32)])


def _adamw(w, g, m, v, name):
    r, c = w.shape
    tm = r if r <= 512 else _pick(r, (512, 352, 256, 128, 64, 32, 16, 8))
    c1 = 1.0 / (1.0 - ADAM_B1 ** ADAM_STEP)
    c2 = 1.0 / (1.0 - ADAM_B2 ** ADAM_STEP)

    def body(w_ref, g_ref, m_ref, v_ref, d_ref, nm_ref, nv_ref):
        gv = g_ref[...]
        nm = ADAM_B1 * m_ref[...] + (1.0 - ADAM_B1) * gv
        nv = ADAM_B2 * v_ref[...] + (1.0 - ADAM_B2) * (gv * gv)
        nm_ref[...] = nm
        nv_ref[...] = nv
        d_ref[...] = -ADAM_LR * ((nm * c1) / (jnp.sqrt(nv * c2) + ADAM_EPS) + ADAM_WD * w_ref[...])

    spec = pl.BlockSpec((tm, c), lambda i: (i, 0))
    return pl.pallas_call(
        body, name=name, grid=(r // tm,), in_specs=[spec] * 4, out_specs=[spec] * 3,
        out_shape=[jax.ShapeDtypeStruct((r, c), F32)] * 3,
        compiler_params=pltpu.CompilerParams(dimension_semantics=("parallel",)),
    )(w, g, m, v)


def _place():
    return lax.axis_index("x"), lax.axis_index("y"), lax.axis_index("c")


def _other_chips(x, y):
    return [(1 - x, y), (x, 1 - y), (1 - x, 1 - y)]


_ANY = pl.BlockSpec(memory_space=pl.ANY)


class _Plan:
    def __init__(self, ins, out_shapes, n_remote, n_local, issue):
        self.ins = list(ins)
        self.out_shapes = list(out_shapes)
        self.issue = issue
        self.scratch = [pltpu.SemaphoreType.DMA((max(n_remote, 1),)),
                        pltpu.SemaphoreType.DMA((max(n_remote, 1),)),
                        pltpu.SemaphoreType.DMA((max(n_local, 1),))]

    def start(self, ins, outs, sems):
        sends, _, locs = self.issue(ins, outs, *sems)
        for cp in locs + sends:
            cp.start()

    def wait(self, ins, outs, sems):
        sends, recvs, locs = self.issue(ins, outs, *sems)
        for make in recvs:
            make().wait_recv()
        for cp in sends:
            cp.wait_send()
        for cp in locs:
            cp.wait()


def _plan_parts(plan):
    if plan is None:
        return [], [], [], []
    return ([_ANY] * len(plan.ins), plan.out_shapes, [_ANY] * len(plan.out_shapes), plan.scratch)


def _run_plan(plan, name):
    n_in, n_out = len(plan.ins), len(plan.out_shapes)

    def body(*refs):
        ins, outs, sems = refs[:n_in], refs[n_in:n_in + n_out], refs[n_in + n_out:]
        plan.start(ins, outs, sems)
        plan.wait(ins, outs, sems)

    return pl.pallas_call(
        body, name=name, in_specs=[_ANY] * n_in, out_specs=[_ANY] * n_out,
        out_shape=plan.out_shapes, scratch_shapes=plan.scratch,
    )(*plan.ins)


def _gather_plan(shards):
    n = len(shards)

    def issue(ins, outs, send_sems, recv_sems, local_sems):
        x, y, c = _place()
        me = 2 * x + y
        sends, recvs, locs = [], [], []
        for p in range(n):
            locs.append(pltpu.make_async_copy(ins[p], outs[p].at[me], local_sems.at[p]))
            for k, (px, py) in enumerate(_other_chips(x, y)):
                sems = dict(send_sem=send_sems.at[3 * p + k], recv_sem=recv_sems.at[3 * p + k],
                            device_id=(px, py, c), device_id_type=MESH)
                sends.append(pltpu.make_async_remote_copy(src_ref=ins[p], dst_ref=outs[p].at[me], **sems))
                recvs.append(functools.partial(pltpu.make_async_remote_copy, src_ref=ins[p],
                                               dst_ref=outs[p].at[2 * px + py], **sems))
        return sends, recvs, locs

    return _Plan(shards, [jax.ShapeDtypeStruct((N_CHIPS,) + s.shape, s.dtype) for s in shards], 3 * n, n, issue)


_REL7 = [(fx, fy, fc) for fx in (0, 1) for fy in (0, 1) for fc in (0, 1)][1:]


def _scatter8_plan(gs):
    n = len(gs)

    def issue(ins, outs, send_sems, recv_sems, local_sems):
        x, y, c = _place()
        sends = []
        for p in range(n):
            hr = gs[p].shape[1] // 2
            for k, (fx, fy, fc) in enumerate(_REL7):
                tx, ty, tc = x ^ fx, y ^ fy, c ^ fc
                src = ins[p].at[2 * tx + ty, pl.ds(pl.multiple_of(tc * hr, SUBLANES), hr), :]
                sends.append(pltpu.make_async_remote_copy(
                    src_ref=src, dst_ref=outs[p].at[k],
                    send_sem=send_sems.at[7 * p + k], recv_sem=recv_sems.at[7 * p + k],
                    device_id=(tx, ty, tc), device_id_type=MESH))
        return sends, [functools.partial(lambda cp: cp, cp) for cp in sends], []

    shapes = [jax.ShapeDtypeStruct((7, g.shape[1] // 2, g.shape[2]), g.dtype) for g in gs]
    return _Plan(gs, shapes, 7 * n, 0, issue)


def _sibling_plan(ts):
    n = len(ts)

    def issue(ins, outs, send_sems, recv_sems, local_sems):
        x, y, c = _place()
        sends = [pltpu.make_async_remote_copy(
            src_ref=ins[p], dst_ref=outs[p], send_sem=send_sems.at[p], recv_sem=recv_sems.at[p],
            device_id=(x, y, 1 - c), device_id_type=MESH) for p in range(n)]
        return sends, [functools.partial(lambda cp: cp, cp) for cp in sends], []

    return _Plan(ts, [jax.ShapeDtypeStruct(t.shape, t.dtype) for t in ts], n, 0, issue)


def _add8(g, recv, chip, core, name):
    s, r, n = g.shape
    hr = r // 2
    th = hr // 2 if (hr // 2) % SUBLANES == 0 else hr
    nt = hr // th

    def body(chip_ref, core_ref, g_ref, r_ref, o_ref):
        acc = g_ref[0].astype(F32)
        for k in range(7):
            acc = acc + r_ref[k].astype(F32)
        o_ref[...] = acc

    return pl.pallas_call(
        body, name=name,
        grid_spec=pltpu.PrefetchScalarGridSpec(
            num_scalar_prefetch=2, grid=(nt,),
            in_specs=[pl.BlockSpec((1, th, n), lambda i, ch, co: (ch[0], co[0] * nt + i, 0)),
                      pl.BlockSpec((7, th, n), lambda i, ch, co: (0, i, 0))],
            out_specs=pl.BlockSpec((th, n), lambda i, ch, co: (i, 0))),
        out_shape=jax.ShapeDtypeStruct((hr, n), F32),
        compiler_params=pltpu.CompilerParams(dimension_semantics=("parallel",)),
    )(chip, core, g, recv)


def _adamw_halves(w, own, other, m, v, core, name):
    r, n = w.shape
    hr = r // 2
    th = hr // 2 if (hr // 2) % SUBLANES == 0 else hr
    tph = hr // th
    c1 = 1.0 / (1.0 - ADAM_B1 ** ADAM_STEP)
    c2 = 1.0 / (1.0 - ADAM_B2 ** ADAM_STEP)

    def body(core_ref, w_ref, a_ref, b_ref, m_ref, v_ref, g_ref, d_ref, nm_ref, nv_ref):
        half = pl.program_id(0) // tph
        gv = jnp.where(half == core_ref[0], a_ref[...], b_ref[...])
        nm = ADAM_B1 * m_ref[...] + (1.0 - ADAM_B1) * gv
        nv = ADAM_B2 * v_ref[...] + (1.0 - ADAM_B2) * (gv * gv)
        g_ref[...] = gv
        nm_ref[...] = nm
        nv_ref[...] = nv
        d_ref[...] = -ADAM_LR * ((nm * c1) / (jnp.sqrt(nv * c2) + ADAM_EPS) + ADAM_WD * w_ref[...])

    full = pl.BlockSpec((th, n), lambda i, co: (i, 0))
    part = pl.BlockSpec((th, n), lambda i, co: (i % tph, 0))
    return pl.pallas_call(
        body, name=name,
        grid_spec=pltpu.PrefetchScalarGridSpec(
            num_scalar_prefetch=1, grid=(2 * tph,),
            in_specs=[full, part, part, full, full], out_specs=[full] * 4),
        out_shape=[jax.ShapeDtypeStruct((r, n), F32)] * 4,
        compiler_params=pltpu.CompilerParams(dimension_semantics=("parallel",)),
    )(core, w, own, other, m, v)


def _allreduce_small(pack, name):
    r, l = pack.shape

    def body(p_ref, o_ref, land, send_sems, recv_sems):
        x, y, c = _place()
        me = 4 * x + 2 * y + c
        land[me] = p_ref[...]
        rel = [(fx, fy, fc) for fx in (0, 1) for fy in (0, 1) for fc in (0, 1)][1:]
        sends = []
        for k, (fx, fy, fc) in enumerate(rel):
            peer = (x ^ fx, y ^ fy, c ^ fc)
            cp = pltpu.make_async_remote_copy(
                src_ref=p_ref, dst_ref=land.at[me], send_sem=send_sems.at[k], recv_sem=recv_sems.at[k],
                device_id=peer, device_id_type=MESH)
            cp.start()
            sends.append(cp)
        for k, (fx, fy, fc) in enumerate(rel):
            src = 4 * (x ^ fx) + 2 * (y ^ fy) + (c ^ fc)
            pltpu.make_async_remote_copy(
                src_ref=p_ref, dst_ref=land.at[src], send_sem=send_sems.at[k], recv_sem=recv_sems.at[k],
                device_id=(x ^ fx, y ^ fy, c ^ fc), device_id_type=MESH).wait_recv()
        for cp in sends:
            cp.wait_send()
        acc = land[0]
        for d in range(1, N_DEV):
            acc = acc + land[d]
        o_ref[...] = acc

    vm = pl.BlockSpec(memory_space=pltpu.VMEM)
    return pl.pallas_call(
        body, name=name, in_specs=[vm], out_specs=vm,
        out_shape=jax.ShapeDtypeStruct((r, l), F32),
        scratch_shapes=[pltpu.VMEM((N_DEV, r, l), F32),
                        pltpu.SemaphoreType.DMA((N_DEV - 1,)), pltpu.SemaphoreType.DMA((N_DEV - 1,))],
    )(pack)


def _flat_rows(a, mult=SUBLANES * LANES):
    f = a.reshape(-1)
    padn = (-f.shape[0]) % mult
    if padn:
        f = jnp.concatenate([f, jnp.zeros((padn,), f.dtype)])
    return f


def _pack(arrs, mult=SUBLANES * LANES, total_mult=None):
    flat = [_flat_rows(a, mult) for a in arrs]
    sizes = [f.shape[0] for f in flat]
    if total_mult is not None:
        padn = (-sum(sizes)) % total_mult
        if padn:
            flat.append(jnp.zeros((padn,), flat[0].dtype))
    return jnp.concatenate(flat).reshape(-1, LANES), sizes


def _unpack(pack, shapes, sizes, lead=()):
    flat = pack.reshape(lead + (-1,))
    out, off = [], 0
    for shp, sz in zip(shapes, sizes):
        n = math.prod(shp)
        out.append(flat[..., off:off + n].reshape(lead + tuple(shp)))
        off += sz
    return out


def _cols_from_shards(g):
    s, k, n = g.shape
    return jnp.transpose(g, (1, 0, 2)).reshape(k, s * n)


def _cols_to_shards(w, s=N_CHIPS):
    k, n = w.shape
    return jnp.transpose(w.reshape(k, s, n // s), (1, 0, 2))


def _ffn_fwd(h, pre, post, w_up, cw, cb, w_down, tag):
    u = _rmsnorm_fwd(h, pre, f"{tag}_prenorm")
    hp = _mm_nn_sh(u, w_up, 2 * D_FF, f"{tag}_up")
    hc = _conv_fwd(hp, 0, 2 * D_FF, cw, cb, f"{tag}_conv")
    act = _ffn_act_fwd(hc, f"{tag}_act")
    o = _mm_nn(act, w_down, F32, f"{tag}_down")
    hn = _postnorm_res_fwd(h, o, post, f"{tag}_postnorm")
    return hn, (h, u, hp, hc, act, o)


def _ffn_bwd(dh, saved, pre, post, w_up, cw, w_down, tag):
    h, u, hp, hc, act, o = saved
    do, dpost = _postnorm_bwd(o, post, dh, f"{tag}_postnorm_bwd")
    dact = _mm_nt(do, w_down, f"{tag}_down_dx")
    dw_down = _mm_tn(act, do, f"{tag}_down_dw")
    dhc = _ffn_act_bwd(hc, dact, f"{tag}_act_bwd")
    dhp, dcw, dcb = _conv_bwd(hp, 0, 2 * D_FF, dhc, cw, f"{tag}_conv_bwd")
    du = _mm_nt_sh(dhp, w_up, f"{tag}_up_dx")
    dw_up = _mm_tn_sh(u, dhp, w_up.shape[2], f"{tag}_up_dw")
    dhn, dpre = _prenorm_bwd(h, pre, du, dh, f"{tag}_prenorm_bwd")
    return dhn, dict(pre=dpre, post=dpost, w_up=dw_up, conv_w=dcw[:3], conv_b=dcb, w_down=dw_down)


class _Exchange:
    GATHER_IN_ATTN = ("l0_w_out", "l0_ffn_w_up", "l0_ffn_w_down", "l1_w_in")
    GATHER_IN_SSD = ("l1_w_out", "l1_ffn_w_up", "l1_ffn_w_down")
    AFTER_L1_OUT = ("l1_ffn_w_up", "l1_ffn_w_down", "l1_w_out")
    AFTER_L0_OUT = ("l1_w_in", "l0_ffn_w_up", "l0_ffn_w_down", "l0_w_out")
    LAST = ("l0_w_in",)

    def __init__(self, late_shards):
        self.late = dict(late_shards)
        self.slabs = {}
        self.recv = {}

    def gather_plan(self, names):
        return _gather_plan([self.late[n] for n in names])

    def gathered(self, names, outs):
        return {n: (g if n in _BIG_COL else g.reshape(-1, g.shape[-1])) for n, g in zip(names, outs)}

    def scatter_plan(self, grads, names):
        for n in names:
            g = grads[n]
            self.slabs[n] = g if n in _BIG_COL else g.reshape(N_CHIPS, -1, g.shape[-1])
        return _scatter8_plan([self.slabs[n] for n in names])

    def scattered(self, names, outs):
        self.recv.update(zip(names, outs))


def _local_step(x, tgt, meta, P, ex=None):
    seq, d = x.shape
    lp = seq + BLOCK
    h0 = jnp.concatenate([jnp.zeros((PAD, d), F32), meta, x], axis=0)
    tgt_p = jnp.concatenate([jnp.zeros((BLOCK, d), F32), tgt], axis=0)

    u0 = _rmsnorm_fwd(h0, P["l0_mix_pre_norm"], "l0_mix_prenorm")
    proj0 = _mm_nn_sh(u0, P["l0_w_in"], EVEN_IN, "l0_in")
    xrc = _conv_fwd(proj0, D_MODEL, D_MODEL, P["l0_lru_conv_w"], P["l0_lru_conv_b"], "l0_lru_conv")
    lru_args = (P["l0_lru_w_a"], P["l0_lru_w_x"], P["l0_lru_b_a"], P["l0_lru_b_x"], P["l0_lru_lambda"])
    ya, hl = _lru_fwd(proj0, xrc, *lru_args, "l0_lru")
    yb, outs = _attn_fwd(proj0, P["l0_attn_sinks"], "l0_attn",
                         ex.gather_plan(ex.GATHER_IN_ATTN) if ex else None)
    if ex:
        P = {**P, **ex.gathered(ex.GATHER_IN_ATTN, outs)}
    mix0 = jnp.concatenate([ya, yb], axis=1)
    o0 = _mm_nn(mix0, P["l0_w_out"], F32, "l0_out")
    h1 = _postnorm_res_fwd(h0, o0, P["l0_mix_post_norm"], "l0_mix_postnorm")
    h2, ffn0 = _ffn_fwd(h1, P["l0_ffn_pre_norm"], P["l0_ffn_post_norm"], P["l0_ffn_w_up"],
                        P["l0_ffn_conv_w"], P["l0_ffn_conv_b"], P["l0_ffn_w_down"], "l0_ffn")
    u2 = _rmsnorm_fwd(h2, P["l1_mix_pre_norm"], "l1_mix_prenorm")
    proj1 = _mm_nn_sh(u2, P["l1_w_in"], ODD_IN_PAD, "l1_in")
    xc1 = _conv_fwd(proj1, _ZW, _XBC_W, P["l1_ssm_conv_w"], P["l1_ssm_conv_b"], "l1_ssm_conv")
    xbc, dt = _ssm_prep_fwd(xc1, proj1, P["l1_dt_bias"], "l1_ssm_prep")
    (yssd, states), outs = _ssd_fwd(xbc, dt, P["l1_a_log"], "l1_ssd",
                                    ex.gather_plan(ex.GATHER_IN_SSD) if ex else None)
    if ex:
        P = {**P, **ex.gathered(ex.GATHER_IN_SSD, outs)}
    yn = _ssm_gate_fwd(yssd, xbc, proj1, P["l1_d_skip"], P["l1_gate_norm"], "l1_ssm_gate")
    o1 = _mm_nn(yn, P["l1_w_out"], F32, "l1_out")
    h3 = _postnorm_res_fwd(h2, o1, P["l1_mix_post_norm"], "l1_mix_postnorm")
    h4, ffn1 = _ffn_fwd(h3, P["l1_ffn_pre_norm"], P["l1_ffn_post_norm"], P["l1_ffn_w_up"],
                        P["l1_ffn_conv_w"], P["l1_ffn_conv_b"], P["l1_ffn_w_down"], "l1_ffn")
    dh4, loss_cols = _loss_fwd_bwd(h4, tgt_p, "loss")

    G = {}
    dh3, g = _ffn_bwd(dh4, ffn1, P["l1_ffn_pre_norm"], P["l1_ffn_post_norm"], P["l1_ffn_w_up"],
                      P["l1_ffn_conv_w"], P["l1_ffn_w_down"], "l1_ffn")
    for k, v in g.items():
        G["l1_ffn_" + (k + "_norm" if k in ("pre", "post") else k)] = v
    do1, G["l1_mix_post_norm"] = _postnorm_bwd(o1, P["l1_mix_post_norm"], dh3, "l1_mix_postnorm_bwd")
    dyn = _mm_nt(do1, P["l1_w_out"], "l1_out_dx")
    G["l1_w_out"] = _mm_tn(yn, do1, "l1_out_dw")
    dyssd, dxskip, dz, dd_cols, G["l1_gate_norm"] = _ssm_gate_bwd(
        yssd, xbc, proj1, P["l1_d_skip"], P["l1_gate_norm"], dyn, "l1_ssm_gate_bwd")
    G["l1_d_skip"] = dd_cols.reshape(SSD_HEADS, SSD_P).sum(axis=1)
    (dxs, dbm, dcm, ddt, dalog), outs = _ssd_bwd(
        xbc, dt, P["l1_a_log"], states, dyssd, "l1_ssd_bwd",
        ex.scatter_plan(G, ex.AFTER_L1_OUT) if ex else None)
    if ex:
        ex.scattered(ex.AFTER_L1_OUT, outs)
    G["l1_a_log"] = dalog[0, :SSD_HEADS]
    dxc, ddtr, dbias = _ssm_prep_bwd(xc1, proj1, P["l1_dt_bias"], dxs, dxskip, dbm, dcm, ddt,
                                     "l1_ssm_prep_bwd")
    G["l1_dt_bias"] = dbias[0, :SSD_HEADS]
    dxbc, dcw, dcb = _conv_bwd(proj1, _ZW, _XBC_W, dxc, P["l1_ssm_conv_w"], "l1_ssm_conv_bwd")
    G["l1_ssm_conv_w"] = dcw[:4]
    G["l1_ssm_conv_b"] = dcb
    dproj1 = jnp.concatenate([dz, dxbc, ddtr], axis=1)
    du2 = _mm_nt_sh(dproj1, P["l1_w_in"], "l1_in_dx")
    G["l1_w_in"] = _mm_tn_sh(u2, dproj1, ODD_IN // N_CHIPS, "l1_in_dw")
    dh2, G["l1_mix_pre_norm"] = _prenorm_bwd(h2, P["l1_mix_pre_norm"], du2, dh3, "l1_mix_prenorm_bwd")
    dh1, g = _ffn_bwd(dh2, ffn0, P["l0_ffn_pre_norm"], P["l0_ffn_post_norm"], P["l0_ffn_w_up"],
                      P["l0_ffn_conv_w"], P["l0_ffn_w_down"], "l0_ffn")
    for k, v in g.items():
        G["l0_ffn_" + (k + "_norm" if k in ("pre", "post") else k)] = v
    do0, G["l0_mix_post_norm"] = _postnorm_bwd(o0, P["l0_mix_post_norm"], dh1, "l0_mix_postnorm_bwd")
    dmix = _mm_nt(do0, P["l0_w_out"], "l0_out_dx")
    G["l0_w_out"] = _mm_tn(mix0, do0, "l0_out_dw")
    (dgate, dxrc, G["l0_lru_w_a"], G["l0_lru_w_x"], G["l0_lru_b_a"], G["l0_lru_b_x"],
     G["l0_lru_lambda"]) = _lru_bwd(proj0, xrc, hl, dmix, *lru_args, "l0_lru_bwd")
    dxr, dcw, dcb = _conv_bwd(proj0, D_MODEL, D_MODEL, dxrc, P["l0_lru_conv_w"], "l0_lru_conv_bwd")
    G["l0_lru_conv_w"] = dcw[:4]
    G["l0_lru_conv_b"] = dcb
    (dq, dk, dv, G["l0_attn_sinks"]), outs = _attn_bwd(
        proj0, P["l0_attn_sinks"], dmix, "l0_attn_bwd",
        ex.scatter_plan(G, ex.AFTER_L0_OUT) if ex else None)
    if ex:
        ex.scattered(ex.AFTER_L0_OUT, outs)
    dproj0 = jnp.concatenate([dgate, dxr, dq, dk.astype(BF16), dv.astype(BF16)], axis=1)
    du0 = _mm_nt_sh(dproj0, P["l0_w_in"], "l0_in_dx")
    G["l0_w_in"] = _mm_tn_sh(u0, dproj0, EVEN_IN // N_CHIPS, "l0_in_dw")
    dh0, G["l0_mix_pre_norm"] = _prenorm_bwd(h0, P["l0_mix_pre_norm"], du0, dh1, "l0_mix_prenorm_bwd")
    return loss_cols, dh0[BLOCK:], dh0[PAD:BLOCK], G


_BIG_COL = ("l0_w_in", "l0_ffn_w_up", "l1_w_in", "l1_ffn_w_up")
_BIG_ROW = ("l0_w_out", "l0_ffn_w_down", "l1_w_out", "l1_ffn_w_down")
_BIG = ("l0_w_in", "l0_w_out", "l0_ffn_w_up", "l0_ffn_w_down",
        "l1_w_in", "l1_w_out", "l1_ffn_w_up", "l1_ffn_w_down")
_SMALL_SHARDED = ("meta_tokens", "l0_lru_conv_w", "l0_ffn_conv_w", "l1_ssm_conv_w", "l1_ffn_conv_w")
_WEIGHTS = ("meta_tokens", "l0_mix_pre_norm", "l0_mix_post_norm", "l0_w_in", "l0_lru_conv_w",
            "l0_lru_conv_b", "l0_lru_w_a", "l0_lru_b_a", "l0_lru_w_x", "l0_lru_b_x", "l0_lru_lambda",
            "l0_attn_sinks", "l0_w_out", "l0_ffn_pre_norm", "l0_ffn_post_norm", "l0_ffn_w_up",
            "l0_ffn_conv_w", "l0_ffn_conv_b", "l0_ffn_w_down", "l1_mix_pre_norm", "l1_mix_post_norm",
            "l1_w_in", "l1_ssm_conv_w", "l1_ssm_conv_b", "l1_dt_bias", "l1_a_log", "l1_d_skip",
            "l1_gate_norm", "l1_w_out", "l1_ffn_pre_norm", "l1_ffn_post_norm", "l1_ffn_w_up",
            "l1_ffn_conv_w", "l1_ffn_conv_b", "l1_ffn_w_down")
_REPL = tuple(n for n in _WEIGHTS if n not in _BIG and n not in _SMALL_SHARDED)


def _pad_lanes(v, n=LANES):
    return jnp.concatenate([v, jnp.zeros((n - v.shape[0],), v.dtype)]).reshape(1, n)


def _step(x, tgt, W, M, V):
    cx, cy, cc = _place()
    chip = 2 * cx + cy

    small_pack, small_sizes = _pack([W[n] for n in _SMALL_SHARDED])
    first = _run_plan(_gather_plan([W["l0_w_in"].astype(BF16), small_pack]), "gather_first")
    small_full = _unpack(first[1], [W[n].shape for n in _SMALL_SHARDED], small_sizes, lead=(N_CHIPS,))
    ex = _Exchange({n: W[n].astype(BF16) for n in _BIG if n != "l0_w_in"})

    P = {"l0_w_in": first[0]}
    for n, g in zip(_SMALL_SHARDED, small_full):
        P[n] = _cols_from_shards(g)
    for n in _REPL:
        v = W[n]
        P[n] = v.reshape(1, -1) if v.ndim == 1 else v
    P["l0_lru_w_a"] = W["l0_lru_w_a"].astype(BF16)
    P["l0_lru_w_x"] = W["l0_lru_w_x"].astype(BF16)
    P["l1_dt_bias"] = _pad_lanes(W["l1_dt_bias"])
    P["l1_a_log"] = _pad_lanes(W["l1_a_log"])
    P["l1_d_skip"] = jnp.repeat(W["l1_d_skip"], SSD_P).reshape(1, D_SSM)
    meta = P.pop("meta_tokens")

    loss_cols, grad_x, grad_meta, G = _local_step(x, tgt, meta, P, ex)
    G["meta_tokens"] = grad_meta

    ex.scattered(ex.LAST, _run_plan(ex.scatter_plan(G, ex.LAST), "grad_scatter_last"))
    core_idx = cc.astype(jnp.int32).reshape(1)
    chip_idx = chip.astype(jnp.int32).reshape(1)
    own_half = [_add8(ex.slabs[n], ex.recv[n], chip_idx, core_idx, f"grad_sum_{n}") for n in _BIG]
    other_half = _run_plan(_sibling_plan(own_half), "grad_sibling_swap")
    small_names = list(_REPL) + list(_SMALL_SHARDED)
    small_list = [G[n] for n in small_names] + [loss_cols]
    spack, ssizes = _pack(small_list)
    sred = _allreduce_small(spack, "small_allreduce")
    sfull = _unpack(sred, [a.shape for a in small_list], ssizes)
    loss = 0.5 / D_MODEL * jnp.sum(sfull[-1])
    small_grads = {}
    for n, g in zip(small_names, sfull[:-1]):
        if n in _SMALL_SHARDED:
            wcols = W[n].shape[1]
            g = lax.dynamic_slice_in_dim(g, chip * wcols, wcols, axis=1)
        small_grads[n] = g.reshape(W[n].shape)

    grads, delta, new_m, new_v = {}, {}, {}, {}
    for n, own, other in zip(_BIG, own_half, other_half):
        grads[n], delta[n], new_m[n], new_v[n] = _adamw_halves(
            W[n], own, other, M[n], V[n], core_idx, f"adamw_{n}")
    s_names = [n for n in _WEIGHTS if n not in _BIG]
    tile_elems = 512 * LANES
    wp, wsz = _pack([W[n] for n in s_names], total_mult=tile_elems)
    gp, _ = _pack([small_grads[n] for n in s_names], total_mult=tile_elems)
    mp, _ = _pack([M[n] for n in s_names], total_mult=tile_elems)
    vp, _ = _pack([V[n] for n in s_names], total_mult=tile_elems)
    dp, nmp, nvp = _adamw(wp, gp, mp, vp, "adamw_small")
    shapes = [W[n].shape for n in s_names]
    for n, a, b, c_ in zip(s_names, _unpack(dp, shapes, wsz), _unpack(nmp, shapes, wsz),
                           _unpack(nvp, shapes, wsz)):
        grads[n] = small_grads[n]
        delta[n], new_m[n], new_v[n] = a, b, c_
    return loss, grad_x, grads, delta, new_m, new_v


def kernel(x, meta_tokens, l0_mix_pre_norm, l0_mix_post_norm, l0_w_in, l0_lru_conv_w, l0_lru_conv_b, l0_lru_w_a, l0_lru_b_a, l0_lru_w_x, l0_lru_b_x, l0_lru_lambda, l0_attn_sinks, l0_w_out, l0_ffn_pre_norm, l0_ffn_post_norm, l0_ffn_w_up, l0_ffn_conv_w, l0_ffn_conv_b, l0_ffn_w_down, l1_mix_pre_norm, l1_mix_post_norm, l1_w_in, l1_ssm_conv_w, l1_ssm_conv_b, l1_dt_bias, l1_a_log, l1_d_skip, l1_gate_norm, l1_w_out, l1_ffn_pre_norm, l1_ffn_post_norm, l1_ffn_w_up, l1_ffn_conv_w, l1_ffn_conv_b, l1_ffn_w_down, loss_target, m_meta_tokens, m_l0_mix_pre_norm, m_l0_mix_post_norm, m_l0_w_in, m_l0_lru_conv_w, m_l0_lru_conv_b, m_l0_lru_w_a, m_l0_lru_b_a, m_l0_lru_w_x, m_l0_lru_b_x, m_l0_lru_lambda, m_l0_attn_sinks, m_l0_w_out, m_l0_ffn_pre_norm, m_l0_ffn_post_norm, m_l0_ffn_w_up, m_l0_ffn_conv_w, m_l0_ffn_conv_b, m_l0_ffn_w_down, m_l1_mix_pre_norm, m_l1_mix_post_norm, m_l1_w_in, m_l1_ssm_conv_w, m_l1_ssm_conv_b, m_l1_dt_bias, m_l1_a_log, m_l1_d_skip, m_l1_gate_norm, m_l1_w_out, m_l1_ffn_pre_norm, m_l1_ffn_post_norm, m_l1_ffn_w_up, m_l1_ffn_conv_w, m_l1_ffn_conv_b, m_l1_ffn_w_down, v_meta_tokens, v_l0_mix_pre_norm, v_l0_mix_post_norm, v_l0_w_in, v_l0_lru_conv_w, v_l0_lru_conv_b, v_l0_lru_w_a, v_l0_lru_b_a, v_l0_lru_w_x, v_l0_lru_b_x, v_l0_lru_lambda, v_l0_attn_sinks, v_l0_w_out, v_l0_ffn_pre_norm, v_l0_ffn_post_norm, v_l0_ffn_w_up, v_l0_ffn_conv_w, v_l0_ffn_conv_b, v_l0_ffn_w_down, v_l1_mix_pre_norm, v_l1_mix_post_norm, v_l1_w_in, v_l1_ssm_conv_w, v_l1_ssm_conv_b, v_l1_dt_bias, v_l1_a_log, v_l1_d_skip, v_l1_gate_norm, v_l1_w_out, v_l1_ffn_pre_norm, v_l1_ffn_post_norm, v_l1_ffn_w_up, v_l1_ffn_conv_w, v_l1_ffn_conv_b, v_l1_ffn_w_down):
    args = locals()
    W = {n: args[n] for n in _WEIGHTS}
    M = {n: args["m_" + n] for n in _WEIGHTS}
    V = {n: args["v_" + n] for n in _WEIGHTS}
    loss, grad_x, grads, delta, new_m, new_v = _step(x[0], loss_target[0], W, M, V)
    return (loss, grad_x[None], *[grads[n] for n in _WEIGHTS], *[delta[n] for n in _WEIGHTS],
            *[new_m[n] for n in _WEIGHTS], *[new_v[n] for n in _WEIGHTS])
```

```python
import functools
import math

import jax
import jax.numpy as jnp
from jax import lax
from jax.experimental import pallas as pl
from jax.experimental.pallas import tpu as pltpu

F32 = jnp.float32
BF16 = jnp.bfloat16

D_MODEL = 1024
N_META = 16
BLOCK = 128
PAD = BLOCK - N_META
EPS = 1e-6
LRU_BLOCKS = 8
LRU_BS = 128
LRU_C = 8.0
N_Q_HEADS = 16
N_KV_HEADS = 2
HEAD_DIM = 64
Q_PER_KV = 8
WINDOW = 128
D_SSM = 2048
SSD_HEADS = 32
SSD_GROUPS = 8
SSD_HPG = 4
SSD_P = 64
SSD_N = 128
D_FF = 2816
NEG = -1e30
LANES = 128
SUBLANES = 8

ADAM_LR = 0.001
ADAM_B1 = 0.9
ADAM_B2 = 0.999
ADAM_EPS = 1e-08
ADAM_WD = 0.01
ADAM_STEP = 10

MESH = pl.DeviceIdType.MESH
N_CHIPS = 4
N_DEV = 8


def _pick(n, cands):
    for c in cands:
        if n % c == 0:
            return c
    raise ValueError(f"no tile for {n} in {cands}")


def _col_tile(n, limit=1792):
    best = None
    for t in range(LANES, min(n, limit) + 1, LANES):
        if n % t == 0:
            best = t
    if best is None:
        raise ValueError(f"no lane tile for {n}")
    return best


def _sigmoid(x):
    return 1.0 / (1.0 + jnp.exp(-x))


def _log1p(e):
    u = 1.0 + e
    return jnp.where(u == 1.0, e, jnp.log(u) * (e / jnp.where(u == 1.0, 1.0, u - 1.0)))


def _softplus(x):
    return jnp.maximum(x, 0.0) + _log1p(jnp.exp(-jnp.abs(x)))


def _neg_expm1(x):
    poly = x * (1.0 + x * (0.5 + x * (1.0 / 6.0 + x * (1.0 / 24.0 + x * (1.0 / 120.0)))))
    return -jnp.where(x > -0.05, poly, jnp.exp(x) - 1.0)


_GELU_C = math.sqrt(2.0 / math.pi)


def _gelu(x):
    t = jnp.tanh(_GELU_C * (x + 0.044715 * x * x * x))
    return 0.5 * x * (1.0 + t)


def _gelu_and_grad(x):
    x2 = x * x
    t = jnp.tanh(_GELU_C * (x + 0.044715 * x * x2))
    g = 0.5 * x * (1.0 + t)
    dg = 0.5 * (1.0 + t) + 0.5 * x * (1.0 - t * t) * _GELU_C * (1.0 + 3.0 * 0.044715 * x2)
    return g, dg


def _silu_and_grad(x):
    s = _sigmoid(x)
    return x * s, s * (1.0 + x * (1.0 - s))


def _dot(a, b):
    return jnp.dot(a, b, preferred_element_type=F32)


def _dot_nt(a, b):
    return lax.dot_general(a, b, (((1,), (1,)), ((), ())), preferred_element_type=F32)


def _dot_tn(a, b):
    return lax.dot_general(a, b, (((0,), (0,)), ((), ())), preferred_element_type=F32)


def _row_iota(t):
    return lax.broadcasted_iota(jnp.int32, (t, 1), 0)


def _scan_fwd(a, u, t):
    row = _row_iota(t)
    d = 1
    while d < t:
        m = row >= d
        u_sh = jnp.where(m, pltpu.roll(u, d, 0), 0.0)
        a_sh = jnp.where(m, pltpu.roll(a, d, 0), 1.0)
        u = u + a * u_sh
        a = a * a_sh
        d *= 2
    return a, u


def _scan_rev(c, x, t):
    row = _row_iota(t)
    d = 1
    while d < t:
        m = row < t - d
        x_sh = jnp.where(m, pltpu.roll(x, t - d, 0), 0.0)
        c_sh = jnp.where(m, pltpu.roll(c, t - d, 0), 1.0)
        x = x + c * x_sh
        c = c * c_sh
        d *= 2
    return c, x


def _cumsum_rows(x, t):
    row = _row_iota(t)
    d = 1
    while d < t:
        x = x + jnp.where(row >= d, pltpu.roll(x, d, 0), 0.0)
        d *= 2
    return x


def _rev_cumsum_rows(x, t):
    row = _row_iota(t)
    d = 1
    while d < t:
        x = x + jnp.where(row < t - d, pltpu.roll(x, t - d, 0), 0.0)
        d *= 2
    return x


def _rms_bwd(x, g, dy):
    rs = lax.rsqrt(jnp.mean(x * x, axis=-1, keepdims=True) + EPS)
    gy = dy * g
    dx = rs * gy - x * (rs * rs * rs) * jnp.mean(x * gy, axis=-1, keepdims=True)
    return dx, dy * x * rs


def _mm_nn(a, w, out_dtype, name):
    m, k = a.shape
    n = w.shape[1]
    tm = _pick(m, (640, 512, 256, 128))
    tn = _col_tile(n)

    def body(a_ref, w_ref, o_ref):
        o_ref[...] = _dot(a_ref[...].astype(BF16), w_ref[...]).astype(o_ref.dtype)

    return pl.pallas_call(
        body, name=name, grid=(n // tn, m // tm),
        in_specs=[pl.BlockSpec((tm, k), lambda j, i: (i, 0)),
                  pl.BlockSpec((k, tn), lambda j, i: (0, j))],
        out_specs=pl.BlockSpec((tm, tn), lambda j, i: (i, j)),
        out_shape=jax.ShapeDtypeStruct((m, n), out_dtype),
        compiler_params=pltpu.CompilerParams(dimension_semantics=("parallel", "parallel")),
    )(a, w)


def _mm_nt(dy, w, name):
    m, n = dy.shape
    k = w.shape[0]
    wide = n > 3328
    tm = _pick(m, (320, 256, 128)) if wide else _pick(m, (640, 512, 256, 128))
    tk = _col_tile(k, 512 if wide else 1408)

    def body(dy_ref, w_ref, o_ref):
        o_ref[...] = _dot_nt(dy_ref[...].astype(BF16), w_ref[...])

    return pl.pallas_call(
        body, name=name, grid=(k // tk, m // tm),
        in_specs=[pl.BlockSpec((tm, n), lambda j, i: (i, 0)),
                  pl.BlockSpec((tk, n), lambda j, i: (j, 0))],
        out_specs=pl.BlockSpec((tm, tk), lambda j, i: (i, j)),
        out_shape=jax.ShapeDtypeStruct((m, k), F32),
        compiler_params=pltpu.CompilerParams(dimension_semantics=("parallel", "parallel")),
    )(dy, w)


def _mm_tn(a, dy, name):
    m, k = a.shape
    n = dy.shape[1]
    tm = _pick(m, (640, 512, 256, 128))
    tk = _col_tile(k, 1408)
    tn = _col_tile(n, 1664)
    nsteps = m // tm

    def body(a_ref, dy_ref, o_ref, acc):
        @pl.when(pl.program_id(2) == 0)
        def _():
            acc[...] = jnp.zeros_like(acc)

        acc[...] += _dot_tn(a_ref[...].astype(BF16), dy_ref[...].astype(BF16))

        @pl.when(pl.program_id(2) == nsteps - 1)
        def _():
            o_ref[...] = acc[...].astype(o_ref.dtype)

    return pl.pallas_call(
        body, name=name, grid=(k // tk, n // tn, nsteps),
        in_specs=[pl.BlockSpec((tm, tk), lambda kk, j, i: (i, kk)),
                  pl.BlockSpec((tm, tn), lambda kk, j, i: (i, j))],
        out_specs=pl.BlockSpec((tk, tn), lambda kk, j, i: (kk, j)),
        out_shape=jax.ShapeDtypeStruct((k, n), BF16),
        scratch_shapes=[pltpu.VMEM((tk, tn), F32)],
        compiler_params=pltpu.CompilerParams(
            dimension_semantics=("parallel", "parallel", "arbitrary")),
    )(a, dy)


def _mm_nn_sh(a, w4, n_out, name):
    m, k = a.shape
    s, _, n = w4.shape
    tm = _pick(m, (320, 256, 128))

    def body(a_ref, w_ref, o_ref):
        av = a_ref[...].astype(BF16)
        for j in range(s):
            o_ref[:, j * n:(j + 1) * n] = _dot(av, w_ref[j])
        if n_out > s * n:
            o_ref[:, s * n:] = jnp.zeros((tm, n_out - s * n), F32)

    return pl.pallas_call(
        body, name=name, grid=(m // tm,),
        in_specs=[pl.BlockSpec((tm, k), lambda i: (i, 0)),
                  pl.BlockSpec((s, k, n), lambda i: (0, 0, 0))],
        out_specs=pl.BlockSpec((tm, n_out), lambda i: (i, 0)),
        out_shape=jax.ShapeDtypeStruct((m, n_out), F32),
        compiler_params=pltpu.CompilerParams(dimension_semantics=("parallel",)),
    )(a, w4)


def _mm_nt_sh(dy, w4, name):
    dys = dy if isinstance(dy, (tuple, list)) else (dy,)
    m = dys[0].shape[0]
    s, k, n = w4.shape
    tm = _pick(m, (640, 512, 256, 128))
    tk = _col_tile(k, 256)
    where = _shard_columns(dys, s, n)

    def body(*refs):
        w_ref, o_ref = refs[len(dys)], refs[len(dys) + 1]
        acc = None
        for j, (p, c0) in enumerate(where):
            t = _dot_nt(refs[p][:, c0:c0 + n].astype(BF16), w_ref[j])
            acc = t if acc is None else acc + t
        o_ref[...] = acc

    return pl.pallas_call(
        body, name=name, grid=(k // tk, m // tm),
        in_specs=[pl.BlockSpec((tm, d.shape[1]), lambda j, i: (i, 0)) for d in dys]
        + [pl.BlockSpec((s, tk, n), lambda j, i: (0, j, 0))],
        out_specs=pl.BlockSpec((tm, tk), lambda j, i: (i, j)),
        out_shape=jax.ShapeDtypeStruct((m, k), F32),
        compiler_params=pltpu.CompilerParams(dimension_semantics=("parallel", "parallel")),
    )(*dys, w4)


def _shard_columns(dys, s, n):
    where = []
    for p, d in enumerate(dys):
        where += [(p, c * n) for c in range(d.shape[1] // n)]
    assert len(where) >= s
    return where[:s]


def _mm_tn_sh(a, dy, n, name):
    dys = dy if isinstance(dy, (tuple, list)) else (dy,)
    m, k = a.shape
    s = N_CHIPS
    tm = _pick(m, (640, 512, 256, 128))
    tk = _col_tile(k, 512 if n <= 1024 else 256)
    nsteps = m // tm
    where = _shard_columns(dys, s, n)

    def body(*refs):
        a_ref, o_ref, acc = refs[0], refs[len(dys) + 1], refs[len(dys) + 2]

        @pl.when(pl.program_id(1) == 0)
        def _():
            acc[...] = jnp.zeros_like(acc)

        av = a_ref[...].astype(BF16)
        for j, (p, c0) in enumerate(where):
            acc[j] += _dot_tn(av, refs[1 + p][:, c0:c0 + n].astype(BF16))

        @pl.when(pl.program_id(1) == nsteps - 1)
        def _():
            o_ref[...] = acc[...].astype(o_ref.dtype)

    return pl.pallas_call(
        body, name=name, grid=(k // tk, nsteps),
        in_specs=[pl.BlockSpec((tm, tk), lambda kk, i: (i, kk))]
        + [pl.BlockSpec((tm, d.shape[1]), lambda kk, i: (i, 0)) for d in dys],
        out_specs=pl.BlockSpec((s, tk, n), lambda kk, i: (0, kk, 0)),
        out_shape=jax.ShapeDtypeStruct((s, k, n), BF16),
        scratch_shapes=[pltpu.VMEM((s, tk, n), F32)],
        compiler_params=pltpu.CompilerParams(dimension_semantics=("parallel", "arbitrary")),
    )(a, *dys)


def _rowcall(name, body, lp, tm, rows=(), prevs=(), vecs=(), outs=(), accs=(), scratch=(),
             reverse=False, seq=False):
    nt = lp // tm
    hb = tm // SUBLANES

    def ri(i):
        return nt - 1 - i if reverse else i

    in_specs, args = [], []
    for arr, w, cb in rows:
        in_specs.append(pl.BlockSpec((tm, w), lambda i, cb=cb: (ri(i), cb)))
        args.append(arr)
    for arr, w, cb in prevs:
        in_specs.append(pl.BlockSpec((SUBLANES, w), lambda i, cb=cb: (jnp.maximum(ri(i) * hb - 1, 0), cb)))
        args.append(arr)
    for arr in vecs:
        in_specs.append(pl.BlockSpec(arr.shape, lambda i, nd=arr.ndim: (0,) * nd))
        args.append(arr)
    out_shape, out_specs = [], []
    for w, dt in outs:
        out_shape.append(jax.ShapeDtypeStruct((lp, w), dt))
        out_specs.append(pl.BlockSpec((tm, w), lambda i: (ri(i), 0)))
    for shp, dt in accs:
        out_shape.append(jax.ShapeDtypeStruct(shp, dt))
        out_specs.append(pl.BlockSpec(shp, lambda i, nd=len(shp): (0,) * nd))

    def kern(*refs):
        i = pl.program_id(0)
        body(ri(i), i == 0, *refs)

    sem = ("arbitrary",) if (seq or accs) else ("parallel",)
    res = pl.pallas_call(
        kern, name=name, grid=(nt,), in_specs=in_specs, out_specs=out_specs,
        out_shape=out_shape, scratch_shapes=list(scratch),
        compiler_params=pltpu.CompilerParams(dimension_semantics=sem),
    )(*args)
    return res


def _acc_add(first, ref, val):
    @pl.when(first)
    def _():
        ref[...] = jnp.zeros_like(ref)

    ref[...] += val


def _real_rows(r, tm):
    return (r * tm + _row_iota(tm)) >= PAD


def _rmsnorm_fwd(h, g, name):
    lp, d = h.shape
    tm = _pick(lp, (640, 512, 256, 128))

    def body(r, first, h_ref, g_ref, u_ref):
        x = h_ref[...]
        rs = lax.rsqrt(jnp.mean(x * x, axis=-1, keepdims=True) + EPS)
        u_ref[...] = (x * rs * g_ref[...]).astype(u_ref.dtype)

    return _rowcall(name, body, lp, tm, rows=[(h, d, 0)], vecs=[g], outs=[(d, BF16)])[0]


def _postnorm_res_fwd(h, o, g, name):
    lp, d = h.shape
    tm = _pick(lp, (640, 512, 256, 128))

    def body(r, first, h_ref, o_ref, g_ref, out_ref):
        x = o_ref[...]
        rs = lax.rsqrt(jnp.mean(x * x, axis=-1, keepdims=True) + EPS)
        out_ref[...] = jnp.where(_real_rows(r, tm), h_ref[...] + x * rs * g_ref[...], 0.0)

    return _rowcall(name, body, lp, tm, rows=[(h, d, 0), (o, d, 0)], vecs=[g], outs=[(d, F32)])[0]


def _postnorm_bwd(o, g, dh, name):
    lp, d = o.shape
    tm = _pick(lp, (640, 512, 256, 128))

    def body(r, first, o_ref, dh_ref, g_ref, do_ref, dg_ref):
        dx, dgt = _rms_bwd(o_ref[...], g_ref[...], dh_ref[...])
        do_ref[...] = dx.astype(do_ref.dtype)
        _acc_add(first, dg_ref, jnp.sum(dgt, axis=0, keepdims=True))

    return _rowcall(name, body, lp, tm, rows=[(o, d, 0), (dh, d, 0)], vecs=[g],
                    outs=[(d, BF16)], accs=[((1, d), F32)])


def _prenorm_bwd(h, g, du, dh_res, name):
    lp, d = h.shape
    tm = _pick(lp, (640, 512, 256, 128))

    def body(r, first, h_ref, du_ref, dres_ref, g_ref, dh_ref, dg_ref):
        dx, dgt = _rms_bwd(h_ref[...], g_ref[...], du_ref[...])
        dh_ref[...] = jnp.where(_real_rows(r, tm), dres_ref[...] + dx, 0.0)
        _acc_add(first, dg_ref, jnp.sum(dgt, axis=0, keepdims=True))

    return _rowcall(name, body, lp, tm, rows=[(h, d, 0), (du, d, 0), (dh_res, d, 0)], vecs=[g],
                    outs=[(d, F32)], accs=[((1, d), F32)])


def _loss_fwd_bwd(h, tgt, name):
    lp, d = h.shape
    tm = _pick(lp, (640, 512, 256, 128))

    def body(r, first, h_ref, t_ref, dh_ref, ls_ref):
        tok = (r * tm + _row_iota(tm)) >= BLOCK
        e = jnp.where(tok, h_ref[...] - t_ref[...], 0.0)
        dh_ref[...] = e * (1.0 / d)
        _acc_add(first, ls_ref, jnp.sum(e * e, axis=0, keepdims=True))

    return _rowcall(name, body, lp, tm, rows=[(h, d, 0), (tgt, d, 0)],
                    outs=[(d, F32)], accs=[((1, d), F32)])


def _conv_tiles(lp, width):
    wc = _col_tile(width, 1408)
    tm = _pick(lp, (320, 256, 128))
    return tm, wc


def _conv_fwd(x, col_off, width, w, b, name):
    lp = x.shape[0]
    kk = w.shape[0]
    tm, wc = _conv_tiles(lp, width)
    offb = col_off // wc
    assert col_off % wc == 0
    hb = tm // SUBLANES

    def body(x_ref, xp_ref, w_ref, b_ref, y_ref):
        i = pl.program_id(1)
        xv = x_ref[...]
        halo = jnp.where(i > 0, xp_ref[...], 0.0)
        xx = jnp.concatenate([halo, xv], axis=0)
        acc = b_ref[...] + w_ref[kk - 1:kk, :] * xv
        for j in range(1, kk):
            acc = acc + w_ref[kk - 1 - j:kk - j, :] * pltpu.roll(xx, j, 0)[SUBLANES:, :]
        y_ref[...] = acc

    return pl.pallas_call(
        body, name=name, grid=(width // wc, lp // tm),
        in_specs=[pl.BlockSpec((tm, wc), lambda j, i: (i, offb + j)),
                  pl.BlockSpec((SUBLANES, wc), lambda j, i: (jnp.maximum(i * hb - 1, 0), offb + j)),
                  pl.BlockSpec((kk, wc), lambda j, i: (0, j)),
                  pl.BlockSpec((1, wc), lambda j, i: (0, j))],
        out_specs=pl.BlockSpec((tm, wc), lambda j, i: (i, j)),
        out_shape=jax.ShapeDtypeStruct((lp, width), F32),
        compiler_params=pltpu.CompilerParams(dimension_semantics=("parallel", "parallel")),
    )(x, x, w, b)


def _conv_bwd(x, col_off, width, dy, w, name):
    lp = x.shape[0]
    kk = w.shape[0]
    tm, wc = _conv_tiles(lp, width)
    offb = col_off // wc
    assert col_off % wc == 0
    hb = tm // SUBLANES
    nb8 = lp // SUBLANES

    def body(x_ref, xp_ref, dy_ref, dn_ref, w_ref, dx_ref, dw_ref, db_ref):
        i = pl.program_id(1)
        last = pl.num_programs(1) - 1
        xv = x_ref[...]
        dyv = dy_ref[...]
        xx = jnp.concatenate([jnp.where(i > 0, xp_ref[...], 0.0), xv], axis=0)
        dd = jnp.concatenate([dyv, jnp.where(i < last, dn_ref[...], 0.0)], axis=0)
        dx = w_ref[kk - 1:kk, :] * dyv
        rows = [jnp.sum(dyv * xv, axis=0, keepdims=True)]
        for m in range(1, kk):
            dx = dx + w_ref[kk - 1 - m:kk - m, :] * pltpu.roll(dd, tm + SUBLANES - m, 0)[:tm, :]
            rows.append(jnp.sum(dyv * pltpu.roll(xx, m, 0)[SUBLANES:, :], axis=0, keepdims=True))
        dx_ref[...] = dx.astype(dx_ref.dtype)
        dwp = jnp.concatenate(rows[::-1] + [jnp.zeros((SUBLANES - kk, wc), F32)], axis=0)

        @pl.when(i == 0)
        def _():
            dw_ref[...] = jnp.zeros_like(dw_ref)
            db_ref[...] = jnp.zeros_like(db_ref)

        dw_ref[...] += dwp
        db_ref[...] += jnp.sum(dyv, axis=0, keepdims=True)

    return pl.pallas_call(
        body, name=name, grid=(width // wc, lp // tm),
        in_specs=[pl.BlockSpec((tm, wc), lambda j, i: (i, offb + j)),
                  pl.BlockSpec((SUBLANES, wc), lambda j, i: (jnp.maximum(i * hb - 1, 0), offb + j)),
                  pl.BlockSpec((tm, wc), lambda j, i: (i, j)),
                  pl.BlockSpec((SUBLANES, wc), lambda j, i: (jnp.minimum((i + 1) * hb, nb8 - 1), j)),
                  pl.BlockSpec((kk, wc), lambda j, i: (0, j))],
        out_specs=[pl.BlockSpec((tm, wc), lambda j, i: (i, j)),
                   pl.BlockSpec((SUBLANES, wc), lambda j, i: (0, j)),
                   pl.BlockSpec((1, wc), lambda j, i: (0, j))],
        out_shape=[jax.ShapeDtypeStruct((lp, width), BF16),
                   jax.ShapeDtypeStruct((SUBLANES, width), F32),
                   jax.ShapeDtypeStruct((1, width), F32)],
        compiler_params=pltpu.CompilerParams(dimension_semantics=("parallel", "arbitrary")),
    )(x, x, dy, dy, w)


_FFN_K = 3
_FFN_WC = 1408


def _conv3_ext(x_ext, w_ref, b_ref):
    return (b_ref[...] + w_ref[2:3, :] * x_ext + w_ref[1:2, :] * pltpu.roll(x_ext, 1, 0)
            + w_ref[0:1, :] * pltpu.roll(x_ext, 2, 0))


def _ffn_convact_fwd(hp, cw, cb, name):
    lp = hp.shape[0]
    tm = _pick(lp, (320, 256, 128))
    wc = _FFN_WC
    nj = D_FF // wc
    hb = tm // SUBLANES

    def body(g_ref, gp_ref, u_ref, up_ref, wg_ref, wu_ref, bg_ref, bu_ref, a_ref):
        i = pl.program_id(1)

        def conv(x_ref, p_ref, w_ref, b_ref):
            x_ext = jnp.concatenate([jnp.where(i > 0, p_ref[...], 0.0), x_ref[...]], axis=0)
            return _conv3_ext(x_ext, w_ref, b_ref)[SUBLANES:, :]

        a_ref[...] = (_gelu(conv(g_ref, gp_ref, wg_ref, bg_ref))
                      * conv(u_ref, up_ref, wu_ref, bu_ref)).astype(a_ref.dtype)

    tile = lambda off: pl.BlockSpec((tm, wc), lambda j, i: (i, off + j))
    prev = lambda off: pl.BlockSpec((SUBLANES, wc), lambda j, i: (jnp.maximum(i * hb - 1, 0), off + j))
    vec = lambda rows, off: pl.BlockSpec((rows, wc), lambda j, i: (0, off + j))
    return pl.pallas_call(
        body, name=name, grid=(nj, lp // tm),
        in_specs=[tile(0), prev(0), tile(nj), prev(nj), vec(_FFN_K, 0), vec(_FFN_K, nj), vec(1, 0), vec(1, nj)],
        out_specs=pl.BlockSpec((tm, wc), lambda j, i: (i, j)),
        out_shape=jax.ShapeDtypeStruct((lp, D_FF), BF16),
        compiler_params=pltpu.CompilerParams(dimension_semantics=("parallel", "parallel")),
    )(hp, hp, hp, hp, cw, cw, cb, cb)


def _ffn_convact_bwd(hp, dact, cw, cb, name):
    lp = hp.shape[0]
    tm = _pick(lp, (160, 128))
    wc = _FFN_WC
    nj = D_FF // wc
    hb = tm // SUBLANES
    nb8 = lp // SUBLANES
    ext = tm + SUBLANES

    def body(g_ref, gp_ref, gn_ref, u_ref, up_ref, un_ref, da_ref, dan_ref, wg_ref, wu_ref, bg_ref, bu_ref,
             dg_ref, du_ref, dwg_ref, dwu_ref, dbg_ref, dbu_ref):
        i = pl.program_id(1)
        last = pl.num_programs(1) - 1

        def extended(x_ref, p_ref, n_ref):
            return jnp.concatenate([jnp.where(i > 0, p_ref[...], 0.0), x_ref[...],
                                    jnp.where(i < last, n_ref[...], 0.0)], axis=0)

        xg = extended(g_ref, gp_ref, gn_ref)
        xu = extended(u_ref, up_ref, un_ref)
        hg = _conv3_ext(xg, wg_ref, bg_ref)[SUBLANES:, :]
        hu = _conv3_ext(xu, wu_ref, bu_ref)[SUBLANES:, :]
        da = jnp.concatenate([da_ref[...], jnp.where(i < last, dan_ref[...], 0.0)], axis=0)
        gl, dgl = _gelu_and_grad(hg)
        dhg = da * hu * dgl
        dhu = da * gl

        @pl.when(i == 0)
        def _():
            for r in (dwg_ref, dwu_ref, dbg_ref, dbu_ref):
                r[...] = jnp.zeros_like(r)

        def back(dh, x_ext, w_ref, dx_ref, dw_ref, db_ref):
            dht = dh[:tm, :]
            dx = (w_ref[2:3, :] * dht + w_ref[1:2, :] * pltpu.roll(dh, ext - 1, 0)[:tm, :]
                  + w_ref[0:1, :] * pltpu.roll(dh, ext - 2, 0)[:tm, :])
            dx_ref[...] = dx.astype(dx_ref.dtype)
            rows = [jnp.sum(dht * pltpu.roll(x_ext, m, 0)[SUBLANES:SUBLANES + tm, :], axis=0, keepdims=True)
                    for m in (2, 1, 0)]
            dw_ref[...] += jnp.concatenate(rows + [jnp.zeros((SUBLANES - _FFN_K, wc), F32)], axis=0)
            db_ref[...] += jnp.sum(dht, axis=0, keepdims=True)

        back(dhg, xg, wg_ref, dg_ref, dwg_ref, dbg_ref)
        back(dhu, xu, wu_ref, du_ref, dwu_ref, dbu_ref)

    tile = lambda off: pl.BlockSpec((tm, wc), lambda j, i: (i, off + j))
    prev = lambda off: pl.BlockSpec((SUBLANES, wc), lambda j, i: (jnp.maximum(i * hb - 1, 0), off + j))
    nxt = lambda off: pl.BlockSpec((SUBLANES, wc), lambda j, i: (jnp.minimum((i + 1) * hb, nb8 - 1), off + j))
    vec = lambda rows, off: pl.BlockSpec((rows, wc), lambda j, i: (0, off + j))
    res = pl.pallas_call(
        body, name=name, grid=(nj, lp // tm),
        in_specs=[tile(0), prev(0), nxt(0), tile(nj), prev(nj), nxt(nj), tile(0), nxt(0),
                  vec(_FFN_K, 0), vec(_FFN_K, nj), vec(1, 0), vec(1, nj)],
        out_specs=[tile(0), tile(0), vec(SUBLANES, 0), vec(SUBLANES, 0), vec(1, 0), vec(1, 0)],
        out_shape=[jax.ShapeDtypeStruct((lp, D_FF), BF16), jax.ShapeDtypeStruct((lp, D_FF), BF16),
                   jax.ShapeDtypeStruct((SUBLANES, D_FF), F32), jax.ShapeDtypeStruct((SUBLANES, D_FF), F32),
                   jax.ShapeDtypeStruct((1, D_FF), F32), jax.ShapeDtypeStruct((1, D_FF), F32)],
        compiler_params=pltpu.CompilerParams(dimension_semantics=("parallel", "arbitrary")),
    )(hp, hp, hp, hp, hp, hp, dact, dact, cw, cw, cb, cb)
    dg, du, dwg, dwu, dbg, dbu = res
    return (dg, du), jnp.concatenate([dwg, dwu], axis=1), jnp.concatenate([dbg, dbu], axis=1)


def _lru_gates(x, wa_ref, wx_ref, ba, bx, lam):
    xb = x.astype(BF16)
    za, zx = [], []
    for n in range(LRU_BLOCKS):
        xs = xb[:, n * LRU_BS:(n + 1) * LRU_BS]
        za.append(_dot(xs, wa_ref[n]))
        zx.append(_dot(xs, wx_ref[n]))
    r = _sigmoid(jnp.concatenate(za, axis=1) + ba)
    ig = _sigmoid(jnp.concatenate(zx, axis=1) + bx)
    sp = _softplus(-lam)
    log_a = -LRU_C * r * sp
    a = jnp.exp(log_a)
    om = _neg_expm1(2.0 * log_a)
    mult = jnp.sqrt(om)
    return xb, r, ig, sp, a, om, mult


def _lru_fwd(proj, xrc, wa, wx, ba, bx, lam, name):
    lp, d = xrc.shape
    tm = BLOCK

    def body(r_idx, first, gate_ref, x_ref, wa_ref, wx_ref, ba_ref, bx_ref, lam_ref,
             y_ref, h_ref, carry):
        @pl.when(first)
        def _():
            carry[...] = jnp.zeros_like(carry)

        x = x_ref[...]
        _, _, ig, _, a, _, mult = _lru_gates(x, wa_ref, wx_ref, ba_ref[...], bx_ref[...], lam_ref[...])
        u = jnp.where(_real_rows(r_idx, tm), mult * ig * x, 0.0)
        acum, hloc = _scan_fwd(a, u, tm)
        h = hloc + acum * carry[0:1, :]
        h_ref[...] = h
        carry[0:1, :] = h[tm - 1:tm, :]
        y_ref[...] = (_gelu(gate_ref[...]) * h).astype(y_ref.dtype)

    return _rowcall(name, body, lp, tm, rows=[(proj, d, 0), (xrc, d, 0)],
                    vecs=[wa, wx, ba, bx, lam], outs=[(d, BF16), (d, F32)],
                    scratch=[pltpu.VMEM((SUBLANES, d), F32)], seq=True)


def _lru_bwd(proj, xrc, hl, dmix, wa, wx, ba, bx, lam, name):
    lp, d = xrc.shape
    tm = BLOCK

    def body(r_idx, first, gate_ref, x_ref, h_ref, dy_ref, hp_ref, wa_ref, wx_ref, ba_ref, bx_ref,
             lam_ref, dgate_ref, dx_ref, dwa_ref, dwx_ref, dba_ref, dbx_ref, dlam_ref, carry):
        @pl.when(first)
        def _():
            carry[...] = jnp.zeros_like(carry)
            dwa_ref[...] = jnp.zeros_like(dwa_ref)
            dwx_ref[...] = jnp.zeros_like(dwx_ref)
            dba_ref[...] = jnp.zeros_like(dba_ref)
            dbx_ref[...] = jnp.zeros_like(dbx_ref)
            dlam_ref[...] = jnp.zeros_like(dlam_ref)

        x = x_ref[...]
        lam = lam_ref[...]
        xb, r, ig, sp, a, om, mult = _lru_gates(x, wa_ref, wx_ref, ba_ref[...], bx_ref[...], lam)
        h = h_ref[...]
        dy = dy_ref[...]
        gl, dgl = _gelu_and_grad(gate_ref[...])
        dgate_ref[...] = (dy * h * dgl).astype(dgate_ref.dtype)
        row = _row_iota(tm)
        lastrow = row == tm - 1
        xg = dy * gl + jnp.where(lastrow, carry[0:1, :], 0.0)
        c = jnp.where(lastrow, 1.0, pltpu.roll(a, tm - 1, 0))
        _, g = _scan_rev(c, xg, tm)
        carry[0:1, :] = a[0:1, :] * g[0:1, :]
        hprev_in = jnp.where(r_idx > 0, hp_ref[SUBLANES - 1:SUBLANES, :], 0.0)
        hprev = jnp.where(row == 0, hprev_in, pltpu.roll(h, 1, 0))
        du = jnp.where(_real_rows(r_idx, tm), g, 0.0)
        da = g * hprev
        dmult = du * ig * x
        dig = du * mult * x
        dxv = du * mult * ig
        e2 = 1.0 - om
        dlog_a = da * a - dmult * e2 / mult
        dr = dlog_a * (-LRU_C) * sp
        dsp = jnp.sum(dlog_a * (-LRU_C) * r, axis=0, keepdims=True)
        dlam_ref[...] += -dsp * _sigmoid(-lam)
        dza = dr * r * (1.0 - r)
        dzx = dig * ig * (1.0 - ig)
        dba_ref[...] += jnp.sum(dza, axis=0, keepdims=True)
        dbx_ref[...] += jnp.sum(dzx, axis=0, keepdims=True)
        dzab = dza.astype(BF16)
        dzxb = dzx.astype(BF16)
        parts = []
        for n in range(LRU_BLOCKS):
            sl = slice(n * LRU_BS, (n + 1) * LRU_BS)
            dwa_ref[n] += _dot_tn(xb[:, sl], dzab[:, sl])
            dwx_ref[n] += _dot_tn(xb[:, sl], dzxb[:, sl])
            parts.append(_dot_nt(dzab[:, sl], wa_ref[n]) + _dot_nt(dzxb[:, sl], wx_ref[n]))
        dx_ref[...] = dxv + jnp.concatenate(parts, axis=1)

    return _rowcall(name, body, lp, tm,
                    rows=[(proj, d, 0), (xrc, d, 0), (hl, d, 0), (dmix, d, 0)],
                    prevs=[(hl, d, 0)], vecs=[wa, wx, ba, bx, lam],
                    outs=[(d, BF16), (d, F32)],
                    accs=[((LRU_BLOCKS, LRU_BS, LRU_BS), F32), ((LRU_BLOCKS, LRU_BS, LRU_BS), F32),
                          ((1, d), F32), ((1, d), F32), ((1, d), F32)],
                    scratch=[pltpu.VMEM((SUBLANES, d), F32)], reverse=True, seq=True)


_SLOPES = [2.0 ** (-8.0 * (h + 1) / N_Q_HEADS) for h in range(N_Q_HEADS)]
_QK_SCALE = HEAD_DIM ** -0.5
_QCOL = 2 * D_MODEL // D_MODEL
_KCOL = (3 * D_MODEL) // LANES
_VCOL = _KCOL + 1


def _attn_masks(n):
    start = pl.multiple_of(jnp.maximum(n - 1, 0) * BLOCK, BLOCK)
    qi = n * BLOCK + lax.broadcasted_iota(jnp.int32, (BLOCK, 2 * BLOCK), 0)
    kj = start + lax.broadcasted_iota(jnp.int32, (BLOCK, 2 * BLOCK), 1)
    dist = qi - kj
    ok = (kj >= BLOCK) & (dist >= 0) & (dist < WINDOW)
    dm = (n * BLOCK - PAD + lax.broadcasted_iota(jnp.int32, (BLOCK, N_META), 0)
          - lax.broadcasted_iota(jnp.int32, (BLOCK, N_META), 1))
    okm = dm >= 0
    return start, ok, dist.astype(F32), okm, jnp.minimum(dm, WINDOW).astype(F32)


def _group_rows(ref, g):
    return jnp.concatenate(
        [ref[:, (g * Q_PER_KV + hh) * HEAD_DIM:(g * Q_PER_KV + hh + 1) * HEAD_DIM] for hh in range(Q_PER_KV)],
        axis=0).astype(BF16)


def _attn_probs(qg, kg, kmg, sink_ref, g, ok, distf, okm, dmf):
    slope = jnp.stack([jnp.full((1, 1), _SLOPES[g * Q_PER_KV + hh], F32) for hh in range(Q_PER_KV)])
    sink = jnp.stack([sink_ref[0:1, g * Q_PER_KV + hh:g * Q_PER_KV + hh + 1] for hh in range(Q_PER_KV)])
    s = (_dot_nt(qg, kg) * _QK_SCALE).reshape(Q_PER_KV, BLOCK, 2 * BLOCK)
    sm = (_dot_nt(qg, kmg) * _QK_SCALE).reshape(Q_PER_KV, BLOCK, N_META)
    s = jnp.where(ok[None], s - slope * distf[None], NEG)
    sm = jnp.where(okm[None], sm - slope * dmf[None], NEG)
    mx = jnp.maximum(jnp.maximum(jnp.max(s, axis=-1, keepdims=True),
                                 jnp.max(sm, axis=-1, keepdims=True)), sink)
    p = jnp.exp(s - mx)
    pm = jnp.exp(sm - mx)
    ps = jnp.exp(sink - mx)
    inv = 1.0 / (jnp.sum(p, axis=-1, keepdims=True) + jnp.sum(pm, axis=-1, keepdims=True) + ps)
    return p * inv, pm * inv, ps * inv


def _attn_fwd(proj, sinks, name, plan=None):
    lp = proj.shape[0]
    nblk = lp // BLOCK
    p_in, p_shapes, p_out, p_scr = _plan_parts(plan)

    def body(*refs):
        q_ref, k_ref, v_ref, sink_ref = refs[:4]
        cins = refs[4:4 + len(p_in)]
        o_ref = refs[4 + len(p_in)]
        couts = refs[5 + len(p_in):5 + len(p_in) + len(p_out)]
        sems = refs[5 + len(p_in) + len(p_out):]
        n = pl.program_id(0)
        if plan is not None:
            @pl.when(n == 0)
            def _():
                plan.start(cins, couts, sems)

        start, ok, distf, okm, dmf = _attn_masks(n)
        kb = k_ref[pl.ds(start, 2 * BLOCK), :].astype(BF16)
        vb = v_ref[pl.ds(start, 2 * BLOCK), :].astype(BF16)
        km = k_ref[PAD:BLOCK, :].astype(BF16)
        vm = v_ref[PAD:BLOCK, :].astype(BF16)
        for g in range(N_KV_HEADS):
            gs = slice(g * HEAD_DIM, (g + 1) * HEAD_DIM)
            pn, pmn, _ = _attn_probs(_group_rows(q_ref, g), kb[:, gs], km[:, gs], sink_ref, g,
                                     ok, distf, okm, dmf)
            o = (_dot(pn.astype(BF16).reshape(Q_PER_KV * BLOCK, 2 * BLOCK), vb[:, gs])
                 + _dot(pmn.astype(BF16).reshape(Q_PER_KV * BLOCK, N_META), vm[:, gs]))
            for hh in range(Q_PER_KV):
                h = g * Q_PER_KV + hh
                o_ref[:, h * HEAD_DIM:(h + 1) * HEAD_DIM] = o[hh * BLOCK:(hh + 1) * BLOCK, :].astype(o_ref.dtype)
        if plan is not None:
            @pl.when(n == nblk - 1)
            def _():
                plan.wait(cins, couts, sems)

    res = pl.pallas_call(
        body, name=name, grid=(nblk,),
        in_specs=[pl.BlockSpec((BLOCK, D_MODEL), lambda n: (n, _QCOL)),
                  pl.BlockSpec((lp, LANES), lambda n: (0, _KCOL)),
                  pl.BlockSpec((lp, LANES), lambda n: (0, _VCOL)),
                  pl.BlockSpec(sinks.shape, lambda n: (0, 0))] + p_in,
        out_specs=[pl.BlockSpec((BLOCK, D_MODEL), lambda n: (n, 0))] + p_out,
        out_shape=[jax.ShapeDtypeStruct((lp, D_MODEL), BF16)] + p_shapes,
        scratch_shapes=p_scr,
        compiler_params=pltpu.CompilerParams(dimension_semantics=("arbitrary",)),
    )(proj, proj, proj, sinks, *(plan.ins if plan is not None else []))
    return res[0], res[1:]


def _attn_bwd(proj, sinks, dmix, name, plan=None):
    lp = proj.shape[0]
    nblk = lp // BLOCK

    p_in, p_shapes, p_out, p_scr = _plan_parts(plan)

    def body(*refs):
        q_ref, k_ref, v_ref, sink_ref, dy_ref = refs[:5]
        cins = refs[5:5 + len(p_in)]
        dq_ref, dk_ref, dv_ref, ds_ref = refs[5 + len(p_in):9 + len(p_in)]
        couts = refs[9 + len(p_in):9 + len(p_in) + len(p_out)]
        sems = refs[9 + len(p_in) + len(p_out):]
        n = pl.program_id(0)

        @pl.when(n == 0)
        def _():
            dk_ref[...] = jnp.zeros_like(dk_ref)
            dv_ref[...] = jnp.zeros_like(dv_ref)
            ds_ref[...] = jnp.zeros_like(ds_ref)
            if plan is not None:
                plan.start(cins, couts, sems)

        start, ok, distf, okm, dmf = _attn_masks(n)
        kb = k_ref[pl.ds(start, 2 * BLOCK), :].astype(BF16)
        vb = v_ref[pl.ds(start, 2 * BLOCK), :].astype(BF16)
        km = k_ref[PAD:BLOCK, :].astype(BF16)
        vm = v_ref[PAD:BLOCK, :].astype(BF16)
        lane16 = lax.broadcasted_iota(jnp.int32, (1, N_Q_HEADS), 1)
        dsink = jnp.zeros((1, N_Q_HEADS), F32)
        rows = Q_PER_KV * BLOCK
        for g in range(N_KV_HEADS):
            gs = slice(g * HEAD_DIM, (g + 1) * HEAD_DIM)
            qg = _group_rows(q_ref, g)
            dog = _group_rows(dy_ref, g)
            pn, pmn, psn = _attn_probs(qg, kb[:, gs], km[:, gs], sink_ref, g, ok, distf, okm, dmf)
            dp = _dot_nt(dog, vb[:, gs]).reshape(Q_PER_KV, BLOCK, 2 * BLOCK)
            dpm = _dot_nt(dog, vm[:, gs]).reshape(Q_PER_KV, BLOCK, N_META)
            delta = (jnp.sum(pn * dp, axis=-1, keepdims=True)
                     + jnp.sum(pmn * dpm, axis=-1, keepdims=True))
            dsb = (pn * (dp - delta)).astype(BF16).reshape(rows, 2 * BLOCK)
            dsm = (pmn * (dpm - delta)).astype(BF16).reshape(rows, N_META)
            dsk = jnp.sum(psn * delta, axis=1, keepdims=True)
            for hh in range(Q_PER_KV):
                dsink = dsink - jnp.where(lane16 == g * Q_PER_KV + hh, dsk[hh], 0.0)
            dq = (_dot(dsb, kb[:, gs]) + _dot(dsm, km[:, gs])) * _QK_SCALE
            for hh in range(Q_PER_KV):
                h = g * Q_PER_KV + hh
                dq_ref[:, h * HEAD_DIM:(h + 1) * HEAD_DIM] = dq[hh * BLOCK:(hh + 1) * BLOCK, :].astype(dq_ref.dtype)
            pnb = pn.astype(BF16).reshape(rows, 2 * BLOCK)
            pmnb = pmn.astype(BF16).reshape(rows, N_META)
            dk_ref[pl.ds(start, 2 * BLOCK), gs] += _dot_tn(dsb, qg) * _QK_SCALE
            dv_ref[pl.ds(start, 2 * BLOCK), gs] += _dot_tn(pnb, dog)
            dk_ref[PAD:BLOCK, gs] += _dot_tn(dsm, qg) * _QK_SCALE
            dv_ref[PAD:BLOCK, gs] += _dot_tn(pmnb, dog)
        ds_ref[...] += dsink
        if plan is not None:
            @pl.when(n == nblk - 1)
            def _():
                plan.wait(cins, couts, sems)

    res = pl.pallas_call(
        body, name=name, grid=(nblk,),
        in_specs=[pl.BlockSpec((BLOCK, D_MODEL), lambda n: (n, _QCOL)),
                  pl.BlockSpec((lp, LANES), lambda n: (0, _KCOL)),
                  pl.BlockSpec((lp, LANES), lambda n: (0, _VCOL)),
                  pl.BlockSpec(sinks.shape, lambda n: (0, 0)),
                  pl.BlockSpec((BLOCK, D_MODEL), lambda n: (n, 1))] + p_in,
        out_specs=[pl.BlockSpec((BLOCK, D_MODEL), lambda n: (n, 0)),
                   pl.BlockSpec((lp, LANES), lambda n: (0, 0)),
                   pl.BlockSpec((lp, LANES), lambda n: (0, 0)),
                   pl.BlockSpec((1, N_Q_HEADS), lambda n: (0, 0))] + p_out,
        out_shape=[jax.ShapeDtypeStruct((lp, D_MODEL), BF16),
                   jax.ShapeDtypeStruct((lp, LANES), F32),
                   jax.ShapeDtypeStruct((lp, LANES), F32),
                   jax.ShapeDtypeStruct((1, N_Q_HEADS), F32)] + p_shapes,
        scratch_shapes=p_scr,
        compiler_params=pltpu.CompilerParams(dimension_semantics=("arbitrary",)),
    )(proj, proj, proj, sinks, dmix, *(plan.ins if plan is not None else []))
    return res[:4], res[4:]


_ZW = D_SSM
_XBC_W = D_SSM + 2 * SSD_GROUPS * SSD_N
_DT_COL = (_ZW + _XBC_W) // LANES
EVEN_IN = 3 * D_MODEL + 2 * LANES
ODD_IN = _ZW + _XBC_W + SSD_HEADS
ODD_IN_PAD = _ZW + _XBC_W + LANES


def _ssm_prep_fwd(xc, proj, dt_bias, name):
    lp = xc.shape[0]
    tm = _pick(lp, (320, 256, 128))

    def body(r, first, xc_ref, dtr_ref, b_ref, act_ref, dt_ref):
        real = _real_rows(r, tm)
        act, _ = _silu_and_grad(xc_ref[...])
        act_ref[...] = jnp.where(real, act, 0.0)
        dt_ref[...] = jnp.where(real, _softplus(dtr_ref[...] + b_ref[...]), 0.0)

    return _rowcall(name, body, lp, tm, rows=[(xc, _XBC_W, 0), (proj, LANES, _DT_COL)],
                    vecs=[dt_bias], outs=[(_XBC_W, F32), (LANES, F32)])


def _ssm_prep_bwd(xc, proj, dt_bias, dxs, dxskip, db, dc, ddt, name):
    lp = xc.shape[0]
    tm = BLOCK

    def body(r, first, xc_ref, dtr_ref, dxs_ref, dsk_ref, db_ref, dc_ref, ddt_ref, b_ref,
             dxc_ref, ddtr_ref, dbias_ref):
        real = _real_rows(r, tm)
        _, ds = _silu_and_grad(xc_ref[...])
        dxc_ref[:, :D_SSM] = jnp.where(real, (dxs_ref[...] + dsk_ref[...]) * ds[:, :D_SSM], 0.0)
        dxc_ref[:, D_SSM:D_SSM + 1024] = jnp.where(real, db_ref[...] * ds[:, D_SSM:D_SSM + 1024], 0.0)
        dxc_ref[:, D_SSM + 1024:] = jnp.where(real, dc_ref[...] * ds[:, D_SSM + 1024:], 0.0)
        dd = jnp.where(real, ddt_ref[...] * _sigmoid(dtr_ref[...] + b_ref[...]), 0.0)
        ddtr_ref[...] = dd.astype(ddtr_ref.dtype)
        _acc_add(first, dbias_ref, jnp.sum(dd, axis=0, keepdims=True))

    return _rowcall(name, body, lp, tm,
                    rows=[(xc, _XBC_W, 0), (proj, LANES, _DT_COL), (dxs, D_SSM, 0), (dxskip, D_SSM, 0),
                          (db, 1024, 0), (dc, 1024, 0), (ddt, LANES, 0)],
                    vecs=[dt_bias], outs=[(_XBC_W, F32), (LANES, BF16)], accs=[((1, LANES), F32)])


def _ssd_common(dt, alog):
    a = -jnp.exp(alog)
    cs = _cumsum_rows(dt * a, BLOCK)
    cst = cs.T
    cl = cs[BLOCK - 1:BLOCK, :]
    tril = (lax.broadcasted_iota(jnp.int32, (BLOCK, BLOCK), 0)
            >= lax.broadcasted_iota(jnp.int32, (BLOCK, BLOCK), 1))
    return a, cs, cst, cl, jnp.exp(cs), jnp.exp(cl - cs), jnp.exp(cl), tril


def _head_cols(ecl, g):
    lane = lax.broadcasted_iota(jnp.int32, (1, SSD_HPG * SSD_P), 1)
    e = [ecl[:, SSD_HPG * g + hh:SSD_HPG * g + hh + 1] for hh in range(SSD_HPG)]
    return jnp.where(lane < SSD_P, e[0], jnp.where(lane < 2 * SSD_P, e[1],
                                                   jnp.where(lane < 3 * SSD_P, e[2], e[3])))


def _ssd_fwd(xbc, dt, alog, name, plan=None):
    lp = xbc.shape[0]
    nc = lp // BLOCK
    gw = SSD_HPG * SSD_P
    p_in, p_shapes, p_out, p_scr = _plan_parts(plan)

    def body(*refs):
        xs_ref, b_ref, c_ref, dt_ref, alog_ref = refs[:5]
        cins = refs[5:5 + len(p_in)]
        y_ref, so_ref = refs[5 + len(p_in):7 + len(p_in)]
        couts = refs[7 + len(p_in):7 + len(p_in) + len(p_out)]
        st, fx = refs[7 + len(p_in) + len(p_out):9 + len(p_in) + len(p_out)]
        sems = refs[9 + len(p_in) + len(p_out):]
        n = pl.program_id(0)

        @pl.when(n == 0)
        def _():
            st[...] = jnp.zeros_like(st)
            if plan is not None:
                plan.start(cins, couts, sems)

        dtv = dt_ref[...]
        _, cs, cst, cl, e, f, ecl, tril = _ssd_common(dtv, alog_ref[...])
        for g in range(SSD_GROUPS):
            bg = b_ref[:, g * SSD_N:(g + 1) * SSD_N].astype(BF16)
            cg = c_ref[:, g * SSD_N:(g + 1) * SSD_N].astype(BF16)
            gm = _dot_nt(cg, bg)
            stg = st[g]
            so_ref[0, g] = stg
            yoff = _dot(cg, stg.astype(BF16))
            for hh in range(SSD_HPG):
                h = SSD_HPG * g + hh
                hs = slice(h * SSD_P, (h + 1) * SSD_P)
                seg = cs[:, h:h + 1] - cst[h:h + 1, :]
                m = gm * jnp.exp(jnp.where(tril, seg, NEG))
                xdt = xs_ref[:, hs] * dtv[:, h:h + 1]
                y_ref[:, hs] = (_dot(m.astype(BF16), xdt.astype(BF16))
                                + e[:, h:h + 1] * yoff[:, hh * SSD_P:(hh + 1) * SSD_P])
                fx[:, hh * SSD_P:(hh + 1) * SSD_P] = f[:, h:h + 1] * xdt
            st[g] = stg * _head_cols(ecl, g) + _dot_tn(bg, fx[...].astype(BF16))
        if plan is not None:
            @pl.when(n == nc - 1)
            def _():
                plan.wait(cins, couts, sems)

    res = pl.pallas_call(
        body, name=name, grid=(nc,),
        in_specs=[pl.BlockSpec((BLOCK, D_SSM), lambda n: (n, 0)),
                  pl.BlockSpec((BLOCK, 1024), lambda n: (n, 2)),
                  pl.BlockSpec((BLOCK, 1024), lambda n: (n, 3)),
                  pl.BlockSpec((BLOCK, LANES), lambda n: (n, 0)),
                  pl.BlockSpec((1, LANES), lambda n: (0, 0))] + p_in,
        out_specs=[pl.BlockSpec((BLOCK, D_SSM), lambda n: (n, 0)),
                   pl.BlockSpec((1, SSD_GROUPS, SSD_N, gw), lambda n: (n, 0, 0, 0))] + p_out,
        out_shape=[jax.ShapeDtypeStruct((lp, D_SSM), F32),
                   jax.ShapeDtypeStruct((nc, SSD_GROUPS, SSD_N, gw), F32)] + p_shapes,
        scratch_shapes=[pltpu.VMEM((SSD_GROUPS, SSD_N, gw), F32), pltpu.VMEM((BLOCK, gw), F32)] + p_scr,
        compiler_params=pltpu.CompilerParams(dimension_semantics=("arbitrary",)),
    )(xbc, xbc, xbc, dt, alog, *(plan.ins if plan is not None else []))
    return res[:2], res[2:]


def _ssd_bwd(xbc, dt, alog, states, dy, name, plan=None):
    lp = xbc.shape[0]
    nc = lp // BLOCK
    gw = SSD_HPG * SSD_P
    p_in, p_shapes, p_out, p_scr = _plan_parts(plan)

    def body(*refs):
        xs_ref, b_ref, c_ref, dt_ref, alog_ref, dy_ref, st_ref = refs[:7]
        cins = refs[7:7 + len(p_in)]
        dxs_ref, db_ref, dc_ref, ddt_ref, dalog_ref = refs[7 + len(p_in):12 + len(p_in)]
        couts = refs[12 + len(p_in):12 + len(p_in) + len(p_out)]
        dst, edy, fx = refs[12 + len(p_in) + len(p_out):15 + len(p_in) + len(p_out)]
        sems = refs[15 + len(p_in) + len(p_out):]
        i = pl.program_id(0)

        @pl.when(i == 0)
        def _():
            dst[...] = jnp.zeros_like(dst)
            dalog_ref[...] = jnp.zeros_like(dalog_ref)
            if plan is not None:
                plan.start(cins, couts, sems)

        dtv = dt_ref[...]
        a, cs, cst, cl, e, f, ecl, tril = _ssd_common(dtv, alog_ref[...])
        lane = lax.broadcasted_iota(jnp.int32, (1, LANES), 1)
        sub = _row_iota(BLOCK)
        dcs = jnp.zeros((BLOCK, LANES), F32)
        dcst = jnp.zeros((LANES, BLOCK), F32)
        dcl = jnp.zeros((1, LANES), F32)
        ddtx = jnp.zeros((BLOCK, LANES), F32)
        for g in range(SSD_GROUPS):
            bg = b_ref[:, g * SSD_N:(g + 1) * SSD_N].astype(BF16)
            cg = c_ref[:, g * SSD_N:(g + 1) * SSD_N].astype(BF16)
            gm = _dot_nt(cg, bg)
            stg = st_ref[0, g]
            stb = stg.astype(BF16)
            dso = dst[g]
            dsob = dso.astype(BF16)
            yraw = _dot(cg, stb)
            dfx = _dot(bg, dsob)
            prodsum = jnp.sum(dso * stg, axis=0, keepdims=True)
            dgm = jnp.zeros((BLOCK, BLOCK), F32)
            for hh in range(SSD_HPG):
                h = SSD_HPG * g + hh
                hs = slice(h * SSD_P, (h + 1) * SSD_P)
                ls = slice(hh * SSD_P, (hh + 1) * SSD_P)
                eh = e[:, h:h + 1]
                fh = f[:, h:h + 1]
                dth = dtv[:, h:h + 1]
                xh = xs_ref[:, hs]
                xdt = xh * dth
                dyh = dy_ref[:, hs]
                dyb = dyh.astype(BF16)
                lam = jnp.exp(jnp.where(tril, cs[:, h:h + 1] - cst[h:h + 1, :], NEG))
                m = gm * lam
                dm = _dot_nt(dyb, xdt.astype(BF16))
                dxdt = _dot_tn(m.astype(BF16), dyb) + fh * dfx[:, ls]
                w = dm * m
                dgm = dgm + dm * lam
                dff = jnp.sum(dfx[:, ls] * xdt, axis=1, keepdims=True) * fh
                col = (jnp.sum(w, axis=1, keepdims=True)
                       + jnp.sum(dyh * yraw[:, ls], axis=1, keepdims=True) * eh - dff)
                onl = (lane == h).astype(F32)
                dcs = dcs + col * onl
                dcst = dcst - (sub == h).astype(F32) * jnp.sum(w, axis=0, keepdims=True)
                dclh = (jnp.sum(dff, axis=0, keepdims=True)
                        + ecl[:, h:h + 1] * jnp.sum(prodsum[:, ls], axis=1, keepdims=True))
                dcl = dcl + dclh * onl
                ddtx = ddtx + jnp.sum(dxdt * xh, axis=1, keepdims=True) * onl
                dxs_ref[:, hs] = dxdt * dth
                edy[:, ls] = eh * dyh
                fx[:, ls] = fh * xdt
            edyb = edy[...].astype(BF16)
            fxb = fx[...].astype(BF16)
            dgb = dgm.astype(BF16)
            dc_ref[:, g * SSD_N:(g + 1) * SSD_N] = _dot_nt(edyb, stb) + _dot(dgb, bg)
            db_ref[:, g * SSD_N:(g + 1) * SSD_N] = _dot_nt(fxb, dsob) + _dot_tn(dgb, cg)
            dst[g] = dso * _head_cols(ecl, g) + _dot_tn(cg, edyb)
        dcs = dcs + dcst.T + jnp.where(sub == BLOCK - 1, dcl, 0.0)
        dda = _rev_cumsum_rows(dcs, BLOCK)
        ddt_ref[...] = ddtx + dda * a
        dalog_ref[...] += jnp.sum(dda * dtv, axis=0, keepdims=True) * a
        if plan is not None:
            @pl.when(i == nc - 1)
            def _():
                plan.wait(cins, couts, sems)

    rev = lambda i: nc - 1 - i
    res = pl.pallas_call(
        body, name=name, grid=(nc,),
        in_specs=[pl.BlockSpec((BLOCK, D_SSM), lambda i: (rev(i), 0)),
                  pl.BlockSpec((BLOCK, 1024), lambda i: (rev(i), 2)),
                  pl.BlockSpec((BLOCK, 1024), lambda i: (rev(i), 3)),
                  pl.BlockSpec((BLOCK, LANES), lambda i: (rev(i), 0)),
                  pl.BlockSpec((1, LANES), lambda i: (0, 0)),
                  pl.BlockSpec((BLOCK, D_SSM), lambda i: (rev(i), 0)),
                  pl.BlockSpec((1, SSD_GROUPS, SSD_N, gw), lambda i: (rev(i), 0, 0, 0))] + p_in,
        out_specs=[pl.BlockSpec((BLOCK, D_SSM), lambda i: (rev(i), 0)),
                   pl.BlockSpec((BLOCK, 1024), lambda i: (rev(i), 0)),
                   pl.BlockSpec((BLOCK, 1024), lambda i: (rev(i), 0)),
                   pl.BlockSpec((BLOCK, LANES), lambda i: (rev(i), 0)),
                   pl.BlockSpec((1, LANES), lambda i: (0, 0))] + p_out,
        out_shape=[jax.ShapeDtypeStruct((lp, D_SSM), F32),
                   jax.ShapeDtypeStruct((lp, 1024), F32),
                   jax.ShapeDtypeStruct((lp, 1024), F32),
                   jax.ShapeDtypeStruct((lp, LANES), F32),
                   jax.ShapeDtypeStruct((1, LANES), F32)] + p_shapes,
        scratch_shapes=[pltpu.VMEM((SSD_GROUPS, SSD_N, gw), F32),
                        pltpu.VMEM((BLOCK, gw), F32), pltpu.VMEM((BLOCK, gw), F32)] + p_scr,
        compiler_params=pltpu.CompilerParams(dimension_semantics=("arbitrary",)),
    )(xbc, xbc, xbc, dt, alog, dy, states, *(plan.ins if plan is not None else []))
    return res[:5], res[5:]


_GN_GROUPS = 8
_GN_W = D_SSM // _GN_GROUPS


def _ssm_gate_fwd(yssd, xbc, proj, dskip, gnorm, name):
    lp = yssd.shape[0]
    tm = _pick(lp, (320, 256, 128))

    def body(r, first, y_ref, x_ref, z_ref, d_ref, g_ref, o_ref):
        sz, _ = _silu_and_grad(z_ref[...])
        y2 = (y_ref[...] + d_ref[...] * x_ref[...]) * sz
        for k in range(_GN_GROUPS):
            sl = slice(k * _GN_W, (k + 1) * _GN_W)
            yk = y2[:, sl]
            rs = lax.rsqrt(jnp.mean(yk * yk, axis=-1, keepdims=True) + EPS)
            o_ref[:, sl] = (yk * rs * g_ref[:, sl]).astype(o_ref.dtype)

    return _rowcall(name, body, lp, tm, rows=[(yssd, D_SSM, 0), (xbc, D_SSM, 0), (proj, D_SSM, 0)],
                    vecs=[dskip, gnorm], outs=[(D_SSM, BF16)])[0]


def _ssm_gate_bwd(yssd, xbc, proj, dskip, gnorm, dyn, name):
    lp = yssd.shape[0]
    tm = BLOCK

    def body(r, first, y_ref, x_ref, z_ref, dyn_ref, d_ref, g_ref,
             dy_ref, dx_ref, dz_ref, dd_ref, dg_ref):
        z = z_ref[...]
        sz, dsz = _silu_and_grad(z)
        xs = x_ref[...]
        y1 = y_ref[...] + d_ref[...] * xs
        y2 = y1 * sz
        dyn = dyn_ref[...]
        for k in range(_GN_GROUPS):
            sl = slice(k * _GN_W, (k + 1) * _GN_W)
            dx, dgt = _rms_bwd(y2[:, sl], g_ref[:, sl], dyn[:, sl])
            dy1 = dx * sz[:, sl]
            dy_ref[:, sl] = dy1
            dx_ref[:, sl] = dy1 * d_ref[:, sl]
            dz_ref[:, sl] = (dx * y1[:, sl] * dsz[:, sl]).astype(dz_ref.dtype)

            @pl.when(first)
            def _():
                dd_ref[:, sl] = jnp.zeros((1, _GN_W), F32)
                dg_ref[:, sl] = jnp.zeros((1, _GN_W), F32)

            dd_ref[:, sl] += jnp.sum(dy1 * xs[:, sl], axis=0, keepdims=True)
            dg_ref[:, sl] += jnp.sum(dgt, axis=0, keepdims=True)

    return _rowcall(name, body, lp, tm,
                    rows=[(yssd, D_SSM, 0), (xbc, D_SSM, 0), (proj, D_SSM, 0), (dyn, D_SSM, 0)],
                    vecs=[dskip, gnorm], outs=[(D_SSM, F32), (D_SSM, F32), (D_SSM, BF16)],
                    accs=[((1, D_SSM), F32), ((1, D_SSM), F32)])


def _adamw(w, g, m, v, name):
    r, c = w.shape
    tm = r if r <= 512 else _pick(r, (512, 352, 256, 128, 64, 32, 16, 8))
    c1 = 1.0 / (1.0 - ADAM_B1 ** ADAM_STEP)
    c2 = 1.0 / (1.0 - ADAM_B2 ** ADAM_STEP)

    def body(w_ref, g_ref, m_ref, v_ref, d_ref, nm_ref, nv_ref):
        gv = g_ref[...]
        nm = ADAM_B1 * m_ref[...] + (1.0 - ADAM_B1) * gv
        nv = ADAM_B2 * v_ref[...] + (1.0 - ADAM_B2) * (gv * gv)
        nm_ref[...] = nm
        nv_ref[...] = nv
        d_ref[...] = -ADAM_LR * ((nm * c1) / (jnp.sqrt(nv * c2) + ADAM_EPS) + ADAM_WD * w_ref[...])

    spec = pl.BlockSpec((tm, c), lambda i: (i, 0))
    return pl.pallas_call(
        body, name=name, grid=(r // tm,), in_specs=[spec] * 4, out_specs=[spec] * 3,
        out_shape=[jax.ShapeDtypeStruct((r, c), F32)] * 3,
        compiler_params=pltpu.CompilerParams(dimension_semantics=("parallel",)),
    )(w, g, m, v)


def _place():
    return lax.axis_index("x"), lax.axis_index("y"), lax.axis_index("c")


def _other_chips(x, y):
    return [(1 - x, y), (x, 1 - y), (1 - x, 1 - y)]


_ANY = pl.BlockSpec(memory_space=pl.ANY)


class _Plan:
    def __init__(self, ins, out_shapes, n_remote, n_local, issue):
        self.ins = list(ins)
        self.out_shapes = list(out_shapes)
        self.issue = issue
        self.scratch = [pltpu.SemaphoreType.DMA((max(n_remote, 1),)),
                        pltpu.SemaphoreType.DMA((max(n_remote, 1),)),
                        pltpu.SemaphoreType.DMA((max(n_local, 1),))]

    def start(self, ins, outs, sems):
        sends, _, locs = self.issue(ins, outs, *sems)
        for cp in locs + sends:
            cp.start()

    def wait(self, ins, outs, sems):
        sends, recvs, locs = self.issue(ins, outs, *sems)
        for make in recvs:
            make().wait_recv()
        for cp in sends:
            cp.wait_send()
        for cp in locs:
            cp.wait()


def _plan_parts(plan):
    if plan is None:
        return [], [], [], []
    return ([_ANY] * len(plan.ins), plan.out_shapes, [_ANY] * len(plan.out_shapes), plan.scratch)


def _run_plan(plan, name):
    n_in, n_out = len(plan.ins), len(plan.out_shapes)

    def body(*refs):
        ins, outs, sems = refs[:n_in], refs[n_in:n_in + n_out], refs[n_in + n_out:]
        plan.start(ins, outs, sems)
        plan.wait(ins, outs, sems)

    return pl.pallas_call(
        body, name=name, in_specs=[_ANY] * n_in, out_specs=[_ANY] * n_out,
        out_shape=plan.out_shapes, scratch_shapes=plan.scratch,
    )(*plan.ins)


def _gather_plan(shards):
    n = len(shards)

    def issue(ins, outs, send_sems, recv_sems, local_sems):
        x, y, c = _place()
        me = 2 * x + y
        sends, recvs, locs = [], [], []
        for p in range(n):
            locs.append(pltpu.make_async_copy(ins[p], outs[p].at[me], local_sems.at[p]))
            for k, (px, py) in enumerate(_other_chips(x, y)):
                sems = dict(send_sem=send_sems.at[3 * p + k], recv_sem=recv_sems.at[3 * p + k],
                            device_id=(px, py, c), device_id_type=MESH)
                sends.append(pltpu.make_async_remote_copy(src_ref=ins[p], dst_ref=outs[p].at[me], **sems))
                recvs.append(functools.partial(pltpu.make_async_remote_copy, src_ref=ins[p],
                                               dst_ref=outs[p].at[2 * px + py], **sems))
        return sends, recvs, locs

    return _Plan(shards, [jax.ShapeDtypeStruct((N_CHIPS,) + s.shape, s.dtype) for s in shards], 3 * n, n, issue)


_REL7 = [(fx, fy, fc) for fx in (0, 1) for fy in (0, 1) for fc in (0, 1)][1:]


def _scatter8_plan(gs):
    n = len(gs)

    def issue(ins, outs, send_sems, recv_sems, local_sems):
        x, y, c = _place()
        sends = []
        for p in range(n):
            hr = gs[p].shape[1] // 2
            for k, (fx, fy, fc) in enumerate(_REL7):
                tx, ty, tc = x ^ fx, y ^ fy, c ^ fc
                src = ins[p].at[2 * tx + ty, pl.ds(pl.multiple_of(tc * hr, SUBLANES), hr), :]
                sends.append(pltpu.make_async_remote_copy(
                    src_ref=src, dst_ref=outs[p].at[k],
                    send_sem=send_sems.at[7 * p + k], recv_sem=recv_sems.at[7 * p + k],
                    device_id=(tx, ty, tc), device_id_type=MESH))
        return sends, [functools.partial(lambda cp: cp, cp) for cp in sends], []

    shapes = [jax.ShapeDtypeStruct((7, g.shape[1] // 2, g.shape[2]), g.dtype) for g in gs]
    return _Plan(gs, shapes, 7 * n, 0, issue)


def _sibling_plan(ts):
    n = len(ts)

    def issue(ins, outs, send_sems, recv_sems, local_sems):
        x, y, c = _place()
        sends = [pltpu.make_async_remote_copy(
            src_ref=ins[p], dst_ref=outs[p], send_sem=send_sems.at[p], recv_sem=recv_sems.at[p],
            device_id=(x, y, 1 - c), device_id_type=MESH) for p in range(n)]
        return sends, [functools.partial(lambda cp: cp, cp) for cp in sends], []

    return _Plan(ts, [jax.ShapeDtypeStruct(t.shape, t.dtype) for t in ts], n, 0, issue)


def _add8(g, recv, chip, core, name):
    s, r, n = g.shape
    hr = r // 2
    th = hr // 2 if (hr // 2) % SUBLANES == 0 else hr
    nt = hr // th

    def body(chip_ref, core_ref, g_ref, r_ref, o_ref):
        acc = g_ref[0].astype(F32)
        for k in range(7):
            acc = acc + r_ref[k].astype(F32)
        o_ref[...] = acc

    return pl.pallas_call(
        body, name=name,
        grid_spec=pltpu.PrefetchScalarGridSpec(
            num_scalar_prefetch=2, grid=(nt,),
            in_specs=[pl.BlockSpec((1, th, n), lambda i, ch, co: (ch[0], co[0] * nt + i, 0)),
                      pl.BlockSpec((7, th, n), lambda i, ch, co: (0, i, 0))],
            out_specs=pl.BlockSpec((th, n), lambda i, ch, co: (i, 0))),
        out_shape=jax.ShapeDtypeStruct((hr, n), F32),
        compiler_params=pltpu.CompilerParams(dimension_semantics=("parallel",)),
    )(chip, core, g, recv)


def _adamw_halves(w, own, other, m, v, core, name):
    r, n = w.shape
    hr = r // 2
    th = hr // 2 if (hr // 2) % SUBLANES == 0 else hr
    tph = hr // th
    c1 = 1.0 / (1.0 - ADAM_B1 ** ADAM_STEP)
    c2 = 1.0 / (1.0 - ADAM_B2 ** ADAM_STEP)

    def body(core_ref, w_ref, a_ref, b_ref, m_ref, v_ref, g_ref, d_ref, nm_ref, nv_ref):
        half = pl.program_id(0) // tph
        gv = jnp.where(half == core_ref[0], a_ref[...], b_ref[...])
        nm = ADAM_B1 * m_ref[...] + (1.0 - ADAM_B1) * gv
        nv = ADAM_B2 * v_ref[...] + (1.0 - ADAM_B2) * (gv * gv)
        g_ref[...] = gv
        nm_ref[...] = nm
        nv_ref[...] = nv
        d_ref[...] = -ADAM_LR * ((nm * c1) / (jnp.sqrt(nv * c2) + ADAM_EPS) + ADAM_WD * w_ref[...])

    full = pl.BlockSpec((th, n), lambda i, co: (i, 0))
    part = pl.BlockSpec((th, n), lambda i, co: (i % tph, 0))
    return pl.pallas_call(
        body, name=name,
        grid_spec=pltpu.PrefetchScalarGridSpec(
            num_scalar_prefetch=1, grid=(2 * tph,),
            in_specs=[full, part, part, full, full], out_specs=[full] * 4),
        out_shape=[jax.ShapeDtypeStruct((r, n), F32)] * 4,
        compiler_params=pltpu.CompilerParams(dimension_semantics=("parallel",)),
    )(core, w, own, other, m, v)


def _allreduce_small(pack, name):
    r, l = pack.shape

    def body(p_ref, o_ref, land, send_sems, recv_sems):
        x, y, c = _place()
        me = 4 * x + 2 * y + c
        land[me] = p_ref[...]
        rel = [(fx, fy, fc) for fx in (0, 1) for fy in (0, 1) for fc in (0, 1)][1:]
        sends = []
        for k, (fx, fy, fc) in enumerate(rel):
            peer = (x ^ fx, y ^ fy, c ^ fc)
            cp = pltpu.make_async_remote_copy(
                src_ref=p_ref, dst_ref=land.at[me], send_sem=send_sems.at[k], recv_sem=recv_sems.at[k],
                device_id=peer, device_id_type=MESH)
            cp.start()
            sends.append(cp)
        for k, (fx, fy, fc) in enumerate(rel):
            src = 4 * (x ^ fx) + 2 * (y ^ fy) + (c ^ fc)
            pltpu.make_async_remote_copy(
                src_ref=p_ref, dst_ref=land.at[src], send_sem=send_sems.at[k], recv_sem=recv_sems.at[k],
                device_id=(x ^ fx, y ^ fy, c ^ fc), device_id_type=MESH).wait_recv()
        for cp in sends:
            cp.wait_send()
        acc = land[0]
        for d in range(1, N_DEV):
            acc = acc + land[d]
        o_ref[...] = acc

    vm = pl.BlockSpec(memory_space=pltpu.VMEM)
    return pl.pallas_call(
        body, name=name, in_specs=[vm], out_specs=vm,
        out_shape=jax.ShapeDtypeStruct((r, l), F32),
        scratch_shapes=[pltpu.VMEM((N_DEV, r, l), F32),
                        pltpu.SemaphoreType.DMA((N_DEV - 1,)), pltpu.SemaphoreType.DMA((N_DEV - 1,))],
    )(pack)


def _flat_rows(a, mult=SUBLANES * LANES):
    f = a.reshape(-1)
    padn = (-f.shape[0]) % mult
    if padn:
        f = jnp.concatenate([f, jnp.zeros((padn,), f.dtype)])
    return f


def _pack(arrs, mult=SUBLANES * LANES, total_mult=None):
    flat = [_flat_rows(a, mult) for a in arrs]
    sizes = [f.shape[0] for f in flat]
    if total_mult is not None:
        padn = (-sum(sizes)) % total_mult
        if padn:
            flat.append(jnp.zeros((padn,), flat[0].dtype))
    return jnp.concatenate(flat).reshape(-1, LANES), sizes


def _unpack(pack, shapes, sizes, lead=()):
    flat = pack.reshape(lead + (-1,))
    out, off = [], 0
    for shp, sz in zip(shapes, sizes):
        n = math.prod(shp)
        out.append(flat[..., off:off + n].reshape(lead + tuple(shp)))
        off += sz
    return out


def _cols_from_shards(g):
    s, k, n = g.shape
    return jnp.transpose(g, (1, 0, 2)).reshape(k, s * n)


def _cols_to_shards(w, s=N_CHIPS):
    k, n = w.shape
    return jnp.transpose(w.reshape(k, s, n // s), (1, 0, 2))


def _ffn_fwd(h, pre, post, w_up, cw, cb, w_down, tag):
    u = _rmsnorm_fwd(h, pre, f"{tag}_prenorm")
    hp = _mm_nn_sh(u, w_up, 2 * D_FF, f"{tag}_up")
    act = _ffn_convact_fwd(hp, cw, cb, f"{tag}_convact")
    o = _mm_nn(act, w_down, F32, f"{tag}_down")
    hn = _postnorm_res_fwd(h, o, post, f"{tag}_postnorm")
    return hn, (h, u, hp, act, o)


def _ffn_bwd(dh, saved, pre, post, w_up, cw, cb, w_down, tag):
    h, u, hp, act, o = saved
    do, dpost = _postnorm_bwd(o, post, dh, f"{tag}_postnorm_bwd")
    dact = _mm_nt(do, w_down, f"{tag}_down_dx")
    dw_down = _mm_tn(act, do, f"{tag}_down_dw")
    dhp, dcw, dcb = _ffn_convact_bwd(hp, dact, cw, cb, f"{tag}_convact_bwd")
    du = _mm_nt_sh(dhp, w_up, f"{tag}_up_dx")
    dw_up = _mm_tn_sh(u, dhp, w_up.shape[2], f"{tag}_up_dw")
    dhn, dpre = _prenorm_bwd(h, pre, du, dh, f"{tag}_prenorm_bwd")
    return dhn, dict(pre=dpre, post=dpost, w_up=dw_up, conv_w=dcw[:3], conv_b=dcb, w_down=dw_down)


class _Exchange:
    GATHER_IN_ATTN = ("l0_w_out", "l0_ffn_w_up", "l0_ffn_w_down", "l1_w_in")
    GATHER_IN_SSD = ("l1_w_out", "l1_ffn_w_up", "l1_ffn_w_down")
    AFTER_L1_OUT = ("l1_ffn_w_up", "l1_ffn_w_down", "l1_w_out")
    AFTER_L0_OUT = ("l1_w_in", "l0_ffn_w_up", "l0_ffn_w_down", "l0_w_out")
    LAST = ("l0_w_in",)

    def __init__(self, late_shards):
        self.late = dict(late_shards)
        self.slabs = {}
        self.recv = {}

    def gather_plan(self, names):
        return _gather_plan([self.late[n] for n in names])

    def gathered(self, names, outs):
        return {n: (g if n in _BIG_COL else g.reshape(-1, g.shape[-1])) for n, g in zip(names, outs)}

    def scatter_plan(self, grads, names):
        for n in names:
            g = grads[n]
            self.slabs[n] = g if n in _BIG_COL else g.reshape(N_CHIPS, -1, g.shape[-1])
        return _scatter8_plan([self.slabs[n] for n in names])

    def scattered(self, names, outs):
        self.recv.update(zip(names, outs))


def _local_step(x, tgt, meta, P, ex=None):
    seq, d = x.shape
    lp = seq + BLOCK
    h0 = jnp.concatenate([jnp.zeros((PAD, d), F32), meta, x], axis=0)
    tgt_p = jnp.concatenate([jnp.zeros((BLOCK, d), F32), tgt], axis=0)

    u0 = _rmsnorm_fwd(h0, P["l0_mix_pre_norm"], "l0_mix_prenorm")
    proj0 = _mm_nn_sh(u0, P["l0_w_in"], EVEN_IN, "l0_in")
    xrc = _conv_fwd(proj0, D_MODEL, D_MODEL, P["l0_lru_conv_w"], P["l0_lru_conv_b"], "l0_lru_conv")
    lru_args = (P["l0_lru_w_a"], P["l0_lru_w_x"], P["l0_lru_b_a"], P["l0_lru_b_x"], P["l0_lru_lambda"])
    ya, hl = _lru_fwd(proj0, xrc, *lru_args, "l0_lru")
    yb, outs = _attn_fwd(proj0, P["l0_attn_sinks"], "l0_attn",
                         ex.gather_plan(ex.GATHER_IN_ATTN) if ex else None)
    if ex:
        P = {**P, **ex.gathered(ex.GATHER_IN_ATTN, outs)}
    mix0 = jnp.concatenate([ya, yb], axis=1)
    o0 = _mm_nn(mix0, P["l0_w_out"], F32, "l0_out")
    h1 = _postnorm_res_fwd(h0, o0, P["l0_mix_post_norm"], "l0_mix_postnorm")
    h2, ffn0 = _ffn_fwd(h1, P["l0_ffn_pre_norm"], P["l0_ffn_post_norm"], P["l0_ffn_w_up"],
                        P["l0_ffn_conv_w"], P["l0_ffn_conv_b"], P["l0_ffn_w_down"], "l0_ffn")
    u2 = _rmsnorm_fwd(h2, P["l1_mix_pre_norm"], "l1_mix_prenorm")
    proj1 = _mm_nn_sh(u2, P["l1_w_in"], ODD_IN_PAD, "l1_in")
    xc1 = _conv_fwd(proj1, _ZW, _XBC_W, P["l1_ssm_conv_w"], P["l1_ssm_conv_b"], "l1_ssm_conv")
    xbc, dt = _ssm_prep_fwd(xc1, proj1, P["l1_dt_bias"], "l1_ssm_prep")
    (yssd, states), outs = _ssd_fwd(xbc, dt, P["l1_a_log"], "l1_ssd",
                                    ex.gather_plan(ex.GATHER_IN_SSD) if ex else None)
    if ex:
        P = {**P, **ex.gathered(ex.GATHER_IN_SSD, outs)}
    yn = _ssm_gate_fwd(yssd, xbc, proj1, P["l1_d_skip"], P["l1_gate_norm"], "l1_ssm_gate")
    o1 = _mm_nn(yn, P["l1_w_out"], F32, "l1_out")
    h3 = _postnorm_res_fwd(h2, o1, P["l1_mix_post_norm"], "l1_mix_postnorm")
    h4, ffn1 = _ffn_fwd(h3, P["l1_ffn_pre_norm"], P["l1_ffn_post_norm"], P["l1_ffn_w_up"],
                        P["l1_ffn_conv_w"], P["l1_ffn_conv_b"], P["l1_ffn_w_down"], "l1_ffn")
    dh4, loss_cols = _loss_fwd_bwd(h4, tgt_p, "loss")

    G = {}
    dh3, g = _ffn_bwd(dh4, ffn1, P["l1_ffn_pre_norm"], P["l1_ffn_post_norm"], P["l1_ffn_w_up"],
                      P["l1_ffn_conv_w"], P["l1_ffn_conv_b"], P["l1_ffn_w_down"], "l1_ffn")
    for k, v in g.items():
        G["l1_ffn_" + (k + "_norm" if k in ("pre", "post") else k)] = v
    do1, G["l1_mix_post_norm"] = _postnorm_bwd(o1, P["l1_mix_post_norm"], dh3, "l1_mix_postnorm_bwd")
    dyn = _mm_nt(do1, P["l1_w_out"], "l1_out_dx")
    G["l1_w_out"] = _mm_tn(yn, do1, "l1_out_dw")
    dyssd, dxskip, dz, dd_cols, G["l1_gate_norm"] = _ssm_gate_bwd(
        yssd, xbc, proj1, P["l1_d_skip"], P["l1_gate_norm"], dyn, "l1_ssm_gate_bwd")
    G["l1_d_skip"] = dd_cols.reshape(SSD_HEADS, SSD_P).sum(axis=1)
    (dxs, dbm, dcm, ddt, dalog), outs = _ssd_bwd(
        xbc, dt, P["l1_a_log"], states, dyssd, "l1_ssd_bwd",
        ex.scatter_plan(G, ex.AFTER_L1_OUT) if ex else None)
    if ex:
        ex.scattered(ex.AFTER_L1_OUT, outs)
    G["l1_a_log"] = dalog[0, :SSD_HEADS]
    dxc, ddtr, dbias = _ssm_prep_bwd(xc1, proj1, P["l1_dt_bias"], dxs, dxskip, dbm, dcm, ddt,
                                     "l1_ssm_prep_bwd")
    G["l1_dt_bias"] = dbias[0, :SSD_HEADS]
    dxbc, dcw, dcb = _conv_bwd(proj1, _ZW, _XBC_W, dxc, P["l1_ssm_conv_w"], "l1_ssm_conv_bwd")
    G["l1_ssm_conv_w"] = dcw[:4]
    G["l1_ssm_conv_b"] = dcb
    dproj1 = jnp.concatenate([dz, dxbc, ddtr], axis=1)
    du2 = _mm_nt_sh(dproj1, P["l1_w_in"], "l1_in_dx")
    G["l1_w_in"] = _mm_tn_sh(u2, dproj1, ODD_IN // N_CHIPS, "l1_in_dw")
    dh2, G["l1_mix_pre_norm"] = _prenorm_bwd(h2, P["l1_mix_pre_norm"], du2, dh3, "l1_mix_prenorm_bwd")
    dh1, g = _ffn_bwd(dh2, ffn0, P["l0_ffn_pre_norm"], P["l0_ffn_post_norm"], P["l0_ffn_w_up"],
                      P["l0_ffn_conv_w"], P["l0_ffn_conv_b"], P["l0_ffn_w_down"], "l0_ffn")
    for k, v in g.items():
        G["l0_ffn_" + (k + "_norm" if k in ("pre", "post") else k)] = v
    do0, G["l0_mix_post_norm"] = _postnorm_bwd(o0, P["l0_mix_post_norm"], dh1, "l0_mix_postnorm_bwd")
    dmix = _mm_nt(do0, P["l0_w_out"], "l0_out_dx")
    G["l0_w_out"] = _mm_tn(mix0, do0, "l0_out_dw")
    (dgate, dxrc, G["l0_lru_w_a"], G["l0_lru_w_x"], G["l0_lru_b_a"], G["l0_lru_b_x"],
     G["l0_lru_lambda"]) = _lru_bwd(proj0, xrc, hl, dmix, *lru_args, "l0_lru_bwd")
    dxr, dcw, dcb = _conv_bwd(proj0, D_MODEL, D_MODEL, dxrc, P["l0_lru_conv_w"], "l0_lru_conv_bwd")
    G["l0_lru_conv_w"] = dcw[:4]
    G["l0_lru_conv_b"] = dcb
    (dq, dk, dv, G["l0_attn_sinks"]), outs = _attn_bwd(
        proj0, P["l0_attn_sinks"], dmix, "l0_attn_bwd",
        ex.scatter_plan(G, ex.AFTER_L0_OUT) if ex else None)
    if ex:
        ex.scattered(ex.AFTER_L0_OUT, outs)
    dproj0 = jnp.concatenate([dgate, dxr, dq, dk.astype(BF16), dv.astype(BF16)], axis=1)
    du0 = _mm_nt_sh(dproj0, P["l0_w_in"], "l0_in_dx")
    G["l0_w_in"] = _mm_tn_sh(u0, dproj0, EVEN_IN // N_CHIPS, "l0_in_dw")
    dh0, G["l0_mix_pre_norm"] = _prenorm_bwd(h0, P["l0_mix_pre_norm"], du0, dh1, "l0_mix_prenorm_bwd")
    return loss_cols, dh0[BLOCK:], dh0[PAD:BLOCK], G


_BIG_COL = ("l0_w_in", "l0_ffn_w_up", "l1_w_in", "l1_ffn_w_up")
_BIG_ROW = ("l0_w_out", "l0_ffn_w_down", "l1_w_out", "l1_ffn_w_down")
_BIG = ("l0_w_in", "l0_w_out", "l0_ffn_w_up", "l0_ffn_w_down",
        "l1_w_in", "l1_w_out", "l1_ffn_w_up", "l1_ffn_w_down")
_SMALL_SHARDED = ("meta_tokens", "l0_lru_conv_w", "l0_ffn_conv_w", "l1_ssm_conv_w", "l1_ffn_conv_w")
_WEIGHTS = ("meta_tokens", "l0_mix_pre_norm", "l0_mix_post_norm", "l0_w_in", "l0_lru_conv_w",
            "l0_lru_conv_b", "l0_lru_w_a", "l0_lru_b_a", "l0_lru_w_x", "l0_lru_b_x", "l0_lru_lambda",
            "l0_attn_sinks", "l0_w_out", "l0_ffn_pre_norm", "l0_ffn_post_norm", "l0_ffn_w_up",
            "l0_ffn_conv_w", "l0_ffn_conv_b", "l0_ffn_w_down", "l1_mix_pre_norm", "l1_mix_post_norm",
            "l1_w_in", "l1_ssm_conv_w", "l1_ssm_conv_b", "l1_dt_bias", "l1_a_log", "l1_d_skip",
            "l1_gate_norm", "l1_w_out", "l1_ffn_pre_norm", "l1_ffn_post_norm", "l1_ffn_w_up",
            "l1_ffn_conv_w", "l1_ffn_conv_b", "l1_ffn_w_down")
_REPL = tuple(n for n in _WEIGHTS if n not in _BIG and n not in _SMALL_SHARDED)


def _pad_lanes(v, n=LANES):
    return jnp.concatenate([v, jnp.zeros((n - v.shape[0],), v.dtype)]).reshape(1, n)


def _step(x, tgt, W, M, V):
    cx, cy, cc = _place()
    chip = 2 * cx + cy

    small_pack, small_sizes = _pack([W[n] for n in _SMALL_SHARDED])
    first = _run_plan(_gather_plan([W["l0_w_in"].astype(BF16), small_pack]), "gather_first")
    small_full = _unpack(first[1], [W[n].shape for n in _SMALL_SHARDED], small_sizes, lead=(N_CHIPS,))
    ex = _Exchange({n: W[n].astype(BF16) for n in _BIG if n != "l0_w_in"})

    P = {"l0_w_in": first[0]}
    for n, g in zip(_SMALL_SHARDED, small_full):
        P[n] = _cols_from_shards(g)
    for n in _REPL:
        v = W[n]
        P[n] = v.reshape(1, -1) if v.ndim == 1 else v
    P["l0_lru_w_a"] = W["l0_lru_w_a"].astype(BF16)
    P["l0_lru_w_x"] = W["l0_lru_w_x"].astype(BF16)
    P["l1_dt_bias"] = _pad_lanes(W["l1_dt_bias"])
    P["l1_a_log"] = _pad_lanes(W["l1_a_log"])
    P["l1_d_skip"] = jnp.repeat(W["l1_d_skip"], SSD_P).reshape(1, D_SSM)
    meta = P.pop("meta_tokens")

    loss_cols, grad_x, grad_meta, G = _local_step(x, tgt, meta, P, ex)
    G["meta_tokens"] = grad_meta

    ex.scattered(ex.LAST, _run_plan(ex.scatter_plan(G, ex.LAST), "grad_scatter_last"))
    core_idx = cc.astype(jnp.int32).reshape(1)
    chip_idx = chip.astype(jnp.int32).reshape(1)
    own_half = [_add8(ex.slabs[n], ex.recv[n], chip_idx, core_idx, f"grad_sum_{n}") for n in _BIG]
    other_half = _run_plan(_sibling_plan(own_half), "grad_sibling_swap")
    small_names = list(_REPL) + list(_SMALL_SHARDED)
    small_list = [G[n] for n in small_names] + [loss_cols]
    spack, ssizes = _pack(small_list)
    sred = _allreduce_small(spack, "small_allreduce")
    sfull = _unpack(sred, [a.shape for a in small_list], ssizes)
    loss = 0.5 / D_MODEL * jnp.sum(sfull[-1])
    small_grads = {}
    for n, g in zip(small_names, sfull[:-1]):
        if n in _SMALL_SHARDED:
            wcols = W[n].shape[1]
            g = lax.dynamic_slice_in_dim(g, chip * wcols, wcols, axis=1)
        small_grads[n] = g.reshape(W[n].shape)

    grads, delta, new_m, new_v = {}, {}, {}, {}
    for n, own, other in zip(_BIG, own_half, other_half):
        grads[n], delta[n], new_m[n], new_v[n] = _adamw_halves(
            W[n], own, other, M[n], V[n], core_idx, f"adamw_{n}")
    s_names = [n for n in _WEIGHTS if n not in _BIG]
    tile_elems = 512 * LANES
    wp, wsz = _pack([W[n] for n in s_names], total_mult=tile_elems)
    gp, _ = _pack([small_grads[n] for n in s_names], total_mult=tile_elems)
    mp, _ = _pack([M[n] for n in s_names], total_mult=tile_elems)
    vp, _ = _pack([V[n] for n in s_names], total_mult=tile_elems)
    dp, nmp, nvp = _adamw(wp, gp, mp, vp, "adamw_small")
    shapes = [W[n].shape for n in s_names]
    for n, a, b, c_ in zip(s_names, _unpack(dp, shapes, wsz), _unpack(nmp, shapes, wsz),
                           _unpack(nvp, shapes, wsz)):
        grads[n] = small_grads[n]
        delta[n], new_m[n], new_v[n] = a, b, c_
    return loss, grad_x, grads, delta, new_m, new_v


def kernel(x, meta_tokens, l0_mix_pre_norm, l0_mix_post_norm, l0_w_in, l0_lru_conv_w, l0_lru_conv_b, l0_lru_w_a, l0_lru_b_a, l0_lru_w_x, l0_lru_b_x, l0_lru_lambda, l0_attn_sinks, l0_w_out, l0_ffn_pre_norm, l0_ffn_post_norm, l0_ffn_w_up, l0_ffn_conv_w, l0_ffn_conv_b, l0_ffn_w_down, l1_mix_pre_norm, l1_mix_post_norm, l1_w_in, l1_ssm_conv_w, l1_ssm_conv_b, l1_dt_bias, l1_a_log, l1_d_skip, l1_gate_norm, l1_w_out, l1_ffn_pre_norm, l1_ffn_post_norm, l1_ffn_w_up, l1_ffn_conv_w, l1_ffn_conv_b, l1_ffn_w_down, loss_target, m_meta_tokens, m_l0_mix_pre_norm, m_l0_mix_post_norm, m_l0_w_in, m_l0_lru_conv_w, m_l0_lru_conv_b, m_l0_lru_w_a, m_l0_lru_b_a, m_l0_lru_w_x, m_l0_lru_b_x, m_l0_lru_lambda, m_l0_attn_sinks, m_l0_w_out, m_l0_ffn_pre_norm, m_l0_ffn_post_norm, m_l0_ffn_w_up, m_l0_ffn_conv_w, m_l0_ffn_conv_b, m_l0_ffn_w_down, m_l1_mix_pre_norm, m_l1_mix_post_norm, m_l1_w_in, m_l1_ssm_conv_w, m_l1_ssm_conv_b, m_l1_dt_bias, m_l1_a_log, m_l1_d_skip, m_l1_gate_norm, m_l1_w_out, m_l1_ffn_pre_norm, m_l1_ffn_post_norm, m_l1_ffn_w_up, m_l1_ffn_conv_w, m_l1_ffn_conv_b, m_l1_ffn_w_down, v_meta_tokens, v_l0_mix_pre_norm, v_l0_mix_post_norm, v_l0_w_in, v_l0_lru_conv_w, v_l0_lru_conv_b, v_l0_lru_w_a, v_l0_lru_b_a, v_l0_lru_w_x, v_l0_lru_b_x, v_l0_lru_lambda, v_l0_attn_sinks, v_l0_w_out, v_l0_ffn_pre_norm, v_l0_ffn_post_norm, v_l0_ffn_w_up, v_l0_ffn_conv_w, v_l0_ffn_conv_b, v_l0_ffn_w_down, v_l1_mix_pre_norm, v_l1_mix_post_norm, v_l1_w_in, v_l1_ssm_conv_w, v_l1_ssm_conv_b, v_l1_dt_bias, v_l1_a_log, v_l1_d_skip, v_l1_gate_norm, v_l1_w_out, v_l1_ffn_pre_norm, v_l1_ffn_post_norm, v_l1_ffn_w_up, v_l1_ffn_conv_w, v_l1_ffn_conv_b, v_l1_ffn_w_down):
    args = locals()
    W = {n: args[n] for n in _WEIGHTS}
    M = {n: args["m_" + n] for n in _WEIGHTS}
    V = {n: args["v_" + n] for n in _WEIGHTS}
    loss, grad_x, grads, delta, new_m, new_v = _step(x[0], loss_target[0], W, M, V)
    return (loss, grad_x[None], *[grads[n] for n in _WEIGHTS], *[delta[n] for n in _WEIGHTS],
            *[new_m[n] for n in _WEIGHTS], *[new_v[n] for n in _WEIGHTS])
```

```python
import functools
import math

import jax
import jax.numpy as jnp
from jax import lax
from jax.experimental import pallas as pl
from jax.experimental.pallas import tpu as pltpu

F32 = jnp.float32
BF16 = jnp.bfloat16

D_MODEL = 1024
N_META = 16
BLOCK = 128
PAD = BLOCK - N_META
EPS = 1e-6
LRU_BLOCKS = 8
LRU_BS = 128
LRU_C = 8.0
N_Q_HEADS = 16
N_KV_HEADS = 2
HEAD_DIM = 64
Q_PER_KV = 8
WINDOW = 128
D_SSM = 2048
SSD_HEADS = 32
SSD_GROUPS = 8
SSD_HPG = 4
SSD_P = 64
SSD_N = 128
D_FF = 2816
NEG = -1e30
LANES = 128
SUBLANES = 8

ADAM_LR = 0.001
ADAM_B1 = 0.9
ADAM_B2 = 0.999
ADAM_EPS = 1e-08
ADAM_WD = 0.01
ADAM_STEP = 10

MESH = pl.DeviceIdType.MESH
N_CHIPS = 4
N_DEV = 8


def _pick(n, cands):
    for c in cands:
        if n % c == 0:
            return c
    raise ValueError(f"no tile for {n} in {cands}")


def _col_tile(n, limit=1792):
    best = None
    for t in range(LANES, min(n, limit) + 1, LANES):
        if n % t == 0:
            best = t
    if best is None:
        raise ValueError(f"no lane tile for {n}")
    return best


def _sigmoid(x):
    return 1.0 / (1.0 + jnp.exp(-x))


def _log1p(e):
    u = 1.0 + e
    return jnp.where(u == 1.0, e, jnp.log(u) * (e / jnp.where(u == 1.0, 1.0, u - 1.0)))


def _softplus(x):
    return jnp.maximum(x, 0.0) + _log1p(jnp.exp(-jnp.abs(x)))


def _neg_expm1(x):
    poly = x * (1.0 + x * (0.5 + x * (1.0 / 6.0 + x * (1.0 / 24.0 + x * (1.0 / 120.0)))))
    return -jnp.where(x > -0.05, poly, jnp.exp(x) - 1.0)


_GELU_C = math.sqrt(2.0 / math.pi)


def _gelu(x):
    t = jnp.tanh(_GELU_C * (x + 0.044715 * x * x * x))
    return 0.5 * x * (1.0 + t)


def _gelu_and_grad(x):
    x2 = x * x
    t = jnp.tanh(_GELU_C * (x + 0.044715 * x * x2))
    g = 0.5 * x * (1.0 + t)
    dg = 0.5 * (1.0 + t) + 0.5 * x * (1.0 - t * t) * _GELU_C * (1.0 + 3.0 * 0.044715 * x2)
    return g, dg


def _silu_and_grad(x):
    s = _sigmoid(x)
    return x * s, s * (1.0 + x * (1.0 - s))


def _dot(a, b):
    return jnp.dot(a, b, preferred_element_type=F32)


def _dot_nt(a, b):
    return lax.dot_general(a, b, (((1,), (1,)), ((), ())), preferred_element_type=F32)


def _dot_tn(a, b):
    return lax.dot_general(a, b, (((0,), (0,)), ((), ())), preferred_element_type=F32)


def _row_iota(t):
    return lax.broadcasted_iota(jnp.int32, (t, 1), 0)


def _scan_fwd(a, u, t):
    row = _row_iota(t)
    d = 1
    while d < t:
        m = row >= d
        u_sh = jnp.where(m, pltpu.roll(u, d, 0), 0.0)
        a_sh = jnp.where(m, pltpu.roll(a, d, 0), 1.0)
        u = u + a * u_sh
        a = a * a_sh
        d *= 2
    return a, u


def _scan_rev(c, x, t):
    row = _row_iota(t)
    d = 1
    while d < t:
        m = row < t - d
        x_sh = jnp.where(m, pltpu.roll(x, t - d, 0), 0.0)
        c_sh = jnp.where(m, pltpu.roll(c, t - d, 0), 1.0)
        x = x + c * x_sh
        c = c * c_sh
        d *= 2
    return c, x


def _cumsum_rows(x, t):
    row = _row_iota(t)
    d = 1
    while d < t:
        x = x + jnp.where(row >= d, pltpu.roll(x, d, 0), 0.0)
        d *= 2
    return x


def _rev_cumsum_rows(x, t):
    row = _row_iota(t)
    d = 1
    while d < t:
        x = x + jnp.where(row < t - d, pltpu.roll(x, t - d, 0), 0.0)
        d *= 2
    return x


def _rms_bwd(x, g, dy):
    rs = lax.rsqrt(jnp.mean(x * x, axis=-1, keepdims=True) + EPS)
    gy = dy * g
    dx = rs * gy - x * (rs * rs * rs) * jnp.mean(x * gy, axis=-1, keepdims=True)
    return dx, dy * x * rs


def _mm_nn(a, w, out_dtype, name):
    m, k = a.shape
    n = w.shape[1]
    tm = _pick(m, (640, 512, 256, 128))
    tn = _col_tile(n)

    def body(a_ref, w_ref, o_ref):
        o_ref[...] = _dot(a_ref[...].astype(BF16), w_ref[...]).astype(o_ref.dtype)

    return pl.pallas_call(
        body, name=name, grid=(n // tn, m // tm),
        in_specs=[pl.BlockSpec((tm, k), lambda j, i: (i, 0)),
                  pl.BlockSpec((k, tn), lambda j, i: (0, j))],
        out_specs=pl.BlockSpec((tm, tn), lambda j, i: (i, j)),
        out_shape=jax.ShapeDtypeStruct((m, n), out_dtype),
        compiler_params=pltpu.CompilerParams(dimension_semantics=("parallel", "parallel")),
    )(a, w)


def _mm_nt(dy, w, name):
    m, n = dy.shape
    k = w.shape[0]
    wide = n > 3328
    tm = _pick(m, (320, 256, 128)) if wide else _pick(m, (640, 512, 256, 128))
    tk = _col_tile(k, 512 if wide else 1408)

    def body(dy_ref, w_ref, o_ref):
        o_ref[...] = _dot_nt(dy_ref[...].astype(BF16), w_ref[...])

    return pl.pallas_call(
        body, name=name, grid=(k // tk, m // tm),
        in_specs=[pl.BlockSpec((tm, n), lambda j, i: (i, 0)),
                  pl.BlockSpec((tk, n), lambda j, i: (j, 0))],
        out_specs=pl.BlockSpec((tm, tk), lambda j, i: (i, j)),
        out_shape=jax.ShapeDtypeStruct((m, k), F32),
        compiler_params=pltpu.CompilerParams(dimension_semantics=("parallel", "parallel")),
    )(dy, w)


def _mm_tn(a, dy, name):
    m, k = a.shape
    n = dy.shape[1]
    tm = _pick(m, (640, 512, 256, 128))
    tk = _col_tile(k, 1408)
    tn = _col_tile(n, 1664)
    nsteps = m // tm

    def body(a_ref, dy_ref, o_ref, acc):
        @pl.when(pl.program_id(2) == 0)
        def _():
            acc[...] = jnp.zeros_like(acc)

        acc[...] += _dot_tn(a_ref[...].astype(BF16), dy_ref[...].astype(BF16))

        @pl.when(pl.program_id(2) == nsteps - 1)
        def _():
            o_ref[...] = acc[...].astype(o_ref.dtype)

    return pl.pallas_call(
        body, name=name, grid=(k // tk, n // tn, nsteps),
        in_specs=[pl.BlockSpec((tm, tk), lambda kk, j, i: (i, kk)),
                  pl.BlockSpec((tm, tn), lambda kk, j, i: (i, j))],
        out_specs=pl.BlockSpec((tk, tn), lambda kk, j, i: (kk, j)),
        out_shape=jax.ShapeDtypeStruct((k, n), BF16),
        scratch_shapes=[pltpu.VMEM((tk, tn), F32)],
        compiler_params=pltpu.CompilerParams(
            dimension_semantics=("parallel", "parallel", "arbitrary")),
    )(a, dy)


def _mm_nn_sh(a, w4, n_out, name):
    m, k = a.shape
    s, _, n = w4.shape
    tm = _pick(m, (320, 256, 128))

    def body(a_ref, w_ref, o_ref):
        av = a_ref[...].astype(BF16)
        for j in range(s):
            o_ref[:, j * n:(j + 1) * n] = _dot(av, w_ref[j])
        if n_out > s * n:
            o_ref[:, s * n:] = jnp.zeros((tm, n_out - s * n), F32)

    return pl.pallas_call(
        body, name=name, grid=(m // tm,),
        in_specs=[pl.BlockSpec((tm, k), lambda i: (i, 0)),
                  pl.BlockSpec((s, k, n), lambda i: (0, 0, 0))],
        out_specs=pl.BlockSpec((tm, n_out), lambda i: (i, 0)),
        out_shape=jax.ShapeDtypeStruct((m, n_out), F32),
        compiler_params=pltpu.CompilerParams(dimension_semantics=("parallel",)),
    )(a, w4)


def _mm_nt_sh(dy, w4, name):
    dys = dy if isinstance(dy, (tuple, list)) else (dy,)
    m = dys[0].shape[0]
    s, k, n = w4.shape
    tm = _pick(m, (640, 512, 256, 128))
    tk = _col_tile(k, 256)
    where = _shard_columns(dys, s, n)

    def body(*refs):
        w_ref, o_ref = refs[len(dys)], refs[len(dys) + 1]
        acc = None
        for j, (p, c0) in enumerate(where):
            t = _dot_nt(refs[p][:, c0:c0 + n].astype(BF16), w_ref[j])
            acc = t if acc is None else acc + t
        o_ref[...] = acc

    return pl.pallas_call(
        body, name=name, grid=(k // tk, m // tm),
        in_specs=[pl.BlockSpec((tm, d.shape[1]), lambda j, i: (i, 0)) for d in dys]
        + [pl.BlockSpec((s, tk, n), lambda j, i: (0, j, 0))],
        out_specs=pl.BlockSpec((tm, tk), lambda j, i: (i, j)),
        out_shape=jax.ShapeDtypeStruct((m, k), F32),
        compiler_params=pltpu.CompilerParams(dimension_semantics=("parallel", "parallel")),
    )(*dys, w4)


def _shard_columns(dys, s, n):
    where = []
    for p, d in enumerate(dys):
        where += [(p, c * n) for c in range(d.shape[1] // n)]
    assert len(where) >= s
    return where[:s]


def _mm_tn_sh(a, dy, n, name):
    dys = dy if isinstance(dy, (tuple, list)) else (dy,)
    m, k = a.shape
    s = N_CHIPS
    tm = _pick(m, (640, 512, 256, 128))
    tk = _col_tile(k, 512 if n <= 1024 else 256)
    nsteps = m // tm
    where = _shard_columns(dys, s, n)

    def body(*refs):
        a_ref, o_ref, acc = refs[0], refs[len(dys) + 1], refs[len(dys) + 2]

        @pl.when(pl.program_id(1) == 0)
        def _():
            acc[...] = jnp.zeros_like(acc)

        av = a_ref[...].astype(BF16)
        for j, (p, c0) in enumerate(where):
            acc[j] += _dot_tn(av, refs[1 + p][:, c0:c0 + n].astype(BF16))

        @pl.when(pl.program_id(1) == nsteps - 1)
        def _():
            o_ref[...] = acc[...].astype(o_ref.dtype)

    return pl.pallas_call(
        body, name=name, grid=(k // tk, nsteps),
        in_specs=[pl.BlockSpec((tm, tk), lambda kk, i: (i, kk))]
        + [pl.BlockSpec((tm, d.shape[1]), lambda kk, i: (i, 0)) for d in dys],
        out_specs=pl.BlockSpec((s, tk, n), lambda kk, i: (0, kk, 0)),
        out_shape=jax.ShapeDtypeStruct((s, k, n), BF16),
        scratch_shapes=[pltpu.VMEM((s, tk, n), F32)],
        compiler_params=pltpu.CompilerParams(dimension_semantics=("parallel", "arbitrary")),
    )(a, *dys)


def _rowcall(name, body, lp, tm, rows=(), prevs=(), vecs=(), outs=(), accs=(), scratch=(),
             reverse=False, seq=False):
    nt = lp // tm
    hb = tm // SUBLANES

    def ri(i):
        return nt - 1 - i if reverse else i

    in_specs, args = [], []
    for arr, w, cb in rows:
        in_specs.append(pl.BlockSpec((tm, w), lambda i, cb=cb: (ri(i), cb)))
        args.append(arr)
    for arr, w, cb in prevs:
        in_specs.append(pl.BlockSpec((SUBLANES, w), lambda i, cb=cb: (jnp.maximum(ri(i) * hb - 1, 0), cb)))
        args.append(arr)
    for arr in vecs:
        in_specs.append(pl.BlockSpec(arr.shape, lambda i, nd=arr.ndim: (0,) * nd))
        args.append(arr)
    out_shape, out_specs = [], []
    for w, dt in outs:
        out_shape.append(jax.ShapeDtypeStruct((lp, w), dt))
        out_specs.append(pl.BlockSpec((tm, w), lambda i: (ri(i), 0)))
    for shp, dt in accs:
        out_shape.append(jax.ShapeDtypeStruct(shp, dt))
        out_specs.append(pl.BlockSpec(shp, lambda i, nd=len(shp): (0,) * nd))

    def kern(*refs):
        i = pl.program_id(0)
        body(ri(i), i == 0, *refs)

    sem = ("arbitrary",) if (seq or accs) else ("parallel",)
    res = pl.pallas_call(
        kern, name=name, grid=(nt,), in_specs=in_specs, out_specs=out_specs,
        out_shape=out_shape, scratch_shapes=list(scratch),
        compiler_params=pltpu.CompilerParams(dimension_semantics=sem),
    )(*args)
    return res


def _acc_add(first, ref, val):
    @pl.when(first)
    def _():
        ref[...] = jnp.zeros_like(ref)

    ref[...] += val


def _real_rows(r, tm):
    return (r * tm + _row_iota(tm)) >= PAD


def _rmsnorm_fwd(h, g, name):
    lp, d = h.shape
    tm = _pick(lp, (640, 512, 256, 128))

    def body(r, first, h_ref, g_ref, u_ref):
        x = h_ref[...]
        rs = lax.rsqrt(jnp.mean(x * x, axis=-1, keepdims=True) + EPS)
        u_ref[...] = (x * rs * g_ref[...]).astype(u_ref.dtype)

    return _rowcall(name, body, lp, tm, rows=[(h, d, 0)], vecs=[g], outs=[(d, BF16)])[0]


def _postnorm_res_fwd(h, o, g, name):
    lp, d = h.shape
    tm = _pick(lp, (640, 512, 256, 128))

    def body(r, first, h_ref, o_ref, g_ref, out_ref):
        x = o_ref[...]
        rs = lax.rsqrt(jnp.mean(x * x, axis=-1, keepdims=True) + EPS)
        out_ref[...] = jnp.where(_real_rows(r, tm), h_ref[...] + x * rs * g_ref[...], 0.0)

    return _rowcall(name, body, lp, tm, rows=[(h, d, 0), (o, d, 0)], vecs=[g], outs=[(d, F32)])[0]


def _postnorm_bwd(o, g, dh, name):
    lp, d = o.shape
    tm = _pick(lp, (640, 512, 256, 128))

    def body(r, first, o_ref, dh_ref, g_ref, do_ref, dg_ref):
        dx, dgt = _rms_bwd(o_ref[...], g_ref[...], dh_ref[...])
        do_ref[...] = dx.astype(do_ref.dtype)
        _acc_add(first, dg_ref, jnp.sum(dgt, axis=0, keepdims=True))

    return _rowcall(name, body, lp, tm, rows=[(o, d, 0), (dh, d, 0)], vecs=[g],
                    outs=[(d, BF16)], accs=[((1, d), F32)])


def _prenorm_bwd(h, g, du, dh_res, name):
    lp, d = h.shape
    tm = _pick(lp, (640, 512, 256, 128))

    def body(r, first, h_ref, du_ref, dres_ref, g_ref, dh_ref, dg_ref):
        dx, dgt = _rms_bwd(h_ref[...], g_ref[...], du_ref[...])
        dh_ref[...] = jnp.where(_real_rows(r, tm), dres_ref[...] + dx, 0.0)
        _acc_add(first, dg_ref, jnp.sum(dgt, axis=0, keepdims=True))

    return _rowcall(name, body, lp, tm, rows=[(h, d, 0), (du, d, 0), (dh_res, d, 0)], vecs=[g],
                    outs=[(d, F32)], accs=[((1, d), F32)])


def _loss_fwd_bwd(h, tgt, name):
    lp, d = h.shape
    tm = _pick(lp, (640, 512, 256, 128))

    def body(r, first, h_ref, t_ref, dh_ref, ls_ref):
        tok = (r * tm + _row_iota(tm)) >= BLOCK
        e = jnp.where(tok, h_ref[...] - t_ref[...], 0.0)
        dh_ref[...] = e * (1.0 / d)
        _acc_add(first, ls_ref, jnp.sum(e * e, axis=0, keepdims=True))

    return _rowcall(name, body, lp, tm, rows=[(h, d, 0), (tgt, d, 0)],
                    outs=[(d, F32)], accs=[((1, d), F32)])


def _conv_tiles(lp, width):
    wc = _col_tile(width, 1408)
    tm = _pick(lp, (320, 256, 128))
    return tm, wc


def _conv_fwd(x, col_off, width, w, b, name):
    lp = x.shape[0]
    kk = w.shape[0]
    tm, wc = _conv_tiles(lp, width)
    offb = col_off // wc
    assert col_off % wc == 0
    hb = tm // SUBLANES

    def body(x_ref, xp_ref, w_ref, b_ref, y_ref):
        i = pl.program_id(1)
        xv = x_ref[...]
        halo = jnp.where(i > 0, xp_ref[...], 0.0)
        xx = jnp.concatenate([halo, xv], axis=0)
        acc = b_ref[...] + w_ref[kk - 1:kk, :] * xv
        for j in range(1, kk):
            acc = acc + w_ref[kk - 1 - j:kk - j, :] * pltpu.roll(xx, j, 0)[SUBLANES:, :]
        y_ref[...] = acc

    return pl.pallas_call(
        body, name=name, grid=(width // wc, lp // tm),
        in_specs=[pl.BlockSpec((tm, wc), lambda j, i: (i, offb + j)),
                  pl.BlockSpec((SUBLANES, wc), lambda j, i: (jnp.maximum(i * hb - 1, 0), offb + j)),
                  pl.BlockSpec((kk, wc), lambda j, i: (0, j)),
                  pl.BlockSpec((1, wc), lambda j, i: (0, j))],
        out_specs=pl.BlockSpec((tm, wc), lambda j, i: (i, j)),
        out_shape=jax.ShapeDtypeStruct((lp, width), F32),
        compiler_params=pltpu.CompilerParams(dimension_semantics=("parallel", "parallel")),
    )(x, x, w, b)


def _conv_bwd(x, col_off, width, dy, w, name, w_col_off=0):
    lp = x.shape[0]
    kk = w.shape[0]
    tm, wc = _conv_tiles(lp, width)
    offb = col_off // wc
    woffb = w_col_off // wc
    assert col_off % wc == 0 and w_col_off % wc == 0
    hb = tm // SUBLANES
    hrows = SUBLANES * (4 // dy.dtype.itemsize)
    ext = tm + hrows

    def body(x_ref, xp_ref, dy_ref, dn_ref, w_ref, dx_ref, dw_ref, db_ref):
        i = pl.program_id(1)
        last = pl.num_programs(1) - 1
        xv = x_ref[...]
        dyv = dy_ref[...].astype(F32)
        xx = jnp.concatenate([jnp.where(i > 0, xp_ref[...], 0.0), xv], axis=0)
        dd = jnp.concatenate([dyv, jnp.where(i < last, dn_ref[...].astype(F32), 0.0)], axis=0)
        dx = w_ref[kk - 1:kk, :] * dyv
        rows = [jnp.sum(dyv * xv, axis=0, keepdims=True)]
        for m in range(1, kk):
            dx = dx + w_ref[kk - 1 - m:kk - m, :] * pltpu.roll(dd, ext - m, 0)[:tm, :]
            rows.append(jnp.sum(dyv * pltpu.roll(xx, m, 0)[SUBLANES:, :], axis=0, keepdims=True))
        dx_ref[...] = dx.astype(dx_ref.dtype)
        dwp = jnp.concatenate(rows[::-1] + [jnp.zeros((SUBLANES - kk, wc), F32)], axis=0)

        @pl.when(i == 0)
        def _():
            dw_ref[...] = jnp.zeros_like(dw_ref)
            db_ref[...] = jnp.zeros_like(db_ref)

        dw_ref[...] += dwp
        db_ref[...] += jnp.sum(dyv, axis=0, keepdims=True)

    return pl.pallas_call(
        body, name=name, grid=(width // wc, lp // tm),
        in_specs=[pl.BlockSpec((tm, wc), lambda j, i: (i, offb + j)),
                  pl.BlockSpec((SUBLANES, wc), lambda j, i: (jnp.maximum(i * hb - 1, 0), offb + j)),
                  pl.BlockSpec((tm, wc), lambda j, i: (i, j)),
                  pl.BlockSpec((hrows, wc), lambda j, i: (jnp.minimum((i + 1) * (tm // hrows), lp // hrows - 1), j)),
                  pl.BlockSpec((kk, wc), lambda j, i: (0, woffb + j))],
        out_specs=[pl.BlockSpec((tm, wc), lambda j, i: (i, j)),
                   pl.BlockSpec((SUBLANES, wc), lambda j, i: (0, j)),
                   pl.BlockSpec((1, wc), lambda j, i: (0, j))],
        out_shape=[jax.ShapeDtypeStruct((lp, width), BF16),
                   jax.ShapeDtypeStruct((SUBLANES, width), F32),
                   jax.ShapeDtypeStruct((1, width), F32)],
        compiler_params=pltpu.CompilerParams(dimension_semantics=("parallel", "arbitrary")),
    )(x, x, dy, dy, w)


_FFN_K = 3
_FFN_WC = 1408


def _conv3_ext(x_ext, w_ref, b_ref):
    return (b_ref[...] + w_ref[2:3, :] * x_ext + w_ref[1:2, :] * pltpu.roll(x_ext, 1, 0)
            + w_ref[0:1, :] * pltpu.roll(x_ext, 2, 0))


def _ffn_convact_fwd(hp, cw, cb, name):
    lp = hp.shape[0]
    tm = _pick(lp, (320, 256, 128))
    wc = _FFN_WC
    nj = D_FF // wc
    hb = tm // SUBLANES

    def body(g_ref, gp_ref, u_ref, up_ref, wg_ref, wu_ref, bg_ref, bu_ref, a_ref, hg_ref, hu_ref):
        i = pl.program_id(1)

        def conv(x_ref, p_ref, w_ref, b_ref):
            x_ext = jnp.concatenate([jnp.where(i > 0, p_ref[...], 0.0), x_ref[...]], axis=0)
            return _conv3_ext(x_ext, w_ref, b_ref)[SUBLANES:, :]

        hg = conv(g_ref, gp_ref, wg_ref, bg_ref)
        hu = conv(u_ref, up_ref, wu_ref, bu_ref)
        a_ref[...] = (_gelu(hg) * hu).astype(a_ref.dtype)
        hg_ref[...] = hg.astype(hg_ref.dtype)
        hu_ref[...] = hu.astype(hu_ref.dtype)

    tile = lambda off: pl.BlockSpec((tm, wc), lambda j, i: (i, off + j))
    prev = lambda off: pl.BlockSpec((SUBLANES, wc), lambda j, i: (jnp.maximum(i * hb - 1, 0), off + j))
    vec = lambda rows, off: pl.BlockSpec((rows, wc), lambda j, i: (0, off + j))
    return pl.pallas_call(
        body, name=name, grid=(nj, lp // tm),
        in_specs=[tile(0), prev(0), tile(nj), prev(nj), vec(_FFN_K, 0), vec(_FFN_K, nj), vec(1, 0), vec(1, nj)],
        out_specs=[tile(0)] * 3,
        out_shape=[jax.ShapeDtypeStruct((lp, D_FF), BF16)] * 3,
        compiler_params=pltpu.CompilerParams(dimension_semantics=("parallel", "parallel")),
    )(hp, hp, hp, hp, cw, cw, cb, cb)


def _ffn_act_bwd(hg, hu, dact, name):
    lp = hg.shape[0]
    tm = _pick(lp, (320, 256, 128))

    def body(r, first, g_ref, u_ref, da_ref, dg_ref, du_ref):
        gl, dgl = _gelu_and_grad(g_ref[...].astype(F32))
        da = da_ref[...]
        dg_ref[...] = (da * u_ref[...].astype(F32) * dgl).astype(dg_ref.dtype)
        du_ref[...] = (da * gl).astype(du_ref.dtype)

    return _rowcall(name, body, lp, tm, rows=[(hg, D_FF, 0), (hu, D_FF, 0), (dact, D_FF, 0)],
                    outs=[(D_FF, BF16), (D_FF, BF16)])


def _lru_gates(x, wa_ref, wx_ref, ba, bx, lam):
    xb = x.astype(BF16)
    za, zx = [], []
    for n in range(LRU_BLOCKS):
        xs = xb[:, n * LRU_BS:(n + 1) * LRU_BS]
        za.append(_dot(xs, wa_ref[n]))
        zx.append(_dot(xs, wx_ref[n]))
    r = _sigmoid(jnp.concatenate(za, axis=1) + ba)
    ig = _sigmoid(jnp.concatenate(zx, axis=1) + bx)
    sp = _softplus(-lam)
    log_a = -LRU_C * r * sp
    a = jnp.exp(log_a)
    om = _neg_expm1(2.0 * log_a)
    mult = jnp.sqrt(om)
    return xb, r, ig, sp, a, om, mult


def _lru_fwd(proj, xrc, wa, wx, ba, bx, lam, name):
    lp, d = xrc.shape
    tm = BLOCK

    def body(r_idx, first, gate_ref, x_ref, wa_ref, wx_ref, ba_ref, bx_ref, lam_ref,
             y_ref, h_ref, carry):
        @pl.when(first)
        def _():
            carry[...] = jnp.zeros_like(carry)

        x = x_ref[...]
        _, _, ig, _, a, _, mult = _lru_gates(x, wa_ref, wx_ref, ba_ref[...], bx_ref[...], lam_ref[...])
        u = jnp.where(_real_rows(r_idx, tm), mult * ig * x, 0.0)
        acum, hloc = _scan_fwd(a, u, tm)
        h = hloc + acum * carry[0:1, :]
        h_ref[...] = h
        carry[0:1, :] = h[tm - 1:tm, :]
        y_ref[...] = (_gelu(gate_ref[...]) * h).astype(y_ref.dtype)

    return _rowcall(name, body, lp, tm, rows=[(proj, d, 0), (xrc, d, 0)],
                    vecs=[wa, wx, ba, bx, lam], outs=[(d, BF16), (d, F32)],
                    scratch=[pltpu.VMEM((SUBLANES, d), F32)], seq=True)


def _lru_bwd(proj, xrc, hl, dmix, wa, wx, ba, bx, lam, name):
    lp, d = xrc.shape
    tm = BLOCK

    def body(r_idx, first, gate_ref, x_ref, h_ref, dy_ref, hp_ref, wa_ref, wx_ref, ba_ref, bx_ref,
             lam_ref, dgate_ref, dx_ref, dwa_ref, dwx_ref, dba_ref, dbx_ref, dlam_ref, carry):
        @pl.when(first)
        def _():
            carry[...] = jnp.zeros_like(carry)
            dwa_ref[...] = jnp.zeros_like(dwa_ref)
            dwx_ref[...] = jnp.zeros_like(dwx_ref)
            dba_ref[...] = jnp.zeros_like(dba_ref)
            dbx_ref[...] = jnp.zeros_like(dbx_ref)
            dlam_ref[...] = jnp.zeros_like(dlam_ref)

        x = x_ref[...]
        lam = lam_ref[...]
        xb, r, ig, sp, a, om, mult = _lru_gates(x, wa_ref, wx_ref, ba_ref[...], bx_ref[...], lam)
        h = h_ref[...]
        dy = dy_ref[...]
        gl, dgl = _gelu_and_grad(gate_ref[...])
        dgate_ref[...] = (dy * h * dgl).astype(dgate_ref.dtype)
        row = _row_iota(tm)
        lastrow = row == tm - 1
        xg = dy * gl + jnp.where(lastrow, carry[0:1, :], 0.0)
        c = jnp.where(lastrow, 1.0, pltpu.roll(a, tm - 1, 0))
        _, g = _scan_rev(c, xg, tm)
        carry[0:1, :] = a[0:1, :] * g[0:1, :]
        hprev_in = jnp.where(r_idx > 0, hp_ref[SUBLANES - 1:SUBLANES, :], 0.0)
        hprev = jnp.where(row == 0, hprev_in, pltpu.roll(h, 1, 0))
        du = jnp.where(_real_rows(r_idx, tm), g, 0.0)
        da = g * hprev
        dmult = du * ig * x
        dig = du * mult * x
        dxv = du * mult * ig
        e2 = 1.0 - om
        dlog_a = da * a - dmult * e2 / mult
        dr = dlog_a * (-LRU_C) * sp
        dsp = jnp.sum(dlog_a * (-LRU_C) * r, axis=0, keepdims=True)
        dlam_ref[...] += -dsp * _sigmoid(-lam)
        dza = dr * r * (1.0 - r)
        dzx = dig * ig * (1.0 - ig)
        dba_ref[...] += jnp.sum(dza, axis=0, keepdims=True)
        dbx_ref[...] += jnp.sum(dzx, axis=0, keepdims=True)
        dzab = dza.astype(BF16)
        dzxb = dzx.astype(BF16)
        parts = []
        for n in range(LRU_BLOCKS):
            sl = slice(n * LRU_BS, (n + 1) * LRU_BS)
            dwa_ref[n] += _dot_tn(xb[:, sl], dzab[:, sl])
            dwx_ref[n] += _dot_tn(xb[:, sl], dzxb[:, sl])
            parts.append(_dot_nt(dzab[:, sl], wa_ref[n]) + _dot_nt(dzxb[:, sl], wx_ref[n]))
        dx_ref[...] = dxv + jnp.concatenate(parts, axis=1)

    return _rowcall(name, body, lp, tm,
                    rows=[(proj, d, 0), (xrc, d, 0), (hl, d, 0), (dmix, d, 0)],
                    prevs=[(hl, d, 0)], vecs=[wa, wx, ba, bx, lam],
                    outs=[(d, BF16), (d, F32)],
                    accs=[((LRU_BLOCKS, LRU_BS, LRU_BS), F32), ((LRU_BLOCKS, LRU_BS, LRU_BS), F32),
                          ((1, d), F32), ((1, d), F32), ((1, d), F32)],
                    scratch=[pltpu.VMEM((SUBLANES, d), F32)], reverse=True, seq=True)


_SLOPES = [2.0 ** (-8.0 * (h + 1) / N_Q_HEADS) for h in range(N_Q_HEADS)]
_QK_SCALE = HEAD_DIM ** -0.5
_QCOL = 2 * D_MODEL // D_MODEL
_KCOL = (3 * D_MODEL) // LANES
_VCOL = _KCOL + 1


def _attn_masks(n):
    start = pl.multiple_of(jnp.maximum(n - 1, 0) * BLOCK, BLOCK)
    qi = n * BLOCK + lax.broadcasted_iota(jnp.int32, (BLOCK, 2 * BLOCK), 0)
    kj = start + lax.broadcasted_iota(jnp.int32, (BLOCK, 2 * BLOCK), 1)
    dist = qi - kj
    ok = (kj >= BLOCK) & (dist >= 0) & (dist < WINDOW)
    dm = (n * BLOCK - PAD + lax.broadcasted_iota(jnp.int32, (BLOCK, N_META), 0)
          - lax.broadcasted_iota(jnp.int32, (BLOCK, N_META), 1))
    okm = dm >= 0
    return start, ok, dist.astype(F32), okm, jnp.minimum(dm, WINDOW).astype(F32)


def _group_rows(ref, g):
    return jnp.concatenate(
        [ref[:, (g * Q_PER_KV + hh) * HEAD_DIM:(g * Q_PER_KV + hh + 1) * HEAD_DIM] for hh in range(Q_PER_KV)],
        axis=0).astype(BF16)


def _attn_probs(qg, kg, kmg, sink_ref, g, ok, distf, okm, dmf):
    slope = jnp.stack([jnp.full((1, 1), _SLOPES[g * Q_PER_KV + hh], F32) for hh in range(Q_PER_KV)])
    sink = jnp.stack([sink_ref[0:1, g * Q_PER_KV + hh:g * Q_PER_KV + hh + 1] for hh in range(Q_PER_KV)])
    s = (_dot_nt(qg, kg) * _QK_SCALE).reshape(Q_PER_KV, BLOCK, 2 * BLOCK)
    sm = (_dot_nt(qg, kmg) * _QK_SCALE).reshape(Q_PER_KV, BLOCK, N_META)
    s = jnp.where(ok[None], s - slope * distf[None], NEG)
    sm = jnp.where(okm[None], sm - slope * dmf[None], NEG)
    mx = jnp.maximum(jnp.maximum(jnp.max(s, axis=-1, keepdims=True),
                                 jnp.max(sm, axis=-1, keepdims=True)), sink)
    p = jnp.exp(s - mx)
    pm = jnp.exp(sm - mx)
    ps = jnp.exp(sink - mx)
    inv = 1.0 / (jnp.sum(p, axis=-1, keepdims=True) + jnp.sum(pm, axis=-1, keepdims=True) + ps)
    return p * inv, pm * inv, ps * inv


def _attn_fwd(proj, sinks, name, plan=None):
    lp = proj.shape[0]
    nblk = lp // BLOCK
    p_in, p_shapes, p_out, p_scr = _plan_parts(plan)

    def body(*refs):
        q_ref, k_ref, v_ref, sink_ref = refs[:4]
        cins = refs[4:4 + len(p_in)]
        o_ref = refs[4 + len(p_in)]
        couts = refs[5 + len(p_in):5 + len(p_in) + len(p_out)]
        sems = refs[5 + len(p_in) + len(p_out):]
        n = pl.program_id(0)
        if plan is not None:
            @pl.when(n == 0)
            def _():
                plan.start(cins, couts, sems)

        start, ok, distf, okm, dmf = _attn_masks(n)
        kb = k_ref[pl.ds(start, 2 * BLOCK), :].astype(BF16)
        vb = v_ref[pl.ds(start, 2 * BLOCK), :].astype(BF16)
        km = k_ref[PAD:BLOCK, :].astype(BF16)
        vm = v_ref[PAD:BLOCK, :].astype(BF16)
        for g in range(N_KV_HEADS):
            gs = slice(g * HEAD_DIM, (g + 1) * HEAD_DIM)
            pn, pmn, _ = _attn_probs(_group_rows(q_ref, g), kb[:, gs], km[:, gs], sink_ref, g,
                                     ok, distf, okm, dmf)
            o = (_dot(pn.astype(BF16).reshape(Q_PER_KV * BLOCK, 2 * BLOCK), vb[:, gs])
                 + _dot(pmn.astype(BF16).reshape(Q_PER_KV * BLOCK, N_META), vm[:, gs]))
            for hh in range(Q_PER_KV):
                h = g * Q_PER_KV + hh
                o_ref[:, h * HEAD_DIM:(h + 1) * HEAD_DIM] = o[hh * BLOCK:(hh + 1) * BLOCK, :].astype(o_ref.dtype)
        if plan is not None:
            @pl.when(n == nblk - 1)
            def _():
                plan.wait(cins, couts, sems)

    res = pl.pallas_call(
        body, name=name, grid=(nblk,),
        in_specs=[pl.BlockSpec((BLOCK, D_MODEL), lambda n: (n, _QCOL)),
                  pl.BlockSpec((lp, LANES), lambda n: (0, _KCOL)),
                  pl.BlockSpec((lp, LANES), lambda n: (0, _VCOL)),
                  pl.BlockSpec(sinks.shape, lambda n: (0, 0))] + p_in,
        out_specs=[pl.BlockSpec((BLOCK, D_MODEL), lambda n: (n, 0))] + p_out,
        out_shape=[jax.ShapeDtypeStruct((lp, D_MODEL), BF16)] + p_shapes,
        scratch_shapes=p_scr,
        compiler_params=pltpu.CompilerParams(dimension_semantics=("arbitrary",)),
    )(proj, proj, proj, sinks, *(plan.ins if plan is not None else []))
    return res[0], res[1:]


def _attn_bwd(proj, sinks, dmix, name, plan=None):
    lp = proj.shape[0]
    nblk = lp // BLOCK

    p_in, p_shapes, p_out, p_scr = _plan_parts(plan)

    def body(*refs):
        q_ref, k_ref, v_ref, sink_ref, dy_ref = refs[:5]
        cins = refs[5:5 + len(p_in)]
        dq_ref, dk_ref, dv_ref, ds_ref = refs[5 + len(p_in):9 + len(p_in)]
        couts = refs[9 + len(p_in):9 + len(p_in) + len(p_out)]
        sems = refs[9 + len(p_in) + len(p_out):]
        n = pl.program_id(0)

        @pl.when(n == 0)
        def _():
            dk_ref[...] = jnp.zeros_like(dk_ref)
            dv_ref[...] = jnp.zeros_like(dv_ref)
            ds_ref[...] = jnp.zeros_like(ds_ref)
            if plan is not None:
                plan.start(cins, couts, sems)

        start, ok, distf, okm, dmf = _attn_masks(n)
        kb = k_ref[pl.ds(start, 2 * BLOCK), :].astype(BF16)
        vb = v_ref[pl.ds(start, 2 * BLOCK), :].astype(BF16)
        km = k_ref[PAD:BLOCK, :].astype(BF16)
        vm = v_ref[PAD:BLOCK, :].astype(BF16)
        lane16 = lax.broadcasted_iota(jnp.int32, (1, N_Q_HEADS), 1)
        dsink = jnp.zeros((1, N_Q_HEADS), F32)
        rows = Q_PER_KV * BLOCK
        for g in range(N_KV_HEADS):
            gs = slice(g * HEAD_DIM, (g + 1) * HEAD_DIM)
            qg = _group_rows(q_ref, g)
            dog = _group_rows(dy_ref, g)
            pn, pmn, psn = _attn_probs(qg, kb[:, gs], km[:, gs], sink_ref, g, ok, distf, okm, dmf)
            dp = _dot_nt(dog, vb[:, gs]).reshape(Q_PER_KV, BLOCK, 2 * BLOCK)
            dpm = _dot_nt(dog, vm[:, gs]).reshape(Q_PER_KV, BLOCK, N_META)
            delta = (jnp.sum(pn * dp, axis=-1, keepdims=True)
                     + jnp.sum(pmn * dpm, axis=-1, keepdims=True))
            dsb = (pn * (dp - delta)).astype(BF16).reshape(rows, 2 * BLOCK)
            dsm = (pmn * (dpm - delta)).astype(BF16).reshape(rows, N_META)
            dsk = jnp.sum(psn * delta, axis=1, keepdims=True)
            for hh in range(Q_PER_KV):
                dsink = dsink - jnp.where(lane16 == g * Q_PER_KV + hh, dsk[hh], 0.0)
            dq = (_dot(dsb, kb[:, gs]) + _dot(dsm, km[:, gs])) * _QK_SCALE
            for hh in range(Q_PER_KV):
                h = g * Q_PER_KV + hh
                dq_ref[:, h * HEAD_DIM:(h + 1) * HEAD_DIM] = dq[hh * BLOCK:(hh + 1) * BLOCK, :].astype(dq_ref.dtype)
            pnb = pn.astype(BF16).reshape(rows, 2 * BLOCK)
            pmnb = pmn.astype(BF16).reshape(rows, N_META)
            dk_ref[pl.ds(start, 2 * BLOCK), gs] += _dot_tn(dsb, qg) * _QK_SCALE
            dv_ref[pl.ds(start, 2 * BLOCK), gs] += _dot_tn(pnb, dog)
            dk_ref[PAD:BLOCK, gs] += _dot_tn(dsm, qg) * _QK_SCALE
            dv_ref[PAD:BLOCK, gs] += _dot_tn(pmnb, dog)
        ds_ref[...] += dsink
        if plan is not None:
            @pl.when(n == nblk - 1)
            def _():
                plan.wait(cins, couts, sems)

    res = pl.pallas_call(
        body, name=name, grid=(nblk,),
        in_specs=[pl.BlockSpec((BLOCK, D_MODEL), lambda n: (n, _QCOL)),
                  pl.BlockSpec((lp, LANES), lambda n: (0, _KCOL)),
                  pl.BlockSpec((lp, LANES), lambda n: (0, _VCOL)),
                  pl.BlockSpec(sinks.shape, lambda n: (0, 0)),
                  pl.BlockSpec((BLOCK, D_MODEL), lambda n: (n, 1))] + p_in,
        out_specs=[pl.BlockSpec((BLOCK, D_MODEL), lambda n: (n, 0)),
                   pl.BlockSpec((lp, LANES), lambda n: (0, 0)),
                   pl.BlockSpec((lp, LANES), lambda n: (0, 0)),
                   pl.BlockSpec((1, N_Q_HEADS), lambda n: (0, 0))] + p_out,
        out_shape=[jax.ShapeDtypeStruct((lp, D_MODEL), BF16),
                   jax.ShapeDtypeStruct((lp, LANES), F32),
                   jax.ShapeDtypeStruct((lp, LANES), F32),
                   jax.ShapeDtypeStruct((1, N_Q_HEADS), F32)] + p_shapes,
        scratch_shapes=p_scr,
        compiler_params=pltpu.CompilerParams(dimension_semantics=("arbitrary",)),
    )(proj, proj, proj, sinks, dmix, *(plan.ins if plan is not None else []))
    return res[:4], res[4:]


_ZW = D_SSM
_XBC_W = D_SSM + 2 * SSD_GROUPS * SSD_N
_DT_COL = (_ZW + _XBC_W) // LANES
EVEN_IN = 3 * D_MODEL + 2 * LANES
ODD_IN = _ZW + _XBC_W + SSD_HEADS
ODD_IN_PAD = _ZW + _XBC_W + LANES


def _ssm_prep_fwd(xc, proj, dt_bias, name):
    lp = xc.shape[0]
    tm = _pick(lp, (320, 256, 128))

    def body(r, first, xc_ref, dtr_ref, b_ref, act_ref, dt_ref):
        real = _real_rows(r, tm)
        act, _ = _silu_and_grad(xc_ref[...])
        act_ref[...] = jnp.where(real, act, 0.0)
        dt_ref[...] = jnp.where(real, _softplus(dtr_ref[...] + b_ref[...]), 0.0)

    return _rowcall(name, body, lp, tm, rows=[(xc, _XBC_W, 0), (proj, LANES, _DT_COL)],
                    vecs=[dt_bias], outs=[(_XBC_W, F32), (LANES, F32)])


def _ssm_prep_bwd(xc, proj, dt_bias, dxs, dxskip, db, dc, ddt, name):
    lp = xc.shape[0]
    tm = BLOCK

    def body(r, first, xc_ref, dtr_ref, dxs_ref, dsk_ref, db_ref, dc_ref, ddt_ref, b_ref,
             dxc_ref, ddtr_ref, dbias_ref):
        real = _real_rows(r, tm)
        _, ds = _silu_and_grad(xc_ref[...])
        dxc_ref[:, :D_SSM] = jnp.where(real, (dxs_ref[...] + dsk_ref[...]) * ds[:, :D_SSM], 0.0)
        dxc_ref[:, D_SSM:D_SSM + 1024] = jnp.where(real, db_ref[...] * ds[:, D_SSM:D_SSM + 1024], 0.0)
        dxc_ref[:, D_SSM + 1024:] = jnp.where(real, dc_ref[...] * ds[:, D_SSM + 1024:], 0.0)
        dd = jnp.where(real, ddt_ref[...] * _sigmoid(dtr_ref[...] + b_ref[...]), 0.0)
        ddtr_ref[...] = dd.astype(ddtr_ref.dtype)
        _acc_add(first, dbias_ref, jnp.sum(dd, axis=0, keepdims=True))

    return _rowcall(name, body, lp, tm,
                    rows=[(xc, _XBC_W, 0), (proj, LANES, _DT_COL), (dxs, D_SSM, 0), (dxskip, D_SSM, 0),
                          (db, 1024, 0), (dc, 1024, 0), (ddt, LANES, 0)],
                    vecs=[dt_bias], outs=[(_XBC_W, F32), (LANES, BF16)], accs=[((1, LANES), F32)])


def _ssd_common(dt, alog):
    a = -jnp.exp(alog)
    cs = _cumsum_rows(dt * a, BLOCK)
    cst = cs.T
    cl = cs[BLOCK - 1:BLOCK, :]
    tril = (lax.broadcasted_iota(jnp.int32, (BLOCK, BLOCK), 0)
            >= lax.broadcasted_iota(jnp.int32, (BLOCK, BLOCK), 1))
    return a, cs, cst, cl, jnp.exp(cs), jnp.exp(cl - cs), jnp.exp(cl), tril


def _head_cols(ecl, g):
    lane = lax.broadcasted_iota(jnp.int32, (1, SSD_HPG * SSD_P), 1)
    e = [ecl[:, SSD_HPG * g + hh:SSD_HPG * g + hh + 1] for hh in range(SSD_HPG)]
    return jnp.where(lane < SSD_P, e[0], jnp.where(lane < 2 * SSD_P, e[1],
                                                   jnp.where(lane < 3 * SSD_P, e[2], e[3])))


def _ssd_fwd(xbc, dt, alog, name, plan=None):
    lp = xbc.shape[0]
    nc = lp // BLOCK
    gw = SSD_HPG * SSD_P
    p_in, p_shapes, p_out, p_scr = _plan_parts(plan)

    def body(*refs):
        xs_ref, b_ref, c_ref, dt_ref, alog_ref = refs[:5]
        cins = refs[5:5 + len(p_in)]
        y_ref, so_ref = refs[5 + len(p_in):7 + len(p_in)]
        couts = refs[7 + len(p_in):7 + len(p_in) + len(p_out)]
        st, fx = refs[7 + len(p_in) + len(p_out):9 + len(p_in) + len(p_out)]
        sems = refs[9 + len(p_in) + len(p_out):]
        n = pl.program_id(0)

        @pl.when(n == 0)
        def _():
            st[...] = jnp.zeros_like(st)
            if plan is not None:
                plan.start(cins, couts, sems)

        dtv = dt_ref[...]
        _, cs, cst, cl, e, f, ecl, tril = _ssd_common(dtv, alog_ref[...])
        for g in range(SSD_GROUPS):
            bg = b_ref[:, g * SSD_N:(g + 1) * SSD_N].astype(BF16)
            cg = c_ref[:, g * SSD_N:(g + 1) * SSD_N].astype(BF16)
            gm = _dot_nt(cg, bg)
            stg = st[g]
            so_ref[0, g] = stg
            yoff = _dot(cg, stg.astype(BF16))
            for hh in range(SSD_HPG):
                h = SSD_HPG * g + hh
                hs = slice(h * SSD_P, (h + 1) * SSD_P)
                seg = cs[:, h:h + 1] - cst[h:h + 1, :]
                m = gm * jnp.exp(jnp.where(tril, seg, NEG))
                xdt = xs_ref[:, hs] * dtv[:, h:h + 1]
                y_ref[:, hs] = (_dot(m.astype(BF16), xdt.astype(BF16))
                                + e[:, h:h + 1] * yoff[:, hh * SSD_P:(hh + 1) * SSD_P])
                fx[:, hh * SSD_P:(hh + 1) * SSD_P] = f[:, h:h + 1] * xdt
            st[g] = stg * _head_cols(ecl, g) + _dot_tn(bg, fx[...].astype(BF16))
        if plan is not None:
            @pl.when(n == nc - 1)
            def _():
                plan.wait(cins, couts, sems)

    res = pl.pallas_call(
        body, name=name, grid=(nc,),
        in_specs=[pl.BlockSpec((BLOCK, D_SSM), lambda n: (n, 0)),
                  pl.BlockSpec((BLOCK, 1024), lambda n: (n, 2)),
                  pl.BlockSpec((BLOCK, 1024), lambda n: (n, 3)),
                  pl.BlockSpec((BLOCK, LANES), lambda n: (n, 0)),
                  pl.BlockSpec((1, LANES), lambda n: (0, 0))] + p_in,
        out_specs=[pl.BlockSpec((BLOCK, D_SSM), lambda n: (n, 0)),
                   pl.BlockSpec((1, SSD_GROUPS, SSD_N, gw), lambda n: (n, 0, 0, 0))] + p_out,
        out_shape=[jax.ShapeDtypeStruct((lp, D_SSM), F32),
                   jax.ShapeDtypeStruct((nc, SSD_GROUPS, SSD_N, gw), F32)] + p_shapes,
        scratch_shapes=[pltpu.VMEM((SSD_GROUPS, SSD_N, gw), F32), pltpu.VMEM((BLOCK, gw), F32)] + p_scr,
        compiler_params=pltpu.CompilerParams(dimension_semantics=("arbitrary",)),
    )(xbc, xbc, xbc, dt, alog, *(plan.ins if plan is not None else []))
    return res[:2], res[2:]


def _ssd_bwd(xbc, dt, alog, states, dy, name, plan=None):
    lp = xbc.shape[0]
    nc = lp // BLOCK
    gw = SSD_HPG * SSD_P
    p_in, p_shapes, p_out, p_scr = _plan_parts(plan)

    def body(*refs):
        xs_ref, b_ref, c_ref, dt_ref, alog_ref, dy_ref, st_ref = refs[:7]
        cins = refs[7:7 + len(p_in)]
        dxs_ref, db_ref, dc_ref, ddt_ref, dalog_ref = refs[7 + len(p_in):12 + len(p_in)]
        couts = refs[12 + len(p_in):12 + len(p_in) + len(p_out)]
        dst, edy, fx = refs[12 + len(p_in) + len(p_out):15 + len(p_in) + len(p_out)]
        sems = refs[15 + len(p_in) + len(p_out):]
        i = pl.program_id(0)

        @pl.when(i == 0)
        def _():
            dst[...] = jnp.zeros_like(dst)
            dalog_ref[...] = jnp.zeros_like(dalog_ref)
            if plan is not None:
                plan.start(cins, couts, sems)

        dtv = dt_ref[...]
        a, cs, cst, cl, e, f, ecl, tril = _ssd_common(dtv, alog_ref[...])
        lane = lax.broadcasted_iota(jnp.int32, (1, LANES), 1)
        sub = _row_iota(BLOCK)
        dcs = jnp.zeros((BLOCK, LANES), F32)
        dcst = jnp.zeros((LANES, BLOCK), F32)
        dcl = jnp.zeros((1, LANES), F32)
        ddtx = jnp.zeros((BLOCK, LANES), F32)
        for g in range(SSD_GROUPS):
            bg = b_ref[:, g * SSD_N:(g + 1) * SSD_N].astype(BF16)
            cg = c_ref[:, g * SSD_N:(g + 1) * SSD_N].astype(BF16)
            gm = _dot_nt(cg, bg)
            stg = st_ref[0, g]
            stb = stg.astype(BF16)
            dso = dst[g]
            dsob = dso.astype(BF16)
            yraw = _dot(cg, stb)
            dfx = _dot(bg, dsob)
            prodsum = jnp.sum(dso * stg, axis=0, keepdims=True)
            dgm = jnp.zeros((BLOCK, BLOCK), F32)
            for hh in range(SSD_HPG):
                h = SSD_HPG * g + hh
                hs = slice(h * SSD_P, (h + 1) * SSD_P)
                ls = slice(hh * SSD_P, (hh + 1) * SSD_P)
                eh = e[:, h:h + 1]
                fh = f[:, h:h + 1]
                dth = dtv[:, h:h + 1]
                xh = xs_ref[:, hs]
                xdt = xh * dth
                dyh = dy_ref[:, hs]
                dyb = dyh.astype(BF16)
                lam = jnp.exp(jnp.where(tril, cs[:, h:h + 1] - cst[h:h + 1, :], NEG))
                m = gm * lam
                dm = _dot_nt(dyb, xdt.astype(BF16))
                dxdt = _dot_tn(m.astype(BF16), dyb) + fh * dfx[:, ls]
                w = dm * m
                dgm = dgm + dm * lam
                dff = jnp.sum(dfx[:, ls] * xdt, axis=1, keepdims=True) * fh
                col = (jnp.sum(w, axis=1, keepdims=True)
                       + jnp.sum(dyh * yraw[:, ls], axis=1, keepdims=True) * eh - dff)
                onl = (lane == h).astype(F32)
                dcs = dcs + col * onl
                dcst = dcst - (sub == h).astype(F32) * jnp.sum(w, axis=0, keepdims=True)
                dclh = (jnp.sum(dff, axis=0, keepdims=True)
                        + ecl[:, h:h + 1] * jnp.sum(prodsum[:, ls], axis=1, keepdims=True))
                dcl = dcl + dclh * onl
                ddtx = ddtx + jnp.sum(dxdt * xh, axis=1, keepdims=True) * onl
                dxs_ref[:, hs] = dxdt * dth
                edy[:, ls] = eh * dyh
                fx[:, ls] = fh * xdt
            edyb = edy[...].astype(BF16)
            fxb = fx[...].astype(BF16)
            dgb = dgm.astype(BF16)
            dc_ref[:, g * SSD_N:(g + 1) * SSD_N] = _dot_nt(edyb, stb) + _dot(dgb, bg)
            db_ref[:, g * SSD_N:(g + 1) * SSD_N] = _dot_nt(fxb, dsob) + _dot_tn(dgb, cg)
            dst[g] = dso * _head_cols(ecl, g) + _dot_tn(cg, edyb)
        dcs = dcs + dcst.T + jnp.where(sub == BLOCK - 1, dcl, 0.0)
        dda = _rev_cumsum_rows(dcs, BLOCK)
        ddt_ref[...] = ddtx + dda * a
        dalog_ref[...] += jnp.sum(dda * dtv, axis=0, keepdims=True) * a
        if plan is not None:
            @pl.when(i == nc - 1)
            def _():
                plan.wait(cins, couts, sems)

    rev = lambda i: nc - 1 - i
    res = pl.pallas_call(
        body, name=name, grid=(nc,),
        in_specs=[pl.BlockSpec((BLOCK, D_SSM), lambda i: (rev(i), 0)),
                  pl.BlockSpec((BLOCK, 1024), lambda i: (rev(i), 2)),
                  pl.BlockSpec((BLOCK, 1024), lambda i: (rev(i), 3)),
                  pl.BlockSpec((BLOCK, LANES), lambda i: (rev(i), 0)),
                  pl.BlockSpec((1, LANES), lambda i: (0, 0)),
                  pl.BlockSpec((BLOCK, D_SSM), lambda i: (rev(i), 0)),
                  pl.BlockSpec((1, SSD_GROUPS, SSD_N, gw), lambda i: (rev(i), 0, 0, 0))] + p_in,
        out_specs=[pl.BlockSpec((BLOCK, D_SSM), lambda i: (rev(i), 0)),
                   pl.BlockSpec((BLOCK, 1024), lambda i: (rev(i), 0)),
                   pl.BlockSpec((BLOCK, 1024), lambda i: (rev(i), 0)),
                   pl.BlockSpec((BLOCK, LANES), lambda i: (rev(i), 0)),
                   pl.BlockSpec((1, LANES), lambda i: (0, 0))] + p_out,
        out_shape=[jax.ShapeDtypeStruct((lp, D_SSM), F32),
                   jax.ShapeDtypeStruct((lp, 1024), F32),
                   jax.ShapeDtypeStruct((lp, 1024), F32),
                   jax.ShapeDtypeStruct((lp, LANES), F32),
                   jax.ShapeDtypeStruct((1, LANES), F32)] + p_shapes,
        scratch_shapes=[pltpu.VMEM((SSD_GROUPS, SSD_N, gw), F32),
                        pltpu.VMEM((BLOCK, gw), F32), pltpu.VMEM((BLOCK, gw), F32)] + p_scr,
        compiler_params=pltpu.CompilerParams(dimension_semantics=("arbitrary",)),
    )(xbc, xbc, xbc, dt, alog, dy, states, *(plan.ins if plan is not None else []))
    return res[:5], res[5:]


_GN_GROUPS = 8
_GN_W = D_SSM // _GN_GROUPS


def _ssm_gate_fwd(yssd, xbc, proj, dskip, gnorm, name):
    lp = yssd.shape[0]
    tm = _pick(lp, (320, 256, 128))

    def body(r, first, y_ref, x_ref, z_ref, d_ref, g_ref, o_ref):
        sz, _ = _silu_and_grad(z_ref[...])
        y2 = (y_ref[...] + d_ref[...] * x_ref[...]) * sz
        for k in range(_GN_GROUPS):
            sl = slice(k * _GN_W, (k + 1) * _GN_W)
            yk = y2[:, sl]
            rs = lax.rsqrt(jnp.mean(yk * yk, axis=-1, keepdims=True) + EPS)
            o_ref[:, sl] = (yk * rs * g_ref[:, sl]).astype(o_ref.dtype)

    return _rowcall(name, body, lp, tm, rows=[(yssd, D_SSM, 0), (xbc, D_SSM, 0), (proj, D_SSM, 0)],
                    vecs=[dskip, gnorm], outs=[(D_SSM, BF16)])[0]


def _ssm_gate_bwd(yssd, xbc, proj, dskip, gnorm, dyn, name):
    lp = yssd.shape[0]
    tm = BLOCK

    def body(r, first, y_ref, x_ref, z_ref, dyn_ref, d_ref, g_ref,
             dy_ref, dx_ref, dz_ref, dd_ref, dg_ref):
        z = z_ref[...]
        sz, dsz = _silu_and_grad(z)
        xs = x_ref[...]
        y1 = y_ref[...] + d_ref[...] * xs
        y2 = y1 * sz
        dyn = dyn_ref[...]
        for k in range(_GN_GROUPS):
            sl = slice(k * _GN_W, (k + 1) * _GN_W)
            dx, dgt = _rms_bwd(y2[:, sl], g_ref[:, sl], dyn[:, sl])
            dy1 = dx * sz[:, sl]
            dy_ref[:, sl] = dy1
            dx_ref[:, sl] = dy1 * d_ref[:, sl]
            dz_ref[:, sl] = (dx * y1[:, sl] * dsz[:, sl]).astype(dz_ref.dtype)

            @pl.when(first)
            def _():
                dd_ref[:, sl] = jnp.zeros((1, _GN_W), F32)
                dg_ref[:, sl] = jnp.zeros((1, _GN_W), F32)

            dd_ref[:, sl] += jnp.sum(dy1 * xs[:, sl], axis=0, keepdims=True)
            dg_ref[:, sl] += jnp.sum(dgt, axis=0, keepdims=True)

    return _rowcall(name, body, lp, tm,
                    rows=[(yssd, D_SSM, 0), (xbc, D_SSM, 0), (proj, D_SSM, 0), (dyn, D_SSM, 0)],
                    vecs=[dskip, gnorm], outs=[(D_SSM, F32), (D_SSM, F32), (D_SSM, BF16)],
                    accs=[((1, D_SSM), F32), ((1, D_SSM), F32)])


def _adamw(w, g, m, v, name):
    r, c = w.shape
    tm = r if r <= 512 else _pick(r, (512, 352, 256, 128, 64, 32, 16, 8))
    c1 = 1.0 / (1.0 - ADAM_B1 ** ADAM_STEP)
    c2 = 1.0 / (1.0 - ADAM_B2 ** ADAM_STEP)

    def body(w_ref, g_ref, m_ref, v_ref, d_ref, nm_ref, nv_ref):
        gv = g_ref[...]
        nm = ADAM_B1 * m_ref[...] + (1.0 - ADAM_B1) * gv
        nv = ADAM_B2 * v_ref[...] + (1.0 - ADAM_B2) * (gv * gv)
        nm_ref[...] = nm
        nv_ref[...] = nv
        d_ref[...] = -ADAM_LR * ((nm * c1) / (jnp.sqrt(nv * c2) + ADAM_EPS) + ADAM_WD * w_ref[...])

    spec = pl.BlockSpec((tm, c), lambda i: (i, 0))
    return pl.pallas_call(
        body, name=name, grid=(r // tm,), in_specs=[spec] * 4, out_specs=[spec] * 3,
        out_shape=[jax.ShapeDtypeStruct((r, c), F32)] * 3,
        compiler_params=pltpu.CompilerParams(dimension_semantics=("parallel",)),
    )(w, g, m, v)


def _place():
    return lax.axis_index("x"), lax.axis_index("y"), lax.axis_index("c")


def _other_chips(x, y):
    return [(1 - x, y), (x, 1 - y), (1 - x, 1 - y)]


_ANY = pl.BlockSpec(memory_space=pl.ANY)


class _Plan:
    def __init__(self, ins, out_shapes, n_remote, n_local, issue):
        self.ins = list(ins)
        self.out_shapes = list(out_shapes)
        self.issue = issue
        self.scratch = [pltpu.SemaphoreType.DMA((max(n_remote, 1),)),
                        pltpu.SemaphoreType.DMA((max(n_remote, 1),)),
                        pltpu.SemaphoreType.DMA((max(n_local, 1),))]

    def start(self, ins, outs, sems):
        sends, _, locs = self.issue(ins, outs, *sems)
        for cp in locs + sends:
            cp.start()

    def wait(self, ins, outs, sems):
        sends, recvs, locs = self.issue(ins, outs, *sems)
        for make in recvs:
            make().wait_recv()
        for cp in sends:
            cp.wait_send()
        for cp in locs:
            cp.wait()


def _plan_parts(plan):
    if plan is None:
        return [], [], [], []
    return ([_ANY] * len(plan.ins), plan.out_shapes, [_ANY] * len(plan.out_shapes), plan.scratch)


def _run_plan(plan, name):
    n_in, n_out = len(plan.ins), len(plan.out_shapes)

    def body(*refs):
        ins, outs, sems = refs[:n_in], refs[n_in:n_in + n_out], refs[n_in + n_out:]
        plan.start(ins, outs, sems)
        plan.wait(ins, outs, sems)

    return pl.pallas_call(
        body, name=name, in_specs=[_ANY] * n_in, out_specs=[_ANY] * n_out,
        out_shape=plan.out_shapes, scratch_shapes=plan.scratch,
    )(*plan.ins)


def _gather_plan(shards):
    n = len(shards)

    def issue(ins, outs, send_sems, recv_sems, local_sems):
        x, y, c = _place()
        me = 2 * x + y
        sends, recvs, locs = [], [], []
        for p in range(n):
            locs.append(pltpu.make_async_copy(ins[p], outs[p].at[me], local_sems.at[p]))
            for k, (px, py) in enumerate(_other_chips(x, y)):
                sems = dict(send_sem=send_sems.at[3 * p + k], recv_sem=recv_sems.at[3 * p + k],
                            device_id=(px, py, c), device_id_type=MESH)
                sends.append(pltpu.make_async_remote_copy(src_ref=ins[p], dst_ref=outs[p].at[me], **sems))
                recvs.append(functools.partial(pltpu.make_async_remote_copy, src_ref=ins[p],
                                               dst_ref=outs[p].at[2 * px + py], **sems))
        return sends, recvs, locs

    return _Plan(shards, [jax.ShapeDtypeStruct((N_CHIPS,) + s.shape, s.dtype) for s in shards], 3 * n, n, issue)


_REL7 = [(fx, fy, fc) for fx in (0, 1) for fy in (0, 1) for fc in (0, 1)][1:]


def _scatter8_plan(gs):
    n = len(gs)

    def issue(ins, outs, send_sems, recv_sems, local_sems):
        x, y, c = _place()
        sends = []
        for p in range(n):
            hr = gs[p].shape[1] // 2
            for k, (fx, fy, fc) in enumerate(_REL7):
                tx, ty, tc = x ^ fx, y ^ fy, c ^ fc
                src = ins[p].at[2 * tx + ty, pl.ds(pl.multiple_of(tc * hr, SUBLANES), hr), :]
                sends.append(pltpu.make_async_remote_copy(
                    src_ref=src, dst_ref=outs[p].at[k],
                    send_sem=send_sems.at[7 * p + k], recv_sem=recv_sems.at[7 * p + k],
                    device_id=(tx, ty, tc), device_id_type=MESH))
        return sends, [functools.partial(lambda cp: cp, cp) for cp in sends], []

    shapes = [jax.ShapeDtypeStruct((7, g.shape[1] // 2, g.shape[2]), g.dtype) for g in gs]
    return _Plan(gs, shapes, 7 * n, 0, issue)


def _sibling_plan(ts):
    n = len(ts)

    def issue(ins, outs, send_sems, recv_sems, local_sems):
        x, y, c = _place()
        sends = [pltpu.make_async_remote_copy(
            src_ref=ins[p], dst_ref=outs[p], send_sem=send_sems.at[p], recv_sem=recv_sems.at[p],
            device_id=(x, y, 1 - c), device_id_type=MESH) for p in range(n)]
        return sends, [functools.partial(lambda cp: cp, cp) for cp in sends], []

    return _Plan(ts, [jax.ShapeDtypeStruct(t.shape, t.dtype) for t in ts], n, 0, issue)


def _add8(g, recv, chip, core, name):
    s, r, n = g.shape
    hr = r // 2
    th = hr // 2 if (hr // 2) % SUBLANES == 0 else hr
    nt = hr // th

    def body(chip_ref, core_ref, g_ref, r_ref, o_ref):
        acc = g_ref[0].astype(F32)
        for k in range(7):
            acc = acc + r_ref[k].astype(F32)
        o_ref[...] = acc

    return pl.pallas_call(
        body, name=name,
        grid_spec=pltpu.PrefetchScalarGridSpec(
            num_scalar_prefetch=2, grid=(nt,),
            in_specs=[pl.BlockSpec((1, th, n), lambda i, ch, co: (ch[0], co[0] * nt + i, 0)),
                      pl.BlockSpec((7, th, n), lambda i, ch, co: (0, i, 0))],
            out_specs=pl.BlockSpec((th, n), lambda i, ch, co: (i, 0))),
        out_shape=jax.ShapeDtypeStruct((hr, n), F32),
        compiler_params=pltpu.CompilerParams(dimension_semantics=("parallel",)),
    )(chip, core, g, recv)


def _adamw_halves(w, own, other, m, v, core, name):
    r, n = w.shape
    hr = r // 2
    th = hr // 2 if (hr // 2) % SUBLANES == 0 else hr
    tph = hr // th
    c1 = 1.0 / (1.0 - ADAM_B1 ** ADAM_STEP)
    c2 = 1.0 / (1.0 - ADAM_B2 ** ADAM_STEP)

    def body(core_ref, w_ref, a_ref, b_ref, m_ref, v_ref, g_ref, d_ref, nm_ref, nv_ref):
        half = pl.program_id(0) // tph
        gv = jnp.where(half == core_ref[0], a_ref[...], b_ref[...])
        nm = ADAM_B1 * m_ref[...] + (1.0 - ADAM_B1) * gv
        nv = ADAM_B2 * v_ref[...] + (1.0 - ADAM_B2) * (gv * gv)
        g_ref[...] = gv
        nm_ref[...] = nm
        nv_ref[...] = nv
        d_ref[...] = -ADAM_LR * ((nm * c1) / (jnp.sqrt(nv * c2) + ADAM_EPS) + ADAM_WD * w_ref[...])

    full = pl.BlockSpec((th, n), lambda i, co: (i, 0))
    part = pl.BlockSpec((th, n), lambda i, co: (i % tph, 0))
    return pl.pallas_call(
        body, name=name,
        grid_spec=pltpu.PrefetchScalarGridSpec(
            num_scalar_prefetch=1, grid=(2 * tph,),
            in_specs=[full, part, part, full, full], out_specs=[full] * 4),
        out_shape=[jax.ShapeDtypeStruct((r, n), F32)] * 4,
        compiler_params=pltpu.CompilerParams(dimension_semantics=("parallel",)),
    )(core, w, own, other, m, v)


def _allreduce_small(pack, name):
    r, l = pack.shape

    def body(p_ref, o_ref, land, send_sems, recv_sems):
        x, y, c = _place()
        me = 4 * x + 2 * y + c
        land[me] = p_ref[...]
        rel = [(fx, fy, fc) for fx in (0, 1) for fy in (0, 1) for fc in (0, 1)][1:]
        sends = []
        for k, (fx, fy, fc) in enumerate(rel):
            peer = (x ^ fx, y ^ fy, c ^ fc)
            cp = pltpu.make_async_remote_copy(
                src_ref=p_ref, dst_ref=land.at[me], send_sem=send_sems.at[k], recv_sem=recv_sems.at[k],
                device_id=peer, device_id_type=MESH)
            cp.start()
            sends.append(cp)
        for k, (fx, fy, fc) in enumerate(rel):
            src = 4 * (x ^ fx) + 2 * (y ^ fy) + (c ^ fc)
            pltpu.make_async_remote_copy(
                src_ref=p_ref, dst_ref=land.at[src], send_sem=send_sems.at[k], recv_sem=recv_sems.at[k],
                device_id=(x ^ fx, y ^ fy, c ^ fc), device_id_type=MESH).wait_recv()
        for cp in sends:
            cp.wait_send()
        acc = land[0]
        for d in range(1, N_DEV):
            acc = acc + land[d]
        o_ref[...] = acc

    vm = pl.BlockSpec(memory_space=pltpu.VMEM)
    return pl.pallas_call(
        body, name=name, in_specs=[vm], out_specs=vm,
        out_shape=jax.ShapeDtypeStruct((r, l), F32),
        scratch_shapes=[pltpu.VMEM((N_DEV, r, l), F32),
                        pltpu.SemaphoreType.DMA((N_DEV - 1,)), pltpu.SemaphoreType.DMA((N_DEV - 1,))],
    )(pack)


def _flat_rows(a, mult=SUBLANES * LANES):
    f = a.reshape(-1)
    padn = (-f.shape[0]) % mult
    if padn:
        f = jnp.concatenate([f, jnp.zeros((padn,), f.dtype)])
    return f


def _pack(arrs, mult=SUBLANES * LANES, total_mult=None):
    flat = [_flat_rows(a, mult) for a in arrs]
    sizes = [f.shape[0] for f in flat]
    if total_mult is not None:
        padn = (-sum(sizes)) % total_mult
        if padn:
            flat.append(jnp.zeros((padn,), flat[0].dtype))
    return jnp.concatenate(flat).reshape(-1, LANES), sizes


def _unpack(pack, shapes, sizes, lead=()):
    flat = pack.reshape(lead + (-1,))
    out, off = [], 0
    for shp, sz in zip(shapes, sizes):
        n = math.prod(shp)
        out.append(flat[..., off:off + n].reshape(lead + tuple(shp)))
        off += sz
    return out


def _cols_from_shards(g):
    s, k, n = g.shape
    return jnp.transpose(g, (1, 0, 2)).reshape(k, s * n)


def _cols_to_shards(w, s=N_CHIPS):
    k, n = w.shape
    return jnp.transpose(w.reshape(k, s, n // s), (1, 0, 2))


def _ffn_fwd(h, pre, post, w_up, cw, cb, w_down, tag):
    u = _rmsnorm_fwd(h, pre, f"{tag}_prenorm")
    hp = _mm_nn_sh(u, w_up, 2 * D_FF, f"{tag}_up")
    act, hg, hu = _ffn_convact_fwd(hp, cw, cb, f"{tag}_convact")
    o = _mm_nn(act, w_down, F32, f"{tag}_down")
    hn = _postnorm_res_fwd(h, o, post, f"{tag}_postnorm")
    return hn, (h, u, hp, hg, hu, act, o)


def _ffn_bwd(dh, saved, pre, post, w_up, cw, w_down, tag):
    h, u, hp, hg, hu, act, o = saved
    do, dpost = _postnorm_bwd(o, post, dh, f"{tag}_postnorm_bwd")
    dact = _mm_nt(do, w_down, f"{tag}_down_dx")
    dw_down = _mm_tn(act, do, f"{tag}_down_dw")
    dhg, dhu = _ffn_act_bwd(hg, hu, dact, f"{tag}_act_bwd")
    dxg, dwg, dbg = _conv_bwd(hp, 0, D_FF, dhg, cw, f"{tag}_conv_bwd_gate")
    dxu, dwu, dbu = _conv_bwd(hp, D_FF, D_FF, dhu, cw, f"{tag}_conv_bwd_up", w_col_off=D_FF)
    dhp = (dxg, dxu)
    dcw = jnp.concatenate([dwg, dwu], axis=1)
    dcb = jnp.concatenate([dbg, dbu], axis=1)
    du = _mm_nt_sh(dhp, w_up, f"{tag}_up_dx")
    dw_up = _mm_tn_sh(u, dhp, w_up.shape[2], f"{tag}_up_dw")
    dhn, dpre = _prenorm_bwd(h, pre, du, dh, f"{tag}_prenorm_bwd")
    return dhn, dict(pre=dpre, post=dpost, w_up=dw_up, conv_w=dcw[:3], conv_b=dcb, w_down=dw_down)


class _Exchange:
    GATHER_IN_ATTN = ("l0_w_out", "l0_ffn_w_up", "l0_ffn_w_down", "l1_w_in")
    GATHER_IN_SSD = ("l1_w_out", "l1_ffn_w_up", "l1_ffn_w_down")
    AFTER_L1_OUT = ("l1_ffn_w_up", "l1_ffn_w_down", "l1_w_out")
    AFTER_L0_OUT = ("l1_w_in", "l0_ffn_w_up", "l0_ffn_w_down", "l0_w_out")
    LAST = ("l0_w_in",)

    def __init__(self, late_shards):
        self.late = dict(late_shards)
        self.slabs = {}
        self.recv = {}

    def gather_plan(self, names):
        return _gather_plan([self.late[n] for n in names])

    def gathered(self, names, outs):
        return {n: (g if n in _BIG_COL else g.reshape(-1, g.shape[-1])) for n, g in zip(names, outs)}

    def scatter_plan(self, grads, names):
        for n in names:
            g = grads[n]
            self.slabs[n] = g if n in _BIG_COL else g.reshape(N_CHIPS, -1, g.shape[-1])
        return _scatter8_plan([self.slabs[n] for n in names])

    def scattered(self, names, outs):
        self.recv.update(zip(names, outs))


def _local_step(x, tgt, meta, P, ex=None):
    seq, d = x.shape
    lp = seq + BLOCK
    h0 = jnp.concatenate([jnp.zeros((PAD, d), F32), meta, x], axis=0)
    tgt_p = jnp.concatenate([jnp.zeros((BLOCK, d), F32), tgt], axis=0)

    u0 = _rmsnorm_fwd(h0, P["l0_mix_pre_norm"], "l0_mix_prenorm")
    proj0 = _mm_nn_sh(u0, P["l0_w_in"], EVEN_IN, "l0_in")
    xrc = _conv_fwd(proj0, D_MODEL, D_MODEL, P["l0_lru_conv_w"], P["l0_lru_conv_b"], "l0_lru_conv")
    lru_args = (P["l0_lru_w_a"], P["l0_lru_w_x"], P["l0_lru_b_a"], P["l0_lru_b_x"], P["l0_lru_lambda"])
    ya, hl = _lru_fwd(proj0, xrc, *lru_args, "l0_lru")
    yb, outs = _attn_fwd(proj0, P["l0_attn_sinks"], "l0_attn",
                         ex.gather_plan(ex.GATHER_IN_ATTN) if ex else None)
    if ex:
        P = {**P, **ex.gathered(ex.GATHER_IN_ATTN, outs)}
    mix0 = jnp.concatenate([ya, yb], axis=1)
    o0 = _mm_nn(mix0, P["l0_w_out"], F32, "l0_out")
    h1 = _postnorm_res_fwd(h0, o0, P["l0_mix_post_norm"], "l0_mix_postnorm")
    h2, ffn0 = _ffn_fwd(h1, P["l0_ffn_pre_norm"], P["l0_ffn_post_norm"], P["l0_ffn_w_up"],
                        P["l0_ffn_conv_w"], P["l0_ffn_conv_b"], P["l0_ffn_w_down"], "l0_ffn")
    u2 = _rmsnorm_fwd(h2, P["l1_mix_pre_norm"], "l1_mix_prenorm")
    proj1 = _mm_nn_sh(u2, P["l1_w_in"], ODD_IN_PAD, "l1_in")
    xc1 = _conv_fwd(proj1, _ZW, _XBC_W, P["l1_ssm_conv_w"], P["l1_ssm_conv_b"], "l1_ssm_conv")
    xbc, dt = _ssm_prep_fwd(xc1, proj1, P["l1_dt_bias"], "l1_ssm_prep")
    (yssd, states), outs = _ssd_fwd(xbc, dt, P["l1_a_log"], "l1_ssd",
                                    ex.gather_plan(ex.GATHER_IN_SSD) if ex else None)
    if ex:
        P = {**P, **ex.gathered(ex.GATHER_IN_SSD, outs)}
    yn = _ssm_gate_fwd(yssd, xbc, proj1, P["l1_d_skip"], P["l1_gate_norm"], "l1_ssm_gate")
    o1 = _mm_nn(yn, P["l1_w_out"], F32, "l1_out")
    h3 = _postnorm_res_fwd(h2, o1, P["l1_mix_post_norm"], "l1_mix_postnorm")
    h4, ffn1 = _ffn_fwd(h3, P["l1_ffn_pre_norm"], P["l1_ffn_post_norm"], P["l1_ffn_w_up"],
                        P["l1_ffn_conv_w"], P["l1_ffn_conv_b"], P["l1_ffn_w_down"], "l1_ffn")
    dh4, loss_cols = _loss_fwd_bwd(h4, tgt_p, "loss")

    G = {}
    dh3, g = _ffn_bwd(dh4, ffn1, P["l1_ffn_pre_norm"], P["l1_ffn_post_norm"], P["l1_ffn_w_up"],
                      P["l1_ffn_conv_w"], P["l1_ffn_w_down"], "l1_ffn")
    for k, v in g.items():
        G["l1_ffn_" + (k + "_norm" if k in ("pre", "post") else k)] = v
    do1, G["l1_mix_post_norm"] = _postnorm_bwd(o1, P["l1_mix_post_norm"], dh3, "l1_mix_postnorm_bwd")
    dyn = _mm_nt(do1, P["l1_w_out"], "l1_out_dx")
    G["l1_w_out"] = _mm_tn(yn, do1, "l1_out_dw")
    dyssd, dxskip, dz, dd_cols, G["l1_gate_norm"] = _ssm_gate_bwd(
        yssd, xbc, proj1, P["l1_d_skip"], P["l1_gate_norm"], dyn, "l1_ssm_gate_bwd")
    G["l1_d_skip"] = dd_cols.reshape(SSD_HEADS, SSD_P).sum(axis=1)
    (dxs, dbm, dcm, ddt, dalog), outs = _ssd_bwd(
        xbc, dt, P["l1_a_log"], states, dyssd, "l1_ssd_bwd",
        ex.scatter_plan(G, ex.AFTER_L1_OUT) if ex else None)
    if ex:
        ex.scattered(ex.AFTER_L1_OUT, outs)
    G["l1_a_log"] = dalog[0, :SSD_HEADS]
    dxc, ddtr, dbias = _ssm_prep_bwd(xc1, proj1, P["l1_dt_bias"], dxs, dxskip, dbm, dcm, ddt,
                                     "l1_ssm_prep_bwd")
    G["l1_dt_bias"] = dbias[0, :SSD_HEADS]
    dxbc, dcw, dcb = _conv_bwd(proj1, _ZW, _XBC_W, dxc, P["l1_ssm_conv_w"], "l1_ssm_conv_bwd")
    G["l1_ssm_conv_w"] = dcw[:4]
    G["l1_ssm_conv_b"] = dcb
    dproj1 = jnp.concatenate([dz, dxbc, ddtr], axis=1)
    du2 = _mm_nt_sh(dproj1, P["l1_w_in"], "l1_in_dx")
    G["l1_w_in"] = _mm_tn_sh(u2, dproj1, ODD_IN // N_CHIPS, "l1_in_dw")
    dh2, G["l1_mix_pre_norm"] = _prenorm_bwd(h2, P["l1_mix_pre_norm"], du2, dh3, "l1_mix_prenorm_bwd")
    dh1, g = _ffn_bwd(dh2, ffn0, P["l0_ffn_pre_norm"], P["l0_ffn_post_norm"], P["l0_ffn_w_up"],
                      P["l0_ffn_conv_w"], P["l0_ffn_w_down"], "l0_ffn")
    for k, v in g.items():
        G["l0_ffn_" + (k + "_norm" if k in ("pre", "post") else k)] = v
    do0, G["l0_mix_post_norm"] = _postnorm_bwd(o0, P["l0_mix_post_norm"], dh1, "l0_mix_postnorm_bwd")
    dmix = _mm_nt(do0, P["l0_w_out"], "l0_out_dx")
    G["l0_w_out"] = _mm_tn(mix0, do0, "l0_out_dw")
    (dgate, dxrc, G["l0_lru_w_a"], G["l0_lru_w_x"], G["l0_lru_b_a"], G["l0_lru_b_x"],
     G["l0_lru_lambda"]) = _lru_bwd(proj0, xrc, hl, dmix, *lru_args, "l0_lru_bwd")
    dxr, dcw, dcb = _conv_bwd(proj0, D_MODEL, D_MODEL, dxrc, P["l0_lru_conv_w"], "l0_lru_conv_bwd")
    G["l0_lru_conv_w"] = dcw[:4]
    G["l0_lru_conv_b"] = dcb
    (dq, dk, dv, G["l0_attn_sinks"]), outs = _attn_bwd(
        proj0, P["l0_attn_sinks"], dmix, "l0_attn_bwd",
        ex.scatter_plan(G, ex.AFTER_L0_OUT) if ex else None)
    if ex:
        ex.scattered(ex.AFTER_L0_OUT, outs)
    dproj0 = jnp.concatenate([dgate, dxr, dq, dk.astype(BF16), dv.astype(BF16)], axis=1)
    du0 = _mm_nt_sh(dproj0, P["l0_w_in"], "l0_in_dx")
    G["l0_w_in"] = _mm_tn_sh(u0, dproj0, EVEN_IN // N_CHIPS, "l0_in_dw")
    dh0, G["l0_mix_pre_norm"] = _prenorm_bwd(h0, P["l0_mix_pre_norm"], du0, dh1, "l0_mix_prenorm_bwd")
    return loss_cols, dh0[BLOCK:], dh0[PAD:BLOCK], G


_BIG_COL = ("l0_w_in", "l0_ffn_w_up", "l1_w_in", "l1_ffn_w_up")
_BIG_ROW = ("l0_w_out", "l0_ffn_w_down", "l1_w_out", "l1_ffn_w_down")
_BIG = ("l0_w_in", "l0_w_out", "l0_ffn_w_up", "l0_ffn_w_down",
        "l1_w_in", "l1_w_out", "l1_ffn_w_up", "l1_ffn_w_down")
_SMALL_SHARDED = ("meta_tokens", "l0_lru_conv_w", "l0_ffn_conv_w", "l1_ssm_conv_w", "l1_ffn_conv_w")
_WEIGHTS = ("meta_tokens", "l0_mix_pre_norm", "l0_mix_post_norm", "l0_w_in", "l0_lru_conv_w",
            "l0_lru_conv_b", "l0_lru_w_a", "l0_lru_b_a", "l0_lru_w_x", "l0_lru_b_x", "l0_lru_lambda",
            "l0_attn_sinks", "l0_w_out", "l0_ffn_pre_norm", "l0_ffn_post_norm", "l0_ffn_w_up",
            "l0_ffn_conv_w", "l0_ffn_conv_b", "l0_ffn_w_down", "l1_mix_pre_norm", "l1_mix_post_norm",
            "l1_w_in", "l1_ssm_conv_w", "l1_ssm_conv_b", "l1_dt_bias", "l1_a_log", "l1_d_skip",
            "l1_gate_norm", "l1_w_out", "l1_ffn_pre_norm", "l1_ffn_post_norm", "l1_ffn_w_up",
            "l1_ffn_conv_w", "l1_ffn_conv_b", "l1_ffn_w_down")
_REPL = tuple(n for n in _WEIGHTS if n not in _BIG and n not in _SMALL_SHARDED)


def _pad_lanes(v, n=LANES):
    return jnp.concatenate([v, jnp.zeros((n - v.shape[0],), v.dtype)]).reshape(1, n)


def _step(x, tgt, W, M, V):
    cx, cy, cc = _place()
    chip = 2 * cx + cy

    small_pack, small_sizes = _pack([W[n] for n in _SMALL_SHARDED])
    first = _run_plan(_gather_plan([W["l0_w_in"].astype(BF16), small_pack]), "gather_first")
    small_full = _unpack(first[1], [W[n].shape for n in _SMALL_SHARDED], small_sizes, lead=(N_CHIPS,))
    ex = _Exchange({n: W[n].astype(BF16) for n in _BIG if n != "l0_w_in"})

    P = {"l0_w_in": first[0]}
    for n, g in zip(_SMALL_SHARDED, small_full):
        P[n] = _cols_from_shards(g)
    for n in _REPL:
        v = W[n]
        P[n] = v.reshape(1, -1) if v.ndim == 1 else v
    P["l0_lru_w_a"] = W["l0_lru_w_a"].astype(BF16)
    P["l0_lru_w_x"] = W["l0_lru_w_x"].astype(BF16)
    P["l1_dt_bias"] = _pad_lanes(W["l1_dt_bias"])
    P["l1_a_log"] = _pad_lanes(W["l1_a_log"])
    P["l1_d_skip"] = jnp.repeat(W["l1_d_skip"], SSD_P).reshape(1, D_SSM)
    meta = P.pop("meta_tokens")

    loss_cols, grad_x, grad_meta, G = _local_step(x, tgt, meta, P, ex)
    G["meta_tokens"] = grad_meta

    ex.scattered(ex.LAST, _run_plan(ex.scatter_plan(G, ex.LAST), "grad_scatter_last"))
    core_idx = cc.astype(jnp.int32).reshape(1)
    chip_idx = chip.astype(jnp.int32).reshape(1)
    own_half = [_add8(ex.slabs[n], ex.recv[n], chip_idx, core_idx, f"grad_sum_{n}") for n in _BIG]
    other_half = _run_plan(_sibling_plan(own_half), "grad_sibling_swap")
    small_names = list(_REPL) + list(_SMALL_SHARDED)
    small_list = [G[n] for n in small_names] + [loss_cols]
    spack, ssizes = _pack(small_list)
    sred = _allreduce_small(spack, "small_allreduce")
    sfull = _unpack(sred, [a.shape for a in small_list], ssizes)
    loss = 0.5 / D_MODEL * jnp.sum(sfull[-1])
    small_grads = {}
    for n, g in zip(small_names, sfull[:-1]):
        if n in _SMALL_SHARDED:
            wcols = W[n].shape[1]
            g = lax.dynamic_slice_in_dim(g, chip * wcols, wcols, axis=1)
        small_grads[n] = g.reshape(W[n].shape)

    grads, delta, new_m, new_v = {}, {}, {}, {}
    for n, own, other in zip(_BIG, own_half, other_half):
        grads[n], delta[n], new_m[n], new_v[n] = _adamw_halves(
            W[n], own, other, M[n], V[n], core_idx, f"adamw_{n}")
    s_names = [n for n in _WEIGHTS if n not in _BIG]
    tile_elems = 512 * LANES
    wp, wsz = _pack([W[n] for n in s_names], total_mult=tile_elems)
    gp, _ = _pack([small_grads[n] for n in s_names], total_mult=tile_elems)
    mp, _ = _pack([M[n] for n in s_names], total_mult=tile_elems)
    vp, _ = _pack([V[n] for n in s_names], total_mult=tile_elems)
    dp, nmp, nvp = _adamw(wp, gp, mp, vp, "adamw_small")
    shapes = [W[n].shape for n in s_names]
    for n, a, b, c_ in zip(s_names, _unpack(dp, shapes, wsz), _unpack(nmp, shapes, wsz),
                           _unpack(nvp, shapes, wsz)):
        grads[n] = small_grads[n]
        delta[n], new_m[n], new_v[n] = a, b, c_
    return loss, grad_x, grads, delta, new_m, new_v


def kernel(x, meta_tokens, l0_mix_pre_norm, l0_mix_post_norm, l0_w_in, l0_lru_conv_w, l0_lru_conv_b, l0_lru_w_a, l0_lru_b_a, l0_lru_w_x, l0_lru_b_x, l0_lru_lambda, l0_attn_sinks, l0_w_out, l0_ffn_pre_norm, l0_ffn_post_norm, l0_ffn_w_up, l0_ffn_conv_w, l0_ffn_conv_b, l0_ffn_w_down, l1_mix_pre_norm, l1_mix_post_norm, l1_w_in, l1_ssm_conv_w, l1_ssm_conv_b, l1_dt_bias, l1_a_log, l1_d_skip, l1_gate_norm, l1_w_out, l1_ffn_pre_norm, l1_ffn_post_norm, l1_ffn_w_up, l1_ffn_conv_w, l1_ffn_conv_b, l1_ffn_w_down, loss_target, m_meta_tokens, m_l0_mix_pre_norm, m_l0_mix_post_norm, m_l0_w_in, m_l0_lru_conv_w, m_l0_lru_conv_b, m_l0_lru_w_a, m_l0_lru_b_a, m_l0_lru_w_x, m_l0_lru_b_x, m_l0_lru_lambda, m_l0_attn_sinks, m_l0_w_out, m_l0_ffn_pre_norm, m_l0_ffn_post_norm, m_l0_ffn_w_up, m_l0_ffn_conv_w, m_l0_ffn_conv_b, m_l0_ffn_w_down, m_l1_mix_pre_norm, m_l1_mix_post_norm, m_l1_w_in, m_l1_ssm_conv_w, m_l1_ssm_conv_b, m_l1_dt_bias, m_l1_a_log, m_l1_d_skip, m_l1_gate_norm, m_l1_w_out, m_l1_ffn_pre_norm, m_l1_ffn_post_norm, m_l1_ffn_w_up, m_l1_ffn_conv_w, m_l1_ffn_conv_b, m_l1_ffn_w_down, v_meta_tokens, v_l0_mix_pre_norm, v_l0_mix_post_norm, v_l0_w_in, v_l0_lru_conv_w, v_l0_lru_conv_b, v_l0_lru_w_a, v_l0_lru_b_a, v_l0_lru_w_x, v_l0_lru_b_x, v_l0_lru_lambda, v_l0_attn_sinks, v_l0_w_out, v_l0_ffn_pre_norm, v_l0_ffn_post_norm, v_l0_ffn_w_up, v_l0_ffn_conv_w, v_l0_ffn_conv_b, v_l0_ffn_w_down, v_l1_mix_pre_norm, v_l1_mix_post_norm, v_l1_w_in, v_l1_ssm_conv_w, v_l1_ssm_conv_b, v_l1_dt_bias, v_l1_a_log, v_l1_d_skip, v_l1_gate_norm, v_l1_w_out, v_l1_ffn_pre_norm, v_l1_ffn_post_norm, v_l1_ffn_w_up, v_l1_ffn_conv_w, v_l1_ffn_conv_b, v_l1_ffn_w_down):
    args = locals()
    W = {n: args[n] for n in _WEIGHTS}
    M = {n: args["m_" + n] for n in _WEIGHTS}
    V = {n: args["v_" + n] for n in _WEIGHTS}
    loss, grad_x, grads, delta, new_m, new_v = _step(x[0], loss_target[0], W, M, V)
    return (loss, grad_x[None], *[grads[n] for n in _WEIGHTS], *[delta[n] for n in _WEIGHTS],
            *[new_m[n] for n in _WEIGHTS], *[new_v[n] for n in _WEIGHTS])
```

```python
import functools
import math

import jax
import jax.numpy as jnp
from jax import lax
from jax.experimental import pallas as pl
from jax.experimental.pallas import tpu as pltpu

F32 = jnp.float32
BF16 = jnp.bfloat16

D_MODEL = 1024
N_META = 16
BLOCK = 128
PAD = BLOCK - N_META
EPS = 1e-6
LRU_BLOCKS = 8
LRU_BS = 128
LRU_C = 8.0
N_Q_HEADS = 16
N_KV_HEADS = 2
HEAD_DIM = 64
Q_PER_KV = 8
WINDOW = 128
D_SSM = 2048
SSD_HEADS = 32
SSD_GROUPS = 8
SSD_HPG = 4
SSD_P = 64
SSD_N = 128
D_FF = 2816
NEG = -1e30
LANES = 128
SUBLANES = 8

ADAM_LR = 0.001
ADAM_B1 = 0.9
ADAM_B2 = 0.999
ADAM_EPS = 1e-08
ADAM_WD = 0.01
ADAM_STEP = 10

MESH = pl.DeviceIdType.MESH
N_CHIPS = 4
N_DEV = 8


def _pick(n, cands):
    for c in cands:
        if n % c == 0:
            return c
    raise ValueError(f"no tile for {n} in {cands}")


def _col_tile(n, limit=1792):
    best = None
    for t in range(LANES, min(n, limit) + 1, LANES):
        if n % t == 0:
            best = t
    if best is None:
        raise ValueError(f"no lane tile for {n}")
    return best


def _sigmoid(x):
    return 1.0 / (1.0 + jnp.exp(-x))


def _log1p(e):
    u = 1.0 + e
    return jnp.where(u == 1.0, e, jnp.log(u) * (e / jnp.where(u == 1.0, 1.0, u - 1.0)))


def _softplus(x):
    return jnp.maximum(x, 0.0) + _log1p(jnp.exp(-jnp.abs(x)))


def _neg_expm1(x):
    poly = x * (1.0 + x * (0.5 + x * (1.0 / 6.0 + x * (1.0 / 24.0 + x * (1.0 / 120.0)))))
    return -jnp.where(x > -0.05, poly, jnp.exp(x) - 1.0)


_GELU_C = math.sqrt(2.0 / math.pi)


def _gelu(x):
    t = jnp.tanh(_GELU_C * (x + 0.044715 * x * x * x))
    return 0.5 * x * (1.0 + t)


def _gelu_and_grad(x):
    x2 = x * x
    t = jnp.tanh(_GELU_C * (x + 0.044715 * x * x2))
    g = 0.5 * x * (1.0 + t)
    dg = 0.5 * (1.0 + t) + 0.5 * x * (1.0 - t * t) * _GELU_C * (1.0 + 3.0 * 0.044715 * x2)
    return g, dg


def _silu_and_grad(x):
    s = _sigmoid(x)
    return x * s, s * (1.0 + x * (1.0 - s))


def _dot(a, b):
    return jnp.dot(a, b, preferred_element_type=F32)


def _dot_nt(a, b):
    return lax.dot_general(a, b, (((1,), (1,)), ((), ())), preferred_element_type=F32)


def _dot_tn(a, b):
    return lax.dot_general(a, b, (((0,), (0,)), ((), ())), preferred_element_type=F32)


def _row_iota(t):
    return lax.broadcasted_iota(jnp.int32, (t, 1), 0)


def _scan_fwd(a, u, t):
    row = _row_iota(t)
    d = 1
    while d < t:
        m = row >= d
        u_sh = jnp.where(m, pltpu.roll(u, d, 0), 0.0)
        a_sh = jnp.where(m, pltpu.roll(a, d, 0), 1.0)
        u = u + a * u_sh
        a = a * a_sh
        d *= 2
    return a, u


def _scan_rev(c, x, t):
    row = _row_iota(t)
    d = 1
    while d < t:
        m = row < t - d
        x_sh = jnp.where(m, pltpu.roll(x, t - d, 0), 0.0)
        c_sh = jnp.where(m, pltpu.roll(c, t - d, 0), 1.0)
        x = x + c * x_sh
        c = c * c_sh
        d *= 2
    return c, x


def _cumsum_rows(x, t):
    row = _row_iota(t)
    d = 1
    while d < t:
        x = x + jnp.where(row >= d, pltpu.roll(x, d, 0), 0.0)
        d *= 2
    return x


def _rev_cumsum_rows(x, t):
    row = _row_iota(t)
    d = 1
    while d < t:
        x = x + jnp.where(row < t - d, pltpu.roll(x, t - d, 0), 0.0)
        d *= 2
    return x


def _rms_bwd(x, g, dy):
    rs = lax.rsqrt(jnp.mean(x * x, axis=-1, keepdims=True) + EPS)
    gy = dy * g
    dx = rs * gy - x * (rs * rs * rs) * jnp.mean(x * gy, axis=-1, keepdims=True)
    return dx, dy * x * rs


def _mm_nn(a, w, out_dtype, name):
    m, k = a.shape
    n = w.shape[1]
    tm = _pick(m, (640, 512, 256, 128))
    tn = _col_tile(n)

    def body(a_ref, w_ref, o_ref):
        o_ref[...] = _dot(a_ref[...].astype(BF16), w_ref[...]).astype(o_ref.dtype)

    return pl.pallas_call(
        body, name=name, grid=(n // tn, m // tm),
        in_specs=[pl.BlockSpec((tm, k), lambda j, i: (i, 0)),
                  pl.BlockSpec((k, tn), lambda j, i: (0, j))],
        out_specs=pl.BlockSpec((tm, tn), lambda j, i: (i, j)),
        out_shape=jax.ShapeDtypeStruct((m, n), out_dtype),
        compiler_params=pltpu.CompilerParams(dimension_semantics=("parallel", "parallel")),
    )(a, w)


def _mm_nt(dy, w, name):
    m, n = dy.shape
    k = w.shape[0]
    wide = n > 3328
    tm = _pick(m, (320, 256, 128)) if wide else _pick(m, (640, 512, 256, 128))
    tk = _col_tile(k, 512 if wide else 1408)

    def body(dy_ref, w_ref, o_ref):
        o_ref[...] = _dot_nt(dy_ref[...].astype(BF16), w_ref[...])

    return pl.pallas_call(
        body, name=name, grid=(k // tk, m // tm),
        in_specs=[pl.BlockSpec((tm, n), lambda j, i: (i, 0)),
                  pl.BlockSpec((tk, n), lambda j, i: (j, 0))],
        out_specs=pl.BlockSpec((tm, tk), lambda j, i: (i, j)),
        out_shape=jax.ShapeDtypeStruct((m, k), F32),
        compiler_params=pltpu.CompilerParams(dimension_semantics=("parallel", "parallel")),
    )(dy, w)


def _mm_tn(a, dy, name):
    m, k = a.shape
    n = dy.shape[1]
    tm = _pick(m, (640, 512, 256, 128))
    tk = _col_tile(k, 1408)
    tn = _col_tile(n, 1664)
    nsteps = m // tm

    def body(a_ref, dy_ref, o_ref, acc):
        @pl.when(pl.program_id(2) == 0)
        def _():
            acc[...] = jnp.zeros_like(acc)

        acc[...] += _dot_tn(a_ref[...].astype(BF16), dy_ref[...].astype(BF16))

        @pl.when(pl.program_id(2) == nsteps - 1)
        def _():
            o_ref[...] = acc[...].astype(o_ref.dtype)

    return pl.pallas_call(
        body, name=name, grid=(k // tk, n // tn, nsteps),
        in_specs=[pl.BlockSpec((tm, tk), lambda kk, j, i: (i, kk)),
                  pl.BlockSpec((tm, tn), lambda kk, j, i: (i, j))],
        out_specs=pl.BlockSpec((tk, tn), lambda kk, j, i: (kk, j)),
        out_shape=jax.ShapeDtypeStruct((k, n), BF16),
        scratch_shapes=[pltpu.VMEM((tk, tn), F32)],
        compiler_params=pltpu.CompilerParams(
            dimension_semantics=("parallel", "parallel", "arbitrary")),
    )(a, dy)


def _mm_nn_sh(a, w4, n_out, name):
    m, k = a.shape
    s, _, n = w4.shape
    tm = _pick(m, (320, 256, 128))

    def body(a_ref, w_ref, o_ref):
        av = a_ref[...].astype(BF16)
        for j in range(s):
            o_ref[:, j * n:(j + 1) * n] = _dot(av, w_ref[j])
        if n_out > s * n:
            o_ref[:, s * n:] = jnp.zeros((tm, n_out - s * n), F32)

    return pl.pallas_call(
        body, name=name, grid=(m // tm,),
        in_specs=[pl.BlockSpec((tm, k), lambda i: (i, 0)),
                  pl.BlockSpec((s, k, n), lambda i: (0, 0, 0))],
        out_specs=pl.BlockSpec((tm, n_out), lambda i: (i, 0)),
        out_shape=jax.ShapeDtypeStruct((m, n_out), F32),
        compiler_params=pltpu.CompilerParams(dimension_semantics=("parallel",)),
    )(a, w4)


def _mm_nt_sh(dy, w4, name):
    dys = dy if isinstance(dy, (tuple, list)) else (dy,)
    m = dys[0].shape[0]
    s, k, n = w4.shape
    tm = _pick(m, (640, 512, 256, 128))
    tk = _col_tile(k, 256)
    where = _shard_columns(dys, s, n)

    def body(*refs):
        w_ref, o_ref = refs[len(dys)], refs[len(dys) + 1]
        acc = None
        for j, (p, c0) in enumerate(where):
            t = _dot_nt(refs[p][:, c0:c0 + n].astype(BF16), w_ref[j])
            acc = t if acc is None else acc + t
        o_ref[...] = acc

    return pl.pallas_call(
        body, name=name, grid=(k // tk, m // tm),
        in_specs=[pl.BlockSpec((tm, d.shape[1]), lambda j, i: (i, 0)) for d in dys]
        + [pl.BlockSpec((s, tk, n), lambda j, i: (0, j, 0))],
        out_specs=pl.BlockSpec((tm, tk), lambda j, i: (i, j)),
        out_shape=jax.ShapeDtypeStruct((m, k), F32),
        compiler_params=pltpu.CompilerParams(dimension_semantics=("parallel", "parallel")),
    )(*dys, w4)


def _shard_columns(dys, s, n):
    where = []
    for p, d in enumerate(dys):
        where += [(p, c * n) for c in range(d.shape[1] // n)]
    assert len(where) >= s
    return where[:s]


def _mm_tn_sh(a, dy, n, name):
    dys = dy if isinstance(dy, (tuple, list)) else (dy,)
    m, k = a.shape
    s = N_CHIPS
    tm = _pick(m, (640, 512, 256, 128))
    tk = _col_tile(k, 512 if n <= 1024 else 256)
    nsteps = m // tm
    where = _shard_columns(dys, s, n)

    def body(*refs):
        a_ref, o_ref, acc = refs[0], refs[len(dys) + 1], refs[len(dys) + 2]

        @pl.when(pl.program_id(1) == 0)
        def _():
            acc[...] = jnp.zeros_like(acc)

        av = a_ref[...].astype(BF16)
        for j, (p, c0) in enumerate(where):
            acc[j] += _dot_tn(av, refs[1 + p][:, c0:c0 + n].astype(BF16))

        @pl.when(pl.program_id(1) == nsteps - 1)
        def _():
            o_ref[...] = acc[...].astype(o_ref.dtype)

    return pl.pallas_call(
        body, name=name, grid=(k // tk, nsteps),
        in_specs=[pl.BlockSpec((tm, tk), lambda kk, i: (i, kk))]
        + [pl.BlockSpec((tm, d.shape[1]), lambda kk, i: (i, 0)) for d in dys],
        out_specs=pl.BlockSpec((s, tk, n), lambda kk, i: (0, kk, 0)),
        out_shape=jax.ShapeDtypeStruct((s, k, n), BF16),
        scratch_shapes=[pltpu.VMEM((s, tk, n), F32)],
        compiler_params=pltpu.CompilerParams(dimension_semantics=("parallel", "arbitrary")),
    )(a, *dys)


def _rowcall(name, body, lp, tm, rows=(), prevs=(), vecs=(), outs=(), accs=(), scratch=(),
             reverse=False, seq=False):
    nt = lp // tm
    hb = tm // SUBLANES

    def ri(i):
        return nt - 1 - i if reverse else i

    in_specs, args = [], []
    for arr, w, cb in rows:
        in_specs.append(pl.BlockSpec((tm, w), lambda i, cb=cb: (ri(i), cb)))
        args.append(arr)
    for arr, w, cb in prevs:
        in_specs.append(pl.BlockSpec((SUBLANES, w), lambda i, cb=cb: (jnp.maximum(ri(i) * hb - 1, 0), cb)))
        args.append(arr)
    for arr in vecs:
        in_specs.append(pl.BlockSpec(arr.shape, lambda i, nd=arr.ndim: (0,) * nd))
        args.append(arr)
    out_shape, out_specs = [], []
    for w, dt in outs:
        out_shape.append(jax.ShapeDtypeStruct((lp, w), dt))
        out_specs.append(pl.BlockSpec((tm, w), lambda i: (ri(i), 0)))
    for shp, dt in accs:
        out_shape.append(jax.ShapeDtypeStruct(shp, dt))
        out_specs.append(pl.BlockSpec(shp, lambda i, nd=len(shp): (0,) * nd))

    def kern(*refs):
        i = pl.program_id(0)
        body(ri(i), i == 0, *refs)

    sem = ("arbitrary",) if (seq or accs) else ("parallel",)
    res = pl.pallas_call(
        kern, name=name, grid=(nt,), in_specs=in_specs, out_specs=out_specs,
        out_shape=out_shape, scratch_shapes=list(scratch),
        compiler_params=pltpu.CompilerParams(dimension_semantics=sem),
    )(*args)
    return res


def _acc_add(first, ref, val):
    @pl.when(first)
    def _():
        ref[...] = jnp.zeros_like(ref)

    ref[...] += val


def _real_rows(r, tm):
    return (r * tm + _row_iota(tm)) >= PAD


def _rmsnorm_fwd(h, g, name):
    lp, d = h.shape
    tm = _pick(lp, (640, 512, 256, 128))

    def body(r, first, h_ref, g_ref, u_ref):
        x = h_ref[...]
        rs = lax.rsqrt(jnp.mean(x * x, axis=-1, keepdims=True) + EPS)
        u_ref[...] = (x * rs * g_ref[...]).astype(u_ref.dtype)

    return _rowcall(name, body, lp, tm, rows=[(h, d, 0)], vecs=[g], outs=[(d, BF16)])[0]


def _postnorm_res_fwd(h, o, g, name):
    lp, d = h.shape
    tm = _pick(lp, (640, 512, 256, 128))

    def body(r, first, h_ref, o_ref, g_ref, out_ref):
        x = o_ref[...]
        rs = lax.rsqrt(jnp.mean(x * x, axis=-1, keepdims=True) + EPS)
        out_ref[...] = jnp.where(_real_rows(r, tm), h_ref[...] + x * rs * g_ref[...], 0.0)

    return _rowcall(name, body, lp, tm, rows=[(h, d, 0), (o, d, 0)], vecs=[g], outs=[(d, F32)])[0]


def _postnorm_bwd(o, g, dh, name):
    lp, d = o.shape
    tm = _pick(lp, (640, 512, 256, 128))

    def body(r, first, o_ref, dh_ref, g_ref, do_ref, dg_ref):
        dx, dgt = _rms_bwd(o_ref[...], g_ref[...], dh_ref[...])
        do_ref[...] = dx.astype(do_ref.dtype)
        _acc_add(first, dg_ref, jnp.sum(dgt, axis=0, keepdims=True))

    return _rowcall(name, body, lp, tm, rows=[(o, d, 0), (dh, d, 0)], vecs=[g],
                    outs=[(d, BF16)], accs=[((1, d), F32)])


def _prenorm_bwd(h, g, du, dh_res, name):
    lp, d = h.shape
    tm = _pick(lp, (640, 512, 256, 128))

    def body(r, first, h_ref, du_ref, dres_ref, g_ref, dh_ref, dg_ref):
        dx, dgt = _rms_bwd(h_ref[...], g_ref[...], du_ref[...])
        dh_ref[...] = jnp.where(_real_rows(r, tm), dres_ref[...] + dx, 0.0)
        _acc_add(first, dg_ref, jnp.sum(dgt, axis=0, keepdims=True))

    return _rowcall(name, body, lp, tm, rows=[(h, d, 0), (du, d, 0), (dh_res, d, 0)], vecs=[g],
                    outs=[(d, F32)], accs=[((1, d), F32)])


def _loss_fwd_bwd(h, tgt, name):
    lp, d = h.shape
    tm = _pick(lp, (640, 512, 256, 128))

    def body(r, first, h_ref, t_ref, dh_ref, ls_ref):
        tok = (r * tm + _row_iota(tm)) >= BLOCK
        e = jnp.where(tok, h_ref[...] - t_ref[...], 0.0)
        dh_ref[...] = e * (1.0 / d)
        _acc_add(first, ls_ref, jnp.sum(e * e, axis=0, keepdims=True))

    return _rowcall(name, body, lp, tm, rows=[(h, d, 0), (tgt, d, 0)],
                    outs=[(d, F32)], accs=[((1, d), F32)])


def _conv_tiles(lp, width):
    wc = _col_tile(width, 1408)
    tm = _pick(lp, (320, 256, 128))
    return tm, wc


def _conv_fwd(x, col_off, width, w, b, name):
    lp = x.shape[0]
    kk = w.shape[0]
    tm, wc = _conv_tiles(lp, width)
    offb = col_off // wc
    assert col_off % wc == 0
    hb = tm // SUBLANES

    def body(x_ref, xp_ref, w_ref, b_ref, y_ref):
        i = pl.program_id(1)
        xv = x_ref[...]
        halo = jnp.where(i > 0, xp_ref[...], 0.0)
        xx = jnp.concatenate([halo, xv], axis=0)
        acc = b_ref[...] + w_ref[kk - 1:kk, :] * xv
        for j in range(1, kk):
            acc = acc + w_ref[kk - 1 - j:kk - j, :] * pltpu.roll(xx, j, 0)[SUBLANES:, :]
        y_ref[...] = acc

    return pl.pallas_call(
        body, name=name, grid=(width // wc, lp // tm),
        in_specs=[pl.BlockSpec((tm, wc), lambda j, i: (i, offb + j)),
                  pl.BlockSpec((SUBLANES, wc), lambda j, i: (jnp.maximum(i * hb - 1, 0), offb + j)),
                  pl.BlockSpec((kk, wc), lambda j, i: (0, j)),
                  pl.BlockSpec((1, wc), lambda j, i: (0, j))],
        out_specs=pl.BlockSpec((tm, wc), lambda j, i: (i, j)),
        out_shape=jax.ShapeDtypeStruct((lp, width), F32),
        compiler_params=pltpu.CompilerParams(dimension_semantics=("parallel", "parallel")),
    )(x, x, w, b)


def _conv_bwd(x, col_off, width, dy, w, name, w_col_off=0):
    lp = x.shape[0]
    kk = w.shape[0]
    tm, wc = _conv_tiles(lp, width)
    offb = col_off // wc
    woffb = w_col_off // wc
    assert col_off % wc == 0 and w_col_off % wc == 0
    hb = tm // SUBLANES
    hrows = SUBLANES * (4 // dy.dtype.itemsize)
    ext = tm + hrows

    def body(x_ref, xp_ref, dy_ref, dn_ref, w_ref, dx_ref, dw_ref, db_ref):
        i = pl.program_id(1)
        last = pl.num_programs(1) - 1
        xv = x_ref[...]
        dyv = dy_ref[...].astype(F32)
        xx = jnp.concatenate([jnp.where(i > 0, xp_ref[...], 0.0), xv], axis=0)
        dd = jnp.concatenate([dyv, jnp.where(i < last, dn_ref[...].astype(F32), 0.0)], axis=0)
        dx = w_ref[kk - 1:kk, :] * dyv
        rows = [jnp.sum(dyv * xv, axis=0, keepdims=True)]
        for m in range(1, kk):
            dx = dx + w_ref[kk - 1 - m:kk - m, :] * pltpu.roll(dd, ext - m, 0)[:tm, :]
            rows.append(jnp.sum(dyv * pltpu.roll(xx, m, 0)[SUBLANES:, :], axis=0, keepdims=True))
        dx_ref[...] = dx.astype(dx_ref.dtype)
        dwp = jnp.concatenate(rows[::-1] + [jnp.zeros((SUBLANES - kk, wc), F32)], axis=0)

        @pl.when(i == 0)
        def _():
            dw_ref[...] = jnp.zeros_like(dw_ref)
            db_ref[...] = jnp.zeros_like(db_ref)

        dw_ref[...] += dwp
        db_ref[...] += jnp.sum(dyv, axis=0, keepdims=True)

    return pl.pallas_call(
        body, name=name, grid=(width // wc, lp // tm),
        in_specs=[pl.BlockSpec((tm, wc), lambda j, i: (i, offb + j)),
                  pl.BlockSpec((SUBLANES, wc), lambda j, i: (jnp.maximum(i * hb - 1, 0), offb + j)),
                  pl.BlockSpec((tm, wc), lambda j, i: (i, j)),
                  pl.BlockSpec((hrows, wc), lambda j, i: (jnp.minimum((i + 1) * (tm // hrows), lp // hrows - 1), j)),
                  pl.BlockSpec((kk, wc), lambda j, i: (0, woffb + j))],
        out_specs=[pl.BlockSpec((tm, wc), lambda j, i: (i, j)),
                   pl.BlockSpec((SUBLANES, wc), lambda j, i: (0, j)),
                   pl.BlockSpec((1, wc), lambda j, i: (0, j))],
        out_shape=[jax.ShapeDtypeStruct((lp, width), BF16),
                   jax.ShapeDtypeStruct((SUBLANES, width), F32),
                   jax.ShapeDtypeStruct((1, width), F32)],
        compiler_params=pltpu.CompilerParams(dimension_semantics=("parallel", "arbitrary")),
    )(x, x, dy, dy, w)


_FFN_K = 3
_FFN_WC = 1408


def _conv3_ext(x_ext, w_ref, b_ref):
    return (b_ref[...] + w_ref[2:3, :] * x_ext + w_ref[1:2, :] * pltpu.roll(x_ext, 1, 0)
            + w_ref[0:1, :] * pltpu.roll(x_ext, 2, 0))


def _ffn_convact_fwd(hp, cw, cb, name):
    lp = hp.shape[0]
    tm = _pick(lp, (320, 256, 128))
    wc = _FFN_WC
    nj = D_FF // wc
    hb = tm // SUBLANES

    def body(g_ref, gp_ref, u_ref, up_ref, wg_ref, wu_ref, bg_ref, bu_ref, a_ref, hg_ref, hu_ref):
        i = pl.program_id(1)

        def conv(x_ref, p_ref, w_ref, b_ref):
            x_ext = jnp.concatenate([jnp.where(i > 0, p_ref[...], 0.0), x_ref[...]], axis=0)
            return _conv3_ext(x_ext, w_ref, b_ref)[SUBLANES:, :]

        hg = conv(g_ref, gp_ref, wg_ref, bg_ref)
        hu = conv(u_ref, up_ref, wu_ref, bu_ref)
        a_ref[...] = (_gelu(hg) * hu).astype(a_ref.dtype)
        hg_ref[...] = hg.astype(hg_ref.dtype)
        hu_ref[...] = hu.astype(hu_ref.dtype)

    tile = lambda off: pl.BlockSpec((tm, wc), lambda j, i: (i, off + j))
    prev = lambda off: pl.BlockSpec((SUBLANES, wc), lambda j, i: (jnp.maximum(i * hb - 1, 0), off + j))
    vec = lambda rows, off: pl.BlockSpec((rows, wc), lambda j, i: (0, off + j))
    return pl.pallas_call(
        body, name=name, grid=(nj, lp // tm),
        in_specs=[tile(0), prev(0), tile(nj), prev(nj), vec(_FFN_K, 0), vec(_FFN_K, nj), vec(1, 0), vec(1, nj)],
        out_specs=[tile(0)] * 3,
        out_shape=[jax.ShapeDtypeStruct((lp, D_FF), BF16)] * 3,
        compiler_params=pltpu.CompilerParams(dimension_semantics=("parallel", "parallel")),
    )(hp, hp, hp, hp, cw, cw, cb, cb)


def _ffn_act_bwd(hg, hu, dact, name):
    lp = hg.shape[0]
    tm = _pick(lp, (320, 256, 128))

    def body(r, first, g_ref, u_ref, da_ref, dg_ref, du_ref):
        gl, dgl = _gelu_and_grad(g_ref[...].astype(F32))
        da = da_ref[...]
        dg_ref[...] = (da * u_ref[...].astype(F32) * dgl).astype(dg_ref.dtype)
        du_ref[...] = (da * gl).astype(du_ref.dtype)

    return _rowcall(name, body, lp, tm, rows=[(hg, D_FF, 0), (hu, D_FF, 0), (dact, D_FF, 0)],
                    outs=[(D_FF, BF16), (D_FF, BF16)])


def _lru_gates(x, wa_ref, wx_ref, ba, bx, lam):
    xb = x.astype(BF16)
    za, zx = [], []
    for n in range(LRU_BLOCKS):
        xs = xb[:, n * LRU_BS:(n + 1) * LRU_BS]
        za.append(_dot(xs, wa_ref[n]))
        zx.append(_dot(xs, wx_ref[n]))
    r = _sigmoid(jnp.concatenate(za, axis=1) + ba)
    ig = _sigmoid(jnp.concatenate(zx, axis=1) + bx)
    sp = _softplus(-lam)
    log_a = -LRU_C * r * sp
    a = jnp.exp(log_a)
    om = _neg_expm1(2.0 * log_a)
    mult = jnp.sqrt(om)
    return xb, r, ig, sp, a, om, mult


def _lru_fwd(proj, xrc, wa, wx, ba, bx, lam, name):
    lp, d = xrc.shape
    tm = BLOCK

    def body(r_idx, first, gate_ref, x_ref, wa_ref, wx_ref, ba_ref, bx_ref, lam_ref,
             y_ref, h_ref, carry):
        @pl.when(first)
        def _():
            carry[...] = jnp.zeros_like(carry)

        x = x_ref[...]
        _, _, ig, _, a, _, mult = _lru_gates(x, wa_ref, wx_ref, ba_ref[...], bx_ref[...], lam_ref[...])
        u = jnp.where(_real_rows(r_idx, tm), mult * ig * x, 0.0)
        acum, hloc = _scan_fwd(a, u, tm)
        h = hloc + acum * carry[0:1, :]
        h_ref[...] = h
        carry[0:1, :] = h[tm - 1:tm, :]
        y_ref[...] = (_gelu(gate_ref[...]) * h).astype(y_ref.dtype)

    return _rowcall(name, body, lp, tm, rows=[(proj, d, 0), (xrc, d, 0)],
                    vecs=[wa, wx, ba, bx, lam], outs=[(d, BF16), (d, F32)],
                    scratch=[pltpu.VMEM((SUBLANES, d), F32)], seq=True)


def _lru_bwd(proj, xrc, hl, dmix, wa, wx, ba, bx, lam, name):
    lp, d = xrc.shape
    tm = BLOCK

    def body(r_idx, first, gate_ref, x_ref, h_ref, dy_ref, hp_ref, wa_ref, wx_ref, ba_ref, bx_ref,
             lam_ref, dgate_ref, dx_ref, dwa_ref, dwx_ref, dba_ref, dbx_ref, dlam_ref, carry):
        @pl.when(first)
        def _():
            carry[...] = jnp.zeros_like(carry)
            dwa_ref[...] = jnp.zeros_like(dwa_ref)
            dwx_ref[...] = jnp.zeros_like(dwx_ref)
            dba_ref[...] = jnp.zeros_like(dba_ref)
            dbx_ref[...] = jnp.zeros_like(dbx_ref)
            dlam_ref[...] = jnp.zeros_like(dlam_ref)

        x = x_ref[...]
        lam = lam_ref[...]
        xb, r, ig, sp, a, om, mult = _lru_gates(x, wa_ref, wx_ref, ba_ref[...], bx_ref[...], lam)
        h = h_ref[...]
        dy = dy_ref[...]
        gl, dgl = _gelu_and_grad(gate_ref[...])
        dgate_ref[...] = (dy * h * dgl).astype(dgate_ref.dtype)
        row = _row_iota(tm)
        lastrow = row == tm - 1
        xg = dy * gl + jnp.where(lastrow, carry[0:1, :], 0.0)
        c = jnp.where(lastrow, 1.0, pltpu.roll(a, tm - 1, 0))
        _, g = _scan_rev(c, xg, tm)
        carry[0:1, :] = a[0:1, :] * g[0:1, :]
        hprev_in = jnp.where(r_idx > 0, hp_ref[SUBLANES - 1:SUBLANES, :], 0.0)
        hprev = jnp.where(row == 0, hprev_in, pltpu.roll(h, 1, 0))
        du = jnp.where(_real_rows(r_idx, tm), g, 0.0)
        da = g * hprev
        dmult = du * ig * x
        dig = du * mult * x
        dxv = du * mult * ig
        e2 = 1.0 - om
        dlog_a = da * a - dmult * e2 / mult
        dr = dlog_a * (-LRU_C) * sp
        dsp = jnp.sum(dlog_a * (-LRU_C) * r, axis=0, keepdims=True)
        dlam_ref[...] += -dsp * _sigmoid(-lam)
        dza = dr * r * (1.0 - r)
        dzx = dig * ig * (1.0 - ig)
        dba_ref[...] += jnp.sum(dza, axis=0, keepdims=True)
        dbx_ref[...] += jnp.sum(dzx, axis=0, keepdims=True)
        dzab = dza.astype(BF16)
        dzxb = dzx.astype(BF16)
        parts = []
        for n in range(LRU_BLOCKS):
            sl = slice(n * LRU_BS, (n + 1) * LRU_BS)
            dwa_ref[n] += _dot_tn(xb[:, sl], dzab[:, sl])
            dwx_ref[n] += _dot_tn(xb[:, sl], dzxb[:, sl])
            parts.append(_dot_nt(dzab[:, sl], wa_ref[n]) + _dot_nt(dzxb[:, sl], wx_ref[n]))
        dx_ref[...] = dxv + jnp.concatenate(parts, axis=1)

    return _rowcall(name, body, lp, tm,
                    rows=[(proj, d, 0), (xrc, d, 0), (hl, d, 0), (dmix, d, 0)],
                    prevs=[(hl, d, 0)], vecs=[wa, wx, ba, bx, lam],
                    outs=[(d, BF16), (d, F32)],
                    accs=[((LRU_BLOCKS, LRU_BS, LRU_BS), F32), ((LRU_BLOCKS, LRU_BS, LRU_BS), F32),
                          ((1, d), F32), ((1, d), F32), ((1, d), F32)],
                    scratch=[pltpu.VMEM((SUBLANES, d), F32)], reverse=True, seq=True)


_SLOPES = [2.0 ** (-8.0 * (h + 1) / N_Q_HEADS) for h in range(N_Q_HEADS)]
_QK_SCALE = HEAD_DIM ** -0.5
_QCOL = 2 * D_MODEL // D_MODEL
_KCOL = (3 * D_MODEL) // LANES
_VCOL = _KCOL + 1


def _attn_masks(n):
    start = pl.multiple_of(jnp.maximum(n - 1, 0) * BLOCK, BLOCK)
    qi = n * BLOCK + lax.broadcasted_iota(jnp.int32, (BLOCK, 2 * BLOCK), 0)
    kj = start + lax.broadcasted_iota(jnp.int32, (BLOCK, 2 * BLOCK), 1)
    dist = qi - kj
    ok = (kj >= BLOCK) & (dist >= 0) & (dist < WINDOW)
    dm = (n * BLOCK - PAD + lax.broadcasted_iota(jnp.int32, (BLOCK, N_META), 0)
          - lax.broadcasted_iota(jnp.int32, (BLOCK, N_META), 1))
    okm = dm >= 0
    return start, ok, dist.astype(F32), okm, jnp.minimum(dm, WINDOW).astype(F32)


def _group_rows(ref, g):
    return jnp.concatenate(
        [ref[:, (g * Q_PER_KV + hh) * HEAD_DIM:(g * Q_PER_KV + hh + 1) * HEAD_DIM] for hh in range(Q_PER_KV)],
        axis=0).astype(BF16)


def _attn_probs(qg, kg, kmg, sink_ref, g, ok, distf, okm, dmf):
    slope = jnp.stack([jnp.full((1, 1), _SLOPES[g * Q_PER_KV + hh], F32) for hh in range(Q_PER_KV)])
    sink = jnp.stack([sink_ref[0:1, g * Q_PER_KV + hh:g * Q_PER_KV + hh + 1] for hh in range(Q_PER_KV)])
    s = (_dot_nt(qg, kg) * _QK_SCALE).reshape(Q_PER_KV, BLOCK, 2 * BLOCK)
    sm = (_dot_nt(qg, kmg) * _QK_SCALE).reshape(Q_PER_KV, BLOCK, N_META)
    s = jnp.where(ok[None], s - slope * distf[None], NEG)
    sm = jnp.where(okm[None], sm - slope * dmf[None], NEG)
    mx = jnp.maximum(jnp.maximum(jnp.max(s, axis=-1, keepdims=True),
                                 jnp.max(sm, axis=-1, keepdims=True)), sink)
    p = jnp.exp(s - mx)
    pm = jnp.exp(sm - mx)
    ps = jnp.exp(sink - mx)
    inv = 1.0 / (jnp.sum(p, axis=-1, keepdims=True) + jnp.sum(pm, axis=-1, keepdims=True) + ps)
    return p * inv, pm * inv, ps * inv


def _attn_fwd(proj, sinks, name, plan=None):
    lp = proj.shape[0]
    nblk = lp // BLOCK
    p_in, p_shapes, p_out, p_scr = _plan_parts(plan)

    def body(*refs):
        q_ref, k_ref, v_ref, sink_ref = refs[:4]
        cins = refs[4:4 + len(p_in)]
        o_ref = refs[4 + len(p_in)]
        couts = refs[5 + len(p_in):5 + len(p_in) + len(p_out)]
        sems = refs[5 + len(p_in) + len(p_out):]
        n = pl.program_id(0)
        if plan is not None:
            @pl.when(n == 0)
            def _():
                plan.start(cins, couts, sems)

        start, ok, distf, okm, dmf = _attn_masks(n)
        kb = k_ref[pl.ds(start, 2 * BLOCK), :].astype(BF16)
        vb = v_ref[pl.ds(start, 2 * BLOCK), :].astype(BF16)
        km = k_ref[PAD:BLOCK, :].astype(BF16)
        vm = v_ref[PAD:BLOCK, :].astype(BF16)
        for g in range(N_KV_HEADS):
            gs = slice(g * HEAD_DIM, (g + 1) * HEAD_DIM)
            pn, pmn, _ = _attn_probs(_group_rows(q_ref, g), kb[:, gs], km[:, gs], sink_ref, g,
                                     ok, distf, okm, dmf)
            o = (_dot(pn.astype(BF16).reshape(Q_PER_KV * BLOCK, 2 * BLOCK), vb[:, gs])
                 + _dot(pmn.astype(BF16).reshape(Q_PER_KV * BLOCK, N_META), vm[:, gs]))
            for hh in range(Q_PER_KV):
                h = g * Q_PER_KV + hh
                o_ref[:, h * HEAD_DIM:(h + 1) * HEAD_DIM] = o[hh * BLOCK:(hh + 1) * BLOCK, :].astype(o_ref.dtype)
        if plan is not None:
            @pl.when(n == nblk - 1)
            def _():
                plan.wait(cins, couts, sems)

    res = pl.pallas_call(
        body, name=name, grid=(nblk,),
        in_specs=[pl.BlockSpec((BLOCK, D_MODEL), lambda n: (n, _QCOL)),
                  pl.BlockSpec((lp, LANES), lambda n: (0, _KCOL)),
                  pl.BlockSpec((lp, LANES), lambda n: (0, _VCOL)),
                  pl.BlockSpec(sinks.shape, lambda n: (0, 0))] + p_in,
        out_specs=[pl.BlockSpec((BLOCK, D_MODEL), lambda n: (n, 0))] + p_out,
        out_shape=[jax.ShapeDtypeStruct((lp, D_MODEL), BF16)] + p_shapes,
        scratch_shapes=p_scr,
        compiler_params=pltpu.CompilerParams(dimension_semantics=("arbitrary",)),
    )(proj, proj, proj, sinks, *(plan.ins if plan is not None else []))
    return res[0], res[1:]


def _attn_bwd(proj, sinks, dmix, name, plan=None):
    lp = proj.shape[0]
    nblk = lp // BLOCK

    p_in, p_shapes, p_out, p_scr = _plan_parts(plan)

    def body(*refs):
        q_ref, k_ref, v_ref, sink_ref, dy_ref = refs[:5]
        cins = refs[5:5 + len(p_in)]
        dq_ref, dk_ref, dv_ref, ds_ref = refs[5 + len(p_in):9 + len(p_in)]
        couts = refs[9 + len(p_in):9 + len(p_in) + len(p_out)]
        sems = refs[9 + len(p_in) + len(p_out):]
        n = pl.program_id(0)

        @pl.when(n == 0)
        def _():
            dk_ref[...] = jnp.zeros_like(dk_ref)
            dv_ref[...] = jnp.zeros_like(dv_ref)
            ds_ref[...] = jnp.zeros_like(ds_ref)
            if plan is not None:
                plan.start(cins, couts, sems)

        start, ok, distf, okm, dmf = _attn_masks(n)
        kb = k_ref[pl.ds(start, 2 * BLOCK), :].astype(BF16)
        vb = v_ref[pl.ds(start, 2 * BLOCK), :].astype(BF16)
        km = k_ref[PAD:BLOCK, :].astype(BF16)
        vm = v_ref[PAD:BLOCK, :].astype(BF16)
        lane16 = lax.broadcasted_iota(jnp.int32, (1, N_Q_HEADS), 1)
        dsink = jnp.zeros((1, N_Q_HEADS), F32)
        rows = Q_PER_KV * BLOCK
        for g in range(N_KV_HEADS):
            gs = slice(g * HEAD_DIM, (g + 1) * HEAD_DIM)
            qg = _group_rows(q_ref, g)
            dog = _group_rows(dy_ref, g)
            pn, pmn, psn = _attn_probs(qg, kb[:, gs], km[:, gs], sink_ref, g, ok, distf, okm, dmf)
            dp = _dot_nt(dog, vb[:, gs]).reshape(Q_PER_KV, BLOCK, 2 * BLOCK)
            dpm = _dot_nt(dog, vm[:, gs]).reshape(Q_PER_KV, BLOCK, N_META)
            delta = (jnp.sum(pn * dp, axis=-1, keepdims=True)
                     + jnp.sum(pmn * dpm, axis=-1, keepdims=True))
            dsb = (pn * (dp - delta)).astype(BF16).reshape(rows, 2 * BLOCK)
            dsm = (pmn * (dpm - delta)).astype(BF16).reshape(rows, N_META)
            dsk = jnp.sum(psn * delta, axis=1, keepdims=True)
            for hh in range(Q_PER_KV):
                dsink = dsink - jnp.where(lane16 == g * Q_PER_KV + hh, dsk[hh], 0.0)
            dq = (_dot(dsb, kb[:, gs]) + _dot(dsm, km[:, gs])) * _QK_SCALE
            for hh in range(Q_PER_KV):
                h = g * Q_PER_KV + hh
                dq_ref[:, h * HEAD_DIM:(h + 1) * HEAD_DIM] = dq[hh * BLOCK:(hh + 1) * BLOCK, :].astype(dq_ref.dtype)
            pnb = pn.astype(BF16).reshape(rows, 2 * BLOCK)
            pmnb = pmn.astype(BF16).reshape(rows, N_META)
            dk_ref[pl.ds(start, 2 * BLOCK), gs] += _dot_tn(dsb, qg) * _QK_SCALE
            dv_ref[pl.ds(start, 2 * BLOCK), gs] += _dot_tn(pnb, dog)
            dk_ref[PAD:BLOCK, gs] += _dot_tn(dsm, qg) * _QK_SCALE
            dv_ref[PAD:BLOCK, gs] += _dot_tn(pmnb, dog)
        ds_ref[...] += dsink
        if plan is not None:
            @pl.when(n == nblk - 1)
            def _():
                plan.wait(cins, couts, sems)

    res = pl.pallas_call(
        body, name=name, grid=(nblk,),
        in_specs=[pl.BlockSpec((BLOCK, D_MODEL), lambda n: (n, _QCOL)),
                  pl.BlockSpec((lp, LANES), lambda n: (0, _KCOL)),
                  pl.BlockSpec((lp, LANES), lambda n: (0, _VCOL)),
                  pl.BlockSpec(sinks.shape, lambda n: (0, 0)),
                  pl.BlockSpec((BLOCK, D_MODEL), lambda n: (n, 1))] + p_in,
        out_specs=[pl.BlockSpec((BLOCK, D_MODEL), lambda n: (n, 0)),
                   pl.BlockSpec((lp, LANES), lambda n: (0, 0)),
                   pl.BlockSpec((lp, LANES), lambda n: (0, 0)),
                   pl.BlockSpec((1, N_Q_HEADS), lambda n: (0, 0))] + p_out,
        out_shape=[jax.ShapeDtypeStruct((lp, D_MODEL), BF16),
                   jax.ShapeDtypeStruct((lp, LANES), F32),
                   jax.ShapeDtypeStruct((lp, LANES), F32),
                   jax.ShapeDtypeStruct((1, N_Q_HEADS), F32)] + p_shapes,
        scratch_shapes=p_scr,
        compiler_params=pltpu.CompilerParams(dimension_semantics=("arbitrary",)),
    )(proj, proj, proj, sinks, dmix, *(plan.ins if plan is not None else []))
    return res[:4], res[4:]


_ZW = D_SSM
_XBC_W = D_SSM + 2 * SSD_GROUPS * SSD_N
_DT_COL = (_ZW + _XBC_W) // LANES
EVEN_IN = 3 * D_MODEL + 2 * LANES
ODD_IN = _ZW + _XBC_W + SSD_HEADS
ODD_IN_PAD = _ZW + _XBC_W + LANES


def _ssm_prep_fwd(xc, proj, dt_bias, name):
    lp = xc.shape[0]
    tm = _pick(lp, (320, 256, 128))

    def body(r, first, xc_ref, dtr_ref, b_ref, act_ref, dt_ref):
        real = _real_rows(r, tm)
        act, _ = _silu_and_grad(xc_ref[...])
        act_ref[...] = jnp.where(real, act, 0.0)
        dt_ref[...] = jnp.where(real, _softplus(dtr_ref[...] + b_ref[...]), 0.0)

    return _rowcall(name, body, lp, tm, rows=[(xc, _XBC_W, 0), (proj, LANES, _DT_COL)],
                    vecs=[dt_bias], outs=[(_XBC_W, F32), (LANES, F32)])


def _ssm_prep_bwd(xc, proj, dt_bias, dxs, dxskip, db, dc, ddt, name):
    lp = xc.shape[0]
    tm = BLOCK

    def body(r, first, xc_ref, dtr_ref, dxs_ref, dsk_ref, db_ref, dc_ref, ddt_ref, b_ref,
             dxc_ref, ddtr_ref, dbias_ref):
        real = _real_rows(r, tm)
        _, ds = _silu_and_grad(xc_ref[...])
        dxc_ref[:, :D_SSM] = jnp.where(real, (dxs_ref[...] + dsk_ref[...]) * ds[:, :D_SSM], 0.0)
        dxc_ref[:, D_SSM:D_SSM + 1024] = jnp.where(real, db_ref[...] * ds[:, D_SSM:D_SSM + 1024], 0.0)
        dxc_ref[:, D_SSM + 1024:] = jnp.where(real, dc_ref[...] * ds[:, D_SSM + 1024:], 0.0)
        dd = jnp.where(real, ddt_ref[...] * _sigmoid(dtr_ref[...] + b_ref[...]), 0.0)
        ddtr_ref[...] = dd.astype(ddtr_ref.dtype)
        _acc_add(first, dbias_ref, jnp.sum(dd, axis=0, keepdims=True))

    return _rowcall(name, body, lp, tm,
                    rows=[(xc, _XBC_W, 0), (proj, LANES, _DT_COL), (dxs, D_SSM, 0), (dxskip, D_SSM, 0),
                          (db, 1024, 0), (dc, 1024, 0), (ddt, LANES, 0)],
                    vecs=[dt_bias], outs=[(_XBC_W, F32), (LANES, BF16)], accs=[((1, LANES), F32)])


def _ssd_common(dt, alog):
    a = -jnp.exp(alog)
    cs = _cumsum_rows(dt * a, BLOCK)
    cst = cs.T
    cl = cs[BLOCK - 1:BLOCK, :]
    tril = (lax.broadcasted_iota(jnp.int32, (BLOCK, BLOCK), 0)
            >= lax.broadcasted_iota(jnp.int32, (BLOCK, BLOCK), 1))
    return a, cs, cst, cl, jnp.exp(cs), jnp.exp(cl - cs), jnp.exp(cl), tril


def _head_cols(ecl, g):
    lane = lax.broadcasted_iota(jnp.int32, (1, SSD_HPG * SSD_P), 1)
    e = [ecl[:, SSD_HPG * g + hh:SSD_HPG * g + hh + 1] for hh in range(SSD_HPG)]
    return jnp.where(lane < SSD_P, e[0], jnp.where(lane < 2 * SSD_P, e[1],
                                                   jnp.where(lane < 3 * SSD_P, e[2], e[3])))


def _ssd_fwd(xbc, dt, alog, name, plan=None):
    lp = xbc.shape[0]
    nc = lp // BLOCK
    gw = SSD_HPG * SSD_P
    p_in, p_shapes, p_out, p_scr = _plan_parts(plan)

    def body(*refs):
        xs_ref, b_ref, c_ref, dt_ref, alog_ref = refs[:5]
        cins = refs[5:5 + len(p_in)]
        y_ref, so_ref = refs[5 + len(p_in):7 + len(p_in)]
        couts = refs[7 + len(p_in):7 + len(p_in) + len(p_out)]
        st, fx = refs[7 + len(p_in) + len(p_out):9 + len(p_in) + len(p_out)]
        sems = refs[9 + len(p_in) + len(p_out):]
        n = pl.program_id(0)

        @pl.when(n == 0)
        def _():
            st[...] = jnp.zeros_like(st)
            if plan is not None:
                plan.start(cins, couts, sems)

        dtv = dt_ref[...]
        _, cs, cst, cl, e, f, ecl, tril = _ssd_common(dtv, alog_ref[...])
        for g in range(SSD_GROUPS):
            bg = b_ref[:, g * SSD_N:(g + 1) * SSD_N].astype(BF16)
            cg = c_ref[:, g * SSD_N:(g + 1) * SSD_N].astype(BF16)
            gm = _dot_nt(cg, bg)
            stg = st[g]
            so_ref[0, g] = stg
            yoff = _dot(cg, stg.astype(BF16))
            heads = [SSD_HPG * g + hh for hh in range(SSD_HPG)]
            cols = lambda v: jnp.stack([v[:, h:h + 1] for h in heads])
            x4 = jnp.stack([xs_ref[:, h * SSD_P:(h + 1) * SSD_P] for h in heads])
            csr = jnp.stack([cst[h:h + 1, :] for h in heads])
            m = gm[None] * jnp.exp(jnp.where(tril[None], cols(cs) - csr, NEG))
            xdt = x4 * cols(dtv)
            yoff4 = jnp.stack([yoff[:, hh * SSD_P:(hh + 1) * SSD_P] for hh in range(SSD_HPG)])
            y4 = (jnp.einsum("hls,hsp->hlp", m.astype(BF16), xdt.astype(BF16), preferred_element_type=F32)
                  + cols(e) * yoff4)
            fx4 = cols(f) * xdt
            for hh, h in enumerate(heads):
                y_ref[:, h * SSD_P:(h + 1) * SSD_P] = y4[hh]
                fx[:, hh * SSD_P:(hh + 1) * SSD_P] = fx4[hh]
            st[g] = stg * _head_cols(ecl, g) + _dot_tn(bg, fx[...].astype(BF16))
        if plan is not None:
            @pl.when(n == nc - 1)
            def _():
                plan.wait(cins, couts, sems)

    res = pl.pallas_call(
        body, name=name, grid=(nc,),
        in_specs=[pl.BlockSpec((BLOCK, D_SSM), lambda n: (n, 0)),
                  pl.BlockSpec((BLOCK, 1024), lambda n: (n, 2)),
                  pl.BlockSpec((BLOCK, 1024), lambda n: (n, 3)),
                  pl.BlockSpec((BLOCK, LANES), lambda n: (n, 0)),
                  pl.BlockSpec((1, LANES), lambda n: (0, 0))] + p_in,
        out_specs=[pl.BlockSpec((BLOCK, D_SSM), lambda n: (n, 0)),
                   pl.BlockSpec((1, SSD_GROUPS, SSD_N, gw), lambda n: (n, 0, 0, 0))] + p_out,
        out_shape=[jax.ShapeDtypeStruct((lp, D_SSM), F32),
                   jax.ShapeDtypeStruct((nc, SSD_GROUPS, SSD_N, gw), F32)] + p_shapes,
        scratch_shapes=[pltpu.VMEM((SSD_GROUPS, SSD_N, gw), F32), pltpu.VMEM((BLOCK, gw), F32)] + p_scr,
        compiler_params=pltpu.CompilerParams(dimension_semantics=("arbitrary",)),
    )(xbc, xbc, xbc, dt, alog, *(plan.ins if plan is not None else []))
    return res[:2], res[2:]


def _ssd_bwd(xbc, dt, alog, states, dy, name, plan=None):
    lp = xbc.shape[0]
    nc = lp // BLOCK
    gw = SSD_HPG * SSD_P
    p_in, p_shapes, p_out, p_scr = _plan_parts(plan)

    def body(*refs):
        xs_ref, b_ref, c_ref, dt_ref, alog_ref, dy_ref, st_ref = refs[:7]
        cins = refs[7:7 + len(p_in)]
        dxs_ref, db_ref, dc_ref, ddt_ref, dalog_ref = refs[7 + len(p_in):12 + len(p_in)]
        couts = refs[12 + len(p_in):12 + len(p_in) + len(p_out)]
        dst, edy, fx = refs[12 + len(p_in) + len(p_out):15 + len(p_in) + len(p_out)]
        sems = refs[15 + len(p_in) + len(p_out):]
        i = pl.program_id(0)

        @pl.when(i == 0)
        def _():
            dst[...] = jnp.zeros_like(dst)
            dalog_ref[...] = jnp.zeros_like(dalog_ref)
            if plan is not None:
                plan.start(cins, couts, sems)

        dtv = dt_ref[...]
        a, cs, cst, cl, e, f, ecl, tril = _ssd_common(dtv, alog_ref[...])
        lane = lax.broadcasted_iota(jnp.int32, (1, LANES), 1)
        sub = _row_iota(BLOCK)
        triu = (lax.broadcasted_iota(jnp.int32, (BLOCK, BLOCK), 1)
                >= lax.broadcasted_iota(jnp.int32, (BLOCK, BLOCK), 0))
        dcs = jnp.zeros((BLOCK, LANES), F32)
        dcst = jnp.zeros((LANES, BLOCK), F32)
        dcl = jnp.zeros((1, LANES), F32)
        ddtx = jnp.zeros((BLOCK, LANES), F32)
        for g in range(SSD_GROUPS):
            bg = b_ref[:, g * SSD_N:(g + 1) * SSD_N].astype(BF16)
            cg = c_ref[:, g * SSD_N:(g + 1) * SSD_N].astype(BF16)
            gm = _dot_nt(cg, bg)
            stg = st_ref[0, g]
            stb = stg.astype(BF16)
            dso = dst[g]
            dsob = dso.astype(BF16)
            yraw = _dot(cg, stb)
            dfx = _dot(bg, dsob)
            prodsum = jnp.sum(dso * stg, axis=0, keepdims=True)
            heads = [SSD_HPG * g + hh for hh in range(SSD_HPG)]
            cols = lambda v: jnp.stack([v[:, h:h + 1] for h in heads])
            parts = lambda v: jnp.stack([v[:, hh * SSD_P:(hh + 1) * SSD_P] for hh in range(SSD_HPG)])
            x4 = jnp.stack([xs_ref[:, h * SSD_P:(h + 1) * SSD_P] for h in heads])
            dy4 = jnp.stack([dy_ref[:, h * SSD_P:(h + 1) * SSD_P] for h in heads])
            csc, dtc, ec, fc = cols(cs), cols(dtv), cols(e), cols(f)
            csr = jnp.stack([cst[h:h + 1, :] for h in heads])
            seg = csc - csr
            lam = jnp.exp(jnp.where(tril[None], seg, NEG))
            lamt = jnp.exp(jnp.where(triu[None], -seg, NEG))
            m = gm[None] * lam
            mt = _dot_nt(bg, cg)[None] * lamt
            xdt = x4 * dtc
            dyb = dy4.astype(BF16)
            dm = jnp.einsum("hlp,hsp->hls", dyb, xdt.astype(BF16), preferred_element_type=F32)
            dfx4 = parts(dfx)
            dxdt = jnp.einsum("hsl,hlp->hsp", mt.astype(BF16), dyb, preferred_element_type=F32) + fc * dfx4
            w = dm * m
            dgm = jnp.sum(dm * lam, axis=0)
            dff = jnp.sum(dfx4 * xdt, axis=2, keepdims=True) * fc
            colv = (jnp.sum(w, axis=2, keepdims=True)
                    + jnp.sum(dy4 * parts(yraw), axis=2, keepdims=True) * ec - dff)
            roww = jnp.sum(w, axis=1, keepdims=True)
            ddtc = jnp.sum(dxdt * x4, axis=2, keepdims=True)
            dffs = jnp.sum(dff, axis=1, keepdims=True)
            dxs4 = dxdt * dtc
            edy4 = ec * dy4
            fx4 = fc * xdt
            for hh, h in enumerate(heads):
                ls = slice(hh * SSD_P, (hh + 1) * SSD_P)
                onl = (lane == h).astype(F32)
                dcs = dcs + colv[hh] * onl
                dcst = dcst - (sub == h).astype(F32) * roww[hh]
                dcl = dcl + (dffs[hh] + ecl[:, h:h + 1] * jnp.sum(prodsum[:, ls], axis=1, keepdims=True)) * onl
                ddtx = ddtx + ddtc[hh] * onl
                dxs_ref[:, h * SSD_P:(h + 1) * SSD_P] = dxs4[hh]
                edy[:, ls] = edy4[hh]
                fx[:, ls] = fx4[hh]
            edyb = edy[...].astype(BF16)
            fxb = fx[...].astype(BF16)
            dgb = dgm.astype(BF16)
            dc_ref[:, g * SSD_N:(g + 1) * SSD_N] = _dot_nt(edyb, stb) + _dot(dgb, bg)
            db_ref[:, g * SSD_N:(g + 1) * SSD_N] = _dot_nt(fxb, dsob) + _dot_tn(dgb, cg)
            dst[g] = dso * _head_cols(ecl, g) + _dot_tn(cg, edyb)
        dcs = dcs + dcst.T + jnp.where(sub == BLOCK - 1, dcl, 0.0)
        dda = _rev_cumsum_rows(dcs, BLOCK)
        ddt_ref[...] = ddtx + dda * a
        dalog_ref[...] += jnp.sum(dda * dtv, axis=0, keepdims=True) * a
        if plan is not None:
            @pl.when(i == nc - 1)
            def _():
                plan.wait(cins, couts, sems)

    rev = lambda i: nc - 1 - i
    res = pl.pallas_call(
        body, name=name, grid=(nc,),
        in_specs=[pl.BlockSpec((BLOCK, D_SSM), lambda i: (rev(i), 0)),
                  pl.BlockSpec((BLOCK, 1024), lambda i: (rev(i), 2)),
                  pl.BlockSpec((BLOCK, 1024), lambda i: (rev(i), 3)),
                  pl.BlockSpec((BLOCK, LANES), lambda i: (rev(i), 0)),
                  pl.BlockSpec((1, LANES), lambda i: (0, 0)),
                  pl.BlockSpec((BLOCK, D_SSM), lambda i: (rev(i), 0)),
                  pl.BlockSpec((1, SSD_GROUPS, SSD_N, gw), lambda i: (rev(i), 0, 0, 0))] + p_in,
        out_specs=[pl.BlockSpec((BLOCK, D_SSM), lambda i: (rev(i), 0)),
                   pl.BlockSpec((BLOCK, 1024), lambda i: (rev(i), 0)),
                   pl.BlockSpec((BLOCK, 1024), lambda i: (rev(i), 0)),
                   pl.BlockSpec((BLOCK, LANES), lambda i: (rev(i), 0)),
                   pl.BlockSpec((1, LANES), lambda i: (0, 0))] + p_out,
        out_shape=[jax.ShapeDtypeStruct((lp, D_SSM), F32),
                   jax.ShapeDtypeStruct((lp, 1024), F32),
                   jax.ShapeDtypeStruct((lp, 1024), F32),
                   jax.ShapeDtypeStruct((lp, LANES), F32),
                   jax.ShapeDtypeStruct((1, LANES), F32)] + p_shapes,
        scratch_shapes=[pltpu.VMEM((SSD_GROUPS, SSD_N, gw), F32),
                        pltpu.VMEM((BLOCK, gw), F32), pltpu.VMEM((BLOCK, gw), F32)] + p_scr,
        compiler_params=pltpu.CompilerParams(dimension_semantics=("arbitrary",)),
    )(xbc, xbc, xbc, dt, alog, dy, states, *(plan.ins if plan is not None else []))
    return res[:5], res[5:]


_GN_GROUPS = 8
_GN_W = D_SSM // _GN_GROUPS


def _ssm_gate_fwd(yssd, xbc, proj, dskip, gnorm, name):
    lp = yssd.shape[0]
    tm = _pick(lp, (320, 256, 128))

    def body(r, first, y_ref, x_ref, z_ref, d_ref, g_ref, o_ref):
        sz, _ = _silu_and_grad(z_ref[...])
        y2 = (y_ref[...] + d_ref[...] * x_ref[...]) * sz
        for k in range(_GN_GROUPS):
            sl = slice(k * _GN_W, (k + 1) * _GN_W)
            yk = y2[:, sl]
            rs = lax.rsqrt(jnp.mean(yk * yk, axis=-1, keepdims=True) + EPS)
            o_ref[:, sl] = (yk * rs * g_ref[:, sl]).astype(o_ref.dtype)

    return _rowcall(name, body, lp, tm, rows=[(yssd, D_SSM, 0), (xbc, D_SSM, 0), (proj, D_SSM, 0)],
                    vecs=[dskip, gnorm], outs=[(D_SSM, BF16)])[0]


def _ssm_gate_bwd(yssd, xbc, proj, dskip, gnorm, dyn, name):
    lp = yssd.shape[0]
    tm = BLOCK

    def body(r, first, y_ref, x_ref, z_ref, dyn_ref, d_ref, g_ref,
             dy_ref, dx_ref, dz_ref, dd_ref, dg_ref):
        z = z_ref[...]
        sz, dsz = _silu_and_grad(z)
        xs = x_ref[...]
        y1 = y_ref[...] + d_ref[...] * xs
        y2 = y1 * sz
        dyn = dyn_ref[...]
        for k in range(_GN_GROUPS):
            sl = slice(k * _GN_W, (k + 1) * _GN_W)
            dx, dgt = _rms_bwd(y2[:, sl], g_ref[:, sl], dyn[:, sl])
            dy1 = dx * sz[:, sl]
            dy_ref[:, sl] = dy1
            dx_ref[:, sl] = dy1 * d_ref[:, sl]
            dz_ref[:, sl] = (dx * y1[:, sl] * dsz[:, sl]).astype(dz_ref.dtype)

            @pl.when(first)
            def _():
                dd_ref[:, sl] = jnp.zeros((1, _GN_W), F32)
                dg_ref[:, sl] = jnp.zeros((1, _GN_W), F32)

            dd_ref[:, sl] += jnp.sum(dy1 * xs[:, sl], axis=0, keepdims=True)
            dg_ref[:, sl] += jnp.sum(dgt, axis=0, keepdims=True)

    return _rowcall(name, body, lp, tm,
                    rows=[(yssd, D_SSM, 0), (xbc, D_SSM, 0), (proj, D_SSM, 0), (dyn, D_SSM, 0)],
                    vecs=[dskip, gnorm], outs=[(D_SSM, F32), (D_SSM, F32), (D_SSM, BF16)],
                    accs=[((1, D_SSM), F32), ((1, D_SSM), F32)])


def _adamw(w, g, m, v, name):
    r, c = w.shape
    tm = r if r <= 512 else _pick(r, (512, 352, 256, 128, 64, 32, 16, 8))
    c1 = 1.0 / (1.0 - ADAM_B1 ** ADAM_STEP)
    c2 = 1.0 / (1.0 - ADAM_B2 ** ADAM_STEP)

    def body(w_ref, g_ref, m_ref, v_ref, d_ref, nm_ref, nv_ref):
        gv = g_ref[...]
        nm = ADAM_B1 * m_ref[...] + (1.0 - ADAM_B1) * gv
        nv = ADAM_B2 * v_ref[...] + (1.0 - ADAM_B2) * (gv * gv)
        nm_ref[...] = nm
        nv_ref[...] = nv
        d_ref[...] = -ADAM_LR * ((nm * c1) / (jnp.sqrt(nv * c2) + ADAM_EPS) + ADAM_WD * w_ref[...])

    spec = pl.BlockSpec((tm, c), lambda i: (i, 0))
    return pl.pallas_call(
        body, name=name, grid=(r // tm,), in_specs=[spec] * 4, out_specs=[spec] * 3,
        out_shape=[jax.ShapeDtypeStruct((r, c), F32)] * 3,
        compiler_params=pltpu.CompilerParams(dimension_semantics=("parallel",)),
    )(w, g, m, v)


def _place():
    return lax.axis_index("x"), lax.axis_index("y"), lax.axis_index("c")


def _other_chips(x, y):
    return [(1 - x, y), (x, 1 - y), (1 - x, 1 - y)]


_ANY = pl.BlockSpec(memory_space=pl.ANY)


class _Plan:
    def __init__(self, ins, out_shapes, n_remote, n_local, issue):
        self.ins = list(ins)
        self.out_shapes = list(out_shapes)
        self.issue = issue
        self.scratch = [pltpu.SemaphoreType.DMA((max(n_remote, 1),)),
                        pltpu.SemaphoreType.DMA((max(n_remote, 1),)),
                        pltpu.SemaphoreType.DMA((max(n_local, 1),))]

    def start(self, ins, outs, sems):
        sends, _, locs = self.issue(ins, outs, *sems)
        for cp in locs + sends:
            cp.start()

    def wait(self, ins, outs, sems):
        sends, recvs, locs = self.issue(ins, outs, *sems)
        for make in recvs:
            make().wait_recv()
        for cp in sends:
            cp.wait_send()
        for cp in locs:
            cp.wait()


def _plan_parts(plan):
    if plan is None:
        return [], [], [], []
    return ([_ANY] * len(plan.ins), plan.out_shapes, [_ANY] * len(plan.out_shapes), plan.scratch)


def _run_plan(plan, name):
    n_in, n_out = len(plan.ins), len(plan.out_shapes)

    def body(*refs):
        ins, outs, sems = refs[:n_in], refs[n_in:n_in + n_out], refs[n_in + n_out:]
        plan.start(ins, outs, sems)
        plan.wait(ins, outs, sems)

    return pl.pallas_call(
        body, name=name, in_specs=[_ANY] * n_in, out_specs=[_ANY] * n_out,
        out_shape=plan.out_shapes, scratch_shapes=plan.scratch,
    )(*plan.ins)


def _gather_plan(shards):
    n = len(shards)

    def issue(ins, outs, send_sems, recv_sems, local_sems):
        x, y, c = _place()
        me = 2 * x + y
        sends, recvs, locs = [], [], []
        for p in range(n):
            locs.append(pltpu.make_async_copy(ins[p], outs[p].at[me], local_sems.at[p]))
            for k, (px, py) in enumerate(_other_chips(x, y)):
                sems = dict(send_sem=send_sems.at[3 * p + k], recv_sem=recv_sems.at[3 * p + k],
                            device_id=(px, py, c), device_id_type=MESH)
                sends.append(pltpu.make_async_remote_copy(src_ref=ins[p], dst_ref=outs[p].at[me], **sems))
                recvs.append(functools.partial(pltpu.make_async_remote_copy, src_ref=ins[p],
                                               dst_ref=outs[p].at[2 * px + py], **sems))
        return sends, recvs, locs

    return _Plan(shards, [jax.ShapeDtypeStruct((N_CHIPS,) + s.shape, s.dtype) for s in shards], 3 * n, n, issue)


_REL7 = [(fx, fy, fc) for fx in (0, 1) for fy in (0, 1) for fc in (0, 1)][1:]


def _scatter8_plan(gs):
    n = len(gs)

    def issue(ins, outs, send_sems, recv_sems, local_sems):
        x, y, c = _place()
        sends = []
        for p in range(n):
            hr = gs[p].shape[1] // 2
            for k, (fx, fy, fc) in enumerate(_REL7):
                tx, ty, tc = x ^ fx, y ^ fy, c ^ fc
                src = ins[p].at[2 * tx + ty, pl.ds(pl.multiple_of(tc * hr, SUBLANES), hr), :]
                sends.append(pltpu.make_async_remote_copy(
                    src_ref=src, dst_ref=outs[p].at[k],
                    send_sem=send_sems.at[7 * p + k], recv_sem=recv_sems.at[7 * p + k],
                    device_id=(tx, ty, tc), device_id_type=MESH))
        return sends, [functools.partial(lambda cp: cp, cp) for cp in sends], []

    shapes = [jax.ShapeDtypeStruct((7, g.shape[1] // 2, g.shape[2]), g.dtype) for g in gs]
    return _Plan(gs, shapes, 7 * n, 0, issue)


def _sibling_plan(ts):
    n = len(ts)

    def issue(ins, outs, send_sems, recv_sems, local_sems):
        x, y, c = _place()
        sends = [pltpu.make_async_remote_copy(
            src_ref=ins[p], dst_ref=outs[p], send_sem=send_sems.at[p], recv_sem=recv_sems.at[p],
            device_id=(x, y, 1 - c), device_id_type=MESH) for p in range(n)]
        return sends, [functools.partial(lambda cp: cp, cp) for cp in sends], []

    return _Plan(ts, [jax.ShapeDtypeStruct(t.shape, t.dtype) for t in ts], n, 0, issue)


def _add8(g, recv, chip, core, name):
    s, r, n = g.shape
    hr = r // 2
    th = hr // 2 if (hr // 2) % SUBLANES == 0 else hr
    nt = hr // th

    def body(chip_ref, core_ref, g_ref, r_ref, o_ref):
        acc = g_ref[0].astype(F32)
        for k in range(7):
            acc = acc + r_ref[k].astype(F32)
        o_ref[...] = acc

    return pl.pallas_call(
        body, name=name,
        grid_spec=pltpu.PrefetchScalarGridSpec(
            num_scalar_prefetch=2, grid=(nt,),
            in_specs=[pl.BlockSpec((1, th, n), lambda i, ch, co: (ch[0], co[0] * nt + i, 0)),
                      pl.BlockSpec((7, th, n), lambda i, ch, co: (0, i, 0))],
            out_specs=pl.BlockSpec((th, n), lambda i, ch, co: (i, 0))),
        out_shape=jax.ShapeDtypeStruct((hr, n), F32),
        compiler_params=pltpu.CompilerParams(dimension_semantics=("parallel",)),
    )(chip, core, g, recv)


def _adamw_halves(w, own, other, m, v, core, name):
    r, n = w.shape
    hr = r // 2
    th = hr // 2 if (hr // 2) % SUBLANES == 0 else hr
    tph = hr // th
    c1 = 1.0 / (1.0 - ADAM_B1 ** ADAM_STEP)
    c2 = 1.0 / (1.0 - ADAM_B2 ** ADAM_STEP)

    def body(core_ref, w_ref, a_ref, b_ref, m_ref, v_ref, g_ref, d_ref, nm_ref, nv_ref):
        half = pl.program_id(0) // tph
        gv = jnp.where(half == core_ref[0], a_ref[...], b_ref[...])
        nm = ADAM_B1 * m_ref[...] + (1.0 - ADAM_B1) * gv
        nv = ADAM_B2 * v_ref[...] + (1.0 - ADAM_B2) * (gv * gv)
        g_ref[...] = gv
        nm_ref[...] = nm
        nv_ref[...] = nv
        d_ref[...] = -ADAM_LR * ((nm * c1) / (jnp.sqrt(nv * c2) + ADAM_EPS) + ADAM_WD * w_ref[...])

    full = pl.BlockSpec((th, n), lambda i, co: (i, 0))
    part = pl.BlockSpec((th, n), lambda i, co: (i % tph, 0))
    return pl.pallas_call(
        body, name=name,
        grid_spec=pltpu.PrefetchScalarGridSpec(
            num_scalar_prefetch=1, grid=(2 * tph,),
            in_specs=[full, part, part, full, full], out_specs=[full] * 4),
        out_shape=[jax.ShapeDtypeStruct((r, n), F32)] * 4,
        compiler_params=pltpu.CompilerParams(dimension_semantics=("parallel",)),
    )(core, w, own, other, m, v)


def _allreduce_small(pack, name):
    r, l = pack.shape

    def body(p_ref, o_ref, land, send_sems, recv_sems):
        x, y, c = _place()
        me = 4 * x + 2 * y + c
        land[me] = p_ref[...]
        rel = [(fx, fy, fc) for fx in (0, 1) for fy in (0, 1) for fc in (0, 1)][1:]
        sends = []
        for k, (fx, fy, fc) in enumerate(rel):
            peer = (x ^ fx, y ^ fy, c ^ fc)
            cp = pltpu.make_async_remote_copy(
                src_ref=p_ref, dst_ref=land.at[me], send_sem=send_sems.at[k], recv_sem=recv_sems.at[k],
                device_id=peer, device_id_type=MESH)
            cp.start()
            sends.append(cp)
        for k, (fx, fy, fc) in enumerate(rel):
            src = 4 * (x ^ fx) + 2 * (y ^ fy) + (c ^ fc)
            pltpu.make_async_remote_copy(
                src_ref=p_ref, dst_ref=land.at[src], send_sem=send_sems.at[k], recv_sem=recv_sems.at[k],
                device_id=(x ^ fx, y ^ fy, c ^ fc), device_id_type=MESH).wait_recv()
        for cp in sends:
            cp.wait_send()
        acc = land[0]
        for d in range(1, N_DEV):
            acc = acc + land[d]
        o_ref[...] = acc

    vm = pl.BlockSpec(memory_space=pltpu.VMEM)
    return pl.pallas_call(
        body, name=name, in_specs=[vm], out_specs=vm,
        out_shape=jax.ShapeDtypeStruct((r, l), F32),
        scratch_shapes=[pltpu.VMEM((N_DEV, r, l), F32),
                        pltpu.SemaphoreType.DMA((N_DEV - 1,)), pltpu.SemaphoreType.DMA((N_DEV - 1,))],
    )(pack)


def _flat_rows(a, mult=SUBLANES * LANES):
    f = a.reshape(-1)
    padn = (-f.shape[0]) % mult
    if padn:
        f = jnp.concatenate([f, jnp.zeros((padn,), f.dtype)])
    return f


def _pack(arrs, mult=SUBLANES * LANES, total_mult=None):
    flat = [_flat_rows(a, mult) for a in arrs]
    sizes = [f.shape[0] for f in flat]
    if total_mult is not None:
        padn = (-sum(sizes)) % total_mult
        if padn:
            flat.append(jnp.zeros((padn,), flat[0].dtype))
    return jnp.concatenate(flat).reshape(-1, LANES), sizes


def _unpack(pack, shapes, sizes, lead=()):
    flat = pack.reshape(lead + (-1,))
    out, off = [], 0
    for shp, sz in zip(shapes, sizes):
        n = math.prod(shp)
        out.append(flat[..., off:off + n].reshape(lead + tuple(shp)))
        off += sz
    return out


def _cols_from_shards(g):
    s, k, n = g.shape
    return jnp.transpose(g, (1, 0, 2)).reshape(k, s * n)


def _cols_to_shards(w, s=N_CHIPS):
    k, n = w.shape
    return jnp.transpose(w.reshape(k, s, n // s), (1, 0, 2))


def _ffn_fwd(h, pre, post, w_up, cw, cb, w_down, tag):
    u = _rmsnorm_fwd(h, pre, f"{tag}_prenorm")
    hp = _mm_nn_sh(u, w_up, 2 * D_FF, f"{tag}_up")
    act, hg, hu = _ffn_convact_fwd(hp, cw, cb, f"{tag}_convact")
    o = _mm_nn(act, w_down, F32, f"{tag}_down")
    hn = _postnorm_res_fwd(h, o, post, f"{tag}_postnorm")
    return hn, (h, u, hp, hg, hu, act, o)


def _ffn_bwd(dh, saved, pre, post, w_up, cw, w_down, tag):
    h, u, hp, hg, hu, act, o = saved
    do, dpost = _postnorm_bwd(o, post, dh, f"{tag}_postnorm_bwd")
    dact = _mm_nt(do, w_down, f"{tag}_down_dx")
    dw_down = _mm_tn(act, do, f"{tag}_down_dw")
    dhg, dhu = _ffn_act_bwd(hg, hu, dact, f"{tag}_act_bwd")
    dxg, dwg, dbg = _conv_bwd(hp, 0, D_FF, dhg, cw, f"{tag}_conv_bwd_gate")
    dxu, dwu, dbu = _conv_bwd(hp, D_FF, D_FF, dhu, cw, f"{tag}_conv_bwd_up", w_col_off=D_FF)
    dhp = (dxg, dxu)
    dcw = jnp.concatenate([dwg, dwu], axis=1)
    dcb = jnp.concatenate([dbg, dbu], axis=1)
    du = _mm_nt_sh(dhp, w_up, f"{tag}_up_dx")
    dw_up = _mm_tn_sh(u, dhp, w_up.shape[2], f"{tag}_up_dw")
    dhn, dpre = _prenorm_bwd(h, pre, du, dh, f"{tag}_prenorm_bwd")
    return dhn, dict(pre=dpre, post=dpost, w_up=dw_up, conv_w=dcw[:3], conv_b=dcb, w_down=dw_down)


class _Exchange:
    GATHER_IN_ATTN = ("l0_w_out", "l0_ffn_w_up", "l0_ffn_w_down", "l1_w_in")
    GATHER_IN_SSD = ("l1_w_out", "l1_ffn_w_up", "l1_ffn_w_down")
    AFTER_L1_OUT = ("l1_ffn_w_up", "l1_ffn_w_down", "l1_w_out")
    AFTER_L0_OUT = ("l1_w_in", "l0_ffn_w_up", "l0_ffn_w_down", "l0_w_out")
    LAST = ("l0_w_in",)

    def __init__(self, late_shards):
        self.late = dict(late_shards)
        self.slabs = {}
        self.recv = {}

    def gather_plan(self, names):
        return _gather_plan([self.late[n] for n in names])

    def gathered(self, names, outs):
        return {n: (g if n in _BIG_COL else g.reshape(-1, g.shape[-1])) for n, g in zip(names, outs)}

    def scatter_plan(self, grads, names):
        for n in names:
            g = grads[n]
            self.slabs[n] = g if n in _BIG_COL else g.reshape(N_CHIPS, -1, g.shape[-1])
        return _scatter8_plan([self.slabs[n] for n in names])

    def scattered(self, names, outs):
        self.recv.update(zip(names, outs))


def _local_step(x, tgt, meta, P, ex=None):
    seq, d = x.shape
    lp = seq + BLOCK
    h0 = jnp.concatenate([jnp.zeros((PAD, d), F32), meta, x], axis=0)
    tgt_p = jnp.concatenate([jnp.zeros((BLOCK, d), F32), tgt], axis=0)

    u0 = _rmsnorm_fwd(h0, P["l0_mix_pre_norm"], "l0_mix_prenorm")
    proj0 = _mm_nn_sh(u0, P["l0_w_in"], EVEN_IN, "l0_in")
    xrc = _conv_fwd(proj0, D_MODEL, D_MODEL, P["l0_lru_conv_w"], P["l0_lru_conv_b"], "l0_lru_conv")
    lru_args = (P["l0_lru_w_a"], P["l0_lru_w_x"], P["l0_lru_b_a"], P["l0_lru_b_x"], P["l0_lru_lambda"])
    ya, hl = _lru_fwd(proj0, xrc, *lru_args, "l0_lru")
    yb, outs = _attn_fwd(proj0, P["l0_attn_sinks"], "l0_attn",
                         ex.gather_plan(ex.GATHER_IN_ATTN) if ex else None)
    if ex:
        P = {**P, **ex.gathered(ex.GATHER_IN_ATTN, outs)}
    mix0 = jnp.concatenate([ya, yb], axis=1)
    o0 = _mm_nn(mix0, P["l0_w_out"], F32, "l0_out")
    h1 = _postnorm_res_fwd(h0, o0, P["l0_mix_post_norm"], "l0_mix_postnorm")
    h2, ffn0 = _ffn_fwd(h1, P["l0_ffn_pre_norm"], P["l0_ffn_post_norm"], P["l0_ffn_w_up"],
                        P["l0_ffn_conv_w"], P["l0_ffn_conv_b"], P["l0_ffn_w_down"], "l0_ffn")
    u2 = _rmsnorm_fwd(h2, P["l1_mix_pre_norm"], "l1_mix_prenorm")
    proj1 = _mm_nn_sh(u2, P["l1_w_in"], ODD_IN_PAD, "l1_in")
    xc1 = _conv_fwd(proj1, _ZW, _XBC_W, P["l1_ssm_conv_w"], P["l1_ssm_conv_b"], "l1_ssm_conv")
    xbc, dt = _ssm_prep_fwd(xc1, proj1, P["l1_dt_bias"], "l1_ssm_prep")
    (yssd, states), outs = _ssd_fwd(xbc, dt, P["l1_a_log"], "l1_ssd",
                                    ex.gather_plan(ex.GATHER_IN_SSD) if ex else None)
    if ex:
        P = {**P, **ex.gathered(ex.GATHER_IN_SSD, outs)}
    yn = _ssm_gate_fwd(yssd, xbc, proj1, P["l1_d_skip"], P["l1_gate_norm"], "l1_ssm_gate")
    o1 = _mm_nn(yn, P["l1_w_out"], F32, "l1_out")
    h3 = _postnorm_res_fwd(h2, o1, P["l1_mix_post_norm"], "l1_mix_postnorm")
    h4, ffn1 = _ffn_fwd(h3, P["l1_ffn_pre_norm"], P["l1_ffn_post_norm"], P["l1_ffn_w_up"],
                        P["l1_ffn_conv_w"], P["l1_ffn_conv_b"], P["l1_ffn_w_down"], "l1_ffn")
    dh4, loss_cols = _loss_fwd_bwd(h4, tgt_p, "loss")

    G = {}
    dh3, g = _ffn_bwd(dh4, ffn1, P["l1_ffn_pre_norm"], P["l1_ffn_post_norm"], P["l1_ffn_w_up"],
                      P["l1_ffn_conv_w"], P["l1_ffn_w_down"], "l1_ffn")
    for k, v in g.items():
        G["l1_ffn_" + (k + "_norm" if k in ("pre", "post") else k)] = v
    do1, G["l1_mix_post_norm"] = _postnorm_bwd(o1, P["l1_mix_post_norm"], dh3, "l1_mix_postnorm_bwd")
    dyn = _mm_nt(do1, P["l1_w_out"], "l1_out_dx")
    G["l1_w_out"] = _mm_tn(yn, do1, "l1_out_dw")
    dyssd, dxskip, dz, dd_cols, G["l1_gate_norm"] = _ssm_gate_bwd(
        yssd, xbc, proj1, P["l1_d_skip"], P["l1_gate_norm"], dyn, "l1_ssm_gate_bwd")
    G["l1_d_skip"] = dd_cols.reshape(SSD_HEADS, SSD_P).sum(axis=1)
    (dxs, dbm, dcm, ddt, dalog), outs = _ssd_bwd(
        xbc, dt, P["l1_a_log"], states, dyssd, "l1_ssd_bwd",
        ex.scatter_plan(G, ex.AFTER_L1_OUT) if ex else None)
    if ex:
        ex.scattered(ex.AFTER_L1_OUT, outs)
    G["l1_a_log"] = dalog[0, :SSD_HEADS]
    dxc, ddtr, dbias = _ssm_prep_bwd(xc1, proj1, P["l1_dt_bias"], dxs, dxskip, dbm, dcm, ddt,
                                     "l1_ssm_prep_bwd")
    G["l1_dt_bias"] = dbias[0, :SSD_HEADS]
    dxbc, dcw, dcb = _conv_bwd(proj1, _ZW, _XBC_W, dxc, P["l1_ssm_conv_w"], "l1_ssm_conv_bwd")
    G["l1_ssm_conv_w"] = dcw[:4]
    G["l1_ssm_conv_b"] = dcb
    dproj1 = jnp.concatenate([dz, dxbc, ddtr], axis=1)
    du2 = _mm_nt_sh(dproj1, P["l1_w_in"], "l1_in_dx")
    G["l1_w_in"] = _mm_tn_sh(u2, dproj1, ODD_IN // N_CHIPS, "l1_in_dw")
    dh2, G["l1_mix_pre_norm"] = _prenorm_bwd(h2, P["l1_mix_pre_norm"], du2, dh3, "l1_mix_prenorm_bwd")
    dh1, g = _ffn_bwd(dh2, ffn0, P["l0_ffn_pre_norm"], P["l0_ffn_post_norm"], P["l0_ffn_w_up"],
                      P["l0_ffn_conv_w"], P["l0_ffn_w_down"], "l0_ffn")
    for k, v in g.items():
        G["l0_ffn_" + (k + "_norm" if k in ("pre", "post") else k)] = v
    do0, G["l0_mix_post_norm"] = _postnorm_bwd(o0, P["l0_mix_post_norm"], dh1, "l0_mix_postnorm_bwd")
    dmix = _mm_nt(do0, P["l0_w_out"], "l0_out_dx")
    G["l0_w_out"] = _mm_tn(mix0, do0, "l0_out_dw")
    (dgate, dxrc, G["l0_lru_w_a"], G["l0_lru_w_x"], G["l0_lru_b_a"], G["l0_lru_b_x"],
     G["l0_lru_lambda"]) = _lru_bwd(proj0, xrc, hl, dmix, *lru_args, "l0_lru_bwd")
    dxr, dcw, dcb = _conv_bwd(proj0, D_MODEL, D_MODEL, dxrc, P["l0_lru_conv_w"], "l0_lru_conv_bwd")
    G["l0_lru_conv_w"] = dcw[:4]
    G["l0_lru_conv_b"] = dcb
    (dq, dk, dv, G["l0_attn_sinks"]), outs = _attn_bwd(
        proj0, P["l0_attn_sinks"], dmix, "l0_attn_bwd",
        ex.scatter_plan(G, ex.AFTER_L0_OUT) if ex else None)
    if ex:
        ex.scattered(ex.AFTER_L0_OUT, outs)
    dproj0 = jnp.concatenate([dgate, dxr, dq, dk.astype(BF16), dv.astype(BF16)], axis=1)
    du0 = _mm_nt_sh(dproj0, P["l0_w_in"], "l0_in_dx")
    G["l0_w_in"] = _mm_tn_sh(u0, dproj0, EVEN_IN // N_CHIPS, "l0_in_dw")
    dh0, G["l0_mix_pre_norm"] = _prenorm_bwd(h0, P["l0_mix_pre_norm"], du0, dh1, "l0_mix_prenorm_bwd")
    return loss_cols, dh0[BLOCK:], dh0[PAD:BLOCK], G


_BIG_COL = ("l0_w_in", "l0_ffn_w_up", "l1_w_in", "l1_ffn_w_up")
_BIG_ROW = ("l0_w_out", "l0_ffn_w_down", "l1_w_out", "l1_ffn_w_down")
_BIG = ("l0_w_in", "l0_w_out", "l0_ffn_w_up", "l0_ffn_w_down",
        "l1_w_in", "l1_w_out", "l1_ffn_w_up", "l1_ffn_w_down")
_SMALL_SHARDED = ("meta_tokens", "l0_lru_conv_w", "l0_ffn_conv_w", "l1_ssm_conv_w", "l1_ffn_conv_w")
_WEIGHTS = ("meta_tokens", "l0_mix_pre_norm", "l0_mix_post_norm", "l0_w_in", "l0_lru_conv_w",
            "l0_lru_conv_b", "l0_lru_w_a", "l0_lru_b_a", "l0_lru_w_x", "l0_lru_b_x", "l0_lru_lambda",
            "l0_attn_sinks", "l0_w_out", "l0_ffn_pre_norm", "l0_ffn_post_norm", "l0_ffn_w_up",
            "l0_ffn_conv_w", "l0_ffn_conv_b", "l0_ffn_w_down", "l1_mix_pre_norm", "l1_mix_post_norm",
            "l1_w_in", "l1_ssm_conv_w", "l1_ssm_conv_b", "l1_dt_bias", "l1_a_log", "l1_d_skip",
            "l1_gate_norm", "l1_w_out", "l1_ffn_pre_norm", "l1_ffn_post_norm", "l1_ffn_w_up",
            "l1_ffn_conv_w", "l1_ffn_conv_b", "l1_ffn_w_down")
_REPL = tuple(n for n in _WEIGHTS if n not in _BIG and n not in _SMALL_SHARDED)


def _pad_lanes(v, n=LANES):
    return jnp.concatenate([v, jnp.zeros((n - v.shape[0],), v.dtype)]).reshape(1, n)


def _step(x, tgt, W, M, V):
    cx, cy, cc = _place()
    chip = 2 * cx + cy

    small_pack, small_sizes = _pack([W[n] for n in _SMALL_SHARDED])
    first = _run_plan(_gather_plan([W["l0_w_in"].astype(BF16), small_pack]), "gather_first")
    small_full = _unpack(first[1], [W[n].shape for n in _SMALL_SHARDED], small_sizes, lead=(N_CHIPS,))
    ex = _Exchange({n: W[n].astype(BF16) for n in _BIG if n != "l0_w_in"})

    P = {"l0_w_in": first[0]}
    for n, g in zip(_SMALL_SHARDED, small_full):
        P[n] = _cols_from_shards(g)
    for n in _REPL:
        v = W[n]
        P[n] = v.reshape(1, -1) if v.ndim == 1 else v
    P["l0_lru_w_a"] = W["l0_lru_w_a"].astype(BF16)
    P["l0_lru_w_x"] = W["l0_lru_w_x"].astype(BF16)
    P["l1_dt_bias"] = _pad_lanes(W["l1_dt_bias"])
    P["l1_a_log"] = _pad_lanes(W["l1_a_log"])
    P["l1_d_skip"] = jnp.repeat(W["l1_d_skip"], SSD_P).reshape(1, D_SSM)
    meta = P.pop("meta_tokens")

    loss_cols, grad_x, grad_meta, G = _local_step(x, tgt, meta, P, ex)
    G["meta_tokens"] = grad_meta

    ex.scattered(ex.LAST, _run_plan(ex.scatter_plan(G, ex.LAST), "grad_scatter_last"))
    core_idx = cc.astype(jnp.int32).reshape(1)
    chip_idx = chip.astype(jnp.int32).reshape(1)
    own_half = [_add8(ex.slabs[n], ex.recv[n], chip_idx, core_idx, f"grad_sum_{n}") for n in _BIG]
    other_half = _run_plan(_sibling_plan(own_half), "grad_sibling_swap")
    small_names = list(_REPL) + list(_SMALL_SHARDED)
    small_list = [G[n] for n in small_names] + [loss_cols]
    spack, ssizes = _pack(small_list)
    sred = _allreduce_small(spack, "small_allreduce")
    sfull = _unpack(sred, [a.shape for a in small_list], ssizes)
    loss = 0.5 / D_MODEL * jnp.sum(sfull[-1])
    small_grads = {}
    for n, g in zip(small_names, sfull[:-1]):
        if n in _SMALL_SHARDED:
            wcols = W[n].shape[1]
            g = lax.dynamic_slice_in_dim(g, chip * wcols, wcols, axis=1)
        small_grads[n] = g.reshape(W[n].shape)

    grads, delta, new_m, new_v = {}, {}, {}, {}
    for n, own, other in zip(_BIG, own_half, other_half):
        grads[n], delta[n], new_m[n], new_v[n] = _adamw_halves(
            W[n], own, other, M[n], V[n], core_idx, f"adamw_{n}")
    s_names = [n for n in _WEIGHTS if n not in _BIG]
    tile_elems = 512 * LANES
    wp, wsz = _pack([W[n] for n in s_names], total_mult=tile_elems)
    gp, _ = _pack([small_grads[n] for n in s_names], total_mult=tile_elems)
    mp, _ = _pack([M[n] for n in s_names], total_mult=tile_elems)
    vp, _ = _pack([V[n] for n in s_names], total_mult=tile_elems)
    dp, nmp, nvp = _adamw(wp, gp, mp, vp, "adamw_small")
    shapes = [W[n].shape for n in s_names]
    for n, a, b, c_ in zip(s_names, _unpack(dp, shapes, wsz), _unpack(nmp, shapes, wsz),
                           _unpack(nvp, shapes, wsz)):
        grads[n] = small_grads[n]
        delta[n], new_m[n], new_v[n] = a, b, c_
    return loss, grad_x, grads, delta, new_m, new_v


def kernel(x, meta_tokens, l0_mix_pre_norm, l0_mix_post_norm, l0_w_in, l0_lru_conv_w, l0_lru_conv_b, l0_lru_w_a, l0_lru_b_a, l0_lru_w_x, l0_lru_b_x, l0_lru_lambda, l0_attn_sinks, l0_w_out, l0_ffn_pre_norm, l0_ffn_post_norm, l0_ffn_w_up, l0_ffn_conv_w, l0_ffn_conv_b, l0_ffn_w_down, l1_mix_pre_norm, l1_mix_post_norm, l1_w_in, l1_ssm_conv_w, l1_ssm_conv_b, l1_dt_bias, l1_a_log, l1_d_skip, l1_gate_norm, l1_w_out, l1_ffn_pre_norm, l1_ffn_post_norm, l1_ffn_w_up, l1_ffn_conv_w, l1_ffn_conv_b, l1_ffn_w_down, loss_target, m_meta_tokens, m_l0_mix_pre_norm, m_l0_mix_post_norm, m_l0_w_in, m_l0_lru_conv_w, m_l0_lru_conv_b, m_l0_lru_w_a, m_l0_lru_b_a, m_l0_lru_w_x, m_l0_lru_b_x, m_l0_lru_lambda, m_l0_attn_sinks, m_l0_w_out, m_l0_ffn_pre_norm, m_l0_ffn_post_norm, m_l0_ffn_w_up, m_l0_ffn_conv_w, m_l0_ffn_conv_b, m_l0_ffn_w_down, m_l1_mix_pre_norm, m_l1_mix_post_norm, m_l1_w_in, m_l1_ssm_conv_w, m_l1_ssm_conv_b, m_l1_dt_bias, m_l1_a_log, m_l1_d_skip, m_l1_gate_norm, m_l1_w_out, m_l1_ffn_pre_norm, m_l1_ffn_post_norm, m_l1_ffn_w_up, m_l1_ffn_conv_w, m_l1_ffn_conv_b, m_l1_ffn_w_down, v_meta_tokens, v_l0_mix_pre_norm, v_l0_mix_post_norm, v_l0_w_in, v_l0_lru_conv_w, v_l0_lru_conv_b, v_l0_lru_w_a, v_l0_lru_b_a, v_l0_lru_w_x, v_l0_lru_b_x, v_l0_lru_lambda, v_l0_attn_sinks, v_l0_w_out, v_l0_ffn_pre_norm, v_l0_ffn_post_norm, v_l0_ffn_w_up, v_l0_ffn_conv_w, v_l0_ffn_conv_b, v_l0_ffn_w_down, v_l1_mix_pre_norm, v_l1_mix_post_norm, v_l1_w_in, v_l1_ssm_conv_w, v_l1_ssm_conv_b, v_l1_dt_bias, v_l1_a_log, v_l1_d_skip, v_l1_gate_norm, v_l1_w_out, v_l1_ffn_pre_norm, v_l1_ffn_post_norm, v_l1_ffn_w_up, v_l1_ffn_conv_w, v_l1_ffn_conv_b, v_l1_ffn_w_down):
    args = locals()
    W = {n: args[n] for n in _WEIGHTS}
    M = {n: args["m_" + n] for n in _WEIGHTS}
    V = {n: args["v_" + n] for n in _WEIGHTS}
    loss, grad_x, grads, delta, new_m, new_v = _step(x[0], loss_target[0], W, M, V)
    return (loss, grad_x[None], *[grads[n] for n in _WEIGHTS], *[delta[n] for n in _WEIGHTS],
            *[new_m[n] for n in _WEIGHTS], *[new_v[n] for n in _WEIGHTS])
```

```python
import functools
import math

import jax
import jax.numpy as jnp
from jax import lax
from jax.experimental import pallas as pl
from jax.experimental.pallas import tpu as pltpu

F32 = jnp.float32
BF16 = jnp.bfloat16

D_MODEL = 1024
N_META = 16
BLOCK = 128
PAD = BLOCK - N_META
EPS = 1e-6
LRU_BLOCKS = 8
LRU_BS = 128
LRU_C = 8.0
N_Q_HEADS = 16
N_KV_HEADS = 2
HEAD_DIM = 64
Q_PER_KV = 8
WINDOW = 128
D_SSM = 2048
SSD_HEADS = 32
SSD_GROUPS = 8
SSD_HPG = 4
SSD_P = 64
SSD_N = 128
D_FF = 2816
NEG = -1e30
LANES = 128
SUBLANES = 8

ADAM_LR = 0.001
ADAM_B1 = 0.9
ADAM_B2 = 0.999
ADAM_EPS = 1e-08
ADAM_WD = 0.01
ADAM_STEP = 10

MESH = pl.DeviceIdType.MESH
N_CHIPS = 4
N_DEV = 8


def _pick(n, cands):
    for c in cands:
        if n % c == 0:
            return c
    raise ValueError(f"no tile for {n} in {cands}")


def _col_tile(n, limit=1792):
    best = None
    for t in range(LANES, min(n, limit) + 1, LANES):
        if n % t == 0:
            best = t
    if best is None:
        raise ValueError(f"no lane tile for {n}")
    return best


def _sigmoid(x):
    return 0.5 + 0.5 * jnp.tanh(0.5 * x)


def _log1p(e):
    u = 1.0 + e
    return jnp.where(u == 1.0, e, jnp.log(u) * (e / jnp.where(u == 1.0, 1.0, u - 1.0)))


def _softplus(x):
    return jnp.maximum(x, 0.0) + _log1p(jnp.exp(-jnp.abs(x)))


def _neg_expm1(x):
    poly = x * (1.0 + x * (0.5 + x * (1.0 / 6.0 + x * (1.0 / 24.0 + x * (1.0 / 120.0)))))
    return -jnp.where(x > -0.05, poly, jnp.exp(x) - 1.0)


_GELU_C = math.sqrt(2.0 / math.pi)


def _gelu(x):
    t = jnp.tanh(_GELU_C * (x + 0.044715 * x * x * x))
    return 0.5 * x * (1.0 + t)


def _gelu_and_grad(x):
    x2 = x * x
    t = jnp.tanh(_GELU_C * (x + 0.044715 * x * x2))
    g = 0.5 * x * (1.0 + t)
    dg = 0.5 * (1.0 + t) + 0.5 * x * (1.0 - t * t) * _GELU_C * (1.0 + 3.0 * 0.044715 * x2)
    return g, dg


def _silu_and_grad(x):
    s = _sigmoid(x)
    return x * s, s * (1.0 + x * (1.0 - s))


def _dot(a, b):
    return jnp.dot(a, b, preferred_element_type=F32)


def _dot_nt(a, b):
    return lax.dot_general(a, b, (((1,), (1,)), ((), ())), preferred_element_type=F32)


def _dot_tn(a, b):
    return lax.dot_general(a, b, (((0,), (0,)), ((), ())), preferred_element_type=F32)


def _row_iota(t):
    return lax.broadcasted_iota(jnp.int32, (t, 1), 0)


def _scan_fwd(a, u, t):
    row = _row_iota(t)
    d = 1
    while d < t:
        m = row >= d
        u_sh = jnp.where(m, pltpu.roll(u, d, 0), 0.0)
        a_sh = jnp.where(m, pltpu.roll(a, d, 0), 1.0)
        u = u + a * u_sh
        a = a * a_sh
        d *= 2
    return a, u


def _scan_rev(c, x, t):
    row = _row_iota(t)
    d = 1
    while d < t:
        m = row < t - d
        x_sh = jnp.where(m, pltpu.roll(x, t - d, 0), 0.0)
        c_sh = jnp.where(m, pltpu.roll(c, t - d, 0), 1.0)
        x = x + c * x_sh
        c = c * c_sh
        d *= 2
    return c, x


def _cumsum_rows(x, t):
    row = _row_iota(t)
    d = 1
    while d < t:
        x = x + jnp.where(row >= d, pltpu.roll(x, d, 0), 0.0)
        d *= 2
    return x


def _rev_cumsum_rows(x, t):
    row = _row_iota(t)
    d = 1
    while d < t:
        x = x + jnp.where(row < t - d, pltpu.roll(x, t - d, 0), 0.0)
        d *= 2
    return x


def _rms_bwd(x, g, dy):
    rs = lax.rsqrt(jnp.mean(x * x, axis=-1, keepdims=True) + EPS)
    gy = dy * g
    dx = rs * gy - x * (rs * rs * rs) * jnp.mean(x * gy, axis=-1, keepdims=True)
    return dx, dy * x * rs


def _mm_nn(a, w, out_dtype, name):
    parts = a if isinstance(a, (tuple, list)) else (a,)
    m = parts[0].shape[0]
    k, n = w.shape
    tm = _pick(m, (640, 512, 256, 128))
    tn = _col_tile(n)
    offs = [sum(p.shape[1] for p in parts[:i]) for i in range(len(parts))]

    def body(*refs):
        w_ref, o_ref = refs[len(parts)], refs[len(parts) + 1]
        acc = None
        for a_ref, p, off in zip(refs, parts, offs):
            t = _dot(a_ref[...].astype(BF16), w_ref[off:off + p.shape[1], :])
            acc = t if acc is None else acc + t
        o_ref[...] = acc.astype(o_ref.dtype)

    return pl.pallas_call(
        body, name=name, grid=(n // tn, m // tm),
        in_specs=[pl.BlockSpec((tm, p.shape[1]), lambda j, i: (i, 0)) for p in parts]
        + [pl.BlockSpec((k, tn), lambda j, i: (0, j))],
        out_specs=pl.BlockSpec((tm, tn), lambda j, i: (i, j)),
        out_shape=jax.ShapeDtypeStruct((m, n), out_dtype),
        compiler_params=pltpu.CompilerParams(dimension_semantics=("parallel", "parallel")),
    )(*parts, w)


def _mm_nt(dy, w, name):
    m, n = dy.shape
    k = w.shape[0]
    wide = n > 3328
    tm = _pick(m, (320, 256, 128)) if wide else _pick(m, (640, 512, 256, 128))
    tk = _col_tile(k, 512 if wide else 1408)

    def body(dy_ref, w_ref, o_ref):
        o_ref[...] = _dot_nt(dy_ref[...].astype(BF16), w_ref[...])

    return pl.pallas_call(
        body, name=name, grid=(k // tk, m // tm),
        in_specs=[pl.BlockSpec((tm, n), lambda j, i: (i, 0)),
                  pl.BlockSpec((tk, n), lambda j, i: (j, 0))],
        out_specs=pl.BlockSpec((tm, tk), lambda j, i: (i, j)),
        out_shape=jax.ShapeDtypeStruct((m, k), F32),
        compiler_params=pltpu.CompilerParams(dimension_semantics=("parallel", "parallel")),
    )(dy, w)


def _mm_tn(a, dy, name):
    m, k = a.shape
    n = dy.shape[1]
    tm = _pick(m, (640, 512, 256, 128))
    tk = _col_tile(k, 1408)
    tn = _col_tile(n, 1664)
    nsteps = m // tm

    def body(a_ref, dy_ref, o_ref, acc):
        @pl.when(pl.program_id(2) == 0)
        def _():
            acc[...] = jnp.zeros_like(acc)

        acc[...] += _dot_tn(a_ref[...].astype(BF16), dy_ref[...].astype(BF16))

        @pl.when(pl.program_id(2) == nsteps - 1)
        def _():
            o_ref[...] = acc[...].astype(o_ref.dtype)

    return pl.pallas_call(
        body, name=name, grid=(k // tk, n // tn, nsteps),
        in_specs=[pl.BlockSpec((tm, tk), lambda kk, j, i: (i, kk)),
                  pl.BlockSpec((tm, tn), lambda kk, j, i: (i, j))],
        out_specs=pl.BlockSpec((tk, tn), lambda kk, j, i: (kk, j)),
        out_shape=jax.ShapeDtypeStruct((k, n), BF16),
        scratch_shapes=[pltpu.VMEM((tk, tn), F32)],
        compiler_params=pltpu.CompilerParams(
            dimension_semantics=("parallel", "parallel", "arbitrary")),
    )(a, dy)


def _mm_nn_sh(a, w4, n_out, name):
    m, k = a.shape
    s, _, n = w4.shape
    tm = _pick(m, (320, 256, 128))

    def body(a_ref, w_ref, o_ref):
        av = a_ref[...].astype(BF16)
        for j in range(s):
            o_ref[:, j * n:(j + 1) * n] = _dot(av, w_ref[j])
        if n_out > s * n:
            o_ref[:, s * n:] = jnp.zeros((tm, n_out - s * n), F32)

    return pl.pallas_call(
        body, name=name, grid=(m // tm,),
        in_specs=[pl.BlockSpec((tm, k), lambda i: (i, 0)),
                  pl.BlockSpec((s, k, n), lambda i: (0, 0, 0))],
        out_specs=pl.BlockSpec((tm, n_out), lambda i: (i, 0)),
        out_shape=jax.ShapeDtypeStruct((m, n_out), F32),
        compiler_params=pltpu.CompilerParams(dimension_semantics=("parallel",)),
    )(a, w4)


def _mm_nt_sh(dy, w4, name):
    dys = dy if isinstance(dy, (tuple, list)) else (dy,)
    m = dys[0].shape[0]
    s, k, n = w4.shape
    tm = _pick(m, (640, 512, 256, 128))
    tk = _col_tile(k, 1024)
    where = _shard_columns(dys, s, n)

    def body(*refs):
        w_ref, o_ref = refs[len(dys)], refs[len(dys) + 1]
        acc = None
        for j, (p, c0) in enumerate(where):
            t = _dot_nt(refs[p][:, c0:c0 + n].astype(BF16), w_ref[j])
            acc = t if acc is None else acc + t
        o_ref[...] = acc

    return pl.pallas_call(
        body, name=name, grid=(k // tk, m // tm),
        in_specs=[pl.BlockSpec((tm, d.shape[1]), lambda j, i: (i, 0)) for d in dys]
        + [pl.BlockSpec((s, tk, n), lambda j, i: (0, j, 0))],
        out_specs=pl.BlockSpec((tm, tk), lambda j, i: (i, j)),
        out_shape=jax.ShapeDtypeStruct((m, k), F32),
        compiler_params=pltpu.CompilerParams(dimension_semantics=("parallel", "parallel")),
    )(*dys, w4)


def _shard_columns(dys, s, n):
    where = []
    for p, d in enumerate(dys):
        where += [(p, c * n) for c in range(d.shape[1] // n)]
    assert len(where) >= s
    return where[:s]


def _mm_tn_sh(a, dy, n, name):
    dys = dy if isinstance(dy, (tuple, list)) else (dy,)
    m, k = a.shape
    s = N_CHIPS
    tm = _pick(m, (640, 512, 256, 128))
    tk = _col_tile(k, 512 if n <= 1024 else 256)
    nsteps = m // tm
    where = _shard_columns(dys, s, n)

    def body(*refs):
        a_ref, o_ref, acc = refs[0], refs[len(dys) + 1], refs[len(dys) + 2]

        @pl.when(pl.program_id(1) == 0)
        def _():
            acc[...] = jnp.zeros_like(acc)

        av = a_ref[...].astype(BF16)
        for j, (p, c0) in enumerate(where):
            acc[j] += _dot_tn(av, refs[1 + p][:, c0:c0 + n].astype(BF16))

        @pl.when(pl.program_id(1) == nsteps - 1)
        def _():
            o_ref[...] = acc[...].astype(o_ref.dtype)

    return pl.pallas_call(
        body, name=name, grid=(k // tk, nsteps),
        in_specs=[pl.BlockSpec((tm, tk), lambda kk, i: (i, kk))]
        + [pl.BlockSpec((tm, d.shape[1]), lambda kk, i: (i, 0)) for d in dys],
        out_specs=pl.BlockSpec((s, tk, n), lambda kk, i: (0, kk, 0)),
        out_shape=jax.ShapeDtypeStruct((s, k, n), BF16),
        scratch_shapes=[pltpu.VMEM((s, tk, n), F32)],
        compiler_params=pltpu.CompilerParams(dimension_semantics=("parallel", "arbitrary")),
    )(a, *dys)


def _rowcall(name, body, lp, tm, rows=(), prevs=(), vecs=(), outs=(), accs=(), scratch=(),
             reverse=False, seq=False):
    nt = lp // tm
    hb = tm // SUBLANES

    def ri(i):
        return nt - 1 - i if reverse else i

    in_specs, args = [], []
    for arr, w, cb in rows:
        in_specs.append(pl.BlockSpec((tm, w), lambda i, cb=cb: (ri(i), cb)))
        args.append(arr)
    for arr, w, cb in prevs:
        in_specs.append(pl.BlockSpec((SUBLANES, w), lambda i, cb=cb: (jnp.maximum(ri(i) * hb - 1, 0), cb)))
        args.append(arr)
    for arr in vecs:
        in_specs.append(pl.BlockSpec(arr.shape, lambda i, nd=arr.ndim: (0,) * nd))
        args.append(arr)
    out_shape, out_specs = [], []
    for w, dt in outs:
        out_shape.append(jax.ShapeDtypeStruct((lp, w), dt))
        out_specs.append(pl.BlockSpec((tm, w), lambda i: (ri(i), 0)))
    for shp, dt in accs:
        out_shape.append(jax.ShapeDtypeStruct(shp, dt))
        out_specs.append(pl.BlockSpec(shp, lambda i, nd=len(shp): (0,) * nd))

    def kern(*refs):
        i = pl.program_id(0)
        body(ri(i), i == 0, *refs)

    sem = ("arbitrary",) if (seq or accs) else ("parallel",)
    res = pl.pallas_call(
        kern, name=name, grid=(nt,), in_specs=in_specs, out_specs=out_specs,
        out_shape=out_shape, scratch_shapes=list(scratch),
        compiler_params=pltpu.CompilerParams(dimension_semantics=sem),
    )(*args)
    return res


def _acc_add(first, ref, val):
    @pl.when(first)
    def _():
        ref[...] = jnp.zeros_like(ref)

    ref[...] += val


def _real_rows(r, tm):
    return (r * tm + _row_iota(tm)) >= PAD


def _rmsnorm_fwd(h, g, name):
    lp, d = h.shape
    tm = _pick(lp, (640, 512, 256, 128))

    def body(r, first, h_ref, g_ref, u_ref):
        x = h_ref[...]
        rs = lax.rsqrt(jnp.mean(x * x, axis=-1, keepdims=True) + EPS)
        u_ref[...] = (x * rs * g_ref[...]).astype(u_ref.dtype)

    return _rowcall(name, body, lp, tm, rows=[(h, d, 0)], vecs=[g], outs=[(d, BF16)])[0]


def _postnorm_res_fwd(h, o, g, name):
    lp, d = h.shape
    tm = _pick(lp, (640, 512, 256, 128))

    def body(r, first, h_ref, o_ref, g_ref, out_ref):
        x = o_ref[...]
        rs = lax.rsqrt(jnp.mean(x * x, axis=-1, keepdims=True) + EPS)
        out_ref[...] = jnp.where(_real_rows(r, tm), h_ref[...] + x * rs * g_ref[...], 0.0)

    return _rowcall(name, body, lp, tm, rows=[(h, d, 0), (o, d, 0)], vecs=[g], outs=[(d, F32)])[0]


def _postnorm_bwd(o, g, dh, name):
    lp, d = o.shape
    tm = _pick(lp, (640, 512, 256, 128))

    def body(r, first, o_ref, dh_ref, g_ref, do_ref, dg_ref):
        dx, dgt = _rms_bwd(o_ref[...], g_ref[...], dh_ref[...])
        do_ref[...] = dx.astype(do_ref.dtype)
        _acc_add(first, dg_ref, jnp.sum(dgt, axis=0, keepdims=True))

    return _rowcall(name, body, lp, tm, rows=[(o, d, 0), (dh, d, 0)], vecs=[g],
                    outs=[(d, BF16)], accs=[((1, d), F32)])


def _prenorm_bwd(h, g, du, dh_res, name):
    lp, d = h.shape
    tm = _pick(lp, (640, 512, 256, 128))

    def body(r, first, h_ref, du_ref, dres_ref, g_ref, dh_ref, dg_ref):
        dx, dgt = _rms_bwd(h_ref[...], g_ref[...], du_ref[...])
        dh_ref[...] = jnp.where(_real_rows(r, tm), dres_ref[...] + dx, 0.0)
        _acc_add(first, dg_ref, jnp.sum(dgt, axis=0, keepdims=True))

    return _rowcall(name, body, lp, tm, rows=[(h, d, 0), (du, d, 0), (dh_res, d, 0)], vecs=[g],
                    outs=[(d, F32)], accs=[((1, d), F32)])


def _loss_fwd_bwd(h, tgt, name):
    lp, d = h.shape
    tm = _pick(lp, (640, 512, 256, 128))

    def body(r, first, h_ref, t_ref, dh_ref, ls_ref):
        tok = (r * tm + _row_iota(tm)) >= BLOCK
        e = jnp.where(tok, h_ref[...] - t_ref[...], 0.0)
        dh_ref[...] = e * (1.0 / d)
        _acc_add(first, ls_ref, jnp.sum(e * e, axis=0, keepdims=True))

    return _rowcall(name, body, lp, tm, rows=[(h, d, 0), (tgt, d, 0)],
                    outs=[(d, F32)], accs=[((1, d), F32)])


def _conv_tiles(lp, width):
    wc = _col_tile(width, 1408)
    tm = _pick(lp, (320, 256, 128))
    return tm, wc


def _conv_fwd(x, col_off, width, w, b, name):
    lp = x.shape[0]
    kk = w.shape[0]
    tm, wc = _conv_tiles(lp, width)
    offb = col_off // wc
    assert col_off % wc == 0
    hb = tm // SUBLANES

    def body(x_ref, xp_ref, w_ref, b_ref, y_ref):
        i = pl.program_id(1)
        xv = x_ref[...]
        halo = jnp.where(i > 0, xp_ref[...], 0.0)
        xx = jnp.concatenate([halo, xv], axis=0)
        acc = b_ref[...] + w_ref[kk - 1:kk, :] * xv
        for j in range(1, kk):
            acc = acc + w_ref[kk - 1 - j:kk - j, :] * pltpu.roll(xx, j, 0)[SUBLANES:, :]
        y_ref[...] = acc

    return pl.pallas_call(
        body, name=name, grid=(width // wc, lp // tm),
        in_specs=[pl.BlockSpec((tm, wc), lambda j, i: (i, offb + j)),
                  pl.BlockSpec((SUBLANES, wc), lambda j, i: (jnp.maximum(i * hb - 1, 0), offb + j)),
                  pl.BlockSpec((kk, wc), lambda j, i: (0, j)),
                  pl.BlockSpec((1, wc), lambda j, i: (0, j))],
        out_specs=pl.BlockSpec((tm, wc), lambda j, i: (i, j)),
        out_shape=jax.ShapeDtypeStruct((lp, width), F32),
        compiler_params=pltpu.CompilerParams(dimension_semantics=("parallel", "parallel")),
    )(x, x, w, b)


def _conv_bwd(x, col_off, width, dy, w, name, w_col_off=0):
    lp = x.shape[0]
    kk = w.shape[0]
    tm, wc = _conv_tiles(lp, width)
    offb = col_off // wc
    woffb = w_col_off // wc
    assert col_off % wc == 0 and w_col_off % wc == 0
    hb = tm // SUBLANES
    hrows = SUBLANES * (4 // dy.dtype.itemsize)
    ext = tm + hrows

    def body(x_ref, xp_ref, dy_ref, dn_ref, w_ref, dx_ref, dw_ref, db_ref):
        i = pl.program_id(1)
        last = pl.num_programs(1) - 1
        xv = x_ref[...]
        dyv = dy_ref[...].astype(F32)
        xx = jnp.concatenate([jnp.where(i > 0, xp_ref[...], 0.0), xv], axis=0)
        dd = jnp.concatenate([dyv, jnp.where(i < last, dn_ref[...].astype(F32), 0.0)], axis=0)
        dx = w_ref[kk - 1:kk, :] * dyv
        rows = [jnp.sum(dyv * xv, axis=0, keepdims=True)]
        for m in range(1, kk):
            dx = dx + w_ref[kk - 1 - m:kk - m, :] * pltpu.roll(dd, ext - m, 0)[:tm, :]
            rows.append(jnp.sum(dyv * pltpu.roll(xx, m, 0)[SUBLANES:, :], axis=0, keepdims=True))
        dx_ref[...] = dx.astype(dx_ref.dtype)
        dwp = jnp.concatenate(rows[::-1] + [jnp.zeros((SUBLANES - kk, wc), F32)], axis=0)

        @pl.when(i == 0)
        def _():
            dw_ref[...] = jnp.zeros_like(dw_ref)
            db_ref[...] = jnp.zeros_like(db_ref)

        dw_ref[...] += dwp
        db_ref[...] += jnp.sum(dyv, axis=0, keepdims=True)

    return pl.pallas_call(
        body, name=name, grid=(width // wc, lp // tm),
        in_specs=[pl.BlockSpec((tm, wc), lambda j, i: (i, offb + j)),
                  pl.BlockSpec((SUBLANES, wc), lambda j, i: (jnp.maximum(i * hb - 1, 0), offb + j)),
                  pl.BlockSpec((tm, wc), lambda j, i: (i, j)),
                  pl.BlockSpec((hrows, wc), lambda j, i: (jnp.minimum((i + 1) * (tm // hrows), lp // hrows - 1), j)),
                  pl.BlockSpec((kk, wc), lambda j, i: (0, woffb + j))],
        out_specs=[pl.BlockSpec((tm, wc), lambda j, i: (i, j)),
                   pl.BlockSpec((SUBLANES, wc), lambda j, i: (0, j)),
                   pl.BlockSpec((1, wc), lambda j, i: (0, j))],
        out_shape=[jax.ShapeDtypeStruct((lp, width), BF16),
                   jax.ShapeDtypeStruct((SUBLANES, width), F32),
                   jax.ShapeDtypeStruct((1, width), F32)],
        compiler_params=pltpu.CompilerParams(dimension_semantics=("parallel", "arbitrary")),
    )(x, x, dy, dy, w)


_FFN_K = 3
_FFN_WC = 1408


def _conv3_ext(x_ext, w_ref, b_ref):
    return (b_ref[...] + w_ref[2:3, :] * x_ext + w_ref[1:2, :] * pltpu.roll(x_ext, 1, 0)
            + w_ref[0:1, :] * pltpu.roll(x_ext, 2, 0))


def _ffn_convact_fwd(hp, cw, cb, name):
    lp = hp.shape[0]
    tm = _pick(lp, (320, 256, 128))
    wc = _FFN_WC
    nj = D_FF // wc
    hb = tm // SUBLANES

    def body(g_ref, gp_ref, u_ref, up_ref, wg_ref, wu_ref, bg_ref, bu_ref, a_ref, hg_ref, hu_ref):
        i = pl.program_id(1)

        def conv(x_ref, p_ref, w_ref, b_ref):
            x_ext = jnp.concatenate([jnp.where(i > 0, p_ref[...], 0.0), x_ref[...]], axis=0)
            return _conv3_ext(x_ext, w_ref, b_ref)[SUBLANES:, :]

        hg = conv(g_ref, gp_ref, wg_ref, bg_ref)
        hu = conv(u_ref, up_ref, wu_ref, bu_ref)
        a_ref[...] = (_gelu(hg) * hu).astype(a_ref.dtype)
        hg_ref[...] = hg.astype(hg_ref.dtype)
        hu_ref[...] = hu.astype(hu_ref.dtype)

    tile = lambda off: pl.BlockSpec((tm, wc), lambda j, i: (i, off + j))
    prev = lambda off: pl.BlockSpec((SUBLANES, wc), lambda j, i: (jnp.maximum(i * hb - 1, 0), off + j))
    vec = lambda rows, off: pl.BlockSpec((rows, wc), lambda j, i: (0, off + j))
    return pl.pallas_call(
        body, name=name, grid=(nj, lp // tm),
        in_specs=[tile(0), prev(0), tile(nj), prev(nj), vec(_FFN_K, 0), vec(_FFN_K, nj), vec(1, 0), vec(1, nj)],
        out_specs=[tile(0)] * 3,
        out_shape=[jax.ShapeDtypeStruct((lp, D_FF), BF16)] * 3,
        compiler_params=pltpu.CompilerParams(dimension_semantics=("parallel", "parallel")),
    )(hp, hp, hp, hp, cw, cw, cb, cb)


def _ffn_act_bwd(hg, hu, dact, name):
    lp = hg.shape[0]
    tm = _pick(lp, (320, 256, 128))

    def body(r, first, g_ref, u_ref, da_ref, dg_ref, du_ref):
        gl, dgl = _gelu_and_grad(g_ref[...].astype(F32))
        da = da_ref[...]
        dg_ref[...] = (da * u_ref[...].astype(F32) * dgl).astype(dg_ref.dtype)
        du_ref[...] = (da * gl).astype(du_ref.dtype)

    return _rowcall(name, body, lp, tm, rows=[(hg, D_FF, 0), (hu, D_FF, 0), (dact, D_FF, 0)],
                    outs=[(D_FF, BF16), (D_FF, BF16)])


def _lru_gates(x, wa_ref, wx_ref, ba, bx, lam):
    xb = x.astype(BF16)
    za, zx = [], []
    for n in range(LRU_BLOCKS):
        xs = xb[:, n * LRU_BS:(n + 1) * LRU_BS]
        za.append(_dot(xs, wa_ref[n]))
        zx.append(_dot(xs, wx_ref[n]))
    r = _sigmoid(jnp.concatenate(za, axis=1) + ba)
    ig = _sigmoid(jnp.concatenate(zx, axis=1) + bx)
    sp = _softplus(-lam)
    log_a = -LRU_C * r * sp
    a = jnp.exp(log_a)
    om = _neg_expm1(2.0 * log_a)
    mult = jnp.sqrt(om)
    return xb, r, ig, sp, a, om, mult


def _lru_fwd(proj, xrc, wa, wx, ba, bx, lam, name):
    lp, d = xrc.shape
    tm = BLOCK

    def body(r_idx, first, gate_ref, x_ref, wa_ref, wx_ref, ba_ref, bx_ref, lam_ref,
             y_ref, h_ref, carry):
        @pl.when(first)
        def _():
            carry[...] = jnp.zeros_like(carry)

        x = x_ref[...]
        _, _, ig, _, a, _, mult = _lru_gates(x, wa_ref, wx_ref, ba_ref[...], bx_ref[...], lam_ref[...])
        u = jnp.where(_real_rows(r_idx, tm), mult * ig * x, 0.0)
        acum, hloc = _scan_fwd(a, u, tm)
        h = hloc + acum * carry[0:1, :]
        h_ref[...] = h
        carry[0:1, :] = h[tm - 1:tm, :]
        y_ref[...] = (_gelu(gate_ref[...]) * h).astype(y_ref.dtype)

    return _rowcall(name, body, lp, tm, rows=[(proj, d, 0), (xrc, d, 0)],
                    vecs=[wa, wx, ba, bx, lam], outs=[(d, BF16), (d, F32)],
                    scratch=[pltpu.VMEM((SUBLANES, d), F32)], seq=True)


def _lru_bwd(proj, xrc, hl, dmix, wa, wx, ba, bx, lam, name):
    lp, d = xrc.shape
    tm = BLOCK

    def body(r_idx, first, gate_ref, x_ref, h_ref, dy_ref, hp_ref, wa_ref, wx_ref, ba_ref, bx_ref,
             lam_ref, dgate_ref, dx_ref, dwa_ref, dwx_ref, dba_ref, dbx_ref, dlam_ref, carry):
        @pl.when(first)
        def _():
            carry[...] = jnp.zeros_like(carry)
            dwa_ref[...] = jnp.zeros_like(dwa_ref)
            dwx_ref[...] = jnp.zeros_like(dwx_ref)
            dba_ref[...] = jnp.zeros_like(dba_ref)
            dbx_ref[...] = jnp.zeros_like(dbx_ref)
            dlam_ref[...] = jnp.zeros_like(dlam_ref)

        x = x_ref[...]
        lam = lam_ref[...]
        xb, r, ig, sp, a, om, mult = _lru_gates(x, wa_ref, wx_ref, ba_ref[...], bx_ref[...], lam)
        h = h_ref[...]
        dy = dy_ref[...]
        gl, dgl = _gelu_and_grad(gate_ref[...])
        dgate_ref[...] = (dy * h * dgl).astype(dgate_ref.dtype)
        row = _row_iota(tm)
        lastrow = row == tm - 1
        xg = dy * gl + jnp.where(lastrow, carry[0:1, :], 0.0)
        c = jnp.where(lastrow, 1.0, pltpu.roll(a, tm - 1, 0))
        _, g = _scan_rev(c, xg, tm)
        carry[0:1, :] = a[0:1, :] * g[0:1, :]
        hprev_in = jnp.where(r_idx > 0, hp_ref[SUBLANES - 1:SUBLANES, :], 0.0)
        hprev = jnp.where(row == 0, hprev_in, pltpu.roll(h, 1, 0))
        du = jnp.where(_real_rows(r_idx, tm), g, 0.0)
        da = g * hprev
        dmult = du * ig * x
        dig = du * mult * x
        dxv = du * mult * ig
        e2 = 1.0 - om
        dlog_a = da * a - dmult * e2 / mult
        dr = dlog_a * (-LRU_C) * sp
        dsp = jnp.sum(dlog_a * (-LRU_C) * r, axis=0, keepdims=True)
        dlam_ref[...] += -dsp * _sigmoid(-lam)
        dza = dr * r * (1.0 - r)
        dzx = dig * ig * (1.0 - ig)
        dba_ref[...] += jnp.sum(dza, axis=0, keepdims=True)
        dbx_ref[...] += jnp.sum(dzx, axis=0, keepdims=True)
        dzab = dza.astype(BF16)
        dzxb = dzx.astype(BF16)
        parts = []
        for n in range(LRU_BLOCKS):
            sl = slice(n * LRU_BS, (n + 1) * LRU_BS)
            dwa_ref[n] += _dot_tn(xb[:, sl], dzab[:, sl])
            dwx_ref[n] += _dot_tn(xb[:, sl], dzxb[:, sl])
            parts.append(_dot_nt(dzab[:, sl], wa_ref[n]) + _dot_nt(dzxb[:, sl], wx_ref[n]))
        dx_ref[...] = dxv + jnp.concatenate(parts, axis=1)

    return _rowcall(name, body, lp, tm,
                    rows=[(proj, d, 0), (xrc, d, 0), (hl, d, 0), (dmix, d, 0)],
                    prevs=[(hl, d, 0)], vecs=[wa, wx, ba, bx, lam],
                    outs=[(d, BF16), (d, F32)],
                    accs=[((LRU_BLOCKS, LRU_BS, LRU_BS), F32), ((LRU_BLOCKS, LRU_BS, LRU_BS), F32),
                          ((1, d), F32), ((1, d), F32), ((1, d), F32)],
                    scratch=[pltpu.VMEM((SUBLANES, d), F32)], reverse=True, seq=True)


_SLOPES = [2.0 ** (-8.0 * (h + 1) / N_Q_HEADS) for h in range(N_Q_HEADS)]
_QK_SCALE = HEAD_DIM ** -0.5
_QCOL = 2 * D_MODEL // D_MODEL
_KCOL = (3 * D_MODEL) // LANES
_VCOL = _KCOL + 1


def _attn_masks(n):
    start = pl.multiple_of(jnp.maximum(n - 1, 0) * BLOCK, BLOCK)
    qi = n * BLOCK + lax.broadcasted_iota(jnp.int32, (BLOCK, 2 * BLOCK), 0)
    kj = start + lax.broadcasted_iota(jnp.int32, (BLOCK, 2 * BLOCK), 1)
    dist = qi - kj
    ok = (kj >= BLOCK) & (dist >= 0) & (dist < WINDOW)
    dm = (n * BLOCK - PAD + lax.broadcasted_iota(jnp.int32, (BLOCK, N_META), 0)
          - lax.broadcasted_iota(jnp.int32, (BLOCK, N_META), 1))
    okm = dm >= 0
    return start, ok, dist.astype(F32), okm, jnp.minimum(dm, WINDOW).astype(F32)


def _group_rows(ref, g):
    return jnp.concatenate(
        [ref[:, (g * Q_PER_KV + hh) * HEAD_DIM:(g * Q_PER_KV + hh + 1) * HEAD_DIM] for hh in range(Q_PER_KV)],
        axis=0).astype(BF16)


def _attn_probs(qg, kg, kmg, sink_ref, g, ok, distf, okm, dmf):
    slope = jnp.stack([jnp.full((1, 1), _SLOPES[g * Q_PER_KV + hh], F32) for hh in range(Q_PER_KV)])
    sink = jnp.stack([sink_ref[0:1, g * Q_PER_KV + hh:g * Q_PER_KV + hh + 1] for hh in range(Q_PER_KV)])
    s = (_dot_nt(qg, kg) * _QK_SCALE).reshape(Q_PER_KV, BLOCK, 2 * BLOCK)
    sm = (_dot_nt(qg, kmg) * _QK_SCALE).reshape(Q_PER_KV, BLOCK, N_META)
    s = jnp.where(ok[None], s - slope * distf[None], NEG)
    sm = jnp.where(okm[None], sm - slope * dmf[None], NEG)
    mx = jnp.maximum(jnp.maximum(jnp.max(s, axis=-1, keepdims=True),
                                 jnp.max(sm, axis=-1, keepdims=True)), sink)
    p = jnp.exp(s - mx)
    pm = jnp.exp(sm - mx)
    ps = jnp.exp(sink - mx)
    inv = 1.0 / (jnp.sum(p, axis=-1, keepdims=True) + jnp.sum(pm, axis=-1, keepdims=True) + ps)
    return p * inv, pm * inv, ps * inv


def _attn_fwd(proj, sinks, name, plan=None):
    lp = proj.shape[0]
    nblk = lp // BLOCK
    p_in, p_shapes, p_out, p_scr = _plan_parts(plan)

    def body(*refs):
        q_ref, k_ref, v_ref, sink_ref = refs[:4]
        cins = refs[4:4 + len(p_in)]
        o_ref = refs[4 + len(p_in)]
        couts = refs[5 + len(p_in):5 + len(p_in) + len(p_out)]
        sems = refs[5 + len(p_in) + len(p_out):]
        n = pl.program_id(0)
        if plan is not None:
            @pl.when(n == 0)
            def _():
                plan.start(cins, couts, sems)

        start, ok, distf, okm, dmf = _attn_masks(n)
        kb = k_ref[pl.ds(start, 2 * BLOCK), :].astype(BF16)
        vb = v_ref[pl.ds(start, 2 * BLOCK), :].astype(BF16)
        km = k_ref[PAD:BLOCK, :].astype(BF16)
        vm = v_ref[PAD:BLOCK, :].astype(BF16)
        for g in range(N_KV_HEADS):
            gs = slice(g * HEAD_DIM, (g + 1) * HEAD_DIM)
            pn, pmn, _ = _attn_probs(_group_rows(q_ref, g), kb[:, gs], km[:, gs], sink_ref, g,
                                     ok, distf, okm, dmf)
            o = (_dot(pn.astype(BF16).reshape(Q_PER_KV * BLOCK, 2 * BLOCK), vb[:, gs])
                 + _dot(pmn.astype(BF16).reshape(Q_PER_KV * BLOCK, N_META), vm[:, gs]))
            for hh in range(Q_PER_KV):
                h = g * Q_PER_KV + hh
                o_ref[:, h * HEAD_DIM:(h + 1) * HEAD_DIM] = o[hh * BLOCK:(hh + 1) * BLOCK, :].astype(o_ref.dtype)
        if plan is not None:
            @pl.when(n == nblk - 1)
            def _():
                plan.wait(cins, couts, sems)

    res = pl.pallas_call(
        body, name=name, grid=(nblk,),
        in_specs=[pl.BlockSpec((BLOCK, D_MODEL), lambda n: (n, _QCOL)),
                  pl.BlockSpec((lp, LANES), lambda n: (0, _KCOL)),
                  pl.BlockSpec((lp, LANES), lambda n: (0, _VCOL)),
                  pl.BlockSpec(sinks.shape, lambda n: (0, 0))] + p_in,
        out_specs=[pl.BlockSpec((BLOCK, D_MODEL), lambda n: (n, 0))] + p_out,
        out_shape=[jax.ShapeDtypeStruct((lp, D_MODEL), BF16)] + p_shapes,
        scratch_shapes=p_scr,
        compiler_params=pltpu.CompilerParams(dimension_semantics=("arbitrary",)),
    )(proj, proj, proj, sinks, *(plan.ins if plan is not None else []))
    return res[0], res[1:]


def _attn_bwd(proj, sinks, dmix, name, plan=None):
    lp = proj.shape[0]
    nblk = lp // BLOCK

    p_in, p_shapes, p_out, p_scr = _plan_parts(plan)

    def body(*refs):
        q_ref, k_ref, v_ref, sink_ref, dy_ref = refs[:5]
        cins = refs[5:5 + len(p_in)]
        dq_ref, dk_ref, dv_ref, ds_ref = refs[5 + len(p_in):9 + len(p_in)]
        couts = refs[9 + len(p_in):9 + len(p_in) + len(p_out)]
        sems = refs[9 + len(p_in) + len(p_out):]
        n = pl.program_id(0)

        @pl.when(n == 0)
        def _():
            dk_ref[...] = jnp.zeros_like(dk_ref)
            dv_ref[...] = jnp.zeros_like(dv_ref)
            ds_ref[...] = jnp.zeros_like(ds_ref)
            if plan is not None:
                plan.start(cins, couts, sems)

        start, ok, distf, okm, dmf = _attn_masks(n)
        kb = k_ref[pl.ds(start, 2 * BLOCK), :].astype(BF16)
        vb = v_ref[pl.ds(start, 2 * BLOCK), :].astype(BF16)
        km = k_ref[PAD:BLOCK, :].astype(BF16)
        vm = v_ref[PAD:BLOCK, :].astype(BF16)
        lane16 = lax.broadcasted_iota(jnp.int32, (1, N_Q_HEADS), 1)
        dsink = jnp.zeros((1, N_Q_HEADS), F32)
        rows = Q_PER_KV * BLOCK
        for g in range(N_KV_HEADS):
            gs = slice(g * HEAD_DIM, (g + 1) * HEAD_DIM)
            qg = _group_rows(q_ref, g)
            dog = _group_rows(dy_ref, g)
            pn, pmn, psn = _attn_probs(qg, kb[:, gs], km[:, gs], sink_ref, g, ok, distf, okm, dmf)
            dp = _dot_nt(dog, vb[:, gs]).reshape(Q_PER_KV, BLOCK, 2 * BLOCK)
            dpm = _dot_nt(dog, vm[:, gs]).reshape(Q_PER_KV, BLOCK, N_META)
            delta = (jnp.sum(pn * dp, axis=-1, keepdims=True)
                     + jnp.sum(pmn * dpm, axis=-1, keepdims=True))
            dsb = (pn * (dp - delta)).astype(BF16).reshape(rows, 2 * BLOCK)
            dsm = (pmn * (dpm - delta)).astype(BF16).reshape(rows, N_META)
            dsk = jnp.sum(psn * delta, axis=1, keepdims=True)
            for hh in range(Q_PER_KV):
                dsink = dsink - jnp.where(lane16 == g * Q_PER_KV + hh, dsk[hh], 0.0)
            dq = (_dot(dsb, kb[:, gs]) + _dot(dsm, km[:, gs])) * _QK_SCALE
            for hh in range(Q_PER_KV):
                h = g * Q_PER_KV + hh
                dq_ref[:, h * HEAD_DIM:(h + 1) * HEAD_DIM] = dq[hh * BLOCK:(hh + 1) * BLOCK, :].astype(dq_ref.dtype)
            pnb = pn.astype(BF16).reshape(rows, 2 * BLOCK)
            pmnb = pmn.astype(BF16).reshape(rows, N_META)
            dk_ref[pl.ds(start, 2 * BLOCK), gs] += _dot_tn(dsb, qg) * _QK_SCALE
            dv_ref[pl.ds(start, 2 * BLOCK), gs] += _dot_tn(pnb, dog)
            dk_ref[PAD:BLOCK, gs] += _dot_tn(dsm, qg) * _QK_SCALE
            dv_ref[PAD:BLOCK, gs] += _dot_tn(pmnb, dog)
        ds_ref[...] += dsink
        if plan is not None:
            @pl.when(n == nblk - 1)
            def _():
                plan.wait(cins, couts, sems)

    res = pl.pallas_call(
        body, name=name, grid=(nblk,),
        in_specs=[pl.BlockSpec((BLOCK, D_MODEL), lambda n: (n, _QCOL)),
                  pl.BlockSpec((lp, LANES), lambda n: (0, _KCOL)),
                  pl.BlockSpec((lp, LANES), lambda n: (0, _VCOL)),
                  pl.BlockSpec(sinks.shape, lambda n: (0, 0)),
                  pl.BlockSpec((BLOCK, D_MODEL), lambda n: (n, 1))] + p_in,
        out_specs=[pl.BlockSpec((BLOCK, D_MODEL), lambda n: (n, 0)),
                   pl.BlockSpec((lp, LANES), lambda n: (0, 0)),
                   pl.BlockSpec((lp, LANES), lambda n: (0, 0)),
                   pl.BlockSpec((1, N_Q_HEADS), lambda n: (0, 0))] + p_out,
        out_shape=[jax.ShapeDtypeStruct((lp, D_MODEL), BF16),
                   jax.ShapeDtypeStruct((lp, LANES), F32),
                   jax.ShapeDtypeStruct((lp, LANES), F32),
                   jax.ShapeDtypeStruct((1, N_Q_HEADS), F32)] + p_shapes,
        scratch_shapes=p_scr,
        compiler_params=pltpu.CompilerParams(dimension_semantics=("arbitrary",)),
    )(proj, proj, proj, sinks, dmix, *(plan.ins if plan is not None else []))
    return res[:4], res[4:]


_ZW = D_SSM
_XBC_W = D_SSM + 2 * SSD_GROUPS * SSD_N
_DT_COL = (_ZW + _XBC_W) // LANES
EVEN_IN = 3 * D_MODEL + 2 * LANES
ODD_IN = _ZW + _XBC_W + SSD_HEADS
ODD_IN_PAD = _ZW + _XBC_W + LANES


def _ssm_prep_fwd(xc, proj, dt_bias, name):
    lp = xc.shape[0]
    tm = _pick(lp, (320, 256, 128))

    def body(r, first, xc_ref, dtr_ref, b_ref, act_ref, dt_ref):
        real = _real_rows(r, tm)
        act, _ = _silu_and_grad(xc_ref[...])
        act_ref[...] = jnp.where(real, act, 0.0)
        dt_ref[...] = jnp.where(real, _softplus(dtr_ref[...] + b_ref[...]), 0.0)

    return _rowcall(name, body, lp, tm, rows=[(xc, _XBC_W, 0), (proj, LANES, _DT_COL)],
                    vecs=[dt_bias], outs=[(_XBC_W, F32), (LANES, F32)])


def _ssm_prep_bwd(xc, proj, dt_bias, dxs, dxskip, db, dc, ddt, name):
    lp = xc.shape[0]
    tm = BLOCK

    def body(r, first, xc_ref, dtr_ref, dxs_ref, dsk_ref, db_ref, dc_ref, ddt_ref, b_ref,
             dxc_ref, ddtr_ref, dbias_ref):
        real = _real_rows(r, tm)
        _, ds = _silu_and_grad(xc_ref[...])
        up = lambda ref: ref[...].astype(F32)
        dxc_ref[:, :D_SSM] = jnp.where(
            real, (up(dxs_ref) + up(dsk_ref)) * ds[:, :D_SSM], 0.0).astype(dxc_ref.dtype)
        dxc_ref[:, D_SSM:D_SSM + 1024] = jnp.where(
            real, up(db_ref) * ds[:, D_SSM:D_SSM + 1024], 0.0).astype(dxc_ref.dtype)
        dxc_ref[:, D_SSM + 1024:] = jnp.where(
            real, up(dc_ref) * ds[:, D_SSM + 1024:], 0.0).astype(dxc_ref.dtype)
        dd = jnp.where(real, ddt_ref[...] * _sigmoid(dtr_ref[...] + b_ref[...]), 0.0)
        ddtr_ref[...] = dd.astype(ddtr_ref.dtype)
        _acc_add(first, dbias_ref, jnp.sum(dd, axis=0, keepdims=True))

    return _rowcall(name, body, lp, tm,
                    rows=[(xc, _XBC_W, 0), (proj, LANES, _DT_COL), (dxs, D_SSM, 0), (dxskip, D_SSM, 0),
                          (db, 1024, 0), (dc, 1024, 0), (ddt, LANES, 0)],
                    vecs=[dt_bias], outs=[(_XBC_W, BF16), (LANES, BF16)], accs=[((1, LANES), F32)])


def _ssd_common(dt, alog):
    a = -jnp.exp(alog)
    cs = _cumsum_rows(dt * a, BLOCK)
    cst = cs.T
    cl = cs[BLOCK - 1:BLOCK, :]
    tril = (lax.broadcasted_iota(jnp.int32, (BLOCK, BLOCK), 0)
            >= lax.broadcasted_iota(jnp.int32, (BLOCK, BLOCK), 1))
    return a, cs, cst, cl, jnp.exp(cs), jnp.exp(cl - cs), jnp.exp(cl), tril


def _head_cols(ecl, g):
    lane = lax.broadcasted_iota(jnp.int32, (1, SSD_HPG * SSD_P), 1)
    e = [ecl[:, SSD_HPG * g + hh:SSD_HPG * g + hh + 1] for hh in range(SSD_HPG)]
    return jnp.where(lane < SSD_P, e[0], jnp.where(lane < 2 * SSD_P, e[1],
                                                   jnp.where(lane < 3 * SSD_P, e[2], e[3])))


def _ssd_fwd(xbc, dt, alog, name, plan=None):
    lp = xbc.shape[0]
    nc = lp // BLOCK
    gw = SSD_HPG * SSD_P
    p_in, p_shapes, p_out, p_scr = _plan_parts(plan)

    def body(*refs):
        xs_ref, b_ref, c_ref, dt_ref, alog_ref = refs[:5]
        cins = refs[5:5 + len(p_in)]
        y_ref, so_ref = refs[5 + len(p_in):7 + len(p_in)]
        couts = refs[7 + len(p_in):7 + len(p_in) + len(p_out)]
        st, fx = refs[7 + len(p_in) + len(p_out):9 + len(p_in) + len(p_out)]
        sems = refs[9 + len(p_in) + len(p_out):]
        n = pl.program_id(0)

        @pl.when(n == 0)
        def _():
            st[...] = jnp.zeros_like(st)
            if plan is not None:
                plan.start(cins, couts, sems)

        dtv = dt_ref[...]
        _, cs, cst, cl, e, f, ecl, tril = _ssd_common(dtv, alog_ref[...])
        for g in range(SSD_GROUPS):
            bg = b_ref[:, g * SSD_N:(g + 1) * SSD_N].astype(BF16)
            cg = c_ref[:, g * SSD_N:(g + 1) * SSD_N].astype(BF16)
            gm = _dot_nt(cg, bg)
            stg = st[g]
            so_ref[0, g] = stg
            yoff = _dot(cg, stg.astype(BF16))
            heads = [SSD_HPG * g + hh for hh in range(SSD_HPG)]
            cols = lambda v: jnp.stack([v[:, h:h + 1] for h in heads])
            x4 = jnp.stack([xs_ref[:, h * SSD_P:(h + 1) * SSD_P] for h in heads])
            csr = jnp.stack([cst[h:h + 1, :] for h in heads])
            m = gm[None] * jnp.exp(jnp.where(tril[None], cols(cs) - csr, NEG))
            xdt = x4 * cols(dtv)
            yoff4 = jnp.stack([yoff[:, hh * SSD_P:(hh + 1) * SSD_P] for hh in range(SSD_HPG)])
            y4 = (jnp.einsum("hls,hsp->hlp", m.astype(BF16), xdt.astype(BF16), preferred_element_type=F32)
                  + cols(e) * yoff4)
            fx4 = cols(f) * xdt
            for hh, h in enumerate(heads):
                y_ref[:, h * SSD_P:(h + 1) * SSD_P] = y4[hh]
                fx[:, hh * SSD_P:(hh + 1) * SSD_P] = fx4[hh]
            st[g] = stg * _head_cols(ecl, g) + _dot_tn(bg, fx[...].astype(BF16))
        if plan is not None:
            @pl.when(n == nc - 1)
            def _():
                plan.wait(cins, couts, sems)

    res = pl.pallas_call(
        body, name=name, grid=(nc,),
        in_specs=[pl.BlockSpec((BLOCK, D_SSM), lambda n: (n, 0)),
                  pl.BlockSpec((BLOCK, 1024), lambda n: (n, 2)),
                  pl.BlockSpec((BLOCK, 1024), lambda n: (n, 3)),
                  pl.BlockSpec((BLOCK, LANES), lambda n: (n, 0)),
                  pl.BlockSpec((1, LANES), lambda n: (0, 0))] + p_in,
        out_specs=[pl.BlockSpec((BLOCK, D_SSM), lambda n: (n, 0)),
                   pl.BlockSpec((1, SSD_GROUPS, SSD_N, gw), lambda n: (n, 0, 0, 0))] + p_out,
        out_shape=[jax.ShapeDtypeStruct((lp, D_SSM), F32),
                   jax.ShapeDtypeStruct((nc, SSD_GROUPS, SSD_N, gw), F32)] + p_shapes,
        scratch_shapes=[pltpu.VMEM((SSD_GROUPS, SSD_N, gw), F32), pltpu.VMEM((BLOCK, gw), F32)] + p_scr,
        compiler_params=pltpu.CompilerParams(dimension_semantics=("arbitrary",)),
    )(xbc, xbc, xbc, dt, alog, *(plan.ins if plan is not None else []))
    return res[:2], res[2:]


def _ssd_bwd(xbc, dt, alog, states, dy, name, plan=None):
    lp = xbc.shape[0]
    nc = lp // BLOCK
    gw = SSD_HPG * SSD_P
    p_in, p_shapes, p_out, p_scr = _plan_parts(plan)

    def body(*refs):
        xs_ref, b_ref, c_ref, dt_ref, alog_ref, dy_ref, st_ref = refs[:7]
        cins = refs[7:7 + len(p_in)]
        dxs_ref, db_ref, dc_ref, ddt_ref, dalog_ref = refs[7 + len(p_in):12 + len(p_in)]
        couts = refs[12 + len(p_in):12 + len(p_in) + len(p_out)]
        dst, edy, fx = refs[12 + len(p_in) + len(p_out):15 + len(p_in) + len(p_out)]
        sems = refs[15 + len(p_in) + len(p_out):]
        i = pl.program_id(0)

        @pl.when(i == 0)
        def _():
            dst[...] = jnp.zeros_like(dst)
            dalog_ref[...] = jnp.zeros_like(dalog_ref)
            if plan is not None:
                plan.start(cins, couts, sems)

        dtv = dt_ref[...]
        a, cs, cst, cl, e, f, ecl, tril = _ssd_common(dtv, alog_ref[...])
        lane = lax.broadcasted_iota(jnp.int32, (1, LANES), 1)
        sub = _row_iota(BLOCK)
        triu = (lax.broadcasted_iota(jnp.int32, (BLOCK, BLOCK), 1)
                >= lax.broadcasted_iota(jnp.int32, (BLOCK, BLOCK), 0))
        dcs = jnp.zeros((BLOCK, LANES), F32)
        dcst = jnp.zeros((LANES, BLOCK), F32)
        dcl = jnp.zeros((1, LANES), F32)
        ddtx = jnp.zeros((BLOCK, LANES), F32)
        for g in range(SSD_GROUPS):
            bg = b_ref[:, g * SSD_N:(g + 1) * SSD_N].astype(BF16)
            cg = c_ref[:, g * SSD_N:(g + 1) * SSD_N].astype(BF16)
            gm = _dot_nt(cg, bg)
            stg = st_ref[0, g]
            stb = stg.astype(BF16)
            dso = dst[g]
            dsob = dso.astype(BF16)
            yraw = _dot(cg, stb)
            dfx = _dot(bg, dsob)
            prodsum = jnp.sum(dso * stg, axis=0, keepdims=True)
            heads = [SSD_HPG * g + hh for hh in range(SSD_HPG)]
            cols = lambda v: jnp.stack([v[:, h:h + 1] for h in heads])
            parts = lambda v: jnp.stack([v[:, hh * SSD_P:(hh + 1) * SSD_P] for hh in range(SSD_HPG)])
            x4 = jnp.stack([xs_ref[:, h * SSD_P:(h + 1) * SSD_P] for h in heads])
            dy4 = jnp.stack([dy_ref[:, h * SSD_P:(h + 1) * SSD_P] for h in heads])
            csc, dtc, ec, fc = cols(cs), cols(dtv), cols(e), cols(f)
            csr = jnp.stack([cst[h:h + 1, :] for h in heads])
            seg = csc - csr
            lam = jnp.exp(jnp.where(tril[None], seg, NEG))
            lamt = jnp.exp(jnp.where(triu[None], -seg, NEG))
            m = gm[None] * lam
            mt = _dot_nt(bg, cg)[None] * lamt
            xdt = x4 * dtc
            dyb = dy4.astype(BF16)
            dm = jnp.einsum("hlp,hsp->hls", dyb, xdt.astype(BF16), preferred_element_type=F32)
            dfx4 = parts(dfx)
            dxdt = jnp.einsum("hsl,hlp->hsp", mt.astype(BF16), dyb, preferred_element_type=F32) + fc * dfx4
            w = dm * m
            dgm = jnp.sum(dm * lam, axis=0)
            dff = jnp.sum(dfx4 * xdt, axis=2, keepdims=True) * fc
            colv = (jnp.sum(w, axis=2, keepdims=True)
                    + jnp.sum(dy4 * parts(yraw), axis=2, keepdims=True) * ec - dff)
            roww = jnp.sum(w, axis=1, keepdims=True)
            ddtc = jnp.sum(dxdt * x4, axis=2, keepdims=True)
            dffs = jnp.sum(dff, axis=1, keepdims=True)
            dxs4 = dxdt * dtc
            edy4 = ec * dy4
            fx4 = fc * xdt
            for hh, h in enumerate(heads):
                ls = slice(hh * SSD_P, (hh + 1) * SSD_P)
                onl = (lane == h).astype(F32)
                dcs = dcs + colv[hh] * onl
                dcst = dcst - (sub == h).astype(F32) * roww[hh]
                dcl = dcl + (dffs[hh] + ecl[:, h:h + 1] * jnp.sum(prodsum[:, ls], axis=1, keepdims=True)) * onl
                ddtx = ddtx + ddtc[hh] * onl
                dxs_ref[:, h * SSD_P:(h + 1) * SSD_P] = dxs4[hh].astype(dxs_ref.dtype)
                edy[:, ls] = edy4[hh]
                fx[:, ls] = fx4[hh]
            edyb = edy[...].astype(BF16)
            fxb = fx[...].astype(BF16)
            dgb = dgm.astype(BF16)
            dc_ref[:, g * SSD_N:(g + 1) * SSD_N] = (_dot_nt(edyb, stb) + _dot(dgb, bg)).astype(dc_ref.dtype)
            db_ref[:, g * SSD_N:(g + 1) * SSD_N] = (_dot_nt(fxb, dsob) + _dot_tn(dgb, cg)).astype(db_ref.dtype)
            dst[g] = dso * _head_cols(ecl, g) + _dot_tn(cg, edyb)
        dcs = dcs + dcst.T + jnp.where(sub == BLOCK - 1, dcl, 0.0)
        dda = _rev_cumsum_rows(dcs, BLOCK)
        ddt_ref[...] = ddtx + dda * a
        dalog_ref[...] += jnp.sum(dda * dtv, axis=0, keepdims=True) * a
        if plan is not None:
            @pl.when(i == nc - 1)
            def _():
                plan.wait(cins, couts, sems)

    rev = lambda i: nc - 1 - i
    res = pl.pallas_call(
        body, name=name, grid=(nc,),
        in_specs=[pl.BlockSpec((BLOCK, D_SSM), lambda i: (rev(i), 0)),
                  pl.BlockSpec((BLOCK, 1024), lambda i: (rev(i), 2)),
                  pl.BlockSpec((BLOCK, 1024), lambda i: (rev(i), 3)),
                  pl.BlockSpec((BLOCK, LANES), lambda i: (rev(i), 0)),
                  pl.BlockSpec((1, LANES), lambda i: (0, 0)),
                  pl.BlockSpec((BLOCK, D_SSM), lambda i: (rev(i), 0)),
                  pl.BlockSpec((1, SSD_GROUPS, SSD_N, gw), lambda i: (rev(i), 0, 0, 0))] + p_in,
        out_specs=[pl.BlockSpec((BLOCK, D_SSM), lambda i: (rev(i), 0)),
                   pl.BlockSpec((BLOCK, 1024), lambda i: (rev(i), 0)),
                   pl.BlockSpec((BLOCK, 1024), lambda i: (rev(i), 0)),
                   pl.BlockSpec((BLOCK, LANES), lambda i: (rev(i), 0)),
                   pl.BlockSpec((1, LANES), lambda i: (0, 0))] + p_out,
        out_shape=[jax.ShapeDtypeStruct((lp, D_SSM), BF16),
                   jax.ShapeDtypeStruct((lp, 1024), BF16),
                   jax.ShapeDtypeStruct((lp, 1024), BF16),
                   jax.ShapeDtypeStruct((lp, LANES), F32),
                   jax.ShapeDtypeStruct((1, LANES), F32)] + p_shapes,
        scratch_shapes=[pltpu.VMEM((SSD_GROUPS, SSD_N, gw), F32),
                        pltpu.VMEM((BLOCK, gw), F32), pltpu.VMEM((BLOCK, gw), F32)] + p_scr,
        compiler_params=pltpu.CompilerParams(dimension_semantics=("arbitrary",)),
    )(xbc, xbc, xbc, dt, alog, dy, states, *(plan.ins if plan is not None else []))
    return res[:5], res[5:]


_GN_GROUPS = 8
_GN_W = D_SSM // _GN_GROUPS


def _ssm_gate_fwd(yssd, xbc, proj, dskip, gnorm, name):
    lp = yssd.shape[0]
    tm = _pick(lp, (320, 256, 128))

    def body(r, first, y_ref, x_ref, z_ref, d_ref, g_ref, o_ref):
        sz, _ = _silu_and_grad(z_ref[...])
        y2 = (y_ref[...] + d_ref[...] * x_ref[...]) * sz
        for k in range(_GN_GROUPS):
            sl = slice(k * _GN_W, (k + 1) * _GN_W)
            yk = y2[:, sl]
            rs = lax.rsqrt(jnp.mean(yk * yk, axis=-1, keepdims=True) + EPS)
            o_ref[:, sl] = (yk * rs * g_ref[:, sl]).astype(o_ref.dtype)

    return _rowcall(name, body, lp, tm, rows=[(yssd, D_SSM, 0), (xbc, D_SSM, 0), (proj, D_SSM, 0)],
                    vecs=[dskip, gnorm], outs=[(D_SSM, BF16)])[0]


def _ssm_gate_bwd(yssd, xbc, proj, dskip, gnorm, dyn, name):
    lp = yssd.shape[0]
    tm = BLOCK

    def body(r, first, y_ref, x_ref, z_ref, dyn_ref, d_ref, g_ref,
             dy_ref, dx_ref, dz_ref, dd_ref, dg_ref):
        z = z_ref[...]
        sz, dsz = _silu_and_grad(z)
        xs = x_ref[...]
        y1 = y_ref[...] + d_ref[...] * xs
        y2 = y1 * sz
        dyn = dyn_ref[...]
        for k in range(_GN_GROUPS):
            sl = slice(k * _GN_W, (k + 1) * _GN_W)
            dx, dgt = _rms_bwd(y2[:, sl], g_ref[:, sl], dyn[:, sl])
            dy1 = dx * sz[:, sl]
            dy_ref[:, sl] = dy1.astype(dy_ref.dtype)
            dx_ref[:, sl] = (dy1 * d_ref[:, sl]).astype(dx_ref.dtype)
            dz_ref[:, sl] = (dx * y1[:, sl] * dsz[:, sl]).astype(dz_ref.dtype)

            @pl.when(first)
            def _():
                dd_ref[:, sl] = jnp.zeros((1, _GN_W), F32)
                dg_ref[:, sl] = jnp.zeros((1, _GN_W), F32)

            dd_ref[:, sl] += jnp.sum(dy1 * xs[:, sl], axis=0, keepdims=True)
            dg_ref[:, sl] += jnp.sum(dgt, axis=0, keepdims=True)

    return _rowcall(name, body, lp, tm,
                    rows=[(yssd, D_SSM, 0), (xbc, D_SSM, 0), (proj, D_SSM, 0), (dyn, D_SSM, 0)],
                    vecs=[dskip, gnorm], outs=[(D_SSM, BF16), (D_SSM, BF16), (D_SSM, BF16)],
                    accs=[((1, D_SSM), F32), ((1, D_SSM), F32)])


def _adamw(w, g, m, v, name):
    r, c = w.shape
    tm = r if r <= 512 else _pick(r, (512, 352, 256, 128, 64, 32, 16, 8))
    c1 = 1.0 / (1.0 - ADAM_B1 ** ADAM_STEP)
    c2 = 1.0 / (1.0 - ADAM_B2 ** ADAM_STEP)

    def body(w_ref, g_ref, m_ref, v_ref, d_ref, nm_ref, nv_ref):
        gv = g_ref[...]
        nm = ADAM_B1 * m_ref[...] + (1.0 - ADAM_B1) * gv
        nv = ADAM_B2 * v_ref[...] + (1.0 - ADAM_B2) * (gv * gv)
        nm_ref[...] = nm
        nv_ref[...] = nv
        d_ref[...] = -ADAM_LR * ((nm * c1) / (jnp.sqrt(nv * c2) + ADAM_EPS) + ADAM_WD * w_ref[...])

    spec = pl.BlockSpec((tm, c), lambda i: (i, 0))
    return pl.pallas_call(
        body, name=name, grid=(r // tm,), in_specs=[spec] * 4, out_specs=[spec] * 3,
        out_shape=[jax.ShapeDtypeStruct((r, c), F32)] * 3,
        compiler_params=pltpu.CompilerParams(dimension_semantics=("parallel",)),
    )(w, g, m, v)


def _place():
    return lax.axis_index("x"), lax.axis_index("y"), lax.axis_index("c")


def _other_chips(x, y):
    return [(1 - x, y), (x, 1 - y), (1 - x, 1 - y)]


_ANY = pl.BlockSpec(memory_space=pl.ANY)


class _Plan:
    def __init__(self, ins, out_shapes, n_remote, n_local, issue):
        self.ins = list(ins)
        self.out_shapes = list(out_shapes)
        self.issue = issue
        self.scratch = [pltpu.SemaphoreType.DMA((max(n_remote, 1),)),
                        pltpu.SemaphoreType.DMA((max(n_remote, 1),)),
                        pltpu.SemaphoreType.DMA((max(n_local, 1),))]

    def start(self, ins, outs, sems):
        sends, _, locs = self.issue(ins, outs, *sems)
        for cp in locs + sends:
            cp.start()

    def wait(self, ins, outs, sems):
        sends, recvs, locs = self.issue(ins, outs, *sems)
        for make in recvs:
            make().wait_recv()
        for cp in sends:
            cp.wait_send()
        for cp in locs:
            cp.wait()


def _plan_parts(plan):
    if plan is None:
        return [], [], [], []
    return ([_ANY] * len(plan.ins), plan.out_shapes, [_ANY] * len(plan.out_shapes), plan.scratch)


def _run_plan(plan, name):
    n_in, n_out = len(plan.ins), len(plan.out_shapes)

    def body(*refs):
        ins, outs, sems = refs[:n_in], refs[n_in:n_in + n_out], refs[n_in + n_out:]
        plan.start(ins, outs, sems)
        plan.wait(ins, outs, sems)

    return pl.pallas_call(
        body, name=name, in_specs=[_ANY] * n_in, out_specs=[_ANY] * n_out,
        out_shape=plan.out_shapes, scratch_shapes=plan.scratch,
    )(*plan.ins)


def _gather_plan(shards):
    n = len(shards)

    def issue(ins, outs, send_sems, recv_sems, local_sems):
        x, y, c = _place()
        me = 2 * x + y
        sends, recvs, locs = [], [], []
        for p in range(n):
            locs.append(pltpu.make_async_copy(ins[p], outs[p].at[me], local_sems.at[p]))
            for k, (px, py) in enumerate(_other_chips(x, y)):
                sems = dict(send_sem=send_sems.at[3 * p + k], recv_sem=recv_sems.at[3 * p + k],
                            device_id=(px, py, c), device_id_type=MESH)
                sends.append(pltpu.make_async_remote_copy(src_ref=ins[p], dst_ref=outs[p].at[me], **sems))
                recvs.append(functools.partial(pltpu.make_async_remote_copy, src_ref=ins[p],
                                               dst_ref=outs[p].at[2 * px + py], **sems))
        return sends, recvs, locs

    return _Plan(shards, [jax.ShapeDtypeStruct((N_CHIPS,) + s.shape, s.dtype) for s in shards], 3 * n, n, issue)


_REL7 = [(fx, fy, fc) for fx in (0, 1) for fy in (0, 1) for fc in (0, 1)][1:]


def _scatter8_plan(gs):
    n = len(gs)

    def issue(ins, outs, send_sems, recv_sems, local_sems):
        x, y, c = _place()
        sends = []
        for p in range(n):
            hr = gs[p].shape[1] // 2
            for k, (fx, fy, fc) in enumerate(_REL7):
                tx, ty, tc = x ^ fx, y ^ fy, c ^ fc
                src = ins[p].at[2 * tx + ty, pl.ds(pl.multiple_of(tc * hr, SUBLANES), hr), :]
                sends.append(pltpu.make_async_remote_copy(
                    src_ref=src, dst_ref=outs[p].at[k],
                    send_sem=send_sems.at[7 * p + k], recv_sem=recv_sems.at[7 * p + k],
                    device_id=(tx, ty, tc), device_id_type=MESH))
        return sends, [functools.partial(lambda cp: cp, cp) for cp in sends], []

    shapes = [jax.ShapeDtypeStruct((7, g.shape[1] // 2, g.shape[2]), g.dtype) for g in gs]
    return _Plan(gs, shapes, 7 * n, 0, issue)


def _sibling_plan(ts):
    n = len(ts)

    def issue(ins, outs, send_sems, recv_sems, local_sems):
        x, y, c = _place()
        sends = [pltpu.make_async_remote_copy(
            src_ref=ins[p], dst_ref=outs[p], send_sem=send_sems.at[p], recv_sem=recv_sems.at[p],
            device_id=(x, y, 1 - c), device_id_type=MESH) for p in range(n)]
        return sends, [functools.partial(lambda cp: cp, cp) for cp in sends], []

    return _Plan(ts, [jax.ShapeDtypeStruct(t.shape, t.dtype) for t in ts], n, 0, issue)


def _add8(g, recv, chip, core, name):
    s, r, n = g.shape
    hr = r // 2
    th = hr // 2 if (hr // 2) % SUBLANES == 0 else hr
    nt = hr // th

    def body(chip_ref, core_ref, g_ref, r_ref, o_ref):
        acc = g_ref[0].astype(F32)
        for k in range(7):
            acc = acc + r_ref[k].astype(F32)
        o_ref[...] = acc

    return pl.pallas_call(
        body, name=name,
        grid_spec=pltpu.PrefetchScalarGridSpec(
            num_scalar_prefetch=2, grid=(nt,),
            in_specs=[pl.BlockSpec((1, th, n), lambda i, ch, co: (ch[0], co[0] * nt + i, 0)),
                      pl.BlockSpec((7, th, n), lambda i, ch, co: (0, i, 0))],
            out_specs=pl.BlockSpec((th, n), lambda i, ch, co: (i, 0))),
        out_shape=jax.ShapeDtypeStruct((hr, n), F32),
        compiler_params=pltpu.CompilerParams(dimension_semantics=("parallel",)),
    )(chip, core, g, recv)


def _adamw_halves(w, own, other, m, v, core, name):
    r, n = w.shape
    hr = r // 2
    th = hr // 2 if (hr // 2) % SUBLANES == 0 else hr
    tph = hr // th
    c1 = 1.0 / (1.0 - ADAM_B1 ** ADAM_STEP)
    c2 = 1.0 / (1.0 - ADAM_B2 ** ADAM_STEP)

    def body(core_ref, w_ref, a_ref, b_ref, m_ref, v_ref, g_ref, d_ref, nm_ref, nv_ref):
        half = pl.program_id(0) // tph
        gv = jnp.where(half == core_ref[0], a_ref[...], b_ref[...])
        nm = ADAM_B1 * m_ref[...] + (1.0 - ADAM_B1) * gv
        nv = ADAM_B2 * v_ref[...] + (1.0 - ADAM_B2) * (gv * gv)
        g_ref[...] = gv
        nm_ref[...] = nm
        nv_ref[...] = nv
        d_ref[...] = -ADAM_LR * ((nm * c1) / (jnp.sqrt(nv * c2) + ADAM_EPS) + ADAM_WD * w_ref[...])

    full = pl.BlockSpec((th, n), lambda i, co: (i, 0))
    part = pl.BlockSpec((th, n), lambda i, co: (i % tph, 0))
    return pl.pallas_call(
        body, name=name,
        grid_spec=pltpu.PrefetchScalarGridSpec(
            num_scalar_prefetch=1, grid=(2 * tph,),
            in_specs=[full, part, part, full, full], out_specs=[full] * 4),
        out_shape=[jax.ShapeDtypeStruct((r, n), F32)] * 4,
        compiler_params=pltpu.CompilerParams(dimension_semantics=("parallel",)),
    )(core, w, own, other, m, v)


def _allreduce_small(pack, name):
    r, l = pack.shape

    def body(p_ref, o_ref, land, send_sems, recv_sems):
        x, y, c = _place()
        me = 4 * x + 2 * y + c
        land[me] = p_ref[...]
        rel = [(fx, fy, fc) for fx in (0, 1) for fy in (0, 1) for fc in (0, 1)][1:]
        sends = []
        for k, (fx, fy, fc) in enumerate(rel):
            peer = (x ^ fx, y ^ fy, c ^ fc)
            cp = pltpu.make_async_remote_copy(
                src_ref=p_ref, dst_ref=land.at[me], send_sem=send_sems.at[k], recv_sem=recv_sems.at[k],
                device_id=peer, device_id_type=MESH)
            cp.start()
            sends.append(cp)
        for k, (fx, fy, fc) in enumerate(rel):
            src = 4 * (x ^ fx) + 2 * (y ^ fy) + (c ^ fc)
            pltpu.make_async_remote_copy(
                src_ref=p_ref, dst_ref=land.at[src], send_sem=send_sems.at[k], recv_sem=recv_sems.at[k],
                device_id=(x ^ fx, y ^ fy, c ^ fc), device_id_type=MESH).wait_recv()
        for cp in sends:
            cp.wait_send()
        acc = land[0]
        for d in range(1, N_DEV):
            acc = acc + land[d]
        o_ref[...] = acc

    vm = pl.BlockSpec(memory_space=pltpu.VMEM)
    return pl.pallas_call(
        body, name=name, in_specs=[vm], out_specs=vm,
        out_shape=jax.ShapeDtypeStruct((r, l), F32),
        scratch_shapes=[pltpu.VMEM((N_DEV, r, l), F32),
                        pltpu.SemaphoreType.DMA((N_DEV - 1,)), pltpu.SemaphoreType.DMA((N_DEV - 1,))],
    )(pack)


def _flat_rows(a, mult=SUBLANES * LANES):
    f = a.reshape(-1)
    padn = (-f.shape[0]) % mult
    if padn:
        f = jnp.concatenate([f, jnp.zeros((padn,), f.dtype)])
    return f


def _pack(arrs, mult=SUBLANES * LANES, total_mult=None):
    flat = [_flat_rows(a, mult) for a in arrs]
    sizes = [f.shape[0] for f in flat]
    if total_mult is not None:
        padn = (-sum(sizes)) % total_mult
        if padn:
            flat.append(jnp.zeros((padn,), flat[0].dtype))
    return jnp.concatenate(flat).reshape(-1, LANES), sizes


def _unpack(pack, shapes, sizes, lead=()):
    flat = pack.reshape(lead + (-1,))
    out, off = [], 0
    for shp, sz in zip(shapes, sizes):
        n = math.prod(shp)
        out.append(flat[..., off:off + n].reshape(lead + tuple(shp)))
        off += sz
    return out


def _cols_from_shards(g):
    s, k, n = g.shape
    return jnp.transpose(g, (1, 0, 2)).reshape(k, s * n)


def _cols_to_shards(w, s=N_CHIPS):
    k, n = w.shape
    return jnp.transpose(w.reshape(k, s, n // s), (1, 0, 2))


def _ffn_fwd(h, pre, post, w_up, cw, cb, w_down, tag):
    u = _rmsnorm_fwd(h, pre, f"{tag}_prenorm")
    hp = _mm_nn_sh(u, w_up, 2 * D_FF, f"{tag}_up")
    act, hg, hu = _ffn_convact_fwd(hp, cw, cb, f"{tag}_convact")
    o = _mm_nn(act, w_down, F32, f"{tag}_down")
    hn = _postnorm_res_fwd(h, o, post, f"{tag}_postnorm")
    return hn, (h, u, hp, hg, hu, act, o)


def _ffn_bwd(dh, saved, pre, post, w_up, cw, w_down, tag):
    h, u, hp, hg, hu, act, o = saved
    do, dpost = _postnorm_bwd(o, post, dh, f"{tag}_postnorm_bwd")
    dact = _mm_nt(do, w_down, f"{tag}_down_dx")
    dw_down = _mm_tn(act, do, f"{tag}_down_dw")
    dhg, dhu = _ffn_act_bwd(hg, hu, dact, f"{tag}_act_bwd")
    dxg, dwg, dbg = _conv_bwd(hp, 0, D_FF, dhg, cw, f"{tag}_conv_bwd_gate")
    dxu, dwu, dbu = _conv_bwd(hp, D_FF, D_FF, dhu, cw, f"{tag}_conv_bwd_up", w_col_off=D_FF)
    dhp = (dxg, dxu)
    dcw = jnp.concatenate([dwg, dwu], axis=1)
    dcb = jnp.concatenate([dbg, dbu], axis=1)
    du = _mm_nt_sh(dhp, w_up, f"{tag}_up_dx")
    dw_up = _mm_tn_sh(u, dhp, w_up.shape[2], f"{tag}_up_dw")
    dhn, dpre = _prenorm_bwd(h, pre, du, dh, f"{tag}_prenorm_bwd")
    return dhn, dict(pre=dpre, post=dpost, w_up=dw_up, conv_w=dcw[:3], conv_b=dcb, w_down=dw_down)


class _Exchange:
    GATHER_IN_ATTN = ("l0_w_out", "l0_ffn_w_up", "l0_ffn_w_down", "l1_w_in")
    GATHER_IN_SSD = ("l1_w_out", "l1_ffn_w_up", "l1_ffn_w_down")
    AFTER_L1_OUT = ("l1_ffn_w_up", "l1_ffn_w_down", "l1_w_out")
    AFTER_L0_OUT = ("l1_w_in", "l0_ffn_w_up", "l0_ffn_w_down", "l0_w_out")
    LAST = ("l0_w_in",)

    def __init__(self, late_shards):
        self.late = dict(late_shards)
        self.slabs = {}
        self.recv = {}

    def gather_plan(self, names):
        return _gather_plan([self.late[n] for n in names])

    def gathered(self, names, outs):
        return {n: (g if n in _BIG_COL else g.reshape(-1, g.shape[-1])) for n, g in zip(names, outs)}

    def scatter_plan(self, grads, names):
        for n in names:
            g = grads[n]
            self.slabs[n] = g if n in _BIG_COL else g.reshape(N_CHIPS, -1, g.shape[-1])
        return _scatter8_plan([self.slabs[n] for n in names])

    def scattered(self, names, outs):
        self.recv.update(zip(names, outs))


def _local_step(x, tgt, meta, P, ex=None):
    seq, d = x.shape
    lp = seq + BLOCK
    h0 = jnp.concatenate([jnp.zeros((PAD, d), F32), meta, x], axis=0)
    tgt_p = jnp.concatenate([jnp.zeros((BLOCK, d), F32), tgt], axis=0)

    u0 = _rmsnorm_fwd(h0, P["l0_mix_pre_norm"], "l0_mix_prenorm")
    proj0 = _mm_nn_sh(u0, P["l0_w_in"], EVEN_IN, "l0_in")
    xrc = _conv_fwd(proj0, D_MODEL, D_MODEL, P["l0_lru_conv_w"], P["l0_lru_conv_b"], "l0_lru_conv")
    lru_args = (P["l0_lru_w_a"], P["l0_lru_w_x"], P["l0_lru_b_a"], P["l0_lru_b_x"], P["l0_lru_lambda"])
    ya, hl = _lru_fwd(proj0, xrc, *lru_args, "l0_lru")
    yb, outs = _attn_fwd(proj0, P["l0_attn_sinks"], "l0_attn",
                         ex.gather_plan(ex.GATHER_IN_ATTN) if ex else None)
    if ex:
        P = {**P, **ex.gathered(ex.GATHER_IN_ATTN, outs)}
    o0 = _mm_nn((ya, yb), P["l0_w_out"], F32, "l0_out")
    h1 = _postnorm_res_fwd(h0, o0, P["l0_mix_post_norm"], "l0_mix_postnorm")
    h2, ffn0 = _ffn_fwd(h1, P["l0_ffn_pre_norm"], P["l0_ffn_post_norm"], P["l0_ffn_w_up"],
                        P["l0_ffn_conv_w"], P["l0_ffn_conv_b"], P["l0_ffn_w_down"], "l0_ffn")
    u2 = _rmsnorm_fwd(h2, P["l1_mix_pre_norm"], "l1_mix_prenorm")
    proj1 = _mm_nn_sh(u2, P["l1_w_in"], ODD_IN_PAD, "l1_in")
    xc1 = _conv_fwd(proj1, _ZW, _XBC_W, P["l1_ssm_conv_w"], P["l1_ssm_conv_b"], "l1_ssm_conv")
    xbc, dt = _ssm_prep_fwd(xc1, proj1, P["l1_dt_bias"], "l1_ssm_prep")
    (yssd, states), outs = _ssd_fwd(xbc, dt, P["l1_a_log"], "l1_ssd",
                                    ex.gather_plan(ex.GATHER_IN_SSD) if ex else None)
    if ex:
        P = {**P, **ex.gathered(ex.GATHER_IN_SSD, outs)}
    yn = _ssm_gate_fwd(yssd, xbc, proj1, P["l1_d_skip"], P["l1_gate_norm"], "l1_ssm_gate")
    o1 = _mm_nn(yn, P["l1_w_out"], F32, "l1_out")
    h3 = _postnorm_res_fwd(h2, o1, P["l1_mix_post_norm"], "l1_mix_postnorm")
    h4, ffn1 = _ffn_fwd(h3, P["l1_ffn_pre_norm"], P["l1_ffn_post_norm"], P["l1_ffn_w_up"],
                        P["l1_ffn_conv_w"], P["l1_ffn_conv_b"], P["l1_ffn_w_down"], "l1_ffn")
    dh4, loss_cols = _loss_fwd_bwd(h4, tgt_p, "loss")

    G = {}
    dh3, g = _ffn_bwd(dh4, ffn1, P["l1_ffn_pre_norm"], P["l1_ffn_post_norm"], P["l1_ffn_w_up"],
                      P["l1_ffn_conv_w"], P["l1_ffn_w_down"], "l1_ffn")
    for k, v in g.items():
        G["l1_ffn_" + (k + "_norm" if k in ("pre", "post") else k)] = v
    do1, G["l1_mix_post_norm"] = _postnorm_bwd(o1, P["l1_mix_post_norm"], dh3, "l1_mix_postnorm_bwd")
    dyn = _mm_nt(do1, P["l1_w_out"], "l1_out_dx")
    G["l1_w_out"] = _mm_tn(yn, do1, "l1_out_dw")
    dyssd, dxskip, dz, dd_cols, G["l1_gate_norm"] = _ssm_gate_bwd(
        yssd, xbc, proj1, P["l1_d_skip"], P["l1_gate_norm"], dyn, "l1_ssm_gate_bwd")
    G["l1_d_skip"] = dd_cols.reshape(SSD_HEADS, SSD_P).sum(axis=1)
    (dxs, dbm, dcm, ddt, dalog), outs = _ssd_bwd(
        xbc, dt, P["l1_a_log"], states, dyssd, "l1_ssd_bwd",
        ex.scatter_plan(G, ex.AFTER_L1_OUT) if ex else None)
    if ex:
        ex.scattered(ex.AFTER_L1_OUT, outs)
    G["l1_a_log"] = dalog[0, :SSD_HEADS]
    dxc, ddtr, dbias = _ssm_prep_bwd(xc1, proj1, P["l1_dt_bias"], dxs, dxskip, dbm, dcm, ddt,
                                     "l1_ssm_prep_bwd")
    G["l1_dt_bias"] = dbias[0, :SSD_HEADS]
    dxbc, dcw, dcb = _conv_bwd(proj1, _ZW, _XBC_W, dxc, P["l1_ssm_conv_w"], "l1_ssm_conv_bwd")
    G["l1_ssm_conv_w"] = dcw[:4]
    G["l1_ssm_conv_b"] = dcb
    dproj1 = jnp.concatenate([dz, dxbc, ddtr], axis=1)
    du2 = _mm_nt_sh(dproj1, P["l1_w_in"], "l1_in_dx")
    G["l1_w_in"] = _mm_tn_sh(u2, dproj1, ODD_IN // N_CHIPS, "l1_in_dw")
    dh2, G["l1_mix_pre_norm"] = _prenorm_bwd(h2, P["l1_mix_pre_norm"], du2, dh3, "l1_mix_prenorm_bwd")
    dh1, g = _ffn_bwd(dh2, ffn0, P["l0_ffn_pre_norm"], P["l0_ffn_post_norm"], P["l0_ffn_w_up"],
                      P["l0_ffn_conv_w"], P["l0_ffn_w_down"], "l0_ffn")
    for k, v in g.items():
        G["l0_ffn_" + (k + "_norm" if k in ("pre", "post") else k)] = v
    do0, G["l0_mix_post_norm"] = _postnorm_bwd(o0, P["l0_mix_post_norm"], dh1, "l0_mix_postnorm_bwd")
    dmix = _mm_nt(do0, P["l0_w_out"], "l0_out_dx")
    G["l0_w_out"] = jnp.concatenate([_mm_tn(ya, do0, "l0_out_dw_lru"), _mm_tn(yb, do0, "l0_out_dw_attn")], axis=0)
    (dgate, dxrc, G["l0_lru_w_a"], G["l0_lru_w_x"], G["l0_lru_b_a"], G["l0_lru_b_x"],
     G["l0_lru_lambda"]) = _lru_bwd(proj0, xrc, hl, dmix, *lru_args, "l0_lru_bwd")
    dxr, dcw, dcb = _conv_bwd(proj0, D_MODEL, D_MODEL, dxrc, P["l0_lru_conv_w"], "l0_lru_conv_bwd")
    G["l0_lru_conv_w"] = dcw[:4]
    G["l0_lru_conv_b"] = dcb
    (dq, dk, dv, G["l0_attn_sinks"]), outs = _attn_bwd(
        proj0, P["l0_attn_sinks"], dmix, "l0_attn_bwd",
        ex.scatter_plan(G, ex.AFTER_L0_OUT) if ex else None)
    if ex:
        ex.scattered(ex.AFTER_L0_OUT, outs)
    dproj0 = jnp.concatenate([dgate, dxr, dq, dk.astype(BF16), dv.astype(BF16)], axis=1)
    du0 = _mm_nt_sh(dproj0, P["l0_w_in"], "l0_in_dx")
    G["l0_w_in"] = _mm_tn_sh(u0, dproj0, EVEN_IN // N_CHIPS, "l0_in_dw")
    dh0, G["l0_mix_pre_norm"] = _prenorm_bwd(h0, P["l0_mix_pre_norm"], du0, dh1, "l0_mix_prenorm_bwd")
    return loss_cols, dh0[BLOCK:], dh0[PAD:BLOCK], G


_BIG_COL = ("l0_w_in", "l0_ffn_w_up", "l1_w_in", "l1_ffn_w_up")
_BIG_ROW = ("l0_w_out", "l0_ffn_w_down", "l1_w_out", "l1_ffn_w_down")
_BIG = ("l0_w_in", "l0_w_out", "l0_ffn_w_up", "l0_ffn_w_down",
        "l1_w_in", "l1_w_out", "l1_ffn_w_up", "l1_ffn_w_down")
_SMALL_SHARDED = ("meta_tokens", "l0_lru_conv_w", "l0_ffn_conv_w", "l1_ssm_conv_w", "l1_ffn_conv_w")
_WEIGHTS = ("meta_tokens", "l0_mix_pre_norm", "l0_mix_post_norm", "l0_w_in", "l0_lru_conv_w",
            "l0_lru_conv_b", "l0_lru_w_a", "l0_lru_b_a", "l0_lru_w_x", "l0_lru_b_x", "l0_lru_lambda",
            "l0_attn_sinks", "l0_w_out", "l0_ffn_pre_norm", "l0_ffn_post_norm", "l0_ffn_w_up",
            "l0_ffn_conv_w", "l0_ffn_conv_b", "l0_ffn_w_down", "l1_mix_pre_norm", "l1_mix_post_norm",
            "l1_w_in", "l1_ssm_conv_w", "l1_ssm_conv_b", "l1_dt_bias", "l1_a_log", "l1_d_skip",
            "l1_gate_norm", "l1_w_out", "l1_ffn_pre_norm", "l1_ffn_post_norm", "l1_ffn_w_up",
            "l1_ffn_conv_w", "l1_ffn_conv_b", "l1_ffn_w_down")
_REPL = tuple(n for n in _WEIGHTS if n not in _BIG and n not in _SMALL_SHARDED)


def _pad_lanes(v, n=LANES):
    return jnp.concatenate([v, jnp.zeros((n - v.shape[0],), v.dtype)]).reshape(1, n)


def _step(x, tgt, W, M, V):
    cx, cy, cc = _place()
    chip = 2 * cx + cy

    small_pack, small_sizes = _pack([W[n] for n in _SMALL_SHARDED])
    first = _run_plan(_gather_plan([W["l0_w_in"].astype(BF16), small_pack]), "gather_first")
    small_full = _unpack(first[1], [W[n].shape for n in _SMALL_SHARDED], small_sizes, lead=(N_CHIPS,))
    ex = _Exchange({n: W[n].astype(BF16) for n in _BIG if n != "l0_w_in"})

    P = {"l0_w_in": first[0]}
    for n, g in zip(_SMALL_SHARDED, small_full):
        P[n] = _cols_from_shards(g)
    for n in _REPL:
        v = W[n]
        P[n] = v.reshape(1, -1) if v.ndim == 1 else v
    P["l0_lru_w_a"] = W["l0_lru_w_a"].astype(BF16)
    P["l0_lru_w_x"] = W["l0_lru_w_x"].astype(BF16)
    P["l1_dt_bias"] = _pad_lanes(W["l1_dt_bias"])
    P["l1_a_log"] = _pad_lanes(W["l1_a_log"])
    P["l1_d_skip"] = jnp.repeat(W["l1_d_skip"], SSD_P).reshape(1, D_SSM)
    meta = P.pop("meta_tokens")

    loss_cols, grad_x, grad_meta, G = _local_step(x, tgt, meta, P, ex)
    G["meta_tokens"] = grad_meta

    ex.scattered(ex.LAST, _run_plan(ex.scatter_plan(G, ex.LAST), "grad_scatter_last"))
    core_idx = cc.astype(jnp.int32).reshape(1)
    chip_idx = chip.astype(jnp.int32).reshape(1)
    own_half = [_add8(ex.slabs[n], ex.recv[n], chip_idx, core_idx, f"grad_sum_{n}") for n in _BIG]
    other_half = _run_plan(_sibling_plan(own_half), "grad_sibling_swap")
    small_names = list(_REPL) + list(_SMALL_SHARDED)
    small_list = [G[n] for n in small_names] + [loss_cols]
    spack, ssizes = _pack(small_list)
    sred = _allreduce_small(spack, "small_allreduce")
    sfull = _unpack(sred, [a.shape for a in small_list], ssizes)
    loss = 0.5 / D_MODEL * jnp.sum(sfull[-1])
    small_grads = {}
    for n, g in zip(small_names, sfull[:-1]):
        if n in _SMALL_SHARDED:
            wcols = W[n].shape[1]
            g = lax.dynamic_slice_in_dim(g, chip * wcols, wcols, axis=1)
        small_grads[n] = g.reshape(W[n].shape)

    grads, delta, new_m, new_v = {}, {}, {}, {}
    for n, own, other in zip(_BIG, own_half, other_half):
        grads[n], delta[n], new_m[n], new_v[n] = _adamw_halves(
            W[n], own, other, M[n], V[n], core_idx, f"adamw_{n}")
    s_names = [n for n in _WEIGHTS if n not in _BIG]
    tile_elems = 512 * LANES
    wp, wsz = _pack([W[n] for n in s_names], total_mult=tile_elems)
    gp, _ = _pack([small_grads[n] for n in s_names], total_mult=tile_elems)
    mp, _ = _pack([M[n] for n in s_names], total_mult=tile_elems)
    vp, _ = _pack([V[n] for n in s_names], total_mult=tile_elems)
    dp, nmp, nvp = _adamw(wp, gp, mp, vp, "adamw_small")
    shapes = [W[n].shape for n in s_names]
    for n, a, b, c_ in zip(s_names, _unpack(dp, shapes, wsz), _unpack(nmp, shapes, wsz),
                           _unpack(nvp, shapes, wsz)):
        grads[n] = small_grads[n]
        delta[n], new_m[n], new_v[n] = a, b, c_
    return loss, grad_x, grads, delta, new_m, new_v


def kernel(x, meta_tokens, l0_mix_pre_norm, l0_mix_post_norm, l0_w_in, l0_lru_conv_w, l0_lru_conv_b, l0_lru_w_a, l0_lru_b_a, l0_lru_w_x, l0_lru_b_x, l0_lru_lambda, l0_attn_sinks, l0_w_out, l0_ffn_pre_norm, l0_ffn_post_norm, l0_ffn_w_up, l0_ffn_conv_w, l0_ffn_conv_b, l0_ffn_w_down, l1_mix_pre_norm, l1_mix_post_norm, l1_w_in, l1_ssm_conv_w, l1_ssm_conv_b, l1_dt_bias, l1_a_log, l1_d_skip, l1_gate_norm, l1_w_out, l1_ffn_pre_norm, l1_ffn_post_norm, l1_ffn_w_up, l1_ffn_conv_w, l1_ffn_conv_b, l1_ffn_w_down, loss_target, m_meta_tokens, m_l0_mix_pre_norm, m_l0_mix_post_norm, m_l0_w_in, m_l0_lru_conv_w, m_l0_lru_conv_b, m_l0_lru_w_a, m_l0_lru_b_a, m_l0_lru_w_x, m_l0_lru_b_x, m_l0_lru_lambda, m_l0_attn_sinks, m_l0_w_out, m_l0_ffn_pre_norm, m_l0_ffn_post_norm, m_l0_ffn_w_up, m_l0_ffn_conv_w, m_l0_ffn_conv_b, m_l0_ffn_w_down, m_l1_mix_pre_norm, m_l1_mix_post_norm, m_l1_w_in, m_l1_ssm_conv_w, m_l1_ssm_conv_b, m_l1_dt_bias, m_l1_a_log, m_l1_d_skip, m_l1_gate_norm, m_l1_w_out, m_l1_ffn_pre_norm, m_l1_ffn_post_norm, m_l1_ffn_w_up, m_l1_ffn_conv_w, m_l1_ffn_conv_b, m_l1_ffn_w_down, v_meta_tokens, v_l0_mix_pre_norm, v_l0_mix_post_norm, v_l0_w_in, v_l0_lru_conv_w, v_l0_lru_conv_b, v_l0_lru_w_a, v_l0_lru_b_a, v_l0_lru_w_x, v_l0_lru_b_x, v_l0_lru_lambda, v_l0_attn_sinks, v_l0_w_out, v_l0_ffn_pre_norm, v_l0_ffn_post_norm, v_l0_ffn_w_up, v_l0_ffn_conv_w, v_l0_ffn_conv_b, v_l0_ffn_w_down, v_l1_mix_pre_norm, v_l1_mix_post_norm, v_l1_w_in, v_l1_ssm_conv_w, v_l1_ssm_conv_b, v_l1_dt_bias, v_l1_a_log, v_l1_d_skip, v_l1_gate_norm, v_l1_w_out, v_l1_ffn_pre_norm, v_l1_ffn_post_norm, v_l1_ffn_w_up, v_l1_ffn_conv_w, v_l1_ffn_conv_b, v_l1_ffn_w_down):
    args = locals()
    W = {n: args[n] for n in _WEIGHTS}
    M = {n: args["m_" + n] for n in _WEIGHTS}
    V = {n: args["v_" + n] for n in _WEIGHTS}
    loss, grad_x, grads, delta, new_m, new_v = _step(x[0], loss_target[0], W, M, V)
    return (loss, grad_x[None], *[grads[n] for n in _WEIGHTS], *[delta[n] for n in _WEIGHTS],
            *[new_m[n] for n in _WEIGHTS], *[new_v[n] for n in _WEIGHTS])
```

```python
import functools
import math

import jax
import jax.numpy as jnp
from jax import lax
from jax.experimental import pallas as pl
from jax.experimental.pallas import tpu as pltpu

F32 = jnp.float32
BF16 = jnp.bfloat16

D_MODEL = 1024
N_META = 16
BLOCK = 128
PAD = BLOCK - N_META
EPS = 1e-6
LRU_BLOCKS = 8
LRU_BS = 128
LRU_C = 8.0
N_Q_HEADS = 16
N_KV_HEADS = 2
HEAD_DIM = 64
Q_PER_KV = 8
WINDOW = 128
D_SSM = 2048
SSD_HEADS = 32
SSD_GROUPS = 8
SSD_HPG = 4
SSD_P = 64
SSD_N = 128
D_FF = 2816
NEG = -1e30
LANES = 128
SUBLANES = 8
_VMEM_LIMIT_WIDE = 62 * 1024 * 1024

ADAM_LR = 0.001
ADAM_B1 = 0.9
ADAM_B2 = 0.999
ADAM_EPS = 1e-08
ADAM_WD = 0.01
ADAM_STEP = 10

MESH = pl.DeviceIdType.MESH
N_CHIPS = 4
N_DEV = 8


def _pick(n, cands):
    for c in cands:
        if n % c == 0:
            return c
    raise ValueError(f"no tile for {n} in {cands}")


def _col_tile(n, limit=1792):
    best = None
    for t in range(LANES, min(n, limit) + 1, LANES):
        if n % t == 0:
            best = t
    if best is None:
        raise ValueError(f"no lane tile for {n}")
    return best


def _sigmoid(x):
    return 0.5 + 0.5 * jnp.tanh(0.5 * x)


def _log1p(e):
    u = 1.0 + e
    return jnp.where(u == 1.0, e, jnp.log(u) * (e / jnp.where(u == 1.0, 1.0, u - 1.0)))


def _softplus(x):
    return jnp.maximum(x, 0.0) + _log1p(jnp.exp(-jnp.abs(x)))


def _neg_expm1(x):
    poly = x * (1.0 + x * (0.5 + x * (1.0 / 6.0 + x * (1.0 / 24.0 + x * (1.0 / 120.0)))))
    return -jnp.where(x > -0.05, poly, jnp.exp(x) - 1.0)


_GELU_C = math.sqrt(2.0 / math.pi)


def _gelu(x):
    t = jnp.tanh(_GELU_C * (x + 0.044715 * x * x * x))
    return 0.5 * x * (1.0 + t)


def _gelu_and_grad(x):
    x2 = x * x
    t = jnp.tanh(_GELU_C * (x + 0.044715 * x * x2))
    g = 0.5 * x * (1.0 + t)
    dg = 0.5 * (1.0 + t) + 0.5 * x * (1.0 - t * t) * _GELU_C * (1.0 + 3.0 * 0.044715 * x2)
    return g, dg


def _silu_and_grad(x):
    s = _sigmoid(x)
    return x * s, s * (1.0 + x * (1.0 - s))


def _dot(a, b):
    return jnp.dot(a, b, preferred_element_type=F32)


def _dot_nt(a, b):
    return lax.dot_general(a, b, (((1,), (1,)), ((), ())), preferred_element_type=F32)


def _dot_tn(a, b):
    return lax.dot_general(a, b, (((0,), (0,)), ((), ())), preferred_element_type=F32)


def _row_iota(t):
    return lax.broadcasted_iota(jnp.int32, (t, 1), 0)


def _scan_fwd(a, u, t):
    row = _row_iota(t)
    d = 1
    while d < t:
        m = row >= d
        u_sh = jnp.where(m, pltpu.roll(u, d, 0), 0.0)
        a_sh = jnp.where(m, pltpu.roll(a, d, 0), 1.0)
        u = u + a * u_sh
        a = a * a_sh
        d *= 2
    return a, u


def _scan_rev(c, x, t):
    row = _row_iota(t)
    d = 1
    while d < t:
        m = row < t - d
        x_sh = jnp.where(m, pltpu.roll(x, t - d, 0), 0.0)
        c_sh = jnp.where(m, pltpu.roll(c, t - d, 0), 1.0)
        x = x + c * x_sh
        c = c * c_sh
        d *= 2
    return c, x


def _cumsum_rows(x, t):
    row = _row_iota(t)
    d = 1
    while d < t:
        x = x + jnp.where(row >= d, pltpu.roll(x, d, 0), 0.0)
        d *= 2
    return x


def _rev_cumsum_rows(x, t):
    row = _row_iota(t)
    d = 1
    while d < t:
        x = x + jnp.where(row < t - d, pltpu.roll(x, t - d, 0), 0.0)
        d *= 2
    return x


def _rms_bwd(x, g, dy):
    rs = lax.rsqrt(jnp.mean(x * x, axis=-1, keepdims=True) + EPS)
    gy = dy * g
    dx = rs * gy - x * (rs * rs * rs) * jnp.mean(x * gy, axis=-1, keepdims=True)
    return dx, dy * x * rs


def _mm_nn(a, w, out_dtype, name):
    parts = a if isinstance(a, (tuple, list)) else (a,)
    m = parts[0].shape[0]
    k, n = w.shape
    tm = _pick(m, (640, 512, 256, 128))
    tn = _col_tile(n)
    offs = [sum(p.shape[1] for p in parts[:i]) for i in range(len(parts))]

    def body(*refs):
        w_ref, o_ref = refs[len(parts)], refs[len(parts) + 1]
        acc = None
        for a_ref, p, off in zip(refs, parts, offs):
            t = _dot(a_ref[...].astype(BF16), w_ref[off:off + p.shape[1], :])
            acc = t if acc is None else acc + t
        o_ref[...] = acc.astype(o_ref.dtype)

    return pl.pallas_call(
        body, name=name, grid=(n // tn, m // tm),
        in_specs=[pl.BlockSpec((tm, p.shape[1]), lambda j, i: (i, 0)) for p in parts]
        + [pl.BlockSpec((k, tn), lambda j, i: (0, j))],
        out_specs=pl.BlockSpec((tm, tn), lambda j, i: (i, j)),
        out_shape=jax.ShapeDtypeStruct((m, n), out_dtype),
        compiler_params=pltpu.CompilerParams(dimension_semantics=("parallel", "parallel")),
    )(*parts, w)


def _mm_nt(dy, w, name):
    m, n = dy.shape
    k = w.shape[0]
    wide = n > 3328
    tm = _pick(m, (320, 256, 128)) if wide else _pick(m, (640, 512, 256, 128))
    tk = _col_tile(k, 512 if wide else 1408)

    def body(dy_ref, w_ref, o_ref):
        o_ref[...] = _dot_nt(dy_ref[...].astype(BF16), w_ref[...])

    return pl.pallas_call(
        body, name=name, grid=(k // tk, m // tm),
        in_specs=[pl.BlockSpec((tm, n), lambda j, i: (i, 0)),
                  pl.BlockSpec((tk, n), lambda j, i: (j, 0))],
        out_specs=pl.BlockSpec((tm, tk), lambda j, i: (i, j)),
        out_shape=jax.ShapeDtypeStruct((m, k), F32),
        compiler_params=pltpu.CompilerParams(dimension_semantics=("parallel", "parallel")),
    )(dy, w)


def _mm_tn(a, dy, name):
    m, k = a.shape
    n = dy.shape[1]
    tm = _pick(m, (640, 512, 256, 128))
    tk = _col_tile(k, 1408)
    tn = _col_tile(n, 1664)
    nsteps = m // tm

    def body(a_ref, dy_ref, o_ref, acc):
        @pl.when(pl.program_id(2) == 0)
        def _():
            acc[...] = jnp.zeros_like(acc)

        acc[...] += _dot_tn(a_ref[...].astype(BF16), dy_ref[...].astype(BF16))

        @pl.when(pl.program_id(2) == nsteps - 1)
        def _():
            o_ref[...] = acc[...].astype(o_ref.dtype)

    return pl.pallas_call(
        body, name=name, grid=(k // tk, n // tn, nsteps),
        in_specs=[pl.BlockSpec((tm, tk), lambda kk, j, i: (i, kk)),
                  pl.BlockSpec((tm, tn), lambda kk, j, i: (i, j))],
        out_specs=pl.BlockSpec((tk, tn), lambda kk, j, i: (kk, j)),
        out_shape=jax.ShapeDtypeStruct((k, n), BF16),
        scratch_shapes=[pltpu.VMEM((tk, tn), F32)],
        compiler_params=pltpu.CompilerParams(
            dimension_semantics=("parallel", "parallel", "arbitrary")),
    )(a, dy)


def _mm_nn_sh(a, w4, n_out, name):
    m, k = a.shape
    s, _, n = w4.shape
    tm = _pick(m, (320, 256, 128))

    def body(a_ref, w_ref, o_ref):
        av = a_ref[...].astype(BF16)
        for j in range(s):
            o_ref[:, j * n:(j + 1) * n] = _dot(av, w_ref[j])
        if n_out > s * n:
            o_ref[:, s * n:] = jnp.zeros((tm, n_out - s * n), F32)

    return pl.pallas_call(
        body, name=name, grid=(m // tm,),
        in_specs=[pl.BlockSpec((tm, k), lambda i: (i, 0)),
                  pl.BlockSpec((s, k, n), lambda i: (0, 0, 0))],
        out_specs=pl.BlockSpec((tm, n_out), lambda i: (i, 0)),
        out_shape=jax.ShapeDtypeStruct((m, n_out), F32),
        compiler_params=pltpu.CompilerParams(dimension_semantics=("parallel",)),
    )(a, w4)


def _mm_nt_sh(dy, w4, name, plan=None):
    dys = dy if isinstance(dy, (tuple, list)) else (dy,)
    m = dys[0].shape[0]
    s, k, n = w4.shape
    tm = _pick(m, (640, 512, 256, 128))
    tk = _col_tile(k, 1024)
    where = _shard_columns(dys, s, n)
    p_in, p_shapes, p_out, p_scr = _plan_parts(plan)
    nd, nj, ni = len(dys), k // tk, m // tm

    def body(*refs):
        w_ref = refs[nd]
        cins = refs[nd + 1:nd + 1 + len(p_in)]
        o_ref = refs[nd + 1 + len(p_in)]
        couts = refs[nd + 2 + len(p_in):nd + 2 + len(p_in) + len(p_out)]
        sems = refs[nd + 2 + len(p_in) + len(p_out):]
        step = pl.program_id(0) * ni + pl.program_id(1)
        if plan is not None:
            @pl.when(step == 0)
            def _():
                plan.start(cins, couts, sems)

        acc = None
        for j, (p, c0) in enumerate(where):
            t = _dot_nt(refs[p][:, c0:c0 + n].astype(BF16), w_ref[j])
            acc = t if acc is None else acc + t
        o_ref[...] = acc
        if plan is not None:
            @pl.when(step == nj * ni - 1)
            def _():
                plan.wait(cins, couts, sems)

    sem = ("arbitrary", "arbitrary") if plan is not None else ("parallel", "parallel")
    res = pl.pallas_call(
        body, name=name, grid=(nj, ni),
        in_specs=[pl.BlockSpec((tm, d.shape[1]), lambda j, i: (i, 0)) for d in dys]
        + [pl.BlockSpec((s, tk, n), lambda j, i: (0, j, 0))] + p_in,
        out_specs=[pl.BlockSpec((tm, tk), lambda j, i: (i, j))] + p_out,
        out_shape=[jax.ShapeDtypeStruct((m, k), F32)] + p_shapes,
        scratch_shapes=p_scr,
        compiler_params=pltpu.CompilerParams(dimension_semantics=sem),
    )(*dys, w4, *(plan.ins if plan is not None else []))
    return res[0] if plan is None else (res[0], res[1:])


def _shard_columns(dys, s, n):
    where = []
    for p, d in enumerate(dys):
        where += [(p, c * n) for c in range(d.shape[1] // n)]
    assert len(where) >= s
    return where[:s]


def _mm_tn_sh(a, dy, n, name):
    dys = dy if isinstance(dy, (tuple, list)) else (dy,)
    m, k = a.shape
    s = N_CHIPS
    tm = _pick(m, (640, 512, 256, 128))
    tk = _col_tile(k, 512)
    nsteps = m // tm
    where = _shard_columns(dys, s, n)

    def body(*refs):
        a_ref, o_ref, acc = refs[0], refs[len(dys) + 1], refs[len(dys) + 2]

        @pl.when(pl.program_id(1) == 0)
        def _():
            acc[...] = jnp.zeros_like(acc)

        av = a_ref[...].astype(BF16)
        for j, (p, c0) in enumerate(where):
            acc[j] += _dot_tn(av, refs[1 + p][:, c0:c0 + n].astype(BF16))

        @pl.when(pl.program_id(1) == nsteps - 1)
        def _():
            o_ref[...] = acc[...].astype(o_ref.dtype)

    return pl.pallas_call(
        body, name=name, grid=(k // tk, nsteps),
        in_specs=[pl.BlockSpec((tm, tk), lambda kk, i: (i, kk))]
        + [pl.BlockSpec((tm, d.shape[1]), lambda kk, i: (i, 0)) for d in dys],
        out_specs=pl.BlockSpec((s, tk, n), lambda kk, i: (0, kk, 0)),
        out_shape=jax.ShapeDtypeStruct((s, k, n), BF16),
        scratch_shapes=[pltpu.VMEM((s, tk, n), F32)],
        compiler_params=pltpu.CompilerParams(dimension_semantics=("parallel", "arbitrary"),
                                             vmem_limit_bytes=_VMEM_LIMIT_WIDE),
    )(a, *dys)


def _rowcall(name, body, lp, tm, rows=(), prevs=(), vecs=(), outs=(), accs=(), scratch=(),
             reverse=False, seq=False):
    nt = lp // tm
    hb = tm // SUBLANES

    def ri(i):
        return nt - 1 - i if reverse else i

    in_specs, args = [], []
    for arr, w, cb in rows:
        in_specs.append(pl.BlockSpec((tm, w), lambda i, cb=cb: (ri(i), cb)))
        args.append(arr)
    for arr, w, cb in prevs:
        in_specs.append(pl.BlockSpec((SUBLANES, w), lambda i, cb=cb: (jnp.maximum(ri(i) * hb - 1, 0), cb)))
        args.append(arr)
    for arr in vecs:
        in_specs.append(pl.BlockSpec(arr.shape, lambda i, nd=arr.ndim: (0,) * nd))
        args.append(arr)
    out_shape, out_specs = [], []
    for w, dt in outs:
        out_shape.append(jax.ShapeDtypeStruct((lp, w), dt))
        out_specs.append(pl.BlockSpec((tm, w), lambda i: (ri(i), 0)))
    for shp, dt in accs:
        out_shape.append(jax.ShapeDtypeStruct(shp, dt))
        out_specs.append(pl.BlockSpec(shp, lambda i, nd=len(shp): (0,) * nd))

    def kern(*refs):
        i = pl.program_id(0)
        body(ri(i), i == 0, *refs)

    sem = ("arbitrary",) if (seq or accs) else ("parallel",)
    res = pl.pallas_call(
        kern, name=name, grid=(nt,), in_specs=in_specs, out_specs=out_specs,
        out_shape=out_shape, scratch_shapes=list(scratch),
        compiler_params=pltpu.CompilerParams(dimension_semantics=sem),
    )(*args)
    return res


def _acc_add(first, ref, val):
    @pl.when(first)
    def _():
        ref[...] = jnp.zeros_like(ref)

    ref[...] += val


def _real_rows(r, tm):
    return (r * tm + _row_iota(tm)) >= PAD


def _rmsnorm_fwd(h, g, name):
    lp, d = h.shape
    tm = _pick(lp, (640, 512, 256, 128))

    def body(r, first, h_ref, g_ref, u_ref):
        x = h_ref[...]
        rs = lax.rsqrt(jnp.mean(x * x, axis=-1, keepdims=True) + EPS)
        u_ref[...] = (x * rs * g_ref[...]).astype(u_ref.dtype)

    return _rowcall(name, body, lp, tm, rows=[(h, d, 0)], vecs=[g], outs=[(d, BF16)])[0]


def _postnorm_res_fwd(h, o, g, name):
    lp, d = h.shape
    tm = _pick(lp, (640, 512, 256, 128))

    def body(r, first, h_ref, o_ref, g_ref, out_ref):
        x = o_ref[...]
        rs = lax.rsqrt(jnp.mean(x * x, axis=-1, keepdims=True) + EPS)
        out_ref[...] = jnp.where(_real_rows(r, tm), h_ref[...] + x * rs * g_ref[...], 0.0)

    return _rowcall(name, body, lp, tm, rows=[(h, d, 0), (o, d, 0)], vecs=[g], outs=[(d, F32)])[0]


def _postnorm_bwd(o, g, dh, name):
    lp, d = o.shape
    tm = _pick(lp, (640, 512, 256, 128))

    def body(r, first, o_ref, dh_ref, g_ref, do_ref, dg_ref):
        dx, dgt = _rms_bwd(o_ref[...], g_ref[...], dh_ref[...])
        do_ref[...] = dx.astype(do_ref.dtype)
        _acc_add(first, dg_ref, jnp.sum(dgt, axis=0, keepdims=True))

    return _rowcall(name, body, lp, tm, rows=[(o, d, 0), (dh, d, 0)], vecs=[g],
                    outs=[(d, BF16)], accs=[((1, d), F32)])


def _prenorm_bwd(h, g, du, dh_res, name):
    lp, d = h.shape
    tm = _pick(lp, (640, 512, 256, 128))

    def body(r, first, h_ref, du_ref, dres_ref, g_ref, dh_ref, dg_ref):
        dx, dgt = _rms_bwd(h_ref[...], g_ref[...], du_ref[...])
        dh_ref[...] = jnp.where(_real_rows(r, tm), dres_ref[...] + dx, 0.0)
        _acc_add(first, dg_ref, jnp.sum(dgt, axis=0, keepdims=True))

    return _rowcall(name, body, lp, tm, rows=[(h, d, 0), (du, d, 0), (dh_res, d, 0)], vecs=[g],
                    outs=[(d, F32)], accs=[((1, d), F32)])


def _loss_fwd_bwd(h, tgt, name):
    lp, d = h.shape
    tm = _pick(lp, (640, 512, 256, 128))

    def body(r, first, h_ref, t_ref, dh_ref, ls_ref):
        tok = (r * tm + _row_iota(tm)) >= BLOCK
        e = jnp.where(tok, h_ref[...] - t_ref[...], 0.0)
        dh_ref[...] = e * (1.0 / d)
        _acc_add(first, ls_ref, jnp.sum(e * e, axis=0, keepdims=True))

    return _rowcall(name, body, lp, tm, rows=[(h, d, 0), (tgt, d, 0)],
                    outs=[(d, F32)], accs=[((1, d), F32)])


def _conv_tiles(lp, width):
    wc = _col_tile(width, 1408)
    tm = _pick(lp, (320, 256, 128))
    return tm, wc


def _conv_fwd(x, col_off, width, w, b, name):
    lp = x.shape[0]
    kk = w.shape[0]
    tm, wc = _conv_tiles(lp, width)
    offb = col_off // wc
    assert col_off % wc == 0
    hb = tm // SUBLANES

    def body(x_ref, xp_ref, w_ref, b_ref, y_ref):
        i = pl.program_id(1)
        xv = x_ref[...]
        halo = jnp.where(i > 0, xp_ref[...], 0.0)
        xx = jnp.concatenate([halo, xv], axis=0)
        acc = b_ref[...] + w_ref[kk - 1:kk, :] * xv
        for j in range(1, kk):
            acc = acc + w_ref[kk - 1 - j:kk - j, :] * pltpu.roll(xx, j, 0)[SUBLANES:, :]
        y_ref[...] = acc

    return pl.pallas_call(
        body, name=name, grid=(width // wc, lp // tm),
        in_specs=[pl.BlockSpec((tm, wc), lambda j, i: (i, offb + j)),
                  pl.BlockSpec((SUBLANES, wc), lambda j, i: (jnp.maximum(i * hb - 1, 0), offb + j)),
                  pl.BlockSpec((kk, wc), lambda j, i: (0, j)),
                  pl.BlockSpec((1, wc), lambda j, i: (0, j))],
        out_specs=pl.BlockSpec((tm, wc), lambda j, i: (i, j)),
        out_shape=jax.ShapeDtypeStruct((lp, width), F32),
        compiler_params=pltpu.CompilerParams(dimension_semantics=("parallel", "parallel")),
    )(x, x, w, b)


def _conv_bwd(x, col_off, width, dy, w, name, w_col_off=0):
    lp = x.shape[0]
    kk = w.shape[0]
    tm, wc = _conv_tiles(lp, width)
    offb = col_off // wc
    woffb = w_col_off // wc
    assert col_off % wc == 0 and w_col_off % wc == 0
    hb = tm // SUBLANES
    hrows = SUBLANES * (4 // dy.dtype.itemsize)
    ext = tm + hrows

    def body(x_ref, xp_ref, dy_ref, dn_ref, w_ref, dx_ref, dw_ref, db_ref):
        i = pl.program_id(1)
        last = pl.num_programs(1) - 1
        xv = x_ref[...]
        dyv = dy_ref[...].astype(F32)
        xx = jnp.concatenate([jnp.where(i > 0, xp_ref[...], 0.0), xv], axis=0)
        dd = jnp.concatenate([dyv, jnp.where(i < last, dn_ref[...].astype(F32), 0.0)], axis=0)
        dx = w_ref[kk - 1:kk, :] * dyv
        rows = [jnp.sum(dyv * xv, axis=0, keepdims=True)]
        for m in range(1, kk):
            dx = dx + w_ref[kk - 1 - m:kk - m, :] * pltpu.roll(dd, ext - m, 0)[:tm, :]
            rows.append(jnp.sum(dyv * pltpu.roll(xx, m, 0)[SUBLANES:, :], axis=0, keepdims=True))
        dx_ref[...] = dx.astype(dx_ref.dtype)
        dwp = jnp.concatenate(rows[::-1] + [jnp.zeros((SUBLANES - kk, wc), F32)], axis=0)

        @pl.when(i == 0)
        def _():
            dw_ref[...] = jnp.zeros_like(dw_ref)
            db_ref[...] = jnp.zeros_like(db_ref)

        dw_ref[...] += dwp
        db_ref[...] += jnp.sum(dyv, axis=0, keepdims=True)

    return pl.pallas_call(
        body, name=name, grid=(width // wc, lp // tm),
        in_specs=[pl.BlockSpec((tm, wc), lambda j, i: (i, offb + j)),
                  pl.BlockSpec((SUBLANES, wc), lambda j, i: (jnp.maximum(i * hb - 1, 0), offb + j)),
                  pl.BlockSpec((tm, wc), lambda j, i: (i, j)),
                  pl.BlockSpec((hrows, wc), lambda j, i: (jnp.minimum((i + 1) * (tm // hrows), lp // hrows - 1), j)),
                  pl.BlockSpec((kk, wc), lambda j, i: (0, woffb + j))],
        out_specs=[pl.BlockSpec((tm, wc), lambda j, i: (i, j)),
                   pl.BlockSpec((SUBLANES, wc), lambda j, i: (0, j)),
                   pl.BlockSpec((1, wc), lambda j, i: (0, j))],
        out_shape=[jax.ShapeDtypeStruct((lp, width), BF16),
                   jax.ShapeDtypeStruct((SUBLANES, width), F32),
                   jax.ShapeDtypeStruct((1, width), F32)],
        compiler_params=pltpu.CompilerParams(dimension_semantics=("parallel", "arbitrary")),
    )(x, x, dy, dy, w)


_FFN_K = 3
_FFN_WC = 1408


def _conv3_ext(x_ext, w_ref, b_ref):
    return (b_ref[...] + w_ref[2:3, :] * x_ext + w_ref[1:2, :] * pltpu.roll(x_ext, 1, 0)
            + w_ref[0:1, :] * pltpu.roll(x_ext, 2, 0))


def _ffn_convact_fwd(hp, cw, cb, name):
    lp = hp.shape[0]
    tm = _pick(lp, (320, 256, 128))
    wc = _FFN_WC
    nj = D_FF // wc
    hb = tm // SUBLANES

    def body(g_ref, gp_ref, u_ref, up_ref, wg_ref, wu_ref, bg_ref, bu_ref, a_ref, hg_ref, hu_ref):
        i = pl.program_id(1)

        def conv(x_ref, p_ref, w_ref, b_ref):
            x_ext = jnp.concatenate([jnp.where(i > 0, p_ref[...], 0.0), x_ref[...]], axis=0)
            return _conv3_ext(x_ext, w_ref, b_ref)[SUBLANES:, :]

        hg = conv(g_ref, gp_ref, wg_ref, bg_ref)
        hu = conv(u_ref, up_ref, wu_ref, bu_ref)
        a_ref[...] = (_gelu(hg) * hu).astype(a_ref.dtype)
        hg_ref[...] = hg.astype(hg_ref.dtype)
        hu_ref[...] = hu.astype(hu_ref.dtype)

    tile = lambda off: pl.BlockSpec((tm, wc), lambda j, i: (i, off + j))
    prev = lambda off: pl.BlockSpec((SUBLANES, wc), lambda j, i: (jnp.maximum(i * hb - 1, 0), off + j))
    vec = lambda rows, off: pl.BlockSpec((rows, wc), lambda j, i: (0, off + j))
    return pl.pallas_call(
        body, name=name, grid=(nj, lp // tm),
        in_specs=[tile(0), prev(0), tile(nj), prev(nj), vec(_FFN_K, 0), vec(_FFN_K, nj), vec(1, 0), vec(1, nj)],
        out_specs=[tile(0)] * 3,
        out_shape=[jax.ShapeDtypeStruct((lp, D_FF), BF16)] * 3,
        compiler_params=pltpu.CompilerParams(dimension_semantics=("parallel", "parallel")),
    )(hp, hp, hp, hp, cw, cw, cb, cb)


def _ffn_act_bwd(hg, hu, dact, name):
    lp = hg.shape[0]
    tm = _pick(lp, (320, 256, 128))

    def body(r, first, g_ref, u_ref, da_ref, dg_ref, du_ref):
        gl, dgl = _gelu_and_grad(g_ref[...].astype(F32))
        da = da_ref[...]
        dg_ref[...] = (da * u_ref[...].astype(F32) * dgl).astype(dg_ref.dtype)
        du_ref[...] = (da * gl).astype(du_ref.dtype)

    return _rowcall(name, body, lp, tm, rows=[(hg, D_FF, 0), (hu, D_FF, 0), (dact, D_FF, 0)],
                    outs=[(D_FF, BF16), (D_FF, BF16)])


def _lru_gates(x, wa_ref, wx_ref, ba, bx, lam):
    xb = x.astype(BF16)
    za, zx = [], []
    for n in range(LRU_BLOCKS):
        xs = xb[:, n * LRU_BS:(n + 1) * LRU_BS]
        za.append(_dot(xs, wa_ref[n]))
        zx.append(_dot(xs, wx_ref[n]))
    r = _sigmoid(jnp.concatenate(za, axis=1) + ba)
    ig = _sigmoid(jnp.concatenate(zx, axis=1) + bx)
    sp = _softplus(-lam)
    log_a = -LRU_C * r * sp
    a = jnp.exp(log_a)
    om = _neg_expm1(2.0 * log_a)
    mult = jnp.sqrt(om)
    return xb, r, ig, sp, a, om, mult


def _lru_fwd(proj, xrc, wa, wx, ba, bx, lam, name):
    lp, d = xrc.shape
    tm = BLOCK

    def body(r_idx, first, gate_ref, x_ref, wa_ref, wx_ref, ba_ref, bx_ref, lam_ref,
             y_ref, h_ref, carry):
        @pl.when(first)
        def _():
            carry[...] = jnp.zeros_like(carry)

        x = x_ref[...]
        _, _, ig, _, a, _, mult = _lru_gates(x, wa_ref, wx_ref, ba_ref[...], bx_ref[...], lam_ref[...])
        u = jnp.where(_real_rows(r_idx, tm), mult * ig * x, 0.0)
        acum, hloc = _scan_fwd(a, u, tm)
        h = hloc + acum * carry[0:1, :]
        h_ref[...] = h
        carry[0:1, :] = h[tm - 1:tm, :]
        y_ref[...] = (_gelu(gate_ref[...]) * h).astype(y_ref.dtype)

    return _rowcall(name, body, lp, tm, rows=[(proj, d, 0), (xrc, d, 0)],
                    vecs=[wa, wx, ba, bx, lam], outs=[(d, BF16), (d, F32)],
                    scratch=[pltpu.VMEM((SUBLANES, d), F32)], seq=True)


def _lru_bwd(proj, xrc, hl, dmix, wa, wx, ba, bx, lam, name):
    lp, d = xrc.shape
    tm = BLOCK

    def body(r_idx, first, gate_ref, x_ref, h_ref, dy_ref, hp_ref, wa_ref, wx_ref, ba_ref, bx_ref,
             lam_ref, dgate_ref, dx_ref, dwa_ref, dwx_ref, dba_ref, dbx_ref, dlam_ref, carry):
        @pl.when(first)
        def _():
            carry[...] = jnp.zeros_like(carry)
            dwa_ref[...] = jnp.zeros_like(dwa_ref)
            dwx_ref[...] = jnp.zeros_like(dwx_ref)
            dba_ref[...] = jnp.zeros_like(dba_ref)
            dbx_ref[...] = jnp.zeros_like(dbx_ref)
            dlam_ref[...] = jnp.zeros_like(dlam_ref)

        x = x_ref[...]
        lam = lam_ref[...]
        xb, r, ig, sp, a, om, mult = _lru_gates(x, wa_ref, wx_ref, ba_ref[...], bx_ref[...], lam)
        h = h_ref[...]
        dy = dy_ref[...]
        gl, dgl = _gelu_and_grad(gate_ref[...])
        dgate_ref[...] = (dy * h * dgl).astype(dgate_ref.dtype)
        row = _row_iota(tm)
        lastrow = row == tm - 1
        xg = dy * gl + jnp.where(lastrow, carry[0:1, :], 0.0)
        c = jnp.where(lastrow, 1.0, pltpu.roll(a, tm - 1, 0))
        _, g = _scan_rev(c, xg, tm)
        carry[0:1, :] = a[0:1, :] * g[0:1, :]
        hprev_in = jnp.where(r_idx > 0, hp_ref[SUBLANES - 1:SUBLANES, :], 0.0)
        hprev = jnp.where(row == 0, hprev_in, pltpu.roll(h, 1, 0))
        du = jnp.where(_real_rows(r_idx, tm), g, 0.0)
        da = g * hprev
        dmult = du * ig * x
        dig = du * mult * x
        dxv = du * mult * ig
        e2 = 1.0 - om
        dlog_a = da * a - dmult * e2 / mult
        dr = dlog_a * (-LRU_C) * sp
        dsp = jnp.sum(dlog_a * (-LRU_C) * r, axis=0, keepdims=True)
        dlam_ref[...] += -dsp * _sigmoid(-lam)
        dza = dr * r * (1.0 - r)
        dzx = dig * ig * (1.0 - ig)
        dba_ref[...] += jnp.sum(dza, axis=0, keepdims=True)
        dbx_ref[...] += jnp.sum(dzx, axis=0, keepdims=True)
        dzab = dza.astype(BF16)
        dzxb = dzx.astype(BF16)
        parts = []
        for n in range(LRU_BLOCKS):
            sl = slice(n * LRU_BS, (n + 1) * LRU_BS)
            dwa_ref[n] += _dot_tn(xb[:, sl], dzab[:, sl])
            dwx_ref[n] += _dot_tn(xb[:, sl], dzxb[:, sl])
            parts.append(_dot_nt(dzab[:, sl], wa_ref[n]) + _dot_nt(dzxb[:, sl], wx_ref[n]))
        dx_ref[...] = dxv + jnp.concatenate(parts, axis=1)

    return _rowcall(name, body, lp, tm,
                    rows=[(proj, d, 0), (xrc, d, 0), (hl, d, 0), (dmix, d, 0)],
                    prevs=[(hl, d, 0)], vecs=[wa, wx, ba, bx, lam],
                    outs=[(d, BF16), (d, F32)],
                    accs=[((LRU_BLOCKS, LRU_BS, LRU_BS), F32), ((LRU_BLOCKS, LRU_BS, LRU_BS), F32),
                          ((1, d), F32), ((1, d), F32), ((1, d), F32)],
                    scratch=[pltpu.VMEM((SUBLANES, d), F32)], reverse=True, seq=True)


_SLOPES = [2.0 ** (-8.0 * (h + 1) / N_Q_HEADS) for h in range(N_Q_HEADS)]
_QK_SCALE = HEAD_DIM ** -0.5
_QCOL = 2 * D_MODEL // D_MODEL
_KCOL = (3 * D_MODEL) // LANES
_VCOL = _KCOL + 1


def _attn_masks(n):
    start = pl.multiple_of(jnp.maximum(n - 1, 0) * BLOCK, BLOCK)
    qi = n * BLOCK + lax.broadcasted_iota(jnp.int32, (BLOCK, 2 * BLOCK), 0)
    kj = start + lax.broadcasted_iota(jnp.int32, (BLOCK, 2 * BLOCK), 1)
    dist = qi - kj
    ok = (kj >= BLOCK) & (dist >= 0) & (dist < WINDOW)
    dm = (n * BLOCK - PAD + lax.broadcasted_iota(jnp.int32, (BLOCK, N_META), 0)
          - lax.broadcasted_iota(jnp.int32, (BLOCK, N_META), 1))
    okm = dm >= 0
    return start, ok, dist.astype(F32), okm, jnp.minimum(dm, WINDOW).astype(F32)


def _group_rows(ref, g):
    return jnp.concatenate(
        [ref[:, (g * Q_PER_KV + hh) * HEAD_DIM:(g * Q_PER_KV + hh + 1) * HEAD_DIM] for hh in range(Q_PER_KV)],
        axis=0).astype(BF16)


def _attn_probs(qg, kg, kmg, sink_ref, g, ok, distf, okm, dmf):
    slope = jnp.stack([jnp.full((1, 1), _SLOPES[g * Q_PER_KV + hh], F32) for hh in range(Q_PER_KV)])
    sink = jnp.stack([sink_ref[0:1, g * Q_PER_KV + hh:g * Q_PER_KV + hh + 1] for hh in range(Q_PER_KV)])
    s = (_dot_nt(qg, kg) * _QK_SCALE).reshape(Q_PER_KV, BLOCK, 2 * BLOCK)
    sm = (_dot_nt(qg, kmg) * _QK_SCALE).reshape(Q_PER_KV, BLOCK, N_META)
    s = jnp.where(ok[None], s - slope * distf[None], NEG)
    sm = jnp.where(okm[None], sm - slope * dmf[None], NEG)
    mx = jnp.maximum(jnp.maximum(jnp.max(s, axis=-1, keepdims=True),
                                 jnp.max(sm, axis=-1, keepdims=True)), sink)
    p = jnp.exp(s - mx)
    pm = jnp.exp(sm - mx)
    ps = jnp.exp(sink - mx)
    inv = 1.0 / (jnp.sum(p, axis=-1, keepdims=True) + jnp.sum(pm, axis=-1, keepdims=True) + ps)
    return p * inv, pm * inv, ps * inv


def _attn_fwd(proj, sinks, name, plan=None):
    lp = proj.shape[0]
    nblk = lp // BLOCK
    p_in, p_shapes, p_out, p_scr = _plan_parts(plan)

    def body(*refs):
        q_ref, k_ref, v_ref, sink_ref = refs[:4]
        cins = refs[4:4 + len(p_in)]
        o_ref = refs[4 + len(p_in)]
        couts = refs[5 + len(p_in):5 + len(p_in) + len(p_out)]
        sems = refs[5 + len(p_in) + len(p_out):]
        n = pl.program_id(0)
        if plan is not None:
            @pl.when(n == 0)
            def _():
                plan.start(cins, couts, sems)

        start, ok, distf, okm, dmf = _attn_masks(n)
        kb = k_ref[pl.ds(start, 2 * BLOCK), :].astype(BF16)
        vb = v_ref[pl.ds(start, 2 * BLOCK), :].astype(BF16)
        km = k_ref[PAD:BLOCK, :].astype(BF16)
        vm = v_ref[PAD:BLOCK, :].astype(BF16)
        for g in range(N_KV_HEADS):
            gs = slice(g * HEAD_DIM, (g + 1) * HEAD_DIM)
            pn, pmn, _ = _attn_probs(_group_rows(q_ref, g), kb[:, gs], km[:, gs], sink_ref, g,
                                     ok, distf, okm, dmf)
            o = (_dot(pn.astype(BF16).reshape(Q_PER_KV * BLOCK, 2 * BLOCK), vb[:, gs])
                 + _dot(pmn.astype(BF16).reshape(Q_PER_KV * BLOCK, N_META), vm[:, gs]))
            for hh in range(Q_PER_KV):
                h = g * Q_PER_KV + hh
                o_ref[:, h * HEAD_DIM:(h + 1) * HEAD_DIM] = o[hh * BLOCK:(hh + 1) * BLOCK, :].astype(o_ref.dtype)
        if plan is not None:
            @pl.when(n == nblk - 1)
            def _():
                plan.wait(cins, couts, sems)

    res = pl.pallas_call(
        body, name=name, grid=(nblk,),
        in_specs=[pl.BlockSpec((BLOCK, D_MODEL), lambda n: (n, _QCOL)),
                  pl.BlockSpec((lp, LANES), lambda n: (0, _KCOL)),
                  pl.BlockSpec((lp, LANES), lambda n: (0, _VCOL)),
                  pl.BlockSpec(sinks.shape, lambda n: (0, 0))] + p_in,
        out_specs=[pl.BlockSpec((BLOCK, D_MODEL), lambda n: (n, 0))] + p_out,
        out_shape=[jax.ShapeDtypeStruct((lp, D_MODEL), BF16)] + p_shapes,
        scratch_shapes=p_scr,
        compiler_params=pltpu.CompilerParams(dimension_semantics=("arbitrary",)),
    )(proj, proj, proj, sinks, *(plan.ins if plan is not None else []))
    return res[0], res[1:]


def _attn_bwd(proj, sinks, dmix, name, plan=None):
    lp = proj.shape[0]
    nblk = lp // BLOCK

    p_in, p_shapes, p_out, p_scr = _plan_parts(plan)

    def body(*refs):
        q_ref, k_ref, v_ref, sink_ref, dy_ref = refs[:5]
        cins = refs[5:5 + len(p_in)]
        dq_ref, dk_ref, dv_ref, ds_ref = refs[5 + len(p_in):9 + len(p_in)]
        couts = refs[9 + len(p_in):9 + len(p_in) + len(p_out)]
        sems = refs[9 + len(p_in) + len(p_out):]
        n = pl.program_id(0)

        @pl.when(n == 0)
        def _():
            dk_ref[...] = jnp.zeros_like(dk_ref)
            dv_ref[...] = jnp.zeros_like(dv_ref)
            ds_ref[...] = jnp.zeros_like(ds_ref)
            if plan is not None:
                plan.start(cins, couts, sems)

        start, ok, distf, okm, dmf = _attn_masks(n)
        kb = k_ref[pl.ds(start, 2 * BLOCK), :].astype(BF16)
        vb = v_ref[pl.ds(start, 2 * BLOCK), :].astype(BF16)
        km = k_ref[PAD:BLOCK, :].astype(BF16)
        vm = v_ref[PAD:BLOCK, :].astype(BF16)
        lane16 = lax.broadcasted_iota(jnp.int32, (1, N_Q_HEADS), 1)
        dsink = jnp.zeros((1, N_Q_HEADS), F32)
        rows = Q_PER_KV * BLOCK
        for g in range(N_KV_HEADS):
            gs = slice(g * HEAD_DIM, (g + 1) * HEAD_DIM)
            qg = _group_rows(q_ref, g)
            dog = _group_rows(dy_ref, g)
            pn, pmn, psn = _attn_probs(qg, kb[:, gs], km[:, gs], sink_ref, g, ok, distf, okm, dmf)
            dp = _dot_nt(dog, vb[:, gs]).reshape(Q_PER_KV, BLOCK, 2 * BLOCK)
            dpm = _dot_nt(dog, vm[:, gs]).reshape(Q_PER_KV, BLOCK, N_META)
            delta = (jnp.sum(pn * dp, axis=-1, keepdims=True)
                     + jnp.sum(pmn * dpm, axis=-1, keepdims=True))
            dsb = (pn * (dp - delta)).astype(BF16).reshape(rows, 2 * BLOCK)
            dsm = (pmn * (dpm - delta)).astype(BF16).reshape(rows, N_META)
            dsk = jnp.sum(psn * delta, axis=1, keepdims=True)
            for hh in range(Q_PER_KV):
                dsink = dsink - jnp.where(lane16 == g * Q_PER_KV + hh, dsk[hh], 0.0)
            dq = (_dot(dsb, kb[:, gs]) + _dot(dsm, km[:, gs])) * _QK_SCALE
            for hh in range(Q_PER_KV):
                h = g * Q_PER_KV + hh
                dq_ref[:, h * HEAD_DIM:(h + 1) * HEAD_DIM] = dq[hh * BLOCK:(hh + 1) * BLOCK, :].astype(dq_ref.dtype)
            pnb = pn.astype(BF16).reshape(rows, 2 * BLOCK)
            pmnb = pmn.astype(BF16).reshape(rows, N_META)
            dk_ref[pl.ds(start, 2 * BLOCK), gs] += _dot_tn(dsb, qg) * _QK_SCALE
            dv_ref[pl.ds(start, 2 * BLOCK), gs] += _dot_tn(pnb, dog)
            dk_ref[PAD:BLOCK, gs] += _dot_tn(dsm, qg) * _QK_SCALE
            dv_ref[PAD:BLOCK, gs] += _dot_tn(pmnb, dog)
        ds_ref[...] += dsink
        if plan is not None:
            @pl.when(n == nblk - 1)
            def _():
                plan.wait(cins, couts, sems)

    res = pl.pallas_call(
        body, name=name, grid=(nblk,),
        in_specs=[pl.BlockSpec((BLOCK, D_MODEL), lambda n: (n, _QCOL)),
                  pl.BlockSpec((lp, LANES), lambda n: (0, _KCOL)),
                  pl.BlockSpec((lp, LANES), lambda n: (0, _VCOL)),
                  pl.BlockSpec(sinks.shape, lambda n: (0, 0)),
                  pl.BlockSpec((BLOCK, D_MODEL), lambda n: (n, 1))] + p_in,
        out_specs=[pl.BlockSpec((BLOCK, D_MODEL), lambda n: (n, 0)),
                   pl.BlockSpec((lp, LANES), lambda n: (0, 0)),
                   pl.BlockSpec((lp, LANES), lambda n: (0, 0)),
                   pl.BlockSpec((1, N_Q_HEADS), lambda n: (0, 0))] + p_out,
        out_shape=[jax.ShapeDtypeStruct((lp, D_MODEL), BF16),
                   jax.ShapeDtypeStruct((lp, LANES), F32),
                   jax.ShapeDtypeStruct((lp, LANES), F32),
                   jax.ShapeDtypeStruct((1, N_Q_HEADS), F32)] + p_shapes,
        scratch_shapes=p_scr,
        compiler_params=pltpu.CompilerParams(dimension_semantics=("arbitrary",)),
    )(proj, proj, proj, sinks, dmix, *(plan.ins if plan is not None else []))
    return res[:4], res[4:]


_ZW = D_SSM
_XBC_W = D_SSM + 2 * SSD_GROUPS * SSD_N
_DT_COL = (_ZW + _XBC_W) // LANES
EVEN_IN = 3 * D_MODEL + 2 * LANES
ODD_IN = _ZW + _XBC_W + SSD_HEADS
ODD_IN_PAD = _ZW + _XBC_W + LANES


def _ssm_prep_fwd(xc, proj, dt_bias, name):
    lp = xc.shape[0]
    tm = _pick(lp, (320, 256, 128))

    def body(r, first, xc_ref, dtr_ref, b_ref, act_ref, dt_ref):
        real = _real_rows(r, tm)
        act, _ = _silu_and_grad(xc_ref[...])
        act_ref[...] = jnp.where(real, act, 0.0)
        dt_ref[...] = jnp.where(real, _softplus(dtr_ref[...] + b_ref[...]), 0.0)

    return _rowcall(name, body, lp, tm, rows=[(xc, _XBC_W, 0), (proj, LANES, _DT_COL)],
                    vecs=[dt_bias], outs=[(_XBC_W, F32), (LANES, F32)])


def _ssm_prep_bwd(xc, proj, dt_bias, dxs, dxskip, db, dc, ddt, name):
    lp = xc.shape[0]
    tm = BLOCK

    def body(r, first, xc_ref, dtr_ref, dxs_ref, dsk_ref, db_ref, dc_ref, ddt_ref, b_ref,
             dxc_ref, ddtr_ref, dbias_ref):
        real = _real_rows(r, tm)
        _, ds = _silu_and_grad(xc_ref[...])
        up = lambda ref: ref[...].astype(F32)
        dxc_ref[:, :D_SSM] = jnp.where(
            real, (up(dxs_ref) + up(dsk_ref)) * ds[:, :D_SSM], 0.0).astype(dxc_ref.dtype)
        dxc_ref[:, D_SSM:D_SSM + 1024] = jnp.where(
            real, up(db_ref) * ds[:, D_SSM:D_SSM + 1024], 0.0).astype(dxc_ref.dtype)
        dxc_ref[:, D_SSM + 1024:] = jnp.where(
            real, up(dc_ref) * ds[:, D_SSM + 1024:], 0.0).astype(dxc_ref.dtype)
        dd = jnp.where(real, ddt_ref[...] * _sigmoid(dtr_ref[...] + b_ref[...]), 0.0)
        ddtr_ref[...] = dd.astype(ddtr_ref.dtype)
        _acc_add(first, dbias_ref, jnp.sum(dd, axis=0, keepdims=True))

    return _rowcall(name, body, lp, tm,
                    rows=[(xc, _XBC_W, 0), (proj, LANES, _DT_COL), (dxs, D_SSM, 0), (dxskip, D_SSM, 0),
                          (db, 1024, 0), (dc, 1024, 0), (ddt, LANES, 0)],
                    vecs=[dt_bias], outs=[(_XBC_W, BF16), (LANES, BF16)], accs=[((1, LANES), F32)])


def _ssd_common(dt, alog):
    a = -jnp.exp(alog)
    cs = _cumsum_rows(dt * a, BLOCK)
    cst = cs.T
    cl = cs[BLOCK - 1:BLOCK, :]
    tril = (lax.broadcasted_iota(jnp.int32, (BLOCK, BLOCK), 0)
            >= lax.broadcasted_iota(jnp.int32, (BLOCK, BLOCK), 1))
    return a, cs, cst, cl, jnp.exp(cs), jnp.exp(cl - cs), jnp.exp(cl), tril


def _head_cols(ecl, g):
    lane = lax.broadcasted_iota(jnp.int32, (1, SSD_HPG * SSD_P), 1)
    e = [ecl[:, SSD_HPG * g + hh:SSD_HPG * g + hh + 1] for hh in range(SSD_HPG)]
    return jnp.where(lane < SSD_P, e[0], jnp.where(lane < 2 * SSD_P, e[1],
                                                   jnp.where(lane < 3 * SSD_P, e[2], e[3])))


def _ssd_fwd(xbc, dt, alog, name, plan=None):
    lp = xbc.shape[0]
    nc = lp // BLOCK
    gw = SSD_HPG * SSD_P
    p_in, p_shapes, p_out, p_scr = _plan_parts(plan)

    def body(*refs):
        xs_ref, b_ref, c_ref, dt_ref, alog_ref = refs[:5]
        cins = refs[5:5 + len(p_in)]
        y_ref, so_ref = refs[5 + len(p_in):7 + len(p_in)]
        couts = refs[7 + len(p_in):7 + len(p_in) + len(p_out)]
        st, fx = refs[7 + len(p_in) + len(p_out):9 + len(p_in) + len(p_out)]
        sems = refs[9 + len(p_in) + len(p_out):]
        n = pl.program_id(0)

        @pl.when(n == 0)
        def _():
            st[...] = jnp.zeros_like(st)
            if plan is not None:
                plan.start(cins, couts, sems)

        dtv = dt_ref[...]
        _, cs, cst, cl, e, f, ecl, tril = _ssd_common(dtv, alog_ref[...])
        for g in range(SSD_GROUPS):
            bg = b_ref[:, g * SSD_N:(g + 1) * SSD_N].astype(BF16)
            cg = c_ref[:, g * SSD_N:(g + 1) * SSD_N].astype(BF16)
            gm = _dot_nt(cg, bg)
            stg = st[g]
            so_ref[0, g] = stg
            yoff = _dot(cg, stg.astype(BF16))
            heads = [SSD_HPG * g + hh for hh in range(SSD_HPG)]
            cols = lambda v: jnp.stack([v[:, h:h + 1] for h in heads])
            x4 = jnp.stack([xs_ref[:, h * SSD_P:(h + 1) * SSD_P] for h in heads])
            csr = jnp.stack([cst[h:h + 1, :] for h in heads])
            m = gm[None] * jnp.exp(jnp.where(tril[None], cols(cs) - csr, NEG))
            xdt = x4 * cols(dtv)
            yoff4 = jnp.stack([yoff[:, hh * SSD_P:(hh + 1) * SSD_P] for hh in range(SSD_HPG)])
            y4 = (jnp.einsum("hls,hsp->hlp", m.astype(BF16), xdt.astype(BF16), preferred_element_type=F32)
                  + cols(e) * yoff4)
            fx4 = cols(f) * xdt
            for hh, h in enumerate(heads):
                y_ref[:, h * SSD_P:(h + 1) * SSD_P] = y4[hh]
                fx[:, hh * SSD_P:(hh + 1) * SSD_P] = fx4[hh]
            st[g] = stg * _head_cols(ecl, g) + _dot_tn(bg, fx[...].astype(BF16))
        if plan is not None:
            @pl.when(n == nc - 1)
            def _():
                plan.wait(cins, couts, sems)

    res = pl.pallas_call(
        body, name=name, grid=(nc,),
        in_specs=[pl.BlockSpec((BLOCK, D_SSM), lambda n: (n, 0)),
                  pl.BlockSpec((BLOCK, 1024), lambda n: (n, 2)),
                  pl.BlockSpec((BLOCK, 1024), lambda n: (n, 3)),
                  pl.BlockSpec((BLOCK, LANES), lambda n: (n, 0)),
                  pl.BlockSpec((1, LANES), lambda n: (0, 0))] + p_in,
        out_specs=[pl.BlockSpec((BLOCK, D_SSM), lambda n: (n, 0)),
                   pl.BlockSpec((1, SSD_GROUPS, SSD_N, gw), lambda n: (n, 0, 0, 0))] + p_out,
        out_shape=[jax.ShapeDtypeStruct((lp, D_SSM), F32),
                   jax.ShapeDtypeStruct((nc, SSD_GROUPS, SSD_N, gw), F32)] + p_shapes,
        scratch_shapes=[pltpu.VMEM((SSD_GROUPS, SSD_N, gw), F32), pltpu.VMEM((BLOCK, gw), F32)] + p_scr,
        compiler_params=pltpu.CompilerParams(dimension_semantics=("arbitrary",)),
    )(xbc, xbc, xbc, dt, alog, *(plan.ins if plan is not None else []))
    return res[:2], res[2:]


def _ssd_bwd(xbc, dt, alog, states, dy, name, plan=None):
    lp = xbc.shape[0]
    nc = lp // BLOCK
    gw = SSD_HPG * SSD_P
    p_in, p_shapes, p_out, p_scr = _plan_parts(plan)

    def body(*refs):
        xs_ref, b_ref, c_ref, dt_ref, alog_ref, dy_ref, st_ref = refs[:7]
        cins = refs[7:7 + len(p_in)]
        dxs_ref, db_ref, dc_ref, ddt_ref, dalog_ref = refs[7 + len(p_in):12 + len(p_in)]
        couts = refs[12 + len(p_in):12 + len(p_in) + len(p_out)]
        dst, edy, fx = refs[12 + len(p_in) + len(p_out):15 + len(p_in) + len(p_out)]
        sems = refs[15 + len(p_in) + len(p_out):]
        i = pl.program_id(0)

        @pl.when(i == 0)
        def _():
            dst[...] = jnp.zeros_like(dst)
            dalog_ref[...] = jnp.zeros_like(dalog_ref)
            if plan is not None:
                plan.start(cins, couts, sems)

        dtv = dt_ref[...]
        a, cs, cst, cl, e, f, ecl, tril = _ssd_common(dtv, alog_ref[...])
        lane = lax.broadcasted_iota(jnp.int32, (1, LANES), 1)
        sub = _row_iota(BLOCK)
        triu = (lax.broadcasted_iota(jnp.int32, (BLOCK, BLOCK), 1)
                >= lax.broadcasted_iota(jnp.int32, (BLOCK, BLOCK), 0))
        dcs = jnp.zeros((BLOCK, LANES), F32)
        dcst = jnp.zeros((LANES, BLOCK), F32)
        dcl = jnp.zeros((1, LANES), F32)
        ddtx = jnp.zeros((BLOCK, LANES), F32)
        for g in range(SSD_GROUPS):
            bg = b_ref[:, g * SSD_N:(g + 1) * SSD_N].astype(BF16)
            cg = c_ref[:, g * SSD_N:(g + 1) * SSD_N].astype(BF16)
            gm = _dot_nt(cg, bg)
            stg = st_ref[0, g]
            stb = stg.astype(BF16)
            dso = dst[g]
            dsob = dso.astype(BF16)
            yraw = _dot(cg, stb)
            dfx = _dot(bg, dsob)
            prodsum = jnp.sum(dso * stg, axis=0, keepdims=True)
            heads = [SSD_HPG * g + hh for hh in range(SSD_HPG)]
            cols = lambda v: jnp.stack([v[:, h:h + 1] for h in heads])
            parts = lambda v: jnp.stack([v[:, hh * SSD_P:(hh + 1) * SSD_P] for hh in range(SSD_HPG)])
            x4 = jnp.stack([xs_ref[:, h * SSD_P:(h + 1) * SSD_P] for h in heads])
            dy4 = jnp.stack([dy_ref[:, h * SSD_P:(h + 1) * SSD_P] for h in heads])
            csc, dtc, ec, fc = cols(cs), cols(dtv), cols(e), cols(f)
            csr = jnp.stack([cst[h:h + 1, :] for h in heads])
            seg = csc - csr
            lam = jnp.exp(jnp.where(tril[None], seg, NEG))
            lamt = jnp.exp(jnp.where(triu[None], -seg, NEG))
            m = gm[None] * lam
            mt = _dot_nt(bg, cg)[None] * lamt
            xdt = x4 * dtc
            dyb = dy4.astype(BF16)
            dm = jnp.einsum("hlp,hsp->hls", dyb, xdt.astype(BF16), preferred_element_type=F32)
            dfx4 = parts(dfx)
            dxdt = jnp.einsum("hsl,hlp->hsp", mt.astype(BF16), dyb, preferred_element_type=F32) + fc * dfx4
            w = dm * m
            dgm = jnp.sum(dm * lam, axis=0)
            dff = jnp.sum(dfx4 * xdt, axis=2, keepdims=True) * fc
            colv = (jnp.sum(w, axis=2, keepdims=True)
                    + jnp.sum(dy4 * parts(yraw), axis=2, keepdims=True) * ec - dff)
            roww = jnp.sum(w, axis=1, keepdims=True)
            ddtc = jnp.sum(dxdt * x4, axis=2, keepdims=True)
            dffs = jnp.sum(dff, axis=1, keepdims=True)
            dxs4 = dxdt * dtc
            edy4 = ec * dy4
            fx4 = fc * xdt
            for hh, h in enumerate(heads):
                ls = slice(hh * SSD_P, (hh + 1) * SSD_P)
                onl = (lane == h).astype(F32)
                dcs = dcs + colv[hh] * onl
                dcst = dcst - (sub == h).astype(F32) * roww[hh]
                dcl = dcl + (dffs[hh] + ecl[:, h:h + 1] * jnp.sum(prodsum[:, ls], axis=1, keepdims=True)) * onl
                ddtx = ddtx + ddtc[hh] * onl
                dxs_ref[:, h * SSD_P:(h + 1) * SSD_P] = dxs4[hh].astype(dxs_ref.dtype)
                edy[:, ls] = edy4[hh]
                fx[:, ls] = fx4[hh]
            edyb = edy[...].astype(BF16)
            fxb = fx[...].astype(BF16)
            dgb = dgm.astype(BF16)
            dc_ref[:, g * SSD_N:(g + 1) * SSD_N] = (_dot_nt(edyb, stb) + _dot(dgb, bg)).astype(dc_ref.dtype)
            db_ref[:, g * SSD_N:(g + 1) * SSD_N] = (_dot_nt(fxb, dsob) + _dot_tn(dgb, cg)).astype(db_ref.dtype)
            dst[g] = dso * _head_cols(ecl, g) + _dot_tn(cg, edyb)
        dcs = dcs + dcst.T + jnp.where(sub == BLOCK - 1, dcl, 0.0)
        dda = _rev_cumsum_rows(dcs, BLOCK)
        ddt_ref[...] = ddtx + dda * a
        dalog_ref[...] += jnp.sum(dda * dtv, axis=0, keepdims=True) * a
        if plan is not None:
            @pl.when(i == nc - 1)
            def _():
                plan.wait(cins, couts, sems)

    rev = lambda i: nc - 1 - i
    res = pl.pallas_call(
        body, name=name, grid=(nc,),
        in_specs=[pl.BlockSpec((BLOCK, D_SSM), lambda i: (rev(i), 0)),
                  pl.BlockSpec((BLOCK, 1024), lambda i: (rev(i), 2)),
                  pl.BlockSpec((BLOCK, 1024), lambda i: (rev(i), 3)),
                  pl.BlockSpec((BLOCK, LANES), lambda i: (rev(i), 0)),
                  pl.BlockSpec((1, LANES), lambda i: (0, 0)),
                  pl.BlockSpec((BLOCK, D_SSM), lambda i: (rev(i), 0)),
                  pl.BlockSpec((1, SSD_GROUPS, SSD_N, gw), lambda i: (rev(i), 0, 0, 0))] + p_in,
        out_specs=[pl.BlockSpec((BLOCK, D_SSM), lambda i: (rev(i), 0)),
                   pl.BlockSpec((BLOCK, 1024), lambda i: (rev(i), 0)),
                   pl.BlockSpec((BLOCK, 1024), lambda i: (rev(i), 0)),
                   pl.BlockSpec((BLOCK, LANES), lambda i: (rev(i), 0)),
                   pl.BlockSpec((1, LANES), lambda i: (0, 0))] + p_out,
        out_shape=[jax.ShapeDtypeStruct((lp, D_SSM), BF16),
                   jax.ShapeDtypeStruct((lp, 1024), BF16),
                   jax.ShapeDtypeStruct((lp, 1024), BF16),
                   jax.ShapeDtypeStruct((lp, LANES), F32),
                   jax.ShapeDtypeStruct((1, LANES), F32)] + p_shapes,
        scratch_shapes=[pltpu.VMEM((SSD_GROUPS, SSD_N, gw), F32),
                        pltpu.VMEM((BLOCK, gw), F32), pltpu.VMEM((BLOCK, gw), F32)] + p_scr,
        compiler_params=pltpu.CompilerParams(dimension_semantics=("arbitrary",)),
    )(xbc, xbc, xbc, dt, alog, dy, states, *(plan.ins if plan is not None else []))
    return res[:5], res[5:]


_GN_GROUPS = 8
_GN_W = D_SSM // _GN_GROUPS


def _ssm_gate_fwd(yssd, xbc, proj, dskip, gnorm, name):
    lp = yssd.shape[0]
    tm = _pick(lp, (320, 256, 128))

    def body(r, first, y_ref, x_ref, z_ref, d_ref, g_ref, o_ref):
        sz, _ = _silu_and_grad(z_ref[...])
        y2 = (y_ref[...] + d_ref[...] * x_ref[...]) * sz
        for k in range(_GN_GROUPS):
            sl = slice(k * _GN_W, (k + 1) * _GN_W)
            yk = y2[:, sl]
            rs = lax.rsqrt(jnp.mean(yk * yk, axis=-1, keepdims=True) + EPS)
            o_ref[:, sl] = (yk * rs * g_ref[:, sl]).astype(o_ref.dtype)

    return _rowcall(name, body, lp, tm, rows=[(yssd, D_SSM, 0), (xbc, D_SSM, 0), (proj, D_SSM, 0)],
                    vecs=[dskip, gnorm], outs=[(D_SSM, BF16)])[0]


def _ssm_gate_bwd(yssd, xbc, proj, dskip, gnorm, dyn, name):
    lp = yssd.shape[0]
    tm = BLOCK

    def body(r, first, y_ref, x_ref, z_ref, dyn_ref, d_ref, g_ref,
             dy_ref, dx_ref, dz_ref, dd_ref, dg_ref):
        z = z_ref[...]
        sz, dsz = _silu_and_grad(z)
        xs = x_ref[...]
        y1 = y_ref[...] + d_ref[...] * xs
        y2 = y1 * sz
        dyn = dyn_ref[...]
        for k in range(_GN_GROUPS):
            sl = slice(k * _GN_W, (k + 1) * _GN_W)
            dx, dgt = _rms_bwd(y2[:, sl], g_ref[:, sl], dyn[:, sl])
            dy1 = dx * sz[:, sl]
            dy_ref[:, sl] = dy1.astype(dy_ref.dtype)
            dx_ref[:, sl] = (dy1 * d_ref[:, sl]).astype(dx_ref.dtype)
            dz_ref[:, sl] = (dx * y1[:, sl] * dsz[:, sl]).astype(dz_ref.dtype)

            @pl.when(first)
            def _():
                dd_ref[:, sl] = jnp.zeros((1, _GN_W), F32)
                dg_ref[:, sl] = jnp.zeros((1, _GN_W), F32)

            dd_ref[:, sl] += jnp.sum(dy1 * xs[:, sl], axis=0, keepdims=True)
            dg_ref[:, sl] += jnp.sum(dgt, axis=0, keepdims=True)

    return _rowcall(name, body, lp, tm,
                    rows=[(yssd, D_SSM, 0), (xbc, D_SSM, 0), (proj, D_SSM, 0), (dyn, D_SSM, 0)],
                    vecs=[dskip, gnorm], outs=[(D_SSM, BF16), (D_SSM, BF16), (D_SSM, BF16)],
                    accs=[((1, D_SSM), F32), ((1, D_SSM), F32)])


def _adamw(w, g, m, v, name):
    r, c = w.shape
    tm = r if r <= 512 else _pick(r, (512, 352, 256, 128, 64, 32, 16, 8))
    c1 = 1.0 / (1.0 - ADAM_B1 ** ADAM_STEP)
    c2 = 1.0 / (1.0 - ADAM_B2 ** ADAM_STEP)

    def body(w_ref, g_ref, m_ref, v_ref, d_ref, nm_ref, nv_ref):
        gv = g_ref[...]
        nm = ADAM_B1 * m_ref[...] + (1.0 - ADAM_B1) * gv
        nv = ADAM_B2 * v_ref[...] + (1.0 - ADAM_B2) * (gv * gv)
        nm_ref[...] = nm
        nv_ref[...] = nv
        d_ref[...] = -ADAM_LR * ((nm * c1) / (jnp.sqrt(nv * c2) + ADAM_EPS) + ADAM_WD * w_ref[...])

    spec = pl.BlockSpec((tm, c), lambda i: (i, 0))
    return pl.pallas_call(
        body, name=name, grid=(r // tm,), in_specs=[spec] * 4, out_specs=[spec] * 3,
        out_shape=[jax.ShapeDtypeStruct((r, c), F32)] * 3,
        compiler_params=pltpu.CompilerParams(dimension_semantics=("parallel",)),
    )(w, g, m, v)


def _place():
    return lax.axis_index("x"), lax.axis_index("y"), lax.axis_index("c")


def _other_chips(x, y):
    return [(1 - x, y), (x, 1 - y), (1 - x, 1 - y)]


_ANY = pl.BlockSpec(memory_space=pl.ANY)


class _Plan:
    def __init__(self, ins, out_shapes, n_remote, n_local, issue):
        self.ins = list(ins)
        self.out_shapes = list(out_shapes)
        self.issue = issue
        self.scratch = [pltpu.SemaphoreType.DMA((max(n_remote, 1),)),
                        pltpu.SemaphoreType.DMA((max(n_remote, 1),)),
                        pltpu.SemaphoreType.DMA((max(n_local, 1),))]

    def start(self, ins, outs, sems):
        sends, _, locs = self.issue(ins, outs, *sems)
        for cp in locs + sends:
            cp.start()

    def wait(self, ins, outs, sems):
        sends, recvs, locs = self.issue(ins, outs, *sems)
        for make in recvs:
            make().wait_recv()
        for cp in sends:
            cp.wait_send()
        for cp in locs:
            cp.wait()


def _plan_parts(plan):
    if plan is None:
        return [], [], [], []
    return ([_ANY] * len(plan.ins), plan.out_shapes, [_ANY] * len(plan.out_shapes), plan.scratch)


def _run_plan(plan, name):
    n_in, n_out = len(plan.ins), len(plan.out_shapes)

    def body(*refs):
        ins, outs, sems = refs[:n_in], refs[n_in:n_in + n_out], refs[n_in + n_out:]
        plan.start(ins, outs, sems)
        plan.wait(ins, outs, sems)

    return pl.pallas_call(
        body, name=name, in_specs=[_ANY] * n_in, out_specs=[_ANY] * n_out,
        out_shape=plan.out_shapes, scratch_shapes=plan.scratch,
    )(*plan.ins)


def _gather_plan(shards):
    n = len(shards)

    def issue(ins, outs, send_sems, recv_sems, local_sems):
        x, y, c = _place()
        me = 2 * x + y
        sends, recvs, locs = [], [], []
        for p in range(n):
            locs.append(pltpu.make_async_copy(ins[p], outs[p].at[me], local_sems.at[p]))
            for k, (px, py) in enumerate(_other_chips(x, y)):
                sems = dict(send_sem=send_sems.at[3 * p + k], recv_sem=recv_sems.at[3 * p + k],
                            device_id=(px, py, c), device_id_type=MESH)
                sends.append(pltpu.make_async_remote_copy(src_ref=ins[p], dst_ref=outs[p].at[me], **sems))
                recvs.append(functools.partial(pltpu.make_async_remote_copy, src_ref=ins[p],
                                               dst_ref=outs[p].at[2 * px + py], **sems))
        return sends, recvs, locs

    return _Plan(shards, [jax.ShapeDtypeStruct((N_CHIPS,) + s.shape, s.dtype) for s in shards], 3 * n, n, issue)


_REL7 = [(fx, fy, fc) for fx in (0, 1) for fy in (0, 1) for fc in (0, 1)][1:]


def _scatter8_plan(gs):
    n = len(gs)

    def issue(ins, outs, send_sems, recv_sems, local_sems):
        x, y, c = _place()
        sends = []
        for p in range(n):
            hr = gs[p].shape[1] // 2
            for k, (fx, fy, fc) in enumerate(_REL7):
                tx, ty, tc = x ^ fx, y ^ fy, c ^ fc
                src = ins[p].at[2 * tx + ty, pl.ds(pl.multiple_of(tc * hr, SUBLANES), hr), :]
                sends.append(pltpu.make_async_remote_copy(
                    src_ref=src, dst_ref=outs[p].at[k],
                    send_sem=send_sems.at[7 * p + k], recv_sem=recv_sems.at[7 * p + k],
                    device_id=(tx, ty, tc), device_id_type=MESH))
        return sends, [functools.partial(lambda cp: cp, cp) for cp in sends], []

    shapes = [jax.ShapeDtypeStruct((7, g.shape[1] // 2, g.shape[2]), g.dtype) for g in gs]
    return _Plan(gs, shapes, 7 * n, 0, issue)


def _sibling_plan(ts):
    n = len(ts)

    def issue(ins, outs, send_sems, recv_sems, local_sems):
        x, y, c = _place()
        sends = [pltpu.make_async_remote_copy(
            src_ref=ins[p], dst_ref=outs[p], send_sem=send_sems.at[p], recv_sem=recv_sems.at[p],
            device_id=(x, y, 1 - c), device_id_type=MESH) for p in range(n)]
        return sends, [functools.partial(lambda cp: cp, cp) for cp in sends], []

    return _Plan(ts, [jax.ShapeDtypeStruct(t.shape, t.dtype) for t in ts], n, 0, issue)


def _add8(g, recv, chip, core, name):
    s, r, n = g.shape
    hr = r // 2
    th = hr // 2 if (hr // 2) % SUBLANES == 0 else hr
    nt = hr // th

    def body(chip_ref, core_ref, g_ref, r_ref, o_ref):
        acc = g_ref[0].astype(F32)
        for k in range(7):
            acc = acc + r_ref[k].astype(F32)
        o_ref[...] = acc

    return pl.pallas_call(
        body, name=name,
        grid_spec=pltpu.PrefetchScalarGridSpec(
            num_scalar_prefetch=2, grid=(nt,),
            in_specs=[pl.BlockSpec((1, th, n), lambda i, ch, co: (ch[0], co[0] * nt + i, 0)),
                      pl.BlockSpec((7, th, n), lambda i, ch, co: (0, i, 0))],
            out_specs=pl.BlockSpec((th, n), lambda i, ch, co: (i, 0))),
        out_shape=jax.ShapeDtypeStruct((hr, n), F32),
        compiler_params=pltpu.CompilerParams(dimension_semantics=("parallel",)),
    )(chip, core, g, recv)


def _adamw_halves(w, own, other, m, v, core, name):
    r, n = w.shape
    hr = r // 2
    th = hr // 2 if (hr // 2) % SUBLANES == 0 else hr
    tph = hr // th
    c1 = 1.0 / (1.0 - ADAM_B1 ** ADAM_STEP)
    c2 = 1.0 / (1.0 - ADAM_B2 ** ADAM_STEP)

    def body(core_ref, w_ref, a_ref, b_ref, m_ref, v_ref, g_ref, d_ref, nm_ref, nv_ref):
        half = pl.program_id(0) // tph
        gv = jnp.where(half == core_ref[0], a_ref[...], b_ref[...])
        nm = ADAM_B1 * m_ref[...] + (1.0 - ADAM_B1) * gv
        nv = ADAM_B2 * v_ref[...] + (1.0 - ADAM_B2) * (gv * gv)
        g_ref[...] = gv
        nm_ref[...] = nm
        nv_ref[...] = nv
        d_ref[...] = -ADAM_LR * ((nm * c1) / (jnp.sqrt(nv * c2) + ADAM_EPS) + ADAM_WD * w_ref[...])

    full = pl.BlockSpec((th, n), lambda i, co: (i, 0))
    part = pl.BlockSpec((th, n), lambda i, co: (i % tph, 0))
    return pl.pallas_call(
        body, name=name,
        grid_spec=pltpu.PrefetchScalarGridSpec(
            num_scalar_prefetch=1, grid=(2 * tph,),
            in_specs=[full, part, part, full, full], out_specs=[full] * 4),
        out_shape=[jax.ShapeDtypeStruct((r, n), F32)] * 4,
        compiler_params=pltpu.CompilerParams(dimension_semantics=("parallel",)),
    )(core, w, own, other, m, v)


def _allreduce_small(pack, name):
    r, l = pack.shape

    def body(p_ref, o_ref, land, send_sems, recv_sems):
        x, y, c = _place()
        me = 4 * x + 2 * y + c
        land[me] = p_ref[...]
        rel = [(fx, fy, fc) for fx in (0, 1) for fy in (0, 1) for fc in (0, 1)][1:]
        sends = []
        for k, (fx, fy, fc) in enumerate(rel):
            peer = (x ^ fx, y ^ fy, c ^ fc)
            cp = pltpu.make_async_remote_copy(
                src_ref=p_ref, dst_ref=land.at[me], send_sem=send_sems.at[k], recv_sem=recv_sems.at[k],
                device_id=peer, device_id_type=MESH)
            cp.start()
            sends.append(cp)
        for k, (fx, fy, fc) in enumerate(rel):
            src = 4 * (x ^ fx) + 2 * (y ^ fy) + (c ^ fc)
            pltpu.make_async_remote_copy(
                src_ref=p_ref, dst_ref=land.at[src], send_sem=send_sems.at[k], recv_sem=recv_sems.at[k],
                device_id=(x ^ fx, y ^ fy, c ^ fc), device_id_type=MESH).wait_recv()
        for cp in sends:
            cp.wait_send()
        acc = land[0]
        for d in range(1, N_DEV):
            acc = acc + land[d]
        o_ref[...] = acc

    vm = pl.BlockSpec(memory_space=pltpu.VMEM)
    return pl.pallas_call(
        body, name=name, in_specs=[vm], out_specs=vm,
        out_shape=jax.ShapeDtypeStruct((r, l), F32),
        scratch_shapes=[pltpu.VMEM((N_DEV, r, l), F32),
                        pltpu.SemaphoreType.DMA((N_DEV - 1,)), pltpu.SemaphoreType.DMA((N_DEV - 1,))],
    )(pack)


def _flat_rows(a, mult=SUBLANES * LANES):
    f = a.reshape(-1)
    padn = (-f.shape[0]) % mult
    if padn:
        f = jnp.concatenate([f, jnp.zeros((padn,), f.dtype)])
    return f


def _pack(arrs, mult=SUBLANES * LANES, total_mult=None):
    flat = [_flat_rows(a, mult) for a in arrs]
    sizes = [f.shape[0] for f in flat]
    if total_mult is not None:
        padn = (-sum(sizes)) % total_mult
        if padn:
            flat.append(jnp.zeros((padn,), flat[0].dtype))
    return jnp.concatenate(flat).reshape(-1, LANES), sizes


def _unpack(pack, shapes, sizes, lead=()):
    flat = pack.reshape(lead + (-1,))
    out, off = [], 0
    for shp, sz in zip(shapes, sizes):
        n = math.prod(shp)
        out.append(flat[..., off:off + n].reshape(lead + tuple(shp)))
        off += sz
    return out


def _cols_from_shards(g):
    s, k, n = g.shape
    return jnp.transpose(g, (1, 0, 2)).reshape(k, s * n)


def _cols_to_shards(w, s=N_CHIPS):
    k, n = w.shape
    return jnp.transpose(w.reshape(k, s, n // s), (1, 0, 2))


def _ffn_fwd(h, pre, post, w_up, cw, cb, w_down, tag):
    u = _rmsnorm_fwd(h, pre, f"{tag}_prenorm")
    hp = _mm_nn_sh(u, w_up, 2 * D_FF, f"{tag}_up")
    act, hg, hu = _ffn_convact_fwd(hp, cw, cb, f"{tag}_convact")
    o = _mm_nn(act, w_down, F32, f"{tag}_down")
    hn = _postnorm_res_fwd(h, o, post, f"{tag}_postnorm")
    return hn, (h, u, hp, hg, hu, act, o)


def _ffn_bwd(dh, saved, pre, post, w_up, cw, w_down, tag):
    h, u, hp, hg, hu, act, o = saved
    do, dpost = _postnorm_bwd(o, post, dh, f"{tag}_postnorm_bwd")
    dact = _mm_nt(do, w_down, f"{tag}_down_dx")
    dw_down = _mm_tn(act, do, f"{tag}_down_dw")
    dhg, dhu = _ffn_act_bwd(hg, hu, dact, f"{tag}_act_bwd")
    dxg, dwg, dbg = _conv_bwd(hp, 0, D_FF, dhg, cw, f"{tag}_conv_bwd_gate")
    dxu, dwu, dbu = _conv_bwd(hp, D_FF, D_FF, dhu, cw, f"{tag}_conv_bwd_up", w_col_off=D_FF)
    dhp = (dxg, dxu)
    dcw = jnp.concatenate([dwg, dwu], axis=1)
    dcb = jnp.concatenate([dbg, dbu], axis=1)
    du = _mm_nt_sh(dhp, w_up, f"{tag}_up_dx")
    dw_up = _mm_tn_sh(u, dhp, w_up.shape[2], f"{tag}_up_dw")
    dhn, dpre = _prenorm_bwd(h, pre, du, dh, f"{tag}_prenorm_bwd")
    return dhn, dict(pre=dpre, post=dpost, w_up=dw_up, conv_w=dcw[:3], conv_b=dcb, w_down=dw_down)


class _Exchange:
    GATHER_IN_ATTN = ("l0_w_out", "l0_ffn_w_up", "l0_ffn_w_down", "l1_w_in")
    GATHER_IN_SSD = ("l1_w_out", "l1_ffn_w_up", "l1_ffn_w_down")
    AFTER_L1_OUT = ("l1_ffn_w_up", "l1_ffn_w_down", "l1_w_out")
    AFTER_L0_OUT = ("l1_w_in", "l0_ffn_w_up", "l0_ffn_w_down", "l0_w_out")
    LAST = ("l0_w_in",)

    def __init__(self, late_shards):
        self.late = dict(late_shards)
        self.slabs = {}
        self.recv = {}

    def gather_plan(self, names):
        return _gather_plan([self.late[n] for n in names])

    def gathered(self, names, outs):
        return {n: (g if n in _BIG_COL else g.reshape(-1, g.shape[-1])) for n, g in zip(names, outs)}

    def scatter_plan(self, grads, names):
        for n in names:
            g = grads[n]
            self.slabs[n] = g if n in _BIG_COL else g.reshape(N_CHIPS, -1, g.shape[-1])
        return _scatter8_plan([self.slabs[n] for n in names])

    def scattered(self, names, outs):
        self.recv.update(zip(names, outs))


def _local_step(x, tgt, meta, P, ex=None):
    seq, d = x.shape
    lp = seq + BLOCK
    h0 = jnp.concatenate([jnp.zeros((PAD, d), F32), meta, x], axis=0)
    tgt_p = jnp.concatenate([jnp.zeros((BLOCK, d), F32), tgt], axis=0)

    u0 = _rmsnorm_fwd(h0, P["l0_mix_pre_norm"], "l0_mix_prenorm")
    proj0 = _mm_nn_sh(u0, P["l0_w_in"], EVEN_IN, "l0_in")
    xrc = _conv_fwd(proj0, D_MODEL, D_MODEL, P["l0_lru_conv_w"], P["l0_lru_conv_b"], "l0_lru_conv")
    lru_args = (P["l0_lru_w_a"], P["l0_lru_w_x"], P["l0_lru_b_a"], P["l0_lru_b_x"], P["l0_lru_lambda"])
    ya, hl = _lru_fwd(proj0, xrc, *lru_args, "l0_lru")
    yb, outs = _attn_fwd(proj0, P["l0_attn_sinks"], "l0_attn",
                         ex.gather_plan(ex.GATHER_IN_ATTN) if ex else None)
    if ex:
        P = {**P, **ex.gathered(ex.GATHER_IN_ATTN, outs)}
    o0 = _mm_nn((ya, yb), P["l0_w_out"], F32, "l0_out")
    h1 = _postnorm_res_fwd(h0, o0, P["l0_mix_post_norm"], "l0_mix_postnorm")
    h2, ffn0 = _ffn_fwd(h1, P["l0_ffn_pre_norm"], P["l0_ffn_post_norm"], P["l0_ffn_w_up"],
                        P["l0_ffn_conv_w"], P["l0_ffn_conv_b"], P["l0_ffn_w_down"], "l0_ffn")
    u2 = _rmsnorm_fwd(h2, P["l1_mix_pre_norm"], "l1_mix_prenorm")
    proj1 = _mm_nn_sh(u2, P["l1_w_in"], ODD_IN_PAD, "l1_in")
    xc1 = _conv_fwd(proj1, _ZW, _XBC_W, P["l1_ssm_conv_w"], P["l1_ssm_conv_b"], "l1_ssm_conv")
    xbc, dt = _ssm_prep_fwd(xc1, proj1, P["l1_dt_bias"], "l1_ssm_prep")
    (yssd, states), outs = _ssd_fwd(xbc, dt, P["l1_a_log"], "l1_ssd",
                                    ex.gather_plan(ex.GATHER_IN_SSD) if ex else None)
    if ex:
        P = {**P, **ex.gathered(ex.GATHER_IN_SSD, outs)}
    yn = _ssm_gate_fwd(yssd, xbc, proj1, P["l1_d_skip"], P["l1_gate_norm"], "l1_ssm_gate")
    o1 = _mm_nn(yn, P["l1_w_out"], F32, "l1_out")
    h3 = _postnorm_res_fwd(h2, o1, P["l1_mix_post_norm"], "l1_mix_postnorm")
    h4, ffn1 = _ffn_fwd(h3, P["l1_ffn_pre_norm"], P["l1_ffn_post_norm"], P["l1_ffn_w_up"],
                        P["l1_ffn_conv_w"], P["l1_ffn_conv_b"], P["l1_ffn_w_down"], "l1_ffn")
    dh4, loss_cols = _loss_fwd_bwd(h4, tgt_p, "loss")

    G = {}
    dh3, g = _ffn_bwd(dh4, ffn1, P["l1_ffn_pre_norm"], P["l1_ffn_post_norm"], P["l1_ffn_w_up"],
                      P["l1_ffn_conv_w"], P["l1_ffn_w_down"], "l1_ffn")
    for k, v in g.items():
        G["l1_ffn_" + (k + "_norm" if k in ("pre", "post") else k)] = v
    do1, G["l1_mix_post_norm"] = _postnorm_bwd(o1, P["l1_mix_post_norm"], dh3, "l1_mix_postnorm_bwd")
    dyn = _mm_nt(do1, P["l1_w_out"], "l1_out_dx")
    G["l1_w_out"] = _mm_tn(yn, do1, "l1_out_dw")
    dyssd, dxskip, dz, dd_cols, G["l1_gate_norm"] = _ssm_gate_bwd(
        yssd, xbc, proj1, P["l1_d_skip"], P["l1_gate_norm"], dyn, "l1_ssm_gate_bwd")
    G["l1_d_skip"] = dd_cols.reshape(SSD_HEADS, SSD_P).sum(axis=1)
    (dxs, dbm, dcm, ddt, dalog), outs = _ssd_bwd(
        xbc, dt, P["l1_a_log"], states, dyssd, "l1_ssd_bwd",
        ex.scatter_plan(G, ex.AFTER_L1_OUT) if ex else None)
    if ex:
        ex.scattered(ex.AFTER_L1_OUT, outs)
    G["l1_a_log"] = dalog[0, :SSD_HEADS]
    dxc, ddtr, dbias = _ssm_prep_bwd(xc1, proj1, P["l1_dt_bias"], dxs, dxskip, dbm, dcm, ddt,
                                     "l1_ssm_prep_bwd")
    G["l1_dt_bias"] = dbias[0, :SSD_HEADS]
    dxbc, dcw, dcb = _conv_bwd(proj1, _ZW, _XBC_W, dxc, P["l1_ssm_conv_w"], "l1_ssm_conv_bwd")
    G["l1_ssm_conv_w"] = dcw[:4]
    G["l1_ssm_conv_b"] = dcb
    dproj1 = jnp.concatenate([dz, dxbc, ddtr], axis=1)
    du2 = _mm_nt_sh(dproj1, P["l1_w_in"], "l1_in_dx")
    G["l1_w_in"] = _mm_tn_sh(u2, dproj1, ODD_IN // N_CHIPS, "l1_in_dw")
    dh2, G["l1_mix_pre_norm"] = _prenorm_bwd(h2, P["l1_mix_pre_norm"], du2, dh3, "l1_mix_prenorm_bwd")
    dh1, g = _ffn_bwd(dh2, ffn0, P["l0_ffn_pre_norm"], P["l0_ffn_post_norm"], P["l0_ffn_w_up"],
                      P["l0_ffn_conv_w"], P["l0_ffn_w_down"], "l0_ffn")
    for k, v in g.items():
        G["l0_ffn_" + (k + "_norm" if k in ("pre", "post") else k)] = v
    do0, G["l0_mix_post_norm"] = _postnorm_bwd(o0, P["l0_mix_post_norm"], dh1, "l0_mix_postnorm_bwd")
    dmix = _mm_nt(do0, P["l0_w_out"], "l0_out_dx")
    G["l0_w_out"] = jnp.concatenate([_mm_tn(ya, do0, "l0_out_dw_lru"), _mm_tn(yb, do0, "l0_out_dw_attn")], axis=0)
    (dgate, dxrc, G["l0_lru_w_a"], G["l0_lru_w_x"], G["l0_lru_b_a"], G["l0_lru_b_x"],
     G["l0_lru_lambda"]) = _lru_bwd(proj0, xrc, hl, dmix, *lru_args, "l0_lru_bwd")
    dxr, dcw, dcb = _conv_bwd(proj0, D_MODEL, D_MODEL, dxrc, P["l0_lru_conv_w"], "l0_lru_conv_bwd")
    G["l0_lru_conv_w"] = dcw[:4]
    G["l0_lru_conv_b"] = dcb
    (dq, dk, dv, G["l0_attn_sinks"]), outs = _attn_bwd(
        proj0, P["l0_attn_sinks"], dmix, "l0_attn_bwd",
        ex.scatter_plan(G, ex.AFTER_L0_OUT) if ex else None)
    if ex:
        ex.scattered(ex.AFTER_L0_OUT, outs)
    dproj0 = jnp.concatenate([dgate, dxr, dq, dk.astype(BF16), dv.astype(BF16)], axis=1)
    G["l0_w_in"] = _mm_tn_sh(u0, dproj0, EVEN_IN // N_CHIPS, "l0_in_dw")
    if ex:
        du0, outs = _mm_nt_sh(dproj0, P["l0_w_in"], "l0_in_dx", ex.scatter_plan(G, ex.LAST))
        ex.scattered(ex.LAST, outs)
    else:
        du0 = _mm_nt_sh(dproj0, P["l0_w_in"], "l0_in_dx")
    dh0, G["l0_mix_pre_norm"] = _prenorm_bwd(h0, P["l0_mix_pre_norm"], du0, dh1, "l0_mix_prenorm_bwd")
    return loss_cols, dh0[BLOCK:], dh0[PAD:BLOCK], G


_BIG_COL = ("l0_w_in", "l0_ffn_w_up", "l1_w_in", "l1_ffn_w_up")
_BIG_ROW = ("l0_w_out", "l0_ffn_w_down", "l1_w_out", "l1_ffn_w_down")
_BIG = ("l0_w_in", "l0_w_out", "l0_ffn_w_up", "l0_ffn_w_down",
        "l1_w_in", "l1_w_out", "l1_ffn_w_up", "l1_ffn_w_down")
_SMALL_SHARDED = ("meta_tokens", "l0_lru_conv_w", "l0_ffn_conv_w", "l1_ssm_conv_w", "l1_ffn_conv_w")
_WEIGHTS = ("meta_tokens", "l0_mix_pre_norm", "l0_mix_post_norm", "l0_w_in", "l0_lru_conv_w",
            "l0_lru_conv_b", "l0_lru_w_a", "l0_lru_b_a", "l0_lru_w_x", "l0_lru_b_x", "l0_lru_lambda",
            "l0_attn_sinks", "l0_w_out", "l0_ffn_pre_norm", "l0_ffn_post_norm", "l0_ffn_w_up",
            "l0_ffn_conv_w", "l0_ffn_conv_b", "l0_ffn_w_down", "l1_mix_pre_norm", "l1_mix_post_norm",
            "l1_w_in", "l1_ssm_conv_w", "l1_ssm_conv_b", "l1_dt_bias", "l1_a_log", "l1_d_skip",
            "l1_gate_norm", "l1_w_out", "l1_ffn_pre_norm", "l1_ffn_post_norm", "l1_ffn_w_up",
            "l1_ffn_conv_w", "l1_ffn_conv_b", "l1_ffn_w_down")
_REPL = tuple(n for n in _WEIGHTS if n not in _BIG and n not in _SMALL_SHARDED)


def _pad_lanes(v, n=LANES):
    return jnp.concatenate([v, jnp.zeros((n - v.shape[0],), v.dtype)]).reshape(1, n)


def _step(x, tgt, W, M, V):
    cx, cy, cc = _place()
    chip = 2 * cx + cy

    small_pack, small_sizes = _pack([W[n] for n in _SMALL_SHARDED])
    first = _run_plan(_gather_plan([W["l0_w_in"].astype(BF16), small_pack]), "gather_first")
    small_full = _unpack(first[1], [W[n].shape for n in _SMALL_SHARDED], small_sizes, lead=(N_CHIPS,))
    ex = _Exchange({n: W[n].astype(BF16) for n in _BIG if n != "l0_w_in"})

    P = {"l0_w_in": first[0]}
    for n, g in zip(_SMALL_SHARDED, small_full):
        P[n] = _cols_from_shards(g)
    for n in _REPL:
        v = W[n]
        P[n] = v.reshape(1, -1) if v.ndim == 1 else v
    P["l0_lru_w_a"] = W["l0_lru_w_a"].astype(BF16)
    P["l0_lru_w_x"] = W["l0_lru_w_x"].astype(BF16)
    P["l1_dt_bias"] = _pad_lanes(W["l1_dt_bias"])
    P["l1_a_log"] = _pad_lanes(W["l1_a_log"])
    P["l1_d_skip"] = jnp.repeat(W["l1_d_skip"], SSD_P).reshape(1, D_SSM)
    meta = P.pop("meta_tokens")

    loss_cols, grad_x, grad_meta, G = _local_step(x, tgt, meta, P, ex)
    G["meta_tokens"] = grad_meta

    core_idx = cc.astype(jnp.int32).reshape(1)
    chip_idx = chip.astype(jnp.int32).reshape(1)
    own_half = [_add8(ex.slabs[n], ex.recv[n], chip_idx, core_idx, f"grad_sum_{n}") for n in _BIG]
    other_half = _run_plan(_sibling_plan(own_half), "grad_sibling_swap")
    small_names = list(_REPL) + list(_SMALL_SHARDED)
    small_list = [G[n] for n in small_names] + [loss_cols]
    spack, ssizes = _pack(small_list)
    sred = _allreduce_small(spack, "small_allreduce")
    sfull = _unpack(sred, [a.shape for a in small_list], ssizes)
    loss = 0.5 / D_MODEL * jnp.sum(sfull[-1])
    small_grads = {}
    for n, g in zip(small_names, sfull[:-1]):
        if n in _SMALL_SHARDED:
            wcols = W[n].shape[1]
            g = lax.dynamic_slice_in_dim(g, chip * wcols, wcols, axis=1)
        small_grads[n] = g.reshape(W[n].shape)

    grads, delta, new_m, new_v = {}, {}, {}, {}
    for n, own, other in zip(_BIG, own_half, other_half):
        grads[n], delta[n], new_m[n], new_v[n] = _adamw_halves(
            W[n], own, other, M[n], V[n], core_idx, f"adamw_{n}")
    s_names = [n for n in _WEIGHTS if n not in _BIG]
    tile_elems = 512 * LANES
    wp, wsz = _pack([W[n] for n in s_names], total_mult=tile_elems)
    gp, _ = _pack([small_grads[n] for n in s_names], total_mult=tile_elems)
    mp, _ = _pack([M[n] for n in s_names], total_mult=tile_elems)
    vp, _ = _pack([V[n] for n in s_names], total_mult=tile_elems)
    dp, nmp, nvp = _adamw(wp, gp, mp, vp, "adamw_small")
    shapes = [W[n].shape for n in s_names]
    for n, a, b, c_ in zip(s_names, _unpack(dp, shapes, wsz), _unpack(nmp, shapes, wsz),
                           _unpack(nvp, shapes, wsz)):
        grads[n] = small_grads[n]
        delta[n], new_m[n], new_v[n] = a, b, c_
    return loss, grad_x, grads, delta, new_m, new_v


def kernel(x, meta_tokens, l0_mix_pre_norm, l0_mix_post_norm, l0_w_in, l0_lru_conv_w, l0_lru_conv_b, l0_lru_w_a, l0_lru_b_a, l0_lru_w_x, l0_lru_b_x, l0_lru_lambda, l0_attn_sinks, l0_w_out, l0_ffn_pre_norm, l0_ffn_post_norm, l0_ffn_w_up, l0_ffn_conv_w, l0_ffn_conv_b, l0_ffn_w_down, l1_mix_pre_norm, l1_mix_post_norm, l1_w_in, l1_ssm_conv_w, l1_ssm_conv_b, l1_dt_bias, l1_a_log, l1_d_skip, l1_gate_norm, l1_w_out, l1_ffn_pre_norm, l1_ffn_post_norm, l1_ffn_w_up, l1_ffn_conv_w, l1_ffn_conv_b, l1_ffn_w_down, loss_target, m_meta_tokens, m_l0_mix_pre_norm, m_l0_mix_post_norm, m_l0_w_in, m_l0_lru_conv_w, m_l0_lru_conv_b, m_l0_lru_w_a, m_l0_lru_b_a, m_l0_lru_w_x, m_l0_lru_b_x, m_l0_lru_lambda, m_l0_attn_sinks, m_l0_w_out, m_l0_ffn_pre_norm, m_l0_ffn_post_norm, m_l0_ffn_w_up, m_l0_ffn_conv_w, m_l0_ffn_conv_b, m_l0_ffn_w_down, m_l1_mix_pre_norm, m_l1_mix_post_norm, m_l1_w_in, m_l1_ssm_conv_w, m_l1_ssm_conv_b, m_l1_dt_bias, m_l1_a_log, m_l1_d_skip, m_l1_gate_norm, m_l1_w_out, m_l1_ffn_pre_norm, m_l1_ffn_post_norm, m_l1_ffn_w_up, m_l1_ffn_conv_w, m_l1_ffn_conv_b, m_l1_ffn_w_down, v_meta_tokens, v_l0_mix_pre_norm, v_l0_mix_post_norm, v_l0_w_in, v_l0_lru_conv_w, v_l0_lru_conv_b, v_l0_lru_w_a, v_l0_lru_b_a, v_l0_lru_w_x, v_l0_lru_b_x, v_l0_lru_lambda, v_l0_attn_sinks, v_l0_w_out, v_l0_ffn_pre_norm, v_l0_ffn_post_norm, v_l0_ffn_w_up, v_l0_ffn_conv_w, v_l0_ffn_conv_b, v_l0_ffn_w_down, v_l1_mix_pre_norm, v_l1_mix_post_norm, v_l1_w_in, v_l1_ssm_conv_w, v_l1_ssm_conv_b, v_l1_dt_bias, v_l1_a_log, v_l1_d_skip, v_l1_gate_norm, v_l1_w_out, v_l1_ffn_pre_norm, v_l1_ffn_post_norm, v_l1_ffn_w_up, v_l1_ffn_conv_w, v_l1_ffn_conv_b, v_l1_ffn_w_down):
    args = locals()
    W = {n: args[n] for n in _WEIGHTS}
    M = {n: args["m_" + n] for n in _WEIGHTS}
    V = {n: args["v_" + n] for n in _WEIGHTS}
    loss, grad_x, grads, delta, new_m, new_v = _step(x[0], loss_target[0], W, M, V)
    return (loss, grad_x[None], *[grads[n] for n in _WEIGHTS], *[delta[n] for n in _WEIGHTS],
            *[new_m[n] for n in _WEIGHTS], *[new_v[n] for n in _WEIGHTS])
```

```python
import functools
import math

import jax
import jax.numpy as jnp
from jax import lax
from jax.experimental import pallas as pl
from jax.experimental.pallas import tpu as pltpu

F32 = jnp.float32
BF16 = jnp.bfloat16

D_MODEL = 1024
N_META = 16
BLOCK = 128
PAD = BLOCK - N_META
EPS = 1e-6
LRU_BLOCKS = 8
LRU_BS = 128
LRU_C = 8.0
N_Q_HEADS = 16
N_KV_HEADS = 2
HEAD_DIM = 64
Q_PER_KV = 8
WINDOW = 128
D_SSM = 2048
SSD_HEADS = 32
SSD_GROUPS = 8
SSD_HPG = 4
SSD_P = 64
SSD_N = 128
D_FF = 2816
NEG = -1e30
LANES = 128
SUBLANES = 8
_VMEM_LIMIT_WIDE = 62 * 1024 * 1024

ADAM_LR = 0.001
ADAM_B1 = 0.9
ADAM_B2 = 0.999
ADAM_EPS = 1e-08
ADAM_WD = 0.01
ADAM_STEP = 10

MESH = pl.DeviceIdType.MESH
N_CHIPS = 4
N_DEV = 8


def _pick(n, cands):
    for c in cands:
        if n % c == 0:
            return c
    raise ValueError(f"no tile for {n} in {cands}")


def _col_tile(n, limit=1792):
    best = None
    for t in range(LANES, min(n, limit) + 1, LANES):
        if n % t == 0:
            best = t
    if best is None:
        raise ValueError(f"no lane tile for {n}")
    return best


def _sigmoid(x):
    return 0.5 + 0.5 * jnp.tanh(0.5 * x)


def _log1p(e):
    u = 1.0 + e
    return jnp.where(u == 1.0, e, jnp.log(u) * (e / jnp.where(u == 1.0, 1.0, u - 1.0)))


def _softplus(x):
    return jnp.maximum(x, 0.0) + _log1p(jnp.exp(-jnp.abs(x)))


def _neg_expm1(x):
    poly = x * (1.0 + x * (0.5 + x * (1.0 / 6.0 + x * (1.0 / 24.0 + x * (1.0 / 120.0)))))
    return -jnp.where(x > -0.05, poly, jnp.exp(x) - 1.0)


_GELU_C = math.sqrt(2.0 / math.pi)


def _gelu(x):
    u = 0.5 + 0.5 * jnp.tanh(x * (_GELU_C + (_GELU_C * 0.044715) * (x * x)))
    return x * u


def _gelu_and_grad(x):
    x2 = x * x
    u = 0.5 + 0.5 * jnp.tanh(x * (_GELU_C + (_GELU_C * 0.044715) * x2))
    g = x * u
    dg = u * (1.0 + (x - g) * (2.0 * _GELU_C + (6.0 * 0.044715 * _GELU_C) * x2))
    return g, dg


def _silu_and_grad(x):
    s = _sigmoid(x)
    return x * s, s * (1.0 + x * (1.0 - s))


def _dot(a, b):
    return jnp.dot(a, b, preferred_element_type=F32)


def _dot_nt(a, b):
    return lax.dot_general(a, b, (((1,), (1,)), ((), ())), preferred_element_type=F32)


def _dot_tn(a, b):
    return lax.dot_general(a, b, (((0,), (0,)), ((), ())), preferred_element_type=F32)


def _row_iota(t):
    return lax.broadcasted_iota(jnp.int32, (t, 1), 0)


def _scan_fwd(a, u, t):
    row = _row_iota(t)
    d = 1
    while d < t:
        m = row >= d
        u_sh = jnp.where(m, pltpu.roll(u, d, 0), 0.0)
        a_sh = jnp.where(m, pltpu.roll(a, d, 0), 1.0)
        u = u + a * u_sh
        a = a * a_sh
        d *= 2
    return a, u


def _scan_rev(c, x, t):
    row = _row_iota(t)
    d = 1
    while d < t:
        m = row < t - d
        x_sh = jnp.where(m, pltpu.roll(x, t - d, 0), 0.0)
        c_sh = jnp.where(m, pltpu.roll(c, t - d, 0), 1.0)
        x = x + c * x_sh
        c = c * c_sh
        d *= 2
    return c, x


def _cumsum_rows(x, t):
    row = _row_iota(t)
    d = 1
    while d < t:
        x = x + jnp.where(row >= d, pltpu.roll(x, d, 0), 0.0)
        d *= 2
    return x


def _rev_cumsum_rows(x, t):
    row = _row_iota(t)
    d = 1
    while d < t:
        x = x + jnp.where(row < t - d, pltpu.roll(x, t - d, 0), 0.0)
        d *= 2
    return x


def _rms_bwd(x, g, dy):
    rs = lax.rsqrt(jnp.mean(x * x, axis=-1, keepdims=True) + EPS)
    gy = dy * g
    dx = rs * gy - x * (rs * rs * rs) * jnp.mean(x * gy, axis=-1, keepdims=True)
    return dx, dy * x * rs


def _mm_nn(a, w, out_dtype, name):
    parts = a if isinstance(a, (tuple, list)) else (a,)
    m = parts[0].shape[0]
    k, n = w.shape
    tm = _pick(m, (640, 512, 256, 128))
    tn = _col_tile(n)
    offs = [sum(p.shape[1] for p in parts[:i]) for i in range(len(parts))]

    def body(*refs):
        w_ref, o_ref = refs[len(parts)], refs[len(parts) + 1]
        acc = None
        for a_ref, p, off in zip(refs, parts, offs):
            t = _dot(a_ref[...].astype(BF16), w_ref[off:off + p.shape[1], :])
            acc = t if acc is None else acc + t
        o_ref[...] = acc.astype(o_ref.dtype)

    return pl.pallas_call(
        body, name=name, grid=(n // tn, m // tm),
        in_specs=[pl.BlockSpec((tm, p.shape[1]), lambda j, i: (i, 0)) for p in parts]
        + [pl.BlockSpec((k, tn), lambda j, i: (0, j))],
        out_specs=pl.BlockSpec((tm, tn), lambda j, i: (i, j)),
        out_shape=jax.ShapeDtypeStruct((m, n), out_dtype),
        compiler_params=pltpu.CompilerParams(dimension_semantics=("parallel", "parallel")),
    )(*parts, w)


def _mm_nt(dy, w, name):
    m, n = dy.shape
    k = w.shape[0]
    wide = n > 3328
    tm = _pick(m, (320, 256, 128)) if wide else _pick(m, (640, 512, 256, 128))
    tk = _col_tile(k, 512 if wide else 1408)

    def body(dy_ref, w_ref, o_ref):
        o_ref[...] = _dot_nt(dy_ref[...].astype(BF16), w_ref[...])

    return pl.pallas_call(
        body, name=name, grid=(k // tk, m // tm),
        in_specs=[pl.BlockSpec((tm, n), lambda j, i: (i, 0)),
                  pl.BlockSpec((tk, n), lambda j, i: (j, 0))],
        out_specs=pl.BlockSpec((tm, tk), lambda j, i: (i, j)),
        out_shape=jax.ShapeDtypeStruct((m, k), F32),
        compiler_params=pltpu.CompilerParams(dimension_semantics=("parallel", "parallel")),
    )(dy, w)


def _mm_tn(a, dy, name):
    m, k = a.shape
    n = dy.shape[1]
    tm = _pick(m, (640, 512, 256, 128))
    tk = _col_tile(k, 1408)
    tn = _col_tile(n, 1664)
    nsteps = m // tm

    def body(a_ref, dy_ref, o_ref, acc):
        @pl.when(pl.program_id(2) == 0)
        def _():
            acc[...] = jnp.zeros_like(acc)

        acc[...] += _dot_tn(a_ref[...].astype(BF16), dy_ref[...].astype(BF16))

        @pl.when(pl.program_id(2) == nsteps - 1)
        def _():
            o_ref[...] = acc[...].astype(o_ref.dtype)

    return pl.pallas_call(
        body, name=name, grid=(k // tk, n // tn, nsteps),
        in_specs=[pl.BlockSpec((tm, tk), lambda kk, j, i: (i, kk)),
                  pl.BlockSpec((tm, tn), lambda kk, j, i: (i, j))],
        out_specs=pl.BlockSpec((tk, tn), lambda kk, j, i: (kk, j)),
        out_shape=jax.ShapeDtypeStruct((k, n), BF16),
        scratch_shapes=[pltpu.VMEM((tk, tn), F32)],
        compiler_params=pltpu.CompilerParams(
            dimension_semantics=("parallel", "parallel", "arbitrary")),
    )(a, dy)


def _mm_nn_sh(a, w4, n_out, name):
    m, k = a.shape
    s, _, n = w4.shape
    tm = _pick(m, (320, 256, 128))

    def body(a_ref, w_ref, o_ref):
        av = a_ref[...].astype(BF16)
        for j in range(s):
            o_ref[:, j * n:(j + 1) * n] = _dot(av, w_ref[j])
        if n_out > s * n:
            o_ref[:, s * n:] = jnp.zeros((tm, n_out - s * n), F32)

    return pl.pallas_call(
        body, name=name, grid=(m // tm,),
        in_specs=[pl.BlockSpec((tm, k), lambda i: (i, 0)),
                  pl.BlockSpec((s, k, n), lambda i: (0, 0, 0))],
        out_specs=pl.BlockSpec((tm, n_out), lambda i: (i, 0)),
        out_shape=jax.ShapeDtypeStruct((m, n_out), F32),
        compiler_params=pltpu.CompilerParams(dimension_semantics=("parallel",)),
    )(a, w4)


def _mm_nt_sh(dy, w4, name, plan=None):
    dys = dy if isinstance(dy, (tuple, list)) else (dy,)
    m = dys[0].shape[0]
    s, k, n = w4.shape
    tm = _pick(m, (640, 512, 256, 128))
    tk = _col_tile(k, 1024)
    where = _shard_columns(dys, s, n)
    p_in, p_shapes, p_out, p_scr = _plan_parts(plan)
    nd, nj, ni = len(dys), k // tk, m // tm

    def body(*refs):
        w_ref = refs[nd]
        cins = refs[nd + 1:nd + 1 + len(p_in)]
        o_ref = refs[nd + 1 + len(p_in)]
        couts = refs[nd + 2 + len(p_in):nd + 2 + len(p_in) + len(p_out)]
        sems = refs[nd + 2 + len(p_in) + len(p_out):]
        step = pl.program_id(0) * ni + pl.program_id(1)
        if plan is not None:
            @pl.when(step == 0)
            def _():
                plan.start(cins, couts, sems)

        acc = None
        for j, (p, c0) in enumerate(where):
            t = _dot_nt(refs[p][:, c0:c0 + n].astype(BF16), w_ref[j])
            acc = t if acc is None else acc + t
        o_ref[...] = acc
        if plan is not None:
            @pl.when(step == nj * ni - 1)
            def _():
                plan.wait(cins, couts, sems)

    sem = ("arbitrary", "arbitrary") if plan is not None else ("parallel", "parallel")
    res = pl.pallas_call(
        body, name=name, grid=(nj, ni),
        in_specs=[pl.BlockSpec((tm, d.shape[1]), lambda j, i: (i, 0)) for d in dys]
        + [pl.BlockSpec((s, tk, n), lambda j, i: (0, j, 0))] + p_in,
        out_specs=[pl.BlockSpec((tm, tk), lambda j, i: (i, j))] + p_out,
        out_shape=[jax.ShapeDtypeStruct((m, k), F32)] + p_shapes,
        scratch_shapes=p_scr,
        compiler_params=pltpu.CompilerParams(dimension_semantics=sem),
    )(*dys, w4, *(plan.ins if plan is not None else []))
    return res[0] if plan is None else (res[0], res[1:])


def _shard_columns(dys, s, n):
    where = []
    for p, d in enumerate(dys):
        where += [(p, c * n) for c in range(d.shape[1] // n)]
    assert len(where) >= s
    return where[:s]


def _mm_tn_sh(a, dy, n, name):
    dys = dy if isinstance(dy, (tuple, list)) else (dy,)
    m, k = a.shape
    s = N_CHIPS
    tm = _pick(m, (640, 512, 256, 128))
    tk = _col_tile(k, 512)
    nsteps = m // tm
    where = _shard_columns(dys, s, n)

    def body(*refs):
        a_ref, o_ref, acc = refs[0], refs[len(dys) + 1], refs[len(dys) + 2]

        @pl.when(pl.program_id(1) == 0)
        def _():
            acc[...] = jnp.zeros_like(acc)

        av = a_ref[...].astype(BF16)
        for j, (p, c0) in enumerate(where):
            acc[j] += _dot_tn(av, refs[1 + p][:, c0:c0 + n].astype(BF16))

        @pl.when(pl.program_id(1) == nsteps - 1)
        def _():
            o_ref[...] = acc[...].astype(o_ref.dtype)

    return pl.pallas_call(
        body, name=name, grid=(k // tk, nsteps),
        in_specs=[pl.BlockSpec((tm, tk), lambda kk, i: (i, kk))]
        + [pl.BlockSpec((tm, d.shape[1]), lambda kk, i: (i, 0)) for d in dys],
        out_specs=pl.BlockSpec((s, tk, n), lambda kk, i: (0, kk, 0)),
        out_shape=jax.ShapeDtypeStruct((s, k, n), BF16),
        scratch_shapes=[pltpu.VMEM((s, tk, n), F32)],
        compiler_params=pltpu.CompilerParams(dimension_semantics=("parallel", "arbitrary"),
                                             vmem_limit_bytes=_VMEM_LIMIT_WIDE),
    )(a, *dys)


def _rowcall(name, body, lp, tm, rows=(), prevs=(), vecs=(), outs=(), accs=(), scratch=(),
             reverse=False, seq=False):
    nt = lp // tm
    hb = tm // SUBLANES

    def ri(i):
        return nt - 1 - i if reverse else i

    in_specs, args = [], []
    for arr, w, cb in rows:
        in_specs.append(pl.BlockSpec((tm, w), lambda i, cb=cb: (ri(i), cb)))
        args.append(arr)
    for arr, w, cb in prevs:
        in_specs.append(pl.BlockSpec((SUBLANES, w), lambda i, cb=cb: (jnp.maximum(ri(i) * hb - 1, 0), cb)))
        args.append(arr)
    for arr in vecs:
        in_specs.append(pl.BlockSpec(arr.shape, lambda i, nd=arr.ndim: (0,) * nd))
        args.append(arr)
    out_shape, out_specs = [], []
    for w, dt in outs:
        out_shape.append(jax.ShapeDtypeStruct((lp, w), dt))
        out_specs.append(pl.BlockSpec((tm, w), lambda i: (ri(i), 0)))
    for shp, dt in accs:
        out_shape.append(jax.ShapeDtypeStruct(shp, dt))
        out_specs.append(pl.BlockSpec(shp, lambda i, nd=len(shp): (0,) * nd))

    def kern(*refs):
        i = pl.program_id(0)
        body(ri(i), i == 0, *refs)

    sem = ("arbitrary",) if (seq or accs) else ("parallel",)
    res = pl.pallas_call(
        kern, name=name, grid=(nt,), in_specs=in_specs, out_specs=out_specs,
        out_shape=out_shape, scratch_shapes=list(scratch),
        compiler_params=pltpu.CompilerParams(dimension_semantics=sem),
    )(*args)
    return res


def _acc_add(first, ref, val):
    @pl.when(first)
    def _():
        ref[...] = jnp.zeros_like(ref)

    ref[...] += val


def _real_rows(r, tm):
    return (r * tm + _row_iota(tm)) >= PAD


def _rmsnorm_fwd(h, g, name):
    lp, d = h.shape
    tm = _pick(lp, (640, 512, 256, 128))

    def body(r, first, h_ref, g_ref, u_ref):
        x = h_ref[...]
        rs = lax.rsqrt(jnp.mean(x * x, axis=-1, keepdims=True) + EPS)
        u_ref[...] = (x * rs * g_ref[...]).astype(u_ref.dtype)

    return _rowcall(name, body, lp, tm, rows=[(h, d, 0)], vecs=[g], outs=[(d, BF16)])[0]


def _postnorm_res_fwd(h, o, g, name):
    lp, d = h.shape
    tm = _pick(lp, (640, 512, 256, 128))

    def body(r, first, h_ref, o_ref, g_ref, out_ref):
        x = o_ref[...]
        rs = lax.rsqrt(jnp.mean(x * x, axis=-1, keepdims=True) + EPS)
        out_ref[...] = jnp.where(_real_rows(r, tm), h_ref[...] + x * rs * g_ref[...], 0.0)

    return _rowcall(name, body, lp, tm, rows=[(h, d, 0), (o, d, 0)], vecs=[g], outs=[(d, F32)])[0]


def _postnorm_bwd(o, g, dh, name):
    lp, d = o.shape
    tm = _pick(lp, (640, 512, 256, 128))

    def body(r, first, o_ref, dh_ref, g_ref, do_ref, dg_ref):
        dx, dgt = _rms_bwd(o_ref[...], g_ref[...], dh_ref[...])
        do_ref[...] = dx.astype(do_ref.dtype)
        _acc_add(first, dg_ref, jnp.sum(dgt, axis=0, keepdims=True))

    return _rowcall(name, body, lp, tm, rows=[(o, d, 0), (dh, d, 0)], vecs=[g],
                    outs=[(d, BF16)], accs=[((1, d), F32)])


def _prenorm_bwd(h, g, du, dh_res, name):
    lp, d = h.shape
    tm = _pick(lp, (640, 512, 256, 128))

    def body(r, first, h_ref, du_ref, dres_ref, g_ref, dh_ref, dg_ref):
        dx, dgt = _rms_bwd(h_ref[...], g_ref[...], du_ref[...])
        dh_ref[...] = jnp.where(_real_rows(r, tm), dres_ref[...] + dx, 0.0)
        _acc_add(first, dg_ref, jnp.sum(dgt, axis=0, keepdims=True))

    return _rowcall(name, body, lp, tm, rows=[(h, d, 0), (du, d, 0), (dh_res, d, 0)], vecs=[g],
                    outs=[(d, F32)], accs=[((1, d), F32)])


def _loss_fwd_bwd(h, tgt, name):
    lp, d = h.shape
    tm = _pick(lp, (640, 512, 256, 128))

    def body(r, first, h_ref, t_ref, dh_ref, ls_ref):
        tok = (r * tm + _row_iota(tm)) >= BLOCK
        e = jnp.where(tok, h_ref[...] - t_ref[...], 0.0)
        dh_ref[...] = e * (1.0 / d)
        _acc_add(first, ls_ref, jnp.sum(e * e, axis=0, keepdims=True))

    return _rowcall(name, body, lp, tm, rows=[(h, d, 0), (tgt, d, 0)],
                    outs=[(d, F32)], accs=[((1, d), F32)])


def _conv_tiles(lp, width):
    wc = _col_tile(width, 1408)
    tm = _pick(lp, (320, 256, 128))
    return tm, wc


def _conv_fwd(x, col_off, width, w, b, name):
    lp = x.shape[0]
    kk = w.shape[0]
    tm, wc = _conv_tiles(lp, width)
    offb = col_off // wc
    assert col_off % wc == 0
    hb = tm // SUBLANES

    def body(x_ref, xp_ref, w_ref, b_ref, y_ref):
        i = pl.program_id(1)
        xv = x_ref[...]
        halo = jnp.where(i > 0, xp_ref[...], 0.0)
        xx = jnp.concatenate([halo, xv], axis=0)
        acc = b_ref[...] + w_ref[kk - 1:kk, :] * xv
        for j in range(1, kk):
            acc = acc + w_ref[kk - 1 - j:kk - j, :] * pltpu.roll(xx, j, 0)[SUBLANES:, :]
        y_ref[...] = acc

    return pl.pallas_call(
        body, name=name, grid=(width // wc, lp // tm),
        in_specs=[pl.BlockSpec((tm, wc), lambda j, i: (i, offb + j)),
                  pl.BlockSpec((SUBLANES, wc), lambda j, i: (jnp.maximum(i * hb - 1, 0), offb + j)),
                  pl.BlockSpec((kk, wc), lambda j, i: (0, j)),
                  pl.BlockSpec((1, wc), lambda j, i: (0, j))],
        out_specs=pl.BlockSpec((tm, wc), lambda j, i: (i, j)),
        out_shape=jax.ShapeDtypeStruct((lp, width), F32),
        compiler_params=pltpu.CompilerParams(dimension_semantics=("parallel", "parallel")),
    )(x, x, w, b)


def _conv_bwd(x, col_off, width, dy, w, name, w_col_off=0):
    lp = x.shape[0]
    kk = w.shape[0]
    tm, wc = _conv_tiles(lp, width)
    offb = col_off // wc
    woffb = w_col_off // wc
    assert col_off % wc == 0 and w_col_off % wc == 0
    hrows = SUBLANES * (4 // dy.dtype.itemsize)
    ext = tm + hrows

    def body(x_ref, dy_ref, dn_ref, w_ref, dx_ref, dw_ref, db_ref):
        i = pl.program_id(1)
        last = pl.num_programs(1) - 1
        xv = x_ref[...]
        dyv = dy_ref[...].astype(F32)
        dd = jnp.concatenate([dyv, jnp.where(i < last, dn_ref[...].astype(F32), 0.0)], axis=0)
        dx = w_ref[kk - 1:kk, :] * dyv
        rows = [jnp.sum(dyv * xv, axis=0, keepdims=True)]
        for m in range(1, kk):
            ahead = pltpu.roll(dd, ext - m, 0)[:tm, :]
            dx = dx + w_ref[kk - 1 - m:kk - m, :] * ahead
            rows.append(jnp.sum(ahead * xv, axis=0, keepdims=True))
        dx_ref[...] = dx.astype(dx_ref.dtype)
        dwp = jnp.concatenate(rows[::-1] + [jnp.zeros((SUBLANES - kk, wc), F32)], axis=0)

        @pl.when(i == 0)
        def _():
            dw_ref[...] = jnp.zeros_like(dw_ref)
            db_ref[...] = jnp.zeros_like(db_ref)

        dw_ref[...] += dwp
        db_ref[...] += jnp.sum(dyv, axis=0, keepdims=True)

    return pl.pallas_call(
        body, name=name, grid=(width // wc, lp // tm),
        in_specs=[pl.BlockSpec((tm, wc), lambda j, i: (i, offb + j)),
                  pl.BlockSpec((tm, wc), lambda j, i: (i, j)),
                  pl.BlockSpec((hrows, wc), lambda j, i: (jnp.minimum((i + 1) * (tm // hrows), lp // hrows - 1), j)),
                  pl.BlockSpec((kk, wc), lambda j, i: (0, woffb + j))],
        out_specs=[pl.BlockSpec((tm, wc), lambda j, i: (i, j)),
                   pl.BlockSpec((SUBLANES, wc), lambda j, i: (0, j)),
                   pl.BlockSpec((1, wc), lambda j, i: (0, j))],
        out_shape=[jax.ShapeDtypeStruct((lp, width), BF16),
                   jax.ShapeDtypeStruct((SUBLANES, width), F32),
                   jax.ShapeDtypeStruct((1, width), F32)],
        compiler_params=pltpu.CompilerParams(dimension_semantics=("parallel", "arbitrary")),
    )(x, dy, dy, w)


_FFN_K = 3
_FFN_WC = 1408


def _conv3_ext(x_ext, w_ref, b_ref):
    return (b_ref[...] + w_ref[2:3, :] * x_ext + w_ref[1:2, :] * pltpu.roll(x_ext, 1, 0)
            + w_ref[0:1, :] * pltpu.roll(x_ext, 2, 0))


def _ffn_convact_fwd(hp, cw, cb, name):
    lp = hp.shape[0]
    tm = _pick(lp, (320, 256, 128))
    wc = _FFN_WC
    nj = D_FF // wc
    hb = tm // SUBLANES

    def body(g_ref, gp_ref, u_ref, up_ref, wg_ref, wu_ref, bg_ref, bu_ref, a_ref, hg_ref, hu_ref):
        i = pl.program_id(1)

        def conv(x_ref, p_ref, w_ref, b_ref):
            x_ext = jnp.concatenate([jnp.where(i > 0, p_ref[...], 0.0), x_ref[...]], axis=0)
            return _conv3_ext(x_ext, w_ref, b_ref)[SUBLANES:, :]

        hg = conv(g_ref, gp_ref, wg_ref, bg_ref)
        hu = conv(u_ref, up_ref, wu_ref, bu_ref)
        a_ref[...] = (_gelu(hg) * hu).astype(a_ref.dtype)
        hg_ref[...] = hg.astype(hg_ref.dtype)
        hu_ref[...] = hu.astype(hu_ref.dtype)

    tile = lambda off: pl.BlockSpec((tm, wc), lambda j, i: (i, off + j))
    prev = lambda off: pl.BlockSpec((SUBLANES, wc), lambda j, i: (jnp.maximum(i * hb - 1, 0), off + j))
    vec = lambda rows, off: pl.BlockSpec((rows, wc), lambda j, i: (0, off + j))
    return pl.pallas_call(
        body, name=name, grid=(nj, lp // tm),
        in_specs=[tile(0), prev(0), tile(nj), prev(nj), vec(_FFN_K, 0), vec(_FFN_K, nj), vec(1, 0), vec(1, nj)],
        out_specs=[tile(0)] * 3,
        out_shape=[jax.ShapeDtypeStruct((lp, D_FF), BF16)] * 3,
        compiler_params=pltpu.CompilerParams(dimension_semantics=("parallel", "parallel")),
    )(hp, hp, hp, hp, cw, cw, cb, cb)


def _ffn_act_bwd(hg, hu, dact, name):
    lp = hg.shape[0]
    tm = _pick(lp, (320, 256, 128))

    def body(r, first, g_ref, u_ref, da_ref, dg_ref, du_ref):
        gl, dgl = _gelu_and_grad(g_ref[...].astype(F32))
        da = da_ref[...]
        dg_ref[...] = (da * u_ref[...].astype(F32) * dgl).astype(dg_ref.dtype)
        du_ref[...] = (da * gl).astype(du_ref.dtype)

    return _rowcall(name, body, lp, tm, rows=[(hg, D_FF, 0), (hu, D_FF, 0), (dact, D_FF, 0)],
                    outs=[(D_FF, BF16), (D_FF, BF16)])


def _lru_gates(x, wa_ref, wx_ref, ba, bx, lam):
    xb = x.astype(BF16)
    za, zx = [], []
    for n in range(LRU_BLOCKS):
        xs = xb[:, n * LRU_BS:(n + 1) * LRU_BS]
        za.append(_dot(xs, wa_ref[n]))
        zx.append(_dot(xs, wx_ref[n]))
    r = _sigmoid(jnp.concatenate(za, axis=1) + ba)
    ig = _sigmoid(jnp.concatenate(zx, axis=1) + bx)
    sp = _softplus(-lam)
    log_a = -LRU_C * r * sp
    a = jnp.exp(log_a)
    om = _neg_expm1(2.0 * log_a)
    mult = jnp.sqrt(om)
    return xb, r, ig, sp, a, om, mult


def _lru_fwd(proj, xrc, wa, wx, ba, bx, lam, name):
    lp, d = xrc.shape
    tm = BLOCK

    def body(r_idx, first, gate_ref, x_ref, wa_ref, wx_ref, ba_ref, bx_ref, lam_ref,
             y_ref, h_ref, carry):
        @pl.when(first)
        def _():
            carry[...] = jnp.zeros_like(carry)

        x = x_ref[...]
        _, _, ig, _, a, _, mult = _lru_gates(x, wa_ref, wx_ref, ba_ref[...], bx_ref[...], lam_ref[...])
        u = jnp.where(_real_rows(r_idx, tm), mult * ig * x, 0.0)
        acum, hloc = _scan_fwd(a, u, tm)
        h = hloc + acum * carry[0:1, :]
        h_ref[...] = h
        carry[0:1, :] = h[tm - 1:tm, :]
        y_ref[...] = (_gelu(gate_ref[...]) * h).astype(y_ref.dtype)

    return _rowcall(name, body, lp, tm, rows=[(proj, d, 0), (xrc, d, 0)],
                    vecs=[wa, wx, ba, bx, lam], outs=[(d, BF16), (d, F32)],
                    scratch=[pltpu.VMEM((SUBLANES, d), F32)], seq=True)


def _lru_bwd(proj, xrc, hl, dmix, wa, wx, ba, bx, lam, name):
    lp, d = xrc.shape
    tm = BLOCK

    def body(r_idx, first, gate_ref, x_ref, h_ref, dy_ref, hp_ref, wa_ref, wx_ref, ba_ref, bx_ref,
             lam_ref, dgate_ref, dx_ref, dwa_ref, dwx_ref, dba_ref, dbx_ref, dlam_ref, carry):
        @pl.when(first)
        def _():
            carry[...] = jnp.zeros_like(carry)
            dwa_ref[...] = jnp.zeros_like(dwa_ref)
            dwx_ref[...] = jnp.zeros_like(dwx_ref)
            dba_ref[...] = jnp.zeros_like(dba_ref)
            dbx_ref[...] = jnp.zeros_like(dbx_ref)
            dlam_ref[...] = jnp.zeros_like(dlam_ref)

        x = x_ref[...]
        lam = lam_ref[...]
        xb, r, ig, sp, a, om, mult = _lru_gates(x, wa_ref, wx_ref, ba_ref[...], bx_ref[...], lam)
        h = h_ref[...]
        dy = dy_ref[...]
        gl, dgl = _gelu_and_grad(gate_ref[...])
        dgate_ref[...] = (dy * h * dgl).astype(dgate_ref.dtype)
        row = _row_iota(tm)
        lastrow = row == tm - 1
        xg = dy * gl + jnp.where(lastrow, carry[0:1, :], 0.0)
        c = jnp.where(lastrow, 1.0, pltpu.roll(a, tm - 1, 0))
        _, g = _scan_rev(c, xg, tm)
        carry[0:1, :] = a[0:1, :] * g[0:1, :]
        hprev_in = jnp.where(r_idx > 0, hp_ref[SUBLANES - 1:SUBLANES, :], 0.0)
        hprev = jnp.where(row == 0, hprev_in, pltpu.roll(h, 1, 0))
        du = jnp.where(_real_rows(r_idx, tm), g, 0.0)
        da = g * hprev
        dmult = du * ig * x
        dig = du * mult * x
        dxv = du * mult * ig
        e2 = 1.0 - om
        dlog_a = da * a - dmult * e2 / mult
        dr = dlog_a * (-LRU_C) * sp
        dsp = jnp.sum(dlog_a * (-LRU_C) * r, axis=0, keepdims=True)
        dlam_ref[...] += -dsp * _sigmoid(-lam)
        dza = dr * r * (1.0 - r)
        dzx = dig * ig * (1.0 - ig)
        dba_ref[...] += jnp.sum(dza, axis=0, keepdims=True)
        dbx_ref[...] += jnp.sum(dzx, axis=0, keepdims=True)
        dzab = dza.astype(BF16)
        dzxb = dzx.astype(BF16)
        parts = []
        for n in range(LRU_BLOCKS):
            sl = slice(n * LRU_BS, (n + 1) * LRU_BS)
            dwa_ref[n] += _dot_tn(xb[:, sl], dzab[:, sl])
            dwx_ref[n] += _dot_tn(xb[:, sl], dzxb[:, sl])
            parts.append(_dot_nt(dzab[:, sl], wa_ref[n]) + _dot_nt(dzxb[:, sl], wx_ref[n]))
        dx_ref[...] = dxv + jnp.concatenate(parts, axis=1)

    return _rowcall(name, body, lp, tm,
                    rows=[(proj, d, 0), (xrc, d, 0), (hl, d, 0), (dmix, d, 0)],
                    prevs=[(hl, d, 0)], vecs=[wa, wx, ba, bx, lam],
                    outs=[(d, BF16), (d, F32)],
                    accs=[((LRU_BLOCKS, LRU_BS, LRU_BS), F32), ((LRU_BLOCKS, LRU_BS, LRU_BS), F32),
                          ((1, d), F32), ((1, d), F32), ((1, d), F32)],
                    scratch=[pltpu.VMEM((SUBLANES, d), F32)], reverse=True, seq=True)


_SLOPES = [2.0 ** (-8.0 * (h + 1) / N_Q_HEADS) for h in range(N_Q_HEADS)]
_QK_SCALE = HEAD_DIM ** -0.5
_QCOL = 2 * D_MODEL // D_MODEL
_KCOL = (3 * D_MODEL) // LANES
_VCOL = _KCOL + 1


def _attn_masks(n):
    start = pl.multiple_of(jnp.maximum(n - 1, 0) * BLOCK, BLOCK)
    qi = n * BLOCK + lax.broadcasted_iota(jnp.int32, (BLOCK, 2 * BLOCK), 0)
    kj = start + lax.broadcasted_iota(jnp.int32, (BLOCK, 2 * BLOCK), 1)
    dist = qi - kj
    ok = (kj >= BLOCK) & (dist >= 0) & (dist < WINDOW)
    dm = (n * BLOCK - PAD + lax.broadcasted_iota(jnp.int32, (BLOCK, N_META), 0)
          - lax.broadcasted_iota(jnp.int32, (BLOCK, N_META), 1))
    okm = dm >= 0
    return start, ok, dist.astype(F32), okm, jnp.minimum(dm, WINDOW).astype(F32)


def _group_rows(ref, g):
    return jnp.concatenate(
        [ref[:, (g * Q_PER_KV + hh) * HEAD_DIM:(g * Q_PER_KV + hh + 1) * HEAD_DIM] for hh in range(Q_PER_KV)],
        axis=0).astype(BF16)


def _attn_probs(qg, kg, kmg, sink_ref, g, ok, distf, okm, dmf):
    slope = jnp.stack([jnp.full((1, 1), _SLOPES[g * Q_PER_KV + hh], F32) for hh in range(Q_PER_KV)])
    sink = jnp.stack([sink_ref[0:1, g * Q_PER_KV + hh:g * Q_PER_KV + hh + 1] for hh in range(Q_PER_KV)])
    s = (_dot_nt(qg, kg) * _QK_SCALE).reshape(Q_PER_KV, BLOCK, 2 * BLOCK)
    sm = (_dot_nt(qg, kmg) * _QK_SCALE).reshape(Q_PER_KV, BLOCK, N_META)
    s = jnp.where(ok[None], s - slope * distf[None], NEG)
    sm = jnp.where(okm[None], sm - slope * dmf[None], NEG)
    mx = jnp.maximum(jnp.maximum(jnp.max(s, axis=-1, keepdims=True),
                                 jnp.max(sm, axis=-1, keepdims=True)), sink)
    p = jnp.exp(s - mx)
    pm = jnp.exp(sm - mx)
    ps = jnp.exp(sink - mx)
    inv = 1.0 / (jnp.sum(p, axis=-1, keepdims=True) + jnp.sum(pm, axis=-1, keepdims=True) + ps)
    return p * inv, pm * inv, ps * inv


def _attn_fwd(proj, sinks, name, plan=None):
    lp = proj.shape[0]
    nblk = lp // BLOCK
    p_in, p_shapes, p_out, p_scr = _plan_parts(plan)

    def body(*refs):
        q_ref, k_ref, v_ref, sink_ref = refs[:4]
        cins = refs[4:4 + len(p_in)]
        o_ref = refs[4 + len(p_in)]
        couts = refs[5 + len(p_in):5 + len(p_in) + len(p_out)]
        sems = refs[5 + len(p_in) + len(p_out):]
        n = pl.program_id(0)
        if plan is not None:
            @pl.when(n == 0)
            def _():
                plan.start(cins, couts, sems)

        start, ok, distf, okm, dmf = _attn_masks(n)
        kb = k_ref[pl.ds(start, 2 * BLOCK), :].astype(BF16)
        vb = v_ref[pl.ds(start, 2 * BLOCK), :].astype(BF16)
        km = k_ref[PAD:BLOCK, :].astype(BF16)
        vm = v_ref[PAD:BLOCK, :].astype(BF16)
        for g in range(N_KV_HEADS):
            gs = slice(g * HEAD_DIM, (g + 1) * HEAD_DIM)
            pn, pmn, _ = _attn_probs(_group_rows(q_ref, g), kb[:, gs], km[:, gs], sink_ref, g,
                                     ok, distf, okm, dmf)
            o = (_dot(pn.astype(BF16).reshape(Q_PER_KV * BLOCK, 2 * BLOCK), vb[:, gs])
                 + _dot(pmn.astype(BF16).reshape(Q_PER_KV * BLOCK, N_META), vm[:, gs]))
            for hh in range(Q_PER_KV):
                h = g * Q_PER_KV + hh
                o_ref[:, h * HEAD_DIM:(h + 1) * HEAD_DIM] = o[hh * BLOCK:(hh + 1) * BLOCK, :].astype(o_ref.dtype)
        if plan is not None:
            @pl.when(n == nblk - 1)
            def _():
                plan.wait(cins, couts, sems)

    res = pl.pallas_call(
        body, name=name, grid=(nblk,),
        in_specs=[pl.BlockSpec((BLOCK, D_MODEL), lambda n: (n, _QCOL)),
                  pl.BlockSpec((lp, LANES), lambda n: (0, _KCOL)),
                  pl.BlockSpec((lp, LANES), lambda n: (0, _VCOL)),
                  pl.BlockSpec(sinks.shape, lambda n: (0, 0))] + p_in,
        out_specs=[pl.BlockSpec((BLOCK, D_MODEL), lambda n: (n, 0))] + p_out,
        out_shape=[jax.ShapeDtypeStruct((lp, D_MODEL), BF16)] + p_shapes,
        scratch_shapes=p_scr,
        compiler_params=pltpu.CompilerParams(dimension_semantics=("arbitrary",)),
    )(proj, proj, proj, sinks, *(plan.ins if plan is not None else []))
    return res[0], res[1:]


def _attn_bwd(proj, sinks, dmix, name, plan=None):
    lp = proj.shape[0]
    nblk = lp // BLOCK

    p_in, p_shapes, p_out, p_scr = _plan_parts(plan)

    def body(*refs):
        q_ref, k_ref, v_ref, sink_ref, dy_ref = refs[:5]
        cins = refs[5:5 + len(p_in)]
        dq_ref, dk_ref, dv_ref, ds_ref = refs[5 + len(p_in):9 + len(p_in)]
        couts = refs[9 + len(p_in):9 + len(p_in) + len(p_out)]
        sems = refs[9 + len(p_in) + len(p_out):]
        n = pl.program_id(0)

        @pl.when(n == 0)
        def _():
            dk_ref[...] = jnp.zeros_like(dk_ref)
            dv_ref[...] = jnp.zeros_like(dv_ref)
            ds_ref[...] = jnp.zeros_like(ds_ref)
            if plan is not None:
                plan.start(cins, couts, sems)

        start, ok, distf, okm, dmf = _attn_masks(n)
        kb = k_ref[pl.ds(start, 2 * BLOCK), :].astype(BF16)
        vb = v_ref[pl.ds(start, 2 * BLOCK), :].astype(BF16)
        km = k_ref[PAD:BLOCK, :].astype(BF16)
        vm = v_ref[PAD:BLOCK, :].astype(BF16)
        lane16 = lax.broadcasted_iota(jnp.int32, (1, N_Q_HEADS), 1)
        dsink = jnp.zeros((1, N_Q_HEADS), F32)
        rows = Q_PER_KV * BLOCK
        for g in range(N_KV_HEADS):
            gs = slice(g * HEAD_DIM, (g + 1) * HEAD_DIM)
            qg = _group_rows(q_ref, g)
            dog = _group_rows(dy_ref, g)
            pn, pmn, psn = _attn_probs(qg, kb[:, gs], km[:, gs], sink_ref, g, ok, distf, okm, dmf)
            dp = _dot_nt(dog, vb[:, gs]).reshape(Q_PER_KV, BLOCK, 2 * BLOCK)
            dpm = _dot_nt(dog, vm[:, gs]).reshape(Q_PER_KV, BLOCK, N_META)
            delta = (jnp.sum(pn * dp, axis=-1, keepdims=True)
                     + jnp.sum(pmn * dpm, axis=-1, keepdims=True))
            dsb = (pn * (dp - delta)).astype(BF16).reshape(rows, 2 * BLOCK)
            dsm = (pmn * (dpm - delta)).astype(BF16).reshape(rows, N_META)
            dsk = jnp.sum(psn * delta, axis=1, keepdims=True)
            for hh in range(Q_PER_KV):
                dsink = dsink - jnp.where(lane16 == g * Q_PER_KV + hh, dsk[hh], 0.0)
            dq = (_dot(dsb, kb[:, gs]) + _dot(dsm, km[:, gs])) * _QK_SCALE
            for hh in range(Q_PER_KV):
                h = g * Q_PER_KV + hh
                dq_ref[:, h * HEAD_DIM:(h + 1) * HEAD_DIM] = dq[hh * BLOCK:(hh + 1) * BLOCK, :].astype(dq_ref.dtype)
            pnb = pn.astype(BF16).reshape(rows, 2 * BLOCK)
            pmnb = pmn.astype(BF16).reshape(rows, N_META)
            dk_ref[pl.ds(start, 2 * BLOCK), gs] += _dot_tn(dsb, qg) * _QK_SCALE
            dv_ref[pl.ds(start, 2 * BLOCK), gs] += _dot_tn(pnb, dog)
            dk_ref[PAD:BLOCK, gs] += _dot_tn(dsm, qg) * _QK_SCALE
            dv_ref[PAD:BLOCK, gs] += _dot_tn(pmnb, dog)
        ds_ref[...] += dsink
        if plan is not None:
            @pl.when(n == nblk - 1)
            def _():
                plan.wait(cins, couts, sems)

    res = pl.pallas_call(
        body, name=name, grid=(nblk,),
        in_specs=[pl.BlockSpec((BLOCK, D_MODEL), lambda n: (n, _QCOL)),
                  pl.BlockSpec((lp, LANES), lambda n: (0, _KCOL)),
                  pl.BlockSpec((lp, LANES), lambda n: (0, _VCOL)),
                  pl.BlockSpec(sinks.shape, lambda n: (0, 0)),
                  pl.BlockSpec((BLOCK, D_MODEL), lambda n: (n, 1))] + p_in,
        out_specs=[pl.BlockSpec((BLOCK, D_MODEL), lambda n: (n, 0)),
                   pl.BlockSpec((lp, LANES), lambda n: (0, 0)),
                   pl.BlockSpec((lp, LANES), lambda n: (0, 0)),
                   pl.BlockSpec((1, N_Q_HEADS), lambda n: (0, 0))] + p_out,
        out_shape=[jax.ShapeDtypeStruct((lp, D_MODEL), BF16),
                   jax.ShapeDtypeStruct((lp, LANES), F32),
                   jax.ShapeDtypeStruct((lp, LANES), F32),
                   jax.ShapeDtypeStruct((1, N_Q_HEADS), F32)] + p_shapes,
        scratch_shapes=p_scr,
        compiler_params=pltpu.CompilerParams(dimension_semantics=("arbitrary",)),
    )(proj, proj, proj, sinks, dmix, *(plan.ins if plan is not None else []))
    return res[:4], res[4:]


_ZW = D_SSM
_XBC_W = D_SSM + 2 * SSD_GROUPS * SSD_N
_DT_COL = (_ZW + _XBC_W) // LANES
EVEN_IN = 3 * D_MODEL + 2 * LANES
ODD_IN = _ZW + _XBC_W + SSD_HEADS
ODD_IN_PAD = _ZW + _XBC_W + LANES


def _ssm_prep_fwd(xc, proj, dt_bias, name):
    lp = xc.shape[0]
    tm = _pick(lp, (320, 256, 128))

    def body(r, first, xc_ref, dtr_ref, b_ref, act_ref, dt_ref):
        real = _real_rows(r, tm)
        act, _ = _silu_and_grad(xc_ref[...])
        act_ref[...] = jnp.where(real, act, 0.0)
        dt_ref[...] = jnp.where(real, _softplus(dtr_ref[...] + b_ref[...]), 0.0)

    return _rowcall(name, body, lp, tm, rows=[(xc, _XBC_W, 0), (proj, LANES, _DT_COL)],
                    vecs=[dt_bias], outs=[(_XBC_W, F32), (LANES, F32)])


def _ssm_prep_bwd(xc, proj, dt_bias, dxs, dxskip, db, dc, ddt, name):
    lp = xc.shape[0]
    tm = BLOCK

    def body(r, first, xc_ref, dtr_ref, dxs_ref, dsk_ref, db_ref, dc_ref, ddt_ref, b_ref,
             dxc_ref, ddtr_ref, dbias_ref):
        real = _real_rows(r, tm)
        _, ds = _silu_and_grad(xc_ref[...])
        up = lambda ref: ref[...].astype(F32)
        dxc_ref[:, :D_SSM] = jnp.where(
            real, (up(dxs_ref) + up(dsk_ref)) * ds[:, :D_SSM], 0.0).astype(dxc_ref.dtype)
        dxc_ref[:, D_SSM:D_SSM + 1024] = jnp.where(
            real, up(db_ref) * ds[:, D_SSM:D_SSM + 1024], 0.0).astype(dxc_ref.dtype)
        dxc_ref[:, D_SSM + 1024:] = jnp.where(
            real, up(dc_ref) * ds[:, D_SSM + 1024:], 0.0).astype(dxc_ref.dtype)
        dd = jnp.where(real, ddt_ref[...] * _sigmoid(dtr_ref[...] + b_ref[...]), 0.0)
        ddtr_ref[...] = dd.astype(ddtr_ref.dtype)
        _acc_add(first, dbias_ref, jnp.sum(dd, axis=0, keepdims=True))

    return _rowcall(name, body, lp, tm,
                    rows=[(xc, _XBC_W, 0), (proj, LANES, _DT_COL), (dxs, D_SSM, 0), (dxskip, D_SSM, 0),
                          (db, 1024, 0), (dc, 1024, 0), (ddt, LANES, 0)],
                    vecs=[dt_bias], outs=[(_XBC_W, BF16), (LANES, BF16)], accs=[((1, LANES), F32)])


def _ssd_common(dt, alog):
    a = -jnp.exp(alog)
    cs = _cumsum_rows(dt * a, BLOCK)
    cst = cs.T
    cl = cs[BLOCK - 1:BLOCK, :]
    tril = (lax.broadcasted_iota(jnp.int32, (BLOCK, BLOCK), 0)
            >= lax.broadcasted_iota(jnp.int32, (BLOCK, BLOCK), 1))
    return a, cs, cst, cl, jnp.exp(cs), jnp.exp(cl - cs), jnp.exp(cl), tril


def _head_cols(ecl, g):
    lane = lax.broadcasted_iota(jnp.int32, (1, SSD_HPG * SSD_P), 1)
    e = [ecl[:, SSD_HPG * g + hh:SSD_HPG * g + hh + 1] for hh in range(SSD_HPG)]
    return jnp.where(lane < SSD_P, e[0], jnp.where(lane < 2 * SSD_P, e[1],
                                                   jnp.where(lane < 3 * SSD_P, e[2], e[3])))


def _ssd_fwd(xbc, dt, alog, name, plan=None):
    lp = xbc.shape[0]
    nc = lp // BLOCK
    gw = SSD_HPG * SSD_P
    p_in, p_shapes, p_out, p_scr = _plan_parts(plan)

    def body(*refs):
        xs_ref, b_ref, c_ref, dt_ref, alog_ref = refs[:5]
        cins = refs[5:5 + len(p_in)]
        y_ref, so_ref = refs[5 + len(p_in):7 + len(p_in)]
        couts = refs[7 + len(p_in):7 + len(p_in) + len(p_out)]
        st, fx = refs[7 + len(p_in) + len(p_out):9 + len(p_in) + len(p_out)]
        sems = refs[9 + len(p_in) + len(p_out):]
        n = pl.program_id(0)

        @pl.when(n == 0)
        def _():
            st[...] = jnp.zeros_like(st)
            if plan is not None:
                plan.start(cins, couts, sems)

        dtv = dt_ref[...]
        _, cs, cst, cl, e, f, ecl, tril = _ssd_common(dtv, alog_ref[...])
        for g in range(SSD_GROUPS):
            bg = b_ref[:, g * SSD_N:(g + 1) * SSD_N].astype(BF16)
            cg = c_ref[:, g * SSD_N:(g + 1) * SSD_N].astype(BF16)
            gm = _dot_nt(cg, bg)
            stg = st[g]
            so_ref[0, g] = stg
            yoff = _dot(cg, stg.astype(BF16))
            heads = [SSD_HPG * g + hh for hh in range(SSD_HPG)]
            cols = lambda v: jnp.stack([v[:, h:h + 1] for h in heads])
            x4 = jnp.stack([xs_ref[:, h * SSD_P:(h + 1) * SSD_P] for h in heads])
            csr = jnp.stack([cst[h:h + 1, :] for h in heads])
            m = gm[None] * jnp.exp(jnp.where(tril[None], cols(cs) - csr, NEG))
            xdt = x4 * cols(dtv)
            yoff4 = jnp.stack([yoff[:, hh * SSD_P:(hh + 1) * SSD_P] for hh in range(SSD_HPG)])
            y4 = (jnp.einsum("hls,hsp->hlp", m.astype(BF16), xdt.astype(BF16), preferred_element_type=F32)
                  + cols(e) * yoff4)
            fx4 = cols(f) * xdt
            for hh, h in enumerate(heads):
                y_ref[:, h * SSD_P:(h + 1) * SSD_P] = y4[hh]
                fx[:, hh * SSD_P:(hh + 1) * SSD_P] = fx4[hh]
            st[g] = stg * _head_cols(ecl, g) + _dot_tn(bg, fx[...].astype(BF16))
        if plan is not None:
            @pl.when(n == nc - 1)
            def _():
                plan.wait(cins, couts, sems)

    res = pl.pallas_call(
        body, name=name, grid=(nc,),
        in_specs=[pl.BlockSpec((BLOCK, D_SSM), lambda n: (n, 0)),
                  pl.BlockSpec((BLOCK, 1024), lambda n: (n, 2)),
                  pl.BlockSpec((BLOCK, 1024), lambda n: (n, 3)),
                  pl.BlockSpec((BLOCK, LANES), lambda n: (n, 0)),
                  pl.BlockSpec((1, LANES), lambda n: (0, 0))] + p_in,
        out_specs=[pl.BlockSpec((BLOCK, D_SSM), lambda n: (n, 0)),
                   pl.BlockSpec((1, SSD_GROUPS, SSD_N, gw), lambda n: (n, 0, 0, 0))] + p_out,
        out_shape=[jax.ShapeDtypeStruct((lp, D_SSM), F32),
                   jax.ShapeDtypeStruct((nc, SSD_GROUPS, SSD_N, gw), F32)] + p_shapes,
        scratch_shapes=[pltpu.VMEM((SSD_GROUPS, SSD_N, gw), F32), pltpu.VMEM((BLOCK, gw), F32)] + p_scr,
        compiler_params=pltpu.CompilerParams(dimension_semantics=("arbitrary",)),
    )(xbc, xbc, xbc, dt, alog, *(plan.ins if plan is not None else []))
    return res[:2], res[2:]


def _ssd_bwd(xbc, dt, alog, states, dy, name, plan=None):
    lp = xbc.shape[0]
    nc = lp // BLOCK
    gw = SSD_HPG * SSD_P
    p_in, p_shapes, p_out, p_scr = _plan_parts(plan)

    def body(*refs):
        xs_ref, b_ref, c_ref, dt_ref, alog_ref, dy_ref, st_ref = refs[:7]
        cins = refs[7:7 + len(p_in)]
        dxs_ref, db_ref, dc_ref, ddt_ref, dalog_ref = refs[7 + len(p_in):12 + len(p_in)]
        couts = refs[12 + len(p_in):12 + len(p_in) + len(p_out)]
        dst, edy, fx = refs[12 + len(p_in) + len(p_out):15 + len(p_in) + len(p_out)]
        sems = refs[15 + len(p_in) + len(p_out):]
        i = pl.program_id(0)

        @pl.when(i == 0)
        def _():
            dst[...] = jnp.zeros_like(dst)
            dalog_ref[...] = jnp.zeros_like(dalog_ref)
            if plan is not None:
                plan.start(cins, couts, sems)

        dtv = dt_ref[...]
        a, cs, cst, cl, e, f, ecl, tril = _ssd_common(dtv, alog_ref[...])
        lane = lax.broadcasted_iota(jnp.int32, (1, LANES), 1)
        sub = _row_iota(BLOCK)
        triu = (lax.broadcasted_iota(jnp.int32, (BLOCK, BLOCK), 1)
                >= lax.broadcasted_iota(jnp.int32, (BLOCK, BLOCK), 0))
        dcs = jnp.zeros((BLOCK, LANES), F32)
        dcst = jnp.zeros((LANES, BLOCK), F32)
        dcl = jnp.zeros((1, LANES), F32)
        ddtx = jnp.zeros((BLOCK, LANES), F32)
        for g in range(SSD_GROUPS):
            bg = b_ref[:, g * SSD_N:(g + 1) * SSD_N].astype(BF16)
            cg = c_ref[:, g * SSD_N:(g + 1) * SSD_N].astype(BF16)
            gm = _dot_nt(cg, bg)
            stg = st_ref[0, g]
            stb = stg.astype(BF16)
            dso = dst[g]
            dsob = dso.astype(BF16)
            yraw = _dot(cg, stb)
            dfx = _dot(bg, dsob)
            prodsum = jnp.sum(dso * stg, axis=0, keepdims=True)
            heads = [SSD_HPG * g + hh for hh in range(SSD_HPG)]
            cols = lambda v: jnp.stack([v[:, h:h + 1] for h in heads])
            parts = lambda v: jnp.stack([v[:, hh * SSD_P:(hh + 1) * SSD_P] for hh in range(SSD_HPG)])
            x4 = jnp.stack([xs_ref[:, h * SSD_P:(h + 1) * SSD_P] for h in heads])
            dy4 = jnp.stack([dy_ref[:, h * SSD_P:(h + 1) * SSD_P] for h in heads])
            csc, dtc, ec, fc = cols(cs), cols(dtv), cols(e), cols(f)
            csr = jnp.stack([cst[h:h + 1, :] for h in heads])
            seg = csc - csr
            lam = jnp.exp(jnp.where(tril[None], seg, NEG))
            lamt = jnp.exp(jnp.where(triu[None], -seg, NEG))
            m = gm[None] * lam
            mt = _dot_nt(bg, cg)[None] * lamt
            xdt = x4 * dtc
            dyb = dy4.astype(BF16)
            dm = jnp.einsum("hlp,hsp->hls", dyb, xdt.astype(BF16), preferred_element_type=F32)
            dfx4 = parts(dfx)
            dxdt = jnp.einsum("hsl,hlp->hsp", mt.astype(BF16), dyb, preferred_element_type=F32) + fc * dfx4
            w = dm * m
            dgm = jnp.sum(dm * lam, axis=0)
            dff = jnp.sum(dfx4 * xdt, axis=2, keepdims=True) * fc
            colv = (jnp.sum(w, axis=2, keepdims=True)
                    + jnp.sum(dy4 * parts(yraw), axis=2, keepdims=True) * ec - dff)
            roww = jnp.sum(w, axis=1, keepdims=True)
            ddtc = jnp.sum(dxdt * x4, axis=2, keepdims=True)
            dffs = jnp.sum(dff, axis=1, keepdims=True)
            dxs4 = dxdt * dtc
            edy4 = ec * dy4
            fx4 = fc * xdt
            for hh, h in enumerate(heads):
                ls = slice(hh * SSD_P, (hh + 1) * SSD_P)
                onl = (lane == h).astype(F32)
                dcs = dcs + colv[hh] * onl
                dcst = dcst - (sub == h).astype(F32) * roww[hh]
                dcl = dcl + (dffs[hh] + ecl[:, h:h + 1] * jnp.sum(prodsum[:, ls], axis=1, keepdims=True)) * onl
                ddtx = ddtx + ddtc[hh] * onl
                dxs_ref[:, h * SSD_P:(h + 1) * SSD_P] = dxs4[hh].astype(dxs_ref.dtype)
                edy[:, ls] = edy4[hh]
                fx[:, ls] = fx4[hh]
            edyb = edy[...].astype(BF16)
            fxb = fx[...].astype(BF16)
            dgb = dgm.astype(BF16)
            dc_ref[:, g * SSD_N:(g + 1) * SSD_N] = (_dot_nt(edyb, stb) + _dot(dgb, bg)).astype(dc_ref.dtype)
            db_ref[:, g * SSD_N:(g + 1) * SSD_N] = (_dot_nt(fxb, dsob) + _dot_tn(dgb, cg)).astype(db_ref.dtype)
            dst[g] = dso * _head_cols(ecl, g) + _dot_tn(cg, edyb)
        dcs = dcs + dcst.T + jnp.where(sub == BLOCK - 1, dcl, 0.0)
        dda = _rev_cumsum_rows(dcs, BLOCK)
        ddt_ref[...] = ddtx + dda * a
        dalog_ref[...] += jnp.sum(dda * dtv, axis=0, keepdims=True) * a
        if plan is not None:
            @pl.when(i == nc - 1)
            def _():
                plan.wait(cins, couts, sems)

    rev = lambda i: nc - 1 - i
    res = pl.pallas_call(
        body, name=name, grid=(nc,),
        in_specs=[pl.BlockSpec((BLOCK, D_SSM), lambda i: (rev(i), 0)),
                  pl.BlockSpec((BLOCK, 1024), lambda i: (rev(i), 2)),
                  pl.BlockSpec((BLOCK, 1024), lambda i: (rev(i), 3)),
                  pl.BlockSpec((BLOCK, LANES), lambda i: (rev(i), 0)),
                  pl.BlockSpec((1, LANES), lambda i: (0, 0)),
                  pl.BlockSpec((BLOCK, D_SSM), lambda i: (rev(i), 0)),
                  pl.BlockSpec((1, SSD_GROUPS, SSD_N, gw), lambda i: (rev(i), 0, 0, 0))] + p_in,
        out_specs=[pl.BlockSpec((BLOCK, D_SSM), lambda i: (rev(i), 0)),
                   pl.BlockSpec((BLOCK, 1024), lambda i: (rev(i), 0)),
                   pl.BlockSpec((BLOCK, 1024), lambda i: (rev(i), 0)),
                   pl.BlockSpec((BLOCK, LANES), lambda i: (rev(i), 0)),
                   pl.BlockSpec((1, LANES), lambda i: (0, 0))] + p_out,
        out_shape=[jax.ShapeDtypeStruct((lp, D_SSM), BF16),
                   jax.ShapeDtypeStruct((lp, 1024), BF16),
                   jax.ShapeDtypeStruct((lp, 1024), BF16),
                   jax.ShapeDtypeStruct((lp, LANES), F32),
                   jax.ShapeDtypeStruct((1, LANES), F32)] + p_shapes,
        scratch_shapes=[pltpu.VMEM((SSD_GROUPS, SSD_N, gw), F32),
                        pltpu.VMEM((BLOCK, gw), F32), pltpu.VMEM((BLOCK, gw), F32)] + p_scr,
        compiler_params=pltpu.CompilerParams(dimension_semantics=("arbitrary",)),
    )(xbc, xbc, xbc, dt, alog, dy, states, *(plan.ins if plan is not None else []))
    return res[:5], res[5:]


_GN_GROUPS = 8
_GN_W = D_SSM // _GN_GROUPS


def _ssm_gate_fwd(yssd, xbc, proj, dskip, gnorm, name):
    lp = yssd.shape[0]
    tm = _pick(lp, (320, 256, 128))

    def body(r, first, y_ref, x_ref, z_ref, d_ref, g_ref, o_ref):
        sz, _ = _silu_and_grad(z_ref[...])
        y2 = (y_ref[...] + d_ref[...] * x_ref[...]) * sz
        for k in range(_GN_GROUPS):
            sl = slice(k * _GN_W, (k + 1) * _GN_W)
            yk = y2[:, sl]
            rs = lax.rsqrt(jnp.mean(yk * yk, axis=-1, keepdims=True) + EPS)
            o_ref[:, sl] = (yk * rs * g_ref[:, sl]).astype(o_ref.dtype)

    return _rowcall(name, body, lp, tm, rows=[(yssd, D_SSM, 0), (xbc, D_SSM, 0), (proj, D_SSM, 0)],
                    vecs=[dskip, gnorm], outs=[(D_SSM, BF16)])[0]


def _ssm_gate_bwd(yssd, xbc, proj, dskip, gnorm, dyn, name):
    lp = yssd.shape[0]
    tm = BLOCK

    def body(r, first, y_ref, x_ref, z_ref, dyn_ref, d_ref, g_ref,
             dy_ref, dx_ref, dz_ref, dd_ref, dg_ref):
        z = z_ref[...]
        sz, dsz = _silu_and_grad(z)
        xs = x_ref[...]
        y1 = y_ref[...] + d_ref[...] * xs
        y2 = y1 * sz
        dyn = dyn_ref[...]
        for k in range(_GN_GROUPS):
            sl = slice(k * _GN_W, (k + 1) * _GN_W)
            dx, dgt = _rms_bwd(y2[:, sl], g_ref[:, sl], dyn[:, sl])
            dy1 = dx * sz[:, sl]
            dy_ref[:, sl] = dy1.astype(dy_ref.dtype)
            dx_ref[:, sl] = (dy1 * d_ref[:, sl]).astype(dx_ref.dtype)
            dz_ref[:, sl] = (dx * y1[:, sl] * dsz[:, sl]).astype(dz_ref.dtype)

            @pl.when(first)
            def _():
                dd_ref[:, sl] = jnp.zeros((1, _GN_W), F32)
                dg_ref[:, sl] = jnp.zeros((1, _GN_W), F32)

            dd_ref[:, sl] += jnp.sum(dy1 * xs[:, sl], axis=0, keepdims=True)
            dg_ref[:, sl] += jnp.sum(dgt, axis=0, keepdims=True)

    return _rowcall(name, body, lp, tm,
                    rows=[(yssd, D_SSM, 0), (xbc, D_SSM, 0), (proj, D_SSM, 0), (dyn, D_SSM, 0)],
                    vecs=[dskip, gnorm], outs=[(D_SSM, BF16), (D_SSM, BF16), (D_SSM, BF16)],
                    accs=[((1, D_SSM), F32), ((1, D_SSM), F32)])


def _adamw(w, g, m, v, name):
    r, c = w.shape
    tm = r if r <= 512 else _pick(r, (512, 352, 256, 128, 64, 32, 16, 8))
    c1 = 1.0 / (1.0 - ADAM_B1 ** ADAM_STEP)
    c2 = 1.0 / (1.0 - ADAM_B2 ** ADAM_STEP)

    def body(w_ref, g_ref, m_ref, v_ref, d_ref, nm_ref, nv_ref):
        gv = g_ref[...]
        nm = ADAM_B1 * m_ref[...] + (1.0 - ADAM_B1) * gv
        nv = ADAM_B2 * v_ref[...] + (1.0 - ADAM_B2) * (gv * gv)
        nm_ref[...] = nm
        nv_ref[...] = nv
        d_ref[...] = -ADAM_LR * ((nm * c1) / (jnp.sqrt(nv * c2) + ADAM_EPS) + ADAM_WD * w_ref[...])

    spec = pl.BlockSpec((tm, c), lambda i: (i, 0))
    return pl.pallas_call(
        body, name=name, grid=(r // tm,), in_specs=[spec] * 4, out_specs=[spec] * 3,
        out_shape=[jax.ShapeDtypeStruct((r, c), F32)] * 3,
        compiler_params=pltpu.CompilerParams(dimension_semantics=("parallel",)),
    )(w, g, m, v)


def _place():
    return lax.axis_index("x"), lax.axis_index("y"), lax.axis_index("c")


def _other_chips(x, y):
    return [(1 - x, y), (x, 1 - y), (1 - x, 1 - y)]


_ANY = pl.BlockSpec(memory_space=pl.ANY)


class _Plan:
    def __init__(self, ins, out_shapes, n_remote, n_local, issue):
        self.ins = list(ins)
        self.out_shapes = list(out_shapes)
        self.issue = issue
        self.scratch = [pltpu.SemaphoreType.DMA((max(n_remote, 1),)),
                        pltpu.SemaphoreType.DMA((max(n_remote, 1),)),
                        pltpu.SemaphoreType.DMA((max(n_local, 1),))]

    def start(self, ins, outs, sems):
        sends, _, locs = self.issue(ins, outs, *sems)
        for cp in locs + sends:
            cp.start()

    def wait(self, ins, outs, sems):
        sends, recvs, locs = self.issue(ins, outs, *sems)
        for make in recvs:
            make().wait_recv()
        for cp in sends:
            cp.wait_send()
        for cp in locs:
            cp.wait()


def _plan_parts(plan):
    if plan is None:
        return [], [], [], []
    return ([_ANY] * len(plan.ins), plan.out_shapes, [_ANY] * len(plan.out_shapes), plan.scratch)


def _run_plan(plan, name):
    n_in, n_out = len(plan.ins), len(plan.out_shapes)

    def body(*refs):
        ins, outs, sems = refs[:n_in], refs[n_in:n_in + n_out], refs[n_in + n_out:]
        plan.start(ins, outs, sems)
        plan.wait(ins, outs, sems)

    return pl.pallas_call(
        body, name=name, in_specs=[_ANY] * n_in, out_specs=[_ANY] * n_out,
        out_shape=plan.out_shapes, scratch_shapes=plan.scratch,
    )(*plan.ins)


def _gather_plan(shards):
    n = len(shards)

    def issue(ins, outs, send_sems, recv_sems, local_sems):
        x, y, c = _place()
        me = 2 * x + y
        sends, recvs, locs = [], [], []
        for p in range(n):
            locs.append(pltpu.make_async_copy(ins[p], outs[p].at[me], local_sems.at[p]))
            for k, (px, py) in enumerate(_other_chips(x, y)):
                sems = dict(send_sem=send_sems.at[3 * p + k], recv_sem=recv_sems.at[3 * p + k],
                            device_id=(px, py, c), device_id_type=MESH)
                sends.append(pltpu.make_async_remote_copy(src_ref=ins[p], dst_ref=outs[p].at[me], **sems))
                recvs.append(functools.partial(pltpu.make_async_remote_copy, src_ref=ins[p],
                                               dst_ref=outs[p].at[2 * px + py], **sems))
        return sends, recvs, locs

    return _Plan(shards, [jax.ShapeDtypeStruct((N_CHIPS,) + s.shape, s.dtype) for s in shards], 3 * n, n, issue)


_REL7 = [(fx, fy, fc) for fx in (0, 1) for fy in (0, 1) for fc in (0, 1)][1:]


def _scatter8_plan(gs):
    n = len(gs)

    def issue(ins, outs, send_sems, recv_sems, local_sems):
        x, y, c = _place()
        sends = []
        for p in range(n):
            hr = gs[p].shape[1] // 2
            for k, (fx, fy, fc) in enumerate(_REL7):
                tx, ty, tc = x ^ fx, y ^ fy, c ^ fc
                src = ins[p].at[2 * tx + ty, pl.ds(pl.multiple_of(tc * hr, SUBLANES), hr), :]
                sends.append(pltpu.make_async_remote_copy(
                    src_ref=src, dst_ref=outs[p].at[k],
                    send_sem=send_sems.at[7 * p + k], recv_sem=recv_sems.at[7 * p + k],
                    device_id=(tx, ty, tc), device_id_type=MESH))
        return sends, [functools.partial(lambda cp: cp, cp) for cp in sends], []

    shapes = [jax.ShapeDtypeStruct((7, g.shape[1] // 2, g.shape[2]), g.dtype) for g in gs]
    return _Plan(gs, shapes, 7 * n, 0, issue)


def _sibling_plan(ts):
    n = len(ts)

    def issue(ins, outs, send_sems, recv_sems, local_sems):
        x, y, c = _place()
        sends = [pltpu.make_async_remote_copy(
            src_ref=ins[p], dst_ref=outs[p], send_sem=send_sems.at[p], recv_sem=recv_sems.at[p],
            device_id=(x, y, 1 - c), device_id_type=MESH) for p in range(n)]
        return sends, [functools.partial(lambda cp: cp, cp) for cp in sends], []

    return _Plan(ts, [jax.ShapeDtypeStruct(t.shape, t.dtype) for t in ts], n, 0, issue)


def _add8(g, recv, chip, core, name):
    s, r, n = g.shape
    hr = r // 2
    th = hr // 2 if (hr // 2) % SUBLANES == 0 else hr
    nt = hr // th

    def body(chip_ref, core_ref, g_ref, r_ref, o_ref):
        acc = g_ref[0].astype(F32)
        for k in range(7):
            acc = acc + r_ref[k].astype(F32)
        o_ref[...] = acc

    return pl.pallas_call(
        body, name=name,
        grid_spec=pltpu.PrefetchScalarGridSpec(
            num_scalar_prefetch=2, grid=(nt,),
            in_specs=[pl.BlockSpec((1, th, n), lambda i, ch, co: (ch[0], co[0] * nt + i, 0)),
                      pl.BlockSpec((7, th, n), lambda i, ch, co: (0, i, 0))],
            out_specs=pl.BlockSpec((th, n), lambda i, ch, co: (i, 0))),
        out_shape=jax.ShapeDtypeStruct((hr, n), F32),
        compiler_params=pltpu.CompilerParams(dimension_semantics=("parallel",)),
    )(chip, core, g, recv)


def _adamw_halves(w, own, other, m, v, core, name):
    r, n = w.shape
    hr = r // 2
    th = hr // 2 if (hr // 2) % SUBLANES == 0 else hr
    tph = hr // th
    c1 = 1.0 / (1.0 - ADAM_B1 ** ADAM_STEP)
    c2 = 1.0 / (1.0 - ADAM_B2 ** ADAM_STEP)

    def body(core_ref, w_ref, a_ref, b_ref, m_ref, v_ref, g_ref, d_ref, nm_ref, nv_ref):
        half = pl.program_id(0) // tph
        gv = jnp.where(half == core_ref[0], a_ref[...], b_ref[...])
        nm = ADAM_B1 * m_ref[...] + (1.0 - ADAM_B1) * gv
        nv = ADAM_B2 * v_ref[...] + (1.0 - ADAM_B2) * (gv * gv)
        g_ref[...] = gv
        nm_ref[...] = nm
        nv_ref[...] = nv
        d_ref[...] = -ADAM_LR * ((nm * c1) / (jnp.sqrt(nv * c2) + ADAM_EPS) + ADAM_WD * w_ref[...])

    full = pl.BlockSpec((th, n), lambda i, co: (i, 0))
    part = pl.BlockSpec((th, n), lambda i, co: (i % tph, 0))
    return pl.pallas_call(
        body, name=name,
        grid_spec=pltpu.PrefetchScalarGridSpec(
            num_scalar_prefetch=1, grid=(2 * tph,),
            in_specs=[full, part, part, full, full], out_specs=[full] * 4),
        out_shape=[jax.ShapeDtypeStruct((r, n), F32)] * 4,
        compiler_params=pltpu.CompilerParams(dimension_semantics=("parallel",)),
    )(core, w, own, other, m, v)


def _allreduce_small(pack, name):
    r, l = pack.shape

    def body(p_ref, o_ref, land, send_sems, recv_sems):
        x, y, c = _place()
        me = 4 * x + 2 * y + c
        land[me] = p_ref[...]
        rel = [(fx, fy, fc) for fx in (0, 1) for fy in (0, 1) for fc in (0, 1)][1:]
        sends = []
        for k, (fx, fy, fc) in enumerate(rel):
            peer = (x ^ fx, y ^ fy, c ^ fc)
            cp = pltpu.make_async_remote_copy(
                src_ref=p_ref, dst_ref=land.at[me], send_sem=send_sems.at[k], recv_sem=recv_sems.at[k],
                device_id=peer, device_id_type=MESH)
            cp.start()
            sends.append(cp)
        for k, (fx, fy, fc) in enumerate(rel):
            src = 4 * (x ^ fx) + 2 * (y ^ fy) + (c ^ fc)
            pltpu.make_async_remote_copy(
                src_ref=p_ref, dst_ref=land.at[src], send_sem=send_sems.at[k], recv_sem=recv_sems.at[k],
                device_id=(x ^ fx, y ^ fy, c ^ fc), device_id_type=MESH).wait_recv()
        for cp in sends:
            cp.wait_send()
        acc = land[0]
        for d in range(1, N_DEV):
            acc = acc + land[d]
        o_ref[...] = acc

    vm = pl.BlockSpec(memory_space=pltpu.VMEM)
    return pl.pallas_call(
        body, name=name, in_specs=[vm], out_specs=vm,
        out_shape=jax.ShapeDtypeStruct((r, l), F32),
        scratch_shapes=[pltpu.VMEM((N_DEV, r, l), F32),
                        pltpu.SemaphoreType.DMA((N_DEV - 1,)), pltpu.SemaphoreType.DMA((N_DEV - 1,))],
    )(pack)


def _flat_rows(a, mult=SUBLANES * LANES):
    f = a.reshape(-1)
    padn = (-f.shape[0]) % mult
    if padn:
        f = jnp.concatenate([f, jnp.zeros((padn,), f.dtype)])
    return f


def _pack(arrs, mult=SUBLANES * LANES, total_mult=None):
    flat = [_flat_rows(a, mult) for a in arrs]
    sizes = [f.shape[0] for f in flat]
    if total_mult is not None:
        padn = (-sum(sizes)) % total_mult
        if padn:
            flat.append(jnp.zeros((padn,), flat[0].dtype))
    return jnp.concatenate(flat).reshape(-1, LANES), sizes


def _unpack(pack, shapes, sizes, lead=()):
    flat = pack.reshape(lead + (-1,))
    out, off = [], 0
    for shp, sz in zip(shapes, sizes):
        n = math.prod(shp)
        out.append(flat[..., off:off + n].reshape(lead + tuple(shp)))
        off += sz
    return out


def _cols_from_shards(g):
    s, k, n = g.shape
    return jnp.transpose(g, (1, 0, 2)).reshape(k, s * n)


def _cols_to_shards(w, s=N_CHIPS):
    k, n = w.shape
    return jnp.transpose(w.reshape(k, s, n // s), (1, 0, 2))


def _ffn_fwd(h, pre, post, w_up, cw, cb, w_down, tag):
    u = _rmsnorm_fwd(h, pre, f"{tag}_prenorm")
    hp = _mm_nn_sh(u, w_up, 2 * D_FF, f"{tag}_up")
    act, hg, hu = _ffn_convact_fwd(hp, cw, cb, f"{tag}_convact")
    o = _mm_nn(act, w_down, F32, f"{tag}_down")
    hn = _postnorm_res_fwd(h, o, post, f"{tag}_postnorm")
    return hn, (h, u, hp, hg, hu, act, o)


def _ffn_bwd(dh, saved, pre, post, w_up, cw, w_down, tag):
    h, u, hp, hg, hu, act, o = saved
    do, dpost = _postnorm_bwd(o, post, dh, f"{tag}_postnorm_bwd")
    dact = _mm_nt(do, w_down, f"{tag}_down_dx")
    dw_down = _mm_tn(act, do, f"{tag}_down_dw")
    dhg, dhu = _ffn_act_bwd(hg, hu, dact, f"{tag}_act_bwd")
    dxg, dwg, dbg = _conv_bwd(hp, 0, D_FF, dhg, cw, f"{tag}_conv_bwd_gate")
    dxu, dwu, dbu = _conv_bwd(hp, D_FF, D_FF, dhu, cw, f"{tag}_conv_bwd_up", w_col_off=D_FF)
    dhp = (dxg, dxu)
    dcw = jnp.concatenate([dwg, dwu], axis=1)
    dcb = jnp.concatenate([dbg, dbu], axis=1)
    du = _mm_nt_sh(dhp, w_up, f"{tag}_up_dx")
    dw_up = _mm_tn_sh(u, dhp, w_up.shape[2], f"{tag}_up_dw")
    dhn, dpre = _prenorm_bwd(h, pre, du, dh, f"{tag}_prenorm_bwd")
    return dhn, dict(pre=dpre, post=dpost, w_up=dw_up, conv_w=dcw[:3], conv_b=dcb, w_down=dw_down)


class _Exchange:
    GATHER_IN_ATTN = ("l0_w_out", "l0_ffn_w_up", "l0_ffn_w_down", "l1_w_in")
    GATHER_IN_SSD = ("l1_w_out", "l1_ffn_w_up", "l1_ffn_w_down")
    AFTER_L1_OUT = ("l1_ffn_w_up", "l1_ffn_w_down", "l1_w_out")
    AFTER_L0_OUT = ("l1_w_in", "l0_ffn_w_up", "l0_ffn_w_down", "l0_w_out")
    LAST = ("l0_w_in",)

    def __init__(self, late_shards):
        self.late = dict(late_shards)
        self.slabs = {}
        self.recv = {}

    def gather_plan(self, names):
        return _gather_plan([self.late[n] for n in names])

    def gathered(self, names, outs):
        return {n: (g if n in _BIG_COL else g.reshape(-1, g.shape[-1])) for n, g in zip(names, outs)}

    def scatter_plan(self, grads, names):
        for n in names:
            g = grads[n]
            self.slabs[n] = g if n in _BIG_COL else g.reshape(N_CHIPS, -1, g.shape[-1])
        return _scatter8_plan([self.slabs[n] for n in names])

    def scattered(self, names, outs):
        self.recv.update(zip(names, outs))


def _local_step(x, tgt, meta, P, ex=None):
    seq, d = x.shape
    lp = seq + BLOCK
    h0 = jnp.concatenate([jnp.zeros((PAD, d), F32), meta, x], axis=0)
    tgt_p = jnp.concatenate([jnp.zeros((BLOCK, d), F32), tgt], axis=0)

    u0 = _rmsnorm_fwd(h0, P["l0_mix_pre_norm"], "l0_mix_prenorm")
    proj0 = _mm_nn_sh(u0, P["l0_w_in"], EVEN_IN, "l0_in")
    xrc = _conv_fwd(proj0, D_MODEL, D_MODEL, P["l0_lru_conv_w"], P["l0_lru_conv_b"], "l0_lru_conv")
    lru_args = (P["l0_lru_w_a"], P["l0_lru_w_x"], P["l0_lru_b_a"], P["l0_lru_b_x"], P["l0_lru_lambda"])
    ya, hl = _lru_fwd(proj0, xrc, *lru_args, "l0_lru")
    yb, outs = _attn_fwd(proj0, P["l0_attn_sinks"], "l0_attn",
                         ex.gather_plan(ex.GATHER_IN_ATTN) if ex else None)
    if ex:
        P = {**P, **ex.gathered(ex.GATHER_IN_ATTN, outs)}
    o0 = _mm_nn((ya, yb), P["l0_w_out"], F32, "l0_out")
    h1 = _postnorm_res_fwd(h0, o0, P["l0_mix_post_norm"], "l0_mix_postnorm")
    h2, ffn0 = _ffn_fwd(h1, P["l0_ffn_pre_norm"], P["l0_ffn_post_norm"], P["l0_ffn_w_up"],
                        P["l0_ffn_conv_w"], P["l0_ffn_conv_b"], P["l0_ffn_w_down"], "l0_ffn")
    u2 = _rmsnorm_fwd(h2, P["l1_mix_pre_norm"], "l1_mix_prenorm")
    proj1 = _mm_nn_sh(u2, P["l1_w_in"], ODD_IN_PAD, "l1_in")
    xc1 = _conv_fwd(proj1, _ZW, _XBC_W, P["l1_ssm_conv_w"], P["l1_ssm_conv_b"], "l1_ssm_conv")
    xbc, dt = _ssm_prep_fwd(xc1, proj1, P["l1_dt_bias"], "l1_ssm_prep")
    (yssd, states), outs = _ssd_fwd(xbc, dt, P["l1_a_log"], "l1_ssd",
                                    ex.gather_plan(ex.GATHER_IN_SSD) if ex else None)
    if ex:
        P = {**P, **ex.gathered(ex.GATHER_IN_SSD, outs)}
    yn = _ssm_gate_fwd(yssd, xbc, proj1, P["l1_d_skip"], P["l1_gate_norm"], "l1_ssm_gate")
    o1 = _mm_nn(yn, P["l1_w_out"], F32, "l1_out")
    h3 = _postnorm_res_fwd(h2, o1, P["l1_mix_post_norm"], "l1_mix_postnorm")
    h4, ffn1 = _ffn_fwd(h3, P["l1_ffn_pre_norm"], P["l1_ffn_post_norm"], P["l1_ffn_w_up"],
                        P["l1_ffn_conv_w"], P["l1_ffn_conv_b"], P["l1_ffn_w_down"], "l1_ffn")
    dh4, loss_cols = _loss_fwd_bwd(h4, tgt_p, "loss")

    G = {}
    dh3, g = _ffn_bwd(dh4, ffn1, P["l1_ffn_pre_norm"], P["l1_ffn_post_norm"], P["l1_ffn_w_up"],
                      P["l1_ffn_conv_w"], P["l1_ffn_w_down"], "l1_ffn")
    for k, v in g.items():
        G["l1_ffn_" + (k + "_norm" if k in ("pre", "post") else k)] = v
    do1, G["l1_mix_post_norm"] = _postnorm_bwd(o1, P["l1_mix_post_norm"], dh3, "l1_mix_postnorm_bwd")
    dyn = _mm_nt(do1, P["l1_w_out"], "l1_out_dx")
    G["l1_w_out"] = _mm_tn(yn, do1, "l1_out_dw")
    dyssd, dxskip, dz, dd_cols, G["l1_gate_norm"] = _ssm_gate_bwd(
        yssd, xbc, proj1, P["l1_d_skip"], P["l1_gate_norm"], dyn, "l1_ssm_gate_bwd")
    G["l1_d_skip"] = dd_cols.reshape(SSD_HEADS, SSD_P).sum(axis=1)
    (dxs, dbm, dcm, ddt, dalog), outs = _ssd_bwd(
        xbc, dt, P["l1_a_log"], states, dyssd, "l1_ssd_bwd",
        ex.scatter_plan(G, ex.AFTER_L1_OUT) if ex else None)
    if ex:
        ex.scattered(ex.AFTER_L1_OUT, outs)
    G["l1_a_log"] = dalog[0, :SSD_HEADS]
    dxc, ddtr, dbias = _ssm_prep_bwd(xc1, proj1, P["l1_dt_bias"], dxs, dxskip, dbm, dcm, ddt,
                                     "l1_ssm_prep_bwd")
    G["l1_dt_bias"] = dbias[0, :SSD_HEADS]
    dxbc, dcw, dcb = _conv_bwd(proj1, _ZW, _XBC_W, dxc, P["l1_ssm_conv_w"], "l1_ssm_conv_bwd")
    G["l1_ssm_conv_w"] = dcw[:4]
    G["l1_ssm_conv_b"] = dcb
    dproj1 = jnp.concatenate([dz, dxbc, ddtr], axis=1)
    du2 = _mm_nt_sh(dproj1, P["l1_w_in"], "l1_in_dx")
    G["l1_w_in"] = _mm_tn_sh(u2, dproj1, ODD_IN // N_CHIPS, "l1_in_dw")
    dh2, G["l1_mix_pre_norm"] = _prenorm_bwd(h2, P["l1_mix_pre_norm"], du2, dh3, "l1_mix_prenorm_bwd")
    dh1, g = _ffn_bwd(dh2, ffn0, P["l0_ffn_pre_norm"], P["l0_ffn_post_norm"], P["l0_ffn_w_up"],
                      P["l0_ffn_conv_w"], P["l0_ffn_w_down"], "l0_ffn")
    for k, v in g.items():
        G["l0_ffn_" + (k + "_norm" if k in ("pre", "post") else k)] = v
    do0, G["l0_mix_post_norm"] = _postnorm_bwd(o0, P["l0_mix_post_norm"], dh1, "l0_mix_postnorm_bwd")
    dmix = _mm_nt(do0, P["l0_w_out"], "l0_out_dx")
    G["l0_w_out"] = jnp.concatenate([_mm_tn(ya, do0, "l0_out_dw_lru"), _mm_tn(yb, do0, "l0_out_dw_attn")], axis=0)
    (dgate, dxrc, G["l0_lru_w_a"], G["l0_lru_w_x"], G["l0_lru_b_a"], G["l0_lru_b_x"],
     G["l0_lru_lambda"]) = _lru_bwd(proj0, xrc, hl, dmix, *lru_args, "l0_lru_bwd")
    dxr, dcw, dcb = _conv_bwd(proj0, D_MODEL, D_MODEL, dxrc, P["l0_lru_conv_w"], "l0_lru_conv_bwd")
    G["l0_lru_conv_w"] = dcw[:4]
    G["l0_lru_conv_b"] = dcb
    (dq, dk, dv, G["l0_attn_sinks"]), outs = _attn_bwd(
        proj0, P["l0_attn_sinks"], dmix, "l0_attn_bwd",
        ex.scatter_plan(G, ex.AFTER_L0_OUT) if ex else None)
    if ex:
        ex.scattered(ex.AFTER_L0_OUT, outs)
    dproj0 = jnp.concatenate([dgate, dxr, dq, dk.astype(BF16), dv.astype(BF16)], axis=1)
    G["l0_w_in"] = _mm_tn_sh(u0, dproj0, EVEN_IN // N_CHIPS, "l0_in_dw")
    if ex:
        du0, outs = _mm_nt_sh(dproj0, P["l0_w_in"], "l0_in_dx", ex.scatter_plan(G, ex.LAST))
        ex.scattered(ex.LAST, outs)
    else:
        du0 = _mm_nt_sh(dproj0, P["l0_w_in"], "l0_in_dx")
    dh0, G["l0_mix_pre_norm"] = _prenorm_bwd(h0, P["l0_mix_pre_norm"], du0, dh1, "l0_mix_prenorm_bwd")
    return loss_cols, dh0[BLOCK:], dh0[PAD:BLOCK], G


_BIG_COL = ("l0_w_in", "l0_ffn_w_up", "l1_w_in", "l1_ffn_w_up")
_BIG_ROW = ("l0_w_out", "l0_ffn_w_down", "l1_w_out", "l1_ffn_w_down")
_BIG = ("l0_w_in", "l0_w_out", "l0_ffn_w_up", "l0_ffn_w_down",
        "l1_w_in", "l1_w_out", "l1_ffn_w_up", "l1_ffn_w_down")
_SMALL_SHARDED = ("meta_tokens", "l0_lru_conv_w", "l0_ffn_conv_w", "l1_ssm_conv_w", "l1_ffn_conv_w")
_WEIGHTS = ("meta_tokens", "l0_mix_pre_norm", "l0_mix_post_norm", "l0_w_in", "l0_lru_conv_w",
            "l0_lru_conv_b", "l0_lru_w_a", "l0_lru_b_a", "l0_lru_w_x", "l0_lru_b_x", "l0_lru_lambda",
            "l0_attn_sinks", "l0_w_out", "l0_ffn_pre_norm", "l0_ffn_post_norm", "l0_ffn_w_up",
            "l0_ffn_conv_w", "l0_ffn_conv_b", "l0_ffn_w_down", "l1_mix_pre_norm", "l1_mix_post_norm",
            "l1_w_in", "l1_ssm_conv_w", "l1_ssm_conv_b", "l1_dt_bias", "l1_a_log", "l1_d_skip",
            "l1_gate_norm", "l1_w_out", "l1_ffn_pre_norm", "l1_ffn_post_norm", "l1_ffn_w_up",
            "l1_ffn_conv_w", "l1_ffn_conv_b", "l1_ffn_w_down")
_REPL = tuple(n for n in _WEIGHTS if n not in _BIG and n not in _SMALL_SHARDED)


def _pad_lanes(v, n=LANES):
    return jnp.concatenate([v, jnp.zeros((n - v.shape[0],), v.dtype)]).reshape(1, n)


def _step(x, tgt, W, M, V):
    cx, cy, cc = _place()
    chip = 2 * cx + cy

    small_pack, small_sizes = _pack([W[n] for n in _SMALL_SHARDED])
    first = _run_plan(_gather_plan([W["l0_w_in"].astype(BF16), small_pack]), "gather_first")
    small_full = _unpack(first[1], [W[n].shape for n in _SMALL_SHARDED], small_sizes, lead=(N_CHIPS,))
    ex = _Exchange({n: W[n].astype(BF16) for n in _BIG if n != "l0_w_in"})

    P = {"l0_w_in": first[0]}
    for n, g in zip(_SMALL_SHARDED, small_full):
        P[n] = _cols_from_shards(g)
    for n in _REPL:
        v = W[n]
        P[n] = v.reshape(1, -1) if v.ndim == 1 else v
    P["l0_lru_w_a"] = W["l0_lru_w_a"].astype(BF16)
    P["l0_lru_w_x"] = W["l0_lru_w_x"].astype(BF16)
    P["l1_dt_bias"] = _pad_lanes(W["l1_dt_bias"])
    P["l1_a_log"] = _pad_lanes(W["l1_a_log"])
    P["l1_d_skip"] = jnp.repeat(W["l1_d_skip"], SSD_P).reshape(1, D_SSM)
    meta = P.pop("meta_tokens")

    loss_cols, grad_x, grad_meta, G = _local_step(x, tgt, meta, P, ex)
    G["meta_tokens"] = grad_meta

    core_idx = cc.astype(jnp.int32).reshape(1)
    chip_idx = chip.astype(jnp.int32).reshape(1)
    own_half = [_add8(ex.slabs[n], ex.recv[n], chip_idx, core_idx, f"grad_sum_{n}") for n in _BIG]
    other_half = _run_plan(_sibling_plan(own_half), "grad_sibling_swap")
    small_names = list(_REPL) + list(_SMALL_SHARDED)
    small_list = [G[n] for n in small_names] + [loss_cols]
    spack, ssizes = _pack(small_list)
    sred = _allreduce_small(spack, "small_allreduce")
    sfull = _unpack(sred, [a.shape for a in small_list], ssizes)
    loss = 0.5 / D_MODEL * jnp.sum(sfull[-1])
    small_grads = {}
    for n, g in zip(small_names, sfull[:-1]):
        if n in _SMALL_SHARDED:
            wcols = W[n].shape[1]
            g = lax.dynamic_slice_in_dim(g, chip * wcols, wcols, axis=1)
        small_grads[n] = g.reshape(W[n].shape)

    grads, delta, new_m, new_v = {}, {}, {}, {}
    for n, own, other in zip(_BIG, own_half, other_half):
        grads[n], delta[n], new_m[n], new_v[n] = _adamw_halves(
            W[n], own, other, M[n], V[n], core_idx, f"adamw_{n}")
    s_names = [n for n in _WEIGHTS if n not in _BIG]
    tile_elems = 512 * LANES
    wp, wsz = _pack([W[n] for n in s_names], total_mult=tile_elems)
    gp, _ = _pack([small_grads[n] for n in s_names], total_mult=tile_elems)
    mp, _ = _pack([M[n] for n in s_names], total_mult=tile_elems)
    vp, _ = _pack([V[n] for n in s_names], total_mult=tile_elems)
    dp, nmp, nvp = _adamw(wp, gp, mp, vp, "adamw_small")
    shapes = [W[n].shape for n in s_names]
    for n, a, b, c_ in zip(s_names, _unpack(dp, shapes, wsz), _unpack(nmp, shapes, wsz),
                           _unpack(nvp, shapes, wsz)):
        grads[n] = small_grads[n]
        delta[n], new_m[n], new_v[n] = a, b, c_
    return loss, grad_x, grads, delta, new_m, new_v


def kernel(x, meta_tokens, l0_mix_pre_norm, l0_mix_post_norm, l0_w_in, l0_lru_conv_w, l0_lru_conv_b, l0_lru_w_a, l0_lru_b_a, l0_lru_w_x, l0_lru_b_x, l0_lru_lambda, l0_attn_sinks, l0_w_out, l0_ffn_pre_norm, l0_ffn_post_norm, l0_ffn_w_up, l0_ffn_conv_w, l0_ffn_conv_b, l0_ffn_w_down, l1_mix_pre_norm, l1_mix_post_norm, l1_w_in, l1_ssm_conv_w, l1_ssm_conv_b, l1_dt_bias, l1_a_log, l1_d_skip, l1_gate_norm, l1_w_out, l1_ffn_pre_norm, l1_ffn_post_norm, l1_ffn_w_up, l1_ffn_conv_w, l1_ffn_conv_b, l1_ffn_w_down, loss_target, m_meta_tokens, m_l0_mix_pre_norm, m_l0_mix_post_norm, m_l0_w_in, m_l0_lru_conv_w, m_l0_lru_conv_b, m_l0_lru_w_a, m_l0_lru_b_a, m_l0_lru_w_x, m_l0_lru_b_x, m_l0_lru_lambda, m_l0_attn_sinks, m_l0_w_out, m_l0_ffn_pre_norm, m_l0_ffn_post_norm, m_l0_ffn_w_up, m_l0_ffn_conv_w, m_l0_ffn_conv_b, m_l0_ffn_w_down, m_l1_mix_pre_norm, m_l1_mix_post_norm, m_l1_w_in, m_l1_ssm_conv_w, m_l1_ssm_conv_b, m_l1_dt_bias, m_l1_a_log, m_l1_d_skip, m_l1_gate_norm, m_l1_w_out, m_l1_ffn_pre_norm, m_l1_ffn_post_norm, m_l1_ffn_w_up, m_l1_ffn_conv_w, m_l1_ffn_conv_b, m_l1_ffn_w_down, v_meta_tokens, v_l0_mix_pre_norm, v_l0_mix_post_norm, v_l0_w_in, v_l0_lru_conv_w, v_l0_lru_conv_b, v_l0_lru_w_a, v_l0_lru_b_a, v_l0_lru_w_x, v_l0_lru_b_x, v_l0_lru_lambda, v_l0_attn_sinks, v_l0_w_out, v_l0_ffn_pre_norm, v_l0_ffn_post_norm, v_l0_ffn_w_up, v_l0_ffn_conv_w, v_l0_ffn_conv_b, v_l0_ffn_w_down, v_l1_mix_pre_norm, v_l1_mix_post_norm, v_l1_w_in, v_l1_ssm_conv_w, v_l1_ssm_conv_b, v_l1_dt_bias, v_l1_a_log, v_l1_d_skip, v_l1_gate_norm, v_l1_w_out, v_l1_ffn_pre_norm, v_l1_ffn_post_norm, v_l1_ffn_w_up, v_l1_ffn_conv_w, v_l1_ffn_conv_b, v_l1_ffn_w_down):
    args = locals()
    W = {n: args[n] for n in _WEIGHTS}
    M = {n: args["m_" + n] for n in _WEIGHTS}
    V = {n: args["v_" + n] for n in _WEIGHTS}
    loss, grad_x, grads, delta, new_m, new_v = _step(x[0], loss_target[0], W, M, V)
    return (loss, grad_x[None], *[grads[n] for n in _WEIGHTS], *[delta[n] for n in _WEIGHTS],
            *[new_m[n] for n in _WEIGHTS], *[new_v[n] for n in _WEIGHTS])
```

```python
import functools
import math

import jax
import jax.numpy as jnp
from jax import lax
from jax.experimental import pallas as pl
from jax.experimental.pallas import tpu as pltpu

F32 = jnp.float32
BF16 = jnp.bfloat16

D_MODEL = 1024
N_META = 16
BLOCK = 128
PAD = BLOCK - N_META
EPS = 1e-6
LRU_BLOCKS = 8
LRU_BS = 128
LRU_C = 8.0
N_Q_HEADS = 16
N_KV_HEADS = 2
HEAD_DIM = 64
Q_PER_KV = 8
WINDOW = 128
D_SSM = 2048
SSD_HEADS = 32
SSD_GROUPS = 8
SSD_HPG = 4
SSD_P = 64
SSD_N = 128
D_FF = 2816
NEG = -1e30
LANES = 128
SUBLANES = 8
_VMEM_LIMIT_WIDE = 62 * 1024 * 1024

ADAM_LR = 0.001
ADAM_B1 = 0.9
ADAM_B2 = 0.999
ADAM_EPS = 1e-08
ADAM_WD = 0.01
ADAM_STEP = 10

MESH = pl.DeviceIdType.MESH
N_CHIPS = 4
N_DEV = 8


def _pick(n, cands):
    for c in cands:
        if n % c == 0:
            return c
    raise ValueError(f"no tile for {n} in {cands}")


def _col_tile(n, limit=1792):
    best = None
    for t in range(LANES, min(n, limit) + 1, LANES):
        if n % t == 0:
            best = t
    if best is None:
        raise ValueError(f"no lane tile for {n}")
    return best


def _sigmoid(x):
    return 0.5 + 0.5 * jnp.tanh(0.5 * x)


def _log1p(e):
    u = 1.0 + e
    return jnp.where(u == 1.0, e, jnp.log(u) * (e / jnp.where(u == 1.0, 1.0, u - 1.0)))


def _softplus(x):
    return jnp.maximum(x, 0.0) + _log1p(jnp.exp(-jnp.abs(x)))


def _neg_expm1(x):
    poly = x * (1.0 + x * (0.5 + x * (1.0 / 6.0 + x * (1.0 / 24.0 + x * (1.0 / 120.0)))))
    return -jnp.where(x > -0.05, poly, jnp.exp(x) - 1.0)


_GELU_C = math.sqrt(2.0 / math.pi)


def _gelu(x):
    u = 0.5 + 0.5 * jnp.tanh(x * (_GELU_C + (_GELU_C * 0.044715) * (x * x)))
    return x * u


def _gelu_and_grad(x):
    x2 = x * x
    u = 0.5 + 0.5 * jnp.tanh(x * (_GELU_C + (_GELU_C * 0.044715) * x2))
    g = x * u
    dg = u * (1.0 + (x - g) * (2.0 * _GELU_C + (6.0 * 0.044715 * _GELU_C) * x2))
    return g, dg


def _silu_and_grad(x):
    s = _sigmoid(x)
    return x * s, s * (1.0 + x * (1.0 - s))


def _dot(a, b):
    return jnp.dot(a, b, preferred_element_type=F32)


def _dot_nt(a, b):
    return lax.dot_general(a, b, (((1,), (1,)), ((), ())), preferred_element_type=F32)


def _dot_tn(a, b):
    return lax.dot_general(a, b, (((0,), (0,)), ((), ())), preferred_element_type=F32)


def _row_iota(t):
    return lax.broadcasted_iota(jnp.int32, (t, 1), 0)


def _scan_fwd(a, u, t):
    row = _row_iota(t)
    d = 1
    while d < t:
        m = row >= d
        u_sh = jnp.where(m, pltpu.roll(u, d, 0), 0.0)
        a_sh = jnp.where(m, pltpu.roll(a, d, 0), 1.0)
        u = u + a * u_sh
        a = a * a_sh
        d *= 2
    return a, u


def _scan_rev(c, x, t):
    row = _row_iota(t)
    d = 1
    while d < t:
        m = row < t - d
        x_sh = jnp.where(m, pltpu.roll(x, t - d, 0), 0.0)
        c_sh = jnp.where(m, pltpu.roll(c, t - d, 0), 1.0)
        x = x + c * x_sh
        c = c * c_sh
        d *= 2
    return c, x


def _cumsum_rows(x, t):
    row = _row_iota(t)
    d = 1
    while d < t:
        x = x + jnp.where(row >= d, pltpu.roll(x, d, 0), 0.0)
        d *= 2
    return x


def _rev_cumsum_rows(x, t):
    row = _row_iota(t)
    d = 1
    while d < t:
        x = x + jnp.where(row < t - d, pltpu.roll(x, t - d, 0), 0.0)
        d *= 2
    return x


def _rms_bwd(x, g, dy):
    rs = lax.rsqrt(jnp.mean(x * x, axis=-1, keepdims=True) + EPS)
    gy = dy * g
    dx = rs * gy - x * (rs * rs * rs) * jnp.mean(x * gy, axis=-1, keepdims=True)
    return dx, dy * x * rs


def _mm_nn(a, w, out_dtype, name):
    parts = a if isinstance(a, (tuple, list)) else (a,)
    m = parts[0].shape[0]
    k, n = w.shape
    tm = _pick(m, (640, 512, 256, 128))
    tn = _col_tile(n)
    offs = [sum(p.shape[1] for p in parts[:i]) for i in range(len(parts))]

    def body(*refs):
        w_ref, o_ref = refs[len(parts)], refs[len(parts) + 1]
        acc = None
        for a_ref, p, off in zip(refs, parts, offs):
            t = _dot(a_ref[...].astype(BF16), w_ref[off:off + p.shape[1], :])
            acc = t if acc is None else acc + t
        o_ref[...] = acc.astype(o_ref.dtype)

    return pl.pallas_call(
        body, name=name, grid=(n // tn, m // tm),
        in_specs=[pl.BlockSpec((tm, p.shape[1]), lambda j, i: (i, 0)) for p in parts]
        + [pl.BlockSpec((k, tn), lambda j, i: (0, j))],
        out_specs=pl.BlockSpec((tm, tn), lambda j, i: (i, j)),
        out_shape=jax.ShapeDtypeStruct((m, n), out_dtype),
        compiler_params=pltpu.CompilerParams(dimension_semantics=("parallel", "parallel")),
    )(*parts, w)


def _mm_nt(dy, w, name, out_dtype=F32):
    m, n = dy.shape
    k = w.shape[0]
    wide = n > 3328
    tm = _pick(m, (320, 256, 128)) if wide else _pick(m, (640, 512, 256, 128))
    tk = _col_tile(k, 512 if wide else 1408)

    def body(dy_ref, w_ref, o_ref):
        o_ref[...] = _dot_nt(dy_ref[...].astype(BF16), w_ref[...]).astype(o_ref.dtype)

    return pl.pallas_call(
        body, name=name, grid=(k // tk, m // tm),
        in_specs=[pl.BlockSpec((tm, n), lambda j, i: (i, 0)),
                  pl.BlockSpec((tk, n), lambda j, i: (j, 0))],
        out_specs=pl.BlockSpec((tm, tk), lambda j, i: (i, j)),
        out_shape=jax.ShapeDtypeStruct((m, k), out_dtype),
        compiler_params=pltpu.CompilerParams(dimension_semantics=("parallel", "parallel")),
    )(dy, w)


def _mm_tn(a, dy, name):
    m, k = a.shape
    n = dy.shape[1]
    tm = _pick(m, (640, 512, 256, 128))
    tk = _col_tile(k, 1408)
    tn = _col_tile(n, 1664)
    nsteps = m // tm

    def body(a_ref, dy_ref, o_ref, acc):
        @pl.when(pl.program_id(2) == 0)
        def _():
            acc[...] = jnp.zeros_like(acc)

        acc[...] += _dot_tn(a_ref[...].astype(BF16), dy_ref[...].astype(BF16))

        @pl.when(pl.program_id(2) == nsteps - 1)
        def _():
            o_ref[...] = acc[...].astype(o_ref.dtype)

    return pl.pallas_call(
        body, name=name, grid=(k // tk, n // tn, nsteps),
        in_specs=[pl.BlockSpec((tm, tk), lambda kk, j, i: (i, kk)),
                  pl.BlockSpec((tm, tn), lambda kk, j, i: (i, j))],
        out_specs=pl.BlockSpec((tk, tn), lambda kk, j, i: (kk, j)),
        out_shape=jax.ShapeDtypeStruct((k, n), BF16),
        scratch_shapes=[pltpu.VMEM((tk, tn), F32)],
        compiler_params=pltpu.CompilerParams(
            dimension_semantics=("parallel", "parallel", "arbitrary")),
    )(a, dy)


def _mm_nn_sh(a, w4, n_out, name):
    m, k = a.shape
    s, _, n = w4.shape
    tm = _pick(m, (320, 256, 128))

    def body(a_ref, w_ref, o_ref):
        av = a_ref[...].astype(BF16)
        for j in range(s):
            o_ref[:, j * n:(j + 1) * n] = _dot(av, w_ref[j])
        if n_out > s * n:
            o_ref[:, s * n:] = jnp.zeros((tm, n_out - s * n), F32)

    return pl.pallas_call(
        body, name=name, grid=(m // tm,),
        in_specs=[pl.BlockSpec((tm, k), lambda i: (i, 0)),
                  pl.BlockSpec((s, k, n), lambda i: (0, 0, 0))],
        out_specs=pl.BlockSpec((tm, n_out), lambda i: (i, 0)),
        out_shape=jax.ShapeDtypeStruct((m, n_out), F32),
        compiler_params=pltpu.CompilerParams(dimension_semantics=("parallel",)),
    )(a, w4)


def _mm_nt_sh(dy, w4, name, plan=None):
    dys = dy if isinstance(dy, (tuple, list)) else (dy,)
    m = dys[0].shape[0]
    s, k, n = w4.shape
    tm = _pick(m, (640, 512, 256, 128))
    tk = _col_tile(k, 1024)
    where = _shard_columns(dys, s, n)
    p_in, p_shapes, p_out, p_scr = _plan_parts(plan)
    nd, nj, ni = len(dys), k // tk, m // tm

    def body(*refs):
        w_ref = refs[nd]
        cins = refs[nd + 1:nd + 1 + len(p_in)]
        o_ref = refs[nd + 1 + len(p_in)]
        couts = refs[nd + 2 + len(p_in):nd + 2 + len(p_in) + len(p_out)]
        sems = refs[nd + 2 + len(p_in) + len(p_out):]
        step = pl.program_id(0) * ni + pl.program_id(1)
        if plan is not None:
            @pl.when(step == 0)
            def _():
                plan.start(cins, couts, sems)

        acc = None
        for j, (p, c0) in enumerate(where):
            t = _dot_nt(refs[p][:, c0:c0 + n].astype(BF16), w_ref[j])
            acc = t if acc is None else acc + t
        o_ref[...] = acc
        if plan is not None:
            @pl.when(step == nj * ni - 1)
            def _():
                plan.wait(cins, couts, sems)

    sem = ("arbitrary", "arbitrary") if plan is not None else ("parallel", "parallel")
    res = pl.pallas_call(
        body, name=name, grid=(nj, ni),
        in_specs=[pl.BlockSpec((tm, d.shape[1]), lambda j, i: (i, 0)) for d in dys]
        + [pl.BlockSpec((s, tk, n), lambda j, i: (0, j, 0))] + p_in,
        out_specs=[pl.BlockSpec((tm, tk), lambda j, i: (i, j))] + p_out,
        out_shape=[jax.ShapeDtypeStruct((m, k), F32)] + p_shapes,
        scratch_shapes=p_scr,
        compiler_params=pltpu.CompilerParams(dimension_semantics=sem),
    )(*dys, w4, *(plan.ins if plan is not None else []))
    return res[0] if plan is None else (res[0], res[1:])


def _shard_columns(dys, s, n):
    where = []
    for p, d in enumerate(dys):
        where += [(p, c * n) for c in range(d.shape[1] // n)]
    assert len(where) >= s
    return where[:s]


def _mm_tn_sh(a, dy, n, name):
    dys = dy if isinstance(dy, (tuple, list)) else (dy,)
    m, k = a.shape
    s = N_CHIPS
    tm = _pick(m, (640, 512, 256, 128))
    tk = _col_tile(k, 512)
    nsteps = m // tm
    where = _shard_columns(dys, s, n)

    def body(*refs):
        a_ref, o_ref, acc = refs[0], refs[len(dys) + 1], refs[len(dys) + 2]

        @pl.when(pl.program_id(1) == 0)
        def _():
            acc[...] = jnp.zeros_like(acc)

        av = a_ref[...].astype(BF16)
        for j, (p, c0) in enumerate(where):
            acc[j] += _dot_tn(av, refs[1 + p][:, c0:c0 + n].astype(BF16))

        @pl.when(pl.program_id(1) == nsteps - 1)
        def _():
            o_ref[...] = acc[...].astype(o_ref.dtype)

    return pl.pallas_call(
        body, name=name, grid=(k // tk, nsteps),
        in_specs=[pl.BlockSpec((tm, tk), lambda kk, i: (i, kk))]
        + [pl.BlockSpec((tm, d.shape[1]), lambda kk, i: (i, 0)) for d in dys],
        out_specs=pl.BlockSpec((s, tk, n), lambda kk, i: (0, kk, 0)),
        out_shape=jax.ShapeDtypeStruct((s, k, n), BF16),
        scratch_shapes=[pltpu.VMEM((s, tk, n), F32)],
        compiler_params=pltpu.CompilerParams(dimension_semantics=("parallel", "arbitrary"),
                                             vmem_limit_bytes=_VMEM_LIMIT_WIDE),
    )(a, *dys)


def _rowcall(name, body, lp, tm, rows=(), prevs=(), vecs=(), outs=(), accs=(), scratch=(),
             reverse=False, seq=False):
    nt = lp // tm
    hb = tm // SUBLANES

    def ri(i):
        return nt - 1 - i if reverse else i

    in_specs, args = [], []
    for arr, w, cb in rows:
        in_specs.append(pl.BlockSpec((tm, w), lambda i, cb=cb: (ri(i), cb)))
        args.append(arr)
    for arr, w, cb in prevs:
        in_specs.append(pl.BlockSpec((SUBLANES, w), lambda i, cb=cb: (jnp.maximum(ri(i) * hb - 1, 0), cb)))
        args.append(arr)
    for arr in vecs:
        in_specs.append(pl.BlockSpec(arr.shape, lambda i, nd=arr.ndim: (0,) * nd))
        args.append(arr)
    out_shape, out_specs = [], []
    for w, dt in outs:
        out_shape.append(jax.ShapeDtypeStruct((lp, w), dt))
        out_specs.append(pl.BlockSpec((tm, w), lambda i: (ri(i), 0)))
    for shp, dt in accs:
        out_shape.append(jax.ShapeDtypeStruct(shp, dt))
        out_specs.append(pl.BlockSpec(shp, lambda i, nd=len(shp): (0,) * nd))

    def kern(*refs):
        i = pl.program_id(0)
        body(ri(i), i == 0, *refs)

    sem = ("arbitrary",) if (seq or accs) else ("parallel",)
    res = pl.pallas_call(
        kern, name=name, grid=(nt,), in_specs=in_specs, out_specs=out_specs,
        out_shape=out_shape, scratch_shapes=list(scratch),
        compiler_params=pltpu.CompilerParams(dimension_semantics=sem),
    )(*args)
    return res


def _acc_add(first, ref, val):
    @pl.when(first)
    def _():
        ref[...] = jnp.zeros_like(ref)

    ref[...] += val


def _real_rows(r, tm):
    return (r * tm + _row_iota(tm)) >= PAD


def _rmsnorm_fwd(h, g, name):
    lp, d = h.shape
    tm = _pick(lp, (640, 512, 256, 128))

    def body(r, first, h_ref, g_ref, u_ref):
        x = h_ref[...]
        rs = lax.rsqrt(jnp.mean(x * x, axis=-1, keepdims=True) + EPS)
        u_ref[...] = (x * rs * g_ref[...]).astype(u_ref.dtype)

    return _rowcall(name, body, lp, tm, rows=[(h, d, 0)], vecs=[g], outs=[(d, BF16)])[0]


def _postnorm_res_fwd(h, o, g, name):
    lp, d = h.shape
    tm = _pick(lp, (640, 512, 256, 128))

    def body(r, first, h_ref, o_ref, g_ref, out_ref):
        x = o_ref[...]
        rs = lax.rsqrt(jnp.mean(x * x, axis=-1, keepdims=True) + EPS)
        out_ref[...] = jnp.where(_real_rows(r, tm), h_ref[...] + x * rs * g_ref[...], 0.0)

    return _rowcall(name, body, lp, tm, rows=[(h, d, 0), (o, d, 0)], vecs=[g], outs=[(d, F32)])[0]


def _postnorm_bwd(o, g, dh, name):
    lp, d = o.shape
    tm = _pick(lp, (640, 512, 256, 128))

    def body(r, first, o_ref, dh_ref, g_ref, do_ref, dg_ref):
        dx, dgt = _rms_bwd(o_ref[...], g_ref[...], dh_ref[...])
        do_ref[...] = dx.astype(do_ref.dtype)
        _acc_add(first, dg_ref, jnp.sum(dgt, axis=0, keepdims=True))

    return _rowcall(name, body, lp, tm, rows=[(o, d, 0), (dh, d, 0)], vecs=[g],
                    outs=[(d, BF16)], accs=[((1, d), F32)])


def _prenorm_bwd(h, g, du, dh_res, name):
    lp, d = h.shape
    tm = _pick(lp, (640, 512, 256, 128))

    def body(r, first, h_ref, du_ref, dres_ref, g_ref, dh_ref, dg_ref):
        dx, dgt = _rms_bwd(h_ref[...], g_ref[...], du_ref[...])
        dh_ref[...] = jnp.where(_real_rows(r, tm), dres_ref[...] + dx, 0.0)
        _acc_add(first, dg_ref, jnp.sum(dgt, axis=0, keepdims=True))

    return _rowcall(name, body, lp, tm, rows=[(h, d, 0), (du, d, 0), (dh_res, d, 0)], vecs=[g],
                    outs=[(d, F32)], accs=[((1, d), F32)])


def _loss_fwd_bwd(h, tgt, name):
    lp, d = h.shape
    tm = _pick(lp, (640, 512, 256, 128))

    def body(r, first, h_ref, t_ref, dh_ref, ls_ref):
        tok = (r * tm + _row_iota(tm)) >= BLOCK
        e = jnp.where(tok, h_ref[...] - t_ref[...], 0.0)
        dh_ref[...] = e * (1.0 / d)
        _acc_add(first, ls_ref, jnp.sum(e * e, axis=0, keepdims=True))

    return _rowcall(name, body, lp, tm, rows=[(h, d, 0), (tgt, d, 0)],
                    outs=[(d, F32)], accs=[((1, d), F32)])


def _conv_tiles(lp, width):
    wc = _col_tile(width, 1408)
    tm = _pick(lp, (320, 256, 128))
    return tm, wc


def _conv_fwd(x, col_off, width, w, b, name):
    lp = x.shape[0]
    kk = w.shape[0]
    tm, wc = _conv_tiles(lp, width)
    offb = col_off // wc
    assert col_off % wc == 0
    hb = tm // SUBLANES

    def body(x_ref, xp_ref, w_ref, b_ref, y_ref):
        i = pl.program_id(1)
        xv = x_ref[...]
        halo = jnp.where(i > 0, xp_ref[...], 0.0)
        xx = jnp.concatenate([halo, xv], axis=0)
        acc = b_ref[...] + w_ref[kk - 1:kk, :] * xv
        for j in range(1, kk):
            acc = acc + w_ref[kk - 1 - j:kk - j, :] * pltpu.roll(xx, j, 0)[SUBLANES:, :]
        y_ref[...] = acc

    return pl.pallas_call(
        body, name=name, grid=(width // wc, lp // tm),
        in_specs=[pl.BlockSpec((tm, wc), lambda j, i: (i, offb + j)),
                  pl.BlockSpec((SUBLANES, wc), lambda j, i: (jnp.maximum(i * hb - 1, 0), offb + j)),
                  pl.BlockSpec((kk, wc), lambda j, i: (0, j)),
                  pl.BlockSpec((1, wc), lambda j, i: (0, j))],
        out_specs=pl.BlockSpec((tm, wc), lambda j, i: (i, j)),
        out_shape=jax.ShapeDtypeStruct((lp, width), F32),
        compiler_params=pltpu.CompilerParams(dimension_semantics=("parallel", "parallel")),
    )(x, x, w, b)


def _conv_bwd(x, col_off, width, dy, w, name, w_col_off=0):
    lp = x.shape[0]
    kk = w.shape[0]
    tm, wc = _conv_tiles(lp, width)
    offb = col_off // wc
    woffb = w_col_off // wc
    assert col_off % wc == 0 and w_col_off % wc == 0
    hrows = SUBLANES * (4 // dy.dtype.itemsize)
    ext = tm + hrows

    def body(x_ref, dy_ref, dn_ref, w_ref, dx_ref, dw_ref, db_ref):
        i = pl.program_id(1)
        last = pl.num_programs(1) - 1
        xv = x_ref[...]
        dyv = dy_ref[...].astype(F32)
        dd = jnp.concatenate([dyv, jnp.where(i < last, dn_ref[...].astype(F32), 0.0)], axis=0)
        dx = w_ref[kk - 1:kk, :] * dyv
        rows = [jnp.sum(dyv * xv, axis=0, keepdims=True)]
        for m in range(1, kk):
            ahead = pltpu.roll(dd, ext - m, 0)[:tm, :]
            dx = dx + w_ref[kk - 1 - m:kk - m, :] * ahead
            rows.append(jnp.sum(ahead * xv, axis=0, keepdims=True))
        dx_ref[...] = dx.astype(dx_ref.dtype)
        dwp = jnp.concatenate(rows[::-1] + [jnp.zeros((SUBLANES - kk, wc), F32)], axis=0)

        @pl.when(i == 0)
        def _():
            dw_ref[...] = jnp.zeros_like(dw_ref)
            db_ref[...] = jnp.zeros_like(db_ref)

        dw_ref[...] += dwp
        db_ref[...] += jnp.sum(dyv, axis=0, keepdims=True)

    return pl.pallas_call(
        body, name=name, grid=(width // wc, lp // tm),
        in_specs=[pl.BlockSpec((tm, wc), lambda j, i: (i, offb + j)),
                  pl.BlockSpec((tm, wc), lambda j, i: (i, j)),
                  pl.BlockSpec((hrows, wc), lambda j, i: (jnp.minimum((i + 1) * (tm // hrows), lp // hrows - 1), j)),
                  pl.BlockSpec((kk, wc), lambda j, i: (0, woffb + j))],
        out_specs=[pl.BlockSpec((tm, wc), lambda j, i: (i, j)),
                   pl.BlockSpec((SUBLANES, wc), lambda j, i: (0, j)),
                   pl.BlockSpec((1, wc), lambda j, i: (0, j))],
        out_shape=[jax.ShapeDtypeStruct((lp, width), BF16),
                   jax.ShapeDtypeStruct((SUBLANES, width), F32),
                   jax.ShapeDtypeStruct((1, width), F32)],
        compiler_params=pltpu.CompilerParams(dimension_semantics=("parallel", "arbitrary")),
    )(x, dy, dy, w)


_FFN_K = 3
_FFN_WC = 1408


def _conv3_ext(x_ext, w_ref, b_ref):
    return (b_ref[...] + w_ref[2:3, :] * x_ext + w_ref[1:2, :] * pltpu.roll(x_ext, 1, 0)
            + w_ref[0:1, :] * pltpu.roll(x_ext, 2, 0))


def _ffn_convact_fwd(hp, cw, cb, name):
    lp = hp.shape[0]
    tm = _pick(lp, (320, 256, 128))
    wc = _FFN_WC
    nj = D_FF // wc
    hb = tm // SUBLANES

    def body(g_ref, gp_ref, u_ref, up_ref, wg_ref, wu_ref, bg_ref, bu_ref, a_ref, hg_ref, hu_ref):
        i = pl.program_id(1)

        def conv(x_ref, p_ref, w_ref, b_ref):
            x_ext = jnp.concatenate([jnp.where(i > 0, p_ref[...], 0.0), x_ref[...]], axis=0)
            return _conv3_ext(x_ext, w_ref, b_ref)[SUBLANES:, :]

        hg = conv(g_ref, gp_ref, wg_ref, bg_ref)
        hu = conv(u_ref, up_ref, wu_ref, bu_ref)
        a_ref[...] = (_gelu(hg) * hu).astype(a_ref.dtype)
        hg_ref[...] = hg.astype(hg_ref.dtype)
        hu_ref[...] = hu.astype(hu_ref.dtype)

    tile = lambda off: pl.BlockSpec((tm, wc), lambda j, i: (i, off + j))
    prev = lambda off: pl.BlockSpec((SUBLANES, wc), lambda j, i: (jnp.maximum(i * hb - 1, 0), off + j))
    vec = lambda rows, off: pl.BlockSpec((rows, wc), lambda j, i: (0, off + j))
    return pl.pallas_call(
        body, name=name, grid=(nj, lp // tm),
        in_specs=[tile(0), prev(0), tile(nj), prev(nj), vec(_FFN_K, 0), vec(_FFN_K, nj), vec(1, 0), vec(1, nj)],
        out_specs=[tile(0)] * 3,
        out_shape=[jax.ShapeDtypeStruct((lp, D_FF), BF16)] * 3,
        compiler_params=pltpu.CompilerParams(dimension_semantics=("parallel", "parallel")),
    )(hp, hp, hp, hp, cw, cw, cb, cb)


def _ffn_act_bwd(hg, hu, dact, name):
    lp = hg.shape[0]
    tm = _pick(lp, (320, 256, 128))

    def body(r, first, g_ref, u_ref, da_ref, dg_ref, du_ref):
        gl, dgl = _gelu_and_grad(g_ref[...].astype(F32))
        da = da_ref[...].astype(F32)
        dg_ref[...] = (da * u_ref[...].astype(F32) * dgl).astype(dg_ref.dtype)
        du_ref[...] = (da * gl).astype(du_ref.dtype)

    return _rowcall(name, body, lp, tm, rows=[(hg, D_FF, 0), (hu, D_FF, 0), (dact, D_FF, 0)],
                    outs=[(D_FF, BF16), (D_FF, BF16)])


def _lru_gates(x, wa_ref, wx_ref, ba, bx, lam):
    xb = x.astype(BF16)
    za, zx = [], []
    for n in range(LRU_BLOCKS):
        xs = xb[:, n * LRU_BS:(n + 1) * LRU_BS]
        za.append(_dot(xs, wa_ref[n]))
        zx.append(_dot(xs, wx_ref[n]))
    r = _sigmoid(jnp.concatenate(za, axis=1) + ba)
    ig = _sigmoid(jnp.concatenate(zx, axis=1) + bx)
    sp = _softplus(-lam)
    log_a = -LRU_C * r * sp
    a = jnp.exp(log_a)
    om = _neg_expm1(2.0 * log_a)
    mult = jnp.sqrt(om)
    return xb, r, ig, sp, a, om, mult


def _lru_fwd(proj, xrc, wa, wx, ba, bx, lam, name):
    lp, d = xrc.shape
    tm = BLOCK

    def body(r_idx, first, gate_ref, x_ref, wa_ref, wx_ref, ba_ref, bx_ref, lam_ref,
             y_ref, h_ref, carry):
        @pl.when(first)
        def _():
            carry[...] = jnp.zeros_like(carry)

        x = x_ref[...]
        _, _, ig, _, a, _, mult = _lru_gates(x, wa_ref, wx_ref, ba_ref[...], bx_ref[...], lam_ref[...])
        u = jnp.where(_real_rows(r_idx, tm), mult * ig * x, 0.0)
        acum, hloc = _scan_fwd(a, u, tm)
        h = hloc + acum * carry[0:1, :]
        h_ref[...] = h
        carry[0:1, :] = h[tm - 1:tm, :]
        y_ref[...] = (_gelu(gate_ref[...]) * h).astype(y_ref.dtype)

    return _rowcall(name, body, lp, tm, rows=[(proj, d, 0), (xrc, d, 0)],
                    vecs=[wa, wx, ba, bx, lam], outs=[(d, BF16), (d, F32)],
                    scratch=[pltpu.VMEM((SUBLANES, d), F32)], seq=True)


def _lru_bwd(proj, xrc, hl, dmix, wa, wx, ba, bx, lam, name):
    lp, d = xrc.shape
    tm = BLOCK

    def body(r_idx, first, gate_ref, x_ref, h_ref, dy_ref, hp_ref, wa_ref, wx_ref, ba_ref, bx_ref,
             lam_ref, dgate_ref, dx_ref, dwa_ref, dwx_ref, dba_ref, dbx_ref, dlam_ref, carry):
        @pl.when(first)
        def _():
            carry[...] = jnp.zeros_like(carry)
            dwa_ref[...] = jnp.zeros_like(dwa_ref)
            dwx_ref[...] = jnp.zeros_like(dwx_ref)
            dba_ref[...] = jnp.zeros_like(dba_ref)
            dbx_ref[...] = jnp.zeros_like(dbx_ref)
            dlam_ref[...] = jnp.zeros_like(dlam_ref)

        x = x_ref[...]
        lam = lam_ref[...]
        xb, r, ig, sp, a, om, mult = _lru_gates(x, wa_ref, wx_ref, ba_ref[...], bx_ref[...], lam)
        h = h_ref[...]
        dy = dy_ref[...]
        gl, dgl = _gelu_and_grad(gate_ref[...])
        dgate_ref[...] = (dy * h * dgl).astype(dgate_ref.dtype)
        row = _row_iota(tm)
        lastrow = row == tm - 1
        xg = dy * gl + jnp.where(lastrow, carry[0:1, :], 0.0)
        c = jnp.where(lastrow, 1.0, pltpu.roll(a, tm - 1, 0))
        _, g = _scan_rev(c, xg, tm)
        carry[0:1, :] = a[0:1, :] * g[0:1, :]
        hprev_in = jnp.where(r_idx > 0, hp_ref[SUBLANES - 1:SUBLANES, :], 0.0)
        hprev = jnp.where(row == 0, hprev_in, pltpu.roll(h, 1, 0))
        du = jnp.where(_real_rows(r_idx, tm), g, 0.0)
        da = g * hprev
        dmult = du * ig * x
        dig = du * mult * x
        dxv = du * mult * ig
        e2 = 1.0 - om
        dlog_a = da * a - dmult * e2 / mult
        dr = dlog_a * (-LRU_C) * sp
        dsp = jnp.sum(dlog_a * (-LRU_C) * r, axis=0, keepdims=True)
        dlam_ref[...] += -dsp * _sigmoid(-lam)
        dza = dr * r * (1.0 - r)
        dzx = dig * ig * (1.0 - ig)
        dba_ref[...] += jnp.sum(dza, axis=0, keepdims=True)
        dbx_ref[...] += jnp.sum(dzx, axis=0, keepdims=True)
        dzab = dza.astype(BF16)
        dzxb = dzx.astype(BF16)
        parts = []
        for n in range(LRU_BLOCKS):
            sl = slice(n * LRU_BS, (n + 1) * LRU_BS)
            dwa_ref[n] += _dot_tn(xb[:, sl], dzab[:, sl])
            dwx_ref[n] += _dot_tn(xb[:, sl], dzxb[:, sl])
            parts.append(_dot_nt(dzab[:, sl], wa_ref[n]) + _dot_nt(dzxb[:, sl], wx_ref[n]))
        dx_ref[...] = dxv + jnp.concatenate(parts, axis=1)

    return _rowcall(name, body, lp, tm,
                    rows=[(proj, d, 0), (xrc, d, 0), (hl, d, 0), (dmix, d, 0)],
                    prevs=[(hl, d, 0)], vecs=[wa, wx, ba, bx, lam],
                    outs=[(d, BF16), (d, F32)],
                    accs=[((LRU_BLOCKS, LRU_BS, LRU_BS), F32), ((LRU_BLOCKS, LRU_BS, LRU_BS), F32),
                          ((1, d), F32), ((1, d), F32), ((1, d), F32)],
                    scratch=[pltpu.VMEM((SUBLANES, d), F32)], reverse=True, seq=True)


_SLOPES = [2.0 ** (-8.0 * (h + 1) / N_Q_HEADS) for h in range(N_Q_HEADS)]
_QK_SCALE = HEAD_DIM ** -0.5
_QCOL = 2 * D_MODEL // D_MODEL
_KCOL = (3 * D_MODEL) // LANES
_VCOL = _KCOL + 1


def _attn_masks(n):
    start = pl.multiple_of(jnp.maximum(n - 1, 0) * BLOCK, BLOCK)
    qi = n * BLOCK + lax.broadcasted_iota(jnp.int32, (BLOCK, 2 * BLOCK), 0)
    kj = start + lax.broadcasted_iota(jnp.int32, (BLOCK, 2 * BLOCK), 1)
    dist = qi - kj
    ok = (kj >= BLOCK) & (dist >= 0) & (dist < WINDOW)
    dm = (n * BLOCK - PAD + lax.broadcasted_iota(jnp.int32, (BLOCK, N_META), 0)
          - lax.broadcasted_iota(jnp.int32, (BLOCK, N_META), 1))
    okm = dm >= 0
    return start, ok, dist.astype(F32), okm, jnp.minimum(dm, WINDOW).astype(F32)


def _group_rows(ref, g):
    return jnp.concatenate(
        [ref[:, (g * Q_PER_KV + hh) * HEAD_DIM:(g * Q_PER_KV + hh + 1) * HEAD_DIM] for hh in range(Q_PER_KV)],
        axis=0).astype(BF16)


def _attn_probs(qg, kg, kmg, sink_ref, g, ok, distf, okm, dmf):
    slope = jnp.stack([jnp.full((1, 1), _SLOPES[g * Q_PER_KV + hh], F32) for hh in range(Q_PER_KV)])
    sink = jnp.stack([sink_ref[0:1, g * Q_PER_KV + hh:g * Q_PER_KV + hh + 1] for hh in range(Q_PER_KV)])
    s = (_dot_nt(qg, kg) * _QK_SCALE).reshape(Q_PER_KV, BLOCK, 2 * BLOCK)
    sm = (_dot_nt(qg, kmg) * _QK_SCALE).reshape(Q_PER_KV, BLOCK, N_META)
    s = jnp.where(ok[None], s - slope * distf[None], NEG)
    sm = jnp.where(okm[None], sm - slope * dmf[None], NEG)
    mx = jnp.maximum(jnp.maximum(jnp.max(s, axis=-1, keepdims=True),
                                 jnp.max(sm, axis=-1, keepdims=True)), sink)
    p = jnp.exp(s - mx)
    pm = jnp.exp(sm - mx)
    ps = jnp.exp(sink - mx)
    inv = 1.0 / (jnp.sum(p, axis=-1, keepdims=True) + jnp.sum(pm, axis=-1, keepdims=True) + ps)
    return p * inv, pm * inv, ps * inv


def _attn_fwd(proj, sinks, name, plan=None):
    lp = proj.shape[0]
    nblk = lp // BLOCK
    p_in, p_shapes, p_out, p_scr = _plan_parts(plan)

    def body(*refs):
        q_ref, k_ref, v_ref, sink_ref = refs[:4]
        cins = refs[4:4 + len(p_in)]
        o_ref = refs[4 + len(p_in)]
        couts = refs[5 + len(p_in):5 + len(p_in) + len(p_out)]
        sems = refs[5 + len(p_in) + len(p_out):]
        n = pl.program_id(0)
        if plan is not None:
            @pl.when(n == 0)
            def _():
                plan.start(cins, couts, sems)

        start, ok, distf, okm, dmf = _attn_masks(n)
        kb = k_ref[pl.ds(start, 2 * BLOCK), :].astype(BF16)
        vb = v_ref[pl.ds(start, 2 * BLOCK), :].astype(BF16)
        km = k_ref[PAD:BLOCK, :].astype(BF16)
        vm = v_ref[PAD:BLOCK, :].astype(BF16)
        for g in range(N_KV_HEADS):
            gs = slice(g * HEAD_DIM, (g + 1) * HEAD_DIM)
            pn, pmn, _ = _attn_probs(_group_rows(q_ref, g), kb[:, gs], km[:, gs], sink_ref, g,
                                     ok, distf, okm, dmf)
            o = (_dot(pn.astype(BF16).reshape(Q_PER_KV * BLOCK, 2 * BLOCK), vb[:, gs])
                 + _dot(pmn.astype(BF16).reshape(Q_PER_KV * BLOCK, N_META), vm[:, gs]))
            for hh in range(Q_PER_KV):
                h = g * Q_PER_KV + hh
                o_ref[:, h * HEAD_DIM:(h + 1) * HEAD_DIM] = o[hh * BLOCK:(hh + 1) * BLOCK, :].astype(o_ref.dtype)
        if plan is not None:
            @pl.when(n == nblk - 1)
            def _():
                plan.wait(cins, couts, sems)

    res = pl.pallas_call(
        body, name=name, grid=(nblk,),
        in_specs=[pl.BlockSpec((BLOCK, D_MODEL), lambda n: (n, _QCOL)),
                  pl.BlockSpec((lp, LANES), lambda n: (0, _KCOL)),
                  pl.BlockSpec((lp, LANES), lambda n: (0, _VCOL)),
                  pl.BlockSpec(sinks.shape, lambda n: (0, 0))] + p_in,
        out_specs=[pl.BlockSpec((BLOCK, D_MODEL), lambda n: (n, 0))] + p_out,
        out_shape=[jax.ShapeDtypeStruct((lp, D_MODEL), BF16)] + p_shapes,
        scratch_shapes=p_scr,
        compiler_params=pltpu.CompilerParams(dimension_semantics=("arbitrary",)),
    )(proj, proj, proj, sinks, *(plan.ins if plan is not None else []))
    return res[0], res[1:]


def _attn_bwd(proj, sinks, dmix, name, plan=None):
    lp = proj.shape[0]
    nblk = lp // BLOCK

    p_in, p_shapes, p_out, p_scr = _plan_parts(plan)

    def body(*refs):
        q_ref, k_ref, v_ref, sink_ref, dy_ref = refs[:5]
        cins = refs[5:5 + len(p_in)]
        dq_ref, dk_ref, dv_ref, ds_ref = refs[5 + len(p_in):9 + len(p_in)]
        couts = refs[9 + len(p_in):9 + len(p_in) + len(p_out)]
        sems = refs[9 + len(p_in) + len(p_out):]
        n = pl.program_id(0)

        @pl.when(n == 0)
        def _():
            dk_ref[...] = jnp.zeros_like(dk_ref)
            dv_ref[...] = jnp.zeros_like(dv_ref)
            ds_ref[...] = jnp.zeros_like(ds_ref)
            if plan is not None:
                plan.start(cins, couts, sems)

        start, ok, distf, okm, dmf = _attn_masks(n)
        kb = k_ref[pl.ds(start, 2 * BLOCK), :].astype(BF16)
        vb = v_ref[pl.ds(start, 2 * BLOCK), :].astype(BF16)
        km = k_ref[PAD:BLOCK, :].astype(BF16)
        vm = v_ref[PAD:BLOCK, :].astype(BF16)
        lane16 = lax.broadcasted_iota(jnp.int32, (1, N_Q_HEADS), 1)
        dsink = jnp.zeros((1, N_Q_HEADS), F32)
        rows = Q_PER_KV * BLOCK
        for g in range(N_KV_HEADS):
            gs = slice(g * HEAD_DIM, (g + 1) * HEAD_DIM)
            qg = _group_rows(q_ref, g)
            dog = _group_rows(dy_ref, g)
            pn, pmn, psn = _attn_probs(qg, kb[:, gs], km[:, gs], sink_ref, g, ok, distf, okm, dmf)
            dp = _dot_nt(dog, vb[:, gs]).reshape(Q_PER_KV, BLOCK, 2 * BLOCK)
            dpm = _dot_nt(dog, vm[:, gs]).reshape(Q_PER_KV, BLOCK, N_META)
            delta = (jnp.sum(pn * dp, axis=-1, keepdims=True)
                     + jnp.sum(pmn * dpm, axis=-1, keepdims=True))
            dsb = (pn * (dp - delta)).astype(BF16).reshape(rows, 2 * BLOCK)
            dsm = (pmn * (dpm - delta)).astype(BF16).reshape(rows, N_META)
            dsk = jnp.sum(psn * delta, axis=1, keepdims=True)
            for hh in range(Q_PER_KV):
                dsink = dsink - jnp.where(lane16 == g * Q_PER_KV + hh, dsk[hh], 0.0)
            dq = (_dot(dsb, kb[:, gs]) + _dot(dsm, km[:, gs])) * _QK_SCALE
            for hh in range(Q_PER_KV):
                h = g * Q_PER_KV + hh
                dq_ref[:, h * HEAD_DIM:(h + 1) * HEAD_DIM] = dq[hh * BLOCK:(hh + 1) * BLOCK, :].astype(dq_ref.dtype)
            pnb = pn.astype(BF16).reshape(rows, 2 * BLOCK)
            pmnb = pmn.astype(BF16).reshape(rows, N_META)
            dk_ref[pl.ds(start, 2 * BLOCK), gs] += _dot_tn(dsb, qg) * _QK_SCALE
            dv_ref[pl.ds(start, 2 * BLOCK), gs] += _dot_tn(pnb, dog)
            dk_ref[PAD:BLOCK, gs] += _dot_tn(dsm, qg) * _QK_SCALE
            dv_ref[PAD:BLOCK, gs] += _dot_tn(pmnb, dog)
        ds_ref[...] += dsink
        if plan is not None:
            @pl.when(n == nblk - 1)
            def _():
                plan.wait(cins, couts, sems)

    res = pl.pallas_call(
        body, name=name, grid=(nblk,),
        in_specs=[pl.BlockSpec((BLOCK, D_MODEL), lambda n: (n, _QCOL)),
                  pl.BlockSpec((lp, LANES), lambda n: (0, _KCOL)),
                  pl.BlockSpec((lp, LANES), lambda n: (0, _VCOL)),
                  pl.BlockSpec(sinks.shape, lambda n: (0, 0)),
                  pl.BlockSpec((BLOCK, D_MODEL), lambda n: (n, 1))] + p_in,
        out_specs=[pl.BlockSpec((BLOCK, D_MODEL), lambda n: (n, 0)),
                   pl.BlockSpec((lp, LANES), lambda n: (0, 0)),
                   pl.BlockSpec((lp, LANES), lambda n: (0, 0)),
                   pl.BlockSpec((1, N_Q_HEADS), lambda n: (0, 0))] + p_out,
        out_shape=[jax.ShapeDtypeStruct((lp, D_MODEL), BF16),
                   jax.ShapeDtypeStruct((lp, LANES), F32),
                   jax.ShapeDtypeStruct((lp, LANES), F32),
                   jax.ShapeDtypeStruct((1, N_Q_HEADS), F32)] + p_shapes,
        scratch_shapes=p_scr,
        compiler_params=pltpu.CompilerParams(dimension_semantics=("arbitrary",)),
    )(proj, proj, proj, sinks, dmix, *(plan.ins if plan is not None else []))
    return res[:4], res[4:]


_ZW = D_SSM
_XBC_W = D_SSM + 2 * SSD_GROUPS * SSD_N
_DT_COL = (_ZW + _XBC_W) // LANES
EVEN_IN = 3 * D_MODEL + 2 * LANES
ODD_IN = _ZW + _XBC_W + SSD_HEADS
ODD_IN_PAD = _ZW + _XBC_W + LANES


def _ssm_convprep_fwd(proj, cw, cb, dt_bias, name):
    lp = proj.shape[0]
    kk = cw.shape[0]
    tm, wc = _conv_tiles(lp, _XBC_W)
    offb = _ZW // wc
    nj = _XBC_W // wc
    hb = tm // SUBLANES

    def body(x_ref, xp_ref, dtr_ref, w_ref, b_ref, bias_ref, xc_ref, act_ref, dt_ref):
        i, j = pl.program_id(0), pl.program_id(1)
        real = _real_rows(i, tm)
        xv = x_ref[...]
        xx = jnp.concatenate([jnp.where(i > 0, xp_ref[...], 0.0), xv], axis=0)
        acc = b_ref[...] + w_ref[kk - 1:kk, :] * xv
        for m in range(1, kk):
            acc = acc + w_ref[kk - 1 - m:kk - m, :] * pltpu.roll(xx, m, 0)[SUBLANES:, :]
        xc_ref[...] = acc
        act, _ = _silu_and_grad(acc)
        act_ref[...] = jnp.where(real, act, 0.0)

        @pl.when(j == 0)
        def _():
            dt_ref[...] = jnp.where(real, _softplus(dtr_ref[...] + bias_ref[...]), 0.0)

    return pl.pallas_call(
        body, name=name, grid=(lp // tm, nj),
        in_specs=[pl.BlockSpec((tm, wc), lambda i, j: (i, offb + j)),
                  pl.BlockSpec((SUBLANES, wc), lambda i, j: (jnp.maximum(i * hb - 1, 0), offb + j)),
                  pl.BlockSpec((tm, LANES), lambda i, j: (i, _DT_COL)),
                  pl.BlockSpec((kk, wc), lambda i, j: (0, j)),
                  pl.BlockSpec((1, wc), lambda i, j: (0, j)),
                  pl.BlockSpec((1, LANES), lambda i, j: (0, 0))],
        out_specs=[pl.BlockSpec((tm, wc), lambda i, j: (i, j)),
                   pl.BlockSpec((tm, wc), lambda i, j: (i, j)),
                   pl.BlockSpec((tm, LANES), lambda i, j: (i, 0))],
        out_shape=[jax.ShapeDtypeStruct((lp, _XBC_W), F32), jax.ShapeDtypeStruct((lp, _XBC_W), F32),
                   jax.ShapeDtypeStruct((lp, LANES), F32)],
        compiler_params=pltpu.CompilerParams(dimension_semantics=("parallel", "arbitrary")),
    )(proj, proj, proj, cw, cb, dt_bias)


def _ssm_prep_bwd(xc, proj, dt_bias, dxs, dxskip, db, dc, ddt, name):
    lp = xc.shape[0]
    tm = BLOCK

    def body(r, first, xc_ref, dtr_ref, dxs_ref, dsk_ref, db_ref, dc_ref, ddt_ref, b_ref,
             dxc_ref, ddtr_ref, dbias_ref):
        real = _real_rows(r, tm)
        _, ds = _silu_and_grad(xc_ref[...])
        up = lambda ref: ref[...].astype(F32)
        dxc_ref[:, :D_SSM] = jnp.where(
            real, (up(dxs_ref) + up(dsk_ref)) * ds[:, :D_SSM], 0.0).astype(dxc_ref.dtype)
        dxc_ref[:, D_SSM:D_SSM + 1024] = jnp.where(
            real, up(db_ref) * ds[:, D_SSM:D_SSM + 1024], 0.0).astype(dxc_ref.dtype)
        dxc_ref[:, D_SSM + 1024:] = jnp.where(
            real, up(dc_ref) * ds[:, D_SSM + 1024:], 0.0).astype(dxc_ref.dtype)
        dd = jnp.where(real, ddt_ref[...] * _sigmoid(dtr_ref[...] + b_ref[...]), 0.0)
        ddtr_ref[...] = dd.astype(ddtr_ref.dtype)
        _acc_add(first, dbias_ref, jnp.sum(dd, axis=0, keepdims=True))

    return _rowcall(name, body, lp, tm,
                    rows=[(xc, _XBC_W, 0), (proj, LANES, _DT_COL), (dxs, D_SSM, 0), (dxskip, D_SSM, 0),
                          (db, 1024, 0), (dc, 1024, 0), (ddt, LANES, 0)],
                    vecs=[dt_bias], outs=[(_XBC_W, BF16), (LANES, BF16)], accs=[((1, LANES), F32)])


def _ssd_common(dt, alog):
    a = -jnp.exp(alog)
    cs = _cumsum_rows(dt * a, BLOCK)
    cst = cs.T
    cl = cs[BLOCK - 1:BLOCK, :]
    tril = (lax.broadcasted_iota(jnp.int32, (BLOCK, BLOCK), 0)
            >= lax.broadcasted_iota(jnp.int32, (BLOCK, BLOCK), 1))
    return a, cs, cst, cl, jnp.exp(cs), jnp.exp(cl - cs), jnp.exp(cl), tril


def _head_cols(ecl, g):
    lane = lax.broadcasted_iota(jnp.int32, (1, SSD_HPG * SSD_P), 1)
    e = [ecl[:, SSD_HPG * g + hh:SSD_HPG * g + hh + 1] for hh in range(SSD_HPG)]
    return jnp.where(lane < SSD_P, e[0], jnp.where(lane < 2 * SSD_P, e[1],
                                                   jnp.where(lane < 3 * SSD_P, e[2], e[3])))


def _ssd_fwd(xbc, dt, alog, name, plan=None):
    lp = xbc.shape[0]
    nc = lp // BLOCK
    gw = SSD_HPG * SSD_P
    p_in, p_shapes, p_out, p_scr = _plan_parts(plan)

    def body(*refs):
        xs_ref, b_ref, c_ref, dt_ref, alog_ref = refs[:5]
        cins = refs[5:5 + len(p_in)]
        y_ref, so_ref = refs[5 + len(p_in):7 + len(p_in)]
        couts = refs[7 + len(p_in):7 + len(p_in) + len(p_out)]
        st, fx = refs[7 + len(p_in) + len(p_out):9 + len(p_in) + len(p_out)]
        sems = refs[9 + len(p_in) + len(p_out):]
        n = pl.program_id(0)

        @pl.when(n == 0)
        def _():
            st[...] = jnp.zeros_like(st)
            if plan is not None:
                plan.start(cins, couts, sems)

        dtv = dt_ref[...]
        _, cs, cst, cl, e, f, ecl, tril = _ssd_common(dtv, alog_ref[...])
        for g in range(SSD_GROUPS):
            bg = b_ref[:, g * SSD_N:(g + 1) * SSD_N].astype(BF16)
            cg = c_ref[:, g * SSD_N:(g + 1) * SSD_N].astype(BF16)
            gm = _dot_nt(cg, bg)
            stg = st[g]
            so_ref[0, g] = stg
            yoff = _dot(cg, stg.astype(BF16))
            heads = [SSD_HPG * g + hh for hh in range(SSD_HPG)]
            cols = lambda v: jnp.stack([v[:, h:h + 1] for h in heads])
            x4 = jnp.stack([xs_ref[:, h * SSD_P:(h + 1) * SSD_P] for h in heads])
            csr = jnp.stack([cst[h:h + 1, :] for h in heads])
            m = gm[None] * jnp.exp(jnp.where(tril[None], cols(cs) - csr, NEG))
            xdt = x4 * cols(dtv)
            yoff4 = jnp.stack([yoff[:, hh * SSD_P:(hh + 1) * SSD_P] for hh in range(SSD_HPG)])
            y4 = (jnp.einsum("hls,hsp->hlp", m.astype(BF16), xdt.astype(BF16), preferred_element_type=F32)
                  + cols(e) * yoff4)
            fx4 = cols(f) * xdt
            for hh, h in enumerate(heads):
                y_ref[:, h * SSD_P:(h + 1) * SSD_P] = y4[hh]
                fx[:, hh * SSD_P:(hh + 1) * SSD_P] = fx4[hh]
            st[g] = stg * _head_cols(ecl, g) + _dot_tn(bg, fx[...].astype(BF16))
        if plan is not None:
            @pl.when(n == nc - 1)
            def _():
                plan.wait(cins, couts, sems)

    res = pl.pallas_call(
        body, name=name, grid=(nc,),
        in_specs=[pl.BlockSpec((BLOCK, D_SSM), lambda n: (n, 0)),
                  pl.BlockSpec((BLOCK, 1024), lambda n: (n, 2)),
                  pl.BlockSpec((BLOCK, 1024), lambda n: (n, 3)),
                  pl.BlockSpec((BLOCK, LANES), lambda n: (n, 0)),
                  pl.BlockSpec((1, LANES), lambda n: (0, 0))] + p_in,
        out_specs=[pl.BlockSpec((BLOCK, D_SSM), lambda n: (n, 0)),
                   pl.BlockSpec((1, SSD_GROUPS, SSD_N, gw), lambda n: (n, 0, 0, 0))] + p_out,
        out_shape=[jax.ShapeDtypeStruct((lp, D_SSM), F32),
                   jax.ShapeDtypeStruct((nc, SSD_GROUPS, SSD_N, gw), F32)] + p_shapes,
        scratch_shapes=[pltpu.VMEM((SSD_GROUPS, SSD_N, gw), F32), pltpu.VMEM((BLOCK, gw), F32)] + p_scr,
        compiler_params=pltpu.CompilerParams(dimension_semantics=("arbitrary",)),
    )(xbc, xbc, xbc, dt, alog, *(plan.ins if plan is not None else []))
    return res[:2], res[2:]


def _ssd_bwd(xbc, dt, alog, states, dy, name, plan=None):
    lp = xbc.shape[0]
    nc = lp // BLOCK
    gw = SSD_HPG * SSD_P
    p_in, p_shapes, p_out, p_scr = _plan_parts(plan)

    def body(*refs):
        xs_ref, b_ref, c_ref, dt_ref, alog_ref, dy_ref, st_ref = refs[:7]
        cins = refs[7:7 + len(p_in)]
        dxs_ref, db_ref, dc_ref, ddt_ref, dalog_ref = refs[7 + len(p_in):12 + len(p_in)]
        couts = refs[12 + len(p_in):12 + len(p_in) + len(p_out)]
        dst, edy, fx = refs[12 + len(p_in) + len(p_out):15 + len(p_in) + len(p_out)]
        sems = refs[15 + len(p_in) + len(p_out):]
        i = pl.program_id(0)

        @pl.when(i == 0)
        def _():
            dst[...] = jnp.zeros_like(dst)
            dalog_ref[...] = jnp.zeros_like(dalog_ref)
            if plan is not None:
                plan.start(cins, couts, sems)

        dtv = dt_ref[...]
        a, cs, cst, cl, e, f, ecl, tril = _ssd_common(dtv, alog_ref[...])
        lane = lax.broadcasted_iota(jnp.int32, (1, LANES), 1)
        sub = _row_iota(BLOCK)
        triu = (lax.broadcasted_iota(jnp.int32, (BLOCK, BLOCK), 1)
                >= lax.broadcasted_iota(jnp.int32, (BLOCK, BLOCK), 0))
        dcs = jnp.zeros((BLOCK, LANES), F32)
        dcst = jnp.zeros((LANES, BLOCK), F32)
        dcl = jnp.zeros((1, LANES), F32)
        ddtx = jnp.zeros((BLOCK, LANES), F32)
        for g in range(SSD_GROUPS):
            bg = b_ref[:, g * SSD_N:(g + 1) * SSD_N].astype(BF16)
            cg = c_ref[:, g * SSD_N:(g + 1) * SSD_N].astype(BF16)
            gm = _dot_nt(cg, bg)
            stg = st_ref[0, g]
            stb = stg.astype(BF16)
            dso = dst[g]
            dsob = dso.astype(BF16)
            yraw = _dot(cg, stb)
            dfx = _dot(bg, dsob)
            prodsum = jnp.sum(dso * stg, axis=0, keepdims=True)
            heads = [SSD_HPG * g + hh for hh in range(SSD_HPG)]
            cols = lambda v: jnp.stack([v[:, h:h + 1] for h in heads])
            parts = lambda v: jnp.stack([v[:, hh * SSD_P:(hh + 1) * SSD_P] for hh in range(SSD_HPG)])
            x4 = jnp.stack([xs_ref[:, h * SSD_P:(h + 1) * SSD_P] for h in heads])
            dy4 = jnp.stack([dy_ref[:, h * SSD_P:(h + 1) * SSD_P] for h in heads])
            csc, dtc, ec, fc = cols(cs), cols(dtv), cols(e), cols(f)
            csr = jnp.stack([cst[h:h + 1, :] for h in heads])
            seg = csc - csr
            lam = jnp.exp(jnp.where(tril[None], seg, NEG))
            lamt = jnp.exp(jnp.where(triu[None], -seg, NEG))
            mt = _dot_nt(bg, cg)[None] * lamt
            xdt = x4 * dtc
            dyb = dy4.astype(BF16)
            dm = jnp.einsum("hlp,hsp->hls", dyb, xdt.astype(BF16), preferred_element_type=F32)
            dfx4 = parts(dfx)
            dxdt = jnp.einsum("hsl,hlp->hsp", mt.astype(BF16), dyb, preferred_element_type=F32) + fc * dfx4
            dml = dm * lam
            w = dml * gm[None]
            dgm = jnp.sum(dml, axis=0)
            dff = jnp.sum(dfx4 * xdt, axis=2, keepdims=True) * fc
            colv = (jnp.sum(w, axis=2, keepdims=True)
                    + jnp.sum(dy4 * parts(yraw), axis=2, keepdims=True) * ec - dff)
            roww = jnp.sum(w, axis=1, keepdims=True)
            ddtc = jnp.sum(dxdt * x4, axis=2, keepdims=True)
            dffs = jnp.sum(dff, axis=1, keepdims=True)
            dxs4 = dxdt * dtc
            edy4 = ec * dy4
            fx4 = fc * xdt
            for hh, h in enumerate(heads):
                ls = slice(hh * SSD_P, (hh + 1) * SSD_P)
                onl = (lane == h).astype(F32)
                dcs = dcs + colv[hh] * onl
                dcst = dcst - (sub == h).astype(F32) * roww[hh]
                dcl = dcl + (dffs[hh] + ecl[:, h:h + 1] * jnp.sum(prodsum[:, ls], axis=1, keepdims=True)) * onl
                ddtx = ddtx + ddtc[hh] * onl
                dxs_ref[:, h * SSD_P:(h + 1) * SSD_P] = dxs4[hh].astype(dxs_ref.dtype)
                edy[:, ls] = edy4[hh]
                fx[:, ls] = fx4[hh]
            edyb = edy[...].astype(BF16)
            fxb = fx[...].astype(BF16)
            dgb = dgm.astype(BF16)
            dc_ref[:, g * SSD_N:(g + 1) * SSD_N] = (_dot_nt(edyb, stb) + _dot(dgb, bg)).astype(dc_ref.dtype)
            db_ref[:, g * SSD_N:(g + 1) * SSD_N] = (_dot_nt(fxb, dsob) + _dot_tn(dgb, cg)).astype(db_ref.dtype)
            dst[g] = dso * _head_cols(ecl, g) + _dot_tn(cg, edyb)
        dcs = dcs + dcst.T + jnp.where(sub == BLOCK - 1, dcl, 0.0)
        dda = _rev_cumsum_rows(dcs, BLOCK)
        ddt_ref[...] = ddtx + dda * a
        dalog_ref[...] += jnp.sum(dda * dtv, axis=0, keepdims=True) * a
        if plan is not None:
            @pl.when(i == nc - 1)
            def _():
                plan.wait(cins, couts, sems)

    rev = lambda i: nc - 1 - i
    res = pl.pallas_call(
        body, name=name, grid=(nc,),
        in_specs=[pl.BlockSpec((BLOCK, D_SSM), lambda i: (rev(i), 0)),
                  pl.BlockSpec((BLOCK, 1024), lambda i: (rev(i), 2)),
                  pl.BlockSpec((BLOCK, 1024), lambda i: (rev(i), 3)),
                  pl.BlockSpec((BLOCK, LANES), lambda i: (rev(i), 0)),
                  pl.BlockSpec((1, LANES), lambda i: (0, 0)),
                  pl.BlockSpec((BLOCK, D_SSM), lambda i: (rev(i), 0)),
                  pl.BlockSpec((1, SSD_GROUPS, SSD_N, gw), lambda i: (rev(i), 0, 0, 0))] + p_in,
        out_specs=[pl.BlockSpec((BLOCK, D_SSM), lambda i: (rev(i), 0)),
                   pl.BlockSpec((BLOCK, 1024), lambda i: (rev(i), 0)),
                   pl.BlockSpec((BLOCK, 1024), lambda i: (rev(i), 0)),
                   pl.BlockSpec((BLOCK, LANES), lambda i: (rev(i), 0)),
                   pl.BlockSpec((1, LANES), lambda i: (0, 0))] + p_out,
        out_shape=[jax.ShapeDtypeStruct((lp, D_SSM), BF16),
                   jax.ShapeDtypeStruct((lp, 1024), BF16),
                   jax.ShapeDtypeStruct((lp, 1024), BF16),
                   jax.ShapeDtypeStruct((lp, LANES), F32),
                   jax.ShapeDtypeStruct((1, LANES), F32)] + p_shapes,
        scratch_shapes=[pltpu.VMEM((SSD_GROUPS, SSD_N, gw), F32),
                        pltpu.VMEM((BLOCK, gw), F32), pltpu.VMEM((BLOCK, gw), F32)] + p_scr,
        compiler_params=pltpu.CompilerParams(dimension_semantics=("arbitrary",)),
    )(xbc, xbc, xbc, dt, alog, dy, states, *(plan.ins if plan is not None else []))
    return res[:5], res[5:]


_GN_GROUPS = 8
_GN_W = D_SSM // _GN_GROUPS


def _ssm_gate_fwd(yssd, xbc, proj, dskip, gnorm, name):
    lp = yssd.shape[0]
    tm = _pick(lp, (320, 256, 128))

    def body(r, first, y_ref, x_ref, z_ref, d_ref, g_ref, o_ref):
        sz, _ = _silu_and_grad(z_ref[...])
        y2 = (y_ref[...] + d_ref[...] * x_ref[...]) * sz
        for k in range(_GN_GROUPS):
            sl = slice(k * _GN_W, (k + 1) * _GN_W)
            yk = y2[:, sl]
            rs = lax.rsqrt(jnp.mean(yk * yk, axis=-1, keepdims=True) + EPS)
            o_ref[:, sl] = (yk * rs * g_ref[:, sl]).astype(o_ref.dtype)

    return _rowcall(name, body, lp, tm, rows=[(yssd, D_SSM, 0), (xbc, D_SSM, 0), (proj, D_SSM, 0)],
                    vecs=[dskip, gnorm], outs=[(D_SSM, BF16)])[0]


def _ssm_gate_bwd(yssd, xbc, proj, dskip, gnorm, dyn, name):
    lp = yssd.shape[0]
    tm = BLOCK

    def body(r, first, y_ref, x_ref, z_ref, dyn_ref, d_ref, g_ref,
             dy_ref, dx_ref, dz_ref, dd_ref, dg_ref):
        z = z_ref[...]
        sz, dsz = _silu_and_grad(z)
        xs = x_ref[...]
        y1 = y_ref[...] + d_ref[...] * xs
        y2 = y1 * sz
        dyn = dyn_ref[...]
        for k in range(_GN_GROUPS):
            sl = slice(k * _GN_W, (k + 1) * _GN_W)
            dx, dgt = _rms_bwd(y2[:, sl], g_ref[:, sl], dyn[:, sl])
            dy1 = dx * sz[:, sl]
            dy_ref[:, sl] = dy1.astype(dy_ref.dtype)
            dx_ref[:, sl] = (dy1 * d_ref[:, sl]).astype(dx_ref.dtype)
            dz_ref[:, sl] = (dx * y1[:, sl] * dsz[:, sl]).astype(dz_ref.dtype)

            @pl.when(first)
            def _():
                dd_ref[:, sl] = jnp.zeros((1, _GN_W), F32)
                dg_ref[:, sl] = jnp.zeros((1, _GN_W), F32)

            dd_ref[:, sl] += jnp.sum(dy1 * xs[:, sl], axis=0, keepdims=True)
            dg_ref[:, sl] += jnp.sum(dgt, axis=0, keepdims=True)

    return _rowcall(name, body, lp, tm,
                    rows=[(yssd, D_SSM, 0), (xbc, D_SSM, 0), (proj, D_SSM, 0), (dyn, D_SSM, 0)],
                    vecs=[dskip, gnorm], outs=[(D_SSM, BF16), (D_SSM, BF16), (D_SSM, BF16)],
                    accs=[((1, D_SSM), F32), ((1, D_SSM), F32)])


def _adamw(w, g, m, v, name):
    r, c = w.shape
    tm = r if r <= 512 else _pick(r, (512, 352, 256, 128, 64, 32, 16, 8))
    c1 = 1.0 / (1.0 - ADAM_B1 ** ADAM_STEP)
    c2 = 1.0 / (1.0 - ADAM_B2 ** ADAM_STEP)

    def body(w_ref, g_ref, m_ref, v_ref, d_ref, nm_ref, nv_ref):
        gv = g_ref[...]
        nm = ADAM_B1 * m_ref[...] + (1.0 - ADAM_B1) * gv
        nv = ADAM_B2 * v_ref[...] + (1.0 - ADAM_B2) * (gv * gv)
        nm_ref[...] = nm
        nv_ref[...] = nv
        d_ref[...] = -ADAM_LR * ((nm * c1) / (jnp.sqrt(nv * c2) + ADAM_EPS) + ADAM_WD * w_ref[...])

    spec = pl.BlockSpec((tm, c), lambda i: (i, 0))
    return pl.pallas_call(
        body, name=name, grid=(r // tm,), in_specs=[spec] * 4, out_specs=[spec] * 3,
        out_shape=[jax.ShapeDtypeStruct((r, c), F32)] * 3,
        compiler_params=pltpu.CompilerParams(dimension_semantics=("parallel",)),
    )(w, g, m, v)


def _place():
    return lax.axis_index("x"), lax.axis_index("y"), lax.axis_index("c")


def _other_chips(x, y):
    return [(1 - x, y), (x, 1 - y), (1 - x, 1 - y)]


_ANY = pl.BlockSpec(memory_space=pl.ANY)


class _Plan:
    def __init__(self, ins, out_shapes, n_remote, n_local, issue):
        self.ins = list(ins)
        self.out_shapes = list(out_shapes)
        self.issue = issue
        self.scratch = [pltpu.SemaphoreType.DMA((max(n_remote, 1),)),
                        pltpu.SemaphoreType.DMA((max(n_remote, 1),)),
                        pltpu.SemaphoreType.DMA((max(n_local, 1),))]

    def start(self, ins, outs, sems):
        sends, _, locs = self.issue(ins, outs, *sems)
        for cp in locs + sends:
            cp.start()

    def wait(self, ins, outs, sems):
        sends, recvs, locs = self.issue(ins, outs, *sems)
        for make in recvs:
            make().wait_recv()
        for cp in sends:
            cp.wait_send()
        for cp in locs:
            cp.wait()


def _plan_parts(plan):
    if plan is None:
        return [], [], [], []
    return ([_ANY] * len(plan.ins), plan.out_shapes, [_ANY] * len(plan.out_shapes), plan.scratch)


def _run_plan(plan, name):
    n_in, n_out = len(plan.ins), len(plan.out_shapes)

    def body(*refs):
        ins, outs, sems = refs[:n_in], refs[n_in:n_in + n_out], refs[n_in + n_out:]
        plan.start(ins, outs, sems)
        plan.wait(ins, outs, sems)

    return pl.pallas_call(
        body, name=name, in_specs=[_ANY] * n_in, out_specs=[_ANY] * n_out,
        out_shape=plan.out_shapes, scratch_shapes=plan.scratch,
    )(*plan.ins)


def _gather_plan(shards):
    n = len(shards)

    def issue(ins, outs, send_sems, recv_sems, local_sems):
        x, y, c = _place()
        me = 2 * x + y
        sends, recvs, locs = [], [], []
        for p in range(n):
            locs.append(pltpu.make_async_copy(ins[p], outs[p].at[me], local_sems.at[p]))
            for k, (px, py) in enumerate(_other_chips(x, y)):
                sems = dict(send_sem=send_sems.at[3 * p + k], recv_sem=recv_sems.at[3 * p + k],
                            device_id=(px, py, c), device_id_type=MESH)
                sends.append(pltpu.make_async_remote_copy(src_ref=ins[p], dst_ref=outs[p].at[me], **sems))
                recvs.append(functools.partial(pltpu.make_async_remote_copy, src_ref=ins[p],
                                               dst_ref=outs[p].at[2 * px + py], **sems))
        return sends, recvs, locs

    return _Plan(shards, [jax.ShapeDtypeStruct((N_CHIPS,) + s.shape, s.dtype) for s in shards], 3 * n, n, issue)


_REL7 = [(fx, fy, fc) for fx in (0, 1) for fy in (0, 1) for fc in (0, 1)][1:]


def _scatter8_plan(gs):
    n = len(gs)

    def issue(ins, outs, send_sems, recv_sems, local_sems):
        x, y, c = _place()
        sends = []
        for p in range(n):
            hr = gs[p].shape[1] // 2
            for k, (fx, fy, fc) in enumerate(_REL7):
                tx, ty, tc = x ^ fx, y ^ fy, c ^ fc
                src = ins[p].at[2 * tx + ty, pl.ds(pl.multiple_of(tc * hr, SUBLANES), hr), :]
                sends.append(pltpu.make_async_remote_copy(
                    src_ref=src, dst_ref=outs[p].at[k],
                    send_sem=send_sems.at[7 * p + k], recv_sem=recv_sems.at[7 * p + k],
                    device_id=(tx, ty, tc), device_id_type=MESH))
        return sends, [functools.partial(lambda cp: cp, cp) for cp in sends], []

    shapes = [jax.ShapeDtypeStruct((7, g.shape[1] // 2, g.shape[2]), g.dtype) for g in gs]
    return _Plan(gs, shapes, 7 * n, 0, issue)


def _sibling_plan(ts):
    n = len(ts)

    def issue(ins, outs, send_sems, recv_sems, local_sems):
        x, y, c = _place()
        sends = [pltpu.make_async_remote_copy(
            src_ref=ins[p], dst_ref=outs[p], send_sem=send_sems.at[p], recv_sem=recv_sems.at[p],
            device_id=(x, y, 1 - c), device_id_type=MESH) for p in range(n)]
        return sends, [functools.partial(lambda cp: cp, cp) for cp in sends], []

    return _Plan(ts, [jax.ShapeDtypeStruct(t.shape, t.dtype) for t in ts], n, 0, issue)


def _add8(g, recv, chip, core, name):
    s, r, n = g.shape
    hr = r // 2
    th = hr // 2 if (hr // 2) % SUBLANES == 0 else hr
    nt = hr // th

    def body(chip_ref, core_ref, g_ref, r_ref, o_ref):
        acc = g_ref[0].astype(F32)
        for k in range(7):
            acc = acc + r_ref[k].astype(F32)
        o_ref[...] = acc

    return pl.pallas_call(
        body, name=name,
        grid_spec=pltpu.PrefetchScalarGridSpec(
            num_scalar_prefetch=2, grid=(nt,),
            in_specs=[pl.BlockSpec((1, th, n), lambda i, ch, co: (ch[0], co[0] * nt + i, 0)),
                      pl.BlockSpec((7, th, n), lambda i, ch, co: (0, i, 0))],
            out_specs=pl.BlockSpec((th, n), lambda i, ch, co: (i, 0))),
        out_shape=jax.ShapeDtypeStruct((hr, n), F32),
        compiler_params=pltpu.CompilerParams(dimension_semantics=("parallel",)),
    )(chip, core, g, recv)


def _adamw_halves(w, own, other, m, v, core, name):
    r, n = w.shape
    hr = r // 2
    th = hr // 2 if (hr // 2) % SUBLANES == 0 else hr
    tph = hr // th
    c1 = 1.0 / (1.0 - ADAM_B1 ** ADAM_STEP)
    c2 = 1.0 / (1.0 - ADAM_B2 ** ADAM_STEP)

    def body(core_ref, w_ref, a_ref, b_ref, m_ref, v_ref, g_ref, d_ref, nm_ref, nv_ref):
        half = pl.program_id(0) // tph
        gv = jnp.where(half == core_ref[0], a_ref[...], b_ref[...])
        nm = ADAM_B1 * m_ref[...] + (1.0 - ADAM_B1) * gv
        nv = ADAM_B2 * v_ref[...] + (1.0 - ADAM_B2) * (gv * gv)
        g_ref[...] = gv
        nm_ref[...] = nm
        nv_ref[...] = nv
        d_ref[...] = -ADAM_LR * ((nm * c1) / (jnp.sqrt(nv * c2) + ADAM_EPS) + ADAM_WD * w_ref[...])

    full = pl.BlockSpec((th, n), lambda i, co: (i, 0))
    part = pl.BlockSpec((th, n), lambda i, co: (i % tph, 0))
    return pl.pallas_call(
        body, name=name,
        grid_spec=pltpu.PrefetchScalarGridSpec(
            num_scalar_prefetch=1, grid=(2 * tph,),
            in_specs=[full, part, part, full, full], out_specs=[full] * 4),
        out_shape=[jax.ShapeDtypeStruct((r, n), F32)] * 4,
        compiler_params=pltpu.CompilerParams(dimension_semantics=("parallel",)),
    )(core, w, own, other, m, v)


def _allreduce_small(pack, name):
    r, l = pack.shape
    hr = r // 2
    assert hr % SUBLANES == 0

    def body(p_ref, o_ref, sib, chips, send_sems, recv_sems):
        x, y, c = _place()
        chip = 2 * x + y
        sibling = dict(device_id=(x, y, 1 - c), device_id_type=MESH)
        mine = pl.ds(pl.multiple_of(c * hr, SUBLANES), hr)
        other = pl.ds(pl.multiple_of((1 - c) * hr, SUBLANES), hr)
        a = pltpu.make_async_remote_copy(src_ref=p_ref.at[other], dst_ref=sib, send_sem=send_sems.at[0],
                                         recv_sem=recv_sems.at[0], **sibling)
        a.start()
        a.wait()
        own, got = p_ref[mine, :], sib[...]
        chips[chip] = jnp.where(c == 0, own, got) + jnp.where(c == 0, got, own)
        sends = []
        for k, (px, py) in enumerate(_other_chips(x, y)):
            cp = pltpu.make_async_remote_copy(
                src_ref=chips.at[chip], dst_ref=chips.at[chip], send_sem=send_sems.at[1 + k],
                recv_sem=recv_sems.at[1 + k], device_id=(px, py, c), device_id_type=MESH)
            cp.start()
            sends.append(cp)
        for k, (px, py) in enumerate(_other_chips(x, y)):
            pltpu.make_async_remote_copy(
                src_ref=chips.at[chip], dst_ref=chips.at[2 * px + py], send_sem=send_sems.at[1 + k],
                recv_sem=recv_sems.at[1 + k], device_id=(px, py, c), device_id_type=MESH).wait_recv()
        for cp in sends:
            cp.wait_send()
        o_ref[mine, :] = ((chips[0] + chips[1]) + chips[2]) + chips[3]
        fin = pltpu.make_async_remote_copy(src_ref=o_ref.at[mine], dst_ref=o_ref.at[mine],
                                           send_sem=send_sems.at[4], recv_sem=recv_sems.at[4], **sibling)
        fin.start()
        pltpu.make_async_remote_copy(src_ref=o_ref.at[mine], dst_ref=o_ref.at[other],
                                     send_sem=send_sems.at[4], recv_sem=recv_sems.at[4], **sibling).wait_recv()
        fin.wait_send()

    vm = pl.BlockSpec(memory_space=pltpu.VMEM)
    return pl.pallas_call(
        body, name=name, in_specs=[vm], out_specs=vm,
        out_shape=jax.ShapeDtypeStruct((r, l), F32),
        scratch_shapes=[pltpu.VMEM((hr, l), F32), pltpu.VMEM((N_CHIPS, hr, l), F32),
                        pltpu.SemaphoreType.DMA((5,)), pltpu.SemaphoreType.DMA((5,))],
    )(pack)


def _flat_rows(a, mult=SUBLANES * LANES):
    f = a.reshape(-1)
    padn = (-f.shape[0]) % mult
    if padn:
        f = jnp.concatenate([f, jnp.zeros((padn,), f.dtype)])
    return f


def _pack(arrs, mult=SUBLANES * LANES, total_mult=None):
    flat = [_flat_rows(a, mult) for a in arrs]
    sizes = [f.shape[0] for f in flat]
    if total_mult is not None:
        padn = (-sum(sizes)) % total_mult
        if padn:
            flat.append(jnp.zeros((padn,), flat[0].dtype))
    return jnp.concatenate(flat).reshape(-1, LANES), sizes


def _unpack(pack, shapes, sizes, lead=()):
    flat = pack.reshape(lead + (-1,))
    out, off = [], 0
    for shp, sz in zip(shapes, sizes):
        n = math.prod(shp)
        out.append(flat[..., off:off + n].reshape(lead + tuple(shp)))
        off += sz
    return out


def _cols_from_shards(g):
    s, k, n = g.shape
    return jnp.transpose(g, (1, 0, 2)).reshape(k, s * n)


def _cols_to_shards(w, s=N_CHIPS):
    k, n = w.shape
    return jnp.transpose(w.reshape(k, s, n // s), (1, 0, 2))


def _ffn_fwd(h, pre, post, w_up, cw, cb, w_down, tag):
    u = _rmsnorm_fwd(h, pre, f"{tag}_prenorm")
    hp = _mm_nn_sh(u, w_up, 2 * D_FF, f"{tag}_up")
    act, hg, hu = _ffn_convact_fwd(hp, cw, cb, f"{tag}_convact")
    o = _mm_nn(act, w_down, F32, f"{tag}_down")
    hn = _postnorm_res_fwd(h, o, post, f"{tag}_postnorm")
    return hn, (h, u, hp, hg, hu, act, o)


def _ffn_bwd(dh, saved, pre, post, w_up, cw, w_down, tag):
    h, u, hp, hg, hu, act, o = saved
    do, dpost = _postnorm_bwd(o, post, dh, f"{tag}_postnorm_bwd")
    dact = _mm_nt(do, w_down, f"{tag}_down_dx", BF16)
    dw_down = _mm_tn(act, do, f"{tag}_down_dw")
    dhg, dhu = _ffn_act_bwd(hg, hu, dact, f"{tag}_act_bwd")
    dxg, dwg, dbg = _conv_bwd(hp, 0, D_FF, dhg, cw, f"{tag}_conv_bwd_gate")
    dxu, dwu, dbu = _conv_bwd(hp, D_FF, D_FF, dhu, cw, f"{tag}_conv_bwd_up", w_col_off=D_FF)
    dhp = (dxg, dxu)
    dcw = jnp.concatenate([dwg, dwu], axis=1)
    dcb = jnp.concatenate([dbg, dbu], axis=1)
    du = _mm_nt_sh(dhp, w_up, f"{tag}_up_dx")
    dw_up = _mm_tn_sh(u, dhp, w_up.shape[2], f"{tag}_up_dw")
    dhn, dpre = _prenorm_bwd(h, pre, du, dh, f"{tag}_prenorm_bwd")
    return dhn, dict(pre=dpre, post=dpost, w_up=dw_up, conv_w=dcw[:3], conv_b=dcb, w_down=dw_down)


class _Exchange:
    GATHER_IN_ATTN = ("l0_w_out", "l0_ffn_w_up", "l0_ffn_w_down", "l1_w_in")
    GATHER_IN_SSD = ("l1_w_out", "l1_ffn_w_up", "l1_ffn_w_down")
    AFTER_L1_OUT = ("l1_ffn_w_up", "l1_ffn_w_down", "l1_w_out")
    AFTER_L0_OUT = ("l1_w_in", "l0_ffn_w_up", "l0_ffn_w_down", "l0_w_out")
    LAST = ("l0_w_in",)

    def __init__(self, late_shards):
        self.late = dict(late_shards)
        self.slabs = {}
        self.recv = {}

    def gather_plan(self, names):
        return _gather_plan([self.late[n] for n in names])

    def gathered(self, names, outs):
        return {n: (g if n in _BIG_COL else g.reshape(-1, g.shape[-1])) for n, g in zip(names, outs)}

    def scatter_plan(self, grads, names):
        for n in names:
            g = grads[n]
            self.slabs[n] = g if n in _BIG_COL else g.reshape(N_CHIPS, -1, g.shape[-1])
        return _scatter8_plan([self.slabs[n] for n in names])

    def scattered(self, names, outs):
        self.recv.update(zip(names, outs))


def _local_step(x, tgt, meta, P, ex=None):
    seq, d = x.shape
    lp = seq + BLOCK
    h0 = jnp.concatenate([jnp.zeros((PAD, d), F32), meta, x], axis=0)
    tgt_p = jnp.concatenate([jnp.zeros((BLOCK, d), F32), tgt], axis=0)

    u0 = _rmsnorm_fwd(h0, P["l0_mix_pre_norm"], "l0_mix_prenorm")
    proj0 = _mm_nn_sh(u0, P["l0_w_in"], EVEN_IN, "l0_in")
    xrc = _conv_fwd(proj0, D_MODEL, D_MODEL, P["l0_lru_conv_w"], P["l0_lru_conv_b"], "l0_lru_conv")
    lru_args = (P["l0_lru_w_a"], P["l0_lru_w_x"], P["l0_lru_b_a"], P["l0_lru_b_x"], P["l0_lru_lambda"])
    ya, hl = _lru_fwd(proj0, xrc, *lru_args, "l0_lru")
    yb, outs = _attn_fwd(proj0, P["l0_attn_sinks"], "l0_attn",
                         ex.gather_plan(ex.GATHER_IN_ATTN) if ex else None)
    if ex:
        P = {**P, **ex.gathered(ex.GATHER_IN_ATTN, outs)}
    o0 = _mm_nn((ya, yb), P["l0_w_out"], F32, "l0_out")
    h1 = _postnorm_res_fwd(h0, o0, P["l0_mix_post_norm"], "l0_mix_postnorm")
    h2, ffn0 = _ffn_fwd(h1, P["l0_ffn_pre_norm"], P["l0_ffn_post_norm"], P["l0_ffn_w_up"],
                        P["l0_ffn_conv_w"], P["l0_ffn_conv_b"], P["l0_ffn_w_down"], "l0_ffn")
    u2 = _rmsnorm_fwd(h2, P["l1_mix_pre_norm"], "l1_mix_prenorm")
    proj1 = _mm_nn_sh(u2, P["l1_w_in"], ODD_IN_PAD, "l1_in")
    xc1, xbc, dt = _ssm_convprep_fwd(proj1, P["l1_ssm_conv_w"], P["l1_ssm_conv_b"], P["l1_dt_bias"],
                                     "l1_ssm_convprep")
    (yssd, states), outs = _ssd_fwd(xbc, dt, P["l1_a_log"], "l1_ssd",
                                    ex.gather_plan(ex.GATHER_IN_SSD) if ex else None)
    if ex:
        P = {**P, **ex.gathered(ex.GATHER_IN_SSD, outs)}
    yn = _ssm_gate_fwd(yssd, xbc, proj1, P["l1_d_skip"], P["l1_gate_norm"], "l1_ssm_gate")
    o1 = _mm_nn(yn, P["l1_w_out"], F32, "l1_out")
    h3 = _postnorm_res_fwd(h2, o1, P["l1_mix_post_norm"], "l1_mix_postnorm")
    h4, ffn1 = _ffn_fwd(h3, P["l1_ffn_pre_norm"], P["l1_ffn_post_norm"], P["l1_ffn_w_up"],
                        P["l1_ffn_conv_w"], P["l1_ffn_conv_b"], P["l1_ffn_w_down"], "l1_ffn")
    dh4, loss_cols = _loss_fwd_bwd(h4, tgt_p, "loss")

    G = {}
    dh3, g = _ffn_bwd(dh4, ffn1, P["l1_ffn_pre_norm"], P["l1_ffn_post_norm"], P["l1_ffn_w_up"],
                      P["l1_ffn_conv_w"], P["l1_ffn_w_down"], "l1_ffn")
    for k, v in g.items():
        G["l1_ffn_" + (k + "_norm" if k in ("pre", "post") else k)] = v
    do1, G["l1_mix_post_norm"] = _postnorm_bwd(o1, P["l1_mix_post_norm"], dh3, "l1_mix_postnorm_bwd")
    dyn = _mm_nt(do1, P["l1_w_out"], "l1_out_dx")
    G["l1_w_out"] = _mm_tn(yn, do1, "l1_out_dw")
    dyssd, dxskip, dz, dd_cols, G["l1_gate_norm"] = _ssm_gate_bwd(
        yssd, xbc, proj1, P["l1_d_skip"], P["l1_gate_norm"], dyn, "l1_ssm_gate_bwd")
    G["l1_d_skip"] = dd_cols.reshape(SSD_HEADS, SSD_P).sum(axis=1)
    (dxs, dbm, dcm, ddt, dalog), outs = _ssd_bwd(
        xbc, dt, P["l1_a_log"], states, dyssd, "l1_ssd_bwd",
        ex.scatter_plan(G, ex.AFTER_L1_OUT) if ex else None)
    if ex:
        ex.scattered(ex.AFTER_L1_OUT, outs)
    G["l1_a_log"] = dalog[0, :SSD_HEADS]
    dxc, ddtr, dbias = _ssm_prep_bwd(xc1, proj1, P["l1_dt_bias"], dxs, dxskip, dbm, dcm, ddt,
                                     "l1_ssm_prep_bwd")
    G["l1_dt_bias"] = dbias[0, :SSD_HEADS]
    dxbc, dcw, dcb = _conv_bwd(proj1, _ZW, _XBC_W, dxc, P["l1_ssm_conv_w"], "l1_ssm_conv_bwd")
    G["l1_ssm_conv_w"] = dcw[:4]
    G["l1_ssm_conv_b"] = dcb
    dproj1 = jnp.concatenate([dz, dxbc, ddtr], axis=1)
    du2 = _mm_nt_sh(dproj1, P["l1_w_in"], "l1_in_dx")
    G["l1_w_in"] = _mm_tn_sh(u2, dproj1, ODD_IN // N_CHIPS, "l1_in_dw")
    dh2, G["l1_mix_pre_norm"] = _prenorm_bwd(h2, P["l1_mix_pre_norm"], du2, dh3, "l1_mix_prenorm_bwd")
    dh1, g = _ffn_bwd(dh2, ffn0, P["l0_ffn_pre_norm"], P["l0_ffn_post_norm"], P["l0_ffn_w_up"],
                      P["l0_ffn_conv_w"], P["l0_ffn_w_down"], "l0_ffn")
    for k, v in g.items():
        G["l0_ffn_" + (k + "_norm" if k in ("pre", "post") else k)] = v
    do0, G["l0_mix_post_norm"] = _postnorm_bwd(o0, P["l0_mix_post_norm"], dh1, "l0_mix_postnorm_bwd")
    dmix = _mm_nt(do0, P["l0_w_out"], "l0_out_dx")
    G["l0_w_out"] = jnp.concatenate([_mm_tn(ya, do0, "l0_out_dw_lru"), _mm_tn(yb, do0, "l0_out_dw_attn")], axis=0)
    (dgate, dxrc, G["l0_lru_w_a"], G["l0_lru_w_x"], G["l0_lru_b_a"], G["l0_lru_b_x"],
     G["l0_lru_lambda"]) = _lru_bwd(proj0, xrc, hl, dmix, *lru_args, "l0_lru_bwd")
    dxr, dcw, dcb = _conv_bwd(proj0, D_MODEL, D_MODEL, dxrc, P["l0_lru_conv_w"], "l0_lru_conv_bwd")
    G["l0_lru_conv_w"] = dcw[:4]
    G["l0_lru_conv_b"] = dcb
    (dq, dk, dv, G["l0_attn_sinks"]), outs = _attn_bwd(
        proj0, P["l0_attn_sinks"], dmix, "l0_attn_bwd",
        ex.scatter_plan(G, ex.AFTER_L0_OUT) if ex else None)
    if ex:
        ex.scattered(ex.AFTER_L0_OUT, outs)
    dproj0 = jnp.concatenate([dgate, dxr, dq, dk.astype(BF16), dv.astype(BF16)], axis=1)
    G["l0_w_in"] = _mm_tn_sh(u0, dproj0, EVEN_IN // N_CHIPS, "l0_in_dw")
    if ex:
        du0, outs = _mm_nt_sh(dproj0, P["l0_w_in"], "l0_in_dx", ex.scatter_plan(G, ex.LAST))
        ex.scattered(ex.LAST, outs)
    else:
        du0 = _mm_nt_sh(dproj0, P["l0_w_in"], "l0_in_dx")
    dh0, G["l0_mix_pre_norm"] = _prenorm_bwd(h0, P["l0_mix_pre_norm"], du0, dh1, "l0_mix_prenorm_bwd")
    return loss_cols, dh0[BLOCK:], dh0[PAD:BLOCK], G


_BIG_COL = ("l0_w_in", "l0_ffn_w_up", "l1_w_in", "l1_ffn_w_up")
_BIG_ROW = ("l0_w_out", "l0_ffn_w_down", "l1_w_out", "l1_ffn_w_down")
_BIG = ("l0_w_in", "l0_w_out", "l0_ffn_w_up", "l0_ffn_w_down",
        "l1_w_in", "l1_w_out", "l1_ffn_w_up", "l1_ffn_w_down")
_SMALL_SHARDED = ("meta_tokens", "l0_lru_conv_w", "l0_ffn_conv_w", "l1_ssm_conv_w", "l1_ffn_conv_w")
_WEIGHTS = ("meta_tokens", "l0_mix_pre_norm", "l0_mix_post_norm", "l0_w_in", "l0_lru_conv_w",
            "l0_lru_conv_b", "l0_lru_w_a", "l0_lru_b_a", "l0_lru_w_x", "l0_lru_b_x", "l0_lru_lambda",
            "l0_attn_sinks", "l0_w_out", "l0_ffn_pre_norm", "l0_ffn_post_norm", "l0_ffn_w_up",
            "l0_ffn_conv_w", "l0_ffn_conv_b", "l0_ffn_w_down", "l1_mix_pre_norm", "l1_mix_post_norm",
            "l1_w_in", "l1_ssm_conv_w", "l1_ssm_conv_b", "l1_dt_bias", "l1_a_log", "l1_d_skip",
            "l1_gate_norm", "l1_w_out", "l1_ffn_pre_norm", "l1_ffn_post_norm", "l1_ffn_w_up",
            "l1_ffn_conv_w", "l1_ffn_conv_b", "l1_ffn_w_down")
_REPL = tuple(n for n in _WEIGHTS if n not in _BIG and n not in _SMALL_SHARDED)


def _pad_lanes(v, n=LANES):
    return jnp.concatenate([v, jnp.zeros((n - v.shape[0],), v.dtype)]).reshape(1, n)


def _step(x, tgt, W, M, V):
    cx, cy, cc = _place()
    chip = 2 * cx + cy

    small_pack, small_sizes = _pack([W[n] for n in _SMALL_SHARDED])
    first = _run_plan(_gather_plan([W["l0_w_in"].astype(BF16), small_pack]), "gather_first")
    small_full = _unpack(first[1], [W[n].shape for n in _SMALL_SHARDED], small_sizes, lead=(N_CHIPS,))
    ex = _Exchange({n: W[n].astype(BF16) for n in _BIG if n != "l0_w_in"})

    P = {"l0_w_in": first[0]}
    for n, g in zip(_SMALL_SHARDED, small_full):
        P[n] = _cols_from_shards(g)
    for n in _REPL:
        v = W[n]
        P[n] = v.reshape(1, -1) if v.ndim == 1 else v
    P["l0_lru_w_a"] = W["l0_lru_w_a"].astype(BF16)
    P["l0_lru_w_x"] = W["l0_lru_w_x"].astype(BF16)
    P["l1_dt_bias"] = _pad_lanes(W["l1_dt_bias"])
    P["l1_a_log"] = _pad_lanes(W["l1_a_log"])
    P["l1_d_skip"] = jnp.repeat(W["l1_d_skip"], SSD_P).reshape(1, D_SSM)
    meta = P.pop("meta_tokens")

    loss_cols, grad_x, grad_meta, G = _local_step(x, tgt, meta, P, ex)
    G["meta_tokens"] = grad_meta

    core_idx = cc.astype(jnp.int32).reshape(1)
    chip_idx = chip.astype(jnp.int32).reshape(1)
    own_half = [_add8(ex.slabs[n], ex.recv[n], chip_idx, core_idx, f"grad_sum_{n}") for n in _BIG]
    other_half = _run_plan(_sibling_plan(own_half), "grad_sibling_swap")
    small_names = list(_REPL) + list(_SMALL_SHARDED)
    small_list = [G[n] for n in small_names] + [loss_cols]
    spack, ssizes = _pack(small_list, total_mult=2 * SUBLANES * LANES)
    sred = _allreduce_small(spack, "small_allreduce")
    sfull = _unpack(sred, [a.shape for a in small_list], ssizes)
    loss = 0.5 / D_MODEL * jnp.sum(sfull[-1])
    small_grads = {}
    for n, g in zip(small_names, sfull[:-1]):
        if n in _SMALL_SHARDED:
            wcols = W[n].shape[1]
            g = lax.dynamic_slice_in_dim(g, chip * wcols, wcols, axis=1)
        small_grads[n] = g.reshape(W[n].shape)

    grads, delta, new_m, new_v = {}, {}, {}, {}
    for n, own, other in zip(_BIG, own_half, other_half):
        grads[n], delta[n], new_m[n], new_v[n] = _adamw_halves(
            W[n], own, other, M[n], V[n], core_idx, f"adamw_{n}")
    s_names = [n for n in _WEIGHTS if n not in _BIG]
    tile_elems = 512 * LANES
    wp, wsz = _pack([W[n] for n in s_names], total_mult=tile_elems)
    gp, _ = _pack([small_grads[n] for n in s_names], total_mult=tile_elems)
    mp, _ = _pack([M[n] for n in s_names], total_mult=tile_elems)
    vp, _ = _pack([V[n] for n in s_names], total_mult=tile_elems)
    dp, nmp, nvp = _adamw(wp, gp, mp, vp, "adamw_small")
    shapes = [W[n].shape for n in s_names]
    for n, a, b, c_ in zip(s_names, _unpack(dp, shapes, wsz), _unpack(nmp, shapes, wsz),
                           _unpack(nvp, shapes, wsz)):
        grads[n] = small_grads[n]
        delta[n], new_m[n], new_v[n] = a, b, c_
    return loss, grad_x, grads, delta, new_m, new_v


def kernel(x, meta_tokens, l0_mix_pre_norm, l0_mix_post_norm, l0_w_in, l0_lru_conv_w, l0_lru_conv_b, l0_lru_w_a, l0_lru_b_a, l0_lru_w_x, l0_lru_b_x, l0_lru_lambda, l0_attn_sinks, l0_w_out, l0_ffn_pre_norm, l0_ffn_post_norm, l0_ffn_w_up, l0_ffn_conv_w, l0_ffn_conv_b, l0_ffn_w_down, l1_mix_pre_norm, l1_mix_post_norm, l1_w_in, l1_ssm_conv_w, l1_ssm_conv_b, l1_dt_bias, l1_a_log, l1_d_skip, l1_gate_norm, l1_w_out, l1_ffn_pre_norm, l1_ffn_post_norm, l1_ffn_w_up, l1_ffn_conv_w, l1_ffn_conv_b, l1_ffn_w_down, loss_target, m_meta_tokens, m_l0_mix_pre_norm, m_l0_mix_post_norm, m_l0_w_in, m_l0_lru_conv_w, m_l0_lru_conv_b, m_l0_lru_w_a, m_l0_lru_b_a, m_l0_lru_w_x, m_l0_lru_b_x, m_l0_lru_lambda, m_l0_attn_sinks, m_l0_w_out, m_l0_ffn_pre_norm, m_l0_ffn_post_norm, m_l0_ffn_w_up, m_l0_ffn_conv_w, m_l0_ffn_conv_b, m_l0_ffn_w_down, m_l1_mix_pre_norm, m_l1_mix_post_norm, m_l1_w_in, m_l1_ssm_conv_w, m_l1_ssm_conv_b, m_l1_dt_bias, m_l1_a_log, m_l1_d_skip, m_l1_gate_norm, m_l1_w_out, m_l1_ffn_pre_norm, m_l1_ffn_post_norm, m_l1_ffn_w_up, m_l1_ffn_conv_w, m_l1_ffn_conv_b, m_l1_ffn_w_down, v_meta_tokens, v_l0_mix_pre_norm, v_l0_mix_post_norm, v_l0_w_in, v_l0_lru_conv_w, v_l0_lru_conv_b, v_l0_lru_w_a, v_l0_lru_b_a, v_l0_lru_w_x, v_l0_lru_b_x, v_l0_lru_lambda, v_l0_attn_sinks, v_l0_w_out, v_l0_ffn_pre_norm, v_l0_ffn_post_norm, v_l0_ffn_w_up, v_l0_ffn_conv_w, v_l0_ffn_conv_b, v_l0_ffn_w_down, v_l1_mix_pre_norm, v_l1_mix_post_norm, v_l1_w_in, v_l1_ssm_conv_w, v_l1_ssm_conv_b, v_l1_dt_bias, v_l1_a_log, v_l1_d_skip, v_l1_gate_norm, v_l1_w_out, v_l1_ffn_pre_norm, v_l1_ffn_post_norm, v_l1_ffn_w_up, v_l1_ffn_conv_w, v_l1_ffn_conv_b, v_l1_ffn_w_down):
    args = locals()
    W = {n: args[n] for n in _WEIGHTS}
    M = {n: args["m_" + n] for n in _WEIGHTS}
    V = {n: args["v_" + n] for n in _WEIGHTS}
    loss, grad_x, grads, delta, new_m, new_v = _step(x[0], loss_target[0], W, M, V)
    return (loss, grad_x[None], *[grads[n] for n in _WEIGHTS], *[delta[n] for n in _WEIGHTS],
            *[new_m[n] for n in _WEIGHTS], *[new_v[n] for n in _WEIGHTS])
```

```python
import functools
import math

import jax
import jax.numpy as jnp
from jax import lax
from jax.experimental import pallas as pl
from jax.experimental.pallas import tpu as pltpu

F32 = jnp.float32
BF16 = jnp.bfloat16

D_MODEL = 1024
N_META = 16
BLOCK = 128
PAD = BLOCK - N_META
EPS = 1e-6
LRU_BLOCKS = 8
LRU_BS = 128
LRU_C = 8.0
N_Q_HEADS = 16
N_KV_HEADS = 2
HEAD_DIM = 64
Q_PER_KV = 8
WINDOW = 128
D_SSM = 2048
SSD_HEADS = 32
SSD_GROUPS = 8
SSD_HPG = 4
SSD_P = 64
SSD_N = 128
D_FF = 2816
NEG = -1e30
LANES = 128
SUBLANES = 8
_VMEM_LIMIT_WIDE = 62 * 1024 * 1024

ADAM_LR = 0.001
ADAM_B1 = 0.9
ADAM_B2 = 0.999
ADAM_EPS = 1e-08
ADAM_WD = 0.01
ADAM_STEP = 10

MESH = pl.DeviceIdType.MESH
N_CHIPS = 4
N_DEV = 8


def _pick(n, cands):
    for c in cands:
        if n % c == 0:
            return c
    raise ValueError(f"no tile for {n} in {cands}")


def _col_tile(n, limit=1792):
    best = None
    for t in range(LANES, min(n, limit) + 1, LANES):
        if n % t == 0:
            best = t
    if best is None:
        raise ValueError(f"no lane tile for {n}")
    return best


def _sigmoid(x):
    return 0.5 + 0.5 * jnp.tanh(0.5 * x)


def _log1p(e):
    u = 1.0 + e
    return jnp.where(u == 1.0, e, jnp.log(u) * (e / jnp.where(u == 1.0, 1.0, u - 1.0)))


def _softplus(x):
    return jnp.maximum(x, 0.0) + _log1p(jnp.exp(-jnp.abs(x)))


def _neg_expm1(x):
    poly = x * (1.0 + x * (0.5 + x * (1.0 / 6.0 + x * (1.0 / 24.0 + x * (1.0 / 120.0)))))
    return -jnp.where(x > -0.05, poly, jnp.exp(x) - 1.0)


_GELU_C = math.sqrt(2.0 / math.pi)


def _gelu(x):
    u = 0.5 + 0.5 * jnp.tanh(x * (_GELU_C + (_GELU_C * 0.044715) * (x * x)))
    return x * u


def _gelu_and_grad(x):
    x2 = x * x
    u = 0.5 + 0.5 * jnp.tanh(x * (_GELU_C + (_GELU_C * 0.044715) * x2))
    g = x * u
    dg = u * (1.0 + (x - g) * (2.0 * _GELU_C + (6.0 * 0.044715 * _GELU_C) * x2))
    return g, dg


def _silu_and_grad(x):
    s = _sigmoid(x)
    return x * s, s * (1.0 + x * (1.0 - s))


def _dot(a, b):
    return jnp.dot(a, b, preferred_element_type=F32)


def _dot_nt(a, b):
    return lax.dot_general(a, b, (((1,), (1,)), ((), ())), preferred_element_type=F32)


def _dot_tn(a, b):
    return lax.dot_general(a, b, (((0,), (0,)), ((), ())), preferred_element_type=F32)


def _row_iota(t):
    return lax.broadcasted_iota(jnp.int32, (t, 1), 0)


def _scan_fwd(a, u, t):
    row = _row_iota(t)
    d = 1
    while d < t:
        m = row >= d
        u_sh = jnp.where(m, pltpu.roll(u, d, 0), 0.0)
        a_sh = jnp.where(m, pltpu.roll(a, d, 0), 1.0)
        u = u + a * u_sh
        a = a * a_sh
        d *= 2
    return a, u


def _scan_rev(c, x, t):
    row = _row_iota(t)
    d = 1
    while d < t:
        m = row < t - d
        x_sh = jnp.where(m, pltpu.roll(x, t - d, 0), 0.0)
        c_sh = jnp.where(m, pltpu.roll(c, t - d, 0), 1.0)
        x = x + c * x_sh
        c = c * c_sh
        d *= 2
    return c, x


def _cumsum_rows(x, t):
    row = _row_iota(t)
    d = 1
    while d < t:
        x = x + jnp.where(row >= d, pltpu.roll(x, d, 0), 0.0)
        d *= 2
    return x


def _rev_cumsum_rows(x, t):
    row = _row_iota(t)
    d = 1
    while d < t:
        x = x + jnp.where(row < t - d, pltpu.roll(x, t - d, 0), 0.0)
        d *= 2
    return x


def _rms_bwd(x, g, dy):
    rs = lax.rsqrt(jnp.mean(x * x, axis=-1, keepdims=True) + EPS)
    gy = dy * g
    dx = rs * gy - x * (rs * rs * rs) * jnp.mean(x * gy, axis=-1, keepdims=True)
    return dx, dy * x * rs


def _mm_nn(a, w, out_dtype, name):
    parts = a if isinstance(a, (tuple, list)) else (a,)
    m = parts[0].shape[0]
    k, n = w.shape
    tm = _pick(m, (640, 512, 256, 128))
    tn = _col_tile(n)
    offs = [sum(p.shape[1] for p in parts[:i]) for i in range(len(parts))]

    def body(*refs):
        w_ref, o_ref = refs[len(parts)], refs[len(parts) + 1]
        acc = None
        for a_ref, p, off in zip(refs, parts, offs):
            t = _dot(a_ref[...].astype(BF16), w_ref[off:off + p.shape[1], :])
            acc = t if acc is None else acc + t
        o_ref[...] = acc.astype(o_ref.dtype)

    return pl.pallas_call(
        body, name=name, grid=(n // tn, m // tm),
        in_specs=[pl.BlockSpec((tm, p.shape[1]), lambda j, i: (i, 0)) for p in parts]
        + [pl.BlockSpec((k, tn), lambda j, i: (0, j))],
        out_specs=pl.BlockSpec((tm, tn), lambda j, i: (i, j)),
        out_shape=jax.ShapeDtypeStruct((m, n), out_dtype),
        compiler_params=pltpu.CompilerParams(dimension_semantics=("parallel", "parallel")),
    )(*parts, w)


def _mm_nt(dy, w, name, out_dtype=F32):
    m, n = dy.shape
    k = w.shape[0]
    wide = n > 3328
    tm = _pick(m, (320, 256, 128)) if wide else _pick(m, (640, 512, 256, 128))
    tk = _col_tile(k, 512 if wide else 1408)

    def body(dy_ref, w_ref, o_ref):
        o_ref[...] = _dot_nt(dy_ref[...].astype(BF16), w_ref[...]).astype(o_ref.dtype)

    return pl.pallas_call(
        body, name=name, grid=(k // tk, m // tm),
        in_specs=[pl.BlockSpec((tm, n), lambda j, i: (i, 0)),
                  pl.BlockSpec((tk, n), lambda j, i: (j, 0))],
        out_specs=pl.BlockSpec((tm, tk), lambda j, i: (i, j)),
        out_shape=jax.ShapeDtypeStruct((m, k), out_dtype),
        compiler_params=pltpu.CompilerParams(dimension_semantics=("parallel", "parallel")),
    )(dy, w)


def _mm_tn(a, dy, name):
    m, k = a.shape
    n = dy.shape[1]
    tm = _pick(m, (640, 512, 256, 128))
    tk = _col_tile(k, 1408)
    tn = _col_tile(n, 1664)
    nsteps = m // tm

    def body(a_ref, dy_ref, o_ref, acc):
        @pl.when(pl.program_id(2) == 0)
        def _():
            acc[...] = jnp.zeros_like(acc)

        acc[...] += _dot_tn(a_ref[...].astype(BF16), dy_ref[...].astype(BF16))

        @pl.when(pl.program_id(2) == nsteps - 1)
        def _():
            o_ref[...] = acc[...].astype(o_ref.dtype)

    return pl.pallas_call(
        body, name=name, grid=(k // tk, n // tn, nsteps),
        in_specs=[pl.BlockSpec((tm, tk), lambda kk, j, i: (i, kk)),
                  pl.BlockSpec((tm, tn), lambda kk, j, i: (i, j))],
        out_specs=pl.BlockSpec((tk, tn), lambda kk, j, i: (kk, j)),
        out_shape=jax.ShapeDtypeStruct((k, n), BF16),
        scratch_shapes=[pltpu.VMEM((tk, tn), F32)],
        compiler_params=pltpu.CompilerParams(
            dimension_semantics=("parallel", "parallel", "arbitrary")),
    )(a, dy)


def _mm_nn_sh(a, w4, n_out, name):
    m, k = a.shape
    s, _, n = w4.shape
    tm = _pick(m, (320, 256, 128))

    def body(a_ref, w_ref, o_ref):
        av = a_ref[...].astype(BF16)
        for j in range(s):
            o_ref[:, j * n:(j + 1) * n] = _dot(av, w_ref[j])
        if n_out > s * n:
            o_ref[:, s * n:] = jnp.zeros((tm, n_out - s * n), F32)

    return pl.pallas_call(
        body, name=name, grid=(m // tm,),
        in_specs=[pl.BlockSpec((tm, k), lambda i: (i, 0)),
                  pl.BlockSpec((s, k, n), lambda i: (0, 0, 0))],
        out_specs=pl.BlockSpec((tm, n_out), lambda i: (i, 0)),
        out_shape=jax.ShapeDtypeStruct((m, n_out), F32),
        compiler_params=pltpu.CompilerParams(dimension_semantics=("parallel",)),
    )(a, w4)


def _mm_nt_sh(dy, w4, name, plan=None):
    dys = dy if isinstance(dy, (tuple, list)) else (dy,)
    m = dys[0].shape[0]
    s, k, n = w4.shape
    tm = _pick(m, (640, 512, 256, 128))
    tk = _col_tile(k, 1024)
    where = _shard_columns(dys, s, n)
    p_in, p_shapes, p_out, p_scr = _plan_parts(plan)
    nd, nj, ni = len(dys), k // tk, m // tm

    def body(*refs):
        w_ref = refs[nd]
        cins = refs[nd + 1:nd + 1 + len(p_in)]
        o_ref = refs[nd + 1 + len(p_in)]
        couts = refs[nd + 2 + len(p_in):nd + 2 + len(p_in) + len(p_out)]
        sems = refs[nd + 2 + len(p_in) + len(p_out):]
        step = pl.program_id(0) * ni + pl.program_id(1)
        if plan is not None:
            @pl.when(step == 0)
            def _():
                plan.start(cins, couts, sems)

        acc = None
        for j, (p, c0) in enumerate(where):
            t = _dot_nt(refs[p][:, c0:c0 + n].astype(BF16), w_ref[j])
            acc = t if acc is None else acc + t
        o_ref[...] = acc
        if plan is not None:
            @pl.when(step == nj * ni - 1)
            def _():
                plan.wait(cins, couts, sems)

    sem = ("arbitrary", "arbitrary") if plan is not None else ("parallel", "parallel")
    res = pl.pallas_call(
        body, name=name, grid=(nj, ni),
        in_specs=[pl.BlockSpec((tm, d.shape[1]), lambda j, i: (i, 0)) for d in dys]
        + [pl.BlockSpec((s, tk, n), lambda j, i: (0, j, 0))] + p_in,
        out_specs=[pl.BlockSpec((tm, tk), lambda j, i: (i, j))] + p_out,
        out_shape=[jax.ShapeDtypeStruct((m, k), F32)] + p_shapes,
        scratch_shapes=p_scr,
        compiler_params=pltpu.CompilerParams(dimension_semantics=sem),
    )(*dys, w4, *(plan.ins if plan is not None else []))
    return res[0] if plan is None else (res[0], res[1:])


def _shard_columns(dys, s, n):
    where = []
    for p, d in enumerate(dys):
        where += [(p, c * n) for c in range(d.shape[1] // n)]
    assert len(where) >= s
    return where[:s]


def _mm_tn_sh(a, dy, n, name):
    dys = dy if isinstance(dy, (tuple, list)) else (dy,)
    m, k = a.shape
    s = N_CHIPS
    tm = _pick(m, (640, 512, 256, 128))
    tk = _col_tile(k, 512)
    nsteps = m // tm
    where = _shard_columns(dys, s, n)

    def body(*refs):
        a_ref, o_ref, acc = refs[0], refs[len(dys) + 1], refs[len(dys) + 2]

        @pl.when(pl.program_id(1) == 0)
        def _():
            acc[...] = jnp.zeros_like(acc)

        av = a_ref[...].astype(BF16)
        for j, (p, c0) in enumerate(where):
            acc[j] += _dot_tn(av, refs[1 + p][:, c0:c0 + n].astype(BF16))

        @pl.when(pl.program_id(1) == nsteps - 1)
        def _():
            o_ref[...] = acc[...].astype(o_ref.dtype)

    return pl.pallas_call(
        body, name=name, grid=(k // tk, nsteps),
        in_specs=[pl.BlockSpec((tm, tk), lambda kk, i: (i, kk))]
        + [pl.BlockSpec((tm, d.shape[1]), lambda kk, i: (i, 0)) for d in dys],
        out_specs=pl.BlockSpec((s, tk, n), lambda kk, i: (0, kk, 0)),
        out_shape=jax.ShapeDtypeStruct((s, k, n), BF16),
        scratch_shapes=[pltpu.VMEM((s, tk, n), F32)],
        compiler_params=pltpu.CompilerParams(dimension_semantics=("parallel", "arbitrary"),
                                             vmem_limit_bytes=_VMEM_LIMIT_WIDE),
    )(a, *dys)


def _rowcall(name, body, lp, tm, rows=(), prevs=(), vecs=(), outs=(), accs=(), scratch=(),
             reverse=False, seq=False):
    nt = lp // tm
    hb = tm // SUBLANES

    def ri(i):
        return nt - 1 - i if reverse else i

    in_specs, args = [], []
    for arr, w, cb in rows:
        in_specs.append(pl.BlockSpec((tm, w), lambda i, cb=cb: (ri(i), cb)))
        args.append(arr)
    for arr, w, cb in prevs:
        in_specs.append(pl.BlockSpec((SUBLANES, w), lambda i, cb=cb: (jnp.maximum(ri(i) * hb - 1, 0), cb)))
        args.append(arr)
    for arr in vecs:
        in_specs.append(pl.BlockSpec(arr.shape, lambda i, nd=arr.ndim: (0,) * nd))
        args.append(arr)
    out_shape, out_specs = [], []
    for w, dt in outs:
        out_shape.append(jax.ShapeDtypeStruct((lp, w), dt))
        out_specs.append(pl.BlockSpec((tm, w), lambda i: (ri(i), 0)))
    for shp, dt in accs:
        out_shape.append(jax.ShapeDtypeStruct(shp, dt))
        out_specs.append(pl.BlockSpec(shp, lambda i, nd=len(shp): (0,) * nd))

    def kern(*refs):
        i = pl.program_id(0)
        body(ri(i), i == 0, *refs)

    sem = ("arbitrary",) if (seq or accs) else ("parallel",)
    res = pl.pallas_call(
        kern, name=name, grid=(nt,), in_specs=in_specs, out_specs=out_specs,
        out_shape=out_shape, scratch_shapes=list(scratch),
        compiler_params=pltpu.CompilerParams(dimension_semantics=sem),
    )(*args)
    return res


def _acc_add(first, ref, val):
    @pl.when(first)
    def _():
        ref[...] = jnp.zeros_like(ref)

    ref[...] += val


def _real_rows(r, tm):
    return (r * tm + _row_iota(tm)) >= PAD


def _rmsnorm_fwd(h, g, name):
    lp, d = h.shape
    tm = _pick(lp, (640, 512, 256, 128))

    def body(r, first, h_ref, g_ref, u_ref):
        x = h_ref[...]
        rs = lax.rsqrt(jnp.mean(x * x, axis=-1, keepdims=True) + EPS)
        u_ref[...] = (x * rs * g_ref[...]).astype(u_ref.dtype)

    return _rowcall(name, body, lp, tm, rows=[(h, d, 0)], vecs=[g], outs=[(d, BF16)])[0]


def _postnorm_res_fwd(h, o, g, name, next_pre=None):
    lp, d = h.shape
    tm = _pick(lp, (640, 512, 256, 128))

    def body(r, first, h_ref, o_ref, g_ref, *rest):
        x = o_ref[...]
        rs = lax.rsqrt(jnp.mean(x * x, axis=-1, keepdims=True) + EPS)
        hn = jnp.where(_real_rows(r, tm), h_ref[...] + x * rs * g_ref[...], 0.0)
        rest[-2 if next_pre is not None else -1][...] = hn
        if next_pre is not None:
            rs2 = lax.rsqrt(jnp.mean(hn * hn, axis=-1, keepdims=True) + EPS)
            rest[-1][...] = (hn * rs2 * rest[0][...]).astype(BF16)

    if next_pre is None:
        return _rowcall(name, body, lp, tm, rows=[(h, d, 0), (o, d, 0)], vecs=[g], outs=[(d, F32)])[0]
    return _rowcall(name, body, lp, tm, rows=[(h, d, 0), (o, d, 0)], vecs=[g, next_pre],
                    outs=[(d, F32), (d, BF16)])


def _postnorm_bwd(o, g, dh, name):
    lp, d = o.shape
    tm = _pick(lp, (640, 512, 256, 128))

    def body(r, first, o_ref, dh_ref, g_ref, do_ref, dg_ref):
        dx, dgt = _rms_bwd(o_ref[...], g_ref[...], dh_ref[...])
        do_ref[...] = dx.astype(do_ref.dtype)
        _acc_add(first, dg_ref, jnp.sum(dgt, axis=0, keepdims=True))

    return _rowcall(name, body, lp, tm, rows=[(o, d, 0), (dh, d, 0)], vecs=[g],
                    outs=[(d, BF16)], accs=[((1, d), F32)])


def _prenorm_bwd(h, g, du, dh_res, name):
    lp, d = h.shape
    tm = _pick(lp, (640, 512, 256, 128))

    def body(r, first, h_ref, du_ref, dres_ref, g_ref, dh_ref, dg_ref):
        dx, dgt = _rms_bwd(h_ref[...], g_ref[...], du_ref[...])
        dh_ref[...] = jnp.where(_real_rows(r, tm), dres_ref[...] + dx, 0.0)
        _acc_add(first, dg_ref, jnp.sum(dgt, axis=0, keepdims=True))

    return _rowcall(name, body, lp, tm, rows=[(h, d, 0), (du, d, 0), (dh_res, d, 0)], vecs=[g],
                    outs=[(d, F32)], accs=[((1, d), F32)])


def _loss_fwd_bwd(h, tgt, name):
    lp, d = h.shape
    tm = _pick(lp, (640, 512, 256, 128))

    def body(r, first, h_ref, t_ref, dh_ref, ls_ref):
        tok = (r * tm + _row_iota(tm)) >= BLOCK
        e = jnp.where(tok, h_ref[...] - t_ref[...], 0.0)
        dh_ref[...] = e * (1.0 / d)
        _acc_add(first, ls_ref, jnp.sum(e * e, axis=0, keepdims=True))

    return _rowcall(name, body, lp, tm, rows=[(h, d, 0), (tgt, d, 0)],
                    outs=[(d, F32)], accs=[((1, d), F32)])


def _conv_tiles(lp, width):
    wc = _col_tile(width, 1408)
    tm = _pick(lp, (320, 256, 128))
    return tm, wc


def _conv_fwd(x, col_off, width, w, b, name):
    lp = x.shape[0]
    kk = w.shape[0]
    tm, wc = _conv_tiles(lp, width)
    offb = col_off // wc
    assert col_off % wc == 0
    hb = tm // SUBLANES

    def body(x_ref, xp_ref, w_ref, b_ref, y_ref):
        i = pl.program_id(1)
        xv = x_ref[...]
        halo = jnp.where(i > 0, xp_ref[...], 0.0)
        xx = jnp.concatenate([halo, xv], axis=0)
        acc = b_ref[...] + w_ref[kk - 1:kk, :] * xv
        for j in range(1, kk):
            acc = acc + w_ref[kk - 1 - j:kk - j, :] * pltpu.roll(xx, j, 0)[SUBLANES:, :]
        y_ref[...] = acc

    return pl.pallas_call(
        body, name=name, grid=(width // wc, lp // tm),
        in_specs=[pl.BlockSpec((tm, wc), lambda j, i: (i, offb + j)),
                  pl.BlockSpec((SUBLANES, wc), lambda j, i: (jnp.maximum(i * hb - 1, 0), offb + j)),
                  pl.BlockSpec((kk, wc), lambda j, i: (0, j)),
                  pl.BlockSpec((1, wc), lambda j, i: (0, j))],
        out_specs=pl.BlockSpec((tm, wc), lambda j, i: (i, j)),
        out_shape=jax.ShapeDtypeStruct((lp, width), F32),
        compiler_params=pltpu.CompilerParams(dimension_semantics=("parallel", "parallel")),
    )(x, x, w, b)


def _conv_bwd(x, col_off, width, dy, w, name, w_col_off=0):
    lp = x.shape[0]
    kk = w.shape[0]
    tm, wc = _conv_tiles(lp, width)
    offb = col_off // wc
    woffb = w_col_off // wc
    assert col_off % wc == 0 and w_col_off % wc == 0
    hrows = SUBLANES * (4 // dy.dtype.itemsize)
    ext = tm + hrows

    def body(x_ref, dy_ref, dn_ref, w_ref, dx_ref, dw_ref, db_ref):
        i = pl.program_id(1)
        last = pl.num_programs(1) - 1
        xv = x_ref[...]
        dyv = dy_ref[...].astype(F32)
        dd = jnp.concatenate([dyv, jnp.where(i < last, dn_ref[...].astype(F32), 0.0)], axis=0)
        dx = w_ref[kk - 1:kk, :] * dyv
        rows = [jnp.sum(dyv * xv, axis=0, keepdims=True)]
        for m in range(1, kk):
            ahead = pltpu.roll(dd, ext - m, 0)[:tm, :]
            dx = dx + w_ref[kk - 1 - m:kk - m, :] * ahead
            rows.append(jnp.sum(ahead * xv, axis=0, keepdims=True))
        dx_ref[...] = dx.astype(dx_ref.dtype)
        dwp = jnp.concatenate(rows[::-1] + [jnp.zeros((SUBLANES - kk, wc), F32)], axis=0)

        @pl.when(i == 0)
        def _():
            dw_ref[...] = jnp.zeros_like(dw_ref)
            db_ref[...] = jnp.zeros_like(db_ref)

        dw_ref[...] += dwp
        db_ref[...] += jnp.sum(dyv, axis=0, keepdims=True)

    return pl.pallas_call(
        body, name=name, grid=(width // wc, lp // tm),
        in_specs=[pl.BlockSpec((tm, wc), lambda j, i: (i, offb + j)),
                  pl.BlockSpec((tm, wc), lambda j, i: (i, j)),
                  pl.BlockSpec((hrows, wc), lambda j, i: (jnp.minimum((i + 1) * (tm // hrows), lp // hrows - 1), j)),
                  pl.BlockSpec((kk, wc), lambda j, i: (0, woffb + j))],
        out_specs=[pl.BlockSpec((tm, wc), lambda j, i: (i, j)),
                   pl.BlockSpec((SUBLANES, wc), lambda j, i: (0, j)),
                   pl.BlockSpec((1, wc), lambda j, i: (0, j))],
        out_shape=[jax.ShapeDtypeStruct((lp, width), BF16),
                   jax.ShapeDtypeStruct((SUBLANES, width), F32),
                   jax.ShapeDtypeStruct((1, width), F32)],
        compiler_params=pltpu.CompilerParams(dimension_semantics=("parallel", "arbitrary")),
    )(x, dy, dy, w)


_FFN_K = 3
_FFN_WC = 1408


def _conv3_ext(x_ext, w_ref, b_ref):
    return (b_ref[...] + w_ref[2:3, :] * x_ext + w_ref[1:2, :] * pltpu.roll(x_ext, 1, 0)
            + w_ref[0:1, :] * pltpu.roll(x_ext, 2, 0))


def _ffn_convact_fwd(hp, cw, cb, name):
    lp = hp.shape[0]
    tm = _pick(lp, (320, 256, 128))
    wc = _FFN_WC
    nj = D_FF // wc
    hb = tm // SUBLANES

    def body(g_ref, gp_ref, u_ref, up_ref, wg_ref, wu_ref, bg_ref, bu_ref, a_ref, hg_ref, hu_ref):
        i = pl.program_id(1)

        def conv(x_ref, p_ref, w_ref, b_ref):
            x_ext = jnp.concatenate([jnp.where(i > 0, p_ref[...], 0.0), x_ref[...]], axis=0)
            return _conv3_ext(x_ext, w_ref, b_ref)[SUBLANES:, :]

        hg = conv(g_ref, gp_ref, wg_ref, bg_ref)
        hu = conv(u_ref, up_ref, wu_ref, bu_ref)
        a_ref[...] = (_gelu(hg) * hu).astype(a_ref.dtype)
        hg_ref[...] = hg.astype(hg_ref.dtype)
        hu_ref[...] = hu.astype(hu_ref.dtype)

    tile = lambda off: pl.BlockSpec((tm, wc), lambda j, i: (i, off + j))
    prev = lambda off: pl.BlockSpec((SUBLANES, wc), lambda j, i: (jnp.maximum(i * hb - 1, 0), off + j))
    vec = lambda rows, off: pl.BlockSpec((rows, wc), lambda j, i: (0, off + j))
    return pl.pallas_call(
        body, name=name, grid=(nj, lp // tm),
        in_specs=[tile(0), prev(0), tile(nj), prev(nj), vec(_FFN_K, 0), vec(_FFN_K, nj), vec(1, 0), vec(1, nj)],
        out_specs=[tile(0)] * 3,
        out_shape=[jax.ShapeDtypeStruct((lp, D_FF), BF16)] * 3,
        compiler_params=pltpu.CompilerParams(dimension_semantics=("parallel", "parallel")),
    )(hp, hp, hp, hp, cw, cw, cb, cb)


def _ffn_act_bwd(hg, hu, dact, name):
    lp = hg.shape[0]
    tm = _pick(lp, (320, 256, 128))

    def body(r, first, g_ref, u_ref, da_ref, dg_ref, du_ref):
        gl, dgl = _gelu_and_grad(g_ref[...].astype(F32))
        da = da_ref[...].astype(F32)
        dg_ref[...] = (da * u_ref[...].astype(F32) * dgl).astype(dg_ref.dtype)
        du_ref[...] = (da * gl).astype(du_ref.dtype)

    return _rowcall(name, body, lp, tm, rows=[(hg, D_FF, 0), (hu, D_FF, 0), (dact, D_FF, 0)],
                    outs=[(D_FF, BF16), (D_FF, BF16)])


def _lru_gates(x, wa_ref, wx_ref, ba, bx, lam):
    xb = x.astype(BF16)
    za, zx = [], []
    for n in range(LRU_BLOCKS):
        xs = xb[:, n * LRU_BS:(n + 1) * LRU_BS]
        za.append(_dot(xs, wa_ref[n]))
        zx.append(_dot(xs, wx_ref[n]))
    r = _sigmoid(jnp.concatenate(za, axis=1) + ba)
    ig = _sigmoid(jnp.concatenate(zx, axis=1) + bx)
    sp = _softplus(-lam)
    log_a = -LRU_C * r * sp
    a = jnp.exp(log_a)
    om = _neg_expm1(2.0 * log_a)
    mult = jnp.sqrt(om)
    return xb, r, ig, sp, a, om, mult


def _lru_fwd(proj, xrc, wa, wx, ba, bx, lam, name):
    lp, d = xrc.shape
    tm = BLOCK

    def body(r_idx, first, gate_ref, x_ref, wa_ref, wx_ref, ba_ref, bx_ref, lam_ref,
             y_ref, h_ref, carry):
        @pl.when(first)
        def _():
            carry[...] = jnp.zeros_like(carry)

        x = x_ref[...]
        _, _, ig, _, a, _, mult = _lru_gates(x, wa_ref, wx_ref, ba_ref[...], bx_ref[...], lam_ref[...])
        u = jnp.where(_real_rows(r_idx, tm), mult * ig * x, 0.0)
        acum, hloc = _scan_fwd(a, u, tm)
        h = hloc + acum * carry[0:1, :]
        h_ref[...] = h
        carry[0:1, :] = h[tm - 1:tm, :]
        y_ref[...] = (_gelu(gate_ref[...]) * h).astype(y_ref.dtype)

    return _rowcall(name, body, lp, tm, rows=[(proj, d, 0), (xrc, d, 0)],
                    vecs=[wa, wx, ba, bx, lam], outs=[(d, BF16), (d, F32)],
                    scratch=[pltpu.VMEM((SUBLANES, d), F32)], seq=True)


def _lru_bwd(proj, xrc, hl, dmix, wa, wx, ba, bx, lam, name):
    lp, d = xrc.shape
    tm = BLOCK

    def body(r_idx, first, gate_ref, x_ref, h_ref, dy_ref, hp_ref, wa_ref, wx_ref, ba_ref, bx_ref,
             lam_ref, dgate_ref, dx_ref, dwa_ref, dwx_ref, dba_ref, dbx_ref, dlam_ref, carry):
        @pl.when(first)
        def _():
            carry[...] = jnp.zeros_like(carry)
            dwa_ref[...] = jnp.zeros_like(dwa_ref)
            dwx_ref[...] = jnp.zeros_like(dwx_ref)
            dba_ref[...] = jnp.zeros_like(dba_ref)
            dbx_ref[...] = jnp.zeros_like(dbx_ref)
            dlam_ref[...] = jnp.zeros_like(dlam_ref)

        x = x_ref[...]
        lam = lam_ref[...]
        xb, r, ig, sp, a, om, mult = _lru_gates(x, wa_ref, wx_ref, ba_ref[...], bx_ref[...], lam)
        h = h_ref[...]
        dy = dy_ref[...]
        gl, dgl = _gelu_and_grad(gate_ref[...])
        dgate_ref[...] = (dy * h * dgl).astype(dgate_ref.dtype)
        row = _row_iota(tm)
        lastrow = row == tm - 1
        xg = dy * gl + jnp.where(lastrow, carry[0:1, :], 0.0)
        c = jnp.where(lastrow, 1.0, pltpu.roll(a, tm - 1, 0))
        _, g = _scan_rev(c, xg, tm)
        carry[0:1, :] = a[0:1, :] * g[0:1, :]
        hprev_in = jnp.where(r_idx > 0, hp_ref[SUBLANES - 1:SUBLANES, :], 0.0)
        hprev = jnp.where(row == 0, hprev_in, pltpu.roll(h, 1, 0))
        du = jnp.where(_real_rows(r_idx, tm), g, 0.0)
        da = g * hprev
        dmult = du * ig * x
        dig = du * mult * x
        dxv = du * mult * ig
        e2 = 1.0 - om
        dlog_a = da * a - dmult * e2 / mult
        dr = dlog_a * (-LRU_C) * sp
        dsp = jnp.sum(dlog_a * (-LRU_C) * r, axis=0, keepdims=True)
        dlam_ref[...] += -dsp * _sigmoid(-lam)
        dza = dr * r * (1.0 - r)
        dzx = dig * ig * (1.0 - ig)
        dba_ref[...] += jnp.sum(dza, axis=0, keepdims=True)
        dbx_ref[...] += jnp.sum(dzx, axis=0, keepdims=True)
        dzab = dza.astype(BF16)
        dzxb = dzx.astype(BF16)
        parts = []
        for n in range(LRU_BLOCKS):
            sl = slice(n * LRU_BS, (n + 1) * LRU_BS)
            dwa_ref[n] += _dot_tn(xb[:, sl], dzab[:, sl])
            dwx_ref[n] += _dot_tn(xb[:, sl], dzxb[:, sl])
            parts.append(_dot_nt(dzab[:, sl], wa_ref[n]) + _dot_nt(dzxb[:, sl], wx_ref[n]))
        dx_ref[...] = dxv + jnp.concatenate(parts, axis=1)

    return _rowcall(name, body, lp, tm,
                    rows=[(proj, d, 0), (xrc, d, 0), (hl, d, 0), (dmix, d, 0)],
                    prevs=[(hl, d, 0)], vecs=[wa, wx, ba, bx, lam],
                    outs=[(d, BF16), (d, F32)],
                    accs=[((LRU_BLOCKS, LRU_BS, LRU_BS), F32), ((LRU_BLOCKS, LRU_BS, LRU_BS), F32),
                          ((1, d), F32), ((1, d), F32), ((1, d), F32)],
                    scratch=[pltpu.VMEM((SUBLANES, d), F32)], reverse=True, seq=True)


_SLOPES = [2.0 ** (-8.0 * (h + 1) / N_Q_HEADS) for h in range(N_Q_HEADS)]
_QK_SCALE = HEAD_DIM ** -0.5
_QCOL = 2 * D_MODEL // D_MODEL
_KCOL = (3 * D_MODEL) // LANES
_VCOL = _KCOL + 1


def _attn_masks(n):
    start = pl.multiple_of(jnp.maximum(n - 1, 0) * BLOCK, BLOCK)
    qi = n * BLOCK + lax.broadcasted_iota(jnp.int32, (BLOCK, 2 * BLOCK), 0)
    kj = start + lax.broadcasted_iota(jnp.int32, (BLOCK, 2 * BLOCK), 1)
    dist = qi - kj
    ok = (kj >= BLOCK) & (dist >= 0) & (dist < WINDOW)
    dm = (n * BLOCK - PAD + lax.broadcasted_iota(jnp.int32, (BLOCK, N_META), 0)
          - lax.broadcasted_iota(jnp.int32, (BLOCK, N_META), 1))
    okm = dm >= 0
    return start, ok, dist.astype(F32), okm, jnp.minimum(dm, WINDOW).astype(F32)


def _group_rows(ref, g):
    return jnp.concatenate(
        [ref[:, (g * Q_PER_KV + hh) * HEAD_DIM:(g * Q_PER_KV + hh + 1) * HEAD_DIM] for hh in range(Q_PER_KV)],
        axis=0).astype(BF16)


def _attn_probs(qg, kg, kmg, sink_ref, g, ok, distf, okm, dmf):
    slope = jnp.stack([jnp.full((1, 1), _SLOPES[g * Q_PER_KV + hh], F32) for hh in range(Q_PER_KV)])
    sink = jnp.stack([sink_ref[0:1, g * Q_PER_KV + hh:g * Q_PER_KV + hh + 1] for hh in range(Q_PER_KV)])
    s = (_dot_nt(qg, kg) * _QK_SCALE).reshape(Q_PER_KV, BLOCK, 2 * BLOCK)
    sm = (_dot_nt(qg, kmg) * _QK_SCALE).reshape(Q_PER_KV, BLOCK, N_META)
    s = jnp.where(ok[None], s - slope * distf[None], NEG)
    sm = jnp.where(okm[None], sm - slope * dmf[None], NEG)
    mx = jnp.maximum(jnp.maximum(jnp.max(s, axis=-1, keepdims=True),
                                 jnp.max(sm, axis=-1, keepdims=True)), sink)
    p = jnp.exp(s - mx)
    pm = jnp.exp(sm - mx)
    ps = jnp.exp(sink - mx)
    inv = 1.0 / (jnp.sum(p, axis=-1, keepdims=True) + jnp.sum(pm, axis=-1, keepdims=True) + ps)
    return p * inv, pm * inv, ps * inv


def _attn_fwd(proj, sinks, name, plan=None):
    lp = proj.shape[0]
    nblk = lp // BLOCK
    p_in, p_shapes, p_out, p_scr = _plan_parts(plan)

    def body(*refs):
        q_ref, k_ref, v_ref, sink_ref = refs[:4]
        cins = refs[4:4 + len(p_in)]
        o_ref = refs[4 + len(p_in)]
        couts = refs[5 + len(p_in):5 + len(p_in) + len(p_out)]
        sems = refs[5 + len(p_in) + len(p_out):]
        n = pl.program_id(0)
        if plan is not None:
            @pl.when(n == 0)
            def _():
                plan.start(cins, couts, sems)

        start, ok, distf, okm, dmf = _attn_masks(n)
        kb = k_ref[pl.ds(start, 2 * BLOCK), :].astype(BF16)
        vb = v_ref[pl.ds(start, 2 * BLOCK), :].astype(BF16)
        km = k_ref[PAD:BLOCK, :].astype(BF16)
        vm = v_ref[PAD:BLOCK, :].astype(BF16)
        for g in range(N_KV_HEADS):
            gs = slice(g * HEAD_DIM, (g + 1) * HEAD_DIM)
            pn, pmn, _ = _attn_probs(_group_rows(q_ref, g), kb[:, gs], km[:, gs], sink_ref, g,
                                     ok, distf, okm, dmf)
            o = (_dot(pn.astype(BF16).reshape(Q_PER_KV * BLOCK, 2 * BLOCK), vb[:, gs])
                 + _dot(pmn.astype(BF16).reshape(Q_PER_KV * BLOCK, N_META), vm[:, gs]))
            for hh in range(Q_PER_KV):
                h = g * Q_PER_KV + hh
                o_ref[:, h * HEAD_DIM:(h + 1) * HEAD_DIM] = o[hh * BLOCK:(hh + 1) * BLOCK, :].astype(o_ref.dtype)
        if plan is not None:
            @pl.when(n == nblk - 1)
            def _():
                plan.wait(cins, couts, sems)

    res = pl.pallas_call(
        body, name=name, grid=(nblk,),
        in_specs=[pl.BlockSpec((BLOCK, D_MODEL), lambda n: (n, _QCOL)),
                  pl.BlockSpec((lp, LANES), lambda n: (0, _KCOL)),
                  pl.BlockSpec((lp, LANES), lambda n: (0, _VCOL)),
                  pl.BlockSpec(sinks.shape, lambda n: (0, 0))] + p_in,
        out_specs=[pl.BlockSpec((BLOCK, D_MODEL), lambda n: (n, 0))] + p_out,
        out_shape=[jax.ShapeDtypeStruct((lp, D_MODEL), BF16)] + p_shapes,
        scratch_shapes=p_scr,
        compiler_params=pltpu.CompilerParams(dimension_semantics=("arbitrary",)),
    )(proj, proj, proj, sinks, *(plan.ins if plan is not None else []))
    return res[0], res[1:]


def _attn_bwd(proj, sinks, dmix, name, plan=None):
    lp = proj.shape[0]
    nblk = lp // BLOCK

    p_in, p_shapes, p_out, p_scr = _plan_parts(plan)

    def body(*refs):
        q_ref, k_ref, v_ref, sink_ref, dy_ref = refs[:5]
        cins = refs[5:5 + len(p_in)]
        dq_ref, dk_ref, dv_ref, ds_ref = refs[5 + len(p_in):9 + len(p_in)]
        couts = refs[9 + len(p_in):9 + len(p_in) + len(p_out)]
        sems = refs[9 + len(p_in) + len(p_out):]
        n = pl.program_id(0)

        @pl.when(n == 0)
        def _():
            dk_ref[...] = jnp.zeros_like(dk_ref)
            dv_ref[...] = jnp.zeros_like(dv_ref)
            ds_ref[...] = jnp.zeros_like(ds_ref)
            if plan is not None:
                plan.start(cins, couts, sems)

        start, ok, distf, okm, dmf = _attn_masks(n)
        kb = k_ref[pl.ds(start, 2 * BLOCK), :].astype(BF16)
        vb = v_ref[pl.ds(start, 2 * BLOCK), :].astype(BF16)
        km = k_ref[PAD:BLOCK, :].astype(BF16)
        vm = v_ref[PAD:BLOCK, :].astype(BF16)
        lane16 = lax.broadcasted_iota(jnp.int32, (1, N_Q_HEADS), 1)
        dsink = jnp.zeros((1, N_Q_HEADS), F32)
        rows = Q_PER_KV * BLOCK
        for g in range(N_KV_HEADS):
            gs = slice(g * HEAD_DIM, (g + 1) * HEAD_DIM)
            qg = _group_rows(q_ref, g)
            dog = _group_rows(dy_ref, g)
            pn, pmn, psn = _attn_probs(qg, kb[:, gs], km[:, gs], sink_ref, g, ok, distf, okm, dmf)
            dp = _dot_nt(dog, vb[:, gs]).reshape(Q_PER_KV, BLOCK, 2 * BLOCK)
            dpm = _dot_nt(dog, vm[:, gs]).reshape(Q_PER_KV, BLOCK, N_META)
            delta = (jnp.sum(pn * dp, axis=-1, keepdims=True)
                     + jnp.sum(pmn * dpm, axis=-1, keepdims=True))
            dsb = (pn * (dp - delta)).astype(BF16).reshape(rows, 2 * BLOCK)
            dsm = (pmn * (dpm - delta)).astype(BF16).reshape(rows, N_META)
            dsk = jnp.sum(psn * delta, axis=1, keepdims=True)
            for hh in range(Q_PER_KV):
                dsink = dsink - jnp.where(lane16 == g * Q_PER_KV + hh, dsk[hh], 0.0)
            dq = (_dot(dsb, kb[:, gs]) + _dot(dsm, km[:, gs])) * _QK_SCALE
            for hh in range(Q_PER_KV):
                h = g * Q_PER_KV + hh
                dq_ref[:, h * HEAD_DIM:(h + 1) * HEAD_DIM] = dq[hh * BLOCK:(hh + 1) * BLOCK, :].astype(dq_ref.dtype)
            pnb = pn.astype(BF16).reshape(rows, 2 * BLOCK)
            pmnb = pmn.astype(BF16).reshape(rows, N_META)
            dk_ref[pl.ds(start, 2 * BLOCK), gs] += _dot_tn(dsb, qg) * _QK_SCALE
            dv_ref[pl.ds(start, 2 * BLOCK), gs] += _dot_tn(pnb, dog)
            dk_ref[PAD:BLOCK, gs] += _dot_tn(dsm, qg) * _QK_SCALE
            dv_ref[PAD:BLOCK, gs] += _dot_tn(pmnb, dog)
        ds_ref[...] += dsink
        if plan is not None:
            @pl.when(n == nblk - 1)
            def _():
                plan.wait(cins, couts, sems)

    res = pl.pallas_call(
        body, name=name, grid=(nblk,),
        in_specs=[pl.BlockSpec((BLOCK, D_MODEL), lambda n: (n, _QCOL)),
                  pl.BlockSpec((lp, LANES), lambda n: (0, _KCOL)),
                  pl.BlockSpec((lp, LANES), lambda n: (0, _VCOL)),
                  pl.BlockSpec(sinks.shape, lambda n: (0, 0)),
                  pl.BlockSpec((BLOCK, D_MODEL), lambda n: (n, 1))] + p_in,
        out_specs=[pl.BlockSpec((BLOCK, D_MODEL), lambda n: (n, 0)),
                   pl.BlockSpec((lp, LANES), lambda n: (0, 0)),
                   pl.BlockSpec((lp, LANES), lambda n: (0, 0)),
                   pl.BlockSpec((1, N_Q_HEADS), lambda n: (0, 0))] + p_out,
        out_shape=[jax.ShapeDtypeStruct((lp, D_MODEL), BF16),
                   jax.ShapeDtypeStruct((lp, LANES), F32),
                   jax.ShapeDtypeStruct((lp, LANES), F32),
                   jax.ShapeDtypeStruct((1, N_Q_HEADS), F32)] + p_shapes,
        scratch_shapes=p_scr,
        compiler_params=pltpu.CompilerParams(dimension_semantics=("arbitrary",)),
    )(proj, proj, proj, sinks, dmix, *(plan.ins if plan is not None else []))
    return res[:4], res[4:]


_ZW = D_SSM
_XBC_W = D_SSM + 2 * SSD_GROUPS * SSD_N
_DT_COL = (_ZW + _XBC_W) // LANES
EVEN_IN = 3 * D_MODEL + 2 * LANES
ODD_IN = _ZW + _XBC_W + SSD_HEADS
ODD_IN_PAD = _ZW + _XBC_W + LANES


def _ssm_convprep_fwd(proj, cw, cb, dt_bias, name):
    lp = proj.shape[0]
    kk = cw.shape[0]
    tm, wc = _conv_tiles(lp, _XBC_W)
    offb = _ZW // wc
    nj = _XBC_W // wc
    hb = tm // SUBLANES

    def body(x_ref, xp_ref, dtr_ref, w_ref, b_ref, bias_ref, xc_ref, act_ref, dt_ref):
        i, j = pl.program_id(0), pl.program_id(1)
        real = _real_rows(i, tm)
        xv = x_ref[...]
        xx = jnp.concatenate([jnp.where(i > 0, xp_ref[...], 0.0), xv], axis=0)
        acc = b_ref[...] + w_ref[kk - 1:kk, :] * xv
        for m in range(1, kk):
            acc = acc + w_ref[kk - 1 - m:kk - m, :] * pltpu.roll(xx, m, 0)[SUBLANES:, :]
        xc_ref[...] = acc
        act, _ = _silu_and_grad(acc)
        act_ref[...] = jnp.where(real, act, 0.0)

        @pl.when(j == 0)
        def _():
            dt_ref[...] = jnp.where(real, _softplus(dtr_ref[...] + bias_ref[...]), 0.0)

    return pl.pallas_call(
        body, name=name, grid=(lp // tm, nj),
        in_specs=[pl.BlockSpec((tm, wc), lambda i, j: (i, offb + j)),
                  pl.BlockSpec((SUBLANES, wc), lambda i, j: (jnp.maximum(i * hb - 1, 0), offb + j)),
                  pl.BlockSpec((tm, LANES), lambda i, j: (i, _DT_COL)),
                  pl.BlockSpec((kk, wc), lambda i, j: (0, j)),
                  pl.BlockSpec((1, wc), lambda i, j: (0, j)),
                  pl.BlockSpec((1, LANES), lambda i, j: (0, 0))],
        out_specs=[pl.BlockSpec((tm, wc), lambda i, j: (i, j)),
                   pl.BlockSpec((tm, wc), lambda i, j: (i, j)),
                   pl.BlockSpec((tm, LANES), lambda i, j: (i, 0))],
        out_shape=[jax.ShapeDtypeStruct((lp, _XBC_W), F32), jax.ShapeDtypeStruct((lp, _XBC_W), F32),
                   jax.ShapeDtypeStruct((lp, LANES), F32)],
        compiler_params=pltpu.CompilerParams(dimension_semantics=("parallel", "arbitrary")),
    )(proj, proj, proj, cw, cb, dt_bias)


def _ssm_prep_bwd(xc, proj, dt_bias, dxs, dxskip, db, dc, ddt, name):
    lp = xc.shape[0]
    tm = BLOCK

    def body(r, first, xc_ref, dtr_ref, dxs_ref, dsk_ref, db_ref, dc_ref, ddt_ref, b_ref,
             dxc_ref, ddtr_ref, dbias_ref):
        real = _real_rows(r, tm)
        _, ds = _silu_and_grad(xc_ref[...])
        up = lambda ref: ref[...].astype(F32)
        dxc_ref[:, :D_SSM] = jnp.where(
            real, (up(dxs_ref) + up(dsk_ref)) * ds[:, :D_SSM], 0.0).astype(dxc_ref.dtype)
        dxc_ref[:, D_SSM:D_SSM + 1024] = jnp.where(
            real, up(db_ref) * ds[:, D_SSM:D_SSM + 1024], 0.0).astype(dxc_ref.dtype)
        dxc_ref[:, D_SSM + 1024:] = jnp.where(
            real, up(dc_ref) * ds[:, D_SSM + 1024:], 0.0).astype(dxc_ref.dtype)
        dd = jnp.where(real, ddt_ref[...] * _sigmoid(dtr_ref[...] + b_ref[...]), 0.0)
        ddtr_ref[...] = dd.astype(ddtr_ref.dtype)
        _acc_add(first, dbias_ref, jnp.sum(dd, axis=0, keepdims=True))

    return _rowcall(name, body, lp, tm,
                    rows=[(xc, _XBC_W, 0), (proj, LANES, _DT_COL), (dxs, D_SSM, 0), (dxskip, D_SSM, 0),
                          (db, 1024, 0), (dc, 1024, 0), (ddt, LANES, 0)],
                    vecs=[dt_bias], outs=[(_XBC_W, BF16), (LANES, BF16)], accs=[((1, LANES), F32)])


def _ssd_common(dt, alog):
    a = -jnp.exp(alog)
    cs = _cumsum_rows(dt * a, BLOCK)
    cst = cs.T
    cl = cs[BLOCK - 1:BLOCK, :]
    tril = (lax.broadcasted_iota(jnp.int32, (BLOCK, BLOCK), 0)
            >= lax.broadcasted_iota(jnp.int32, (BLOCK, BLOCK), 1))
    return a, cs, cst, cl, jnp.exp(cs), jnp.exp(cl - cs), jnp.exp(cl), tril


def _head_cols(ecl, g):
    lane = lax.broadcasted_iota(jnp.int32, (1, SSD_HPG * SSD_P), 1)
    e = [ecl[:, SSD_HPG * g + hh:SSD_HPG * g + hh + 1] for hh in range(SSD_HPG)]
    return jnp.where(lane < SSD_P, e[0], jnp.where(lane < 2 * SSD_P, e[1],
                                                   jnp.where(lane < 3 * SSD_P, e[2], e[3])))


def _ssd_fwd(xbc, dt, alog, name, plan=None):
    lp = xbc.shape[0]
    nc = lp // BLOCK
    gw = SSD_HPG * SSD_P
    p_in, p_shapes, p_out, p_scr = _plan_parts(plan)

    def body(*refs):
        xs_ref, b_ref, c_ref, dt_ref, alog_ref = refs[:5]
        cins = refs[5:5 + len(p_in)]
        y_ref, so_ref = refs[5 + len(p_in):7 + len(p_in)]
        couts = refs[7 + len(p_in):7 + len(p_in) + len(p_out)]
        st, fx = refs[7 + len(p_in) + len(p_out):9 + len(p_in) + len(p_out)]
        sems = refs[9 + len(p_in) + len(p_out):]
        n = pl.program_id(0)

        @pl.when(n == 0)
        def _():
            st[...] = jnp.zeros_like(st)
            if plan is not None:
                plan.start(cins, couts, sems)

        dtv = dt_ref[...]
        _, cs, cst, cl, e, f, ecl, tril = _ssd_common(dtv, alog_ref[...])
        for g in range(SSD_GROUPS):
            bg = b_ref[:, g * SSD_N:(g + 1) * SSD_N].astype(BF16)
            cg = c_ref[:, g * SSD_N:(g + 1) * SSD_N].astype(BF16)
            gm = _dot_nt(cg, bg)
            stg = st[g]
            so_ref[0, g] = stg
            yoff = _dot(cg, stg.astype(BF16))
            heads = [SSD_HPG * g + hh for hh in range(SSD_HPG)]
            cols = lambda v: jnp.stack([v[:, h:h + 1] for h in heads])
            x4 = jnp.stack([xs_ref[:, h * SSD_P:(h + 1) * SSD_P] for h in heads])
            csr = jnp.stack([cst[h:h + 1, :] for h in heads])
            m = gm[None] * jnp.exp(jnp.where(tril[None], cols(cs) - csr, NEG))
            xdt = x4 * cols(dtv)
            yoff4 = jnp.stack([yoff[:, hh * SSD_P:(hh + 1) * SSD_P] for hh in range(SSD_HPG)])
            y4 = (jnp.einsum("hls,hsp->hlp", m.astype(BF16), xdt.astype(BF16), preferred_element_type=F32)
                  + cols(e) * yoff4)
            fx4 = cols(f) * xdt
            for hh, h in enumerate(heads):
                y_ref[:, h * SSD_P:(h + 1) * SSD_P] = y4[hh]
                fx[:, hh * SSD_P:(hh + 1) * SSD_P] = fx4[hh]
            st[g] = stg * _head_cols(ecl, g) + _dot_tn(bg, fx[...].astype(BF16))
        if plan is not None:
            @pl.when(n == nc - 1)
            def _():
                plan.wait(cins, couts, sems)

    res = pl.pallas_call(
        body, name=name, grid=(nc,),
        in_specs=[pl.BlockSpec((BLOCK, D_SSM), lambda n: (n, 0)),
                  pl.BlockSpec((BLOCK, 1024), lambda n: (n, 2)),
                  pl.BlockSpec((BLOCK, 1024), lambda n: (n, 3)),
                  pl.BlockSpec((BLOCK, LANES), lambda n: (n, 0)),
                  pl.BlockSpec((1, LANES), lambda n: (0, 0))] + p_in,
        out_specs=[pl.BlockSpec((BLOCK, D_SSM), lambda n: (n, 0)),
                   pl.BlockSpec((1, SSD_GROUPS, SSD_N, gw), lambda n: (n, 0, 0, 0))] + p_out,
        out_shape=[jax.ShapeDtypeStruct((lp, D_SSM), F32),
                   jax.ShapeDtypeStruct((nc, SSD_GROUPS, SSD_N, gw), F32)] + p_shapes,
        scratch_shapes=[pltpu.VMEM((SSD_GROUPS, SSD_N, gw), F32), pltpu.VMEM((BLOCK, gw), F32)] + p_scr,
        compiler_params=pltpu.CompilerParams(dimension_semantics=("arbitrary",)),
    )(xbc, xbc, xbc, dt, alog, *(plan.ins if plan is not None else []))
    return res[:2], res[2:]


def _ssd_bwd(xbc, dt, alog, states, dy, name, plan=None):
    lp = xbc.shape[0]
    nc = lp // BLOCK
    gw = SSD_HPG * SSD_P
    p_in, p_shapes, p_out, p_scr = _plan_parts(plan)

    def body(*refs):
        xs_ref, b_ref, c_ref, dt_ref, alog_ref, dy_ref, st_ref = refs[:7]
        cins = refs[7:7 + len(p_in)]
        dxs_ref, db_ref, dc_ref, ddt_ref, dalog_ref = refs[7 + len(p_in):12 + len(p_in)]
        couts = refs[12 + len(p_in):12 + len(p_in) + len(p_out)]
        dst, edy, fx = refs[12 + len(p_in) + len(p_out):15 + len(p_in) + len(p_out)]
        sems = refs[15 + len(p_in) + len(p_out):]
        i = pl.program_id(0)

        @pl.when(i == 0)
        def _():
            dst[...] = jnp.zeros_like(dst)
            dalog_ref[...] = jnp.zeros_like(dalog_ref)
            if plan is not None:
                plan.start(cins, couts, sems)

        dtv = dt_ref[...]
        a, cs, cst, cl, e, f, ecl, tril = _ssd_common(dtv, alog_ref[...])
        lane = lax.broadcasted_iota(jnp.int32, (1, LANES), 1)
        sub = _row_iota(BLOCK)
        triu = (lax.broadcasted_iota(jnp.int32, (BLOCK, BLOCK), 1)
                >= lax.broadcasted_iota(jnp.int32, (BLOCK, BLOCK), 0))
        dcs = jnp.zeros((BLOCK, LANES), F32)
        dcst = jnp.zeros((LANES, BLOCK), F32)
        dcl = jnp.zeros((1, LANES), F32)
        ddtx = jnp.zeros((BLOCK, LANES), F32)
        for g in range(SSD_GROUPS):
            bg = b_ref[:, g * SSD_N:(g + 1) * SSD_N].astype(BF16)
            cg = c_ref[:, g * SSD_N:(g + 1) * SSD_N].astype(BF16)
            gm = _dot_nt(cg, bg)
            stg = st_ref[0, g]
            stb = stg.astype(BF16)
            dso = dst[g]
            dsob = dso.astype(BF16)
            yraw = _dot(cg, stb)
            dfx = _dot(bg, dsob)
            prodsum = jnp.sum(dso * stg, axis=0, keepdims=True)
            heads = [SSD_HPG * g + hh for hh in range(SSD_HPG)]
            cols = lambda v: jnp.stack([v[:, h:h + 1] for h in heads])
            parts = lambda v: jnp.stack([v[:, hh * SSD_P:(hh + 1) * SSD_P] for hh in range(SSD_HPG)])
            x4 = jnp.stack([xs_ref[:, h * SSD_P:(h + 1) * SSD_P] for h in heads])
            dy4 = jnp.stack([dy_ref[:, h * SSD_P:(h + 1) * SSD_P] for h in heads])
            csc, dtc, ec, fc = cols(cs), cols(dtv), cols(e), cols(f)
            csr = jnp.stack([cst[h:h + 1, :] for h in heads])
            seg = csc - csr
            lam = jnp.exp(jnp.where(tril[None], seg, NEG))
            lamt = jnp.exp(jnp.where(triu[None], -seg, NEG))
            mt = _dot_nt(bg, cg)[None] * lamt
            xdt = x4 * dtc
            dyb = dy4.astype(BF16)
            dm = jnp.einsum("hlp,hsp->hls", dyb, xdt.astype(BF16), preferred_element_type=F32)
            dfx4 = parts(dfx)
            dxdt = jnp.einsum("hsl,hlp->hsp", mt.astype(BF16), dyb, preferred_element_type=F32) + fc * dfx4
            dml = dm * lam
            w = dml * gm[None]
            dgm = jnp.sum(dml, axis=0)
            dff = jnp.sum(dfx4 * xdt, axis=2, keepdims=True) * fc
            colv = (jnp.sum(w, axis=2, keepdims=True)
                    + jnp.sum(dy4 * parts(yraw), axis=2, keepdims=True) * ec - dff)
            roww = jnp.sum(w, axis=1, keepdims=True)
            ddtc = jnp.sum(dxdt * x4, axis=2, keepdims=True)
            dffs = jnp.sum(dff, axis=1, keepdims=True)
            dxs4 = dxdt * dtc
            edy4 = ec * dy4
            fx4 = fc * xdt
            for hh, h in enumerate(heads):
                ls = slice(hh * SSD_P, (hh + 1) * SSD_P)
                onl = (lane == h).astype(F32)
                dcs = dcs + colv[hh] * onl
                dcst = dcst - (sub == h).astype(F32) * roww[hh]
                dcl = dcl + (dffs[hh] + ecl[:, h:h + 1] * jnp.sum(prodsum[:, ls], axis=1, keepdims=True)) * onl
                ddtx = ddtx + ddtc[hh] * onl
                dxs_ref[:, h * SSD_P:(h + 1) * SSD_P] = dxs4[hh].astype(dxs_ref.dtype)
                edy[:, ls] = edy4[hh]
                fx[:, ls] = fx4[hh]
            edyb = edy[...].astype(BF16)
            fxb = fx[...].astype(BF16)
            dgb = dgm.astype(BF16)
            dc_ref[:, g * SSD_N:(g + 1) * SSD_N] = (_dot_nt(edyb, stb) + _dot(dgb, bg)).astype(dc_ref.dtype)
            db_ref[:, g * SSD_N:(g + 1) * SSD_N] = (_dot_nt(fxb, dsob) + _dot_tn(dgb, cg)).astype(db_ref.dtype)
            dst[g] = dso * _head_cols(ecl, g) + _dot_tn(cg, edyb)
        dcs = dcs + dcst.T + jnp.where(sub == BLOCK - 1, dcl, 0.0)
        dda = _rev_cumsum_rows(dcs, BLOCK)
        ddt_ref[...] = ddtx + dda * a
        dalog_ref[...] += jnp.sum(dda * dtv, axis=0, keepdims=True) * a
        if plan is not None:
            @pl.when(i == nc - 1)
            def _():
                plan.wait(cins, couts, sems)

    rev = lambda i: nc - 1 - i
    res = pl.pallas_call(
        body, name=name, grid=(nc,),
        in_specs=[pl.BlockSpec((BLOCK, D_SSM), lambda i: (rev(i), 0)),
                  pl.BlockSpec((BLOCK, 1024), lambda i: (rev(i), 2)),
                  pl.BlockSpec((BLOCK, 1024), lambda i: (rev(i), 3)),
                  pl.BlockSpec((BLOCK, LANES), lambda i: (rev(i), 0)),
                  pl.BlockSpec((1, LANES), lambda i: (0, 0)),
                  pl.BlockSpec((BLOCK, D_SSM), lambda i: (rev(i), 0)),
                  pl.BlockSpec((1, SSD_GROUPS, SSD_N, gw), lambda i: (rev(i), 0, 0, 0))] + p_in,
        out_specs=[pl.BlockSpec((BLOCK, D_SSM), lambda i: (rev(i), 0)),
                   pl.BlockSpec((BLOCK, 1024), lambda i: (rev(i), 0)),
                   pl.BlockSpec((BLOCK, 1024), lambda i: (rev(i), 0)),
                   pl.BlockSpec((BLOCK, LANES), lambda i: (rev(i), 0)),
                   pl.BlockSpec((1, LANES), lambda i: (0, 0))] + p_out,
        out_shape=[jax.ShapeDtypeStruct((lp, D_SSM), BF16),
                   jax.ShapeDtypeStruct((lp, 1024), BF16),
                   jax.ShapeDtypeStruct((lp, 1024), BF16),
                   jax.ShapeDtypeStruct((lp, LANES), F32),
                   jax.ShapeDtypeStruct((1, LANES), F32)] + p_shapes,
        scratch_shapes=[pltpu.VMEM((SSD_GROUPS, SSD_N, gw), F32),
                        pltpu.VMEM((BLOCK, gw), F32), pltpu.VMEM((BLOCK, gw), F32)] + p_scr,
        compiler_params=pltpu.CompilerParams(dimension_semantics=("arbitrary",)),
    )(xbc, xbc, xbc, dt, alog, dy, states, *(plan.ins if plan is not None else []))
    return res[:5], res[5:]


_GN_GROUPS = 8
_GN_W = D_SSM // _GN_GROUPS


def _ssm_gate_fwd(yssd, xbc, proj, dskip, gnorm, name):
    lp = yssd.shape[0]
    tm = _pick(lp, (320, 256, 128))

    def body(r, first, y_ref, x_ref, z_ref, d_ref, g_ref, o_ref):
        sz, _ = _silu_and_grad(z_ref[...])
        y2 = (y_ref[...] + d_ref[...] * x_ref[...]) * sz
        for k in range(_GN_GROUPS):
            sl = slice(k * _GN_W, (k + 1) * _GN_W)
            yk = y2[:, sl]
            rs = lax.rsqrt(jnp.mean(yk * yk, axis=-1, keepdims=True) + EPS)
            o_ref[:, sl] = (yk * rs * g_ref[:, sl]).astype(o_ref.dtype)

    return _rowcall(name, body, lp, tm, rows=[(yssd, D_SSM, 0), (xbc, D_SSM, 0), (proj, D_SSM, 0)],
                    vecs=[dskip, gnorm], outs=[(D_SSM, BF16)])[0]


def _ssm_gate_bwd(yssd, xbc, proj, dskip, gnorm, dyn, name):
    lp = yssd.shape[0]
    tm = BLOCK

    def body(r, first, y_ref, x_ref, z_ref, dyn_ref, d_ref, g_ref,
             dy_ref, dx_ref, dz_ref, dd_ref, dg_ref):
        z = z_ref[...]
        sz, dsz = _silu_and_grad(z)
        xs = x_ref[...]
        y1 = y_ref[...] + d_ref[...] * xs
        y2 = y1 * sz
        dyn = dyn_ref[...]
        for k in range(_GN_GROUPS):
            sl = slice(k * _GN_W, (k + 1) * _GN_W)
            dx, dgt = _rms_bwd(y2[:, sl], g_ref[:, sl], dyn[:, sl])
            dy1 = dx * sz[:, sl]
            dy_ref[:, sl] = dy1.astype(dy_ref.dtype)
            dx_ref[:, sl] = (dy1 * d_ref[:, sl]).astype(dx_ref.dtype)
            dz_ref[:, sl] = (dx * y1[:, sl] * dsz[:, sl]).astype(dz_ref.dtype)

            @pl.when(first)
            def _():
                dd_ref[:, sl] = jnp.zeros((1, _GN_W), F32)
                dg_ref[:, sl] = jnp.zeros((1, _GN_W), F32)

            dd_ref[:, sl] += jnp.sum(dy1 * xs[:, sl], axis=0, keepdims=True)
            dg_ref[:, sl] += jnp.sum(dgt, axis=0, keepdims=True)

    return _rowcall(name, body, lp, tm,
                    rows=[(yssd, D_SSM, 0), (xbc, D_SSM, 0), (proj, D_SSM, 0), (dyn, D_SSM, 0)],
                    vecs=[dskip, gnorm], outs=[(D_SSM, BF16), (D_SSM, BF16), (D_SSM, BF16)],
                    accs=[((1, D_SSM), F32), ((1, D_SSM), F32)])


def _shape2d(shape):
    n = math.prod(shape)
    if len(shape) == 2:
        return tuple(shape)
    return (n // LANES, LANES) if n % LANES == 0 else (1, n)


def _adamw_many(ws, gs, ms, vs, name):
    n = len(ws)
    c1 = 1.0 / (1.0 - ADAM_B1 ** ADAM_STEP)
    c2 = 1.0 / (1.0 - ADAM_B2 ** ADAM_STEP)

    def body(*refs):
        for i in range(n):
            w_ref, g_ref, m_ref, v_ref = (refs[j * n + i] for j in range(4))
            d_ref, nm_ref, nv_ref = (refs[(4 + j) * n + i] for j in range(3))
            gv = g_ref[...]
            nm = ADAM_B1 * m_ref[...] + (1.0 - ADAM_B1) * gv
            nv = ADAM_B2 * v_ref[...] + (1.0 - ADAM_B2) * (gv * gv)
            nm_ref[...] = nm
            nv_ref[...] = nv
            d_ref[...] = -ADAM_LR * ((nm * c1) / (jnp.sqrt(nv * c2) + ADAM_EPS) + ADAM_WD * w_ref[...])

    vm = pl.BlockSpec(memory_space=pltpu.VMEM)
    return pl.pallas_call(
        body, name=name, in_specs=[vm] * (4 * n), out_specs=[vm] * (3 * n),
        out_shape=[jax.ShapeDtypeStruct(w.shape, F32) for w in ws] * 3,
    )(*ws, *gs, *ms, *vs)


def _place():
    return lax.axis_index("x"), lax.axis_index("y"), lax.axis_index("c")


def _other_chips(x, y):
    return [(1 - x, y), (x, 1 - y), (1 - x, 1 - y)]


_ANY = pl.BlockSpec(memory_space=pl.ANY)


class _Plan:
    def __init__(self, ins, out_shapes, n_remote, n_local, issue):
        self.ins = list(ins)
        self.out_shapes = list(out_shapes)
        self.issue = issue
        self.scratch = [pltpu.SemaphoreType.DMA((max(n_remote, 1),)),
                        pltpu.SemaphoreType.DMA((max(n_remote, 1),)),
                        pltpu.SemaphoreType.DMA((max(n_local, 1),))]

    def start(self, ins, outs, sems):
        sends, _, locs = self.issue(ins, outs, *sems)
        for cp in locs + sends:
            cp.start()

    def wait(self, ins, outs, sems):
        sends, recvs, locs = self.issue(ins, outs, *sems)
        for make in recvs:
            make().wait_recv()
        for cp in sends:
            cp.wait_send()
        for cp in locs:
            cp.wait()


def _plan_parts(plan):
    if plan is None:
        return [], [], [], []
    return ([_ANY] * len(plan.ins), plan.out_shapes, [_ANY] * len(plan.out_shapes), plan.scratch)


def _run_plan(plan, name):
    n_in, n_out = len(plan.ins), len(plan.out_shapes)

    def body(*refs):
        ins, outs, sems = refs[:n_in], refs[n_in:n_in + n_out], refs[n_in + n_out:]
        plan.start(ins, outs, sems)
        plan.wait(ins, outs, sems)

    return pl.pallas_call(
        body, name=name, in_specs=[_ANY] * n_in, out_specs=[_ANY] * n_out,
        out_shape=plan.out_shapes, scratch_shapes=plan.scratch,
    )(*plan.ins)


def _gather_plan(shards):
    n = len(shards)

    def issue(ins, outs, send_sems, recv_sems, local_sems):
        x, y, c = _place()
        me = 2 * x + y
        sends, recvs, locs = [], [], []
        for p in range(n):
            locs.append(pltpu.make_async_copy(ins[p], outs[p].at[me], local_sems.at[p]))
            for k, (px, py) in enumerate(_other_chips(x, y)):
                sems = dict(send_sem=send_sems.at[3 * p + k], recv_sem=recv_sems.at[3 * p + k],
                            device_id=(px, py, c), device_id_type=MESH)
                sends.append(pltpu.make_async_remote_copy(src_ref=ins[p], dst_ref=outs[p].at[me], **sems))
                recvs.append(functools.partial(pltpu.make_async_remote_copy, src_ref=ins[p],
                                               dst_ref=outs[p].at[2 * px + py], **sems))
        return sends, recvs, locs

    return _Plan(shards, [jax.ShapeDtypeStruct((N_CHIPS,) + s.shape, s.dtype) for s in shards], 3 * n, n, issue)


_REL7 = [(fx, fy, fc) for fx in (0, 1) for fy in (0, 1) for fc in (0, 1)][1:]


def _scatter8_plan(gs):
    n = len(gs)

    def issue(ins, outs, send_sems, recv_sems, local_sems):
        x, y, c = _place()
        sends = []
        for p in range(n):
            hr = gs[p].shape[1] // 2
            for k, (fx, fy, fc) in enumerate(_REL7):
                tx, ty, tc = x ^ fx, y ^ fy, c ^ fc
                src = ins[p].at[2 * tx + ty, pl.ds(pl.multiple_of(tc * hr, SUBLANES), hr), :]
                sends.append(pltpu.make_async_remote_copy(
                    src_ref=src, dst_ref=outs[p].at[k],
                    send_sem=send_sems.at[7 * p + k], recv_sem=recv_sems.at[7 * p + k],
                    device_id=(tx, ty, tc), device_id_type=MESH))
        return sends, [functools.partial(lambda cp: cp, cp) for cp in sends], []

    shapes = [jax.ShapeDtypeStruct((7, g.shape[1] // 2, g.shape[2]), g.dtype) for g in gs]
    return _Plan(gs, shapes, 7 * n, 0, issue)


def _sibling_plan(ts):
    n = len(ts)

    def issue(ins, outs, send_sems, recv_sems, local_sems):
        x, y, c = _place()
        sends = [pltpu.make_async_remote_copy(
            src_ref=ins[p], dst_ref=outs[p], send_sem=send_sems.at[p], recv_sem=recv_sems.at[p],
            device_id=(x, y, 1 - c), device_id_type=MESH) for p in range(n)]
        return sends, [functools.partial(lambda cp: cp, cp) for cp in sends], []

    return _Plan(ts, [jax.ShapeDtypeStruct(t.shape, t.dtype) for t in ts], n, 0, issue)


def _add8(g, recv, chip, core, name):
    s, r, n = g.shape
    hr = r // 2
    th = hr // 2 if (hr // 2) % SUBLANES == 0 else hr
    nt = hr // th

    def body(chip_ref, core_ref, g_ref, r_ref, o_ref):
        acc = g_ref[0].astype(F32)
        for k in range(7):
            acc = acc + r_ref[k].astype(F32)
        o_ref[...] = acc

    return pl.pallas_call(
        body, name=name,
        grid_spec=pltpu.PrefetchScalarGridSpec(
            num_scalar_prefetch=2, grid=(nt,),
            in_specs=[pl.BlockSpec((1, th, n), lambda i, ch, co: (ch[0], co[0] * nt + i, 0)),
                      pl.BlockSpec((7, th, n), lambda i, ch, co: (0, i, 0))],
            out_specs=pl.BlockSpec((th, n), lambda i, ch, co: (i, 0))),
        out_shape=jax.ShapeDtypeStruct((hr, n), F32),
        compiler_params=pltpu.CompilerParams(dimension_semantics=("parallel",)),
    )(chip, core, g, recv)


def _adamw_halves(w, own, other, m, v, core, name):
    r, n = w.shape
    hr = r // 2
    th = hr // 2 if (hr // 2) % SUBLANES == 0 else hr
    tph = hr // th
    c1 = 1.0 / (1.0 - ADAM_B1 ** ADAM_STEP)
    c2 = 1.0 / (1.0 - ADAM_B2 ** ADAM_STEP)

    def body(core_ref, w_ref, a_ref, b_ref, m_ref, v_ref, g_ref, d_ref, nm_ref, nv_ref):
        half = pl.program_id(0) // tph
        gv = jnp.where(half == core_ref[0], a_ref[...], b_ref[...])
        nm = ADAM_B1 * m_ref[...] + (1.0 - ADAM_B1) * gv
        nv = ADAM_B2 * v_ref[...] + (1.0 - ADAM_B2) * (gv * gv)
        g_ref[...] = gv
        nm_ref[...] = nm
        nv_ref[...] = nv
        d_ref[...] = -ADAM_LR * ((nm * c1) / (jnp.sqrt(nv * c2) + ADAM_EPS) + ADAM_WD * w_ref[...])

    full = pl.BlockSpec((th, n), lambda i, co: (i, 0))
    part = pl.BlockSpec((th, n), lambda i, co: (i % tph, 0))
    return pl.pallas_call(
        body, name=name,
        grid_spec=pltpu.PrefetchScalarGridSpec(
            num_scalar_prefetch=1, grid=(2 * tph,),
            in_specs=[full, part, part, full, full], out_specs=[full] * 4),
        out_shape=[jax.ShapeDtypeStruct((r, n), F32)] * 4,
        compiler_params=pltpu.CompilerParams(dimension_semantics=("parallel",)),
    )(core, w, own, other, m, v)


def _allreduce_small(pack, name):
    r, l = pack.shape
    hr = r // 2
    assert hr % SUBLANES == 0

    def body(p_ref, o_ref, sib, chips, send_sems, recv_sems):
        x, y, c = _place()
        chip = 2 * x + y
        sibling = dict(device_id=(x, y, 1 - c), device_id_type=MESH)
        mine = pl.ds(pl.multiple_of(c * hr, SUBLANES), hr)
        other = pl.ds(pl.multiple_of((1 - c) * hr, SUBLANES), hr)
        a = pltpu.make_async_remote_copy(src_ref=p_ref.at[other], dst_ref=sib, send_sem=send_sems.at[0],
                                         recv_sem=recv_sems.at[0], **sibling)
        a.start()
        a.wait()
        own, got = p_ref[mine, :], sib[...]
        chips[chip] = jnp.where(c == 0, own, got) + jnp.where(c == 0, got, own)
        sends = []
        for k, (px, py) in enumerate(_other_chips(x, y)):
            cp = pltpu.make_async_remote_copy(
                src_ref=chips.at[chip], dst_ref=chips.at[chip], send_sem=send_sems.at[1 + k],
                recv_sem=recv_sems.at[1 + k], device_id=(px, py, c), device_id_type=MESH)
            cp.start()
            sends.append(cp)
        for k, (px, py) in enumerate(_other_chips(x, y)):
            pltpu.make_async_remote_copy(
                src_ref=chips.at[chip], dst_ref=chips.at[2 * px + py], send_sem=send_sems.at[1 + k],
                recv_sem=recv_sems.at[1 + k], device_id=(px, py, c), device_id_type=MESH).wait_recv()
        for cp in sends:
            cp.wait_send()
        o_ref[mine, :] = ((chips[0] + chips[1]) + chips[2]) + chips[3]
        fin = pltpu.make_async_remote_copy(src_ref=o_ref.at[mine], dst_ref=o_ref.at[mine],
                                           send_sem=send_sems.at[4], recv_sem=recv_sems.at[4], **sibling)
        fin.start()
        pltpu.make_async_remote_copy(src_ref=o_ref.at[mine], dst_ref=o_ref.at[other],
                                     send_sem=send_sems.at[4], recv_sem=recv_sems.at[4], **sibling).wait_recv()
        fin.wait_send()

    vm = pl.BlockSpec(memory_space=pltpu.VMEM)
    return pl.pallas_call(
        body, name=name, in_specs=[vm], out_specs=vm,
        out_shape=jax.ShapeDtypeStruct((r, l), F32),
        scratch_shapes=[pltpu.VMEM((hr, l), F32), pltpu.VMEM((N_CHIPS, hr, l), F32),
                        pltpu.SemaphoreType.DMA((5,)), pltpu.SemaphoreType.DMA((5,))],
    )(pack)


def _flat_rows(a, mult=SUBLANES * LANES):
    f = a.reshape(-1)
    padn = (-f.shape[0]) % mult
    if padn:
        f = jnp.concatenate([f, jnp.zeros((padn,), f.dtype)])
    return f


def _pack(arrs, mult=SUBLANES * LANES, total_mult=None):
    flat = [_flat_rows(a, mult) for a in arrs]
    sizes = [f.shape[0] for f in flat]
    if total_mult is not None:
        padn = (-sum(sizes)) % total_mult
        if padn:
            flat.append(jnp.zeros((padn,), flat[0].dtype))
    return jnp.concatenate(flat).reshape(-1, LANES), sizes


def _unpack(pack, shapes, sizes, lead=()):
    flat = pack.reshape(lead + (-1,))
    out, off = [], 0
    for shp, sz in zip(shapes, sizes):
        n = math.prod(shp)
        out.append(flat[..., off:off + n].reshape(lead + tuple(shp)))
        off += sz
    return out


def _cols_from_shards(g):
    s, k, n = g.shape
    return jnp.transpose(g, (1, 0, 2)).reshape(k, s * n)


def _cols_to_shards(w, s=N_CHIPS):
    k, n = w.shape
    return jnp.transpose(w.reshape(k, s, n // s), (1, 0, 2))


def _ffn_fwd(h, u, post, w_up, cw, cb, w_down, tag, next_pre=None):
    hp = _mm_nn_sh(u, w_up, 2 * D_FF, f"{tag}_up")
    act, hg, hu = _ffn_convact_fwd(hp, cw, cb, f"{tag}_convact")
    o = _mm_nn(act, w_down, F32, f"{tag}_down")
    res = _postnorm_res_fwd(h, o, post, f"{tag}_postnorm", next_pre)
    hn, un = res if next_pre is not None else (res, None)
    return hn, un, (h, u, hp, hg, hu, act, o)


def _ffn_bwd(dh, saved, pre, post, w_up, cw, w_down, tag):
    h, u, hp, hg, hu, act, o = saved
    do, dpost = _postnorm_bwd(o, post, dh, f"{tag}_postnorm_bwd")
    dact = _mm_nt(do, w_down, f"{tag}_down_dx", BF16)
    dw_down = _mm_tn(act, do, f"{tag}_down_dw")
    dhg, dhu = _ffn_act_bwd(hg, hu, dact, f"{tag}_act_bwd")
    dxg, dwg, dbg = _conv_bwd(hp, 0, D_FF, dhg, cw, f"{tag}_conv_bwd_gate")
    dxu, dwu, dbu = _conv_bwd(hp, D_FF, D_FF, dhu, cw, f"{tag}_conv_bwd_up", w_col_off=D_FF)
    dhp = (dxg, dxu)
    dcw = jnp.concatenate([dwg, dwu], axis=1)
    dcb = jnp.concatenate([dbg, dbu], axis=1)
    du = _mm_nt_sh(dhp, w_up, f"{tag}_up_dx")
    dw_up = _mm_tn_sh(u, dhp, w_up.shape[2], f"{tag}_up_dw")
    dhn, dpre = _prenorm_bwd(h, pre, du, dh, f"{tag}_prenorm_bwd")
    return dhn, dict(pre=dpre, post=dpost, w_up=dw_up, conv_w=dcw[:3], conv_b=dcb, w_down=dw_down)


class _Exchange:
    GATHER_IN_ATTN = ("l0_w_out", "l0_ffn_w_up", "l0_ffn_w_down", "l1_w_in")
    GATHER_IN_SSD = ("l1_w_out", "l1_ffn_w_up", "l1_ffn_w_down")
    AFTER_L1_OUT = ("l1_ffn_w_up", "l1_ffn_w_down", "l1_w_out")
    AFTER_L0_OUT = ("l1_w_in", "l0_ffn_w_up", "l0_ffn_w_down", "l0_w_out")
    LAST = ("l0_w_in",)

    def __init__(self, late_shards):
        self.late = dict(late_shards)
        self.slabs = {}
        self.recv = {}

    def gather_plan(self, names):
        return _gather_plan([self.late[n] for n in names])

    def gathered(self, names, outs):
        return {n: (g if n in _BIG_COL else g.reshape(-1, g.shape[-1])) for n, g in zip(names, outs)}

    def scatter_plan(self, grads, names):
        for n in names:
            g = grads[n]
            self.slabs[n] = g if n in _BIG_COL else g.reshape(N_CHIPS, -1, g.shape[-1])
        return _scatter8_plan([self.slabs[n] for n in names])

    def scattered(self, names, outs):
        self.recv.update(zip(names, outs))


def _local_step(x, tgt, meta, P, ex=None):
    seq, d = x.shape
    lp = seq + BLOCK
    h0 = jnp.concatenate([jnp.zeros((PAD, d), F32), meta, x], axis=0)
    tgt_p = jnp.concatenate([jnp.zeros((BLOCK, d), F32), tgt], axis=0)

    u0 = _rmsnorm_fwd(h0, P["l0_mix_pre_norm"], "l0_mix_prenorm")
    proj0 = _mm_nn_sh(u0, P["l0_w_in"], EVEN_IN, "l0_in")
    xrc = _conv_fwd(proj0, D_MODEL, D_MODEL, P["l0_lru_conv_w"], P["l0_lru_conv_b"], "l0_lru_conv")
    lru_args = (P["l0_lru_w_a"], P["l0_lru_w_x"], P["l0_lru_b_a"], P["l0_lru_b_x"], P["l0_lru_lambda"])
    ya, hl = _lru_fwd(proj0, xrc, *lru_args, "l0_lru")
    yb, outs = _attn_fwd(proj0, P["l0_attn_sinks"], "l0_attn",
                         ex.gather_plan(ex.GATHER_IN_ATTN) if ex else None)
    if ex:
        P = {**P, **ex.gathered(ex.GATHER_IN_ATTN, outs)}
    o0 = _mm_nn((ya, yb), P["l0_w_out"], F32, "l0_out")
    h1, u1 = _postnorm_res_fwd(h0, o0, P["l0_mix_post_norm"], "l0_mix_postnorm", P["l0_ffn_pre_norm"])
    h2, u2, ffn0 = _ffn_fwd(h1, u1, P["l0_ffn_post_norm"], P["l0_ffn_w_up"], P["l0_ffn_conv_w"],
                            P["l0_ffn_conv_b"], P["l0_ffn_w_down"], "l0_ffn", P["l1_mix_pre_norm"])
    proj1 = _mm_nn_sh(u2, P["l1_w_in"], ODD_IN_PAD, "l1_in")
    xc1, xbc, dt = _ssm_convprep_fwd(proj1, P["l1_ssm_conv_w"], P["l1_ssm_conv_b"], P["l1_dt_bias"],
                                     "l1_ssm_convprep")
    (yssd, states), outs = _ssd_fwd(xbc, dt, P["l1_a_log"], "l1_ssd",
                                    ex.gather_plan(ex.GATHER_IN_SSD) if ex else None)
    if ex:
        P = {**P, **ex.gathered(ex.GATHER_IN_SSD, outs)}
    yn = _ssm_gate_fwd(yssd, xbc, proj1, P["l1_d_skip"], P["l1_gate_norm"], "l1_ssm_gate")
    o1 = _mm_nn(yn, P["l1_w_out"], F32, "l1_out")
    h3, u3 = _postnorm_res_fwd(h2, o1, P["l1_mix_post_norm"], "l1_mix_postnorm", P["l1_ffn_pre_norm"])
    h4, _, ffn1 = _ffn_fwd(h3, u3, P["l1_ffn_post_norm"], P["l1_ffn_w_up"], P["l1_ffn_conv_w"],
                           P["l1_ffn_conv_b"], P["l1_ffn_w_down"], "l1_ffn")
    dh4, loss_cols = _loss_fwd_bwd(h4, tgt_p, "loss")

    G = {}
    dh3, g = _ffn_bwd(dh4, ffn1, P["l1_ffn_pre_norm"], P["l1_ffn_post_norm"], P["l1_ffn_w_up"],
                      P["l1_ffn_conv_w"], P["l1_ffn_w_down"], "l1_ffn")
    for k, v in g.items():
        G["l1_ffn_" + (k + "_norm" if k in ("pre", "post") else k)] = v
    do1, G["l1_mix_post_norm"] = _postnorm_bwd(o1, P["l1_mix_post_norm"], dh3, "l1_mix_postnorm_bwd")
    dyn = _mm_nt(do1, P["l1_w_out"], "l1_out_dx")
    G["l1_w_out"] = _mm_tn(yn, do1, "l1_out_dw")
    dyssd, dxskip, dz, dd_cols, G["l1_gate_norm"] = _ssm_gate_bwd(
        yssd, xbc, proj1, P["l1_d_skip"], P["l1_gate_norm"], dyn, "l1_ssm_gate_bwd")
    G["l1_d_skip"] = dd_cols.reshape(SSD_HEADS, SSD_P).sum(axis=1)
    (dxs, dbm, dcm, ddt, dalog), outs = _ssd_bwd(
        xbc, dt, P["l1_a_log"], states, dyssd, "l1_ssd_bwd",
        ex.scatter_plan(G, ex.AFTER_L1_OUT) if ex else None)
    if ex:
        ex.scattered(ex.AFTER_L1_OUT, outs)
    G["l1_a_log"] = dalog[0, :SSD_HEADS]
    dxc, ddtr, dbias = _ssm_prep_bwd(xc1, proj1, P["l1_dt_bias"], dxs, dxskip, dbm, dcm, ddt,
                                     "l1_ssm_prep_bwd")
    G["l1_dt_bias"] = dbias[0, :SSD_HEADS]
    dxbc, dcw, dcb = _conv_bwd(proj1, _ZW, _XBC_W, dxc, P["l1_ssm_conv_w"], "l1_ssm_conv_bwd")
    G["l1_ssm_conv_w"] = dcw[:4]
    G["l1_ssm_conv_b"] = dcb
    dproj1 = jnp.concatenate([dz, dxbc, ddtr], axis=1)
    du2 = _mm_nt_sh(dproj1, P["l1_w_in"], "l1_in_dx")
    G["l1_w_in"] = _mm_tn_sh(u2, dproj1, ODD_IN // N_CHIPS, "l1_in_dw")
    dh2, G["l1_mix_pre_norm"] = _prenorm_bwd(h2, P["l1_mix_pre_norm"], du2, dh3, "l1_mix_prenorm_bwd")
    dh1, g = _ffn_bwd(dh2, ffn0, P["l0_ffn_pre_norm"], P["l0_ffn_post_norm"], P["l0_ffn_w_up"],
                      P["l0_ffn_conv_w"], P["l0_ffn_w_down"], "l0_ffn")
    for k, v in g.items():
        G["l0_ffn_" + (k + "_norm" if k in ("pre", "post") else k)] = v
    do0, G["l0_mix_post_norm"] = _postnorm_bwd(o0, P["l0_mix_post_norm"], dh1, "l0_mix_postnorm_bwd")
    dmix = _mm_nt(do0, P["l0_w_out"], "l0_out_dx")
    G["l0_w_out"] = jnp.concatenate([_mm_tn(ya, do0, "l0_out_dw_lru"), _mm_tn(yb, do0, "l0_out_dw_attn")], axis=0)
    (dgate, dxrc, G["l0_lru_w_a"], G["l0_lru_w_x"], G["l0_lru_b_a"], G["l0_lru_b_x"],
     G["l0_lru_lambda"]) = _lru_bwd(proj0, xrc, hl, dmix, *lru_args, "l0_lru_bwd")
    dxr, dcw, dcb = _conv_bwd(proj0, D_MODEL, D_MODEL, dxrc, P["l0_lru_conv_w"], "l0_lru_conv_bwd")
    G["l0_lru_conv_w"] = dcw[:4]
    G["l0_lru_conv_b"] = dcb
    (dq, dk, dv, G["l0_attn_sinks"]), outs = _attn_bwd(
        proj0, P["l0_attn_sinks"], dmix, "l0_attn_bwd",
        ex.scatter_plan(G, ex.AFTER_L0_OUT) if ex else None)
    if ex:
        ex.scattered(ex.AFTER_L0_OUT, outs)
    dproj0 = jnp.concatenate([dgate, dxr, dq, dk.astype(BF16), dv.astype(BF16)], axis=1)
    G["l0_w_in"] = _mm_tn_sh(u0, dproj0, EVEN_IN // N_CHIPS, "l0_in_dw")
    if ex:
        du0, outs = _mm_nt_sh(dproj0, P["l0_w_in"], "l0_in_dx", ex.scatter_plan(G, ex.LAST))
        ex.scattered(ex.LAST, outs)
    else:
        du0 = _mm_nt_sh(dproj0, P["l0_w_in"], "l0_in_dx")
    dh0, G["l0_mix_pre_norm"] = _prenorm_bwd(h0, P["l0_mix_pre_norm"], du0, dh1, "l0_mix_prenorm_bwd")
    return loss_cols, dh0[BLOCK:], dh0[PAD:BLOCK], G


_BIG_COL = ("l0_w_in", "l0_ffn_w_up", "l1_w_in", "l1_ffn_w_up")
_BIG_ROW = ("l0_w_out", "l0_ffn_w_down", "l1_w_out", "l1_ffn_w_down")
_BIG = ("l0_w_in", "l0_w_out", "l0_ffn_w_up", "l0_ffn_w_down",
        "l1_w_in", "l1_w_out", "l1_ffn_w_up", "l1_ffn_w_down")
_SMALL_SHARDED = ("meta_tokens", "l0_lru_conv_w", "l0_ffn_conv_w", "l1_ssm_conv_w", "l1_ffn_conv_w")
_WEIGHTS = ("meta_tokens", "l0_mix_pre_norm", "l0_mix_post_norm", "l0_w_in", "l0_lru_conv_w",
            "l0_lru_conv_b", "l0_lru_w_a", "l0_lru_b_a", "l0_lru_w_x", "l0_lru_b_x", "l0_lru_lambda",
            "l0_attn_sinks", "l0_w_out", "l0_ffn_pre_norm", "l0_ffn_post_norm", "l0_ffn_w_up",
            "l0_ffn_conv_w", "l0_ffn_conv_b", "l0_ffn_w_down", "l1_mix_pre_norm", "l1_mix_post_norm",
            "l1_w_in", "l1_ssm_conv_w", "l1_ssm_conv_b", "l1_dt_bias", "l1_a_log", "l1_d_skip",
            "l1_gate_norm", "l1_w_out", "l1_ffn_pre_norm", "l1_ffn_post_norm", "l1_ffn_w_up",
            "l1_ffn_conv_w", "l1_ffn_conv_b", "l1_ffn_w_down")
_REPL = tuple(n for n in _WEIGHTS if n not in _BIG and n not in _SMALL_SHARDED)


def _pad_lanes(v, n=LANES):
    return jnp.concatenate([v, jnp.zeros((n - v.shape[0],), v.dtype)]).reshape(1, n)


def _step(x, tgt, W, M, V):
    cx, cy, cc = _place()
    chip = 2 * cx + cy

    small_pack, small_sizes = _pack([W[n] for n in _SMALL_SHARDED])
    first = _run_plan(_gather_plan([W["l0_w_in"].astype(BF16), small_pack]), "gather_first")
    small_full = _unpack(first[1], [W[n].shape for n in _SMALL_SHARDED], small_sizes, lead=(N_CHIPS,))
    ex = _Exchange({n: W[n].astype(BF16) for n in _BIG if n != "l0_w_in"})

    P = {"l0_w_in": first[0]}
    for n, g in zip(_SMALL_SHARDED, small_full):
        P[n] = _cols_from_shards(g)
    for n in _REPL:
        v = W[n]
        P[n] = v.reshape(1, -1) if v.ndim == 1 else v
    P["l0_lru_w_a"] = W["l0_lru_w_a"].astype(BF16)
    P["l0_lru_w_x"] = W["l0_lru_w_x"].astype(BF16)
    P["l1_dt_bias"] = _pad_lanes(W["l1_dt_bias"])
    P["l1_a_log"] = _pad_lanes(W["l1_a_log"])
    P["l1_d_skip"] = jnp.repeat(W["l1_d_skip"], SSD_P).reshape(1, D_SSM)
    meta = P.pop("meta_tokens")

    loss_cols, grad_x, grad_meta, G = _local_step(x, tgt, meta, P, ex)
    G["meta_tokens"] = grad_meta

    core_idx = cc.astype(jnp.int32).reshape(1)
    chip_idx = chip.astype(jnp.int32).reshape(1)
    own_half = [_add8(ex.slabs[n], ex.recv[n], chip_idx, core_idx, f"grad_sum_{n}") for n in _BIG]
    other_half = _run_plan(_sibling_plan(own_half), "grad_sibling_swap")
    small_names = list(_REPL) + list(_SMALL_SHARDED)
    small_list = [G[n] for n in small_names] + [loss_cols]
    spack, ssizes = _pack(small_list, total_mult=2 * SUBLANES * LANES)
    sred = _allreduce_small(spack, "small_allreduce")
    sfull = _unpack(sred, [a.shape for a in small_list], ssizes)
    loss = 0.5 / D_MODEL * jnp.sum(sfull[-1])
    small_grads = {}
    for n, g in zip(small_names, sfull[:-1]):
        if n in _SMALL_SHARDED:
            wcols = W[n].shape[1]
            g = lax.dynamic_slice_in_dim(g, chip * wcols, wcols, axis=1)
        small_grads[n] = g.reshape(W[n].shape)

    grads, delta, new_m, new_v = {}, {}, {}, {}
    for n, own, other in zip(_BIG, own_half, other_half):
        grads[n], delta[n], new_m[n], new_v[n] = _adamw_halves(
            W[n], own, other, M[n], V[n], core_idx, f"adamw_{n}")
    s_names = [n for n in _WEIGHTS if n not in _BIG]
    as2d = lambda a: a.reshape(_shape2d(a.shape))
    outs = _adamw_many([as2d(W[n]) for n in s_names], [as2d(small_grads[n]) for n in s_names],
                       [as2d(M[n]) for n in s_names], [as2d(V[n]) for n in s_names], "adamw_small")
    k = len(s_names)
    for i, n in enumerate(s_names):
        grads[n] = small_grads[n]
        delta[n], new_m[n], new_v[n] = (outs[j * k + i].reshape(W[n].shape) for j in range(3))
    return loss, grad_x, grads, delta, new_m, new_v


def kernel(x, meta_tokens, l0_mix_pre_norm, l0_mix_post_norm, l0_w_in, l0_lru_conv_w, l0_lru_conv_b, l0_lru_w_a, l0_lru_b_a, l0_lru_w_x, l0_lru_b_x, l0_lru_lambda, l0_attn_sinks, l0_w_out, l0_ffn_pre_norm, l0_ffn_post_norm, l0_ffn_w_up, l0_ffn_conv_w, l0_ffn_conv_b, l0_ffn_w_down, l1_mix_pre_norm, l1_mix_post_norm, l1_w_in, l1_ssm_conv_w, l1_ssm_conv_b, l1_dt_bias, l1_a_log, l1_d_skip, l1_gate_norm, l1_w_out, l1_ffn_pre_norm, l1_ffn_post_norm, l1_ffn_w_up, l1_ffn_conv_w, l1_ffn_conv_b, l1_ffn_w_down, loss_target, m_meta_tokens, m_l0_mix_pre_norm, m_l0_mix_post_norm, m_l0_w_in, m_l0_lru_conv_w, m_l0_lru_conv_b, m_l0_lru_w_a, m_l0_lru_b_a, m_l0_lru_w_x, m_l0_lru_b_x, m_l0_lru_lambda, m_l0_attn_sinks, m_l0_w_out, m_l0_ffn_pre_norm, m_l0_ffn_post_norm, m_l0_ffn_w_up, m_l0_ffn_conv_w, m_l0_ffn_conv_b, m_l0_ffn_w_down, m_l1_mix_pre_norm, m_l1_mix_post_norm, m_l1_w_in, m_l1_ssm_conv_w, m_l1_ssm_conv_b, m_l1_dt_bias, m_l1_a_log, m_l1_d_skip, m_l1_gate_norm, m_l1_w_out, m_l1_ffn_pre_norm, m_l1_ffn_post_norm, m_l1_ffn_w_up, m_l1_ffn_conv_w, m_l1_ffn_conv_b, m_l1_ffn_w_down, v_meta_tokens, v_l0_mix_pre_norm, v_l0_mix_post_norm, v_l0_w_in, v_l0_lru_conv_w, v_l0_lru_conv_b, v_l0_lru_w_a, v_l0_lru_b_a, v_l0_lru_w_x, v_l0_lru_b_x, v_l0_lru_lambda, v_l0_attn_sinks, v_l0_w_out, v_l0_ffn_pre_norm, v_l0_ffn_post_norm, v_l0_ffn_w_up, v_l0_ffn_conv_w, v_l0_ffn_conv_b, v_l0_ffn_w_down, v_l1_mix_pre_norm, v_l1_mix_post_norm, v_l1_w_in, v_l1_ssm_conv_w, v_l1_ssm_conv_b, v_l1_dt_bias, v_l1_a_log, v_l1_d_skip, v_l1_gate_norm, v_l1_w_out, v_l1_ffn_pre_norm, v_l1_ffn_post_norm, v_l1_ffn_w_up, v_l1_ffn_conv_w, v_l1_ffn_conv_b, v_l1_ffn_w_down):
    args = locals()
    W = {n: args[n] for n in _WEIGHTS}
    M = {n: args["m_" + n] for n in _WEIGHTS}
    V = {n: args["v_" + n] for n in _WEIGHTS}
    loss, grad_x, grads, delta, new_m, new_v = _step(x[0], loss_target[0], W, M, V)
    return (loss, grad_x[None], *[grads[n] for n in _WEIGHTS], *[delta[n] for n in _WEIGHTS],
            *[new_m[n] for n in _WEIGHTS], *[new_v[n] for n in _WEIGHTS])
```

```python
import functools
import math

import jax
import jax.numpy as jnp
from jax import lax
from jax.experimental import pallas as pl
from jax.experimental.pallas import tpu as pltpu

F32 = jnp.float32
BF16 = jnp.bfloat16

D_MODEL = 1024
N_META = 16
BLOCK = 128
PAD = BLOCK - N_META
EPS = 1e-6
LRU_BLOCKS = 8
LRU_BS = 128
LRU_C = 8.0
N_Q_HEADS = 16
N_KV_HEADS = 2
HEAD_DIM = 64
Q_PER_KV = 8
WINDOW = 128
D_SSM = 2048
SSD_HEADS = 32
SSD_GROUPS = 8
SSD_HPG = 4
SSD_P = 64
SSD_N = 128
D_FF = 2816
NEG = -1e30
LANES = 128
SUBLANES = 8
_VMEM_LIMIT_WIDE = 62 * 1024 * 1024

ADAM_LR = 0.001
ADAM_B1 = 0.9
ADAM_B2 = 0.999
ADAM_EPS = 1e-08
ADAM_WD = 0.01
ADAM_STEP = 10

MESH = pl.DeviceIdType.MESH
N_CHIPS = 4
N_DEV = 8


def _pick(n, cands):
    for c in cands:
        if n % c == 0:
            return c
    raise ValueError(f"no tile for {n} in {cands}")


def _col_tile(n, limit=1792):
    best = None
    for t in range(LANES, min(n, limit) + 1, LANES):
        if n % t == 0:
            best = t
    if best is None:
        raise ValueError(f"no lane tile for {n}")
    return best


def _sigmoid(x):
    return 0.5 + 0.5 * jnp.tanh(0.5 * x)


def _log1p(e):
    u = 1.0 + e
    return jnp.where(u == 1.0, e, jnp.log(u) * (e / jnp.where(u == 1.0, 1.0, u - 1.0)))


def _softplus(x):
    return jnp.maximum(x, 0.0) + _log1p(jnp.exp(-jnp.abs(x)))


def _neg_expm1(x):
    poly = x * (1.0 + x * (0.5 + x * (1.0 / 6.0 + x * (1.0 / 24.0 + x * (1.0 / 120.0)))))
    return -jnp.where(x > -0.05, poly, jnp.exp(x) - 1.0)


_GELU_C = math.sqrt(2.0 / math.pi)


def _gelu(x):
    u = 0.5 + 0.5 * jnp.tanh(x * (_GELU_C + (_GELU_C * 0.044715) * (x * x)))
    return x * u


def _gelu_and_grad(x):
    x2 = x * x
    u = 0.5 + 0.5 * jnp.tanh(x * (_GELU_C + (_GELU_C * 0.044715) * x2))
    g = x * u
    dg = u * (1.0 + (x - g) * (2.0 * _GELU_C + (6.0 * 0.044715 * _GELU_C) * x2))
    return g, dg


def _silu_and_grad(x):
    s = _sigmoid(x)
    return x * s, s * (1.0 + x * (1.0 - s))


def _dot(a, b):
    return jnp.dot(a, b, preferred_element_type=F32)


def _dot_nt(a, b):
    return lax.dot_general(a, b, (((1,), (1,)), ((), ())), preferred_element_type=F32)


def _dot_tn(a, b):
    return lax.dot_general(a, b, (((0,), (0,)), ((), ())), preferred_element_type=F32)


def _row_iota(t):
    return lax.broadcasted_iota(jnp.int32, (t, 1), 0)


def _scan_fwd(a, u, t):
    row = _row_iota(t)
    d = 1
    while d < t:
        m = row >= d
        u_sh = jnp.where(m, pltpu.roll(u, d, 0), 0.0)
        a_sh = jnp.where(m, pltpu.roll(a, d, 0), 1.0)
        u = u + a * u_sh
        a = a * a_sh
        d *= 2
    return a, u


def _scan_rev(c, x, t):
    row = _row_iota(t)
    d = 1
    while d < t:
        m = row < t - d
        x_sh = jnp.where(m, pltpu.roll(x, t - d, 0), 0.0)
        c_sh = jnp.where(m, pltpu.roll(c, t - d, 0), 1.0)
        x = x + c * x_sh
        c = c * c_sh
        d *= 2
    return c, x


def _cumsum_rows(x, t):
    row = _row_iota(t)
    d = 1
    while d < t:
        x = x + jnp.where(row >= d, pltpu.roll(x, d, 0), 0.0)
        d *= 2
    return x


def _rev_cumsum_rows(x, t):
    row = _row_iota(t)
    d = 1
    while d < t:
        x = x + jnp.where(row < t - d, pltpu.roll(x, t - d, 0), 0.0)
        d *= 2
    return x


def _rms_bwd(x, g, dy):
    rs = lax.rsqrt(jnp.mean(x * x, axis=-1, keepdims=True) + EPS)
    gy = dy * g
    dx = rs * gy - x * (rs * rs * rs) * jnp.mean(x * gy, axis=-1, keepdims=True)
    return dx, dy * x * rs


def _mm_nn(a, w, out_dtype, name):
    parts = a if isinstance(a, (tuple, list)) else (a,)
    m = parts[0].shape[0]
    k, n = w.shape
    tm = _pick(m, (640, 512, 256, 128))
    tn = _col_tile(n)
    offs = [sum(p.shape[1] for p in parts[:i]) for i in range(len(parts))]

    def body(*refs):
        w_ref, o_ref = refs[len(parts)], refs[len(parts) + 1]
        acc = None
        for a_ref, p, off in zip(refs, parts, offs):
            t = _dot(a_ref[...].astype(BF16), w_ref[off:off + p.shape[1], :])
            acc = t if acc is None else acc + t
        o_ref[...] = acc.astype(o_ref.dtype)

    return pl.pallas_call(
        body, name=name, grid=(n // tn, m // tm),
        in_specs=[pl.BlockSpec((tm, p.shape[1]), lambda j, i: (i, 0)) for p in parts]
        + [pl.BlockSpec((k, tn), lambda j, i: (0, j))],
        out_specs=pl.BlockSpec((tm, tn), lambda j, i: (i, j)),
        out_shape=jax.ShapeDtypeStruct((m, n), out_dtype),
        compiler_params=pltpu.CompilerParams(dimension_semantics=("parallel", "parallel")),
    )(*parts, w)


def _mm_nt(dy, w, name, out_dtype=F32):
    m, n = dy.shape
    k = w.shape[0]
    wide = n > 3328
    tm = _pick(m, (320, 256, 128)) if wide else _pick(m, (640, 512, 256, 128))
    tk = _col_tile(k, 512 if wide else 1408)

    def body(dy_ref, w_ref, o_ref):
        o_ref[...] = _dot_nt(dy_ref[...].astype(BF16), w_ref[...]).astype(o_ref.dtype)

    return pl.pallas_call(
        body, name=name, grid=(k // tk, m // tm),
        in_specs=[pl.BlockSpec((tm, n), lambda j, i: (i, 0)),
                  pl.BlockSpec((tk, n), lambda j, i: (j, 0))],
        out_specs=pl.BlockSpec((tm, tk), lambda j, i: (i, j)),
        out_shape=jax.ShapeDtypeStruct((m, k), out_dtype),
        compiler_params=pltpu.CompilerParams(dimension_semantics=("parallel", "parallel")),
    )(dy, w)


def _mm_tn(a, dy, name):
    m, k = a.shape
    n = dy.shape[1]
    tm = _pick(m, (640, 512, 256, 128))
    tk = _col_tile(k, 1408)
    tn = _col_tile(n, 1664)
    nsteps = m // tm

    def body(a_ref, dy_ref, o_ref, acc):
        @pl.when(pl.program_id(2) == 0)
        def _():
            acc[...] = jnp.zeros_like(acc)

        acc[...] += _dot_tn(a_ref[...].astype(BF16), dy_ref[...].astype(BF16))

        @pl.when(pl.program_id(2) == nsteps - 1)
        def _():
            o_ref[...] = acc[...].astype(o_ref.dtype)

    return pl.pallas_call(
        body, name=name, grid=(k // tk, n // tn, nsteps),
        in_specs=[pl.BlockSpec((tm, tk), lambda kk, j, i: (i, kk)),
                  pl.BlockSpec((tm, tn), lambda kk, j, i: (i, j))],
        out_specs=pl.BlockSpec((tk, tn), lambda kk, j, i: (kk, j)),
        out_shape=jax.ShapeDtypeStruct((k, n), BF16),
        scratch_shapes=[pltpu.VMEM((tk, tn), F32)],
        compiler_params=pltpu.CompilerParams(
            dimension_semantics=("parallel", "parallel", "arbitrary")),
    )(a, dy)


def _mm_nn_sh(a, w4, n_out, name):
    m, k = a.shape
    s, _, n = w4.shape
    tm = _pick(m, (320, 256, 128))

    def body(a_ref, w_ref, o_ref):
        av = a_ref[...].astype(BF16)
        for j in range(s):
            o_ref[:, j * n:(j + 1) * n] = _dot(av, w_ref[j])
        if n_out > s * n:
            o_ref[:, s * n:] = jnp.zeros((tm, n_out - s * n), F32)

    return pl.pallas_call(
        body, name=name, grid=(m // tm,),
        in_specs=[pl.BlockSpec((tm, k), lambda i: (i, 0)),
                  pl.BlockSpec((s, k, n), lambda i: (0, 0, 0))],
        out_specs=pl.BlockSpec((tm, n_out), lambda i: (i, 0)),
        out_shape=jax.ShapeDtypeStruct((m, n_out), F32),
        compiler_params=pltpu.CompilerParams(dimension_semantics=("parallel",)),
    )(a, w4)


def _mm_nt_sh(dy, w4, name, plan=None):
    dys = dy if isinstance(dy, (tuple, list)) else (dy,)
    m = dys[0].shape[0]
    s, k, n = w4.shape
    tm = _pick(m, (640, 512, 256, 128))
    tk = _col_tile(k, 1024)
    where = _shard_columns(dys, s, n)
    p_in, p_shapes, p_out, p_scr = _plan_parts(plan)
    nd, nj, ni = len(dys), k // tk, m // tm

    def body(*refs):
        w_ref = refs[nd]
        cins = refs[nd + 1:nd + 1 + len(p_in)]
        o_ref = refs[nd + 1 + len(p_in)]
        couts = refs[nd + 2 + len(p_in):nd + 2 + len(p_in) + len(p_out)]
        sems = refs[nd + 2 + len(p_in) + len(p_out):]
        step = pl.program_id(0) * ni + pl.program_id(1)
        if plan is not None:
            @pl.when(step == 0)
            def _():
                plan.start(cins, couts, sems)

        acc = None
        for j, (p, c0) in enumerate(where):
            t = _dot_nt(refs[p][:, c0:c0 + n].astype(BF16), w_ref[j])
            acc = t if acc is None else acc + t
        o_ref[...] = acc
        if plan is not None:
            @pl.when(step == nj * ni - 1)
            def _():
                plan.wait(cins, couts, sems)

    sem = ("arbitrary", "arbitrary") if plan is not None else ("parallel", "parallel")
    res = pl.pallas_call(
        body, name=name, grid=(nj, ni),
        in_specs=[pl.BlockSpec((tm, d.shape[1]), lambda j, i: (i, 0)) for d in dys]
        + [pl.BlockSpec((s, tk, n), lambda j, i: (0, j, 0))] + p_in,
        out_specs=[pl.BlockSpec((tm, tk), lambda j, i: (i, j))] + p_out,
        out_shape=[jax.ShapeDtypeStruct((m, k), F32)] + p_shapes,
        scratch_shapes=p_scr,
        compiler_params=pltpu.CompilerParams(dimension_semantics=sem),
    )(*dys, w4, *(plan.ins if plan is not None else []))
    return res[0] if plan is None else (res[0], res[1:])


def _shard_columns(dys, s, n):
    where = []
    for p, d in enumerate(dys):
        where += [(p, c * n) for c in range(d.shape[1] // n)]
    assert len(where) >= s
    return where[:s]


def _mm_tn_sh(a, dy, n, name):
    dys = dy if isinstance(dy, (tuple, list)) else (dy,)
    m, k = a.shape
    s = N_CHIPS
    tm = _pick(m, (640, 512, 256, 128))
    tk = _col_tile(k, 512)
    nsteps = m // tm
    where = _shard_columns(dys, s, n)

    def body(*refs):
        a_ref, o_ref, acc = refs[0], refs[len(dys) + 1], refs[len(dys) + 2]

        @pl.when(pl.program_id(1) == 0)
        def _():
            acc[...] = jnp.zeros_like(acc)

        av = a_ref[...].astype(BF16)
        for j, (p, c0) in enumerate(where):
            acc[j] += _dot_tn(av, refs[1 + p][:, c0:c0 + n].astype(BF16))

        @pl.when(pl.program_id(1) == nsteps - 1)
        def _():
            o_ref[...] = acc[...].astype(o_ref.dtype)

    return pl.pallas_call(
        body, name=name, grid=(k // tk, nsteps),
        in_specs=[pl.BlockSpec((tm, tk), lambda kk, i: (i, kk))]
        + [pl.BlockSpec((tm, d.shape[1]), lambda kk, i: (i, 0)) for d in dys],
        out_specs=pl.BlockSpec((s, tk, n), lambda kk, i: (0, kk, 0)),
        out_shape=jax.ShapeDtypeStruct((s, k, n), BF16),
        scratch_shapes=[pltpu.VMEM((s, tk, n), F32)],
        compiler_params=pltpu.CompilerParams(dimension_semantics=("parallel", "arbitrary"),
                                             vmem_limit_bytes=_VMEM_LIMIT_WIDE),
    )(a, *dys)


def _rowcall(name, body, lp, tm, rows=(), prevs=(), vecs=(), outs=(), accs=(), scratch=(),
             reverse=False, seq=False, plan=None):
    p_in, p_shapes, p_out, p_scr = _plan_parts(plan)
    nt = lp // tm
    hb = tm // SUBLANES

    def ri(i):
        return nt - 1 - i if reverse else i

    in_specs, args = [], []
    for arr, w, cb in rows:
        in_specs.append(pl.BlockSpec((tm, w), lambda i, cb=cb: (ri(i), cb)))
        args.append(arr)
    for arr, w, cb in prevs:
        in_specs.append(pl.BlockSpec((SUBLANES, w), lambda i, cb=cb: (jnp.maximum(ri(i) * hb - 1, 0), cb)))
        args.append(arr)
    for arr in vecs:
        in_specs.append(pl.BlockSpec(arr.shape, lambda i, nd=arr.ndim: (0,) * nd))
        args.append(arr)
    out_shape, out_specs = [], []
    for w, dt in outs:
        out_shape.append(jax.ShapeDtypeStruct((lp, w), dt))
        out_specs.append(pl.BlockSpec((tm, w), lambda i: (ri(i), 0)))
    for shp, dt in accs:
        out_shape.append(jax.ShapeDtypeStruct(shp, dt))
        out_specs.append(pl.BlockSpec(shp, lambda i, nd=len(shp): (0,) * nd))

    n_in, n_out, n_scr = len(args), len(out_shape), len(scratch)

    def kern(*refs):
        i = pl.program_id(0)
        own = (refs[:n_in] + refs[n_in + len(p_in):n_in + len(p_in) + n_out]
               + refs[n_in + len(p_in) + n_out + len(p_out):n_in + len(p_in) + n_out + len(p_out) + n_scr])
        cins = refs[n_in:n_in + len(p_in)]
        couts = refs[n_in + len(p_in) + n_out:n_in + len(p_in) + n_out + len(p_out)]
        sems = refs[n_in + len(p_in) + n_out + len(p_out) + n_scr:]
        if plan is not None:
            @pl.when(i == 0)
            def _():
                plan.start(cins, couts, sems)

        body(ri(i), i == 0, *own)
        if plan is not None:
            @pl.when(i == nt - 1)
            def _():
                plan.wait(cins, couts, sems)

    sem = ("arbitrary",) if (seq or accs or plan is not None) else ("parallel",)
    res = pl.pallas_call(
        kern, name=name, grid=(nt,), in_specs=in_specs + p_in, out_specs=out_specs + p_out,
        out_shape=out_shape + p_shapes, scratch_shapes=list(scratch) + p_scr,
        compiler_params=pltpu.CompilerParams(dimension_semantics=sem),
    )(*args, *(plan.ins if plan is not None else []))
    return res if plan is None else (res[:n_out], res[n_out:])


def _acc_add(first, ref, val):
    @pl.when(first)
    def _():
        ref[...] = jnp.zeros_like(ref)

    ref[...] += val


def _real_rows(r, tm):
    return (r * tm + _row_iota(tm)) >= PAD


def _rmsnorm_fwd(h, g, name):
    lp, d = h.shape
    tm = _pick(lp, (640, 512, 256, 128))

    def body(r, first, h_ref, g_ref, u_ref):
        x = h_ref[...]
        rs = lax.rsqrt(jnp.mean(x * x, axis=-1, keepdims=True) + EPS)
        u_ref[...] = (x * rs * g_ref[...]).astype(u_ref.dtype)

    return _rowcall(name, body, lp, tm, rows=[(h, d, 0)], vecs=[g], outs=[(d, BF16)])[0]


def _postnorm_res_fwd(h, o, g, name, next_pre=None):
    lp, d = h.shape
    tm = _pick(lp, (640, 512, 256, 128))

    def body(r, first, h_ref, o_ref, g_ref, *rest):
        x = o_ref[...]
        rs = lax.rsqrt(jnp.mean(x * x, axis=-1, keepdims=True) + EPS)
        hn = jnp.where(_real_rows(r, tm), h_ref[...] + x * rs * g_ref[...], 0.0)
        rest[-2 if next_pre is not None else -1][...] = hn
        if next_pre is not None:
            rs2 = lax.rsqrt(jnp.mean(hn * hn, axis=-1, keepdims=True) + EPS)
            rest[-1][...] = (hn * rs2 * rest[0][...]).astype(BF16)

    if next_pre is None:
        return _rowcall(name, body, lp, tm, rows=[(h, d, 0), (o, d, 0)], vecs=[g], outs=[(d, F32)])[0]
    return _rowcall(name, body, lp, tm, rows=[(h, d, 0), (o, d, 0)], vecs=[g, next_pre],
                    outs=[(d, F32), (d, BF16)])


def _postnorm_bwd(o, g, dh, name):
    lp, d = o.shape
    tm = _pick(lp, (640, 512, 256, 128))

    def body(r, first, o_ref, dh_ref, g_ref, do_ref, dg_ref):
        dx, dgt = _rms_bwd(o_ref[...], g_ref[...], dh_ref[...])
        do_ref[...] = dx.astype(do_ref.dtype)
        _acc_add(first, dg_ref, jnp.sum(dgt, axis=0, keepdims=True))

    return _rowcall(name, body, lp, tm, rows=[(o, d, 0), (dh, d, 0)], vecs=[g],
                    outs=[(d, BF16)], accs=[((1, d), F32)])


def _prenorm_bwd(h, g, du, dh_res, name):
    lp, d = h.shape
    tm = _pick(lp, (640, 512, 256, 128))

    def body(r, first, h_ref, du_ref, dres_ref, g_ref, dh_ref, dg_ref):
        dx, dgt = _rms_bwd(h_ref[...], g_ref[...], du_ref[...])
        dh_ref[...] = jnp.where(_real_rows(r, tm), dres_ref[...] + dx, 0.0)
        _acc_add(first, dg_ref, jnp.sum(dgt, axis=0, keepdims=True))

    return _rowcall(name, body, lp, tm, rows=[(h, d, 0), (du, d, 0), (dh_res, d, 0)], vecs=[g],
                    outs=[(d, F32)], accs=[((1, d), F32)])


def _loss_fwd_bwd(h, tgt, name):
    lp, d = h.shape
    tm = _pick(lp, (640, 512, 256, 128))

    def body(r, first, h_ref, t_ref, dh_ref, ls_ref):
        tok = (r * tm + _row_iota(tm)) >= BLOCK
        e = jnp.where(tok, h_ref[...] - t_ref[...], 0.0)
        dh_ref[...] = e * (1.0 / d)
        _acc_add(first, ls_ref, jnp.sum(e * e, axis=0, keepdims=True))

    return _rowcall(name, body, lp, tm, rows=[(h, d, 0), (tgt, d, 0)],
                    outs=[(d, F32)], accs=[((1, d), F32)])


def _conv_tiles(lp, width):
    wc = _col_tile(width, 1408)
    tm = _pick(lp, (320, 256, 128))
    return tm, wc


def _conv_fwd(x, col_off, width, w, b, name):
    lp = x.shape[0]
    kk = w.shape[0]
    tm, wc = _conv_tiles(lp, width)
    offb = col_off // wc
    assert col_off % wc == 0
    hb = tm // SUBLANES

    def body(x_ref, xp_ref, w_ref, b_ref, y_ref):
        i = pl.program_id(1)
        xv = x_ref[...]
        halo = jnp.where(i > 0, xp_ref[...], 0.0)
        xx = jnp.concatenate([halo, xv], axis=0)
        acc = b_ref[...] + w_ref[kk - 1:kk, :] * xv
        for j in range(1, kk):
            acc = acc + w_ref[kk - 1 - j:kk - j, :] * pltpu.roll(xx, j, 0)[SUBLANES:, :]
        y_ref[...] = acc

    return pl.pallas_call(
        body, name=name, grid=(width // wc, lp // tm),
        in_specs=[pl.BlockSpec((tm, wc), lambda j, i: (i, offb + j)),
                  pl.BlockSpec((SUBLANES, wc), lambda j, i: (jnp.maximum(i * hb - 1, 0), offb + j)),
                  pl.BlockSpec((kk, wc), lambda j, i: (0, j)),
                  pl.BlockSpec((1, wc), lambda j, i: (0, j))],
        out_specs=pl.BlockSpec((tm, wc), lambda j, i: (i, j)),
        out_shape=jax.ShapeDtypeStruct((lp, width), F32),
        compiler_params=pltpu.CompilerParams(dimension_semantics=("parallel", "parallel")),
    )(x, x, w, b)


def _conv_bwd(x, col_off, width, dy, w, name, w_col_off=0):
    lp = x.shape[0]
    kk = w.shape[0]
    tm, wc = _conv_tiles(lp, width)
    offb = col_off // wc
    woffb = w_col_off // wc
    assert col_off % wc == 0 and w_col_off % wc == 0
    hrows = SUBLANES * (4 // dy.dtype.itemsize)
    ext = tm + hrows

    def body(x_ref, dy_ref, dn_ref, w_ref, dx_ref, dw_ref, db_ref):
        i = pl.program_id(1)
        last = pl.num_programs(1) - 1
        xv = x_ref[...]
        dyv = dy_ref[...].astype(F32)
        dd = jnp.concatenate([dyv, jnp.where(i < last, dn_ref[...].astype(F32), 0.0)], axis=0)
        dx = w_ref[kk - 1:kk, :] * dyv
        rows = [jnp.sum(dyv * xv, axis=0, keepdims=True)]
        for m in range(1, kk):
            ahead = pltpu.roll(dd, ext - m, 0)[:tm, :]
            dx = dx + w_ref[kk - 1 - m:kk - m, :] * ahead
            rows.append(jnp.sum(ahead * xv, axis=0, keepdims=True))
        dx_ref[...] = dx.astype(dx_ref.dtype)
        dwp = jnp.concatenate(rows[::-1] + [jnp.zeros((SUBLANES - kk, wc), F32)], axis=0)

        @pl.when(i == 0)
        def _():
            dw_ref[...] = jnp.zeros_like(dw_ref)
            db_ref[...] = jnp.zeros_like(db_ref)

        dw_ref[...] += dwp
        db_ref[...] += jnp.sum(dyv, axis=0, keepdims=True)

    return pl.pallas_call(
        body, name=name, grid=(width // wc, lp // tm),
        in_specs=[pl.BlockSpec((tm, wc), lambda j, i: (i, offb + j)),
                  pl.BlockSpec((tm, wc), lambda j, i: (i, j)),
                  pl.BlockSpec((hrows, wc), lambda j, i: (jnp.minimum((i + 1) * (tm // hrows), lp // hrows - 1), j)),
                  pl.BlockSpec((kk, wc), lambda j, i: (0, woffb + j))],
        out_specs=[pl.BlockSpec((tm, wc), lambda j, i: (i, j)),
                   pl.BlockSpec((SUBLANES, wc), lambda j, i: (0, j)),
                   pl.BlockSpec((1, wc), lambda j, i: (0, j))],
        out_shape=[jax.ShapeDtypeStruct((lp, width), BF16),
                   jax.ShapeDtypeStruct((SUBLANES, width), F32),
                   jax.ShapeDtypeStruct((1, width), F32)],
        compiler_params=pltpu.CompilerParams(dimension_semantics=("parallel", "arbitrary")),
    )(x, dy, dy, w)


_FFN_K = 3
_FFN_WC = 1408


def _conv3_ext(x_ext, w_ref, b_ref):
    return (b_ref[...] + w_ref[2:3, :] * x_ext + w_ref[1:2, :] * pltpu.roll(x_ext, 1, 0)
            + w_ref[0:1, :] * pltpu.roll(x_ext, 2, 0))


def _ffn_convact_fwd(hp, cw, cb, name, plan=None):
    lp = hp.shape[0]
    tm = _pick(lp, (320, 256, 128))
    wc = _FFN_WC
    nj = D_FF // wc
    ni = lp // tm
    hb = tm // SUBLANES
    p_in, p_shapes, p_out, p_scr = _plan_parts(plan)

    def body(*refs):
        g_ref, gp_ref, u_ref, up_ref, wg_ref, wu_ref, bg_ref, bu_ref = refs[:8]
        cins = refs[8:8 + len(p_in)]
        a_ref, hg_ref, hu_ref = refs[8 + len(p_in):11 + len(p_in)]
        couts = refs[11 + len(p_in):11 + len(p_in) + len(p_out)]
        sems = refs[11 + len(p_in) + len(p_out):]
        i = pl.program_id(1)
        step = pl.program_id(0) * ni + i
        if plan is not None:
            @pl.when(step == 0)
            def _():
                plan.start(cins, couts, sems)

        def conv(x_ref, p_ref, w_ref, b_ref):
            x_ext = jnp.concatenate([jnp.where(i > 0, p_ref[...], 0.0), x_ref[...]], axis=0)
            return _conv3_ext(x_ext, w_ref, b_ref)[SUBLANES:, :]

        hg = conv(g_ref, gp_ref, wg_ref, bg_ref)
        hu = conv(u_ref, up_ref, wu_ref, bu_ref)
        a_ref[...] = (_gelu(hg) * hu).astype(a_ref.dtype)
        hg_ref[...] = hg.astype(hg_ref.dtype)
        hu_ref[...] = hu.astype(hu_ref.dtype)
        if plan is not None:
            @pl.when(step == nj * ni - 1)
            def _():
                plan.wait(cins, couts, sems)

    tile = lambda off: pl.BlockSpec((tm, wc), lambda j, i: (i, off + j))
    prev = lambda off: pl.BlockSpec((SUBLANES, wc), lambda j, i: (jnp.maximum(i * hb - 1, 0), off + j))
    vec = lambda rows, off: pl.BlockSpec((rows, wc), lambda j, i: (0, off + j))
    sem = ("arbitrary", "arbitrary") if plan is not None else ("parallel", "parallel")
    res = pl.pallas_call(
        body, name=name, grid=(nj, ni),
        in_specs=[tile(0), prev(0), tile(nj), prev(nj), vec(_FFN_K, 0), vec(_FFN_K, nj), vec(1, 0),
                  vec(1, nj)] + p_in,
        out_specs=[tile(0)] * 3 + p_out,
        out_shape=[jax.ShapeDtypeStruct((lp, D_FF), BF16)] * 3 + p_shapes,
        scratch_shapes=p_scr,
        compiler_params=pltpu.CompilerParams(dimension_semantics=sem),
    )(hp, hp, hp, hp, cw, cw, cb, cb, *(plan.ins if plan is not None else []))
    return res if plan is None else (res[:3], res[3:])


def _ffn_act_bwd(hg, hu, dact, name):
    lp = hg.shape[0]
    tm = _pick(lp, (320, 256, 128))

    def body(r, first, g_ref, u_ref, da_ref, dg_ref, du_ref):
        gl, dgl = _gelu_and_grad(g_ref[...].astype(F32))
        da = da_ref[...].astype(F32)
        dg_ref[...] = (da * u_ref[...].astype(F32) * dgl).astype(dg_ref.dtype)
        du_ref[...] = (da * gl).astype(du_ref.dtype)

    return _rowcall(name, body, lp, tm, rows=[(hg, D_FF, 0), (hu, D_FF, 0), (dact, D_FF, 0)],
                    outs=[(D_FF, BF16), (D_FF, BF16)])


def _lru_gates(x, wa_ref, wx_ref, ba, bx, lam):
    xb = x.astype(BF16)
    za, zx = [], []
    for n in range(LRU_BLOCKS):
        xs = xb[:, n * LRU_BS:(n + 1) * LRU_BS]
        za.append(_dot(xs, wa_ref[n]))
        zx.append(_dot(xs, wx_ref[n]))
    r = _sigmoid(jnp.concatenate(za, axis=1) + ba)
    ig = _sigmoid(jnp.concatenate(zx, axis=1) + bx)
    sp = _softplus(-lam)
    log_a = -LRU_C * r * sp
    a = jnp.exp(log_a)
    om = _neg_expm1(2.0 * log_a)
    mult = jnp.sqrt(om)
    return xb, r, ig, sp, a, om, mult


def _lru_fwd(proj, xrc, wa, wx, ba, bx, lam, name, plan=None):
    lp, d = xrc.shape
    tm = BLOCK

    def body(r_idx, first, gate_ref, x_ref, wa_ref, wx_ref, ba_ref, bx_ref, lam_ref,
             y_ref, h_ref, carry):
        @pl.when(first)
        def _():
            carry[...] = jnp.zeros_like(carry)

        x = x_ref[...]
        _, _, ig, _, a, _, mult = _lru_gates(x, wa_ref, wx_ref, ba_ref[...], bx_ref[...], lam_ref[...])
        u = jnp.where(_real_rows(r_idx, tm), mult * ig * x, 0.0)
        acum, hloc = _scan_fwd(a, u, tm)
        h = hloc + acum * carry[0:1, :]
        h_ref[...] = h
        carry[0:1, :] = h[tm - 1:tm, :]
        y_ref[...] = (_gelu(gate_ref[...]) * h).astype(y_ref.dtype)

    return _rowcall(name, body, lp, tm, rows=[(proj, d, 0), (xrc, d, 0)],
                    vecs=[wa, wx, ba, bx, lam], outs=[(d, BF16), (d, F32)],
                    scratch=[pltpu.VMEM((SUBLANES, d), F32)], seq=True, plan=plan)


def _lru_bwd(proj, xrc, hl, dmix, wa, wx, ba, bx, lam, name, plan=None):
    lp, d = xrc.shape
    tm = BLOCK

    def body(r_idx, first, gate_ref, x_ref, h_ref, dy_ref, hp_ref, wa_ref, wx_ref, ba_ref, bx_ref,
             lam_ref, dgate_ref, dx_ref, dwa_ref, dwx_ref, dba_ref, dbx_ref, dlam_ref, carry):
        @pl.when(first)
        def _():
            carry[...] = jnp.zeros_like(carry)
            dwa_ref[...] = jnp.zeros_like(dwa_ref)
            dwx_ref[...] = jnp.zeros_like(dwx_ref)
            dba_ref[...] = jnp.zeros_like(dba_ref)
            dbx_ref[...] = jnp.zeros_like(dbx_ref)
            dlam_ref[...] = jnp.zeros_like(dlam_ref)

        x = x_ref[...]
        lam = lam_ref[...]
        xb, r, ig, sp, a, om, mult = _lru_gates(x, wa_ref, wx_ref, ba_ref[...], bx_ref[...], lam)
        h = h_ref[...]
        dy = dy_ref[...]
        gl, dgl = _gelu_and_grad(gate_ref[...])
        dgate_ref[...] = (dy * h * dgl).astype(dgate_ref.dtype)
        row = _row_iota(tm)
        lastrow = row == tm - 1
        xg = dy * gl + jnp.where(lastrow, carry[0:1, :], 0.0)
        c = jnp.where(lastrow, 1.0, pltpu.roll(a, tm - 1, 0))
        _, g = _scan_rev(c, xg, tm)
        carry[0:1, :] = a[0:1, :] * g[0:1, :]
        hprev_in = jnp.where(r_idx > 0, hp_ref[SUBLANES - 1:SUBLANES, :], 0.0)
        hprev = jnp.where(row == 0, hprev_in, pltpu.roll(h, 1, 0))
        du = jnp.where(_real_rows(r_idx, tm), g, 0.0)
        da = g * hprev
        dmult = du * ig * x
        dig = du * mult * x
        dxv = du * mult * ig
        e2 = 1.0 - om
        dlog_a = da * a - dmult * e2 / mult
        dr = dlog_a * (-LRU_C) * sp
        dsp = jnp.sum(dlog_a * (-LRU_C) * r, axis=0, keepdims=True)
        dlam_ref[...] += -dsp * _sigmoid(-lam)
        dza = dr * r * (1.0 - r)
        dzx = dig * ig * (1.0 - ig)
        dba_ref[...] += jnp.sum(dza, axis=0, keepdims=True)
        dbx_ref[...] += jnp.sum(dzx, axis=0, keepdims=True)
        dzab = dza.astype(BF16)
        dzxb = dzx.astype(BF16)
        parts = []
        for n in range(LRU_BLOCKS):
            sl = slice(n * LRU_BS, (n + 1) * LRU_BS)
            dwa_ref[n] += _dot_tn(xb[:, sl], dzab[:, sl])
            dwx_ref[n] += _dot_tn(xb[:, sl], dzxb[:, sl])
            parts.append(_dot_nt(dzab[:, sl], wa_ref[n]) + _dot_nt(dzxb[:, sl], wx_ref[n]))
        dx_ref[...] = dxv + jnp.concatenate(parts, axis=1)

    return _rowcall(name, body, lp, tm,
                    rows=[(proj, d, 0), (xrc, d, 0), (hl, d, 0), (dmix, d, 0)],
                    prevs=[(hl, d, 0)], vecs=[wa, wx, ba, bx, lam],
                    outs=[(d, BF16), (d, F32)],
                    accs=[((LRU_BLOCKS, LRU_BS, LRU_BS), F32), ((LRU_BLOCKS, LRU_BS, LRU_BS), F32),
                          ((1, d), F32), ((1, d), F32), ((1, d), F32)],
                    scratch=[pltpu.VMEM((SUBLANES, d), F32)], reverse=True, seq=True, plan=plan)


_SLOPES = [2.0 ** (-8.0 * (h + 1) / N_Q_HEADS) for h in range(N_Q_HEADS)]
_QK_SCALE = HEAD_DIM ** -0.5
_QCOL = 2 * D_MODEL // D_MODEL
_KCOL = (3 * D_MODEL) // LANES
_VCOL = _KCOL + 1


def _attn_masks(n):
    start = pl.multiple_of(jnp.maximum(n - 1, 0) * BLOCK, BLOCK)
    qi = n * BLOCK + lax.broadcasted_iota(jnp.int32, (BLOCK, 2 * BLOCK), 0)
    kj = start + lax.broadcasted_iota(jnp.int32, (BLOCK, 2 * BLOCK), 1)
    dist = qi - kj
    ok = (kj >= BLOCK) & (dist >= 0) & (dist < WINDOW)
    dm = (n * BLOCK - PAD + lax.broadcasted_iota(jnp.int32, (BLOCK, N_META), 0)
          - lax.broadcasted_iota(jnp.int32, (BLOCK, N_META), 1))
    okm = dm >= 0
    return start, ok, dist.astype(F32), okm, jnp.minimum(dm, WINDOW).astype(F32)


def _group_rows(ref, g, scale=None):
    x = jnp.concatenate(
        [ref[:, (g * Q_PER_KV + hh) * HEAD_DIM:(g * Q_PER_KV + hh + 1) * HEAD_DIM] for hh in range(Q_PER_KV)],
        axis=0)
    return (x if scale is None else x * scale).astype(BF16)


def _attn_probs(s, sm, sink_ref, g, ok, distf, okm, dmf):
    slope = jnp.stack([jnp.full((1, 1), _SLOPES[g * Q_PER_KV + hh], F32) for hh in range(Q_PER_KV)])
    sink = jnp.stack([sink_ref[0:1, g * Q_PER_KV + hh:g * Q_PER_KV + hh + 1] for hh in range(Q_PER_KV)])
    s = s.reshape(Q_PER_KV, BLOCK, 2 * BLOCK)
    sm = sm.reshape(Q_PER_KV, BLOCK, N_META)
    s = jnp.where(ok[None], s - slope * distf[None], NEG)
    sm = jnp.where(okm[None], sm - slope * dmf[None], NEG)
    mx = jnp.maximum(jnp.maximum(jnp.max(s, axis=-1, keepdims=True),
                                 jnp.max(sm, axis=-1, keepdims=True)), sink)
    p = jnp.exp(s - mx)
    pm = jnp.exp(sm - mx)
    ps = jnp.exp(sink - mx)
    inv = 1.0 / (jnp.sum(p, axis=-1, keepdims=True) + jnp.sum(pm, axis=-1, keepdims=True) + ps)
    return p, pm, ps, inv


def _attn_fwd(proj, sinks, name, plan=None):
    lp = proj.shape[0]
    nblk = lp // BLOCK
    p_in, p_shapes, p_out, p_scr = _plan_parts(plan)

    def body(*refs):
        q_ref, k_ref, v_ref, sink_ref = refs[:4]
        cins = refs[4:4 + len(p_in)]
        o_ref = refs[4 + len(p_in)]
        couts = refs[5 + len(p_in):5 + len(p_in) + len(p_out)]
        sems = refs[5 + len(p_in) + len(p_out):]
        n = pl.program_id(0)
        if plan is not None:
            @pl.when(n == 0)
            def _():
                plan.start(cins, couts, sems)

        start, ok, distf, okm, dmf = _attn_masks(n)
        kb = k_ref[pl.ds(start, 2 * BLOCK), :].astype(BF16)
        vb = v_ref[pl.ds(start, 2 * BLOCK), :].astype(BF16)
        km = k_ref[PAD:BLOCK, :].astype(BF16)
        vm = v_ref[PAD:BLOCK, :].astype(BF16)
        rows = Q_PER_KV * BLOCK
        gsl = [slice(g * HEAD_DIM, (g + 1) * HEAD_DIM) for g in range(N_KV_HEADS)]
        raw = []
        for g in range(N_KV_HEADS):
            qg = _group_rows(q_ref, g, _QK_SCALE)
            raw.append((_dot_nt(qg, kb[:, gsl[g]]), _dot_nt(qg, km[:, gsl[g]])))
        for g in range(N_KV_HEADS):
            gs = gsl[g]
            p, pm, _, inv = _attn_probs(raw[g][0], raw[g][1], sink_ref, g, ok, distf, okm, dmf)
            o = (_dot(p.astype(BF16).reshape(rows, 2 * BLOCK), vb[:, gs])
                 + _dot(pm.astype(BF16).reshape(rows, N_META), vm[:, gs])) * inv.reshape(rows, 1)
            for hh in range(Q_PER_KV):
                h = g * Q_PER_KV + hh
                o_ref[:, h * HEAD_DIM:(h + 1) * HEAD_DIM] = o[hh * BLOCK:(hh + 1) * BLOCK, :].astype(o_ref.dtype)
        if plan is not None:
            @pl.when(n == nblk - 1)
            def _():
                plan.wait(cins, couts, sems)

    res = pl.pallas_call(
        body, name=name, grid=(nblk,),
        in_specs=[pl.BlockSpec((BLOCK, D_MODEL), lambda n: (n, _QCOL)),
                  pl.BlockSpec((lp, LANES), lambda n: (0, _KCOL)),
                  pl.BlockSpec((lp, LANES), lambda n: (0, _VCOL)),
                  pl.BlockSpec(sinks.shape, lambda n: (0, 0))] + p_in,
        out_specs=[pl.BlockSpec((BLOCK, D_MODEL), lambda n: (n, 0))] + p_out,
        out_shape=[jax.ShapeDtypeStruct((lp, D_MODEL), BF16)] + p_shapes,
        scratch_shapes=p_scr,
        compiler_params=pltpu.CompilerParams(dimension_semantics=("arbitrary",)),
    )(proj, proj, proj, sinks, *(plan.ins if plan is not None else []))
    return res[0], res[1:]


def _attn_bwd(proj, sinks, dmix, name, plan=None):
    lp = proj.shape[0]
    nblk = lp // BLOCK

    p_in, p_shapes, p_out, p_scr = _plan_parts(plan)

    def body(*refs):
        q_ref, k_ref, v_ref, sink_ref, dy_ref = refs[:5]
        cins = refs[5:5 + len(p_in)]
        dq_ref, dk_ref, dv_ref, ds_ref = refs[5 + len(p_in):9 + len(p_in)]
        couts = refs[9 + len(p_in):9 + len(p_in) + len(p_out)]
        sems = refs[9 + len(p_in) + len(p_out):]
        n = pl.program_id(0)

        @pl.when(n == 0)
        def _():
            dk_ref[...] = jnp.zeros_like(dk_ref)
            dv_ref[...] = jnp.zeros_like(dv_ref)
            ds_ref[...] = jnp.zeros_like(ds_ref)
            if plan is not None:
                plan.start(cins, couts, sems)

        start, ok, distf, okm, dmf = _attn_masks(n)
        kb = k_ref[pl.ds(start, 2 * BLOCK), :].astype(BF16)
        vb = v_ref[pl.ds(start, 2 * BLOCK), :].astype(BF16)
        km = k_ref[PAD:BLOCK, :].astype(BF16)
        vm = v_ref[PAD:BLOCK, :].astype(BF16)
        lane16 = lax.broadcasted_iota(jnp.int32, (1, N_Q_HEADS), 1)
        dsink = jnp.zeros((1, N_Q_HEADS), F32)
        rows = Q_PER_KV * BLOCK
        gsl = [slice(g * HEAD_DIM, (g + 1) * HEAD_DIM) for g in range(N_KV_HEADS)]
        pre = []
        for g in range(N_KV_HEADS):
            qg = _group_rows(q_ref, g, _QK_SCALE)
            dog = _group_rows(dy_ref, g)
            pre.append((qg, dog, _dot_nt(qg, kb[:, gsl[g]]), _dot_nt(qg, km[:, gsl[g]]),
                        _dot_nt(dog, vb[:, gsl[g]]), _dot_nt(dog, vm[:, gsl[g]])))
        for g in range(N_KV_HEADS):
            gs = gsl[g]
            qg, dog, s_raw, sm_raw, dp, dpm = pre[g]
            p, pm, ps, inv = _attn_probs(s_raw, sm_raw, sink_ref, g, ok, distf, okm, dmf)
            pn, pmn, psn = p * inv, pm * inv, ps * inv
            dp = dp.reshape(Q_PER_KV, BLOCK, 2 * BLOCK)
            dpm = dpm.reshape(Q_PER_KV, BLOCK, N_META)
            delta = (jnp.sum(pn * dp, axis=-1, keepdims=True)
                     + jnp.sum(pmn * dpm, axis=-1, keepdims=True))
            dsb = (pn * (dp - delta)).astype(BF16).reshape(rows, 2 * BLOCK)
            dsm = (pmn * (dpm - delta)).astype(BF16).reshape(rows, N_META)
            dsk = jnp.sum(psn * delta, axis=1, keepdims=True)
            for hh in range(Q_PER_KV):
                dsink = dsink - jnp.where(lane16 == g * Q_PER_KV + hh, dsk[hh], 0.0)
            dq = (_dot(dsb, kb[:, gs]) + _dot(dsm, km[:, gs])) * _QK_SCALE
            for hh in range(Q_PER_KV):
                h = g * Q_PER_KV + hh
                dq_ref[:, h * HEAD_DIM:(h + 1) * HEAD_DIM] = dq[hh * BLOCK:(hh + 1) * BLOCK, :].astype(dq_ref.dtype)
            pnb = pn.astype(BF16).reshape(rows, 2 * BLOCK)
            pmnb = pmn.astype(BF16).reshape(rows, N_META)
            dk_ref[pl.ds(start, 2 * BLOCK), gs] += _dot_tn(dsb, qg)
            dv_ref[pl.ds(start, 2 * BLOCK), gs] += _dot_tn(pnb, dog)
            dk_ref[PAD:BLOCK, gs] += _dot_tn(dsm, qg)
            dv_ref[PAD:BLOCK, gs] += _dot_tn(pmnb, dog)
        ds_ref[...] += dsink
        if plan is not None:
            @pl.when(n == nblk - 1)
            def _():
                plan.wait(cins, couts, sems)

    res = pl.pallas_call(
        body, name=name, grid=(nblk,),
        in_specs=[pl.BlockSpec((BLOCK, D_MODEL), lambda n: (n, _QCOL)),
                  pl.BlockSpec((lp, LANES), lambda n: (0, _KCOL)),
                  pl.BlockSpec((lp, LANES), lambda n: (0, _VCOL)),
                  pl.BlockSpec(sinks.shape, lambda n: (0, 0)),
                  pl.BlockSpec((BLOCK, D_MODEL), lambda n: (n, 1))] + p_in,
        out_specs=[pl.BlockSpec((BLOCK, D_MODEL), lambda n: (n, 0)),
                   pl.BlockSpec((lp, LANES), lambda n: (0, 0)),
                   pl.BlockSpec((lp, LANES), lambda n: (0, 0)),
                   pl.BlockSpec((1, N_Q_HEADS), lambda n: (0, 0))] + p_out,
        out_shape=[jax.ShapeDtypeStruct((lp, D_MODEL), BF16),
                   jax.ShapeDtypeStruct((lp, LANES), F32),
                   jax.ShapeDtypeStruct((lp, LANES), F32),
                   jax.ShapeDtypeStruct((1, N_Q_HEADS), F32)] + p_shapes,
        scratch_shapes=p_scr,
        compiler_params=pltpu.CompilerParams(dimension_semantics=("arbitrary",)),
    )(proj, proj, proj, sinks, dmix, *(plan.ins if plan is not None else []))
    return res[:4], res[4:]


_ZW = D_SSM
_XBC_W = D_SSM + 2 * SSD_GROUPS * SSD_N
_DT_COL = (_ZW + _XBC_W) // LANES
EVEN_IN = 3 * D_MODEL + 2 * LANES
ODD_IN = _ZW + _XBC_W + SSD_HEADS
ODD_IN_PAD = _ZW + _XBC_W + LANES


def _ssm_convprep_fwd(proj, cw, cb, dt_bias, name):
    lp = proj.shape[0]
    kk = cw.shape[0]
    tm, wc = _conv_tiles(lp, _XBC_W)
    offb = _ZW // wc
    nj = _XBC_W // wc
    hb = tm // SUBLANES

    def body(x_ref, xp_ref, dtr_ref, w_ref, b_ref, bias_ref, xc_ref, act_ref, dt_ref):
        i, j = pl.program_id(0), pl.program_id(1)
        real = _real_rows(i, tm)
        xv = x_ref[...]
        xx = jnp.concatenate([jnp.where(i > 0, xp_ref[...], 0.0), xv], axis=0)
        acc = b_ref[...] + w_ref[kk - 1:kk, :] * xv
        for m in range(1, kk):
            acc = acc + w_ref[kk - 1 - m:kk - m, :] * pltpu.roll(xx, m, 0)[SUBLANES:, :]
        xc_ref[...] = acc
        act, _ = _silu_and_grad(acc)
        act_ref[...] = jnp.where(real, act, 0.0)

        @pl.when(j == 0)
        def _():
            dt_ref[...] = jnp.where(real, _softplus(dtr_ref[...] + bias_ref[...]), 0.0)

    return pl.pallas_call(
        body, name=name, grid=(lp // tm, nj),
        in_specs=[pl.BlockSpec((tm, wc), lambda i, j: (i, offb + j)),
                  pl.BlockSpec((SUBLANES, wc), lambda i, j: (jnp.maximum(i * hb - 1, 0), offb + j)),
                  pl.BlockSpec((tm, LANES), lambda i, j: (i, _DT_COL)),
                  pl.BlockSpec((kk, wc), lambda i, j: (0, j)),
                  pl.BlockSpec((1, wc), lambda i, j: (0, j)),
                  pl.BlockSpec((1, LANES), lambda i, j: (0, 0))],
        out_specs=[pl.BlockSpec((tm, wc), lambda i, j: (i, j)),
                   pl.BlockSpec((tm, wc), lambda i, j: (i, j)),
                   pl.BlockSpec((tm, LANES), lambda i, j: (i, 0))],
        out_shape=[jax.ShapeDtypeStruct((lp, _XBC_W), F32), jax.ShapeDtypeStruct((lp, _XBC_W), F32),
                   jax.ShapeDtypeStruct((lp, LANES), F32)],
        compiler_params=pltpu.CompilerParams(dimension_semantics=("parallel", "arbitrary")),
    )(proj, proj, proj, cw, cb, dt_bias)


def _ssm_prep_bwd(xc, proj, dt_bias, dxs, dxskip, db, dc, ddt, name):
    lp = xc.shape[0]
    tm = BLOCK

    def body(r, first, xc_ref, dtr_ref, dxs_ref, dsk_ref, db_ref, dc_ref, ddt_ref, b_ref,
             dxc_ref, ddtr_ref, dbias_ref):
        real = _real_rows(r, tm)
        _, ds = _silu_and_grad(xc_ref[...])
        up = lambda ref: ref[...].astype(F32)
        dxc_ref[:, :D_SSM] = jnp.where(
            real, (up(dxs_ref) + up(dsk_ref)) * ds[:, :D_SSM], 0.0).astype(dxc_ref.dtype)
        dxc_ref[:, D_SSM:D_SSM + 1024] = jnp.where(
            real, up(db_ref) * ds[:, D_SSM:D_SSM + 1024], 0.0).astype(dxc_ref.dtype)
        dxc_ref[:, D_SSM + 1024:] = jnp.where(
            real, up(dc_ref) * ds[:, D_SSM + 1024:], 0.0).astype(dxc_ref.dtype)
        dd = jnp.where(real, ddt_ref[...] * _sigmoid(dtr_ref[...] + b_ref[...]), 0.0)
        ddtr_ref[...] = dd.astype(ddtr_ref.dtype)
        _acc_add(first, dbias_ref, jnp.sum(dd, axis=0, keepdims=True))

    return _rowcall(name, body, lp, tm,
                    rows=[(xc, _XBC_W, 0), (proj, LANES, _DT_COL), (dxs, D_SSM, 0), (dxskip, D_SSM, 0),
                          (db, 1024, 0), (dc, 1024, 0), (ddt, LANES, 0)],
                    vecs=[dt_bias], outs=[(_XBC_W, BF16), (LANES, BF16)], accs=[((1, LANES), F32)])


def _ssd_common(dt, alog):
    a = -jnp.exp(alog)
    cs = _cumsum_rows(dt * a, BLOCK)
    cst = cs.T
    cl = cs[BLOCK - 1:BLOCK, :]
    tril = (lax.broadcasted_iota(jnp.int32, (BLOCK, BLOCK), 0)
            >= lax.broadcasted_iota(jnp.int32, (BLOCK, BLOCK), 1))
    return a, cs, cst, cl, jnp.exp(cs), jnp.exp(cl - cs), jnp.exp(cl), tril


def _head_cols(ecl, g):
    lane = lax.broadcasted_iota(jnp.int32, (1, SSD_HPG * SSD_P), 1)
    e = [ecl[:, SSD_HPG * g + hh:SSD_HPG * g + hh + 1] for hh in range(SSD_HPG)]
    return jnp.where(lane < SSD_P, e[0], jnp.where(lane < 2 * SSD_P, e[1],
                                                   jnp.where(lane < 3 * SSD_P, e[2], e[3])))


def _ssd_fwd(xbc, dt, alog, name, plan=None):
    lp = xbc.shape[0]
    nc = lp // BLOCK
    gw = SSD_HPG * SSD_P
    p_in, p_shapes, p_out, p_scr = _plan_parts(plan)

    def body(*refs):
        xs_ref, b_ref, c_ref, dt_ref, alog_ref = refs[:5]
        cins = refs[5:5 + len(p_in)]
        y_ref, so_ref = refs[5 + len(p_in):7 + len(p_in)]
        couts = refs[7 + len(p_in):7 + len(p_in) + len(p_out)]
        st, fx = refs[7 + len(p_in) + len(p_out):9 + len(p_in) + len(p_out)]
        sems = refs[9 + len(p_in) + len(p_out):]
        n = pl.program_id(0)

        @pl.when(n == 0)
        def _():
            st[...] = jnp.zeros_like(st)
            if plan is not None:
                plan.start(cins, couts, sems)

        dtv = dt_ref[...]
        _, cs, cst, cl, e, f, ecl, tril = _ssd_common(dtv, alog_ref[...])
        for g in range(SSD_GROUPS):
            bg = b_ref[:, g * SSD_N:(g + 1) * SSD_N].astype(BF16)
            cg = c_ref[:, g * SSD_N:(g + 1) * SSD_N].astype(BF16)
            gm = _dot_nt(cg, bg)
            stg = st[g]
            so_ref[0, g] = stg
            yoff = _dot(cg, stg.astype(BF16))
            heads = [SSD_HPG * g + hh for hh in range(SSD_HPG)]
            cols = lambda v: jnp.stack([v[:, h:h + 1] for h in heads])
            x4 = jnp.stack([xs_ref[:, h * SSD_P:(h + 1) * SSD_P] for h in heads])
            csr = jnp.stack([cst[h:h + 1, :] for h in heads])
            m = gm[None] * jnp.exp(jnp.where(tril[None], cols(cs) - csr, NEG))
            xdt = x4 * cols(dtv)
            yoff4 = jnp.stack([yoff[:, hh * SSD_P:(hh + 1) * SSD_P] for hh in range(SSD_HPG)])
            y4 = (jnp.einsum("hls,hsp->hlp", m.astype(BF16), xdt.astype(BF16), preferred_element_type=F32)
                  + cols(e) * yoff4)
            fx4 = cols(f) * xdt
            for hh, h in enumerate(heads):
                y_ref[:, h * SSD_P:(h + 1) * SSD_P] = y4[hh]
                fx[:, hh * SSD_P:(hh + 1) * SSD_P] = fx4[hh]
            st[g] = stg * _head_cols(ecl, g) + _dot_tn(bg, fx[...].astype(BF16))
        if plan is not None:
            @pl.when(n == nc - 1)
            def _():
                plan.wait(cins, couts, sems)

    res = pl.pallas_call(
        body, name=name, grid=(nc,),
        in_specs=[pl.BlockSpec((BLOCK, D_SSM), lambda n: (n, 0)),
                  pl.BlockSpec((BLOCK, 1024), lambda n: (n, 2)),
                  pl.BlockSpec((BLOCK, 1024), lambda n: (n, 3)),
                  pl.BlockSpec((BLOCK, LANES), lambda n: (n, 0)),
                  pl.BlockSpec((1, LANES), lambda n: (0, 0))] + p_in,
        out_specs=[pl.BlockSpec((BLOCK, D_SSM), lambda n: (n, 0)),
                   pl.BlockSpec((1, SSD_GROUPS, SSD_N, gw), lambda n: (n, 0, 0, 0))] + p_out,
        out_shape=[jax.ShapeDtypeStruct((lp, D_SSM), F32),
                   jax.ShapeDtypeStruct((nc, SSD_GROUPS, SSD_N, gw), F32)] + p_shapes,
        scratch_shapes=[pltpu.VMEM((SSD_GROUPS, SSD_N, gw), F32), pltpu.VMEM((BLOCK, gw), F32)] + p_scr,
        compiler_params=pltpu.CompilerParams(dimension_semantics=("arbitrary",)),
    )(xbc, xbc, xbc, dt, alog, *(plan.ins if plan is not None else []))
    return res[:2], res[2:]


def _ssd_bwd(xbc, dt, alog, states, dy, name, plan=None):
    lp = xbc.shape[0]
    nc = lp // BLOCK
    gw = SSD_HPG * SSD_P
    p_in, p_shapes, p_out, p_scr = _plan_parts(plan)

    def body(*refs):
        xs_ref, b_ref, c_ref, dt_ref, alog_ref, dy_ref, st_ref = refs[:7]
        cins = refs[7:7 + len(p_in)]
        dxs_ref, db_ref, dc_ref, ddt_ref, dalog_ref = refs[7 + len(p_in):12 + len(p_in)]
        couts = refs[12 + len(p_in):12 + len(p_in) + len(p_out)]
        dst, edy, fx = refs[12 + len(p_in) + len(p_out):15 + len(p_in) + len(p_out)]
        sems = refs[15 + len(p_in) + len(p_out):]
        i = pl.program_id(0)

        @pl.when(i == 0)
        def _():
            dst[...] = jnp.zeros_like(dst)
            dalog_ref[...] = jnp.zeros_like(dalog_ref)
            if plan is not None:
                plan.start(cins, couts, sems)

        dtv = dt_ref[...]
        a, cs, cst, cl, e, f, ecl, tril = _ssd_common(dtv, alog_ref[...])
        lane = lax.broadcasted_iota(jnp.int32, (1, LANES), 1)
        sub = _row_iota(BLOCK)
        triu = (lax.broadcasted_iota(jnp.int32, (BLOCK, BLOCK), 1)
                >= lax.broadcasted_iota(jnp.int32, (BLOCK, BLOCK), 0))
        dcs = jnp.zeros((BLOCK, LANES), F32)
        dcst = jnp.zeros((LANES, BLOCK), F32)
        dcl = jnp.zeros((1, LANES), F32)
        ddtx = jnp.zeros((BLOCK, LANES), F32)
        for g in range(SSD_GROUPS):
            bg = b_ref[:, g * SSD_N:(g + 1) * SSD_N].astype(BF16)
            cg = c_ref[:, g * SSD_N:(g + 1) * SSD_N].astype(BF16)
            gm = _dot_nt(cg, bg)
            stg = st_ref[0, g]
            stb = stg.astype(BF16)
            dso = dst[g]
            dsob = dso.astype(BF16)
            yraw = _dot(cg, stb)
            dfx = _dot(bg, dsob)
            prodsum = jnp.sum(dso * stg, axis=0, keepdims=True)
            heads = [SSD_HPG * g + hh for hh in range(SSD_HPG)]
            cols = lambda v: jnp.stack([v[:, h:h + 1] for h in heads])
            parts = lambda v: jnp.stack([v[:, hh * SSD_P:(hh + 1) * SSD_P] for hh in range(SSD_HPG)])
            x4 = jnp.stack([xs_ref[:, h * SSD_P:(h + 1) * SSD_P] for h in heads])
            dy4 = jnp.stack([dy_ref[:, h * SSD_P:(h + 1) * SSD_P] for h in heads])
            csc, dtc, ec, fc = cols(cs), cols(dtv), cols(e), cols(f)
            csr = jnp.stack([cst[h:h + 1, :] for h in heads])
            seg = csc - csr
            lam = jnp.exp(jnp.where(tril[None], seg, NEG))
            lamt = jnp.exp(jnp.where(triu[None], -seg, NEG))
            mt = _dot_nt(bg, cg)[None] * lamt
            xdt = x4 * dtc
            dyb = dy4.astype(BF16)
            dm = jnp.einsum("hlp,hsp->hls", dyb, xdt.astype(BF16), preferred_element_type=F32)
            dfx4 = parts(dfx)
            dxdt = jnp.einsum("hsl,hlp->hsp", mt.astype(BF16), dyb, preferred_element_type=F32) + fc * dfx4
            dml = dm * lam
            w = dml * gm[None]
            dgm = jnp.sum(dml, axis=0)
            dff = jnp.sum(dfx4 * xdt, axis=2, keepdims=True) * fc
            colv = (jnp.sum(w, axis=2, keepdims=True)
                    + jnp.sum(dy4 * parts(yraw), axis=2, keepdims=True) * ec - dff)
            roww = jnp.sum(w, axis=1, keepdims=True)
            ddtc = jnp.sum(dxdt * x4, axis=2, keepdims=True)
            dffs = jnp.sum(dff, axis=1, keepdims=True)
            dxs4 = dxdt * dtc
            edy4 = ec * dy4
            fx4 = fc * xdt
            for hh, h in enumerate(heads):
                ls = slice(hh * SSD_P, (hh + 1) * SSD_P)
                onl = (lane == h).astype(F32)
                dcs = dcs + colv[hh] * onl
                dcst = dcst - (sub == h).astype(F32) * roww[hh]
                dcl = dcl + (dffs[hh] + ecl[:, h:h + 1] * jnp.sum(prodsum[:, ls], axis=1, keepdims=True)) * onl
                ddtx = ddtx + ddtc[hh] * onl
                dxs_ref[:, h * SSD_P:(h + 1) * SSD_P] = dxs4[hh].astype(dxs_ref.dtype)
                edy[:, ls] = edy4[hh]
                fx[:, ls] = fx4[hh]
            edyb = edy[...].astype(BF16)
            fxb = fx[...].astype(BF16)
            dgb = dgm.astype(BF16)
            dc_ref[:, g * SSD_N:(g + 1) * SSD_N] = (_dot_nt(edyb, stb) + _dot(dgb, bg)).astype(dc_ref.dtype)
            db_ref[:, g * SSD_N:(g + 1) * SSD_N] = (_dot_nt(fxb, dsob) + _dot_tn(dgb, cg)).astype(db_ref.dtype)
            dst[g] = dso * _head_cols(ecl, g) + _dot_tn(cg, edyb)
        dcs = dcs + dcst.T + jnp.where(sub == BLOCK - 1, dcl, 0.0)
        dda = _rev_cumsum_rows(dcs, BLOCK)
        ddt_ref[...] = ddtx + dda * a
        dalog_ref[...] += jnp.sum(dda * dtv, axis=0, keepdims=True) * a
        if plan is not None:
            @pl.when(i == nc - 1)
            def _():
                plan.wait(cins, couts, sems)

    rev = lambda i: nc - 1 - i
    res = pl.pallas_call(
        body, name=name, grid=(nc,),
        in_specs=[pl.BlockSpec((BLOCK, D_SSM), lambda i: (rev(i), 0)),
                  pl.BlockSpec((BLOCK, 1024), lambda i: (rev(i), 2)),
                  pl.BlockSpec((BLOCK, 1024), lambda i: (rev(i), 3)),
                  pl.BlockSpec((BLOCK, LANES), lambda i: (rev(i), 0)),
                  pl.BlockSpec((1, LANES), lambda i: (0, 0)),
                  pl.BlockSpec((BLOCK, D_SSM), lambda i: (rev(i), 0)),
                  pl.BlockSpec((1, SSD_GROUPS, SSD_N, gw), lambda i: (rev(i), 0, 0, 0))] + p_in,
        out_specs=[pl.BlockSpec((BLOCK, D_SSM), lambda i: (rev(i), 0)),
                   pl.BlockSpec((BLOCK, 1024), lambda i: (rev(i), 0)),
                   pl.BlockSpec((BLOCK, 1024), lambda i: (rev(i), 0)),
                   pl.BlockSpec((BLOCK, LANES), lambda i: (rev(i), 0)),
                   pl.BlockSpec((1, LANES), lambda i: (0, 0))] + p_out,
        out_shape=[jax.ShapeDtypeStruct((lp, D_SSM), BF16),
                   jax.ShapeDtypeStruct((lp, 1024), BF16),
                   jax.ShapeDtypeStruct((lp, 1024), BF16),
                   jax.ShapeDtypeStruct((lp, LANES), F32),
                   jax.ShapeDtypeStruct((1, LANES), F32)] + p_shapes,
        scratch_shapes=[pltpu.VMEM((SSD_GROUPS, SSD_N, gw), F32),
                        pltpu.VMEM((BLOCK, gw), F32), pltpu.VMEM((BLOCK, gw), F32)] + p_scr,
        compiler_params=pltpu.CompilerParams(dimension_semantics=("arbitrary",)),
    )(xbc, xbc, xbc, dt, alog, dy, states, *(plan.ins if plan is not None else []))
    return res[:5], res[5:]


_GN_GROUPS = 8
_GN_W = D_SSM // _GN_GROUPS


def _ssm_gate_fwd(yssd, xbc, proj, dskip, gnorm, name):
    lp = yssd.shape[0]
    tm = _pick(lp, (320, 256, 128))

    def body(r, first, y_ref, x_ref, z_ref, d_ref, g_ref, o_ref):
        sz, _ = _silu_and_grad(z_ref[...])
        y2 = (y_ref[...] + d_ref[...] * x_ref[...]) * sz
        for k in range(_GN_GROUPS):
            sl = slice(k * _GN_W, (k + 1) * _GN_W)
            yk = y2[:, sl]
            rs = lax.rsqrt(jnp.mean(yk * yk, axis=-1, keepdims=True) + EPS)
            o_ref[:, sl] = (yk * rs * g_ref[:, sl]).astype(o_ref.dtype)

    return _rowcall(name, body, lp, tm, rows=[(yssd, D_SSM, 0), (xbc, D_SSM, 0), (proj, D_SSM, 0)],
                    vecs=[dskip, gnorm], outs=[(D_SSM, BF16)])[0]


def _ssm_gate_bwd(yssd, xbc, proj, dskip, gnorm, dyn, name):
    lp = yssd.shape[0]
    tm = BLOCK

    def body(r, first, y_ref, x_ref, z_ref, dyn_ref, d_ref, g_ref,
             dy_ref, dx_ref, dz_ref, dd_ref, dg_ref):
        z = z_ref[...]
        sz, dsz = _silu_and_grad(z)
        xs = x_ref[...]
        y1 = y_ref[...] + d_ref[...] * xs
        y2 = y1 * sz
        dyn = dyn_ref[...]
        for k in range(_GN_GROUPS):
            sl = slice(k * _GN_W, (k + 1) * _GN_W)
            dx, dgt = _rms_bwd(y2[:, sl], g_ref[:, sl], dyn[:, sl])
            dy1 = dx * sz[:, sl]
            dy_ref[:, sl] = dy1.astype(dy_ref.dtype)
            dx_ref[:, sl] = (dy1 * d_ref[:, sl]).astype(dx_ref.dtype)
            dz_ref[:, sl] = (dx * y1[:, sl] * dsz[:, sl]).astype(dz_ref.dtype)

            @pl.when(first)
            def _():
                dd_ref[:, sl] = jnp.zeros((1, _GN_W), F32)
                dg_ref[:, sl] = jnp.zeros((1, _GN_W), F32)

            dd_ref[:, sl] += jnp.sum(dy1 * xs[:, sl], axis=0, keepdims=True)
            dg_ref[:, sl] += jnp.sum(dgt, axis=0, keepdims=True)

    return _rowcall(name, body, lp, tm,
                    rows=[(yssd, D_SSM, 0), (xbc, D_SSM, 0), (proj, D_SSM, 0), (dyn, D_SSM, 0)],
                    vecs=[dskip, gnorm], outs=[(D_SSM, BF16), (D_SSM, BF16), (D_SSM, BF16)],
                    accs=[((1, D_SSM), F32), ((1, D_SSM), F32)])


def _shape2d(shape):
    n = math.prod(shape)
    if len(shape) == 2:
        return tuple(shape)
    return (n // LANES, LANES) if n % LANES == 0 else (1, n)


def _adamw_many(ws, gs, ms, vs, name):
    n = len(ws)
    c1 = 1.0 / (1.0 - ADAM_B1 ** ADAM_STEP)
    c2 = 1.0 / (1.0 - ADAM_B2 ** ADAM_STEP)

    def body(*refs):
        for i in range(n):
            w_ref, g_ref, m_ref, v_ref = (refs[j * n + i] for j in range(4))
            d_ref, nm_ref, nv_ref = (refs[(4 + j) * n + i] for j in range(3))
            gv = g_ref[...]
            nm = ADAM_B1 * m_ref[...] + (1.0 - ADAM_B1) * gv
            nv = ADAM_B2 * v_ref[...] + (1.0 - ADAM_B2) * (gv * gv)
            nm_ref[...] = nm
            nv_ref[...] = nv
            d_ref[...] = -ADAM_LR * ((nm * c1) / (jnp.sqrt(nv * c2) + ADAM_EPS) + ADAM_WD * w_ref[...])

    vm = pl.BlockSpec(memory_space=pltpu.VMEM)
    return pl.pallas_call(
        body, name=name, in_specs=[vm] * (4 * n), out_specs=[vm] * (3 * n),
        out_shape=[jax.ShapeDtypeStruct(w.shape, F32) for w in ws] * 3,
    )(*ws, *gs, *ms, *vs)


def _place():
    return lax.axis_index("x"), lax.axis_index("y"), lax.axis_index("c")


def _other_chips(x, y):
    return [(1 - x, y), (x, 1 - y), (1 - x, 1 - y)]


_ANY = pl.BlockSpec(memory_space=pl.ANY)


class _Plan:
    def __init__(self, ins, out_shapes, n_remote, n_local, issue):
        self.ins = list(ins)
        self.out_shapes = list(out_shapes)
        self.issue = issue
        self.scratch = [pltpu.SemaphoreType.DMA((max(n_remote, 1),)),
                        pltpu.SemaphoreType.DMA((max(n_remote, 1),)),
                        pltpu.SemaphoreType.DMA((max(n_local, 1),))]

    def start(self, ins, outs, sems):
        sends, _, locs = self.issue(ins, outs, *sems)
        for cp in locs + sends:
            cp.start()

    def wait(self, ins, outs, sems):
        sends, recvs, locs = self.issue(ins, outs, *sems)
        for make in recvs:
            make().wait_recv()
        for cp in sends:
            cp.wait_send()
        for cp in locs:
            cp.wait()


def _plan_parts(plan):
    if plan is None:
        return [], [], [], []
    return ([_ANY] * len(plan.ins), plan.out_shapes, [_ANY] * len(plan.out_shapes), plan.scratch)


def _run_plan(plan, name):
    n_in, n_out = len(plan.ins), len(plan.out_shapes)

    def body(*refs):
        ins, outs, sems = refs[:n_in], refs[n_in:n_in + n_out], refs[n_in + n_out:]
        plan.start(ins, outs, sems)
        plan.wait(ins, outs, sems)

    return pl.pallas_call(
        body, name=name, in_specs=[_ANY] * n_in, out_specs=[_ANY] * n_out,
        out_shape=plan.out_shapes, scratch_shapes=plan.scratch,
    )(*plan.ins)


def _gather_plan(shards):
    n = len(shards)

    def issue(ins, outs, send_sems, recv_sems, local_sems):
        x, y, c = _place()
        me = 2 * x + y
        sends, recvs, locs = [], [], []
        for p in range(n):
            locs.append(pltpu.make_async_copy(ins[p], outs[p].at[me], local_sems.at[p]))
            for k, (px, py) in enumerate(_other_chips(x, y)):
                sems = dict(send_sem=send_sems.at[3 * p + k], recv_sem=recv_sems.at[3 * p + k],
                            device_id=(px, py, c), device_id_type=MESH)
                sends.append(pltpu.make_async_remote_copy(src_ref=ins[p], dst_ref=outs[p].at[me], **sems))
                recvs.append(functools.partial(pltpu.make_async_remote_copy, src_ref=ins[p],
                                               dst_ref=outs[p].at[2 * px + py], **sems))
        return sends, recvs, locs

    return _Plan(shards, [jax.ShapeDtypeStruct((N_CHIPS,) + s.shape, s.dtype) for s in shards], 3 * n, n, issue)


_REL7 = [(fx, fy, fc) for fx in (0, 1) for fy in (0, 1) for fc in (0, 1)][1:]


def _scatter8_plan(gs):
    n = len(gs)

    def issue(ins, outs, send_sems, recv_sems, local_sems):
        x, y, c = _place()
        sends = []
        for p in range(n):
            hr = gs[p].shape[1] // 2
            for k, (fx, fy, fc) in enumerate(_REL7):
                tx, ty, tc = x ^ fx, y ^ fy, c ^ fc
                src = ins[p].at[2 * tx + ty, pl.ds(pl.multiple_of(tc * hr, SUBLANES), hr), :]
                sends.append(pltpu.make_async_remote_copy(
                    src_ref=src, dst_ref=outs[p].at[k],
                    send_sem=send_sems.at[7 * p + k], recv_sem=recv_sems.at[7 * p + k],
                    device_id=(tx, ty, tc), device_id_type=MESH))
        return sends, [functools.partial(lambda cp: cp, cp) for cp in sends], []

    shapes = [jax.ShapeDtypeStruct((7, g.shape[1] // 2, g.shape[2]), g.dtype) for g in gs]
    return _Plan(gs, shapes, 7 * n, 0, issue)


def _sibling_plan(ts):
    n = len(ts)

    def issue(ins, outs, send_sems, recv_sems, local_sems):
        x, y, c = _place()
        sends = [pltpu.make_async_remote_copy(
            src_ref=ins[p], dst_ref=outs[p], send_sem=send_sems.at[p], recv_sem=recv_sems.at[p],
            device_id=(x, y, 1 - c), device_id_type=MESH) for p in range(n)]
        return sends, [functools.partial(lambda cp: cp, cp) for cp in sends], []

    return _Plan(ts, [jax.ShapeDtypeStruct(t.shape, t.dtype) for t in ts], n, 0, issue)


def _add8(g, recv, chip, core, name):
    s, r, n = g.shape
    hr = r // 2
    th = hr // 2 if (hr // 2) % SUBLANES == 0 else hr
    nt = hr // th

    def body(chip_ref, core_ref, g_ref, r_ref, o_ref):
        acc = g_ref[0].astype(F32)
        for k in range(7):
            acc = acc + r_ref[k].astype(F32)
        o_ref[...] = acc

    return pl.pallas_call(
        body, name=name,
        grid_spec=pltpu.PrefetchScalarGridSpec(
            num_scalar_prefetch=2, grid=(nt,),
            in_specs=[pl.BlockSpec((1, th, n), lambda i, ch, co: (ch[0], co[0] * nt + i, 0)),
                      pl.BlockSpec((7, th, n), lambda i, ch, co: (0, i, 0))],
            out_specs=pl.BlockSpec((th, n), lambda i, ch, co: (i, 0))),
        out_shape=jax.ShapeDtypeStruct((hr, n), F32),
        compiler_params=pltpu.CompilerParams(dimension_semantics=("parallel",)),
    )(chip, core, g, recv)


def _adamw_halves(w, own, other, m, v, core, name):
    r, n = w.shape
    hr = r // 2
    th = hr // 2 if (hr // 2) % SUBLANES == 0 else hr
    tph = hr // th
    c1 = 1.0 / (1.0 - ADAM_B1 ** ADAM_STEP)
    c2 = 1.0 / (1.0 - ADAM_B2 ** ADAM_STEP)

    def body(core_ref, w_ref, a_ref, b_ref, m_ref, v_ref, g_ref, d_ref, nm_ref, nv_ref):
        half = pl.program_id(0) // tph
        gv = jnp.where(half == core_ref[0], a_ref[...], b_ref[...])
        nm = ADAM_B1 * m_ref[...] + (1.0 - ADAM_B1) * gv
        nv = ADAM_B2 * v_ref[...] + (1.0 - ADAM_B2) * (gv * gv)
        g_ref[...] = gv
        nm_ref[...] = nm
        nv_ref[...] = nv
        d_ref[...] = -ADAM_LR * ((nm * c1) / (jnp.sqrt(nv * c2) + ADAM_EPS) + ADAM_WD * w_ref[...])

    full = pl.BlockSpec((th, n), lambda i, co: (i, 0))
    part = pl.BlockSpec((th, n), lambda i, co: (i % tph, 0))
    return pl.pallas_call(
        body, name=name,
        grid_spec=pltpu.PrefetchScalarGridSpec(
            num_scalar_prefetch=1, grid=(2 * tph,),
            in_specs=[full, part, part, full, full], out_specs=[full] * 4),
        out_shape=[jax.ShapeDtypeStruct((r, n), F32)] * 4,
        compiler_params=pltpu.CompilerParams(dimension_semantics=("parallel",)),
    )(core, w, own, other, m, v)


def _allreduce_small(pack, name):
    r, l = pack.shape
    hr = r // 2
    assert hr % SUBLANES == 0

    def body(p_ref, o_ref, sib, chips, send_sems, recv_sems):
        x, y, c = _place()
        chip = 2 * x + y
        sibling = dict(device_id=(x, y, 1 - c), device_id_type=MESH)
        mine = pl.ds(pl.multiple_of(c * hr, SUBLANES), hr)
        other = pl.ds(pl.multiple_of((1 - c) * hr, SUBLANES), hr)
        a = pltpu.make_async_remote_copy(src_ref=p_ref.at[other], dst_ref=sib, send_sem=send_sems.at[0],
                                         recv_sem=recv_sems.at[0], **sibling)
        a.start()
        a.wait()
        own, got = p_ref[mine, :], sib[...]
        chips[chip] = jnp.where(c == 0, own, got) + jnp.where(c == 0, got, own)
        sends = []
        for k, (px, py) in enumerate(_other_chips(x, y)):
            cp = pltpu.make_async_remote_copy(
                src_ref=chips.at[chip], dst_ref=chips.at[chip], send_sem=send_sems.at[1 + k],
                recv_sem=recv_sems.at[1 + k], device_id=(px, py, c), device_id_type=MESH)
            cp.start()
            sends.append(cp)
        for k, (px, py) in enumerate(_other_chips(x, y)):
            pltpu.make_async_remote_copy(
                src_ref=chips.at[chip], dst_ref=chips.at[2 * px + py], send_sem=send_sems.at[1 + k],
                recv_sem=recv_sems.at[1 + k], device_id=(px, py, c), device_id_type=MESH).wait_recv()
        for cp in sends:
            cp.wait_send()
        o_ref[mine, :] = ((chips[0] + chips[1]) + chips[2]) + chips[3]
        fin = pltpu.make_async_remote_copy(src_ref=o_ref.at[mine], dst_ref=o_ref.at[mine],
                                           send_sem=send_sems.at[4], recv_sem=recv_sems.at[4], **sibling)
        fin.start()
        pltpu.make_async_remote_copy(src_ref=o_ref.at[mine], dst_ref=o_ref.at[other],
                                     send_sem=send_sems.at[4], recv_sem=recv_sems.at[4], **sibling).wait_recv()
        fin.wait_send()

    vm = pl.BlockSpec(memory_space=pltpu.VMEM)
    return pl.pallas_call(
        body, name=name, in_specs=[vm], out_specs=vm,
        out_shape=jax.ShapeDtypeStruct((r, l), F32),
        scratch_shapes=[pltpu.VMEM((hr, l), F32), pltpu.VMEM((N_CHIPS, hr, l), F32),
                        pltpu.SemaphoreType.DMA((5,)), pltpu.SemaphoreType.DMA((5,))],
    )(pack)


def _flat_rows(a, mult=SUBLANES * LANES):
    f = a.reshape(-1)
    padn = (-f.shape[0]) % mult
    if padn:
        f = jnp.concatenate([f, jnp.zeros((padn,), f.dtype)])
    return f


def _pack(arrs, mult=SUBLANES * LANES, total_mult=None):
    flat = [_flat_rows(a, mult) for a in arrs]
    sizes = [f.shape[0] for f in flat]
    if total_mult is not None:
        padn = (-sum(sizes)) % total_mult
        if padn:
            flat.append(jnp.zeros((padn,), flat[0].dtype))
    return jnp.concatenate(flat).reshape(-1, LANES), sizes


def _unpack(pack, shapes, sizes, lead=()):
    flat = pack.reshape(lead + (-1,))
    out, off = [], 0
    for shp, sz in zip(shapes, sizes):
        n = math.prod(shp)
        out.append(flat[..., off:off + n].reshape(lead + tuple(shp)))
        off += sz
    return out


def _cols_from_shards(g):
    s, k, n = g.shape
    return jnp.transpose(g, (1, 0, 2)).reshape(k, s * n)


def _cols_to_shards(w, s=N_CHIPS):
    k, n = w.shape
    return jnp.transpose(w.reshape(k, s, n // s), (1, 0, 2))


def _ffn_fwd(h, u, post, w_up, cw, cb, w_down, tag, next_pre=None, plan=None):
    hp = _mm_nn_sh(u, w_up, 2 * D_FF, f"{tag}_up")
    if plan is None:
        (act, hg, hu), pouts = _ffn_convact_fwd(hp, cw, cb, f"{tag}_convact"), ()
    else:
        (act, hg, hu), pouts = _ffn_convact_fwd(hp, cw, cb, f"{tag}_convact", plan)
    o = _mm_nn(act, w_down, F32, f"{tag}_down")
    res = _postnorm_res_fwd(h, o, post, f"{tag}_postnorm", next_pre)
    hn, un = res if next_pre is not None else (res, None)
    return hn, un, (h, u, hp, hg, hu, act, o), pouts


def _ffn_bwd(dh, saved, pre, post, w_up, cw, w_down, tag):
    h, u, hp, hg, hu, act, o = saved
    do, dpost = _postnorm_bwd(o, post, dh, f"{tag}_postnorm_bwd")
    dact = _mm_nt(do, w_down, f"{tag}_down_dx", BF16)
    dw_down = _mm_tn(act, do, f"{tag}_down_dw")
    dhg, dhu = _ffn_act_bwd(hg, hu, dact, f"{tag}_act_bwd")
    dxg, dwg, dbg = _conv_bwd(hp, 0, D_FF, dhg, cw, f"{tag}_conv_bwd_gate")
    dxu, dwu, dbu = _conv_bwd(hp, D_FF, D_FF, dhu, cw, f"{tag}_conv_bwd_up", w_col_off=D_FF)
    dhp = (dxg, dxu)
    dcw = jnp.concatenate([dwg, dwu], axis=1)
    dcb = jnp.concatenate([dbg, dbu], axis=1)
    du = _mm_nt_sh(dhp, w_up, f"{tag}_up_dx")
    dw_up = _mm_tn_sh(u, dhp, w_up.shape[2], f"{tag}_up_dw")
    dhn, dpre = _prenorm_bwd(h, pre, du, dh, f"{tag}_prenorm_bwd")
    return dhn, dict(pre=dpre, post=dpost, w_up=dw_up, conv_w=dcw[:3], conv_b=dcb, w_down=dw_down)


class _Exchange:
    GATHER_IN_LRU = ("l0_w_out", "l0_ffn_w_down")
    GATHER_IN_ATTN = ("l0_ffn_w_up", "l1_w_out")
    GATHER_IN_FFN0 = ("l1_w_in",)
    GATHER_IN_SSD = ("l1_ffn_w_up", "l1_ffn_w_down")
    AFTER_L1_OUT = ("l1_ffn_w_up", "l1_ffn_w_down", "l1_w_out")
    IN_LRU_BWD = ("l1_w_in",)
    AFTER_L0_OUT = ("l0_ffn_w_up", "l0_ffn_w_down", "l0_w_out")
    LAST = ("l0_w_in",)

    def __init__(self, late_shards):
        self.late = dict(late_shards)
        self.slabs = {}
        self.recv = {}

    def gather_plan(self, names):
        return _gather_plan([self.late[n] for n in names])

    def gathered(self, names, outs):
        return {n: (g if n in _BIG_COL else g.reshape(-1, g.shape[-1])) for n, g in zip(names, outs)}

    def scatter_plan(self, grads, names):
        for n in names:
            g = grads[n]
            self.slabs[n] = g if n in _BIG_COL else g.reshape(N_CHIPS, -1, g.shape[-1])
        return _scatter8_plan([self.slabs[n] for n in names])

    def scattered(self, names, outs):
        self.recv.update(zip(names, outs))


def _local_step(x, tgt, meta, P, ex=None):
    seq, d = x.shape
    lp = seq + BLOCK
    h0 = jnp.concatenate([jnp.zeros((PAD, d), F32), meta, x], axis=0)
    tgt_p = jnp.concatenate([jnp.zeros((BLOCK, d), F32), tgt], axis=0)

    u0 = _rmsnorm_fwd(h0, P["l0_mix_pre_norm"], "l0_mix_prenorm")
    proj0 = _mm_nn_sh(u0, P["l0_w_in"], EVEN_IN, "l0_in")
    xrc = _conv_fwd(proj0, D_MODEL, D_MODEL, P["l0_lru_conv_w"], P["l0_lru_conv_b"], "l0_lru_conv")
    lru_args = (P["l0_lru_w_a"], P["l0_lru_w_x"], P["l0_lru_b_a"], P["l0_lru_b_x"], P["l0_lru_lambda"])
    if ex:
        (ya, hl), outs = _lru_fwd(proj0, xrc, *lru_args, "l0_lru", ex.gather_plan(ex.GATHER_IN_LRU))
        P = {**P, **ex.gathered(ex.GATHER_IN_LRU, outs)}
    else:
        ya, hl = _lru_fwd(proj0, xrc, *lru_args, "l0_lru")
    yb, outs = _attn_fwd(proj0, P["l0_attn_sinks"], "l0_attn",
                         ex.gather_plan(ex.GATHER_IN_ATTN) if ex else None)
    if ex:
        P = {**P, **ex.gathered(ex.GATHER_IN_ATTN, outs)}
    o0 = _mm_nn((ya, yb), P["l0_w_out"], F32, "l0_out")
    h1, u1 = _postnorm_res_fwd(h0, o0, P["l0_mix_post_norm"], "l0_mix_postnorm", P["l0_ffn_pre_norm"])
    h2, u2, ffn0, outs = _ffn_fwd(h1, u1, P["l0_ffn_post_norm"], P["l0_ffn_w_up"], P["l0_ffn_conv_w"],
                                  P["l0_ffn_conv_b"], P["l0_ffn_w_down"], "l0_ffn", P["l1_mix_pre_norm"],
                                  ex.gather_plan(ex.GATHER_IN_FFN0) if ex else None)
    if ex:
        P = {**P, **ex.gathered(ex.GATHER_IN_FFN0, outs)}
    proj1 = _mm_nn_sh(u2, P["l1_w_in"], ODD_IN_PAD, "l1_in")
    xc1, xbc, dt = _ssm_convprep_fwd(proj1, P["l1_ssm_conv_w"], P["l1_ssm_conv_b"], P["l1_dt_bias"],
                                     "l1_ssm_convprep")
    (yssd, states), outs = _ssd_fwd(xbc, dt, P["l1_a_log"], "l1_ssd",
                                    ex.gather_plan(ex.GATHER_IN_SSD) if ex else None)
    if ex:
        P = {**P, **ex.gathered(ex.GATHER_IN_SSD, outs)}
    yn = _ssm_gate_fwd(yssd, xbc, proj1, P["l1_d_skip"], P["l1_gate_norm"], "l1_ssm_gate")
    o1 = _mm_nn(yn, P["l1_w_out"], F32, "l1_out")
    h3, u3 = _postnorm_res_fwd(h2, o1, P["l1_mix_post_norm"], "l1_mix_postnorm", P["l1_ffn_pre_norm"])
    h4, _, ffn1, _ = _ffn_fwd(h3, u3, P["l1_ffn_post_norm"], P["l1_ffn_w_up"], P["l1_ffn_conv_w"],
                              P["l1_ffn_conv_b"], P["l1_ffn_w_down"], "l1_ffn")
    dh4, loss_cols = _loss_fwd_bwd(h4, tgt_p, "loss")

    G = {}
    dh3, g = _ffn_bwd(dh4, ffn1, P["l1_ffn_pre_norm"], P["l1_ffn_post_norm"], P["l1_ffn_w_up"],
                      P["l1_ffn_conv_w"], P["l1_ffn_w_down"], "l1_ffn")
    for k, v in g.items():
        G["l1_ffn_" + (k + "_norm" if k in ("pre", "post") else k)] = v
    do1, G["l1_mix_post_norm"] = _postnorm_bwd(o1, P["l1_mix_post_norm"], dh3, "l1_mix_postnorm_bwd")
    dyn = _mm_nt(do1, P["l1_w_out"], "l1_out_dx")
    G["l1_w_out"] = _mm_tn(yn, do1, "l1_out_dw")
    dyssd, dxskip, dz, dd_cols, G["l1_gate_norm"] = _ssm_gate_bwd(
        yssd, xbc, proj1, P["l1_d_skip"], P["l1_gate_norm"], dyn, "l1_ssm_gate_bwd")
    G["l1_d_skip"] = dd_cols.reshape(SSD_HEADS, SSD_P).sum(axis=1)
    (dxs, dbm, dcm, ddt, dalog), outs = _ssd_bwd(
        xbc, dt, P["l1_a_log"], states, dyssd, "l1_ssd_bwd",
        ex.scatter_plan(G, ex.AFTER_L1_OUT) if ex else None)
    if ex:
        ex.scattered(ex.AFTER_L1_OUT, outs)
    G["l1_a_log"] = dalog[0, :SSD_HEADS]
    dxc, ddtr, dbias = _ssm_prep_bwd(xc1, proj1, P["l1_dt_bias"], dxs, dxskip, dbm, dcm, ddt,
                                     "l1_ssm_prep_bwd")
    G["l1_dt_bias"] = dbias[0, :SSD_HEADS]
    dxbc, dcw, dcb = _conv_bwd(proj1, _ZW, _XBC_W, dxc, P["l1_ssm_conv_w"], "l1_ssm_conv_bwd")
    G["l1_ssm_conv_w"] = dcw[:4]
    G["l1_ssm_conv_b"] = dcb
    dproj1 = jnp.concatenate([dz, dxbc, ddtr], axis=1)
    du2 = _mm_nt_sh(dproj1, P["l1_w_in"], "l1_in_dx")
    G["l1_w_in"] = _mm_tn_sh(u2, dproj1, ODD_IN // N_CHIPS, "l1_in_dw")
    dh2, G["l1_mix_pre_norm"] = _prenorm_bwd(h2, P["l1_mix_pre_norm"], du2, dh3, "l1_mix_prenorm_bwd")
    dh1, g = _ffn_bwd(dh2, ffn0, P["l0_ffn_pre_norm"], P["l0_ffn_post_norm"], P["l0_ffn_w_up"],
                      P["l0_ffn_conv_w"], P["l0_ffn_w_down"], "l0_ffn")
    for k, v in g.items():
        G["l0_ffn_" + (k + "_norm" if k in ("pre", "post") else k)] = v
    do0, G["l0_mix_post_norm"] = _postnorm_bwd(o0, P["l0_mix_post_norm"], dh1, "l0_mix_postnorm_bwd")
    dmix = _mm_nt(do0, P["l0_w_out"], "l0_out_dx")
    G["l0_w_out"] = jnp.concatenate([_mm_tn(ya, do0, "l0_out_dw_lru"), _mm_tn(yb, do0, "l0_out_dw_attn")], axis=0)
    if ex:
        lru_out, outs = _lru_bwd(proj0, xrc, hl, dmix, *lru_args, "l0_lru_bwd",
                                 ex.scatter_plan(G, ex.IN_LRU_BWD))
        ex.scattered(ex.IN_LRU_BWD, outs)
    else:
        lru_out = _lru_bwd(proj0, xrc, hl, dmix, *lru_args, "l0_lru_bwd")
    (dgate, dxrc, G["l0_lru_w_a"], G["l0_lru_w_x"], G["l0_lru_b_a"], G["l0_lru_b_x"],
     G["l0_lru_lambda"]) = lru_out
    dxr, dcw, dcb = _conv_bwd(proj0, D_MODEL, D_MODEL, dxrc, P["l0_lru_conv_w"], "l0_lru_conv_bwd")
    G["l0_lru_conv_w"] = dcw[:4]
    G["l0_lru_conv_b"] = dcb
    (dq, dk, dv, G["l0_attn_sinks"]), outs = _attn_bwd(
        proj0, P["l0_attn_sinks"], dmix, "l0_attn_bwd",
        ex.scatter_plan(G, ex.AFTER_L0_OUT) if ex else None)
    if ex:
        ex.scattered(ex.AFTER_L0_OUT, outs)
    dproj0 = jnp.concatenate([dgate, dxr, dq, dk.astype(BF16), dv.astype(BF16)], axis=1)
    G["l0_w_in"] = _mm_tn_sh(u0, dproj0, EVEN_IN // N_CHIPS, "l0_in_dw")
    if ex:
        du0, outs = _mm_nt_sh(dproj0, P["l0_w_in"], "l0_in_dx", ex.scatter_plan(G, ex.LAST))
        ex.scattered(ex.LAST, outs)
    else:
        du0 = _mm_nt_sh(dproj0, P["l0_w_in"], "l0_in_dx")
    dh0, G["l0_mix_pre_norm"] = _prenorm_bwd(h0, P["l0_mix_pre_norm"], du0, dh1, "l0_mix_prenorm_bwd")
    return loss_cols, dh0[BLOCK:], dh0[PAD:BLOCK], G


_BIG_COL = ("l0_w_in", "l0_ffn_w_up", "l1_w_in", "l1_ffn_w_up")
_BIG_ROW = ("l0_w_out", "l0_ffn_w_down", "l1_w_out", "l1_ffn_w_down")
_BIG = ("l0_w_in", "l0_w_out", "l0_ffn_w_up", "l0_ffn_w_down",
        "l1_w_in", "l1_w_out", "l1_ffn_w_up", "l1_ffn_w_down")
_SMALL_SHARDED = ("meta_tokens", "l0_lru_conv_w", "l0_ffn_conv_w", "l1_ssm_conv_w", "l1_ffn_conv_w")
_WEIGHTS = ("meta_tokens", "l0_mix_pre_norm", "l0_mix_post_norm", "l0_w_in", "l0_lru_conv_w",
            "l0_lru_conv_b", "l0_lru_w_a", "l0_lru_b_a", "l0_lru_w_x", "l0_lru_b_x", "l0_lru_lambda",
            "l0_attn_sinks", "l0_w_out", "l0_ffn_pre_norm", "l0_ffn_post_norm", "l0_ffn_w_up",
            "l0_ffn_conv_w", "l0_ffn_conv_b", "l0_ffn_w_down", "l1_mix_pre_norm", "l1_mix_post_norm",
            "l1_w_in", "l1_ssm_conv_w", "l1_ssm_conv_b", "l1_dt_bias", "l1_a_log", "l1_d_skip",
            "l1_gate_norm", "l1_w_out", "l1_ffn_pre_norm", "l1_ffn_post_norm", "l1_ffn_w_up",
            "l1_ffn_conv_w", "l1_ffn_conv_b", "l1_ffn_w_down")
_REPL = tuple(n for n in _WEIGHTS if n not in _BIG and n not in _SMALL_SHARDED)


def _pad_lanes(v, n=LANES):
    return jnp.concatenate([v, jnp.zeros((n - v.shape[0],), v.dtype)]).reshape(1, n)


def _step(x, tgt, W, M, V):
    cx, cy, cc = _place()
    chip = 2 * cx + cy

    small_pack, small_sizes = _pack([W[n] for n in _SMALL_SHARDED])
    first = _run_plan(_gather_plan([W["l0_w_in"].astype(BF16), small_pack]), "gather_first")
    small_full = _unpack(first[1], [W[n].shape for n in _SMALL_SHARDED], small_sizes, lead=(N_CHIPS,))
    ex = _Exchange({n: W[n].astype(BF16) for n in _BIG if n != "l0_w_in"})

    P = {"l0_w_in": first[0]}
    for n, g in zip(_SMALL_SHARDED, small_full):
        P[n] = _cols_from_shards(g)
    for n in _REPL:
        v = W[n]
        P[n] = v.reshape(1, -1) if v.ndim == 1 else v
    P["l0_lru_w_a"] = W["l0_lru_w_a"].astype(BF16)
    P["l0_lru_w_x"] = W["l0_lru_w_x"].astype(BF16)
    P["l1_dt_bias"] = _pad_lanes(W["l1_dt_bias"])
    P["l1_a_log"] = _pad_lanes(W["l1_a_log"])
    P["l1_d_skip"] = jnp.repeat(W["l1_d_skip"], SSD_P).reshape(1, D_SSM)
    meta = P.pop("meta_tokens")

    loss_cols, grad_x, grad_meta, G = _local_step(x, tgt, meta, P, ex)
    G["meta_tokens"] = grad_meta

    core_idx = cc.astype(jnp.int32).reshape(1)
    chip_idx = chip.astype(jnp.int32).reshape(1)
    own_half = [_add8(ex.slabs[n], ex.recv[n], chip_idx, core_idx, f"grad_sum_{n}") for n in _BIG]
    other_half = _run_plan(_sibling_plan(own_half), "grad_sibling_swap")
    small_names = list(_REPL) + list(_SMALL_SHARDED)
    small_list = [G[n] for n in small_names] + [loss_cols]
    spack, ssizes = _pack(small_list, total_mult=2 * SUBLANES * LANES)
    sred = _allreduce_small(spack, "small_allreduce")
    sfull = _unpack(sred, [a.shape for a in small_list], ssizes)
    loss = 0.5 / D_MODEL * jnp.sum(sfull[-1])
    small_grads = {}
    for n, g in zip(small_names, sfull[:-1]):
        if n in _SMALL_SHARDED:
            wcols = W[n].shape[1]
            g = lax.dynamic_slice_in_dim(g, chip * wcols, wcols, axis=1)
        small_grads[n] = g.reshape(W[n].shape)

    grads, delta, new_m, new_v = {}, {}, {}, {}
    for n, own, other in zip(_BIG, own_half, other_half):
        grads[n], delta[n], new_m[n], new_v[n] = _adamw_halves(
            W[n], own, other, M[n], V[n], core_idx, f"adamw_{n}")
    s_names = [n for n in _WEIGHTS if n not in _BIG]
    as2d = lambda a: a.reshape(_shape2d(a.shape))
    outs = _adamw_many([as2d(W[n]) for n in s_names], [as2d(small_grads[n]) for n in s_names],
                       [as2d(M[n]) for n in s_names], [as2d(V[n]) for n in s_names], "adamw_small")
    k = len(s_names)
    for i, n in enumerate(s_names):
        grads[n] = small_grads[n]
        delta[n], new_m[n], new_v[n] = (outs[j * k + i].reshape(W[n].shape) for j in range(3))
    return loss, grad_x, grads, delta, new_m, new_v


def kernel(x, meta_tokens, l0_mix_pre_norm, l0_mix_post_norm, l0_w_in, l0_lru_conv_w, l0_lru_conv_b, l0_lru_w_a, l0_lru_b_a, l0_lru_w_x, l0_lru_b_x, l0_lru_lambda, l0_attn_sinks, l0_w_out, l0_ffn_pre_norm, l0_ffn_post_norm, l0_ffn_w_up, l0_ffn_conv_w, l0_ffn_conv_b, l0_ffn_w_down, l1_mix_pre_norm, l1_mix_post_norm, l1_w_in, l1_ssm_conv_w, l1_ssm_conv_b, l1_dt_bias, l1_a_log, l1_d_skip, l1_gate_norm, l1_w_out, l1_ffn_pre_norm, l1_ffn_post_norm, l1_ffn_w_up, l1_ffn_conv_w, l1_ffn_conv_b, l1_ffn_w_down, loss_target, m_meta_tokens, m_l0_mix_pre_norm, m_l0_mix_post_norm, m_l0_w_in, m_l0_lru_conv_w, m_l0_lru_conv_b, m_l0_lru_w_a, m_l0_lru_b_a, m_l0_lru_w_x, m_l0_lru_b_x, m_l0_lru_lambda, m_l0_attn_sinks, m_l0_w_out, m_l0_ffn_pre_norm, m_l0_ffn_post_norm, m_l0_ffn_w_up, m_l0_ffn_conv_w, m_l0_ffn_conv_b, m_l0_ffn_w_down, m_l1_mix_pre_norm, m_l1_mix_post_norm, m_l1_w_in, m_l1_ssm_conv_w, m_l1_ssm_conv_b, m_l1_dt_bias, m_l1_a_log, m_l1_d_skip, m_l1_gate_norm, m_l1_w_out, m_l1_ffn_pre_norm, m_l1_ffn_post_norm, m_l1_ffn_w_up, m_l1_ffn_conv_w, m_l1_ffn_conv_b, m_l1_ffn_w_down, v_meta_tokens, v_l0_mix_pre_norm, v_l0_mix_post_norm, v_l0_w_in, v_l0_lru_conv_w, v_l0_lru_conv_b, v_l0_lru_w_a, v_l0_lru_b_a, v_l0_lru_w_x, v_l0_lru_b_x, v_l0_lru_lambda, v_l0_attn_sinks, v_l0_w_out, v_l0_ffn_pre_norm, v_l0_ffn_post_norm, v_l0_ffn_w_up, v_l0_ffn_conv_w, v_l0_ffn_conv_b, v_l0_ffn_w_down, v_l1_mix_pre_norm, v_l1_mix_post_norm, v_l1_w_in, v_l1_ssm_conv_w, v_l1_ssm_conv_b, v_l1_dt_bias, v_l1_a_log, v_l1_d_skip, v_l1_gate_norm, v_l1_w_out, v_l1_ffn_pre_norm, v_l1_ffn_post_norm, v_l1_ffn_w_up, v_l1_ffn_conv_w, v_l1_ffn_conv_b, v_l1_ffn_w_down):
    args = locals()
    W = {n: args[n] for n in _WEIGHTS}
    M = {n: args["m_" + n] for n in _WEIGHTS}
    V = {n: args["v_" + n] for n in _WEIGHTS}
    loss, grad_x, grads, delta, new_m, new_v = _step(x[0], loss_target[0], W, M, V)
    return (loss, grad_x[None], *[grads[n] for n in _WEIGHTS], *[delta[n] for n in _WEIGHTS],
            *[new_m[n] for n in _WEIGHTS], *[new_v[n] for n in _WEIGHTS])
```

```python
import functools
import math

import jax
import jax.numpy as jnp
from jax import lax
from jax.experimental import pallas as pl
from jax.experimental.pallas import tpu as pltpu

F32 = jnp.float32
BF16 = jnp.bfloat16

D_MODEL = 1024
N_META = 16
BLOCK = 128
PAD = BLOCK - N_META
EPS = 1e-6
LRU_BLOCKS = 8
LRU_BS = 128
LRU_C = 8.0
N_Q_HEADS = 16
N_KV_HEADS = 2
HEAD_DIM = 64
Q_PER_KV = 8
WINDOW = 128
D_SSM = 2048
SSD_HEADS = 32
SSD_GROUPS = 8
SSD_HPG = 4
SSD_P = 64
SSD_N = 128
D_FF = 2816
NEG = -1e30
LANES = 128
SUBLANES = 8
_VMEM_LIMIT_WIDE = 62 * 1024 * 1024

ADAM_LR = 0.001
ADAM_B1 = 0.9
ADAM_B2 = 0.999
ADAM_EPS = 1e-08
ADAM_WD = 0.01
ADAM_STEP = 10

MESH = pl.DeviceIdType.MESH
N_CHIPS = 4


def _pick(n, cands):
    for c in cands:
        if n % c == 0:
            return c
    raise ValueError(f"no tile for {n} in {cands}")


def _col_tile(n, limit=1792):
    best = None
    for t in range(LANES, min(n, limit) + 1, LANES):
        if n % t == 0:
            best = t
    if best is None:
        raise ValueError(f"no lane tile for {n}")
    return best


def _sigmoid(x):
    return 0.5 + 0.5 * jnp.tanh(0.5 * x)


def _log1p(e):
    u = 1.0 + e
    return jnp.where(u == 1.0, e, jnp.log(u) * (e / jnp.where(u == 1.0, 1.0, u - 1.0)))


def _softplus(x):
    return jnp.maximum(x, 0.0) + _log1p(jnp.exp(-jnp.abs(x)))


def _neg_expm1(x):
    poly = x * (1.0 + x * (0.5 + x * (1.0 / 6.0 + x * (1.0 / 24.0 + x * (1.0 / 120.0)))))
    return -jnp.where(x > -0.05, poly, jnp.exp(x) - 1.0)


_GELU_C = math.sqrt(2.0 / math.pi)


def _gelu(x):
    u = 0.5 + 0.5 * jnp.tanh(x * (_GELU_C + (_GELU_C * 0.044715) * (x * x)))
    return x * u


def _gelu_and_grad(x):
    x2 = x * x
    u = 0.5 + 0.5 * jnp.tanh(x * (_GELU_C + (_GELU_C * 0.044715) * x2))
    g = x * u
    dg = u * (1.0 + (x - g) * (2.0 * _GELU_C + (6.0 * 0.044715 * _GELU_C) * x2))
    return g, dg


def _silu_and_grad(x):
    s = _sigmoid(x)
    return x * s, s * (1.0 + x * (1.0 - s))


def _dot(a, b):
    return jnp.dot(a, b, preferred_element_type=F32)


def _dot_nt(a, b):
    return lax.dot_general(a, b, (((1,), (1,)), ((), ())), preferred_element_type=F32)


def _dot_tn(a, b):
    return lax.dot_general(a, b, (((0,), (0,)), ((), ())), preferred_element_type=F32)


def _row_iota(t):
    return lax.broadcasted_iota(jnp.int32, (t, 1), 0)


def _scan_fwd(a, u, t):
    row = _row_iota(t)
    d = 1
    while d < t:
        m = row >= d
        u_sh = jnp.where(m, pltpu.roll(u, d, 0), 0.0)
        a_sh = jnp.where(m, pltpu.roll(a, d, 0), 1.0)
        u = u + a * u_sh
        a = a * a_sh
        d *= 2
    return a, u


def _scan_rev(c, x, t):
    row = _row_iota(t)
    d = 1
    while d < t:
        m = row < t - d
        x_sh = jnp.where(m, pltpu.roll(x, t - d, 0), 0.0)
        c_sh = jnp.where(m, pltpu.roll(c, t - d, 0), 1.0)
        x = x + c * x_sh
        c = c * c_sh
        d *= 2
    return c, x


def _cumsum_rows(x, t):
    row = _row_iota(t)
    d = 1
    while d < t:
        x = x + jnp.where(row >= d, pltpu.roll(x, d, 0), 0.0)
        d *= 2
    return x


def _rev_cumsum_rows(x, t):
    row = _row_iota(t)
    d = 1
    while d < t:
        x = x + jnp.where(row < t - d, pltpu.roll(x, t - d, 0), 0.0)
        d *= 2
    return x


def _rms_bwd(x, g, dy):
    rs = lax.rsqrt(jnp.mean(x * x, axis=-1, keepdims=True) + EPS)
    gy = dy * g
    dx = rs * gy - x * (rs * rs * rs) * jnp.mean(x * gy, axis=-1, keepdims=True)
    return dx, dy * x * rs


def _mm_nn(a, w, out_dtype, name):
    parts = a if isinstance(a, (tuple, list)) else (a,)
    m = parts[0].shape[0]
    k, n = w.shape
    tm = _pick(m, (640, 512, 256, 128))
    tn = _col_tile(n)
    offs = [sum(p.shape[1] for p in parts[:i]) for i in range(len(parts))]

    def body(*refs):
        w_ref, o_ref = refs[len(parts)], refs[len(parts) + 1]
        acc = None
        for a_ref, p, off in zip(refs, parts, offs):
            t = _dot(a_ref[...].astype(BF16), w_ref[off:off + p.shape[1], :])
            acc = t if acc is None else acc + t
        o_ref[...] = acc.astype(o_ref.dtype)

    return pl.pallas_call(
        body, name=name, grid=(n // tn, m // tm),
        in_specs=[pl.BlockSpec((tm, p.shape[1]), lambda j, i: (i, 0)) for p in parts]
        + [pl.BlockSpec((k, tn), lambda j, i: (0, j))],
        out_specs=pl.BlockSpec((tm, tn), lambda j, i: (i, j)),
        out_shape=jax.ShapeDtypeStruct((m, n), out_dtype),
        compiler_params=pltpu.CompilerParams(dimension_semantics=("parallel", "parallel")),
    )(*parts, w)


def _mm_nt(dy, w, name, out_dtype=F32):
    m, n = dy.shape
    k = w.shape[0]
    wide = n > 3328
    tm = _pick(m, (320, 256, 128)) if wide else _pick(m, (640, 512, 256, 128))
    tk = _col_tile(k, 512 if wide else 1408)

    def body(dy_ref, w_ref, o_ref):
        o_ref[...] = _dot_nt(dy_ref[...].astype(BF16), w_ref[...]).astype(o_ref.dtype)

    return pl.pallas_call(
        body, name=name, grid=(k // tk, m // tm),
        in_specs=[pl.BlockSpec((tm, n), lambda j, i: (i, 0)),
                  pl.BlockSpec((tk, n), lambda j, i: (j, 0))],
        out_specs=pl.BlockSpec((tm, tk), lambda j, i: (i, j)),
        out_shape=jax.ShapeDtypeStruct((m, k), out_dtype),
        compiler_params=pltpu.CompilerParams(dimension_semantics=("parallel", "parallel")),
    )(dy, w)


def _mm_tn(a, dy, name):
    m, k = a.shape
    n = dy.shape[1]
    tm = _pick(m, (640, 512, 256, 128))
    tk = _col_tile(k, 1408)
    tn = _col_tile(n, 1664)
    nsteps = m // tm

    def body(a_ref, dy_ref, o_ref, acc):
        @pl.when(pl.program_id(2) == 0)
        def _():
            acc[...] = jnp.zeros_like(acc)

        acc[...] += _dot_tn(a_ref[...].astype(BF16), dy_ref[...].astype(BF16))

        @pl.when(pl.program_id(2) == nsteps - 1)
        def _():
            o_ref[...] = acc[...].astype(o_ref.dtype)

    return pl.pallas_call(
        body, name=name, grid=(k // tk, n // tn, nsteps),
        in_specs=[pl.BlockSpec((tm, tk), lambda kk, j, i: (i, kk)),
                  pl.BlockSpec((tm, tn), lambda kk, j, i: (i, j))],
        out_specs=pl.BlockSpec((tk, tn), lambda kk, j, i: (kk, j)),
        out_shape=jax.ShapeDtypeStruct((k, n), BF16),
        scratch_shapes=[pltpu.VMEM((tk, tn), F32)],
        compiler_params=pltpu.CompilerParams(
            dimension_semantics=("parallel", "parallel", "arbitrary")),
    )(a, dy)


def _mm_nn_sh(a, w4, n_out, name):
    m, k = a.shape
    s, _, n = w4.shape
    tm = _pick(m, (320, 256, 128))

    def body(a_ref, w_ref, o_ref):
        av = a_ref[...].astype(BF16)
        for j in range(s):
            o_ref[:, j * n:(j + 1) * n] = _dot(av, w_ref[j])
        if n_out > s * n:
            o_ref[:, s * n:] = jnp.zeros((tm, n_out - s * n), F32)

    return pl.pallas_call(
        body, name=name, grid=(m // tm,),
        in_specs=[pl.BlockSpec((tm, k), lambda i: (i, 0)),
                  pl.BlockSpec((s, k, n), lambda i: (0, 0, 0))],
        out_specs=pl.BlockSpec((tm, n_out), lambda i: (i, 0)),
        out_shape=jax.ShapeDtypeStruct((m, n_out), F32),
        compiler_params=pltpu.CompilerParams(dimension_semantics=("parallel",)),
    )(a, w4)


def _mm_nt_sh(dy, w4, name, plan=None):
    dys = dy if isinstance(dy, (tuple, list)) else (dy,)
    m = dys[0].shape[0]
    s, k, n = w4.shape
    tm = _pick(m, (640, 512, 256, 128))
    tk = _col_tile(k, 1024)
    where = _shard_columns(dys, s, n)
    p_in, p_shapes, p_out, p_scr = _plan_parts(plan)
    nd, nj, ni = len(dys), k // tk, m // tm

    def body(*refs):
        w_ref = refs[nd]
        cins = refs[nd + 1:nd + 1 + len(p_in)]
        o_ref = refs[nd + 1 + len(p_in)]
        couts = refs[nd + 2 + len(p_in):nd + 2 + len(p_in) + len(p_out)]
        sems = refs[nd + 2 + len(p_in) + len(p_out):]
        step = pl.program_id(0) * ni + pl.program_id(1)
        if plan is not None:
            @pl.when(step == 0)
            def _():
                plan.start(cins, couts, sems)

        acc = None
        for j, (p, c0) in enumerate(where):
            t = _dot_nt(refs[p][:, c0:c0 + n].astype(BF16), w_ref[j])
            acc = t if acc is None else acc + t
        o_ref[...] = acc
        if plan is not None:
            @pl.when(step == nj * ni - 1)
            def _():
                plan.wait(cins, couts, sems)

    sem = ("arbitrary", "arbitrary") if plan is not None else ("parallel", "parallel")
    res = pl.pallas_call(
        body, name=name, grid=(nj, ni),
        in_specs=[pl.BlockSpec((tm, d.shape[1]), lambda j, i: (i, 0)) for d in dys]
        + [pl.BlockSpec((s, tk, n), lambda j, i: (0, j, 0))] + p_in,
        out_specs=[pl.BlockSpec((tm, tk), lambda j, i: (i, j))] + p_out,
        out_shape=[jax.ShapeDtypeStruct((m, k), F32)] + p_shapes,
        scratch_shapes=p_scr,
        compiler_params=pltpu.CompilerParams(dimension_semantics=sem),
    )(*dys, w4, *(plan.ins if plan is not None else []))
    return res[0] if plan is None else (res[0], res[1:])


def _shard_columns(dys, s, n):
    where = []
    for p, d in enumerate(dys):
        where += [(p, c * n) for c in range(d.shape[1] // n)]
    assert len(where) >= s
    return where[:s]


def _mm_tn_sh(a, dy, n, name):
    dys = dy if isinstance(dy, (tuple, list)) else (dy,)
    m, k = a.shape
    s = N_CHIPS
    tm = _pick(m, (640, 512, 256, 128))
    tk = _col_tile(k, 512)
    nsteps = m // tm
    where = _shard_columns(dys, s, n)

    def body(*refs):
        a_ref, o_ref, acc = refs[0], refs[len(dys) + 1], refs[len(dys) + 2]

        @pl.when(pl.program_id(1) == 0)
        def _():
            acc[...] = jnp.zeros_like(acc)

        av = a_ref[...].astype(BF16)
        for j, (p, c0) in enumerate(where):
            acc[j] += _dot_tn(av, refs[1 + p][:, c0:c0 + n].astype(BF16))

        @pl.when(pl.program_id(1) == nsteps - 1)
        def _():
            o_ref[...] = acc[...].astype(o_ref.dtype)

    return pl.pallas_call(
        body, name=name, grid=(k // tk, nsteps),
        in_specs=[pl.BlockSpec((tm, tk), lambda kk, i: (i, kk))]
        + [pl.BlockSpec((tm, d.shape[1]), lambda kk, i: (i, 0)) for d in dys],
        out_specs=pl.BlockSpec((s, tk, n), lambda kk, i: (0, kk, 0)),
        out_shape=jax.ShapeDtypeStruct((s, k, n), BF16),
        scratch_shapes=[pltpu.VMEM((s, tk, n), F32)],
        compiler_params=pltpu.CompilerParams(dimension_semantics=("parallel", "arbitrary"),
                                             vmem_limit_bytes=_VMEM_LIMIT_WIDE),
    )(a, *dys)


def _rowcall(name, body, lp, tm, rows=(), prevs=(), vecs=(), outs=(), accs=(), scratch=(),
             reverse=False, seq=False, plan=None):
    p_in, p_shapes, p_out, p_scr = _plan_parts(plan)
    nt = lp // tm
    hb = tm // SUBLANES

    def ri(i):
        return nt - 1 - i if reverse else i

    in_specs, args = [], []
    for arr, w, cb in rows:
        in_specs.append(pl.BlockSpec((tm, w), lambda i, cb=cb: (ri(i), cb)))
        args.append(arr)
    for arr, w, cb in prevs:
        in_specs.append(pl.BlockSpec((SUBLANES, w), lambda i, cb=cb: (jnp.maximum(ri(i) * hb - 1, 0), cb)))
        args.append(arr)
    for arr in vecs:
        in_specs.append(pl.BlockSpec(arr.shape, lambda i, nd=arr.ndim: (0,) * nd))
        args.append(arr)
    out_shape, out_specs = [], []
    for w, dt in outs:
        out_shape.append(jax.ShapeDtypeStruct((lp, w), dt))
        out_specs.append(pl.BlockSpec((tm, w), lambda i: (ri(i), 0)))
    for shp, dt in accs:
        out_shape.append(jax.ShapeDtypeStruct(shp, dt))
        out_specs.append(pl.BlockSpec(shp, lambda i, nd=len(shp): (0,) * nd))

    n_in, n_out, n_scr = len(args), len(out_shape), len(scratch)

    def kern(*refs):
        i = pl.program_id(0)
        own = (refs[:n_in] + refs[n_in + len(p_in):n_in + len(p_in) + n_out]
               + refs[n_in + len(p_in) + n_out + len(p_out):n_in + len(p_in) + n_out + len(p_out) + n_scr])
        cins = refs[n_in:n_in + len(p_in)]
        couts = refs[n_in + len(p_in) + n_out:n_in + len(p_in) + n_out + len(p_out)]
        sems = refs[n_in + len(p_in) + n_out + len(p_out) + n_scr:]
        if plan is not None:
            @pl.when(i == 0)
            def _():
                plan.start(cins, couts, sems)

        body(ri(i), i == 0, *own)
        if plan is not None:
            @pl.when(i == nt - 1)
            def _():
                plan.wait(cins, couts, sems)

    sem = ("arbitrary",) if (seq or accs or plan is not None) else ("parallel",)
    res = pl.pallas_call(
        kern, name=name, grid=(nt,), in_specs=in_specs + p_in, out_specs=out_specs + p_out,
        out_shape=out_shape + p_shapes, scratch_shapes=list(scratch) + p_scr,
        compiler_params=pltpu.CompilerParams(dimension_semantics=sem),
    )(*args, *(plan.ins if plan is not None else []))
    return res if plan is None else (res[:n_out], res[n_out:])


def _acc_add(first, ref, val):
    @pl.when(first)
    def _():
        ref[...] = jnp.zeros_like(ref)

    ref[...] += val


def _real_rows(r, tm):
    return (r * tm + _row_iota(tm)) >= PAD


def _rmsnorm_fwd(h, g, name):
    lp, d = h.shape
    tm = _pick(lp, (640, 512, 256, 128))

    def body(r, first, h_ref, g_ref, u_ref):
        x = h_ref[...]
        rs = lax.rsqrt(jnp.mean(x * x, axis=-1, keepdims=True) + EPS)
        u_ref[...] = (x * rs * g_ref[...]).astype(u_ref.dtype)

    return _rowcall(name, body, lp, tm, rows=[(h, d, 0)], vecs=[g], outs=[(d, BF16)])[0]


def _postnorm_res_fwd(h, o, g, name, next_pre=None):
    lp, d = h.shape
    tm = _pick(lp, (640, 512, 256, 128))

    def body(r, first, h_ref, o_ref, g_ref, *rest):
        x = o_ref[...]
        rs = lax.rsqrt(jnp.mean(x * x, axis=-1, keepdims=True) + EPS)
        hn = jnp.where(_real_rows(r, tm), h_ref[...] + x * rs * g_ref[...], 0.0)
        rest[-2 if next_pre is not None else -1][...] = hn
        if next_pre is not None:
            rs2 = lax.rsqrt(jnp.mean(hn * hn, axis=-1, keepdims=True) + EPS)
            rest[-1][...] = (hn * rs2 * rest[0][...]).astype(BF16)

    if next_pre is None:
        return _rowcall(name, body, lp, tm, rows=[(h, d, 0), (o, d, 0)], vecs=[g], outs=[(d, F32)])[0]
    return _rowcall(name, body, lp, tm, rows=[(h, d, 0), (o, d, 0)], vecs=[g, next_pre],
                    outs=[(d, F32), (d, BF16)])


def _postnorm_bwd(o, g, dh, name):
    lp, d = o.shape
    tm = _pick(lp, (640, 512, 256, 128))

    def body(r, first, o_ref, dh_ref, g_ref, do_ref, dg_ref):
        dx, dgt = _rms_bwd(o_ref[...], g_ref[...], dh_ref[...])
        do_ref[...] = dx.astype(do_ref.dtype)
        _acc_add(first, dg_ref, jnp.sum(dgt, axis=0, keepdims=True))

    return _rowcall(name, body, lp, tm, rows=[(o, d, 0), (dh, d, 0)], vecs=[g],
                    outs=[(d, BF16)], accs=[((1, d), F32)])


def _prenorm_bwd(h, g, du, dh_res, name):
    lp, d = h.shape
    tm = _pick(lp, (640, 512, 256, 128))

    def body(r, first, h_ref, du_ref, dres_ref, g_ref, dh_ref, dg_ref):
        dx, dgt = _rms_bwd(h_ref[...], g_ref[...], du_ref[...])
        dh_ref[...] = jnp.where(_real_rows(r, tm), dres_ref[...] + dx, 0.0)
        _acc_add(first, dg_ref, jnp.sum(dgt, axis=0, keepdims=True))

    return _rowcall(name, body, lp, tm, rows=[(h, d, 0), (du, d, 0), (dh_res, d, 0)], vecs=[g],
                    outs=[(d, F32)], accs=[((1, d), F32)])


def _loss_fwd_bwd(h, tgt, name):
    lp, d = h.shape
    tm = _pick(lp, (640, 512, 256, 128))

    def body(r, first, h_ref, t_ref, dh_ref, ls_ref):
        tok = (r * tm + _row_iota(tm)) >= BLOCK
        e = jnp.where(tok, h_ref[...] - t_ref[...], 0.0)
        dh_ref[...] = e * (1.0 / d)
        _acc_add(first, ls_ref, jnp.sum(e * e, axis=0, keepdims=True))

    return _rowcall(name, body, lp, tm, rows=[(h, d, 0), (tgt, d, 0)],
                    outs=[(d, F32)], accs=[((1, d), F32)])


def _conv_tiles(lp, width):
    wc = _col_tile(width, 1408)
    tm = _pick(lp, (320, 256, 128))
    return tm, wc


def _conv_fwd(x, col_off, width, w, b, name):
    lp = x.shape[0]
    kk = w.shape[0]
    tm, wc = _conv_tiles(lp, width)
    offb = col_off // wc
    assert col_off % wc == 0
    hb = tm // SUBLANES

    def body(x_ref, xp_ref, w_ref, b_ref, y_ref):
        i = pl.program_id(1)
        xv = x_ref[...]
        halo = jnp.where(i > 0, xp_ref[...], 0.0)
        xx = jnp.concatenate([halo, xv], axis=0)
        acc = b_ref[...] + w_ref[kk - 1:kk, :] * xv
        for j in range(1, kk):
            acc = acc + w_ref[kk - 1 - j:kk - j, :] * pltpu.roll(xx, j, 0)[SUBLANES:, :]
        y_ref[...] = acc

    return pl.pallas_call(
        body, name=name, grid=(width // wc, lp // tm),
        in_specs=[pl.BlockSpec((tm, wc), lambda j, i: (i, offb + j)),
                  pl.BlockSpec((SUBLANES, wc), lambda j, i: (jnp.maximum(i * hb - 1, 0), offb + j)),
                  pl.BlockSpec((kk, wc), lambda j, i: (0, j)),
                  pl.BlockSpec((1, wc), lambda j, i: (0, j))],
        out_specs=pl.BlockSpec((tm, wc), lambda j, i: (i, j)),
        out_shape=jax.ShapeDtypeStruct((lp, width), F32),
        compiler_params=pltpu.CompilerParams(dimension_semantics=("parallel", "parallel")),
    )(x, x, w, b)


def _conv_bwd(x, col_off, width, dy, w, name, w_col_off=0):
    lp = x.shape[0]
    kk = w.shape[0]
    tm, wc = _conv_tiles(lp, width)
    offb = col_off // wc
    woffb = w_col_off // wc
    assert col_off % wc == 0 and w_col_off % wc == 0
    hrows = SUBLANES * (4 // dy.dtype.itemsize)
    ext = tm + hrows

    def body(x_ref, dy_ref, dn_ref, w_ref, dx_ref, dw_ref, db_ref):
        i = pl.program_id(1)
        last = pl.num_programs(1) - 1
        xv = x_ref[...]
        dyv = dy_ref[...].astype(F32)
        dd = jnp.concatenate([dyv, jnp.where(i < last, dn_ref[...].astype(F32), 0.0)], axis=0)
        dx = w_ref[kk - 1:kk, :] * dyv
        rows = [jnp.sum(dyv * xv, axis=0, keepdims=True)]
        for m in range(1, kk):
            ahead = pltpu.roll(dd, ext - m, 0)[:tm, :]
            dx = dx + w_ref[kk - 1 - m:kk - m, :] * ahead
            rows.append(jnp.sum(ahead * xv, axis=0, keepdims=True))
        dx_ref[...] = dx.astype(dx_ref.dtype)
        dwp = jnp.concatenate(rows[::-1] + [jnp.zeros((SUBLANES - kk, wc), F32)], axis=0)

        @pl.when(i == 0)
        def _():
            dw_ref[...] = jnp.zeros_like(dw_ref)
            db_ref[...] = jnp.zeros_like(db_ref)

        dw_ref[...] += dwp
        db_ref[...] += jnp.sum(dyv, axis=0, keepdims=True)

    return pl.pallas_call(
        body, name=name, grid=(width // wc, lp // tm),
        in_specs=[pl.BlockSpec((tm, wc), lambda j, i: (i, offb + j)),
                  pl.BlockSpec((tm, wc), lambda j, i: (i, j)),
                  pl.BlockSpec((hrows, wc), lambda j, i: (jnp.minimum((i + 1) * (tm // hrows), lp // hrows - 1), j)),
                  pl.BlockSpec((kk, wc), lambda j, i: (0, woffb + j))],
        out_specs=[pl.BlockSpec((tm, wc), lambda j, i: (i, j)),
                   pl.BlockSpec((SUBLANES, wc), lambda j, i: (0, j)),
                   pl.BlockSpec((1, wc), lambda j, i: (0, j))],
        out_shape=[jax.ShapeDtypeStruct((lp, width), BF16),
                   jax.ShapeDtypeStruct((SUBLANES, width), F32),
                   jax.ShapeDtypeStruct((1, width), F32)],
        compiler_params=pltpu.CompilerParams(dimension_semantics=("parallel", "arbitrary")),
    )(x, dy, dy, w)


_FFN_K = 3
_FFN_WC = 1408


def _conv3_ext(x_ext, w_ref, b_ref):
    return (b_ref[...] + w_ref[2:3, :] * x_ext + w_ref[1:2, :] * pltpu.roll(x_ext, 1, 0)
            + w_ref[0:1, :] * pltpu.roll(x_ext, 2, 0))


def _ffn_convact_fwd(hp, cw, cb, name, plan=None):
    lp = hp.shape[0]
    tm = _pick(lp, (320, 256, 128))
    wc = _FFN_WC
    nj = D_FF // wc
    ni = lp // tm
    hb = tm // SUBLANES
    p_in, p_shapes, p_out, p_scr = _plan_parts(plan)

    def body(*refs):
        g_ref, gp_ref, u_ref, up_ref, wg_ref, wu_ref, bg_ref, bu_ref = refs[:8]
        cins = refs[8:8 + len(p_in)]
        a_ref, hg_ref, hu_ref = refs[8 + len(p_in):11 + len(p_in)]
        couts = refs[11 + len(p_in):11 + len(p_in) + len(p_out)]
        sems = refs[11 + len(p_in) + len(p_out):]
        i = pl.program_id(1)
        step = pl.program_id(0) * ni + i
        if plan is not None:
            @pl.when(step == 0)
            def _():
                plan.start(cins, couts, sems)

        def conv(x_ref, p_ref, w_ref, b_ref):
            x_ext = jnp.concatenate([jnp.where(i > 0, p_ref[...], 0.0), x_ref[...]], axis=0)
            return _conv3_ext(x_ext, w_ref, b_ref)[SUBLANES:, :]

        hg = conv(g_ref, gp_ref, wg_ref, bg_ref)
        hu = conv(u_ref, up_ref, wu_ref, bu_ref)
        a_ref[...] = (_gelu(hg) * hu).astype(a_ref.dtype)
        hg_ref[...] = hg.astype(hg_ref.dtype)
        hu_ref[...] = hu.astype(hu_ref.dtype)
        if plan is not None:
            @pl.when(step == nj * ni - 1)
            def _():
                plan.wait(cins, couts, sems)

    tile = lambda off: pl.BlockSpec((tm, wc), lambda j, i: (i, off + j))
    prev = lambda off: pl.BlockSpec((SUBLANES, wc), lambda j, i: (jnp.maximum(i * hb - 1, 0), off + j))
    vec = lambda rows, off: pl.BlockSpec((rows, wc), lambda j, i: (0, off + j))
    sem = ("arbitrary", "arbitrary") if plan is not None else ("parallel", "parallel")
    res = pl.pallas_call(
        body, name=name, grid=(nj, ni),
        in_specs=[tile(0), prev(0), tile(nj), prev(nj), vec(_FFN_K, 0), vec(_FFN_K, nj), vec(1, 0),
                  vec(1, nj)] + p_in,
        out_specs=[tile(0)] * 3 + p_out,
        out_shape=[jax.ShapeDtypeStruct((lp, D_FF), BF16)] * 3 + p_shapes,
        scratch_shapes=p_scr,
        compiler_params=pltpu.CompilerParams(dimension_semantics=sem),
    )(hp, hp, hp, hp, cw, cw, cb, cb, *(plan.ins if plan is not None else []))
    return res if plan is None else (res[:3], res[3:])


def _ffn_act_bwd(hg, hu, dact, name):
    lp = hg.shape[0]
    tm = _pick(lp, (320, 256, 128))

    def body(r, first, g_ref, u_ref, da_ref, dg_ref, du_ref):
        gl, dgl = _gelu_and_grad(g_ref[...].astype(F32))
        da = da_ref[...].astype(F32)
        dg_ref[...] = (da * u_ref[...].astype(F32) * dgl).astype(dg_ref.dtype)
        du_ref[...] = (da * gl).astype(du_ref.dtype)

    return _rowcall(name, body, lp, tm, rows=[(hg, D_FF, 0), (hu, D_FF, 0), (dact, D_FF, 0)],
                    outs=[(D_FF, BF16), (D_FF, BF16)])


def _lru_gates(x, wa_ref, wx_ref, ba, bx, lam):
    xb = x.astype(BF16)
    za, zx = [], []
    for n in range(LRU_BLOCKS):
        xs = xb[:, n * LRU_BS:(n + 1) * LRU_BS]
        za.append(_dot(xs, wa_ref[n]))
        zx.append(_dot(xs, wx_ref[n]))
    r = _sigmoid(jnp.concatenate(za, axis=1) + ba)
    ig = _sigmoid(jnp.concatenate(zx, axis=1) + bx)
    sp = _softplus(-lam)
    log_a = -LRU_C * r * sp
    a = jnp.exp(log_a)
    om = _neg_expm1(2.0 * log_a)
    mult = jnp.sqrt(om)
    return xb, r, ig, sp, a, om, mult


def _lru_fwd(proj, xrc, wa, wx, ba, bx, lam, name, plan=None):
    lp, d = xrc.shape
    tm = BLOCK

    def body(r_idx, first, gate_ref, x_ref, wa_ref, wx_ref, ba_ref, bx_ref, lam_ref,
             y_ref, h_ref, carry):
        @pl.when(first)
        def _():
            carry[...] = jnp.zeros_like(carry)

        x = x_ref[...]
        _, _, ig, _, a, _, mult = _lru_gates(x, wa_ref, wx_ref, ba_ref[...], bx_ref[...], lam_ref[...])
        u = jnp.where(_real_rows(r_idx, tm), mult * ig * x, 0.0)
        acum, hloc = _scan_fwd(a, u, tm)
        h = hloc + acum * carry[0:1, :]
        h_ref[...] = h
        carry[0:1, :] = h[tm - 1:tm, :]
        y_ref[...] = (_gelu(gate_ref[...]) * h).astype(y_ref.dtype)

    return _rowcall(name, body, lp, tm, rows=[(proj, d, 0), (xrc, d, 0)],
                    vecs=[wa, wx, ba, bx, lam], outs=[(d, BF16), (d, F32)],
                    scratch=[pltpu.VMEM((SUBLANES, d), F32)], seq=True, plan=plan)


def _lru_bwd(proj, xrc, hl, dmix, wa, wx, ba, bx, lam, name, plan=None):
    lp, d = xrc.shape
    tm = BLOCK

    def body(r_idx, first, gate_ref, x_ref, h_ref, dy_ref, hp_ref, wa_ref, wx_ref, ba_ref, bx_ref,
             lam_ref, dgate_ref, dx_ref, dwa_ref, dwx_ref, dba_ref, dbx_ref, dlam_ref, carry):
        @pl.when(first)
        def _():
            carry[...] = jnp.zeros_like(carry)
            dwa_ref[...] = jnp.zeros_like(dwa_ref)
            dwx_ref[...] = jnp.zeros_like(dwx_ref)
            dba_ref[...] = jnp.zeros_like(dba_ref)
            dbx_ref[...] = jnp.zeros_like(dbx_ref)
            dlam_ref[...] = jnp.zeros_like(dlam_ref)

        x = x_ref[...]
        lam = lam_ref[...]
        xb, r, ig, sp, a, om, mult = _lru_gates(x, wa_ref, wx_ref, ba_ref[...], bx_ref[...], lam)
        h = h_ref[...]
        dy = dy_ref[...]
        gl, dgl = _gelu_and_grad(gate_ref[...])
        dgate_ref[...] = (dy * h * dgl).astype(dgate_ref.dtype)
        row = _row_iota(tm)
        lastrow = row == tm - 1
        xg = dy * gl + jnp.where(lastrow, carry[0:1, :], 0.0)
        c = jnp.where(lastrow, 1.0, pltpu.roll(a, tm - 1, 0))
        _, g = _scan_rev(c, xg, tm)
        carry[0:1, :] = a[0:1, :] * g[0:1, :]
        hprev_in = jnp.where(r_idx > 0, hp_ref[SUBLANES - 1:SUBLANES, :], 0.0)
        hprev = jnp.where(row == 0, hprev_in, pltpu.roll(h, 1, 0))
        du = jnp.where(_real_rows(r_idx, tm), g, 0.0)
        da = g * hprev
        dmult = du * ig * x
        dig = du * mult * x
        dxv = du * mult * ig
        e2 = 1.0 - om
        dlog_a = da * a - dmult * e2 / mult
        dr = dlog_a * (-LRU_C) * sp
        dsp = jnp.sum(dlog_a * (-LRU_C) * r, axis=0, keepdims=True)
        dlam_ref[...] += -dsp * _sigmoid(-lam)
        dza = dr * r * (1.0 - r)
        dzx = dig * ig * (1.0 - ig)
        dba_ref[...] += jnp.sum(dza, axis=0, keepdims=True)
        dbx_ref[...] += jnp.sum(dzx, axis=0, keepdims=True)
        dzab = dza.astype(BF16)
        dzxb = dzx.astype(BF16)
        parts = []
        for n in range(LRU_BLOCKS):
            sl = slice(n * LRU_BS, (n + 1) * LRU_BS)
            dwa_ref[n] += _dot_tn(xb[:, sl], dzab[:, sl])
            dwx_ref[n] += _dot_tn(xb[:, sl], dzxb[:, sl])
            parts.append(_dot_nt(dzab[:, sl], wa_ref[n]) + _dot_nt(dzxb[:, sl], wx_ref[n]))
        dx_ref[...] = dxv + jnp.concatenate(parts, axis=1)

    return _rowcall(name, body, lp, tm,
                    rows=[(proj, d, 0), (xrc, d, 0), (hl, d, 0), (dmix, d, 0)],
                    prevs=[(hl, d, 0)], vecs=[wa, wx, ba, bx, lam],
                    outs=[(d, BF16), (d, F32)],
                    accs=[((LRU_BLOCKS, LRU_BS, LRU_BS), F32), ((LRU_BLOCKS, LRU_BS, LRU_BS), F32),
                          ((1, d), F32), ((1, d), F32), ((1, d), F32)],
                    scratch=[pltpu.VMEM((SUBLANES, d), F32)], reverse=True, seq=True, plan=plan)


_SLOPES = [2.0 ** (-8.0 * (h + 1) / N_Q_HEADS) for h in range(N_Q_HEADS)]
_QK_SCALE = HEAD_DIM ** -0.5
_QCOL = 2 * D_MODEL // D_MODEL
_KCOL = (3 * D_MODEL) // LANES
_VCOL = _KCOL + 1


def _attn_masks(n):
    start = pl.multiple_of(jnp.maximum(n - 1, 0) * BLOCK, BLOCK)
    qi = n * BLOCK + lax.broadcasted_iota(jnp.int32, (BLOCK, 2 * BLOCK), 0)
    kj = start + lax.broadcasted_iota(jnp.int32, (BLOCK, 2 * BLOCK), 1)
    dist = qi - kj
    ok = (kj >= BLOCK) & (dist >= 0) & (dist < WINDOW)
    dm = (n * BLOCK - PAD + lax.broadcasted_iota(jnp.int32, (BLOCK, N_META), 0)
          - lax.broadcasted_iota(jnp.int32, (BLOCK, N_META), 1))
    okm = dm >= 0
    return start, ok, dist.astype(F32), okm, jnp.minimum(dm, WINDOW).astype(F32)


def _group_rows(ref, g, scale=None):
    x = jnp.concatenate(
        [ref[:, (g * Q_PER_KV + hh) * HEAD_DIM:(g * Q_PER_KV + hh + 1) * HEAD_DIM] for hh in range(Q_PER_KV)],
        axis=0)
    return (x if scale is None else x * scale).astype(BF16)


def _attn_probs(s, sm, sink_ref, g, ok, distf, okm, dmf):
    slope = jnp.stack([jnp.full((1, 1), _SLOPES[g * Q_PER_KV + hh], F32) for hh in range(Q_PER_KV)])
    sink = jnp.stack([sink_ref[0:1, g * Q_PER_KV + hh:g * Q_PER_KV + hh + 1] for hh in range(Q_PER_KV)])
    s = s.reshape(Q_PER_KV, BLOCK, 2 * BLOCK)
    sm = sm.reshape(Q_PER_KV, BLOCK, N_META)
    s = jnp.where(ok[None], s - slope * distf[None], NEG)
    sm = jnp.where(okm[None], sm - slope * dmf[None], NEG)
    mx = jnp.maximum(jnp.maximum(jnp.max(s, axis=-1, keepdims=True),
                                 jnp.max(sm, axis=-1, keepdims=True)), sink)
    p = jnp.exp(s - mx)
    pm = jnp.exp(sm - mx)
    ps = jnp.exp(sink - mx)
    inv = 1.0 / (jnp.sum(p, axis=-1, keepdims=True) + jnp.sum(pm, axis=-1, keepdims=True) + ps)
    return p, pm, ps, inv


def _attn_fwd(proj, sinks, name, plan=None):
    lp = proj.shape[0]
    nblk = lp // BLOCK
    p_in, p_shapes, p_out, p_scr = _plan_parts(plan)

    def body(*refs):
        q_ref, k_ref, v_ref, sink_ref = refs[:4]
        cins = refs[4:4 + len(p_in)]
        o_ref = refs[4 + len(p_in)]
        couts = refs[5 + len(p_in):5 + len(p_in) + len(p_out)]
        sems = refs[5 + len(p_in) + len(p_out):]
        n = pl.program_id(0)
        if plan is not None:
            @pl.when(n == 0)
            def _():
                plan.start(cins, couts, sems)

        start, ok, distf, okm, dmf = _attn_masks(n)
        kb = k_ref[pl.ds(start, 2 * BLOCK), :].astype(BF16)
        vb = v_ref[pl.ds(start, 2 * BLOCK), :].astype(BF16)
        km = k_ref[PAD:BLOCK, :].astype(BF16)
        vm = v_ref[PAD:BLOCK, :].astype(BF16)
        rows = Q_PER_KV * BLOCK
        gsl = [slice(g * HEAD_DIM, (g + 1) * HEAD_DIM) for g in range(N_KV_HEADS)]
        raw = []
        for g in range(N_KV_HEADS):
            qg = _group_rows(q_ref, g, _QK_SCALE)
            raw.append((_dot_nt(qg, kb[:, gsl[g]]), _dot_nt(qg, km[:, gsl[g]])))
        for g in range(N_KV_HEADS):
            gs = gsl[g]
            p, pm, _, inv = _attn_probs(raw[g][0], raw[g][1], sink_ref, g, ok, distf, okm, dmf)
            o = (_dot(p.astype(BF16).reshape(rows, 2 * BLOCK), vb[:, gs])
                 + _dot(pm.astype(BF16).reshape(rows, N_META), vm[:, gs])) * inv.reshape(rows, 1)
            for hh in range(Q_PER_KV):
                h = g * Q_PER_KV + hh
                o_ref[:, h * HEAD_DIM:(h + 1) * HEAD_DIM] = o[hh * BLOCK:(hh + 1) * BLOCK, :].astype(o_ref.dtype)
        if plan is not None:
            @pl.when(n == nblk - 1)
            def _():
                plan.wait(cins, couts, sems)

    res = pl.pallas_call(
        body, name=name, grid=(nblk,),
        in_specs=[pl.BlockSpec((BLOCK, D_MODEL), lambda n: (n, _QCOL)),
                  pl.BlockSpec((lp, LANES), lambda n: (0, _KCOL)),
                  pl.BlockSpec((lp, LANES), lambda n: (0, _VCOL)),
                  pl.BlockSpec(sinks.shape, lambda n: (0, 0))] + p_in,
        out_specs=[pl.BlockSpec((BLOCK, D_MODEL), lambda n: (n, 0))] + p_out,
        out_shape=[jax.ShapeDtypeStruct((lp, D_MODEL), BF16)] + p_shapes,
        scratch_shapes=p_scr,
        compiler_params=pltpu.CompilerParams(dimension_semantics=("arbitrary",)),
    )(proj, proj, proj, sinks, *(plan.ins if plan is not None else []))
    return res[0], res[1:]


def _attn_bwd(proj, sinks, dmix, name, plan=None):
    lp = proj.shape[0]
    nblk = lp // BLOCK

    p_in, p_shapes, p_out, p_scr = _plan_parts(plan)

    def body(*refs):
        q_ref, k_ref, v_ref, sink_ref, dy_ref = refs[:5]
        cins = refs[5:5 + len(p_in)]
        dq_ref, dk_ref, dv_ref, ds_ref = refs[5 + len(p_in):9 + len(p_in)]
        couts = refs[9 + len(p_in):9 + len(p_in) + len(p_out)]
        sems = refs[9 + len(p_in) + len(p_out):]
        n = pl.program_id(0)

        @pl.when(n == 0)
        def _():
            dk_ref[...] = jnp.zeros_like(dk_ref)
            dv_ref[...] = jnp.zeros_like(dv_ref)
            ds_ref[...] = jnp.zeros_like(ds_ref)
            if plan is not None:
                plan.start(cins, couts, sems)

        start, ok, distf, okm, dmf = _attn_masks(n)
        kb = k_ref[pl.ds(start, 2 * BLOCK), :].astype(BF16)
        vb = v_ref[pl.ds(start, 2 * BLOCK), :].astype(BF16)
        km = k_ref[PAD:BLOCK, :].astype(BF16)
        vm = v_ref[PAD:BLOCK, :].astype(BF16)
        lane16 = lax.broadcasted_iota(jnp.int32, (1, N_Q_HEADS), 1)
        dsink = jnp.zeros((1, N_Q_HEADS), F32)
        rows = Q_PER_KV * BLOCK
        gsl = [slice(g * HEAD_DIM, (g + 1) * HEAD_DIM) for g in range(N_KV_HEADS)]
        pre = []
        for g in range(N_KV_HEADS):
            qg = _group_rows(q_ref, g, _QK_SCALE)
            dog = _group_rows(dy_ref, g)
            pre.append((qg, dog, _dot_nt(qg, kb[:, gsl[g]]), _dot_nt(qg, km[:, gsl[g]]),
                        _dot_nt(dog, vb[:, gsl[g]]), _dot_nt(dog, vm[:, gsl[g]])))
        for g in range(N_KV_HEADS):
            gs = gsl[g]
            qg, dog, s_raw, sm_raw, dp, dpm = pre[g]
            p, pm, ps, inv = _attn_probs(s_raw, sm_raw, sink_ref, g, ok, distf, okm, dmf)
            pn, pmn, psn = p * inv, pm * inv, ps * inv
            dp = dp.reshape(Q_PER_KV, BLOCK, 2 * BLOCK)
            dpm = dpm.reshape(Q_PER_KV, BLOCK, N_META)
            delta = (jnp.sum(pn * dp, axis=-1, keepdims=True)
                     + jnp.sum(pmn * dpm, axis=-1, keepdims=True))
            dsb = (pn * (dp - delta)).astype(BF16).reshape(rows, 2 * BLOCK)
            dsm = (pmn * (dpm - delta)).astype(BF16).reshape(rows, N_META)
            dsk = jnp.sum(psn * delta, axis=1, keepdims=True)
            for hh in range(Q_PER_KV):
                dsink = dsink - jnp.where(lane16 == g * Q_PER_KV + hh, dsk[hh], 0.0)
            dq = (_dot(dsb, kb[:, gs]) + _dot(dsm, km[:, gs])) * _QK_SCALE
            for hh in range(Q_PER_KV):
                h = g * Q_PER_KV + hh
                dq_ref[:, h * HEAD_DIM:(h + 1) * HEAD_DIM] = dq[hh * BLOCK:(hh + 1) * BLOCK, :].astype(dq_ref.dtype)
            pnb = pn.astype(BF16).reshape(rows, 2 * BLOCK)
            pmnb = pmn.astype(BF16).reshape(rows, N_META)
            dk_ref[pl.ds(start, 2 * BLOCK), gs] += _dot_tn(dsb, qg)
            dv_ref[pl.ds(start, 2 * BLOCK), gs] += _dot_tn(pnb, dog)
            dk_ref[PAD:BLOCK, gs] += _dot_tn(dsm, qg)
            dv_ref[PAD:BLOCK, gs] += _dot_tn(pmnb, dog)
        ds_ref[...] += dsink
        if plan is not None:
            @pl.when(n == nblk - 1)
            def _():
                plan.wait(cins, couts, sems)

    res = pl.pallas_call(
        body, name=name, grid=(nblk,),
        in_specs=[pl.BlockSpec((BLOCK, D_MODEL), lambda n: (n, _QCOL)),
                  pl.BlockSpec((lp, LANES), lambda n: (0, _KCOL)),
                  pl.BlockSpec((lp, LANES), lambda n: (0, _VCOL)),
                  pl.BlockSpec(sinks.shape, lambda n: (0, 0)),
                  pl.BlockSpec((BLOCK, D_MODEL), lambda n: (n, 1))] + p_in,
        out_specs=[pl.BlockSpec((BLOCK, D_MODEL), lambda n: (n, 0)),
                   pl.BlockSpec((lp, LANES), lambda n: (0, 0)),
                   pl.BlockSpec((lp, LANES), lambda n: (0, 0)),
                   pl.BlockSpec((1, N_Q_HEADS), lambda n: (0, 0))] + p_out,
        out_shape=[jax.ShapeDtypeStruct((lp, D_MODEL), BF16),
                   jax.ShapeDtypeStruct((lp, LANES), F32),
                   jax.ShapeDtypeStruct((lp, LANES), F32),
                   jax.ShapeDtypeStruct((1, N_Q_HEADS), F32)] + p_shapes,
        scratch_shapes=p_scr,
        compiler_params=pltpu.CompilerParams(dimension_semantics=("arbitrary",)),
    )(proj, proj, proj, sinks, dmix, *(plan.ins if plan is not None else []))
    return res[:4], res[4:]


_ZW = D_SSM
_XBC_W = D_SSM + 2 * SSD_GROUPS * SSD_N
_DT_COL = (_ZW + _XBC_W) // LANES
EVEN_IN = 3 * D_MODEL + 2 * LANES
ODD_IN = _ZW + _XBC_W + SSD_HEADS
ODD_IN_PAD = _ZW + _XBC_W + LANES


def _ssm_convprep_fwd(proj, cw, cb, dt_bias, name):
    lp = proj.shape[0]
    kk = cw.shape[0]
    tm, wc = _conv_tiles(lp, _XBC_W)
    offb = _ZW // wc
    nj = _XBC_W // wc
    hb = tm // SUBLANES

    def body(x_ref, xp_ref, dtr_ref, w_ref, b_ref, bias_ref, xc_ref, act_ref, dt_ref):
        i, j = pl.program_id(0), pl.program_id(1)
        real = _real_rows(i, tm)
        xv = x_ref[...]
        xx = jnp.concatenate([jnp.where(i > 0, xp_ref[...], 0.0), xv], axis=0)
        acc = b_ref[...] + w_ref[kk - 1:kk, :] * xv
        for m in range(1, kk):
            acc = acc + w_ref[kk - 1 - m:kk - m, :] * pltpu.roll(xx, m, 0)[SUBLANES:, :]
        xc_ref[...] = acc
        act, _ = _silu_and_grad(acc)
        act_ref[...] = jnp.where(real, act, 0.0)

        @pl.when(j == 0)
        def _():
            dt_ref[...] = jnp.where(real, _softplus(dtr_ref[...] + bias_ref[...]), 0.0)

    return pl.pallas_call(
        body, name=name, grid=(lp // tm, nj),
        in_specs=[pl.BlockSpec((tm, wc), lambda i, j: (i, offb + j)),
                  pl.BlockSpec((SUBLANES, wc), lambda i, j: (jnp.maximum(i * hb - 1, 0), offb + j)),
                  pl.BlockSpec((tm, LANES), lambda i, j: (i, _DT_COL)),
                  pl.BlockSpec((kk, wc), lambda i, j: (0, j)),
                  pl.BlockSpec((1, wc), lambda i, j: (0, j)),
                  pl.BlockSpec((1, LANES), lambda i, j: (0, 0))],
        out_specs=[pl.BlockSpec((tm, wc), lambda i, j: (i, j)),
                   pl.BlockSpec((tm, wc), lambda i, j: (i, j)),
                   pl.BlockSpec((tm, LANES), lambda i, j: (i, 0))],
        out_shape=[jax.ShapeDtypeStruct((lp, _XBC_W), F32), jax.ShapeDtypeStruct((lp, _XBC_W), F32),
                   jax.ShapeDtypeStruct((lp, LANES), F32)],
        compiler_params=pltpu.CompilerParams(dimension_semantics=("parallel", "arbitrary")),
    )(proj, proj, proj, cw, cb, dt_bias)


def _ssm_prep_bwd(xc, proj, dt_bias, dxs, dxskip, db, dc, ddt, name):
    lp = xc.shape[0]
    tm = BLOCK

    def body(r, first, xc_ref, dtr_ref, dxs_ref, dsk_ref, db_ref, dc_ref, ddt_ref, b_ref,
             dxc_ref, ddtr_ref, dbias_ref):
        real = _real_rows(r, tm)
        _, ds = _silu_and_grad(xc_ref[...])
        up = lambda ref: ref[...].astype(F32)
        dxc_ref[:, :D_SSM] = jnp.where(
            real, (up(dxs_ref) + up(dsk_ref)) * ds[:, :D_SSM], 0.0).astype(dxc_ref.dtype)
        dxc_ref[:, D_SSM:D_SSM + 1024] = jnp.where(
            real, up(db_ref) * ds[:, D_SSM:D_SSM + 1024], 0.0).astype(dxc_ref.dtype)
        dxc_ref[:, D_SSM + 1024:] = jnp.where(
            real, up(dc_ref) * ds[:, D_SSM + 1024:], 0.0).astype(dxc_ref.dtype)
        dd = jnp.where(real, ddt_ref[...] * _sigmoid(dtr_ref[...] + b_ref[...]), 0.0)
        ddtr_ref[...] = dd.astype(ddtr_ref.dtype)
        _acc_add(first, dbias_ref, jnp.sum(dd, axis=0, keepdims=True))

    return _rowcall(name, body, lp, tm,
                    rows=[(xc, _XBC_W, 0), (proj, LANES, _DT_COL), (dxs, D_SSM, 0), (dxskip, D_SSM, 0),
                          (db, 1024, 0), (dc, 1024, 0), (ddt, LANES, 0)],
                    vecs=[dt_bias], outs=[(_XBC_W, BF16), (LANES, BF16)], accs=[((1, LANES), F32)])


def _ssd_common(dt, alog):
    a = -jnp.exp(alog)
    cs = _cumsum_rows(dt * a, BLOCK)
    cst = cs.T
    cl = cs[BLOCK - 1:BLOCK, :]
    tril = (lax.broadcasted_iota(jnp.int32, (BLOCK, BLOCK), 0)
            >= lax.broadcasted_iota(jnp.int32, (BLOCK, BLOCK), 1))
    return a, cs, cst, cl, jnp.exp(cs), jnp.exp(cl - cs), jnp.exp(cl), tril


def _head_cols(ecl, g):
    lane = lax.broadcasted_iota(jnp.int32, (1, SSD_HPG * SSD_P), 1)
    e = [ecl[:, SSD_HPG * g + hh:SSD_HPG * g + hh + 1] for hh in range(SSD_HPG)]
    return jnp.where(lane < SSD_P, e[0], jnp.where(lane < 2 * SSD_P, e[1],
                                                   jnp.where(lane < 3 * SSD_P, e[2], e[3])))


def _ssd_fwd(xbc, dt, alog, name, plan=None):
    lp = xbc.shape[0]
    nc = lp // BLOCK
    gw = SSD_HPG * SSD_P
    p_in, p_shapes, p_out, p_scr = _plan_parts(plan)

    def body(*refs):
        xs_ref, b_ref, c_ref, dt_ref, alog_ref = refs[:5]
        cins = refs[5:5 + len(p_in)]
        y_ref, so_ref = refs[5 + len(p_in):7 + len(p_in)]
        couts = refs[7 + len(p_in):7 + len(p_in) + len(p_out)]
        st, fx = refs[7 + len(p_in) + len(p_out):9 + len(p_in) + len(p_out)]
        sems = refs[9 + len(p_in) + len(p_out):]
        n = pl.program_id(0)

        @pl.when(n == 0)
        def _():
            st[...] = jnp.zeros_like(st)
            if plan is not None:
                plan.start(cins, couts, sems)

        dtv = dt_ref[...]
        _, cs, cst, cl, e, f, ecl, tril = _ssd_common(dtv, alog_ref[...])
        pre = []
        for g in range(SSD_GROUPS):
            bg = b_ref[:, g * SSD_N:(g + 1) * SSD_N].astype(BF16)
            cg = c_ref[:, g * SSD_N:(g + 1) * SSD_N].astype(BF16)
            stg = st[g]
            so_ref[0, g] = stg
            pre.append((bg, stg, _dot_nt(cg, bg), _dot(cg, stg.astype(BF16))))
        for g in range(SSD_GROUPS):
            bg, stg, gm, yoff = pre[g]
            heads = [SSD_HPG * g + hh for hh in range(SSD_HPG)]
            cols = lambda v: jnp.stack([v[:, h:h + 1] for h in heads])
            x4 = jnp.stack([xs_ref[:, h * SSD_P:(h + 1) * SSD_P] for h in heads])
            csr = jnp.stack([cst[h:h + 1, :] for h in heads])
            m = gm[None] * jnp.exp(jnp.where(tril[None], cols(cs) - csr, NEG))
            xdt = x4 * cols(dtv)
            yoff4 = jnp.stack([yoff[:, hh * SSD_P:(hh + 1) * SSD_P] for hh in range(SSD_HPG)])
            y4 = (jnp.einsum("hls,hsp->hlp", m.astype(BF16), xdt.astype(BF16), preferred_element_type=F32)
                  + cols(e) * yoff4)
            fx4 = cols(f) * xdt
            for hh, h in enumerate(heads):
                y_ref[:, h * SSD_P:(h + 1) * SSD_P] = y4[hh]
                fx[:, hh * SSD_P:(hh + 1) * SSD_P] = fx4[hh]
            st[g] = stg * _head_cols(ecl, g) + _dot_tn(bg, fx[...].astype(BF16))
        if plan is not None:
            @pl.when(n == nc - 1)
            def _():
                plan.wait(cins, couts, sems)

    res = pl.pallas_call(
        body, name=name, grid=(nc,),
        in_specs=[pl.BlockSpec((BLOCK, D_SSM), lambda n: (n, 0)),
                  pl.BlockSpec((BLOCK, 1024), lambda n: (n, 2)),
                  pl.BlockSpec((BLOCK, 1024), lambda n: (n, 3)),
                  pl.BlockSpec((BLOCK, LANES), lambda n: (n, 0)),
                  pl.BlockSpec((1, LANES), lambda n: (0, 0))] + p_in,
        out_specs=[pl.BlockSpec((BLOCK, D_SSM), lambda n: (n, 0)),
                   pl.BlockSpec((1, SSD_GROUPS, SSD_N, gw), lambda n: (n, 0, 0, 0))] + p_out,
        out_shape=[jax.ShapeDtypeStruct((lp, D_SSM), F32),
                   jax.ShapeDtypeStruct((nc, SSD_GROUPS, SSD_N, gw), F32)] + p_shapes,
        scratch_shapes=[pltpu.VMEM((SSD_GROUPS, SSD_N, gw), F32), pltpu.VMEM((BLOCK, gw), F32)] + p_scr,
        compiler_params=pltpu.CompilerParams(dimension_semantics=("arbitrary",)),
    )(xbc, xbc, xbc, dt, alog, *(plan.ins if plan is not None else []))
    return res[:2], res[2:]


def _ssd_bwd(xbc, dt, alog, states, dy, name, plan=None):
    lp = xbc.shape[0]
    nc = lp // BLOCK
    gw = SSD_HPG * SSD_P
    p_in, p_shapes, p_out, p_scr = _plan_parts(plan)

    def body(*refs):
        xs_ref, b_ref, c_ref, dt_ref, alog_ref, dy_ref, st_ref = refs[:7]
        cins = refs[7:7 + len(p_in)]
        dxs_ref, db_ref, dc_ref, ddt_ref, dalog_ref = refs[7 + len(p_in):12 + len(p_in)]
        couts = refs[12 + len(p_in):12 + len(p_in) + len(p_out)]
        dst, edy, fx = refs[12 + len(p_in) + len(p_out):15 + len(p_in) + len(p_out)]
        sems = refs[15 + len(p_in) + len(p_out):]
        i = pl.program_id(0)

        @pl.when(i == 0)
        def _():
            dst[...] = jnp.zeros_like(dst)
            dalog_ref[...] = jnp.zeros_like(dalog_ref)
            if plan is not None:
                plan.start(cins, couts, sems)

        dtv = dt_ref[...]
        a, cs, cst, cl, e, f, ecl, tril = _ssd_common(dtv, alog_ref[...])
        lane = lax.broadcasted_iota(jnp.int32, (1, LANES), 1)
        sub = _row_iota(BLOCK)
        triu = (lax.broadcasted_iota(jnp.int32, (BLOCK, BLOCK), 1)
                >= lax.broadcasted_iota(jnp.int32, (BLOCK, BLOCK), 0))
        dcs = jnp.zeros((BLOCK, LANES), F32)
        dcst = jnp.zeros((LANES, BLOCK), F32)
        dcl = jnp.zeros((1, LANES), F32)
        ddtx = jnp.zeros((BLOCK, LANES), F32)
        pre = []
        for g in range(SSD_GROUPS):
            bg = b_ref[:, g * SSD_N:(g + 1) * SSD_N].astype(BF16)
            cg = c_ref[:, g * SSD_N:(g + 1) * SSD_N].astype(BF16)
            stb = st_ref[0, g].astype(BF16)
            dsob = dst[g].astype(BF16)
            pre.append((bg, cg, stb, dsob, _dot_nt(cg, bg), _dot_nt(bg, cg), _dot(cg, stb), _dot(bg, dsob)))
        for g in range(SSD_GROUPS):
            bg, cg, stb, dsob, gm, gmt, yraw, dfx = pre[g]
            dso = dst[g]
            prodsum = jnp.sum(dso * st_ref[0, g], axis=0, keepdims=True)
            heads = [SSD_HPG * g + hh for hh in range(SSD_HPG)]
            cols = lambda v: jnp.stack([v[:, h:h + 1] for h in heads])
            parts = lambda v: jnp.stack([v[:, hh * SSD_P:(hh + 1) * SSD_P] for hh in range(SSD_HPG)])
            x4 = jnp.stack([xs_ref[:, h * SSD_P:(h + 1) * SSD_P] for h in heads])
            dy4 = jnp.stack([dy_ref[:, h * SSD_P:(h + 1) * SSD_P] for h in heads])
            csc, dtc, ec, fc = cols(cs), cols(dtv), cols(e), cols(f)
            csr = jnp.stack([cst[h:h + 1, :] for h in heads])
            seg = csc - csr
            lam = jnp.exp(jnp.where(tril[None], seg, NEG))
            lamt = jnp.exp(jnp.where(triu[None], -seg, NEG))
            mt = gmt[None] * lamt
            xdt = x4 * dtc
            dyb = dy4.astype(BF16)
            dm = jnp.einsum("hlp,hsp->hls", dyb, xdt.astype(BF16), preferred_element_type=F32)
            dfx4 = parts(dfx)
            dxdt = jnp.einsum("hsl,hlp->hsp", mt.astype(BF16), dyb, preferred_element_type=F32) + fc * dfx4
            dml = dm * lam
            w = dml * gm[None]
            dgm = jnp.sum(dml, axis=0)
            dff = jnp.sum(dfx4 * xdt, axis=2, keepdims=True) * fc
            colv = (jnp.sum(w, axis=2, keepdims=True)
                    + jnp.sum(dy4 * parts(yraw), axis=2, keepdims=True) * ec - dff)
            roww = jnp.sum(w, axis=1, keepdims=True)
            ddtc = jnp.sum(dxdt * x4, axis=2, keepdims=True)
            dffs = jnp.sum(dff, axis=1, keepdims=True)
            dxs4 = dxdt * dtc
            edy4 = ec * dy4
            fx4 = fc * xdt
            for hh, h in enumerate(heads):
                ls = slice(hh * SSD_P, (hh + 1) * SSD_P)
                onl = (lane == h).astype(F32)
                dcs = dcs + colv[hh] * onl
                dcst = dcst - (sub == h).astype(F32) * roww[hh]
                dcl = dcl + (dffs[hh] + ecl[:, h:h + 1] * jnp.sum(prodsum[:, ls], axis=1, keepdims=True)) * onl
                ddtx = ddtx + ddtc[hh] * onl
                dxs_ref[:, h * SSD_P:(h + 1) * SSD_P] = dxs4[hh].astype(dxs_ref.dtype)
                edy[:, ls] = edy4[hh]
                fx[:, ls] = fx4[hh]
            edyb = edy[...].astype(BF16)
            fxb = fx[...].astype(BF16)
            dgb = dgm.astype(BF16)
            dc_ref[:, g * SSD_N:(g + 1) * SSD_N] = (_dot_nt(edyb, stb) + _dot(dgb, bg)).astype(dc_ref.dtype)
            db_ref[:, g * SSD_N:(g + 1) * SSD_N] = (_dot_nt(fxb, dsob) + _dot_tn(dgb, cg)).astype(db_ref.dtype)
            dst[g] = dso * _head_cols(ecl, g) + _dot_tn(cg, edyb)
        dcs = dcs + dcst.T + jnp.where(sub == BLOCK - 1, dcl, 0.0)
        dda = _rev_cumsum_rows(dcs, BLOCK)
        ddt_ref[...] = ddtx + dda * a
        dalog_ref[...] += jnp.sum(dda * dtv, axis=0, keepdims=True) * a
        if plan is not None:
            @pl.when(i == nc - 1)
            def _():
                plan.wait(cins, couts, sems)

    rev = lambda i: nc - 1 - i
    res = pl.pallas_call(
        body, name=name, grid=(nc,),
        in_specs=[pl.BlockSpec((BLOCK, D_SSM), lambda i: (rev(i), 0)),
                  pl.BlockSpec((BLOCK, 1024), lambda i: (rev(i), 2)),
                  pl.BlockSpec((BLOCK, 1024), lambda i: (rev(i), 3)),
                  pl.BlockSpec((BLOCK, LANES), lambda i: (rev(i), 0)),
                  pl.BlockSpec((1, LANES), lambda i: (0, 0)),
                  pl.BlockSpec((BLOCK, D_SSM), lambda i: (rev(i), 0)),
                  pl.BlockSpec((1, SSD_GROUPS, SSD_N, gw), lambda i: (rev(i), 0, 0, 0))] + p_in,
        out_specs=[pl.BlockSpec((BLOCK, D_SSM), lambda i: (rev(i), 0)),
                   pl.BlockSpec((BLOCK, 1024), lambda i: (rev(i), 0)),
                   pl.BlockSpec((BLOCK, 1024), lambda i: (rev(i), 0)),
                   pl.BlockSpec((BLOCK, LANES), lambda i: (rev(i), 0)),
                   pl.BlockSpec((1, LANES), lambda i: (0, 0))] + p_out,
        out_shape=[jax.ShapeDtypeStruct((lp, D_SSM), BF16),
                   jax.ShapeDtypeStruct((lp, 1024), BF16),
                   jax.ShapeDtypeStruct((lp, 1024), BF16),
                   jax.ShapeDtypeStruct((lp, LANES), F32),
                   jax.ShapeDtypeStruct((1, LANES), F32)] + p_shapes,
        scratch_shapes=[pltpu.VMEM((SSD_GROUPS, SSD_N, gw), F32),
                        pltpu.VMEM((BLOCK, gw), F32), pltpu.VMEM((BLOCK, gw), F32)] + p_scr,
        compiler_params=pltpu.CompilerParams(dimension_semantics=("arbitrary",)),
    )(xbc, xbc, xbc, dt, alog, dy, states, *(plan.ins if plan is not None else []))
    return res[:5], res[5:]


_GN_GROUPS = 8
_GN_W = D_SSM // _GN_GROUPS


def _ssm_gate_fwd(yssd, xbc, proj, dskip, gnorm, name):
    lp = yssd.shape[0]
    tm = _pick(lp, (320, 256, 128))

    def body(r, first, y_ref, x_ref, z_ref, d_ref, g_ref, o_ref):
        sz, _ = _silu_and_grad(z_ref[...])
        y2 = (y_ref[...] + d_ref[...] * x_ref[...]) * sz
        for k in range(_GN_GROUPS):
            sl = slice(k * _GN_W, (k + 1) * _GN_W)
            yk = y2[:, sl]
            rs = lax.rsqrt(jnp.mean(yk * yk, axis=-1, keepdims=True) + EPS)
            o_ref[:, sl] = (yk * rs * g_ref[:, sl]).astype(o_ref.dtype)

    return _rowcall(name, body, lp, tm, rows=[(yssd, D_SSM, 0), (xbc, D_SSM, 0), (proj, D_SSM, 0)],
                    vecs=[dskip, gnorm], outs=[(D_SSM, BF16)])[0]


def _ssm_gate_bwd(yssd, xbc, proj, dskip, gnorm, dyn, name):
    lp = yssd.shape[0]
    tm = BLOCK

    def body(r, first, y_ref, x_ref, z_ref, dyn_ref, d_ref, g_ref,
             dy_ref, dx_ref, dz_ref, dd_ref, dg_ref):
        z = z_ref[...]
        sz, dsz = _silu_and_grad(z)
        xs = x_ref[...]
        y1 = y_ref[...] + d_ref[...] * xs
        y2 = y1 * sz
        dyn = dyn_ref[...]
        for k in range(_GN_GROUPS):
            sl = slice(k * _GN_W, (k + 1) * _GN_W)
            dx, dgt = _rms_bwd(y2[:, sl], g_ref[:, sl], dyn[:, sl])
            dy1 = dx * sz[:, sl]
            dy_ref[:, sl] = dy1.astype(dy_ref.dtype)
            dx_ref[:, sl] = (dy1 * d_ref[:, sl]).astype(dx_ref.dtype)
            dz_ref[:, sl] = (dx * y1[:, sl] * dsz[:, sl]).astype(dz_ref.dtype)

            @pl.when(first)
            def _():
                dd_ref[:, sl] = jnp.zeros((1, _GN_W), F32)
                dg_ref[:, sl] = jnp.zeros((1, _GN_W), F32)

            dd_ref[:, sl] += jnp.sum(dy1 * xs[:, sl], axis=0, keepdims=True)
            dg_ref[:, sl] += jnp.sum(dgt, axis=0, keepdims=True)

    return _rowcall(name, body, lp, tm,
                    rows=[(yssd, D_SSM, 0), (xbc, D_SSM, 0), (proj, D_SSM, 0), (dyn, D_SSM, 0)],
                    vecs=[dskip, gnorm], outs=[(D_SSM, BF16), (D_SSM, BF16), (D_SSM, BF16)],
                    accs=[((1, D_SSM), F32), ((1, D_SSM), F32)])


def _shape2d(shape):
    n = math.prod(shape)
    if len(shape) == 2:
        return tuple(shape)
    return (n // LANES, LANES) if n % LANES == 0 else (1, n)


def _adamw_many(ws, gs, ms, vs, name):
    n = len(ws)
    c1 = 1.0 / (1.0 - ADAM_B1 ** ADAM_STEP)
    c2 = 1.0 / (1.0 - ADAM_B2 ** ADAM_STEP)

    def body(*refs):
        for i in range(n):
            w_ref, g_ref, m_ref, v_ref = (refs[j * n + i] for j in range(4))
            d_ref, nm_ref, nv_ref = (refs[(4 + j) * n + i] for j in range(3))
            gv = g_ref[...]
            nm = ADAM_B1 * m_ref[...] + (1.0 - ADAM_B1) * gv
            nv = ADAM_B2 * v_ref[...] + (1.0 - ADAM_B2) * (gv * gv)
            nm_ref[...] = nm
            nv_ref[...] = nv
            d_ref[...] = -ADAM_LR * ((nm * c1) / (jnp.sqrt(nv * c2) + ADAM_EPS) + ADAM_WD * w_ref[...])

    vm = pl.BlockSpec(memory_space=pltpu.VMEM)
    return pl.pallas_call(
        body, name=name, in_specs=[vm] * (4 * n), out_specs=[vm] * (3 * n),
        out_shape=[jax.ShapeDtypeStruct(w.shape, F32) for w in ws] * 3,
    )(*ws, *gs, *ms, *vs)


def _place():
    return lax.axis_index("x"), lax.axis_index("y"), lax.axis_index("c")


def _other_chips(x, y):
    return [(1 - x, y), (x, 1 - y), (1 - x, 1 - y)]


_ANY = pl.BlockSpec(memory_space=pl.ANY)


class _Plan:
    def __init__(self, ins, out_shapes, n_remote, n_local, issue):
        self.ins = list(ins)
        self.out_shapes = list(out_shapes)
        self.issue = issue
        self.scratch = [pltpu.SemaphoreType.DMA((max(n_remote, 1),)),
                        pltpu.SemaphoreType.DMA((max(n_remote, 1),)),
                        pltpu.SemaphoreType.DMA((max(n_local, 1),))]

    def start(self, ins, outs, sems):
        sends, _, locs = self.issue(ins, outs, *sems)
        for cp in locs + sends:
            cp.start()

    def wait(self, ins, outs, sems):
        sends, recvs, locs = self.issue(ins, outs, *sems)
        for make in recvs:
            make().wait_recv()
        for cp in sends:
            cp.wait_send()
        for cp in locs:
            cp.wait()


def _plan_parts(plan):
    if plan is None:
        return [], [], [], []
    return ([_ANY] * len(plan.ins), plan.out_shapes, [_ANY] * len(plan.out_shapes), plan.scratch)


def _run_plan(plan, name):
    n_in, n_out = len(plan.ins), len(plan.out_shapes)

    def body(*refs):
        ins, outs, sems = refs[:n_in], refs[n_in:n_in + n_out], refs[n_in + n_out:]
        plan.start(ins, outs, sems)
        plan.wait(ins, outs, sems)

    return pl.pallas_call(
        body, name=name, in_specs=[_ANY] * n_in, out_specs=[_ANY] * n_out,
        out_shape=plan.out_shapes, scratch_shapes=plan.scratch,
    )(*plan.ins)


def _gather_plan(shards):
    n = len(shards)

    def issue(ins, outs, send_sems, recv_sems, local_sems):
        x, y, c = _place()
        me = 2 * x + y
        sends, recvs, locs = [], [], []
        for p in range(n):
            locs.append(pltpu.make_async_copy(ins[p], outs[p].at[me], local_sems.at[p]))
            for k, (px, py) in enumerate(_other_chips(x, y)):
                sems = dict(send_sem=send_sems.at[3 * p + k], recv_sem=recv_sems.at[3 * p + k],
                            device_id=(px, py, c), device_id_type=MESH)
                sends.append(pltpu.make_async_remote_copy(src_ref=ins[p], dst_ref=outs[p].at[me], **sems))
                recvs.append(functools.partial(pltpu.make_async_remote_copy, src_ref=ins[p],
                                               dst_ref=outs[p].at[2 * px + py], **sems))
        return sends, recvs, locs

    return _Plan(shards, [jax.ShapeDtypeStruct((N_CHIPS,) + s.shape, s.dtype) for s in shards], 3 * n, n, issue)


_REL7 = [(fx, fy, fc) for fx in (0, 1) for fy in (0, 1) for fc in (0, 1)][1:]


def _scatter8_plan(gs):
    n = len(gs)

    def issue(ins, outs, send_sems, recv_sems, local_sems):
        x, y, c = _place()
        sends = []
        for p in range(n):
            hr = gs[p].shape[1] // 2
            for k, (fx, fy, fc) in enumerate(_REL7):
                tx, ty, tc = x ^ fx, y ^ fy, c ^ fc
                src = ins[p].at[2 * tx + ty, pl.ds(pl.multiple_of(tc * hr, SUBLANES), hr), :]
                sends.append(pltpu.make_async_remote_copy(
                    src_ref=src, dst_ref=outs[p].at[k],
                    send_sem=send_sems.at[7 * p + k], recv_sem=recv_sems.at[7 * p + k],
                    device_id=(tx, ty, tc), device_id_type=MESH))
        return sends, [functools.partial(lambda cp: cp, cp) for cp in sends], []

    shapes = [jax.ShapeDtypeStruct((7, g.shape[1] // 2, g.shape[2]), g.dtype) for g in gs]
    return _Plan(gs, shapes, 7 * n, 0, issue)


def _sibling_plan(ts):
    n = len(ts)

    def issue(ins, outs, send_sems, recv_sems, local_sems):
        x, y, c = _place()
        sends = [pltpu.make_async_remote_copy(
            src_ref=ins[p], dst_ref=outs[p], send_sem=send_sems.at[p], recv_sem=recv_sems.at[p],
            device_id=(x, y, 1 - c), device_id_type=MESH) for p in range(n)]
        return sends, [functools.partial(lambda cp: cp, cp) for cp in sends], []

    return _Plan(ts, [jax.ShapeDtypeStruct(t.shape, t.dtype) for t in ts], n, 0, issue)


def _add8(g, recv, chip, core, name):
    s, r, n = g.shape
    hr = r // 2
    th = hr // 2 if (hr // 2) % SUBLANES == 0 else hr
    nt = hr // th

    def body(chip_ref, core_ref, g_ref, r_ref, o_ref):
        acc = g_ref[0].astype(F32)
        for k in range(7):
            acc = acc + r_ref[k].astype(F32)
        o_ref[...] = acc

    return pl.pallas_call(
        body, name=name,
        grid_spec=pltpu.PrefetchScalarGridSpec(
            num_scalar_prefetch=2, grid=(nt,),
            in_specs=[pl.BlockSpec((1, th, n), lambda i, ch, co: (ch[0], co[0] * nt + i, 0)),
                      pl.BlockSpec((7, th, n), lambda i, ch, co: (0, i, 0))],
            out_specs=pl.BlockSpec((th, n), lambda i, ch, co: (i, 0))),
        out_shape=jax.ShapeDtypeStruct((hr, n), F32),
        compiler_params=pltpu.CompilerParams(dimension_semantics=("parallel",)),
    )(chip, core, g, recv)


def _adamw_halves(w, own, other, m, v, core, name):
    r, n = w.shape
    hr = r // 2
    th = hr // 2 if (hr // 2) % SUBLANES == 0 else hr
    tph = hr // th
    c1 = 1.0 / (1.0 - ADAM_B1 ** ADAM_STEP)
    c2 = 1.0 / (1.0 - ADAM_B2 ** ADAM_STEP)

    def body(core_ref, w_ref, a_ref, b_ref, m_ref, v_ref, g_ref, d_ref, nm_ref, nv_ref):
        half = pl.program_id(0) // tph
        gv = jnp.where(half == core_ref[0], a_ref[...], b_ref[...])
        nm = ADAM_B1 * m_ref[...] + (1.0 - ADAM_B1) * gv
        nv = ADAM_B2 * v_ref[...] + (1.0 - ADAM_B2) * (gv * gv)
        g_ref[...] = gv
        nm_ref[...] = nm
        nv_ref[...] = nv
        d_ref[...] = -ADAM_LR * ((nm * c1) / (jnp.sqrt(nv * c2) + ADAM_EPS) + ADAM_WD * w_ref[...])

    full = pl.BlockSpec((th, n), lambda i, co: (i, 0))
    part = pl.BlockSpec((th, n), lambda i, co: (i % tph, 0))
    return pl.pallas_call(
        body, name=name,
        grid_spec=pltpu.PrefetchScalarGridSpec(
            num_scalar_prefetch=1, grid=(2 * tph,),
            in_specs=[full, part, part, full, full], out_specs=[full] * 4),
        out_shape=[jax.ShapeDtypeStruct((r, n), F32)] * 4,
        compiler_params=pltpu.CompilerParams(dimension_semantics=("parallel",)),
    )(core, w, own, other, m, v)


def _allreduce_small(pack, name):
    r, l = pack.shape
    hr = r // 2
    assert hr % SUBLANES == 0

    def body(p_ref, o_ref, sib, chips, send_sems, recv_sems):
        x, y, c = _place()
        chip = 2 * x + y
        sibling = dict(device_id=(x, y, 1 - c), device_id_type=MESH)
        mine = pl.ds(pl.multiple_of(c * hr, SUBLANES), hr)
        other = pl.ds(pl.multiple_of((1 - c) * hr, SUBLANES), hr)
        a = pltpu.make_async_remote_copy(src_ref=p_ref.at[other], dst_ref=sib, send_sem=send_sems.at[0],
                                         recv_sem=recv_sems.at[0], **sibling)
        a.start()
        a.wait()
        own, got = p_ref[mine, :], sib[...]
        chips[chip] = jnp.where(c == 0, own, got) + jnp.where(c == 0, got, own)
        sends = []
        for k, (px, py) in enumerate(_other_chips(x, y)):
            cp = pltpu.make_async_remote_copy(
                src_ref=chips.at[chip], dst_ref=chips.at[chip], send_sem=send_sems.at[1 + k],
                recv_sem=recv_sems.at[1 + k], device_id=(px, py, c), device_id_type=MESH)
            cp.start()
            sends.append(cp)
        for k, (px, py) in enumerate(_other_chips(x, y)):
            pltpu.make_async_remote_copy(
                src_ref=chips.at[chip], dst_ref=chips.at[2 * px + py], send_sem=send_sems.at[1 + k],
                recv_sem=recv_sems.at[1 + k], device_id=(px, py, c), device_id_type=MESH).wait_recv()
        for cp in sends:
            cp.wait_send()
        o_ref[mine, :] = ((chips[0] + chips[1]) + chips[2]) + chips[3]
        fin = pltpu.make_async_remote_copy(src_ref=o_ref.at[mine], dst_ref=o_ref.at[mine],
                                           send_sem=send_sems.at[4], recv_sem=recv_sems.at[4], **sibling)
        fin.start()
        pltpu.make_async_remote_copy(src_ref=o_ref.at[mine], dst_ref=o_ref.at[other],
                                     send_sem=send_sems.at[4], recv_sem=recv_sems.at[4], **sibling).wait_recv()
        fin.wait_send()

    vm = pl.BlockSpec(memory_space=pltpu.VMEM)
    return pl.pallas_call(
        body, name=name, in_specs=[vm], out_specs=vm,
        out_shape=jax.ShapeDtypeStruct((r, l), F32),
        scratch_shapes=[pltpu.VMEM((hr, l), F32), pltpu.VMEM((N_CHIPS, hr, l), F32),
                        pltpu.SemaphoreType.DMA((5,)), pltpu.SemaphoreType.DMA((5,))],
    )(pack)


def _flat_rows(a, mult=SUBLANES * LANES):
    f = a.reshape(-1)
    padn = (-f.shape[0]) % mult
    if padn:
        f = jnp.concatenate([f, jnp.zeros((padn,), f.dtype)])
    return f


def _pack(arrs, mult=SUBLANES * LANES, total_mult=None):
    flat = [_flat_rows(a, mult) for a in arrs]
    sizes = [f.shape[0] for f in flat]
    if total_mult is not None:
        padn = (-sum(sizes)) % total_mult
        if padn:
            flat.append(jnp.zeros((padn,), flat[0].dtype))
    return jnp.concatenate(flat).reshape(-1, LANES), sizes


def _unpack(pack, shapes, sizes, lead=()):
    flat = pack.reshape(lead + (-1,))
    out, off = [], 0
    for shp, sz in zip(shapes, sizes):
        n = math.prod(shp)
        out.append(flat[..., off:off + n].reshape(lead + tuple(shp)))
        off += sz
    return out


def _cols_from_shards(g):
    s, k, n = g.shape
    return jnp.transpose(g, (1, 0, 2)).reshape(k, s * n)


def _ffn_fwd(h, u, post, w_up, cw, cb, w_down, tag, next_pre=None, plan=None):
    hp = _mm_nn_sh(u, w_up, 2 * D_FF, f"{tag}_up")
    if plan is None:
        (act, hg, hu), pouts = _ffn_convact_fwd(hp, cw, cb, f"{tag}_convact"), ()
    else:
        (act, hg, hu), pouts = _ffn_convact_fwd(hp, cw, cb, f"{tag}_convact", plan)
    o = _mm_nn(act, w_down, F32, f"{tag}_down")
    res = _postnorm_res_fwd(h, o, post, f"{tag}_postnorm", next_pre)
    hn, un = res if next_pre is not None else (res, None)
    return hn, un, (h, u, hp, hg, hu, act, o), pouts


def _ffn_bwd(dh, saved, pre, post, w_up, cw, w_down, tag):
    h, u, hp, hg, hu, act, o = saved
    do, dpost = _postnorm_bwd(o, post, dh, f"{tag}_postnorm_bwd")
    dact = _mm_nt(do, w_down, f"{tag}_down_dx", BF16)
    dw_down = _mm_tn(act, do, f"{tag}_down_dw")
    dhg, dhu = _ffn_act_bwd(hg, hu, dact, f"{tag}_act_bwd")
    dxg, dwg, dbg = _conv_bwd(hp, 0, D_FF, dhg, cw, f"{tag}_conv_bwd_gate")
    dxu, dwu, dbu = _conv_bwd(hp, D_FF, D_FF, dhu, cw, f"{tag}_conv_bwd_up", w_col_off=D_FF)
    dhp = (dxg, dxu)
    dcw = jnp.concatenate([dwg, dwu], axis=1)
    dcb = jnp.concatenate([dbg, dbu], axis=1)
    du = _mm_nt_sh(dhp, w_up, f"{tag}_up_dx")
    dw_up = _mm_tn_sh(u, dhp, w_up.shape[2], f"{tag}_up_dw")
    dhn, dpre = _prenorm_bwd(h, pre, du, dh, f"{tag}_prenorm_bwd")
    return dhn, dict(pre=dpre, post=dpost, w_up=dw_up, conv_w=dcw[:3], conv_b=dcb, w_down=dw_down)


class _Exchange:
    GATHER_IN_LRU = ("l0_w_out", "l0_ffn_w_down")
    GATHER_IN_ATTN = ("l0_ffn_w_up", "l1_w_out")
    GATHER_IN_FFN0 = ("l1_w_in",)
    GATHER_IN_SSD = ("l1_ffn_w_up", "l1_ffn_w_down")
    AFTER_L1_OUT = ("l1_ffn_w_up", "l1_ffn_w_down", "l1_w_out")
    IN_LRU_BWD = ("l1_w_in",)
    AFTER_L0_OUT = ("l0_ffn_w_up", "l0_ffn_w_down", "l0_w_out")
    LAST = ("l0_w_in",)

    def __init__(self, late_shards):
        self.late = dict(late_shards)
        self.slabs = {}
        self.recv = {}

    def gather_plan(self, names):
        return _gather_plan([self.late[n] for n in names])

    def gathered(self, names, outs):
        return {n: (g if n in _BIG_COL else g.reshape(-1, g.shape[-1])) for n, g in zip(names, outs)}

    def scatter_plan(self, grads, names):
        for n in names:
            g = grads[n]
            self.slabs[n] = g if n in _BIG_COL else g.reshape(N_CHIPS, -1, g.shape[-1])
        return _scatter8_plan([self.slabs[n] for n in names])

    def scattered(self, names, outs):
        self.recv.update(zip(names, outs))


def _local_step(x, tgt, meta, P, ex=None):
    seq, d = x.shape
    lp = seq + BLOCK
    h0 = jnp.concatenate([jnp.zeros((PAD, d), F32), meta, x], axis=0)
    tgt_p = jnp.concatenate([jnp.zeros((BLOCK, d), F32), tgt], axis=0)

    u0 = _rmsnorm_fwd(h0, P["l0_mix_pre_norm"], "l0_mix_prenorm")
    proj0 = _mm_nn_sh(u0, P["l0_w_in"], EVEN_IN, "l0_in")
    xrc = _conv_fwd(proj0, D_MODEL, D_MODEL, P["l0_lru_conv_w"], P["l0_lru_conv_b"], "l0_lru_conv")
    lru_args = (P["l0_lru_w_a"], P["l0_lru_w_x"], P["l0_lru_b_a"], P["l0_lru_b_x"], P["l0_lru_lambda"])
    if ex:
        (ya, hl), outs = _lru_fwd(proj0, xrc, *lru_args, "l0_lru", ex.gather_plan(ex.GATHER_IN_LRU))
        P = {**P, **ex.gathered(ex.GATHER_IN_LRU, outs)}
    else:
        ya, hl = _lru_fwd(proj0, xrc, *lru_args, "l0_lru")
    yb, outs = _attn_fwd(proj0, P["l0_attn_sinks"], "l0_attn",
                         ex.gather_plan(ex.GATHER_IN_ATTN) if ex else None)
    if ex:
        P = {**P, **ex.gathered(ex.GATHER_IN_ATTN, outs)}
    o0 = _mm_nn((ya, yb), P["l0_w_out"], F32, "l0_out")
    h1, u1 = _postnorm_res_fwd(h0, o0, P["l0_mix_post_norm"], "l0_mix_postnorm", P["l0_ffn_pre_norm"])
    h2, u2, ffn0, outs = _ffn_fwd(h1, u1, P["l0_ffn_post_norm"], P["l0_ffn_w_up"], P["l0_ffn_conv_w"],
                                  P["l0_ffn_conv_b"], P["l0_ffn_w_down"], "l0_ffn", P["l1_mix_pre_norm"],
                                  ex.gather_plan(ex.GATHER_IN_FFN0) if ex else None)
    if ex:
        P = {**P, **ex.gathered(ex.GATHER_IN_FFN0, outs)}
    proj1 = _mm_nn_sh(u2, P["l1_w_in"], ODD_IN_PAD, "l1_in")
    xc1, xbc, dt = _ssm_convprep_fwd(proj1, P["l1_ssm_conv_w"], P["l1_ssm_conv_b"], P["l1_dt_bias"],
                                     "l1_ssm_convprep")
    (yssd, states), outs = _ssd_fwd(xbc, dt, P["l1_a_log"], "l1_ssd",
                                    ex.gather_plan(ex.GATHER_IN_SSD) if ex else None)
    if ex:
        P = {**P, **ex.gathered(ex.GATHER_IN_SSD, outs)}
    yn = _ssm_gate_fwd(yssd, xbc, proj1, P["l1_d_skip"], P["l1_gate_norm"], "l1_ssm_gate")
    o1 = _mm_nn(yn, P["l1_w_out"], F32, "l1_out")
    h3, u3 = _postnorm_res_fwd(h2, o1, P["l1_mix_post_norm"], "l1_mix_postnorm", P["l1_ffn_pre_norm"])
    h4, _, ffn1, _ = _ffn_fwd(h3, u3, P["l1_ffn_post_norm"], P["l1_ffn_w_up"], P["l1_ffn_conv_w"],
                              P["l1_ffn_conv_b"], P["l1_ffn_w_down"], "l1_ffn")
    dh4, loss_cols = _loss_fwd_bwd(h4, tgt_p, "loss")

    G = {}
    dh3, g = _ffn_bwd(dh4, ffn1, P["l1_ffn_pre_norm"], P["l1_ffn_post_norm"], P["l1_ffn_w_up"],
                      P["l1_ffn_conv_w"], P["l1_ffn_w_down"], "l1_ffn")
    for k, v in g.items():
        G["l1_ffn_" + (k + "_norm" if k in ("pre", "post") else k)] = v
    do1, G["l1_mix_post_norm"] = _postnorm_bwd(o1, P["l1_mix_post_norm"], dh3, "l1_mix_postnorm_bwd")
    dyn = _mm_nt(do1, P["l1_w_out"], "l1_out_dx")
    G["l1_w_out"] = _mm_tn(yn, do1, "l1_out_dw")
    dyssd, dxskip, dz, dd_cols, G["l1_gate_norm"] = _ssm_gate_bwd(
        yssd, xbc, proj1, P["l1_d_skip"], P["l1_gate_norm"], dyn, "l1_ssm_gate_bwd")
    G["l1_d_skip"] = dd_cols.reshape(SSD_HEADS, SSD_P).sum(axis=1)
    (dxs, dbm, dcm, ddt, dalog), outs = _ssd_bwd(
        xbc, dt, P["l1_a_log"], states, dyssd, "l1_ssd_bwd",
        ex.scatter_plan(G, ex.AFTER_L1_OUT) if ex else None)
    if ex:
        ex.scattered(ex.AFTER_L1_OUT, outs)
    G["l1_a_log"] = dalog[0, :SSD_HEADS]
    dxc, ddtr, dbias = _ssm_prep_bwd(xc1, proj1, P["l1_dt_bias"], dxs, dxskip, dbm, dcm, ddt,
                                     "l1_ssm_prep_bwd")
    G["l1_dt_bias"] = dbias[0, :SSD_HEADS]
    dxbc, dcw, dcb = _conv_bwd(proj1, _ZW, _XBC_W, dxc, P["l1_ssm_conv_w"], "l1_ssm_conv_bwd")
    G["l1_ssm_conv_w"] = dcw[:4]
    G["l1_ssm_conv_b"] = dcb
    dproj1 = jnp.concatenate([dz, dxbc, ddtr], axis=1)
    du2 = _mm_nt_sh(dproj1, P["l1_w_in"], "l1_in_dx")
    G["l1_w_in"] = _mm_tn_sh(u2, dproj1, ODD_IN // N_CHIPS, "l1_in_dw")
    dh2, G["l1_mix_pre_norm"] = _prenorm_bwd(h2, P["l1_mix_pre_norm"], du2, dh3, "l1_mix_prenorm_bwd")
    dh1, g = _ffn_bwd(dh2, ffn0, P["l0_ffn_pre_norm"], P["l0_ffn_post_norm"], P["l0_ffn_w_up"],
                      P["l0_ffn_conv_w"], P["l0_ffn_w_down"], "l0_ffn")
    for k, v in g.items():
        G["l0_ffn_" + (k + "_norm" if k in ("pre", "post") else k)] = v
    do0, G["l0_mix_post_norm"] = _postnorm_bwd(o0, P["l0_mix_post_norm"], dh1, "l0_mix_postnorm_bwd")
    dmix = _mm_nt(do0, P["l0_w_out"], "l0_out_dx")
    G["l0_w_out"] = jnp.concatenate([_mm_tn(ya, do0, "l0_out_dw_lru"), _mm_tn(yb, do0, "l0_out_dw_attn")], axis=0)
    if ex:
        lru_out, outs = _lru_bwd(proj0, xrc, hl, dmix, *lru_args, "l0_lru_bwd",
                                 ex.scatter_plan(G, ex.IN_LRU_BWD))
        ex.scattered(ex.IN_LRU_BWD, outs)
    else:
        lru_out = _lru_bwd(proj0, xrc, hl, dmix, *lru_args, "l0_lru_bwd")
    (dgate, dxrc, G["l0_lru_w_a"], G["l0_lru_w_x"], G["l0_lru_b_a"], G["l0_lru_b_x"],
     G["l0_lru_lambda"]) = lru_out
    dxr, dcw, dcb = _conv_bwd(proj0, D_MODEL, D_MODEL, dxrc, P["l0_lru_conv_w"], "l0_lru_conv_bwd")
    G["l0_lru_conv_w"] = dcw[:4]
    G["l0_lru_conv_b"] = dcb
    (dq, dk, dv, G["l0_attn_sinks"]), outs = _attn_bwd(
        proj0, P["l0_attn_sinks"], dmix, "l0_attn_bwd",
        ex.scatter_plan(G, ex.AFTER_L0_OUT) if ex else None)
    if ex:
        ex.scattered(ex.AFTER_L0_OUT, outs)
    dproj0 = jnp.concatenate([dgate, dxr, dq, dk.astype(BF16), dv.astype(BF16)], axis=1)
    G["l0_w_in"] = _mm_tn_sh(u0, dproj0, EVEN_IN // N_CHIPS, "l0_in_dw")
    if ex:
        du0, outs = _mm_nt_sh(dproj0, P["l0_w_in"], "l0_in_dx", ex.scatter_plan(G, ex.LAST))
        ex.scattered(ex.LAST, outs)
    else:
        du0 = _mm_nt_sh(dproj0, P["l0_w_in"], "l0_in_dx")
    dh0, G["l0_mix_pre_norm"] = _prenorm_bwd(h0, P["l0_mix_pre_norm"], du0, dh1, "l0_mix_prenorm_bwd")
    return loss_cols, dh0[BLOCK:], dh0[PAD:BLOCK], G


_BIG_COL = ("l0_w_in", "l0_ffn_w_up", "l1_w_in", "l1_ffn_w_up")
_BIG = ("l0_w_in", "l0_w_out", "l0_ffn_w_up", "l0_ffn_w_down",
        "l1_w_in", "l1_w_out", "l1_ffn_w_up", "l1_ffn_w_down")
_SMALL_SHARDED = ("meta_tokens", "l0_lru_conv_w", "l0_ffn_conv_w", "l1_ssm_conv_w", "l1_ffn_conv_w")
_WEIGHTS = ("meta_tokens", "l0_mix_pre_norm", "l0_mix_post_norm", "l0_w_in", "l0_lru_conv_w",
            "l0_lru_conv_b", "l0_lru_w_a", "l0_lru_b_a", "l0_lru_w_x", "l0_lru_b_x", "l0_lru_lambda",
            "l0_attn_sinks", "l0_w_out", "l0_ffn_pre_norm", "l0_ffn_post_norm", "l0_ffn_w_up",
            "l0_ffn_conv_w", "l0_ffn_conv_b", "l0_ffn_w_down", "l1_mix_pre_norm", "l1_mix_post_norm",
            "l1_w_in", "l1_ssm_conv_w", "l1_ssm_conv_b", "l1_dt_bias", "l1_a_log", "l1_d_skip",
            "l1_gate_norm", "l1_w_out", "l1_ffn_pre_norm", "l1_ffn_post_norm", "l1_ffn_w_up",
            "l1_ffn_conv_w", "l1_ffn_conv_b", "l1_ffn_w_down")
_REPL = tuple(n for n in _WEIGHTS if n not in _BIG and n not in _SMALL_SHARDED)


def _pad_lanes(v, n=LANES):
    return jnp.concatenate([v, jnp.zeros((n - v.shape[0],), v.dtype)]).reshape(1, n)


def _step(x, tgt, W, M, V):
    cx, cy, cc = _place()
    chip = 2 * cx + cy

    small_pack, small_sizes = _pack([W[n] for n in _SMALL_SHARDED])
    first = _run_plan(_gather_plan([W["l0_w_in"].astype(BF16), small_pack]), "gather_first")
    small_full = _unpack(first[1], [W[n].shape for n in _SMALL_SHARDED], small_sizes, lead=(N_CHIPS,))
    ex = _Exchange({n: W[n].astype(BF16) for n in _BIG if n != "l0_w_in"})

    P = {"l0_w_in": first[0]}
    for n, g in zip(_SMALL_SHARDED, small_full):
        P[n] = _cols_from_shards(g)
    for n in _REPL:
        v = W[n]
        P[n] = v.reshape(1, -1) if v.ndim == 1 else v
    P["l0_lru_w_a"] = W["l0_lru_w_a"].astype(BF16)
    P["l0_lru_w_x"] = W["l0_lru_w_x"].astype(BF16)
    P["l1_dt_bias"] = _pad_lanes(W["l1_dt_bias"])
    P["l1_a_log"] = _pad_lanes(W["l1_a_log"])
    P["l1_d_skip"] = jnp.repeat(W["l1_d_skip"], SSD_P).reshape(1, D_SSM)
    meta = P.pop("meta_tokens")

    loss_cols, grad_x, grad_meta, G = _local_step(x, tgt, meta, P, ex)
    G["meta_tokens"] = grad_meta

    core_idx = cc.astype(jnp.int32).reshape(1)
    chip_idx = chip.astype(jnp.int32).reshape(1)
    own_half = [_add8(ex.slabs[n], ex.recv[n], chip_idx, core_idx, f"grad_sum_{n}") for n in _BIG]
    other_half = _run_plan(_sibling_plan(own_half), "grad_sibling_swap")
    small_names = list(_REPL) + list(_SMALL_SHARDED)
    small_list = [G[n] for n in small_names] + [loss_cols]
    spack, ssizes = _pack(small_list, total_mult=2 * SUBLANES * LANES)
    sred = _allreduce_small(spack, "small_allreduce")
    sfull = _unpack(sred, [a.shape for a in small_list], ssizes)
    loss = 0.5 / D_MODEL * jnp.sum(sfull[-1])
    small_grads = {}
    for n, g in zip(small_names, sfull[:-1]):
        if n in _SMALL_SHARDED:
            wcols = W[n].shape[1]
            g = lax.dynamic_slice_in_dim(g, chip * wcols, wcols, axis=1)
        small_grads[n] = g.reshape(W[n].shape)

    grads, delta, new_m, new_v = {}, {}, {}, {}
    for n, own, other in zip(_BIG, own_half, other_half):
        grads[n], delta[n], new_m[n], new_v[n] = _adamw_halves(
            W[n], own, other, M[n], V[n], core_idx, f"adamw_{n}")
    s_names = [n for n in _WEIGHTS if n not in _BIG]
    as2d = lambda a: a.reshape(_shape2d(a.shape))
    outs = _adamw_many([as2d(W[n]) for n in s_names], [as2d(small_grads[n]) for n in s_names],
                       [as2d(M[n]) for n in s_names], [as2d(V[n]) for n in s_names], "adamw_small")
    k = len(s_names)
    for i, n in enumerate(s_names):
        grads[n] = small_grads[n]
        delta[n], new_m[n], new_v[n] = (outs[j * k + i].reshape(W[n].shape) for j in range(3))
    return loss, grad_x, grads, delta, new_m, new_v


def kernel(x, meta_tokens, l0_mix_pre_norm, l0_mix_post_norm, l0_w_in, l0_lru_conv_w, l0_lru_conv_b, l0_lru_w_a, l0_lru_b_a, l0_lru_w_x, l0_lru_b_x, l0_lru_lambda, l0_attn_sinks, l0_w_out, l0_ffn_pre_norm, l0_ffn_post_norm, l0_ffn_w_up, l0_ffn_conv_w, l0_ffn_conv_b, l0_ffn_w_down, l1_mix_pre_norm, l1_mix_post_norm, l1_w_in, l1_ssm_conv_w, l1_ssm_conv_b, l1_dt_bias, l1_a_log, l1_d_skip, l1_gate_norm, l1_w_out, l1_ffn_pre_norm, l1_ffn_post_norm, l1_ffn_w_up, l1_ffn_conv_w, l1_ffn_conv_b, l1_ffn_w_down, loss_target, m_meta_tokens, m_l0_mix_pre_norm, m_l0_mix_post_norm, m_l0_w_in, m_l0_lru_conv_w, m_l0_lru_conv_b, m_l0_lru_w_a, m_l0_lru_b_a, m_l0_lru_w_x, m_l0_lru_b_x, m_l0_lru_lambda, m_l0_attn_sinks, m_l0_w_out, m_l0_ffn_pre_norm, m_l0_ffn_post_norm, m_l0_ffn_w_up, m_l0_ffn_conv_w, m_l0_ffn_conv_b, m_l0_ffn_w_down, m_l1_mix_pre_norm, m_l1_mix_post_norm, m_l1_w_in, m_l1_ssm_conv_w, m_l1_ssm_conv_b, m_l1_dt_bias, m_l1_a_log, m_l1_d_skip, m_l1_gate_norm, m_l1_w_out, m_l1_ffn_pre_norm, m_l1_ffn_post_norm, m_l1_ffn_w_up, m_l1_ffn_conv_w, m_l1_ffn_conv_b, m_l1_ffn_w_down, v_meta_tokens, v_l0_mix_pre_norm, v_l0_mix_post_norm, v_l0_w_in, v_l0_lru_conv_w, v_l0_lru_conv_b, v_l0_lru_w_a, v_l0_lru_b_a, v_l0_lru_w_x, v_l0_lru_b_x, v_l0_lru_lambda, v_l0_attn_sinks, v_l0_w_out, v_l0_ffn_pre_norm, v_l0_ffn_post_norm, v_l0_ffn_w_up, v_l0_ffn_conv_w, v_l0_ffn_conv_b, v_l0_ffn_w_down, v_l1_mix_pre_norm, v_l1_mix_post_norm, v_l1_w_in, v_l1_ssm_conv_w, v_l1_ssm_conv_b, v_l1_dt_bias, v_l1_a_log, v_l1_d_skip, v_l1_gate_norm, v_l1_w_out, v_l1_ffn_pre_norm, v_l1_ffn_post_norm, v_l1_ffn_w_up, v_l1_ffn_conv_w, v_l1_ffn_conv_b, v_l1_ffn_w_down):
    args = locals()
    W = {n: args[n] for n in _WEIGHTS}
    M = {n: args["m_" + n] for n in _WEIGHTS}
    V = {n: args["v_" + n] for n in _WEIGHTS}
    loss, grad_x, grads, delta, new_m, new_v = _step(x[0], loss_target[0], W, M, V)
    return (loss, grad_x[None], *[grads[n] for n in _WEIGHTS], *[delta[n] for n in _WEIGHTS],
            *[new_m[n] for n in _WEIGHTS], *[new_v[n] for n in _WEIGHTS])
```

```python
import functools
import math

import jax
import jax.numpy as jnp
from jax import lax
from jax.experimental import pallas as pl
from jax.experimental.pallas import tpu as pltpu

F32 = jnp.float32
BF16 = jnp.bfloat16

D_MODEL = 1024
N_META = 16
BLOCK = 128
PAD = BLOCK - N_META
EPS = 1e-6
LRU_BLOCKS = 8
LRU_BS = 128
LRU_C = 8.0
N_Q_HEADS = 16
N_KV_HEADS = 2
HEAD_DIM = 64
Q_PER_KV = 8
WINDOW = 128
D_SSM = 2048
SSD_HEADS = 32
SSD_GROUPS = 8
SSD_HPG = 4
SSD_P = 64
SSD_N = 128
D_FF = 2816
NEG = -1e30
LANES = 128
SUBLANES = 8
_VMEM_LIMIT_WIDE = 62 * 1024 * 1024

ADAM_LR = 0.001
ADAM_B1 = 0.9
ADAM_B2 = 0.999
ADAM_EPS = 1e-08
ADAM_WD = 0.01
ADAM_STEP = 10

MESH = pl.DeviceIdType.MESH
N_CHIPS = 4


def _pick(n, cands):
    for c in cands:
        if n % c == 0:
            return c
    raise ValueError(f"no tile for {n} in {cands}")


def _col_tile(n, limit=1792):
    best = None
    for t in range(LANES, min(n, limit) + 1, LANES):
        if n % t == 0:
            best = t
    if best is None:
        raise ValueError(f"no lane tile for {n}")
    return best


def _sigmoid(x):
    return 0.5 + 0.5 * jnp.tanh(0.5 * x)


def _log1p(e):
    u = 1.0 + e
    return jnp.where(u == 1.0, e, jnp.log(u) * (e / jnp.where(u == 1.0, 1.0, u - 1.0)))


def _softplus(x):
    return jnp.maximum(x, 0.0) + _log1p(jnp.exp(-jnp.abs(x)))


def _neg_expm1(x):
    poly = x * (1.0 + x * (0.5 + x * (1.0 / 6.0 + x * (1.0 / 24.0 + x * (1.0 / 120.0)))))
    return -jnp.where(x > -0.05, poly, jnp.exp(x) - 1.0)


_GELU_C = math.sqrt(2.0 / math.pi)


def _gelu(x):
    u = 0.5 + 0.5 * jnp.tanh(x * (_GELU_C + (_GELU_C * 0.044715) * (x * x)))
    return x * u


def _gelu_and_grad(x):
    x2 = x * x
    u = 0.5 + 0.5 * jnp.tanh(x * (_GELU_C + (_GELU_C * 0.044715) * x2))
    g = x * u
    dg = u * (1.0 + (x - g) * (2.0 * _GELU_C + (6.0 * 0.044715 * _GELU_C) * x2))
    return g, dg


def _silu_and_grad(x):
    s = _sigmoid(x)
    return x * s, s * (1.0 + x * (1.0 - s))


def _dot(a, b):
    return jnp.dot(a, b, preferred_element_type=F32)


def _dot_nt(a, b):
    return lax.dot_general(a, b, (((1,), (1,)), ((), ())), preferred_element_type=F32)


def _dot_tn(a, b):
    return lax.dot_general(a, b, (((0,), (0,)), ((), ())), preferred_element_type=F32)


def _row_iota(t):
    return lax.broadcasted_iota(jnp.int32, (t, 1), 0)


def _scan_fwd(a, u, t):
    row = _row_iota(t)
    d = 1
    while d < t:
        m = row >= d
        u_sh = jnp.where(m, pltpu.roll(u, d, 0), 0.0)
        a_sh = jnp.where(m, pltpu.roll(a, d, 0), 1.0)
        u = u + a * u_sh
        a = a * a_sh
        d *= 2
    return a, u


def _scan_rev(c, x, t):
    row = _row_iota(t)
    d = 1
    while d < t:
        m = row < t - d
        x_sh = jnp.where(m, pltpu.roll(x, t - d, 0), 0.0)
        c_sh = jnp.where(m, pltpu.roll(c, t - d, 0), 1.0)
        x = x + c * x_sh
        c = c * c_sh
        d *= 2
    return c, x


def _cumsum_rows(x, t):
    row = _row_iota(t)
    d = 1
    while d < t:
        x = x + jnp.where(row >= d, pltpu.roll(x, d, 0), 0.0)
        d *= 2
    return x


def _rev_cumsum_rows(x, t):
    row = _row_iota(t)
    d = 1
    while d < t:
        x = x + jnp.where(row < t - d, pltpu.roll(x, t - d, 0), 0.0)
        d *= 2
    return x


def _rms_bwd(x, g, dy):
    rs = lax.rsqrt(jnp.mean(x * x, axis=-1, keepdims=True) + EPS)
    gy = dy * g
    dx = rs * gy - x * (rs * rs * rs) * jnp.mean(x * gy, axis=-1, keepdims=True)
    return dx, dy * x * rs


def _mm_nt(dy, w, name):
    m, n = dy.shape
    k = w.shape[0]
    wide = n > 3328
    tm = _pick(m, (320, 256, 128)) if wide else _pick(m, (640, 512, 256, 128))
    tk = _col_tile(k, 512 if wide else 1408)

    def body(dy_ref, w_ref, o_ref):
        o_ref[...] = _dot_nt(dy_ref[...].astype(BF16), w_ref[...])

    return pl.pallas_call(
        body, name=name, grid=(k // tk, m // tm),
        in_specs=[pl.BlockSpec((tm, n), lambda j, i: (i, 0)),
                  pl.BlockSpec((tk, n), lambda j, i: (j, 0))],
        out_specs=pl.BlockSpec((tm, tk), lambda j, i: (i, j)),
        out_shape=jax.ShapeDtypeStruct((m, k), F32),
        compiler_params=pltpu.CompilerParams(dimension_semantics=("parallel", "parallel")),
    )(dy, w)


def _mm_tn(a, dy, name):
    m, k = a.shape
    n = dy.shape[1]
    tm = _pick(m, (640, 512, 256, 128))
    tk = _col_tile(k, 1408)
    tn = _col_tile(n, 1664)
    nsteps = m // tm

    def body(a_ref, dy_ref, o_ref, acc):
        @pl.when(pl.program_id(2) == 0)
        def _():
            acc[...] = jnp.zeros_like(acc)

        acc[...] += _dot_tn(a_ref[...].astype(BF16), dy_ref[...].astype(BF16))

        @pl.when(pl.program_id(2) == nsteps - 1)
        def _():
            o_ref[...] = acc[...].astype(o_ref.dtype)

    return pl.pallas_call(
        body, name=name, grid=(k // tk, n // tn, nsteps),
        in_specs=[pl.BlockSpec((tm, tk), lambda kk, j, i: (i, kk)),
                  pl.BlockSpec((tm, tn), lambda kk, j, i: (i, j))],
        out_specs=pl.BlockSpec((tk, tn), lambda kk, j, i: (kk, j)),
        out_shape=jax.ShapeDtypeStruct((k, n), BF16),
        scratch_shapes=[pltpu.VMEM((tk, tn), F32)],
        compiler_params=pltpu.CompilerParams(
            dimension_semantics=("parallel", "parallel", "arbitrary")),
    )(a, dy)


def _mm_nn_sh(a, w4, n_out, name):
    m, k = a.shape
    s, _, n = w4.shape
    tm = _pick(m, (320, 256, 128))

    def body(a_ref, w_ref, o_ref):
        av = a_ref[...].astype(BF16)
        for j in range(s):
            o_ref[:, j * n:(j + 1) * n] = _dot(av, w_ref[j])
        if n_out > s * n:
            o_ref[:, s * n:] = jnp.zeros((tm, n_out - s * n), F32)

    return pl.pallas_call(
        body, name=name, grid=(m // tm,),
        in_specs=[pl.BlockSpec((tm, k), lambda i: (i, 0)),
                  pl.BlockSpec((s, k, n), lambda i: (0, 0, 0))],
        out_specs=pl.BlockSpec((tm, n_out), lambda i: (i, 0)),
        out_shape=jax.ShapeDtypeStruct((m, n_out), F32),
        compiler_params=pltpu.CompilerParams(dimension_semantics=("parallel",)),
    )(a, w4)


def _mm_nt_sh(dy, w4, name, plan=None):
    dys = dy if isinstance(dy, (tuple, list)) else (dy,)
    m = dys[0].shape[0]
    s, k, n = w4.shape
    tm = _pick(m, (640, 512, 256, 128))
    tk = _col_tile(k, 1024)
    where = _shard_columns(dys, s, n)
    p_in, p_shapes, p_out, p_scr = _plan_parts(plan)
    nd, nj, ni = len(dys), k // tk, m // tm

    def body(*refs):
        w_ref = refs[nd]
        cins = refs[nd + 1:nd + 1 + len(p_in)]
        o_ref = refs[nd + 1 + len(p_in)]
        couts = refs[nd + 2 + len(p_in):nd + 2 + len(p_in) + len(p_out)]
        sems = refs[nd + 2 + len(p_in) + len(p_out):]
        step = pl.program_id(0) * ni + pl.program_id(1)
        if plan is not None:
            @pl.when(step == 0)
            def _():
                plan.start(cins, couts, sems)

        acc = None
        for j, (p, c0) in enumerate(where):
            t = _dot_nt(refs[p][:, c0:c0 + n].astype(BF16), w_ref[j])
            acc = t if acc is None else acc + t
        o_ref[...] = acc
        if plan is not None:
            @pl.when(step == nj * ni - 1)
            def _():
                plan.wait(cins, couts, sems)

    sem = ("arbitrary", "arbitrary") if plan is not None else ("parallel", "parallel")
    res = pl.pallas_call(
        body, name=name, grid=(nj, ni),
        in_specs=[pl.BlockSpec((tm, d.shape[1]), lambda j, i: (i, 0)) for d in dys]
        + [pl.BlockSpec((s, tk, n), lambda j, i: (0, j, 0))] + p_in,
        out_specs=[pl.BlockSpec((tm, tk), lambda j, i: (i, j))] + p_out,
        out_shape=[jax.ShapeDtypeStruct((m, k), F32)] + p_shapes,
        scratch_shapes=p_scr,
        compiler_params=pltpu.CompilerParams(dimension_semantics=sem),
    )(*dys, w4, *(plan.ins if plan is not None else []))
    return res[0] if plan is None else (res[0], res[1:])


def _shard_columns(dys, s, n):
    where = []
    for p, d in enumerate(dys):
        where += [(p, c * n) for c in range(d.shape[1] // n)]
    assert len(where) >= s
    return where[:s]


def _mm_tn_sh(a, dy, n, name):
    dys = dy if isinstance(dy, (tuple, list)) else (dy,)
    m, k = a.shape
    s = N_CHIPS
    tm = _pick(m, (640, 512, 256, 128))
    tk = _col_tile(k, 512)
    nsteps = m // tm
    where = _shard_columns(dys, s, n)

    def body(*refs):
        a_ref, o_ref, acc = refs[0], refs[len(dys) + 1], refs[len(dys) + 2]

        @pl.when(pl.program_id(1) == 0)
        def _():
            acc[...] = jnp.zeros_like(acc)

        av = a_ref[...].astype(BF16)
        for j, (p, c0) in enumerate(where):
            acc[j] += _dot_tn(av, refs[1 + p][:, c0:c0 + n].astype(BF16))

        @pl.when(pl.program_id(1) == nsteps - 1)
        def _():
            o_ref[...] = acc[...].astype(o_ref.dtype)

    return pl.pallas_call(
        body, name=name, grid=(k // tk, nsteps),
        in_specs=[pl.BlockSpec((tm, tk), lambda kk, i: (i, kk))]
        + [pl.BlockSpec((tm, d.shape[1]), lambda kk, i: (i, 0)) for d in dys],
        out_specs=pl.BlockSpec((s, tk, n), lambda kk, i: (0, kk, 0)),
        out_shape=jax.ShapeDtypeStruct((s, k, n), BF16),
        scratch_shapes=[pltpu.VMEM((s, tk, n), F32)],
        compiler_params=pltpu.CompilerParams(dimension_semantics=("parallel", "arbitrary"),
                                             vmem_limit_bytes=_VMEM_LIMIT_WIDE),
    )(a, *dys)


def _rowcall(name, body, lp, tm, rows=(), prevs=(), vecs=(), outs=(), accs=(), scratch=(),
             reverse=False, seq=False, plan=None):
    p_in, p_shapes, p_out, p_scr = _plan_parts(plan)
    nt = lp // tm
    hb = tm // SUBLANES

    def ri(i):
        return nt - 1 - i if reverse else i

    in_specs, args = [], []
    for arr, w, cb in rows:
        in_specs.append(pl.BlockSpec((tm, w), lambda i, cb=cb: (ri(i), cb)))
        args.append(arr)
    for arr, w, cb in prevs:
        in_specs.append(pl.BlockSpec((SUBLANES, w), lambda i, cb=cb: (jnp.maximum(ri(i) * hb - 1, 0), cb)))
        args.append(arr)
    for arr in vecs:
        in_specs.append(pl.BlockSpec(arr.shape, lambda i, nd=arr.ndim: (0,) * nd))
        args.append(arr)
    out_shape, out_specs = [], []
    for w, dt in outs:
        out_shape.append(jax.ShapeDtypeStruct((lp, w), dt))
        out_specs.append(pl.BlockSpec((tm, w), lambda i: (ri(i), 0)))
    for shp, dt in accs:
        out_shape.append(jax.ShapeDtypeStruct(shp, dt))
        out_specs.append(pl.BlockSpec(shp, lambda i, nd=len(shp): (0,) * nd))

    n_in, n_out, n_scr = len(args), len(out_shape), len(scratch)

    def kern(*refs):
        i = pl.program_id(0)
        own = (refs[:n_in] + refs[n_in + len(p_in):n_in + len(p_in) + n_out]
               + refs[n_in + len(p_in) + n_out + len(p_out):n_in + len(p_in) + n_out + len(p_out) + n_scr])
        cins = refs[n_in:n_in + len(p_in)]
        couts = refs[n_in + len(p_in) + n_out:n_in + len(p_in) + n_out + len(p_out)]
        sems = refs[n_in + len(p_in) + n_out + len(p_out) + n_scr:]
        if plan is not None:
            @pl.when(i == 0)
            def _():
                plan.start(cins, couts, sems)

        body(ri(i), i == 0, *own)
        if plan is not None:
            @pl.when(i == nt - 1)
            def _():
                plan.wait(cins, couts, sems)

    sem = ("arbitrary",) if (seq or accs or plan is not None) else ("parallel",)
    res = pl.pallas_call(
        kern, name=name, grid=(nt,), in_specs=in_specs + p_in, out_specs=out_specs + p_out,
        out_shape=out_shape + p_shapes, scratch_shapes=list(scratch) + p_scr,
        compiler_params=pltpu.CompilerParams(dimension_semantics=sem),
    )(*args, *(plan.ins if plan is not None else []))
    return res if plan is None else (res[:n_out], res[n_out:])


def _acc_add(first, ref, val):
    @pl.when(first)
    def _():
        ref[...] = jnp.zeros_like(ref)

    ref[...] += val


def _real_rows(r, tm):
    return (r * tm + _row_iota(tm)) >= PAD


def _rmsnorm_fwd(h, g, name):
    lp, d = h.shape
    tm = _pick(lp, (640, 512, 256, 128))

    def body(r, first, h_ref, g_ref, u_ref):
        x = h_ref[...]
        rs = lax.rsqrt(jnp.mean(x * x, axis=-1, keepdims=True) + EPS)
        u_ref[...] = (x * rs * g_ref[...]).astype(u_ref.dtype)

    return _rowcall(name, body, lp, tm, rows=[(h, d, 0)], vecs=[g], outs=[(d, BF16)])[0]


def _mm_postnorm_res(a, w, h, g, name, next_pre=None):
    parts = a if isinstance(a, (tuple, list)) else (a,)
    lp, d = h.shape
    k = w.shape[0]
    tm = _pick(lp, (640, 512, 256, 128))
    offs = [sum(p.shape[1] for p in parts[:i]) for i in range(len(parts))]
    np_ = len(parts)

    def body(*refs):
        w_ref, h_ref, g_ref = refs[np_], refs[np_ + 1], refs[np_ + 2]
        rest = refs[np_ + 3:]
        acc = None
        for a_ref, p, off in zip(refs, parts, offs):
            t = _dot(a_ref[...].astype(BF16), w_ref[off:off + p.shape[1], :])
            acc = t if acc is None else acc + t
        outs = rest[1:] if next_pre is not None else rest
        outs[0][...] = acc
        rs = lax.rsqrt(jnp.mean(acc * acc, axis=-1, keepdims=True) + EPS)
        hn = jnp.where(_real_rows(pl.program_id(0), tm), h_ref[...] + acc * rs * g_ref[...], 0.0)
        outs[1][...] = hn
        if next_pre is not None:
            rs2 = lax.rsqrt(jnp.mean(hn * hn, axis=-1, keepdims=True) + EPS)
            outs[2][...] = (hn * rs2 * rest[0][...]).astype(BF16)

    row = pl.BlockSpec((tm, d), lambda i: (i, 0))
    vec = pl.BlockSpec((1, d), lambda i: (0, 0))
    n_vec = 2 if next_pre is not None else 1
    return pl.pallas_call(
        body, name=name, grid=(lp // tm,),
        in_specs=[pl.BlockSpec((tm, p.shape[1]), lambda i: (i, 0)) for p in parts]
        + [pl.BlockSpec((k, d), lambda i: (0, 0)), row] + [vec] * n_vec,
        out_specs=[row] * (2 + (next_pre is not None)),
        out_shape=[jax.ShapeDtypeStruct((lp, d), F32)] * 2
        + ([jax.ShapeDtypeStruct((lp, d), BF16)] if next_pre is not None else []),
        compiler_params=pltpu.CompilerParams(dimension_semantics=("parallel",)),
    )(*parts, w, h, g, *([next_pre] if next_pre is not None else []))


def _postnorm_bwd(o, g, dh, name):
    lp, d = o.shape
    tm = _pick(lp, (640, 512, 256, 128))

    def body(r, first, o_ref, dh_ref, g_ref, do_ref, dg_ref):
        dx, dgt = _rms_bwd(o_ref[...], g_ref[...], dh_ref[...])
        do_ref[...] = dx.astype(do_ref.dtype)
        _acc_add(first, dg_ref, jnp.sum(dgt, axis=0, keepdims=True))

    return _rowcall(name, body, lp, tm, rows=[(o, d, 0), (dh, d, 0)], vecs=[g],
                    outs=[(d, BF16)], accs=[((1, d), F32)])


def _prenorm_bwd(h, g, du, dh_res, name):
    lp, d = h.shape
    tm = _pick(lp, (640, 512, 256, 128))

    def body(r, first, h_ref, du_ref, dres_ref, g_ref, dh_ref, dg_ref):
        dx, dgt = _rms_bwd(h_ref[...], g_ref[...], du_ref[...])
        dh_ref[...] = jnp.where(_real_rows(r, tm), dres_ref[...] + dx, 0.0)
        _acc_add(first, dg_ref, jnp.sum(dgt, axis=0, keepdims=True))

    return _rowcall(name, body, lp, tm, rows=[(h, d, 0), (du, d, 0), (dh_res, d, 0)], vecs=[g],
                    outs=[(d, F32)], accs=[((1, d), F32)])


def _loss_fwd_bwd(h, tgt, name):
    lp, d = h.shape
    tm = _pick(lp, (640, 512, 256, 128))

    def body(r, first, h_ref, t_ref, dh_ref, ls_ref):
        tok = (r * tm + _row_iota(tm)) >= BLOCK
        e = jnp.where(tok, h_ref[...] - t_ref[...], 0.0)
        dh_ref[...] = e * (1.0 / d)
        _acc_add(first, ls_ref, jnp.sum(e * e, axis=0, keepdims=True))

    return _rowcall(name, body, lp, tm, rows=[(h, d, 0), (tgt, d, 0)],
                    outs=[(d, F32)], accs=[((1, d), F32)])


def _conv_tiles(lp, width):
    wc = _col_tile(width, 1408)
    tm = _pick(lp, (320, 256, 128))
    return tm, wc


def _conv_fwd(x, col_off, width, w, b, name):
    lp = x.shape[0]
    kk = w.shape[0]
    tm, wc = _conv_tiles(lp, width)
    offb = col_off // wc
    assert col_off % wc == 0
    hb = tm // SUBLANES

    def body(x_ref, xp_ref, w_ref, b_ref, y_ref):
        i = pl.program_id(1)
        xv = x_ref[...]
        halo = jnp.where(i > 0, xp_ref[...], 0.0)
        xx = jnp.concatenate([halo, xv], axis=0)
        acc = b_ref[...] + w_ref[kk - 1:kk, :] * xv
        for j in range(1, kk):
            acc = acc + w_ref[kk - 1 - j:kk - j, :] * pltpu.roll(xx, j, 0)[SUBLANES:, :]
        y_ref[...] = acc

    return pl.pallas_call(
        body, name=name, grid=(width // wc, lp // tm),
        in_specs=[pl.BlockSpec((tm, wc), lambda j, i: (i, offb + j)),
                  pl.BlockSpec((SUBLANES, wc), lambda j, i: (jnp.maximum(i * hb - 1, 0), offb + j)),
                  pl.BlockSpec((kk, wc), lambda j, i: (0, j)),
                  pl.BlockSpec((1, wc), lambda j, i: (0, j))],
        out_specs=pl.BlockSpec((tm, wc), lambda j, i: (i, j)),
        out_shape=jax.ShapeDtypeStruct((lp, width), F32),
        compiler_params=pltpu.CompilerParams(dimension_semantics=("parallel", "parallel")),
    )(x, x, w, b)


def _conv_bwd(x, col_off, width, dy, w, name, w_col_off=0):
    lp = x.shape[0]
    kk = w.shape[0]
    tm, wc = _conv_tiles(lp, width)
    offb = col_off // wc
    woffb = w_col_off // wc
    assert col_off % wc == 0 and w_col_off % wc == 0
    hrows = SUBLANES * (4 // dy.dtype.itemsize)
    ext = tm + hrows

    def body(x_ref, dy_ref, dn_ref, w_ref, dx_ref, dw_ref, db_ref):
        i = pl.program_id(1)
        last = pl.num_programs(1) - 1
        xv = x_ref[...]
        dyv = dy_ref[...].astype(F32)
        dd = jnp.concatenate([dyv, jnp.where(i < last, dn_ref[...].astype(F32), 0.0)], axis=0)
        dx = w_ref[kk - 1:kk, :] * dyv
        rows = [jnp.sum(dyv * xv, axis=0, keepdims=True)]
        for m in range(1, kk):
            ahead = pltpu.roll(dd, ext - m, 0)[:tm, :]
            dx = dx + w_ref[kk - 1 - m:kk - m, :] * ahead
            rows.append(jnp.sum(ahead * xv, axis=0, keepdims=True))
        dx_ref[...] = dx.astype(dx_ref.dtype)
        dwp = jnp.concatenate(rows[::-1] + [jnp.zeros((SUBLANES - kk, wc), F32)], axis=0)

        @pl.when(i == 0)
        def _():
            dw_ref[...] = jnp.zeros_like(dw_ref)
            db_ref[...] = jnp.zeros_like(db_ref)

        dw_ref[...] += dwp
        db_ref[...] += jnp.sum(dyv, axis=0, keepdims=True)

    return pl.pallas_call(
        body, name=name, grid=(width // wc, lp // tm),
        in_specs=[pl.BlockSpec((tm, wc), lambda j, i: (i, offb + j)),
                  pl.BlockSpec((tm, wc), lambda j, i: (i, j)),
                  pl.BlockSpec((hrows, wc), lambda j, i: (jnp.minimum((i + 1) * (tm // hrows), lp // hrows - 1), j)),
                  pl.BlockSpec((kk, wc), lambda j, i: (0, woffb + j))],
        out_specs=[pl.BlockSpec((tm, wc), lambda j, i: (i, j)),
                   pl.BlockSpec((SUBLANES, wc), lambda j, i: (0, j)),
                   pl.BlockSpec((1, wc), lambda j, i: (0, j))],
        out_shape=[jax.ShapeDtypeStruct((lp, width), BF16),
                   jax.ShapeDtypeStruct((SUBLANES, width), F32),
                   jax.ShapeDtypeStruct((1, width), F32)],
        compiler_params=pltpu.CompilerParams(dimension_semantics=("parallel", "arbitrary")),
    )(x, dy, dy, w)


_FFN_K = 3
_FFN_WC = 1408


def _conv3_ext(x_ext, w_ref, b_ref):
    return (b_ref[...] + w_ref[2:3, :] * x_ext + w_ref[1:2, :] * pltpu.roll(x_ext, 1, 0)
            + w_ref[0:1, :] * pltpu.roll(x_ext, 2, 0))


def _ffn_convact_fwd(hp, cw, cb, name, plan=None):
    lp = hp.shape[0]
    tm = _pick(lp, (320, 256, 128))
    wc = _FFN_WC
    nj = D_FF // wc
    ni = lp // tm
    hb = tm // SUBLANES
    p_in, p_shapes, p_out, p_scr = _plan_parts(plan)

    def body(*refs):
        g_ref, gp_ref, u_ref, up_ref, wg_ref, wu_ref, bg_ref, bu_ref = refs[:8]
        cins = refs[8:8 + len(p_in)]
        a_ref, hg_ref, hu_ref = refs[8 + len(p_in):11 + len(p_in)]
        couts = refs[11 + len(p_in):11 + len(p_in) + len(p_out)]
        sems = refs[11 + len(p_in) + len(p_out):]
        i = pl.program_id(1)
        step = pl.program_id(0) * ni + i
        if plan is not None:
            @pl.when(step == 0)
            def _():
                plan.start(cins, couts, sems)

        def conv(x_ref, p_ref, w_ref, b_ref):
            x_ext = jnp.concatenate([jnp.where(i > 0, p_ref[...], 0.0), x_ref[...]], axis=0)
            return _conv3_ext(x_ext, w_ref, b_ref)[SUBLANES:, :]

        hg = conv(g_ref, gp_ref, wg_ref, bg_ref)
        hu = conv(u_ref, up_ref, wu_ref, bu_ref)
        a_ref[...] = (_gelu(hg) * hu).astype(a_ref.dtype)
        hg_ref[...] = hg.astype(hg_ref.dtype)
        hu_ref[...] = hu.astype(hu_ref.dtype)
        if plan is not None:
            @pl.when(step == nj * ni - 1)
            def _():
                plan.wait(cins, couts, sems)

    tile = lambda off: pl.BlockSpec((tm, wc), lambda j, i: (i, off + j))
    prev = lambda off: pl.BlockSpec((SUBLANES, wc), lambda j, i: (jnp.maximum(i * hb - 1, 0), off + j))
    vec = lambda rows, off: pl.BlockSpec((rows, wc), lambda j, i: (0, off + j))
    sem = ("arbitrary", "arbitrary") if plan is not None else ("parallel", "parallel")
    res = pl.pallas_call(
        body, name=name, grid=(nj, ni),
        in_specs=[tile(0), prev(0), tile(nj), prev(nj), vec(_FFN_K, 0), vec(_FFN_K, nj), vec(1, 0),
                  vec(1, nj)] + p_in,
        out_specs=[tile(0)] * 3 + p_out,
        out_shape=[jax.ShapeDtypeStruct((lp, D_FF), BF16)] * 3 + p_shapes,
        scratch_shapes=p_scr,
        compiler_params=pltpu.CompilerParams(dimension_semantics=sem),
    )(hp, hp, hp, hp, cw, cw, cb, cb, *(plan.ins if plan is not None else []))
    return res if plan is None else (res[:3], res[3:])


def _ffn_down_act_bwd(do, w_down, hg, hu, name):
    lp, d = do.shape
    tm = _pick(lp, (640, 512, 256, 128))
    tk = _FFN_WC

    def body(do_ref, w_ref, g_ref, u_ref, dg_ref, du_ref):
        da = _dot_nt(do_ref[...], w_ref[...])
        gl, dgl = _gelu_and_grad(g_ref[...].astype(F32))
        dg_ref[...] = (da * u_ref[...].astype(F32) * dgl).astype(dg_ref.dtype)
        du_ref[...] = (da * gl).astype(du_ref.dtype)

    tile = pl.BlockSpec((tm, tk), lambda j, i: (i, j))
    return pl.pallas_call(
        body, name=name, grid=(D_FF // tk, lp // tm),
        in_specs=[pl.BlockSpec((tm, d), lambda j, i: (i, 0)),
                  pl.BlockSpec((tk, d), lambda j, i: (j, 0)), tile, tile],
        out_specs=[tile, tile],
        out_shape=[jax.ShapeDtypeStruct((lp, D_FF), BF16)] * 2,
        compiler_params=pltpu.CompilerParams(dimension_semantics=("parallel", "parallel")),
    )(do, w_down, hg, hu)


def _lru_gates(x, wa_ref, wx_ref, ba, bx, lam):
    xb = x.astype(BF16)
    za, zx = [], []
    for n in range(LRU_BLOCKS):
        xs = xb[:, n * LRU_BS:(n + 1) * LRU_BS]
        za.append(_dot(xs, wa_ref[n]))
        zx.append(_dot(xs, wx_ref[n]))
    r = _sigmoid(jnp.concatenate(za, axis=1) + ba)
    ig = _sigmoid(jnp.concatenate(zx, axis=1) + bx)
    sp = _softplus(-lam)
    log_a = -LRU_C * r * sp
    a = jnp.exp(log_a)
    om = _neg_expm1(2.0 * log_a)
    mult = jnp.sqrt(om)
    return xb, r, ig, sp, a, om, mult


def _lru_fwd(proj, xrc, wa, wx, ba, bx, lam, name, plan=None):
    lp, d = xrc.shape
    tm = BLOCK

    def body(r_idx, first, gate_ref, x_ref, wa_ref, wx_ref, ba_ref, bx_ref, lam_ref,
             y_ref, h_ref, carry):
        @pl.when(first)
        def _():
            carry[...] = jnp.zeros_like(carry)

        x = x_ref[...]
        _, _, ig, _, a, _, mult = _lru_gates(x, wa_ref, wx_ref, ba_ref[...], bx_ref[...], lam_ref[...])
        u = jnp.where(_real_rows(r_idx, tm), mult * ig * x, 0.0)
        acum, hloc = _scan_fwd(a, u, tm)
        h = hloc + acum * carry[0:1, :]
        h_ref[...] = h
        carry[0:1, :] = h[tm - 1:tm, :]
        y_ref[...] = (_gelu(gate_ref[...]) * h).astype(y_ref.dtype)

    return _rowcall(name, body, lp, tm, rows=[(proj, d, 0), (xrc, d, 0)],
                    vecs=[wa, wx, ba, bx, lam], outs=[(d, BF16), (d, F32)],
                    scratch=[pltpu.VMEM((SUBLANES, d), F32)], seq=True, plan=plan)


def _lru_bwd(proj, xrc, hl, dmix, wa, wx, ba, bx, lam, name, plan=None):
    lp, d = xrc.shape
    tm = BLOCK

    def body(r_idx, first, gate_ref, x_ref, h_ref, dy_ref, hp_ref, wa_ref, wx_ref, ba_ref, bx_ref,
             lam_ref, dgate_ref, dx_ref, dwa_ref, dwx_ref, dba_ref, dbx_ref, dlam_ref, carry):
        @pl.when(first)
        def _():
            carry[...] = jnp.zeros_like(carry)
            dwa_ref[...] = jnp.zeros_like(dwa_ref)
            dwx_ref[...] = jnp.zeros_like(dwx_ref)
            dba_ref[...] = jnp.zeros_like(dba_ref)
            dbx_ref[...] = jnp.zeros_like(dbx_ref)
            dlam_ref[...] = jnp.zeros_like(dlam_ref)

        x = x_ref[...]
        lam = lam_ref[...]
        xb, r, ig, sp, a, om, mult = _lru_gates(x, wa_ref, wx_ref, ba_ref[...], bx_ref[...], lam)
        h = h_ref[...]
        dy = dy_ref[...]
        gl, dgl = _gelu_and_grad(gate_ref[...])
        dgate_ref[...] = (dy * h * dgl).astype(dgate_ref.dtype)
        row = _row_iota(tm)
        lastrow = row == tm - 1
        xg = dy * gl + jnp.where(lastrow, carry[0:1, :], 0.0)
        c = jnp.where(lastrow, 1.0, pltpu.roll(a, tm - 1, 0))
        _, g = _scan_rev(c, xg, tm)
        carry[0:1, :] = a[0:1, :] * g[0:1, :]
        hprev_in = jnp.where(r_idx > 0, hp_ref[SUBLANES - 1:SUBLANES, :], 0.0)
        hprev = jnp.where(row == 0, hprev_in, pltpu.roll(h, 1, 0))
        du = jnp.where(_real_rows(r_idx, tm), g, 0.0)
        da = g * hprev
        dmult = du * ig * x
        dig = du * mult * x
        dxv = du * mult * ig
        e2 = 1.0 - om
        dlog_a = da * a - dmult * e2 / mult
        dr = dlog_a * (-LRU_C) * sp
        dsp = jnp.sum(dlog_a * (-LRU_C) * r, axis=0, keepdims=True)
        dlam_ref[...] += -dsp * _sigmoid(-lam)
        dza = dr * r * (1.0 - r)
        dzx = dig * ig * (1.0 - ig)
        dba_ref[...] += jnp.sum(dza, axis=0, keepdims=True)
        dbx_ref[...] += jnp.sum(dzx, axis=0, keepdims=True)
        dzab = dza.astype(BF16)
        dzxb = dzx.astype(BF16)
        parts = []
        for n in range(LRU_BLOCKS):
            sl = slice(n * LRU_BS, (n + 1) * LRU_BS)
            dwa_ref[n] += _dot_tn(xb[:, sl], dzab[:, sl])
            dwx_ref[n] += _dot_tn(xb[:, sl], dzxb[:, sl])
            parts.append(_dot_nt(dzab[:, sl], wa_ref[n]) + _dot_nt(dzxb[:, sl], wx_ref[n]))
        dx_ref[...] = dxv + jnp.concatenate(parts, axis=1)

    return _rowcall(name, body, lp, tm,
                    rows=[(proj, d, 0), (xrc, d, 0), (hl, d, 0), (dmix, d, 0)],
                    prevs=[(hl, d, 0)], vecs=[wa, wx, ba, bx, lam],
                    outs=[(d, BF16), (d, F32)],
                    accs=[((LRU_BLOCKS, LRU_BS, LRU_BS), F32), ((LRU_BLOCKS, LRU_BS, LRU_BS), F32),
                          ((1, d), F32), ((1, d), F32), ((1, d), F32)],
                    scratch=[pltpu.VMEM((SUBLANES, d), F32)], reverse=True, seq=True, plan=plan)


_SLOPES = [2.0 ** (-8.0 * (h + 1) / N_Q_HEADS) for h in range(N_Q_HEADS)]
_QK_SCALE = HEAD_DIM ** -0.5
_QCOL = 2 * D_MODEL // D_MODEL
_KCOL = (3 * D_MODEL) // LANES
_VCOL = _KCOL + 1


def _attn_masks(n):
    start = pl.multiple_of(jnp.maximum(n - 1, 0) * BLOCK, BLOCK)
    qi = n * BLOCK + lax.broadcasted_iota(jnp.int32, (BLOCK, 2 * BLOCK), 0)
    kj = start + lax.broadcasted_iota(jnp.int32, (BLOCK, 2 * BLOCK), 1)
    dist = qi - kj
    ok = (kj >= BLOCK) & (dist >= 0) & (dist < WINDOW)
    dm = (n * BLOCK - PAD + lax.broadcasted_iota(jnp.int32, (BLOCK, N_META), 0)
          - lax.broadcasted_iota(jnp.int32, (BLOCK, N_META), 1))
    okm = dm >= 0
    return start, ok, dist.astype(F32), okm, jnp.minimum(dm, WINDOW).astype(F32)


def _group_rows(ref, g, scale=None):
    x = jnp.concatenate(
        [ref[:, (g * Q_PER_KV + hh) * HEAD_DIM:(g * Q_PER_KV + hh + 1) * HEAD_DIM] for hh in range(Q_PER_KV)],
        axis=0)
    return (x if scale is None else x * scale).astype(BF16)


def _attn_probs(s, sm, sink_ref, g, ok, distf, okm, dmf):
    slope = jnp.stack([jnp.full((1, 1), _SLOPES[g * Q_PER_KV + hh], F32) for hh in range(Q_PER_KV)])
    sink = jnp.stack([sink_ref[0:1, g * Q_PER_KV + hh:g * Q_PER_KV + hh + 1] for hh in range(Q_PER_KV)])
    s = s.reshape(Q_PER_KV, BLOCK, 2 * BLOCK)
    sm = sm.reshape(Q_PER_KV, BLOCK, N_META)
    s = jnp.where(ok[None], s - slope * distf[None], NEG)
    sm = jnp.where(okm[None], sm - slope * dmf[None], NEG)
    mx = jnp.maximum(jnp.maximum(jnp.max(s, axis=-1, keepdims=True),
                                 jnp.max(sm, axis=-1, keepdims=True)), sink)
    p = jnp.exp(s - mx)
    pm = jnp.exp(sm - mx)
    ps = jnp.exp(sink - mx)
    inv = 1.0 / (jnp.sum(p, axis=-1, keepdims=True) + jnp.sum(pm, axis=-1, keepdims=True) + ps)
    return p, pm, ps, inv


def _attn_fwd(proj, sinks, name, plan=None):
    lp = proj.shape[0]
    nblk = lp // BLOCK
    p_in, p_shapes, p_out, p_scr = _plan_parts(plan)

    def body(*refs):
        q_ref, k_ref, v_ref, sink_ref = refs[:4]
        cins = refs[4:4 + len(p_in)]
        o_ref = refs[4 + len(p_in)]
        couts = refs[5 + len(p_in):5 + len(p_in) + len(p_out)]
        sems = refs[5 + len(p_in) + len(p_out):]
        n = pl.program_id(0)
        if plan is not None:
            @pl.when(n == 0)
            def _():
                plan.start(cins, couts, sems)

        start, ok, distf, okm, dmf = _attn_masks(n)
        kb = k_ref[pl.ds(start, 2 * BLOCK), :].astype(BF16)
        vb = v_ref[pl.ds(start, 2 * BLOCK), :].astype(BF16)
        km = k_ref[PAD:BLOCK, :].astype(BF16)
        vm = v_ref[PAD:BLOCK, :].astype(BF16)
        rows = Q_PER_KV * BLOCK
        gsl = [slice(g * HEAD_DIM, (g + 1) * HEAD_DIM) for g in range(N_KV_HEADS)]
        raw = []
        for g in range(N_KV_HEADS):
            qg = _group_rows(q_ref, g, _QK_SCALE)
            raw.append((_dot_nt(qg, kb[:, gsl[g]]), _dot_nt(qg, km[:, gsl[g]])))
        for g in range(N_KV_HEADS):
            gs = gsl[g]
            p, pm, _, inv = _attn_probs(raw[g][0], raw[g][1], sink_ref, g, ok, distf, okm, dmf)
            o = (_dot(p.astype(BF16).reshape(rows, 2 * BLOCK), vb[:, gs])
                 + _dot(pm.astype(BF16).reshape(rows, N_META), vm[:, gs])) * inv.reshape(rows, 1)
            for hh in range(Q_PER_KV):
                h = g * Q_PER_KV + hh
                o_ref[:, h * HEAD_DIM:(h + 1) * HEAD_DIM] = o[hh * BLOCK:(hh + 1) * BLOCK, :].astype(o_ref.dtype)
        if plan is not None:
            @pl.when(n == nblk - 1)
            def _():
                plan.wait(cins, couts, sems)

    res = pl.pallas_call(
        body, name=name, grid=(nblk,),
        in_specs=[pl.BlockSpec((BLOCK, D_MODEL), lambda n: (n, _QCOL)),
                  pl.BlockSpec((lp, LANES), lambda n: (0, _KCOL)),
                  pl.BlockSpec((lp, LANES), lambda n: (0, _VCOL)),
                  pl.BlockSpec(sinks.shape, lambda n: (0, 0))] + p_in,
        out_specs=[pl.BlockSpec((BLOCK, D_MODEL), lambda n: (n, 0))] + p_out,
        out_shape=[jax.ShapeDtypeStruct((lp, D_MODEL), BF16)] + p_shapes,
        scratch_shapes=p_scr,
        compiler_params=pltpu.CompilerParams(dimension_semantics=("arbitrary",)),
    )(proj, proj, proj, sinks, *(plan.ins if plan is not None else []))
    return res[0], res[1:]


def _attn_bwd(proj, sinks, dmix, name, plan=None):
    lp = proj.shape[0]
    nblk = lp // BLOCK

    p_in, p_shapes, p_out, p_scr = _plan_parts(plan)

    def body(*refs):
        q_ref, k_ref, v_ref, sink_ref, dy_ref = refs[:5]
        cins = refs[5:5 + len(p_in)]
        dq_ref, dk_ref, dv_ref, ds_ref = refs[5 + len(p_in):9 + len(p_in)]
        couts = refs[9 + len(p_in):9 + len(p_in) + len(p_out)]
        sems = refs[9 + len(p_in) + len(p_out):]
        n = pl.program_id(0)

        @pl.when(n == 0)
        def _():
            dk_ref[...] = jnp.zeros_like(dk_ref)
            dv_ref[...] = jnp.zeros_like(dv_ref)
            ds_ref[...] = jnp.zeros_like(ds_ref)
            if plan is not None:
                plan.start(cins, couts, sems)

        start, ok, distf, okm, dmf = _attn_masks(n)
        kb = k_ref[pl.ds(start, 2 * BLOCK), :].astype(BF16)
        vb = v_ref[pl.ds(start, 2 * BLOCK), :].astype(BF16)
        km = k_ref[PAD:BLOCK, :].astype(BF16)
        vm = v_ref[PAD:BLOCK, :].astype(BF16)
        lane16 = lax.broadcasted_iota(jnp.int32, (1, N_Q_HEADS), 1)
        dsink = jnp.zeros((1, N_Q_HEADS), F32)
        rows = Q_PER_KV * BLOCK
        gsl = [slice(g * HEAD_DIM, (g + 1) * HEAD_DIM) for g in range(N_KV_HEADS)]
        pre = []
        for g in range(N_KV_HEADS):
            qg = _group_rows(q_ref, g, _QK_SCALE)
            dog = _group_rows(dy_ref, g)
            pre.append((qg, dog, _dot_nt(qg, kb[:, gsl[g]]), _dot_nt(qg, km[:, gsl[g]]),
                        _dot_nt(dog, vb[:, gsl[g]]), _dot_nt(dog, vm[:, gsl[g]])))
        for g in range(N_KV_HEADS):
            gs = gsl[g]
            qg, dog, s_raw, sm_raw, dp, dpm = pre[g]
            p, pm, ps, inv = _attn_probs(s_raw, sm_raw, sink_ref, g, ok, distf, okm, dmf)
            pn, pmn, psn = p * inv, pm * inv, ps * inv
            dp = dp.reshape(Q_PER_KV, BLOCK, 2 * BLOCK)
            dpm = dpm.reshape(Q_PER_KV, BLOCK, N_META)
            delta = (jnp.sum(pn * dp, axis=-1, keepdims=True)
                     + jnp.sum(pmn * dpm, axis=-1, keepdims=True))
            dsb = (pn * (dp - delta)).astype(BF16).reshape(rows, 2 * BLOCK)
            dsm = (pmn * (dpm - delta)).astype(BF16).reshape(rows, N_META)
            dsk = jnp.sum(psn * delta, axis=1, keepdims=True)
            for hh in range(Q_PER_KV):
                dsink = dsink - jnp.where(lane16 == g * Q_PER_KV + hh, dsk[hh], 0.0)
            dq = (_dot(dsb, kb[:, gs]) + _dot(dsm, km[:, gs])) * _QK_SCALE
            for hh in range(Q_PER_KV):
                h = g * Q_PER_KV + hh
                dq_ref[:, h * HEAD_DIM:(h + 1) * HEAD_DIM] = dq[hh * BLOCK:(hh + 1) * BLOCK, :].astype(dq_ref.dtype)
            pnb = pn.astype(BF16).reshape(rows, 2 * BLOCK)
            pmnb = pmn.astype(BF16).reshape(rows, N_META)
            dk_ref[pl.ds(start, 2 * BLOCK), gs] += _dot_tn(dsb, qg)
            dv_ref[pl.ds(start, 2 * BLOCK), gs] += _dot_tn(pnb, dog)
            dk_ref[PAD:BLOCK, gs] += _dot_tn(dsm, qg)
            dv_ref[PAD:BLOCK, gs] += _dot_tn(pmnb, dog)
        ds_ref[...] += dsink
        if plan is not None:
            @pl.when(n == nblk - 1)
            def _():
                plan.wait(cins, couts, sems)

    res = pl.pallas_call(
        body, name=name, grid=(nblk,),
        in_specs=[pl.BlockSpec((BLOCK, D_MODEL), lambda n: (n, _QCOL)),
                  pl.BlockSpec((lp, LANES), lambda n: (0, _KCOL)),
                  pl.BlockSpec((lp, LANES), lambda n: (0, _VCOL)),
                  pl.BlockSpec(sinks.shape, lambda n: (0, 0)),
                  pl.BlockSpec((BLOCK, D_MODEL), lambda n: (n, 1))] + p_in,
        out_specs=[pl.BlockSpec((BLOCK, D_MODEL), lambda n: (n, 0)),
                   pl.BlockSpec((lp, LANES), lambda n: (0, 0)),
                   pl.BlockSpec((lp, LANES), lambda n: (0, 0)),
                   pl.BlockSpec((1, N_Q_HEADS), lambda n: (0, 0))] + p_out,
        out_shape=[jax.ShapeDtypeStruct((lp, D_MODEL), BF16),
                   jax.ShapeDtypeStruct((lp, LANES), F32),
                   jax.ShapeDtypeStruct((lp, LANES), F32),
                   jax.ShapeDtypeStruct((1, N_Q_HEADS), F32)] + p_shapes,
        scratch_shapes=p_scr,
        compiler_params=pltpu.CompilerParams(dimension_semantics=("arbitrary",)),
    )(proj, proj, proj, sinks, dmix, *(plan.ins if plan is not None else []))
    return res[:4], res[4:]


_ZW = D_SSM
_XBC_W = D_SSM + 2 * SSD_GROUPS * SSD_N
_DT_COL = (_ZW + _XBC_W) // LANES
EVEN_IN = 3 * D_MODEL + 2 * LANES
ODD_IN = _ZW + _XBC_W + SSD_HEADS
ODD_IN_PAD = _ZW + _XBC_W + LANES


def _ssm_convprep_fwd(proj, cw, cb, dt_bias, name):
    lp = proj.shape[0]
    kk = cw.shape[0]
    tm, wc = _conv_tiles(lp, _XBC_W)
    offb = _ZW // wc
    nj = _XBC_W // wc
    hb = tm // SUBLANES

    def body(x_ref, xp_ref, dtr_ref, w_ref, b_ref, bias_ref, xc_ref, act_ref, dt_ref):
        i, j = pl.program_id(0), pl.program_id(1)
        real = _real_rows(i, tm)
        xv = x_ref[...]
        xx = jnp.concatenate([jnp.where(i > 0, xp_ref[...], 0.0), xv], axis=0)
        acc = b_ref[...] + w_ref[kk - 1:kk, :] * xv
        for m in range(1, kk):
            acc = acc + w_ref[kk - 1 - m:kk - m, :] * pltpu.roll(xx, m, 0)[SUBLANES:, :]
        xc_ref[...] = acc
        act, _ = _silu_and_grad(acc)
        act_ref[...] = jnp.where(real, act, 0.0)

        @pl.when(j == 0)
        def _():
            dt_ref[...] = jnp.where(real, _softplus(dtr_ref[...] + bias_ref[...]), 0.0)

    return pl.pallas_call(
        body, name=name, grid=(lp // tm, nj),
        in_specs=[pl.BlockSpec((tm, wc), lambda i, j: (i, offb + j)),
                  pl.BlockSpec((SUBLANES, wc), lambda i, j: (jnp.maximum(i * hb - 1, 0), offb + j)),
                  pl.BlockSpec((tm, LANES), lambda i, j: (i, _DT_COL)),
                  pl.BlockSpec((kk, wc), lambda i, j: (0, j)),
                  pl.BlockSpec((1, wc), lambda i, j: (0, j)),
                  pl.BlockSpec((1, LANES), lambda i, j: (0, 0))],
        out_specs=[pl.BlockSpec((tm, wc), lambda i, j: (i, j)),
                   pl.BlockSpec((tm, wc), lambda i, j: (i, j)),
                   pl.BlockSpec((tm, LANES), lambda i, j: (i, 0))],
        out_shape=[jax.ShapeDtypeStruct((lp, _XBC_W), F32), jax.ShapeDtypeStruct((lp, _XBC_W), F32),
                   jax.ShapeDtypeStruct((lp, LANES), F32)],
        compiler_params=pltpu.CompilerParams(dimension_semantics=("parallel", "arbitrary")),
    )(proj, proj, proj, cw, cb, dt_bias)


def _ssm_prep_bwd(xc, proj, dt_bias, dxs, dxskip, db, dc, ddt, name):
    lp = xc.shape[0]
    tm = BLOCK

    def body(r, first, xc_ref, dtr_ref, dxs_ref, dsk_ref, db_ref, dc_ref, ddt_ref, b_ref,
             dxc_ref, ddtr_ref, dbias_ref):
        real = _real_rows(r, tm)
        _, ds = _silu_and_grad(xc_ref[...])
        up = lambda ref: ref[...].astype(F32)
        dxc_ref[:, :D_SSM] = jnp.where(
            real, (up(dxs_ref) + up(dsk_ref)) * ds[:, :D_SSM], 0.0).astype(dxc_ref.dtype)
        dxc_ref[:, D_SSM:D_SSM + 1024] = jnp.where(
            real, up(db_ref) * ds[:, D_SSM:D_SSM + 1024], 0.0).astype(dxc_ref.dtype)
        dxc_ref[:, D_SSM + 1024:] = jnp.where(
            real, up(dc_ref) * ds[:, D_SSM + 1024:], 0.0).astype(dxc_ref.dtype)
        dd = jnp.where(real, ddt_ref[...] * _sigmoid(dtr_ref[...] + b_ref[...]), 0.0)
        ddtr_ref[...] = dd.astype(ddtr_ref.dtype)
        _acc_add(first, dbias_ref, jnp.sum(dd, axis=0, keepdims=True))

    return _rowcall(name, body, lp, tm,
                    rows=[(xc, _XBC_W, 0), (proj, LANES, _DT_COL), (dxs, D_SSM, 0), (dxskip, D_SSM, 0),
                          (db, 1024, 0), (dc, 1024, 0), (ddt, LANES, 0)],
                    vecs=[dt_bias], outs=[(_XBC_W, BF16), (LANES, BF16)], accs=[((1, LANES), F32)])


def _ssd_common(dt, alog):
    a = -jnp.exp(alog)
    cs = _cumsum_rows(dt * a, BLOCK)
    cst = cs.T
    cl = cs[BLOCK - 1:BLOCK, :]
    tril = (lax.broadcasted_iota(jnp.int32, (BLOCK, BLOCK), 0)
            >= lax.broadcasted_iota(jnp.int32, (BLOCK, BLOCK), 1))
    return a, cs, cst, cl, jnp.exp(cs), jnp.exp(cl - cs), jnp.exp(cl), tril


def _head_cols(ecl, g):
    lane = lax.broadcasted_iota(jnp.int32, (1, SSD_HPG * SSD_P), 1)
    e = [ecl[:, SSD_HPG * g + hh:SSD_HPG * g + hh + 1] for hh in range(SSD_HPG)]
    return jnp.where(lane < SSD_P, e[0], jnp.where(lane < 2 * SSD_P, e[1],
                                                   jnp.where(lane < 3 * SSD_P, e[2], e[3])))


def _ssd_fwd(xbc, dt, alog, name, plan=None):
    lp = xbc.shape[0]
    nc = lp // BLOCK
    gw = SSD_HPG * SSD_P
    p_in, p_shapes, p_out, p_scr = _plan_parts(plan)

    def body(*refs):
        xs_ref, b_ref, c_ref, dt_ref, alog_ref = refs[:5]
        cins = refs[5:5 + len(p_in)]
        y_ref, so_ref = refs[5 + len(p_in):7 + len(p_in)]
        couts = refs[7 + len(p_in):7 + len(p_in) + len(p_out)]
        st, fx = refs[7 + len(p_in) + len(p_out):9 + len(p_in) + len(p_out)]
        sems = refs[9 + len(p_in) + len(p_out):]
        n = pl.program_id(0)

        @pl.when(n == 0)
        def _():
            st[...] = jnp.zeros_like(st)
            if plan is not None:
                plan.start(cins, couts, sems)

        dtv = dt_ref[...]
        _, cs, cst, cl, e, f, ecl, tril = _ssd_common(dtv, alog_ref[...])
        pre = []
        for g in range(SSD_GROUPS):
            bg = b_ref[:, g * SSD_N:(g + 1) * SSD_N].astype(BF16)
            cg = c_ref[:, g * SSD_N:(g + 1) * SSD_N].astype(BF16)
            stg = st[g]
            so_ref[0, g] = stg
            pre.append((bg, stg, _dot_nt(cg, bg), _dot(cg, stg.astype(BF16))))
        for g in range(SSD_GROUPS):
            bg, stg, gm, yoff = pre[g]
            heads = [SSD_HPG * g + hh for hh in range(SSD_HPG)]
            cols = lambda v: jnp.stack([v[:, h:h + 1] for h in heads])
            x4 = jnp.stack([xs_ref[:, h * SSD_P:(h + 1) * SSD_P] for h in heads])
            csr = jnp.stack([cst[h:h + 1, :] for h in heads])
            m = gm[None] * jnp.exp(jnp.where(tril[None], cols(cs) - csr, NEG))
            xdt = x4 * cols(dtv)
            yoff4 = jnp.stack([yoff[:, hh * SSD_P:(hh + 1) * SSD_P] for hh in range(SSD_HPG)])
            y4 = (jnp.einsum("hls,hsp->hlp", m.astype(BF16), xdt.astype(BF16), preferred_element_type=F32)
                  + cols(e) * yoff4)
            fx4 = cols(f) * xdt
            for hh, h in enumerate(heads):
                y_ref[:, h * SSD_P:(h + 1) * SSD_P] = y4[hh]
                fx[:, hh * SSD_P:(hh + 1) * SSD_P] = fx4[hh]
            st[g] = stg * _head_cols(ecl, g) + _dot_tn(bg, fx[...].astype(BF16))
        if plan is not None:
            @pl.when(n == nc - 1)
            def _():
                plan.wait(cins, couts, sems)

    res = pl.pallas_call(
        body, name=name, grid=(nc,),
        in_specs=[pl.BlockSpec((BLOCK, D_SSM), lambda n: (n, 0)),
                  pl.BlockSpec((BLOCK, 1024), lambda n: (n, 2)),
                  pl.BlockSpec((BLOCK, 1024), lambda n: (n, 3)),
                  pl.BlockSpec((BLOCK, LANES), lambda n: (n, 0)),
                  pl.BlockSpec((1, LANES), lambda n: (0, 0))] + p_in,
        out_specs=[pl.BlockSpec((BLOCK, D_SSM), lambda n: (n, 0)),
                   pl.BlockSpec((1, SSD_GROUPS, SSD_N, gw), lambda n: (n, 0, 0, 0))] + p_out,
        out_shape=[jax.ShapeDtypeStruct((lp, D_SSM), F32),
                   jax.ShapeDtypeStruct((nc, SSD_GROUPS, SSD_N, gw), F32)] + p_shapes,
        scratch_shapes=[pltpu.VMEM((SSD_GROUPS, SSD_N, gw), F32), pltpu.VMEM((BLOCK, gw), F32)] + p_scr,
        compiler_params=pltpu.CompilerParams(dimension_semantics=("arbitrary",)),
    )(xbc, xbc, xbc, dt, alog, *(plan.ins if plan is not None else []))
    return res[:2], res[2:]


def _ssd_bwd(xbc, dt, alog, states, dy, name, plan=None):
    lp = xbc.shape[0]
    nc = lp // BLOCK
    gw = SSD_HPG * SSD_P
    p_in, p_shapes, p_out, p_scr = _plan_parts(plan)

    def body(*refs):
        xs_ref, b_ref, c_ref, dt_ref, alog_ref, dy_ref, st_ref = refs[:7]
        cins = refs[7:7 + len(p_in)]
        dxs_ref, db_ref, dc_ref, ddt_ref, dalog_ref = refs[7 + len(p_in):12 + len(p_in)]
        couts = refs[12 + len(p_in):12 + len(p_in) + len(p_out)]
        dst, edy, fx = refs[12 + len(p_in) + len(p_out):15 + len(p_in) + len(p_out)]
        sems = refs[15 + len(p_in) + len(p_out):]
        i = pl.program_id(0)

        @pl.when(i == 0)
        def _():
            dst[...] = jnp.zeros_like(dst)
            dalog_ref[...] = jnp.zeros_like(dalog_ref)
            if plan is not None:
                plan.start(cins, couts, sems)

        dtv = dt_ref[...]
        a, cs, cst, cl, e, f, ecl, tril = _ssd_common(dtv, alog_ref[...])
        lane = lax.broadcasted_iota(jnp.int32, (1, LANES), 1)
        sub = _row_iota(BLOCK)
        triu = (lax.broadcasted_iota(jnp.int32, (BLOCK, BLOCK), 1)
                >= lax.broadcasted_iota(jnp.int32, (BLOCK, BLOCK), 0))
        dcs = jnp.zeros((BLOCK, LANES), F32)
        dcst = jnp.zeros((LANES, BLOCK), F32)
        dcl = jnp.zeros((1, LANES), F32)
        ddtx = jnp.zeros((BLOCK, LANES), F32)
        pre = []
        for g in range(SSD_GROUPS):
            bg = b_ref[:, g * SSD_N:(g + 1) * SSD_N].astype(BF16)
            cg = c_ref[:, g * SSD_N:(g + 1) * SSD_N].astype(BF16)
            stb = st_ref[0, g].astype(BF16)
            dsob = dst[g].astype(BF16)
            pre.append((bg, cg, stb, dsob, _dot_nt(cg, bg), _dot_nt(bg, cg), _dot(cg, stb), _dot(bg, dsob)))
        for g in range(SSD_GROUPS):
            bg, cg, stb, dsob, gm, gmt, yraw, dfx = pre[g]
            dso = dst[g]
            prodsum = jnp.sum(dso * st_ref[0, g], axis=0, keepdims=True)
            heads = [SSD_HPG * g + hh for hh in range(SSD_HPG)]
            cols = lambda v: jnp.stack([v[:, h:h + 1] for h in heads])
            parts = lambda v: jnp.stack([v[:, hh * SSD_P:(hh + 1) * SSD_P] for hh in range(SSD_HPG)])
            x4 = jnp.stack([xs_ref[:, h * SSD_P:(h + 1) * SSD_P] for h in heads])
            dy4 = jnp.stack([dy_ref[:, h * SSD_P:(h + 1) * SSD_P] for h in heads])
            csc, dtc, ec, fc = cols(cs), cols(dtv), cols(e), cols(f)
            csr = jnp.stack([cst[h:h + 1, :] for h in heads])
            seg = csc - csr
            lam = jnp.exp(jnp.where(tril[None], seg, NEG))
            lamt = jnp.exp(jnp.where(triu[None], -seg, NEG))
            mt = gmt[None] * lamt
            xdt = x4 * dtc
            dyb = dy4.astype(BF16)
            dm = jnp.einsum("hlp,hsp->hls", dyb, xdt.astype(BF16), preferred_element_type=F32)
            dfx4 = parts(dfx)
            dxdt = jnp.einsum("hsl,hlp->hsp", mt.astype(BF16), dyb, preferred_element_type=F32) + fc * dfx4
            dml = dm * lam
            w = dml * gm[None]
            dgm = jnp.sum(dml, axis=0)
            dff = jnp.sum(dfx4 * xdt, axis=2, keepdims=True) * fc
            colv = (jnp.sum(w, axis=2, keepdims=True)
                    + jnp.sum(dy4 * parts(yraw), axis=2, keepdims=True) * ec - dff)
            roww = jnp.sum(w, axis=1, keepdims=True)
            ddtc = jnp.sum(dxdt * x4, axis=2, keepdims=True)
            dffs = jnp.sum(dff, axis=1, keepdims=True)
            dxs4 = dxdt * dtc
            edy4 = ec * dy4
            fx4 = fc * xdt
            for hh, h in enumerate(heads):
                ls = slice(hh * SSD_P, (hh + 1) * SSD_P)
                onl = (lane == h).astype(F32)
                dcs = dcs + colv[hh] * onl
                dcst = dcst - (sub == h).astype(F32) * roww[hh]
                dcl = dcl + (dffs[hh] + ecl[:, h:h + 1] * jnp.sum(prodsum[:, ls], axis=1, keepdims=True)) * onl
                ddtx = ddtx + ddtc[hh] * onl
                dxs_ref[:, h * SSD_P:(h + 1) * SSD_P] = dxs4[hh].astype(dxs_ref.dtype)
                edy[:, ls] = edy4[hh]
                fx[:, ls] = fx4[hh]
            edyb = edy[...].astype(BF16)
            fxb = fx[...].astype(BF16)
            dgb = dgm.astype(BF16)
            dc_ref[:, g * SSD_N:(g + 1) * SSD_N] = (_dot_nt(edyb, stb) + _dot(dgb, bg)).astype(dc_ref.dtype)
            db_ref[:, g * SSD_N:(g + 1) * SSD_N] = (_dot_nt(fxb, dsob) + _dot_tn(dgb, cg)).astype(db_ref.dtype)
            dst[g] = dso * _head_cols(ecl, g) + _dot_tn(cg, edyb)
        dcs = dcs + dcst.T + jnp.where(sub == BLOCK - 1, dcl, 0.0)
        dda = _rev_cumsum_rows(dcs, BLOCK)
        ddt_ref[...] = ddtx + dda * a
        dalog_ref[...] += jnp.sum(dda * dtv, axis=0, keepdims=True) * a
        if plan is not None:
            @pl.when(i == nc - 1)
            def _():
                plan.wait(cins, couts, sems)

    rev = lambda i: nc - 1 - i
    res = pl.pallas_call(
        body, name=name, grid=(nc,),
        in_specs=[pl.BlockSpec((BLOCK, D_SSM), lambda i: (rev(i), 0)),
                  pl.BlockSpec((BLOCK, 1024), lambda i: (rev(i), 2)),
                  pl.BlockSpec((BLOCK, 1024), lambda i: (rev(i), 3)),
                  pl.BlockSpec((BLOCK, LANES), lambda i: (rev(i), 0)),
                  pl.BlockSpec((1, LANES), lambda i: (0, 0)),
                  pl.BlockSpec((BLOCK, D_SSM), lambda i: (rev(i), 0)),
                  pl.BlockSpec((1, SSD_GROUPS, SSD_N, gw), lambda i: (rev(i), 0, 0, 0))] + p_in,
        out_specs=[pl.BlockSpec((BLOCK, D_SSM), lambda i: (rev(i), 0)),
                   pl.BlockSpec((BLOCK, 1024), lambda i: (rev(i), 0)),
                   pl.BlockSpec((BLOCK, 1024), lambda i: (rev(i), 0)),
                   pl.BlockSpec((BLOCK, LANES), lambda i: (rev(i), 0)),
                   pl.BlockSpec((1, LANES), lambda i: (0, 0))] + p_out,
        out_shape=[jax.ShapeDtypeStruct((lp, D_SSM), BF16),
                   jax.ShapeDtypeStruct((lp, 1024), BF16),
                   jax.ShapeDtypeStruct((lp, 1024), BF16),
                   jax.ShapeDtypeStruct((lp, LANES), F32),
                   jax.ShapeDtypeStruct((1, LANES), F32)] + p_shapes,
        scratch_shapes=[pltpu.VMEM((SSD_GROUPS, SSD_N, gw), F32),
                        pltpu.VMEM((BLOCK, gw), F32), pltpu.VMEM((BLOCK, gw), F32)] + p_scr,
        compiler_params=pltpu.CompilerParams(dimension_semantics=("arbitrary",)),
    )(xbc, xbc, xbc, dt, alog, dy, states, *(plan.ins if plan is not None else []))
    return res[:5], res[5:]


_GN_GROUPS = 8
_GN_W = D_SSM // _GN_GROUPS


def _ssm_gate_fwd(yssd, xbc, proj, dskip, gnorm, name):
    lp = yssd.shape[0]
    tm = _pick(lp, (320, 256, 128))

    def body(r, first, y_ref, x_ref, z_ref, d_ref, g_ref, o_ref):
        sz, _ = _silu_and_grad(z_ref[...])
        y2 = (y_ref[...] + d_ref[...] * x_ref[...]) * sz
        for k in range(_GN_GROUPS):
            sl = slice(k * _GN_W, (k + 1) * _GN_W)
            yk = y2[:, sl]
            rs = lax.rsqrt(jnp.mean(yk * yk, axis=-1, keepdims=True) + EPS)
            o_ref[:, sl] = (yk * rs * g_ref[:, sl]).astype(o_ref.dtype)

    return _rowcall(name, body, lp, tm, rows=[(yssd, D_SSM, 0), (xbc, D_SSM, 0), (proj, D_SSM, 0)],
                    vecs=[dskip, gnorm], outs=[(D_SSM, BF16)])[0]


def _ssm_out_gate_bwd(do, w_out, yssd, xbc, proj, dskip, gnorm, name):
    lp, d = do.shape
    tm = _pick(lp, (256, 128))
    tk = D_SSM // 2

    def body(do_ref, w_ref, y_ref, x_ref, z_ref, d_ref, g_ref, dy_ref, dx_ref, dz_ref, dd_ref, dg_ref):
        first = pl.program_id(1) == 0
        dyn = _dot_nt(do_ref[...], w_ref[...])
        z = z_ref[...]
        sz, dsz = _silu_and_grad(z)
        xs = x_ref[...]
        y1 = y_ref[...] + d_ref[...] * xs
        y2 = y1 * sz
        for k in range(tk // _GN_W):
            sl = slice(k * _GN_W, (k + 1) * _GN_W)
            dx, dgt = _rms_bwd(y2[:, sl], g_ref[:, sl], dyn[:, sl])
            dy1 = dx * sz[:, sl]
            dy_ref[:, sl] = dy1.astype(dy_ref.dtype)
            dx_ref[:, sl] = (dy1 * d_ref[:, sl]).astype(dx_ref.dtype)
            dz_ref[:, sl] = (dx * y1[:, sl] * dsz[:, sl]).astype(dz_ref.dtype)

            @pl.when(first)
            def _():
                dd_ref[:, sl] = jnp.zeros((1, _GN_W), F32)
                dg_ref[:, sl] = jnp.zeros((1, _GN_W), F32)

            dd_ref[:, sl] += jnp.sum(dy1 * xs[:, sl], axis=0, keepdims=True)
            dg_ref[:, sl] += jnp.sum(dgt, axis=0, keepdims=True)

    tile = pl.BlockSpec((tm, tk), lambda j, i: (i, j))
    vec = pl.BlockSpec((1, tk), lambda j, i: (0, j))
    return pl.pallas_call(
        body, name=name, grid=(D_SSM // tk, lp // tm),
        in_specs=[pl.BlockSpec((tm, d), lambda j, i: (i, 0)),
                  pl.BlockSpec((tk, d), lambda j, i: (j, 0)), tile, tile, tile, vec, vec],
        out_specs=[tile, tile, tile, vec, vec],
        out_shape=[jax.ShapeDtypeStruct((lp, D_SSM), BF16)] * 3 + [jax.ShapeDtypeStruct((1, D_SSM), F32)] * 2,
        compiler_params=pltpu.CompilerParams(dimension_semantics=("parallel", "arbitrary")),
    )(do, w_out, yssd, xbc, proj, dskip, gnorm)


def _shape2d(shape):
    n = math.prod(shape)
    if len(shape) == 2:
        return tuple(shape)
    return (n // LANES, LANES) if n % LANES == 0 else (1, n)


def _adamw_many(ws, gs, ms, vs, name):
    n = len(ws)
    c1 = 1.0 / (1.0 - ADAM_B1 ** ADAM_STEP)
    c2 = 1.0 / (1.0 - ADAM_B2 ** ADAM_STEP)

    def body(*refs):
        for i in range(n):
            w_ref, g_ref, m_ref, v_ref = (refs[j * n + i] for j in range(4))
            d_ref, nm_ref, nv_ref = (refs[(4 + j) * n + i] for j in range(3))
            gv = g_ref[...]
            nm = ADAM_B1 * m_ref[...] + (1.0 - ADAM_B1) * gv
            nv = ADAM_B2 * v_ref[...] + (1.0 - ADAM_B2) * (gv * gv)
            nm_ref[...] = nm
            nv_ref[...] = nv
            d_ref[...] = -ADAM_LR * ((nm * c1) / (jnp.sqrt(nv * c2) + ADAM_EPS) + ADAM_WD * w_ref[...])

    vm = pl.BlockSpec(memory_space=pltpu.VMEM)
    return pl.pallas_call(
        body, name=name, in_specs=[vm] * (4 * n), out_specs=[vm] * (3 * n),
        out_shape=[jax.ShapeDtypeStruct(w.shape, F32) for w in ws] * 3,
    )(*ws, *gs, *ms, *vs)


def _place():
    return lax.axis_index("x"), lax.axis_index("y"), lax.axis_index("c")


def _other_chips(x, y):
    return [(1 - x, y), (x, 1 - y), (1 - x, 1 - y)]


_ANY = pl.BlockSpec(memory_space=pl.ANY)


class _Plan:
    def __init__(self, ins, out_shapes, n_remote, n_local, issue):
        self.ins = list(ins)
        self.out_shapes = list(out_shapes)
        self.issue = issue
        self.scratch = [pltpu.SemaphoreType.DMA((max(n_remote, 1),)),
                        pltpu.SemaphoreType.DMA((max(n_remote, 1),)),
                        pltpu.SemaphoreType.DMA((max(n_local, 1),))]

    def start(self, ins, outs, sems):
        sends, _, locs = self.issue(ins, outs, *sems)
        for cp in locs + sends:
            cp.start()

    def wait(self, ins, outs, sems):
        sends, recvs, locs = self.issue(ins, outs, *sems)
        for make in recvs:
            make().wait_recv()
        for cp in sends:
            cp.wait_send()
        for cp in locs:
            cp.wait()


def _plan_parts(plan):
    if plan is None:
        return [], [], [], []
    return ([_ANY] * len(plan.ins), plan.out_shapes, [_ANY] * len(plan.out_shapes), plan.scratch)


def _run_plan(plan, name):
    n_in, n_out = len(plan.ins), len(plan.out_shapes)

    def body(*refs):
        ins, outs, sems = refs[:n_in], refs[n_in:n_in + n_out], refs[n_in + n_out:]
        plan.start(ins, outs, sems)
        plan.wait(ins, outs, sems)

    return pl.pallas_call(
        body, name=name, in_specs=[_ANY] * n_in, out_specs=[_ANY] * n_out,
        out_shape=plan.out_shapes, scratch_shapes=plan.scratch,
    )(*plan.ins)


def _gather_plan(shards):
    n = len(shards)

    def issue(ins, outs, send_sems, recv_sems, local_sems):
        x, y, c = _place()
        me = 2 * x + y
        sends, recvs, locs = [], [], []
        for p in range(n):
            locs.append(pltpu.make_async_copy(ins[p], outs[p].at[me], local_sems.at[p]))
            for k, (px, py) in enumerate(_other_chips(x, y)):
                sems = dict(send_sem=send_sems.at[3 * p + k], recv_sem=recv_sems.at[3 * p + k],
                            device_id=(px, py, c), device_id_type=MESH)
                sends.append(pltpu.make_async_remote_copy(src_ref=ins[p], dst_ref=outs[p].at[me], **sems))
                recvs.append(functools.partial(pltpu.make_async_remote_copy, src_ref=ins[p],
                                               dst_ref=outs[p].at[2 * px + py], **sems))
        return sends, recvs, locs

    return _Plan(shards, [jax.ShapeDtypeStruct((N_CHIPS,) + s.shape, s.dtype) for s in shards], 3 * n, n, issue)


_REL7 = [(fx, fy, fc) for fx in (0, 1) for fy in (0, 1) for fc in (0, 1)][1:]


def _scatter8_plan(gs):
    n = len(gs)

    def issue(ins, outs, send_sems, recv_sems, local_sems):
        x, y, c = _place()
        sends = []
        for p in range(n):
            hr = gs[p].shape[1] // 2
            for k, (fx, fy, fc) in enumerate(_REL7):
                tx, ty, tc = x ^ fx, y ^ fy, c ^ fc
                src = ins[p].at[2 * tx + ty, pl.ds(pl.multiple_of(tc * hr, SUBLANES), hr), :]
                sends.append(pltpu.make_async_remote_copy(
                    src_ref=src, dst_ref=outs[p].at[k],
                    send_sem=send_sems.at[7 * p + k], recv_sem=recv_sems.at[7 * p + k],
                    device_id=(tx, ty, tc), device_id_type=MESH))
        return sends, [functools.partial(lambda cp: cp, cp) for cp in sends], []

    shapes = [jax.ShapeDtypeStruct((7, g.shape[1] // 2, g.shape[2]), g.dtype) for g in gs]
    return _Plan(gs, shapes, 7 * n, 0, issue)


def _sibling_plan(ts):
    n = len(ts)

    def issue(ins, outs, send_sems, recv_sems, local_sems):
        x, y, c = _place()
        sends = [pltpu.make_async_remote_copy(
            src_ref=ins[p], dst_ref=outs[p], send_sem=send_sems.at[p], recv_sem=recv_sems.at[p],
            device_id=(x, y, 1 - c), device_id_type=MESH) for p in range(n)]
        return sends, [functools.partial(lambda cp: cp, cp) for cp in sends], []

    return _Plan(ts, [jax.ShapeDtypeStruct(t.shape, t.dtype) for t in ts], n, 0, issue)


def _add8(g, recv, chip, core, name):
    s, r, n = g.shape
    hr = r // 2
    th = hr // 2 if (hr // 2) % SUBLANES == 0 else hr
    nt = hr // th

    def body(chip_ref, core_ref, g_ref, r_ref, o_ref):
        acc = g_ref[0].astype(F32)
        for k in range(7):
            acc = acc + r_ref[k].astype(F32)
        o_ref[...] = acc

    return pl.pallas_call(
        body, name=name,
        grid_spec=pltpu.PrefetchScalarGridSpec(
            num_scalar_prefetch=2, grid=(nt,),
            in_specs=[pl.BlockSpec((1, th, n), lambda i, ch, co: (ch[0], co[0] * nt + i, 0)),
                      pl.BlockSpec((7, th, n), lambda i, ch, co: (0, i, 0))],
            out_specs=pl.BlockSpec((th, n), lambda i, ch, co: (i, 0))),
        out_shape=jax.ShapeDtypeStruct((hr, n), F32),
        compiler_params=pltpu.CompilerParams(dimension_semantics=("parallel",)),
    )(chip, core, g, recv)


def _adamw_halves(w, own, other, m, v, core, name):
    r, n = w.shape
    hr = r // 2
    th = hr // 2 if (hr // 2) % SUBLANES == 0 else hr
    tph = hr // th
    c1 = 1.0 / (1.0 - ADAM_B1 ** ADAM_STEP)
    c2 = 1.0 / (1.0 - ADAM_B2 ** ADAM_STEP)

    def body(core_ref, w_ref, a_ref, b_ref, m_ref, v_ref, g_ref, d_ref, nm_ref, nv_ref):
        half = pl.program_id(0) // tph
        gv = jnp.where(half == core_ref[0], a_ref[...], b_ref[...])
        nm = ADAM_B1 * m_ref[...] + (1.0 - ADAM_B1) * gv
        nv = ADAM_B2 * v_ref[...] + (1.0 - ADAM_B2) * (gv * gv)
        g_ref[...] = gv
        nm_ref[...] = nm
        nv_ref[...] = nv
        d_ref[...] = -ADAM_LR * ((nm * c1) / (jnp.sqrt(nv * c2) + ADAM_EPS) + ADAM_WD * w_ref[...])

    full = pl.BlockSpec((th, n), lambda i, co: (i, 0))
    part = pl.BlockSpec((th, n), lambda i, co: (i % tph, 0))
    return pl.pallas_call(
        body, name=name,
        grid_spec=pltpu.PrefetchScalarGridSpec(
            num_scalar_prefetch=1, grid=(2 * tph,),
            in_specs=[full, part, part, full, full], out_specs=[full] * 4),
        out_shape=[jax.ShapeDtypeStruct((r, n), F32)] * 4,
        compiler_params=pltpu.CompilerParams(dimension_semantics=("parallel",)),
    )(core, w, own, other, m, v)


def _allreduce_small(pack, name):
    r, l = pack.shape
    hr = r // 2
    assert hr % SUBLANES == 0

    def body(p_ref, o_ref, sib, chips, send_sems, recv_sems):
        x, y, c = _place()
        chip = 2 * x + y
        sibling = dict(device_id=(x, y, 1 - c), device_id_type=MESH)
        mine = pl.ds(pl.multiple_of(c * hr, SUBLANES), hr)
        other = pl.ds(pl.multiple_of((1 - c) * hr, SUBLANES), hr)
        a = pltpu.make_async_remote_copy(src_ref=p_ref.at[other], dst_ref=sib, send_sem=send_sems.at[0],
                                         recv_sem=recv_sems.at[0], **sibling)
        a.start()
        a.wait()
        own, got = p_ref[mine, :], sib[...]
        chips[chip] = jnp.where(c == 0, own, got) + jnp.where(c == 0, got, own)
        sends = []
        for k, (px, py) in enumerate(_other_chips(x, y)):
            cp = pltpu.make_async_remote_copy(
                src_ref=chips.at[chip], dst_ref=chips.at[chip], send_sem=send_sems.at[1 + k],
                recv_sem=recv_sems.at[1 + k], device_id=(px, py, c), device_id_type=MESH)
            cp.start()
            sends.append(cp)
        for k, (px, py) in enumerate(_other_chips(x, y)):
            pltpu.make_async_remote_copy(
                src_ref=chips.at[chip], dst_ref=chips.at[2 * px + py], send_sem=send_sems.at[1 + k],
                recv_sem=recv_sems.at[1 + k], device_id=(px, py, c), device_id_type=MESH).wait_recv()
        for cp in sends:
            cp.wait_send()
        o_ref[mine, :] = ((chips[0] + chips[1]) + chips[2]) + chips[3]
        fin = pltpu.make_async_remote_copy(src_ref=o_ref.at[mine], dst_ref=o_ref.at[mine],
                                           send_sem=send_sems.at[4], recv_sem=recv_sems.at[4], **sibling)
        fin.start()
        pltpu.make_async_remote_copy(src_ref=o_ref.at[mine], dst_ref=o_ref.at[other],
                                     send_sem=send_sems.at[4], recv_sem=recv_sems.at[4], **sibling).wait_recv()
        fin.wait_send()

    vm = pl.BlockSpec(memory_space=pltpu.VMEM)
    return pl.pallas_call(
        body, name=name, in_specs=[vm], out_specs=vm,
        out_shape=jax.ShapeDtypeStruct((r, l), F32),
        scratch_shapes=[pltpu.VMEM((hr, l), F32), pltpu.VMEM((N_CHIPS, hr, l), F32),
                        pltpu.SemaphoreType.DMA((5,)), pltpu.SemaphoreType.DMA((5,))],
    )(pack)


def _flat_rows(a, mult=SUBLANES * LANES):
    f = a.reshape(-1)
    padn = (-f.shape[0]) % mult
    if padn:
        f = jnp.concatenate([f, jnp.zeros((padn,), f.dtype)])
    return f


def _pack(arrs, mult=SUBLANES * LANES, total_mult=None):
    flat = [_flat_rows(a, mult) for a in arrs]
    sizes = [f.shape[0] for f in flat]
    if total_mult is not None:
        padn = (-sum(sizes)) % total_mult
        if padn:
            flat.append(jnp.zeros((padn,), flat[0].dtype))
    return jnp.concatenate(flat).reshape(-1, LANES), sizes


def _unpack(pack, shapes, sizes, lead=()):
    flat = pack.reshape(lead + (-1,))
    out, off = [], 0
    for shp, sz in zip(shapes, sizes):
        n = math.prod(shp)
        out.append(flat[..., off:off + n].reshape(lead + tuple(shp)))
        off += sz
    return out


def _cols_from_shards(g):
    s, k, n = g.shape
    return jnp.transpose(g, (1, 0, 2)).reshape(k, s * n)


def _ffn_fwd(h, u, post, w_up, cw, cb, w_down, tag, next_pre=None, plan=None):
    hp = _mm_nn_sh(u, w_up, 2 * D_FF, f"{tag}_up")
    if plan is None:
        (act, hg, hu), pouts = _ffn_convact_fwd(hp, cw, cb, f"{tag}_convact"), ()
    else:
        (act, hg, hu), pouts = _ffn_convact_fwd(hp, cw, cb, f"{tag}_convact", plan)
    res = _mm_postnorm_res(act, w_down, h, post, f"{tag}_down_postnorm", next_pre)
    o, hn, un = res if next_pre is not None else (*res, None)
    return hn, un, (h, u, hp, hg, hu, act, o), pouts


def _ffn_bwd(dh, saved, pre, post, w_up, cw, w_down, tag):
    h, u, hp, hg, hu, act, o = saved
    do, dpost = _postnorm_bwd(o, post, dh, f"{tag}_postnorm_bwd")
    dhg, dhu = _ffn_down_act_bwd(do, w_down, hg, hu, f"{tag}_down_dx_act_bwd")
    dw_down = _mm_tn(act, do, f"{tag}_down_dw")
    dxg, dwg, dbg = _conv_bwd(hp, 0, D_FF, dhg, cw, f"{tag}_conv_bwd_gate")
    dxu, dwu, dbu = _conv_bwd(hp, D_FF, D_FF, dhu, cw, f"{tag}_conv_bwd_up", w_col_off=D_FF)
    dhp = (dxg, dxu)
    dcw = jnp.concatenate([dwg, dwu], axis=1)
    dcb = jnp.concatenate([dbg, dbu], axis=1)
    du = _mm_nt_sh(dhp, w_up, f"{tag}_up_dx")
    dw_up = _mm_tn_sh(u, dhp, w_up.shape[2], f"{tag}_up_dw")
    dhn, dpre = _prenorm_bwd(h, pre, du, dh, f"{tag}_prenorm_bwd")
    return dhn, dict(pre=dpre, post=dpost, w_up=dw_up, conv_w=dcw[:3], conv_b=dcb, w_down=dw_down)


class _Exchange:
    GATHER_IN_LRU = ("l0_w_out", "l0_ffn_w_down")
    GATHER_IN_ATTN = ("l0_ffn_w_up", "l1_w_out")
    GATHER_IN_FFN0 = ("l1_w_in",)
    GATHER_IN_SSD = ("l1_ffn_w_up", "l1_ffn_w_down")
    AFTER_L1_OUT = ("l1_ffn_w_up", "l1_ffn_w_down", "l1_w_out")
    IN_LRU_BWD = ("l1_w_in",)
    AFTER_L0_OUT = ("l0_ffn_w_up", "l0_ffn_w_down", "l0_w_out")
    LAST = ("l0_w_in",)

    def __init__(self, late_shards):
        self.late = dict(late_shards)
        self.slabs = {}
        self.recv = {}

    def gather_plan(self, names):
        return _gather_plan([self.late[n] for n in names])

    def gathered(self, names, outs):
        return {n: (g if n in _BIG_COL else g.reshape(-1, g.shape[-1])) for n, g in zip(names, outs)}

    def scatter_plan(self, grads, names):
        for n in names:
            g = grads[n]
            self.slabs[n] = g if n in _BIG_COL else g.reshape(N_CHIPS, -1, g.shape[-1])
        return _scatter8_plan([self.slabs[n] for n in names])

    def scattered(self, names, outs):
        self.recv.update(zip(names, outs))


def _local_step(x, tgt, meta, P, ex=None):
    seq, d = x.shape
    lp = seq + BLOCK
    h0 = jnp.concatenate([jnp.zeros((PAD, d), F32), meta, x], axis=0)
    tgt_p = jnp.concatenate([jnp.zeros((BLOCK, d), F32), tgt], axis=0)

    u0 = _rmsnorm_fwd(h0, P["l0_mix_pre_norm"], "l0_mix_prenorm")
    proj0 = _mm_nn_sh(u0, P["l0_w_in"], EVEN_IN, "l0_in")
    xrc = _conv_fwd(proj0, D_MODEL, D_MODEL, P["l0_lru_conv_w"], P["l0_lru_conv_b"], "l0_lru_conv")
    lru_args = (P["l0_lru_w_a"], P["l0_lru_w_x"], P["l0_lru_b_a"], P["l0_lru_b_x"], P["l0_lru_lambda"])
    if ex:
        (ya, hl), outs = _lru_fwd(proj0, xrc, *lru_args, "l0_lru", ex.gather_plan(ex.GATHER_IN_LRU))
        P = {**P, **ex.gathered(ex.GATHER_IN_LRU, outs)}
    else:
        ya, hl = _lru_fwd(proj0, xrc, *lru_args, "l0_lru")
    yb, outs = _attn_fwd(proj0, P["l0_attn_sinks"], "l0_attn",
                         ex.gather_plan(ex.GATHER_IN_ATTN) if ex else None)
    if ex:
        P = {**P, **ex.gathered(ex.GATHER_IN_ATTN, outs)}
    o0, h1, u1 = _mm_postnorm_res((ya, yb), P["l0_w_out"], h0, P["l0_mix_post_norm"], "l0_out_postnorm",
                                  P["l0_ffn_pre_norm"])
    h2, u2, ffn0, outs = _ffn_fwd(h1, u1, P["l0_ffn_post_norm"], P["l0_ffn_w_up"], P["l0_ffn_conv_w"],
                                  P["l0_ffn_conv_b"], P["l0_ffn_w_down"], "l0_ffn", P["l1_mix_pre_norm"],
                                  ex.gather_plan(ex.GATHER_IN_FFN0) if ex else None)
    if ex:
        P = {**P, **ex.gathered(ex.GATHER_IN_FFN0, outs)}
    proj1 = _mm_nn_sh(u2, P["l1_w_in"], ODD_IN_PAD, "l1_in")
    xc1, xbc, dt = _ssm_convprep_fwd(proj1, P["l1_ssm_conv_w"], P["l1_ssm_conv_b"], P["l1_dt_bias"],
                                     "l1_ssm_convprep")
    (yssd, states), outs = _ssd_fwd(xbc, dt, P["l1_a_log"], "l1_ssd",
                                    ex.gather_plan(ex.GATHER_IN_SSD) if ex else None)
    if ex:
        P = {**P, **ex.gathered(ex.GATHER_IN_SSD, outs)}
    yn = _ssm_gate_fwd(yssd, xbc, proj1, P["l1_d_skip"], P["l1_gate_norm"], "l1_ssm_gate")
    o1, h3, u3 = _mm_postnorm_res(yn, P["l1_w_out"], h2, P["l1_mix_post_norm"], "l1_out_postnorm",
                                  P["l1_ffn_pre_norm"])
    h4, _, ffn1, _ = _ffn_fwd(h3, u3, P["l1_ffn_post_norm"], P["l1_ffn_w_up"], P["l1_ffn_conv_w"],
                              P["l1_ffn_conv_b"], P["l1_ffn_w_down"], "l1_ffn")
    dh4, loss_cols = _loss_fwd_bwd(h4, tgt_p, "loss")

    G = {}
    dh3, g = _ffn_bwd(dh4, ffn1, P["l1_ffn_pre_norm"], P["l1_ffn_post_norm"], P["l1_ffn_w_up"],
                      P["l1_ffn_conv_w"], P["l1_ffn_w_down"], "l1_ffn")
    for k, v in g.items():
        G["l1_ffn_" + (k + "_norm" if k in ("pre", "post") else k)] = v
    do1, G["l1_mix_post_norm"] = _postnorm_bwd(o1, P["l1_mix_post_norm"], dh3, "l1_mix_postnorm_bwd")
    G["l1_w_out"] = _mm_tn(yn, do1, "l1_out_dw")
    dyssd, dxskip, dz, dd_cols, G["l1_gate_norm"] = _ssm_out_gate_bwd(
        do1, P["l1_w_out"], yssd, xbc, proj1, P["l1_d_skip"], P["l1_gate_norm"], "l1_out_dx_gate_bwd")
    G["l1_d_skip"] = dd_cols.reshape(SSD_HEADS, SSD_P).sum(axis=1)
    (dxs, dbm, dcm, ddt, dalog), outs = _ssd_bwd(
        xbc, dt, P["l1_a_log"], states, dyssd, "l1_ssd_bwd",
        ex.scatter_plan(G, ex.AFTER_L1_OUT) if ex else None)
    if ex:
        ex.scattered(ex.AFTER_L1_OUT, outs)
    G["l1_a_log"] = dalog[0, :SSD_HEADS]
    dxc, ddtr, dbias = _ssm_prep_bwd(xc1, proj1, P["l1_dt_bias"], dxs, dxskip, dbm, dcm, ddt,
                                     "l1_ssm_prep_bwd")
    G["l1_dt_bias"] = dbias[0, :SSD_HEADS]
    dxbc, dcw, dcb = _conv_bwd(proj1, _ZW, _XBC_W, dxc, P["l1_ssm_conv_w"], "l1_ssm_conv_bwd")
    G["l1_ssm_conv_w"] = dcw[:4]
    G["l1_ssm_conv_b"] = dcb
    dproj1 = jnp.concatenate([dz, dxbc, ddtr], axis=1)
    du2 = _mm_nt_sh(dproj1, P["l1_w_in"], "l1_in_dx")
    G["l1_w_in"] = _mm_tn_sh(u2, dproj1, ODD_IN // N_CHIPS, "l1_in_dw")
    dh2, G["l1_mix_pre_norm"] = _prenorm_bwd(h2, P["l1_mix_pre_norm"], du2, dh3, "l1_mix_prenorm_bwd")
    dh1, g = _ffn_bwd(dh2, ffn0, P["l0_ffn_pre_norm"], P["l0_ffn_post_norm"], P["l0_ffn_w_up"],
                      P["l0_ffn_conv_w"], P["l0_ffn_w_down"], "l0_ffn")
    for k, v in g.items():
        G["l0_ffn_" + (k + "_norm" if k in ("pre", "post") else k)] = v
    do0, G["l0_mix_post_norm"] = _postnorm_bwd(o0, P["l0_mix_post_norm"], dh1, "l0_mix_postnorm_bwd")
    dmix = _mm_nt(do0, P["l0_w_out"], "l0_out_dx")
    G["l0_w_out"] = jnp.concatenate([_mm_tn(ya, do0, "l0_out_dw_lru"), _mm_tn(yb, do0, "l0_out_dw_attn")], axis=0)
    if ex:
        lru_out, outs = _lru_bwd(proj0, xrc, hl, dmix, *lru_args, "l0_lru_bwd",
                                 ex.scatter_plan(G, ex.IN_LRU_BWD))
        ex.scattered(ex.IN_LRU_BWD, outs)
    else:
        lru_out = _lru_bwd(proj0, xrc, hl, dmix, *lru_args, "l0_lru_bwd")
    (dgate, dxrc, G["l0_lru_w_a"], G["l0_lru_w_x"], G["l0_lru_b_a"], G["l0_lru_b_x"],
     G["l0_lru_lambda"]) = lru_out
    dxr, dcw, dcb = _conv_bwd(proj0, D_MODEL, D_MODEL, dxrc, P["l0_lru_conv_w"], "l0_lru_conv_bwd")
    G["l0_lru_conv_w"] = dcw[:4]
    G["l0_lru_conv_b"] = dcb
    (dq, dk, dv, G["l0_attn_sinks"]), outs = _attn_bwd(
        proj0, P["l0_attn_sinks"], dmix, "l0_attn_bwd",
        ex.scatter_plan(G, ex.AFTER_L0_OUT) if ex else None)
    if ex:
        ex.scattered(ex.AFTER_L0_OUT, outs)
    dproj0 = jnp.concatenate([dgate, dxr, dq, dk.astype(BF16), dv.astype(BF16)], axis=1)
    G["l0_w_in"] = _mm_tn_sh(u0, dproj0, EVEN_IN // N_CHIPS, "l0_in_dw")
    if ex:
        du0, outs = _mm_nt_sh(dproj0, P["l0_w_in"], "l0_in_dx", ex.scatter_plan(G, ex.LAST))
        ex.scattered(ex.LAST, outs)
    else:
        du0 = _mm_nt_sh(dproj0, P["l0_w_in"], "l0_in_dx")
    dh0, G["l0_mix_pre_norm"] = _prenorm_bwd(h0, P["l0_mix_pre_norm"], du0, dh1, "l0_mix_prenorm_bwd")
    return loss_cols, dh0[BLOCK:], dh0[PAD:BLOCK], G


_BIG_COL = ("l0_w_in", "l0_ffn_w_up", "l1_w_in", "l1_ffn_w_up")
_BIG = ("l0_w_in", "l0_w_out", "l0_ffn_w_up", "l0_ffn_w_down",
        "l1_w_in", "l1_w_out", "l1_ffn_w_up", "l1_ffn_w_down")
_SMALL_SHARDED = ("meta_tokens", "l0_lru_conv_w", "l0_ffn_conv_w", "l1_ssm_conv_w", "l1_ffn_conv_w")
_WEIGHTS = ("meta_tokens", "l0_mix_pre_norm", "l0_mix_post_norm", "l0_w_in", "l0_lru_conv_w",
            "l0_lru_conv_b", "l0_lru_w_a", "l0_lru_b_a", "l0_lru_w_x", "l0_lru_b_x", "l0_lru_lambda",
            "l0_attn_sinks", "l0_w_out", "l0_ffn_pre_norm", "l0_ffn_post_norm", "l0_ffn_w_up",
            "l0_ffn_conv_w", "l0_ffn_conv_b", "l0_ffn_w_down", "l1_mix_pre_norm", "l1_mix_post_norm",
            "l1_w_in", "l1_ssm_conv_w", "l1_ssm_conv_b", "l1_dt_bias", "l1_a_log", "l1_d_skip",
            "l1_gate_norm", "l1_w_out", "l1_ffn_pre_norm", "l1_ffn_post_norm", "l1_ffn_w_up",
            "l1_ffn_conv_w", "l1_ffn_conv_b", "l1_ffn_w_down")
_REPL = tuple(n for n in _WEIGHTS if n not in _BIG and n not in _SMALL_SHARDED)


def _pad_lanes(v, n=LANES):
    return jnp.concatenate([v, jnp.zeros((n - v.shape[0],), v.dtype)]).reshape(1, n)


def _step(x, tgt, W, M, V):
    cx, cy, cc = _place()
    chip = 2 * cx + cy

    small_pack, small_sizes = _pack([W[n] for n in _SMALL_SHARDED])
    first = _run_plan(_gather_plan([W["l0_w_in"].astype(BF16), small_pack]), "gather_first")
    small_full = _unpack(first[1], [W[n].shape for n in _SMALL_SHARDED], small_sizes, lead=(N_CHIPS,))
    ex = _Exchange({n: W[n].astype(BF16) for n in _BIG if n != "l0_w_in"})

    P = {"l0_w_in": first[0]}
    for n, g in zip(_SMALL_SHARDED, small_full):
        P[n] = _cols_from_shards(g)
    for n in _REPL:
        v = W[n]
        P[n] = v.reshape(1, -1) if v.ndim == 1 else v
    P["l0_lru_w_a"] = W["l0_lru_w_a"].astype(BF16)
    P["l0_lru_w_x"] = W["l0_lru_w_x"].astype(BF16)
    P["l1_dt_bias"] = _pad_lanes(W["l1_dt_bias"])
    P["l1_a_log"] = _pad_lanes(W["l1_a_log"])
    P["l1_d_skip"] = jnp.repeat(W["l1_d_skip"], SSD_P).reshape(1, D_SSM)
    meta = P.pop("meta_tokens")

    loss_cols, grad_x, grad_meta, G = _local_step(x, tgt, meta, P, ex)
    G["meta_tokens"] = grad_meta

    core_idx = cc.astype(jnp.int32).reshape(1)
    chip_idx = chip.astype(jnp.int32).reshape(1)
    own_half = [_add8(ex.slabs[n], ex.recv[n], chip_idx, core_idx, f"grad_sum_{n}") for n in _BIG]
    other_half = _run_plan(_sibling_plan(own_half), "grad_sibling_swap")
    small_names = list(_REPL) + list(_SMALL_SHARDED)
    small_list = [G[n] for n in small_names] + [loss_cols]
    spack, ssizes = _pack(small_list, total_mult=2 * SUBLANES * LANES)
    sred = _allreduce_small(spack, "small_allreduce")
    sfull = _unpack(sred, [a.shape for a in small_list], ssizes)
    loss = 0.5 / D_MODEL * jnp.sum(sfull[-1])
    small_grads = {}
    for n, g in zip(small_names, sfull[:-1]):
        if n in _SMALL_SHARDED:
            wcols = W[n].shape[1]
            g = lax.dynamic_slice_in_dim(g, chip * wcols, wcols, axis=1)
        small_grads[n] = g.reshape(W[n].shape)

    grads, delta, new_m, new_v = {}, {}, {}, {}
    for n, own, other in zip(_BIG, own_half, other_half):
        grads[n], delta[n], new_m[n], new_v[n] = _adamw_halves(
            W[n], own, other, M[n], V[n], core_idx, f"adamw_{n}")
    s_names = [n for n in _WEIGHTS if n not in _BIG]
    as2d = lambda a: a.reshape(_shape2d(a.shape))
    outs = _adamw_many([as2d(W[n]) for n in s_names], [as2d(small_grads[n]) for n in s_names],
                       [as2d(M[n]) for n in s_names], [as2d(V[n]) for n in s_names], "adamw_small")
    k = len(s_names)
    for i, n in enumerate(s_names):
        grads[n] = small_grads[n]
        delta[n], new_m[n], new_v[n] = (outs[j * k + i].reshape(W[n].shape) for j in range(3))
    return loss, grad_x, grads, delta, new_m, new_v


def kernel(x, meta_tokens, l0_mix_pre_norm, l0_mix_post_norm, l0_w_in, l0_lru_conv_w, l0_lru_conv_b, l0_lru_w_a, l0_lru_b_a, l0_lru_w_x, l0_lru_b_x, l0_lru_lambda, l0_attn_sinks, l0_w_out, l0_ffn_pre_norm, l0_ffn_post_norm, l0_ffn_w_up, l0_ffn_conv_w, l0_ffn_conv_b, l0_ffn_w_down, l1_mix_pre_norm, l1_mix_post_norm, l1_w_in, l1_ssm_conv_w, l1_ssm_conv_b, l1_dt_bias, l1_a_log, l1_d_skip, l1_gate_norm, l1_w_out, l1_ffn_pre_norm, l1_ffn_post_norm, l1_ffn_w_up, l1_ffn_conv_w, l1_ffn_conv_b, l1_ffn_w_down, loss_target, m_meta_tokens, m_l0_mix_pre_norm, m_l0_mix_post_norm, m_l0_w_in, m_l0_lru_conv_w, m_l0_lru_conv_b, m_l0_lru_w_a, m_l0_lru_b_a, m_l0_lru_w_x, m_l0_lru_b_x, m_l0_lru_lambda, m_l0_attn_sinks, m_l0_w_out, m_l0_ffn_pre_norm, m_l0_ffn_post_norm, m_l0_ffn_w_up, m_l0_ffn_conv_w, m_l0_ffn_conv_b, m_l0_ffn_w_down, m_l1_mix_pre_norm, m_l1_mix_post_norm, m_l1_w_in, m_l1_ssm_conv_w, m_l1_ssm_conv_b, m_l1_dt_bias, m_l1_a_log, m_l1_d_skip, m_l1_gate_norm, m_l1_w_out, m_l1_ffn_pre_norm, m_l1_ffn_post_norm, m_l1_ffn_w_up, m_l1_ffn_conv_w, m_l1_ffn_conv_b, m_l1_ffn_w_down, v_meta_tokens, v_l0_mix_pre_norm, v_l0_mix_post_norm, v_l0_w_in, v_l0_lru_conv_w, v_l0_lru_conv_b, v_l0_lru_w_a, v_l0_lru_b_a, v_l0_lru_w_x, v_l0_lru_b_x, v_l0_lru_lambda, v_l0_attn_sinks, v_l0_w_out, v_l0_ffn_pre_norm, v_l0_ffn_post_norm, v_l0_ffn_w_up, v_l0_ffn_conv_w, v_l0_ffn_conv_b, v_l0_ffn_w_down, v_l1_mix_pre_norm, v_l1_mix_post_norm, v_l1_w_in, v_l1_ssm_conv_w, v_l1_ssm_conv_b, v_l1_dt_bias, v_l1_a_log, v_l1_d_skip, v_l1_gate_norm, v_l1_w_out, v_l1_ffn_pre_norm, v_l1_ffn_post_norm, v_l1_ffn_w_up, v_l1_ffn_conv_w, v_l1_ffn_conv_b, v_l1_ffn_w_down):
    args = locals()
    W = {n: args[n] for n in _WEIGHTS}
    M = {n: args["m_" + n] for n in _WEIGHTS}
    V = {n: args["v_" + n] for n in _WEIGHTS}
    loss, grad_x, grads, delta, new_m, new_v = _step(x[0], loss_target[0], W, M, V)
    return (loss, grad_x[None], *[grads[n] for n in _WEIGHTS], *[delta[n] for n in _WEIGHTS],
            *[new_m[n] for n in _WEIGHTS], *[new_v[n] for n in _WEIGHTS])
```

```python
import functools
import math

import jax
import jax.numpy as jnp
from jax import lax
from jax.experimental import pallas as pl
from jax.experimental.pallas import tpu as pltpu

F32 = jnp.float32
BF16 = jnp.bfloat16

D_MODEL = 1024
N_META = 16
BLOCK = 128
PAD = BLOCK - N_META
EPS = 1e-6
LRU_BLOCKS = 8
LRU_BS = 128
LRU_C = 8.0
N_Q_HEADS = 16
N_KV_HEADS = 2
HEAD_DIM = 64
Q_PER_KV = 8
WINDOW = 128
D_SSM = 2048
SSD_HEADS = 32
SSD_GROUPS = 8
SSD_HPG = 4
SSD_P = 64
SSD_N = 128
D_FF = 2816
NEG = -1e30
LANES = 128
SUBLANES = 8
_VMEM_LIMIT_WIDE = 62 * 1024 * 1024

ADAM_LR = 0.001
ADAM_B1 = 0.9
ADAM_B2 = 0.999
ADAM_EPS = 1e-08
ADAM_WD = 0.01
ADAM_STEP = 10

MESH = pl.DeviceIdType.MESH
N_CHIPS = 4


def _pick(n, cands):
    for c in cands:
        if n % c == 0:
            return c
    raise ValueError(f"no tile for {n} in {cands}")


def _col_tile(n, limit=1792):
    best = None
    for t in range(LANES, min(n, limit) + 1, LANES):
        if n % t == 0:
            best = t
    if best is None:
        raise ValueError(f"no lane tile for {n}")
    return best


def _sigmoid(x):
    return 0.5 + 0.5 * jnp.tanh(0.5 * x)


def _log1p(e):
    u = 1.0 + e
    return jnp.where(u == 1.0, e, jnp.log(u) * (e / jnp.where(u == 1.0, 1.0, u - 1.0)))


def _softplus(x):
    return jnp.maximum(x, 0.0) + _log1p(jnp.exp(-jnp.abs(x)))


def _neg_expm1(x):
    poly = x * (1.0 + x * (0.5 + x * (1.0 / 6.0 + x * (1.0 / 24.0 + x * (1.0 / 120.0)))))
    return -jnp.where(x > -0.05, poly, jnp.exp(x) - 1.0)


_GELU_C = math.sqrt(2.0 / math.pi)


def _gelu(x):
    u = 0.5 + 0.5 * jnp.tanh(x * (_GELU_C + (_GELU_C * 0.044715) * (x * x)))
    return x * u


def _gelu_and_grad(x):
    x2 = x * x
    u = 0.5 + 0.5 * jnp.tanh(x * (_GELU_C + (_GELU_C * 0.044715) * x2))
    g = x * u
    dg = u * (1.0 + (x - g) * (2.0 * _GELU_C + (6.0 * 0.044715 * _GELU_C) * x2))
    return g, dg


def _silu_and_grad(x):
    s = _sigmoid(x)
    return x * s, s * (1.0 + x * (1.0 - s))


def _dot(a, b):
    return jnp.dot(a, b, preferred_element_type=F32)


def _dot_nt(a, b):
    return lax.dot_general(a, b, (((1,), (1,)), ((), ())), preferred_element_type=F32)


def _dot_tn(a, b):
    return lax.dot_general(a, b, (((0,), (0,)), ((), ())), preferred_element_type=F32)


def _row_iota(t):
    return lax.broadcasted_iota(jnp.int32, (t, 1), 0)


def _scan_fwd(a, u, t):
    row = _row_iota(t)
    d = 1
    while d < t:
        m = row >= d
        u_sh = jnp.where(m, pltpu.roll(u, d, 0), 0.0)
        a_sh = jnp.where(m, pltpu.roll(a, d, 0), 1.0)
        u = u + a * u_sh
        a = a * a_sh
        d *= 2
    return a, u


def _scan_rev(c, x, t):
    row = _row_iota(t)
    d = 1
    while d < t:
        m = row < t - d
        x_sh = jnp.where(m, pltpu.roll(x, t - d, 0), 0.0)
        c_sh = jnp.where(m, pltpu.roll(c, t - d, 0), 1.0)
        x = x + c * x_sh
        c = c * c_sh
        d *= 2
    return c, x


def _cumsum_rows(x, t):
    row = _row_iota(t)
    d = 1
    while d < t:
        x = x + jnp.where(row >= d, pltpu.roll(x, d, 0), 0.0)
        d *= 2
    return x


def _rev_cumsum_rows(x, t):
    row = _row_iota(t)
    d = 1
    while d < t:
        x = x + jnp.where(row < t - d, pltpu.roll(x, t - d, 0), 0.0)
        d *= 2
    return x


def _rms_bwd(x, g, dy):
    rs = lax.rsqrt(jnp.mean(x * x, axis=-1, keepdims=True) + EPS)
    gy = dy * g
    dx = rs * gy - x * (rs * rs * rs) * jnp.mean(x * gy, axis=-1, keepdims=True)
    return dx, dy * x * rs


def _mm_nt(dy, w, name):
    m, n = dy.shape
    k = w.shape[0]
    wide = n > 3328
    tm = _pick(m, (320, 256, 128)) if wide else _pick(m, (640, 512, 256, 128))
    tk = _col_tile(k, 512 if wide else 1408)

    def body(dy_ref, w_ref, o_ref):
        o_ref[...] = _dot_nt(dy_ref[...].astype(BF16), w_ref[...])

    return pl.pallas_call(
        body, name=name, grid=(k // tk, m // tm),
        in_specs=[pl.BlockSpec((tm, n), lambda j, i: (i, 0)),
                  pl.BlockSpec((tk, n), lambda j, i: (j, 0))],
        out_specs=pl.BlockSpec((tm, tk), lambda j, i: (i, j)),
        out_shape=jax.ShapeDtypeStruct((m, k), F32),
        compiler_params=pltpu.CompilerParams(dimension_semantics=("parallel", "parallel")),
    )(dy, w)


def _mm_tn(a, dy, name):
    m, k = a.shape
    n = dy.shape[1]
    tm = _pick(m, (640, 512, 256, 128))
    tk = _col_tile(k, 1408)
    tn = _col_tile(n, 1664)
    nsteps = m // tm

    def body(a_ref, dy_ref, o_ref, acc):
        @pl.when(pl.program_id(2) == 0)
        def _():
            acc[...] = jnp.zeros_like(acc)

        acc[...] += _dot_tn(a_ref[...].astype(BF16), dy_ref[...].astype(BF16))

        @pl.when(pl.program_id(2) == nsteps - 1)
        def _():
            o_ref[...] = acc[...].astype(o_ref.dtype)

    return pl.pallas_call(
        body, name=name, grid=(k // tk, n // tn, nsteps),
        in_specs=[pl.BlockSpec((tm, tk), lambda kk, j, i: (i, kk)),
                  pl.BlockSpec((tm, tn), lambda kk, j, i: (i, j))],
        out_specs=pl.BlockSpec((tk, tn), lambda kk, j, i: (kk, j)),
        out_shape=jax.ShapeDtypeStruct((k, n), BF16),
        scratch_shapes=[pltpu.VMEM((tk, tn), F32)],
        compiler_params=pltpu.CompilerParams(
            dimension_semantics=("parallel", "parallel", "arbitrary")),
    )(a, dy)


def _mm_nn_sh(a, w4, n_out, name):
    m, k = a.shape
    s, _, n = w4.shape
    tm = _pick(m, (320, 256, 128))

    def body(a_ref, w_ref, o_ref):
        av = a_ref[...].astype(BF16)
        for j in range(s):
            o_ref[:, j * n:(j + 1) * n] = _dot(av, w_ref[j])
        if n_out > s * n:
            o_ref[:, s * n:] = jnp.zeros((tm, n_out - s * n), F32)

    return pl.pallas_call(
        body, name=name, grid=(m // tm,),
        in_specs=[pl.BlockSpec((tm, k), lambda i: (i, 0)),
                  pl.BlockSpec((s, k, n), lambda i: (0, 0, 0))],
        out_specs=pl.BlockSpec((tm, n_out), lambda i: (i, 0)),
        out_shape=jax.ShapeDtypeStruct((m, n_out), F32),
        compiler_params=pltpu.CompilerParams(dimension_semantics=("parallel",)),
    )(a, w4)


def _mm_nt_sh(dy, w4, name, plan=None):
    dys = dy if isinstance(dy, (tuple, list)) else (dy,)
    m = dys[0].shape[0]
    s, k, n = w4.shape
    tm = _pick(m, (640, 512, 256, 128))
    tk = _col_tile(k, 1024)
    where = _shard_columns(dys, s, n)
    p_in, p_shapes, p_out, p_scr = _plan_parts(plan)
    nd, nj, ni = len(dys), k // tk, m // tm

    def body(*refs):
        w_ref = refs[nd]
        cins = refs[nd + 1:nd + 1 + len(p_in)]
        o_ref = refs[nd + 1 + len(p_in)]
        couts = refs[nd + 2 + len(p_in):nd + 2 + len(p_in) + len(p_out)]
        sems = refs[nd + 2 + len(p_in) + len(p_out):]
        step = pl.program_id(0) * ni + pl.program_id(1)
        if plan is not None:
            @pl.when(step == 0)
            def _():
                plan.start(cins, couts, sems)

        acc = None
        for j, (p, c0) in enumerate(where):
            t = _dot_nt(refs[p][:, c0:c0 + n].astype(BF16), w_ref[j])
            acc = t if acc is None else acc + t
        o_ref[...] = acc
        if plan is not None:
            @pl.when(step == nj * ni - 1)
            def _():
                plan.wait(cins, couts, sems)

    sem = ("arbitrary", "arbitrary") if plan is not None else ("parallel", "parallel")
    res = pl.pallas_call(
        body, name=name, grid=(nj, ni),
        in_specs=[pl.BlockSpec((tm, d.shape[1]), lambda j, i: (i, 0)) for d in dys]
        + [pl.BlockSpec((s, tk, n), lambda j, i: (0, j, 0))] + p_in,
        out_specs=[pl.BlockSpec((tm, tk), lambda j, i: (i, j))] + p_out,
        out_shape=[jax.ShapeDtypeStruct((m, k), F32)] + p_shapes,
        scratch_shapes=p_scr,
        compiler_params=pltpu.CompilerParams(dimension_semantics=sem),
    )(*dys, w4, *(plan.ins if plan is not None else []))
    return res[0] if plan is None else (res[0], res[1:])


def _shard_columns(dys, s, n):
    where = []
    for p, d in enumerate(dys):
        where += [(p, c * n) for c in range(d.shape[1] // n)]
    assert len(where) >= s
    return where[:s]


def _mm_tn_sh(a, dy, n, name):
    dys = dy if isinstance(dy, (tuple, list)) else (dy,)
    m, k = a.shape
    s = N_CHIPS
    tm = _pick(m, (640, 512, 256, 128))
    tk = _col_tile(k, 512)
    nsteps = m // tm
    where = _shard_columns(dys, s, n)

    def body(*refs):
        a_ref, o_ref, acc = refs[0], refs[len(dys) + 1], refs[len(dys) + 2]

        @pl.when(pl.program_id(1) == 0)
        def _():
            acc[...] = jnp.zeros_like(acc)

        av = a_ref[...].astype(BF16)
        for j, (p, c0) in enumerate(where):
            acc[j] += _dot_tn(av, refs[1 + p][:, c0:c0 + n].astype(BF16))

        @pl.when(pl.program_id(1) == nsteps - 1)
        def _():
            o_ref[...] = acc[...].astype(o_ref.dtype)

    return pl.pallas_call(
        body, name=name, grid=(k // tk, nsteps),
        in_specs=[pl.BlockSpec((tm, tk), lambda kk, i: (i, kk))]
        + [pl.BlockSpec((tm, d.shape[1]), lambda kk, i: (i, 0)) for d in dys],
        out_specs=pl.BlockSpec((s, tk, n), lambda kk, i: (0, kk, 0)),
        out_shape=jax.ShapeDtypeStruct((s, k, n), BF16),
        scratch_shapes=[pltpu.VMEM((s, tk, n), F32)],
        compiler_params=pltpu.CompilerParams(dimension_semantics=("parallel", "arbitrary"),
                                             vmem_limit_bytes=_VMEM_LIMIT_WIDE),
    )(a, *dys)


def _rowcall(name, body, lp, tm, rows=(), prevs=(), vecs=(), outs=(), accs=(), scratch=(),
             reverse=False, seq=False, plan=None):
    p_in, p_shapes, p_out, p_scr = _plan_parts(plan)
    nt = lp // tm
    hb = tm // SUBLANES

    def ri(i):
        return nt - 1 - i if reverse else i

    in_specs, args = [], []
    for arr, w, cb in rows:
        in_specs.append(pl.BlockSpec((tm, w), lambda i, cb=cb: (ri(i), cb)))
        args.append(arr)
    for arr, w, cb in prevs:
        in_specs.append(pl.BlockSpec((SUBLANES, w), lambda i, cb=cb: (jnp.maximum(ri(i) * hb - 1, 0), cb)))
        args.append(arr)
    for arr in vecs:
        in_specs.append(pl.BlockSpec(arr.shape, lambda i, nd=arr.ndim: (0,) * nd))
        args.append(arr)
    out_shape, out_specs = [], []
    for w, dt in outs:
        out_shape.append(jax.ShapeDtypeStruct((lp, w), dt))
        out_specs.append(pl.BlockSpec((tm, w), lambda i: (ri(i), 0)))
    for shp, dt in accs:
        out_shape.append(jax.ShapeDtypeStruct(shp, dt))
        out_specs.append(pl.BlockSpec(shp, lambda i, nd=len(shp): (0,) * nd))

    n_in, n_out, n_scr = len(args), len(out_shape), len(scratch)

    def kern(*refs):
        i = pl.program_id(0)
        own = (refs[:n_in] + refs[n_in + len(p_in):n_in + len(p_in) + n_out]
               + refs[n_in + len(p_in) + n_out + len(p_out):n_in + len(p_in) + n_out + len(p_out) + n_scr])
        cins = refs[n_in:n_in + len(p_in)]
        couts = refs[n_in + len(p_in) + n_out:n_in + len(p_in) + n_out + len(p_out)]
        sems = refs[n_in + len(p_in) + n_out + len(p_out) + n_scr:]
        if plan is not None:
            @pl.when(i == 0)
            def _():
                plan.start(cins, couts, sems)

        body(ri(i), i == 0, *own)
        if plan is not None:
            @pl.when(i == nt - 1)
            def _():
                plan.wait(cins, couts, sems)

    sem = ("arbitrary",) if (seq or accs or plan is not None) else ("parallel",)
    res = pl.pallas_call(
        kern, name=name, grid=(nt,), in_specs=in_specs + p_in, out_specs=out_specs + p_out,
        out_shape=out_shape + p_shapes, scratch_shapes=list(scratch) + p_scr,
        compiler_params=pltpu.CompilerParams(dimension_semantics=sem),
    )(*args, *(plan.ins if plan is not None else []))
    return res if plan is None else (res[:n_out], res[n_out:])


def _acc_add(first, ref, val):
    @pl.when(first)
    def _():
        ref[...] = jnp.zeros_like(ref)

    ref[...] += val


def _real_rows(r, tm):
    return (r * tm + _row_iota(tm)) >= PAD


def _rmsnorm_fwd(h, g, name):
    lp, d = h.shape
    tm = _pick(lp, (640, 512, 256, 128))

    def body(r, first, h_ref, g_ref, u_ref):
        x = h_ref[...]
        rs = lax.rsqrt(jnp.mean(x * x, axis=-1, keepdims=True) + EPS)
        u_ref[...] = (x * rs * g_ref[...]).astype(u_ref.dtype)

    return _rowcall(name, body, lp, tm, rows=[(h, d, 0)], vecs=[g], outs=[(d, BF16)])[0]


def _mm_postnorm_res(a, w, h, g, name, next_pre=None):
    parts = a if isinstance(a, (tuple, list)) else (a,)
    lp, d = h.shape
    k = w.shape[0]
    tm = _pick(lp, (640, 512, 256, 128))
    offs = [sum(p.shape[1] for p in parts[:i]) for i in range(len(parts))]
    np_ = len(parts)

    def body(*refs):
        w_ref, h_ref, g_ref = refs[np_], refs[np_ + 1], refs[np_ + 2]
        rest = refs[np_ + 3:]
        acc = None
        for a_ref, p, off in zip(refs, parts, offs):
            t = _dot(a_ref[...].astype(BF16), w_ref[off:off + p.shape[1], :])
            acc = t if acc is None else acc + t
        outs = rest[1:] if next_pre is not None else rest
        outs[0][...] = acc
        rs = lax.rsqrt(jnp.mean(acc * acc, axis=-1, keepdims=True) + EPS)
        hn = jnp.where(_real_rows(pl.program_id(0), tm), h_ref[...] + acc * rs * g_ref[...], 0.0)
        outs[1][...] = hn
        if next_pre is not None:
            rs2 = lax.rsqrt(jnp.mean(hn * hn, axis=-1, keepdims=True) + EPS)
            outs[2][...] = (hn * rs2 * rest[0][...]).astype(BF16)

    row = pl.BlockSpec((tm, d), lambda i: (i, 0))
    vec = pl.BlockSpec((1, d), lambda i: (0, 0))
    n_vec = 2 if next_pre is not None else 1
    return pl.pallas_call(
        body, name=name, grid=(lp // tm,),
        in_specs=[pl.BlockSpec((tm, p.shape[1]), lambda i: (i, 0)) for p in parts]
        + [pl.BlockSpec((k, d), lambda i: (0, 0)), row] + [vec] * n_vec,
        out_specs=[row] * (2 + (next_pre is not None)),
        out_shape=[jax.ShapeDtypeStruct((lp, d), F32)] * 2
        + ([jax.ShapeDtypeStruct((lp, d), BF16)] if next_pre is not None else []),
        compiler_params=pltpu.CompilerParams(dimension_semantics=("parallel",)),
    )(*parts, w, h, g, *([next_pre] if next_pre is not None else []))


def _mm_postnorm_loss(a, w, h, g, tgt, name):
    lp, d = h.shape
    k = w.shape[0]
    tm = _pick(lp, (640, 512, 256, 128))

    def body(a_ref, w_ref, h_ref, g_ref, t_ref, o_ref, dh_ref, ls_ref):
        i = pl.program_id(0)
        acc = _dot(a_ref[...].astype(BF16), w_ref[...])
        o_ref[...] = acc
        rs = lax.rsqrt(jnp.mean(acc * acc, axis=-1, keepdims=True) + EPS)
        tok = (i * tm + _row_iota(tm)) >= BLOCK
        e = jnp.where(tok, h_ref[...] + acc * rs * g_ref[...] - t_ref[...], 0.0)
        dh_ref[...] = e * (1.0 / d)
        _acc_add(i == 0, ls_ref, jnp.sum(e * e, axis=0, keepdims=True))

    row = pl.BlockSpec((tm, d), lambda i: (i, 0))
    vec = pl.BlockSpec((1, d), lambda i: (0, 0))
    return pl.pallas_call(
        body, name=name, grid=(lp // tm,),
        in_specs=[pl.BlockSpec((tm, k), lambda i: (i, 0)), pl.BlockSpec((k, d), lambda i: (0, 0)),
                  row, vec, row],
        out_specs=[row, row, vec],
        out_shape=[jax.ShapeDtypeStruct((lp, d), F32), jax.ShapeDtypeStruct((lp, d), F32),
                   jax.ShapeDtypeStruct((1, d), F32)],
        compiler_params=pltpu.CompilerParams(dimension_semantics=("arbitrary",)),
    )(a, w, h, g, tgt)


def _postnorm_bwd(o, g, dh, name):
    lp, d = o.shape
    tm = _pick(lp, (640, 512, 256, 128))

    def body(r, first, o_ref, dh_ref, g_ref, do_ref, dg_ref):
        dx, dgt = _rms_bwd(o_ref[...], g_ref[...], dh_ref[...])
        do_ref[...] = dx.astype(do_ref.dtype)
        _acc_add(first, dg_ref, jnp.sum(dgt, axis=0, keepdims=True))

    return _rowcall(name, body, lp, tm, rows=[(o, d, 0), (dh, d, 0)], vecs=[g],
                    outs=[(d, BF16)], accs=[((1, d), F32)])


def _prenorm_bwd(h, g, du, dh_res, name):
    lp, d = h.shape
    tm = _pick(lp, (640, 512, 256, 128))

    def body(r, first, h_ref, du_ref, dres_ref, g_ref, dh_ref, dg_ref):
        dx, dgt = _rms_bwd(h_ref[...], g_ref[...], du_ref[...])
        dh_ref[...] = jnp.where(_real_rows(r, tm), dres_ref[...] + dx, 0.0)
        _acc_add(first, dg_ref, jnp.sum(dgt, axis=0, keepdims=True))

    return _rowcall(name, body, lp, tm, rows=[(h, d, 0), (du, d, 0), (dh_res, d, 0)], vecs=[g],
                    outs=[(d, F32)], accs=[((1, d), F32)])


def _conv_tiles(lp, width):
    wc = _col_tile(width, 1408)
    tm = _pick(lp, (320, 256, 128))
    return tm, wc


def _conv_fwd(x, col_off, width, w, b, name):
    lp = x.shape[0]
    kk = w.shape[0]
    tm, wc = _conv_tiles(lp, width)
    offb = col_off // wc
    assert col_off % wc == 0
    hb = tm // SUBLANES

    def body(x_ref, xp_ref, w_ref, b_ref, y_ref):
        i = pl.program_id(1)
        xv = x_ref[...]
        halo = jnp.where(i > 0, xp_ref[...], 0.0)
        xx = jnp.concatenate([halo, xv], axis=0)
        acc = b_ref[...] + w_ref[kk - 1:kk, :] * xv
        for j in range(1, kk):
            acc = acc + w_ref[kk - 1 - j:kk - j, :] * pltpu.roll(xx, j, 0)[SUBLANES:, :]
        y_ref[...] = acc

    return pl.pallas_call(
        body, name=name, grid=(width // wc, lp // tm),
        in_specs=[pl.BlockSpec((tm, wc), lambda j, i: (i, offb + j)),
                  pl.BlockSpec((SUBLANES, wc), lambda j, i: (jnp.maximum(i * hb - 1, 0), offb + j)),
                  pl.BlockSpec((kk, wc), lambda j, i: (0, j)),
                  pl.BlockSpec((1, wc), lambda j, i: (0, j))],
        out_specs=pl.BlockSpec((tm, wc), lambda j, i: (i, j)),
        out_shape=jax.ShapeDtypeStruct((lp, width), F32),
        compiler_params=pltpu.CompilerParams(dimension_semantics=("parallel", "parallel")),
    )(x, x, w, b)


def _conv_bwd(x, col_off, width, dy, w, name, w_col_off=0):
    lp = x.shape[0]
    kk = w.shape[0]
    tm, wc = _conv_tiles(lp, width)
    offb = col_off // wc
    woffb = w_col_off // wc
    assert col_off % wc == 0 and w_col_off % wc == 0
    hrows = SUBLANES * (4 // dy.dtype.itemsize)
    ext = tm + hrows

    def body(x_ref, dy_ref, dn_ref, w_ref, dx_ref, dw_ref, db_ref):
        i = pl.program_id(1)
        last = pl.num_programs(1) - 1
        xv = x_ref[...]
        dyv = dy_ref[...].astype(F32)
        dd = jnp.concatenate([dyv, jnp.where(i < last, dn_ref[...].astype(F32), 0.0)], axis=0)
        dx = w_ref[kk - 1:kk, :] * dyv
        rows = [jnp.sum(dyv * xv, axis=0, keepdims=True)]
        for m in range(1, kk):
            ahead = pltpu.roll(dd, ext - m, 0)[:tm, :]
            dx = dx + w_ref[kk - 1 - m:kk - m, :] * ahead
            rows.append(jnp.sum(ahead * xv, axis=0, keepdims=True))
        dx_ref[...] = dx.astype(dx_ref.dtype)
        dwp = jnp.concatenate(rows[::-1] + [jnp.zeros((SUBLANES - kk, wc), F32)], axis=0)

        @pl.when(i == 0)
        def _():
            dw_ref[...] = jnp.zeros_like(dw_ref)
            db_ref[...] = jnp.zeros_like(db_ref)

        dw_ref[...] += dwp
        db_ref[...] += jnp.sum(dyv, axis=0, keepdims=True)

    return pl.pallas_call(
        body, name=name, grid=(width // wc, lp // tm),
        in_specs=[pl.BlockSpec((tm, wc), lambda j, i: (i, offb + j)),
                  pl.BlockSpec((tm, wc), lambda j, i: (i, j)),
                  pl.BlockSpec((hrows, wc), lambda j, i: (jnp.minimum((i + 1) * (tm // hrows), lp // hrows - 1), j)),
                  pl.BlockSpec((kk, wc), lambda j, i: (0, woffb + j))],
        out_specs=[pl.BlockSpec((tm, wc), lambda j, i: (i, j)),
                   pl.BlockSpec((SUBLANES, wc), lambda j, i: (0, j)),
                   pl.BlockSpec((1, wc), lambda j, i: (0, j))],
        out_shape=[jax.ShapeDtypeStruct((lp, width), BF16),
                   jax.ShapeDtypeStruct((SUBLANES, width), F32),
                   jax.ShapeDtypeStruct((1, width), F32)],
        compiler_params=pltpu.CompilerParams(dimension_semantics=("parallel", "arbitrary")),
    )(x, dy, dy, w)


_FFN_K = 3
_FFN_WC = 1408


def _conv3_ext(x_ext, w_ref, b_ref):
    return (b_ref[...] + w_ref[2:3, :] * x_ext + w_ref[1:2, :] * pltpu.roll(x_ext, 1, 0)
            + w_ref[0:1, :] * pltpu.roll(x_ext, 2, 0))


def _ffn_convact_fwd(hp, cw, cb, name, plan=None):
    lp = hp.shape[0]
    tm = _pick(lp, (320, 256, 128))
    wc = _FFN_WC
    nj = D_FF // wc
    ni = lp // tm
    hb = tm // SUBLANES
    p_in, p_shapes, p_out, p_scr = _plan_parts(plan)

    def body(*refs):
        g_ref, gp_ref, u_ref, up_ref, wg_ref, wu_ref, bg_ref, bu_ref = refs[:8]
        cins = refs[8:8 + len(p_in)]
        a_ref, hg_ref, hu_ref = refs[8 + len(p_in):11 + len(p_in)]
        couts = refs[11 + len(p_in):11 + len(p_in) + len(p_out)]
        sems = refs[11 + len(p_in) + len(p_out):]
        i = pl.program_id(1)
        step = pl.program_id(0) * ni + i
        if plan is not None:
            @pl.when(step == 0)
            def _():
                plan.start(cins, couts, sems)

        def conv(x_ref, p_ref, w_ref, b_ref):
            x_ext = jnp.concatenate([jnp.where(i > 0, p_ref[...], 0.0), x_ref[...]], axis=0)
            return _conv3_ext(x_ext, w_ref, b_ref)[SUBLANES:, :]

        hg = conv(g_ref, gp_ref, wg_ref, bg_ref)
        hu = conv(u_ref, up_ref, wu_ref, bu_ref)
        a_ref[...] = (_gelu(hg) * hu).astype(a_ref.dtype)
        hg_ref[...] = hg.astype(hg_ref.dtype)
        hu_ref[...] = hu.astype(hu_ref.dtype)
        if plan is not None:
            @pl.when(step == nj * ni - 1)
            def _():
                plan.wait(cins, couts, sems)

    tile = lambda off: pl.BlockSpec((tm, wc), lambda j, i: (i, off + j))
    prev = lambda off: pl.BlockSpec((SUBLANES, wc), lambda j, i: (jnp.maximum(i * hb - 1, 0), off + j))
    vec = lambda rows, off: pl.BlockSpec((rows, wc), lambda j, i: (0, off + j))
    sem = ("arbitrary", "arbitrary") if plan is not None else ("parallel", "parallel")
    res = pl.pallas_call(
        body, name=name, grid=(nj, ni),
        in_specs=[tile(0), prev(0), tile(nj), prev(nj), vec(_FFN_K, 0), vec(_FFN_K, nj), vec(1, 0),
                  vec(1, nj)] + p_in,
        out_specs=[tile(0)] * 3 + p_out,
        out_shape=[jax.ShapeDtypeStruct((lp, D_FF), BF16)] * 3 + p_shapes,
        scratch_shapes=p_scr,
        compiler_params=pltpu.CompilerParams(dimension_semantics=sem),
    )(hp, hp, hp, hp, cw, cw, cb, cb, *(plan.ins if plan is not None else []))
    return res if plan is None else (res[:3], res[3:])


def _ffn_down_act_bwd(do, w_down, hg, hu, name):
    lp, d = do.shape
    tm = _pick(lp, (640, 512, 256, 128))
    tk = _FFN_WC

    def body(do_ref, w_ref, g_ref, u_ref, dg_ref, du_ref):
        da = _dot_nt(do_ref[...], w_ref[...])
        gl, dgl = _gelu_and_grad(g_ref[...].astype(F32))
        dg_ref[...] = (da * u_ref[...].astype(F32) * dgl).astype(dg_ref.dtype)
        du_ref[...] = (da * gl).astype(du_ref.dtype)

    tile = pl.BlockSpec((tm, tk), lambda j, i: (i, j))
    return pl.pallas_call(
        body, name=name, grid=(D_FF // tk, lp // tm),
        in_specs=[pl.BlockSpec((tm, d), lambda j, i: (i, 0)),
                  pl.BlockSpec((tk, d), lambda j, i: (j, 0)), tile, tile],
        out_specs=[tile, tile],
        out_shape=[jax.ShapeDtypeStruct((lp, D_FF), BF16)] * 2,
        compiler_params=pltpu.CompilerParams(dimension_semantics=("parallel", "parallel")),
    )(do, w_down, hg, hu)


def _lru_gates(x, wa_ref, wx_ref, ba, bx, lam):
    xb = x.astype(BF16)
    za, zx = [], []
    for n in range(LRU_BLOCKS):
        xs = xb[:, n * LRU_BS:(n + 1) * LRU_BS]
        za.append(_dot(xs, wa_ref[n]))
        zx.append(_dot(xs, wx_ref[n]))
    r = _sigmoid(jnp.concatenate(za, axis=1) + ba)
    ig = _sigmoid(jnp.concatenate(zx, axis=1) + bx)
    sp = _softplus(-lam)
    log_a = -LRU_C * r * sp
    a = jnp.exp(log_a)
    om = _neg_expm1(2.0 * log_a)
    mult = jnp.sqrt(om)
    return xb, r, ig, sp, a, om, mult


def _lru_fwd(proj, xrc, wa, wx, ba, bx, lam, name, plan=None):
    lp, d = xrc.shape
    tm = BLOCK

    def body(r_idx, first, gate_ref, x_ref, wa_ref, wx_ref, ba_ref, bx_ref, lam_ref,
             y_ref, h_ref, carry):
        @pl.when(first)
        def _():
            carry[...] = jnp.zeros_like(carry)

        x = x_ref[...]
        _, _, ig, _, a, _, mult = _lru_gates(x, wa_ref, wx_ref, ba_ref[...], bx_ref[...], lam_ref[...])
        u = jnp.where(_real_rows(r_idx, tm), mult * ig * x, 0.0)
        acum, hloc = _scan_fwd(a, u, tm)
        h = hloc + acum * carry[0:1, :]
        h_ref[...] = h
        carry[0:1, :] = h[tm - 1:tm, :]
        y_ref[...] = (_gelu(gate_ref[...]) * h).astype(y_ref.dtype)

    return _rowcall(name, body, lp, tm, rows=[(proj, d, 0), (xrc, d, 0)],
                    vecs=[wa, wx, ba, bx, lam], outs=[(d, BF16), (d, F32)],
                    scratch=[pltpu.VMEM((SUBLANES, d), F32)], seq=True, plan=plan)


def _lru_bwd(proj, xrc, hl, dmix, wa, wx, ba, bx, lam, name, plan=None):
    lp, d = xrc.shape
    tm = BLOCK

    def body(r_idx, first, gate_ref, x_ref, h_ref, dy_ref, hp_ref, wa_ref, wx_ref, ba_ref, bx_ref,
             lam_ref, dgate_ref, dx_ref, dwa_ref, dwx_ref, dba_ref, dbx_ref, dlam_ref, carry):
        @pl.when(first)
        def _():
            carry[...] = jnp.zeros_like(carry)
            dwa_ref[...] = jnp.zeros_like(dwa_ref)
            dwx_ref[...] = jnp.zeros_like(dwx_ref)
            dba_ref[...] = jnp.zeros_like(dba_ref)
            dbx_ref[...] = jnp.zeros_like(dbx_ref)
            dlam_ref[...] = jnp.zeros_like(dlam_ref)

        x = x_ref[...]
        lam = lam_ref[...]
        xb, r, ig, sp, a, om, mult = _lru_gates(x, wa_ref, wx_ref, ba_ref[...], bx_ref[...], lam)
        h = h_ref[...]
        dy = dy_ref[...]
        gl, dgl = _gelu_and_grad(gate_ref[...])
        dgate_ref[...] = (dy * h * dgl).astype(dgate_ref.dtype)
        row = _row_iota(tm)
        lastrow = row == tm - 1
        xg = dy * gl + jnp.where(lastrow, carry[0:1, :], 0.0)
        c = jnp.where(lastrow, 1.0, pltpu.roll(a, tm - 1, 0))
        _, g = _scan_rev(c, xg, tm)
        carry[0:1, :] = a[0:1, :] * g[0:1, :]
        hprev_in = jnp.where(r_idx > 0, hp_ref[SUBLANES - 1:SUBLANES, :], 0.0)
        hprev = jnp.where(row == 0, hprev_in, pltpu.roll(h, 1, 0))
        du = jnp.where(_real_rows(r_idx, tm), g, 0.0)
        da = g * hprev
        dmult = du * ig * x
        dig = du * mult * x
        dxv = du * mult * ig
        e2 = 1.0 - om
        dlog_a = da * a - dmult * e2 / mult
        dr = dlog_a * (-LRU_C) * sp
        dsp = jnp.sum(dlog_a * (-LRU_C) * r, axis=0, keepdims=True)
        dlam_ref[...] += -dsp * _sigmoid(-lam)
        dza = dr * r * (1.0 - r)
        dzx = dig * ig * (1.0 - ig)
        dba_ref[...] += jnp.sum(dza, axis=0, keepdims=True)
        dbx_ref[...] += jnp.sum(dzx, axis=0, keepdims=True)
        dzab = dza.astype(BF16)
        dzxb = dzx.astype(BF16)
        parts = []
        for n in range(LRU_BLOCKS):
            sl = slice(n * LRU_BS, (n + 1) * LRU_BS)
            dwa_ref[n] += _dot_tn(xb[:, sl], dzab[:, sl])
            dwx_ref[n] += _dot_tn(xb[:, sl], dzxb[:, sl])
            parts.append(_dot_nt(dzab[:, sl], wa_ref[n]) + _dot_nt(dzxb[:, sl], wx_ref[n]))
        dx_ref[...] = dxv + jnp.concatenate(parts, axis=1)

    return _rowcall(name, body, lp, tm,
                    rows=[(proj, d, 0), (xrc, d, 0), (hl, d, 0), (dmix, d, 0)],
                    prevs=[(hl, d, 0)], vecs=[wa, wx, ba, bx, lam],
                    outs=[(d, BF16), (d, F32)],
                    accs=[((LRU_BLOCKS, LRU_BS, LRU_BS), F32), ((LRU_BLOCKS, LRU_BS, LRU_BS), F32),
                          ((1, d), F32), ((1, d), F32), ((1, d), F32)],
                    scratch=[pltpu.VMEM((SUBLANES, d), F32)], reverse=True, seq=True, plan=plan)


_SLOPES = [2.0 ** (-8.0 * (h + 1) / N_Q_HEADS) for h in range(N_Q_HEADS)]
_QK_SCALE = HEAD_DIM ** -0.5
_QCOL = 2 * D_MODEL // D_MODEL
_KCOL = (3 * D_MODEL) // LANES
_VCOL = _KCOL + 1


def _attn_masks(n):
    start = pl.multiple_of(jnp.maximum(n - 1, 0) * BLOCK, BLOCK)
    qi = n * BLOCK + lax.broadcasted_iota(jnp.int32, (BLOCK, 2 * BLOCK), 0)
    kj = start + lax.broadcasted_iota(jnp.int32, (BLOCK, 2 * BLOCK), 1)
    dist = qi - kj
    ok = (kj >= BLOCK) & (dist >= 0) & (dist < WINDOW)
    dm = (n * BLOCK - PAD + lax.broadcasted_iota(jnp.int32, (BLOCK, N_META), 0)
          - lax.broadcasted_iota(jnp.int32, (BLOCK, N_META), 1))
    okm = dm >= 0
    return start, ok, dist.astype(F32), okm, jnp.minimum(dm, WINDOW).astype(F32)


def _group_rows(ref, g, scale=None):
    x = jnp.concatenate(
        [ref[:, (g * Q_PER_KV + hh) * HEAD_DIM:(g * Q_PER_KV + hh + 1) * HEAD_DIM] for hh in range(Q_PER_KV)],
        axis=0)
    return (x if scale is None else x * scale).astype(BF16)


def _attn_probs(s, sm, sink_ref, g, ok, distf, okm, dmf):
    slope = jnp.stack([jnp.full((1, 1), _SLOPES[g * Q_PER_KV + hh], F32) for hh in range(Q_PER_KV)])
    sink = jnp.stack([sink_ref[0:1, g * Q_PER_KV + hh:g * Q_PER_KV + hh + 1] for hh in range(Q_PER_KV)])
    s = s.reshape(Q_PER_KV, BLOCK, 2 * BLOCK)
    sm = sm.reshape(Q_PER_KV, BLOCK, N_META)
    s = jnp.where(ok[None], s - slope * distf[None], NEG)
    sm = jnp.where(okm[None], sm - slope * dmf[None], NEG)
    mx = jnp.maximum(jnp.maximum(jnp.max(s, axis=-1, keepdims=True),
                                 jnp.max(sm, axis=-1, keepdims=True)), sink)
    p = jnp.exp(s - mx)
    pm = jnp.exp(sm - mx)
    ps = jnp.exp(sink - mx)
    inv = 1.0 / (jnp.sum(p, axis=-1, keepdims=True) + jnp.sum(pm, axis=-1, keepdims=True) + ps)
    return p, pm, ps, inv


def _attn_fwd(proj, sinks, name, plan=None):
    lp = proj.shape[0]
    nblk = lp // BLOCK
    p_in, p_shapes, p_out, p_scr = _plan_parts(plan)

    def body(*refs):
        q_ref, k_ref, v_ref, sink_ref = refs[:4]
        cins = refs[4:4 + len(p_in)]
        o_ref = refs[4 + len(p_in)]
        couts = refs[5 + len(p_in):5 + len(p_in) + len(p_out)]
        sems = refs[5 + len(p_in) + len(p_out):]
        n = pl.program_id(0)
        if plan is not None:
            @pl.when(n == 0)
            def _():
                plan.start(cins, couts, sems)

        start, ok, distf, okm, dmf = _attn_masks(n)
        kb = k_ref[pl.ds(start, 2 * BLOCK), :].astype(BF16)
        vb = v_ref[pl.ds(start, 2 * BLOCK), :].astype(BF16)
        km = k_ref[PAD:BLOCK, :].astype(BF16)
        vm = v_ref[PAD:BLOCK, :].astype(BF16)
        rows = Q_PER_KV * BLOCK
        gsl = [slice(g * HEAD_DIM, (g + 1) * HEAD_DIM) for g in range(N_KV_HEADS)]
        raw = []
        for g in range(N_KV_HEADS):
            qg = _group_rows(q_ref, g, _QK_SCALE)
            raw.append((_dot_nt(qg, kb[:, gsl[g]]), _dot_nt(qg, km[:, gsl[g]])))
        for g in range(N_KV_HEADS):
            gs = gsl[g]
            p, pm, _, inv = _attn_probs(raw[g][0], raw[g][1], sink_ref, g, ok, distf, okm, dmf)
            o = (_dot(p.astype(BF16).reshape(rows, 2 * BLOCK), vb[:, gs])
                 + _dot(pm.astype(BF16).reshape(rows, N_META), vm[:, gs])) * inv.reshape(rows, 1)
            for hh in range(Q_PER_KV):
                h = g * Q_PER_KV + hh
                o_ref[:, h * HEAD_DIM:(h + 1) * HEAD_DIM] = o[hh * BLOCK:(hh + 1) * BLOCK, :].astype(o_ref.dtype)
        if plan is not None:
            @pl.when(n == nblk - 1)
            def _():
                plan.wait(cins, couts, sems)

    res = pl.pallas_call(
        body, name=name, grid=(nblk,),
        in_specs=[pl.BlockSpec((BLOCK, D_MODEL), lambda n: (n, _QCOL)),
                  pl.BlockSpec((lp, LANES), lambda n: (0, _KCOL)),
                  pl.BlockSpec((lp, LANES), lambda n: (0, _VCOL)),
                  pl.BlockSpec(sinks.shape, lambda n: (0, 0))] + p_in,
        out_specs=[pl.BlockSpec((BLOCK, D_MODEL), lambda n: (n, 0))] + p_out,
        out_shape=[jax.ShapeDtypeStruct((lp, D_MODEL), BF16)] + p_shapes,
        scratch_shapes=p_scr,
        compiler_params=pltpu.CompilerParams(dimension_semantics=("arbitrary",)),
    )(proj, proj, proj, sinks, *(plan.ins if plan is not None else []))
    return res[0], res[1:]


def _attn_bwd(proj, sinks, dmix, name, plan=None):
    lp = proj.shape[0]
    nblk = lp // BLOCK

    p_in, p_shapes, p_out, p_scr = _plan_parts(plan)

    def body(*refs):
        q_ref, k_ref, v_ref, sink_ref, dy_ref = refs[:5]
        cins = refs[5:5 + len(p_in)]
        dq_ref, dk_ref, dv_ref, ds_ref = refs[5 + len(p_in):9 + len(p_in)]
        couts = refs[9 + len(p_in):9 + len(p_in) + len(p_out)]
        sems = refs[9 + len(p_in) + len(p_out):]
        n = pl.program_id(0)

        @pl.when(n == 0)
        def _():
            dk_ref[...] = jnp.zeros_like(dk_ref)
            dv_ref[...] = jnp.zeros_like(dv_ref)
            ds_ref[...] = jnp.zeros_like(ds_ref)
            if plan is not None:
                plan.start(cins, couts, sems)

        start, ok, distf, okm, dmf = _attn_masks(n)
        kb = k_ref[pl.ds(start, 2 * BLOCK), :].astype(BF16)
        vb = v_ref[pl.ds(start, 2 * BLOCK), :].astype(BF16)
        km = k_ref[PAD:BLOCK, :].astype(BF16)
        vm = v_ref[PAD:BLOCK, :].astype(BF16)
        lane16 = lax.broadcasted_iota(jnp.int32, (1, N_Q_HEADS), 1)
        dsink = jnp.zeros((1, N_Q_HEADS), F32)
        rows = Q_PER_KV * BLOCK
        gsl = [slice(g * HEAD_DIM, (g + 1) * HEAD_DIM) for g in range(N_KV_HEADS)]
        pre = []
        for g in range(N_KV_HEADS):
            qg = _group_rows(q_ref, g, _QK_SCALE)
            dog = _group_rows(dy_ref, g)
            pre.append((qg, dog, _dot_nt(qg, kb[:, gsl[g]]), _dot_nt(qg, km[:, gsl[g]]),
                        _dot_nt(dog, vb[:, gsl[g]]), _dot_nt(dog, vm[:, gsl[g]])))
        for g in range(N_KV_HEADS):
            gs = gsl[g]
            qg, dog, s_raw, sm_raw, dp, dpm = pre[g]
            p, pm, ps, inv = _attn_probs(s_raw, sm_raw, sink_ref, g, ok, distf, okm, dmf)
            pn, pmn, psn = p * inv, pm * inv, ps * inv
            dp = dp.reshape(Q_PER_KV, BLOCK, 2 * BLOCK)
            dpm = dpm.reshape(Q_PER_KV, BLOCK, N_META)
            delta = (jnp.sum(pn * dp, axis=-1, keepdims=True)
                     + jnp.sum(pmn * dpm, axis=-1, keepdims=True))
            dsb = (pn * (dp - delta)).astype(BF16).reshape(rows, 2 * BLOCK)
            dsm = (pmn * (dpm - delta)).astype(BF16).reshape(rows, N_META)
            dsk = jnp.sum(psn * delta, axis=1, keepdims=True)
            for hh in range(Q_PER_KV):
                dsink = dsink - jnp.where(lane16 == g * Q_PER_KV + hh, dsk[hh], 0.0)
            dq = (_dot(dsb, kb[:, gs]) + _dot(dsm, km[:, gs])) * _QK_SCALE
            for hh in range(Q_PER_KV):
                h = g * Q_PER_KV + hh
                dq_ref[:, h * HEAD_DIM:(h + 1) * HEAD_DIM] = dq[hh * BLOCK:(hh + 1) * BLOCK, :].astype(dq_ref.dtype)
            pnb = pn.astype(BF16).reshape(rows, 2 * BLOCK)
            pmnb = pmn.astype(BF16).reshape(rows, N_META)
            dk_ref[pl.ds(start, 2 * BLOCK), gs] += _dot_tn(dsb, qg)
            dv_ref[pl.ds(start, 2 * BLOCK), gs] += _dot_tn(pnb, dog)
            dk_ref[PAD:BLOCK, gs] += _dot_tn(dsm, qg)
            dv_ref[PAD:BLOCK, gs] += _dot_tn(pmnb, dog)
        ds_ref[...] += dsink
        if plan is not None:
            @pl.when(n == nblk - 1)
            def _():
                plan.wait(cins, couts, sems)

    res = pl.pallas_call(
        body, name=name, grid=(nblk,),
        in_specs=[pl.BlockSpec((BLOCK, D_MODEL), lambda n: (n, _QCOL)),
                  pl.BlockSpec((lp, LANES), lambda n: (0, _KCOL)),
                  pl.BlockSpec((lp, LANES), lambda n: (0, _VCOL)),
                  pl.BlockSpec(sinks.shape, lambda n: (0, 0)),
                  pl.BlockSpec((BLOCK, D_MODEL), lambda n: (n, 1))] + p_in,
        out_specs=[pl.BlockSpec((BLOCK, D_MODEL), lambda n: (n, 0)),
                   pl.BlockSpec((lp, LANES), lambda n: (0, 0)),
                   pl.BlockSpec((lp, LANES), lambda n: (0, 0)),
                   pl.BlockSpec((1, N_Q_HEADS), lambda n: (0, 0))] + p_out,
        out_shape=[jax.ShapeDtypeStruct((lp, D_MODEL), BF16),
                   jax.ShapeDtypeStruct((lp, LANES), F32),
                   jax.ShapeDtypeStruct((lp, LANES), F32),
                   jax.ShapeDtypeStruct((1, N_Q_HEADS), F32)] + p_shapes,
        scratch_shapes=p_scr,
        compiler_params=pltpu.CompilerParams(dimension_semantics=("arbitrary",)),
    )(proj, proj, proj, sinks, dmix, *(plan.ins if plan is not None else []))
    return res[:4], res[4:]


_ZW = D_SSM
_XBC_W = D_SSM + 2 * SSD_GROUPS * SSD_N
_DT_COL = (_ZW + _XBC_W) // LANES
EVEN_IN = 3 * D_MODEL + 2 * LANES
ODD_IN = _ZW + _XBC_W + SSD_HEADS
ODD_IN_PAD = _ZW + _XBC_W + LANES


def _ssm_convprep_fwd(proj, cw, cb, dt_bias, name):
    lp = proj.shape[0]
    kk = cw.shape[0]
    tm, wc = _conv_tiles(lp, _XBC_W)
    offb = _ZW // wc
    nj = _XBC_W // wc
    hb = tm // SUBLANES

    def body(x_ref, xp_ref, dtr_ref, w_ref, b_ref, bias_ref, xc_ref, act_ref, dt_ref):
        i, j = pl.program_id(0), pl.program_id(1)
        real = _real_rows(i, tm)
        xv = x_ref[...]
        xx = jnp.concatenate([jnp.where(i > 0, xp_ref[...], 0.0), xv], axis=0)
        acc = b_ref[...] + w_ref[kk - 1:kk, :] * xv
        for m in range(1, kk):
            acc = acc + w_ref[kk - 1 - m:kk - m, :] * pltpu.roll(xx, m, 0)[SUBLANES:, :]
        xc_ref[...] = acc
        act, _ = _silu_and_grad(acc)
        act_ref[...] = jnp.where(real, act, 0.0)

        @pl.when(j == 0)
        def _():
            dt_ref[...] = jnp.where(real, _softplus(dtr_ref[...] + bias_ref[...]), 0.0)

    return pl.pallas_call(
        body, name=name, grid=(lp // tm, nj),
        in_specs=[pl.BlockSpec((tm, wc), lambda i, j: (i, offb + j)),
                  pl.BlockSpec((SUBLANES, wc), lambda i, j: (jnp.maximum(i * hb - 1, 0), offb + j)),
                  pl.BlockSpec((tm, LANES), lambda i, j: (i, _DT_COL)),
                  pl.BlockSpec((kk, wc), lambda i, j: (0, j)),
                  pl.BlockSpec((1, wc), lambda i, j: (0, j)),
                  pl.BlockSpec((1, LANES), lambda i, j: (0, 0))],
        out_specs=[pl.BlockSpec((tm, wc), lambda i, j: (i, j)),
                   pl.BlockSpec((tm, wc), lambda i, j: (i, j)),
                   pl.BlockSpec((tm, LANES), lambda i, j: (i, 0))],
        out_shape=[jax.ShapeDtypeStruct((lp, _XBC_W), F32), jax.ShapeDtypeStruct((lp, _XBC_W), F32),
                   jax.ShapeDtypeStruct((lp, LANES), F32)],
        compiler_params=pltpu.CompilerParams(dimension_semantics=("parallel", "arbitrary")),
    )(proj, proj, proj, cw, cb, dt_bias)


def _ssm_prep_bwd(xc, proj, dt_bias, dxs, dxskip, db, dc, ddt, name):
    lp = xc.shape[0]
    tm = BLOCK

    def body(r, first, xc_ref, dtr_ref, dxs_ref, dsk_ref, db_ref, dc_ref, ddt_ref, b_ref,
             dxc_ref, ddtr_ref, dbias_ref):
        real = _real_rows(r, tm)
        _, ds = _silu_and_grad(xc_ref[...])
        up = lambda ref: ref[...].astype(F32)
        dxc_ref[:, :D_SSM] = jnp.where(
            real, (up(dxs_ref) + up(dsk_ref)) * ds[:, :D_SSM], 0.0).astype(dxc_ref.dtype)
        dxc_ref[:, D_SSM:D_SSM + 1024] = jnp.where(
            real, up(db_ref) * ds[:, D_SSM:D_SSM + 1024], 0.0).astype(dxc_ref.dtype)
        dxc_ref[:, D_SSM + 1024:] = jnp.where(
            real, up(dc_ref) * ds[:, D_SSM + 1024:], 0.0).astype(dxc_ref.dtype)
        dd = jnp.where(real, ddt_ref[...] * _sigmoid(dtr_ref[...] + b_ref[...]), 0.0)
        ddtr_ref[...] = dd.astype(ddtr_ref.dtype)
        _acc_add(first, dbias_ref, jnp.sum(dd, axis=0, keepdims=True))

    return _rowcall(name, body, lp, tm,
                    rows=[(xc, _XBC_W, 0), (proj, LANES, _DT_COL), (dxs, D_SSM, 0), (dxskip, D_SSM, 0),
                          (db, 1024, 0), (dc, 1024, 0), (ddt, LANES, 0)],
                    vecs=[dt_bias], outs=[(_XBC_W, BF16), (LANES, BF16)], accs=[((1, LANES), F32)])


def _ssd_common(dt, alog):
    a = -jnp.exp(alog)
    cs = _cumsum_rows(dt * a, BLOCK)
    cst = cs.T
    cl = cs[BLOCK - 1:BLOCK, :]
    tril = (lax.broadcasted_iota(jnp.int32, (BLOCK, BLOCK), 0)
            >= lax.broadcasted_iota(jnp.int32, (BLOCK, BLOCK), 1))
    return a, cs, cst, cl, jnp.exp(cs), jnp.exp(cl - cs), jnp.exp(cl), tril


def _head_cols(ecl, g):
    lane = lax.broadcasted_iota(jnp.int32, (1, SSD_HPG * SSD_P), 1)
    e = [ecl[:, SSD_HPG * g + hh:SSD_HPG * g + hh + 1] for hh in range(SSD_HPG)]
    return jnp.where(lane < SSD_P, e[0], jnp.where(lane < 2 * SSD_P, e[1],
                                                   jnp.where(lane < 3 * SSD_P, e[2], e[3])))


def _ssd_fwd(xbc, dt, alog, name, plan=None):
    lp = xbc.shape[0]
    nc = lp // BLOCK
    gw = SSD_HPG * SSD_P
    p_in, p_shapes, p_out, p_scr = _plan_parts(plan)

    def body(*refs):
        xs_ref, b_ref, c_ref, dt_ref, alog_ref = refs[:5]
        cins = refs[5:5 + len(p_in)]
        y_ref, so_ref = refs[5 + len(p_in):7 + len(p_in)]
        couts = refs[7 + len(p_in):7 + len(p_in) + len(p_out)]
        st, fx = refs[7 + len(p_in) + len(p_out):9 + len(p_in) + len(p_out)]
        sems = refs[9 + len(p_in) + len(p_out):]
        n = pl.program_id(0)

        @pl.when(n == 0)
        def _():
            st[...] = jnp.zeros_like(st)
            if plan is not None:
                plan.start(cins, couts, sems)

        dtv = dt_ref[...]
        _, cs, cst, cl, e, f, ecl, tril = _ssd_common(dtv, alog_ref[...])
        pre = []
        for g in range(SSD_GROUPS):
            bg = b_ref[:, g * SSD_N:(g + 1) * SSD_N].astype(BF16)
            cg = c_ref[:, g * SSD_N:(g + 1) * SSD_N].astype(BF16)
            stg = st[g]
            so_ref[0, g] = stg
            pre.append((bg, stg, _dot_nt(cg, bg), _dot(cg, stg.astype(BF16))))
        for g in range(SSD_GROUPS):
            bg, stg, gm, yoff = pre[g]
            heads = [SSD_HPG * g + hh for hh in range(SSD_HPG)]
            cols = lambda v: jnp.stack([v[:, h:h + 1] for h in heads])
            x4 = jnp.stack([xs_ref[:, h * SSD_P:(h + 1) * SSD_P] for h in heads])
            csr = jnp.stack([cst[h:h + 1, :] for h in heads])
            m = gm[None] * jnp.exp(jnp.where(tril[None], cols(cs) - csr, NEG))
            xdt = x4 * cols(dtv)
            yoff4 = jnp.stack([yoff[:, hh * SSD_P:(hh + 1) * SSD_P] for hh in range(SSD_HPG)])
            y4 = (jnp.einsum("hls,hsp->hlp", m.astype(BF16), xdt.astype(BF16), preferred_element_type=F32)
                  + cols(e) * yoff4)
            fx4 = cols(f) * xdt
            for hh, h in enumerate(heads):
                y_ref[:, h * SSD_P:(h + 1) * SSD_P] = y4[hh]
                fx[:, hh * SSD_P:(hh + 1) * SSD_P] = fx4[hh]
            st[g] = stg * _head_cols(ecl, g) + _dot_tn(bg, fx[...].astype(BF16))
        if plan is not None:
            @pl.when(n == nc - 1)
            def _():
                plan.wait(cins, couts, sems)

    res = pl.pallas_call(
        body, name=name, grid=(nc,),
        in_specs=[pl.BlockSpec((BLOCK, D_SSM), lambda n: (n, 0)),
                  pl.BlockSpec((BLOCK, 1024), lambda n: (n, 2)),
                  pl.BlockSpec((BLOCK, 1024), lambda n: (n, 3)),
                  pl.BlockSpec((BLOCK, LANES), lambda n: (n, 0)),
                  pl.BlockSpec((1, LANES), lambda n: (0, 0))] + p_in,
        out_specs=[pl.BlockSpec((BLOCK, D_SSM), lambda n: (n, 0)),
                   pl.BlockSpec((1, SSD_GROUPS, SSD_N, gw), lambda n: (n, 0, 0, 0))] + p_out,
        out_shape=[jax.ShapeDtypeStruct((lp, D_SSM), F32),
                   jax.ShapeDtypeStruct((nc, SSD_GROUPS, SSD_N, gw), F32)] + p_shapes,
        scratch_shapes=[pltpu.VMEM((SSD_GROUPS, SSD_N, gw), F32), pltpu.VMEM((BLOCK, gw), F32)] + p_scr,
        compiler_params=pltpu.CompilerParams(dimension_semantics=("arbitrary",)),
    )(xbc, xbc, xbc, dt, alog, *(plan.ins if plan is not None else []))
    return res[:2], res[2:]


def _ssd_bwd(xbc, dt, alog, states, dy, name, plan=None):
    lp = xbc.shape[0]
    nc = lp // BLOCK
    gw = SSD_HPG * SSD_P
    p_in, p_shapes, p_out, p_scr = _plan_parts(plan)

    def body(*refs):
        xs_ref, b_ref, c_ref, dt_ref, alog_ref, dy_ref, st_ref = refs[:7]
        cins = refs[7:7 + len(p_in)]
        dxs_ref, db_ref, dc_ref, ddt_ref, dalog_ref = refs[7 + len(p_in):12 + len(p_in)]
        couts = refs[12 + len(p_in):12 + len(p_in) + len(p_out)]
        dst, edy, fx = refs[12 + len(p_in) + len(p_out):15 + len(p_in) + len(p_out)]
        sems = refs[15 + len(p_in) + len(p_out):]
        i = pl.program_id(0)

        @pl.when(i == 0)
        def _():
            dst[...] = jnp.zeros_like(dst)
            dalog_ref[...] = jnp.zeros_like(dalog_ref)
            if plan is not None:
                plan.start(cins, couts, sems)

        dtv = dt_ref[...]
        a, cs, cst, cl, e, f, ecl, tril = _ssd_common(dtv, alog_ref[...])
        lane = lax.broadcasted_iota(jnp.int32, (1, LANES), 1)
        sub = _row_iota(BLOCK)
        triu = (lax.broadcasted_iota(jnp.int32, (BLOCK, BLOCK), 1)
                >= lax.broadcasted_iota(jnp.int32, (BLOCK, BLOCK), 0))
        dcs = jnp.zeros((BLOCK, LANES), F32)
        dcst = jnp.zeros((LANES, BLOCK), F32)
        dcl = jnp.zeros((1, LANES), F32)
        ddtx = jnp.zeros((BLOCK, LANES), F32)
        pre = []
        for g in range(SSD_GROUPS):
            bg = b_ref[:, g * SSD_N:(g + 1) * SSD_N].astype(BF16)
            cg = c_ref[:, g * SSD_N:(g + 1) * SSD_N].astype(BF16)
            stb = st_ref[0, g].astype(BF16)
            dsob = dst[g].astype(BF16)
            pre.append((bg, cg, stb, dsob, _dot_nt(cg, bg), _dot_nt(bg, cg), _dot(cg, stb), _dot(bg, dsob)))
        for g in range(SSD_GROUPS):
            bg, cg, stb, dsob, gm, gmt, yraw, dfx = pre[g]
            dso = dst[g]
            prodsum = jnp.sum(dso * st_ref[0, g], axis=0, keepdims=True)
            heads = [SSD_HPG * g + hh for hh in range(SSD_HPG)]
            cols = lambda v: jnp.stack([v[:, h:h + 1] for h in heads])
            parts = lambda v: jnp.stack([v[:, hh * SSD_P:(hh + 1) * SSD_P] for hh in range(SSD_HPG)])
            x4 = jnp.stack([xs_ref[:, h * SSD_P:(h + 1) * SSD_P] for h in heads])
            dy4 = jnp.stack([dy_ref[:, h * SSD_P:(h + 1) * SSD_P] for h in heads])
            csc, dtc, ec, fc = cols(cs), cols(dtv), cols(e), cols(f)
            csr = jnp.stack([cst[h:h + 1, :] for h in heads])
            seg = csc - csr
            lam = jnp.exp(jnp.where(tril[None], seg, NEG))
            lamt = jnp.exp(jnp.where(triu[None], -seg, NEG))
            mt = gmt[None] * lamt
            xdt = x4 * dtc
            dyb = dy4.astype(BF16)
            dm = jnp.einsum("hlp,hsp->hls", dyb, xdt.astype(BF16), preferred_element_type=F32)
            dfx4 = parts(dfx)
            dxdt = jnp.einsum("hsl,hlp->hsp", mt.astype(BF16), dyb, preferred_element_type=F32) + fc * dfx4
            dml = dm * lam
            w = dml * gm[None]
            dgm = jnp.sum(dml, axis=0)
            dff = jnp.sum(dfx4 * xdt, axis=2, keepdims=True) * fc
            colv = (jnp.sum(w, axis=2, keepdims=True)
                    + jnp.sum(dy4 * parts(yraw), axis=2, keepdims=True) * ec - dff)
            roww = jnp.sum(w, axis=1, keepdims=True)
            ddtc = jnp.sum(dxdt * x4, axis=2, keepdims=True)
            dffs = jnp.sum(dff, axis=1, keepdims=True)
            dxs4 = dxdt * dtc
            edy4 = ec * dy4
            fx4 = fc * xdt
            for hh, h in enumerate(heads):
                ls = slice(hh * SSD_P, (hh + 1) * SSD_P)
                onl = (lane == h).astype(F32)
                dcs = dcs + colv[hh] * onl
                dcst = dcst - (sub == h).astype(F32) * roww[hh]
                dcl = dcl + (dffs[hh] + ecl[:, h:h + 1] * jnp.sum(prodsum[:, ls], axis=1, keepdims=True)) * onl
                ddtx = ddtx + ddtc[hh] * onl
                dxs_ref[:, h * SSD_P:(h + 1) * SSD_P] = dxs4[hh].astype(dxs_ref.dtype)
                edy[:, ls] = edy4[hh]
                fx[:, ls] = fx4[hh]
            edyb = edy[...].astype(BF16)
            fxb = fx[...].astype(BF16)
            dgb = dgm.astype(BF16)
            dc_ref[:, g * SSD_N:(g + 1) * SSD_N] = (_dot_nt(edyb, stb) + _dot(dgb, bg)).astype(dc_ref.dtype)
            db_ref[:, g * SSD_N:(g + 1) * SSD_N] = (_dot_nt(fxb, dsob) + _dot_tn(dgb, cg)).astype(db_ref.dtype)
            dst[g] = dso * _head_cols(ecl, g) + _dot_tn(cg, edyb)
        dcs = dcs + dcst.T + jnp.where(sub == BLOCK - 1, dcl, 0.0)
        dda = _rev_cumsum_rows(dcs, BLOCK)
        ddt_ref[...] = ddtx + dda * a
        dalog_ref[...] += jnp.sum(dda * dtv, axis=0, keepdims=True) * a
        if plan is not None:
            @pl.when(i == nc - 1)
            def _():
                plan.wait(cins, couts, sems)

    rev = lambda i: nc - 1 - i
    res = pl.pallas_call(
        body, name=name, grid=(nc,),
        in_specs=[pl.BlockSpec((BLOCK, D_SSM), lambda i: (rev(i), 0)),
                  pl.BlockSpec((BLOCK, 1024), lambda i: (rev(i), 2)),
                  pl.BlockSpec((BLOCK, 1024), lambda i: (rev(i), 3)),
                  pl.BlockSpec((BLOCK, LANES), lambda i: (rev(i), 0)),
                  pl.BlockSpec((1, LANES), lambda i: (0, 0)),
                  pl.BlockSpec((BLOCK, D_SSM), lambda i: (rev(i), 0)),
                  pl.BlockSpec((1, SSD_GROUPS, SSD_N, gw), lambda i: (rev(i), 0, 0, 0))] + p_in,
        out_specs=[pl.BlockSpec((BLOCK, D_SSM), lambda i: (rev(i), 0)),
                   pl.BlockSpec((BLOCK, 1024), lambda i: (rev(i), 0)),
                   pl.BlockSpec((BLOCK, 1024), lambda i: (rev(i), 0)),
                   pl.BlockSpec((BLOCK, LANES), lambda i: (rev(i), 0)),
                   pl.BlockSpec((1, LANES), lambda i: (0, 0))] + p_out,
        out_shape=[jax.ShapeDtypeStruct((lp, D_SSM), BF16),
                   jax.ShapeDtypeStruct((lp, 1024), BF16),
                   jax.ShapeDtypeStruct((lp, 1024), BF16),
                   jax.ShapeDtypeStruct((lp, LANES), F32),
                   jax.ShapeDtypeStruct((1, LANES), F32)] + p_shapes,
        scratch_shapes=[pltpu.VMEM((SSD_GROUPS, SSD_N, gw), F32),
                        pltpu.VMEM((BLOCK, gw), F32), pltpu.VMEM((BLOCK, gw), F32)] + p_scr,
        compiler_params=pltpu.CompilerParams(dimension_semantics=("arbitrary",)),
    )(xbc, xbc, xbc, dt, alog, dy, states, *(plan.ins if plan is not None else []))
    return res[:5], res[5:]


_GN_GROUPS = 8
_GN_W = D_SSM // _GN_GROUPS


def _ssm_gate_out_postnorm(yssd, xbc, proj, dskip, gnorm, w_out, h, g, next_pre, name):
    lp, d = h.shape
    tm = _pick(lp, (320, 256, 128))

    def body(y_ref, x_ref, z_ref, d_ref, gn_ref, w_ref, h_ref, g_ref, np_ref, yn_ref, o_ref, hn_ref, u_ref):
        sz, _ = _silu_and_grad(z_ref[...])
        y2 = (y_ref[...] + d_ref[...] * x_ref[...]) * sz
        for k in range(_GN_GROUPS):
            sl = slice(k * _GN_W, (k + 1) * _GN_W)
            yk = y2[:, sl]
            rs = lax.rsqrt(jnp.mean(yk * yk, axis=-1, keepdims=True) + EPS)
            yn_ref[:, sl] = (yk * rs * gn_ref[:, sl]).astype(yn_ref.dtype)
        acc = _dot(yn_ref[...], w_ref[...])
        o_ref[...] = acc
        rs = lax.rsqrt(jnp.mean(acc * acc, axis=-1, keepdims=True) + EPS)
        hn = jnp.where(_real_rows(pl.program_id(0), tm), h_ref[...] + acc * rs * g_ref[...], 0.0)
        hn_ref[...] = hn
        rs2 = lax.rsqrt(jnp.mean(hn * hn, axis=-1, keepdims=True) + EPS)
        u_ref[...] = (hn * rs2 * np_ref[...]).astype(u_ref.dtype)

    wide = pl.BlockSpec((tm, D_SSM), lambda i: (i, 0))
    row = pl.BlockSpec((tm, d), lambda i: (i, 0))
    vec = lambda n: pl.BlockSpec((1, n), lambda i: (0, 0))
    return pl.pallas_call(
        body, name=name, grid=(lp // tm,),
        in_specs=[wide, wide, wide, vec(D_SSM), vec(D_SSM), pl.BlockSpec((D_SSM, d), lambda i: (0, 0)),
                  row, vec(d), vec(d)],
        out_specs=[wide, row, row, row],
        out_shape=[jax.ShapeDtypeStruct((lp, D_SSM), BF16), jax.ShapeDtypeStruct((lp, d), F32),
                   jax.ShapeDtypeStruct((lp, d), F32), jax.ShapeDtypeStruct((lp, d), BF16)],
        compiler_params=pltpu.CompilerParams(dimension_semantics=("parallel",),
                                             vmem_limit_bytes=_VMEM_LIMIT_WIDE),
    )(yssd, xbc, proj, dskip, gnorm, w_out, h, g, next_pre)


def _ssm_out_gate_bwd(do, w_out, yssd, xbc, proj, dskip, gnorm, name):
    lp, d = do.shape
    tm = _pick(lp, (640, 512, 256, 128))
    tk = D_SSM // 2

    def body(do_ref, w_ref, y_ref, x_ref, z_ref, d_ref, g_ref, dy_ref, dx_ref, dz_ref, dd_ref, dg_ref):
        first = pl.program_id(1) == 0
        dyn = _dot_nt(do_ref[...], w_ref[...])
        z = z_ref[...]
        sz, dsz = _silu_and_grad(z)
        xs = x_ref[...]
        y1 = y_ref[...] + d_ref[...] * xs
        y2 = y1 * sz
        for k in range(tk // _GN_W):
            sl = slice(k * _GN_W, (k + 1) * _GN_W)
            dx, dgt = _rms_bwd(y2[:, sl], g_ref[:, sl], dyn[:, sl])
            dy1 = dx * sz[:, sl]
            dy_ref[:, sl] = dy1.astype(dy_ref.dtype)
            dx_ref[:, sl] = (dy1 * d_ref[:, sl]).astype(dx_ref.dtype)
            dz_ref[:, sl] = (dx * y1[:, sl] * dsz[:, sl]).astype(dz_ref.dtype)

            @pl.when(first)
            def _():
                dd_ref[:, sl] = jnp.zeros((1, _GN_W), F32)
                dg_ref[:, sl] = jnp.zeros((1, _GN_W), F32)

            dd_ref[:, sl] += jnp.sum(dy1 * xs[:, sl], axis=0, keepdims=True)
            dg_ref[:, sl] += jnp.sum(dgt, axis=0, keepdims=True)

    tile = pl.BlockSpec((tm, tk), lambda j, i: (i, j))
    vec = pl.BlockSpec((1, tk), lambda j, i: (0, j))
    return pl.pallas_call(
        body, name=name, grid=(D_SSM // tk, lp // tm),
        in_specs=[pl.BlockSpec((tm, d), lambda j, i: (i, 0)),
                  pl.BlockSpec((tk, d), lambda j, i: (j, 0)), tile, tile, tile, vec, vec],
        out_specs=[tile, tile, tile, vec, vec],
        out_shape=[jax.ShapeDtypeStruct((lp, D_SSM), BF16)] * 3 + [jax.ShapeDtypeStruct((1, D_SSM), F32)] * 2,
        compiler_params=pltpu.CompilerParams(dimension_semantics=("parallel", "arbitrary"),
                                             vmem_limit_bytes=_VMEM_LIMIT_WIDE),
    )(do, w_out, yssd, xbc, proj, dskip, gnorm)


def _shape2d(shape):
    n = math.prod(shape)
    if len(shape) == 2:
        return tuple(shape)
    return (n // LANES, LANES) if n % LANES == 0 else (1, n)


def _adamw_many(ws, gs, ms, vs, name):
    n = len(ws)
    c1 = 1.0 / (1.0 - ADAM_B1 ** ADAM_STEP)
    c2 = 1.0 / (1.0 - ADAM_B2 ** ADAM_STEP)

    def body(*refs):
        for i in range(n):
            w_ref, g_ref, m_ref, v_ref = (refs[j * n + i] for j in range(4))
            d_ref, nm_ref, nv_ref = (refs[(4 + j) * n + i] for j in range(3))
            gv = g_ref[...]
            nm = ADAM_B1 * m_ref[...] + (1.0 - ADAM_B1) * gv
            nv = ADAM_B2 * v_ref[...] + (1.0 - ADAM_B2) * (gv * gv)
            nm_ref[...] = nm
            nv_ref[...] = nv
            d_ref[...] = -ADAM_LR * ((nm * c1) / (jnp.sqrt(nv * c2) + ADAM_EPS) + ADAM_WD * w_ref[...])

    vm = pl.BlockSpec(memory_space=pltpu.VMEM)
    return pl.pallas_call(
        body, name=name, in_specs=[vm] * (4 * n), out_specs=[vm] * (3 * n),
        out_shape=[jax.ShapeDtypeStruct(w.shape, F32) for w in ws] * 3,
    )(*ws, *gs, *ms, *vs)


def _place():
    return lax.axis_index("x"), lax.axis_index("y"), lax.axis_index("c")


def _other_chips(x, y):
    return [(1 - x, y), (x, 1 - y), (1 - x, 1 - y)]


_ANY = pl.BlockSpec(memory_space=pl.ANY)


class _Plan:
    def __init__(self, ins, out_shapes, n_remote, n_local, issue):
        self.ins = list(ins)
        self.out_shapes = list(out_shapes)
        self.issue = issue
        self.scratch = [pltpu.SemaphoreType.DMA((max(n_remote, 1),)),
                        pltpu.SemaphoreType.DMA((max(n_remote, 1),)),
                        pltpu.SemaphoreType.DMA((max(n_local, 1),))]

    def start(self, ins, outs, sems):
        sends, _, locs = self.issue(ins, outs, *sems)
        for cp in locs + sends:
            cp.start()

    def wait(self, ins, outs, sems):
        sends, recvs, locs = self.issue(ins, outs, *sems)
        for make in recvs:
            make().wait_recv()
        for cp in sends:
            cp.wait_send()
        for cp in locs:
            cp.wait()


def _plan_parts(plan):
    if plan is None:
        return [], [], [], []
    return ([_ANY] * len(plan.ins), plan.out_shapes, [_ANY] * len(plan.out_shapes), plan.scratch)


def _run_plan(plan, name):
    n_in, n_out = len(plan.ins), len(plan.out_shapes)

    def body(*refs):
        ins, outs, sems = refs[:n_in], refs[n_in:n_in + n_out], refs[n_in + n_out:]
        plan.start(ins, outs, sems)
        plan.wait(ins, outs, sems)

    return pl.pallas_call(
        body, name=name, in_specs=[_ANY] * n_in, out_specs=[_ANY] * n_out,
        out_shape=plan.out_shapes, scratch_shapes=plan.scratch,
    )(*plan.ins)


def _gather_plan(shards):
    n = len(shards)

    def issue(ins, outs, send_sems, recv_sems, local_sems):
        x, y, c = _place()
        me = 2 * x + y
        sends, recvs, locs = [], [], []
        for p in range(n):
            locs.append(pltpu.make_async_copy(ins[p], outs[p].at[me], local_sems.at[p]))
            for k, (px, py) in enumerate(_other_chips(x, y)):
                sems = dict(send_sem=send_sems.at[3 * p + k], recv_sem=recv_sems.at[3 * p + k],
                            device_id=(px, py, c), device_id_type=MESH)
                sends.append(pltpu.make_async_remote_copy(src_ref=ins[p], dst_ref=outs[p].at[me], **sems))
                recvs.append(functools.partial(pltpu.make_async_remote_copy, src_ref=ins[p],
                                               dst_ref=outs[p].at[2 * px + py], **sems))
        return sends, recvs, locs

    return _Plan(shards, [jax.ShapeDtypeStruct((N_CHIPS,) + s.shape, s.dtype) for s in shards], 3 * n, n, issue)


_REL7 = [(fx, fy, fc) for fx in (0, 1) for fy in (0, 1) for fc in (0, 1)][1:]


def _scatter8_plan(gs):
    n = len(gs)

    def issue(ins, outs, send_sems, recv_sems, local_sems):
        x, y, c = _place()
        sends = []
        for p in range(n):
            hr = gs[p].shape[1] // 2
            for k, (fx, fy, fc) in enumerate(_REL7):
                tx, ty, tc = x ^ fx, y ^ fy, c ^ fc
                src = ins[p].at[2 * tx + ty, pl.ds(pl.multiple_of(tc * hr, SUBLANES), hr), :]
                sends.append(pltpu.make_async_remote_copy(
                    src_ref=src, dst_ref=outs[p].at[k],
                    send_sem=send_sems.at[7 * p + k], recv_sem=recv_sems.at[7 * p + k],
                    device_id=(tx, ty, tc), device_id_type=MESH))
        return sends, [functools.partial(lambda cp: cp, cp) for cp in sends], []

    shapes = [jax.ShapeDtypeStruct((7, g.shape[1] // 2, g.shape[2]), g.dtype) for g in gs]
    return _Plan(gs, shapes, 7 * n, 0, issue)


def _sibling_plan(ts):
    n = len(ts)

    def issue(ins, outs, send_sems, recv_sems, local_sems):
        x, y, c = _place()
        sends = [pltpu.make_async_remote_copy(
            src_ref=ins[p], dst_ref=outs[p], send_sem=send_sems.at[p], recv_sem=recv_sems.at[p],
            device_id=(x, y, 1 - c), device_id_type=MESH) for p in range(n)]
        return sends, [functools.partial(lambda cp: cp, cp) for cp in sends], []

    return _Plan(ts, [jax.ShapeDtypeStruct(t.shape, t.dtype) for t in ts], n, 0, issue)


def _add8(g, recv, chip, core, name):
    s, r, n = g.shape
    hr = r // 2
    th = hr // 2 if (hr // 2) % SUBLANES == 0 else hr
    nt = hr // th

    def body(chip_ref, core_ref, g_ref, r_ref, o_ref):
        acc = g_ref[0].astype(F32)
        for k in range(7):
            acc = acc + r_ref[k].astype(F32)
        o_ref[...] = acc

    return pl.pallas_call(
        body, name=name,
        grid_spec=pltpu.PrefetchScalarGridSpec(
            num_scalar_prefetch=2, grid=(nt,),
            in_specs=[pl.BlockSpec((1, th, n), lambda i, ch, co: (ch[0], co[0] * nt + i, 0)),
                      pl.BlockSpec((7, th, n), lambda i, ch, co: (0, i, 0))],
            out_specs=pl.BlockSpec((th, n), lambda i, ch, co: (i, 0))),
        out_shape=jax.ShapeDtypeStruct((hr, n), F32),
        compiler_params=pltpu.CompilerParams(dimension_semantics=("parallel",)),
    )(chip, core, g, recv)


def _adamw_halves(w, own, other, m, v, core, name):
    r, n = w.shape
    hr = r // 2
    th = hr // 2 if (hr // 2) % SUBLANES == 0 else hr
    tph = hr // th
    c1 = 1.0 / (1.0 - ADAM_B1 ** ADAM_STEP)
    c2 = 1.0 / (1.0 - ADAM_B2 ** ADAM_STEP)

    def body(core_ref, w_ref, a_ref, b_ref, m_ref, v_ref, g_ref, d_ref, nm_ref, nv_ref):
        half = pl.program_id(0) // tph
        gv = jnp.where(half == core_ref[0], a_ref[...], b_ref[...])
        nm = ADAM_B1 * m_ref[...] + (1.0 - ADAM_B1) * gv
        nv = ADAM_B2 * v_ref[...] + (1.0 - ADAM_B2) * (gv * gv)
        g_ref[...] = gv
        nm_ref[...] = nm
        nv_ref[...] = nv
        d_ref[...] = -ADAM_LR * ((nm * c1) / (jnp.sqrt(nv * c2) + ADAM_EPS) + ADAM_WD * w_ref[...])

    full = pl.BlockSpec((th, n), lambda i, co: (i, 0))
    part = pl.BlockSpec((th, n), lambda i, co: (i % tph, 0))
    return pl.pallas_call(
        body, name=name,
        grid_spec=pltpu.PrefetchScalarGridSpec(
            num_scalar_prefetch=1, grid=(2 * tph,),
            in_specs=[full, part, part, full, full], out_specs=[full] * 4),
        out_shape=[jax.ShapeDtypeStruct((r, n), F32)] * 4,
        compiler_params=pltpu.CompilerParams(dimension_semantics=("parallel",)),
    )(core, w, own, other, m, v)


def _allreduce_small(pack, name):
    r, l = pack.shape
    hr = r // 2
    assert hr % SUBLANES == 0

    def body(p_ref, o_ref, sib, chips, send_sems, recv_sems):
        x, y, c = _place()
        chip = 2 * x + y
        sibling = dict(device_id=(x, y, 1 - c), device_id_type=MESH)
        mine = pl.ds(pl.multiple_of(c * hr, SUBLANES), hr)
        other = pl.ds(pl.multiple_of((1 - c) * hr, SUBLANES), hr)
        a = pltpu.make_async_remote_copy(src_ref=p_ref.at[other], dst_ref=sib, send_sem=send_sems.at[0],
                                         recv_sem=recv_sems.at[0], **sibling)
        a.start()
        a.wait()
        own, got = p_ref[mine, :], sib[...]
        chips[chip] = jnp.where(c == 0, own, got) + jnp.where(c == 0, got, own)
        sends = []
        for k, (px, py) in enumerate(_other_chips(x, y)):
            cp = pltpu.make_async_remote_copy(
                src_ref=chips.at[chip], dst_ref=chips.at[chip], send_sem=send_sems.at[1 + k],
                recv_sem=recv_sems.at[1 + k], device_id=(px, py, c), device_id_type=MESH)
            cp.start()
            sends.append(cp)
        for k, (px, py) in enumerate(_other_chips(x, y)):
            pltpu.make_async_remote_copy(
                src_ref=chips.at[chip], dst_ref=chips.at[2 * px + py], send_sem=send_sems.at[1 + k],
                recv_sem=recv_sems.at[1 + k], device_id=(px, py, c), device_id_type=MESH).wait_recv()
        for cp in sends:
            cp.wait_send()
        o_ref[mine, :] = ((chips[0] + chips[1]) + chips[2]) + chips[3]
        fin = pltpu.make_async_remote_copy(src_ref=o_ref.at[mine], dst_ref=o_ref.at[mine],
                                           send_sem=send_sems.at[4], recv_sem=recv_sems.at[4], **sibling)
        fin.start()
        pltpu.make_async_remote_copy(src_ref=o_ref.at[mine], dst_ref=o_ref.at[other],
                                     send_sem=send_sems.at[4], recv_sem=recv_sems.at[4], **sibling).wait_recv()
        fin.wait_send()

    vm = pl.BlockSpec(memory_space=pltpu.VMEM)
    return pl.pallas_call(
        body, name=name, in_specs=[vm], out_specs=vm,
        out_shape=jax.ShapeDtypeStruct((r, l), F32),
        scratch_shapes=[pltpu.VMEM((hr, l), F32), pltpu.VMEM((N_CHIPS, hr, l), F32),
                        pltpu.SemaphoreType.DMA((5,)), pltpu.SemaphoreType.DMA((5,))],
    )(pack)


def _flat_rows(a, mult=SUBLANES * LANES):
    f = a.reshape(-1)
    padn = (-f.shape[0]) % mult
    if padn:
        f = jnp.concatenate([f, jnp.zeros((padn,), f.dtype)])
    return f


def _pack(arrs, mult=SUBLANES * LANES, total_mult=None):
    flat = [_flat_rows(a, mult) for a in arrs]
    sizes = [f.shape[0] for f in flat]
    if total_mult is not None:
        padn = (-sum(sizes)) % total_mult
        if padn:
            flat.append(jnp.zeros((padn,), flat[0].dtype))
    return jnp.concatenate(flat).reshape(-1, LANES), sizes


def _unpack(pack, shapes, sizes, lead=()):
    flat = pack.reshape(lead + (-1,))
    out, off = [], 0
    for shp, sz in zip(shapes, sizes):
        n = math.prod(shp)
        out.append(flat[..., off:off + n].reshape(lead + tuple(shp)))
        off += sz
    return out


def _cols_from_shards(g):
    s, k, n = g.shape
    return jnp.transpose(g, (1, 0, 2)).reshape(k, s * n)


def _ffn_fwd(h, u, post, w_up, cw, cb, w_down, tag, next_pre=None, plan=None, loss_tgt=None):
    hp = _mm_nn_sh(u, w_up, 2 * D_FF, f"{tag}_up")
    if plan is None:
        (act, hg, hu), pouts = _ffn_convact_fwd(hp, cw, cb, f"{tag}_convact"), ()
    else:
        (act, hg, hu), pouts = _ffn_convact_fwd(hp, cw, cb, f"{tag}_convact", plan)
    if loss_tgt is not None:
        o, hn, un = _mm_postnorm_loss(act, w_down, h, post, loss_tgt, f"{tag}_down_postnorm_loss")
    else:
        res = _mm_postnorm_res(act, w_down, h, post, f"{tag}_down_postnorm", next_pre)
        o, hn, un = res if next_pre is not None else (*res, None)
    return hn, un, (h, u, hp, hg, hu, act, o), pouts


def _ffn_bwd(dh, saved, pre, post, w_up, cw, w_down, tag):
    h, u, hp, hg, hu, act, o = saved
    do, dpost = _postnorm_bwd(o, post, dh, f"{tag}_postnorm_bwd")
    dhg, dhu = _ffn_down_act_bwd(do, w_down, hg, hu, f"{tag}_down_dx_act_bwd")
    dw_down = _mm_tn(act, do, f"{tag}_down_dw")
    dxg, dwg, dbg = _conv_bwd(hp, 0, D_FF, dhg, cw, f"{tag}_conv_bwd_gate")
    dxu, dwu, dbu = _conv_bwd(hp, D_FF, D_FF, dhu, cw, f"{tag}_conv_bwd_up", w_col_off=D_FF)
    dhp = (dxg, dxu)
    dcw = jnp.concatenate([dwg, dwu], axis=1)
    dcb = jnp.concatenate([dbg, dbu], axis=1)
    du = _mm_nt_sh(dhp, w_up, f"{tag}_up_dx")
    dw_up = _mm_tn_sh(u, dhp, w_up.shape[2], f"{tag}_up_dw")
    dhn, dpre = _prenorm_bwd(h, pre, du, dh, f"{tag}_prenorm_bwd")
    return dhn, dict(pre=dpre, post=dpost, w_up=dw_up, conv_w=dcw[:3], conv_b=dcb, w_down=dw_down)


class _Exchange:
    GATHER_IN_LRU = ("l0_w_out", "l0_ffn_w_down")
    GATHER_IN_ATTN = ("l0_ffn_w_up", "l1_w_out")
    GATHER_IN_FFN0 = ("l1_w_in",)
    GATHER_IN_SSD = ("l1_ffn_w_up", "l1_ffn_w_down")
    AFTER_L1_OUT = ("l1_ffn_w_up", "l1_ffn_w_down", "l1_w_out")
    IN_LRU_BWD = ("l1_w_in",)
    AFTER_L0_OUT = ("l0_ffn_w_up", "l0_ffn_w_down", "l0_w_out")
    LAST = ("l0_w_in",)

    def __init__(self, late_shards):
        self.late = dict(late_shards)
        self.slabs = {}
        self.recv = {}

    def gather_plan(self, names):
        return _gather_plan([self.late[n] for n in names])

    def gathered(self, names, outs):
        return {n: (g if n in _BIG_COL else g.reshape(-1, g.shape[-1])) for n, g in zip(names, outs)}

    def scatter_plan(self, grads, names):
        for n in names:
            g = grads[n]
            self.slabs[n] = g if n in _BIG_COL else g.reshape(N_CHIPS, -1, g.shape[-1])
        return _scatter8_plan([self.slabs[n] for n in names])

    def scattered(self, names, outs):
        self.recv.update(zip(names, outs))


def _local_step(x, tgt, meta, P, ex=None):
    seq, d = x.shape
    lp = seq + BLOCK
    h0 = jnp.concatenate([jnp.zeros((PAD, d), F32), meta, x], axis=0)
    tgt_p = jnp.concatenate([jnp.zeros((BLOCK, d), F32), tgt], axis=0)

    u0 = _rmsnorm_fwd(h0, P["l0_mix_pre_norm"], "l0_mix_prenorm")
    proj0 = _mm_nn_sh(u0, P["l0_w_in"], EVEN_IN, "l0_in")
    xrc = _conv_fwd(proj0, D_MODEL, D_MODEL, P["l0_lru_conv_w"], P["l0_lru_conv_b"], "l0_lru_conv")
    lru_args = (P["l0_lru_w_a"], P["l0_lru_w_x"], P["l0_lru_b_a"], P["l0_lru_b_x"], P["l0_lru_lambda"])
    if ex:
        (ya, hl), outs = _lru_fwd(proj0, xrc, *lru_args, "l0_lru", ex.gather_plan(ex.GATHER_IN_LRU))
        P = {**P, **ex.gathered(ex.GATHER_IN_LRU, outs)}
    else:
        ya, hl = _lru_fwd(proj0, xrc, *lru_args, "l0_lru")
    yb, outs = _attn_fwd(proj0, P["l0_attn_sinks"], "l0_attn",
                         ex.gather_plan(ex.GATHER_IN_ATTN) if ex else None)
    if ex:
        P = {**P, **ex.gathered(ex.GATHER_IN_ATTN, outs)}
    o0, h1, u1 = _mm_postnorm_res((ya, yb), P["l0_w_out"], h0, P["l0_mix_post_norm"], "l0_out_postnorm",
                                  P["l0_ffn_pre_norm"])
    h2, u2, ffn0, outs = _ffn_fwd(h1, u1, P["l0_ffn_post_norm"], P["l0_ffn_w_up"], P["l0_ffn_conv_w"],
                                  P["l0_ffn_conv_b"], P["l0_ffn_w_down"], "l0_ffn", P["l1_mix_pre_norm"],
                                  ex.gather_plan(ex.GATHER_IN_FFN0) if ex else None)
    if ex:
        P = {**P, **ex.gathered(ex.GATHER_IN_FFN0, outs)}
    proj1 = _mm_nn_sh(u2, P["l1_w_in"], ODD_IN_PAD, "l1_in")
    xc1, xbc, dt = _ssm_convprep_fwd(proj1, P["l1_ssm_conv_w"], P["l1_ssm_conv_b"], P["l1_dt_bias"],
                                     "l1_ssm_convprep")
    (yssd, states), outs = _ssd_fwd(xbc, dt, P["l1_a_log"], "l1_ssd",
                                    ex.gather_plan(ex.GATHER_IN_SSD) if ex else None)
    if ex:
        P = {**P, **ex.gathered(ex.GATHER_IN_SSD, outs)}
    yn, o1, h3, u3 = _ssm_gate_out_postnorm(
        yssd, xbc, proj1, P["l1_d_skip"], P["l1_gate_norm"], P["l1_w_out"], h2, P["l1_mix_post_norm"],
        P["l1_ffn_pre_norm"], "l1_gate_out_postnorm")
    dh4, loss_cols, ffn1, _ = _ffn_fwd(h3, u3, P["l1_ffn_post_norm"], P["l1_ffn_w_up"], P["l1_ffn_conv_w"],
                                       P["l1_ffn_conv_b"], P["l1_ffn_w_down"], "l1_ffn", loss_tgt=tgt_p)

    G = {}
    dh3, g = _ffn_bwd(dh4, ffn1, P["l1_ffn_pre_norm"], P["l1_ffn_post_norm"], P["l1_ffn_w_up"],
                      P["l1_ffn_conv_w"], P["l1_ffn_w_down"], "l1_ffn")
    for k, v in g.items():
        G["l1_ffn_" + (k + "_norm" if k in ("pre", "post") else k)] = v
    do1, G["l1_mix_post_norm"] = _postnorm_bwd(o1, P["l1_mix_post_norm"], dh3, "l1_mix_postnorm_bwd")
    G["l1_w_out"] = _mm_tn(yn, do1, "l1_out_dw")
    dyssd, dxskip, dz, dd_cols, G["l1_gate_norm"] = _ssm_out_gate_bwd(
        do1, P["l1_w_out"], yssd, xbc, proj1, P["l1_d_skip"], P["l1_gate_norm"], "l1_out_dx_gate_bwd")
    G["l1_d_skip"] = dd_cols.reshape(SSD_HEADS, SSD_P).sum(axis=1)
    (dxs, dbm, dcm, ddt, dalog), outs = _ssd_bwd(
        xbc, dt, P["l1_a_log"], states, dyssd, "l1_ssd_bwd",
        ex.scatter_plan(G, ex.AFTER_L1_OUT) if ex else None)
    if ex:
        ex.scattered(ex.AFTER_L1_OUT, outs)
    G["l1_a_log"] = dalog[0, :SSD_HEADS]
    dxc, ddtr, dbias = _ssm_prep_bwd(xc1, proj1, P["l1_dt_bias"], dxs, dxskip, dbm, dcm, ddt,
                                     "l1_ssm_prep_bwd")
    G["l1_dt_bias"] = dbias[0, :SSD_HEADS]
    dxbc, dcw, dcb = _conv_bwd(proj1, _ZW, _XBC_W, dxc, P["l1_ssm_conv_w"], "l1_ssm_conv_bwd")
    G["l1_ssm_conv_w"] = dcw[:4]
    G["l1_ssm_conv_b"] = dcb
    dproj1 = jnp.concatenate([dz, dxbc, ddtr], axis=1)
    du2 = _mm_nt_sh(dproj1, P["l1_w_in"], "l1_in_dx")
    G["l1_w_in"] = _mm_tn_sh(u2, dproj1, ODD_IN // N_CHIPS, "l1_in_dw")
    dh2, G["l1_mix_pre_norm"] = _prenorm_bwd(h2, P["l1_mix_pre_norm"], du2, dh3, "l1_mix_prenorm_bwd")
    dh1, g = _ffn_bwd(dh2, ffn0, P["l0_ffn_pre_norm"], P["l0_ffn_post_norm"], P["l0_ffn_w_up"],
                      P["l0_ffn_conv_w"], P["l0_ffn_w_down"], "l0_ffn")
    for k, v in g.items():
        G["l0_ffn_" + (k + "_norm" if k in ("pre", "post") else k)] = v
    do0, G["l0_mix_post_norm"] = _postnorm_bwd(o0, P["l0_mix_post_norm"], dh1, "l0_mix_postnorm_bwd")
    dmix = _mm_nt(do0, P["l0_w_out"], "l0_out_dx")
    G["l0_w_out"] = jnp.concatenate([_mm_tn(ya, do0, "l0_out_dw_lru"), _mm_tn(yb, do0, "l0_out_dw_attn")], axis=0)
    if ex:
        lru_out, outs = _lru_bwd(proj0, xrc, hl, dmix, *lru_args, "l0_lru_bwd",
                                 ex.scatter_plan(G, ex.IN_LRU_BWD))
        ex.scattered(ex.IN_LRU_BWD, outs)
    else:
        lru_out = _lru_bwd(proj0, xrc, hl, dmix, *lru_args, "l0_lru_bwd")
    (dgate, dxrc, G["l0_lru_w_a"], G["l0_lru_w_x"], G["l0_lru_b_a"], G["l0_lru_b_x"],
     G["l0_lru_lambda"]) = lru_out
    dxr, dcw, dcb = _conv_bwd(proj0, D_MODEL, D_MODEL, dxrc, P["l0_lru_conv_w"], "l0_lru_conv_bwd")
    G["l0_lru_conv_w"] = dcw[:4]
    G["l0_lru_conv_b"] = dcb
    (dq, dk, dv, G["l0_attn_sinks"]), outs = _attn_bwd(
        proj0, P["l0_attn_sinks"], dmix, "l0_attn_bwd",
        ex.scatter_plan(G, ex.AFTER_L0_OUT) if ex else None)
    if ex:
        ex.scattered(ex.AFTER_L0_OUT, outs)
    dproj0 = jnp.concatenate([dgate, dxr, dq, dk.astype(BF16), dv.astype(BF16)], axis=1)
    G["l0_w_in"] = _mm_tn_sh(u0, dproj0, EVEN_IN // N_CHIPS, "l0_in_dw")
    if ex:
        du0, outs = _mm_nt_sh(dproj0, P["l0_w_in"], "l0_in_dx", ex.scatter_plan(G, ex.LAST))
        ex.scattered(ex.LAST, outs)
    else:
        du0 = _mm_nt_sh(dproj0, P["l0_w_in"], "l0_in_dx")
    dh0, G["l0_mix_pre_norm"] = _prenorm_bwd(h0, P["l0_mix_pre_norm"], du0, dh1, "l0_mix_prenorm_bwd")
    return loss_cols, dh0[BLOCK:], dh0[PAD:BLOCK], G


_BIG_COL = ("l0_w_in", "l0_ffn_w_up", "l1_w_in", "l1_ffn_w_up")
_BIG = ("l0_w_in", "l0_w_out", "l0_ffn_w_up", "l0_ffn_w_down",
        "l1_w_in", "l1_w_out", "l1_ffn_w_up", "l1_ffn_w_down")
_SMALL_SHARDED = ("meta_tokens", "l0_lru_conv_w", "l0_ffn_conv_w", "l1_ssm_conv_w", "l1_ffn_conv_w")
_WEIGHTS = ("meta_tokens", "l0_mix_pre_norm", "l0_mix_post_norm", "l0_w_in", "l0_lru_conv_w",
            "l0_lru_conv_b", "l0_lru_w_a", "l0_lru_b_a", "l0_lru_w_x", "l0_lru_b_x", "l0_lru_lambda",
            "l0_attn_sinks", "l0_w_out", "l0_ffn_pre_norm", "l0_ffn_post_norm", "l0_ffn_w_up",
            "l0_ffn_conv_w", "l0_ffn_conv_b", "l0_ffn_w_down", "l1_mix_pre_norm", "l1_mix_post_norm",
            "l1_w_in", "l1_ssm_conv_w", "l1_ssm_conv_b", "l1_dt_bias", "l1_a_log", "l1_d_skip",
            "l1_gate_norm", "l1_w_out", "l1_ffn_pre_norm", "l1_ffn_post_norm", "l1_ffn_w_up",
            "l1_ffn_conv_w", "l1_ffn_conv_b", "l1_ffn_w_down")
_REPL = tuple(n for n in _WEIGHTS if n not in _BIG and n not in _SMALL_SHARDED)


def _pad_lanes(v, n=LANES):
    return jnp.concatenate([v, jnp.zeros((n - v.shape[0],), v.dtype)]).reshape(1, n)


def _step(x, tgt, W, M, V):
    cx, cy, cc = _place()
    chip = 2 * cx + cy

    small_pack, small_sizes = _pack([W[n] for n in _SMALL_SHARDED])
    first = _run_plan(_gather_plan([W["l0_w_in"].astype(BF16), small_pack]), "gather_first")
    small_full = _unpack(first[1], [W[n].shape for n in _SMALL_SHARDED], small_sizes, lead=(N_CHIPS,))
    ex = _Exchange({n: W[n].astype(BF16) for n in _BIG if n != "l0_w_in"})

    P = {"l0_w_in": first[0]}
    for n, g in zip(_SMALL_SHARDED, small_full):
        P[n] = _cols_from_shards(g)
    for n in _REPL:
        v = W[n]
        P[n] = v.reshape(1, -1) if v.ndim == 1 else v
    P["l0_lru_w_a"] = W["l0_lru_w_a"].astype(BF16)
    P["l0_lru_w_x"] = W["l0_lru_w_x"].astype(BF16)
    P["l1_dt_bias"] = _pad_lanes(W["l1_dt_bias"])
    P["l1_a_log"] = _pad_lanes(W["l1_a_log"])
    P["l1_d_skip"] = jnp.repeat(W["l1_d_skip"], SSD_P).reshape(1, D_SSM)
    meta = P.pop("meta_tokens")

    loss_cols, grad_x, grad_meta, G = _local_step(x, tgt, meta, P, ex)
    G["meta_tokens"] = grad_meta

    core_idx = cc.astype(jnp.int32).reshape(1)
    chip_idx = chip.astype(jnp.int32).reshape(1)
    own_half = [_add8(ex.slabs[n], ex.recv[n], chip_idx, core_idx, f"grad_sum_{n}") for n in _BIG]
    other_half = _run_plan(_sibling_plan(own_half), "grad_sibling_swap")
    small_names = list(_REPL) + list(_SMALL_SHARDED)
    small_list = [G[n] for n in small_names] + [loss_cols]
    spack, ssizes = _pack(small_list, total_mult=2 * SUBLANES * LANES)
    sred = _allreduce_small(spack, "small_allreduce")
    sfull = _unpack(sred, [a.shape for a in small_list], ssizes)
    loss = 0.5 / D_MODEL * jnp.sum(sfull[-1])
    small_grads = {}
    for n, g in zip(small_names, sfull[:-1]):
        if n in _SMALL_SHARDED:
            wcols = W[n].shape[1]
            g = lax.dynamic_slice_in_dim(g, chip * wcols, wcols, axis=1)
        small_grads[n] = g.reshape(W[n].shape)

    grads, delta, new_m, new_v = {}, {}, {}, {}
    for n, own, other in zip(_BIG, own_half, other_half):
        grads[n], delta[n], new_m[n], new_v[n] = _adamw_halves(
            W[n], own, other, M[n], V[n], core_idx, f"adamw_{n}")
    s_names = [n for n in _WEIGHTS if n not in _BIG]
    as2d = lambda a: a.reshape(_shape2d(a.shape))
    outs = _adamw_many([as2d(W[n]) for n in s_names], [as2d(small_grads[n]) for n in s_names],
                       [as2d(M[n]) for n in s_names], [as2d(V[n]) for n in s_names], "adamw_small")
    k = len(s_names)
    for i, n in enumerate(s_names):
        grads[n] = small_grads[n]
        delta[n], new_m[n], new_v[n] = (outs[j * k + i].reshape(W[n].shape) for j in range(3))
    return loss, grad_x, grads, delta, new_m, new_v


def kernel(x, meta_tokens, l0_mix_pre_norm, l0_mix_post_norm, l0_w_in, l0_lru_conv_w, l0_lru_conv_b, l0_lru_w_a, l0_lru_b_a, l0_lru_w_x, l0_lru_b_x, l0_lru_lambda, l0_attn_sinks, l0_w_out, l0_ffn_pre_norm, l0_ffn_post_norm, l0_ffn_w_up, l0_ffn_conv_w, l0_ffn_conv_b, l0_ffn_w_down, l1_mix_pre_norm, l1_mix_post_norm, l1_w_in, l1_ssm_conv_w, l1_ssm_conv_b, l1_dt_bias, l1_a_log, l1_d_skip, l1_gate_norm, l1_w_out, l1_ffn_pre_norm, l1_ffn_post_norm, l1_ffn_w_up, l1_ffn_conv_w, l1_ffn_conv_b, l1_ffn_w_down, loss_target, m_meta_tokens, m_l0_mix_pre_norm, m_l0_mix_post_norm, m_l0_w_in, m_l0_lru_conv_w, m_l0_lru_conv_b, m_l0_lru_w_a, m_l0_lru_b_a, m_l0_lru_w_x, m_l0_lru_b_x, m_l0_lru_lambda, m_l0_attn_sinks, m_l0_w_out, m_l0_ffn_pre_norm, m_l0_ffn_post_norm, m_l0_ffn_w_up, m_l0_ffn_conv_w, m_l0_ffn_conv_b, m_l0_ffn_w_down, m_l1_mix_pre_norm, m_l1_mix_post_norm, m_l1_w_in, m_l1_ssm_conv_w, m_l1_ssm_conv_b, m_l1_dt_bias, m_l1_a_log, m_l1_d_skip, m_l1_gate_norm, m_l1_w_out, m_l1_ffn_pre_norm, m_l1_ffn_post_norm, m_l1_ffn_w_up, m_l1_ffn_conv_w, m_l1_ffn_conv_b, m_l1_ffn_w_down, v_meta_tokens, v_l0_mix_pre_norm, v_l0_mix_post_norm, v_l0_w_in, v_l0_lru_conv_w, v_l0_lru_conv_b, v_l0_lru_w_a, v_l0_lru_b_a, v_l0_lru_w_x, v_l0_lru_b_x, v_l0_lru_lambda, v_l0_attn_sinks, v_l0_w_out, v_l0_ffn_pre_norm, v_l0_ffn_post_norm, v_l0_ffn_w_up, v_l0_ffn_conv_w, v_l0_ffn_conv_b, v_l0_ffn_w_down, v_l1_mix_pre_norm, v_l1_mix_post_norm, v_l1_w_in, v_l1_ssm_conv_w, v_l1_ssm_conv_b, v_l1_dt_bias, v_l1_a_log, v_l1_d_skip, v_l1_gate_norm, v_l1_w_out, v_l1_ffn_pre_norm, v_l1_ffn_post_norm, v_l1_ffn_w_up, v_l1_ffn_conv_w, v_l1_ffn_conv_b, v_l1_ffn_w_down):
    args = locals()
    W = {n: args[n] for n in _WEIGHTS}
    M = {n: args["m_" + n] for n in _WEIGHTS}
    V = {n: args["v_" + n] for n in _WEIGHTS}
    loss, grad_x, grads, delta, new_m, new_v = _step(x[0], loss_target[0], W, M, V)
    return (loss, grad_x[None], *[grads[n] for n in _WEIGHTS], *[delta[n] for n in _WEIGHTS],
            *[new_m[n] for n in _WEIGHTS], *[new_v[n] for n in _WEIGHTS])
```

```python
import functools
import math

import jax
import jax.numpy as jnp
from jax import lax
from jax.experimental import pallas as pl
from jax.experimental.pallas import tpu as pltpu

F32 = jnp.float32
BF16 = jnp.bfloat16

D_MODEL = 1024
N_META = 16
BLOCK = 128
PAD = BLOCK - N_META
EPS = 1e-6
LRU_BLOCKS = 8
LRU_BS = 128
LRU_C = 8.0
N_Q_HEADS = 16
N_KV_HEADS = 2
HEAD_DIM = 64
Q_PER_KV = 8
WINDOW = 128
D_SSM = 2048
SSD_HEADS = 32
SSD_GROUPS = 8
SSD_HPG = 4
SSD_P = 64
SSD_N = 128
D_FF = 2816
NEG = -1e30
LANES = 128
SUBLANES = 8
_VMEM_LIMIT_WIDE = 62 * 1024 * 1024

ADAM_LR = 0.001
ADAM_B1 = 0.9
ADAM_B2 = 0.999
ADAM_EPS = 1e-08
ADAM_WD = 0.01
ADAM_STEP = 10

MESH = pl.DeviceIdType.MESH
N_CHIPS = 4


def _pick(n, cands):
    for c in cands:
        if n % c == 0:
            return c
    raise ValueError(f"no tile for {n} in {cands}")


def _col_tile(n, limit=1792):
    best = None
    for t in range(LANES, min(n, limit) + 1, LANES):
        if n % t == 0:
            best = t
    if best is None:
        raise ValueError(f"no lane tile for {n}")
    return best


def _sigmoid(x):
    return 0.5 + 0.5 * jnp.tanh(0.5 * x)


def _log1p(e):
    u = 1.0 + e
    return jnp.where(u == 1.0, e, jnp.log(u) * (e / jnp.where(u == 1.0, 1.0, u - 1.0)))


def _softplus(x):
    return jnp.maximum(x, 0.0) + _log1p(jnp.exp(-jnp.abs(x)))


def _neg_expm1(x):
    poly = x * (1.0 + x * (0.5 + x * (1.0 / 6.0 + x * (1.0 / 24.0 + x * (1.0 / 120.0)))))
    return -jnp.where(x > -0.05, poly, jnp.exp(x) - 1.0)


_GELU_C = math.sqrt(2.0 / math.pi)


def _gelu(x):
    u = 0.5 + 0.5 * jnp.tanh(x * (_GELU_C + (_GELU_C * 0.044715) * (x * x)))
    return x * u


def _gelu_and_grad(x):
    x2 = x * x
    u = 0.5 + 0.5 * jnp.tanh(x * (_GELU_C + (_GELU_C * 0.044715) * x2))
    g = x * u
    dg = u * (1.0 + (x - g) * (2.0 * _GELU_C + (6.0 * 0.044715 * _GELU_C) * x2))
    return g, dg


def _silu_and_grad(x):
    s = _sigmoid(x)
    return x * s, s * (1.0 + x * (1.0 - s))


def _dot(a, b):
    return jnp.dot(a, b, preferred_element_type=F32)


def _dot_nt(a, b):
    return lax.dot_general(a, b, (((1,), (1,)), ((), ())), preferred_element_type=F32)


def _dot_tn(a, b):
    return lax.dot_general(a, b, (((0,), (0,)), ((), ())), preferred_element_type=F32)


def _row_iota(t):
    return lax.broadcasted_iota(jnp.int32, (t, 1), 0)


def _scan_fwd(a, u, t):
    row = _row_iota(t)
    d = 1
    while d < t:
        m = row >= d
        u_sh = jnp.where(m, pltpu.roll(u, d, 0), 0.0)
        a_sh = jnp.where(m, pltpu.roll(a, d, 0), 1.0)
        u = u + a * u_sh
        a = a * a_sh
        d *= 2
    return a, u


def _scan_rev(c, x, t):
    row = _row_iota(t)
    d = 1
    while d < t:
        m = row < t - d
        x_sh = jnp.where(m, pltpu.roll(x, t - d, 0), 0.0)
        c_sh = jnp.where(m, pltpu.roll(c, t - d, 0), 1.0)
        x = x + c * x_sh
        c = c * c_sh
        d *= 2
    return c, x


def _cumsum_rows(x, t):
    row = _row_iota(t)
    d = 1
    while d < t:
        x = x + jnp.where(row >= d, pltpu.roll(x, d, 0), 0.0)
        d *= 2
    return x


def _rev_cumsum_rows(x, t):
    row = _row_iota(t)
    d = 1
    while d < t:
        x = x + jnp.where(row < t - d, pltpu.roll(x, t - d, 0), 0.0)
        d *= 2
    return x


def _rms_bwd(x, g, dy):
    rs = lax.rsqrt(jnp.mean(x * x, axis=-1, keepdims=True) + EPS)
    gy = dy * g
    dx = rs * gy - x * (rs * rs * rs) * jnp.mean(x * gy, axis=-1, keepdims=True)
    return dx, dy * x * rs


def _mm_nt(dy, w, name):
    m, n = dy.shape
    k = w.shape[0]
    wide = n > 3328
    tm = _pick(m, (320, 256, 128)) if wide else _pick(m, (640, 512, 256, 128))
    tk = _col_tile(k, 512 if wide else 1408)

    def body(dy_ref, w_ref, o_ref):
        o_ref[...] = _dot_nt(dy_ref[...].astype(BF16), w_ref[...])

    return pl.pallas_call(
        body, name=name, grid=(k // tk, m // tm),
        in_specs=[pl.BlockSpec((tm, n), lambda j, i: (i, 0)),
                  pl.BlockSpec((tk, n), lambda j, i: (j, 0))],
        out_specs=pl.BlockSpec((tm, tk), lambda j, i: (i, j)),
        out_shape=jax.ShapeDtypeStruct((m, k), F32),
        compiler_params=pltpu.CompilerParams(dimension_semantics=("parallel", "parallel")),
    )(dy, w)


def _mm_tn(a, dy, name):
    m, k = a.shape
    n = dy.shape[1]
    tm = _pick(m, (640, 512, 256, 128))
    tk = _col_tile(k, 1408)
    tn = _col_tile(n, 1664)
    nsteps = m // tm

    def body(a_ref, dy_ref, o_ref, acc):
        @pl.when(pl.program_id(2) == 0)
        def _():
            acc[...] = jnp.zeros_like(acc)

        acc[...] += _dot_tn(a_ref[...].astype(BF16), dy_ref[...].astype(BF16))

        @pl.when(pl.program_id(2) == nsteps - 1)
        def _():
            o_ref[...] = acc[...].astype(o_ref.dtype)

    return pl.pallas_call(
        body, name=name, grid=(k // tk, n // tn, nsteps),
        in_specs=[pl.BlockSpec((tm, tk), lambda kk, j, i: (i, kk)),
                  pl.BlockSpec((tm, tn), lambda kk, j, i: (i, j))],
        out_specs=pl.BlockSpec((tk, tn), lambda kk, j, i: (kk, j)),
        out_shape=jax.ShapeDtypeStruct((k, n), BF16),
        scratch_shapes=[pltpu.VMEM((tk, tn), F32)],
        compiler_params=pltpu.CompilerParams(
            dimension_semantics=("parallel", "parallel", "arbitrary")),
    )(a, dy)


def _mm_nn_sh(a, w4, n_out, name):
    m, k = a.shape
    s, _, n = w4.shape
    tm = _pick(m, (320, 256, 128))

    def body(a_ref, w_ref, o_ref):
        av = a_ref[...].astype(BF16)
        for j in range(s):
            o_ref[:, j * n:(j + 1) * n] = _dot(av, w_ref[j])
        if n_out > s * n:
            o_ref[:, s * n:] = jnp.zeros((tm, n_out - s * n), F32)

    return pl.pallas_call(
        body, name=name, grid=(m // tm,),
        in_specs=[pl.BlockSpec((tm, k), lambda i: (i, 0)),
                  pl.BlockSpec((s, k, n), lambda i: (0, 0, 0))],
        out_specs=pl.BlockSpec((tm, n_out), lambda i: (i, 0)),
        out_shape=jax.ShapeDtypeStruct((m, n_out), F32),
        compiler_params=pltpu.CompilerParams(dimension_semantics=("parallel",)),
    )(a, w4)


def _mm_nt_sh(dy, w4, norm, name, plan=None):
    dys = dy if isinstance(dy, (tuple, list)) else (dy,)
    h, g, dres = norm
    m = dys[0].shape[0]
    s, k, n = w4.shape
    tm = _pick(m, (640, 512, 256, 128))
    where = _shard_columns(dys, s, n)
    p_in, p_shapes, p_out, p_scr = _plan_parts(plan)
    nd, ni = len(dys), m // tm

    def body(*refs):
        w_ref, h_ref, g_ref, dres_ref = refs[nd:nd + 4]
        cins = refs[nd + 4:nd + 4 + len(p_in)]
        dh_ref, dg_ref = refs[nd + 4 + len(p_in):nd + 6 + len(p_in)]
        couts = refs[nd + 6 + len(p_in):nd + 6 + len(p_in) + len(p_out)]
        sems = refs[nd + 6 + len(p_in) + len(p_out):]
        i = pl.program_id(0)
        if plan is not None:
            @pl.when(i == 0)
            def _():
                plan.start(cins, couts, sems)

        du = None
        for j, (p, c0) in enumerate(where):
            t = _dot_nt(refs[p][:, c0:c0 + n].astype(BF16), w_ref[j])
            du = t if du is None else du + t
        dx, dgt = _rms_bwd(h_ref[...], g_ref[...], du)
        dh_ref[...] = jnp.where(_real_rows(i, tm), dres_ref[...] + dx, 0.0)
        _acc_add(i == 0, dg_ref, jnp.sum(dgt, axis=0, keepdims=True))
        if plan is not None:
            @pl.when(i == ni - 1)
            def _():
                plan.wait(cins, couts, sems)

    row = pl.BlockSpec((tm, k), lambda i: (i, 0))
    vec = pl.BlockSpec((1, k), lambda i: (0, 0))
    res = pl.pallas_call(
        body, name=name, grid=(ni,),
        in_specs=[pl.BlockSpec((tm, d.shape[1]), lambda i: (i, 0)) for d in dys]
        + [pl.BlockSpec((s, k, n), lambda i: (0, 0, 0)), row, vec, row] + p_in,
        out_specs=[row, vec] + p_out,
        out_shape=[jax.ShapeDtypeStruct((m, k), F32), jax.ShapeDtypeStruct((1, k), F32)] + p_shapes,
        scratch_shapes=p_scr,
        compiler_params=pltpu.CompilerParams(dimension_semantics=("arbitrary",),
                                             vmem_limit_bytes=_VMEM_LIMIT_WIDE),
    )(*dys, w4, h, g, dres, *(plan.ins if plan is not None else []))
    return (res[0], res[1]), res[2:]


def _shard_columns(dys, s, n):
    where = []
    for p, d in enumerate(dys):
        where += [(p, c * n) for c in range(d.shape[1] // n)]
    assert len(where) >= s
    return where[:s]


def _mm_tn_sh(a, dy, n, name):
    dys = dy if isinstance(dy, (tuple, list)) else (dy,)
    m, k = a.shape
    s = N_CHIPS
    tm = _pick(m, (640, 512, 256, 128))
    tk = _col_tile(k, 512)
    nsteps = m // tm
    where = _shard_columns(dys, s, n)

    def body(*refs):
        a_ref, o_ref, acc = refs[0], refs[len(dys) + 1], refs[len(dys) + 2]

        @pl.when(pl.program_id(1) == 0)
        def _():
            acc[...] = jnp.zeros_like(acc)

        av = a_ref[...].astype(BF16)
        for j, (p, c0) in enumerate(where):
            acc[j] += _dot_tn(av, refs[1 + p][:, c0:c0 + n].astype(BF16))

        @pl.when(pl.program_id(1) == nsteps - 1)
        def _():
            o_ref[...] = acc[...].astype(o_ref.dtype)

    return pl.pallas_call(
        body, name=name, grid=(k // tk, nsteps),
        in_specs=[pl.BlockSpec((tm, tk), lambda kk, i: (i, kk))]
        + [pl.BlockSpec((tm, d.shape[1]), lambda kk, i: (i, 0)) for d in dys],
        out_specs=pl.BlockSpec((s, tk, n), lambda kk, i: (0, kk, 0)),
        out_shape=jax.ShapeDtypeStruct((s, k, n), BF16),
        scratch_shapes=[pltpu.VMEM((s, tk, n), F32)],
        compiler_params=pltpu.CompilerParams(dimension_semantics=("parallel", "arbitrary"),
                                             vmem_limit_bytes=_VMEM_LIMIT_WIDE),
    )(a, *dys)


def _rowcall(name, body, lp, tm, rows=(), prevs=(), vecs=(), outs=(), accs=(), scratch=(),
             reverse=False, seq=False, plan=None):
    p_in, p_shapes, p_out, p_scr = _plan_parts(plan)
    nt = lp // tm
    hb = tm // SUBLANES

    def ri(i):
        return nt - 1 - i if reverse else i

    in_specs, args = [], []
    for arr, w, cb in rows:
        in_specs.append(pl.BlockSpec((tm, w), lambda i, cb=cb: (ri(i), cb)))
        args.append(arr)
    for arr, w, cb in prevs:
        in_specs.append(pl.BlockSpec((SUBLANES, w), lambda i, cb=cb: (jnp.maximum(ri(i) * hb - 1, 0), cb)))
        args.append(arr)
    for arr in vecs:
        in_specs.append(pl.BlockSpec(arr.shape, lambda i, nd=arr.ndim: (0,) * nd))
        args.append(arr)
    out_shape, out_specs = [], []
    for w, dt in outs:
        out_shape.append(jax.ShapeDtypeStruct((lp, w), dt))
        out_specs.append(pl.BlockSpec((tm, w), lambda i: (ri(i), 0)))
    for shp, dt in accs:
        out_shape.append(jax.ShapeDtypeStruct(shp, dt))
        out_specs.append(pl.BlockSpec(shp, lambda i, nd=len(shp): (0,) * nd))

    n_in, n_out, n_scr = len(args), len(out_shape), len(scratch)

    def kern(*refs):
        i = pl.program_id(0)
        own = (refs[:n_in] + refs[n_in + len(p_in):n_in + len(p_in) + n_out]
               + refs[n_in + len(p_in) + n_out + len(p_out):n_in + len(p_in) + n_out + len(p_out) + n_scr])
        cins = refs[n_in:n_in + len(p_in)]
        couts = refs[n_in + len(p_in) + n_out:n_in + len(p_in) + n_out + len(p_out)]
        sems = refs[n_in + len(p_in) + n_out + len(p_out) + n_scr:]
        if plan is not None:
            @pl.when(i == 0)
            def _():
                plan.start(cins, couts, sems)

        body(ri(i), i == 0, *own)
        if plan is not None:
            @pl.when(i == nt - 1)
            def _():
                plan.wait(cins, couts, sems)

    sem = ("arbitrary",) if (seq or accs or plan is not None) else ("parallel",)
    res = pl.pallas_call(
        kern, name=name, grid=(nt,), in_specs=in_specs + p_in, out_specs=out_specs + p_out,
        out_shape=out_shape + p_shapes, scratch_shapes=list(scratch) + p_scr,
        compiler_params=pltpu.CompilerParams(dimension_semantics=sem),
    )(*args, *(plan.ins if plan is not None else []))
    return res if plan is None else (res[:n_out], res[n_out:])


def _acc_add(first, ref, val):
    @pl.when(first)
    def _():
        ref[...] = jnp.zeros_like(ref)

    ref[...] += val


def _real_rows(r, tm):
    return (r * tm + _row_iota(tm)) >= PAD


def _rmsnorm_fwd(h, g, name):
    lp, d = h.shape
    tm = _pick(lp, (640, 512, 256, 128))

    def body(r, first, h_ref, g_ref, u_ref):
        x = h_ref[...]
        rs = lax.rsqrt(jnp.mean(x * x, axis=-1, keepdims=True) + EPS)
        u_ref[...] = (x * rs * g_ref[...]).astype(u_ref.dtype)

    return _rowcall(name, body, lp, tm, rows=[(h, d, 0)], vecs=[g], outs=[(d, BF16)])[0]


def _mm_postnorm_res(a, w, h, g, name, next_pre=None):
    parts = a if isinstance(a, (tuple, list)) else (a,)
    lp, d = h.shape
    k = w.shape[0]
    tm = _pick(lp, (640, 512, 256, 128))
    offs = [sum(p.shape[1] for p in parts[:i]) for i in range(len(parts))]
    np_ = len(parts)

    def body(*refs):
        w_ref, h_ref, g_ref = refs[np_], refs[np_ + 1], refs[np_ + 2]
        rest = refs[np_ + 3:]
        acc = None
        for a_ref, p, off in zip(refs, parts, offs):
            t = _dot(a_ref[...].astype(BF16), w_ref[off:off + p.shape[1], :])
            acc = t if acc is None else acc + t
        outs = rest[1:] if next_pre is not None else rest
        outs[0][...] = acc
        rs = lax.rsqrt(jnp.mean(acc * acc, axis=-1, keepdims=True) + EPS)
        hn = jnp.where(_real_rows(pl.program_id(0), tm), h_ref[...] + acc * rs * g_ref[...], 0.0)
        outs[1][...] = hn
        if next_pre is not None:
            rs2 = lax.rsqrt(jnp.mean(hn * hn, axis=-1, keepdims=True) + EPS)
            outs[2][...] = (hn * rs2 * rest[0][...]).astype(BF16)

    row = pl.BlockSpec((tm, d), lambda i: (i, 0))
    vec = pl.BlockSpec((1, d), lambda i: (0, 0))
    n_vec = 2 if next_pre is not None else 1
    return pl.pallas_call(
        body, name=name, grid=(lp // tm,),
        in_specs=[pl.BlockSpec((tm, p.shape[1]), lambda i: (i, 0)) for p in parts]
        + [pl.BlockSpec((k, d), lambda i: (0, 0)), row] + [vec] * n_vec,
        out_specs=[row] * (2 + (next_pre is not None)),
        out_shape=[jax.ShapeDtypeStruct((lp, d), F32)] * 2
        + ([jax.ShapeDtypeStruct((lp, d), BF16)] if next_pre is not None else []),
        compiler_params=pltpu.CompilerParams(dimension_semantics=("parallel",)),
    )(*parts, w, h, g, *([next_pre] if next_pre is not None else []))


def _mm_postnorm_loss(a, w, h, g, tgt, name):
    lp, d = h.shape
    k = w.shape[0]
    tm = _pick(lp, (640, 512, 256, 128))

    def body(a_ref, w_ref, h_ref, g_ref, t_ref, o_ref, dh_ref, ls_ref):
        i = pl.program_id(0)
        acc = _dot(a_ref[...].astype(BF16), w_ref[...])
        o_ref[...] = acc
        rs = lax.rsqrt(jnp.mean(acc * acc, axis=-1, keepdims=True) + EPS)
        tok = (i * tm + _row_iota(tm)) >= BLOCK
        e = jnp.where(tok, h_ref[...] + acc * rs * g_ref[...] - t_ref[...], 0.0)
        dh_ref[...] = e * (1.0 / d)
        _acc_add(i == 0, ls_ref, jnp.sum(e * e, axis=0, keepdims=True))

    row = pl.BlockSpec((tm, d), lambda i: (i, 0))
    vec = pl.BlockSpec((1, d), lambda i: (0, 0))
    return pl.pallas_call(
        body, name=name, grid=(lp // tm,),
        in_specs=[pl.BlockSpec((tm, k), lambda i: (i, 0)), pl.BlockSpec((k, d), lambda i: (0, 0)),
                  row, vec, row],
        out_specs=[row, row, vec],
        out_shape=[jax.ShapeDtypeStruct((lp, d), F32), jax.ShapeDtypeStruct((lp, d), F32),
                   jax.ShapeDtypeStruct((1, d), F32)],
        compiler_params=pltpu.CompilerParams(dimension_semantics=("arbitrary",)),
    )(a, w, h, g, tgt)


def _postnorm_bwd(o, g, dh, name):
    lp, d = o.shape
    tm = _pick(lp, (640, 512, 256, 128))

    def body(r, first, o_ref, dh_ref, g_ref, do_ref, dg_ref):
        dx, dgt = _rms_bwd(o_ref[...], g_ref[...], dh_ref[...])
        do_ref[...] = dx.astype(do_ref.dtype)
        _acc_add(first, dg_ref, jnp.sum(dgt, axis=0, keepdims=True))

    return _rowcall(name, body, lp, tm, rows=[(o, d, 0), (dh, d, 0)], vecs=[g],
                    outs=[(d, BF16)], accs=[((1, d), F32)])


def _conv_tiles(lp, width):
    wc = _col_tile(width, 1408)
    tm = _pick(lp, (320, 256, 128))
    return tm, wc


def _conv_fwd(x, col_off, width, w, b, name):
    lp = x.shape[0]
    kk = w.shape[0]
    tm, wc = _conv_tiles(lp, width)
    offb = col_off // wc
    assert col_off % wc == 0
    hb = tm // SUBLANES

    def body(x_ref, xp_ref, w_ref, b_ref, y_ref):
        i = pl.program_id(1)
        xv = x_ref[...]
        halo = jnp.where(i > 0, xp_ref[...], 0.0)
        xx = jnp.concatenate([halo, xv], axis=0)
        acc = b_ref[...] + w_ref[kk - 1:kk, :] * xv
        for j in range(1, kk):
            acc = acc + w_ref[kk - 1 - j:kk - j, :] * pltpu.roll(xx, j, 0)[SUBLANES:, :]
        y_ref[...] = acc

    return pl.pallas_call(
        body, name=name, grid=(width // wc, lp // tm),
        in_specs=[pl.BlockSpec((tm, wc), lambda j, i: (i, offb + j)),
                  pl.BlockSpec((SUBLANES, wc), lambda j, i: (jnp.maximum(i * hb - 1, 0), offb + j)),
                  pl.BlockSpec((kk, wc), lambda j, i: (0, j)),
                  pl.BlockSpec((1, wc), lambda j, i: (0, j))],
        out_specs=pl.BlockSpec((tm, wc), lambda j, i: (i, j)),
        out_shape=jax.ShapeDtypeStruct((lp, width), F32),
        compiler_params=pltpu.CompilerParams(dimension_semantics=("parallel", "parallel")),
    )(x, x, w, b)


def _conv_bwd(x, col_off, width, dy, w, name, w_col_off=0):
    lp = x.shape[0]
    kk = w.shape[0]
    tm, wc = _conv_tiles(lp, width)
    offb = col_off // wc
    woffb = w_col_off // wc
    assert col_off % wc == 0 and w_col_off % wc == 0
    hrows = SUBLANES * (4 // dy.dtype.itemsize)
    ext = tm + hrows

    def body(x_ref, dy_ref, dn_ref, w_ref, dx_ref, dw_ref, db_ref):
        i = pl.program_id(1)
        last = pl.num_programs(1) - 1
        xv = x_ref[...]
        dyv = dy_ref[...].astype(F32)
        dd = jnp.concatenate([dyv, jnp.where(i < last, dn_ref[...].astype(F32), 0.0)], axis=0)
        dx = w_ref[kk - 1:kk, :] * dyv
        rows = [jnp.sum(dyv * xv, axis=0, keepdims=True)]
        for m in range(1, kk):
            ahead = pltpu.roll(dd, ext - m, 0)[:tm, :]
            dx = dx + w_ref[kk - 1 - m:kk - m, :] * ahead
            rows.append(jnp.sum(ahead * xv, axis=0, keepdims=True))
        dx_ref[...] = dx.astype(dx_ref.dtype)
        dwp = jnp.concatenate(rows[::-1] + [jnp.zeros((SUBLANES - kk, wc), F32)], axis=0)

        @pl.when(i == 0)
        def _():
            dw_ref[...] = jnp.zeros_like(dw_ref)
            db_ref[...] = jnp.zeros_like(db_ref)

        dw_ref[...] += dwp
        db_ref[...] += jnp.sum(dyv, axis=0, keepdims=True)

    return pl.pallas_call(
        body, name=name, grid=(width // wc, lp // tm),
        in_specs=[pl.BlockSpec((tm, wc), lambda j, i: (i, offb + j)),
                  pl.BlockSpec((tm, wc), lambda j, i: (i, j)),
                  pl.BlockSpec((hrows, wc), lambda j, i: (jnp.minimum((i + 1) * (tm // hrows), lp // hrows - 1), j)),
                  pl.BlockSpec((kk, wc), lambda j, i: (0, woffb + j))],
        out_specs=[pl.BlockSpec((tm, wc), lambda j, i: (i, j)),
                   pl.BlockSpec((SUBLANES, wc), lambda j, i: (0, j)),
                   pl.BlockSpec((1, wc), lambda j, i: (0, j))],
        out_shape=[jax.ShapeDtypeStruct((lp, width), BF16),
                   jax.ShapeDtypeStruct((SUBLANES, width), F32),
                   jax.ShapeDtypeStruct((1, width), F32)],
        compiler_params=pltpu.CompilerParams(dimension_semantics=("parallel", "arbitrary")),
    )(x, dy, dy, w)


_FFN_K = 3
_FFN_WC = 1408


def _conv3_ext(x_ext, w_ref, b_ref):
    return (b_ref[...] + w_ref[2:3, :] * x_ext + w_ref[1:2, :] * pltpu.roll(x_ext, 1, 0)
            + w_ref[0:1, :] * pltpu.roll(x_ext, 2, 0))


def _ffn_convact_fwd(hp, cw, cb, name, plan=None):
    lp = hp.shape[0]
    tm = _pick(lp, (320, 256, 128))
    wc = _FFN_WC
    nj = D_FF // wc
    ni = lp // tm
    hb = tm // SUBLANES
    p_in, p_shapes, p_out, p_scr = _plan_parts(plan)

    def body(*refs):
        g_ref, gp_ref, u_ref, up_ref, wg_ref, wu_ref, bg_ref, bu_ref = refs[:8]
        cins = refs[8:8 + len(p_in)]
        a_ref, hg_ref, hu_ref = refs[8 + len(p_in):11 + len(p_in)]
        couts = refs[11 + len(p_in):11 + len(p_in) + len(p_out)]
        sems = refs[11 + len(p_in) + len(p_out):]
        i = pl.program_id(1)
        step = pl.program_id(0) * ni + i
        if plan is not None:
            @pl.when(step == 0)
            def _():
                plan.start(cins, couts, sems)

        def conv(x_ref, p_ref, w_ref, b_ref):
            x_ext = jnp.concatenate([jnp.where(i > 0, p_ref[...], 0.0), x_ref[...]], axis=0)
            return _conv3_ext(x_ext, w_ref, b_ref)[SUBLANES:, :]

        hg = conv(g_ref, gp_ref, wg_ref, bg_ref)
        hu = conv(u_ref, up_ref, wu_ref, bu_ref)
        a_ref[...] = (_gelu(hg) * hu).astype(a_ref.dtype)
        hg_ref[...] = hg.astype(hg_ref.dtype)
        hu_ref[...] = hu.astype(hu_ref.dtype)
        if plan is not None:
            @pl.when(step == nj * ni - 1)
            def _():
                plan.wait(cins, couts, sems)

    tile = lambda off: pl.BlockSpec((tm, wc), lambda j, i: (i, off + j))
    prev = lambda off: pl.BlockSpec((SUBLANES, wc), lambda j, i: (jnp.maximum(i * hb - 1, 0), off + j))
    vec = lambda rows, off: pl.BlockSpec((rows, wc), lambda j, i: (0, off + j))
    sem = ("arbitrary", "arbitrary") if plan is not None else ("parallel", "parallel")
    res = pl.pallas_call(
        body, name=name, grid=(nj, ni),
        in_specs=[tile(0), prev(0), tile(nj), prev(nj), vec(_FFN_K, 0), vec(_FFN_K, nj), vec(1, 0),
                  vec(1, nj)] + p_in,
        out_specs=[tile(0)] * 3 + p_out,
        out_shape=[jax.ShapeDtypeStruct((lp, D_FF), BF16)] * 3 + p_shapes,
        scratch_shapes=p_scr,
        compiler_params=pltpu.CompilerParams(dimension_semantics=sem),
    )(hp, hp, hp, hp, cw, cw, cb, cb, *(plan.ins if plan is not None else []))
    return res if plan is None else (res[:3], res[3:])


def _ffn_down_act_bwd(do, w_down, hg, hu, name):
    lp, d = do.shape
    tm = _pick(lp, (640, 512, 256, 128))
    tk = _FFN_WC

    def body(do_ref, w_ref, g_ref, u_ref, dg_ref, du_ref):
        da = _dot_nt(do_ref[...], w_ref[...])
        gl, dgl = _gelu_and_grad(g_ref[...].astype(F32))
        dg_ref[...] = (da * u_ref[...].astype(F32) * dgl).astype(dg_ref.dtype)
        du_ref[...] = (da * gl).astype(du_ref.dtype)

    tile = pl.BlockSpec((tm, tk), lambda j, i: (i, j))
    return pl.pallas_call(
        body, name=name, grid=(D_FF // tk, lp // tm),
        in_specs=[pl.BlockSpec((tm, d), lambda j, i: (i, 0)),
                  pl.BlockSpec((tk, d), lambda j, i: (j, 0)), tile, tile],
        out_specs=[tile, tile],
        out_shape=[jax.ShapeDtypeStruct((lp, D_FF), BF16)] * 2,
        compiler_params=pltpu.CompilerParams(dimension_semantics=("parallel", "parallel")),
    )(do, w_down, hg, hu)


def _lru_gates(x, wa_ref, wx_ref, ba, bx, lam):
    xb = x.astype(BF16)
    za, zx = [], []
    for n in range(LRU_BLOCKS):
        xs = xb[:, n * LRU_BS:(n + 1) * LRU_BS]
        za.append(_dot(xs, wa_ref[n]))
        zx.append(_dot(xs, wx_ref[n]))
    r = _sigmoid(jnp.concatenate(za, axis=1) + ba)
    ig = _sigmoid(jnp.concatenate(zx, axis=1) + bx)
    sp = _softplus(-lam)
    log_a = -LRU_C * r * sp
    a = jnp.exp(log_a)
    om = _neg_expm1(2.0 * log_a)
    mult = jnp.sqrt(om)
    return xb, r, ig, sp, a, om, mult


def _lru_fwd(proj, xrc, wa, wx, ba, bx, lam, name, plan=None):
    lp, d = xrc.shape
    tm = BLOCK

    def body(r_idx, first, gate_ref, x_ref, wa_ref, wx_ref, ba_ref, bx_ref, lam_ref,
             y_ref, h_ref, carry):
        @pl.when(first)
        def _():
            carry[...] = jnp.zeros_like(carry)

        x = x_ref[...]
        _, _, ig, _, a, _, mult = _lru_gates(x, wa_ref, wx_ref, ba_ref[...], bx_ref[...], lam_ref[...])
        u = jnp.where(_real_rows(r_idx, tm), mult * ig * x, 0.0)
        acum, hloc = _scan_fwd(a, u, tm)
        h = hloc + acum * carry[0:1, :]
        h_ref[...] = h
        carry[0:1, :] = h[tm - 1:tm, :]
        y_ref[...] = (_gelu(gate_ref[...]) * h).astype(y_ref.dtype)

    return _rowcall(name, body, lp, tm, rows=[(proj, d, 0), (xrc, d, 0)],
                    vecs=[wa, wx, ba, bx, lam], outs=[(d, BF16), (d, F32)],
                    scratch=[pltpu.VMEM((SUBLANES, d), F32)], seq=True, plan=plan)


def _lru_bwd(proj, xrc, hl, dmix, wa, wx, ba, bx, lam, name, plan=None):
    lp, d = xrc.shape
    tm = BLOCK

    def body(r_idx, first, gate_ref, x_ref, h_ref, dy_ref, hp_ref, wa_ref, wx_ref, ba_ref, bx_ref,
             lam_ref, dgate_ref, dx_ref, dwa_ref, dwx_ref, dba_ref, dbx_ref, dlam_ref, carry):
        @pl.when(first)
        def _():
            carry[...] = jnp.zeros_like(carry)
            dwa_ref[...] = jnp.zeros_like(dwa_ref)
            dwx_ref[...] = jnp.zeros_like(dwx_ref)
            dba_ref[...] = jnp.zeros_like(dba_ref)
            dbx_ref[...] = jnp.zeros_like(dbx_ref)
            dlam_ref[...] = jnp.zeros_like(dlam_ref)

        x = x_ref[...]
        lam = lam_ref[...]
        xb, r, ig, sp, a, om, mult = _lru_gates(x, wa_ref, wx_ref, ba_ref[...], bx_ref[...], lam)
        h = h_ref[...]
        dy = dy_ref[...]
        gl, dgl = _gelu_and_grad(gate_ref[...])
        dgate_ref[...] = (dy * h * dgl).astype(dgate_ref.dtype)
        row = _row_iota(tm)
        lastrow = row == tm - 1
        xg = dy * gl + jnp.where(lastrow, carry[0:1, :], 0.0)
        c = jnp.where(lastrow, 1.0, pltpu.roll(a, tm - 1, 0))
        _, g = _scan_rev(c, xg, tm)
        carry[0:1, :] = a[0:1, :] * g[0:1, :]
        hprev_in = jnp.where(r_idx > 0, hp_ref[SUBLANES - 1:SUBLANES, :], 0.0)
        hprev = jnp.where(row == 0, hprev_in, pltpu.roll(h, 1, 0))
        du = jnp.where(_real_rows(r_idx, tm), g, 0.0)
        da = g * hprev
        dmult = du * ig * x
        dig = du * mult * x
        dxv = du * mult * ig
        e2 = 1.0 - om
        dlog_a = da * a - dmult * e2 / mult
        dr = dlog_a * (-LRU_C) * sp
        dsp = jnp.sum(dlog_a * (-LRU_C) * r, axis=0, keepdims=True)
        dlam_ref[...] += -dsp * _sigmoid(-lam)
        dza = dr * r * (1.0 - r)
        dzx = dig * ig * (1.0 - ig)
        dba_ref[...] += jnp.sum(dza, axis=0, keepdims=True)
        dbx_ref[...] += jnp.sum(dzx, axis=0, keepdims=True)
        dzab = dza.astype(BF16)
        dzxb = dzx.astype(BF16)
        parts = []
        for n in range(LRU_BLOCKS):
            sl = slice(n * LRU_BS, (n + 1) * LRU_BS)
            dwa_ref[n] += _dot_tn(xb[:, sl], dzab[:, sl])
            dwx_ref[n] += _dot_tn(xb[:, sl], dzxb[:, sl])
            parts.append(_dot_nt(dzab[:, sl], wa_ref[n]) + _dot_nt(dzxb[:, sl], wx_ref[n]))
        dx_ref[...] = dxv + jnp.concatenate(parts, axis=1)

    return _rowcall(name, body, lp, tm,
                    rows=[(proj, d, 0), (xrc, d, 0), (hl, d, 0), (dmix, d, 0)],
                    prevs=[(hl, d, 0)], vecs=[wa, wx, ba, bx, lam],
                    outs=[(d, BF16), (d, F32)],
                    accs=[((LRU_BLOCKS, LRU_BS, LRU_BS), F32), ((LRU_BLOCKS, LRU_BS, LRU_BS), F32),
                          ((1, d), F32), ((1, d), F32), ((1, d), F32)],
                    scratch=[pltpu.VMEM((SUBLANES, d), F32)], reverse=True, seq=True, plan=plan)


_SLOPES = [2.0 ** (-8.0 * (h + 1) / N_Q_HEADS) for h in range(N_Q_HEADS)]
_QK_SCALE = HEAD_DIM ** -0.5
_QCOL = 2 * D_MODEL // D_MODEL
_KCOL = (3 * D_MODEL) // LANES
_VCOL = _KCOL + 1


def _attn_masks(n):
    start = pl.multiple_of(jnp.maximum(n - 1, 0) * BLOCK, BLOCK)
    qi = n * BLOCK + lax.broadcasted_iota(jnp.int32, (BLOCK, 2 * BLOCK), 0)
    kj = start + lax.broadcasted_iota(jnp.int32, (BLOCK, 2 * BLOCK), 1)
    dist = qi - kj
    ok = (kj >= BLOCK) & (dist >= 0) & (dist < WINDOW)
    dm = (n * BLOCK - PAD + lax.broadcasted_iota(jnp.int32, (BLOCK, N_META), 0)
          - lax.broadcasted_iota(jnp.int32, (BLOCK, N_META), 1))
    okm = dm >= 0
    return start, ok, dist.astype(F32), okm, jnp.minimum(dm, WINDOW).astype(F32)


def _group_rows(ref, g, scale=None):
    x = jnp.concatenate(
        [ref[:, (g * Q_PER_KV + hh) * HEAD_DIM:(g * Q_PER_KV + hh + 1) * HEAD_DIM] for hh in range(Q_PER_KV)],
        axis=0)
    return (x if scale is None else x * scale).astype(BF16)


def _attn_probs(s, sm, sink_ref, g, ok, distf, okm, dmf):
    slope = jnp.stack([jnp.full((1, 1), _SLOPES[g * Q_PER_KV + hh], F32) for hh in range(Q_PER_KV)])
    sink = jnp.stack([sink_ref[0:1, g * Q_PER_KV + hh:g * Q_PER_KV + hh + 1] for hh in range(Q_PER_KV)])
    s = s.reshape(Q_PER_KV, BLOCK, 2 * BLOCK)
    sm = sm.reshape(Q_PER_KV, BLOCK, N_META)
    s = jnp.where(ok[None], s - slope * distf[None], NEG)
    sm = jnp.where(okm[None], sm - slope * dmf[None], NEG)
    mx = jnp.maximum(jnp.maximum(jnp.max(s, axis=-1, keepdims=True),
                                 jnp.max(sm, axis=-1, keepdims=True)), sink)
    p = jnp.exp(s - mx)
    pm = jnp.exp(sm - mx)
    ps = jnp.exp(sink - mx)
    inv = 1.0 / (jnp.sum(p, axis=-1, keepdims=True) + jnp.sum(pm, axis=-1, keepdims=True) + ps)
    return p, pm, ps, inv


def _attn_fwd(proj, sinks, name, plan=None):
    lp = proj.shape[0]
    nblk = lp // BLOCK
    p_in, p_shapes, p_out, p_scr = _plan_parts(plan)

    def body(*refs):
        q_ref, k_ref, v_ref, sink_ref = refs[:4]
        cins = refs[4:4 + len(p_in)]
        o_ref = refs[4 + len(p_in)]
        couts = refs[5 + len(p_in):5 + len(p_in) + len(p_out)]
        sems = refs[5 + len(p_in) + len(p_out):]
        n = pl.program_id(0)
        if plan is not None:
            @pl.when(n == 0)
            def _():
                plan.start(cins, couts, sems)

        start, ok, distf, okm, dmf = _attn_masks(n)
        kb = k_ref[pl.ds(start, 2 * BLOCK), :].astype(BF16)
        vb = v_ref[pl.ds(start, 2 * BLOCK), :].astype(BF16)
        km = k_ref[PAD:BLOCK, :].astype(BF16)
        vm = v_ref[PAD:BLOCK, :].astype(BF16)
        rows = Q_PER_KV * BLOCK
        gsl = [slice(g * HEAD_DIM, (g + 1) * HEAD_DIM) for g in range(N_KV_HEADS)]
        raw = []
        for g in range(N_KV_HEADS):
            qg = _group_rows(q_ref, g, _QK_SCALE)
            raw.append((_dot_nt(qg, kb[:, gsl[g]]), _dot_nt(qg, km[:, gsl[g]])))
        for g in range(N_KV_HEADS):
            gs = gsl[g]
            p, pm, _, inv = _attn_probs(raw[g][0], raw[g][1], sink_ref, g, ok, distf, okm, dmf)
            o = (_dot(p.astype(BF16).reshape(rows, 2 * BLOCK), vb[:, gs])
                 + _dot(pm.astype(BF16).reshape(rows, N_META), vm[:, gs])) * inv.reshape(rows, 1)
            for hh in range(Q_PER_KV):
                h = g * Q_PER_KV + hh
                o_ref[:, h * HEAD_DIM:(h + 1) * HEAD_DIM] = o[hh * BLOCK:(hh + 1) * BLOCK, :].astype(o_ref.dtype)
        if plan is not None:
            @pl.when(n == nblk - 1)
            def _():
                plan.wait(cins, couts, sems)

    res = pl.pallas_call(
        body, name=name, grid=(nblk,),
        in_specs=[pl.BlockSpec((BLOCK, D_MODEL), lambda n: (n, _QCOL)),
                  pl.BlockSpec((lp, LANES), lambda n: (0, _KCOL)),
                  pl.BlockSpec((lp, LANES), lambda n: (0, _VCOL)),
                  pl.BlockSpec(sinks.shape, lambda n: (0, 0))] + p_in,
        out_specs=[pl.BlockSpec((BLOCK, D_MODEL), lambda n: (n, 0))] + p_out,
        out_shape=[jax.ShapeDtypeStruct((lp, D_MODEL), BF16)] + p_shapes,
        scratch_shapes=p_scr,
        compiler_params=pltpu.CompilerParams(dimension_semantics=("arbitrary",)),
    )(proj, proj, proj, sinks, *(plan.ins if plan is not None else []))
    return res[0], res[1:]


def _attn_bwd(proj, sinks, dmix, name, plan=None):
    lp = proj.shape[0]
    nblk = lp // BLOCK

    p_in, p_shapes, p_out, p_scr = _plan_parts(plan)

    def body(*refs):
        q_ref, k_ref, v_ref, sink_ref, dy_ref = refs[:5]
        cins = refs[5:5 + len(p_in)]
        dq_ref, dk_ref, dv_ref, ds_ref = refs[5 + len(p_in):9 + len(p_in)]
        couts = refs[9 + len(p_in):9 + len(p_in) + len(p_out)]
        sems = refs[9 + len(p_in) + len(p_out):]
        n = pl.program_id(0)

        @pl.when(n == 0)
        def _():
            dk_ref[...] = jnp.zeros_like(dk_ref)
            dv_ref[...] = jnp.zeros_like(dv_ref)
            ds_ref[...] = jnp.zeros_like(ds_ref)
            if plan is not None:
                plan.start(cins, couts, sems)

        start, ok, distf, okm, dmf = _attn_masks(n)
        kb = k_ref[pl.ds(start, 2 * BLOCK), :].astype(BF16)
        vb = v_ref[pl.ds(start, 2 * BLOCK), :].astype(BF16)
        km = k_ref[PAD:BLOCK, :].astype(BF16)
        vm = v_ref[PAD:BLOCK, :].astype(BF16)
        lane16 = lax.broadcasted_iota(jnp.int32, (1, N_Q_HEADS), 1)
        dsink = jnp.zeros((1, N_Q_HEADS), F32)
        rows = Q_PER_KV * BLOCK
        gsl = [slice(g * HEAD_DIM, (g + 1) * HEAD_DIM) for g in range(N_KV_HEADS)]
        pre = []
        for g in range(N_KV_HEADS):
            qg = _group_rows(q_ref, g, _QK_SCALE)
            dog = _group_rows(dy_ref, g)
            pre.append((qg, dog, _dot_nt(qg, kb[:, gsl[g]]), _dot_nt(qg, km[:, gsl[g]]),
                        _dot_nt(dog, vb[:, gsl[g]]), _dot_nt(dog, vm[:, gsl[g]])))
        for g in range(N_KV_HEADS):
            gs = gsl[g]
            qg, dog, s_raw, sm_raw, dp, dpm = pre[g]
            p, pm, ps, inv = _attn_probs(s_raw, sm_raw, sink_ref, g, ok, distf, okm, dmf)
            pn, pmn, psn = p * inv, pm * inv, ps * inv
            dp = dp.reshape(Q_PER_KV, BLOCK, 2 * BLOCK)
            dpm = dpm.reshape(Q_PER_KV, BLOCK, N_META)
            delta = (jnp.sum(pn * dp, axis=-1, keepdims=True)
                     + jnp.sum(pmn * dpm, axis=-1, keepdims=True))
            dsb = (pn * (dp - delta)).astype(BF16).reshape(rows, 2 * BLOCK)
            dsm = (pmn * (dpm - delta)).astype(BF16).reshape(rows, N_META)
            dsk = jnp.sum(psn * delta, axis=1, keepdims=True)
            for hh in range(Q_PER_KV):
                dsink = dsink - jnp.where(lane16 == g * Q_PER_KV + hh, dsk[hh], 0.0)
            dq = (_dot(dsb, kb[:, gs]) + _dot(dsm, km[:, gs])) * _QK_SCALE
            for hh in range(Q_PER_KV):
                h = g * Q_PER_KV + hh
                dq_ref[:, h * HEAD_DIM:(h + 1) * HEAD_DIM] = dq[hh * BLOCK:(hh + 1) * BLOCK, :].astype(dq_ref.dtype)
            pnb = pn.astype(BF16).reshape(rows, 2 * BLOCK)
            pmnb = pmn.astype(BF16).reshape(rows, N_META)
            dk_ref[pl.ds(start, 2 * BLOCK), gs] += _dot_tn(dsb, qg)
            dv_ref[pl.ds(start, 2 * BLOCK), gs] += _dot_tn(pnb, dog)
            dk_ref[PAD:BLOCK, gs] += _dot_tn(dsm, qg)
            dv_ref[PAD:BLOCK, gs] += _dot_tn(pmnb, dog)
        ds_ref[...] += dsink
        if plan is not None:
            @pl.when(n == nblk - 1)
            def _():
                plan.wait(cins, couts, sems)

    res = pl.pallas_call(
        body, name=name, grid=(nblk,),
        in_specs=[pl.BlockSpec((BLOCK, D_MODEL), lambda n: (n, _QCOL)),
                  pl.BlockSpec((lp, LANES), lambda n: (0, _KCOL)),
                  pl.BlockSpec((lp, LANES), lambda n: (0, _VCOL)),
                  pl.BlockSpec(sinks.shape, lambda n: (0, 0)),
                  pl.BlockSpec((BLOCK, D_MODEL), lambda n: (n, 1))] + p_in,
        out_specs=[pl.BlockSpec((BLOCK, D_MODEL), lambda n: (n, 0)),
                   pl.BlockSpec((lp, LANES), lambda n: (0, 0)),
                   pl.BlockSpec((lp, LANES), lambda n: (0, 0)),
                   pl.BlockSpec((1, N_Q_HEADS), lambda n: (0, 0))] + p_out,
        out_shape=[jax.ShapeDtypeStruct((lp, D_MODEL), BF16),
                   jax.ShapeDtypeStruct((lp, LANES), F32),
                   jax.ShapeDtypeStruct((lp, LANES), F32),
                   jax.ShapeDtypeStruct((1, N_Q_HEADS), F32)] + p_shapes,
        scratch_shapes=p_scr,
        compiler_params=pltpu.CompilerParams(dimension_semantics=("arbitrary",)),
    )(proj, proj, proj, sinks, dmix, *(plan.ins if plan is not None else []))
    return res[:4], res[4:]


_ZW = D_SSM
_XBC_W = D_SSM + 2 * SSD_GROUPS * SSD_N
_DT_COL = (_ZW + _XBC_W) // LANES
EVEN_IN = 3 * D_MODEL + 2 * LANES
ODD_IN = _ZW + _XBC_W + SSD_HEADS
ODD_IN_PAD = _ZW + _XBC_W + LANES


def _ssm_convprep_fwd(proj, cw, cb, dt_bias, name):
    lp = proj.shape[0]
    kk = cw.shape[0]
    tm, wc = _conv_tiles(lp, _XBC_W)
    offb = _ZW // wc
    nj = _XBC_W // wc
    hb = tm // SUBLANES

    def body(x_ref, xp_ref, dtr_ref, w_ref, b_ref, bias_ref, xc_ref, act_ref, dt_ref):
        i, j = pl.program_id(0), pl.program_id(1)
        real = _real_rows(i, tm)
        xv = x_ref[...]
        xx = jnp.concatenate([jnp.where(i > 0, xp_ref[...], 0.0), xv], axis=0)
        acc = b_ref[...] + w_ref[kk - 1:kk, :] * xv
        for m in range(1, kk):
            acc = acc + w_ref[kk - 1 - m:kk - m, :] * pltpu.roll(xx, m, 0)[SUBLANES:, :]
        xc_ref[...] = acc
        act, _ = _silu_and_grad(acc)
        act_ref[...] = jnp.where(real, act, 0.0)

        @pl.when(j == 0)
        def _():
            dt_ref[...] = jnp.where(real, _softplus(dtr_ref[...] + bias_ref[...]), 0.0)

    return pl.pallas_call(
        body, name=name, grid=(lp // tm, nj),
        in_specs=[pl.BlockSpec((tm, wc), lambda i, j: (i, offb + j)),
                  pl.BlockSpec((SUBLANES, wc), lambda i, j: (jnp.maximum(i * hb - 1, 0), offb + j)),
                  pl.BlockSpec((tm, LANES), lambda i, j: (i, _DT_COL)),
                  pl.BlockSpec((kk, wc), lambda i, j: (0, j)),
                  pl.BlockSpec((1, wc), lambda i, j: (0, j)),
                  pl.BlockSpec((1, LANES), lambda i, j: (0, 0))],
        out_specs=[pl.BlockSpec((tm, wc), lambda i, j: (i, j)),
                   pl.BlockSpec((tm, wc), lambda i, j: (i, j)),
                   pl.BlockSpec((tm, LANES), lambda i, j: (i, 0))],
        out_shape=[jax.ShapeDtypeStruct((lp, _XBC_W), F32), jax.ShapeDtypeStruct((lp, _XBC_W), F32),
                   jax.ShapeDtypeStruct((lp, LANES), F32)],
        compiler_params=pltpu.CompilerParams(dimension_semantics=("parallel", "arbitrary")),
    )(proj, proj, proj, cw, cb, dt_bias)


def _ssm_prep_bwd(xc, proj, dt_bias, dxs, dxskip, db, dc, ddt, name):
    lp = xc.shape[0]
    tm = BLOCK

    def body(r, first, xc_ref, dtr_ref, dxs_ref, dsk_ref, db_ref, dc_ref, ddt_ref, b_ref,
             dxc_ref, ddtr_ref, dbias_ref):
        real = _real_rows(r, tm)
        _, ds = _silu_and_grad(xc_ref[...])
        up = lambda ref: ref[...].astype(F32)
        dxc_ref[:, :D_SSM] = jnp.where(
            real, (up(dxs_ref) + up(dsk_ref)) * ds[:, :D_SSM], 0.0).astype(dxc_ref.dtype)
        dxc_ref[:, D_SSM:D_SSM + 1024] = jnp.where(
            real, up(db_ref) * ds[:, D_SSM:D_SSM + 1024], 0.0).astype(dxc_ref.dtype)
        dxc_ref[:, D_SSM + 1024:] = jnp.where(
            real, up(dc_ref) * ds[:, D_SSM + 1024:], 0.0).astype(dxc_ref.dtype)
        dd = jnp.where(real, ddt_ref[...] * _sigmoid(dtr_ref[...] + b_ref[...]), 0.0)
        ddtr_ref[...] = dd.astype(ddtr_ref.dtype)
        _acc_add(first, dbias_ref, jnp.sum(dd, axis=0, keepdims=True))

    return _rowcall(name, body, lp, tm,
                    rows=[(xc, _XBC_W, 0), (proj, LANES, _DT_COL), (dxs, D_SSM, 0), (dxskip, D_SSM, 0),
                          (db, 1024, 0), (dc, 1024, 0), (ddt, LANES, 0)],
                    vecs=[dt_bias], outs=[(_XBC_W, BF16), (LANES, BF16)], accs=[((1, LANES), F32)])


def _ssd_common(dt, alog):
    a = -jnp.exp(alog)
    cs = _cumsum_rows(dt * a, BLOCK)
    cst = cs.T
    cl = cs[BLOCK - 1:BLOCK, :]
    tril = (lax.broadcasted_iota(jnp.int32, (BLOCK, BLOCK), 0)
            >= lax.broadcasted_iota(jnp.int32, (BLOCK, BLOCK), 1))
    return a, cs, cst, cl, jnp.exp(cs), jnp.exp(cl - cs), jnp.exp(cl), tril


def _head_cols(ecl, g):
    lane = lax.broadcasted_iota(jnp.int32, (1, SSD_HPG * SSD_P), 1)
    e = [ecl[:, SSD_HPG * g + hh:SSD_HPG * g + hh + 1] for hh in range(SSD_HPG)]
    return jnp.where(lane < SSD_P, e[0], jnp.where(lane < 2 * SSD_P, e[1],
                                                   jnp.where(lane < 3 * SSD_P, e[2], e[3])))


def _ssd_fwd(xbc, dt, alog, name, plan=None):
    lp = xbc.shape[0]
    nc = lp // BLOCK
    gw = SSD_HPG * SSD_P
    p_in, p_shapes, p_out, p_scr = _plan_parts(plan)

    def body(*refs):
        xs_ref, b_ref, c_ref, dt_ref, alog_ref = refs[:5]
        cins = refs[5:5 + len(p_in)]
        y_ref, so_ref = refs[5 + len(p_in):7 + len(p_in)]
        couts = refs[7 + len(p_in):7 + len(p_in) + len(p_out)]
        st, fx = refs[7 + len(p_in) + len(p_out):9 + len(p_in) + len(p_out)]
        sems = refs[9 + len(p_in) + len(p_out):]
        n = pl.program_id(0)

        @pl.when(n == 0)
        def _():
            st[...] = jnp.zeros_like(st)
            if plan is not None:
                plan.start(cins, couts, sems)

        dtv = dt_ref[...]
        _, cs, cst, cl, e, f, ecl, tril = _ssd_common(dtv, alog_ref[...])
        pre = []
        for g in range(SSD_GROUPS):
            bg = b_ref[:, g * SSD_N:(g + 1) * SSD_N].astype(BF16)
            cg = c_ref[:, g * SSD_N:(g + 1) * SSD_N].astype(BF16)
            stg = st[g]
            so_ref[0, g] = stg
            pre.append((bg, stg, _dot_nt(cg, bg), _dot(cg, stg.astype(BF16))))
        for g in range(SSD_GROUPS):
            bg, stg, gm, yoff = pre[g]
            heads = [SSD_HPG * g + hh for hh in range(SSD_HPG)]
            cols = lambda v: jnp.stack([v[:, h:h + 1] for h in heads])
            x4 = jnp.stack([xs_ref[:, h * SSD_P:(h + 1) * SSD_P] for h in heads])
            csr = jnp.stack([cst[h:h + 1, :] for h in heads])
            m = gm[None] * jnp.exp(jnp.where(tril[None], cols(cs) - csr, NEG))
            xdt = x4 * cols(dtv)
            yoff4 = jnp.stack([yoff[:, hh * SSD_P:(hh + 1) * SSD_P] for hh in range(SSD_HPG)])
            y4 = (jnp.einsum("hls,hsp->hlp", m.astype(BF16), xdt.astype(BF16), preferred_element_type=F32)
                  + cols(e) * yoff4)
            fx4 = cols(f) * xdt
            for hh, h in enumerate(heads):
                y_ref[:, h * SSD_P:(h + 1) * SSD_P] = y4[hh]
                fx[:, hh * SSD_P:(hh + 1) * SSD_P] = fx4[hh]
            st[g] = stg * _head_cols(ecl, g) + _dot_tn(bg, fx[...].astype(BF16))
        if plan is not None:
            @pl.when(n == nc - 1)
            def _():
                plan.wait(cins, couts, sems)

    res = pl.pallas_call(
        body, name=name, grid=(nc,),
        in_specs=[pl.BlockSpec((BLOCK, D_SSM), lambda n: (n, 0)),
                  pl.BlockSpec((BLOCK, 1024), lambda n: (n, 2)),
                  pl.BlockSpec((BLOCK, 1024), lambda n: (n, 3)),
                  pl.BlockSpec((BLOCK, LANES), lambda n: (n, 0)),
                  pl.BlockSpec((1, LANES), lambda n: (0, 0))] + p_in,
        out_specs=[pl.BlockSpec((BLOCK, D_SSM), lambda n: (n, 0)),
                   pl.BlockSpec((1, SSD_GROUPS, SSD_N, gw), lambda n: (n, 0, 0, 0))] + p_out,
        out_shape=[jax.ShapeDtypeStruct((lp, D_SSM), F32),
                   jax.ShapeDtypeStruct((nc, SSD_GROUPS, SSD_N, gw), F32)] + p_shapes,
        scratch_shapes=[pltpu.VMEM((SSD_GROUPS, SSD_N, gw), F32), pltpu.VMEM((BLOCK, gw), F32)] + p_scr,
        compiler_params=pltpu.CompilerParams(dimension_semantics=("arbitrary",)),
    )(xbc, xbc, xbc, dt, alog, *(plan.ins if plan is not None else []))
    return res[:2], res[2:]


def _ssd_bwd(xbc, dt, alog, states, dy, name, plan=None):
    lp = xbc.shape[0]
    nc = lp // BLOCK
    gw = SSD_HPG * SSD_P
    p_in, p_shapes, p_out, p_scr = _plan_parts(plan)

    def body(*refs):
        xs_ref, b_ref, c_ref, dt_ref, alog_ref, dy_ref, st_ref = refs[:7]
        cins = refs[7:7 + len(p_in)]
        dxs_ref, db_ref, dc_ref, ddt_ref, dalog_ref = refs[7 + len(p_in):12 + len(p_in)]
        couts = refs[12 + len(p_in):12 + len(p_in) + len(p_out)]
        dst, edy, fx = refs[12 + len(p_in) + len(p_out):15 + len(p_in) + len(p_out)]
        sems = refs[15 + len(p_in) + len(p_out):]
        i = pl.program_id(0)

        @pl.when(i == 0)
        def _():
            dst[...] = jnp.zeros_like(dst)
            dalog_ref[...] = jnp.zeros_like(dalog_ref)
            if plan is not None:
                plan.start(cins, couts, sems)

        dtv = dt_ref[...]
        a, cs, cst, cl, e, f, ecl, tril = _ssd_common(dtv, alog_ref[...])
        lane = lax.broadcasted_iota(jnp.int32, (1, LANES), 1)
        sub = _row_iota(BLOCK)
        triu = (lax.broadcasted_iota(jnp.int32, (BLOCK, BLOCK), 1)
                >= lax.broadcasted_iota(jnp.int32, (BLOCK, BLOCK), 0))
        dcs = jnp.zeros((BLOCK, LANES), F32)
        dcst = jnp.zeros((LANES, BLOCK), F32)
        dcl = jnp.zeros((1, LANES), F32)
        ddtx = jnp.zeros((BLOCK, LANES), F32)
        pre = []
        for g in range(SSD_GROUPS):
            bg = b_ref[:, g * SSD_N:(g + 1) * SSD_N].astype(BF16)
            cg = c_ref[:, g * SSD_N:(g + 1) * SSD_N].astype(BF16)
            stb = st_ref[0, g].astype(BF16)
            dsob = dst[g].astype(BF16)
            pre.append((bg, cg, stb, dsob, _dot_nt(cg, bg), _dot_nt(bg, cg), _dot(cg, stb), _dot(bg, dsob)))
        for g in range(SSD_GROUPS):
            bg, cg, stb, dsob, gm, gmt, yraw, dfx = pre[g]
            dso = dst[g]
            prodsum = jnp.sum(dso * st_ref[0, g], axis=0, keepdims=True)
            heads = [SSD_HPG * g + hh for hh in range(SSD_HPG)]
            cols = lambda v: jnp.stack([v[:, h:h + 1] for h in heads])
            parts = lambda v: jnp.stack([v[:, hh * SSD_P:(hh + 1) * SSD_P] for hh in range(SSD_HPG)])
            x4 = jnp.stack([xs_ref[:, h * SSD_P:(h + 1) * SSD_P] for h in heads])
            dy4 = jnp.stack([dy_ref[:, h * SSD_P:(h + 1) * SSD_P] for h in heads])
            csc, dtc, ec, fc = cols(cs), cols(dtv), cols(e), cols(f)
            csr = jnp.stack([cst[h:h + 1, :] for h in heads])
            seg = csc - csr
            lam = jnp.exp(jnp.where(tril[None], seg, NEG))
            lamt = jnp.exp(jnp.where(triu[None], -seg, NEG))
            mt = gmt[None] * lamt
            xdt = x4 * dtc
            dyb = dy4.astype(BF16)
            dm = jnp.einsum("hlp,hsp->hls", dyb, xdt.astype(BF16), preferred_element_type=F32)
            dfx4 = parts(dfx)
            dxdt = jnp.einsum("hsl,hlp->hsp", mt.astype(BF16), dyb, preferred_element_type=F32) + fc * dfx4
            dml = dm * lam
            w = dml * gm[None]
            dgm = jnp.sum(dml, axis=0)
            dff = jnp.sum(dfx4 * xdt, axis=2, keepdims=True) * fc
            colv = (jnp.sum(w, axis=2, keepdims=True)
                    + jnp.sum(dy4 * parts(yraw), axis=2, keepdims=True) * ec - dff)
            roww = jnp.sum(w, axis=1, keepdims=True)
            ddtc = jnp.sum(dxdt * x4, axis=2, keepdims=True)
            dffs = jnp.sum(dff, axis=1, keepdims=True)
            dxs4 = dxdt * dtc
            edy4 = ec * dy4
            fx4 = fc * xdt
            for hh, h in enumerate(heads):
                ls = slice(hh * SSD_P, (hh + 1) * SSD_P)
                onl = (lane == h).astype(F32)
                dcs = dcs + colv[hh] * onl
                dcst = dcst - (sub == h).astype(F32) * roww[hh]
                dcl = dcl + (dffs[hh] + ecl[:, h:h + 1] * jnp.sum(prodsum[:, ls], axis=1, keepdims=True)) * onl
                ddtx = ddtx + ddtc[hh] * onl
                dxs_ref[:, h * SSD_P:(h + 1) * SSD_P] = dxs4[hh].astype(dxs_ref.dtype)
                edy[:, ls] = edy4[hh]
                fx[:, ls] = fx4[hh]
            edyb = edy[...].astype(BF16)
            fxb = fx[...].astype(BF16)
            dgb = dgm.astype(BF16)
            dc_ref[:, g * SSD_N:(g + 1) * SSD_N] = (_dot_nt(edyb, stb) + _dot(dgb, bg)).astype(dc_ref.dtype)
            db_ref[:, g * SSD_N:(g + 1) * SSD_N] = (_dot_nt(fxb, dsob) + _dot_tn(dgb, cg)).astype(db_ref.dtype)
            dst[g] = dso * _head_cols(ecl, g) + _dot_tn(cg, edyb)
        dcs = dcs + dcst.T + jnp.where(sub == BLOCK - 1, dcl, 0.0)
        dda = _rev_cumsum_rows(dcs, BLOCK)
        ddt_ref[...] = ddtx + dda * a
        dalog_ref[...] += jnp.sum(dda * dtv, axis=0, keepdims=True) * a
        if plan is not None:
            @pl.when(i == nc - 1)
            def _():
                plan.wait(cins, couts, sems)

    rev = lambda i: nc - 1 - i
    res = pl.pallas_call(
        body, name=name, grid=(nc,),
        in_specs=[pl.BlockSpec((BLOCK, D_SSM), lambda i: (rev(i), 0)),
                  pl.BlockSpec((BLOCK, 1024), lambda i: (rev(i), 2)),
                  pl.BlockSpec((BLOCK, 1024), lambda i: (rev(i), 3)),
                  pl.BlockSpec((BLOCK, LANES), lambda i: (rev(i), 0)),
                  pl.BlockSpec((1, LANES), lambda i: (0, 0)),
                  pl.BlockSpec((BLOCK, D_SSM), lambda i: (rev(i), 0)),
                  pl.BlockSpec((1, SSD_GROUPS, SSD_N, gw), lambda i: (rev(i), 0, 0, 0))] + p_in,
        out_specs=[pl.BlockSpec((BLOCK, D_SSM), lambda i: (rev(i), 0)),
                   pl.BlockSpec((BLOCK, 1024), lambda i: (rev(i), 0)),
                   pl.BlockSpec((BLOCK, 1024), lambda i: (rev(i), 0)),
                   pl.BlockSpec((BLOCK, LANES), lambda i: (rev(i), 0)),
                   pl.BlockSpec((1, LANES), lambda i: (0, 0))] + p_out,
        out_shape=[jax.ShapeDtypeStruct((lp, D_SSM), BF16),
                   jax.ShapeDtypeStruct((lp, 1024), BF16),
                   jax.ShapeDtypeStruct((lp, 1024), BF16),
                   jax.ShapeDtypeStruct((lp, LANES), F32),
                   jax.ShapeDtypeStruct((1, LANES), F32)] + p_shapes,
        scratch_shapes=[pltpu.VMEM((SSD_GROUPS, SSD_N, gw), F32),
                        pltpu.VMEM((BLOCK, gw), F32), pltpu.VMEM((BLOCK, gw), F32)] + p_scr,
        compiler_params=pltpu.CompilerParams(dimension_semantics=("arbitrary",)),
    )(xbc, xbc, xbc, dt, alog, dy, states, *(plan.ins if plan is not None else []))
    return res[:5], res[5:]


_GN_GROUPS = 8
_GN_W = D_SSM // _GN_GROUPS


def _ssm_gate_out_postnorm(yssd, xbc, proj, dskip, gnorm, w_out, h, g, next_pre, name):
    lp, d = h.shape
    tm = _pick(lp, (320, 256, 128))

    def body(y_ref, x_ref, z_ref, d_ref, gn_ref, w_ref, h_ref, g_ref, np_ref, yn_ref, o_ref, hn_ref, u_ref):
        sz, _ = _silu_and_grad(z_ref[...])
        y2 = (y_ref[...] + d_ref[...] * x_ref[...]) * sz
        for k in range(_GN_GROUPS):
            sl = slice(k * _GN_W, (k + 1) * _GN_W)
            yk = y2[:, sl]
            rs = lax.rsqrt(jnp.mean(yk * yk, axis=-1, keepdims=True) + EPS)
            yn_ref[:, sl] = (yk * rs * gn_ref[:, sl]).astype(yn_ref.dtype)
        acc = _dot(yn_ref[...], w_ref[...])
        o_ref[...] = acc
        rs = lax.rsqrt(jnp.mean(acc * acc, axis=-1, keepdims=True) + EPS)
        hn = jnp.where(_real_rows(pl.program_id(0), tm), h_ref[...] + acc * rs * g_ref[...], 0.0)
        hn_ref[...] = hn
        rs2 = lax.rsqrt(jnp.mean(hn * hn, axis=-1, keepdims=True) + EPS)
        u_ref[...] = (hn * rs2 * np_ref[...]).astype(u_ref.dtype)

    wide = pl.BlockSpec((tm, D_SSM), lambda i: (i, 0))
    row = pl.BlockSpec((tm, d), lambda i: (i, 0))
    vec = lambda n: pl.BlockSpec((1, n), lambda i: (0, 0))
    return pl.pallas_call(
        body, name=name, grid=(lp // tm,),
        in_specs=[wide, wide, wide, vec(D_SSM), vec(D_SSM), pl.BlockSpec((D_SSM, d), lambda i: (0, 0)),
                  row, vec(d), vec(d)],
        out_specs=[wide, row, row, row],
        out_shape=[jax.ShapeDtypeStruct((lp, D_SSM), BF16), jax.ShapeDtypeStruct((lp, d), F32),
                   jax.ShapeDtypeStruct((lp, d), F32), jax.ShapeDtypeStruct((lp, d), BF16)],
        compiler_params=pltpu.CompilerParams(dimension_semantics=("parallel",),
                                             vmem_limit_bytes=_VMEM_LIMIT_WIDE),
    )(yssd, xbc, proj, dskip, gnorm, w_out, h, g, next_pre)


def _ssm_out_gate_bwd(do, w_out, yssd, xbc, proj, dskip, gnorm, name):
    lp, d = do.shape
    tm = _pick(lp, (640, 512, 256, 128))
    tk = D_SSM // 2

    def body(do_ref, w_ref, y_ref, x_ref, z_ref, d_ref, g_ref, dy_ref, dx_ref, dz_ref, dd_ref, dg_ref):
        first = pl.program_id(1) == 0
        dyn = _dot_nt(do_ref[...], w_ref[...])
        z = z_ref[...]
        sz, dsz = _silu_and_grad(z)
        xs = x_ref[...]
        y1 = y_ref[...] + d_ref[...] * xs
        y2 = y1 * sz
        for k in range(tk // _GN_W):
            sl = slice(k * _GN_W, (k + 1) * _GN_W)
            dx, dgt = _rms_bwd(y2[:, sl], g_ref[:, sl], dyn[:, sl])
            dy1 = dx * sz[:, sl]
            dy_ref[:, sl] = dy1.astype(dy_ref.dtype)
            dx_ref[:, sl] = (dy1 * d_ref[:, sl]).astype(dx_ref.dtype)
            dz_ref[:, sl] = (dx * y1[:, sl] * dsz[:, sl]).astype(dz_ref.dtype)

            @pl.when(first)
            def _():
                dd_ref[:, sl] = jnp.zeros((1, _GN_W), F32)
                dg_ref[:, sl] = jnp.zeros((1, _GN_W), F32)

            dd_ref[:, sl] += jnp.sum(dy1 * xs[:, sl], axis=0, keepdims=True)
            dg_ref[:, sl] += jnp.sum(dgt, axis=0, keepdims=True)

    tile = pl.BlockSpec((tm, tk), lambda j, i: (i, j))
    vec = pl.BlockSpec((1, tk), lambda j, i: (0, j))
    return pl.pallas_call(
        body, name=name, grid=(D_SSM // tk, lp // tm),
        in_specs=[pl.BlockSpec((tm, d), lambda j, i: (i, 0)),
                  pl.BlockSpec((tk, d), lambda j, i: (j, 0)), tile, tile, tile, vec, vec],
        out_specs=[tile, tile, tile, vec, vec],
        out_shape=[jax.ShapeDtypeStruct((lp, D_SSM), BF16)] * 3 + [jax.ShapeDtypeStruct((1, D_SSM), F32)] * 2,
        compiler_params=pltpu.CompilerParams(dimension_semantics=("parallel", "arbitrary"),
                                             vmem_limit_bytes=_VMEM_LIMIT_WIDE),
    )(do, w_out, yssd, xbc, proj, dskip, gnorm)


def _shape2d(shape):
    n = math.prod(shape)
    if len(shape) == 2:
        return tuple(shape)
    return (n // LANES, LANES) if n % LANES == 0 else (1, n)


def _adamw_many(ws, gs, ms, vs, name):
    n = len(ws)
    c1 = 1.0 / (1.0 - ADAM_B1 ** ADAM_STEP)
    c2 = 1.0 / (1.0 - ADAM_B2 ** ADAM_STEP)

    def body(*refs):
        for i in range(n):
            w_ref, g_ref, m_ref, v_ref = (refs[j * n + i] for j in range(4))
            d_ref, nm_ref, nv_ref = (refs[(4 + j) * n + i] for j in range(3))
            gv = g_ref[...]
            nm = ADAM_B1 * m_ref[...] + (1.0 - ADAM_B1) * gv
            nv = ADAM_B2 * v_ref[...] + (1.0 - ADAM_B2) * (gv * gv)
            nm_ref[...] = nm
            nv_ref[...] = nv
            d_ref[...] = -ADAM_LR * ((nm * c1) / (jnp.sqrt(nv * c2) + ADAM_EPS) + ADAM_WD * w_ref[...])

    vm = pl.BlockSpec(memory_space=pltpu.VMEM)
    return pl.pallas_call(
        body, name=name, in_specs=[vm] * (4 * n), out_specs=[vm] * (3 * n),
        out_shape=[jax.ShapeDtypeStruct(w.shape, F32) for w in ws] * 3,
    )(*ws, *gs, *ms, *vs)


def _place():
    return lax.axis_index("x"), lax.axis_index("y"), lax.axis_index("c")


def _other_chips(x, y):
    return [(1 - x, y), (x, 1 - y), (1 - x, 1 - y)]


_ANY = pl.BlockSpec(memory_space=pl.ANY)


class _Plan:
    def __init__(self, ins, out_shapes, n_remote, n_local, issue):
        self.ins = list(ins)
        self.out_shapes = list(out_shapes)
        self.issue = issue
        self.scratch = [pltpu.SemaphoreType.DMA((max(n_remote, 1),)),
                        pltpu.SemaphoreType.DMA((max(n_remote, 1),)),
                        pltpu.SemaphoreType.DMA((max(n_local, 1),))]

    def start(self, ins, outs, sems):
        sends, _, locs = self.issue(ins, outs, *sems)
        for cp in locs + sends:
            cp.start()

    def wait(self, ins, outs, sems):
        sends, recvs, locs = self.issue(ins, outs, *sems)
        for make in recvs:
            make().wait_recv()
        for cp in sends:
            cp.wait_send()
        for cp in locs:
            cp.wait()


def _plan_parts(plan):
    if plan is None:
        return [], [], [], []
    return ([_ANY] * len(plan.ins), plan.out_shapes, [_ANY] * len(plan.out_shapes), plan.scratch)


def _run_plan(plan, name):
    n_in, n_out = len(plan.ins), len(plan.out_shapes)

    def body(*refs):
        ins, outs, sems = refs[:n_in], refs[n_in:n_in + n_out], refs[n_in + n_out:]
        plan.start(ins, outs, sems)
        plan.wait(ins, outs, sems)

    return pl.pallas_call(
        body, name=name, in_specs=[_ANY] * n_in, out_specs=[_ANY] * n_out,
        out_shape=plan.out_shapes, scratch_shapes=plan.scratch,
    )(*plan.ins)


def _gather_plan(shards):
    n = len(shards)

    def issue(ins, outs, send_sems, recv_sems, local_sems):
        x, y, c = _place()
        me = 2 * x + y
        sends, recvs, locs = [], [], []
        for p in range(n):
            locs.append(pltpu.make_async_copy(ins[p], outs[p].at[me], local_sems.at[p]))
            for k, (px, py) in enumerate(_other_chips(x, y)):
                sems = dict(send_sem=send_sems.at[3 * p + k], recv_sem=recv_sems.at[3 * p + k],
                            device_id=(px, py, c), device_id_type=MESH)
                sends.append(pltpu.make_async_remote_copy(src_ref=ins[p], dst_ref=outs[p].at[me], **sems))
                recvs.append(functools.partial(pltpu.make_async_remote_copy, src_ref=ins[p],
                                               dst_ref=outs[p].at[2 * px + py], **sems))
        return sends, recvs, locs

    return _Plan(shards, [jax.ShapeDtypeStruct((N_CHIPS,) + s.shape, s.dtype) for s in shards], 3 * n, n, issue)


_REL7 = [(fx, fy, fc) for fx in (0, 1) for fy in (0, 1) for fc in (0, 1)][1:]


def _scatter8_plan(gs):
    n = len(gs)

    def issue(ins, outs, send_sems, recv_sems, local_sems):
        x, y, c = _place()
        sends = []
        for p in range(n):
            hr = gs[p].shape[1] // 2
            for k, (fx, fy, fc) in enumerate(_REL7):
                tx, ty, tc = x ^ fx, y ^ fy, c ^ fc
                src = ins[p].at[2 * tx + ty, pl.ds(pl.multiple_of(tc * hr, SUBLANES), hr), :]
                sends.append(pltpu.make_async_remote_copy(
                    src_ref=src, dst_ref=outs[p].at[k],
                    send_sem=send_sems.at[7 * p + k], recv_sem=recv_sems.at[7 * p + k],
                    device_id=(tx, ty, tc), device_id_type=MESH))
        return sends, [functools.partial(lambda cp: cp, cp) for cp in sends], []

    shapes = [jax.ShapeDtypeStruct((7, g.shape[1] // 2, g.shape[2]), g.dtype) for g in gs]
    return _Plan(gs, shapes, 7 * n, 0, issue)


def _sibling_plan(ts):
    n = len(ts)

    def issue(ins, outs, send_sems, recv_sems, local_sems):
        x, y, c = _place()
        sends = [pltpu.make_async_remote_copy(
            src_ref=ins[p], dst_ref=outs[p], send_sem=send_sems.at[p], recv_sem=recv_sems.at[p],
            device_id=(x, y, 1 - c), device_id_type=MESH) for p in range(n)]
        return sends, [functools.partial(lambda cp: cp, cp) for cp in sends], []

    return _Plan(ts, [jax.ShapeDtypeStruct(t.shape, t.dtype) for t in ts], n, 0, issue)


def _add8(g, recv, chip, core, name):
    s, r, n = g.shape
    hr = r // 2
    th = hr // 2 if (hr // 2) % SUBLANES == 0 else hr
    nt = hr // th

    def body(chip_ref, core_ref, g_ref, r_ref, o_ref):
        acc = g_ref[0].astype(F32)
        for k in range(7):
            acc = acc + r_ref[k].astype(F32)
        o_ref[...] = acc

    return pl.pallas_call(
        body, name=name,
        grid_spec=pltpu.PrefetchScalarGridSpec(
            num_scalar_prefetch=2, grid=(nt,),
            in_specs=[pl.BlockSpec((1, th, n), lambda i, ch, co: (ch[0], co[0] * nt + i, 0)),
                      pl.BlockSpec((7, th, n), lambda i, ch, co: (0, i, 0))],
            out_specs=pl.BlockSpec((th, n), lambda i, ch, co: (i, 0))),
        out_shape=jax.ShapeDtypeStruct((hr, n), F32),
        compiler_params=pltpu.CompilerParams(dimension_semantics=("parallel",)),
    )(chip, core, g, recv)


def _adamw_halves(w, own, other, m, v, core, name):
    r, n = w.shape
    hr = r // 2
    th = hr // 2 if (hr // 2) % SUBLANES == 0 else hr
    tph = hr // th
    c1 = 1.0 / (1.0 - ADAM_B1 ** ADAM_STEP)
    c2 = 1.0 / (1.0 - ADAM_B2 ** ADAM_STEP)

    def body(core_ref, w_ref, a_ref, b_ref, m_ref, v_ref, g_ref, d_ref, nm_ref, nv_ref):
        half = pl.program_id(0) // tph
        gv = jnp.where(half == core_ref[0], a_ref[...], b_ref[...])
        nm = ADAM_B1 * m_ref[...] + (1.0 - ADAM_B1) * gv
        nv = ADAM_B2 * v_ref[...] + (1.0 - ADAM_B2) * (gv * gv)
        g_ref[...] = gv
        nm_ref[...] = nm
        nv_ref[...] = nv
        d_ref[...] = -ADAM_LR * ((nm * c1) / (jnp.sqrt(nv * c2) + ADAM_EPS) + ADAM_WD * w_ref[...])

    full = pl.BlockSpec((th, n), lambda i, co: (i, 0))
    part = pl.BlockSpec((th, n), lambda i, co: (i % tph, 0))
    return pl.pallas_call(
        body, name=name,
        grid_spec=pltpu.PrefetchScalarGridSpec(
            num_scalar_prefetch=1, grid=(2 * tph,),
            in_specs=[full, part, part, full, full], out_specs=[full] * 4),
        out_shape=[jax.ShapeDtypeStruct((r, n), F32)] * 4,
        compiler_params=pltpu.CompilerParams(dimension_semantics=("parallel",)),
    )(core, w, own, other, m, v)


def _allreduce_small(pack, name):
    r, l = pack.shape
    hr = r // 2
    assert hr % SUBLANES == 0

    def body(p_ref, o_ref, sib, chips, send_sems, recv_sems):
        x, y, c = _place()
        chip = 2 * x + y
        sibling = dict(device_id=(x, y, 1 - c), device_id_type=MESH)
        mine = pl.ds(pl.multiple_of(c * hr, SUBLANES), hr)
        other = pl.ds(pl.multiple_of((1 - c) * hr, SUBLANES), hr)
        a = pltpu.make_async_remote_copy(src_ref=p_ref.at[other], dst_ref=sib, send_sem=send_sems.at[0],
                                         recv_sem=recv_sems.at[0], **sibling)
        a.start()
        a.wait()
        own, got = p_ref[mine, :], sib[...]
        chips[chip] = jnp.where(c == 0, own, got) + jnp.where(c == 0, got, own)
        sends = []
        for k, (px, py) in enumerate(_other_chips(x, y)):
            cp = pltpu.make_async_remote_copy(
                src_ref=chips.at[chip], dst_ref=chips.at[chip], send_sem=send_sems.at[1 + k],
                recv_sem=recv_sems.at[1 + k], device_id=(px, py, c), device_id_type=MESH)
            cp.start()
            sends.append(cp)
        for k, (px, py) in enumerate(_other_chips(x, y)):
            pltpu.make_async_remote_copy(
                src_ref=chips.at[chip], dst_ref=chips.at[2 * px + py], send_sem=send_sems.at[1 + k],
                recv_sem=recv_sems.at[1 + k], device_id=(px, py, c), device_id_type=MESH).wait_recv()
        for cp in sends:
            cp.wait_send()
        o_ref[mine, :] = ((chips[0] + chips[1]) + chips[2]) + chips[3]
        fin = pltpu.make_async_remote_copy(src_ref=o_ref.at[mine], dst_ref=o_ref.at[mine],
                                           send_sem=send_sems.at[4], recv_sem=recv_sems.at[4], **sibling)
        fin.start()
        pltpu.make_async_remote_copy(src_ref=o_ref.at[mine], dst_ref=o_ref.at[other],
                                     send_sem=send_sems.at[4], recv_sem=recv_sems.at[4], **sibling).wait_recv()
        fin.wait_send()

    vm = pl.BlockSpec(memory_space=pltpu.VMEM)
    return pl.pallas_call(
        body, name=name, in_specs=[vm], out_specs=vm,
        out_shape=jax.ShapeDtypeStruct((r, l), F32),
        scratch_shapes=[pltpu.VMEM((hr, l), F32), pltpu.VMEM((N_CHIPS, hr, l), F32),
                        pltpu.SemaphoreType.DMA((5,)), pltpu.SemaphoreType.DMA((5,))],
    )(pack)


def _flat_rows(a, mult=SUBLANES * LANES):
    f = a.reshape(-1)
    padn = (-f.shape[0]) % mult
    if padn:
        f = jnp.concatenate([f, jnp.zeros((padn,), f.dtype)])
    return f


def _pack(arrs, mult=SUBLANES * LANES, total_mult=None):
    flat = [_flat_rows(a, mult) for a in arrs]
    sizes = [f.shape[0] for f in flat]
    if total_mult is not None:
        padn = (-sum(sizes)) % total_mult
        if padn:
            flat.append(jnp.zeros((padn,), flat[0].dtype))
    return jnp.concatenate(flat).reshape(-1, LANES), sizes


def _unpack(pack, shapes, sizes, lead=()):
    flat = pack.reshape(lead + (-1,))
    out, off = [], 0
    for shp, sz in zip(shapes, sizes):
        n = math.prod(shp)
        out.append(flat[..., off:off + n].reshape(lead + tuple(shp)))
        off += sz
    return out


def _cols_from_shards(g):
    s, k, n = g.shape
    return jnp.transpose(g, (1, 0, 2)).reshape(k, s * n)


def _ffn_fwd(h, u, post, w_up, cw, cb, w_down, tag, next_pre=None, plan=None, loss_tgt=None):
    hp = _mm_nn_sh(u, w_up, 2 * D_FF, f"{tag}_up")
    if plan is None:
        (act, hg, hu), pouts = _ffn_convact_fwd(hp, cw, cb, f"{tag}_convact"), ()
    else:
        (act, hg, hu), pouts = _ffn_convact_fwd(hp, cw, cb, f"{tag}_convact", plan)
    if loss_tgt is not None:
        o, hn, un = _mm_postnorm_loss(act, w_down, h, post, loss_tgt, f"{tag}_down_postnorm_loss")
    else:
        res = _mm_postnorm_res(act, w_down, h, post, f"{tag}_down_postnorm", next_pre)
        o, hn, un = res if next_pre is not None else (*res, None)
    return hn, un, (h, u, hp, hg, hu, act, o), pouts


def _ffn_bwd(dh, saved, pre, post, w_up, cw, w_down, tag):
    h, u, hp, hg, hu, act, o = saved
    do, dpost = _postnorm_bwd(o, post, dh, f"{tag}_postnorm_bwd")
    dhg, dhu = _ffn_down_act_bwd(do, w_down, hg, hu, f"{tag}_down_dx_act_bwd")
    dw_down = _mm_tn(act, do, f"{tag}_down_dw")
    dxg, dwg, dbg = _conv_bwd(hp, 0, D_FF, dhg, cw, f"{tag}_conv_bwd_gate")
    dxu, dwu, dbu = _conv_bwd(hp, D_FF, D_FF, dhu, cw, f"{tag}_conv_bwd_up", w_col_off=D_FF)
    dhp = (dxg, dxu)
    dcw = jnp.concatenate([dwg, dwu], axis=1)
    dcb = jnp.concatenate([dbg, dbu], axis=1)
    (dhn, dpre), _ = _mm_nt_sh(dhp, w_up, (h, pre, dh), f"{tag}_up_dx_prenorm_bwd")
    dw_up = _mm_tn_sh(u, dhp, w_up.shape[2], f"{tag}_up_dw")
    return dhn, dict(pre=dpre, post=dpost, w_up=dw_up, conv_w=dcw[:3], conv_b=dcb, w_down=dw_down)


class _Exchange:
    GATHER_IN_LRU = ("l0_w_out", "l0_ffn_w_down")
    GATHER_IN_ATTN = ("l0_ffn_w_up", "l1_w_out")
    GATHER_IN_FFN0 = ("l1_w_in",)
    GATHER_IN_SSD = ("l1_ffn_w_up", "l1_ffn_w_down")
    AFTER_L1_OUT = ("l1_ffn_w_up", "l1_ffn_w_down", "l1_w_out")
    IN_LRU_BWD = ("l1_w_in",)
    AFTER_L0_OUT = ("l0_ffn_w_up", "l0_ffn_w_down", "l0_w_out")
    LAST = ("l0_w_in",)

    def __init__(self, late_shards):
        self.late = dict(late_shards)
        self.slabs = {}
        self.recv = {}

    def gather_plan(self, names):
        return _gather_plan([self.late[n] for n in names])

    def gathered(self, names, outs):
        return {n: (g if n in _BIG_COL else g.reshape(-1, g.shape[-1])) for n, g in zip(names, outs)}

    def scatter_plan(self, grads, names):
        for n in names:
            g = grads[n]
            self.slabs[n] = g if n in _BIG_COL else g.reshape(N_CHIPS, -1, g.shape[-1])
        return _scatter8_plan([self.slabs[n] for n in names])

    def scattered(self, names, outs):
        self.recv.update(zip(names, outs))


def _local_step(x, tgt, meta, P, ex=None):
    seq, d = x.shape
    lp = seq + BLOCK
    h0 = jnp.concatenate([jnp.zeros((PAD, d), F32), meta, x], axis=0)
    tgt_p = jnp.concatenate([jnp.zeros((BLOCK, d), F32), tgt], axis=0)

    u0 = _rmsnorm_fwd(h0, P["l0_mix_pre_norm"], "l0_mix_prenorm")
    proj0 = _mm_nn_sh(u0, P["l0_w_in"], EVEN_IN, "l0_in")
    xrc = _conv_fwd(proj0, D_MODEL, D_MODEL, P["l0_lru_conv_w"], P["l0_lru_conv_b"], "l0_lru_conv")
    lru_args = (P["l0_lru_w_a"], P["l0_lru_w_x"], P["l0_lru_b_a"], P["l0_lru_b_x"], P["l0_lru_lambda"])
    if ex:
        (ya, hl), outs = _lru_fwd(proj0, xrc, *lru_args, "l0_lru", ex.gather_plan(ex.GATHER_IN_LRU))
        P = {**P, **ex.gathered(ex.GATHER_IN_LRU, outs)}
    else:
        ya, hl = _lru_fwd(proj0, xrc, *lru_args, "l0_lru")
    yb, outs = _attn_fwd(proj0, P["l0_attn_sinks"], "l0_attn",
                         ex.gather_plan(ex.GATHER_IN_ATTN) if ex else None)
    if ex:
        P = {**P, **ex.gathered(ex.GATHER_IN_ATTN, outs)}
    o0, h1, u1 = _mm_postnorm_res((ya, yb), P["l0_w_out"], h0, P["l0_mix_post_norm"], "l0_out_postnorm",
                                  P["l0_ffn_pre_norm"])
    h2, u2, ffn0, outs = _ffn_fwd(h1, u1, P["l0_ffn_post_norm"], P["l0_ffn_w_up"], P["l0_ffn_conv_w"],
                                  P["l0_ffn_conv_b"], P["l0_ffn_w_down"], "l0_ffn", P["l1_mix_pre_norm"],
                                  ex.gather_plan(ex.GATHER_IN_FFN0) if ex else None)
    if ex:
        P = {**P, **ex.gathered(ex.GATHER_IN_FFN0, outs)}
    proj1 = _mm_nn_sh(u2, P["l1_w_in"], ODD_IN_PAD, "l1_in")
    xc1, xbc, dt = _ssm_convprep_fwd(proj1, P["l1_ssm_conv_w"], P["l1_ssm_conv_b"], P["l1_dt_bias"],
                                     "l1_ssm_convprep")
    (yssd, states), outs = _ssd_fwd(xbc, dt, P["l1_a_log"], "l1_ssd",
                                    ex.gather_plan(ex.GATHER_IN_SSD) if ex else None)
    if ex:
        P = {**P, **ex.gathered(ex.GATHER_IN_SSD, outs)}
    yn, o1, h3, u3 = _ssm_gate_out_postnorm(
        yssd, xbc, proj1, P["l1_d_skip"], P["l1_gate_norm"], P["l1_w_out"], h2, P["l1_mix_post_norm"],
        P["l1_ffn_pre_norm"], "l1_gate_out_postnorm")
    dh4, loss_cols, ffn1, _ = _ffn_fwd(h3, u3, P["l1_ffn_post_norm"], P["l1_ffn_w_up"], P["l1_ffn_conv_w"],
                                       P["l1_ffn_conv_b"], P["l1_ffn_w_down"], "l1_ffn", loss_tgt=tgt_p)

    G = {}
    dh3, g = _ffn_bwd(dh4, ffn1, P["l1_ffn_pre_norm"], P["l1_ffn_post_norm"], P["l1_ffn_w_up"],
                      P["l1_ffn_conv_w"], P["l1_ffn_w_down"], "l1_ffn")
    for k, v in g.items():
        G["l1_ffn_" + (k + "_norm" if k in ("pre", "post") else k)] = v
    do1, G["l1_mix_post_norm"] = _postnorm_bwd(o1, P["l1_mix_post_norm"], dh3, "l1_mix_postnorm_bwd")
    G["l1_w_out"] = _mm_tn(yn, do1, "l1_out_dw")
    dyssd, dxskip, dz, dd_cols, G["l1_gate_norm"] = _ssm_out_gate_bwd(
        do1, P["l1_w_out"], yssd, xbc, proj1, P["l1_d_skip"], P["l1_gate_norm"], "l1_out_dx_gate_bwd")
    G["l1_d_skip"] = dd_cols.reshape(SSD_HEADS, SSD_P).sum(axis=1)
    (dxs, dbm, dcm, ddt, dalog), outs = _ssd_bwd(
        xbc, dt, P["l1_a_log"], states, dyssd, "l1_ssd_bwd",
        ex.scatter_plan(G, ex.AFTER_L1_OUT) if ex else None)
    if ex:
        ex.scattered(ex.AFTER_L1_OUT, outs)
    G["l1_a_log"] = dalog[0, :SSD_HEADS]
    dxc, ddtr, dbias = _ssm_prep_bwd(xc1, proj1, P["l1_dt_bias"], dxs, dxskip, dbm, dcm, ddt,
                                     "l1_ssm_prep_bwd")
    G["l1_dt_bias"] = dbias[0, :SSD_HEADS]
    dxbc, dcw, dcb = _conv_bwd(proj1, _ZW, _XBC_W, dxc, P["l1_ssm_conv_w"], "l1_ssm_conv_bwd")
    G["l1_ssm_conv_w"] = dcw[:4]
    G["l1_ssm_conv_b"] = dcb
    dproj1 = jnp.concatenate([dz, dxbc, ddtr], axis=1)
    (dh2, G["l1_mix_pre_norm"]), _ = _mm_nt_sh(dproj1, P["l1_w_in"], (h2, P["l1_mix_pre_norm"], dh3),
                                               "l1_in_dx_prenorm_bwd")
    G["l1_w_in"] = _mm_tn_sh(u2, dproj1, ODD_IN // N_CHIPS, "l1_in_dw")
    dh1, g = _ffn_bwd(dh2, ffn0, P["l0_ffn_pre_norm"], P["l0_ffn_post_norm"], P["l0_ffn_w_up"],
                      P["l0_ffn_conv_w"], P["l0_ffn_w_down"], "l0_ffn")
    for k, v in g.items():
        G["l0_ffn_" + (k + "_norm" if k in ("pre", "post") else k)] = v
    do0, G["l0_mix_post_norm"] = _postnorm_bwd(o0, P["l0_mix_post_norm"], dh1, "l0_mix_postnorm_bwd")
    dmix = _mm_nt(do0, P["l0_w_out"], "l0_out_dx")
    G["l0_w_out"] = jnp.concatenate([_mm_tn(ya, do0, "l0_out_dw_lru"), _mm_tn(yb, do0, "l0_out_dw_attn")], axis=0)
    if ex:
        lru_out, outs = _lru_bwd(proj0, xrc, hl, dmix, *lru_args, "l0_lru_bwd",
                                 ex.scatter_plan(G, ex.IN_LRU_BWD))
        ex.scattered(ex.IN_LRU_BWD, outs)
    else:
        lru_out = _lru_bwd(proj0, xrc, hl, dmix, *lru_args, "l0_lru_bwd")
    (dgate, dxrc, G["l0_lru_w_a"], G["l0_lru_w_x"], G["l0_lru_b_a"], G["l0_lru_b_x"],
     G["l0_lru_lambda"]) = lru_out
    dxr, dcw, dcb = _conv_bwd(proj0, D_MODEL, D_MODEL, dxrc, P["l0_lru_conv_w"], "l0_lru_conv_bwd")
    G["l0_lru_conv_w"] = dcw[:4]
    G["l0_lru_conv_b"] = dcb
    (dq, dk, dv, G["l0_attn_sinks"]), outs = _attn_bwd(
        proj0, P["l0_attn_sinks"], dmix, "l0_attn_bwd",
        ex.scatter_plan(G, ex.AFTER_L0_OUT) if ex else None)
    if ex:
        ex.scattered(ex.AFTER_L0_OUT, outs)
    dproj0 = jnp.concatenate([dgate, dxr, dq, dk.astype(BF16), dv.astype(BF16)], axis=1)
    G["l0_w_in"] = _mm_tn_sh(u0, dproj0, EVEN_IN // N_CHIPS, "l0_in_dw")
    (dh0, G["l0_mix_pre_norm"]), outs = _mm_nt_sh(
        dproj0, P["l0_w_in"], (h0, P["l0_mix_pre_norm"], dh1), "l0_in_dx_prenorm_bwd",
        ex.scatter_plan(G, ex.LAST) if ex else None)
    if ex:
        ex.scattered(ex.LAST, outs)
    return loss_cols, dh0[BLOCK:], dh0[PAD:BLOCK], G


_BIG_COL = ("l0_w_in", "l0_ffn_w_up", "l1_w_in", "l1_ffn_w_up")
_BIG = ("l0_w_in", "l0_w_out", "l0_ffn_w_up", "l0_ffn_w_down",
        "l1_w_in", "l1_w_out", "l1_ffn_w_up", "l1_ffn_w_down")
_SMALL_SHARDED = ("meta_tokens", "l0_lru_conv_w", "l0_ffn_conv_w", "l1_ssm_conv_w", "l1_ffn_conv_w")
_WEIGHTS = ("meta_tokens", "l0_mix_pre_norm", "l0_mix_post_norm", "l0_w_in", "l0_lru_conv_w",
            "l0_lru_conv_b", "l0_lru_w_a", "l0_lru_b_a", "l0_lru_w_x", "l0_lru_b_x", "l0_lru_lambda",
            "l0_attn_sinks", "l0_w_out", "l0_ffn_pre_norm", "l0_ffn_post_norm", "l0_ffn_w_up",
            "l0_ffn_conv_w", "l0_ffn_conv_b", "l0_ffn_w_down", "l1_mix_pre_norm", "l1_mix_post_norm",
            "l1_w_in", "l1_ssm_conv_w", "l1_ssm_conv_b", "l1_dt_bias", "l1_a_log", "l1_d_skip",
            "l1_gate_norm", "l1_w_out", "l1_ffn_pre_norm", "l1_ffn_post_norm", "l1_ffn_w_up",
            "l1_ffn_conv_w", "l1_ffn_conv_b", "l1_ffn_w_down")
_REPL = tuple(n for n in _WEIGHTS if n not in _BIG and n not in _SMALL_SHARDED)


def _pad_lanes(v, n=LANES):
    return jnp.concatenate([v, jnp.zeros((n - v.shape[0],), v.dtype)]).reshape(1, n)


def _step(x, tgt, W, M, V):
    cx, cy, cc = _place()
    chip = 2 * cx + cy

    small_pack, small_sizes = _pack([W[n] for n in _SMALL_SHARDED])
    first = _run_plan(_gather_plan([W["l0_w_in"].astype(BF16), small_pack]), "gather_first")
    small_full = _unpack(first[1], [W[n].shape for n in _SMALL_SHARDED], small_sizes, lead=(N_CHIPS,))
    ex = _Exchange({n: W[n].astype(BF16) for n in _BIG if n != "l0_w_in"})

    P = {"l0_w_in": first[0]}
    for n, g in zip(_SMALL_SHARDED, small_full):
        P[n] = _cols_from_shards(g)
    for n in _REPL:
        v = W[n]
        P[n] = v.reshape(1, -1) if v.ndim == 1 else v
    P["l0_lru_w_a"] = W["l0_lru_w_a"].astype(BF16)
    P["l0_lru_w_x"] = W["l0_lru_w_x"].astype(BF16)
    P["l1_dt_bias"] = _pad_lanes(W["l1_dt_bias"])
    P["l1_a_log"] = _pad_lanes(W["l1_a_log"])
    P["l1_d_skip"] = jnp.repeat(W["l1_d_skip"], SSD_P).reshape(1, D_SSM)
    meta = P.pop("meta_tokens")

    loss_cols, grad_x, grad_meta, G = _local_step(x, tgt, meta, P, ex)
    G["meta_tokens"] = grad_meta

    core_idx = cc.astype(jnp.int32).reshape(1)
    chip_idx = chip.astype(jnp.int32).reshape(1)
    own_half = [_add8(ex.slabs[n], ex.recv[n], chip_idx, core_idx, f"grad_sum_{n}") for n in _BIG]
    other_half = _run_plan(_sibling_plan(own_half), "grad_sibling_swap")
    small_names = list(_REPL) + list(_SMALL_SHARDED)
    small_list = [G[n] for n in small_names] + [loss_cols]
    spack, ssizes = _pack(small_list, total_mult=2 * SUBLANES * LANES)
    sred = _allreduce_small(spack, "small_allreduce")
    sfull = _unpack(sred, [a.shape for a in small_list], ssizes)
    loss = 0.5 / D_MODEL * jnp.sum(sfull[-1])
    small_grads = {}
    for n, g in zip(small_names, sfull[:-1]):
        if n in _SMALL_SHARDED:
            wcols = W[n].shape[1]
            g = lax.dynamic_slice_in_dim(g, chip * wcols, wcols, axis=1)
        small_grads[n] = g.reshape(W[n].shape)

    grads, delta, new_m, new_v = {}, {}, {}, {}
    for n, own, other in zip(_BIG, own_half, other_half):
        grads[n], delta[n], new_m[n], new_v[n] = _adamw_halves(
            W[n], own, other, M[n], V[n], core_idx, f"adamw_{n}")
    s_names = [n for n in _WEIGHTS if n not in _BIG]
    as2d = lambda a: a.reshape(_shape2d(a.shape))
    outs = _adamw_many([as2d(W[n]) for n in s_names], [as2d(small_grads[n]) for n in s_names],
                       [as2d(M[n]) for n in s_names], [as2d(V[n]) for n in s_names], "adamw_small")
    k = len(s_names)
    for i, n in enumerate(s_names):
        grads[n] = small_grads[n]
        delta[n], new_m[n], new_v[n] = (outs[j * k + i].reshape(W[n].shape) for j in range(3))
    return loss, grad_x, grads, delta, new_m, new_v


def kernel(x, meta_tokens, l0_mix_pre_norm, l0_mix_post_norm, l0_w_in, l0_lru_conv_w, l0_lru_conv_b, l0_lru_w_a, l0_lru_b_a, l0_lru_w_x, l0_lru_b_x, l0_lru_lambda, l0_attn_sinks, l0_w_out, l0_ffn_pre_norm, l0_ffn_post_norm, l0_ffn_w_up, l0_ffn_conv_w, l0_ffn_conv_b, l0_ffn_w_down, l1_mix_pre_norm, l1_mix_post_norm, l1_w_in, l1_ssm_conv_w, l1_ssm_conv_b, l1_dt_bias, l1_a_log, l1_d_skip, l1_gate_norm, l1_w_out, l1_ffn_pre_norm, l1_ffn_post_norm, l1_ffn_w_up, l1_ffn_conv_w, l1_ffn_conv_b, l1_ffn_w_down, loss_target, m_meta_tokens, m_l0_mix_pre_norm, m_l0_mix_post_norm, m_l0_w_in, m_l0_lru_conv_w, m_l0_lru_conv_b, m_l0_lru_w_a, m_l0_lru_b_a, m_l0_lru_w_x, m_l0_lru_b_x, m_l0_lru_lambda, m_l0_attn_sinks, m_l0_w_out, m_l0_ffn_pre_norm, m_l0_ffn_post_norm, m_l0_ffn_w_up, m_l0_ffn_conv_w, m_l0_ffn_conv_b, m_l0_ffn_w_down, m_l1_mix_pre_norm, m_l1_mix_post_norm, m_l1_w_in, m_l1_ssm_conv_w, m_l1_ssm_conv_b, m_l1_dt_bias, m_l1_a_log, m_l1_d_skip, m_l1_gate_norm, m_l1_w_out, m_l1_ffn_pre_norm, m_l1_ffn_post_norm, m_l1_ffn_w_up, m_l1_ffn_conv_w, m_l1_ffn_conv_b, m_l1_ffn_w_down, v_meta_tokens, v_l0_mix_pre_norm, v_l0_mix_post_norm, v_l0_w_in, v_l0_lru_conv_w, v_l0_lru_conv_b, v_l0_lru_w_a, v_l0_lru_b_a, v_l0_lru_w_x, v_l0_lru_b_x, v_l0_lru_lambda, v_l0_attn_sinks, v_l0_w_out, v_l0_ffn_pre_norm, v_l0_ffn_post_norm, v_l0_ffn_w_up, v_l0_ffn_conv_w, v_l0_ffn_conv_b, v_l0_ffn_w_down, v_l1_mix_pre_norm, v_l1_mix_post_norm, v_l1_w_in, v_l1_ssm_conv_w, v_l1_ssm_conv_b, v_l1_dt_bias, v_l1_a_log, v_l1_d_skip, v_l1_gate_norm, v_l1_w_out, v_l1_ffn_pre_norm, v_l1_ffn_post_norm, v_l1_ffn_w_up, v_l1_ffn_conv_w, v_l1_ffn_conv_b, v_l1_ffn_w_down):
    args = locals()
    W = {n: args[n] for n in _WEIGHTS}
    M = {n: args["m_" + n] for n in _WEIGHTS}
    V = {n: args["v_" + n] for n in _WEIGHTS}
    loss, grad_x, grads, delta, new_m, new_v = _step(x[0], loss_target[0], W, M, V)
    return (loss, grad_x[None], *[grads[n] for n in _WEIGHTS], *[delta[n] for n in _WEIGHTS],
            *[new_m[n] for n in _WEIGHTS], *[new_v[n] for n in _WEIGHTS])
```

```python
import functools
import math

import jax
import jax.numpy as jnp
from jax import lax
from jax.experimental import pallas as pl
from jax.experimental.pallas import tpu as pltpu

F32 = jnp.float32
BF16 = jnp.bfloat16

D_MODEL = 1024
N_META = 16
BLOCK = 128
PAD = BLOCK - N_META
EPS = 1e-6
LRU_BLOCKS = 8
LRU_BS = 128
LRU_C = 8.0
N_Q_HEADS = 16
N_KV_HEADS = 2
HEAD_DIM = 64
Q_PER_KV = 8
WINDOW = 128
D_SSM = 2048
SSD_HEADS = 32
SSD_GROUPS = 8
SSD_HPG = 4
SSD_P = 64
SSD_N = 128
D_FF = 2816
NEG = -1e30
LANES = 128
SUBLANES = 8
_VMEM_LIMIT_WIDE = 62 * 1024 * 1024

ADAM_LR = 0.001
ADAM_B1 = 0.9
ADAM_B2 = 0.999
ADAM_EPS = 1e-08
ADAM_WD = 0.01
ADAM_STEP = 10

MESH = pl.DeviceIdType.MESH
N_CHIPS = 4


def _pick(n, cands):
    for c in cands:
        if n % c == 0:
            return c
    raise ValueError(f"no tile for {n} in {cands}")


def _col_tile(n, limit=1792):
    best = None
    for t in range(LANES, min(n, limit) + 1, LANES):
        if n % t == 0:
            best = t
    if best is None:
        raise ValueError(f"no lane tile for {n}")
    return best


def _sigmoid(x):
    return 0.5 + 0.5 * jnp.tanh(0.5 * x)


def _log1p(e):
    u = 1.0 + e
    return jnp.where(u == 1.0, e, jnp.log(u) * (e / jnp.where(u == 1.0, 1.0, u - 1.0)))


def _softplus(x):
    return jnp.maximum(x, 0.0) + _log1p(jnp.exp(-jnp.abs(x)))


def _neg_expm1(x):
    poly = x * (1.0 + x * (0.5 + x * (1.0 / 6.0 + x * (1.0 / 24.0 + x * (1.0 / 120.0)))))
    return -jnp.where(x > -0.05, poly, jnp.exp(x) - 1.0)


_GELU_C = math.sqrt(2.0 / math.pi)


def _gelu(x):
    u = 0.5 + 0.5 * jnp.tanh(x * (_GELU_C + (_GELU_C * 0.044715) * (x * x)))
    return x * u


def _gelu_and_grad(x):
    x2 = x * x
    u = 0.5 + 0.5 * jnp.tanh(x * (_GELU_C + (_GELU_C * 0.044715) * x2))
    g = x * u
    dg = u * (1.0 + (x - g) * (2.0 * _GELU_C + (6.0 * 0.044715 * _GELU_C) * x2))
    return g, dg


def _silu_and_grad(x):
    s = _sigmoid(x)
    return x * s, s * (1.0 + x * (1.0 - s))


def _dot(a, b):
    return jnp.dot(a, b, preferred_element_type=F32)


def _dot_nt(a, b):
    return lax.dot_general(a, b, (((1,), (1,)), ((), ())), preferred_element_type=F32)


def _dot_tn(a, b):
    return lax.dot_general(a, b, (((0,), (0,)), ((), ())), preferred_element_type=F32)


def _row_iota(t):
    return lax.broadcasted_iota(jnp.int32, (t, 1), 0)


def _scan_fwd(a, u, t):
    row = _row_iota(t)
    d = 1
    while d < t:
        m = row >= d
        u_sh = jnp.where(m, pltpu.roll(u, d, 0), 0.0)
        a_sh = jnp.where(m, pltpu.roll(a, d, 0), 1.0)
        u = u + a * u_sh
        a = a * a_sh
        d *= 2
    return a, u


def _scan_rev(c, x, t):
    row = _row_iota(t)
    d = 1
    while d < t:
        m = row < t - d
        x_sh = jnp.where(m, pltpu.roll(x, t - d, 0), 0.0)
        c_sh = jnp.where(m, pltpu.roll(c, t - d, 0), 1.0)
        x = x + c * x_sh
        c = c * c_sh
        d *= 2
    return c, x


def _cumsum_rows(x, t):
    row = _row_iota(t)
    d = 1
    while d < t:
        x = x + jnp.where(row >= d, pltpu.roll(x, d, 0), 0.0)
        d *= 2
    return x


def _rev_cumsum_rows(x, t):
    row = _row_iota(t)
    d = 1
    while d < t:
        x = x + jnp.where(row < t - d, pltpu.roll(x, t - d, 0), 0.0)
        d *= 2
    return x


def _rms_bwd(x, g, dy):
    rs = lax.rsqrt(jnp.mean(x * x, axis=-1, keepdims=True) + EPS)
    gy = dy * g
    dx = rs * gy - x * (rs * rs * rs) * jnp.mean(x * gy, axis=-1, keepdims=True)
    return dx, dy * x * rs


def _mm_nt_postnorm_bwd(o, gain, dh, w, name):
    m, d = o.shape
    k = w.shape[0]
    tm = _pick(m, (640, 512, 256, 128))

    def body(o_ref, gain_ref, dh_ref, w_ref, y_ref, do_ref, dgain_ref):
        y_ref[...] = _dot_nt(_postnorm_bwd_tile(o_ref, gain_ref, dh_ref, do_ref, dgain_ref), w_ref[...])

    row = pl.BlockSpec((tm, d), lambda i: (i, 0))
    vec = pl.BlockSpec((1, d), lambda i: (0, 0))
    return pl.pallas_call(
        body, name=name, grid=(m // tm,),
        in_specs=[row, vec, row, pl.BlockSpec((k, d), lambda i: (0, 0))],
        out_specs=[pl.BlockSpec((tm, k), lambda i: (i, 0)), row, vec],
        out_shape=[jax.ShapeDtypeStruct((m, k), F32), jax.ShapeDtypeStruct((m, d), BF16),
                   jax.ShapeDtypeStruct((1, d), F32)],
        compiler_params=pltpu.CompilerParams(dimension_semantics=("arbitrary",)),
    )(o, gain, dh, w)


def _mm_tn(a, dy, name):
    m, k = a.shape
    n = dy.shape[1]
    tm = _pick(m, (640, 512, 256, 128))
    tk = _col_tile(k, 1408)
    tn = _col_tile(n, 1664)
    nsteps = m // tm

    def body(a_ref, dy_ref, o_ref, acc):
        @pl.when(pl.program_id(2) == 0)
        def _():
            acc[...] = jnp.zeros_like(acc)

        acc[...] += _dot_tn(a_ref[...].astype(BF16), dy_ref[...].astype(BF16))

        @pl.when(pl.program_id(2) == nsteps - 1)
        def _():
            o_ref[...] = acc[...].astype(o_ref.dtype)

    return pl.pallas_call(
        body, name=name, grid=(k // tk, n // tn, nsteps),
        in_specs=[pl.BlockSpec((tm, tk), lambda kk, j, i: (i, kk)),
                  pl.BlockSpec((tm, tn), lambda kk, j, i: (i, j))],
        out_specs=pl.BlockSpec((tk, tn), lambda kk, j, i: (kk, j)),
        out_shape=jax.ShapeDtypeStruct((k, n), BF16),
        scratch_shapes=[pltpu.VMEM((tk, tn), F32)],
        compiler_params=pltpu.CompilerParams(
            dimension_semantics=("parallel", "parallel", "arbitrary")),
    )(a, dy)


def _mm_nn_sh(a, w4, n_out, name):
    m, k = a.shape
    s, _, n = w4.shape
    tm = _pick(m, (320, 256, 128))

    def body(a_ref, w_ref, o_ref):
        av = a_ref[...].astype(BF16)
        for j in range(s):
            o_ref[:, j * n:(j + 1) * n] = _dot(av, w_ref[j])
        if n_out > s * n:
            o_ref[:, s * n:] = jnp.zeros((tm, n_out - s * n), F32)

    return pl.pallas_call(
        body, name=name, grid=(m // tm,),
        in_specs=[pl.BlockSpec((tm, k), lambda i: (i, 0)),
                  pl.BlockSpec((s, k, n), lambda i: (0, 0, 0))],
        out_specs=pl.BlockSpec((tm, n_out), lambda i: (i, 0)),
        out_shape=jax.ShapeDtypeStruct((m, n_out), F32),
        compiler_params=pltpu.CompilerParams(dimension_semantics=("parallel",)),
    )(a, w4)


def _mm_nt_sh(dy, w4, norm, name, plan=None):
    dys = dy if isinstance(dy, (tuple, list)) else (dy,)
    h, g, dres = norm
    m = dys[0].shape[0]
    s, k, n = w4.shape
    tm = _pick(m, (640, 512, 256, 128))
    where = _shard_columns(dys, s, n)
    p_in, p_shapes, p_out, p_scr = _plan_parts(plan)
    nd, ni = len(dys), m // tm

    def body(*refs):
        w_ref, h_ref, g_ref, dres_ref = refs[nd:nd + 4]
        cins = refs[nd + 4:nd + 4 + len(p_in)]
        dh_ref, dg_ref = refs[nd + 4 + len(p_in):nd + 6 + len(p_in)]
        couts = refs[nd + 6 + len(p_in):nd + 6 + len(p_in) + len(p_out)]
        sems = refs[nd + 6 + len(p_in) + len(p_out):]
        i = pl.program_id(0)
        if plan is not None:
            @pl.when(i == 0)
            def _():
                plan.start(cins, couts, sems)

        du = None
        for j, (p, c0) in enumerate(where):
            t = _dot_nt(refs[p][:, c0:c0 + n].astype(BF16), w_ref[j])
            du = t if du is None else du + t
        dx, dgt = _rms_bwd(h_ref[...], g_ref[...], du)
        dh_ref[...] = jnp.where(_real_rows(i, tm), dres_ref[...] + dx, 0.0)
        _acc_add(i == 0, dg_ref, jnp.sum(dgt, axis=0, keepdims=True))
        if plan is not None:
            @pl.when(i == ni - 1)
            def _():
                plan.wait(cins, couts, sems)

    row = pl.BlockSpec((tm, k), lambda i: (i, 0))
    vec = pl.BlockSpec((1, k), lambda i: (0, 0))
    res = pl.pallas_call(
        body, name=name, grid=(ni,),
        in_specs=[pl.BlockSpec((tm, d.shape[1]), lambda i: (i, 0)) for d in dys]
        + [pl.BlockSpec((s, k, n), lambda i: (0, 0, 0)), row, vec, row] + p_in,
        out_specs=[row, vec] + p_out,
        out_shape=[jax.ShapeDtypeStruct((m, k), F32), jax.ShapeDtypeStruct((1, k), F32)] + p_shapes,
        scratch_shapes=p_scr,
        compiler_params=pltpu.CompilerParams(dimension_semantics=("arbitrary",),
                                             vmem_limit_bytes=_VMEM_LIMIT_WIDE),
    )(*dys, w4, h, g, dres, *(plan.ins if plan is not None else []))
    return (res[0], res[1]), res[2:]


def _shard_columns(dys, s, n):
    where = []
    for p, d in enumerate(dys):
        where += [(p, c * n) for c in range(d.shape[1] // n)]
    assert len(where) >= s
    return where[:s]


def _mm_tn_sh(a, dy, n, name):
    dys = dy if isinstance(dy, (tuple, list)) else (dy,)
    m, k = a.shape
    s = N_CHIPS
    tm = _pick(m, (640, 512, 256, 128))
    tk = _col_tile(k, 512)
    nsteps = m // tm
    where = _shard_columns(dys, s, n)

    def body(*refs):
        a_ref, o_ref, acc = refs[0], refs[len(dys) + 1], refs[len(dys) + 2]

        @pl.when(pl.program_id(1) == 0)
        def _():
            acc[...] = jnp.zeros_like(acc)

        av = a_ref[...].astype(BF16)
        for j, (p, c0) in enumerate(where):
            acc[j] += _dot_tn(av, refs[1 + p][:, c0:c0 + n].astype(BF16))

        @pl.when(pl.program_id(1) == nsteps - 1)
        def _():
            o_ref[...] = acc[...].astype(o_ref.dtype)

    return pl.pallas_call(
        body, name=name, grid=(k // tk, nsteps),
        in_specs=[pl.BlockSpec((tm, tk), lambda kk, i: (i, kk))]
        + [pl.BlockSpec((tm, d.shape[1]), lambda kk, i: (i, 0)) for d in dys],
        out_specs=pl.BlockSpec((s, tk, n), lambda kk, i: (0, kk, 0)),
        out_shape=jax.ShapeDtypeStruct((s, k, n), BF16),
        scratch_shapes=[pltpu.VMEM((s, tk, n), F32)],
        compiler_params=pltpu.CompilerParams(dimension_semantics=("parallel", "arbitrary"),
                                             vmem_limit_bytes=_VMEM_LIMIT_WIDE),
    )(a, *dys)


def _rowcall(name, body, lp, tm, rows=(), prevs=(), vecs=(), outs=(), accs=(), scratch=(),
             reverse=False, seq=False, plan=None):
    p_in, p_shapes, p_out, p_scr = _plan_parts(plan)
    nt = lp // tm
    hb = tm // SUBLANES

    def ri(i):
        return nt - 1 - i if reverse else i

    in_specs, args = [], []
    for arr, w, cb in rows:
        in_specs.append(pl.BlockSpec((tm, w), lambda i, cb=cb: (ri(i), cb)))
        args.append(arr)
    for arr, w, cb in prevs:
        in_specs.append(pl.BlockSpec((SUBLANES, w), lambda i, cb=cb: (jnp.maximum(ri(i) * hb - 1, 0), cb)))
        args.append(arr)
    for arr in vecs:
        in_specs.append(pl.BlockSpec(arr.shape, lambda i, nd=arr.ndim: (0,) * nd))
        args.append(arr)
    out_shape, out_specs = [], []
    for w, dt in outs:
        out_shape.append(jax.ShapeDtypeStruct((lp, w), dt))
        out_specs.append(pl.BlockSpec((tm, w), lambda i: (ri(i), 0)))
    for shp, dt in accs:
        out_shape.append(jax.ShapeDtypeStruct(shp, dt))
        out_specs.append(pl.BlockSpec(shp, lambda i, nd=len(shp): (0,) * nd))

    n_in, n_out, n_scr = len(args), len(out_shape), len(scratch)

    def kern(*refs):
        i = pl.program_id(0)
        own = (refs[:n_in] + refs[n_in + len(p_in):n_in + len(p_in) + n_out]
               + refs[n_in + len(p_in) + n_out + len(p_out):n_in + len(p_in) + n_out + len(p_out) + n_scr])
        cins = refs[n_in:n_in + len(p_in)]
        couts = refs[n_in + len(p_in) + n_out:n_in + len(p_in) + n_out + len(p_out)]
        sems = refs[n_in + len(p_in) + n_out + len(p_out) + n_scr:]
        if plan is not None:
            @pl.when(i == 0)
            def _():
                plan.start(cins, couts, sems)

        body(ri(i), i == 0, *own)
        if plan is not None:
            @pl.when(i == nt - 1)
            def _():
                plan.wait(cins, couts, sems)

    sem = ("arbitrary",) if (seq or accs or plan is not None) else ("parallel",)
    res = pl.pallas_call(
        kern, name=name, grid=(nt,), in_specs=in_specs + p_in, out_specs=out_specs + p_out,
        out_shape=out_shape + p_shapes, scratch_shapes=list(scratch) + p_scr,
        compiler_params=pltpu.CompilerParams(dimension_semantics=sem),
    )(*args, *(plan.ins if plan is not None else []))
    return res if plan is None else (res[:n_out], res[n_out:])


def _acc_add(first, ref, val):
    @pl.when(first)
    def _():
        ref[...] = jnp.zeros_like(ref)

    ref[...] += val


def _real_rows(r, tm):
    return (r * tm + _row_iota(tm)) >= PAD


def _rmsnorm_fwd(h, g, name):
    lp, d = h.shape
    tm = _pick(lp, (640, 512, 256, 128))

    def body(r, first, h_ref, g_ref, u_ref):
        x = h_ref[...]
        rs = lax.rsqrt(jnp.mean(x * x, axis=-1, keepdims=True) + EPS)
        u_ref[...] = (x * rs * g_ref[...]).astype(u_ref.dtype)

    return _rowcall(name, body, lp, tm, rows=[(h, d, 0)], vecs=[g], outs=[(d, BF16)])[0]


def _mm_postnorm_res(a, w, h, g, name, next_pre=None):
    parts = a if isinstance(a, (tuple, list)) else (a,)
    lp, d = h.shape
    k = w.shape[0]
    tm = _pick(lp, (640, 512, 256, 128))
    offs = [sum(p.shape[1] for p in parts[:i]) for i in range(len(parts))]
    np_ = len(parts)

    def body(*refs):
        w_ref, h_ref, g_ref = refs[np_], refs[np_ + 1], refs[np_ + 2]
        rest = refs[np_ + 3:]
        acc = None
        for a_ref, p, off in zip(refs, parts, offs):
            t = _dot(a_ref[...].astype(BF16), w_ref[off:off + p.shape[1], :])
            acc = t if acc is None else acc + t
        outs = rest[1:] if next_pre is not None else rest
        outs[0][...] = acc
        rs = lax.rsqrt(jnp.mean(acc * acc, axis=-1, keepdims=True) + EPS)
        hn = jnp.where(_real_rows(pl.program_id(0), tm), h_ref[...] + acc * rs * g_ref[...], 0.0)
        outs[1][...] = hn
        if next_pre is not None:
            rs2 = lax.rsqrt(jnp.mean(hn * hn, axis=-1, keepdims=True) + EPS)
            outs[2][...] = (hn * rs2 * rest[0][...]).astype(BF16)

    row = pl.BlockSpec((tm, d), lambda i: (i, 0))
    vec = pl.BlockSpec((1, d), lambda i: (0, 0))
    n_vec = 2 if next_pre is not None else 1
    return pl.pallas_call(
        body, name=name, grid=(lp // tm,),
        in_specs=[pl.BlockSpec((tm, p.shape[1]), lambda i: (i, 0)) for p in parts]
        + [pl.BlockSpec((k, d), lambda i: (0, 0)), row] + [vec] * n_vec,
        out_specs=[row] * (2 + (next_pre is not None)),
        out_shape=[jax.ShapeDtypeStruct((lp, d), F32)] * 2
        + ([jax.ShapeDtypeStruct((lp, d), BF16)] if next_pre is not None else []),
        compiler_params=pltpu.CompilerParams(dimension_semantics=("parallel",)),
    )(*parts, w, h, g, *([next_pre] if next_pre is not None else []))


def _mm_postnorm_loss(a, w, h, g, tgt, name):
    lp, d = h.shape
    k = w.shape[0]
    tm = _pick(lp, (640, 512, 256, 128))

    def body(a_ref, w_ref, h_ref, g_ref, t_ref, o_ref, dh_ref, ls_ref):
        i = pl.program_id(0)
        acc = _dot(a_ref[...].astype(BF16), w_ref[...])
        o_ref[...] = acc
        rs = lax.rsqrt(jnp.mean(acc * acc, axis=-1, keepdims=True) + EPS)
        tok = (i * tm + _row_iota(tm)) >= BLOCK
        e = jnp.where(tok, h_ref[...] + acc * rs * g_ref[...] - t_ref[...], 0.0)
        dh_ref[...] = e * (1.0 / d)
        _acc_add(i == 0, ls_ref, jnp.sum(e * e, axis=0, keepdims=True))

    row = pl.BlockSpec((tm, d), lambda i: (i, 0))
    vec = pl.BlockSpec((1, d), lambda i: (0, 0))
    return pl.pallas_call(
        body, name=name, grid=(lp // tm,),
        in_specs=[pl.BlockSpec((tm, k), lambda i: (i, 0)), pl.BlockSpec((k, d), lambda i: (0, 0)),
                  row, vec, row],
        out_specs=[row, row, vec],
        out_shape=[jax.ShapeDtypeStruct((lp, d), F32), jax.ShapeDtypeStruct((lp, d), F32),
                   jax.ShapeDtypeStruct((1, d), F32)],
        compiler_params=pltpu.CompilerParams(dimension_semantics=("arbitrary",)),
    )(a, w, h, g, tgt)


def _conv_tiles(lp, width):
    wc = _col_tile(width, 1408)
    tm = _pick(lp, (320, 256, 128))
    return tm, wc


def _conv_fwd(x, col_off, width, w, b, name):
    lp = x.shape[0]
    kk = w.shape[0]
    tm, wc = _conv_tiles(lp, width)
    offb = col_off // wc
    assert col_off % wc == 0
    hb = tm // SUBLANES

    def body(x_ref, xp_ref, w_ref, b_ref, y_ref):
        i = pl.program_id(1)
        xv = x_ref[...]
        halo = jnp.where(i > 0, xp_ref[...], 0.0)
        xx = jnp.concatenate([halo, xv], axis=0)
        acc = b_ref[...] + w_ref[kk - 1:kk, :] * xv
        for j in range(1, kk):
            acc = acc + w_ref[kk - 1 - j:kk - j, :] * pltpu.roll(xx, j, 0)[SUBLANES:, :]
        y_ref[...] = acc

    return pl.pallas_call(
        body, name=name, grid=(width // wc, lp // tm),
        in_specs=[pl.BlockSpec((tm, wc), lambda j, i: (i, offb + j)),
                  pl.BlockSpec((SUBLANES, wc), lambda j, i: (jnp.maximum(i * hb - 1, 0), offb + j)),
                  pl.BlockSpec((kk, wc), lambda j, i: (0, j)),
                  pl.BlockSpec((1, wc), lambda j, i: (0, j))],
        out_specs=pl.BlockSpec((tm, wc), lambda j, i: (i, j)),
        out_shape=jax.ShapeDtypeStruct((lp, width), F32),
        compiler_params=pltpu.CompilerParams(dimension_semantics=("parallel", "parallel")),
    )(x, x, w, b)


def _conv_bwd(x, col_off, width, dy, w, name, w_col_off=0):
    lp = x.shape[0]
    kk = w.shape[0]
    tm, wc = _conv_tiles(lp, width)
    offb = col_off // wc
    woffb = w_col_off // wc
    assert col_off % wc == 0 and w_col_off % wc == 0
    hrows = SUBLANES * (4 // dy.dtype.itemsize)
    ext = tm + hrows

    def body(x_ref, dy_ref, dn_ref, w_ref, dx_ref, dw_ref, db_ref):
        i = pl.program_id(1)
        last = pl.num_programs(1) - 1
        xv = x_ref[...]
        dyv = dy_ref[...].astype(F32)
        dd = jnp.concatenate([dyv, jnp.where(i < last, dn_ref[...].astype(F32), 0.0)], axis=0)
        dx = w_ref[kk - 1:kk, :] * dyv
        rows = [jnp.sum(dyv * xv, axis=0, keepdims=True)]
        for m in range(1, kk):
            ahead = pltpu.roll(dd, ext - m, 0)[:tm, :]
            dx = dx + w_ref[kk - 1 - m:kk - m, :] * ahead
            rows.append(jnp.sum(ahead * xv, axis=0, keepdims=True))
        dx_ref[...] = dx.astype(dx_ref.dtype)
        dwp = jnp.concatenate(rows[::-1] + [jnp.zeros((SUBLANES - kk, wc), F32)], axis=0)

        @pl.when(i == 0)
        def _():
            dw_ref[...] = jnp.zeros_like(dw_ref)
            db_ref[...] = jnp.zeros_like(db_ref)

        dw_ref[...] += dwp
        db_ref[...] += jnp.sum(dyv, axis=0, keepdims=True)

    return pl.pallas_call(
        body, name=name, grid=(width // wc, lp // tm),
        in_specs=[pl.BlockSpec((tm, wc), lambda j, i: (i, offb + j)),
                  pl.BlockSpec((tm, wc), lambda j, i: (i, j)),
                  pl.BlockSpec((hrows, wc), lambda j, i: (jnp.minimum((i + 1) * (tm // hrows), lp // hrows - 1), j)),
                  pl.BlockSpec((kk, wc), lambda j, i: (0, woffb + j))],
        out_specs=[pl.BlockSpec((tm, wc), lambda j, i: (i, j)),
                   pl.BlockSpec((SUBLANES, wc), lambda j, i: (0, j)),
                   pl.BlockSpec((1, wc), lambda j, i: (0, j))],
        out_shape=[jax.ShapeDtypeStruct((lp, width), BF16),
                   jax.ShapeDtypeStruct((SUBLANES, width), F32),
                   jax.ShapeDtypeStruct((1, width), F32)],
        compiler_params=pltpu.CompilerParams(dimension_semantics=("parallel", "arbitrary")),
    )(x, dy, dy, w)


_FFN_K = 3
_FFN_WC = 1408


def _conv3_ext(x_ext, w_ref, b_ref):
    return (b_ref[...] + w_ref[2:3, :] * x_ext + w_ref[1:2, :] * pltpu.roll(x_ext, 1, 0)
            + w_ref[0:1, :] * pltpu.roll(x_ext, 2, 0))


def _ffn_convact_fwd(hp, cw, cb, name, plan=None):
    lp = hp.shape[0]
    tm = _pick(lp, (320, 256, 128))
    wc = _FFN_WC
    nj = D_FF // wc
    ni = lp // tm
    hb = tm // SUBLANES
    p_in, p_shapes, p_out, p_scr = _plan_parts(plan)

    def body(*refs):
        g_ref, gp_ref, u_ref, up_ref, wg_ref, wu_ref, bg_ref, bu_ref = refs[:8]
        cins = refs[8:8 + len(p_in)]
        a_ref, hg_ref, hu_ref = refs[8 + len(p_in):11 + len(p_in)]
        couts = refs[11 + len(p_in):11 + len(p_in) + len(p_out)]
        sems = refs[11 + len(p_in) + len(p_out):]
        i = pl.program_id(1)
        step = pl.program_id(0) * ni + i
        if plan is not None:
            @pl.when(step == 0)
            def _():
                plan.start(cins, couts, sems)

        def conv(x_ref, p_ref, w_ref, b_ref):
            x_ext = jnp.concatenate([jnp.where(i > 0, p_ref[...], 0.0), x_ref[...]], axis=0)
            return _conv3_ext(x_ext, w_ref, b_ref)[SUBLANES:, :]

        hg = conv(g_ref, gp_ref, wg_ref, bg_ref)
        hu = conv(u_ref, up_ref, wu_ref, bu_ref)
        a_ref[...] = (_gelu(hg) * hu).astype(a_ref.dtype)
        hg_ref[...] = hg.astype(hg_ref.dtype)
        hu_ref[...] = hu.astype(hu_ref.dtype)
        if plan is not None:
            @pl.when(step == nj * ni - 1)
            def _():
                plan.wait(cins, couts, sems)

    tile = lambda off: pl.BlockSpec((tm, wc), lambda j, i: (i, off + j))
    prev = lambda off: pl.BlockSpec((SUBLANES, wc), lambda j, i: (jnp.maximum(i * hb - 1, 0), off + j))
    vec = lambda rows, off: pl.BlockSpec((rows, wc), lambda j, i: (0, off + j))
    sem = ("arbitrary", "arbitrary") if plan is not None else ("parallel", "parallel")
    res = pl.pallas_call(
        body, name=name, grid=(nj, ni),
        in_specs=[tile(0), prev(0), tile(nj), prev(nj), vec(_FFN_K, 0), vec(_FFN_K, nj), vec(1, 0),
                  vec(1, nj)] + p_in,
        out_specs=[tile(0)] * 3 + p_out,
        out_shape=[jax.ShapeDtypeStruct((lp, D_FF), BF16)] * 3 + p_shapes,
        scratch_shapes=p_scr,
        compiler_params=pltpu.CompilerParams(dimension_semantics=sem),
    )(hp, hp, hp, hp, cw, cw, cb, cb, *(plan.ins if plan is not None else []))
    return res if plan is None else (res[:3], res[3:])


def _postnorm_bwd_tile(o_ref, gain_ref, dh_ref, do_ref, dgain_ref):
    first = pl.program_id(0) == 0
    dx, dgt = _rms_bwd(o_ref[...], gain_ref[...], dh_ref[...])
    dob = dx.astype(BF16)
    do_ref[...] = dob
    _acc_add(first, dgain_ref, jnp.sum(dgt, axis=0, keepdims=True))
    return dob


def _ffn_down_act_bwd(o, gain, dh, w_down, hg, hu, name):
    lp, d = o.shape
    tm = _pick(lp, (320, 256, 128))
    tk = _FFN_WC

    def body(o_ref, gain_ref, dh_ref, w_ref, g_ref, u_ref, dg_ref, du_ref, do_ref, dgain_ref):
        dob = _postnorm_bwd_tile(o_ref, gain_ref, dh_ref, do_ref, dgain_ref)
        for j in range(D_FF // tk):
            cs = slice(j * tk, (j + 1) * tk)
            da = _dot_nt(dob, w_ref[cs, :])
            gl, dgl = _gelu_and_grad(g_ref[:, cs].astype(F32))
            dg_ref[:, cs] = (da * u_ref[:, cs].astype(F32) * dgl).astype(dg_ref.dtype)
            du_ref[:, cs] = (da * gl).astype(du_ref.dtype)

    wide = pl.BlockSpec((tm, D_FF), lambda i: (i, 0))
    row = pl.BlockSpec((tm, d), lambda i: (i, 0))
    vec = pl.BlockSpec((1, d), lambda i: (0, 0))
    return pl.pallas_call(
        body, name=name, grid=(lp // tm,),
        in_specs=[row, vec, row, pl.BlockSpec((D_FF, d), lambda i: (0, 0)), wide, wide],
        out_specs=[wide, wide, row, vec],
        out_shape=[jax.ShapeDtypeStruct((lp, D_FF), BF16)] * 2
        + [jax.ShapeDtypeStruct((lp, d), BF16), jax.ShapeDtypeStruct((1, d), F32)],
        compiler_params=pltpu.CompilerParams(dimension_semantics=("arbitrary",),
                                             vmem_limit_bytes=_VMEM_LIMIT_WIDE),
    )(o, gain, dh, w_down, hg, hu)


def _lru_gates(x, wa_ref, wx_ref, ba, bx, lam):
    xb = x.astype(BF16)
    za, zx = [], []
    for n in range(LRU_BLOCKS):
        xs = xb[:, n * LRU_BS:(n + 1) * LRU_BS]
        za.append(_dot(xs, wa_ref[n]))
        zx.append(_dot(xs, wx_ref[n]))
    r = _sigmoid(jnp.concatenate(za, axis=1) + ba)
    ig = _sigmoid(jnp.concatenate(zx, axis=1) + bx)
    sp = _softplus(-lam)
    log_a = -LRU_C * r * sp
    a = jnp.exp(log_a)
    om = _neg_expm1(2.0 * log_a)
    mult = jnp.sqrt(om)
    return xb, r, ig, sp, a, om, mult


def _lru_fwd(proj, xrc, wa, wx, ba, bx, lam, name, plan=None):
    lp, d = xrc.shape
    tm = BLOCK

    def body(r_idx, first, gate_ref, x_ref, wa_ref, wx_ref, ba_ref, bx_ref, lam_ref,
             y_ref, h_ref, carry):
        @pl.when(first)
        def _():
            carry[...] = jnp.zeros_like(carry)

        x = x_ref[...]
        _, _, ig, _, a, _, mult = _lru_gates(x, wa_ref, wx_ref, ba_ref[...], bx_ref[...], lam_ref[...])
        u = jnp.where(_real_rows(r_idx, tm), mult * ig * x, 0.0)
        acum, hloc = _scan_fwd(a, u, tm)
        h = hloc + acum * carry[0:1, :]
        h_ref[...] = h
        carry[0:1, :] = h[tm - 1:tm, :]
        y_ref[...] = (_gelu(gate_ref[...]) * h).astype(y_ref.dtype)

    return _rowcall(name, body, lp, tm, rows=[(proj, d, 0), (xrc, d, 0)],
                    vecs=[wa, wx, ba, bx, lam], outs=[(d, BF16), (d, F32)],
                    scratch=[pltpu.VMEM((SUBLANES, d), F32)], seq=True, plan=plan)


def _lru_bwd(proj, xrc, hl, dmix, wa, wx, ba, bx, lam, name, plan=None):
    lp, d = xrc.shape
    tm = BLOCK

    def body(r_idx, first, gate_ref, x_ref, h_ref, dy_ref, hp_ref, wa_ref, wx_ref, ba_ref, bx_ref,
             lam_ref, dgate_ref, dx_ref, dwa_ref, dwx_ref, dba_ref, dbx_ref, dlam_ref, carry):
        @pl.when(first)
        def _():
            carry[...] = jnp.zeros_like(carry)
            dwa_ref[...] = jnp.zeros_like(dwa_ref)
            dwx_ref[...] = jnp.zeros_like(dwx_ref)
            dba_ref[...] = jnp.zeros_like(dba_ref)
            dbx_ref[...] = jnp.zeros_like(dbx_ref)
            dlam_ref[...] = jnp.zeros_like(dlam_ref)

        x = x_ref[...]
        lam = lam_ref[...]
        xb, r, ig, sp, a, om, mult = _lru_gates(x, wa_ref, wx_ref, ba_ref[...], bx_ref[...], lam)
        h = h_ref[...]
        dy = dy_ref[...]
        gl, dgl = _gelu_and_grad(gate_ref[...])
        dgate_ref[...] = (dy * h * dgl).astype(dgate_ref.dtype)
        row = _row_iota(tm)
        lastrow = row == tm - 1
        xg = dy * gl + jnp.where(lastrow, carry[0:1, :], 0.0)
        c = jnp.where(lastrow, 1.0, pltpu.roll(a, tm - 1, 0))
        _, g = _scan_rev(c, xg, tm)
        carry[0:1, :] = a[0:1, :] * g[0:1, :]
        hprev_in = jnp.where(r_idx > 0, hp_ref[SUBLANES - 1:SUBLANES, :], 0.0)
        hprev = jnp.where(row == 0, hprev_in, pltpu.roll(h, 1, 0))
        du = jnp.where(_real_rows(r_idx, tm), g, 0.0)
        da = g * hprev
        dmult = du * ig * x
        dig = du * mult * x
        dxv = du * mult * ig
        e2 = 1.0 - om
        dlog_a = da * a - dmult * e2 / mult
        dr = dlog_a * (-LRU_C) * sp
        dsp = jnp.sum(dlog_a * (-LRU_C) * r, axis=0, keepdims=True)
        dlam_ref[...] += -dsp * _sigmoid(-lam)
        dza = dr * r * (1.0 - r)
        dzx = dig * ig * (1.0 - ig)
        dba_ref[...] += jnp.sum(dza, axis=0, keepdims=True)
        dbx_ref[...] += jnp.sum(dzx, axis=0, keepdims=True)
        dzab = dza.astype(BF16)
        dzxb = dzx.astype(BF16)
        parts = []
        for n in range(LRU_BLOCKS):
            sl = slice(n * LRU_BS, (n + 1) * LRU_BS)
            dwa_ref[n] += _dot_tn(xb[:, sl], dzab[:, sl])
            dwx_ref[n] += _dot_tn(xb[:, sl], dzxb[:, sl])
            parts.append(_dot_nt(dzab[:, sl], wa_ref[n]) + _dot_nt(dzxb[:, sl], wx_ref[n]))
        dx_ref[...] = dxv + jnp.concatenate(parts, axis=1)

    return _rowcall(name, body, lp, tm,
                    rows=[(proj, d, 0), (xrc, d, 0), (hl, d, 0), (dmix, d, 0)],
                    prevs=[(hl, d, 0)], vecs=[wa, wx, ba, bx, lam],
                    outs=[(d, BF16), (d, F32)],
                    accs=[((LRU_BLOCKS, LRU_BS, LRU_BS), F32), ((LRU_BLOCKS, LRU_BS, LRU_BS), F32),
                          ((1, d), F32), ((1, d), F32), ((1, d), F32)],
                    scratch=[pltpu.VMEM((SUBLANES, d), F32)], reverse=True, seq=True, plan=plan)


_SLOPES = [2.0 ** (-8.0 * (h + 1) / N_Q_HEADS) for h in range(N_Q_HEADS)]
_QK_SCALE = HEAD_DIM ** -0.5
_QCOL = 2 * D_MODEL // D_MODEL
_KCOL = (3 * D_MODEL) // LANES
_VCOL = _KCOL + 1


def _attn_masks(n):
    start = pl.multiple_of(jnp.maximum(n - 1, 0) * BLOCK, BLOCK)
    qi = n * BLOCK + lax.broadcasted_iota(jnp.int32, (BLOCK, 2 * BLOCK), 0)
    kj = start + lax.broadcasted_iota(jnp.int32, (BLOCK, 2 * BLOCK), 1)
    dist = qi - kj
    ok = (kj >= BLOCK) & (dist >= 0) & (dist < WINDOW)
    dm = (n * BLOCK - PAD + lax.broadcasted_iota(jnp.int32, (BLOCK, N_META), 0)
          - lax.broadcasted_iota(jnp.int32, (BLOCK, N_META), 1))
    okm = dm >= 0
    return start, ok, dist.astype(F32), okm, jnp.minimum(dm, WINDOW).astype(F32)


def _group_rows(ref, g, scale=None):
    x = jnp.concatenate(
        [ref[:, (g * Q_PER_KV + hh) * HEAD_DIM:(g * Q_PER_KV + hh + 1) * HEAD_DIM] for hh in range(Q_PER_KV)],
        axis=0)
    return (x if scale is None else x * scale).astype(BF16)


def _attn_probs(s, sm, sink_ref, g, ok, distf, okm, dmf):
    slope = jnp.stack([jnp.full((1, 1), _SLOPES[g * Q_PER_KV + hh], F32) for hh in range(Q_PER_KV)])
    sink = jnp.stack([sink_ref[0:1, g * Q_PER_KV + hh:g * Q_PER_KV + hh + 1] for hh in range(Q_PER_KV)])
    s = s.reshape(Q_PER_KV, BLOCK, 2 * BLOCK)
    sm = sm.reshape(Q_PER_KV, BLOCK, N_META)
    s = jnp.where(ok[None], s - slope * distf[None], NEG)
    sm = jnp.where(okm[None], sm - slope * dmf[None], NEG)
    mx = jnp.maximum(jnp.maximum(jnp.max(s, axis=-1, keepdims=True),
                                 jnp.max(sm, axis=-1, keepdims=True)), sink)
    p = jnp.exp(s - mx)
    pm = jnp.exp(sm - mx)
    ps = jnp.exp(sink - mx)
    inv = 1.0 / (jnp.sum(p, axis=-1, keepdims=True) + jnp.sum(pm, axis=-1, keepdims=True) + ps)
    return p, pm, ps, inv


def _attn_fwd(proj, sinks, name, plan=None):
    lp = proj.shape[0]
    nblk = lp // BLOCK
    p_in, p_shapes, p_out, p_scr = _plan_parts(plan)

    def body(*refs):
        q_ref, k_ref, v_ref, sink_ref = refs[:4]
        cins = refs[4:4 + len(p_in)]
        o_ref = refs[4 + len(p_in)]
        couts = refs[5 + len(p_in):5 + len(p_in) + len(p_out)]
        sems = refs[5 + len(p_in) + len(p_out):]
        n = pl.program_id(0)
        if plan is not None:
            @pl.when(n == 0)
            def _():
                plan.start(cins, couts, sems)

        start, ok, distf, okm, dmf = _attn_masks(n)
        kb = k_ref[pl.ds(start, 2 * BLOCK), :].astype(BF16)
        vb = v_ref[pl.ds(start, 2 * BLOCK), :].astype(BF16)
        km = k_ref[PAD:BLOCK, :].astype(BF16)
        vm = v_ref[PAD:BLOCK, :].astype(BF16)
        rows = Q_PER_KV * BLOCK
        gsl = [slice(g * HEAD_DIM, (g + 1) * HEAD_DIM) for g in range(N_KV_HEADS)]
        raw = []
        for g in range(N_KV_HEADS):
            qg = _group_rows(q_ref, g, _QK_SCALE)
            raw.append((_dot_nt(qg, kb[:, gsl[g]]), _dot_nt(qg, km[:, gsl[g]])))
        for g in range(N_KV_HEADS):
            gs = gsl[g]
            p, pm, _, inv = _attn_probs(raw[g][0], raw[g][1], sink_ref, g, ok, distf, okm, dmf)
            o = (_dot(p.astype(BF16).reshape(rows, 2 * BLOCK), vb[:, gs])
                 + _dot(pm.astype(BF16).reshape(rows, N_META), vm[:, gs])) * inv.reshape(rows, 1)
            for hh in range(Q_PER_KV):
                h = g * Q_PER_KV + hh
                o_ref[:, h * HEAD_DIM:(h + 1) * HEAD_DIM] = o[hh * BLOCK:(hh + 1) * BLOCK, :].astype(o_ref.dtype)
        if plan is not None:
            @pl.when(n == nblk - 1)
            def _():
                plan.wait(cins, couts, sems)

    res = pl.pallas_call(
        body, name=name, grid=(nblk,),
        in_specs=[pl.BlockSpec((BLOCK, D_MODEL), lambda n: (n, _QCOL)),
                  pl.BlockSpec((lp, LANES), lambda n: (0, _KCOL)),
                  pl.BlockSpec((lp, LANES), lambda n: (0, _VCOL)),
                  pl.BlockSpec(sinks.shape, lambda n: (0, 0))] + p_in,
        out_specs=[pl.BlockSpec((BLOCK, D_MODEL), lambda n: (n, 0))] + p_out,
        out_shape=[jax.ShapeDtypeStruct((lp, D_MODEL), BF16)] + p_shapes,
        scratch_shapes=p_scr,
        compiler_params=pltpu.CompilerParams(dimension_semantics=("arbitrary",)),
    )(proj, proj, proj, sinks, *(plan.ins if plan is not None else []))
    return res[0], res[1:]


def _attn_bwd(proj, sinks, dmix, name, plan=None):
    lp = proj.shape[0]
    nblk = lp // BLOCK

    p_in, p_shapes, p_out, p_scr = _plan_parts(plan)

    def body(*refs):
        q_ref, k_ref, v_ref, sink_ref, dy_ref = refs[:5]
        cins = refs[5:5 + len(p_in)]
        dq_ref, dk_ref, dv_ref, ds_ref = refs[5 + len(p_in):9 + len(p_in)]
        couts = refs[9 + len(p_in):9 + len(p_in) + len(p_out)]
        sems = refs[9 + len(p_in) + len(p_out):]
        n = pl.program_id(0)

        @pl.when(n == 0)
        def _():
            dk_ref[...] = jnp.zeros_like(dk_ref)
            dv_ref[...] = jnp.zeros_like(dv_ref)
            ds_ref[...] = jnp.zeros_like(ds_ref)
            if plan is not None:
                plan.start(cins, couts, sems)

        start, ok, distf, okm, dmf = _attn_masks(n)
        kb = k_ref[pl.ds(start, 2 * BLOCK), :].astype(BF16)
        vb = v_ref[pl.ds(start, 2 * BLOCK), :].astype(BF16)
        km = k_ref[PAD:BLOCK, :].astype(BF16)
        vm = v_ref[PAD:BLOCK, :].astype(BF16)
        lane16 = lax.broadcasted_iota(jnp.int32, (1, N_Q_HEADS), 1)
        dsink = jnp.zeros((1, N_Q_HEADS), F32)
        rows = Q_PER_KV * BLOCK
        gsl = [slice(g * HEAD_DIM, (g + 1) * HEAD_DIM) for g in range(N_KV_HEADS)]
        pre = []
        for g in range(N_KV_HEADS):
            qg = _group_rows(q_ref, g, _QK_SCALE)
            dog = _group_rows(dy_ref, g)
            pre.append((qg, dog, _dot_nt(qg, kb[:, gsl[g]]), _dot_nt(qg, km[:, gsl[g]]),
                        _dot_nt(dog, vb[:, gsl[g]]), _dot_nt(dog, vm[:, gsl[g]])))
        for g in range(N_KV_HEADS):
            gs = gsl[g]
            qg, dog, s_raw, sm_raw, dp, dpm = pre[g]
            p, pm, ps, inv = _attn_probs(s_raw, sm_raw, sink_ref, g, ok, distf, okm, dmf)
            pn, pmn, psn = p * inv, pm * inv, ps * inv
            dp = dp.reshape(Q_PER_KV, BLOCK, 2 * BLOCK)
            dpm = dpm.reshape(Q_PER_KV, BLOCK, N_META)
            delta = (jnp.sum(pn * dp, axis=-1, keepdims=True)
                     + jnp.sum(pmn * dpm, axis=-1, keepdims=True))
            dsb = (pn * (dp - delta)).astype(BF16).reshape(rows, 2 * BLOCK)
            dsm = (pmn * (dpm - delta)).astype(BF16).reshape(rows, N_META)
            dsk = jnp.sum(psn * delta, axis=1, keepdims=True)
            for hh in range(Q_PER_KV):
                dsink = dsink - jnp.where(lane16 == g * Q_PER_KV + hh, dsk[hh], 0.0)
            dq = (_dot(dsb, kb[:, gs]) + _dot(dsm, km[:, gs])) * _QK_SCALE
            for hh in range(Q_PER_KV):
                h = g * Q_PER_KV + hh
                dq_ref[:, h * HEAD_DIM:(h + 1) * HEAD_DIM] = dq[hh * BLOCK:(hh + 1) * BLOCK, :].astype(dq_ref.dtype)
            pnb = pn.astype(BF16).reshape(rows, 2 * BLOCK)
            pmnb = pmn.astype(BF16).reshape(rows, N_META)
            dk_ref[pl.ds(start, 2 * BLOCK), gs] += _dot_tn(dsb, qg)
            dv_ref[pl.ds(start, 2 * BLOCK), gs] += _dot_tn(pnb, dog)
            dk_ref[PAD:BLOCK, gs] += _dot_tn(dsm, qg)
            dv_ref[PAD:BLOCK, gs] += _dot_tn(pmnb, dog)
        ds_ref[...] += dsink
        if plan is not None:
            @pl.when(n == nblk - 1)
            def _():
                plan.wait(cins, couts, sems)

    res = pl.pallas_call(
        body, name=name, grid=(nblk,),
        in_specs=[pl.BlockSpec((BLOCK, D_MODEL), lambda n: (n, _QCOL)),
                  pl.BlockSpec((lp, LANES), lambda n: (0, _KCOL)),
                  pl.BlockSpec((lp, LANES), lambda n: (0, _VCOL)),
                  pl.BlockSpec(sinks.shape, lambda n: (0, 0)),
                  pl.BlockSpec((BLOCK, D_MODEL), lambda n: (n, 1))] + p_in,
        out_specs=[pl.BlockSpec((BLOCK, D_MODEL), lambda n: (n, 0)),
                   pl.BlockSpec((lp, LANES), lambda n: (0, 0)),
                   pl.BlockSpec((lp, LANES), lambda n: (0, 0)),
                   pl.BlockSpec((1, N_Q_HEADS), lambda n: (0, 0))] + p_out,
        out_shape=[jax.ShapeDtypeStruct((lp, D_MODEL), BF16),
                   jax.ShapeDtypeStruct((lp, LANES), F32),
                   jax.ShapeDtypeStruct((lp, LANES), F32),
                   jax.ShapeDtypeStruct((1, N_Q_HEADS), F32)] + p_shapes,
        scratch_shapes=p_scr,
        compiler_params=pltpu.CompilerParams(dimension_semantics=("arbitrary",)),
    )(proj, proj, proj, sinks, dmix, *(plan.ins if plan is not None else []))
    return res[:4], res[4:]


_ZW = D_SSM
_XBC_W = D_SSM + 2 * SSD_GROUPS * SSD_N
_DT_COL = (_ZW + _XBC_W) // LANES
EVEN_IN = 3 * D_MODEL + 2 * LANES
ODD_IN = _ZW + _XBC_W + SSD_HEADS
ODD_IN_PAD = _ZW + _XBC_W + LANES


def _ssm_convprep_fwd(proj, cw, cb, dt_bias, name):
    lp = proj.shape[0]
    kk = cw.shape[0]
    tm, wc = _conv_tiles(lp, _XBC_W)
    offb = _ZW // wc
    nj = _XBC_W // wc
    hb = tm // SUBLANES

    def body(x_ref, xp_ref, dtr_ref, w_ref, b_ref, bias_ref, xc_ref, act_ref, dt_ref):
        i, j = pl.program_id(0), pl.program_id(1)
        real = _real_rows(i, tm)
        xv = x_ref[...]
        xx = jnp.concatenate([jnp.where(i > 0, xp_ref[...], 0.0), xv], axis=0)
        acc = b_ref[...] + w_ref[kk - 1:kk, :] * xv
        for m in range(1, kk):
            acc = acc + w_ref[kk - 1 - m:kk - m, :] * pltpu.roll(xx, m, 0)[SUBLANES:, :]
        xc_ref[...] = acc
        act, _ = _silu_and_grad(acc)
        act_ref[...] = jnp.where(real, act, 0.0)

        @pl.when(j == 0)
        def _():
            dt_ref[...] = jnp.where(real, _softplus(dtr_ref[...] + bias_ref[...]), 0.0)

    return pl.pallas_call(
        body, name=name, grid=(lp // tm, nj),
        in_specs=[pl.BlockSpec((tm, wc), lambda i, j: (i, offb + j)),
                  pl.BlockSpec((SUBLANES, wc), lambda i, j: (jnp.maximum(i * hb - 1, 0), offb + j)),
                  pl.BlockSpec((tm, LANES), lambda i, j: (i, _DT_COL)),
                  pl.BlockSpec((kk, wc), lambda i, j: (0, j)),
                  pl.BlockSpec((1, wc), lambda i, j: (0, j)),
                  pl.BlockSpec((1, LANES), lambda i, j: (0, 0))],
        out_specs=[pl.BlockSpec((tm, wc), lambda i, j: (i, j)),
                   pl.BlockSpec((tm, wc), lambda i, j: (i, j)),
                   pl.BlockSpec((tm, LANES), lambda i, j: (i, 0))],
        out_shape=[jax.ShapeDtypeStruct((lp, _XBC_W), F32), jax.ShapeDtypeStruct((lp, _XBC_W), F32),
                   jax.ShapeDtypeStruct((lp, LANES), F32)],
        compiler_params=pltpu.CompilerParams(dimension_semantics=("parallel", "arbitrary")),
    )(proj, proj, proj, cw, cb, dt_bias)


def _ssm_prep_bwd(xc, proj, dt_bias, dxs, dxskip, db, dc, ddt, name):
    lp = xc.shape[0]
    tm = BLOCK

    def body(r, first, xc_ref, dtr_ref, dxs_ref, dsk_ref, db_ref, dc_ref, ddt_ref, b_ref,
             dxc_ref, ddtr_ref, dbias_ref):
        real = _real_rows(r, tm)
        _, ds = _silu_and_grad(xc_ref[...])
        up = lambda ref: ref[...].astype(F32)
        dxc_ref[:, :D_SSM] = jnp.where(
            real, (up(dxs_ref) + up(dsk_ref)) * ds[:, :D_SSM], 0.0).astype(dxc_ref.dtype)
        dxc_ref[:, D_SSM:D_SSM + 1024] = jnp.where(
            real, up(db_ref) * ds[:, D_SSM:D_SSM + 1024], 0.0).astype(dxc_ref.dtype)
        dxc_ref[:, D_SSM + 1024:] = jnp.where(
            real, up(dc_ref) * ds[:, D_SSM + 1024:], 0.0).astype(dxc_ref.dtype)
        dd = jnp.where(real, ddt_ref[...] * _sigmoid(dtr_ref[...] + b_ref[...]), 0.0)
        ddtr_ref[...] = dd.astype(ddtr_ref.dtype)
        _acc_add(first, dbias_ref, jnp.sum(dd, axis=0, keepdims=True))

    return _rowcall(name, body, lp, tm,
                    rows=[(xc, _XBC_W, 0), (proj, LANES, _DT_COL), (dxs, D_SSM, 0), (dxskip, D_SSM, 0),
                          (db, 1024, 0), (dc, 1024, 0), (ddt, LANES, 0)],
                    vecs=[dt_bias], outs=[(_XBC_W, BF16), (LANES, BF16)], accs=[((1, LANES), F32)])


def _ssd_common(dt, alog):
    a = -jnp.exp(alog)
    cs = _cumsum_rows(dt * a, BLOCK)
    cst = cs.T
    cl = cs[BLOCK - 1:BLOCK, :]
    tril = (lax.broadcasted_iota(jnp.int32, (BLOCK, BLOCK), 0)
            >= lax.broadcasted_iota(jnp.int32, (BLOCK, BLOCK), 1))
    return a, cs, cst, cl, jnp.exp(cs), jnp.exp(cl - cs), jnp.exp(cl), tril


def _head_cols(ecl, g):
    lane = lax.broadcasted_iota(jnp.int32, (1, SSD_HPG * SSD_P), 1)
    e = [ecl[:, SSD_HPG * g + hh:SSD_HPG * g + hh + 1] for hh in range(SSD_HPG)]
    return jnp.where(lane < SSD_P, e[0], jnp.where(lane < 2 * SSD_P, e[1],
                                                   jnp.where(lane < 3 * SSD_P, e[2], e[3])))


def _ssd_fwd(xbc, dt, alog, name, plan=None):
    lp = xbc.shape[0]
    nc = lp // BLOCK
    gw = SSD_HPG * SSD_P
    p_in, p_shapes, p_out, p_scr = _plan_parts(plan)

    def body(*refs):
        xs_ref, b_ref, c_ref, dt_ref, alog_ref = refs[:5]
        cins = refs[5:5 + len(p_in)]
        y_ref, so_ref = refs[5 + len(p_in):7 + len(p_in)]
        couts = refs[7 + len(p_in):7 + len(p_in) + len(p_out)]
        st, fx = refs[7 + len(p_in) + len(p_out):9 + len(p_in) + len(p_out)]
        sems = refs[9 + len(p_in) + len(p_out):]
        n = pl.program_id(0)

        @pl.when(n == 0)
        def _():
            st[...] = jnp.zeros_like(st)
            if plan is not None:
                plan.start(cins, couts, sems)

        dtv = dt_ref[...]
        _, cs, cst, cl, e, f, ecl, tril = _ssd_common(dtv, alog_ref[...])
        pre = []
        for g in range(SSD_GROUPS):
            bg = b_ref[:, g * SSD_N:(g + 1) * SSD_N].astype(BF16)
            cg = c_ref[:, g * SSD_N:(g + 1) * SSD_N].astype(BF16)
            stg = st[g]
            so_ref[0, g] = stg
            pre.append((bg, stg, _dot_nt(cg, bg), _dot(cg, stg.astype(BF16))))
        for g in range(SSD_GROUPS):
            bg, stg, gm, yoff = pre[g]
            heads = [SSD_HPG * g + hh for hh in range(SSD_HPG)]
            cols = lambda v: jnp.stack([v[:, h:h + 1] for h in heads])
            x4 = jnp.stack([xs_ref[:, h * SSD_P:(h + 1) * SSD_P] for h in heads])
            csr = jnp.stack([cst[h:h + 1, :] for h in heads])
            m = gm[None] * jnp.exp(jnp.where(tril[None], cols(cs) - csr, NEG))
            xdt = x4 * cols(dtv)
            yoff4 = jnp.stack([yoff[:, hh * SSD_P:(hh + 1) * SSD_P] for hh in range(SSD_HPG)])
            y4 = (jnp.einsum("hls,hsp->hlp", m.astype(BF16), xdt.astype(BF16), preferred_element_type=F32)
                  + cols(e) * yoff4)
            fx4 = cols(f) * xdt
            for hh, h in enumerate(heads):
                y_ref[:, h * SSD_P:(h + 1) * SSD_P] = y4[hh]
                fx[:, hh * SSD_P:(hh + 1) * SSD_P] = fx4[hh]
            st[g] = stg * _head_cols(ecl, g) + _dot_tn(bg, fx[...].astype(BF16))
        if plan is not None:
            @pl.when(n == nc - 1)
            def _():
                plan.wait(cins, couts, sems)

    res = pl.pallas_call(
        body, name=name, grid=(nc,),
        in_specs=[pl.BlockSpec((BLOCK, D_SSM), lambda n: (n, 0)),
                  pl.BlockSpec((BLOCK, 1024), lambda n: (n, 2)),
                  pl.BlockSpec((BLOCK, 1024), lambda n: (n, 3)),
                  pl.BlockSpec((BLOCK, LANES), lambda n: (n, 0)),
                  pl.BlockSpec((1, LANES), lambda n: (0, 0))] + p_in,
        out_specs=[pl.BlockSpec((BLOCK, D_SSM), lambda n: (n, 0)),
                   pl.BlockSpec((1, SSD_GROUPS, SSD_N, gw), lambda n: (n, 0, 0, 0))] + p_out,
        out_shape=[jax.ShapeDtypeStruct((lp, D_SSM), F32),
                   jax.ShapeDtypeStruct((nc, SSD_GROUPS, SSD_N, gw), F32)] + p_shapes,
        scratch_shapes=[pltpu.VMEM((SSD_GROUPS, SSD_N, gw), F32), pltpu.VMEM((BLOCK, gw), F32)] + p_scr,
        compiler_params=pltpu.CompilerParams(dimension_semantics=("arbitrary",)),
    )(xbc, xbc, xbc, dt, alog, *(plan.ins if plan is not None else []))
    return res[:2], res[2:]


def _ssd_bwd(xbc, dt, alog, states, dy, name, plan=None):
    lp = xbc.shape[0]
    nc = lp // BLOCK
    gw = SSD_HPG * SSD_P
    p_in, p_shapes, p_out, p_scr = _plan_parts(plan)

    def body(*refs):
        xs_ref, b_ref, c_ref, dt_ref, alog_ref, dy_ref, st_ref = refs[:7]
        cins = refs[7:7 + len(p_in)]
        dxs_ref, db_ref, dc_ref, ddt_ref, dalog_ref = refs[7 + len(p_in):12 + len(p_in)]
        couts = refs[12 + len(p_in):12 + len(p_in) + len(p_out)]
        dst, edy, fx = refs[12 + len(p_in) + len(p_out):15 + len(p_in) + len(p_out)]
        sems = refs[15 + len(p_in) + len(p_out):]
        i = pl.program_id(0)

        @pl.when(i == 0)
        def _():
            dst[...] = jnp.zeros_like(dst)
            dalog_ref[...] = jnp.zeros_like(dalog_ref)
            if plan is not None:
                plan.start(cins, couts, sems)

        dtv = dt_ref[...]
        a, cs, cst, cl, e, f, ecl, tril = _ssd_common(dtv, alog_ref[...])
        lane = lax.broadcasted_iota(jnp.int32, (1, LANES), 1)
        sub = _row_iota(BLOCK)
        triu = (lax.broadcasted_iota(jnp.int32, (BLOCK, BLOCK), 1)
                >= lax.broadcasted_iota(jnp.int32, (BLOCK, BLOCK), 0))
        dcs = jnp.zeros((BLOCK, LANES), F32)
        dcst = jnp.zeros((LANES, BLOCK), F32)
        dcl = jnp.zeros((1, LANES), F32)
        ddtx = jnp.zeros((BLOCK, LANES), F32)
        pre = []
        for g in range(SSD_GROUPS):
            bg = b_ref[:, g * SSD_N:(g + 1) * SSD_N].astype(BF16)
            cg = c_ref[:, g * SSD_N:(g + 1) * SSD_N].astype(BF16)
            stb = st_ref[0, g].astype(BF16)
            dsob = dst[g].astype(BF16)
            pre.append((bg, cg, stb, dsob, _dot_nt(cg, bg), _dot_nt(bg, cg), _dot(cg, stb), _dot(bg, dsob)))
        for g in range(SSD_GROUPS):
            bg, cg, stb, dsob, gm, gmt, yraw, dfx = pre[g]
            dso = dst[g]
            prodsum = jnp.sum(dso * st_ref[0, g], axis=0, keepdims=True)
            heads = [SSD_HPG * g + hh for hh in range(SSD_HPG)]
            cols = lambda v: jnp.stack([v[:, h:h + 1] for h in heads])
            parts = lambda v: jnp.stack([v[:, hh * SSD_P:(hh + 1) * SSD_P] for hh in range(SSD_HPG)])
            x4 = jnp.stack([xs_ref[:, h * SSD_P:(h + 1) * SSD_P] for h in heads])
            dy4 = jnp.stack([dy_ref[:, h * SSD_P:(h + 1) * SSD_P] for h in heads])
            csc, dtc, ec, fc = cols(cs), cols(dtv), cols(e), cols(f)
            csr = jnp.stack([cst[h:h + 1, :] for h in heads])
            seg = csc - csr
            lam = jnp.exp(jnp.where(tril[None], seg, NEG))
            lamt = jnp.exp(jnp.where(triu[None], -seg, NEG))
            mt = gmt[None] * lamt
            xdt = x4 * dtc
            dyb = dy4.astype(BF16)
            dm = jnp.einsum("hlp,hsp->hls", dyb, xdt.astype(BF16), preferred_element_type=F32)
            dfx4 = parts(dfx)
            dxdt = jnp.einsum("hsl,hlp->hsp", mt.astype(BF16), dyb, preferred_element_type=F32) + fc * dfx4
            dml = dm * lam
            w = dml * gm[None]
            dgm = jnp.sum(dml, axis=0)
            dff = jnp.sum(dfx4 * xdt, axis=2, keepdims=True) * fc
            colv = (jnp.sum(w, axis=2, keepdims=True)
                    + jnp.sum(dy4 * parts(yraw), axis=2, keepdims=True) * ec - dff)
            roww = jnp.sum(w, axis=1, keepdims=True)
            ddtc = jnp.sum(dxdt * x4, axis=2, keepdims=True)
            dffs = jnp.sum(dff, axis=1, keepdims=True)
            dxs4 = dxdt * dtc
            edy4 = ec * dy4
            fx4 = fc * xdt
            for hh, h in enumerate(heads):
                ls = slice(hh * SSD_P, (hh + 1) * SSD_P)
                onl = (lane == h).astype(F32)
                dcs = dcs + colv[hh] * onl
                dcst = dcst - (sub == h).astype(F32) * roww[hh]
                dcl = dcl + (dffs[hh] + ecl[:, h:h + 1] * jnp.sum(prodsum[:, ls], axis=1, keepdims=True)) * onl
                ddtx = ddtx + ddtc[hh] * onl
                dxs_ref[:, h * SSD_P:(h + 1) * SSD_P] = dxs4[hh].astype(dxs_ref.dtype)
                edy[:, ls] = edy4[hh]
                fx[:, ls] = fx4[hh]
            edyb = edy[...].astype(BF16)
            fxb = fx[...].astype(BF16)
            dgb = dgm.astype(BF16)
            dc_ref[:, g * SSD_N:(g + 1) * SSD_N] = (_dot_nt(edyb, stb) + _dot(dgb, bg)).astype(dc_ref.dtype)
            db_ref[:, g * SSD_N:(g + 1) * SSD_N] = (_dot_nt(fxb, dsob) + _dot_tn(dgb, cg)).astype(db_ref.dtype)
            dst[g] = dso * _head_cols(ecl, g) + _dot_tn(cg, edyb)
        dcs = dcs + dcst.T + jnp.where(sub == BLOCK - 1, dcl, 0.0)
        dda = _rev_cumsum_rows(dcs, BLOCK)
        ddt_ref[...] = ddtx + dda * a
        dalog_ref[...] += jnp.sum(dda * dtv, axis=0, keepdims=True) * a
        if plan is not None:
            @pl.when(i == nc - 1)
            def _():
                plan.wait(cins, couts, sems)

    rev = lambda i: nc - 1 - i
    res = pl.pallas_call(
        body, name=name, grid=(nc,),
        in_specs=[pl.BlockSpec((BLOCK, D_SSM), lambda i: (rev(i), 0)),
                  pl.BlockSpec((BLOCK, 1024), lambda i: (rev(i), 2)),
                  pl.BlockSpec((BLOCK, 1024), lambda i: (rev(i), 3)),
                  pl.BlockSpec((BLOCK, LANES), lambda i: (rev(i), 0)),
                  pl.BlockSpec((1, LANES), lambda i: (0, 0)),
                  pl.BlockSpec((BLOCK, D_SSM), lambda i: (rev(i), 0)),
                  pl.BlockSpec((1, SSD_GROUPS, SSD_N, gw), lambda i: (rev(i), 0, 0, 0))] + p_in,
        out_specs=[pl.BlockSpec((BLOCK, D_SSM), lambda i: (rev(i), 0)),
                   pl.BlockSpec((BLOCK, 1024), lambda i: (rev(i), 0)),
                   pl.BlockSpec((BLOCK, 1024), lambda i: (rev(i), 0)),
                   pl.BlockSpec((BLOCK, LANES), lambda i: (rev(i), 0)),
                   pl.BlockSpec((1, LANES), lambda i: (0, 0))] + p_out,
        out_shape=[jax.ShapeDtypeStruct((lp, D_SSM), BF16),
                   jax.ShapeDtypeStruct((lp, 1024), BF16),
                   jax.ShapeDtypeStruct((lp, 1024), BF16),
                   jax.ShapeDtypeStruct((lp, LANES), F32),
                   jax.ShapeDtypeStruct((1, LANES), F32)] + p_shapes,
        scratch_shapes=[pltpu.VMEM((SSD_GROUPS, SSD_N, gw), F32),
                        pltpu.VMEM((BLOCK, gw), F32), pltpu.VMEM((BLOCK, gw), F32)] + p_scr,
        compiler_params=pltpu.CompilerParams(dimension_semantics=("arbitrary",)),
    )(xbc, xbc, xbc, dt, alog, dy, states, *(plan.ins if plan is not None else []))
    return res[:5], res[5:]


_GN_GROUPS = 8
_GN_W = D_SSM // _GN_GROUPS


def _ssm_gate_out_postnorm(yssd, xbc, proj, dskip, gnorm, w_out, h, g, next_pre, name):
    lp, d = h.shape
    tm = _pick(lp, (320, 256, 128))

    def body(y_ref, x_ref, z_ref, d_ref, gn_ref, w_ref, h_ref, g_ref, np_ref, yn_ref, o_ref, hn_ref, u_ref):
        sz, _ = _silu_and_grad(z_ref[...])
        y2 = (y_ref[...] + d_ref[...] * x_ref[...]) * sz
        for k in range(_GN_GROUPS):
            sl = slice(k * _GN_W, (k + 1) * _GN_W)
            yk = y2[:, sl]
            rs = lax.rsqrt(jnp.mean(yk * yk, axis=-1, keepdims=True) + EPS)
            yn_ref[:, sl] = (yk * rs * gn_ref[:, sl]).astype(yn_ref.dtype)
        acc = _dot(yn_ref[...], w_ref[...])
        o_ref[...] = acc
        rs = lax.rsqrt(jnp.mean(acc * acc, axis=-1, keepdims=True) + EPS)
        hn = jnp.where(_real_rows(pl.program_id(0), tm), h_ref[...] + acc * rs * g_ref[...], 0.0)
        hn_ref[...] = hn
        rs2 = lax.rsqrt(jnp.mean(hn * hn, axis=-1, keepdims=True) + EPS)
        u_ref[...] = (hn * rs2 * np_ref[...]).astype(u_ref.dtype)

    wide = pl.BlockSpec((tm, D_SSM), lambda i: (i, 0))
    row = pl.BlockSpec((tm, d), lambda i: (i, 0))
    vec = lambda n: pl.BlockSpec((1, n), lambda i: (0, 0))
    return pl.pallas_call(
        body, name=name, grid=(lp // tm,),
        in_specs=[wide, wide, wide, vec(D_SSM), vec(D_SSM), pl.BlockSpec((D_SSM, d), lambda i: (0, 0)),
                  row, vec(d), vec(d)],
        out_specs=[wide, row, row, row],
        out_shape=[jax.ShapeDtypeStruct((lp, D_SSM), BF16), jax.ShapeDtypeStruct((lp, d), F32),
                   jax.ShapeDtypeStruct((lp, d), F32), jax.ShapeDtypeStruct((lp, d), BF16)],
        compiler_params=pltpu.CompilerParams(dimension_semantics=("parallel",),
                                             vmem_limit_bytes=_VMEM_LIMIT_WIDE),
    )(yssd, xbc, proj, dskip, gnorm, w_out, h, g, next_pre)


def _ssm_out_gate_bwd(o, gain, dh, w_out, yssd, xbc, proj, dskip, gnorm, name):
    lp, d = o.shape
    tm = _pick(lp, (320, 256, 128))
    tk = D_SSM // 2

    def body(o_ref, gain_ref, dh_ref, w_ref, y_ref, x_ref, z_ref, d_ref, g_ref,
             dy_ref, dx_ref, dz_ref, dd_ref, dg_ref, do_ref, dgain_ref):
        first = pl.program_id(0) == 0
        dob = _postnorm_bwd_tile(o_ref, gain_ref, dh_ref, do_ref, dgain_ref)
        z = z_ref[...]
        sz, dsz = _silu_and_grad(z)
        xs = x_ref[...]
        y1 = y_ref[...] + d_ref[...] * xs
        y2 = y1 * sz
        for k in range(_GN_GROUPS):
            sl = slice(k * _GN_W, (k + 1) * _GN_W)
            if k % (tk // _GN_W) == 0:
                dyn = _dot_nt(dob, w_ref[k * _GN_W:k * _GN_W + tk, :])
            loc = slice((k % (tk // _GN_W)) * _GN_W, (k % (tk // _GN_W) + 1) * _GN_W)
            dx, dgt = _rms_bwd(y2[:, sl], g_ref[:, sl], dyn[:, loc])
            dy1 = dx * sz[:, sl]
            dy_ref[:, sl] = dy1.astype(dy_ref.dtype)
            dx_ref[:, sl] = (dy1 * d_ref[:, sl]).astype(dx_ref.dtype)
            dz_ref[:, sl] = (dx * y1[:, sl] * dsz[:, sl]).astype(dz_ref.dtype)

            @pl.when(first)
            def _():
                dd_ref[:, sl] = jnp.zeros((1, _GN_W), F32)
                dg_ref[:, sl] = jnp.zeros((1, _GN_W), F32)

            dd_ref[:, sl] += jnp.sum(dy1 * xs[:, sl], axis=0, keepdims=True)
            dg_ref[:, sl] += jnp.sum(dgt, axis=0, keepdims=True)

    tile = pl.BlockSpec((tm, D_SSM), lambda i: (i, 0))
    vec = pl.BlockSpec((1, D_SSM), lambda i: (0, 0))
    row = pl.BlockSpec((tm, d), lambda i: (i, 0))
    rvec = pl.BlockSpec((1, d), lambda i: (0, 0))
    return pl.pallas_call(
        body, name=name, grid=(lp // tm,),
        in_specs=[row, rvec, row, pl.BlockSpec((D_SSM, d), lambda i: (0, 0)), tile, tile, tile, vec, vec],
        out_specs=[tile, tile, tile, vec, vec, row, rvec],
        out_shape=[jax.ShapeDtypeStruct((lp, D_SSM), BF16)] * 3 + [jax.ShapeDtypeStruct((1, D_SSM), F32)] * 2
        + [jax.ShapeDtypeStruct((lp, d), BF16), jax.ShapeDtypeStruct((1, d), F32)],
        compiler_params=pltpu.CompilerParams(dimension_semantics=("arbitrary",),
                                             vmem_limit_bytes=_VMEM_LIMIT_WIDE),
    )(o, gain, dh, w_out, yssd, xbc, proj, dskip, gnorm)


def _shape2d(shape):
    n = math.prod(shape)
    if len(shape) == 2:
        return tuple(shape)
    return (n // LANES, LANES) if n % LANES == 0 else (1, n)


def _adamw_many(ws, gs, ms, vs, name):
    n = len(ws)
    c1 = 1.0 / (1.0 - ADAM_B1 ** ADAM_STEP)
    c2 = 1.0 / (1.0 - ADAM_B2 ** ADAM_STEP)

    def body(*refs):
        for i in range(n):
            w_ref, g_ref, m_ref, v_ref = (refs[j * n + i] for j in range(4))
            d_ref, nm_ref, nv_ref = (refs[(4 + j) * n + i] for j in range(3))
            gv = g_ref[...]
            nm = ADAM_B1 * m_ref[...] + (1.0 - ADAM_B1) * gv
            nv = ADAM_B2 * v_ref[...] + (1.0 - ADAM_B2) * (gv * gv)
            nm_ref[...] = nm
            nv_ref[...] = nv
            d_ref[...] = -ADAM_LR * ((nm * c1) / (jnp.sqrt(nv * c2) + ADAM_EPS) + ADAM_WD * w_ref[...])

    vm = pl.BlockSpec(memory_space=pltpu.VMEM)
    return pl.pallas_call(
        body, name=name, in_specs=[vm] * (4 * n), out_specs=[vm] * (3 * n),
        out_shape=[jax.ShapeDtypeStruct(w.shape, F32) for w in ws] * 3,
    )(*ws, *gs, *ms, *vs)


def _place():
    return lax.axis_index("x"), lax.axis_index("y"), lax.axis_index("c")


def _other_chips(x, y):
    return [(1 - x, y), (x, 1 - y), (1 - x, 1 - y)]


_ANY = pl.BlockSpec(memory_space=pl.ANY)


class _Plan:
    def __init__(self, ins, out_shapes, n_remote, n_local, issue):
        self.ins = list(ins)
        self.out_shapes = list(out_shapes)
        self.issue = issue
        self.scratch = [pltpu.SemaphoreType.DMA((max(n_remote, 1),)),
                        pltpu.SemaphoreType.DMA((max(n_remote, 1),)),
                        pltpu.SemaphoreType.DMA((max(n_local, 1),))]

    def start(self, ins, outs, sems):
        sends, _, locs = self.issue(ins, outs, *sems)
        for cp in locs + sends:
            cp.start()

    def wait(self, ins, outs, sems):
        sends, recvs, locs = self.issue(ins, outs, *sems)
        for make in recvs:
            make().wait_recv()
        for cp in sends:
            cp.wait_send()
        for cp in locs:
            cp.wait()


def _plan_parts(plan):
    if plan is None:
        return [], [], [], []
    return ([_ANY] * len(plan.ins), plan.out_shapes, [_ANY] * len(plan.out_shapes), plan.scratch)


def _run_plan(plan, name):
    n_in, n_out = len(plan.ins), len(plan.out_shapes)

    def body(*refs):
        ins, outs, sems = refs[:n_in], refs[n_in:n_in + n_out], refs[n_in + n_out:]
        plan.start(ins, outs, sems)
        plan.wait(ins, outs, sems)

    return pl.pallas_call(
        body, name=name, in_specs=[_ANY] * n_in, out_specs=[_ANY] * n_out,
        out_shape=plan.out_shapes, scratch_shapes=plan.scratch,
    )(*plan.ins)


def _gather_plan(shards):
    n = len(shards)

    def issue(ins, outs, send_sems, recv_sems, local_sems):
        x, y, c = _place()
        me = 2 * x + y
        sends, recvs, locs = [], [], []
        for p in range(n):
            locs.append(pltpu.make_async_copy(ins[p], outs[p].at[me], local_sems.at[p]))
            for k, (px, py) in enumerate(_other_chips(x, y)):
                sems = dict(send_sem=send_sems.at[3 * p + k], recv_sem=recv_sems.at[3 * p + k],
                            device_id=(px, py, c), device_id_type=MESH)
                sends.append(pltpu.make_async_remote_copy(src_ref=ins[p], dst_ref=outs[p].at[me], **sems))
                recvs.append(functools.partial(pltpu.make_async_remote_copy, src_ref=ins[p],
                                               dst_ref=outs[p].at[2 * px + py], **sems))
        return sends, recvs, locs

    return _Plan(shards, [jax.ShapeDtypeStruct((N_CHIPS,) + s.shape, s.dtype) for s in shards], 3 * n, n, issue)


_REL7 = [(fx, fy, fc) for fx in (0, 1) for fy in (0, 1) for fc in (0, 1)][1:]


def _scatter8_plan(gs):
    n = len(gs)

    def issue(ins, outs, send_sems, recv_sems, local_sems):
        x, y, c = _place()
        sends = []
        for p in range(n):
            hr = gs[p].shape[1] // 2
            for k, (fx, fy, fc) in enumerate(_REL7):
                tx, ty, tc = x ^ fx, y ^ fy, c ^ fc
                src = ins[p].at[2 * tx + ty, pl.ds(pl.multiple_of(tc * hr, SUBLANES), hr), :]
                sends.append(pltpu.make_async_remote_copy(
                    src_ref=src, dst_ref=outs[p].at[k],
                    send_sem=send_sems.at[7 * p + k], recv_sem=recv_sems.at[7 * p + k],
                    device_id=(tx, ty, tc), device_id_type=MESH))
        return sends, [functools.partial(lambda cp: cp, cp) for cp in sends], []

    shapes = [jax.ShapeDtypeStruct((7, g.shape[1] // 2, g.shape[2]), g.dtype) for g in gs]
    return _Plan(gs, shapes, 7 * n, 0, issue)


def _sibling_plan(ts):
    n = len(ts)

    def issue(ins, outs, send_sems, recv_sems, local_sems):
        x, y, c = _place()
        sends = [pltpu.make_async_remote_copy(
            src_ref=ins[p], dst_ref=outs[p], send_sem=send_sems.at[p], recv_sem=recv_sems.at[p],
            device_id=(x, y, 1 - c), device_id_type=MESH) for p in range(n)]
        return sends, [functools.partial(lambda cp: cp, cp) for cp in sends], []

    return _Plan(ts, [jax.ShapeDtypeStruct(t.shape, t.dtype) for t in ts], n, 0, issue)


def _add8(g, recv, chip, core, name):
    s, r, n = g.shape
    hr = r // 2
    th = hr // 2 if (hr // 2) % SUBLANES == 0 else hr
    nt = hr // th

    def body(chip_ref, core_ref, g_ref, r_ref, o_ref):
        acc = g_ref[0].astype(F32)
        for k in range(7):
            acc = acc + r_ref[k].astype(F32)
        o_ref[...] = acc

    return pl.pallas_call(
        body, name=name,
        grid_spec=pltpu.PrefetchScalarGridSpec(
            num_scalar_prefetch=2, grid=(nt,),
            in_specs=[pl.BlockSpec((1, th, n), lambda i, ch, co: (ch[0], co[0] * nt + i, 0)),
                      pl.BlockSpec((7, th, n), lambda i, ch, co: (0, i, 0))],
            out_specs=pl.BlockSpec((th, n), lambda i, ch, co: (i, 0))),
        out_shape=jax.ShapeDtypeStruct((hr, n), F32),
        compiler_params=pltpu.CompilerParams(dimension_semantics=("parallel",)),
    )(chip, core, g, recv)


def _adamw_halves(w, own, other, m, v, core, name):
    r, n = w.shape
    hr = r // 2
    th = hr // 2 if (hr // 2) % SUBLANES == 0 else hr
    tph = hr // th
    c1 = 1.0 / (1.0 - ADAM_B1 ** ADAM_STEP)
    c2 = 1.0 / (1.0 - ADAM_B2 ** ADAM_STEP)

    def body(core_ref, w_ref, a_ref, b_ref, m_ref, v_ref, g_ref, d_ref, nm_ref, nv_ref):
        half = pl.program_id(0) // tph
        gv = jnp.where(half == core_ref[0], a_ref[...], b_ref[...])
        nm = ADAM_B1 * m_ref[...] + (1.0 - ADAM_B1) * gv
        nv = ADAM_B2 * v_ref[...] + (1.0 - ADAM_B2) * (gv * gv)
        g_ref[...] = gv
        nm_ref[...] = nm
        nv_ref[...] = nv
        d_ref[...] = -ADAM_LR * ((nm * c1) / (jnp.sqrt(nv * c2) + ADAM_EPS) + ADAM_WD * w_ref[...])

    full = pl.BlockSpec((th, n), lambda i, co: (i, 0))
    part = pl.BlockSpec((th, n), lambda i, co: (i % tph, 0))
    return pl.pallas_call(
        body, name=name,
        grid_spec=pltpu.PrefetchScalarGridSpec(
            num_scalar_prefetch=1, grid=(2 * tph,),
            in_specs=[full, part, part, full, full], out_specs=[full] * 4),
        out_shape=[jax.ShapeDtypeStruct((r, n), F32)] * 4,
        compiler_params=pltpu.CompilerParams(dimension_semantics=("parallel",)),
    )(core, w, own, other, m, v)


def _allreduce_small(pack, name):
    r, l = pack.shape
    hr = r // 2
    assert hr % SUBLANES == 0

    def body(p_ref, o_ref, sib, chips, send_sems, recv_sems):
        x, y, c = _place()
        chip = 2 * x + y
        sibling = dict(device_id=(x, y, 1 - c), device_id_type=MESH)
        mine = pl.ds(pl.multiple_of(c * hr, SUBLANES), hr)
        other = pl.ds(pl.multiple_of((1 - c) * hr, SUBLANES), hr)
        a = pltpu.make_async_remote_copy(src_ref=p_ref.at[other], dst_ref=sib, send_sem=send_sems.at[0],
                                         recv_sem=recv_sems.at[0], **sibling)
        a.start()
        a.wait()
        own, got = p_ref[mine, :], sib[...]
        chips[chip] = jnp.where(c == 0, own, got) + jnp.where(c == 0, got, own)
        sends = []
        for k, (px, py) in enumerate(_other_chips(x, y)):
            cp = pltpu.make_async_remote_copy(
                src_ref=chips.at[chip], dst_ref=chips.at[chip], send_sem=send_sems.at[1 + k],
                recv_sem=recv_sems.at[1 + k], device_id=(px, py, c), device_id_type=MESH)
            cp.start()
            sends.append(cp)
        for k, (px, py) in enumerate(_other_chips(x, y)):
            pltpu.make_async_remote_copy(
                src_ref=chips.at[chip], dst_ref=chips.at[2 * px + py], send_sem=send_sems.at[1 + k],
                recv_sem=recv_sems.at[1 + k], device_id=(px, py, c), device_id_type=MESH).wait_recv()
        for cp in sends:
            cp.wait_send()
        o_ref[mine, :] = ((chips[0] + chips[1]) + chips[2]) + chips[3]
        fin = pltpu.make_async_remote_copy(src_ref=o_ref.at[mine], dst_ref=o_ref.at[mine],
                                           send_sem=send_sems.at[4], recv_sem=recv_sems.at[4], **sibling)
        fin.start()
        pltpu.make_async_remote_copy(src_ref=o_ref.at[mine], dst_ref=o_ref.at[other],
                                     send_sem=send_sems.at[4], recv_sem=recv_sems.at[4], **sibling).wait_recv()
        fin.wait_send()

    vm = pl.BlockSpec(memory_space=pltpu.VMEM)
    return pl.pallas_call(
        body, name=name, in_specs=[vm], out_specs=vm,
        out_shape=jax.ShapeDtypeStruct((r, l), F32),
        scratch_shapes=[pltpu.VMEM((hr, l), F32), pltpu.VMEM((N_CHIPS, hr, l), F32),
                        pltpu.SemaphoreType.DMA((5,)), pltpu.SemaphoreType.DMA((5,))],
    )(pack)


def _flat_rows(a, mult=SUBLANES * LANES):
    f = a.reshape(-1)
    padn = (-f.shape[0]) % mult
    if padn:
        f = jnp.concatenate([f, jnp.zeros((padn,), f.dtype)])
    return f


def _pack(arrs, mult=SUBLANES * LANES, total_mult=None):
    flat = [_flat_rows(a, mult) for a in arrs]
    sizes = [f.shape[0] for f in flat]
    if total_mult is not None:
        padn = (-sum(sizes)) % total_mult
        if padn:
            flat.append(jnp.zeros((padn,), flat[0].dtype))
    return jnp.concatenate(flat).reshape(-1, LANES), sizes


def _unpack(pack, shapes, sizes, lead=()):
    flat = pack.reshape(lead + (-1,))
    out, off = [], 0
    for shp, sz in zip(shapes, sizes):
        n = math.prod(shp)
        out.append(flat[..., off:off + n].reshape(lead + tuple(shp)))
        off += sz
    return out


def _cols_from_shards(g):
    s, k, n = g.shape
    return jnp.transpose(g, (1, 0, 2)).reshape(k, s * n)


def _ffn_fwd(h, u, post, w_up, cw, cb, w_down, tag, next_pre=None, plan=None, loss_tgt=None):
    hp = _mm_nn_sh(u, w_up, 2 * D_FF, f"{tag}_up")
    if plan is None:
        (act, hg, hu), pouts = _ffn_convact_fwd(hp, cw, cb, f"{tag}_convact"), ()
    else:
        (act, hg, hu), pouts = _ffn_convact_fwd(hp, cw, cb, f"{tag}_convact", plan)
    if loss_tgt is not None:
        o, hn, un = _mm_postnorm_loss(act, w_down, h, post, loss_tgt, f"{tag}_down_postnorm_loss")
    else:
        res = _mm_postnorm_res(act, w_down, h, post, f"{tag}_down_postnorm", next_pre)
        o, hn, un = res if next_pre is not None else (*res, None)
    return hn, un, (h, u, hp, hg, hu, act, o), pouts


def _ffn_bwd(dh, saved, pre, post, w_up, cw, w_down, tag):
    h, u, hp, hg, hu, act, o = saved
    dhg, dhu, do, dpost = _ffn_down_act_bwd(o, post, dh, w_down, hg, hu, f"{tag}_down_dx_act_bwd")
    dw_down = _mm_tn(act, do, f"{tag}_down_dw")
    dxg, dwg, dbg = _conv_bwd(hp, 0, D_FF, dhg, cw, f"{tag}_conv_bwd_gate")
    dxu, dwu, dbu = _conv_bwd(hp, D_FF, D_FF, dhu, cw, f"{tag}_conv_bwd_up", w_col_off=D_FF)
    dhp = (dxg, dxu)
    dcw = jnp.concatenate([dwg, dwu], axis=1)
    dcb = jnp.concatenate([dbg, dbu], axis=1)
    (dhn, dpre), _ = _mm_nt_sh(dhp, w_up, (h, pre, dh), f"{tag}_up_dx_prenorm_bwd")
    dw_up = _mm_tn_sh(u, dhp, w_up.shape[2], f"{tag}_up_dw")
    return dhn, dict(pre=dpre, post=dpost, w_up=dw_up, conv_w=dcw[:3], conv_b=dcb, w_down=dw_down)


class _Exchange:
    GATHER_IN_LRU = ("l0_w_out", "l0_ffn_w_down")
    GATHER_IN_ATTN = ("l0_ffn_w_up", "l1_w_out")
    GATHER_IN_FFN0 = ("l1_w_in",)
    GATHER_IN_SSD = ("l1_ffn_w_up", "l1_ffn_w_down")
    AFTER_L1_OUT = ("l1_ffn_w_up", "l1_ffn_w_down", "l1_w_out")
    IN_LRU_BWD = ("l1_w_in",)
    AFTER_L0_OUT = ("l0_ffn_w_up", "l0_ffn_w_down", "l0_w_out")
    LAST = ("l0_w_in",)

    def __init__(self, late_shards):
        self.late = dict(late_shards)
        self.slabs = {}
        self.recv = {}

    def gather_plan(self, names):
        return _gather_plan([self.late[n] for n in names])

    def gathered(self, names, outs):
        return {n: (g if n in _BIG_COL else g.reshape(-1, g.shape[-1])) for n, g in zip(names, outs)}

    def scatter_plan(self, grads, names):
        for n in names:
            g = grads[n]
            self.slabs[n] = g if n in _BIG_COL else g.reshape(N_CHIPS, -1, g.shape[-1])
        return _scatter8_plan([self.slabs[n] for n in names])

    def scattered(self, names, outs):
        self.recv.update(zip(names, outs))


def _local_step(x, tgt, meta, P, ex=None):
    seq, d = x.shape
    lp = seq + BLOCK
    h0 = jnp.concatenate([jnp.zeros((PAD, d), F32), meta, x], axis=0)
    tgt_p = jnp.concatenate([jnp.zeros((BLOCK, d), F32), tgt], axis=0)

    u0 = _rmsnorm_fwd(h0, P["l0_mix_pre_norm"], "l0_mix_prenorm")
    proj0 = _mm_nn_sh(u0, P["l0_w_in"], EVEN_IN, "l0_in")
    xrc = _conv_fwd(proj0, D_MODEL, D_MODEL, P["l0_lru_conv_w"], P["l0_lru_conv_b"], "l0_lru_conv")
    lru_args = (P["l0_lru_w_a"], P["l0_lru_w_x"], P["l0_lru_b_a"], P["l0_lru_b_x"], P["l0_lru_lambda"])
    if ex:
        (ya, hl), outs = _lru_fwd(proj0, xrc, *lru_args, "l0_lru", ex.gather_plan(ex.GATHER_IN_LRU))
        P = {**P, **ex.gathered(ex.GATHER_IN_LRU, outs)}
    else:
        ya, hl = _lru_fwd(proj0, xrc, *lru_args, "l0_lru")
    yb, outs = _attn_fwd(proj0, P["l0_attn_sinks"], "l0_attn",
                         ex.gather_plan(ex.GATHER_IN_ATTN) if ex else None)
    if ex:
        P = {**P, **ex.gathered(ex.GATHER_IN_ATTN, outs)}
    o0, h1, u1 = _mm_postnorm_res((ya, yb), P["l0_w_out"], h0, P["l0_mix_post_norm"], "l0_out_postnorm",
                                  P["l0_ffn_pre_norm"])
    h2, u2, ffn0, outs = _ffn_fwd(h1, u1, P["l0_ffn_post_norm"], P["l0_ffn_w_up"], P["l0_ffn_conv_w"],
                                  P["l0_ffn_conv_b"], P["l0_ffn_w_down"], "l0_ffn", P["l1_mix_pre_norm"],
                                  ex.gather_plan(ex.GATHER_IN_FFN0) if ex else None)
    if ex:
        P = {**P, **ex.gathered(ex.GATHER_IN_FFN0, outs)}
    proj1 = _mm_nn_sh(u2, P["l1_w_in"], ODD_IN_PAD, "l1_in")
    xc1, xbc, dt = _ssm_convprep_fwd(proj1, P["l1_ssm_conv_w"], P["l1_ssm_conv_b"], P["l1_dt_bias"],
                                     "l1_ssm_convprep")
    (yssd, states), outs = _ssd_fwd(xbc, dt, P["l1_a_log"], "l1_ssd",
                                    ex.gather_plan(ex.GATHER_IN_SSD) if ex else None)
    if ex:
        P = {**P, **ex.gathered(ex.GATHER_IN_SSD, outs)}
    yn, o1, h3, u3 = _ssm_gate_out_postnorm(
        yssd, xbc, proj1, P["l1_d_skip"], P["l1_gate_norm"], P["l1_w_out"], h2, P["l1_mix_post_norm"],
        P["l1_ffn_pre_norm"], "l1_gate_out_postnorm")
    dh4, loss_cols, ffn1, _ = _ffn_fwd(h3, u3, P["l1_ffn_post_norm"], P["l1_ffn_w_up"], P["l1_ffn_conv_w"],
                                       P["l1_ffn_conv_b"], P["l1_ffn_w_down"], "l1_ffn", loss_tgt=tgt_p)

    G = {}
    dh3, g = _ffn_bwd(dh4, ffn1, P["l1_ffn_pre_norm"], P["l1_ffn_post_norm"], P["l1_ffn_w_up"],
                      P["l1_ffn_conv_w"], P["l1_ffn_w_down"], "l1_ffn")
    for k, v in g.items():
        G["l1_ffn_" + (k + "_norm" if k in ("pre", "post") else k)] = v
    dyssd, dxskip, dz, dd_cols, G["l1_gate_norm"], do1, G["l1_mix_post_norm"] = _ssm_out_gate_bwd(
        o1, P["l1_mix_post_norm"], dh3, P["l1_w_out"], yssd, xbc, proj1, P["l1_d_skip"], P["l1_gate_norm"],
        "l1_out_dx_gate_bwd")
    G["l1_w_out"] = _mm_tn(yn, do1, "l1_out_dw")
    G["l1_d_skip"] = dd_cols.reshape(SSD_HEADS, SSD_P).sum(axis=1)
    (dxs, dbm, dcm, ddt, dalog), outs = _ssd_bwd(
        xbc, dt, P["l1_a_log"], states, dyssd, "l1_ssd_bwd",
        ex.scatter_plan(G, ex.AFTER_L1_OUT) if ex else None)
    if ex:
        ex.scattered(ex.AFTER_L1_OUT, outs)
    G["l1_a_log"] = dalog[0, :SSD_HEADS]
    dxc, ddtr, dbias = _ssm_prep_bwd(xc1, proj1, P["l1_dt_bias"], dxs, dxskip, dbm, dcm, ddt,
                                     "l1_ssm_prep_bwd")
    G["l1_dt_bias"] = dbias[0, :SSD_HEADS]
    dxbc, dcw, dcb = _conv_bwd(proj1, _ZW, _XBC_W, dxc, P["l1_ssm_conv_w"], "l1_ssm_conv_bwd")
    G["l1_ssm_conv_w"] = dcw[:4]
    G["l1_ssm_conv_b"] = dcb
    dproj1 = jnp.concatenate([dz, dxbc, ddtr], axis=1)
    (dh2, G["l1_mix_pre_norm"]), _ = _mm_nt_sh(dproj1, P["l1_w_in"], (h2, P["l1_mix_pre_norm"], dh3),
                                               "l1_in_dx_prenorm_bwd")
    G["l1_w_in"] = _mm_tn_sh(u2, dproj1, ODD_IN // N_CHIPS, "l1_in_dw")
    dh1, g = _ffn_bwd(dh2, ffn0, P["l0_ffn_pre_norm"], P["l0_ffn_post_norm"], P["l0_ffn_w_up"],
                      P["l0_ffn_conv_w"], P["l0_ffn_w_down"], "l0_ffn")
    for k, v in g.items():
        G["l0_ffn_" + (k + "_norm" if k in ("pre", "post") else k)] = v
    dmix, do0, G["l0_mix_post_norm"] = _mm_nt_postnorm_bwd(o0, P["l0_mix_post_norm"], dh1, P["l0_w_out"],
                                                           "l0_out_dx")
    G["l0_w_out"] = jnp.concatenate([_mm_tn(ya, do0, "l0_out_dw_lru"), _mm_tn(yb, do0, "l0_out_dw_attn")], axis=0)
    if ex:
        lru_out, outs = _lru_bwd(proj0, xrc, hl, dmix, *lru_args, "l0_lru_bwd",
                                 ex.scatter_plan(G, ex.IN_LRU_BWD))
        ex.scattered(ex.IN_LRU_BWD, outs)
    else:
        lru_out = _lru_bwd(proj0, xrc, hl, dmix, *lru_args, "l0_lru_bwd")
    (dgate, dxrc, G["l0_lru_w_a"], G["l0_lru_w_x"], G["l0_lru_b_a"], G["l0_lru_b_x"],
     G["l0_lru_lambda"]) = lru_out
    dxr, dcw, dcb = _conv_bwd(proj0, D_MODEL, D_MODEL, dxrc, P["l0_lru_conv_w"], "l0_lru_conv_bwd")
    G["l0_lru_conv_w"] = dcw[:4]
    G["l0_lru_conv_b"] = dcb
    (dq, dk, dv, G["l0_attn_sinks"]), outs = _attn_bwd(
        proj0, P["l0_attn_sinks"], dmix, "l0_attn_bwd",
        ex.scatter_plan(G, ex.AFTER_L0_OUT) if ex else None)
    if ex:
        ex.scattered(ex.AFTER_L0_OUT, outs)
    dproj0 = jnp.concatenate([dgate, dxr, dq, dk.astype(BF16), dv.astype(BF16)], axis=1)
    G["l0_w_in"] = _mm_tn_sh(u0, dproj0, EVEN_IN // N_CHIPS, "l0_in_dw")
    (dh0, G["l0_mix_pre_norm"]), outs = _mm_nt_sh(
        dproj0, P["l0_w_in"], (h0, P["l0_mix_pre_norm"], dh1), "l0_in_dx_prenorm_bwd",
        ex.scatter_plan(G, ex.LAST) if ex else None)
    if ex:
        ex.scattered(ex.LAST, outs)
    return loss_cols, dh0[BLOCK:], dh0[PAD:BLOCK], G


_BIG_COL = ("l0_w_in", "l0_ffn_w_up", "l1_w_in", "l1_ffn_w_up")
_BIG = ("l0_w_in", "l0_w_out", "l0_ffn_w_up", "l0_ffn_w_down",
        "l1_w_in", "l1_w_out", "l1_ffn_w_up", "l1_ffn_w_down")
_SMALL_SHARDED = ("meta_tokens", "l0_lru_conv_w", "l0_ffn_conv_w", "l1_ssm_conv_w", "l1_ffn_conv_w")
_WEIGHTS = ("meta_tokens", "l0_mix_pre_norm", "l0_mix_post_norm", "l0_w_in", "l0_lru_conv_w",
            "l0_lru_conv_b", "l0_lru_w_a", "l0_lru_b_a", "l0_lru_w_x", "l0_lru_b_x", "l0_lru_lambda",
            "l0_attn_sinks", "l0_w_out", "l0_ffn_pre_norm", "l0_ffn_post_norm", "l0_ffn_w_up",
            "l0_ffn_conv_w", "l0_ffn_conv_b", "l0_ffn_w_down", "l1_mix_pre_norm", "l1_mix_post_norm",
            "l1_w_in", "l1_ssm_conv_w", "l1_ssm_conv_b", "l1_dt_bias", "l1_a_log", "l1_d_skip",
            "l1_gate_norm", "l1_w_out", "l1_ffn_pre_norm", "l1_ffn_post_norm", "l1_ffn_w_up",
            "l1_ffn_conv_w", "l1_ffn_conv_b", "l1_ffn_w_down")
_REPL = tuple(n for n in _WEIGHTS if n not in _BIG and n not in _SMALL_SHARDED)


def _pad_lanes(v, n=LANES):
    return jnp.concatenate([v, jnp.zeros((n - v.shape[0],), v.dtype)]).reshape(1, n)


def _step(x, tgt, W, M, V):
    cx, cy, cc = _place()
    chip = 2 * cx + cy

    small_pack, small_sizes = _pack([W[n] for n in _SMALL_SHARDED])
    first = _run_plan(_gather_plan([W["l0_w_in"].astype(BF16), small_pack]), "gather_first")
    small_full = _unpack(first[1], [W[n].shape for n in _SMALL_SHARDED], small_sizes, lead=(N_CHIPS,))
    ex = _Exchange({n: W[n].astype(BF16) for n in _BIG if n != "l0_w_in"})

    P = {"l0_w_in": first[0]}
    for n, g in zip(_SMALL_SHARDED, small_full):
        P[n] = _cols_from_shards(g)
    for n in _REPL:
        v = W[n]
        P[n] = v.reshape(1, -1) if v.ndim == 1 else v
    P["l0_lru_w_a"] = W["l0_lru_w_a"].astype(BF16)
    P["l0_lru_w_x"] = W["l0_lru_w_x"].astype(BF16)
    P["l1_dt_bias"] = _pad_lanes(W["l1_dt_bias"])
    P["l1_a_log"] = _pad_lanes(W["l1_a_log"])
    P["l1_d_skip"] = jnp.repeat(W["l1_d_skip"], SSD_P).reshape(1, D_SSM)
    meta = P.pop("meta_tokens")

    loss_cols, grad_x, grad_meta, G = _local_step(x, tgt, meta, P, ex)
    G["meta_tokens"] = grad_meta

    core_idx = cc.astype(jnp.int32).reshape(1)
    chip_idx = chip.astype(jnp.int32).reshape(1)
    own_half = [_add8(ex.slabs[n], ex.recv[n], chip_idx, core_idx, f"grad_sum_{n}") for n in _BIG]
    other_half = _run_plan(_sibling_plan(own_half), "grad_sibling_swap")
    small_names = list(_REPL) + list(_SMALL_SHARDED)
    small_list = [G[n] for n in small_names] + [loss_cols]
    spack, ssizes = _pack(small_list, total_mult=2 * SUBLANES * LANES)
    sred = _allreduce_small(spack, "small_allreduce")
    sfull = _unpack(sred, [a.shape for a in small_list], ssizes)
    loss = 0.5 / D_MODEL * jnp.sum(sfull[-1])
    small_grads = {}
    for n, g in zip(small_names, sfull[:-1]):
        if n in _SMALL_SHARDED:
            wcols = W[n].shape[1]
            g = lax.dynamic_slice_in_dim(g, chip * wcols, wcols, axis=1)
        small_grads[n] = g.reshape(W[n].shape)

    grads, delta, new_m, new_v = {}, {}, {}, {}
    for n, own, other in zip(_BIG, own_half, other_half):
        grads[n], delta[n], new_m[n], new_v[n] = _adamw_halves(
            W[n], own, other, M[n], V[n], core_idx, f"adamw_{n}")
    s_names = [n for n in _WEIGHTS if n not in _BIG]
    as2d = lambda a: a.reshape(_shape2d(a.shape))
    outs = _adamw_many([as2d(W[n]) for n in s_names], [as2d(small_grads[n]) for n in s_names],
                       [as2d(M[n]) for n in s_names], [as2d(V[n]) for n in s_names], "adamw_small")
    k = len(s_names)
    for i, n in enumerate(s_names):
        grads[n] = small_grads[n]
        delta[n], new_m[n], new_v[n] = (outs[j * k + i].reshape(W[n].shape) for j in range(3))
    return loss, grad_x, grads, delta, new_m, new_v


def kernel(x, meta_tokens, l0_mix_pre_norm, l0_mix_post_norm, l0_w_in, l0_lru_conv_w, l0_lru_conv_b, l0_lru_w_a, l0_lru_b_a, l0_lru_w_x, l0_lru_b_x, l0_lru_lambda, l0_attn_sinks, l0_w_out, l0_ffn_pre_norm, l0_ffn_post_norm, l0_ffn_w_up, l0_ffn_conv_w, l0_ffn_conv_b, l0_ffn_w_down, l1_mix_pre_norm, l1_mix_post_norm, l1_w_in, l1_ssm_conv_w, l1_ssm_conv_b, l1_dt_bias, l1_a_log, l1_d_skip, l1_gate_norm, l1_w_out, l1_ffn_pre_norm, l1_ffn_post_norm, l1_ffn_w_up, l1_ffn_conv_w, l1_ffn_conv_b, l1_ffn_w_down, loss_target, m_meta_tokens, m_l0_mix_pre_norm, m_l0_mix_post_norm, m_l0_w_in, m_l0_lru_conv_w, m_l0_lru_conv_b, m_l0_lru_w_a, m_l0_lru_b_a, m_l0_lru_w_x, m_l0_lru_b_x, m_l0_lru_lambda, m_l0_attn_sinks, m_l0_w_out, m_l0_ffn_pre_norm, m_l0_ffn_post_norm, m_l0_ffn_w_up, m_l0_ffn_conv_w, m_l0_ffn_conv_b, m_l0_ffn_w_down, m_l1_mix_pre_norm, m_l1_mix_post_norm, m_l1_w_in, m_l1_ssm_conv_w, m_l1_ssm_conv_b, m_l1_dt_bias, m_l1_a_log, m_l1_d_skip, m_l1_gate_norm, m_l1_w_out, m_l1_ffn_pre_norm, m_l1_ffn_post_norm, m_l1_ffn_w_up, m_l1_ffn_conv_w, m_l1_ffn_conv_b, m_l1_ffn_w_down, v_meta_tokens, v_l0_mix_pre_norm, v_l0_mix_post_norm, v_l0_w_in, v_l0_lru_conv_w, v_l0_lru_conv_b, v_l0_lru_w_a, v_l0_lru_b_a, v_l0_lru_w_x, v_l0_lru_b_x, v_l0_lru_lambda, v_l0_attn_sinks, v_l0_w_out, v_l0_ffn_pre_norm, v_l0_ffn_post_norm, v_l0_ffn_w_up, v_l0_ffn_conv_w, v_l0_ffn_conv_b, v_l0_ffn_w_down, v_l1_mix_pre_norm, v_l1_mix_post_norm, v_l1_w_in, v_l1_ssm_conv_w, v_l1_ssm_conv_b, v_l1_dt_bias, v_l1_a_log, v_l1_d_skip, v_l1_gate_norm, v_l1_w_out, v_l1_ffn_pre_norm, v_l1_ffn_post_norm, v_l1_ffn_w_up, v_l1_ffn_conv_w, v_l1_ffn_conv_b, v_l1_ffn_w_down):
    args = locals()
    W = {n: args[n] for n in _WEIGHTS}
    M = {n: args["m_" + n] for n in _WEIGHTS}
    V = {n: args["v_" + n] for n in _WEIGHTS}
    loss, grad_x, grads, delta, new_m, new_v = _step(x[0], loss_target[0], W, M, V)
    return (loss, grad_x[None], *[grads[n] for n in _WEIGHTS], *[delta[n] for n in _WEIGHTS],
            *[new_m[n] for n in _WEIGHTS], *[new_v[n] for n in _WEIGHTS])
```

```python
import functools
import math

import jax
import jax.numpy as jnp
from jax import lax
from jax.experimental import pallas as pl
from jax.experimental.pallas import tpu as pltpu

F32 = jnp.float32
BF16 = jnp.bfloat16

D_MODEL = 1024
N_META = 16
BLOCK = 128
PAD = BLOCK - N_META
EPS = 1e-6
LRU_BLOCKS = 8
LRU_BS = 128
LRU_C = 8.0
N_Q_HEADS = 16
N_KV_HEADS = 2
HEAD_DIM = 64
Q_PER_KV = 8
WINDOW = 128
D_SSM = 2048
SSD_HEADS = 32
SSD_GROUPS = 8
SSD_HPG = 4
SSD_P = 64
SSD_N = 128
D_FF = 2816
NEG = -1e30
LANES = 128
SUBLANES = 8
_VMEM_LIMIT_WIDE = 62 * 1024 * 1024

ADAM_LR = 0.001
ADAM_B1 = 0.9
ADAM_B2 = 0.999
ADAM_EPS = 1e-08
ADAM_WD = 0.01
ADAM_STEP = 10

MESH = pl.DeviceIdType.MESH
N_CHIPS = 4


def _pick(n, cands):
    for c in cands:
        if n % c == 0:
            return c
    raise ValueError(f"no tile for {n} in {cands}")


def _col_tile(n, limit=1792):
    best = None
    for t in range(LANES, min(n, limit) + 1, LANES):
        if n % t == 0:
            best = t
    if best is None:
        raise ValueError(f"no lane tile for {n}")
    return best


def _sigmoid(x):
    return 0.5 + 0.5 * jnp.tanh(0.5 * x)


def _log1p(e):
    u = 1.0 + e
    return jnp.where(u == 1.0, e, jnp.log(u) * (e / jnp.where(u == 1.0, 1.0, u - 1.0)))


def _softplus(x):
    return jnp.maximum(x, 0.0) + _log1p(jnp.exp(-jnp.abs(x)))


def _neg_expm1(x):
    poly = x * (1.0 + x * (0.5 + x * (1.0 / 6.0 + x * (1.0 / 24.0 + x * (1.0 / 120.0)))))
    return -jnp.where(x > -0.05, poly, jnp.exp(x) - 1.0)


_GELU_C = math.sqrt(2.0 / math.pi)


def _gelu(x):
    u = 0.5 + 0.5 * jnp.tanh(x * (_GELU_C + (_GELU_C * 0.044715) * (x * x)))
    return x * u


def _gelu_and_grad(x):
    x2 = x * x
    u = 0.5 + 0.5 * jnp.tanh(x * (_GELU_C + (_GELU_C * 0.044715) * x2))
    g = x * u
    dg = u * (1.0 + (x - g) * (2.0 * _GELU_C + (6.0 * 0.044715 * _GELU_C) * x2))
    return g, dg


def _silu_and_grad(x):
    s = _sigmoid(x)
    return x * s, s * (1.0 + x * (1.0 - s))


def _dot(a, b):
    return jnp.dot(a, b, preferred_element_type=F32)


def _dot_nt(a, b):
    return lax.dot_general(a, b, (((1,), (1,)), ((), ())), preferred_element_type=F32)


def _dot_tn(a, b):
    return lax.dot_general(a, b, (((0,), (0,)), ((), ())), preferred_element_type=F32)


def _row_iota(t):
    return lax.broadcasted_iota(jnp.int32, (t, 1), 0)


def _scan_fwd(a, u, t):
    row = _row_iota(t)
    d = 1
    while d < t:
        m = row >= d
        u_sh = jnp.where(m, pltpu.roll(u, d, 0), 0.0)
        a_sh = jnp.where(m, pltpu.roll(a, d, 0), 1.0)
        u = u + a * u_sh
        a = a * a_sh
        d *= 2
    return a, u


def _scan_rev(c, x, t):
    row = _row_iota(t)
    d = 1
    while d < t:
        m = row < t - d
        x_sh = jnp.where(m, pltpu.roll(x, t - d, 0), 0.0)
        c_sh = jnp.where(m, pltpu.roll(c, t - d, 0), 1.0)
        x = x + c * x_sh
        c = c * c_sh
        d *= 2
    return c, x


def _cumsum_rows(x, t):
    row = _row_iota(t)
    d = 1
    while d < t:
        x = x + jnp.where(row >= d, pltpu.roll(x, d, 0), 0.0)
        d *= 2
    return x


def _rev_cumsum_rows(x, t):
    row = _row_iota(t)
    d = 1
    while d < t:
        x = x + jnp.where(row < t - d, pltpu.roll(x, t - d, 0), 0.0)
        d *= 2
    return x


def _rms_bwd(x, g, dy):
    rs = lax.rsqrt(jnp.mean(x * x, axis=-1, keepdims=True) + EPS)
    gy = dy * g
    dx = rs * gy - x * (rs * rs * rs) * jnp.mean(x * gy, axis=-1, keepdims=True)
    return dx, dy * x * rs


def _mm_nt_postnorm_bwd(o, gain, dh, w, name):
    m, d = o.shape
    k = w.shape[0]
    tm = _pick(m, (640, 512, 256, 128))

    def body(o_ref, gain_ref, dh_ref, w_ref, y_ref, do_ref, dgain_ref):
        y_ref[...] = _dot_nt(_postnorm_bwd_tile(o_ref, gain_ref, dh_ref, do_ref, dgain_ref), w_ref[...])

    row = pl.BlockSpec((tm, d), lambda i: (i, 0))
    vec = pl.BlockSpec((1, d), lambda i: (0, 0))
    return pl.pallas_call(
        body, name=name, grid=(m // tm,),
        in_specs=[row, vec, row, pl.BlockSpec((k, d), lambda i: (0, 0))],
        out_specs=[pl.BlockSpec((tm, k), lambda i: (i, 0)), row, vec],
        out_shape=[jax.ShapeDtypeStruct((m, k), F32), jax.ShapeDtypeStruct((m, d), BF16),
                   jax.ShapeDtypeStruct((1, d), F32)],
        compiler_params=pltpu.CompilerParams(dimension_semantics=("arbitrary",)),
    )(o, gain, dh, w)


def _mm_tn(a, dy, name):
    m, k = a.shape
    n = dy.shape[1]
    tm = _pick(m, (640, 512, 256, 128))
    tk = _col_tile(k, 1408)
    tn = _col_tile(n, 1664)
    nsteps = m // tm

    def body(a_ref, dy_ref, o_ref, acc):
        @pl.when(pl.program_id(2) == 0)
        def _():
            acc[...] = jnp.zeros_like(acc)

        acc[...] += _dot_tn(a_ref[...].astype(BF16), dy_ref[...].astype(BF16))

        @pl.when(pl.program_id(2) == nsteps - 1)
        def _():
            o_ref[...] = acc[...].astype(o_ref.dtype)

    return pl.pallas_call(
        body, name=name, grid=(k // tk, n // tn, nsteps),
        in_specs=[pl.BlockSpec((tm, tk), lambda kk, j, i: (i, kk)),
                  pl.BlockSpec((tm, tn), lambda kk, j, i: (i, j))],
        out_specs=pl.BlockSpec((tk, tn), lambda kk, j, i: (kk, j)),
        out_shape=jax.ShapeDtypeStruct((k, n), BF16),
        scratch_shapes=[pltpu.VMEM((tk, tn), F32)],
        compiler_params=pltpu.CompilerParams(
            dimension_semantics=("parallel", "parallel", "arbitrary")),
    )(a, dy)


def _mm_nn_sh(a, w4, n_out, name):
    m, k = a.shape
    s, _, n = w4.shape
    tm = _pick(m, (320, 256, 128))

    def body(a_ref, w_ref, o_ref):
        av = a_ref[...].astype(BF16)
        for j in range(s):
            o_ref[:, j * n:(j + 1) * n] = _dot(av, w_ref[j])
        if n_out > s * n:
            o_ref[:, s * n:] = jnp.zeros((tm, n_out - s * n), F32)

    return pl.pallas_call(
        body, name=name, grid=(m // tm,),
        in_specs=[pl.BlockSpec((tm, k), lambda i: (i, 0)),
                  pl.BlockSpec((s, k, n), lambda i: (0, 0, 0))],
        out_specs=pl.BlockSpec((tm, n_out), lambda i: (i, 0)),
        out_shape=jax.ShapeDtypeStruct((m, n_out), F32),
        compiler_params=pltpu.CompilerParams(dimension_semantics=("parallel",)),
    )(a, w4)


def _mm_nt_sh(dy, w4, norm, name, plan=None):
    dys = dy if isinstance(dy, (tuple, list)) else (dy,)
    h, g, dres = norm
    m = dys[0].shape[0]
    s, k, n = w4.shape
    tm = _pick(m, (640, 512, 256, 128))
    where = _shard_columns(dys, s, n)
    p_in, p_shapes, p_out, p_scr = _plan_parts(plan)
    nd, ni = len(dys), m // tm

    def body(*refs):
        w_ref, h_ref, g_ref, dres_ref = refs[nd:nd + 4]
        cins = refs[nd + 4:nd + 4 + len(p_in)]
        dh_ref, dg_ref = refs[nd + 4 + len(p_in):nd + 6 + len(p_in)]
        couts = refs[nd + 6 + len(p_in):nd + 6 + len(p_in) + len(p_out)]
        sems = refs[nd + 6 + len(p_in) + len(p_out):]
        i = pl.program_id(0)
        if plan is not None:
            @pl.when(i == 0)
            def _():
                plan.start(cins, couts, sems)

        du = None
        for j, (p, c0) in enumerate(where):
            t = _dot_nt(refs[p][:, c0:c0 + n].astype(BF16), w_ref[j])
            du = t if du is None else du + t
        dx, dgt = _rms_bwd(h_ref[...], g_ref[...], du)
        dh_ref[...] = jnp.where(_real_rows(i, tm), dres_ref[...] + dx, 0.0)
        _acc_add(i == 0, dg_ref, jnp.sum(dgt, axis=0, keepdims=True))
        if plan is not None:
            @pl.when(i == ni - 1)
            def _():
                plan.wait(cins, couts, sems)

    row = pl.BlockSpec((tm, k), lambda i: (i, 0))
    vec = pl.BlockSpec((1, k), lambda i: (0, 0))
    res = pl.pallas_call(
        body, name=name, grid=(ni,),
        in_specs=[pl.BlockSpec((tm, d.shape[1]), lambda i: (i, 0)) for d in dys]
        + [pl.BlockSpec((s, k, n), lambda i: (0, 0, 0)), row, vec, row] + p_in,
        out_specs=[row, vec] + p_out,
        out_shape=[jax.ShapeDtypeStruct((m, k), F32), jax.ShapeDtypeStruct((1, k), F32)] + p_shapes,
        scratch_shapes=p_scr,
        compiler_params=pltpu.CompilerParams(dimension_semantics=("arbitrary",),
                                             vmem_limit_bytes=_VMEM_LIMIT_WIDE),
    )(*dys, w4, h, g, dres, *(plan.ins if plan is not None else []))
    return (res[0], res[1]), res[2:]


def _shard_columns(dys, s, n):
    where = []
    for p, d in enumerate(dys):
        where += [(p, c * n) for c in range(d.shape[1] // n)]
    assert len(where) >= s
    return where[:s]


def _mm_tn_sh(a, dy, n, name):
    dys = dy if isinstance(dy, (tuple, list)) else (dy,)
    m, k = a.shape
    s = N_CHIPS
    tm = _pick(m, (640, 512, 256, 128))
    tk = _col_tile(k, 512)
    nsteps = m // tm
    where = _shard_columns(dys, s, n)

    def body(*refs):
        a_ref, o_ref, acc = refs[0], refs[len(dys) + 1], refs[len(dys) + 2]

        @pl.when(pl.program_id(1) == 0)
        def _():
            acc[...] = jnp.zeros_like(acc)

        av = a_ref[...].astype(BF16)
        for j, (p, c0) in enumerate(where):
            acc[j] += _dot_tn(av, refs[1 + p][:, c0:c0 + n].astype(BF16))

        @pl.when(pl.program_id(1) == nsteps - 1)
        def _():
            o_ref[...] = acc[...].astype(o_ref.dtype)

    return pl.pallas_call(
        body, name=name, grid=(k // tk, nsteps),
        in_specs=[pl.BlockSpec((tm, tk), lambda kk, i: (i, kk))]
        + [pl.BlockSpec((tm, d.shape[1]), lambda kk, i: (i, 0)) for d in dys],
        out_specs=pl.BlockSpec((s, tk, n), lambda kk, i: (0, kk, 0)),
        out_shape=jax.ShapeDtypeStruct((s, k, n), BF16),
        scratch_shapes=[pltpu.VMEM((s, tk, n), F32)],
        compiler_params=pltpu.CompilerParams(dimension_semantics=("parallel", "arbitrary"),
                                             vmem_limit_bytes=_VMEM_LIMIT_WIDE),
    )(a, *dys)


def _rowcall(name, body, lp, tm, rows=(), prevs=(), vecs=(), outs=(), accs=(), scratch=(),
             reverse=False, seq=False, plan=None):
    p_in, p_shapes, p_out, p_scr = _plan_parts(plan)
    nt = lp // tm
    hb = tm // SUBLANES

    def ri(i):
        return nt - 1 - i if reverse else i

    in_specs, args = [], []
    for arr, w, cb in rows:
        in_specs.append(pl.BlockSpec((tm, w), lambda i, cb=cb: (ri(i), cb)))
        args.append(arr)
    for arr, w, cb in prevs:
        in_specs.append(pl.BlockSpec((SUBLANES, w), lambda i, cb=cb: (jnp.maximum(ri(i) * hb - 1, 0), cb)))
        args.append(arr)
    for arr in vecs:
        in_specs.append(pl.BlockSpec(arr.shape, lambda i, nd=arr.ndim: (0,) * nd))
        args.append(arr)
    out_shape, out_specs = [], []
    for w, dt in outs:
        out_shape.append(jax.ShapeDtypeStruct((lp, w), dt))
        out_specs.append(pl.BlockSpec((tm, w), lambda i: (ri(i), 0)))
    for shp, dt in accs:
        out_shape.append(jax.ShapeDtypeStruct(shp, dt))
        out_specs.append(pl.BlockSpec(shp, lambda i, nd=len(shp): (0,) * nd))

    n_in, n_out, n_scr = len(args), len(out_shape), len(scratch)

    def kern(*refs):
        i = pl.program_id(0)
        own = (refs[:n_in] + refs[n_in + len(p_in):n_in + len(p_in) + n_out]
               + refs[n_in + len(p_in) + n_out + len(p_out):n_in + len(p_in) + n_out + len(p_out) + n_scr])
        cins = refs[n_in:n_in + len(p_in)]
        couts = refs[n_in + len(p_in) + n_out:n_in + len(p_in) + n_out + len(p_out)]
        sems = refs[n_in + len(p_in) + n_out + len(p_out) + n_scr:]
        if plan is not None:
            @pl.when(i == 0)
            def _():
                plan.start(cins, couts, sems)

        body(ri(i), i == 0, *own)
        if plan is not None:
            @pl.when(i == nt - 1)
            def _():
                plan.wait(cins, couts, sems)

    sem = ("arbitrary",) if (seq or accs or plan is not None) else ("parallel",)
    res = pl.pallas_call(
        kern, name=name, grid=(nt,), in_specs=in_specs + p_in, out_specs=out_specs + p_out,
        out_shape=out_shape + p_shapes, scratch_shapes=list(scratch) + p_scr,
        compiler_params=pltpu.CompilerParams(dimension_semantics=sem),
    )(*args, *(plan.ins if plan is not None else []))
    return res if plan is None else (res[:n_out], res[n_out:])


def _acc_add(first, ref, val):
    @pl.when(first)
    def _():
        ref[...] = jnp.zeros_like(ref)

    ref[...] += val


def _real_rows(r, tm):
    return (r * tm + _row_iota(tm)) >= PAD


def _rmsnorm_fwd(h, g, name):
    lp, d = h.shape
    tm = _pick(lp, (640, 512, 256, 128))

    def body(r, first, h_ref, g_ref, u_ref):
        x = h_ref[...]
        rs = lax.rsqrt(jnp.mean(x * x, axis=-1, keepdims=True) + EPS)
        u_ref[...] = (x * rs * g_ref[...]).astype(u_ref.dtype)

    return _rowcall(name, body, lp, tm, rows=[(h, d, 0)], vecs=[g], outs=[(d, BF16)])[0]


def _mm_postnorm_res(a, w, h, g, name, next_pre=None):
    parts = a if isinstance(a, (tuple, list)) else (a,)
    lp, d = h.shape
    k = w.shape[0]
    tm = _pick(lp, (640, 512, 256, 128))
    offs = [sum(p.shape[1] for p in parts[:i]) for i in range(len(parts))]
    np_ = len(parts)

    def body(*refs):
        w_ref, h_ref, g_ref = refs[np_], refs[np_ + 1], refs[np_ + 2]
        rest = refs[np_ + 3:]
        acc = None
        for a_ref, p, off in zip(refs, parts, offs):
            t = _dot(a_ref[...].astype(BF16), w_ref[off:off + p.shape[1], :])
            acc = t if acc is None else acc + t
        outs = rest[1:] if next_pre is not None else rest
        outs[0][...] = acc
        rs = lax.rsqrt(jnp.mean(acc * acc, axis=-1, keepdims=True) + EPS)
        hn = jnp.where(_real_rows(pl.program_id(0), tm), h_ref[...] + acc * rs * g_ref[...], 0.0)
        outs[1][...] = hn
        if next_pre is not None:
            rs2 = lax.rsqrt(jnp.mean(hn * hn, axis=-1, keepdims=True) + EPS)
            outs[2][...] = (hn * rs2 * rest[0][...]).astype(BF16)

    row = pl.BlockSpec((tm, d), lambda i: (i, 0))
    vec = pl.BlockSpec((1, d), lambda i: (0, 0))
    n_vec = 2 if next_pre is not None else 1
    return pl.pallas_call(
        body, name=name, grid=(lp // tm,),
        in_specs=[pl.BlockSpec((tm, p.shape[1]), lambda i: (i, 0)) for p in parts]
        + [pl.BlockSpec((k, d), lambda i: (0, 0)), row] + [vec] * n_vec,
        out_specs=[row] * (2 + (next_pre is not None)),
        out_shape=[jax.ShapeDtypeStruct((lp, d), F32)] * 2
        + ([jax.ShapeDtypeStruct((lp, d), BF16)] if next_pre is not None else []),
        compiler_params=pltpu.CompilerParams(dimension_semantics=("parallel",)),
    )(*parts, w, h, g, *([next_pre] if next_pre is not None else []))


def _mm_postnorm_loss(a, w, h, g, tgt, name):
    lp, d = h.shape
    k = w.shape[0]
    tm = _pick(lp, (640, 512, 256, 128))

    def body(a_ref, w_ref, h_ref, g_ref, t_ref, o_ref, dh_ref, ls_ref):
        i = pl.program_id(0)
        acc = _dot(a_ref[...].astype(BF16), w_ref[...])
        o_ref[...] = acc
        rs = lax.rsqrt(jnp.mean(acc * acc, axis=-1, keepdims=True) + EPS)
        tok = (i * tm + _row_iota(tm)) >= BLOCK
        e = jnp.where(tok, h_ref[...] + acc * rs * g_ref[...] - t_ref[...], 0.0)
        dh_ref[...] = e * (1.0 / d)
        _acc_add(i == 0, ls_ref, jnp.sum(e * e, axis=0, keepdims=True))

    row = pl.BlockSpec((tm, d), lambda i: (i, 0))
    vec = pl.BlockSpec((1, d), lambda i: (0, 0))
    return pl.pallas_call(
        body, name=name, grid=(lp // tm,),
        in_specs=[pl.BlockSpec((tm, k), lambda i: (i, 0)), pl.BlockSpec((k, d), lambda i: (0, 0)),
                  row, vec, row],
        out_specs=[row, row, vec],
        out_shape=[jax.ShapeDtypeStruct((lp, d), F32), jax.ShapeDtypeStruct((lp, d), F32),
                   jax.ShapeDtypeStruct((1, d), F32)],
        compiler_params=pltpu.CompilerParams(dimension_semantics=("arbitrary",)),
    )(a, w, h, g, tgt)


def _conv_tiles(lp, width):
    wc = _col_tile(width, 1408)
    tm = _pick(lp, (320, 256, 128))
    return tm, wc


def _conv_fwd(x, col_off, width, w, b, name):
    lp = x.shape[0]
    kk = w.shape[0]
    tm, wc = _conv_tiles(lp, width)
    offb = col_off // wc
    assert col_off % wc == 0
    hb = tm // SUBLANES

    def body(x_ref, xp_ref, w_ref, b_ref, y_ref):
        i = pl.program_id(1)
        xv = x_ref[...]
        halo = jnp.where(i > 0, xp_ref[...], 0.0)
        xx = jnp.concatenate([halo, xv], axis=0)
        acc = b_ref[...] + w_ref[kk - 1:kk, :] * xv
        for j in range(1, kk):
            acc = acc + w_ref[kk - 1 - j:kk - j, :] * pltpu.roll(xx, j, 0)[SUBLANES:, :]
        y_ref[...] = acc

    return pl.pallas_call(
        body, name=name, grid=(width // wc, lp // tm),
        in_specs=[pl.BlockSpec((tm, wc), lambda j, i: (i, offb + j)),
                  pl.BlockSpec((SUBLANES, wc), lambda j, i: (jnp.maximum(i * hb - 1, 0), offb + j)),
                  pl.BlockSpec((kk, wc), lambda j, i: (0, j)),
                  pl.BlockSpec((1, wc), lambda j, i: (0, j))],
        out_specs=pl.BlockSpec((tm, wc), lambda j, i: (i, j)),
        out_shape=jax.ShapeDtypeStruct((lp, width), F32),
        compiler_params=pltpu.CompilerParams(dimension_semantics=("parallel", "parallel")),
    )(x, x, w, b)


def _conv_bwd(x, col_off, width, dy, w, name, w_col_off=0):
    lp = x.shape[0]
    kk = w.shape[0]
    tm, wc = _conv_tiles(lp, width)
    offb = col_off // wc
    woffb = w_col_off // wc
    assert col_off % wc == 0 and w_col_off % wc == 0
    hrows = SUBLANES * (4 // dy.dtype.itemsize)
    ext = tm + hrows

    def body(x_ref, dy_ref, dn_ref, w_ref, dx_ref, dw_ref, db_ref):
        i = pl.program_id(1)
        last = pl.num_programs(1) - 1
        xv = x_ref[...]
        dyv = dy_ref[...].astype(F32)
        dd = jnp.concatenate([dyv, jnp.where(i < last, dn_ref[...].astype(F32), 0.0)], axis=0)
        dx = w_ref[kk - 1:kk, :] * dyv
        rows = [jnp.sum(dyv * xv, axis=0, keepdims=True)]
        for m in range(1, kk):
            ahead = pltpu.roll(dd, ext - m, 0)[:tm, :]
            dx = dx + w_ref[kk - 1 - m:kk - m, :] * ahead
            rows.append(jnp.sum(ahead * xv, axis=0, keepdims=True))
        dx_ref[...] = dx.astype(dx_ref.dtype)
        dwp = jnp.concatenate(rows[::-1] + [jnp.zeros((SUBLANES - kk, wc), F32)], axis=0)

        @pl.when(i == 0)
        def _():
            dw_ref[...] = jnp.zeros_like(dw_ref)
            db_ref[...] = jnp.zeros_like(db_ref)

        dw_ref[...] += dwp
        db_ref[...] += jnp.sum(dyv, axis=0, keepdims=True)

    return pl.pallas_call(
        body, name=name, grid=(width // wc, lp // tm),
        in_specs=[pl.BlockSpec((tm, wc), lambda j, i: (i, offb + j)),
                  pl.BlockSpec((tm, wc), lambda j, i: (i, j)),
                  pl.BlockSpec((hrows, wc), lambda j, i: (jnp.minimum((i + 1) * (tm // hrows), lp // hrows - 1), j)),
                  pl.BlockSpec((kk, wc), lambda j, i: (0, woffb + j))],
        out_specs=[pl.BlockSpec((tm, wc), lambda j, i: (i, j)),
                   pl.BlockSpec((SUBLANES, wc), lambda j, i: (0, j)),
                   pl.BlockSpec((1, wc), lambda j, i: (0, j))],
        out_shape=[jax.ShapeDtypeStruct((lp, width), BF16),
                   jax.ShapeDtypeStruct((SUBLANES, width), F32),
                   jax.ShapeDtypeStruct((1, width), F32)],
        compiler_params=pltpu.CompilerParams(dimension_semantics=("parallel", "arbitrary")),
    )(x, dy, dy, w)


_FFN_K = 3
_FFN_WC = 1408


def _conv3_ext(x_ext, w_ref, b_ref):
    return (b_ref[...] + w_ref[2:3, :] * x_ext + w_ref[1:2, :] * pltpu.roll(x_ext, 1, 0)
            + w_ref[0:1, :] * pltpu.roll(x_ext, 2, 0))


def _ffn_convact_fwd(hp, cw, cb, name, plan=None):
    lp = hp.shape[0]
    tm = _pick(lp, (320, 256, 128))
    wc = _FFN_WC
    nj = D_FF // wc
    ni = lp // tm
    hb = tm // SUBLANES
    p_in, p_shapes, p_out, p_scr = _plan_parts(plan)

    def body(*refs):
        g_ref, gp_ref, u_ref, up_ref, wg_ref, wu_ref, bg_ref, bu_ref = refs[:8]
        cins = refs[8:8 + len(p_in)]
        a_ref, hg_ref, hu_ref = refs[8 + len(p_in):11 + len(p_in)]
        couts = refs[11 + len(p_in):11 + len(p_in) + len(p_out)]
        sems = refs[11 + len(p_in) + len(p_out):]
        i = pl.program_id(1)
        step = pl.program_id(0) * ni + i
        if plan is not None:
            @pl.when(step == 0)
            def _():
                plan.start(cins, couts, sems)

        def conv(x_ref, p_ref, w_ref, b_ref):
            x_ext = jnp.concatenate([jnp.where(i > 0, p_ref[...], 0.0), x_ref[...]], axis=0)
            return _conv3_ext(x_ext, w_ref, b_ref)[SUBLANES:, :]

        hg = conv(g_ref, gp_ref, wg_ref, bg_ref)
        hu = conv(u_ref, up_ref, wu_ref, bu_ref)
        a_ref[...] = (_gelu(hg) * hu).astype(a_ref.dtype)
        hg_ref[...] = hg.astype(hg_ref.dtype)
        hu_ref[...] = hu.astype(hu_ref.dtype)
        if plan is not None:
            @pl.when(step == nj * ni - 1)
            def _():
                plan.wait(cins, couts, sems)

    tile = lambda off: pl.BlockSpec((tm, wc), lambda j, i: (i, off + j))
    prev = lambda off: pl.BlockSpec((SUBLANES, wc), lambda j, i: (jnp.maximum(i * hb - 1, 0), off + j))
    vec = lambda rows, off: pl.BlockSpec((rows, wc), lambda j, i: (0, off + j))
    sem = ("arbitrary", "arbitrary") if plan is not None else ("parallel", "parallel")
    res = pl.pallas_call(
        body, name=name, grid=(nj, ni),
        in_specs=[tile(0), prev(0), tile(nj), prev(nj), vec(_FFN_K, 0), vec(_FFN_K, nj), vec(1, 0),
                  vec(1, nj)] + p_in,
        out_specs=[tile(0)] * 3 + p_out,
        out_shape=[jax.ShapeDtypeStruct((lp, D_FF), BF16)] * 3 + p_shapes,
        scratch_shapes=p_scr,
        compiler_params=pltpu.CompilerParams(dimension_semantics=sem),
    )(hp, hp, hp, hp, cw, cw, cb, cb, *(plan.ins if plan is not None else []))
    return res if plan is None else (res[:3], res[3:])


def _ffn_up_convact(u, w4, cw, cb, name, plan=None):
    lp, k = u.shape
    s, _, n = w4.shape
    tm = _pick(lp, (256, 128))
    ni = lp // tm
    p_in, p_shapes, p_out, p_scr = _plan_parts(plan)

    def body(*refs):
        u_ref, w_ref, cw_ref, cb_ref = refs[:4]
        cins = refs[4:4 + len(p_in)]
        hp_ref, a_ref, hg_ref, hu_ref = refs[4 + len(p_in):8 + len(p_in)]
        couts = refs[8 + len(p_in):8 + len(p_in) + len(p_out)]
        carry = refs[8 + len(p_in) + len(p_out)]
        sems = refs[9 + len(p_in) + len(p_out):]
        i = pl.program_id(0)

        @pl.when(i == 0)
        def _():
            carry[...] = jnp.zeros_like(carry)
            if plan is not None:
                plan.start(cins, couts, sems)

        uv = u_ref[...].astype(BF16)
        for j in range(s):
            hp_ref[:, j * n:(j + 1) * n] = _dot(uv, w_ref[j])

        def conv(cols):
            x_ext = jnp.concatenate([carry[:, cols], hp_ref[:, cols]], axis=0)
            w = cw_ref[:, cols]
            y = (cb_ref[:, cols] + w[2:3, :] * x_ext + w[1:2, :] * pltpu.roll(x_ext, 1, 0)
                 + w[0:1, :] * pltpu.roll(x_ext, 2, 0))
            return y[SUBLANES:, :]

        for c in range(D_FF // n):
            gs = slice(c * n, (c + 1) * n)
            hg = conv(gs)
            hu = conv(slice(D_FF + c * n, D_FF + (c + 1) * n))
            a_ref[:, gs] = (_gelu(hg) * hu).astype(a_ref.dtype)
            hg_ref[:, gs] = hg.astype(hg_ref.dtype)
            hu_ref[:, gs] = hu.astype(hu_ref.dtype)
        carry[...] = hp_ref[tm - SUBLANES:tm, :]
        if plan is not None:
            @pl.when(i == ni - 1)
            def _():
                plan.wait(cins, couts, sems)

    half = pl.BlockSpec((tm, D_FF), lambda i: (i, 0))
    res = pl.pallas_call(
        body, name=name, grid=(ni,),
        in_specs=[pl.BlockSpec((tm, k), lambda i: (i, 0)), pl.BlockSpec((s, k, n), lambda i: (0, 0, 0)),
                  pl.BlockSpec(cw.shape, lambda i: (0, 0)), pl.BlockSpec(cb.shape, lambda i: (0, 0))] + p_in,
        out_specs=[pl.BlockSpec((tm, 2 * D_FF), lambda i: (i, 0)), half, half, half] + p_out,
        out_shape=[jax.ShapeDtypeStruct((lp, 2 * D_FF), F32)] + [jax.ShapeDtypeStruct((lp, D_FF), BF16)] * 3
        + p_shapes,
        scratch_shapes=[pltpu.VMEM((SUBLANES, 2 * D_FF), F32)] + p_scr,
        compiler_params=pltpu.CompilerParams(dimension_semantics=("arbitrary",),
                                             vmem_limit_bytes=_VMEM_LIMIT_WIDE),
    )(u, w4, cw, cb, *(plan.ins if plan is not None else []))
    return res[:4], res[4:]


def _postnorm_bwd_tile(o_ref, gain_ref, dh_ref, do_ref, dgain_ref):
    first = pl.program_id(0) == 0
    dx, dgt = _rms_bwd(o_ref[...], gain_ref[...], dh_ref[...])
    dob = dx.astype(BF16)
    do_ref[...] = dob
    _acc_add(first, dgain_ref, jnp.sum(dgt, axis=0, keepdims=True))
    return dob


def _ffn_down_act_bwd(o, gain, dh, w_down, hg, hu, name):
    lp, d = o.shape
    tm = _pick(lp, (320, 256, 128))
    tk = _FFN_WC

    def body(o_ref, gain_ref, dh_ref, w_ref, g_ref, u_ref, dg_ref, du_ref, do_ref, dgain_ref):
        dob = _postnorm_bwd_tile(o_ref, gain_ref, dh_ref, do_ref, dgain_ref)
        for j in range(D_FF // tk):
            cs = slice(j * tk, (j + 1) * tk)
            da = _dot_nt(dob, w_ref[cs, :])
            gl, dgl = _gelu_and_grad(g_ref[:, cs].astype(F32))
            dg_ref[:, cs] = (da * u_ref[:, cs].astype(F32) * dgl).astype(dg_ref.dtype)
            du_ref[:, cs] = (da * gl).astype(du_ref.dtype)

    wide = pl.BlockSpec((tm, D_FF), lambda i: (i, 0))
    row = pl.BlockSpec((tm, d), lambda i: (i, 0))
    vec = pl.BlockSpec((1, d), lambda i: (0, 0))
    return pl.pallas_call(
        body, name=name, grid=(lp // tm,),
        in_specs=[row, vec, row, pl.BlockSpec((D_FF, d), lambda i: (0, 0)), wide, wide],
        out_specs=[wide, wide, row, vec],
        out_shape=[jax.ShapeDtypeStruct((lp, D_FF), BF16)] * 2
        + [jax.ShapeDtypeStruct((lp, d), BF16), jax.ShapeDtypeStruct((1, d), F32)],
        compiler_params=pltpu.CompilerParams(dimension_semantics=("arbitrary",),
                                             vmem_limit_bytes=_VMEM_LIMIT_WIDE),
    )(o, gain, dh, w_down, hg, hu)


def _lru_gates(x, wa_ref, wx_ref, ba, bx, lam):
    xb = x.astype(BF16)
    za, zx = [], []
    for n in range(LRU_BLOCKS):
        xs = xb[:, n * LRU_BS:(n + 1) * LRU_BS]
        za.append(_dot(xs, wa_ref[n]))
        zx.append(_dot(xs, wx_ref[n]))
    r = _sigmoid(jnp.concatenate(za, axis=1) + ba)
    ig = _sigmoid(jnp.concatenate(zx, axis=1) + bx)
    sp = _softplus(-lam)
    log_a = -LRU_C * r * sp
    a = jnp.exp(log_a)
    om = _neg_expm1(2.0 * log_a)
    mult = jnp.sqrt(om)
    return xb, r, ig, sp, a, om, mult


def _lru_fwd(proj, xrc, wa, wx, ba, bx, lam, name, plan=None):
    lp, d = xrc.shape
    tm = BLOCK

    def body(r_idx, first, gate_ref, x_ref, wa_ref, wx_ref, ba_ref, bx_ref, lam_ref,
             y_ref, h_ref, carry):
        @pl.when(first)
        def _():
            carry[...] = jnp.zeros_like(carry)

        x = x_ref[...]
        _, _, ig, _, a, _, mult = _lru_gates(x, wa_ref, wx_ref, ba_ref[...], bx_ref[...], lam_ref[...])
        u = jnp.where(_real_rows(r_idx, tm), mult * ig * x, 0.0)
        acum, hloc = _scan_fwd(a, u, tm)
        h = hloc + acum * carry[0:1, :]
        h_ref[...] = h
        carry[0:1, :] = h[tm - 1:tm, :]
        y_ref[...] = (_gelu(gate_ref[...]) * h).astype(y_ref.dtype)

    return _rowcall(name, body, lp, tm, rows=[(proj, d, 0), (xrc, d, 0)],
                    vecs=[wa, wx, ba, bx, lam], outs=[(d, BF16), (d, F32)],
                    scratch=[pltpu.VMEM((SUBLANES, d), F32)], seq=True, plan=plan)


def _lru_bwd(proj, xrc, hl, dmix, wa, wx, ba, bx, lam, name, plan=None):
    lp, d = xrc.shape
    tm = BLOCK

    def body(r_idx, first, gate_ref, x_ref, h_ref, dy_ref, hp_ref, wa_ref, wx_ref, ba_ref, bx_ref,
             lam_ref, dgate_ref, dx_ref, dwa_ref, dwx_ref, dba_ref, dbx_ref, dlam_ref, carry):
        @pl.when(first)
        def _():
            carry[...] = jnp.zeros_like(carry)
            dwa_ref[...] = jnp.zeros_like(dwa_ref)
            dwx_ref[...] = jnp.zeros_like(dwx_ref)
            dba_ref[...] = jnp.zeros_like(dba_ref)
            dbx_ref[...] = jnp.zeros_like(dbx_ref)
            dlam_ref[...] = jnp.zeros_like(dlam_ref)

        x = x_ref[...]
        lam = lam_ref[...]
        xb, r, ig, sp, a, om, mult = _lru_gates(x, wa_ref, wx_ref, ba_ref[...], bx_ref[...], lam)
        h = h_ref[...]
        dy = dy_ref[...]
        gl, dgl = _gelu_and_grad(gate_ref[...])
        dgate_ref[...] = (dy * h * dgl).astype(dgate_ref.dtype)
        row = _row_iota(tm)
        lastrow = row == tm - 1
        xg = dy * gl + jnp.where(lastrow, carry[0:1, :], 0.0)
        c = jnp.where(lastrow, 1.0, pltpu.roll(a, tm - 1, 0))
        _, g = _scan_rev(c, xg, tm)
        carry[0:1, :] = a[0:1, :] * g[0:1, :]
        hprev_in = jnp.where(r_idx > 0, hp_ref[SUBLANES - 1:SUBLANES, :], 0.0)
        hprev = jnp.where(row == 0, hprev_in, pltpu.roll(h, 1, 0))
        du = jnp.where(_real_rows(r_idx, tm), g, 0.0)
        da = g * hprev
        dmult = du * ig * x
        dig = du * mult * x
        dxv = du * mult * ig
        e2 = 1.0 - om
        dlog_a = da * a - dmult * e2 / mult
        dr = dlog_a * (-LRU_C) * sp
        dsp = jnp.sum(dlog_a * (-LRU_C) * r, axis=0, keepdims=True)
        dlam_ref[...] += -dsp * _sigmoid(-lam)
        dza = dr * r * (1.0 - r)
        dzx = dig * ig * (1.0 - ig)
        dba_ref[...] += jnp.sum(dza, axis=0, keepdims=True)
        dbx_ref[...] += jnp.sum(dzx, axis=0, keepdims=True)
        dzab = dza.astype(BF16)
        dzxb = dzx.astype(BF16)
        parts = []
        for n in range(LRU_BLOCKS):
            sl = slice(n * LRU_BS, (n + 1) * LRU_BS)
            dwa_ref[n] += _dot_tn(xb[:, sl], dzab[:, sl])
            dwx_ref[n] += _dot_tn(xb[:, sl], dzxb[:, sl])
            parts.append(_dot_nt(dzab[:, sl], wa_ref[n]) + _dot_nt(dzxb[:, sl], wx_ref[n]))
        dx_ref[...] = dxv + jnp.concatenate(parts, axis=1)

    return _rowcall(name, body, lp, tm,
                    rows=[(proj, d, 0), (xrc, d, 0), (hl, d, 0), (dmix, d, 0)],
                    prevs=[(hl, d, 0)], vecs=[wa, wx, ba, bx, lam],
                    outs=[(d, BF16), (d, F32)],
                    accs=[((LRU_BLOCKS, LRU_BS, LRU_BS), F32), ((LRU_BLOCKS, LRU_BS, LRU_BS), F32),
                          ((1, d), F32), ((1, d), F32), ((1, d), F32)],
                    scratch=[pltpu.VMEM((SUBLANES, d), F32)], reverse=True, seq=True, plan=plan)


_SLOPES = [2.0 ** (-8.0 * (h + 1) / N_Q_HEADS) for h in range(N_Q_HEADS)]
_QK_SCALE = HEAD_DIM ** -0.5
_QCOL = 2 * D_MODEL // D_MODEL
_KCOL = (3 * D_MODEL) // LANES
_VCOL = _KCOL + 1


def _attn_masks(n):
    start = pl.multiple_of(jnp.maximum(n - 1, 0) * BLOCK, BLOCK)
    qi = n * BLOCK + lax.broadcasted_iota(jnp.int32, (BLOCK, 2 * BLOCK), 0)
    kj = start + lax.broadcasted_iota(jnp.int32, (BLOCK, 2 * BLOCK), 1)
    dist = qi - kj
    ok = (kj >= BLOCK) & (dist >= 0) & (dist < WINDOW)
    dm = (n * BLOCK - PAD + lax.broadcasted_iota(jnp.int32, (BLOCK, N_META), 0)
          - lax.broadcasted_iota(jnp.int32, (BLOCK, N_META), 1))
    okm = dm >= 0
    return start, ok, dist.astype(F32), okm, jnp.minimum(dm, WINDOW).astype(F32)


def _group_rows(ref, g, scale=None):
    x = jnp.concatenate(
        [ref[:, (g * Q_PER_KV + hh) * HEAD_DIM:(g * Q_PER_KV + hh + 1) * HEAD_DIM] for hh in range(Q_PER_KV)],
        axis=0)
    return (x if scale is None else x * scale).astype(BF16)


def _attn_probs(s, sm, sink_ref, g, ok, distf, okm, dmf):
    slope = jnp.stack([jnp.full((1, 1), _SLOPES[g * Q_PER_KV + hh], F32) for hh in range(Q_PER_KV)])
    sink = jnp.stack([sink_ref[0:1, g * Q_PER_KV + hh:g * Q_PER_KV + hh + 1] for hh in range(Q_PER_KV)])
    s = s.reshape(Q_PER_KV, BLOCK, 2 * BLOCK)
    sm = sm.reshape(Q_PER_KV, BLOCK, N_META)
    s = jnp.where(ok[None], s - slope * distf[None], NEG)
    sm = jnp.where(okm[None], sm - slope * dmf[None], NEG)
    mx = jnp.maximum(jnp.maximum(jnp.max(s, axis=-1, keepdims=True),
                                 jnp.max(sm, axis=-1, keepdims=True)), sink)
    p = jnp.exp(s - mx)
    pm = jnp.exp(sm - mx)
    ps = jnp.exp(sink - mx)
    inv = 1.0 / (jnp.sum(p, axis=-1, keepdims=True) + jnp.sum(pm, axis=-1, keepdims=True) + ps)
    return p, pm, ps, inv


def _attn_fwd(proj, sinks, name, plan=None):
    lp = proj.shape[0]
    nblk = lp // BLOCK
    p_in, p_shapes, p_out, p_scr = _plan_parts(plan)

    def body(*refs):
        q_ref, k_ref, v_ref, sink_ref = refs[:4]
        cins = refs[4:4 + len(p_in)]
        o_ref = refs[4 + len(p_in)]
        couts = refs[5 + len(p_in):5 + len(p_in) + len(p_out)]
        sems = refs[5 + len(p_in) + len(p_out):]
        n = pl.program_id(0)
        if plan is not None:
            @pl.when(n == 0)
            def _():
                plan.start(cins, couts, sems)

        start, ok, distf, okm, dmf = _attn_masks(n)
        kb = k_ref[pl.ds(start, 2 * BLOCK), :].astype(BF16)
        vb = v_ref[pl.ds(start, 2 * BLOCK), :].astype(BF16)
        km = k_ref[PAD:BLOCK, :].astype(BF16)
        vm = v_ref[PAD:BLOCK, :].astype(BF16)
        rows = Q_PER_KV * BLOCK
        gsl = [slice(g * HEAD_DIM, (g + 1) * HEAD_DIM) for g in range(N_KV_HEADS)]
        raw = []
        for g in range(N_KV_HEADS):
            qg = _group_rows(q_ref, g, _QK_SCALE)
            raw.append((_dot_nt(qg, kb[:, gsl[g]]), _dot_nt(qg, km[:, gsl[g]])))
        for g in range(N_KV_HEADS):
            gs = gsl[g]
            p, pm, _, inv = _attn_probs(raw[g][0], raw[g][1], sink_ref, g, ok, distf, okm, dmf)
            o = (_dot(p.astype(BF16).reshape(rows, 2 * BLOCK), vb[:, gs])
                 + _dot(pm.astype(BF16).reshape(rows, N_META), vm[:, gs])) * inv.reshape(rows, 1)
            for hh in range(Q_PER_KV):
                h = g * Q_PER_KV + hh
                o_ref[:, h * HEAD_DIM:(h + 1) * HEAD_DIM] = o[hh * BLOCK:(hh + 1) * BLOCK, :].astype(o_ref.dtype)
        if plan is not None:
            @pl.when(n == nblk - 1)
            def _():
                plan.wait(cins, couts, sems)

    res = pl.pallas_call(
        body, name=name, grid=(nblk,),
        in_specs=[pl.BlockSpec((BLOCK, D_MODEL), lambda n: (n, _QCOL)),
                  pl.BlockSpec((lp, LANES), lambda n: (0, _KCOL)),
                  pl.BlockSpec((lp, LANES), lambda n: (0, _VCOL)),
                  pl.BlockSpec(sinks.shape, lambda n: (0, 0))] + p_in,
        out_specs=[pl.BlockSpec((BLOCK, D_MODEL), lambda n: (n, 0))] + p_out,
        out_shape=[jax.ShapeDtypeStruct((lp, D_MODEL), BF16)] + p_shapes,
        scratch_shapes=p_scr,
        compiler_params=pltpu.CompilerParams(dimension_semantics=("arbitrary",)),
    )(proj, proj, proj, sinks, *(plan.ins if plan is not None else []))
    return res[0], res[1:]


def _attn_bwd(proj, sinks, dmix, name, plan=None):
    lp = proj.shape[0]
    nblk = lp // BLOCK

    p_in, p_shapes, p_out, p_scr = _plan_parts(plan)

    def body(*refs):
        q_ref, k_ref, v_ref, sink_ref, dy_ref = refs[:5]
        cins = refs[5:5 + len(p_in)]
        dq_ref, dk_ref, dv_ref, ds_ref = refs[5 + len(p_in):9 + len(p_in)]
        couts = refs[9 + len(p_in):9 + len(p_in) + len(p_out)]
        sems = refs[9 + len(p_in) + len(p_out):]
        n = pl.program_id(0)

        @pl.when(n == 0)
        def _():
            dk_ref[...] = jnp.zeros_like(dk_ref)
            dv_ref[...] = jnp.zeros_like(dv_ref)
            ds_ref[...] = jnp.zeros_like(ds_ref)
            if plan is not None:
                plan.start(cins, couts, sems)

        start, ok, distf, okm, dmf = _attn_masks(n)
        kb = k_ref[pl.ds(start, 2 * BLOCK), :].astype(BF16)
        vb = v_ref[pl.ds(start, 2 * BLOCK), :].astype(BF16)
        km = k_ref[PAD:BLOCK, :].astype(BF16)
        vm = v_ref[PAD:BLOCK, :].astype(BF16)
        lane16 = lax.broadcasted_iota(jnp.int32, (1, N_Q_HEADS), 1)
        dsink = jnp.zeros((1, N_Q_HEADS), F32)
        rows = Q_PER_KV * BLOCK
        gsl = [slice(g * HEAD_DIM, (g + 1) * HEAD_DIM) for g in range(N_KV_HEADS)]
        pre = []
        for g in range(N_KV_HEADS):
            qg = _group_rows(q_ref, g, _QK_SCALE)
            dog = _group_rows(dy_ref, g)
            pre.append((qg, dog, _dot_nt(qg, kb[:, gsl[g]]), _dot_nt(qg, km[:, gsl[g]]),
                        _dot_nt(dog, vb[:, gsl[g]]), _dot_nt(dog, vm[:, gsl[g]])))
        for g in range(N_KV_HEADS):
            gs = gsl[g]
            qg, dog, s_raw, sm_raw, dp, dpm = pre[g]
            p, pm, ps, inv = _attn_probs(s_raw, sm_raw, sink_ref, g, ok, distf, okm, dmf)
            pn, pmn, psn = p * inv, pm * inv, ps * inv
            dp = dp.reshape(Q_PER_KV, BLOCK, 2 * BLOCK)
            dpm = dpm.reshape(Q_PER_KV, BLOCK, N_META)
            delta = (jnp.sum(pn * dp, axis=-1, keepdims=True)
                     + jnp.sum(pmn * dpm, axis=-1, keepdims=True))
            dsb = (pn * (dp - delta)).astype(BF16).reshape(rows, 2 * BLOCK)
            dsm = (pmn * (dpm - delta)).astype(BF16).reshape(rows, N_META)
            dsk = jnp.sum(psn * delta, axis=1, keepdims=True)
            for hh in range(Q_PER_KV):
                dsink = dsink - jnp.where(lane16 == g * Q_PER_KV + hh, dsk[hh], 0.0)
            dq = (_dot(dsb, kb[:, gs]) + _dot(dsm, km[:, gs])) * _QK_SCALE
            for hh in range(Q_PER_KV):
                h = g * Q_PER_KV + hh
                dq_ref[:, h * HEAD_DIM:(h + 1) * HEAD_DIM] = dq[hh * BLOCK:(hh + 1) * BLOCK, :].astype(dq_ref.dtype)
            pnb = pn.astype(BF16).reshape(rows, 2 * BLOCK)
            pmnb = pmn.astype(BF16).reshape(rows, N_META)
            dk_ref[pl.ds(start, 2 * BLOCK), gs] += _dot_tn(dsb, qg)
            dv_ref[pl.ds(start, 2 * BLOCK), gs] += _dot_tn(pnb, dog)
            dk_ref[PAD:BLOCK, gs] += _dot_tn(dsm, qg)
            dv_ref[PAD:BLOCK, gs] += _dot_tn(pmnb, dog)
        ds_ref[...] += dsink
        if plan is not None:
            @pl.when(n == nblk - 1)
            def _():
                plan.wait(cins, couts, sems)

    res = pl.pallas_call(
        body, name=name, grid=(nblk,),
        in_specs=[pl.BlockSpec((BLOCK, D_MODEL), lambda n: (n, _QCOL)),
                  pl.BlockSpec((lp, LANES), lambda n: (0, _KCOL)),
                  pl.BlockSpec((lp, LANES), lambda n: (0, _VCOL)),
                  pl.BlockSpec(sinks.shape, lambda n: (0, 0)),
                  pl.BlockSpec((BLOCK, D_MODEL), lambda n: (n, 1))] + p_in,
        out_specs=[pl.BlockSpec((BLOCK, D_MODEL), lambda n: (n, 0)),
                   pl.BlockSpec((lp, LANES), lambda n: (0, 0)),
                   pl.BlockSpec((lp, LANES), lambda n: (0, 0)),
                   pl.BlockSpec((1, N_Q_HEADS), lambda n: (0, 0))] + p_out,
        out_shape=[jax.ShapeDtypeStruct((lp, D_MODEL), BF16),
                   jax.ShapeDtypeStruct((lp, LANES), F32),
                   jax.ShapeDtypeStruct((lp, LANES), F32),
                   jax.ShapeDtypeStruct((1, N_Q_HEADS), F32)] + p_shapes,
        scratch_shapes=p_scr,
        compiler_params=pltpu.CompilerParams(dimension_semantics=("arbitrary",)),
    )(proj, proj, proj, sinks, dmix, *(plan.ins if plan is not None else []))
    return res[:4], res[4:]


_ZW = D_SSM
_XBC_W = D_SSM + 2 * SSD_GROUPS * SSD_N
_DT_COL = (_ZW + _XBC_W) // LANES
EVEN_IN = 3 * D_MODEL + 2 * LANES
ODD_IN = _ZW + _XBC_W + SSD_HEADS
ODD_IN_PAD = _ZW + _XBC_W + LANES


def _ssm_convprep_fwd(proj, cw, cb, dt_bias, name):
    lp = proj.shape[0]
    kk = cw.shape[0]
    tm, wc = _conv_tiles(lp, _XBC_W)
    offb = _ZW // wc
    nj = _XBC_W // wc
    hb = tm // SUBLANES

    def body(x_ref, xp_ref, dtr_ref, w_ref, b_ref, bias_ref, xc_ref, act_ref, dt_ref):
        i, j = pl.program_id(0), pl.program_id(1)
        real = _real_rows(i, tm)
        xv = x_ref[...]
        xx = jnp.concatenate([jnp.where(i > 0, xp_ref[...], 0.0), xv], axis=0)
        acc = b_ref[...] + w_ref[kk - 1:kk, :] * xv
        for m in range(1, kk):
            acc = acc + w_ref[kk - 1 - m:kk - m, :] * pltpu.roll(xx, m, 0)[SUBLANES:, :]
        xc_ref[...] = acc
        act, _ = _silu_and_grad(acc)
        act_ref[...] = jnp.where(real, act, 0.0)

        @pl.when(j == 0)
        def _():
            dt_ref[...] = jnp.where(real, _softplus(dtr_ref[...] + bias_ref[...]), 0.0)

    return pl.pallas_call(
        body, name=name, grid=(lp // tm, nj),
        in_specs=[pl.BlockSpec((tm, wc), lambda i, j: (i, offb + j)),
                  pl.BlockSpec((SUBLANES, wc), lambda i, j: (jnp.maximum(i * hb - 1, 0), offb + j)),
                  pl.BlockSpec((tm, LANES), lambda i, j: (i, _DT_COL)),
                  pl.BlockSpec((kk, wc), lambda i, j: (0, j)),
                  pl.BlockSpec((1, wc), lambda i, j: (0, j)),
                  pl.BlockSpec((1, LANES), lambda i, j: (0, 0))],
        out_specs=[pl.BlockSpec((tm, wc), lambda i, j: (i, j)),
                   pl.BlockSpec((tm, wc), lambda i, j: (i, j)),
                   pl.BlockSpec((tm, LANES), lambda i, j: (i, 0))],
        out_shape=[jax.ShapeDtypeStruct((lp, _XBC_W), F32), jax.ShapeDtypeStruct((lp, _XBC_W), F32),
                   jax.ShapeDtypeStruct((lp, LANES), F32)],
        compiler_params=pltpu.CompilerParams(dimension_semantics=("parallel", "arbitrary")),
    )(proj, proj, proj, cw, cb, dt_bias)


def _ssm_prep_bwd(xc, proj, dt_bias, dxs, dxskip, db, dc, ddt, name):
    lp = xc.shape[0]
    tm = BLOCK

    def body(r, first, xc_ref, dtr_ref, dxs_ref, dsk_ref, db_ref, dc_ref, ddt_ref, b_ref,
             dxc_ref, ddtr_ref, dbias_ref):
        real = _real_rows(r, tm)
        _, ds = _silu_and_grad(xc_ref[...])
        up = lambda ref: ref[...].astype(F32)
        dxc_ref[:, :D_SSM] = jnp.where(
            real, (up(dxs_ref) + up(dsk_ref)) * ds[:, :D_SSM], 0.0).astype(dxc_ref.dtype)
        dxc_ref[:, D_SSM:D_SSM + 1024] = jnp.where(
            real, up(db_ref) * ds[:, D_SSM:D_SSM + 1024], 0.0).astype(dxc_ref.dtype)
        dxc_ref[:, D_SSM + 1024:] = jnp.where(
            real, up(dc_ref) * ds[:, D_SSM + 1024:], 0.0).astype(dxc_ref.dtype)
        dd = jnp.where(real, ddt_ref[...] * _sigmoid(dtr_ref[...] + b_ref[...]), 0.0)
        ddtr_ref[...] = dd.astype(ddtr_ref.dtype)
        _acc_add(first, dbias_ref, jnp.sum(dd, axis=0, keepdims=True))

    return _rowcall(name, body, lp, tm,
                    rows=[(xc, _XBC_W, 0), (proj, LANES, _DT_COL), (dxs, D_SSM, 0), (dxskip, D_SSM, 0),
                          (db, 1024, 0), (dc, 1024, 0), (ddt, LANES, 0)],
                    vecs=[dt_bias], outs=[(_XBC_W, BF16), (LANES, BF16)], accs=[((1, LANES), F32)])


def _ssd_common(dt, alog):
    a = -jnp.exp(alog)
    cs = _cumsum_rows(dt * a, BLOCK)
    cst = cs.T
    cl = cs[BLOCK - 1:BLOCK, :]
    tril = (lax.broadcasted_iota(jnp.int32, (BLOCK, BLOCK), 0)
            >= lax.broadcasted_iota(jnp.int32, (BLOCK, BLOCK), 1))
    return a, cs, cst, cl, jnp.exp(cs), jnp.exp(cl - cs), jnp.exp(cl), tril


def _head_cols(ecl, g):
    lane = lax.broadcasted_iota(jnp.int32, (1, SSD_HPG * SSD_P), 1)
    e = [ecl[:, SSD_HPG * g + hh:SSD_HPG * g + hh + 1] for hh in range(SSD_HPG)]
    return jnp.where(lane < SSD_P, e[0], jnp.where(lane < 2 * SSD_P, e[1],
                                                   jnp.where(lane < 3 * SSD_P, e[2], e[3])))


def _ssd_fwd(xbc, dt, alog, name, plan=None):
    lp = xbc.shape[0]
    nc = lp // BLOCK
    gw = SSD_HPG * SSD_P
    p_in, p_shapes, p_out, p_scr = _plan_parts(plan)

    def body(*refs):
        xs_ref, b_ref, c_ref, dt_ref, alog_ref = refs[:5]
        cins = refs[5:5 + len(p_in)]
        y_ref, so_ref = refs[5 + len(p_in):7 + len(p_in)]
        couts = refs[7 + len(p_in):7 + len(p_in) + len(p_out)]
        st, fx = refs[7 + len(p_in) + len(p_out):9 + len(p_in) + len(p_out)]
        sems = refs[9 + len(p_in) + len(p_out):]
        n = pl.program_id(0)

        @pl.when(n == 0)
        def _():
            st[...] = jnp.zeros_like(st)
            if plan is not None:
                plan.start(cins, couts, sems)

        dtv = dt_ref[...]
        _, cs, cst, cl, e, f, ecl, tril = _ssd_common(dtv, alog_ref[...])
        pre = []
        for g in range(SSD_GROUPS):
            bg = b_ref[:, g * SSD_N:(g + 1) * SSD_N].astype(BF16)
            cg = c_ref[:, g * SSD_N:(g + 1) * SSD_N].astype(BF16)
            stg = st[g]
            so_ref[0, g] = stg
            pre.append((bg, stg, _dot_nt(cg, bg), _dot(cg, stg.astype(BF16))))
        for g in range(SSD_GROUPS):
            bg, stg, gm, yoff = pre[g]
            heads = [SSD_HPG * g + hh for hh in range(SSD_HPG)]
            cols = lambda v: jnp.stack([v[:, h:h + 1] for h in heads])
            x4 = jnp.stack([xs_ref[:, h * SSD_P:(h + 1) * SSD_P] for h in heads])
            csr = jnp.stack([cst[h:h + 1, :] for h in heads])
            m = gm[None] * jnp.exp(jnp.where(tril[None], cols(cs) - csr, NEG))
            xdt = x4 * cols(dtv)
            yoff4 = jnp.stack([yoff[:, hh * SSD_P:(hh + 1) * SSD_P] for hh in range(SSD_HPG)])
            y4 = (jnp.einsum("hls,hsp->hlp", m.astype(BF16), xdt.astype(BF16), preferred_element_type=F32)
                  + cols(e) * yoff4)
            fx4 = cols(f) * xdt
            for hh, h in enumerate(heads):
                y_ref[:, h * SSD_P:(h + 1) * SSD_P] = y4[hh]
                fx[:, hh * SSD_P:(hh + 1) * SSD_P] = fx4[hh]
            st[g] = stg * _head_cols(ecl, g) + _dot_tn(bg, fx[...].astype(BF16))
        if plan is not None:
            @pl.when(n == nc - 1)
            def _():
                plan.wait(cins, couts, sems)

    res = pl.pallas_call(
        body, name=name, grid=(nc,),
        in_specs=[pl.BlockSpec((BLOCK, D_SSM), lambda n: (n, 0)),
                  pl.BlockSpec((BLOCK, 1024), lambda n: (n, 2)),
                  pl.BlockSpec((BLOCK, 1024), lambda n: (n, 3)),
                  pl.BlockSpec((BLOCK, LANES), lambda n: (n, 0)),
                  pl.BlockSpec((1, LANES), lambda n: (0, 0))] + p_in,
        out_specs=[pl.BlockSpec((BLOCK, D_SSM), lambda n: (n, 0)),
                   pl.BlockSpec((1, SSD_GROUPS, SSD_N, gw), lambda n: (n, 0, 0, 0))] + p_out,
        out_shape=[jax.ShapeDtypeStruct((lp, D_SSM), F32),
                   jax.ShapeDtypeStruct((nc, SSD_GROUPS, SSD_N, gw), F32)] + p_shapes,
        scratch_shapes=[pltpu.VMEM((SSD_GROUPS, SSD_N, gw), F32), pltpu.VMEM((BLOCK, gw), F32)] + p_scr,
        compiler_params=pltpu.CompilerParams(dimension_semantics=("arbitrary",)),
    )(xbc, xbc, xbc, dt, alog, *(plan.ins if plan is not None else []))
    return res[:2], res[2:]


def _ssd_bwd(xbc, dt, alog, states, dy, name, plan=None):
    lp = xbc.shape[0]
    nc = lp // BLOCK
    gw = SSD_HPG * SSD_P
    p_in, p_shapes, p_out, p_scr = _plan_parts(plan)

    def body(*refs):
        xs_ref, b_ref, c_ref, dt_ref, alog_ref, dy_ref, st_ref = refs[:7]
        cins = refs[7:7 + len(p_in)]
        dxs_ref, db_ref, dc_ref, ddt_ref, dalog_ref = refs[7 + len(p_in):12 + len(p_in)]
        couts = refs[12 + len(p_in):12 + len(p_in) + len(p_out)]
        dst, edy, fx = refs[12 + len(p_in) + len(p_out):15 + len(p_in) + len(p_out)]
        sems = refs[15 + len(p_in) + len(p_out):]
        i = pl.program_id(0)

        @pl.when(i == 0)
        def _():
            dst[...] = jnp.zeros_like(dst)
            dalog_ref[...] = jnp.zeros_like(dalog_ref)
            if plan is not None:
                plan.start(cins, couts, sems)

        dtv = dt_ref[...]
        a, cs, cst, cl, e, f, ecl, tril = _ssd_common(dtv, alog_ref[...])
        lane = lax.broadcasted_iota(jnp.int32, (1, LANES), 1)
        sub = _row_iota(BLOCK)
        triu = (lax.broadcasted_iota(jnp.int32, (BLOCK, BLOCK), 1)
                >= lax.broadcasted_iota(jnp.int32, (BLOCK, BLOCK), 0))
        dcs = jnp.zeros((BLOCK, LANES), F32)
        dcst = jnp.zeros((LANES, BLOCK), F32)
        dcl = jnp.zeros((1, LANES), F32)
        ddtx = jnp.zeros((BLOCK, LANES), F32)
        pre = []
        for g in range(SSD_GROUPS):
            bg = b_ref[:, g * SSD_N:(g + 1) * SSD_N].astype(BF16)
            cg = c_ref[:, g * SSD_N:(g + 1) * SSD_N].astype(BF16)
            stb = st_ref[0, g].astype(BF16)
            dsob = dst[g].astype(BF16)
            pre.append((bg, cg, stb, dsob, _dot_nt(cg, bg), _dot_nt(bg, cg), _dot(cg, stb), _dot(bg, dsob)))
        for g in range(SSD_GROUPS):
            bg, cg, stb, dsob, gm, gmt, yraw, dfx = pre[g]
            dso = dst[g]
            prodsum = jnp.sum(dso * st_ref[0, g], axis=0, keepdims=True)
            heads = [SSD_HPG * g + hh for hh in range(SSD_HPG)]
            cols = lambda v: jnp.stack([v[:, h:h + 1] for h in heads])
            parts = lambda v: jnp.stack([v[:, hh * SSD_P:(hh + 1) * SSD_P] for hh in range(SSD_HPG)])
            x4 = jnp.stack([xs_ref[:, h * SSD_P:(h + 1) * SSD_P] for h in heads])
            dy4 = jnp.stack([dy_ref[:, h * SSD_P:(h + 1) * SSD_P] for h in heads])
            csc, dtc, ec, fc = cols(cs), cols(dtv), cols(e), cols(f)
            csr = jnp.stack([cst[h:h + 1, :] for h in heads])
            seg = csc - csr
            lam = jnp.exp(jnp.where(tril[None], seg, NEG))
            lamt = jnp.exp(jnp.where(triu[None], -seg, NEG))
            mt = gmt[None] * lamt
            xdt = x4 * dtc
            dyb = dy4.astype(BF16)
            dm = jnp.einsum("hlp,hsp->hls", dyb, xdt.astype(BF16), preferred_element_type=F32)
            dfx4 = parts(dfx)
            dxdt = jnp.einsum("hsl,hlp->hsp", mt.astype(BF16), dyb, preferred_element_type=F32) + fc * dfx4
            dml = dm * lam
            w = dml * gm[None]
            dgm = jnp.sum(dml, axis=0)
            dff = jnp.sum(dfx4 * xdt, axis=2, keepdims=True) * fc
            colv = (jnp.sum(w, axis=2, keepdims=True)
                    + jnp.sum(dy4 * parts(yraw), axis=2, keepdims=True) * ec - dff)
            roww = jnp.sum(w, axis=1, keepdims=True)
            ddtc = jnp.sum(dxdt * x4, axis=2, keepdims=True)
            dffs = jnp.sum(dff, axis=1, keepdims=True)
            dxs4 = dxdt * dtc
            edy4 = ec * dy4
            fx4 = fc * xdt
            for hh, h in enumerate(heads):
                ls = slice(hh * SSD_P, (hh + 1) * SSD_P)
                onl = (lane == h).astype(F32)
                dcs = dcs + colv[hh] * onl
                dcst = dcst - (sub == h).astype(F32) * roww[hh]
                dcl = dcl + (dffs[hh] + ecl[:, h:h + 1] * jnp.sum(prodsum[:, ls], axis=1, keepdims=True)) * onl
                ddtx = ddtx + ddtc[hh] * onl
                dxs_ref[:, h * SSD_P:(h + 1) * SSD_P] = dxs4[hh].astype(dxs_ref.dtype)
                edy[:, ls] = edy4[hh]
                fx[:, ls] = fx4[hh]
            edyb = edy[...].astype(BF16)
            fxb = fx[...].astype(BF16)
            dgb = dgm.astype(BF16)
            dc_ref[:, g * SSD_N:(g + 1) * SSD_N] = (_dot_nt(edyb, stb) + _dot(dgb, bg)).astype(dc_ref.dtype)
            db_ref[:, g * SSD_N:(g + 1) * SSD_N] = (_dot_nt(fxb, dsob) + _dot_tn(dgb, cg)).astype(db_ref.dtype)
            dst[g] = dso * _head_cols(ecl, g) + _dot_tn(cg, edyb)
        dcs = dcs + dcst.T + jnp.where(sub == BLOCK - 1, dcl, 0.0)
        dda = _rev_cumsum_rows(dcs, BLOCK)
        ddt_ref[...] = ddtx + dda * a
        dalog_ref[...] += jnp.sum(dda * dtv, axis=0, keepdims=True) * a
        if plan is not None:
            @pl.when(i == nc - 1)
            def _():
                plan.wait(cins, couts, sems)

    rev = lambda i: nc - 1 - i
    res = pl.pallas_call(
        body, name=name, grid=(nc,),
        in_specs=[pl.BlockSpec((BLOCK, D_SSM), lambda i: (rev(i), 0)),
                  pl.BlockSpec((BLOCK, 1024), lambda i: (rev(i), 2)),
                  pl.BlockSpec((BLOCK, 1024), lambda i: (rev(i), 3)),
                  pl.BlockSpec((BLOCK, LANES), lambda i: (rev(i), 0)),
                  pl.BlockSpec((1, LANES), lambda i: (0, 0)),
                  pl.BlockSpec((BLOCK, D_SSM), lambda i: (rev(i), 0)),
                  pl.BlockSpec((1, SSD_GROUPS, SSD_N, gw), lambda i: (rev(i), 0, 0, 0))] + p_in,
        out_specs=[pl.BlockSpec((BLOCK, D_SSM), lambda i: (rev(i), 0)),
                   pl.BlockSpec((BLOCK, 1024), lambda i: (rev(i), 0)),
                   pl.BlockSpec((BLOCK, 1024), lambda i: (rev(i), 0)),
                   pl.BlockSpec((BLOCK, LANES), lambda i: (rev(i), 0)),
                   pl.BlockSpec((1, LANES), lambda i: (0, 0))] + p_out,
        out_shape=[jax.ShapeDtypeStruct((lp, D_SSM), BF16),
                   jax.ShapeDtypeStruct((lp, 1024), BF16),
                   jax.ShapeDtypeStruct((lp, 1024), BF16),
                   jax.ShapeDtypeStruct((lp, LANES), F32),
                   jax.ShapeDtypeStruct((1, LANES), F32)] + p_shapes,
        scratch_shapes=[pltpu.VMEM((SSD_GROUPS, SSD_N, gw), F32),
                        pltpu.VMEM((BLOCK, gw), F32), pltpu.VMEM((BLOCK, gw), F32)] + p_scr,
        compiler_params=pltpu.CompilerParams(dimension_semantics=("arbitrary",)),
    )(xbc, xbc, xbc, dt, alog, dy, states, *(plan.ins if plan is not None else []))
    return res[:5], res[5:]


_GN_GROUPS = 8
_GN_W = D_SSM // _GN_GROUPS


def _ssm_gate_out_postnorm(yssd, xbc, proj, dskip, gnorm, w_out, h, g, next_pre, name):
    lp, d = h.shape
    tm = _pick(lp, (320, 256, 128))

    def body(y_ref, x_ref, z_ref, d_ref, gn_ref, w_ref, h_ref, g_ref, np_ref, yn_ref, o_ref, hn_ref, u_ref):
        sz, _ = _silu_and_grad(z_ref[...])
        y2 = (y_ref[...] + d_ref[...] * x_ref[...]) * sz
        for k in range(_GN_GROUPS):
            sl = slice(k * _GN_W, (k + 1) * _GN_W)
            yk = y2[:, sl]
            rs = lax.rsqrt(jnp.mean(yk * yk, axis=-1, keepdims=True) + EPS)
            yn_ref[:, sl] = (yk * rs * gn_ref[:, sl]).astype(yn_ref.dtype)
        acc = _dot(yn_ref[...], w_ref[...])
        o_ref[...] = acc
        rs = lax.rsqrt(jnp.mean(acc * acc, axis=-1, keepdims=True) + EPS)
        hn = jnp.where(_real_rows(pl.program_id(0), tm), h_ref[...] + acc * rs * g_ref[...], 0.0)
        hn_ref[...] = hn
        rs2 = lax.rsqrt(jnp.mean(hn * hn, axis=-1, keepdims=True) + EPS)
        u_ref[...] = (hn * rs2 * np_ref[...]).astype(u_ref.dtype)

    wide = pl.BlockSpec((tm, D_SSM), lambda i: (i, 0))
    row = pl.BlockSpec((tm, d), lambda i: (i, 0))
    vec = lambda n: pl.BlockSpec((1, n), lambda i: (0, 0))
    return pl.pallas_call(
        body, name=name, grid=(lp // tm,),
        in_specs=[wide, wide, wide, vec(D_SSM), vec(D_SSM), pl.BlockSpec((D_SSM, d), lambda i: (0, 0)),
                  row, vec(d), vec(d)],
        out_specs=[wide, row, row, row],
        out_shape=[jax.ShapeDtypeStruct((lp, D_SSM), BF16), jax.ShapeDtypeStruct((lp, d), F32),
                   jax.ShapeDtypeStruct((lp, d), F32), jax.ShapeDtypeStruct((lp, d), BF16)],
        compiler_params=pltpu.CompilerParams(dimension_semantics=("parallel",),
                                             vmem_limit_bytes=_VMEM_LIMIT_WIDE),
    )(yssd, xbc, proj, dskip, gnorm, w_out, h, g, next_pre)


def _ssm_out_gate_bwd(o, gain, dh, w_out, yssd, xbc, proj, dskip, gnorm, name):
    lp, d = o.shape
    tm = _pick(lp, (320, 256, 128))
    tk = D_SSM // 2

    def body(o_ref, gain_ref, dh_ref, w_ref, y_ref, x_ref, z_ref, d_ref, g_ref,
             dy_ref, dx_ref, dz_ref, dd_ref, dg_ref, do_ref, dgain_ref):
        first = pl.program_id(0) == 0
        dob = _postnorm_bwd_tile(o_ref, gain_ref, dh_ref, do_ref, dgain_ref)
        z = z_ref[...]
        sz, dsz = _silu_and_grad(z)
        xs = x_ref[...]
        y1 = y_ref[...] + d_ref[...] * xs
        y2 = y1 * sz
        for k in range(_GN_GROUPS):
            sl = slice(k * _GN_W, (k + 1) * _GN_W)
            if k % (tk // _GN_W) == 0:
                dyn = _dot_nt(dob, w_ref[k * _GN_W:k * _GN_W + tk, :])
            loc = slice((k % (tk // _GN_W)) * _GN_W, (k % (tk // _GN_W) + 1) * _GN_W)
            dx, dgt = _rms_bwd(y2[:, sl], g_ref[:, sl], dyn[:, loc])
            dy1 = dx * sz[:, sl]
            dy_ref[:, sl] = dy1.astype(dy_ref.dtype)
            dx_ref[:, sl] = (dy1 * d_ref[:, sl]).astype(dx_ref.dtype)
            dz_ref[:, sl] = (dx * y1[:, sl] * dsz[:, sl]).astype(dz_ref.dtype)

            @pl.when(first)
            def _():
                dd_ref[:, sl] = jnp.zeros((1, _GN_W), F32)
                dg_ref[:, sl] = jnp.zeros((1, _GN_W), F32)

            dd_ref[:, sl] += jnp.sum(dy1 * xs[:, sl], axis=0, keepdims=True)
            dg_ref[:, sl] += jnp.sum(dgt, axis=0, keepdims=True)

    tile = pl.BlockSpec((tm, D_SSM), lambda i: (i, 0))
    vec = pl.BlockSpec((1, D_SSM), lambda i: (0, 0))
    row = pl.BlockSpec((tm, d), lambda i: (i, 0))
    rvec = pl.BlockSpec((1, d), lambda i: (0, 0))
    return pl.pallas_call(
        body, name=name, grid=(lp // tm,),
        in_specs=[row, rvec, row, pl.BlockSpec((D_SSM, d), lambda i: (0, 0)), tile, tile, tile, vec, vec],
        out_specs=[tile, tile, tile, vec, vec, row, rvec],
        out_shape=[jax.ShapeDtypeStruct((lp, D_SSM), BF16)] * 3 + [jax.ShapeDtypeStruct((1, D_SSM), F32)] * 2
        + [jax.ShapeDtypeStruct((lp, d), BF16), jax.ShapeDtypeStruct((1, d), F32)],
        compiler_params=pltpu.CompilerParams(dimension_semantics=("arbitrary",),
                                             vmem_limit_bytes=_VMEM_LIMIT_WIDE),
    )(o, gain, dh, w_out, yssd, xbc, proj, dskip, gnorm)


def _shape2d(shape):
    n = math.prod(shape)
    if len(shape) == 2:
        return tuple(shape)
    return (n // LANES, LANES) if n % LANES == 0 else (1, n)


def _adamw_many(ws, gs, ms, vs, name):
    n = len(ws)
    c1 = 1.0 / (1.0 - ADAM_B1 ** ADAM_STEP)
    c2 = 1.0 / (1.0 - ADAM_B2 ** ADAM_STEP)

    def body(*refs):
        for i in range(n):
            w_ref, g_ref, m_ref, v_ref = (refs[j * n + i] for j in range(4))
            d_ref, nm_ref, nv_ref = (refs[(4 + j) * n + i] for j in range(3))
            gv = g_ref[...]
            nm = ADAM_B1 * m_ref[...] + (1.0 - ADAM_B1) * gv
            nv = ADAM_B2 * v_ref[...] + (1.0 - ADAM_B2) * (gv * gv)
            nm_ref[...] = nm
            nv_ref[...] = nv
            d_ref[...] = -ADAM_LR * ((nm * c1) / (jnp.sqrt(nv * c2) + ADAM_EPS) + ADAM_WD * w_ref[...])

    vm = pl.BlockSpec(memory_space=pltpu.VMEM)
    return pl.pallas_call(
        body, name=name, in_specs=[vm] * (4 * n), out_specs=[vm] * (3 * n),
        out_shape=[jax.ShapeDtypeStruct(w.shape, F32) for w in ws] * 3,
    )(*ws, *gs, *ms, *vs)


def _place():
    return lax.axis_index("x"), lax.axis_index("y"), lax.axis_index("c")


def _other_chips(x, y):
    return [(1 - x, y), (x, 1 - y), (1 - x, 1 - y)]


_ANY = pl.BlockSpec(memory_space=pl.ANY)


class _Plan:
    def __init__(self, ins, out_shapes, n_remote, n_local, issue):
        self.ins = list(ins)
        self.out_shapes = list(out_shapes)
        self.issue = issue
        self.scratch = [pltpu.SemaphoreType.DMA((max(n_remote, 1),)),
                        pltpu.SemaphoreType.DMA((max(n_remote, 1),)),
                        pltpu.SemaphoreType.DMA((max(n_local, 1),))]

    def start(self, ins, outs, sems):
        sends, _, locs = self.issue(ins, outs, *sems)
        for cp in locs + sends:
            cp.start()

    def wait(self, ins, outs, sems):
        sends, recvs, locs = self.issue(ins, outs, *sems)
        for make in recvs:
            make().wait_recv()
        for cp in sends:
            cp.wait_send()
        for cp in locs:
            cp.wait()


def _plan_parts(plan):
    if plan is None:
        return [], [], [], []
    return ([_ANY] * len(plan.ins), plan.out_shapes, [_ANY] * len(plan.out_shapes), plan.scratch)


def _run_plan(plan, name):
    n_in, n_out = len(plan.ins), len(plan.out_shapes)

    def body(*refs):
        ins, outs, sems = refs[:n_in], refs[n_in:n_in + n_out], refs[n_in + n_out:]
        plan.start(ins, outs, sems)
        plan.wait(ins, outs, sems)

    return pl.pallas_call(
        body, name=name, in_specs=[_ANY] * n_in, out_specs=[_ANY] * n_out,
        out_shape=plan.out_shapes, scratch_shapes=plan.scratch,
    )(*plan.ins)


def _gather_plan(shards):
    n = len(shards)

    def issue(ins, outs, send_sems, recv_sems, local_sems):
        x, y, c = _place()
        me = 2 * x + y
        sends, recvs, locs = [], [], []
        for p in range(n):
            locs.append(pltpu.make_async_copy(ins[p], outs[p].at[me], local_sems.at[p]))
            for k, (px, py) in enumerate(_other_chips(x, y)):
                sems = dict(send_sem=send_sems.at[3 * p + k], recv_sem=recv_sems.at[3 * p + k],
                            device_id=(px, py, c), device_id_type=MESH)
                sends.append(pltpu.make_async_remote_copy(src_ref=ins[p], dst_ref=outs[p].at[me], **sems))
                recvs.append(functools.partial(pltpu.make_async_remote_copy, src_ref=ins[p],
                                               dst_ref=outs[p].at[2 * px + py], **sems))
        return sends, recvs, locs

    return _Plan(shards, [jax.ShapeDtypeStruct((N_CHIPS,) + s.shape, s.dtype) for s in shards], 3 * n, n, issue)


_REL7 = [(fx, fy, fc) for fx in (0, 1) for fy in (0, 1) for fc in (0, 1)][1:]


def _scatter8_plan(gs):
    n = len(gs)

    def issue(ins, outs, send_sems, recv_sems, local_sems):
        x, y, c = _place()
        sends = []
        for p in range(n):
            hr = gs[p].shape[1] // 2
            for k, (fx, fy, fc) in enumerate(_REL7):
                tx, ty, tc = x ^ fx, y ^ fy, c ^ fc
                src = ins[p].at[2 * tx + ty, pl.ds(pl.multiple_of(tc * hr, SUBLANES), hr), :]
                sends.append(pltpu.make_async_remote_copy(
                    src_ref=src, dst_ref=outs[p].at[k],
                    send_sem=send_sems.at[7 * p + k], recv_sem=recv_sems.at[7 * p + k],
                    device_id=(tx, ty, tc), device_id_type=MESH))
        return sends, [functools.partial(lambda cp: cp, cp) for cp in sends], []

    shapes = [jax.ShapeDtypeStruct((7, g.shape[1] // 2, g.shape[2]), g.dtype) for g in gs]
    return _Plan(gs, shapes, 7 * n, 0, issue)


def _sibling_plan(ts):
    n = len(ts)

    def issue(ins, outs, send_sems, recv_sems, local_sems):
        x, y, c = _place()
        sends = [pltpu.make_async_remote_copy(
            src_ref=ins[p], dst_ref=outs[p], send_sem=send_sems.at[p], recv_sem=recv_sems.at[p],
            device_id=(x, y, 1 - c), device_id_type=MESH) for p in range(n)]
        return sends, [functools.partial(lambda cp: cp, cp) for cp in sends], []

    return _Plan(ts, [jax.ShapeDtypeStruct(t.shape, t.dtype) for t in ts], n, 0, issue)


def _add8(g, recv, chip, core, name):
    s, r, n = g.shape
    hr = r // 2
    th = hr // 2 if (hr // 2) % SUBLANES == 0 else hr
    nt = hr // th

    def body(chip_ref, core_ref, g_ref, r_ref, o_ref):
        acc = g_ref[0].astype(F32)
        for k in range(7):
            acc = acc + r_ref[k].astype(F32)
        o_ref[...] = acc

    return pl.pallas_call(
        body, name=name,
        grid_spec=pltpu.PrefetchScalarGridSpec(
            num_scalar_prefetch=2, grid=(nt,),
            in_specs=[pl.BlockSpec((1, th, n), lambda i, ch, co: (ch[0], co[0] * nt + i, 0)),
                      pl.BlockSpec((7, th, n), lambda i, ch, co: (0, i, 0))],
            out_specs=pl.BlockSpec((th, n), lambda i, ch, co: (i, 0))),
        out_shape=jax.ShapeDtypeStruct((hr, n), F32),
        compiler_params=pltpu.CompilerParams(dimension_semantics=("parallel",)),
    )(chip, core, g, recv)


def _adamw_halves(w, own, other, m, v, core, name):
    r, n = w.shape
    hr = r // 2
    th = hr // 2 if (hr // 2) % SUBLANES == 0 else hr
    tph = hr // th
    c1 = 1.0 / (1.0 - ADAM_B1 ** ADAM_STEP)
    c2 = 1.0 / (1.0 - ADAM_B2 ** ADAM_STEP)

    def body(core_ref, w_ref, a_ref, b_ref, m_ref, v_ref, g_ref, d_ref, nm_ref, nv_ref):
        half = pl.program_id(0) // tph
        gv = jnp.where(half == core_ref[0], a_ref[...], b_ref[...])
        nm = ADAM_B1 * m_ref[...] + (1.0 - ADAM_B1) * gv
        nv = ADAM_B2 * v_ref[...] + (1.0 - ADAM_B2) * (gv * gv)
        g_ref[...] = gv
        nm_ref[...] = nm
        nv_ref[...] = nv
        d_ref[...] = -ADAM_LR * ((nm * c1) / (jnp.sqrt(nv * c2) + ADAM_EPS) + ADAM_WD * w_ref[...])

    full = pl.BlockSpec((th, n), lambda i, co: (i, 0))
    part = pl.BlockSpec((th, n), lambda i, co: (i % tph, 0))
    return pl.pallas_call(
        body, name=name,
        grid_spec=pltpu.PrefetchScalarGridSpec(
            num_scalar_prefetch=1, grid=(2 * tph,),
            in_specs=[full, part, part, full, full], out_specs=[full] * 4),
        out_shape=[jax.ShapeDtypeStruct((r, n), F32)] * 4,
        compiler_params=pltpu.CompilerParams(dimension_semantics=("parallel",)),
    )(core, w, own, other, m, v)


def _allreduce_small(pack, name):
    r, l = pack.shape
    hr = r // 2
    assert hr % SUBLANES == 0

    def body(p_ref, o_ref, sib, chips, send_sems, recv_sems):
        x, y, c = _place()
        chip = 2 * x + y
        sibling = dict(device_id=(x, y, 1 - c), device_id_type=MESH)
        mine = pl.ds(pl.multiple_of(c * hr, SUBLANES), hr)
        other = pl.ds(pl.multiple_of((1 - c) * hr, SUBLANES), hr)
        a = pltpu.make_async_remote_copy(src_ref=p_ref.at[other], dst_ref=sib, send_sem=send_sems.at[0],
                                         recv_sem=recv_sems.at[0], **sibling)
        a.start()
        a.wait()
        own, got = p_ref[mine, :], sib[...]
        chips[chip] = jnp.where(c == 0, own, got) + jnp.where(c == 0, got, own)
        sends = []
        for k, (px, py) in enumerate(_other_chips(x, y)):
            cp = pltpu.make_async_remote_copy(
                src_ref=chips.at[chip], dst_ref=chips.at[chip], send_sem=send_sems.at[1 + k],
                recv_sem=recv_sems.at[1 + k], device_id=(px, py, c), device_id_type=MESH)
            cp.start()
            sends.append(cp)
        for k, (px, py) in enumerate(_other_chips(x, y)):
            pltpu.make_async_remote_copy(
                src_ref=chips.at[chip], dst_ref=chips.at[2 * px + py], send_sem=send_sems.at[1 + k],
                recv_sem=recv_sems.at[1 + k], device_id=(px, py, c), device_id_type=MESH).wait_recv()
        for cp in sends:
            cp.wait_send()
        o_ref[mine, :] = ((chips[0] + chips[1]) + chips[2]) + chips[3]
        fin = pltpu.make_async_remote_copy(src_ref=o_ref.at[mine], dst_ref=o_ref.at[mine],
                                           send_sem=send_sems.at[4], recv_sem=recv_sems.at[4], **sibling)
        fin.start()
        pltpu.make_async_remote_copy(src_ref=o_ref.at[mine], dst_ref=o_ref.at[other],
                                     send_sem=send_sems.at[4], recv_sem=recv_sems.at[4], **sibling).wait_recv()
        fin.wait_send()

    vm = pl.BlockSpec(memory_space=pltpu.VMEM)
    return pl.pallas_call(
        body, name=name, in_specs=[vm], out_specs=vm,
        out_shape=jax.ShapeDtypeStruct((r, l), F32),
        scratch_shapes=[pltpu.VMEM((hr, l), F32), pltpu.VMEM((N_CHIPS, hr, l), F32),
                        pltpu.SemaphoreType.DMA((5,)), pltpu.SemaphoreType.DMA((5,))],
    )(pack)


def _flat_rows(a, mult=SUBLANES * LANES):
    f = a.reshape(-1)
    padn = (-f.shape[0]) % mult
    if padn:
        f = jnp.concatenate([f, jnp.zeros((padn,), f.dtype)])
    return f


def _pack(arrs, mult=SUBLANES * LANES, total_mult=None):
    flat = [_flat_rows(a, mult) for a in arrs]
    sizes = [f.shape[0] for f in flat]
    if total_mult is not None:
        padn = (-sum(sizes)) % total_mult
        if padn:
            flat.append(jnp.zeros((padn,), flat[0].dtype))
    return jnp.concatenate(flat).reshape(-1, LANES), sizes


def _unpack(pack, shapes, sizes, lead=()):
    flat = pack.reshape(lead + (-1,))
    out, off = [], 0
    for shp, sz in zip(shapes, sizes):
        n = math.prod(shp)
        out.append(flat[..., off:off + n].reshape(lead + tuple(shp)))
        off += sz
    return out


def _cols_from_shards(g):
    s, k, n = g.shape
    return jnp.transpose(g, (1, 0, 2)).reshape(k, s * n)


def _ffn_fwd(h, u, post, w_up, cw, cb, w_down, tag, next_pre=None, plan=None, loss_tgt=None):
    (hp, act, hg, hu), pouts = _ffn_up_convact(u, w_up, cw, cb, f"{tag}_up_convact", plan)
    if loss_tgt is not None:
        o, hn, un = _mm_postnorm_loss(act, w_down, h, post, loss_tgt, f"{tag}_down_postnorm_loss")
    else:
        res = _mm_postnorm_res(act, w_down, h, post, f"{tag}_down_postnorm", next_pre)
        o, hn, un = res if next_pre is not None else (*res, None)
    return hn, un, (h, u, hp, hg, hu, act, o), pouts


def _ffn_bwd(dh, saved, pre, post, w_up, cw, w_down, tag):
    h, u, hp, hg, hu, act, o = saved
    dhg, dhu, do, dpost = _ffn_down_act_bwd(o, post, dh, w_down, hg, hu, f"{tag}_down_dx_act_bwd")
    dw_down = _mm_tn(act, do, f"{tag}_down_dw")
    dxg, dwg, dbg = _conv_bwd(hp, 0, D_FF, dhg, cw, f"{tag}_conv_bwd_gate")
    dxu, dwu, dbu = _conv_bwd(hp, D_FF, D_FF, dhu, cw, f"{tag}_conv_bwd_up", w_col_off=D_FF)
    dhp = (dxg, dxu)
    dcw = jnp.concatenate([dwg, dwu], axis=1)
    dcb = jnp.concatenate([dbg, dbu], axis=1)
    (dhn, dpre), _ = _mm_nt_sh(dhp, w_up, (h, pre, dh), f"{tag}_up_dx_prenorm_bwd")
    dw_up = _mm_tn_sh(u, dhp, w_up.shape[2], f"{tag}_up_dw")
    return dhn, dict(pre=dpre, post=dpost, w_up=dw_up, conv_w=dcw[:3], conv_b=dcb, w_down=dw_down)


class _Exchange:
    GATHER_IN_LRU = ("l0_w_out", "l0_ffn_w_down")
    GATHER_IN_ATTN = ("l0_ffn_w_up", "l1_w_out")
    GATHER_IN_FFN0 = ("l1_w_in",)
    GATHER_IN_SSD = ("l1_ffn_w_up", "l1_ffn_w_down")
    AFTER_L1_OUT = ("l1_ffn_w_up", "l1_ffn_w_down", "l1_w_out")
    IN_LRU_BWD = ("l1_w_in",)
    AFTER_L0_OUT = ("l0_ffn_w_up", "l0_ffn_w_down", "l0_w_out")
    LAST = ("l0_w_in",)

    def __init__(self, late_shards):
        self.late = dict(late_shards)
        self.slabs = {}
        self.recv = {}

    def gather_plan(self, names):
        return _gather_plan([self.late[n] for n in names])

    def gathered(self, names, outs):
        return {n: (g if n in _BIG_COL else g.reshape(-1, g.shape[-1])) for n, g in zip(names, outs)}

    def scatter_plan(self, grads, names):
        for n in names:
            g = grads[n]
            self.slabs[n] = g if n in _BIG_COL else g.reshape(N_CHIPS, -1, g.shape[-1])
        return _scatter8_plan([self.slabs[n] for n in names])

    def scattered(self, names, outs):
        self.recv.update(zip(names, outs))


def _local_step(x, tgt, meta, P, ex=None):
    seq, d = x.shape
    lp = seq + BLOCK
    h0 = jnp.concatenate([jnp.zeros((PAD, d), F32), meta, x], axis=0)
    tgt_p = jnp.concatenate([jnp.zeros((BLOCK, d), F32), tgt], axis=0)

    u0 = _rmsnorm_fwd(h0, P["l0_mix_pre_norm"], "l0_mix_prenorm")
    proj0 = _mm_nn_sh(u0, P["l0_w_in"], EVEN_IN, "l0_in")
    xrc = _conv_fwd(proj0, D_MODEL, D_MODEL, P["l0_lru_conv_w"], P["l0_lru_conv_b"], "l0_lru_conv")
    lru_args = (P["l0_lru_w_a"], P["l0_lru_w_x"], P["l0_lru_b_a"], P["l0_lru_b_x"], P["l0_lru_lambda"])
    if ex:
        (ya, hl), outs = _lru_fwd(proj0, xrc, *lru_args, "l0_lru", ex.gather_plan(ex.GATHER_IN_LRU))
        P = {**P, **ex.gathered(ex.GATHER_IN_LRU, outs)}
    else:
        ya, hl = _lru_fwd(proj0, xrc, *lru_args, "l0_lru")
    yb, outs = _attn_fwd(proj0, P["l0_attn_sinks"], "l0_attn",
                         ex.gather_plan(ex.GATHER_IN_ATTN) if ex else None)
    if ex:
        P = {**P, **ex.gathered(ex.GATHER_IN_ATTN, outs)}
    o0, h1, u1 = _mm_postnorm_res((ya, yb), P["l0_w_out"], h0, P["l0_mix_post_norm"], "l0_out_postnorm",
                                  P["l0_ffn_pre_norm"])
    h2, u2, ffn0, outs = _ffn_fwd(h1, u1, P["l0_ffn_post_norm"], P["l0_ffn_w_up"], P["l0_ffn_conv_w"],
                                  P["l0_ffn_conv_b"], P["l0_ffn_w_down"], "l0_ffn", P["l1_mix_pre_norm"],
                                  ex.gather_plan(ex.GATHER_IN_FFN0) if ex else None)
    if ex:
        P = {**P, **ex.gathered(ex.GATHER_IN_FFN0, outs)}
    proj1 = _mm_nn_sh(u2, P["l1_w_in"], ODD_IN_PAD, "l1_in")
    xc1, xbc, dt = _ssm_convprep_fwd(proj1, P["l1_ssm_conv_w"], P["l1_ssm_conv_b"], P["l1_dt_bias"],
                                     "l1_ssm_convprep")
    (yssd, states), outs = _ssd_fwd(xbc, dt, P["l1_a_log"], "l1_ssd",
                                    ex.gather_plan(ex.GATHER_IN_SSD) if ex else None)
    if ex:
        P = {**P, **ex.gathered(ex.GATHER_IN_SSD, outs)}
    yn, o1, h3, u3 = _ssm_gate_out_postnorm(
        yssd, xbc, proj1, P["l1_d_skip"], P["l1_gate_norm"], P["l1_w_out"], h2, P["l1_mix_post_norm"],
        P["l1_ffn_pre_norm"], "l1_gate_out_postnorm")
    dh4, loss_cols, ffn1, _ = _ffn_fwd(h3, u3, P["l1_ffn_post_norm"], P["l1_ffn_w_up"], P["l1_ffn_conv_w"],
                                       P["l1_ffn_conv_b"], P["l1_ffn_w_down"], "l1_ffn", loss_tgt=tgt_p)

    G = {}
    dh3, g = _ffn_bwd(dh4, ffn1, P["l1_ffn_pre_norm"], P["l1_ffn_post_norm"], P["l1_ffn_w_up"],
                      P["l1_ffn_conv_w"], P["l1_ffn_w_down"], "l1_ffn")
    for k, v in g.items():
        G["l1_ffn_" + (k + "_norm" if k in ("pre", "post") else k)] = v
    dyssd, dxskip, dz, dd_cols, G["l1_gate_norm"], do1, G["l1_mix_post_norm"] = _ssm_out_gate_bwd(
        o1, P["l1_mix_post_norm"], dh3, P["l1_w_out"], yssd, xbc, proj1, P["l1_d_skip"], P["l1_gate_norm"],
        "l1_out_dx_gate_bwd")
    G["l1_w_out"] = _mm_tn(yn, do1, "l1_out_dw")
    G["l1_d_skip"] = dd_cols.reshape(SSD_HEADS, SSD_P).sum(axis=1)
    (dxs, dbm, dcm, ddt, dalog), outs = _ssd_bwd(
        xbc, dt, P["l1_a_log"], states, dyssd, "l1_ssd_bwd",
        ex.scatter_plan(G, ex.AFTER_L1_OUT) if ex else None)
    if ex:
        ex.scattered(ex.AFTER_L1_OUT, outs)
    G["l1_a_log"] = dalog[0, :SSD_HEADS]
    dxc, ddtr, dbias = _ssm_prep_bwd(xc1, proj1, P["l1_dt_bias"], dxs, dxskip, dbm, dcm, ddt,
                                     "l1_ssm_prep_bwd")
    G["l1_dt_bias"] = dbias[0, :SSD_HEADS]
    dxbc, dcw, dcb = _conv_bwd(proj1, _ZW, _XBC_W, dxc, P["l1_ssm_conv_w"], "l1_ssm_conv_bwd")
    G["l1_ssm_conv_w"] = dcw[:4]
    G["l1_ssm_conv_b"] = dcb
    dproj1 = jnp.concatenate([dz, dxbc, ddtr], axis=1)
    (dh2, G["l1_mix_pre_norm"]), _ = _mm_nt_sh(dproj1, P["l1_w_in"], (h2, P["l1_mix_pre_norm"], dh3),
                                               "l1_in_dx_prenorm_bwd")
    G["l1_w_in"] = _mm_tn_sh(u2, dproj1, ODD_IN // N_CHIPS, "l1_in_dw")
    dh1, g = _ffn_bwd(dh2, ffn0, P["l0_ffn_pre_norm"], P["l0_ffn_post_norm"], P["l0_ffn_w_up"],
                      P["l0_ffn_conv_w"], P["l0_ffn_w_down"], "l0_ffn")
    for k, v in g.items():
        G["l0_ffn_" + (k + "_norm" if k in ("pre", "post") else k)] = v
    dmix, do0, G["l0_mix_post_norm"] = _mm_nt_postnorm_bwd(o0, P["l0_mix_post_norm"], dh1, P["l0_w_out"],
                                                           "l0_out_dx")
    G["l0_w_out"] = jnp.concatenate([_mm_tn(ya, do0, "l0_out_dw_lru"), _mm_tn(yb, do0, "l0_out_dw_attn")], axis=0)
    if ex:
        lru_out, outs = _lru_bwd(proj0, xrc, hl, dmix, *lru_args, "l0_lru_bwd",
                                 ex.scatter_plan(G, ex.IN_LRU_BWD))
        ex.scattered(ex.IN_LRU_BWD, outs)
    else:
        lru_out = _lru_bwd(proj0, xrc, hl, dmix, *lru_args, "l0_lru_bwd")
    (dgate, dxrc, G["l0_lru_w_a"], G["l0_lru_w_x"], G["l0_lru_b_a"], G["l0_lru_b_x"],
     G["l0_lru_lambda"]) = lru_out
    dxr, dcw, dcb = _conv_bwd(proj0, D_MODEL, D_MODEL, dxrc, P["l0_lru_conv_w"], "l0_lru_conv_bwd")
    G["l0_lru_conv_w"] = dcw[:4]
    G["l0_lru_conv_b"] = dcb
    (dq, dk, dv, G["l0_attn_sinks"]), outs = _attn_bwd(
        proj0, P["l0_attn_sinks"], dmix, "l0_attn_bwd",
        ex.scatter_plan(G, ex.AFTER_L0_OUT) if ex else None)
    if ex:
        ex.scattered(ex.AFTER_L0_OUT, outs)
    dproj0 = jnp.concatenate([dgate, dxr, dq, dk.astype(BF16), dv.astype(BF16)], axis=1)
    G["l0_w_in"] = _mm_tn_sh(u0, dproj0, EVEN_IN // N_CHIPS, "l0_in_dw")
    (dh0, G["l0_mix_pre_norm"]), outs = _mm_nt_sh(
        dproj0, P["l0_w_in"], (h0, P["l0_mix_pre_norm"], dh1), "l0_in_dx_prenorm_bwd",
        ex.scatter_plan(G, ex.LAST) if ex else None)
    if ex:
        ex.scattered(ex.LAST, outs)
    return loss_cols, dh0[BLOCK:], dh0[PAD:BLOCK], G


_BIG_COL = ("l0_w_in", "l0_ffn_w_up", "l1_w_in", "l1_ffn_w_up")
_BIG = ("l0_w_in", "l0_w_out", "l0_ffn_w_up", "l0_ffn_w_down",
        "l1_w_in", "l1_w_out", "l1_ffn_w_up", "l1_ffn_w_down")
_SMALL_SHARDED = ("meta_tokens", "l0_lru_conv_w", "l0_ffn_conv_w", "l1_ssm_conv_w", "l1_ffn_conv_w")
_WEIGHTS = ("meta_tokens", "l0_mix_pre_norm", "l0_mix_post_norm", "l0_w_in", "l0_lru_conv_w",
            "l0_lru_conv_b", "l0_lru_w_a", "l0_lru_b_a", "l0_lru_w_x", "l0_lru_b_x", "l0_lru_lambda",
            "l0_attn_sinks", "l0_w_out", "l0_ffn_pre_norm", "l0_ffn_post_norm", "l0_ffn_w_up",
            "l0_ffn_conv_w", "l0_ffn_conv_b", "l0_ffn_w_down", "l1_mix_pre_norm", "l1_mix_post_norm",
            "l1_w_in", "l1_ssm_conv_w", "l1_ssm_conv_b", "l1_dt_bias", "l1_a_log", "l1_d_skip",
            "l1_gate_norm", "l1_w_out", "l1_ffn_pre_norm", "l1_ffn_post_norm", "l1_ffn_w_up",
            "l1_ffn_conv_w", "l1_ffn_conv_b", "l1_ffn_w_down")
_REPL = tuple(n for n in _WEIGHTS if n not in _BIG and n not in _SMALL_SHARDED)


def _pad_lanes(v, n=LANES):
    return jnp.concatenate([v, jnp.zeros((n - v.shape[0],), v.dtype)]).reshape(1, n)


def _step(x, tgt, W, M, V):
    cx, cy, cc = _place()
    chip = 2 * cx + cy

    small_pack, small_sizes = _pack([W[n] for n in _SMALL_SHARDED])
    first = _run_plan(_gather_plan([W["l0_w_in"].astype(BF16), small_pack]), "gather_first")
    small_full = _unpack(first[1], [W[n].shape for n in _SMALL_SHARDED], small_sizes, lead=(N_CHIPS,))
    ex = _Exchange({n: W[n].astype(BF16) for n in _BIG if n != "l0_w_in"})

    P = {"l0_w_in": first[0]}
    for n, g in zip(_SMALL_SHARDED, small_full):
        P[n] = _cols_from_shards(g)
    for n in _REPL:
        v = W[n]
        P[n] = v.reshape(1, -1) if v.ndim == 1 else v
    P["l0_lru_w_a"] = W["l0_lru_w_a"].astype(BF16)
    P["l0_lru_w_x"] = W["l0_lru_w_x"].astype(BF16)
    P["l1_dt_bias"] = _pad_lanes(W["l1_dt_bias"])
    P["l1_a_log"] = _pad_lanes(W["l1_a_log"])
    P["l1_d_skip"] = jnp.repeat(W["l1_d_skip"], SSD_P).reshape(1, D_SSM)
    meta = P.pop("meta_tokens")

    loss_cols, grad_x, grad_meta, G = _local_step(x, tgt, meta, P, ex)
    G["meta_tokens"] = grad_meta

    core_idx = cc.astype(jnp.int32).reshape(1)
    chip_idx = chip.astype(jnp.int32).reshape(1)
    own_half = [_add8(ex.slabs[n], ex.recv[n], chip_idx, core_idx, f"grad_sum_{n}") for n in _BIG]
    other_half = _run_plan(_sibling_plan(own_half), "grad_sibling_swap")
    small_names = list(_REPL) + list(_SMALL_SHARDED)
    small_list = [G[n] for n in small_names] + [loss_cols]
    spack, ssizes = _pack(small_list, total_mult=2 * SUBLANES * LANES)
    sred = _allreduce_small(spack, "small_allreduce")
    sfull = _unpack(sred, [a.shape for a in small_list], ssizes)
    loss = 0.5 / D_MODEL * jnp.sum(sfull[-1])
    small_grads = {}
    for n, g in zip(small_names, sfull[:-1]):
        if n in _SMALL_SHARDED:
            wcols = W[n].shape[1]
            g = lax.dynamic_slice_in_dim(g, chip * wcols, wcols, axis=1)
        small_grads[n] = g.reshape(W[n].shape)

    grads, delta, new_m, new_v = {}, {}, {}, {}
    for n, own, other in zip(_BIG, own_half, other_half):
        grads[n], delta[n], new_m[n], new_v[n] = _adamw_halves(
            W[n], own, other, M[n], V[n], core_idx, f"adamw_{n}")
    s_names = [n for n in _WEIGHTS if n not in _BIG]
    as2d = lambda a: a.reshape(_shape2d(a.shape))
    outs = _adamw_many([as2d(W[n]) for n in s_names], [as2d(small_grads[n]) for n in s_names],
                       [as2d(M[n]) for n in s_names], [as2d(V[n]) for n in s_names], "adamw_small")
    k = len(s_names)
    for i, n in enumerate(s_names):
        grads[n] = small_grads[n]
        delta[n], new_m[n], new_v[n] = (outs[j * k + i].reshape(W[n].shape) for j in range(3))
    return loss, grad_x, grads, delta, new_m, new_v


def kernel(x, meta_tokens, l0_mix_pre_norm, l0_mix_post_norm, l0_w_in, l0_lru_conv_w, l0_lru_conv_b, l0_lru_w_a, l0_lru_b_a, l0_lru_w_x, l0_lru_b_x, l0_lru_lambda, l0_attn_sinks, l0_w_out, l0_ffn_pre_norm, l0_ffn_post_norm, l0_ffn_w_up, l0_ffn_conv_w, l0_ffn_conv_b, l0_ffn_w_down, l1_mix_pre_norm, l1_mix_post_norm, l1_w_in, l1_ssm_conv_w, l1_ssm_conv_b, l1_dt_bias, l1_a_log, l1_d_skip, l1_gate_norm, l1_w_out, l1_ffn_pre_norm, l1_ffn_post_norm, l1_ffn_w_up, l1_ffn_conv_w, l1_ffn_conv_b, l1_ffn_w_down, loss_target, m_meta_tokens, m_l0_mix_pre_norm, m_l0_mix_post_norm, m_l0_w_in, m_l0_lru_conv_w, m_l0_lru_conv_b, m_l0_lru_w_a, m_l0_lru_b_a, m_l0_lru_w_x, m_l0_lru_b_x, m_l0_lru_lambda, m_l0_attn_sinks, m_l0_w_out, m_l0_ffn_pre_norm, m_l0_ffn_post_norm, m_l0_ffn_w_up, m_l0_ffn_conv_w, m_l0_ffn_conv_b, m_l0_ffn_w_down, m_l1_mix_pre_norm, m_l1_mix_post_norm, m_l1_w_in, m_l1_ssm_conv_w, m_l1_ssm_conv_b, m_l1_dt_bias, m_l1_a_log, m_l1_d_skip, m_l1_gate_norm, m_l1_w_out, m_l1_ffn_pre_norm, m_l1_ffn_post_norm, m_l1_ffn_w_up, m_l1_ffn_conv_w, m_l1_ffn_conv_b, m_l1_ffn_w_down, v_meta_tokens, v_l0_mix_pre_norm, v_l0_mix_post_norm, v_l0_w_in, v_l0_lru_conv_w, v_l0_lru_conv_b, v_l0_lru_w_a, v_l0_lru_b_a, v_l0_lru_w_x, v_l0_lru_b_x, v_l0_lru_lambda, v_l0_attn_sinks, v_l0_w_out, v_l0_ffn_pre_norm, v_l0_ffn_post_norm, v_l0_ffn_w_up, v_l0_ffn_conv_w, v_l0_ffn_conv_b, v_l0_ffn_w_down, v_l1_mix_pre_norm, v_l1_mix_post_norm, v_l1_w_in, v_l1_ssm_conv_w, v_l1_ssm_conv_b, v_l1_dt_bias, v_l1_a_log, v_l1_d_skip, v_l1_gate_norm, v_l1_w_out, v_l1_ffn_pre_norm, v_l1_ffn_post_norm, v_l1_ffn_w_up, v_l1_ffn_conv_w, v_l1_ffn_conv_b, v_l1_ffn_w_down):
    args = locals()
    W = {n: args[n] for n in _WEIGHTS}
    M = {n: args["m_" + n] for n in _WEIGHTS}
    V = {n: args["v_" + n] for n in _WEIGHTS}
    loss, grad_x, grads, delta, new_m, new_v = _step(x[0], loss_target[0], W, M, V)
    return (loss, grad_x[None], *[grads[n] for n in _WEIGHTS], *[delta[n] for n in _WEIGHTS],
            *[new_m[n] for n in _WEIGHTS], *[new_v[n] for n in _WEIGHTS])
```

```python
import functools
import math

import jax
import jax.numpy as jnp
from jax import lax
from jax.experimental import pallas as pl
from jax.experimental.pallas import tpu as pltpu

F32 = jnp.float32
BF16 = jnp.bfloat16

D_MODEL = 1024
N_META = 16
BLOCK = 128
PAD = BLOCK - N_META
EPS = 1e-6
LRU_BLOCKS = 8
LRU_BS = 128
LRU_C = 8.0
N_Q_HEADS = 16
N_KV_HEADS = 2
HEAD_DIM = 64
Q_PER_KV = 8
WINDOW = 128
D_SSM = 2048
SSD_HEADS = 32
SSD_GROUPS = 8
SSD_HPG = 4
SSD_P = 64
SSD_N = 128
D_FF = 2816
NEG = -1e30
LANES = 128
SUBLANES = 8
_VMEM_LIMIT_WIDE = 62 * 1024 * 1024

ADAM_LR = 0.001
ADAM_B1 = 0.9
ADAM_B2 = 0.999
ADAM_EPS = 1e-08
ADAM_WD = 0.01
ADAM_STEP = 10

MESH = pl.DeviceIdType.MESH
N_CHIPS = 4


def _pick(n, cands):
    for c in cands:
        if n % c == 0:
            return c
    raise ValueError(f"no tile for {n} in {cands}")


def _col_tile(n, limit=1792):
    best = None
    for t in range(LANES, min(n, limit) + 1, LANES):
        if n % t == 0:
            best = t
    if best is None:
        raise ValueError(f"no lane tile for {n}")
    return best


def _sigmoid(x):
    return 0.5 + 0.5 * jnp.tanh(0.5 * x)


def _log1p(e):
    u = 1.0 + e
    return jnp.where(u == 1.0, e, jnp.log(u) * (e / jnp.where(u == 1.0, 1.0, u - 1.0)))


def _softplus(x):
    return jnp.maximum(x, 0.0) + _log1p(jnp.exp(-jnp.abs(x)))


def _neg_expm1(x):
    poly = x * (1.0 + x * (0.5 + x * (1.0 / 6.0 + x * (1.0 / 24.0 + x * (1.0 / 120.0)))))
    return -jnp.where(x > -0.05, poly, jnp.exp(x) - 1.0)


_GELU_C = math.sqrt(2.0 / math.pi)


def _gelu(x):
    u = 0.5 + 0.5 * jnp.tanh(x * (_GELU_C + (_GELU_C * 0.044715) * (x * x)))
    return x * u


def _gelu_and_grad(x):
    x2 = x * x
    u = 0.5 + 0.5 * jnp.tanh(x * (_GELU_C + (_GELU_C * 0.044715) * x2))
    g = x * u
    dg = u * (1.0 + (x - g) * (2.0 * _GELU_C + (6.0 * 0.044715 * _GELU_C) * x2))
    return g, dg


def _silu_and_grad(x):
    s = _sigmoid(x)
    return x * s, s * (1.0 + x * (1.0 - s))


def _dot(a, b):
    return jnp.dot(a, b, preferred_element_type=F32)


def _dot_nt(a, b):
    return lax.dot_general(a, b, (((1,), (1,)), ((), ())), preferred_element_type=F32)


def _dot_tn(a, b):
    return lax.dot_general(a, b, (((0,), (0,)), ((), ())), preferred_element_type=F32)


def _row_iota(t):
    return lax.broadcasted_iota(jnp.int32, (t, 1), 0)


def _scan_fwd(a, u, t):
    row = _row_iota(t)
    d = 1
    while d < t:
        m = row >= d
        u_sh = jnp.where(m, pltpu.roll(u, d, 0), 0.0)
        a_sh = jnp.where(m, pltpu.roll(a, d, 0), 1.0)
        u = u + a * u_sh
        a = a * a_sh
        d *= 2
    return a, u


def _scan_rev(c, x, t):
    row = _row_iota(t)
    d = 1
    while d < t:
        m = row < t - d
        x_sh = jnp.where(m, pltpu.roll(x, t - d, 0), 0.0)
        c_sh = jnp.where(m, pltpu.roll(c, t - d, 0), 1.0)
        x = x + c * x_sh
        c = c * c_sh
        d *= 2
    return c, x


def _cumsum_rows(x, t):
    row = _row_iota(t)
    d = 1
    while d < t:
        x = x + jnp.where(row >= d, pltpu.roll(x, d, 0), 0.0)
        d *= 2
    return x


def _rev_cumsum_rows(x, t):
    row = _row_iota(t)
    d = 1
    while d < t:
        x = x + jnp.where(row < t - d, pltpu.roll(x, t - d, 0), 0.0)
        d *= 2
    return x


def _rms_bwd(x, g, dy):
    rs = lax.rsqrt(jnp.mean(x * x, axis=-1, keepdims=True) + EPS)
    gy = dy * g
    dx = rs * gy - x * (rs * rs * rs) * jnp.mean(x * gy, axis=-1, keepdims=True)
    return dx, dy * x * rs


def _mm_nt_postnorm_bwd(o, gain, dh, w, name):
    m, d = o.shape
    k = w.shape[0]
    tm = _pick(m, (640, 512, 256, 128))

    def body(o_ref, gain_ref, dh_ref, w_ref, y_ref, do_ref, dgain_ref):
        y_ref[...] = _dot_nt(_postnorm_bwd_tile(o_ref, gain_ref, dh_ref, do_ref, dgain_ref), w_ref[...])

    row = pl.BlockSpec((tm, d), lambda i: (i, 0))
    vec = pl.BlockSpec((1, d), lambda i: (0, 0))
    return pl.pallas_call(
        body, name=name, grid=(m // tm,),
        in_specs=[row, vec, row, pl.BlockSpec((k, d), lambda i: (0, 0))],
        out_specs=[pl.BlockSpec((tm, k), lambda i: (i, 0)), row, vec],
        out_shape=[jax.ShapeDtypeStruct((m, k), F32), jax.ShapeDtypeStruct((m, d), BF16),
                   jax.ShapeDtypeStruct((1, d), F32)],
        compiler_params=pltpu.CompilerParams(dimension_semantics=("arbitrary",)),
    )(o, gain, dh, w)


def _mm_tn(a, dy, name):
    m, k = a.shape
    n = dy.shape[1]
    tm = _pick(m, (640, 512, 256, 128))
    tk = _col_tile(k, 1408)
    tn = _col_tile(n, 1664)
    nsteps = m // tm

    def body(a_ref, dy_ref, o_ref, acc):
        @pl.when(pl.program_id(2) == 0)
        def _():
            acc[...] = jnp.zeros_like(acc)

        acc[...] += _dot_tn(a_ref[...].astype(BF16), dy_ref[...].astype(BF16))

        @pl.when(pl.program_id(2) == nsteps - 1)
        def _():
            o_ref[...] = acc[...].astype(o_ref.dtype)

    return pl.pallas_call(
        body, name=name, grid=(k // tk, n // tn, nsteps),
        in_specs=[pl.BlockSpec((tm, tk), lambda kk, j, i: (i, kk)),
                  pl.BlockSpec((tm, tn), lambda kk, j, i: (i, j))],
        out_specs=pl.BlockSpec((tk, tn), lambda kk, j, i: (kk, j)),
        out_shape=jax.ShapeDtypeStruct((k, n), BF16),
        scratch_shapes=[pltpu.VMEM((tk, tn), F32)],
        compiler_params=pltpu.CompilerParams(
            dimension_semantics=("parallel", "parallel", "arbitrary")),
    )(a, dy)


def _mm_nn_sh(a, w4, n_out, name):
    m, k = a.shape
    s, _, n = w4.shape
    tm = _pick(m, (320, 256, 128))

    def body(a_ref, w_ref, o_ref):
        av = a_ref[...].astype(BF16)
        for j in range(s):
            o_ref[:, j * n:(j + 1) * n] = _dot(av, w_ref[j])
        if n_out > s * n:
            o_ref[:, s * n:] = jnp.zeros((tm, n_out - s * n), F32)

    return pl.pallas_call(
        body, name=name, grid=(m // tm,),
        in_specs=[pl.BlockSpec((tm, k), lambda i: (i, 0)),
                  pl.BlockSpec((s, k, n), lambda i: (0, 0, 0))],
        out_specs=pl.BlockSpec((tm, n_out), lambda i: (i, 0)),
        out_shape=jax.ShapeDtypeStruct((m, n_out), F32),
        compiler_params=pltpu.CompilerParams(dimension_semantics=("parallel",)),
    )(a, w4)


def _mm_nt_sh(dy, w4, norm, name, plan=None):
    dys = dy if isinstance(dy, (tuple, list)) else (dy,)
    h, g, dres = norm
    m = dys[0].shape[0]
    s, k, n = w4.shape
    tm = _pick(m, (640, 512, 256, 128))
    where = _shard_columns(dys, s, n)
    p_in, p_shapes, p_out, p_scr = _plan_parts(plan)
    nd, ni = len(dys), m // tm

    def body(*refs):
        w_ref, h_ref, g_ref, dres_ref = refs[nd:nd + 4]
        cins = refs[nd + 4:nd + 4 + len(p_in)]
        dh_ref, dg_ref = refs[nd + 4 + len(p_in):nd + 6 + len(p_in)]
        couts = refs[nd + 6 + len(p_in):nd + 6 + len(p_in) + len(p_out)]
        sems = refs[nd + 6 + len(p_in) + len(p_out):]
        i = pl.program_id(0)
        if plan is not None:
            @pl.when(i == 0)
            def _():
                plan.start(cins, couts, sems)

        du = None
        for j, (p, c0) in enumerate(where):
            t = _dot_nt(refs[p][:, c0:c0 + n].astype(BF16), w_ref[j])
            du = t if du is None else du + t
        dx, dgt = _rms_bwd(h_ref[...], g_ref[...], du)
        dh_ref[...] = jnp.where(_real_rows(i, tm), dres_ref[...] + dx, 0.0)
        _acc_add(i == 0, dg_ref, jnp.sum(dgt, axis=0, keepdims=True))
        if plan is not None:
            @pl.when(i == ni - 1)
            def _():
                plan.wait(cins, couts, sems)

    row = pl.BlockSpec((tm, k), lambda i: (i, 0))
    vec = pl.BlockSpec((1, k), lambda i: (0, 0))
    res = pl.pallas_call(
        body, name=name, grid=(ni,),
        in_specs=[pl.BlockSpec((tm, d.shape[1]), lambda i: (i, 0)) for d in dys]
        + [pl.BlockSpec((s, k, n), lambda i: (0, 0, 0)), row, vec, row] + p_in,
        out_specs=[row, vec] + p_out,
        out_shape=[jax.ShapeDtypeStruct((m, k), F32), jax.ShapeDtypeStruct((1, k), F32)] + p_shapes,
        scratch_shapes=p_scr,
        compiler_params=pltpu.CompilerParams(dimension_semantics=("arbitrary",),
                                             vmem_limit_bytes=_VMEM_LIMIT_WIDE),
    )(*dys, w4, h, g, dres, *(plan.ins if plan is not None else []))
    return (res[0], res[1]), res[2:]


def _shard_columns(dys, s, n):
    where = []
    for p, d in enumerate(dys):
        where += [(p, c * n) for c in range(d.shape[1] // n)]
    assert len(where) >= s
    return where[:s]


def _mm_tn_sh(a, dy, n, name):
    dys = dy if isinstance(dy, (tuple, list)) else (dy,)
    m, k = a.shape
    s = N_CHIPS
    tm = _pick(m, (640, 512, 256, 128))
    tk = _col_tile(k, 512)
    nsteps = m // tm
    where = _shard_columns(dys, s, n)

    def body(*refs):
        a_ref, o_ref, acc = refs[0], refs[len(dys) + 1], refs[len(dys) + 2]

        @pl.when(pl.program_id(1) == 0)
        def _():
            acc[...] = jnp.zeros_like(acc)

        av = a_ref[...].astype(BF16)
        for j, (p, c0) in enumerate(where):
            acc[j] += _dot_tn(av, refs[1 + p][:, c0:c0 + n].astype(BF16))

        @pl.when(pl.program_id(1) == nsteps - 1)
        def _():
            o_ref[...] = acc[...].astype(o_ref.dtype)

    return pl.pallas_call(
        body, name=name, grid=(k // tk, nsteps),
        in_specs=[pl.BlockSpec((tm, tk), lambda kk, i: (i, kk))]
        + [pl.BlockSpec((tm, d.shape[1]), lambda kk, i: (i, 0)) for d in dys],
        out_specs=pl.BlockSpec((s, tk, n), lambda kk, i: (0, kk, 0)),
        out_shape=jax.ShapeDtypeStruct((s, k, n), BF16),
        scratch_shapes=[pltpu.VMEM((s, tk, n), F32)],
        compiler_params=pltpu.CompilerParams(dimension_semantics=("parallel", "arbitrary"),
                                             vmem_limit_bytes=_VMEM_LIMIT_WIDE),
    )(a, *dys)


def _rowcall(name, body, lp, tm, rows=(), prevs=(), vecs=(), outs=(), accs=(), scratch=(),
             reverse=False, seq=False, plan=None):
    p_in, p_shapes, p_out, p_scr = _plan_parts(plan)
    nt = lp // tm
    hb = tm // SUBLANES

    def ri(i):
        return nt - 1 - i if reverse else i

    in_specs, args = [], []
    for arr, w, cb in rows:
        in_specs.append(pl.BlockSpec((tm, w), lambda i, cb=cb: (ri(i), cb)))
        args.append(arr)
    for arr, w, cb in prevs:
        in_specs.append(pl.BlockSpec((SUBLANES, w), lambda i, cb=cb: (jnp.maximum(ri(i) * hb - 1, 0), cb)))
        args.append(arr)
    for arr in vecs:
        in_specs.append(pl.BlockSpec(arr.shape, lambda i, nd=arr.ndim: (0,) * nd))
        args.append(arr)
    out_shape, out_specs = [], []
    for w, dt in outs:
        out_shape.append(jax.ShapeDtypeStruct((lp, w), dt))
        out_specs.append(pl.BlockSpec((tm, w), lambda i: (ri(i), 0)))
    for shp, dt in accs:
        out_shape.append(jax.ShapeDtypeStruct(shp, dt))
        out_specs.append(pl.BlockSpec(shp, lambda i, nd=len(shp): (0,) * nd))

    n_in, n_out, n_scr = len(args), len(out_shape), len(scratch)

    def kern(*refs):
        i = pl.program_id(0)
        own = (refs[:n_in] + refs[n_in + len(p_in):n_in + len(p_in) + n_out]
               + refs[n_in + len(p_in) + n_out + len(p_out):n_in + len(p_in) + n_out + len(p_out) + n_scr])
        cins = refs[n_in:n_in + len(p_in)]
        couts = refs[n_in + len(p_in) + n_out:n_in + len(p_in) + n_out + len(p_out)]
        sems = refs[n_in + len(p_in) + n_out + len(p_out) + n_scr:]
        if plan is not None:
            @pl.when(i == 0)
            def _():
                plan.start(cins, couts, sems)

        body(ri(i), i == 0, *own)
        if plan is not None:
            @pl.when(i == nt - 1)
            def _():
                plan.wait(cins, couts, sems)

    sem = ("arbitrary",) if (seq or accs or plan is not None) else ("parallel",)
    res = pl.pallas_call(
        kern, name=name, grid=(nt,), in_specs=in_specs + p_in, out_specs=out_specs + p_out,
        out_shape=out_shape + p_shapes, scratch_shapes=list(scratch) + p_scr,
        compiler_params=pltpu.CompilerParams(dimension_semantics=sem),
    )(*args, *(plan.ins if plan is not None else []))
    return res if plan is None else (res[:n_out], res[n_out:])


def _acc_add(first, ref, val):
    @pl.when(first)
    def _():
        ref[...] = jnp.zeros_like(ref)

    ref[...] += val


def _real_rows(r, tm):
    return (r * tm + _row_iota(tm)) >= PAD


def _rmsnorm_fwd(h, g, name):
    lp, d = h.shape
    tm = _pick(lp, (640, 512, 256, 128))

    def body(r, first, h_ref, g_ref, u_ref):
        x = h_ref[...]
        rs = lax.rsqrt(jnp.mean(x * x, axis=-1, keepdims=True) + EPS)
        u_ref[...] = (x * rs * g_ref[...]).astype(u_ref.dtype)

    return _rowcall(name, body, lp, tm, rows=[(h, d, 0)], vecs=[g], outs=[(d, BF16)])[0]


def _mm_postnorm_res(a, w, h, g, name, next_pre=None):
    parts = a if isinstance(a, (tuple, list)) else (a,)
    lp, d = h.shape
    k = w.shape[0]
    tm = _pick(lp, (640, 512, 256, 128))
    offs = [sum(p.shape[1] for p in parts[:i]) for i in range(len(parts))]
    np_ = len(parts)

    def body(*refs):
        w_ref, h_ref, g_ref = refs[np_], refs[np_ + 1], refs[np_ + 2]
        rest = refs[np_ + 3:]
        acc = None
        for a_ref, p, off in zip(refs, parts, offs):
            t = _dot(a_ref[...].astype(BF16), w_ref[off:off + p.shape[1], :])
            acc = t if acc is None else acc + t
        outs = rest[1:] if next_pre is not None else rest
        outs[0][...] = acc
        rs = lax.rsqrt(jnp.mean(acc * acc, axis=-1, keepdims=True) + EPS)
        hn = jnp.where(_real_rows(pl.program_id(0), tm), h_ref[...] + acc * rs * g_ref[...], 0.0)
        outs[1][...] = hn
        if next_pre is not None:
            rs2 = lax.rsqrt(jnp.mean(hn * hn, axis=-1, keepdims=True) + EPS)
            outs[2][...] = (hn * rs2 * rest[0][...]).astype(BF16)

    row = pl.BlockSpec((tm, d), lambda i: (i, 0))
    vec = pl.BlockSpec((1, d), lambda i: (0, 0))
    n_vec = 2 if next_pre is not None else 1
    return pl.pallas_call(
        body, name=name, grid=(lp // tm,),
        in_specs=[pl.BlockSpec((tm, p.shape[1]), lambda i: (i, 0)) for p in parts]
        + [pl.BlockSpec((k, d), lambda i: (0, 0)), row] + [vec] * n_vec,
        out_specs=[row] * (2 + (next_pre is not None)),
        out_shape=[jax.ShapeDtypeStruct((lp, d), F32)] * 2
        + ([jax.ShapeDtypeStruct((lp, d), BF16)] if next_pre is not None else []),
        compiler_params=pltpu.CompilerParams(dimension_semantics=("parallel",)),
    )(*parts, w, h, g, *([next_pre] if next_pre is not None else []))


def _mm_postnorm_loss(a, w, h, g, tgt, name):
    lp, d = h.shape
    k = w.shape[0]
    tm = _pick(lp, (640, 512, 256, 128))

    def body(a_ref, w_ref, h_ref, g_ref, t_ref, o_ref, dh_ref, ls_ref):
        i = pl.program_id(0)
        acc = _dot(a_ref[...].astype(BF16), w_ref[...])
        o_ref[...] = acc
        rs = lax.rsqrt(jnp.mean(acc * acc, axis=-1, keepdims=True) + EPS)
        tok = (i * tm + _row_iota(tm)) >= BLOCK
        e = jnp.where(tok, h_ref[...] + acc * rs * g_ref[...] - t_ref[...], 0.0)
        dh_ref[...] = e * (1.0 / d)
        _acc_add(i == 0, ls_ref, jnp.sum(e * e, axis=0, keepdims=True))

    row = pl.BlockSpec((tm, d), lambda i: (i, 0))
    vec = pl.BlockSpec((1, d), lambda i: (0, 0))
    return pl.pallas_call(
        body, name=name, grid=(lp // tm,),
        in_specs=[pl.BlockSpec((tm, k), lambda i: (i, 0)), pl.BlockSpec((k, d), lambda i: (0, 0)),
                  row, vec, row],
        out_specs=[row, row, vec],
        out_shape=[jax.ShapeDtypeStruct((lp, d), F32), jax.ShapeDtypeStruct((lp, d), F32),
                   jax.ShapeDtypeStruct((1, d), F32)],
        compiler_params=pltpu.CompilerParams(dimension_semantics=("arbitrary",)),
    )(a, w, h, g, tgt)


def _conv_tiles(lp, width):
    wc = _col_tile(width, 1408)
    tm = _pick(lp, (320, 256, 128))
    return tm, wc


def _conv_fwd(x, col_off, width, w, b, name):
    lp = x.shape[0]
    kk = w.shape[0]
    tm, wc = _conv_tiles(lp, width)
    offb = col_off // wc
    assert col_off % wc == 0
    hb = tm // SUBLANES

    def body(x_ref, xp_ref, w_ref, b_ref, y_ref):
        i = pl.program_id(1)
        xv = x_ref[...]
        halo = jnp.where(i > 0, xp_ref[...], 0.0)
        xx = jnp.concatenate([halo, xv], axis=0)
        acc = b_ref[...] + w_ref[kk - 1:kk, :] * xv
        for j in range(1, kk):
            acc = acc + w_ref[kk - 1 - j:kk - j, :] * pltpu.roll(xx, j, 0)[SUBLANES:, :]
        y_ref[...] = acc

    return pl.pallas_call(
        body, name=name, grid=(width // wc, lp // tm),
        in_specs=[pl.BlockSpec((tm, wc), lambda j, i: (i, offb + j)),
                  pl.BlockSpec((SUBLANES, wc), lambda j, i: (jnp.maximum(i * hb - 1, 0), offb + j)),
                  pl.BlockSpec((kk, wc), lambda j, i: (0, j)),
                  pl.BlockSpec((1, wc), lambda j, i: (0, j))],
        out_specs=pl.BlockSpec((tm, wc), lambda j, i: (i, j)),
        out_shape=jax.ShapeDtypeStruct((lp, width), F32),
        compiler_params=pltpu.CompilerParams(dimension_semantics=("parallel", "parallel")),
    )(x, x, w, b)


def _conv_bwd(x, col_off, width, dy, w, name, w_col_off=0):
    lp = x.shape[0]
    kk = w.shape[0]
    tm, wc = _conv_tiles(lp, width)
    offb = col_off // wc
    woffb = w_col_off // wc
    assert col_off % wc == 0 and w_col_off % wc == 0
    hrows = SUBLANES * (4 // dy.dtype.itemsize)
    ext = tm + hrows

    def body(x_ref, dy_ref, dn_ref, w_ref, dx_ref, dw_ref, db_ref):
        i = pl.program_id(1)
        last = pl.num_programs(1) - 1
        xv = x_ref[...]
        dyv = dy_ref[...].astype(F32)
        dd = jnp.concatenate([dyv, jnp.where(i < last, dn_ref[...].astype(F32), 0.0)], axis=0)
        dx = w_ref[kk - 1:kk, :] * dyv
        rows = [jnp.sum(dyv * xv, axis=0, keepdims=True)]
        for m in range(1, kk):
            ahead = pltpu.roll(dd, ext - m, 0)[:tm, :]
            dx = dx + w_ref[kk - 1 - m:kk - m, :] * ahead
            rows.append(jnp.sum(ahead * xv, axis=0, keepdims=True))
        dx_ref[...] = dx.astype(dx_ref.dtype)
        dwp = jnp.concatenate(rows[::-1] + [jnp.zeros((SUBLANES - kk, wc), F32)], axis=0)

        @pl.when(i == 0)
        def _():
            dw_ref[...] = jnp.zeros_like(dw_ref)
            db_ref[...] = jnp.zeros_like(db_ref)

        dw_ref[...] += dwp
        db_ref[...] += jnp.sum(dyv, axis=0, keepdims=True)

    return pl.pallas_call(
        body, name=name, grid=(width // wc, lp // tm),
        in_specs=[pl.BlockSpec((tm, wc), lambda j, i: (i, offb + j)),
                  pl.BlockSpec((tm, wc), lambda j, i: (i, j)),
                  pl.BlockSpec((hrows, wc), lambda j, i: (jnp.minimum((i + 1) * (tm // hrows), lp // hrows - 1), j)),
                  pl.BlockSpec((kk, wc), lambda j, i: (0, woffb + j))],
        out_specs=[pl.BlockSpec((tm, wc), lambda j, i: (i, j)),
                   pl.BlockSpec((SUBLANES, wc), lambda j, i: (0, j)),
                   pl.BlockSpec((1, wc), lambda j, i: (0, j))],
        out_shape=[jax.ShapeDtypeStruct((lp, width), BF16),
                   jax.ShapeDtypeStruct((SUBLANES, width), F32),
                   jax.ShapeDtypeStruct((1, width), F32)],
        compiler_params=pltpu.CompilerParams(dimension_semantics=("parallel", "arbitrary")),
    )(x, dy, dy, w)


_FFN_WC = 1408


def _ffn_up_convact(u, w4, cw, cb, name, plan=None):
    lp, k = u.shape
    s, _, n = w4.shape
    tm = _pick(lp, (256, 128))
    ni = lp // tm
    p_in, p_shapes, p_out, p_scr = _plan_parts(plan)

    def body(*refs):
        u_ref, w_ref, cw_ref, cb_ref = refs[:4]
        cins = refs[4:4 + len(p_in)]
        hp_ref, a_ref, hg_ref, hu_ref = refs[4 + len(p_in):8 + len(p_in)]
        couts = refs[8 + len(p_in):8 + len(p_in) + len(p_out)]
        carry = refs[8 + len(p_in) + len(p_out)]
        sems = refs[9 + len(p_in) + len(p_out):]
        i = pl.program_id(0)

        @pl.when(i == 0)
        def _():
            carry[...] = jnp.zeros_like(carry)
            if plan is not None:
                plan.start(cins, couts, sems)

        uv = u_ref[...].astype(BF16)
        for j in range(s):
            hp_ref[:, j * n:(j + 1) * n] = _dot(uv, w_ref[j])

        def conv(cols):
            x_ext = jnp.concatenate([carry[:, cols], hp_ref[:, cols]], axis=0)
            w = cw_ref[:, cols]
            y = (cb_ref[:, cols] + w[2:3, :] * x_ext + w[1:2, :] * pltpu.roll(x_ext, 1, 0)
                 + w[0:1, :] * pltpu.roll(x_ext, 2, 0))
            return y[SUBLANES:, :]

        for c in range(D_FF // n):
            gs = slice(c * n, (c + 1) * n)
            hg = conv(gs)
            hu = conv(slice(D_FF + c * n, D_FF + (c + 1) * n))
            a_ref[:, gs] = (_gelu(hg) * hu).astype(a_ref.dtype)
            hg_ref[:, gs] = hg.astype(hg_ref.dtype)
            hu_ref[:, gs] = hu.astype(hu_ref.dtype)
        carry[...] = hp_ref[tm - SUBLANES:tm, :]
        if plan is not None:
            @pl.when(i == ni - 1)
            def _():
                plan.wait(cins, couts, sems)

    half = pl.BlockSpec((tm, D_FF), lambda i: (i, 0))
    res = pl.pallas_call(
        body, name=name, grid=(ni,),
        in_specs=[pl.BlockSpec((tm, k), lambda i: (i, 0)), pl.BlockSpec((s, k, n), lambda i: (0, 0, 0)),
                  pl.BlockSpec(cw.shape, lambda i: (0, 0)), pl.BlockSpec(cb.shape, lambda i: (0, 0))] + p_in,
        out_specs=[pl.BlockSpec((tm, 2 * D_FF), lambda i: (i, 0)), half, half, half] + p_out,
        out_shape=[jax.ShapeDtypeStruct((lp, 2 * D_FF), F32)] + [jax.ShapeDtypeStruct((lp, D_FF), BF16)] * 3
        + p_shapes,
        scratch_shapes=[pltpu.VMEM((SUBLANES, 2 * D_FF), F32)] + p_scr,
        compiler_params=pltpu.CompilerParams(dimension_semantics=("arbitrary",),
                                             vmem_limit_bytes=_VMEM_LIMIT_WIDE),
    )(u, w4, cw, cb, *(plan.ins if plan is not None else []))
    return res[:4], res[4:]


def _postnorm_bwd_tile(o_ref, gain_ref, dh_ref, do_ref, dgain_ref):
    first = pl.program_id(0) == 0
    dx, dgt = _rms_bwd(o_ref[...], gain_ref[...], dh_ref[...])
    dob = dx.astype(BF16)
    do_ref[...] = dob
    _acc_add(first, dgain_ref, jnp.sum(dgt, axis=0, keepdims=True))
    return dob


def _ffn_down_act_bwd(o, gain, dh, w_down, hg, hu, name):
    lp, d = o.shape
    tm = _pick(lp, (320, 256, 128))
    tk = _FFN_WC

    def body(o_ref, gain_ref, dh_ref, w_ref, g_ref, u_ref, dg_ref, du_ref, do_ref, dgain_ref):
        dob = _postnorm_bwd_tile(o_ref, gain_ref, dh_ref, do_ref, dgain_ref)
        for j in range(D_FF // tk):
            cs = slice(j * tk, (j + 1) * tk)
            da = _dot_nt(dob, w_ref[cs, :])
            gl, dgl = _gelu_and_grad(g_ref[:, cs].astype(F32))
            dg_ref[:, cs] = (da * u_ref[:, cs].astype(F32) * dgl).astype(dg_ref.dtype)
            du_ref[:, cs] = (da * gl).astype(du_ref.dtype)

    wide = pl.BlockSpec((tm, D_FF), lambda i: (i, 0))
    row = pl.BlockSpec((tm, d), lambda i: (i, 0))
    vec = pl.BlockSpec((1, d), lambda i: (0, 0))
    return pl.pallas_call(
        body, name=name, grid=(lp // tm,),
        in_specs=[row, vec, row, pl.BlockSpec((D_FF, d), lambda i: (0, 0)), wide, wide],
        out_specs=[wide, wide, row, vec],
        out_shape=[jax.ShapeDtypeStruct((lp, D_FF), BF16)] * 2
        + [jax.ShapeDtypeStruct((lp, d), BF16), jax.ShapeDtypeStruct((1, d), F32)],
        compiler_params=pltpu.CompilerParams(dimension_semantics=("arbitrary",),
                                             vmem_limit_bytes=_VMEM_LIMIT_WIDE),
    )(o, gain, dh, w_down, hg, hu)


def _lru_gates(x, wa_ref, wx_ref, ba, bx, lam):
    xb = x.astype(BF16)
    za, zx = [], []
    for n in range(LRU_BLOCKS):
        xs = xb[:, n * LRU_BS:(n + 1) * LRU_BS]
        za.append(_dot(xs, wa_ref[n]))
        zx.append(_dot(xs, wx_ref[n]))
    r = _sigmoid(jnp.concatenate(za, axis=1) + ba)
    ig = _sigmoid(jnp.concatenate(zx, axis=1) + bx)
    sp = _softplus(-lam)
    log_a = -LRU_C * r * sp
    a = jnp.exp(log_a)
    om = _neg_expm1(2.0 * log_a)
    mult = jnp.sqrt(om)
    return xb, r, ig, sp, a, om, mult


def _lru_fwd(proj, xrc, wa, wx, ba, bx, lam, name, plan=None):
    lp, d = xrc.shape
    tm = BLOCK

    def body(r_idx, first, gate_ref, x_ref, wa_ref, wx_ref, ba_ref, bx_ref, lam_ref,
             y_ref, h_ref, carry):
        @pl.when(first)
        def _():
            carry[...] = jnp.zeros_like(carry)

        x = x_ref[...]
        _, _, ig, _, a, _, mult = _lru_gates(x, wa_ref, wx_ref, ba_ref[...], bx_ref[...], lam_ref[...])
        u = jnp.where(_real_rows(r_idx, tm), mult * ig * x, 0.0)
        acum, hloc = _scan_fwd(a, u, tm)
        h = hloc + acum * carry[0:1, :]
        h_ref[...] = h
        carry[0:1, :] = h[tm - 1:tm, :]
        y_ref[...] = (_gelu(gate_ref[...]) * h).astype(y_ref.dtype)

    return _rowcall(name, body, lp, tm, rows=[(proj, d, 0), (xrc, d, 0)],
                    vecs=[wa, wx, ba, bx, lam], outs=[(d, BF16), (d, F32)],
                    scratch=[pltpu.VMEM((SUBLANES, d), F32)], seq=True, plan=plan)


def _lru_bwd(proj, xrc, hl, dmix, wa, wx, ba, bx, lam, name, plan=None):
    lp, d = xrc.shape
    tm = BLOCK

    def body(r_idx, first, gate_ref, x_ref, h_ref, dy_ref, hp_ref, wa_ref, wx_ref, ba_ref, bx_ref,
             lam_ref, dgate_ref, dx_ref, dwa_ref, dwx_ref, dba_ref, dbx_ref, dlam_ref, carry):
        @pl.when(first)
        def _():
            carry[...] = jnp.zeros_like(carry)
            dwa_ref[...] = jnp.zeros_like(dwa_ref)
            dwx_ref[...] = jnp.zeros_like(dwx_ref)
            dba_ref[...] = jnp.zeros_like(dba_ref)
            dbx_ref[...] = jnp.zeros_like(dbx_ref)
            dlam_ref[...] = jnp.zeros_like(dlam_ref)

        x = x_ref[...]
        lam = lam_ref[...]
        xb, r, ig, sp, a, om, mult = _lru_gates(x, wa_ref, wx_ref, ba_ref[...], bx_ref[...], lam)
        h = h_ref[...]
        dy = dy_ref[...]
        gl, dgl = _gelu_and_grad(gate_ref[...])
        dgate_ref[...] = (dy * h * dgl).astype(dgate_ref.dtype)
        row = _row_iota(tm)
        lastrow = row == tm - 1
        xg = dy * gl + jnp.where(lastrow, carry[0:1, :], 0.0)
        c = jnp.where(lastrow, 1.0, pltpu.roll(a, tm - 1, 0))
        _, g = _scan_rev(c, xg, tm)
        carry[0:1, :] = a[0:1, :] * g[0:1, :]
        hprev_in = jnp.where(r_idx > 0, hp_ref[SUBLANES - 1:SUBLANES, :], 0.0)
        hprev = jnp.where(row == 0, hprev_in, pltpu.roll(h, 1, 0))
        du = jnp.where(_real_rows(r_idx, tm), g, 0.0)
        da = g * hprev
        dmult = du * ig * x
        dig = du * mult * x
        dxv = du * mult * ig
        e2 = 1.0 - om
        dlog_a = da * a - dmult * e2 / mult
        dr = dlog_a * (-LRU_C) * sp
        dsp = jnp.sum(dlog_a * (-LRU_C) * r, axis=0, keepdims=True)
        dlam_ref[...] += -dsp * _sigmoid(-lam)
        dza = dr * r * (1.0 - r)
        dzx = dig * ig * (1.0 - ig)
        dba_ref[...] += jnp.sum(dza, axis=0, keepdims=True)
        dbx_ref[...] += jnp.sum(dzx, axis=0, keepdims=True)
        dzab = dza.astype(BF16)
        dzxb = dzx.astype(BF16)
        parts = []
        for n in range(LRU_BLOCKS):
            sl = slice(n * LRU_BS, (n + 1) * LRU_BS)
            dwa_ref[n] += _dot_tn(xb[:, sl], dzab[:, sl])
            dwx_ref[n] += _dot_tn(xb[:, sl], dzxb[:, sl])
            parts.append(_dot_nt(dzab[:, sl], wa_ref[n]) + _dot_nt(dzxb[:, sl], wx_ref[n]))
        dx_ref[...] = dxv + jnp.concatenate(parts, axis=1)

    return _rowcall(name, body, lp, tm,
                    rows=[(proj, d, 0), (xrc, d, 0), (hl, d, 0), (dmix, d, 0)],
                    prevs=[(hl, d, 0)], vecs=[wa, wx, ba, bx, lam],
                    outs=[(d, BF16), (d, F32)],
                    accs=[((LRU_BLOCKS, LRU_BS, LRU_BS), F32), ((LRU_BLOCKS, LRU_BS, LRU_BS), F32),
                          ((1, d), F32), ((1, d), F32), ((1, d), F32)],
                    scratch=[pltpu.VMEM((SUBLANES, d), F32)], reverse=True, seq=True, plan=plan)


_SLOPES = [2.0 ** (-8.0 * (h + 1) / N_Q_HEADS) for h in range(N_Q_HEADS)]
_QK_SCALE = HEAD_DIM ** -0.5
_QCOL = 2 * D_MODEL // D_MODEL
_KCOL = (3 * D_MODEL) // LANES
_VCOL = _KCOL + 1


def _attn_masks(n):
    start = pl.multiple_of(jnp.maximum(n - 1, 0) * BLOCK, BLOCK)
    qi = n * BLOCK + lax.broadcasted_iota(jnp.int32, (BLOCK, 2 * BLOCK), 0)
    kj = start + lax.broadcasted_iota(jnp.int32, (BLOCK, 2 * BLOCK), 1)
    dist = qi - kj
    ok = (kj >= BLOCK) & (dist >= 0) & (dist < WINDOW)
    dm = (n * BLOCK - PAD + lax.broadcasted_iota(jnp.int32, (BLOCK, N_META), 0)
          - lax.broadcasted_iota(jnp.int32, (BLOCK, N_META), 1))
    okm = dm >= 0
    return start, ok, dist.astype(F32), okm, jnp.minimum(dm, WINDOW).astype(F32)


def _group_rows(ref, g, scale=None):
    x = jnp.concatenate(
        [ref[:, (g * Q_PER_KV + hh) * HEAD_DIM:(g * Q_PER_KV + hh + 1) * HEAD_DIM] for hh in range(Q_PER_KV)],
        axis=0)
    return (x if scale is None else x * scale).astype(BF16)


def _attn_probs(s, sm, sink_ref, g, ok, distf, okm, dmf):
    slope = jnp.stack([jnp.full((1, 1), _SLOPES[g * Q_PER_KV + hh], F32) for hh in range(Q_PER_KV)])
    sink = jnp.stack([sink_ref[0:1, g * Q_PER_KV + hh:g * Q_PER_KV + hh + 1] for hh in range(Q_PER_KV)])
    s = s.reshape(Q_PER_KV, BLOCK, 2 * BLOCK)
    sm = sm.reshape(Q_PER_KV, BLOCK, N_META)
    s = jnp.where(ok[None], s - slope * distf[None], NEG)
    sm = jnp.where(okm[None], sm - slope * dmf[None], NEG)
    mx = jnp.maximum(jnp.maximum(jnp.max(s, axis=-1, keepdims=True),
                                 jnp.max(sm, axis=-1, keepdims=True)), sink)
    p = jnp.exp(s - mx)
    pm = jnp.exp(sm - mx)
    ps = jnp.exp(sink - mx)
    inv = 1.0 / (jnp.sum(p, axis=-1, keepdims=True) + jnp.sum(pm, axis=-1, keepdims=True) + ps)
    return p, pm, ps, inv


def _attn_fwd(proj, sinks, name, plan=None):
    lp = proj.shape[0]
    nblk = lp // BLOCK
    p_in, p_shapes, p_out, p_scr = _plan_parts(plan)

    def body(*refs):
        q_ref, k_ref, v_ref, sink_ref = refs[:4]
        cins = refs[4:4 + len(p_in)]
        o_ref = refs[4 + len(p_in)]
        couts = refs[5 + len(p_in):5 + len(p_in) + len(p_out)]
        sems = refs[5 + len(p_in) + len(p_out):]
        n = pl.program_id(0)
        if plan is not None:
            @pl.when(n == 0)
            def _():
                plan.start(cins, couts, sems)

        start, ok, distf, okm, dmf = _attn_masks(n)
        kb = k_ref[pl.ds(start, 2 * BLOCK), :].astype(BF16)
        vb = v_ref[pl.ds(start, 2 * BLOCK), :].astype(BF16)
        km = k_ref[PAD:BLOCK, :].astype(BF16)
        vm = v_ref[PAD:BLOCK, :].astype(BF16)
        rows = Q_PER_KV * BLOCK
        gsl = [slice(g * HEAD_DIM, (g + 1) * HEAD_DIM) for g in range(N_KV_HEADS)]
        raw = []
        for g in range(N_KV_HEADS):
            qg = _group_rows(q_ref, g, _QK_SCALE)
            raw.append((_dot_nt(qg, kb[:, gsl[g]]), _dot_nt(qg, km[:, gsl[g]])))
        for g in range(N_KV_HEADS):
            gs = gsl[g]
            p, pm, _, inv = _attn_probs(raw[g][0], raw[g][1], sink_ref, g, ok, distf, okm, dmf)
            o = (_dot(p.astype(BF16).reshape(rows, 2 * BLOCK), vb[:, gs])
                 + _dot(pm.astype(BF16).reshape(rows, N_META), vm[:, gs])) * inv.reshape(rows, 1)
            for hh in range(Q_PER_KV):
                h = g * Q_PER_KV + hh
                o_ref[:, h * HEAD_DIM:(h + 1) * HEAD_DIM] = o[hh * BLOCK:(hh + 1) * BLOCK, :].astype(o_ref.dtype)
        if plan is not None:
            @pl.when(n == nblk - 1)
            def _():
                plan.wait(cins, couts, sems)

    res = pl.pallas_call(
        body, name=name, grid=(nblk,),
        in_specs=[pl.BlockSpec((BLOCK, D_MODEL), lambda n: (n, _QCOL)),
                  pl.BlockSpec((lp, LANES), lambda n: (0, _KCOL)),
                  pl.BlockSpec((lp, LANES), lambda n: (0, _VCOL)),
                  pl.BlockSpec(sinks.shape, lambda n: (0, 0))] + p_in,
        out_specs=[pl.BlockSpec((BLOCK, D_MODEL), lambda n: (n, 0))] + p_out,
        out_shape=[jax.ShapeDtypeStruct((lp, D_MODEL), BF16)] + p_shapes,
        scratch_shapes=p_scr,
        compiler_params=pltpu.CompilerParams(dimension_semantics=("arbitrary",)),
    )(proj, proj, proj, sinks, *(plan.ins if plan is not None else []))
    return res[0], res[1:]


def _attn_bwd(proj, sinks, dmix, name, plan=None):
    lp = proj.shape[0]
    nblk = lp // BLOCK

    p_in, p_shapes, p_out, p_scr = _plan_parts(plan)

    def body(*refs):
        q_ref, k_ref, v_ref, sink_ref, dy_ref = refs[:5]
        cins = refs[5:5 + len(p_in)]
        dq_ref, dk_ref, dv_ref, ds_ref = refs[5 + len(p_in):9 + len(p_in)]
        couts = refs[9 + len(p_in):9 + len(p_in) + len(p_out)]
        sems = refs[9 + len(p_in) + len(p_out):]
        n = pl.program_id(0)

        @pl.when(n == 0)
        def _():
            dk_ref[...] = jnp.zeros_like(dk_ref)
            dv_ref[...] = jnp.zeros_like(dv_ref)
            ds_ref[...] = jnp.zeros_like(ds_ref)
            if plan is not None:
                plan.start(cins, couts, sems)

        start, ok, distf, okm, dmf = _attn_masks(n)
        kb = k_ref[pl.ds(start, 2 * BLOCK), :].astype(BF16)
        vb = v_ref[pl.ds(start, 2 * BLOCK), :].astype(BF16)
        km = k_ref[PAD:BLOCK, :].astype(BF16)
        vm = v_ref[PAD:BLOCK, :].astype(BF16)
        lane16 = lax.broadcasted_iota(jnp.int32, (1, N_Q_HEADS), 1)
        dsink = jnp.zeros((1, N_Q_HEADS), F32)
        rows = Q_PER_KV * BLOCK
        gsl = [slice(g * HEAD_DIM, (g + 1) * HEAD_DIM) for g in range(N_KV_HEADS)]
        pre = []
        for g in range(N_KV_HEADS):
            qg = _group_rows(q_ref, g, _QK_SCALE)
            dog = _group_rows(dy_ref, g)
            pre.append((qg, dog, _dot_nt(qg, kb[:, gsl[g]]), _dot_nt(qg, km[:, gsl[g]]),
                        _dot_nt(dog, vb[:, gsl[g]]), _dot_nt(dog, vm[:, gsl[g]])))
        for g in range(N_KV_HEADS):
            gs = gsl[g]
            qg, dog, s_raw, sm_raw, dp, dpm = pre[g]
            p, pm, ps, inv = _attn_probs(s_raw, sm_raw, sink_ref, g, ok, distf, okm, dmf)
            pn, pmn, psn = p * inv, pm * inv, ps * inv
            dp = dp.reshape(Q_PER_KV, BLOCK, 2 * BLOCK)
            dpm = dpm.reshape(Q_PER_KV, BLOCK, N_META)
            delta = (jnp.sum(pn * dp, axis=-1, keepdims=True)
                     + jnp.sum(pmn * dpm, axis=-1, keepdims=True))
            dsb = (pn * (dp - delta)).astype(BF16).reshape(rows, 2 * BLOCK)
            dsm = (pmn * (dpm - delta)).astype(BF16).reshape(rows, N_META)
            dsk = jnp.sum(psn * delta, axis=1, keepdims=True)
            for hh in range(Q_PER_KV):
                dsink = dsink - jnp.where(lane16 == g * Q_PER_KV + hh, dsk[hh], 0.0)
            dq = (_dot(dsb, kb[:, gs]) + _dot(dsm, km[:, gs])) * _QK_SCALE
            for hh in range(Q_PER_KV):
                h = g * Q_PER_KV + hh
                dq_ref[:, h * HEAD_DIM:(h + 1) * HEAD_DIM] = dq[hh * BLOCK:(hh + 1) * BLOCK, :].astype(dq_ref.dtype)
            pnb = pn.astype(BF16).reshape(rows, 2 * BLOCK)
            pmnb = pmn.astype(BF16).reshape(rows, N_META)
            dk_ref[pl.ds(start, 2 * BLOCK), gs] += _dot_tn(dsb, qg)
            dv_ref[pl.ds(start, 2 * BLOCK), gs] += _dot_tn(pnb, dog)
            dk_ref[PAD:BLOCK, gs] += _dot_tn(dsm, qg)
            dv_ref[PAD:BLOCK, gs] += _dot_tn(pmnb, dog)
        ds_ref[...] += dsink
        if plan is not None:
            @pl.when(n == nblk - 1)
            def _():
                plan.wait(cins, couts, sems)

    res = pl.pallas_call(
        body, name=name, grid=(nblk,),
        in_specs=[pl.BlockSpec((BLOCK, D_MODEL), lambda n: (n, _QCOL)),
                  pl.BlockSpec((lp, LANES), lambda n: (0, _KCOL)),
                  pl.BlockSpec((lp, LANES), lambda n: (0, _VCOL)),
                  pl.BlockSpec(sinks.shape, lambda n: (0, 0)),
                  pl.BlockSpec((BLOCK, D_MODEL), lambda n: (n, 1))] + p_in,
        out_specs=[pl.BlockSpec((BLOCK, D_MODEL), lambda n: (n, 0)),
                   pl.BlockSpec((lp, LANES), lambda n: (0, 0)),
                   pl.BlockSpec((lp, LANES), lambda n: (0, 0)),
                   pl.BlockSpec((1, N_Q_HEADS), lambda n: (0, 0))] + p_out,
        out_shape=[jax.ShapeDtypeStruct((lp, D_MODEL), BF16),
                   jax.ShapeDtypeStruct((lp, LANES), F32),
                   jax.ShapeDtypeStruct((lp, LANES), F32),
                   jax.ShapeDtypeStruct((1, N_Q_HEADS), F32)] + p_shapes,
        scratch_shapes=p_scr,
        compiler_params=pltpu.CompilerParams(dimension_semantics=("arbitrary",)),
    )(proj, proj, proj, sinks, dmix, *(plan.ins if plan is not None else []))
    return res[:4], res[4:]


_ZW = D_SSM
_XBC_W = D_SSM + 2 * SSD_GROUPS * SSD_N
_DT_COL = (_ZW + _XBC_W) // LANES
EVEN_IN = 3 * D_MODEL + 2 * LANES
ODD_IN = _ZW + _XBC_W + SSD_HEADS
ODD_IN_PAD = _ZW + _XBC_W + LANES


def _ssm_convprep_fwd(proj, cw, cb, dt_bias, name):
    lp = proj.shape[0]
    kk = cw.shape[0]
    tm, wc = _conv_tiles(lp, _XBC_W)
    offb = _ZW // wc
    nj = _XBC_W // wc
    hb = tm // SUBLANES

    def body(x_ref, xp_ref, dtr_ref, w_ref, b_ref, bias_ref, xc_ref, act_ref, dt_ref):
        i, j = pl.program_id(0), pl.program_id(1)
        real = _real_rows(i, tm)
        xv = x_ref[...]
        xx = jnp.concatenate([jnp.where(i > 0, xp_ref[...], 0.0), xv], axis=0)
        acc = b_ref[...] + w_ref[kk - 1:kk, :] * xv
        for m in range(1, kk):
            acc = acc + w_ref[kk - 1 - m:kk - m, :] * pltpu.roll(xx, m, 0)[SUBLANES:, :]
        xc_ref[...] = acc
        act, _ = _silu_and_grad(acc)
        act_ref[...] = jnp.where(real, act, 0.0)

        @pl.when(j == 0)
        def _():
            dt_ref[...] = jnp.where(real, _softplus(dtr_ref[...] + bias_ref[...]), 0.0)

    return pl.pallas_call(
        body, name=name, grid=(lp // tm, nj),
        in_specs=[pl.BlockSpec((tm, wc), lambda i, j: (i, offb + j)),
                  pl.BlockSpec((SUBLANES, wc), lambda i, j: (jnp.maximum(i * hb - 1, 0), offb + j)),
                  pl.BlockSpec((tm, LANES), lambda i, j: (i, _DT_COL)),
                  pl.BlockSpec((kk, wc), lambda i, j: (0, j)),
                  pl.BlockSpec((1, wc), lambda i, j: (0, j)),
                  pl.BlockSpec((1, LANES), lambda i, j: (0, 0))],
        out_specs=[pl.BlockSpec((tm, wc), lambda i, j: (i, j)),
                   pl.BlockSpec((tm, wc), lambda i, j: (i, j)),
                   pl.BlockSpec((tm, LANES), lambda i, j: (i, 0))],
        out_shape=[jax.ShapeDtypeStruct((lp, _XBC_W), F32), jax.ShapeDtypeStruct((lp, _XBC_W), F32),
                   jax.ShapeDtypeStruct((lp, LANES), F32)],
        compiler_params=pltpu.CompilerParams(dimension_semantics=("parallel", "arbitrary")),
    )(proj, proj, proj, cw, cb, dt_bias)


def _ssd_common(dt, alog):
    a = -jnp.exp(alog)
    cs = _cumsum_rows(dt * a, BLOCK)
    cst = cs.T
    cl = cs[BLOCK - 1:BLOCK, :]
    tril = (lax.broadcasted_iota(jnp.int32, (BLOCK, BLOCK), 0)
            >= lax.broadcasted_iota(jnp.int32, (BLOCK, BLOCK), 1))
    return a, cs, cst, cl, jnp.exp(cs), jnp.exp(cl - cs), jnp.exp(cl), tril


def _head_cols(ecl, g):
    lane = lax.broadcasted_iota(jnp.int32, (1, SSD_HPG * SSD_P), 1)
    e = [ecl[:, SSD_HPG * g + hh:SSD_HPG * g + hh + 1] for hh in range(SSD_HPG)]
    return jnp.where(lane < SSD_P, e[0], jnp.where(lane < 2 * SSD_P, e[1],
                                                   jnp.where(lane < 3 * SSD_P, e[2], e[3])))


def _ssd_fwd(xbc, dt, alog, name, plan=None):
    lp = xbc.shape[0]
    nc = lp // BLOCK
    gw = SSD_HPG * SSD_P
    p_in, p_shapes, p_out, p_scr = _plan_parts(plan)

    def body(*refs):
        xs_ref, b_ref, c_ref, dt_ref, alog_ref = refs[:5]
        cins = refs[5:5 + len(p_in)]
        y_ref, so_ref = refs[5 + len(p_in):7 + len(p_in)]
        couts = refs[7 + len(p_in):7 + len(p_in) + len(p_out)]
        st, fx = refs[7 + len(p_in) + len(p_out):9 + len(p_in) + len(p_out)]
        sems = refs[9 + len(p_in) + len(p_out):]
        n = pl.program_id(0)

        @pl.when(n == 0)
        def _():
            st[...] = jnp.zeros_like(st)
            if plan is not None:
                plan.start(cins, couts, sems)

        dtv = dt_ref[...]
        _, cs, cst, cl, e, f, ecl, tril = _ssd_common(dtv, alog_ref[...])
        pre = []
        for g in range(SSD_GROUPS):
            bg = b_ref[:, g * SSD_N:(g + 1) * SSD_N].astype(BF16)
            cg = c_ref[:, g * SSD_N:(g + 1) * SSD_N].astype(BF16)
            stg = st[g]
            so_ref[0, g] = stg
            pre.append((bg, stg, _dot_nt(cg, bg), _dot(cg, stg.astype(BF16))))
        for g in range(SSD_GROUPS):
            bg, stg, gm, yoff = pre[g]
            heads = [SSD_HPG * g + hh for hh in range(SSD_HPG)]
            cols = lambda v: jnp.stack([v[:, h:h + 1] for h in heads])
            x4 = jnp.stack([xs_ref[:, h * SSD_P:(h + 1) * SSD_P] for h in heads])
            csr = jnp.stack([cst[h:h + 1, :] for h in heads])
            m = gm[None] * jnp.exp(jnp.where(tril[None], cols(cs) - csr, NEG))
            xdt = x4 * cols(dtv)
            yoff4 = jnp.stack([yoff[:, hh * SSD_P:(hh + 1) * SSD_P] for hh in range(SSD_HPG)])
            y4 = (jnp.einsum("hls,hsp->hlp", m.astype(BF16), xdt.astype(BF16), preferred_element_type=F32)
                  + cols(e) * yoff4)
            fx4 = cols(f) * xdt
            for hh, h in enumerate(heads):
                y_ref[:, h * SSD_P:(h + 1) * SSD_P] = y4[hh]
                fx[:, hh * SSD_P:(hh + 1) * SSD_P] = fx4[hh]
            st[g] = stg * _head_cols(ecl, g) + _dot_tn(bg, fx[...].astype(BF16))
        if plan is not None:
            @pl.when(n == nc - 1)
            def _():
                plan.wait(cins, couts, sems)

    res = pl.pallas_call(
        body, name=name, grid=(nc,),
        in_specs=[pl.BlockSpec((BLOCK, D_SSM), lambda n: (n, 0)),
                  pl.BlockSpec((BLOCK, 1024), lambda n: (n, 2)),
                  pl.BlockSpec((BLOCK, 1024), lambda n: (n, 3)),
                  pl.BlockSpec((BLOCK, LANES), lambda n: (n, 0)),
                  pl.BlockSpec((1, LANES), lambda n: (0, 0))] + p_in,
        out_specs=[pl.BlockSpec((BLOCK, D_SSM), lambda n: (n, 0)),
                   pl.BlockSpec((1, SSD_GROUPS, SSD_N, gw), lambda n: (n, 0, 0, 0))] + p_out,
        out_shape=[jax.ShapeDtypeStruct((lp, D_SSM), F32),
                   jax.ShapeDtypeStruct((nc, SSD_GROUPS, SSD_N, gw), F32)] + p_shapes,
        scratch_shapes=[pltpu.VMEM((SSD_GROUPS, SSD_N, gw), F32), pltpu.VMEM((BLOCK, gw), F32)] + p_scr,
        compiler_params=pltpu.CompilerParams(dimension_semantics=("arbitrary",)),
    )(xbc, xbc, xbc, dt, alog, *(plan.ins if plan is not None else []))
    return res[:2], res[2:]


def _ssd_bwd(xbc, dt, alog, states, dy, xc, proj, dt_bias, dxskip, name, plan=None):
    lp = xbc.shape[0]
    nc = lp // BLOCK
    gw = SSD_HPG * SSD_P
    p_in, p_shapes, p_out, p_scr = _plan_parts(plan)

    def body(*refs):
        xs_ref, b_ref, c_ref, dt_ref, alog_ref, dy_ref, st_ref, xc_ref, dtr_ref, bias_ref, dsk_ref = refs[:11]
        cins = refs[11:11 + len(p_in)]
        dxc_ref, ddtr_ref, dalog_ref, dbias_ref = refs[11 + len(p_in):15 + len(p_in)]
        couts = refs[15 + len(p_in):15 + len(p_in) + len(p_out)]
        dst, edy, fx, gx = refs[15 + len(p_in) + len(p_out):19 + len(p_in) + len(p_out)]
        sems = refs[19 + len(p_in) + len(p_out):]
        i = pl.program_id(0)

        @pl.when(i == 0)
        def _():
            dst[...] = jnp.zeros_like(dst)
            dalog_ref[...] = jnp.zeros_like(dalog_ref)
            dbias_ref[...] = jnp.zeros_like(dbias_ref)
            if plan is not None:
                plan.start(cins, couts, sems)

        dtv = dt_ref[...]
        a, cs, cst, cl, e, f, ecl, tril = _ssd_common(dtv, alog_ref[...])
        lane = lax.broadcasted_iota(jnp.int32, (1, LANES), 1)
        sub = _row_iota(BLOCK)
        triu = (lax.broadcasted_iota(jnp.int32, (BLOCK, BLOCK), 1)
                >= lax.broadcasted_iota(jnp.int32, (BLOCK, BLOCK), 0))
        dcs = jnp.zeros((BLOCK, LANES), F32)
        dcst = jnp.zeros((LANES, BLOCK), F32)
        dcl = jnp.zeros((1, LANES), F32)
        ddtx = jnp.zeros((BLOCK, LANES), F32)
        pre = []
        for g in range(SSD_GROUPS):
            bg = b_ref[:, g * SSD_N:(g + 1) * SSD_N].astype(BF16)
            cg = c_ref[:, g * SSD_N:(g + 1) * SSD_N].astype(BF16)
            stb = st_ref[0, g].astype(BF16)
            dsob = dst[g].astype(BF16)
            pre.append((bg, cg, stb, dsob, _dot_nt(cg, bg), _dot_nt(bg, cg), _dot(cg, stb), _dot(bg, dsob)))
        for g in range(SSD_GROUPS):
            bg, cg, stb, dsob, gm, gmt, yraw, dfx = pre[g]
            dso = dst[g]
            prodsum = jnp.sum(dso * st_ref[0, g], axis=0, keepdims=True)
            heads = [SSD_HPG * g + hh for hh in range(SSD_HPG)]
            cols = lambda v: jnp.stack([v[:, h:h + 1] for h in heads])
            parts = lambda v: jnp.stack([v[:, hh * SSD_P:(hh + 1) * SSD_P] for hh in range(SSD_HPG)])
            x4 = jnp.stack([xs_ref[:, h * SSD_P:(h + 1) * SSD_P] for h in heads])
            dy4 = jnp.stack([dy_ref[:, h * SSD_P:(h + 1) * SSD_P] for h in heads])
            csc, dtc, ec, fc = cols(cs), cols(dtv), cols(e), cols(f)
            csr = jnp.stack([cst[h:h + 1, :] for h in heads])
            seg = csc - csr
            lam = jnp.exp(jnp.where(tril[None], seg, NEG))
            lamt = jnp.exp(jnp.where(triu[None], -seg, NEG))
            mt = gmt[None] * lamt
            xdt = x4 * dtc
            dyb = dy4.astype(BF16)
            dm = jnp.einsum("hlp,hsp->hls", dyb, xdt.astype(BF16), preferred_element_type=F32)
            dfx4 = parts(dfx)
            dxdt = jnp.einsum("hsl,hlp->hsp", mt.astype(BF16), dyb, preferred_element_type=F32) + fc * dfx4
            dml = dm * lam
            w = dml * gm[None]
            dgm = jnp.sum(dml, axis=0)
            dff = jnp.sum(dfx4 * xdt, axis=2, keepdims=True) * fc
            colv = (jnp.sum(w, axis=2, keepdims=True)
                    + jnp.sum(dy4 * parts(yraw), axis=2, keepdims=True) * ec - dff)
            roww = jnp.sum(w, axis=1, keepdims=True)
            ddtc = jnp.sum(dxdt * x4, axis=2, keepdims=True)
            dffs = jnp.sum(dff, axis=1, keepdims=True)
            dxs4 = dxdt * dtc
            edy4 = ec * dy4
            fx4 = fc * xdt
            for hh, h in enumerate(heads):
                ls = slice(hh * SSD_P, (hh + 1) * SSD_P)
                onl = (lane == h).astype(F32)
                dcs = dcs + colv[hh] * onl
                dcst = dcst - (sub == h).astype(F32) * roww[hh]
                dcl = dcl + (dffs[hh] + ecl[:, h:h + 1] * jnp.sum(prodsum[:, ls], axis=1, keepdims=True)) * onl
                ddtx = ddtx + ddtc[hh] * onl
                gx[:, h * SSD_P:(h + 1) * SSD_P] = dxs4[hh]
                edy[:, ls] = edy4[hh]
                fx[:, ls] = fx4[hh]
            edyb = edy[...].astype(BF16)
            fxb = fx[...].astype(BF16)
            dgb = dgm.astype(BF16)
            gx[:, D_SSM + 1024 + g * SSD_N:D_SSM + 1024 + (g + 1) * SSD_N] = _dot_nt(edyb, stb) + _dot(dgb, bg)
            gx[:, D_SSM + g * SSD_N:D_SSM + (g + 1) * SSD_N] = _dot_nt(fxb, dsob) + _dot_tn(dgb, cg)
            dst[g] = dso * _head_cols(ecl, g) + _dot_tn(cg, edyb)
        dcs = dcs + dcst.T + jnp.where(sub == BLOCK - 1, dcl, 0.0)
        dda = _rev_cumsum_rows(dcs, BLOCK)
        dalog_ref[...] += jnp.sum(dda * dtv, axis=0, keepdims=True) * a
        real = _real_rows(nc - 1 - i, BLOCK)
        _, ds = _silu_and_grad(xc_ref[...])
        dxc_ref[:, :D_SSM] = jnp.where(
            real, (gx[:, :D_SSM] + dsk_ref[...].astype(F32)) * ds[:, :D_SSM], 0.0).astype(dxc_ref.dtype)
        dxc_ref[:, D_SSM:] = jnp.where(real, gx[:, D_SSM:] * ds[:, D_SSM:], 0.0).astype(dxc_ref.dtype)
        dd = jnp.where(real, (ddtx + dda * a) * _sigmoid(dtr_ref[...] + bias_ref[...]), 0.0)
        ddtr_ref[...] = dd.astype(ddtr_ref.dtype)
        dbias_ref[...] += jnp.sum(dd, axis=0, keepdims=True)
        if plan is not None:
            @pl.when(i == nc - 1)
            def _():
                plan.wait(cins, couts, sems)

    rev = lambda i: nc - 1 - i
    res = pl.pallas_call(
        body, name=name, grid=(nc,),
        in_specs=[pl.BlockSpec((BLOCK, D_SSM), lambda i: (rev(i), 0)),
                  pl.BlockSpec((BLOCK, 1024), lambda i: (rev(i), 2)),
                  pl.BlockSpec((BLOCK, 1024), lambda i: (rev(i), 3)),
                  pl.BlockSpec((BLOCK, LANES), lambda i: (rev(i), 0)),
                  pl.BlockSpec((1, LANES), lambda i: (0, 0)),
                  pl.BlockSpec((BLOCK, D_SSM), lambda i: (rev(i), 0)),
                  pl.BlockSpec((1, SSD_GROUPS, SSD_N, gw), lambda i: (rev(i), 0, 0, 0)),
                  pl.BlockSpec((BLOCK, _XBC_W), lambda i: (rev(i), 0)),
                  pl.BlockSpec((BLOCK, LANES), lambda i: (rev(i), _DT_COL)),
                  pl.BlockSpec((1, LANES), lambda i: (0, 0)),
                  pl.BlockSpec((BLOCK, D_SSM), lambda i: (rev(i), 0))] + p_in,
        out_specs=[pl.BlockSpec((BLOCK, _XBC_W), lambda i: (rev(i), 0)),
                   pl.BlockSpec((BLOCK, LANES), lambda i: (rev(i), 0)),
                   pl.BlockSpec((1, LANES), lambda i: (0, 0)),
                   pl.BlockSpec((1, LANES), lambda i: (0, 0))] + p_out,
        out_shape=[jax.ShapeDtypeStruct((lp, _XBC_W), BF16),
                   jax.ShapeDtypeStruct((lp, LANES), BF16),
                   jax.ShapeDtypeStruct((1, LANES), F32),
                   jax.ShapeDtypeStruct((1, LANES), F32)] + p_shapes,
        scratch_shapes=[pltpu.VMEM((SSD_GROUPS, SSD_N, gw), F32),
                        pltpu.VMEM((BLOCK, gw), F32), pltpu.VMEM((BLOCK, gw), F32),
                        pltpu.VMEM((BLOCK, _XBC_W), F32)] + p_scr,
        compiler_params=pltpu.CompilerParams(dimension_semantics=("arbitrary",)),
    )(xbc, xbc, xbc, dt, alog, dy, states, xc, proj, dt_bias, dxskip,
      *(plan.ins if plan is not None else []))
    return res[:4], res[4:]


_GN_GROUPS = 8
_GN_W = D_SSM // _GN_GROUPS


def _ssm_gate_out_postnorm(yssd, xbc, proj, dskip, gnorm, w_out, h, g, next_pre, name):
    lp, d = h.shape
    tm = _pick(lp, (320, 256, 128))

    def body(y_ref, x_ref, z_ref, d_ref, gn_ref, w_ref, h_ref, g_ref, np_ref, yn_ref, o_ref, hn_ref, u_ref):
        sz, _ = _silu_and_grad(z_ref[...])
        y2 = (y_ref[...] + d_ref[...] * x_ref[...]) * sz
        for k in range(_GN_GROUPS):
            sl = slice(k * _GN_W, (k + 1) * _GN_W)
            yk = y2[:, sl]
            rs = lax.rsqrt(jnp.mean(yk * yk, axis=-1, keepdims=True) + EPS)
            yn_ref[:, sl] = (yk * rs * gn_ref[:, sl]).astype(yn_ref.dtype)
        acc = _dot(yn_ref[...], w_ref[...])
        o_ref[...] = acc
        rs = lax.rsqrt(jnp.mean(acc * acc, axis=-1, keepdims=True) + EPS)
        hn = jnp.where(_real_rows(pl.program_id(0), tm), h_ref[...] + acc * rs * g_ref[...], 0.0)
        hn_ref[...] = hn
        rs2 = lax.rsqrt(jnp.mean(hn * hn, axis=-1, keepdims=True) + EPS)
        u_ref[...] = (hn * rs2 * np_ref[...]).astype(u_ref.dtype)

    wide = pl.BlockSpec((tm, D_SSM), lambda i: (i, 0))
    row = pl.BlockSpec((tm, d), lambda i: (i, 0))
    vec = lambda n: pl.BlockSpec((1, n), lambda i: (0, 0))
    return pl.pallas_call(
        body, name=name, grid=(lp // tm,),
        in_specs=[wide, wide, wide, vec(D_SSM), vec(D_SSM), pl.BlockSpec((D_SSM, d), lambda i: (0, 0)),
                  row, vec(d), vec(d)],
        out_specs=[wide, row, row, row],
        out_shape=[jax.ShapeDtypeStruct((lp, D_SSM), BF16), jax.ShapeDtypeStruct((lp, d), F32),
                   jax.ShapeDtypeStruct((lp, d), F32), jax.ShapeDtypeStruct((lp, d), BF16)],
        compiler_params=pltpu.CompilerParams(dimension_semantics=("parallel",),
                                             vmem_limit_bytes=_VMEM_LIMIT_WIDE),
    )(yssd, xbc, proj, dskip, gnorm, w_out, h, g, next_pre)


def _ssm_out_gate_bwd(o, gain, dh, w_out, yssd, xbc, proj, dskip, gnorm, name):
    lp, d = o.shape
    tm = _pick(lp, (320, 256, 128))
    tk = D_SSM // 2

    def body(o_ref, gain_ref, dh_ref, w_ref, y_ref, x_ref, z_ref, d_ref, g_ref,
             dy_ref, dx_ref, dz_ref, dd_ref, dg_ref, do_ref, dgain_ref):
        first = pl.program_id(0) == 0
        dob = _postnorm_bwd_tile(o_ref, gain_ref, dh_ref, do_ref, dgain_ref)
        z = z_ref[...]
        sz, dsz = _silu_and_grad(z)
        xs = x_ref[...]
        y1 = y_ref[...] + d_ref[...] * xs
        y2 = y1 * sz
        for k in range(_GN_GROUPS):
            sl = slice(k * _GN_W, (k + 1) * _GN_W)
            if k % (tk // _GN_W) == 0:
                dyn = _dot_nt(dob, w_ref[k * _GN_W:k * _GN_W + tk, :])
            loc = slice((k % (tk // _GN_W)) * _GN_W, (k % (tk // _GN_W) + 1) * _GN_W)
            dx, dgt = _rms_bwd(y2[:, sl], g_ref[:, sl], dyn[:, loc])
            dy1 = dx * sz[:, sl]
            dy_ref[:, sl] = dy1.astype(dy_ref.dtype)
            dx_ref[:, sl] = (dy1 * d_ref[:, sl]).astype(dx_ref.dtype)
            dz_ref[:, sl] = (dx * y1[:, sl] * dsz[:, sl]).astype(dz_ref.dtype)

            @pl.when(first)
            def _():
                dd_ref[:, sl] = jnp.zeros((1, _GN_W), F32)
                dg_ref[:, sl] = jnp.zeros((1, _GN_W), F32)

            dd_ref[:, sl] += jnp.sum(dy1 * xs[:, sl], axis=0, keepdims=True)
            dg_ref[:, sl] += jnp.sum(dgt, axis=0, keepdims=True)

    tile = pl.BlockSpec((tm, D_SSM), lambda i: (i, 0))
    vec = pl.BlockSpec((1, D_SSM), lambda i: (0, 0))
    row = pl.BlockSpec((tm, d), lambda i: (i, 0))
    rvec = pl.BlockSpec((1, d), lambda i: (0, 0))
    return pl.pallas_call(
        body, name=name, grid=(lp // tm,),
        in_specs=[row, rvec, row, pl.BlockSpec((D_SSM, d), lambda i: (0, 0)), tile, tile, tile, vec, vec],
        out_specs=[tile, tile, tile, vec, vec, row, rvec],
        out_shape=[jax.ShapeDtypeStruct((lp, D_SSM), BF16)] * 3 + [jax.ShapeDtypeStruct((1, D_SSM), F32)] * 2
        + [jax.ShapeDtypeStruct((lp, d), BF16), jax.ShapeDtypeStruct((1, d), F32)],
        compiler_params=pltpu.CompilerParams(dimension_semantics=("arbitrary",),
                                             vmem_limit_bytes=_VMEM_LIMIT_WIDE),
    )(o, gain, dh, w_out, yssd, xbc, proj, dskip, gnorm)


def _shape2d(shape):
    n = math.prod(shape)
    if len(shape) == 2:
        return tuple(shape)
    return (n // LANES, LANES) if n % LANES == 0 else (1, n)


def _adamw_many(ws, gs, ms, vs, name):
    n = len(ws)
    c1 = 1.0 / (1.0 - ADAM_B1 ** ADAM_STEP)
    c2 = 1.0 / (1.0 - ADAM_B2 ** ADAM_STEP)

    def body(*refs):
        for i in range(n):
            w_ref, g_ref, m_ref, v_ref = (refs[j * n + i] for j in range(4))
            d_ref, nm_ref, nv_ref = (refs[(4 + j) * n + i] for j in range(3))
            gv = g_ref[...]
            nm = ADAM_B1 * m_ref[...] + (1.0 - ADAM_B1) * gv
            nv = ADAM_B2 * v_ref[...] + (1.0 - ADAM_B2) * (gv * gv)
            nm_ref[...] = nm
            nv_ref[...] = nv
            d_ref[...] = -ADAM_LR * ((nm * c1) / (jnp.sqrt(nv * c2) + ADAM_EPS) + ADAM_WD * w_ref[...])

    vm = pl.BlockSpec(memory_space=pltpu.VMEM)
    return pl.pallas_call(
        body, name=name, in_specs=[vm] * (4 * n), out_specs=[vm] * (3 * n),
        out_shape=[jax.ShapeDtypeStruct(w.shape, F32) for w in ws] * 3,
    )(*ws, *gs, *ms, *vs)


def _place():
    return lax.axis_index("x"), lax.axis_index("y"), lax.axis_index("c")


def _other_chips(x, y):
    return [(1 - x, y), (x, 1 - y), (1 - x, 1 - y)]


_ANY = pl.BlockSpec(memory_space=pl.ANY)


class _Plan:
    def __init__(self, ins, out_shapes, n_remote, n_local, issue):
        self.ins = list(ins)
        self.out_shapes = list(out_shapes)
        self.issue = issue
        self.scratch = [pltpu.SemaphoreType.DMA((max(n_remote, 1),)),
                        pltpu.SemaphoreType.DMA((max(n_remote, 1),)),
                        pltpu.SemaphoreType.DMA((max(n_local, 1),))]

    def start(self, ins, outs, sems):
        sends, _, locs = self.issue(ins, outs, *sems)
        for cp in locs + sends:
            cp.start()

    def wait(self, ins, outs, sems):
        sends, recvs, locs = self.issue(ins, outs, *sems)
        for make in recvs:
            make().wait_recv()
        for cp in sends:
            cp.wait_send()
        for cp in locs:
            cp.wait()


def _plan_parts(plan):
    if plan is None:
        return [], [], [], []
    return ([_ANY] * len(plan.ins), plan.out_shapes, [_ANY] * len(plan.out_shapes), plan.scratch)


def _run_plan(plan, name):
    n_in, n_out = len(plan.ins), len(plan.out_shapes)

    def body(*refs):
        ins, outs, sems = refs[:n_in], refs[n_in:n_in + n_out], refs[n_in + n_out:]
        plan.start(ins, outs, sems)
        plan.wait(ins, outs, sems)

    return pl.pallas_call(
        body, name=name, in_specs=[_ANY] * n_in, out_specs=[_ANY] * n_out,
        out_shape=plan.out_shapes, scratch_shapes=plan.scratch,
    )(*plan.ins)


def _gather_plan(shards):
    n = len(shards)

    def issue(ins, outs, send_sems, recv_sems, local_sems):
        x, y, c = _place()
        me = 2 * x + y
        sends, recvs, locs = [], [], []
        for p in range(n):
            locs.append(pltpu.make_async_copy(ins[p], outs[p].at[me], local_sems.at[p]))
            for k, (px, py) in enumerate(_other_chips(x, y)):
                sems = dict(send_sem=send_sems.at[3 * p + k], recv_sem=recv_sems.at[3 * p + k],
                            device_id=(px, py, c), device_id_type=MESH)
                sends.append(pltpu.make_async_remote_copy(src_ref=ins[p], dst_ref=outs[p].at[me], **sems))
                recvs.append(functools.partial(pltpu.make_async_remote_copy, src_ref=ins[p],
                                               dst_ref=outs[p].at[2 * px + py], **sems))
        return sends, recvs, locs

    return _Plan(shards, [jax.ShapeDtypeStruct((N_CHIPS,) + s.shape, s.dtype) for s in shards], 3 * n, n, issue)


_REL7 = [(fx, fy, fc) for fx in (0, 1) for fy in (0, 1) for fc in (0, 1)][1:]


def _scatter8_plan(gs):
    n = len(gs)

    def issue(ins, outs, send_sems, recv_sems, local_sems):
        x, y, c = _place()
        sends = []
        for p in range(n):
            hr = gs[p].shape[1] // 2
            for k, (fx, fy, fc) in enumerate(_REL7):
                tx, ty, tc = x ^ fx, y ^ fy, c ^ fc
                src = ins[p].at[2 * tx + ty, pl.ds(pl.multiple_of(tc * hr, SUBLANES), hr), :]
                sends.append(pltpu.make_async_remote_copy(
                    src_ref=src, dst_ref=outs[p].at[k],
                    send_sem=send_sems.at[7 * p + k], recv_sem=recv_sems.at[7 * p + k],
                    device_id=(tx, ty, tc), device_id_type=MESH))
        return sends, [functools.partial(lambda cp: cp, cp) for cp in sends], []

    shapes = [jax.ShapeDtypeStruct((7, g.shape[1] // 2, g.shape[2]), g.dtype) for g in gs]
    return _Plan(gs, shapes, 7 * n, 0, issue)


def _sibling_plan(ts):
    n = len(ts)

    def issue(ins, outs, send_sems, recv_sems, local_sems):
        x, y, c = _place()
        sends = [pltpu.make_async_remote_copy(
            src_ref=ins[p], dst_ref=outs[p], send_sem=send_sems.at[p], recv_sem=recv_sems.at[p],
            device_id=(x, y, 1 - c), device_id_type=MESH) for p in range(n)]
        return sends, [functools.partial(lambda cp: cp, cp) for cp in sends], []

    return _Plan(ts, [jax.ShapeDtypeStruct(t.shape, t.dtype) for t in ts], n, 0, issue)


def _add8(g, recv, chip, core, name):
    s, r, n = g.shape
    hr = r // 2
    th = hr // 2 if (hr // 2) % SUBLANES == 0 else hr
    nt = hr // th

    def body(chip_ref, core_ref, g_ref, r_ref, o_ref):
        acc = g_ref[0].astype(F32)
        for k in range(7):
            acc = acc + r_ref[k].astype(F32)
        o_ref[...] = acc

    return pl.pallas_call(
        body, name=name,
        grid_spec=pltpu.PrefetchScalarGridSpec(
            num_scalar_prefetch=2, grid=(nt,),
            in_specs=[pl.BlockSpec((1, th, n), lambda i, ch, co: (ch[0], co[0] * nt + i, 0)),
                      pl.BlockSpec((7, th, n), lambda i, ch, co: (0, i, 0))],
            out_specs=pl.BlockSpec((th, n), lambda i, ch, co: (i, 0))),
        out_shape=jax.ShapeDtypeStruct((hr, n), F32),
        compiler_params=pltpu.CompilerParams(dimension_semantics=("parallel",)),
    )(chip, core, g, recv)


def _adamw_halves(w, own, other, m, v, core, name):
    r, n = w.shape
    hr = r // 2
    th = hr // 2 if (hr // 2) % SUBLANES == 0 else hr
    tph = hr // th
    c1 = 1.0 / (1.0 - ADAM_B1 ** ADAM_STEP)
    c2 = 1.0 / (1.0 - ADAM_B2 ** ADAM_STEP)

    def body(core_ref, w_ref, a_ref, b_ref, m_ref, v_ref, g_ref, d_ref, nm_ref, nv_ref):
        half = pl.program_id(0) // tph
        gv = jnp.where(half == core_ref[0], a_ref[...], b_ref[...])
        nm = ADAM_B1 * m_ref[...] + (1.0 - ADAM_B1) * gv
        nv = ADAM_B2 * v_ref[...] + (1.0 - ADAM_B2) * (gv * gv)
        g_ref[...] = gv
        nm_ref[...] = nm
        nv_ref[...] = nv
        d_ref[...] = -ADAM_LR * ((nm * c1) / (jnp.sqrt(nv * c2) + ADAM_EPS) + ADAM_WD * w_ref[...])

    full = pl.BlockSpec((th, n), lambda i, co: (i, 0))
    part = pl.BlockSpec((th, n), lambda i, co: (i % tph, 0))
    return pl.pallas_call(
        body, name=name,
        grid_spec=pltpu.PrefetchScalarGridSpec(
            num_scalar_prefetch=1, grid=(2 * tph,),
            in_specs=[full, part, part, full, full], out_specs=[full] * 4),
        out_shape=[jax.ShapeDtypeStruct((r, n), F32)] * 4,
        compiler_params=pltpu.CompilerParams(dimension_semantics=("parallel",)),
    )(core, w, own, other, m, v)


def _allreduce_small(pack, name):
    r, l = pack.shape
    hr = r // 2
    assert hr % SUBLANES == 0

    def body(p_ref, o_ref, sib, chips, send_sems, recv_sems):
        x, y, c = _place()
        chip = 2 * x + y
        sibling = dict(device_id=(x, y, 1 - c), device_id_type=MESH)
        mine = pl.ds(pl.multiple_of(c * hr, SUBLANES), hr)
        other = pl.ds(pl.multiple_of((1 - c) * hr, SUBLANES), hr)
        a = pltpu.make_async_remote_copy(src_ref=p_ref.at[other], dst_ref=sib, send_sem=send_sems.at[0],
                                         recv_sem=recv_sems.at[0], **sibling)
        a.start()
        a.wait()
        own, got = p_ref[mine, :], sib[...]
        chips[chip] = jnp.where(c == 0, own, got) + jnp.where(c == 0, got, own)
        sends = []
        for k, (px, py) in enumerate(_other_chips(x, y)):
            cp = pltpu.make_async_remote_copy(
                src_ref=chips.at[chip], dst_ref=chips.at[chip], send_sem=send_sems.at[1 + k],
                recv_sem=recv_sems.at[1 + k], device_id=(px, py, c), device_id_type=MESH)
            cp.start()
            sends.append(cp)
        for k, (px, py) in enumerate(_other_chips(x, y)):
            pltpu.make_async_remote_copy(
                src_ref=chips.at[chip], dst_ref=chips.at[2 * px + py], send_sem=send_sems.at[1 + k],
                recv_sem=recv_sems.at[1 + k], device_id=(px, py, c), device_id_type=MESH).wait_recv()
        for cp in sends:
            cp.wait_send()
        o_ref[mine, :] = ((chips[0] + chips[1]) + chips[2]) + chips[3]
        fin = pltpu.make_async_remote_copy(src_ref=o_ref.at[mine], dst_ref=o_ref.at[mine],
                                           send_sem=send_sems.at[4], recv_sem=recv_sems.at[4], **sibling)
        fin.start()
        pltpu.make_async_remote_copy(src_ref=o_ref.at[mine], dst_ref=o_ref.at[other],
                                     send_sem=send_sems.at[4], recv_sem=recv_sems.at[4], **sibling).wait_recv()
        fin.wait_send()

    vm = pl.BlockSpec(memory_space=pltpu.VMEM)
    return pl.pallas_call(
        body, name=name, in_specs=[vm], out_specs=vm,
        out_shape=jax.ShapeDtypeStruct((r, l), F32),
        scratch_shapes=[pltpu.VMEM((hr, l), F32), pltpu.VMEM((N_CHIPS, hr, l), F32),
                        pltpu.SemaphoreType.DMA((5,)), pltpu.SemaphoreType.DMA((5,))],
    )(pack)


def _flat_rows(a, mult=SUBLANES * LANES):
    f = a.reshape(-1)
    padn = (-f.shape[0]) % mult
    if padn:
        f = jnp.concatenate([f, jnp.zeros((padn,), f.dtype)])
    return f


def _pack(arrs, mult=SUBLANES * LANES, total_mult=None):
    flat = [_flat_rows(a, mult) for a in arrs]
    sizes = [f.shape[0] for f in flat]
    if total_mult is not None:
        padn = (-sum(sizes)) % total_mult
        if padn:
            flat.append(jnp.zeros((padn,), flat[0].dtype))
    return jnp.concatenate(flat).reshape(-1, LANES), sizes


def _unpack(pack, shapes, sizes, lead=()):
    flat = pack.reshape(lead + (-1,))
    out, off = [], 0
    for shp, sz in zip(shapes, sizes):
        n = math.prod(shp)
        out.append(flat[..., off:off + n].reshape(lead + tuple(shp)))
        off += sz
    return out


def _cols_from_shards(g):
    s, k, n = g.shape
    return jnp.transpose(g, (1, 0, 2)).reshape(k, s * n)


def _ffn_fwd(h, u, post, w_up, cw, cb, w_down, tag, next_pre=None, plan=None, loss_tgt=None):
    (hp, act, hg, hu), pouts = _ffn_up_convact(u, w_up, cw, cb, f"{tag}_up_convact", plan)
    if loss_tgt is not None:
        o, hn, un = _mm_postnorm_loss(act, w_down, h, post, loss_tgt, f"{tag}_down_postnorm_loss")
    else:
        res = _mm_postnorm_res(act, w_down, h, post, f"{tag}_down_postnorm", next_pre)
        o, hn, un = res if next_pre is not None else (*res, None)
    return hn, un, (h, u, hp, hg, hu, act, o), pouts


def _ffn_bwd(dh, saved, pre, post, w_up, cw, w_down, tag):
    h, u, hp, hg, hu, act, o = saved
    dhg, dhu, do, dpost = _ffn_down_act_bwd(o, post, dh, w_down, hg, hu, f"{tag}_down_dx_act_bwd")
    dw_down = _mm_tn(act, do, f"{tag}_down_dw")
    dxg, dwg, dbg = _conv_bwd(hp, 0, D_FF, dhg, cw, f"{tag}_conv_bwd_gate")
    dxu, dwu, dbu = _conv_bwd(hp, D_FF, D_FF, dhu, cw, f"{tag}_conv_bwd_up", w_col_off=D_FF)
    dhp = (dxg, dxu)
    dcw = jnp.concatenate([dwg, dwu], axis=1)
    dcb = jnp.concatenate([dbg, dbu], axis=1)
    (dhn, dpre), _ = _mm_nt_sh(dhp, w_up, (h, pre, dh), f"{tag}_up_dx_prenorm_bwd")
    dw_up = _mm_tn_sh(u, dhp, w_up.shape[2], f"{tag}_up_dw")
    return dhn, dict(pre=dpre, post=dpost, w_up=dw_up, conv_w=dcw[:3], conv_b=dcb, w_down=dw_down)


class _Exchange:
    GATHER_IN_LRU = ("l0_w_out", "l0_ffn_w_down")
    GATHER_IN_ATTN = ("l0_ffn_w_up", "l1_w_out")
    GATHER_IN_FFN0 = ("l1_w_in",)
    GATHER_IN_SSD = ("l1_ffn_w_up", "l1_ffn_w_down")
    AFTER_L1_OUT = ("l1_ffn_w_up", "l1_ffn_w_down", "l1_w_out")
    IN_LRU_BWD = ("l1_w_in",)
    AFTER_L0_OUT = ("l0_ffn_w_up", "l0_ffn_w_down", "l0_w_out")
    LAST = ("l0_w_in",)

    def __init__(self, late_shards):
        self.late = dict(late_shards)
        self.slabs = {}
        self.recv = {}

    def gather_plan(self, names):
        return _gather_plan([self.late[n] for n in names])

    def gathered(self, names, outs):
        return {n: (g if n in _BIG_COL else g.reshape(-1, g.shape[-1])) for n, g in zip(names, outs)}

    def scatter_plan(self, grads, names):
        for n in names:
            g = grads[n]
            self.slabs[n] = g if n in _BIG_COL else g.reshape(N_CHIPS, -1, g.shape[-1])
        return _scatter8_plan([self.slabs[n] for n in names])

    def scattered(self, names, outs):
        self.recv.update(zip(names, outs))


def _local_step(x, tgt, meta, P, ex=None):
    seq, d = x.shape
    lp = seq + BLOCK
    h0 = jnp.concatenate([jnp.zeros((PAD, d), F32), meta, x], axis=0)
    tgt_p = jnp.concatenate([jnp.zeros((BLOCK, d), F32), tgt], axis=0)

    u0 = _rmsnorm_fwd(h0, P["l0_mix_pre_norm"], "l0_mix_prenorm")
    proj0 = _mm_nn_sh(u0, P["l0_w_in"], EVEN_IN, "l0_in")
    xrc = _conv_fwd(proj0, D_MODEL, D_MODEL, P["l0_lru_conv_w"], P["l0_lru_conv_b"], "l0_lru_conv")
    lru_args = (P["l0_lru_w_a"], P["l0_lru_w_x"], P["l0_lru_b_a"], P["l0_lru_b_x"], P["l0_lru_lambda"])
    if ex:
        (ya, hl), outs = _lru_fwd(proj0, xrc, *lru_args, "l0_lru", ex.gather_plan(ex.GATHER_IN_LRU))
        P = {**P, **ex.gathered(ex.GATHER_IN_LRU, outs)}
    else:
        ya, hl = _lru_fwd(proj0, xrc, *lru_args, "l0_lru")
    yb, outs = _attn_fwd(proj0, P["l0_attn_sinks"], "l0_attn",
                         ex.gather_plan(ex.GATHER_IN_ATTN) if ex else None)
    if ex:
        P = {**P, **ex.gathered(ex.GATHER_IN_ATTN, outs)}
    o0, h1, u1 = _mm_postnorm_res((ya, yb), P["l0_w_out"], h0, P["l0_mix_post_norm"], "l0_out_postnorm",
                                  P["l0_ffn_pre_norm"])
    h2, u2, ffn0, outs = _ffn_fwd(h1, u1, P["l0_ffn_post_norm"], P["l0_ffn_w_up"], P["l0_ffn_conv_w"],
                                  P["l0_ffn_conv_b"], P["l0_ffn_w_down"], "l0_ffn", P["l1_mix_pre_norm"],
                                  ex.gather_plan(ex.GATHER_IN_FFN0) if ex else None)
    if ex:
        P = {**P, **ex.gathered(ex.GATHER_IN_FFN0, outs)}
    proj1 = _mm_nn_sh(u2, P["l1_w_in"], ODD_IN_PAD, "l1_in")
    xc1, xbc, dt = _ssm_convprep_fwd(proj1, P["l1_ssm_conv_w"], P["l1_ssm_conv_b"], P["l1_dt_bias"],
                                     "l1_ssm_convprep")
    (yssd, states), outs = _ssd_fwd(xbc, dt, P["l1_a_log"], "l1_ssd",
                                    ex.gather_plan(ex.GATHER_IN_SSD) if ex else None)
    if ex:
        P = {**P, **ex.gathered(ex.GATHER_IN_SSD, outs)}
    yn, o1, h3, u3 = _ssm_gate_out_postnorm(
        yssd, xbc, proj1, P["l1_d_skip"], P["l1_gate_norm"], P["l1_w_out"], h2, P["l1_mix_post_norm"],
        P["l1_ffn_pre_norm"], "l1_gate_out_postnorm")
    dh4, loss_cols, ffn1, _ = _ffn_fwd(h3, u3, P["l1_ffn_post_norm"], P["l1_ffn_w_up"], P["l1_ffn_conv_w"],
                                       P["l1_ffn_conv_b"], P["l1_ffn_w_down"], "l1_ffn", loss_tgt=tgt_p)

    G = {}
    dh3, g = _ffn_bwd(dh4, ffn1, P["l1_ffn_pre_norm"], P["l1_ffn_post_norm"], P["l1_ffn_w_up"],
                      P["l1_ffn_conv_w"], P["l1_ffn_w_down"], "l1_ffn")
    for k, v in g.items():
        G["l1_ffn_" + (k + "_norm" if k in ("pre", "post") else k)] = v
    dyssd, dxskip, dz, dd_cols, G["l1_gate_norm"], do1, G["l1_mix_post_norm"] = _ssm_out_gate_bwd(
        o1, P["l1_mix_post_norm"], dh3, P["l1_w_out"], yssd, xbc, proj1, P["l1_d_skip"], P["l1_gate_norm"],
        "l1_out_dx_gate_bwd")
    G["l1_w_out"] = _mm_tn(yn, do1, "l1_out_dw")
    G["l1_d_skip"] = dd_cols.reshape(SSD_HEADS, SSD_P).sum(axis=1)
    (dxc, ddtr, dalog, dbias), outs = _ssd_bwd(
        xbc, dt, P["l1_a_log"], states, dyssd, xc1, proj1, P["l1_dt_bias"], dxskip, "l1_ssd_bwd",
        ex.scatter_plan(G, ex.AFTER_L1_OUT) if ex else None)
    if ex:
        ex.scattered(ex.AFTER_L1_OUT, outs)
    G["l1_a_log"] = dalog[0, :SSD_HEADS]
    G["l1_dt_bias"] = dbias[0, :SSD_HEADS]
    dxbc, dcw, dcb = _conv_bwd(proj1, _ZW, _XBC_W, dxc, P["l1_ssm_conv_w"], "l1_ssm_conv_bwd")
    G["l1_ssm_conv_w"] = dcw[:4]
    G["l1_ssm_conv_b"] = dcb
    dproj1 = jnp.concatenate([dz, dxbc, ddtr], axis=1)
    (dh2, G["l1_mix_pre_norm"]), _ = _mm_nt_sh(dproj1, P["l1_w_in"], (h2, P["l1_mix_pre_norm"], dh3),
                                               "l1_in_dx_prenorm_bwd")
    G["l1_w_in"] = _mm_tn_sh(u2, dproj1, ODD_IN // N_CHIPS, "l1_in_dw")
    dh1, g = _ffn_bwd(dh2, ffn0, P["l0_ffn_pre_norm"], P["l0_ffn_post_norm"], P["l0_ffn_w_up"],
                      P["l0_ffn_conv_w"], P["l0_ffn_w_down"], "l0_ffn")
    for k, v in g.items():
        G["l0_ffn_" + (k + "_norm" if k in ("pre", "post") else k)] = v
    dmix, do0, G["l0_mix_post_norm"] = _mm_nt_postnorm_bwd(o0, P["l0_mix_post_norm"], dh1, P["l0_w_out"],
                                                           "l0_out_dx")
    G["l0_w_out"] = jnp.concatenate([_mm_tn(ya, do0, "l0_out_dw_lru"), _mm_tn(yb, do0, "l0_out_dw_attn")], axis=0)
    if ex:
        lru_out, outs = _lru_bwd(proj0, xrc, hl, dmix, *lru_args, "l0_lru_bwd",
                                 ex.scatter_plan(G, ex.IN_LRU_BWD))
        ex.scattered(ex.IN_LRU_BWD, outs)
    else:
        lru_out = _lru_bwd(proj0, xrc, hl, dmix, *lru_args, "l0_lru_bwd")
    (dgate, dxrc, G["l0_lru_w_a"], G["l0_lru_w_x"], G["l0_lru_b_a"], G["l0_lru_b_x"],
     G["l0_lru_lambda"]) = lru_out
    dxr, dcw, dcb = _conv_bwd(proj0, D_MODEL, D_MODEL, dxrc, P["l0_lru_conv_w"], "l0_lru_conv_bwd")
    G["l0_lru_conv_w"] = dcw[:4]
    G["l0_lru_conv_b"] = dcb
    (dq, dk, dv, G["l0_attn_sinks"]), outs = _attn_bwd(
        proj0, P["l0_attn_sinks"], dmix, "l0_attn_bwd",
        ex.scatter_plan(G, ex.AFTER_L0_OUT) if ex else None)
    if ex:
        ex.scattered(ex.AFTER_L0_OUT, outs)
    dproj0 = jnp.concatenate([dgate, dxr, dq, dk.astype(BF16), dv.astype(BF16)], axis=1)
    G["l0_w_in"] = _mm_tn_sh(u0, dproj0, EVEN_IN // N_CHIPS, "l0_in_dw")
    (dh0, G["l0_mix_pre_norm"]), outs = _mm_nt_sh(
        dproj0, P["l0_w_in"], (h0, P["l0_mix_pre_norm"], dh1), "l0_in_dx_prenorm_bwd",
        ex.scatter_plan(G, ex.LAST) if ex else None)
    if ex:
        ex.scattered(ex.LAST, outs)
    return loss_cols, dh0[BLOCK:], dh0[PAD:BLOCK], G


_BIG_COL = ("l0_w_in", "l0_ffn_w_up", "l1_w_in", "l1_ffn_w_up")
_BIG = ("l0_w_in", "l0_w_out", "l0_ffn_w_up", "l0_ffn_w_down",
        "l1_w_in", "l1_w_out", "l1_ffn_w_up", "l1_ffn_w_down")
_SMALL_SHARDED = ("meta_tokens", "l0_lru_conv_w", "l0_ffn_conv_w", "l1_ssm_conv_w", "l1_ffn_conv_w")
_WEIGHTS = ("meta_tokens", "l0_mix_pre_norm", "l0_mix_post_norm", "l0_w_in", "l0_lru_conv_w",
            "l0_lru_conv_b", "l0_lru_w_a", "l0_lru_b_a", "l0_lru_w_x", "l0_lru_b_x", "l0_lru_lambda",
            "l0_attn_sinks", "l0_w_out", "l0_ffn_pre_norm", "l0_ffn_post_norm", "l0_ffn_w_up",
            "l0_ffn_conv_w", "l0_ffn_conv_b", "l0_ffn_w_down", "l1_mix_pre_norm", "l1_mix_post_norm",
            "l1_w_in", "l1_ssm_conv_w", "l1_ssm_conv_b", "l1_dt_bias", "l1_a_log", "l1_d_skip",
            "l1_gate_norm", "l1_w_out", "l1_ffn_pre_norm", "l1_ffn_post_norm", "l1_ffn_w_up",
            "l1_ffn_conv_w", "l1_ffn_conv_b", "l1_ffn_w_down")
_REPL = tuple(n for n in _WEIGHTS if n not in _BIG and n not in _SMALL_SHARDED)


def _pad_lanes(v, n=LANES):
    return jnp.concatenate([v, jnp.zeros((n - v.shape[0],), v.dtype)]).reshape(1, n)


def _step(x, tgt, W, M, V):
    cx, cy, cc = _place()
    chip = 2 * cx + cy

    small_pack, small_sizes = _pack([W[n] for n in _SMALL_SHARDED])
    first = _run_plan(_gather_plan([W["l0_w_in"].astype(BF16), small_pack]), "gather_first")
    small_full = _unpack(first[1], [W[n].shape for n in _SMALL_SHARDED], small_sizes, lead=(N_CHIPS,))
    ex = _Exchange({n: W[n].astype(BF16) for n in _BIG if n != "l0_w_in"})

    P = {"l0_w_in": first[0]}
    for n, g in zip(_SMALL_SHARDED, small_full):
        P[n] = _cols_from_shards(g)
    for n in _REPL:
        v = W[n]
        P[n] = v.reshape(1, -1) if v.ndim == 1 else v
    P["l0_lru_w_a"] = W["l0_lru_w_a"].astype(BF16)
    P["l0_lru_w_x"] = W["l0_lru_w_x"].astype(BF16)
    P["l1_dt_bias"] = _pad_lanes(W["l1_dt_bias"])
    P["l1_a_log"] = _pad_lanes(W["l1_a_log"])
    P["l1_d_skip"] = jnp.repeat(W["l1_d_skip"], SSD_P).reshape(1, D_SSM)
    meta = P.pop("meta_tokens")

    loss_cols, grad_x, grad_meta, G = _local_step(x, tgt, meta, P, ex)
    G["meta_tokens"] = grad_meta

    core_idx = cc.astype(jnp.int32).reshape(1)
    chip_idx = chip.astype(jnp.int32).reshape(1)
    own_half = [_add8(ex.slabs[n], ex.recv[n], chip_idx, core_idx, f"grad_sum_{n}") for n in _BIG]
    other_half = _run_plan(_sibling_plan(own_half), "grad_sibling_swap")
    small_names = list(_REPL) + list(_SMALL_SHARDED)
    small_list = [G[n] for n in small_names] + [loss_cols]
    spack, ssizes = _pack(small_list, total_mult=2 * SUBLANES * LANES)
    sred = _allreduce_small(spack, "small_allreduce")
    sfull = _unpack(sred, [a.shape for a in small_list], ssizes)
    loss = 0.5 / D_MODEL * jnp.sum(sfull[-1])
    small_grads = {}
    for n, g in zip(small_names, sfull[:-1]):
        if n in _SMALL_SHARDED:
            wcols = W[n].shape[1]
            g = lax.dynamic_slice_in_dim(g, chip * wcols, wcols, axis=1)
        small_grads[n] = g.reshape(W[n].shape)

    grads, delta, new_m, new_v = {}, {}, {}, {}
    for n, own, other in zip(_BIG, own_half, other_half):
        grads[n], delta[n], new_m[n], new_v[n] = _adamw_halves(
            W[n], own, other, M[n], V[n], core_idx, f"adamw_{n}")
    s_names = [n for n in _WEIGHTS if n not in _BIG]
    as2d = lambda a: a.reshape(_shape2d(a.shape))
    outs = _adamw_many([as2d(W[n]) for n in s_names], [as2d(small_grads[n]) for n in s_names],
                       [as2d(M[n]) for n in s_names], [as2d(V[n]) for n in s_names], "adamw_small")
    k = len(s_names)
    for i, n in enumerate(s_names):
        grads[n] = small_grads[n]
        delta[n], new_m[n], new_v[n] = (outs[j * k + i].reshape(W[n].shape) for j in range(3))
    return loss, grad_x, grads, delta, new_m, new_v


def kernel(x, meta_tokens, l0_mix_pre_norm, l0_mix_post_norm, l0_w_in, l0_lru_conv_w, l0_lru_conv_b, l0_lru_w_a, l0_lru_b_a, l0_lru_w_x, l0_lru_b_x, l0_lru_lambda, l0_attn_sinks, l0_w_out, l0_ffn_pre_norm, l0_ffn_post_norm, l0_ffn_w_up, l0_ffn_conv_w, l0_ffn_conv_b, l0_ffn_w_down, l1_mix_pre_norm, l1_mix_post_norm, l1_w_in, l1_ssm_conv_w, l1_ssm_conv_b, l1_dt_bias, l1_a_log, l1_d_skip, l1_gate_norm, l1_w_out, l1_ffn_pre_norm, l1_ffn_post_norm, l1_ffn_w_up, l1_ffn_conv_w, l1_ffn_conv_b, l1_ffn_w_down, loss_target, m_meta_tokens, m_l0_mix_pre_norm, m_l0_mix_post_norm, m_l0_w_in, m_l0_lru_conv_w, m_l0_lru_conv_b, m_l0_lru_w_a, m_l0_lru_b_a, m_l0_lru_w_x, m_l0_lru_b_x, m_l0_lru_lambda, m_l0_attn_sinks, m_l0_w_out, m_l0_ffn_pre_norm, m_l0_ffn_post_norm, m_l0_ffn_w_up, m_l0_ffn_conv_w, m_l0_ffn_conv_b, m_l0_ffn_w_down, m_l1_mix_pre_norm, m_l1_mix_post_norm, m_l1_w_in, m_l1_ssm_conv_w, m_l1_ssm_conv_b, m_l1_dt_bias, m_l1_a_log, m_l1_d_skip, m_l1_gate_norm, m_l1_w_out, m_l1_ffn_pre_norm, m_l1_ffn_post_norm, m_l1_ffn_w_up, m_l1_ffn_conv_w, m_l1_ffn_conv_b, m_l1_ffn_w_down, v_meta_tokens, v_l0_mix_pre_norm, v_l0_mix_post_norm, v_l0_w_in, v_l0_lru_conv_w, v_l0_lru_conv_b, v_l0_lru_w_a, v_l0_lru_b_a, v_l0_lru_w_x, v_l0_lru_b_x, v_l0_lru_lambda, v_l0_attn_sinks, v_l0_w_out, v_l0_ffn_pre_norm, v_l0_ffn_post_norm, v_l0_ffn_w_up, v_l0_ffn_conv_w, v_l0_ffn_conv_b, v_l0_ffn_w_down, v_l1_mix_pre_norm, v_l1_mix_post_norm, v_l1_w_in, v_l1_ssm_conv_w, v_l1_ssm_conv_b, v_l1_dt_bias, v_l1_a_log, v_l1_d_skip, v_l1_gate_norm, v_l1_w_out, v_l1_ffn_pre_norm, v_l1_ffn_post_norm, v_l1_ffn_w_up, v_l1_ffn_conv_w, v_l1_ffn_conv_b, v_l1_ffn_w_down):
    args = locals()
    W = {n: args[n] for n in _WEIGHTS}
    M = {n: args["m_" + n] for n in _WEIGHTS}
    V = {n: args["v_" + n] for n in _WEIGHTS}
    loss, grad_x, grads, delta, new_m, new_v = _step(x[0], loss_target[0], W, M, V)
    return (loss, grad_x[None], *[grads[n] for n in _WEIGHTS], *[delta[n] for n in _WEIGHTS],
            *[new_m[n] for n in _WEIGHTS], *[new_v[n] for n in _WEIGHTS])
```

```python
import functools
import math

import jax
import jax.numpy as jnp
from jax import lax
from jax.experimental import pallas as pl
from jax.experimental.pallas import tpu as pltpu

F32 = jnp.float32
BF16 = jnp.bfloat16

D_MODEL = 1024
N_META = 16
BLOCK = 128
PAD = BLOCK - N_META
EPS = 1e-6
LRU_BLOCKS = 8
LRU_BS = 128
LRU_C = 8.0
N_Q_HEADS = 16
N_KV_HEADS = 2
HEAD_DIM = 64
Q_PER_KV = 8
WINDOW = 128
D_SSM = 2048
SSD_HEADS = 32
SSD_GROUPS = 8
SSD_HPG = 4
SSD_P = 64
SSD_N = 128
D_FF = 2816
NEG = -1e30
LANES = 128
SUBLANES = 8
_VMEM_LIMIT_WIDE = 62 * 1024 * 1024

ADAM_LR = 0.001
ADAM_B1 = 0.9
ADAM_B2 = 0.999
ADAM_EPS = 1e-08
ADAM_WD = 0.01
ADAM_STEP = 10

MESH = pl.DeviceIdType.MESH
N_CHIPS = 4


def _pick(n, cands):
    for c in cands:
        if n % c == 0:
            return c
    raise ValueError(f"no tile for {n} in {cands}")


def _col_tile(n, limit=1792):
    best = None
    for t in range(LANES, min(n, limit) + 1, LANES):
        if n % t == 0:
            best = t
    if best is None:
        raise ValueError(f"no lane tile for {n}")
    return best


def _sigmoid(x):
    return 0.5 + 0.5 * jnp.tanh(0.5 * x)


def _log1p(e):
    u = 1.0 + e
    return jnp.where(u == 1.0, e, jnp.log(u) * (e / jnp.where(u == 1.0, 1.0, u - 1.0)))


def _softplus(x):
    return jnp.maximum(x, 0.0) + _log1p(jnp.exp(-jnp.abs(x)))


def _neg_expm1(x):
    poly = x * (1.0 + x * (0.5 + x * (1.0 / 6.0 + x * (1.0 / 24.0 + x * (1.0 / 120.0)))))
    return -jnp.where(x > -0.05, poly, jnp.exp(x) - 1.0)


_GELU_C = math.sqrt(2.0 / math.pi)


def _gelu(x):
    u = 0.5 + 0.5 * jnp.tanh(x * (_GELU_C + (_GELU_C * 0.044715) * (x * x)))
    return x * u


def _gelu_and_grad(x):
    x2 = x * x
    u = 0.5 + 0.5 * jnp.tanh(x * (_GELU_C + (_GELU_C * 0.044715) * x2))
    g = x * u
    dg = u * (1.0 + (x - g) * (2.0 * _GELU_C + (6.0 * 0.044715 * _GELU_C) * x2))
    return g, dg


def _silu_and_grad(x):
    s = _sigmoid(x)
    return x * s, s * (1.0 + x * (1.0 - s))


def _dot(a, b):
    return jnp.dot(a, b, preferred_element_type=F32)


def _dot_nt(a, b):
    return lax.dot_general(a, b, (((1,), (1,)), ((), ())), preferred_element_type=F32)


def _dot_tn(a, b):
    return lax.dot_general(a, b, (((0,), (0,)), ((), ())), preferred_element_type=F32)


def _row_iota(t):
    return lax.broadcasted_iota(jnp.int32, (t, 1), 0)


def _scan_fwd(a, u, t):
    row = _row_iota(t)
    d = 1
    while d < t:
        m = row >= d
        u_sh = jnp.where(m, pltpu.roll(u, d, 0), 0.0)
        a_sh = jnp.where(m, pltpu.roll(a, d, 0), 1.0)
        u = u + a * u_sh
        a = a * a_sh
        d *= 2
    return a, u


def _scan_rev(c, x, t):
    row = _row_iota(t)
    d = 1
    while d < t:
        m = row < t - d
        x_sh = jnp.where(m, pltpu.roll(x, t - d, 0), 0.0)
        c_sh = jnp.where(m, pltpu.roll(c, t - d, 0), 1.0)
        x = x + c * x_sh
        c = c * c_sh
        d *= 2
    return c, x


def _cumsum_rows(x, t):
    row = _row_iota(t)
    d = 1
    while d < t:
        x = x + jnp.where(row >= d, pltpu.roll(x, d, 0), 0.0)
        d *= 2
    return x


def _rev_cumsum_rows(x, t):
    row = _row_iota(t)
    d = 1
    while d < t:
        x = x + jnp.where(row < t - d, pltpu.roll(x, t - d, 0), 0.0)
        d *= 2
    return x


def _rms_bwd(x, g, dy):
    rs = lax.rsqrt(jnp.mean(x * x, axis=-1, keepdims=True) + EPS)
    gy = dy * g
    dx = rs * gy - x * (rs * rs * rs) * jnp.mean(x * gy, axis=-1, keepdims=True)
    return dx, dy * x * rs


def _mm_nt_postnorm_bwd(o, gain, dh, w, name):
    m, d = o.shape
    k = w.shape[0]
    tm = _pick(m, (640, 512, 256, 128))

    def body(o_ref, gain_ref, dh_ref, w_ref, y_ref, do_ref, dgain_ref):
        y_ref[...] = _dot_nt(_postnorm_bwd_tile(o_ref, gain_ref, dh_ref, do_ref, dgain_ref), w_ref[...])

    row = pl.BlockSpec((tm, d), lambda i: (i, 0))
    vec = pl.BlockSpec((1, d), lambda i: (0, 0))
    return pl.pallas_call(
        body, name=name, grid=(m // tm,),
        in_specs=[row, vec, row, pl.BlockSpec((k, d), lambda i: (0, 0))],
        out_specs=[pl.BlockSpec((tm, k), lambda i: (i, 0)), row, vec],
        out_shape=[jax.ShapeDtypeStruct((m, k), F32), jax.ShapeDtypeStruct((m, d), BF16),
                   jax.ShapeDtypeStruct((1, d), F32)],
        compiler_params=pltpu.CompilerParams(dimension_semantics=("arbitrary",)),
    )(o, gain, dh, w)


def _mm_tn(a, dy, name):
    m, k = a.shape
    n = dy.shape[1]
    tm = _pick(m, (640, 512, 256, 128))
    tk = _col_tile(k, 1408)
    tn = _col_tile(n, 1664)
    nsteps = m // tm

    def body(a_ref, dy_ref, o_ref, acc):
        @pl.when(pl.program_id(2) == 0)
        def _():
            acc[...] = jnp.zeros_like(acc)

        acc[...] += _dot_tn(a_ref[...].astype(BF16), dy_ref[...].astype(BF16))

        @pl.when(pl.program_id(2) == nsteps - 1)
        def _():
            o_ref[...] = acc[...].astype(o_ref.dtype)

    return pl.pallas_call(
        body, name=name, grid=(k // tk, n // tn, nsteps),
        in_specs=[pl.BlockSpec((tm, tk), lambda kk, j, i: (i, kk)),
                  pl.BlockSpec((tm, tn), lambda kk, j, i: (i, j))],
        out_specs=pl.BlockSpec((tk, tn), lambda kk, j, i: (kk, j)),
        out_shape=jax.ShapeDtypeStruct((k, n), BF16),
        scratch_shapes=[pltpu.VMEM((tk, tn), F32)],
        compiler_params=pltpu.CompilerParams(
            dimension_semantics=("parallel", "parallel", "arbitrary")),
    )(a, dy)


def _mm_nn_sh(a, w4, n_out, name, conv=None):
    m, k = a.shape
    s, _, n = w4.shape
    tm = _pick(m, (320, 256, 128))
    if conv is not None:
        c0, cw_, taps, bias = conv
        kk = taps.shape[0]

    def body(*refs):
        a_ref, w_ref = refs[:2]
        av = a_ref[...].astype(BF16)
        o_ref = refs[4] if conv is not None else refs[2]
        for j in range(s):
            o_ref[:, j * n:(j + 1) * n] = _dot(av, w_ref[j])
        if n_out > s * n:
            o_ref[:, s * n:] = jnp.zeros((tm, n_out - s * n), F32)
        if conv is not None:
            t_ref, b_ref, y_ref, carry = refs[2], refs[3], refs[5], refs[6]

            @pl.when(pl.program_id(0) == 0)
            def _():
                carry[...] = jnp.zeros_like(carry)

            xv = o_ref[:, c0:c0 + cw_]
            xx = jnp.concatenate([carry[...], xv], axis=0)
            acc = b_ref[...] + t_ref[kk - 1:kk, :] * xv
            for mm in range(1, kk):
                acc = acc + t_ref[kk - 1 - mm:kk - mm, :] * pltpu.roll(xx, mm, 0)[SUBLANES:, :]
            y_ref[...] = acc
            carry[...] = xv[tm - SUBLANES:, :]

    row_a = pl.BlockSpec((tm, k), lambda i: (i, 0))
    w_spec = pl.BlockSpec((s, k, n), lambda i: (0, 0, 0))
    o_spec = pl.BlockSpec((tm, n_out), lambda i: (i, 0))
    if conv is None:
        return pl.pallas_call(
            body, name=name, grid=(m // tm,), in_specs=[row_a, w_spec], out_specs=o_spec,
            out_shape=jax.ShapeDtypeStruct((m, n_out), F32),
            compiler_params=pltpu.CompilerParams(dimension_semantics=("parallel",)),
        )(a, w4)
    return pl.pallas_call(
        body, name=name, grid=(m // tm,),
        in_specs=[row_a, w_spec, pl.BlockSpec(taps.shape, lambda i: (0, 0)),
                  pl.BlockSpec(bias.shape, lambda i: (0, 0))],
        out_specs=[o_spec, pl.BlockSpec((tm, cw_), lambda i: (i, 0))],
        out_shape=[jax.ShapeDtypeStruct((m, n_out), F32), jax.ShapeDtypeStruct((m, cw_), F32)],
        scratch_shapes=[pltpu.VMEM((SUBLANES, cw_), F32)],
        compiler_params=pltpu.CompilerParams(dimension_semantics=("arbitrary",)),
    )(a, w4, taps, bias)


def _mm_nt_sh(dy, w4, norm, name, plan=None):
    dys = dy if isinstance(dy, (tuple, list)) else (dy,)
    h, g, dres = norm
    m = dys[0].shape[0]
    s, k, n = w4.shape
    tm = _pick(m, (640, 512, 256, 128))
    where = _shard_columns(dys, s, n)
    p_in, p_shapes, p_out, p_scr = _plan_parts(plan)
    nd, ni = len(dys), m // tm

    def body(*refs):
        w_ref, h_ref, g_ref, dres_ref = refs[nd:nd + 4]
        cins = refs[nd + 4:nd + 4 + len(p_in)]
        dh_ref, dg_ref = refs[nd + 4 + len(p_in):nd + 6 + len(p_in)]
        couts = refs[nd + 6 + len(p_in):nd + 6 + len(p_in) + len(p_out)]
        sems = refs[nd + 6 + len(p_in) + len(p_out):]
        i = pl.program_id(0)
        if plan is not None:
            @pl.when(i == 0)
            def _():
                plan.start(cins, couts, sems)

        du = None
        for j, (p, c0) in enumerate(where):
            t = _dot_nt(refs[p][:, c0:c0 + n].astype(BF16), w_ref[j])
            du = t if du is None else du + t
        dx, dgt = _rms_bwd(h_ref[...], g_ref[...], du)
        dh_ref[...] = jnp.where(_real_rows(i, tm), dres_ref[...] + dx, 0.0)
        _acc_add(i == 0, dg_ref, jnp.sum(dgt, axis=0, keepdims=True))
        if plan is not None:
            @pl.when(i == ni - 1)
            def _():
                plan.wait(cins, couts, sems)

    row = pl.BlockSpec((tm, k), lambda i: (i, 0))
    vec = pl.BlockSpec((1, k), lambda i: (0, 0))
    res = pl.pallas_call(
        body, name=name, grid=(ni,),
        in_specs=[pl.BlockSpec((tm, d.shape[1]), lambda i: (i, 0)) for d in dys]
        + [pl.BlockSpec((s, k, n), lambda i: (0, 0, 0)), row, vec, row] + p_in,
        out_specs=[row, vec] + p_out,
        out_shape=[jax.ShapeDtypeStruct((m, k), F32), jax.ShapeDtypeStruct((1, k), F32)] + p_shapes,
        scratch_shapes=p_scr,
        compiler_params=pltpu.CompilerParams(dimension_semantics=("arbitrary",),
                                             vmem_limit_bytes=_VMEM_LIMIT_WIDE),
    )(*dys, w4, h, g, dres, *(plan.ins if plan is not None else []))
    return (res[0], res[1]), res[2:]


def _shard_columns(dys, s, n):
    where = []
    for p, d in enumerate(dys):
        where += [(p, c * n) for c in range(d.shape[1] // n)]
    assert len(where) >= s
    return where[:s]


def _mm_tn_sh(a, dy, n, name):
    dys = dy if isinstance(dy, (tuple, list)) else (dy,)
    m, k = a.shape
    s = N_CHIPS
    tm = _pick(m, (640, 512, 256, 128))
    tk = _col_tile(k, 512)
    nsteps = m // tm
    where = _shard_columns(dys, s, n)

    def body(*refs):
        a_ref, o_ref, acc = refs[0], refs[len(dys) + 1], refs[len(dys) + 2]

        @pl.when(pl.program_id(1) == 0)
        def _():
            acc[...] = jnp.zeros_like(acc)

        av = a_ref[...].astype(BF16)
        for j, (p, c0) in enumerate(where):
            acc[j] += _dot_tn(av, refs[1 + p][:, c0:c0 + n].astype(BF16))

        @pl.when(pl.program_id(1) == nsteps - 1)
        def _():
            o_ref[...] = acc[...].astype(o_ref.dtype)

    return pl.pallas_call(
        body, name=name, grid=(k // tk, nsteps),
        in_specs=[pl.BlockSpec((tm, tk), lambda kk, i: (i, kk))]
        + [pl.BlockSpec((tm, d.shape[1]), lambda kk, i: (i, 0)) for d in dys],
        out_specs=pl.BlockSpec((s, tk, n), lambda kk, i: (0, kk, 0)),
        out_shape=jax.ShapeDtypeStruct((s, k, n), BF16),
        scratch_shapes=[pltpu.VMEM((s, tk, n), F32)],
        compiler_params=pltpu.CompilerParams(dimension_semantics=("parallel", "arbitrary"),
                                             vmem_limit_bytes=_VMEM_LIMIT_WIDE),
    )(a, *dys)


def _rowcall(name, body, lp, tm, rows=(), prevs=(), vecs=(), outs=(), accs=(), scratch=(),
             reverse=False, seq=False, plan=None):
    p_in, p_shapes, p_out, p_scr = _plan_parts(plan)
    nt = lp // tm
    hb = tm // SUBLANES

    def ri(i):
        return nt - 1 - i if reverse else i

    in_specs, args = [], []
    for arr, w, cb in rows:
        in_specs.append(pl.BlockSpec((tm, w), lambda i, cb=cb: (ri(i), cb)))
        args.append(arr)
    for arr, w, cb in prevs:
        in_specs.append(pl.BlockSpec((SUBLANES, w), lambda i, cb=cb: (jnp.maximum(ri(i) * hb - 1, 0), cb)))
        args.append(arr)
    for arr in vecs:
        in_specs.append(pl.BlockSpec(arr.shape, lambda i, nd=arr.ndim: (0,) * nd))
        args.append(arr)
    out_shape, out_specs = [], []
    for w, dt in outs:
        out_shape.append(jax.ShapeDtypeStruct((lp, w), dt))
        out_specs.append(pl.BlockSpec((tm, w), lambda i: (ri(i), 0)))
    for shp, dt in accs:
        out_shape.append(jax.ShapeDtypeStruct(shp, dt))
        out_specs.append(pl.BlockSpec(shp, lambda i, nd=len(shp): (0,) * nd))

    n_in, n_out, n_scr = len(args), len(out_shape), len(scratch)

    def kern(*refs):
        i = pl.program_id(0)
        own = (refs[:n_in] + refs[n_in + len(p_in):n_in + len(p_in) + n_out]
               + refs[n_in + len(p_in) + n_out + len(p_out):n_in + len(p_in) + n_out + len(p_out) + n_scr])
        cins = refs[n_in:n_in + len(p_in)]
        couts = refs[n_in + len(p_in) + n_out:n_in + len(p_in) + n_out + len(p_out)]
        sems = refs[n_in + len(p_in) + n_out + len(p_out) + n_scr:]
        if plan is not None:
            @pl.when(i == 0)
            def _():
                plan.start(cins, couts, sems)

        body(ri(i), i == 0, *own)
        if plan is not None:
            @pl.when(i == nt - 1)
            def _():
                plan.wait(cins, couts, sems)

    sem = ("arbitrary",) if (seq or accs or plan is not None) else ("parallel",)
    res = pl.pallas_call(
        kern, name=name, grid=(nt,), in_specs=in_specs + p_in, out_specs=out_specs + p_out,
        out_shape=out_shape + p_shapes, scratch_shapes=list(scratch) + p_scr,
        compiler_params=pltpu.CompilerParams(dimension_semantics=sem),
    )(*args, *(plan.ins if plan is not None else []))
    return res if plan is None else (res[:n_out], res[n_out:])


def _acc_add(first, ref, val):
    @pl.when(first)
    def _():
        ref[...] = jnp.zeros_like(ref)

    ref[...] += val


def _real_rows(r, tm):
    return (r * tm + _row_iota(tm)) >= PAD


def _rmsnorm_fwd(h, g, name):
    lp, d = h.shape
    tm = _pick(lp, (640, 512, 256, 128))

    def body(r, first, h_ref, g_ref, u_ref):
        x = h_ref[...]
        rs = lax.rsqrt(jnp.mean(x * x, axis=-1, keepdims=True) + EPS)
        u_ref[...] = (x * rs * g_ref[...]).astype(u_ref.dtype)

    return _rowcall(name, body, lp, tm, rows=[(h, d, 0)], vecs=[g], outs=[(d, BF16)])[0]


def _mm_postnorm_res(a, w, h, g, name, next_pre=None):
    parts = a if isinstance(a, (tuple, list)) else (a,)
    lp, d = h.shape
    k = w.shape[0]
    tm = _pick(lp, (640, 512, 256, 128))
    offs = [sum(p.shape[1] for p in parts[:i]) for i in range(len(parts))]
    np_ = len(parts)

    def body(*refs):
        w_ref, h_ref, g_ref = refs[np_], refs[np_ + 1], refs[np_ + 2]
        rest = refs[np_ + 3:]
        acc = None
        for a_ref, p, off in zip(refs, parts, offs):
            t = _dot(a_ref[...].astype(BF16), w_ref[off:off + p.shape[1], :])
            acc = t if acc is None else acc + t
        outs = rest[1:] if next_pre is not None else rest
        outs[0][...] = acc
        rs = lax.rsqrt(jnp.mean(acc * acc, axis=-1, keepdims=True) + EPS)
        hn = jnp.where(_real_rows(pl.program_id(0), tm), h_ref[...] + acc * rs * g_ref[...], 0.0)
        outs[1][...] = hn
        if next_pre is not None:
            rs2 = lax.rsqrt(jnp.mean(hn * hn, axis=-1, keepdims=True) + EPS)
            outs[2][...] = (hn * rs2 * rest[0][...]).astype(BF16)

    row = pl.BlockSpec((tm, d), lambda i: (i, 0))
    vec = pl.BlockSpec((1, d), lambda i: (0, 0))
    n_vec = 2 if next_pre is not None else 1
    return pl.pallas_call(
        body, name=name, grid=(lp // tm,),
        in_specs=[pl.BlockSpec((tm, p.shape[1]), lambda i: (i, 0)) for p in parts]
        + [pl.BlockSpec((k, d), lambda i: (0, 0)), row] + [vec] * n_vec,
        out_specs=[row] * (2 + (next_pre is not None)),
        out_shape=[jax.ShapeDtypeStruct((lp, d), F32)] * 2
        + ([jax.ShapeDtypeStruct((lp, d), BF16)] if next_pre is not None else []),
        compiler_params=pltpu.CompilerParams(dimension_semantics=("parallel",)),
    )(*parts, w, h, g, *([next_pre] if next_pre is not None else []))


def _mm_postnorm_loss(a, w, h, g, tgt, name):
    lp, d = h.shape
    k = w.shape[0]
    tm = _pick(lp, (640, 512, 256, 128))

    def body(a_ref, w_ref, h_ref, g_ref, t_ref, o_ref, dh_ref, ls_ref):
        i = pl.program_id(0)
        acc = _dot(a_ref[...].astype(BF16), w_ref[...])
        o_ref[...] = acc
        rs = lax.rsqrt(jnp.mean(acc * acc, axis=-1, keepdims=True) + EPS)
        tok = (i * tm + _row_iota(tm)) >= BLOCK
        e = jnp.where(tok, h_ref[...] + acc * rs * g_ref[...] - t_ref[...], 0.0)
        dh_ref[...] = e * (1.0 / d)
        _acc_add(i == 0, ls_ref, jnp.sum(e * e, axis=0, keepdims=True))

    row = pl.BlockSpec((tm, d), lambda i: (i, 0))
    vec = pl.BlockSpec((1, d), lambda i: (0, 0))
    return pl.pallas_call(
        body, name=name, grid=(lp // tm,),
        in_specs=[pl.BlockSpec((tm, k), lambda i: (i, 0)), pl.BlockSpec((k, d), lambda i: (0, 0)),
                  row, vec, row],
        out_specs=[row, row, vec],
        out_shape=[jax.ShapeDtypeStruct((lp, d), F32), jax.ShapeDtypeStruct((lp, d), F32),
                   jax.ShapeDtypeStruct((1, d), F32)],
        compiler_params=pltpu.CompilerParams(dimension_semantics=("arbitrary",)),
    )(a, w, h, g, tgt)


def _conv_tiles(lp, width):
    wc = _col_tile(width, 1408)
    tm = _pick(lp, (320, 256, 128))
    return tm, wc


def _conv_bwd(x, col_off, width, dy, w, name, w_col_off=0):
    lp = x.shape[0]
    kk = w.shape[0]
    tm, wc = _conv_tiles(lp, width)
    offb = col_off // wc
    woffb = w_col_off // wc
    assert col_off % wc == 0 and w_col_off % wc == 0
    hrows = SUBLANES * (4 // dy.dtype.itemsize)
    ext = tm + hrows

    def body(x_ref, dy_ref, dn_ref, w_ref, dx_ref, dw_ref, db_ref):
        i = pl.program_id(1)
        last = pl.num_programs(1) - 1
        xv = x_ref[...]
        dyv = dy_ref[...].astype(F32)
        dd = jnp.concatenate([dyv, jnp.where(i < last, dn_ref[...].astype(F32), 0.0)], axis=0)
        dx = w_ref[kk - 1:kk, :] * dyv
        rows = [jnp.sum(dyv * xv, axis=0, keepdims=True)]
        for m in range(1, kk):
            ahead = pltpu.roll(dd, ext - m, 0)[:tm, :]
            dx = dx + w_ref[kk - 1 - m:kk - m, :] * ahead
            rows.append(jnp.sum(ahead * xv, axis=0, keepdims=True))
        dx_ref[...] = dx.astype(dx_ref.dtype)
        dwp = jnp.concatenate(rows[::-1] + [jnp.zeros((SUBLANES - kk, wc), F32)], axis=0)

        @pl.when(i == 0)
        def _():
            dw_ref[...] = jnp.zeros_like(dw_ref)
            db_ref[...] = jnp.zeros_like(db_ref)

        dw_ref[...] += dwp
        db_ref[...] += jnp.sum(dyv, axis=0, keepdims=True)

    return pl.pallas_call(
        body, name=name, grid=(width // wc, lp // tm),
        in_specs=[pl.BlockSpec((tm, wc), lambda j, i: (i, offb + j)),
                  pl.BlockSpec((tm, wc), lambda j, i: (i, j)),
                  pl.BlockSpec((hrows, wc), lambda j, i: (jnp.minimum((i + 1) * (tm // hrows), lp // hrows - 1), j)),
                  pl.BlockSpec((kk, wc), lambda j, i: (0, woffb + j))],
        out_specs=[pl.BlockSpec((tm, wc), lambda j, i: (i, j)),
                   pl.BlockSpec((SUBLANES, wc), lambda j, i: (0, j)),
                   pl.BlockSpec((1, wc), lambda j, i: (0, j))],
        out_shape=[jax.ShapeDtypeStruct((lp, width), BF16),
                   jax.ShapeDtypeStruct((SUBLANES, width), F32),
                   jax.ShapeDtypeStruct((1, width), F32)],
        compiler_params=pltpu.CompilerParams(dimension_semantics=("parallel", "arbitrary")),
    )(x, dy, dy, w)


_FFN_WC = 1408


def _ffn_up_convact(u, w4, cw, cb, name, plan=None):
    lp, k = u.shape
    s, _, n = w4.shape
    tm = _pick(lp, (256, 128))
    ni = lp // tm
    p_in, p_shapes, p_out, p_scr = _plan_parts(plan)

    def body(*refs):
        u_ref, w_ref, cw_ref, cb_ref = refs[:4]
        cins = refs[4:4 + len(p_in)]
        hp_ref, a_ref, hg_ref, hu_ref = refs[4 + len(p_in):8 + len(p_in)]
        couts = refs[8 + len(p_in):8 + len(p_in) + len(p_out)]
        carry = refs[8 + len(p_in) + len(p_out)]
        sems = refs[9 + len(p_in) + len(p_out):]
        i = pl.program_id(0)

        @pl.when(i == 0)
        def _():
            carry[...] = jnp.zeros_like(carry)
            if plan is not None:
                plan.start(cins, couts, sems)

        uv = u_ref[...].astype(BF16)
        for j in range(s):
            hp_ref[:, j * n:(j + 1) * n] = _dot(uv, w_ref[j])

        def conv(cols):
            x_ext = jnp.concatenate([carry[:, cols], hp_ref[:, cols]], axis=0)
            w = cw_ref[:, cols]
            y = (cb_ref[:, cols] + w[2:3, :] * x_ext + w[1:2, :] * pltpu.roll(x_ext, 1, 0)
                 + w[0:1, :] * pltpu.roll(x_ext, 2, 0))
            return y[SUBLANES:, :]

        for c in range(D_FF // n):
            gs = slice(c * n, (c + 1) * n)
            hg = conv(gs)
            hu = conv(slice(D_FF + c * n, D_FF + (c + 1) * n))
            a_ref[:, gs] = (_gelu(hg) * hu).astype(a_ref.dtype)
            hg_ref[:, gs] = hg.astype(hg_ref.dtype)
            hu_ref[:, gs] = hu.astype(hu_ref.dtype)
        carry[...] = hp_ref[tm - SUBLANES:tm, :]
        if plan is not None:
            @pl.when(i == ni - 1)
            def _():
                plan.wait(cins, couts, sems)

    half = pl.BlockSpec((tm, D_FF), lambda i: (i, 0))
    res = pl.pallas_call(
        body, name=name, grid=(ni,),
        in_specs=[pl.BlockSpec((tm, k), lambda i: (i, 0)), pl.BlockSpec((s, k, n), lambda i: (0, 0, 0)),
                  pl.BlockSpec(cw.shape, lambda i: (0, 0)), pl.BlockSpec(cb.shape, lambda i: (0, 0))] + p_in,
        out_specs=[pl.BlockSpec((tm, 2 * D_FF), lambda i: (i, 0)), half, half, half] + p_out,
        out_shape=[jax.ShapeDtypeStruct((lp, 2 * D_FF), F32)] + [jax.ShapeDtypeStruct((lp, D_FF), BF16)] * 3
        + p_shapes,
        scratch_shapes=[pltpu.VMEM((SUBLANES, 2 * D_FF), F32)] + p_scr,
        compiler_params=pltpu.CompilerParams(dimension_semantics=("arbitrary",),
                                             vmem_limit_bytes=_VMEM_LIMIT_WIDE),
    )(u, w4, cw, cb, *(plan.ins if plan is not None else []))
    return res[:4], res[4:]


def _postnorm_bwd_tile(o_ref, gain_ref, dh_ref, do_ref, dgain_ref):
    first = pl.program_id(0) == 0
    dx, dgt = _rms_bwd(o_ref[...], gain_ref[...], dh_ref[...])
    dob = dx.astype(BF16)
    do_ref[...] = dob
    _acc_add(first, dgain_ref, jnp.sum(dgt, axis=0, keepdims=True))
    return dob


def _ffn_down_act_bwd(o, gain, dh, w_down, hg, hu, name):
    lp, d = o.shape
    tm = _pick(lp, (320, 256, 128))
    tk = _FFN_WC

    def body(o_ref, gain_ref, dh_ref, w_ref, g_ref, u_ref, dg_ref, du_ref, do_ref, dgain_ref):
        dob = _postnorm_bwd_tile(o_ref, gain_ref, dh_ref, do_ref, dgain_ref)
        for j in range(D_FF // tk):
            cs = slice(j * tk, (j + 1) * tk)
            da = _dot_nt(dob, w_ref[cs, :])
            gl, dgl = _gelu_and_grad(g_ref[:, cs].astype(F32))
            dg_ref[:, cs] = (da * u_ref[:, cs].astype(F32) * dgl).astype(dg_ref.dtype)
            du_ref[:, cs] = (da * gl).astype(du_ref.dtype)

    wide = pl.BlockSpec((tm, D_FF), lambda i: (i, 0))
    row = pl.BlockSpec((tm, d), lambda i: (i, 0))
    vec = pl.BlockSpec((1, d), lambda i: (0, 0))
    return pl.pallas_call(
        body, name=name, grid=(lp // tm,),
        in_specs=[row, vec, row, pl.BlockSpec((D_FF, d), lambda i: (0, 0)), wide, wide],
        out_specs=[wide, wide, row, vec],
        out_shape=[jax.ShapeDtypeStruct((lp, D_FF), BF16)] * 2
        + [jax.ShapeDtypeStruct((lp, d), BF16), jax.ShapeDtypeStruct((1, d), F32)],
        compiler_params=pltpu.CompilerParams(dimension_semantics=("arbitrary",),
                                             vmem_limit_bytes=_VMEM_LIMIT_WIDE),
    )(o, gain, dh, w_down, hg, hu)


def _lru_gates(x, wa_ref, wx_ref, ba, bx, lam):
    xb = x.astype(BF16)
    za, zx = [], []
    for n in range(LRU_BLOCKS):
        xs = xb[:, n * LRU_BS:(n + 1) * LRU_BS]
        za.append(_dot(xs, wa_ref[n]))
        zx.append(_dot(xs, wx_ref[n]))
    r = _sigmoid(jnp.concatenate(za, axis=1) + ba)
    ig = _sigmoid(jnp.concatenate(zx, axis=1) + bx)
    sp = _softplus(-lam)
    log_a = -LRU_C * r * sp
    a = jnp.exp(log_a)
    om = _neg_expm1(2.0 * log_a)
    mult = jnp.sqrt(om)
    return xb, r, ig, sp, a, om, mult


def _lru_fwd(proj, xrc, wa, wx, ba, bx, lam, name, plan=None):
    lp, d = xrc.shape
    tm = BLOCK

    def body(r_idx, first, gate_ref, x_ref, wa_ref, wx_ref, ba_ref, bx_ref, lam_ref,
             y_ref, h_ref, carry):
        @pl.when(first)
        def _():
            carry[...] = jnp.zeros_like(carry)

        x = x_ref[...]
        _, _, ig, _, a, _, mult = _lru_gates(x, wa_ref, wx_ref, ba_ref[...], bx_ref[...], lam_ref[...])
        u = jnp.where(_real_rows(r_idx, tm), mult * ig * x, 0.0)
        acum, hloc = _scan_fwd(a, u, tm)
        h = hloc + acum * carry[0:1, :]
        h_ref[...] = h
        carry[0:1, :] = h[tm - 1:tm, :]
        y_ref[...] = (_gelu(gate_ref[...]) * h).astype(y_ref.dtype)

    return _rowcall(name, body, lp, tm, rows=[(proj, d, 0), (xrc, d, 0)],
                    vecs=[wa, wx, ba, bx, lam], outs=[(d, BF16), (d, F32)],
                    scratch=[pltpu.VMEM((SUBLANES, d), F32)], seq=True, plan=plan)


def _lru_bwd(proj, xrc, hl, dmix, wa, wx, ba, bx, lam, name, plan=None):
    lp, d = xrc.shape
    tm = BLOCK

    def body(r_idx, first, gate_ref, x_ref, h_ref, dy_ref, hp_ref, wa_ref, wx_ref, ba_ref, bx_ref,
             lam_ref, dgate_ref, dx_ref, dwa_ref, dwx_ref, dba_ref, dbx_ref, dlam_ref, carry):
        @pl.when(first)
        def _():
            carry[...] = jnp.zeros_like(carry)
            dwa_ref[...] = jnp.zeros_like(dwa_ref)
            dwx_ref[...] = jnp.zeros_like(dwx_ref)
            dba_ref[...] = jnp.zeros_like(dba_ref)
            dbx_ref[...] = jnp.zeros_like(dbx_ref)
            dlam_ref[...] = jnp.zeros_like(dlam_ref)

        x = x_ref[...]
        lam = lam_ref[...]
        xb, r, ig, sp, a, om, mult = _lru_gates(x, wa_ref, wx_ref, ba_ref[...], bx_ref[...], lam)
        h = h_ref[...]
        dy = dy_ref[...]
        gl, dgl = _gelu_and_grad(gate_ref[...])
        dgate_ref[...] = (dy * h * dgl).astype(dgate_ref.dtype)
        row = _row_iota(tm)
        lastrow = row == tm - 1
        xg = dy * gl + jnp.where(lastrow, carry[0:1, :], 0.0)
        c = jnp.where(lastrow, 1.0, pltpu.roll(a, tm - 1, 0))
        _, g = _scan_rev(c, xg, tm)
        carry[0:1, :] = a[0:1, :] * g[0:1, :]
        hprev_in = jnp.where(r_idx > 0, hp_ref[SUBLANES - 1:SUBLANES, :], 0.0)
        hprev = jnp.where(row == 0, hprev_in, pltpu.roll(h, 1, 0))
        du = jnp.where(_real_rows(r_idx, tm), g, 0.0)
        da = g * hprev
        dmult = du * ig * x
        dig = du * mult * x
        dxv = du * mult * ig
        e2 = 1.0 - om
        dlog_a = da * a - dmult * e2 / mult
        dr = dlog_a * (-LRU_C) * sp
        dsp = jnp.sum(dlog_a * (-LRU_C) * r, axis=0, keepdims=True)
        dlam_ref[...] += -dsp * _sigmoid(-lam)
        dza = dr * r * (1.0 - r)
        dzx = dig * ig * (1.0 - ig)
        dba_ref[...] += jnp.sum(dza, axis=0, keepdims=True)
        dbx_ref[...] += jnp.sum(dzx, axis=0, keepdims=True)
        dzab = dza.astype(BF16)
        dzxb = dzx.astype(BF16)
        parts = []
        for n in range(LRU_BLOCKS):
            sl = slice(n * LRU_BS, (n + 1) * LRU_BS)
            dwa_ref[n] += _dot_tn(xb[:, sl], dzab[:, sl])
            dwx_ref[n] += _dot_tn(xb[:, sl], dzxb[:, sl])
            parts.append(_dot_nt(dzab[:, sl], wa_ref[n]) + _dot_nt(dzxb[:, sl], wx_ref[n]))
        dx_ref[...] = dxv + jnp.concatenate(parts, axis=1)

    return _rowcall(name, body, lp, tm,
                    rows=[(proj, d, 0), (xrc, d, 0), (hl, d, 0), (dmix, d, 0)],
                    prevs=[(hl, d, 0)], vecs=[wa, wx, ba, bx, lam],
                    outs=[(d, BF16), (d, F32)],
                    accs=[((LRU_BLOCKS, LRU_BS, LRU_BS), F32), ((LRU_BLOCKS, LRU_BS, LRU_BS), F32),
                          ((1, d), F32), ((1, d), F32), ((1, d), F32)],
                    scratch=[pltpu.VMEM((SUBLANES, d), F32)], reverse=True, seq=True, plan=plan)


_SLOPES = [2.0 ** (-8.0 * (h + 1) / N_Q_HEADS) for h in range(N_Q_HEADS)]
_QK_SCALE = HEAD_DIM ** -0.5
_QCOL = 2 * D_MODEL // D_MODEL
_KCOL = (3 * D_MODEL) // LANES
_VCOL = _KCOL + 1


def _attn_masks(n):
    start = pl.multiple_of(jnp.maximum(n - 1, 0) * BLOCK, BLOCK)
    qi = n * BLOCK + lax.broadcasted_iota(jnp.int32, (BLOCK, 2 * BLOCK), 0)
    kj = start + lax.broadcasted_iota(jnp.int32, (BLOCK, 2 * BLOCK), 1)
    dist = qi - kj
    ok = (kj >= BLOCK) & (dist >= 0) & (dist < WINDOW)
    dm = (n * BLOCK - PAD + lax.broadcasted_iota(jnp.int32, (BLOCK, N_META), 0)
          - lax.broadcasted_iota(jnp.int32, (BLOCK, N_META), 1))
    okm = dm >= 0
    return start, ok, dist.astype(F32), okm, jnp.minimum(dm, WINDOW).astype(F32)


def _group_rows(ref, g, scale=None):
    x = jnp.concatenate(
        [ref[:, (g * Q_PER_KV + hh) * HEAD_DIM:(g * Q_PER_KV + hh + 1) * HEAD_DIM] for hh in range(Q_PER_KV)],
        axis=0)
    return (x if scale is None else x * scale).astype(BF16)


def _attn_probs(s, sm, sink_ref, g, ok, distf, okm, dmf):
    slope = jnp.stack([jnp.full((1, 1), _SLOPES[g * Q_PER_KV + hh], F32) for hh in range(Q_PER_KV)])
    sink = jnp.stack([sink_ref[0:1, g * Q_PER_KV + hh:g * Q_PER_KV + hh + 1] for hh in range(Q_PER_KV)])
    s = s.reshape(Q_PER_KV, BLOCK, 2 * BLOCK)
    sm = sm.reshape(Q_PER_KV, BLOCK, N_META)
    s = jnp.where(ok[None], s - slope * distf[None], NEG)
    sm = jnp.where(okm[None], sm - slope * dmf[None], NEG)
    mx = jnp.maximum(jnp.maximum(jnp.max(s, axis=-1, keepdims=True),
                                 jnp.max(sm, axis=-1, keepdims=True)), sink)
    p = jnp.exp(s - mx)
    pm = jnp.exp(sm - mx)
    ps = jnp.exp(sink - mx)
    inv = 1.0 / (jnp.sum(p, axis=-1, keepdims=True) + jnp.sum(pm, axis=-1, keepdims=True) + ps)
    return p, pm, ps, inv


def _attn_fwd(proj, sinks, name, plan=None):
    lp = proj.shape[0]
    nblk = lp // BLOCK
    p_in, p_shapes, p_out, p_scr = _plan_parts(plan)

    def body(*refs):
        q_ref, k_ref, v_ref, sink_ref = refs[:4]
        cins = refs[4:4 + len(p_in)]
        o_ref = refs[4 + len(p_in)]
        couts = refs[5 + len(p_in):5 + len(p_in) + len(p_out)]
        sems = refs[5 + len(p_in) + len(p_out):]
        n = pl.program_id(0)
        if plan is not None:
            @pl.when(n == 0)
            def _():
                plan.start(cins, couts, sems)

        start, ok, distf, okm, dmf = _attn_masks(n)
        kb = k_ref[pl.ds(start, 2 * BLOCK), :].astype(BF16)
        vb = v_ref[pl.ds(start, 2 * BLOCK), :].astype(BF16)
        km = k_ref[PAD:BLOCK, :].astype(BF16)
        vm = v_ref[PAD:BLOCK, :].astype(BF16)
        rows = Q_PER_KV * BLOCK
        gsl = [slice(g * HEAD_DIM, (g + 1) * HEAD_DIM) for g in range(N_KV_HEADS)]
        raw = []
        for g in range(N_KV_HEADS):
            qg = _group_rows(q_ref, g, _QK_SCALE)
            raw.append((_dot_nt(qg, kb[:, gsl[g]]), _dot_nt(qg, km[:, gsl[g]])))
        for g in range(N_KV_HEADS):
            gs = gsl[g]
            p, pm, _, inv = _attn_probs(raw[g][0], raw[g][1], sink_ref, g, ok, distf, okm, dmf)
            o = (_dot(p.astype(BF16).reshape(rows, 2 * BLOCK), vb[:, gs])
                 + _dot(pm.astype(BF16).reshape(rows, N_META), vm[:, gs])) * inv.reshape(rows, 1)
            for hh in range(Q_PER_KV):
                h = g * Q_PER_KV + hh
                o_ref[:, h * HEAD_DIM:(h + 1) * HEAD_DIM] = o[hh * BLOCK:(hh + 1) * BLOCK, :].astype(o_ref.dtype)
        if plan is not None:
            @pl.when(n == nblk - 1)
            def _():
                plan.wait(cins, couts, sems)

    res = pl.pallas_call(
        body, name=name, grid=(nblk,),
        in_specs=[pl.BlockSpec((BLOCK, D_MODEL), lambda n: (n, _QCOL)),
                  pl.BlockSpec((lp, LANES), lambda n: (0, _KCOL)),
                  pl.BlockSpec((lp, LANES), lambda n: (0, _VCOL)),
                  pl.BlockSpec(sinks.shape, lambda n: (0, 0))] + p_in,
        out_specs=[pl.BlockSpec((BLOCK, D_MODEL), lambda n: (n, 0))] + p_out,
        out_shape=[jax.ShapeDtypeStruct((lp, D_MODEL), BF16)] + p_shapes,
        scratch_shapes=p_scr,
        compiler_params=pltpu.CompilerParams(dimension_semantics=("arbitrary",)),
    )(proj, proj, proj, sinks, *(plan.ins if plan is not None else []))
    return res[0], res[1:]


def _attn_bwd(proj, sinks, dmix, name, plan=None):
    lp = proj.shape[0]
    nblk = lp // BLOCK

    p_in, p_shapes, p_out, p_scr = _plan_parts(plan)

    def body(*refs):
        q_ref, k_ref, v_ref, sink_ref, dy_ref = refs[:5]
        cins = refs[5:5 + len(p_in)]
        dq_ref, dk_ref, dv_ref, ds_ref = refs[5 + len(p_in):9 + len(p_in)]
        couts = refs[9 + len(p_in):9 + len(p_in) + len(p_out)]
        sems = refs[9 + len(p_in) + len(p_out):]
        n = pl.program_id(0)

        @pl.when(n == 0)
        def _():
            dk_ref[...] = jnp.zeros_like(dk_ref)
            dv_ref[...] = jnp.zeros_like(dv_ref)
            ds_ref[...] = jnp.zeros_like(ds_ref)
            if plan is not None:
                plan.start(cins, couts, sems)

        start, ok, distf, okm, dmf = _attn_masks(n)
        kb = k_ref[pl.ds(start, 2 * BLOCK), :].astype(BF16)
        vb = v_ref[pl.ds(start, 2 * BLOCK), :].astype(BF16)
        km = k_ref[PAD:BLOCK, :].astype(BF16)
        vm = v_ref[PAD:BLOCK, :].astype(BF16)
        lane16 = lax.broadcasted_iota(jnp.int32, (1, N_Q_HEADS), 1)
        dsink = jnp.zeros((1, N_Q_HEADS), F32)
        rows = Q_PER_KV * BLOCK
        gsl = [slice(g * HEAD_DIM, (g + 1) * HEAD_DIM) for g in range(N_KV_HEADS)]
        pre = []
        for g in range(N_KV_HEADS):
            qg = _group_rows(q_ref, g, _QK_SCALE)
            dog = _group_rows(dy_ref, g)
            pre.append((qg, dog, _dot_nt(qg, kb[:, gsl[g]]), _dot_nt(qg, km[:, gsl[g]]),
                        _dot_nt(dog, vb[:, gsl[g]]), _dot_nt(dog, vm[:, gsl[g]])))
        for g in range(N_KV_HEADS):
            gs = gsl[g]
            qg, dog, s_raw, sm_raw, dp, dpm = pre[g]
            p, pm, ps, inv = _attn_probs(s_raw, sm_raw, sink_ref, g, ok, distf, okm, dmf)
            pn, pmn, psn = p * inv, pm * inv, ps * inv
            dp = dp.reshape(Q_PER_KV, BLOCK, 2 * BLOCK)
            dpm = dpm.reshape(Q_PER_KV, BLOCK, N_META)
            delta = (jnp.sum(pn * dp, axis=-1, keepdims=True)
                     + jnp.sum(pmn * dpm, axis=-1, keepdims=True))
            dsb = (pn * (dp - delta)).astype(BF16).reshape(rows, 2 * BLOCK)
            dsm = (pmn * (dpm - delta)).astype(BF16).reshape(rows, N_META)
            dsk = jnp.sum(psn * delta, axis=1, keepdims=True)
            for hh in range(Q_PER_KV):
                dsink = dsink - jnp.where(lane16 == g * Q_PER_KV + hh, dsk[hh], 0.0)
            dq = (_dot(dsb, kb[:, gs]) + _dot(dsm, km[:, gs])) * _QK_SCALE
            for hh in range(Q_PER_KV):
                h = g * Q_PER_KV + hh
                dq_ref[:, h * HEAD_DIM:(h + 1) * HEAD_DIM] = dq[hh * BLOCK:(hh + 1) * BLOCK, :].astype(dq_ref.dtype)
            pnb = pn.astype(BF16).reshape(rows, 2 * BLOCK)
            pmnb = pmn.astype(BF16).reshape(rows, N_META)
            dk_ref[pl.ds(start, 2 * BLOCK), gs] += _dot_tn(dsb, qg)
            dv_ref[pl.ds(start, 2 * BLOCK), gs] += _dot_tn(pnb, dog)
            dk_ref[PAD:BLOCK, gs] += _dot_tn(dsm, qg)
            dv_ref[PAD:BLOCK, gs] += _dot_tn(pmnb, dog)
        ds_ref[...] += dsink
        if plan is not None:
            @pl.when(n == nblk - 1)
            def _():
                plan.wait(cins, couts, sems)

    res = pl.pallas_call(
        body, name=name, grid=(nblk,),
        in_specs=[pl.BlockSpec((BLOCK, D_MODEL), lambda n: (n, _QCOL)),
                  pl.BlockSpec((lp, LANES), lambda n: (0, _KCOL)),
                  pl.BlockSpec((lp, LANES), lambda n: (0, _VCOL)),
                  pl.BlockSpec(sinks.shape, lambda n: (0, 0)),
                  pl.BlockSpec((BLOCK, D_MODEL), lambda n: (n, 1))] + p_in,
        out_specs=[pl.BlockSpec((BLOCK, D_MODEL), lambda n: (n, 0)),
                   pl.BlockSpec((lp, LANES), lambda n: (0, 0)),
                   pl.BlockSpec((lp, LANES), lambda n: (0, 0)),
                   pl.BlockSpec((1, N_Q_HEADS), lambda n: (0, 0))] + p_out,
        out_shape=[jax.ShapeDtypeStruct((lp, D_MODEL), BF16),
                   jax.ShapeDtypeStruct((lp, LANES), F32),
                   jax.ShapeDtypeStruct((lp, LANES), F32),
                   jax.ShapeDtypeStruct((1, N_Q_HEADS), F32)] + p_shapes,
        scratch_shapes=p_scr,
        compiler_params=pltpu.CompilerParams(dimension_semantics=("arbitrary",)),
    )(proj, proj, proj, sinks, dmix, *(plan.ins if plan is not None else []))
    return res[:4], res[4:]


_ZW = D_SSM
_XBC_W = D_SSM + 2 * SSD_GROUPS * SSD_N
_DT_COL = (_ZW + _XBC_W) // LANES
EVEN_IN = 3 * D_MODEL + 2 * LANES
ODD_IN = _ZW + _XBC_W + SSD_HEADS
ODD_IN_PAD = _ZW + _XBC_W + LANES


def _ssm_convprep_fwd(proj, cw, cb, dt_bias, name):
    lp = proj.shape[0]
    kk = cw.shape[0]
    tm, wc = _conv_tiles(lp, _XBC_W)
    offb = _ZW // wc
    nj = _XBC_W // wc
    hb = tm // SUBLANES

    def body(x_ref, xp_ref, dtr_ref, w_ref, b_ref, bias_ref, xc_ref, act_ref, dt_ref):
        i, j = pl.program_id(0), pl.program_id(1)
        real = _real_rows(i, tm)
        xv = x_ref[...]
        xx = jnp.concatenate([jnp.where(i > 0, xp_ref[...], 0.0), xv], axis=0)
        acc = b_ref[...] + w_ref[kk - 1:kk, :] * xv
        for m in range(1, kk):
            acc = acc + w_ref[kk - 1 - m:kk - m, :] * pltpu.roll(xx, m, 0)[SUBLANES:, :]
        xc_ref[...] = acc
        act, _ = _silu_and_grad(acc)
        act_ref[...] = jnp.where(real, act, 0.0)

        @pl.when(j == 0)
        def _():
            dt_ref[...] = jnp.where(real, _softplus(dtr_ref[...] + bias_ref[...]), 0.0)

    return pl.pallas_call(
        body, name=name, grid=(lp // tm, nj),
        in_specs=[pl.BlockSpec((tm, wc), lambda i, j: (i, offb + j)),
                  pl.BlockSpec((SUBLANES, wc), lambda i, j: (jnp.maximum(i * hb - 1, 0), offb + j)),
                  pl.BlockSpec((tm, LANES), lambda i, j: (i, _DT_COL)),
                  pl.BlockSpec((kk, wc), lambda i, j: (0, j)),
                  pl.BlockSpec((1, wc), lambda i, j: (0, j)),
                  pl.BlockSpec((1, LANES), lambda i, j: (0, 0))],
        out_specs=[pl.BlockSpec((tm, wc), lambda i, j: (i, j)),
                   pl.BlockSpec((tm, wc), lambda i, j: (i, j)),
                   pl.BlockSpec((tm, LANES), lambda i, j: (i, 0))],
        out_shape=[jax.ShapeDtypeStruct((lp, _XBC_W), F32), jax.ShapeDtypeStruct((lp, _XBC_W), F32),
                   jax.ShapeDtypeStruct((lp, LANES), F32)],
        compiler_params=pltpu.CompilerParams(dimension_semantics=("parallel", "arbitrary")),
    )(proj, proj, proj, cw, cb, dt_bias)


def _ssd_common(dt, alog):
    a = -jnp.exp(alog)
    cs = _cumsum_rows(dt * a, BLOCK)
    cst = cs.T
    cl = cs[BLOCK - 1:BLOCK, :]
    tril = (lax.broadcasted_iota(jnp.int32, (BLOCK, BLOCK), 0)
            >= lax.broadcasted_iota(jnp.int32, (BLOCK, BLOCK), 1))
    return a, cs, cst, cl, jnp.exp(cs), jnp.exp(cl - cs), jnp.exp(cl), tril


def _head_cols(ecl, g):
    lane = lax.broadcasted_iota(jnp.int32, (1, SSD_HPG * SSD_P), 1)
    e = [ecl[:, SSD_HPG * g + hh:SSD_HPG * g + hh + 1] for hh in range(SSD_HPG)]
    return jnp.where(lane < SSD_P, e[0], jnp.where(lane < 2 * SSD_P, e[1],
                                                   jnp.where(lane < 3 * SSD_P, e[2], e[3])))


def _ssd_fwd(xbc, dt, alog, name, plan=None):
    lp = xbc.shape[0]
    nc = lp // BLOCK
    gw = SSD_HPG * SSD_P
    p_in, p_shapes, p_out, p_scr = _plan_parts(plan)

    def body(*refs):
        xs_ref, b_ref, c_ref, dt_ref, alog_ref = refs[:5]
        cins = refs[5:5 + len(p_in)]
        y_ref, so_ref = refs[5 + len(p_in):7 + len(p_in)]
        couts = refs[7 + len(p_in):7 + len(p_in) + len(p_out)]
        st, fx = refs[7 + len(p_in) + len(p_out):9 + len(p_in) + len(p_out)]
        sems = refs[9 + len(p_in) + len(p_out):]
        n = pl.program_id(0)

        @pl.when(n == 0)
        def _():
            st[...] = jnp.zeros_like(st)
            if plan is not None:
                plan.start(cins, couts, sems)

        dtv = dt_ref[...]
        _, cs, cst, cl, e, f, ecl, tril = _ssd_common(dtv, alog_ref[...])
        pre = []
        for g in range(SSD_GROUPS):
            bg = b_ref[:, g * SSD_N:(g + 1) * SSD_N].astype(BF16)
            cg = c_ref[:, g * SSD_N:(g + 1) * SSD_N].astype(BF16)
            stg = st[g]
            so_ref[0, g] = stg
            pre.append((bg, stg, _dot_nt(cg, bg), _dot(cg, stg.astype(BF16))))
        for g in range(SSD_GROUPS):
            bg, stg, gm, yoff = pre[g]
            heads = [SSD_HPG * g + hh for hh in range(SSD_HPG)]
            cols = lambda v: jnp.stack([v[:, h:h + 1] for h in heads])
            x4 = jnp.stack([xs_ref[:, h * SSD_P:(h + 1) * SSD_P] for h in heads])
            csr = jnp.stack([cst[h:h + 1, :] for h in heads])
            m = gm[None] * jnp.exp(jnp.where(tril[None], cols(cs) - csr, NEG))
            xdt = x4 * cols(dtv)
            yoff4 = jnp.stack([yoff[:, hh * SSD_P:(hh + 1) * SSD_P] for hh in range(SSD_HPG)])
            y4 = (jnp.einsum("hls,hsp->hlp", m.astype(BF16), xdt.astype(BF16), preferred_element_type=F32)
                  + cols(e) * yoff4)
            fx4 = cols(f) * xdt
            for hh, h in enumerate(heads):
                y_ref[:, h * SSD_P:(h + 1) * SSD_P] = y4[hh]
                fx[:, hh * SSD_P:(hh + 1) * SSD_P] = fx4[hh]
            st[g] = stg * _head_cols(ecl, g) + _dot_tn(bg, fx[...].astype(BF16))
        if plan is not None:
            @pl.when(n == nc - 1)
            def _():
                plan.wait(cins, couts, sems)

    res = pl.pallas_call(
        body, name=name, grid=(nc,),
        in_specs=[pl.BlockSpec((BLOCK, D_SSM), lambda n: (n, 0)),
                  pl.BlockSpec((BLOCK, 1024), lambda n: (n, 2)),
                  pl.BlockSpec((BLOCK, 1024), lambda n: (n, 3)),
                  pl.BlockSpec((BLOCK, LANES), lambda n: (n, 0)),
                  pl.BlockSpec((1, LANES), lambda n: (0, 0))] + p_in,
        out_specs=[pl.BlockSpec((BLOCK, D_SSM), lambda n: (n, 0)),
                   pl.BlockSpec((1, SSD_GROUPS, SSD_N, gw), lambda n: (n, 0, 0, 0))] + p_out,
        out_shape=[jax.ShapeDtypeStruct((lp, D_SSM), F32),
                   jax.ShapeDtypeStruct((nc, SSD_GROUPS, SSD_N, gw), F32)] + p_shapes,
        scratch_shapes=[pltpu.VMEM((SSD_GROUPS, SSD_N, gw), F32), pltpu.VMEM((BLOCK, gw), F32)] + p_scr,
        compiler_params=pltpu.CompilerParams(dimension_semantics=("arbitrary",)),
    )(xbc, xbc, xbc, dt, alog, *(plan.ins if plan is not None else []))
    return res[:2], res[2:]


def _ssd_bwd(xbc, dt, alog, states, dy, xc, proj, dt_bias, dxskip, name, plan=None):
    lp = xbc.shape[0]
    nc = lp // BLOCK
    gw = SSD_HPG * SSD_P
    p_in, p_shapes, p_out, p_scr = _plan_parts(plan)

    def body(*refs):
        xs_ref, b_ref, c_ref, dt_ref, alog_ref, dy_ref, st_ref, xc_ref, dtr_ref, bias_ref, dsk_ref = refs[:11]
        cins = refs[11:11 + len(p_in)]
        dxc_ref, ddtr_ref, dalog_ref, dbias_ref = refs[11 + len(p_in):15 + len(p_in)]
        couts = refs[15 + len(p_in):15 + len(p_in) + len(p_out)]
        dst, edy, fx, gx = refs[15 + len(p_in) + len(p_out):19 + len(p_in) + len(p_out)]
        sems = refs[19 + len(p_in) + len(p_out):]
        i = pl.program_id(0)

        @pl.when(i == 0)
        def _():
            dst[...] = jnp.zeros_like(dst)
            dalog_ref[...] = jnp.zeros_like(dalog_ref)
            dbias_ref[...] = jnp.zeros_like(dbias_ref)
            if plan is not None:
                plan.start(cins, couts, sems)

        dtv = dt_ref[...]
        a, cs, cst, cl, e, f, ecl, tril = _ssd_common(dtv, alog_ref[...])
        lane = lax.broadcasted_iota(jnp.int32, (1, LANES), 1)
        sub = _row_iota(BLOCK)
        triu = (lax.broadcasted_iota(jnp.int32, (BLOCK, BLOCK), 1)
                >= lax.broadcasted_iota(jnp.int32, (BLOCK, BLOCK), 0))
        dcs = jnp.zeros((BLOCK, LANES), F32)
        dcst = jnp.zeros((LANES, BLOCK), F32)
        dcl = jnp.zeros((1, LANES), F32)
        ddtx = jnp.zeros((BLOCK, LANES), F32)
        pre = []
        for g in range(SSD_GROUPS):
            bg = b_ref[:, g * SSD_N:(g + 1) * SSD_N].astype(BF16)
            cg = c_ref[:, g * SSD_N:(g + 1) * SSD_N].astype(BF16)
            stb = st_ref[0, g].astype(BF16)
            dsob = dst[g].astype(BF16)
            pre.append((bg, cg, stb, dsob, _dot_nt(cg, bg), _dot_nt(bg, cg), _dot(cg, stb), _dot(bg, dsob)))
        for g in range(SSD_GROUPS):
            bg, cg, stb, dsob, gm, gmt, yraw, dfx = pre[g]
            dso = dst[g]
            prodsum = jnp.sum(dso * st_ref[0, g], axis=0, keepdims=True)
            heads = [SSD_HPG * g + hh for hh in range(SSD_HPG)]
            cols = lambda v: jnp.stack([v[:, h:h + 1] for h in heads])
            parts = lambda v: jnp.stack([v[:, hh * SSD_P:(hh + 1) * SSD_P] for hh in range(SSD_HPG)])
            x4 = jnp.stack([xs_ref[:, h * SSD_P:(h + 1) * SSD_P] for h in heads])
            dy4 = jnp.stack([dy_ref[:, h * SSD_P:(h + 1) * SSD_P] for h in heads])
            csc, dtc, ec, fc = cols(cs), cols(dtv), cols(e), cols(f)
            csr = jnp.stack([cst[h:h + 1, :] for h in heads])
            seg = csc - csr
            lam = jnp.exp(jnp.where(tril[None], seg, NEG))
            lamt = jnp.exp(jnp.where(triu[None], -seg, NEG))
            mt = gmt[None] * lamt
            xdt = x4 * dtc
            dyb = dy4.astype(BF16)
            dm = jnp.einsum("hlp,hsp->hls", dyb, xdt.astype(BF16), preferred_element_type=F32)
            dfx4 = parts(dfx)
            dxdt = jnp.einsum("hsl,hlp->hsp", mt.astype(BF16), dyb, preferred_element_type=F32) + fc * dfx4
            dml = dm * lam
            w = dml * gm[None]
            dgm = jnp.sum(dml, axis=0)
            dff = jnp.sum(dfx4 * xdt, axis=2, keepdims=True) * fc
            colv = (jnp.sum(w, axis=2, keepdims=True)
                    + jnp.sum(dy4 * parts(yraw), axis=2, keepdims=True) * ec - dff)
            roww = jnp.sum(w, axis=1, keepdims=True)
            ddtc = jnp.sum(dxdt * x4, axis=2, keepdims=True)
            dffs = jnp.sum(dff, axis=1, keepdims=True)
            dxs4 = dxdt * dtc
            edy4 = ec * dy4
            fx4 = fc * xdt
            for hh, h in enumerate(heads):
                ls = slice(hh * SSD_P, (hh + 1) * SSD_P)
                onl = (lane == h).astype(F32)
                dcs = dcs + colv[hh] * onl
                dcst = dcst - (sub == h).astype(F32) * roww[hh]
                dcl = dcl + (dffs[hh] + ecl[:, h:h + 1] * jnp.sum(prodsum[:, ls], axis=1, keepdims=True)) * onl
                ddtx = ddtx + ddtc[hh] * onl
                gx[:, h * SSD_P:(h + 1) * SSD_P] = dxs4[hh]
                edy[:, ls] = edy4[hh]
                fx[:, ls] = fx4[hh]
            edyb = edy[...].astype(BF16)
            fxb = fx[...].astype(BF16)
            dgb = dgm.astype(BF16)
            gx[:, D_SSM + 1024 + g * SSD_N:D_SSM + 1024 + (g + 1) * SSD_N] = _dot_nt(edyb, stb) + _dot(dgb, bg)
            gx[:, D_SSM + g * SSD_N:D_SSM + (g + 1) * SSD_N] = _dot_nt(fxb, dsob) + _dot_tn(dgb, cg)
            dst[g] = dso * _head_cols(ecl, g) + _dot_tn(cg, edyb)
        dcs = dcs + dcst.T + jnp.where(sub == BLOCK - 1, dcl, 0.0)
        dda = _rev_cumsum_rows(dcs, BLOCK)
        dalog_ref[...] += jnp.sum(dda * dtv, axis=0, keepdims=True) * a
        real = _real_rows(nc - 1 - i, BLOCK)
        _, ds = _silu_and_grad(xc_ref[...])
        dxc_ref[:, :D_SSM] = jnp.where(
            real, (gx[:, :D_SSM] + dsk_ref[...].astype(F32)) * ds[:, :D_SSM], 0.0).astype(dxc_ref.dtype)
        dxc_ref[:, D_SSM:] = jnp.where(real, gx[:, D_SSM:] * ds[:, D_SSM:], 0.0).astype(dxc_ref.dtype)
        dd = jnp.where(real, (ddtx + dda * a) * _sigmoid(dtr_ref[...] + bias_ref[...]), 0.0)
        ddtr_ref[...] = dd.astype(ddtr_ref.dtype)
        dbias_ref[...] += jnp.sum(dd, axis=0, keepdims=True)
        if plan is not None:
            @pl.when(i == nc - 1)
            def _():
                plan.wait(cins, couts, sems)

    rev = lambda i: nc - 1 - i
    res = pl.pallas_call(
        body, name=name, grid=(nc,),
        in_specs=[pl.BlockSpec((BLOCK, D_SSM), lambda i: (rev(i), 0)),
                  pl.BlockSpec((BLOCK, 1024), lambda i: (rev(i), 2)),
                  pl.BlockSpec((BLOCK, 1024), lambda i: (rev(i), 3)),
                  pl.BlockSpec((BLOCK, LANES), lambda i: (rev(i), 0)),
                  pl.BlockSpec((1, LANES), lambda i: (0, 0)),
                  pl.BlockSpec((BLOCK, D_SSM), lambda i: (rev(i), 0)),
                  pl.BlockSpec((1, SSD_GROUPS, SSD_N, gw), lambda i: (rev(i), 0, 0, 0)),
                  pl.BlockSpec((BLOCK, _XBC_W), lambda i: (rev(i), 0)),
                  pl.BlockSpec((BLOCK, LANES), lambda i: (rev(i), _DT_COL)),
                  pl.BlockSpec((1, LANES), lambda i: (0, 0)),
                  pl.BlockSpec((BLOCK, D_SSM), lambda i: (rev(i), 0))] + p_in,
        out_specs=[pl.BlockSpec((BLOCK, _XBC_W), lambda i: (rev(i), 0)),
                   pl.BlockSpec((BLOCK, LANES), lambda i: (rev(i), 0)),
                   pl.BlockSpec((1, LANES), lambda i: (0, 0)),
                   pl.BlockSpec((1, LANES), lambda i: (0, 0))] + p_out,
        out_shape=[jax.ShapeDtypeStruct((lp, _XBC_W), BF16),
                   jax.ShapeDtypeStruct((lp, LANES), BF16),
                   jax.ShapeDtypeStruct((1, LANES), F32),
                   jax.ShapeDtypeStruct((1, LANES), F32)] + p_shapes,
        scratch_shapes=[pltpu.VMEM((SSD_GROUPS, SSD_N, gw), F32),
                        pltpu.VMEM((BLOCK, gw), F32), pltpu.VMEM((BLOCK, gw), F32),
                        pltpu.VMEM((BLOCK, _XBC_W), F32)] + p_scr,
        compiler_params=pltpu.CompilerParams(dimension_semantics=("arbitrary",)),
    )(xbc, xbc, xbc, dt, alog, dy, states, xc, proj, dt_bias, dxskip,
      *(plan.ins if plan is not None else []))
    return res[:4], res[4:]


_GN_GROUPS = 8
_GN_W = D_SSM // _GN_GROUPS


def _ssm_gate_out_postnorm(yssd, xbc, proj, dskip, gnorm, w_out, h, g, next_pre, name):
    lp, d = h.shape
    tm = _pick(lp, (320, 256, 128))

    def body(y_ref, x_ref, z_ref, d_ref, gn_ref, w_ref, h_ref, g_ref, np_ref, yn_ref, o_ref, hn_ref, u_ref):
        sz, _ = _silu_and_grad(z_ref[...])
        y2 = (y_ref[...] + d_ref[...] * x_ref[...]) * sz
        for k in range(_GN_GROUPS):
            sl = slice(k * _GN_W, (k + 1) * _GN_W)
            yk = y2[:, sl]
            rs = lax.rsqrt(jnp.mean(yk * yk, axis=-1, keepdims=True) + EPS)
            yn_ref[:, sl] = (yk * rs * gn_ref[:, sl]).astype(yn_ref.dtype)
        acc = _dot(yn_ref[...], w_ref[...])
        o_ref[...] = acc
        rs = lax.rsqrt(jnp.mean(acc * acc, axis=-1, keepdims=True) + EPS)
        hn = jnp.where(_real_rows(pl.program_id(0), tm), h_ref[...] + acc * rs * g_ref[...], 0.0)
        hn_ref[...] = hn
        rs2 = lax.rsqrt(jnp.mean(hn * hn, axis=-1, keepdims=True) + EPS)
        u_ref[...] = (hn * rs2 * np_ref[...]).astype(u_ref.dtype)

    wide = pl.BlockSpec((tm, D_SSM), lambda i: (i, 0))
    row = pl.BlockSpec((tm, d), lambda i: (i, 0))
    vec = lambda n: pl.BlockSpec((1, n), lambda i: (0, 0))
    return pl.pallas_call(
        body, name=name, grid=(lp // tm,),
        in_specs=[wide, wide, wide, vec(D_SSM), vec(D_SSM), pl.BlockSpec((D_SSM, d), lambda i: (0, 0)),
                  row, vec(d), vec(d)],
        out_specs=[wide, row, row, row],
        out_shape=[jax.ShapeDtypeStruct((lp, D_SSM), BF16), jax.ShapeDtypeStruct((lp, d), F32),
                   jax.ShapeDtypeStruct((lp, d), F32), jax.ShapeDtypeStruct((lp, d), BF16)],
        compiler_params=pltpu.CompilerParams(dimension_semantics=("parallel",),
                                             vmem_limit_bytes=_VMEM_LIMIT_WIDE),
    )(yssd, xbc, proj, dskip, gnorm, w_out, h, g, next_pre)


def _ssm_out_gate_bwd(o, gain, dh, w_out, yssd, xbc, proj, dskip, gnorm, name):
    lp, d = o.shape
    tm = _pick(lp, (320, 256, 128))
    tk = D_SSM // 2

    def body(o_ref, gain_ref, dh_ref, w_ref, y_ref, x_ref, z_ref, d_ref, g_ref,
             dy_ref, dx_ref, dz_ref, dd_ref, dg_ref, do_ref, dgain_ref):
        first = pl.program_id(0) == 0
        dob = _postnorm_bwd_tile(o_ref, gain_ref, dh_ref, do_ref, dgain_ref)
        z = z_ref[...]
        sz, dsz = _silu_and_grad(z)
        xs = x_ref[...]
        y1 = y_ref[...] + d_ref[...] * xs
        y2 = y1 * sz
        for k in range(_GN_GROUPS):
            sl = slice(k * _GN_W, (k + 1) * _GN_W)
            if k % (tk // _GN_W) == 0:
                dyn = _dot_nt(dob, w_ref[k * _GN_W:k * _GN_W + tk, :])
            loc = slice((k % (tk // _GN_W)) * _GN_W, (k % (tk // _GN_W) + 1) * _GN_W)
            dx, dgt = _rms_bwd(y2[:, sl], g_ref[:, sl], dyn[:, loc])
            dy1 = dx * sz[:, sl]
            dy_ref[:, sl] = dy1.astype(dy_ref.dtype)
            dx_ref[:, sl] = (dy1 * d_ref[:, sl]).astype(dx_ref.dtype)
            dz_ref[:, sl] = (dx * y1[:, sl] * dsz[:, sl]).astype(dz_ref.dtype)

            @pl.when(first)
            def _():
                dd_ref[:, sl] = jnp.zeros((1, _GN_W), F32)
                dg_ref[:, sl] = jnp.zeros((1, _GN_W), F32)

            dd_ref[:, sl] += jnp.sum(dy1 * xs[:, sl], axis=0, keepdims=True)
            dg_ref[:, sl] += jnp.sum(dgt, axis=0, keepdims=True)

    tile = pl.BlockSpec((tm, D_SSM), lambda i: (i, 0))
    vec = pl.BlockSpec((1, D_SSM), lambda i: (0, 0))
    row = pl.BlockSpec((tm, d), lambda i: (i, 0))
    rvec = pl.BlockSpec((1, d), lambda i: (0, 0))
    return pl.pallas_call(
        body, name=name, grid=(lp // tm,),
        in_specs=[row, rvec, row, pl.BlockSpec((D_SSM, d), lambda i: (0, 0)), tile, tile, tile, vec, vec],
        out_specs=[tile, tile, tile, vec, vec, row, rvec],
        out_shape=[jax.ShapeDtypeStruct((lp, D_SSM), BF16)] * 3 + [jax.ShapeDtypeStruct((1, D_SSM), F32)] * 2
        + [jax.ShapeDtypeStruct((lp, d), BF16), jax.ShapeDtypeStruct((1, d), F32)],
        compiler_params=pltpu.CompilerParams(dimension_semantics=("arbitrary",),
                                             vmem_limit_bytes=_VMEM_LIMIT_WIDE),
    )(o, gain, dh, w_out, yssd, xbc, proj, dskip, gnorm)


def _shape2d(shape):
    n = math.prod(shape)
    if len(shape) == 2:
        return tuple(shape)
    return (n // LANES, LANES) if n % LANES == 0 else (1, n)


def _adamw_many(ws, gs, ms, vs, name):
    n = len(ws)
    c1 = 1.0 / (1.0 - ADAM_B1 ** ADAM_STEP)
    c2 = 1.0 / (1.0 - ADAM_B2 ** ADAM_STEP)

    def body(*refs):
        for i in range(n):
            w_ref, g_ref, m_ref, v_ref = (refs[j * n + i] for j in range(4))
            d_ref, nm_ref, nv_ref = (refs[(4 + j) * n + i] for j in range(3))
            gv = g_ref[...]
            nm = ADAM_B1 * m_ref[...] + (1.0 - ADAM_B1) * gv
            nv = ADAM_B2 * v_ref[...] + (1.0 - ADAM_B2) * (gv * gv)
            nm_ref[...] = nm
            nv_ref[...] = nv
            d_ref[...] = -ADAM_LR * ((nm * c1) / (jnp.sqrt(nv * c2) + ADAM_EPS) + ADAM_WD * w_ref[...])

    vm = pl.BlockSpec(memory_space=pltpu.VMEM)
    return pl.pallas_call(
        body, name=name, in_specs=[vm] * (4 * n), out_specs=[vm] * (3 * n),
        out_shape=[jax.ShapeDtypeStruct(w.shape, F32) for w in ws] * 3,
    )(*ws, *gs, *ms, *vs)


def _place():
    return lax.axis_index("x"), lax.axis_index("y"), lax.axis_index("c")


def _other_chips(x, y):
    return [(1 - x, y), (x, 1 - y), (1 - x, 1 - y)]


_ANY = pl.BlockSpec(memory_space=pl.ANY)


class _Plan:
    def __init__(self, ins, out_shapes, n_remote, n_local, issue):
        self.ins = list(ins)
        self.out_shapes = list(out_shapes)
        self.issue = issue
        self.scratch = [pltpu.SemaphoreType.DMA((max(n_remote, 1),)),
                        pltpu.SemaphoreType.DMA((max(n_remote, 1),)),
                        pltpu.SemaphoreType.DMA((max(n_local, 1),))]

    def start(self, ins, outs, sems):
        sends, _, locs = self.issue(ins, outs, *sems)
        for cp in locs + sends:
            cp.start()

    def wait(self, ins, outs, sems):
        sends, recvs, locs = self.issue(ins, outs, *sems)
        for make in recvs:
            make().wait_recv()
        for cp in sends:
            cp.wait_send()
        for cp in locs:
            cp.wait()


def _plan_parts(plan):
    if plan is None:
        return [], [], [], []
    return ([_ANY] * len(plan.ins), plan.out_shapes, [_ANY] * len(plan.out_shapes), plan.scratch)


def _run_plan(plan, name):
    n_in, n_out = len(plan.ins), len(plan.out_shapes)

    def body(*refs):
        ins, outs, sems = refs[:n_in], refs[n_in:n_in + n_out], refs[n_in + n_out:]
        plan.start(ins, outs, sems)
        plan.wait(ins, outs, sems)

    return pl.pallas_call(
        body, name=name, in_specs=[_ANY] * n_in, out_specs=[_ANY] * n_out,
        out_shape=plan.out_shapes, scratch_shapes=plan.scratch,
    )(*plan.ins)


def _gather_plan(shards):
    n = len(shards)

    def issue(ins, outs, send_sems, recv_sems, local_sems):
        x, y, c = _place()
        me = 2 * x + y
        sends, recvs, locs = [], [], []
        for p in range(n):
            locs.append(pltpu.make_async_copy(ins[p], outs[p].at[me], local_sems.at[p]))
            for k, (px, py) in enumerate(_other_chips(x, y)):
                sems = dict(send_sem=send_sems.at[3 * p + k], recv_sem=recv_sems.at[3 * p + k],
                            device_id=(px, py, c), device_id_type=MESH)
                sends.append(pltpu.make_async_remote_copy(src_ref=ins[p], dst_ref=outs[p].at[me], **sems))
                recvs.append(functools.partial(pltpu.make_async_remote_copy, src_ref=ins[p],
                                               dst_ref=outs[p].at[2 * px + py], **sems))
        return sends, recvs, locs

    return _Plan(shards, [jax.ShapeDtypeStruct((N_CHIPS,) + s.shape, s.dtype) for s in shards], 3 * n, n, issue)


_REL7 = [(fx, fy, fc) for fx in (0, 1) for fy in (0, 1) for fc in (0, 1)][1:]


def _scatter8_plan(gs):
    n = len(gs)

    def issue(ins, outs, send_sems, recv_sems, local_sems):
        x, y, c = _place()
        sends = []
        for p in range(n):
            hr = gs[p].shape[1] // 2
            for k, (fx, fy, fc) in enumerate(_REL7):
                tx, ty, tc = x ^ fx, y ^ fy, c ^ fc
                src = ins[p].at[2 * tx + ty, pl.ds(pl.multiple_of(tc * hr, SUBLANES), hr), :]
                sends.append(pltpu.make_async_remote_copy(
                    src_ref=src, dst_ref=outs[p].at[k],
                    send_sem=send_sems.at[7 * p + k], recv_sem=recv_sems.at[7 * p + k],
                    device_id=(tx, ty, tc), device_id_type=MESH))
        return sends, [functools.partial(lambda cp: cp, cp) for cp in sends], []

    shapes = [jax.ShapeDtypeStruct((7, g.shape[1] // 2, g.shape[2]), g.dtype) for g in gs]
    return _Plan(gs, shapes, 7 * n, 0, issue)


def _sibling_plan(ts):
    n = len(ts)

    def issue(ins, outs, send_sems, recv_sems, local_sems):
        x, y, c = _place()
        sends = [pltpu.make_async_remote_copy(
            src_ref=ins[p], dst_ref=outs[p], send_sem=send_sems.at[p], recv_sem=recv_sems.at[p],
            device_id=(x, y, 1 - c), device_id_type=MESH) for p in range(n)]
        return sends, [functools.partial(lambda cp: cp, cp) for cp in sends], []

    return _Plan(ts, [jax.ShapeDtypeStruct(t.shape, t.dtype) for t in ts], n, 0, issue)


def _add8(g, recv, chip, core, name):
    s, r, n = g.shape
    hr = r // 2
    th = hr // 2 if (hr // 2) % SUBLANES == 0 else hr
    nt = hr // th

    def body(chip_ref, core_ref, g_ref, r_ref, o_ref):
        acc = g_ref[0].astype(F32)
        for k in range(7):
            acc = acc + r_ref[k].astype(F32)
        o_ref[...] = acc

    return pl.pallas_call(
        body, name=name,
        grid_spec=pltpu.PrefetchScalarGridSpec(
            num_scalar_prefetch=2, grid=(nt,),
            in_specs=[pl.BlockSpec((1, th, n), lambda i, ch, co: (ch[0], co[0] * nt + i, 0)),
                      pl.BlockSpec((7, th, n), lambda i, ch, co: (0, i, 0))],
            out_specs=pl.BlockSpec((th, n), lambda i, ch, co: (i, 0))),
        out_shape=jax.ShapeDtypeStruct((hr, n), F32),
        compiler_params=pltpu.CompilerParams(dimension_semantics=("parallel",)),
    )(chip, core, g, recv)


def _adamw_halves(w, own, other, m, v, core, name):
    r, n = w.shape
    hr = r // 2
    th = hr // 2 if (hr // 2) % SUBLANES == 0 else hr
    tph = hr // th
    c1 = 1.0 / (1.0 - ADAM_B1 ** ADAM_STEP)
    c2 = 1.0 / (1.0 - ADAM_B2 ** ADAM_STEP)

    def body(core_ref, w_ref, a_ref, b_ref, m_ref, v_ref, g_ref, d_ref, nm_ref, nv_ref):
        half = pl.program_id(0) // tph
        gv = jnp.where(half == core_ref[0], a_ref[...], b_ref[...])
        nm = ADAM_B1 * m_ref[...] + (1.0 - ADAM_B1) * gv
        nv = ADAM_B2 * v_ref[...] + (1.0 - ADAM_B2) * (gv * gv)
        g_ref[...] = gv
        nm_ref[...] = nm
        nv_ref[...] = nv
        d_ref[...] = -ADAM_LR * ((nm * c1) / (jnp.sqrt(nv * c2) + ADAM_EPS) + ADAM_WD * w_ref[...])

    full = pl.BlockSpec((th, n), lambda i, co: (i, 0))
    part = pl.BlockSpec((th, n), lambda i, co: (i % tph, 0))
    return pl.pallas_call(
        body, name=name,
        grid_spec=pltpu.PrefetchScalarGridSpec(
            num_scalar_prefetch=1, grid=(2 * tph,),
            in_specs=[full, part, part, full, full], out_specs=[full] * 4),
        out_shape=[jax.ShapeDtypeStruct((r, n), F32)] * 4,
        compiler_params=pltpu.CompilerParams(dimension_semantics=("parallel",)),
    )(core, w, own, other, m, v)


def _allreduce_small(pack, name):
    r, l = pack.shape
    hr = r // 2
    assert hr % SUBLANES == 0

    def body(p_ref, o_ref, sib, chips, send_sems, recv_sems):
        x, y, c = _place()
        chip = 2 * x + y
        sibling = dict(device_id=(x, y, 1 - c), device_id_type=MESH)
        mine = pl.ds(pl.multiple_of(c * hr, SUBLANES), hr)
        other = pl.ds(pl.multiple_of((1 - c) * hr, SUBLANES), hr)
        a = pltpu.make_async_remote_copy(src_ref=p_ref.at[other], dst_ref=sib, send_sem=send_sems.at[0],
                                         recv_sem=recv_sems.at[0], **sibling)
        a.start()
        a.wait()
        own, got = p_ref[mine, :], sib[...]
        chips[chip] = jnp.where(c == 0, own, got) + jnp.where(c == 0, got, own)
        sends = []
        for k, (px, py) in enumerate(_other_chips(x, y)):
            cp = pltpu.make_async_remote_copy(
                src_ref=chips.at[chip], dst_ref=chips.at[chip], send_sem=send_sems.at[1 + k],
                recv_sem=recv_sems.at[1 + k], device_id=(px, py, c), device_id_type=MESH)
            cp.start()
            sends.append(cp)
        for k, (px, py) in enumerate(_other_chips(x, y)):
            pltpu.make_async_remote_copy(
                src_ref=chips.at[chip], dst_ref=chips.at[2 * px + py], send_sem=send_sems.at[1 + k],
                recv_sem=recv_sems.at[1 + k], device_id=(px, py, c), device_id_type=MESH).wait_recv()
        for cp in sends:
            cp.wait_send()
        o_ref[mine, :] = ((chips[0] + chips[1]) + chips[2]) + chips[3]
        fin = pltpu.make_async_remote_copy(src_ref=o_ref.at[mine], dst_ref=o_ref.at[mine],
                                           send_sem=send_sems.at[4], recv_sem=recv_sems.at[4], **sibling)
        fin.start()
        pltpu.make_async_remote_copy(src_ref=o_ref.at[mine], dst_ref=o_ref.at[other],
                                     send_sem=send_sems.at[4], recv_sem=recv_sems.at[4], **sibling).wait_recv()
        fin.wait_send()

    vm = pl.BlockSpec(memory_space=pltpu.VMEM)
    return pl.pallas_call(
        body, name=name, in_specs=[vm], out_specs=vm,
        out_shape=jax.ShapeDtypeStruct((r, l), F32),
        scratch_shapes=[pltpu.VMEM((hr, l), F32), pltpu.VMEM((N_CHIPS, hr, l), F32),
                        pltpu.SemaphoreType.DMA((5,)), pltpu.SemaphoreType.DMA((5,))],
    )(pack)


def _flat_rows(a, mult=SUBLANES * LANES):
    f = a.reshape(-1)
    padn = (-f.shape[0]) % mult
    if padn:
        f = jnp.concatenate([f, jnp.zeros((padn,), f.dtype)])
    return f


def _pack(arrs, mult=SUBLANES * LANES, total_mult=None):
    flat = [_flat_rows(a, mult) for a in arrs]
    sizes = [f.shape[0] for f in flat]
    if total_mult is not None:
        padn = (-sum(sizes)) % total_mult
        if padn:
            flat.append(jnp.zeros((padn,), flat[0].dtype))
    return jnp.concatenate(flat).reshape(-1, LANES), sizes


def _unpack(pack, shapes, sizes, lead=()):
    flat = pack.reshape(lead + (-1,))
    out, off = [], 0
    for shp, sz in zip(shapes, sizes):
        n = math.prod(shp)
        out.append(flat[..., off:off + n].reshape(lead + tuple(shp)))
        off += sz
    return out


def _cols_from_shards(g):
    s, k, n = g.shape
    return jnp.transpose(g, (1, 0, 2)).reshape(k, s * n)


def _ffn_fwd(h, u, post, w_up, cw, cb, w_down, tag, next_pre=None, plan=None, loss_tgt=None):
    (hp, act, hg, hu), pouts = _ffn_up_convact(u, w_up, cw, cb, f"{tag}_up_convact", plan)
    if loss_tgt is not None:
        o, hn, un = _mm_postnorm_loss(act, w_down, h, post, loss_tgt, f"{tag}_down_postnorm_loss")
    else:
        res = _mm_postnorm_res(act, w_down, h, post, f"{tag}_down_postnorm", next_pre)
        o, hn, un = res if next_pre is not None else (*res, None)
    return hn, un, (h, u, hp, hg, hu, act, o), pouts


def _ffn_bwd(dh, saved, pre, post, w_up, cw, w_down, tag):
    h, u, hp, hg, hu, act, o = saved
    dhg, dhu, do, dpost = _ffn_down_act_bwd(o, post, dh, w_down, hg, hu, f"{tag}_down_dx_act_bwd")
    dw_down = _mm_tn(act, do, f"{tag}_down_dw")
    dxg, dwg, dbg = _conv_bwd(hp, 0, D_FF, dhg, cw, f"{tag}_conv_bwd_gate")
    dxu, dwu, dbu = _conv_bwd(hp, D_FF, D_FF, dhu, cw, f"{tag}_conv_bwd_up", w_col_off=D_FF)
    dhp = (dxg, dxu)
    dcw = jnp.concatenate([dwg, dwu], axis=1)
    dcb = jnp.concatenate([dbg, dbu], axis=1)
    (dhn, dpre), _ = _mm_nt_sh(dhp, w_up, (h, pre, dh), f"{tag}_up_dx_prenorm_bwd")
    dw_up = _mm_tn_sh(u, dhp, w_up.shape[2], f"{tag}_up_dw")
    return dhn, dict(pre=dpre, post=dpost, w_up=dw_up, conv_w=dcw[:3], conv_b=dcb, w_down=dw_down)


class _Exchange:
    GATHER_IN_LRU = ("l0_w_out", "l0_ffn_w_down")
    GATHER_IN_ATTN = ("l0_ffn_w_up", "l1_w_out")
    GATHER_IN_FFN0 = ("l1_w_in",)
    GATHER_IN_SSD = ("l1_ffn_w_up", "l1_ffn_w_down")
    AFTER_L1_OUT = ("l1_ffn_w_up", "l1_ffn_w_down", "l1_w_out")
    IN_LRU_BWD = ("l1_w_in",)
    AFTER_L0_OUT = ("l0_ffn_w_up", "l0_ffn_w_down", "l0_w_out")
    LAST = ("l0_w_in",)

    def __init__(self, late_shards):
        self.late = dict(late_shards)
        self.slabs = {}
        self.recv = {}

    def gather_plan(self, names):
        return _gather_plan([self.late[n] for n in names])

    def gathered(self, names, outs):
        return {n: (g if n in _BIG_COL else g.reshape(-1, g.shape[-1])) for n, g in zip(names, outs)}

    def scatter_plan(self, grads, names):
        for n in names:
            g = grads[n]
            self.slabs[n] = g if n in _BIG_COL else g.reshape(N_CHIPS, -1, g.shape[-1])
        return _scatter8_plan([self.slabs[n] for n in names])

    def scattered(self, names, outs):
        self.recv.update(zip(names, outs))


def _local_step(x, tgt, meta, P, ex=None):
    seq, d = x.shape
    lp = seq + BLOCK
    h0 = jnp.concatenate([jnp.zeros((PAD, d), F32), meta, x], axis=0)
    tgt_p = jnp.concatenate([jnp.zeros((BLOCK, d), F32), tgt], axis=0)

    u0 = _rmsnorm_fwd(h0, P["l0_mix_pre_norm"], "l0_mix_prenorm")
    proj0, xrc = _mm_nn_sh(u0, P["l0_w_in"], EVEN_IN, "l0_in_lru_conv",
                           (D_MODEL, D_MODEL, P["l0_lru_conv_w"], P["l0_lru_conv_b"]))
    lru_args = (P["l0_lru_w_a"], P["l0_lru_w_x"], P["l0_lru_b_a"], P["l0_lru_b_x"], P["l0_lru_lambda"])
    if ex:
        (ya, hl), outs = _lru_fwd(proj0, xrc, *lru_args, "l0_lru", ex.gather_plan(ex.GATHER_IN_LRU))
        P = {**P, **ex.gathered(ex.GATHER_IN_LRU, outs)}
    else:
        ya, hl = _lru_fwd(proj0, xrc, *lru_args, "l0_lru")
    yb, outs = _attn_fwd(proj0, P["l0_attn_sinks"], "l0_attn",
                         ex.gather_plan(ex.GATHER_IN_ATTN) if ex else None)
    if ex:
        P = {**P, **ex.gathered(ex.GATHER_IN_ATTN, outs)}
    o0, h1, u1 = _mm_postnorm_res((ya, yb), P["l0_w_out"], h0, P["l0_mix_post_norm"], "l0_out_postnorm",
                                  P["l0_ffn_pre_norm"])
    h2, u2, ffn0, outs = _ffn_fwd(h1, u1, P["l0_ffn_post_norm"], P["l0_ffn_w_up"], P["l0_ffn_conv_w"],
                                  P["l0_ffn_conv_b"], P["l0_ffn_w_down"], "l0_ffn", P["l1_mix_pre_norm"],
                                  ex.gather_plan(ex.GATHER_IN_FFN0) if ex else None)
    if ex:
        P = {**P, **ex.gathered(ex.GATHER_IN_FFN0, outs)}
    proj1 = _mm_nn_sh(u2, P["l1_w_in"], ODD_IN_PAD, "l1_in")
    xc1, xbc, dt = _ssm_convprep_fwd(proj1, P["l1_ssm_conv_w"], P["l1_ssm_conv_b"], P["l1_dt_bias"],
                                     "l1_ssm_convprep")
    (yssd, states), outs = _ssd_fwd(xbc, dt, P["l1_a_log"], "l1_ssd",
                                    ex.gather_plan(ex.GATHER_IN_SSD) if ex else None)
    if ex:
        P = {**P, **ex.gathered(ex.GATHER_IN_SSD, outs)}
    yn, o1, h3, u3 = _ssm_gate_out_postnorm(
        yssd, xbc, proj1, P["l1_d_skip"], P["l1_gate_norm"], P["l1_w_out"], h2, P["l1_mix_post_norm"],
        P["l1_ffn_pre_norm"], "l1_gate_out_postnorm")
    dh4, loss_cols, ffn1, _ = _ffn_fwd(h3, u3, P["l1_ffn_post_norm"], P["l1_ffn_w_up"], P["l1_ffn_conv_w"],
                                       P["l1_ffn_conv_b"], P["l1_ffn_w_down"], "l1_ffn", loss_tgt=tgt_p)

    G = {}
    dh3, g = _ffn_bwd(dh4, ffn1, P["l1_ffn_pre_norm"], P["l1_ffn_post_norm"], P["l1_ffn_w_up"],
                      P["l1_ffn_conv_w"], P["l1_ffn_w_down"], "l1_ffn")
    for k, v in g.items():
        G["l1_ffn_" + (k + "_norm" if k in ("pre", "post") else k)] = v
    dyssd, dxskip, dz, dd_cols, G["l1_gate_norm"], do1, G["l1_mix_post_norm"] = _ssm_out_gate_bwd(
        o1, P["l1_mix_post_norm"], dh3, P["l1_w_out"], yssd, xbc, proj1, P["l1_d_skip"], P["l1_gate_norm"],
        "l1_out_dx_gate_bwd")
    G["l1_w_out"] = _mm_tn(yn, do1, "l1_out_dw")
    G["l1_d_skip"] = dd_cols.reshape(SSD_HEADS, SSD_P).sum(axis=1)
    (dxc, ddtr, dalog, dbias), outs = _ssd_bwd(
        xbc, dt, P["l1_a_log"], states, dyssd, xc1, proj1, P["l1_dt_bias"], dxskip, "l1_ssd_bwd",
        ex.scatter_plan(G, ex.AFTER_L1_OUT) if ex else None)
    if ex:
        ex.scattered(ex.AFTER_L1_OUT, outs)
    G["l1_a_log"] = dalog[0, :SSD_HEADS]
    G["l1_dt_bias"] = dbias[0, :SSD_HEADS]
    dxbc, dcw, dcb = _conv_bwd(proj1, _ZW, _XBC_W, dxc, P["l1_ssm_conv_w"], "l1_ssm_conv_bwd")
    G["l1_ssm_conv_w"] = dcw[:4]
    G["l1_ssm_conv_b"] = dcb
    dproj1 = jnp.concatenate([dz, dxbc, ddtr], axis=1)
    (dh2, G["l1_mix_pre_norm"]), _ = _mm_nt_sh(dproj1, P["l1_w_in"], (h2, P["l1_mix_pre_norm"], dh3),
                                               "l1_in_dx_prenorm_bwd")
    G["l1_w_in"] = _mm_tn_sh(u2, dproj1, ODD_IN // N_CHIPS, "l1_in_dw")
    dh1, g = _ffn_bwd(dh2, ffn0, P["l0_ffn_pre_norm"], P["l0_ffn_post_norm"], P["l0_ffn_w_up"],
                      P["l0_ffn_conv_w"], P["l0_ffn_w_down"], "l0_ffn")
    for k, v in g.items():
        G["l0_ffn_" + (k + "_norm" if k in ("pre", "post") else k)] = v
    dmix, do0, G["l0_mix_post_norm"] = _mm_nt_postnorm_bwd(o0, P["l0_mix_post_norm"], dh1, P["l0_w_out"],
                                                           "l0_out_dx")
    G["l0_w_out"] = jnp.concatenate([_mm_tn(ya, do0, "l0_out_dw_lru"), _mm_tn(yb, do0, "l0_out_dw_attn")], axis=0)
    if ex:
        lru_out, outs = _lru_bwd(proj0, xrc, hl, dmix, *lru_args, "l0_lru_bwd",
                                 ex.scatter_plan(G, ex.IN_LRU_BWD))
        ex.scattered(ex.IN_LRU_BWD, outs)
    else:
        lru_out = _lru_bwd(proj0, xrc, hl, dmix, *lru_args, "l0_lru_bwd")
    (dgate, dxrc, G["l0_lru_w_a"], G["l0_lru_w_x"], G["l0_lru_b_a"], G["l0_lru_b_x"],
     G["l0_lru_lambda"]) = lru_out
    dxr, dcw, dcb = _conv_bwd(proj0, D_MODEL, D_MODEL, dxrc, P["l0_lru_conv_w"], "l0_lru_conv_bwd")
    G["l0_lru_conv_w"] = dcw[:4]
    G["l0_lru_conv_b"] = dcb
    (dq, dk, dv, G["l0_attn_sinks"]), outs = _attn_bwd(
        proj0, P["l0_attn_sinks"], dmix, "l0_attn_bwd",
        ex.scatter_plan(G, ex.AFTER_L0_OUT) if ex else None)
    if ex:
        ex.scattered(ex.AFTER_L0_OUT, outs)
    dproj0 = jnp.concatenate([dgate, dxr, dq, dk.astype(BF16), dv.astype(BF16)], axis=1)
    G["l0_w_in"] = _mm_tn_sh(u0, dproj0, EVEN_IN // N_CHIPS, "l0_in_dw")
    (dh0, G["l0_mix_pre_norm"]), outs = _mm_nt_sh(
        dproj0, P["l0_w_in"], (h0, P["l0_mix_pre_norm"], dh1), "l0_in_dx_prenorm_bwd",
        ex.scatter_plan(G, ex.LAST) if ex else None)
    if ex:
        ex.scattered(ex.LAST, outs)
    return loss_cols, dh0[BLOCK:], dh0[PAD:BLOCK], G


_BIG_COL = ("l0_w_in", "l0_ffn_w_up", "l1_w_in", "l1_ffn_w_up")
_BIG = ("l0_w_in", "l0_w_out", "l0_ffn_w_up", "l0_ffn_w_down",
        "l1_w_in", "l1_w_out", "l1_ffn_w_up", "l1_ffn_w_down")
_SMALL_SHARDED = ("meta_tokens", "l0_lru_conv_w", "l0_ffn_conv_w", "l1_ssm_conv_w", "l1_ffn_conv_w")
_WEIGHTS = ("meta_tokens", "l0_mix_pre_norm", "l0_mix_post_norm", "l0_w_in", "l0_lru_conv_w",
            "l0_lru_conv_b", "l0_lru_w_a", "l0_lru_b_a", "l0_lru_w_x", "l0_lru_b_x", "l0_lru_lambda",
            "l0_attn_sinks", "l0_w_out", "l0_ffn_pre_norm", "l0_ffn_post_norm", "l0_ffn_w_up",
            "l0_ffn_conv_w", "l0_ffn_conv_b", "l0_ffn_w_down", "l1_mix_pre_norm", "l1_mix_post_norm",
            "l1_w_in", "l1_ssm_conv_w", "l1_ssm_conv_b", "l1_dt_bias", "l1_a_log", "l1_d_skip",
            "l1_gate_norm", "l1_w_out", "l1_ffn_pre_norm", "l1_ffn_post_norm", "l1_ffn_w_up",
            "l1_ffn_conv_w", "l1_ffn_conv_b", "l1_ffn_w_down")
_REPL = tuple(n for n in _WEIGHTS if n not in _BIG and n not in _SMALL_SHARDED)


def _pad_lanes(v, n=LANES):
    return jnp.concatenate([v, jnp.zeros((n - v.shape[0],), v.dtype)]).reshape(1, n)


def _step(x, tgt, W, M, V):
    cx, cy, cc = _place()
    chip = 2 * cx + cy

    small_pack, small_sizes = _pack([W[n] for n in _SMALL_SHARDED])
    first = _run_plan(_gather_plan([W["l0_w_in"].astype(BF16), small_pack]), "gather_first")
    small_full = _unpack(first[1], [W[n].shape for n in _SMALL_SHARDED], small_sizes, lead=(N_CHIPS,))
    ex = _Exchange({n: W[n].astype(BF16) for n in _BIG if n != "l0_w_in"})

    P = {"l0_w_in": first[0]}
    for n, g in zip(_SMALL_SHARDED, small_full):
        P[n] = _cols_from_shards(g)
    for n in _REPL:
        v = W[n]
        P[n] = v.reshape(1, -1) if v.ndim == 1 else v
    P["l0_lru_w_a"] = W["l0_lru_w_a"].astype(BF16)
    P["l0_lru_w_x"] = W["l0_lru_w_x"].astype(BF16)
    P["l1_dt_bias"] = _pad_lanes(W["l1_dt_bias"])
    P["l1_a_log"] = _pad_lanes(W["l1_a_log"])
    P["l1_d_skip"] = jnp.repeat(W["l1_d_skip"], SSD_P).reshape(1, D_SSM)
    meta = P.pop("meta_tokens")

    loss_cols, grad_x, grad_meta, G = _local_step(x, tgt, meta, P, ex)
    G["meta_tokens"] = grad_meta

    core_idx = cc.astype(jnp.int32).reshape(1)
    chip_idx = chip.astype(jnp.int32).reshape(1)
    own_half = [_add8(ex.slabs[n], ex.recv[n], chip_idx, core_idx, f"grad_sum_{n}") for n in _BIG]
    other_half = _run_plan(_sibling_plan(own_half), "grad_sibling_swap")
    small_names = list(_REPL) + list(_SMALL_SHARDED)
    small_list = [G[n] for n in small_names] + [loss_cols]
    spack, ssizes = _pack(small_list, total_mult=2 * SUBLANES * LANES)
    sred = _allreduce_small(spack, "small_allreduce")
    sfull = _unpack(sred, [a.shape for a in small_list], ssizes)
    loss = 0.5 / D_MODEL * jnp.sum(sfull[-1])
    small_grads = {}
    for n, g in zip(small_names, sfull[:-1]):
        if n in _SMALL_SHARDED:
            wcols = W[n].shape[1]
            g = lax.dynamic_slice_in_dim(g, chip * wcols, wcols, axis=1)
        small_grads[n] = g.reshape(W[n].shape)

    grads, delta, new_m, new_v = {}, {}, {}, {}
    for n, own, other in zip(_BIG, own_half, other_half):
        grads[n], delta[n], new_m[n], new_v[n] = _adamw_halves(
            W[n], own, other, M[n], V[n], core_idx, f"adamw_{n}")
    s_names = [n for n in _WEIGHTS if n not in _BIG]
    as2d = lambda a: a.reshape(_shape2d(a.shape))
    outs = _adamw_many([as2d(W[n]) for n in s_names], [as2d(small_grads[n]) for n in s_names],
                       [as2d(M[n]) for n in s_names], [as2d(V[n]) for n in s_names], "adamw_small")
    k = len(s_names)
    for i, n in enumerate(s_names):
        grads[n] = small_grads[n]
        delta[n], new_m[n], new_v[n] = (outs[j * k + i].reshape(W[n].shape) for j in range(3))
    return loss, grad_x, grads, delta, new_m, new_v


def kernel(x, meta_tokens, l0_mix_pre_norm, l0_mix_post_norm, l0_w_in, l0_lru_conv_w, l0_lru_conv_b, l0_lru_w_a, l0_lru_b_a, l0_lru_w_x, l0_lru_b_x, l0_lru_lambda, l0_attn_sinks, l0_w_out, l0_ffn_pre_norm, l0_ffn_post_norm, l0_ffn_w_up, l0_ffn_conv_w, l0_ffn_conv_b, l0_ffn_w_down, l1_mix_pre_norm, l1_mix_post_norm, l1_w_in, l1_ssm_conv_w, l1_ssm_conv_b, l1_dt_bias, l1_a_log, l1_d_skip, l1_gate_norm, l1_w_out, l1_ffn_pre_norm, l1_ffn_post_norm, l1_ffn_w_up, l1_ffn_conv_w, l1_ffn_conv_b, l1_ffn_w_down, loss_target, m_meta_tokens, m_l0_mix_pre_norm, m_l0_mix_post_norm, m_l0_w_in, m_l0_lru_conv_w, m_l0_lru_conv_b, m_l0_lru_w_a, m_l0_lru_b_a, m_l0_lru_w_x, m_l0_lru_b_x, m_l0_lru_lambda, m_l0_attn_sinks, m_l0_w_out, m_l0_ffn_pre_norm, m_l0_ffn_post_norm, m_l0_ffn_w_up, m_l0_ffn_conv_w, m_l0_ffn_conv_b, m_l0_ffn_w_down, m_l1_mix_pre_norm, m_l1_mix_post_norm, m_l1_w_in, m_l1_ssm_conv_w, m_l1_ssm_conv_b, m_l1_dt_bias, m_l1_a_log, m_l1_d_skip, m_l1_gate_norm, m_l1_w_out, m_l1_ffn_pre_norm, m_l1_ffn_post_norm, m_l1_ffn_w_up, m_l1_ffn_conv_w, m_l1_ffn_conv_b, m_l1_ffn_w_down, v_meta_tokens, v_l0_mix_pre_norm, v_l0_mix_post_norm, v_l0_w_in, v_l0_lru_conv_w, v_l0_lru_conv_b, v_l0_lru_w_a, v_l0_lru_b_a, v_l0_lru_w_x, v_l0_lru_b_x, v_l0_lru_lambda, v_l0_attn_sinks, v_l0_w_out, v_l0_ffn_pre_norm, v_l0_ffn_post_norm, v_l0_ffn_w_up, v_l0_ffn_conv_w, v_l0_ffn_conv_b, v_l0_ffn_w_down, v_l1_mix_pre_norm, v_l1_mix_post_norm, v_l1_w_in, v_l1_ssm_conv_w, v_l1_ssm_conv_b, v_l1_dt_bias, v_l1_a_log, v_l1_d_skip, v_l1_gate_norm, v_l1_w_out, v_l1_ffn_pre_norm, v_l1_ffn_post_norm, v_l1_ffn_w_up, v_l1_ffn_conv_w, v_l1_ffn_conv_b, v_l1_ffn_w_down):
    args = locals()
    W = {n: args[n] for n in _WEIGHTS}
    M = {n: args["m_" + n] for n in _WEIGHTS}
    V = {n: args["v_" + n] for n in _WEIGHTS}
    loss, grad_x, grads, delta, new_m, new_v = _step(x[0], loss_target[0], W, M, V)
    return (loss, grad_x[None], *[grads[n] for n in _WEIGHTS], *[delta[n] for n in _WEIGHTS],
            *[new_m[n] for n in _WEIGHTS], *[new_v[n] for n in _WEIGHTS])
```

```python
import functools
import math

import jax
import jax.numpy as jnp
from jax import lax
from jax.experimental import pallas as pl
from jax.experimental.pallas import tpu as pltpu

F32 = jnp.float32
BF16 = jnp.bfloat16

D_MODEL = 1024
N_META = 16
BLOCK = 128
PAD = BLOCK - N_META
EPS = 1e-6
LRU_BLOCKS = 8
LRU_BS = 128
LRU_C = 8.0
N_Q_HEADS = 16
N_KV_HEADS = 2
HEAD_DIM = 64
Q_PER_KV = 8
WINDOW = 128
D_SSM = 2048
SSD_HEADS = 32
SSD_GROUPS = 8
SSD_HPG = 4
SSD_P = 64
SSD_N = 128
D_FF = 2816
NEG = -1e30
LANES = 128
SUBLANES = 8
_VMEM_LIMIT_WIDE = 62 * 1024 * 1024

ADAM_LR = 0.001
ADAM_B1 = 0.9
ADAM_B2 = 0.999
ADAM_EPS = 1e-08
ADAM_WD = 0.01
ADAM_STEP = 10

MESH = pl.DeviceIdType.MESH
N_CHIPS = 4


def _pick(n, cands):
    for c in cands:
        if n % c == 0:
            return c
    raise ValueError(f"no tile for {n} in {cands}")


def _col_tile(n, limit=1792):
    best = None
    for t in range(LANES, min(n, limit) + 1, LANES):
        if n % t == 0:
            best = t
    if best is None:
        raise ValueError(f"no lane tile for {n}")
    return best


def _sigmoid(x):
    return 0.5 + 0.5 * jnp.tanh(0.5 * x)


def _log1p(e):
    u = 1.0 + e
    return jnp.where(u == 1.0, e, jnp.log(u) * (e / jnp.where(u == 1.0, 1.0, u - 1.0)))


def _softplus(x):
    return jnp.maximum(x, 0.0) + _log1p(jnp.exp(-jnp.abs(x)))


def _neg_expm1(x):
    poly = x * (1.0 + x * (0.5 + x * (1.0 / 6.0 + x * (1.0 / 24.0 + x * (1.0 / 120.0)))))
    return -jnp.where(x > -0.05, poly, jnp.exp(x) - 1.0)


_GELU_C = math.sqrt(2.0 / math.pi)


def _gelu(x):
    u = 0.5 + 0.5 * jnp.tanh(x * (_GELU_C + (_GELU_C * 0.044715) * (x * x)))
    return x * u


def _gelu_and_grad(x):
    x2 = x * x
    u = 0.5 + 0.5 * jnp.tanh(x * (_GELU_C + (_GELU_C * 0.044715) * x2))
    g = x * u
    dg = u * (1.0 + (x - g) * (2.0 * _GELU_C + (6.0 * 0.044715 * _GELU_C) * x2))
    return g, dg


def _silu_and_grad(x):
    s = _sigmoid(x)
    return x * s, s * (1.0 + x * (1.0 - s))


def _dot(a, b):
    return jnp.dot(a, b, preferred_element_type=F32)


def _dot_nt(a, b):
    return lax.dot_general(a, b, (((1,), (1,)), ((), ())), preferred_element_type=F32)


def _dot_tn(a, b):
    return lax.dot_general(a, b, (((0,), (0,)), ((), ())), preferred_element_type=F32)


def _row_iota(t):
    return lax.broadcasted_iota(jnp.int32, (t, 1), 0)


def _scan_fwd(a, u, t):
    row = _row_iota(t)
    d = 1
    while d < t:
        m = row >= d
        u_sh = jnp.where(m, pltpu.roll(u, d, 0), 0.0)
        a_sh = jnp.where(m, pltpu.roll(a, d, 0), 1.0)
        u = u + a * u_sh
        a = a * a_sh
        d *= 2
    return a, u


def _scan_rev(c, x, t):
    row = _row_iota(t)
    d = 1
    while d < t:
        m = row < t - d
        x_sh = jnp.where(m, pltpu.roll(x, t - d, 0), 0.0)
        c_sh = jnp.where(m, pltpu.roll(c, t - d, 0), 1.0)
        x = x + c * x_sh
        c = c * c_sh
        d *= 2
    return c, x


def _cumsum_rows(x, t):
    row = _row_iota(t)
    d = 1
    while d < t:
        x = x + jnp.where(row >= d, pltpu.roll(x, d, 0), 0.0)
        d *= 2
    return x


def _rev_cumsum_rows(x, t):
    row = _row_iota(t)
    d = 1
    while d < t:
        x = x + jnp.where(row < t - d, pltpu.roll(x, t - d, 0), 0.0)
        d *= 2
    return x


def _rms_bwd(x, g, dy):
    rs = lax.rsqrt(jnp.mean(x * x, axis=-1, keepdims=True) + EPS)
    gy = dy * g
    dx = rs * gy - x * (rs * rs * rs) * jnp.mean(x * gy, axis=-1, keepdims=True)
    return dx, dy * x * rs


def _mm_nt_postnorm_bwd(o, gain, dh, w, name):
    m, d = o.shape
    k = w.shape[0]
    tm = _pick(m, (640, 512, 256, 128))

    def body(o_ref, gain_ref, dh_ref, w_ref, y_ref, do_ref, dgain_ref):
        y_ref[...] = _dot_nt(_postnorm_bwd_tile(o_ref, gain_ref, dh_ref, do_ref, dgain_ref), w_ref[...])

    row = pl.BlockSpec((tm, d), lambda i: (i, 0))
    vec = pl.BlockSpec((1, d), lambda i: (0, 0))
    return pl.pallas_call(
        body, name=name, grid=(m // tm,),
        in_specs=[row, vec, row, pl.BlockSpec((k, d), lambda i: (0, 0))],
        out_specs=[pl.BlockSpec((tm, k), lambda i: (i, 0)), row, vec],
        out_shape=[jax.ShapeDtypeStruct((m, k), F32), jax.ShapeDtypeStruct((m, d), BF16),
                   jax.ShapeDtypeStruct((1, d), F32)],
        compiler_params=pltpu.CompilerParams(dimension_semantics=("arbitrary",)),
    )(o, gain, dh, w)


def _mm_tn(a, dy, name):
    m, k = a.shape
    n = dy.shape[1]
    tm = _pick(m, (640, 512, 256, 128))
    tk = _col_tile(k, 1408)
    tn = _col_tile(n, 1664)
    nsteps = m // tm

    def body(a_ref, dy_ref, o_ref, acc):
        @pl.when(pl.program_id(2) == 0)
        def _():
            acc[...] = jnp.zeros_like(acc)

        acc[...] += _dot_tn(a_ref[...].astype(BF16), dy_ref[...].astype(BF16))

        @pl.when(pl.program_id(2) == nsteps - 1)
        def _():
            o_ref[...] = acc[...].astype(o_ref.dtype)

    return pl.pallas_call(
        body, name=name, grid=(k // tk, n // tn, nsteps),
        in_specs=[pl.BlockSpec((tm, tk), lambda kk, j, i: (i, kk)),
                  pl.BlockSpec((tm, tn), lambda kk, j, i: (i, j))],
        out_specs=pl.BlockSpec((tk, tn), lambda kk, j, i: (kk, j)),
        out_shape=jax.ShapeDtypeStruct((k, n), BF16),
        scratch_shapes=[pltpu.VMEM((tk, tn), F32)],
        compiler_params=pltpu.CompilerParams(
            dimension_semantics=("parallel", "parallel", "arbitrary")),
    )(a, dy)


def _mm_nn_sh(a, w4, n_out, name, conv=None):
    m, k = a.shape
    s, _, n = w4.shape
    tm = _pick(m, (320, 256, 128))
    if conv is not None:
        c0, cw_, taps, bias = conv
        kk = taps.shape[0]

    def body(*refs):
        a_ref, w_ref = refs[:2]
        av = a_ref[...].astype(BF16)
        o_ref = refs[4] if conv is not None else refs[2]
        for j in range(s):
            o_ref[:, j * n:(j + 1) * n] = _dot(av, w_ref[j])
        if n_out > s * n:
            o_ref[:, s * n:] = jnp.zeros((tm, n_out - s * n), F32)
        if conv is not None:
            t_ref, b_ref, y_ref, carry = refs[2], refs[3], refs[5], refs[6]

            @pl.when(pl.program_id(0) == 0)
            def _():
                carry[...] = jnp.zeros_like(carry)

            xv = o_ref[:, c0:c0 + cw_]
            xx = jnp.concatenate([carry[...], xv], axis=0)
            acc = b_ref[...] + t_ref[kk - 1:kk, :] * xv
            for mm in range(1, kk):
                acc = acc + t_ref[kk - 1 - mm:kk - mm, :] * pltpu.roll(xx, mm, 0)[SUBLANES:, :]
            y_ref[...] = acc
            carry[...] = xv[tm - SUBLANES:, :]

    row_a = pl.BlockSpec((tm, k), lambda i: (i, 0))
    w_spec = pl.BlockSpec((s, k, n), lambda i: (0, 0, 0))
    o_spec = pl.BlockSpec((tm, n_out), lambda i: (i, 0))
    if conv is None:
        return pl.pallas_call(
            body, name=name, grid=(m // tm,), in_specs=[row_a, w_spec], out_specs=o_spec,
            out_shape=jax.ShapeDtypeStruct((m, n_out), F32),
            compiler_params=pltpu.CompilerParams(dimension_semantics=("parallel",)),
        )(a, w4)
    return pl.pallas_call(
        body, name=name, grid=(m // tm,),
        in_specs=[row_a, w_spec, pl.BlockSpec(taps.shape, lambda i: (0, 0)),
                  pl.BlockSpec(bias.shape, lambda i: (0, 0))],
        out_specs=[o_spec, pl.BlockSpec((tm, cw_), lambda i: (i, 0))],
        out_shape=[jax.ShapeDtypeStruct((m, n_out), F32), jax.ShapeDtypeStruct((m, cw_), F32)],
        scratch_shapes=[pltpu.VMEM((SUBLANES, cw_), F32)],
        compiler_params=pltpu.CompilerParams(dimension_semantics=("arbitrary",)),
    )(a, w4, taps, bias)


def _mm_nt_sh(dy, w4, norm, name, plan=None):
    dys = dy if isinstance(dy, (tuple, list)) else (dy,)
    h, g, dres = norm
    m = dys[0].shape[0]
    s, k, n = w4.shape
    tm = _pick(m, (640, 512, 256, 128))
    where = _shard_columns(dys, s, n)
    p_in, p_shapes, p_out, p_scr = _plan_parts(plan)
    nd, ni = len(dys), m // tm

    def body(*refs):
        w_ref, h_ref, g_ref, dres_ref = refs[nd:nd + 4]
        cins = refs[nd + 4:nd + 4 + len(p_in)]
        dh_ref, dg_ref = refs[nd + 4 + len(p_in):nd + 6 + len(p_in)]
        couts = refs[nd + 6 + len(p_in):nd + 6 + len(p_in) + len(p_out)]
        sems = refs[nd + 6 + len(p_in) + len(p_out):]
        i = pl.program_id(0)
        if plan is not None:
            @pl.when(i == 0)
            def _():
                plan.start(cins, couts, sems)

        du = None
        for j, (p, c0) in enumerate(where):
            t = _dot_nt(refs[p][:, c0:c0 + n].astype(BF16), w_ref[j])
            du = t if du is None else du + t
        dx, dgt = _rms_bwd(h_ref[...], g_ref[...], du)
        dh_ref[...] = jnp.where(_real_rows(i, tm), dres_ref[...] + dx, 0.0)
        _acc_add(i == 0, dg_ref, jnp.sum(dgt, axis=0, keepdims=True))
        if plan is not None:
            @pl.when(i == ni - 1)
            def _():
                plan.wait(cins, couts, sems)

    row = pl.BlockSpec((tm, k), lambda i: (i, 0))
    vec = pl.BlockSpec((1, k), lambda i: (0, 0))
    res = pl.pallas_call(
        body, name=name, grid=(ni,),
        in_specs=[pl.BlockSpec((tm, d.shape[1]), lambda i: (i, 0)) for d in dys]
        + [pl.BlockSpec((s, k, n), lambda i: (0, 0, 0)), row, vec, row] + p_in,
        out_specs=[row, vec] + p_out,
        out_shape=[jax.ShapeDtypeStruct((m, k), F32), jax.ShapeDtypeStruct((1, k), F32)] + p_shapes,
        scratch_shapes=p_scr,
        compiler_params=pltpu.CompilerParams(dimension_semantics=("arbitrary",),
                                             vmem_limit_bytes=_VMEM_LIMIT_WIDE),
    )(*dys, w4, h, g, dres, *(plan.ins if plan is not None else []))
    return (res[0], res[1]), res[2:]


def _shard_columns(dys, s, n):
    where = []
    for p, d in enumerate(dys):
        where += [(p, c * n) for c in range(d.shape[1] // n)]
    assert len(where) >= s
    return where[:s]


def _mm_tn_sh(a, dy, n, name):
    dys = dy if isinstance(dy, (tuple, list)) else (dy,)
    m, k = a.shape
    s = N_CHIPS
    tm = _pick(m, (640, 512, 256, 128))
    tk = _col_tile(k, 512)
    nsteps = m // tm
    where = _shard_columns(dys, s, n)

    def body(*refs):
        a_ref, o_ref, acc = refs[0], refs[len(dys) + 1], refs[len(dys) + 2]

        @pl.when(pl.program_id(1) == 0)
        def _():
            acc[...] = jnp.zeros_like(acc)

        av = a_ref[...].astype(BF16)
        for j, (p, c0) in enumerate(where):
            acc[j] += _dot_tn(av, refs[1 + p][:, c0:c0 + n].astype(BF16))

        @pl.when(pl.program_id(1) == nsteps - 1)
        def _():
            o_ref[...] = acc[...].astype(o_ref.dtype)

    return pl.pallas_call(
        body, name=name, grid=(k // tk, nsteps),
        in_specs=[pl.BlockSpec((tm, tk), lambda kk, i: (i, kk))]
        + [pl.BlockSpec((tm, d.shape[1]), lambda kk, i: (i, 0)) for d in dys],
        out_specs=pl.BlockSpec((s, tk, n), lambda kk, i: (0, kk, 0)),
        out_shape=jax.ShapeDtypeStruct((s, k, n), BF16),
        scratch_shapes=[pltpu.VMEM((s, tk, n), F32)],
        compiler_params=pltpu.CompilerParams(dimension_semantics=("parallel", "arbitrary"),
                                             vmem_limit_bytes=_VMEM_LIMIT_WIDE),
    )(a, *dys)


def _rowcall(name, body, lp, tm, rows=(), prevs=(), vecs=(), outs=(), accs=(), scratch=(),
             reverse=False, seq=False, plan=None):
    p_in, p_shapes, p_out, p_scr = _plan_parts(plan)
    nt = lp // tm
    hb = tm // SUBLANES

    def ri(i):
        return nt - 1 - i if reverse else i

    in_specs, args = [], []
    for arr, w, cb in rows:
        in_specs.append(pl.BlockSpec((tm, w), lambda i, cb=cb: (ri(i), cb)))
        args.append(arr)
    for arr, w, cb in prevs:
        in_specs.append(pl.BlockSpec((SUBLANES, w), lambda i, cb=cb: (jnp.maximum(ri(i) * hb - 1, 0), cb)))
        args.append(arr)
    for arr in vecs:
        in_specs.append(pl.BlockSpec(arr.shape, lambda i, nd=arr.ndim: (0,) * nd))
        args.append(arr)
    out_shape, out_specs = [], []
    for w, dt in outs:
        out_shape.append(jax.ShapeDtypeStruct((lp, w), dt))
        out_specs.append(pl.BlockSpec((tm, w), lambda i: (ri(i), 0)))
    for shp, dt in accs:
        out_shape.append(jax.ShapeDtypeStruct(shp, dt))
        out_specs.append(pl.BlockSpec(shp, lambda i, nd=len(shp): (0,) * nd))

    n_in, n_out, n_scr = len(args), len(out_shape), len(scratch)

    def kern(*refs):
        i = pl.program_id(0)
        own = (refs[:n_in] + refs[n_in + len(p_in):n_in + len(p_in) + n_out]
               + refs[n_in + len(p_in) + n_out + len(p_out):n_in + len(p_in) + n_out + len(p_out) + n_scr])
        cins = refs[n_in:n_in + len(p_in)]
        couts = refs[n_in + len(p_in) + n_out:n_in + len(p_in) + n_out + len(p_out)]
        sems = refs[n_in + len(p_in) + n_out + len(p_out) + n_scr:]
        if plan is not None:
            @pl.when(i == 0)
            def _():
                plan.start(cins, couts, sems)

        body(ri(i), i == 0, *own)
        if plan is not None:
            @pl.when(i == nt - 1)
            def _():
                plan.wait(cins, couts, sems)

    sem = ("arbitrary",) if (seq or accs or plan is not None) else ("parallel",)
    res = pl.pallas_call(
        kern, name=name, grid=(nt,), in_specs=in_specs + p_in, out_specs=out_specs + p_out,
        out_shape=out_shape + p_shapes, scratch_shapes=list(scratch) + p_scr,
        compiler_params=pltpu.CompilerParams(dimension_semantics=sem),
    )(*args, *(plan.ins if plan is not None else []))
    return res if plan is None else (res[:n_out], res[n_out:])


def _acc_add(first, ref, val):
    @pl.when(first)
    def _():
        ref[...] = jnp.zeros_like(ref)

    ref[...] += val


def _real_rows(r, tm):
    return (r * tm + _row_iota(tm)) >= PAD


def _rmsnorm_fwd(h, g, name):
    lp, d = h.shape
    tm = _pick(lp, (640, 512, 256, 128))

    def body(r, first, h_ref, g_ref, u_ref):
        x = h_ref[...]
        rs = lax.rsqrt(jnp.mean(x * x, axis=-1, keepdims=True) + EPS)
        u_ref[...] = (x * rs * g_ref[...]).astype(u_ref.dtype)

    return _rowcall(name, body, lp, tm, rows=[(h, d, 0)], vecs=[g], outs=[(d, BF16)])[0]


def _mm_postnorm_res(a, w, h, g, name, next_pre=None):
    parts = a if isinstance(a, (tuple, list)) else (a,)
    lp, d = h.shape
    k = w.shape[0]
    tm = _pick(lp, (640, 512, 256, 128))
    offs = [sum(p.shape[1] for p in parts[:i]) for i in range(len(parts))]
    np_ = len(parts)

    def body(*refs):
        w_ref, h_ref, g_ref = refs[np_], refs[np_ + 1], refs[np_ + 2]
        rest = refs[np_ + 3:]
        acc = None
        for a_ref, p, off in zip(refs, parts, offs):
            t = _dot(a_ref[...].astype(BF16), w_ref[off:off + p.shape[1], :])
            acc = t if acc is None else acc + t
        outs = rest[1:] if next_pre is not None else rest
        outs[0][...] = acc
        rs = lax.rsqrt(jnp.mean(acc * acc, axis=-1, keepdims=True) + EPS)
        hn = jnp.where(_real_rows(pl.program_id(0), tm), h_ref[...] + acc * rs * g_ref[...], 0.0)
        outs[1][...] = hn
        if next_pre is not None:
            rs2 = lax.rsqrt(jnp.mean(hn * hn, axis=-1, keepdims=True) + EPS)
            outs[2][...] = (hn * rs2 * rest[0][...]).astype(BF16)

    row = pl.BlockSpec((tm, d), lambda i: (i, 0))
    vec = pl.BlockSpec((1, d), lambda i: (0, 0))
    n_vec = 2 if next_pre is not None else 1
    return pl.pallas_call(
        body, name=name, grid=(lp // tm,),
        in_specs=[pl.BlockSpec((tm, p.shape[1]), lambda i: (i, 0)) for p in parts]
        + [pl.BlockSpec((k, d), lambda i: (0, 0)), row] + [vec] * n_vec,
        out_specs=[row] * (2 + (next_pre is not None)),
        out_shape=[jax.ShapeDtypeStruct((lp, d), F32)] * 2
        + ([jax.ShapeDtypeStruct((lp, d), BF16)] if next_pre is not None else []),
        compiler_params=pltpu.CompilerParams(dimension_semantics=("parallel",)),
    )(*parts, w, h, g, *([next_pre] if next_pre is not None else []))


def _mm_postnorm_loss(a, w, h, g, tgt, name):
    lp, d = h.shape
    k = w.shape[0]
    tm = _pick(lp, (640, 512, 256, 128))

    def body(a_ref, w_ref, h_ref, g_ref, t_ref, o_ref, dh_ref, ls_ref):
        i = pl.program_id(0)
        acc = _dot(a_ref[...].astype(BF16), w_ref[...])
        o_ref[...] = acc
        rs = lax.rsqrt(jnp.mean(acc * acc, axis=-1, keepdims=True) + EPS)
        tok = (i * tm + _row_iota(tm)) >= BLOCK
        e = jnp.where(tok, h_ref[...] + acc * rs * g_ref[...] - t_ref[...], 0.0)
        dh_ref[...] = e * (1.0 / d)
        _acc_add(i == 0, ls_ref, jnp.sum(e * e, axis=0, keepdims=True))

    row = pl.BlockSpec((tm, d), lambda i: (i, 0))
    vec = pl.BlockSpec((1, d), lambda i: (0, 0))
    return pl.pallas_call(
        body, name=name, grid=(lp // tm,),
        in_specs=[pl.BlockSpec((tm, k), lambda i: (i, 0)), pl.BlockSpec((k, d), lambda i: (0, 0)),
                  row, vec, row],
        out_specs=[row, row, vec],
        out_shape=[jax.ShapeDtypeStruct((lp, d), F32), jax.ShapeDtypeStruct((lp, d), F32),
                   jax.ShapeDtypeStruct((1, d), F32)],
        compiler_params=pltpu.CompilerParams(dimension_semantics=("arbitrary",)),
    )(a, w, h, g, tgt)


def _conv_tiles(lp, width):
    wc = _col_tile(width, 1408)
    tm = _pick(lp, (320, 256, 128))
    return tm, wc


def _conv_bwd(x, col_off, width, dy, w, name, w_col_off=0):
    lp = x.shape[0]
    kk = w.shape[0]
    tm, wc = _conv_tiles(lp, width)
    offb = col_off // wc
    woffb = w_col_off // wc
    assert col_off % wc == 0 and w_col_off % wc == 0
    hrows = SUBLANES * (4 // dy.dtype.itemsize)
    ext = tm + hrows

    def body(x_ref, dy_ref, dn_ref, w_ref, dx_ref, dw_ref, db_ref):
        i = pl.program_id(1)
        last = pl.num_programs(1) - 1
        xv = x_ref[...]
        dyv = dy_ref[...].astype(F32)
        dd = jnp.concatenate([dyv, jnp.where(i < last, dn_ref[...].astype(F32), 0.0)], axis=0)
        dx = w_ref[kk - 1:kk, :] * dyv
        rows = [jnp.sum(dyv * xv, axis=0, keepdims=True)]
        for m in range(1, kk):
            ahead = pltpu.roll(dd, ext - m, 0)[:tm, :]
            dx = dx + w_ref[kk - 1 - m:kk - m, :] * ahead
            rows.append(jnp.sum(ahead * xv, axis=0, keepdims=True))
        dx_ref[...] = dx.astype(dx_ref.dtype)
        dwp = jnp.concatenate(rows[::-1] + [jnp.zeros((SUBLANES - kk, wc), F32)], axis=0)

        @pl.when(i == 0)
        def _():
            dw_ref[...] = jnp.zeros_like(dw_ref)
            db_ref[...] = jnp.zeros_like(db_ref)

        dw_ref[...] += dwp
        db_ref[...] += jnp.sum(dyv, axis=0, keepdims=True)

    return pl.pallas_call(
        body, name=name, grid=(width // wc, lp // tm),
        in_specs=[pl.BlockSpec((tm, wc), lambda j, i: (i, offb + j)),
                  pl.BlockSpec((tm, wc), lambda j, i: (i, j)),
                  pl.BlockSpec((hrows, wc), lambda j, i: (jnp.minimum((i + 1) * (tm // hrows), lp // hrows - 1), j)),
                  pl.BlockSpec((kk, wc), lambda j, i: (0, woffb + j))],
        out_specs=[pl.BlockSpec((tm, wc), lambda j, i: (i, j)),
                   pl.BlockSpec((SUBLANES, wc), lambda j, i: (0, j)),
                   pl.BlockSpec((1, wc), lambda j, i: (0, j))],
        out_shape=[jax.ShapeDtypeStruct((lp, width), BF16),
                   jax.ShapeDtypeStruct((SUBLANES, width), F32),
                   jax.ShapeDtypeStruct((1, width), F32)],
        compiler_params=pltpu.CompilerParams(dimension_semantics=("parallel", "arbitrary")),
    )(x, dy, dy, w)


_FFN_WC = 1408


def _ffn_up_convact(u, w4, cw, cb, name, plan=None):
    lp, k = u.shape
    s, _, n = w4.shape
    tm = _pick(lp, (320, 256, 128))
    ni = lp // tm
    p_in, p_shapes, p_out, p_scr = _plan_parts(plan)

    def body(*refs):
        u_ref, w_ref, cw_ref, cb_ref = refs[:4]
        cins = refs[4:4 + len(p_in)]
        hp_ref, a_ref, hg_ref, hu_ref = refs[4 + len(p_in):8 + len(p_in)]
        couts = refs[8 + len(p_in):8 + len(p_in) + len(p_out)]
        carry = refs[8 + len(p_in) + len(p_out)]
        sems = refs[9 + len(p_in) + len(p_out):]
        i = pl.program_id(0)

        @pl.when(i == 0)
        def _():
            carry[...] = jnp.zeros_like(carry)
            if plan is not None:
                plan.start(cins, couts, sems)

        uv = u_ref[...].astype(BF16)
        for j in range(s):
            hp_ref[:, j * n:(j + 1) * n] = _dot(uv, w_ref[j])

        def conv(cols):
            x_ext = jnp.concatenate([carry[:, cols], hp_ref[:, cols]], axis=0)
            w = cw_ref[:, cols]
            y = (cb_ref[:, cols] + w[2:3, :] * x_ext + w[1:2, :] * pltpu.roll(x_ext, 1, 0)
                 + w[0:1, :] * pltpu.roll(x_ext, 2, 0))
            return y[SUBLANES:, :]

        for c in range(D_FF // n):
            gs = slice(c * n, (c + 1) * n)
            hg = conv(gs)
            hu = conv(slice(D_FF + c * n, D_FF + (c + 1) * n))
            a_ref[:, gs] = (_gelu(hg) * hu).astype(a_ref.dtype)
            hg_ref[:, gs] = hg.astype(hg_ref.dtype)
            hu_ref[:, gs] = hu.astype(hu_ref.dtype)
        carry[...] = hp_ref[tm - SUBLANES:tm, :]
        if plan is not None:
            @pl.when(i == ni - 1)
            def _():
                plan.wait(cins, couts, sems)

    half = pl.BlockSpec((tm, D_FF), lambda i: (i, 0))
    res = pl.pallas_call(
        body, name=name, grid=(ni,),
        in_specs=[pl.BlockSpec((tm, k), lambda i: (i, 0)), pl.BlockSpec((s, k, n), lambda i: (0, 0, 0)),
                  pl.BlockSpec(cw.shape, lambda i: (0, 0)), pl.BlockSpec(cb.shape, lambda i: (0, 0))] + p_in,
        out_specs=[pl.BlockSpec((tm, 2 * D_FF), lambda i: (i, 0)), half, half, half] + p_out,
        out_shape=[jax.ShapeDtypeStruct((lp, 2 * D_FF), F32)] + [jax.ShapeDtypeStruct((lp, D_FF), BF16)] * 3
        + p_shapes,
        scratch_shapes=[pltpu.VMEM((SUBLANES, 2 * D_FF), F32)] + p_scr,
        compiler_params=pltpu.CompilerParams(dimension_semantics=("arbitrary",),
                                             vmem_limit_bytes=_VMEM_LIMIT_WIDE),
    )(u, w4, cw, cb, *(plan.ins if plan is not None else []))
    return res[:4], res[4:]


def _postnorm_bwd_tile(o_ref, gain_ref, dh_ref, do_ref, dgain_ref):
    first = pl.program_id(0) == 0
    dx, dgt = _rms_bwd(o_ref[...], gain_ref[...], dh_ref[...])
    dob = dx.astype(BF16)
    do_ref[...] = dob
    _acc_add(first, dgain_ref, jnp.sum(dgt, axis=0, keepdims=True))
    return dob


def _ffn_down_act_bwd(o, gain, dh, w_down, hg, hu, name):
    lp, d = o.shape
    tm = _pick(lp, (320, 256, 128))
    tk = _FFN_WC

    def body(o_ref, gain_ref, dh_ref, w_ref, g_ref, u_ref, dg_ref, du_ref, do_ref, dgain_ref):
        dob = _postnorm_bwd_tile(o_ref, gain_ref, dh_ref, do_ref, dgain_ref)
        for j in range(D_FF // tk):
            cs = slice(j * tk, (j + 1) * tk)
            da = _dot_nt(dob, w_ref[cs, :])
            gl, dgl = _gelu_and_grad(g_ref[:, cs].astype(F32))
            dg_ref[:, cs] = (da * u_ref[:, cs].astype(F32) * dgl).astype(dg_ref.dtype)
            du_ref[:, cs] = (da * gl).astype(du_ref.dtype)

    wide = pl.BlockSpec((tm, D_FF), lambda i: (i, 0))
    row = pl.BlockSpec((tm, d), lambda i: (i, 0))
    vec = pl.BlockSpec((1, d), lambda i: (0, 0))
    return pl.pallas_call(
        body, name=name, grid=(lp // tm,),
        in_specs=[row, vec, row, pl.BlockSpec((D_FF, d), lambda i: (0, 0)), wide, wide],
        out_specs=[wide, wide, row, vec],
        out_shape=[jax.ShapeDtypeStruct((lp, D_FF), BF16)] * 2
        + [jax.ShapeDtypeStruct((lp, d), BF16), jax.ShapeDtypeStruct((1, d), F32)],
        compiler_params=pltpu.CompilerParams(dimension_semantics=("arbitrary",),
                                             vmem_limit_bytes=_VMEM_LIMIT_WIDE),
    )(o, gain, dh, w_down, hg, hu)


def _lru_gates(x, wa_ref, wx_ref, ba, bx, lam):
    xb = x.astype(BF16)
    za, zx = [], []
    for n in range(LRU_BLOCKS):
        xs = xb[:, n * LRU_BS:(n + 1) * LRU_BS]
        za.append(_dot(xs, wa_ref[n]))
        zx.append(_dot(xs, wx_ref[n]))
    r = _sigmoid(jnp.concatenate(za, axis=1) + ba)
    ig = _sigmoid(jnp.concatenate(zx, axis=1) + bx)
    sp = _softplus(-lam)
    log_a = -LRU_C * r * sp
    a = jnp.exp(log_a)
    om = _neg_expm1(2.0 * log_a)
    mult = jnp.sqrt(om)
    return xb, r, ig, sp, a, om, mult


def _lru_fwd(proj, xrc, wa, wx, ba, bx, lam, name, plan=None):
    lp, d = xrc.shape
    tm = BLOCK

    def body(r_idx, first, gate_ref, x_ref, wa_ref, wx_ref, ba_ref, bx_ref, lam_ref,
             y_ref, h_ref, carry):
        @pl.when(first)
        def _():
            carry[...] = jnp.zeros_like(carry)

        x = x_ref[...]
        _, _, ig, _, a, _, mult = _lru_gates(x, wa_ref, wx_ref, ba_ref[...], bx_ref[...], lam_ref[...])
        u = jnp.where(_real_rows(r_idx, tm), mult * ig * x, 0.0)
        acum, hloc = _scan_fwd(a, u, tm)
        h = hloc + acum * carry[0:1, :]
        h_ref[...] = h
        carry[0:1, :] = h[tm - 1:tm, :]
        y_ref[...] = (_gelu(gate_ref[...]) * h).astype(y_ref.dtype)

    return _rowcall(name, body, lp, tm, rows=[(proj, d, 0), (xrc, d, 0)],
                    vecs=[wa, wx, ba, bx, lam], outs=[(d, BF16), (d, F32)],
                    scratch=[pltpu.VMEM((SUBLANES, d), F32)], seq=True, plan=plan)


def _lru_bwd(proj, xrc, hl, dmix, wa, wx, ba, bx, lam, name, plan=None):
    lp, d = xrc.shape
    tm = BLOCK

    def body(r_idx, first, gate_ref, x_ref, h_ref, dy_ref, hp_ref, wa_ref, wx_ref, ba_ref, bx_ref,
             lam_ref, dgate_ref, dx_ref, dwa_ref, dwx_ref, dba_ref, dbx_ref, dlam_ref, carry):
        @pl.when(first)
        def _():
            carry[...] = jnp.zeros_like(carry)
            dwa_ref[...] = jnp.zeros_like(dwa_ref)
            dwx_ref[...] = jnp.zeros_like(dwx_ref)
            dba_ref[...] = jnp.zeros_like(dba_ref)
            dbx_ref[...] = jnp.zeros_like(dbx_ref)
            dlam_ref[...] = jnp.zeros_like(dlam_ref)

        x = x_ref[...]
        lam = lam_ref[...]
        xb, r, ig, sp, a, om, mult = _lru_gates(x, wa_ref, wx_ref, ba_ref[...], bx_ref[...], lam)
        h = h_ref[...]
        dy = dy_ref[...]
        gl, dgl = _gelu_and_grad(gate_ref[...])
        dgate_ref[...] = (dy * h * dgl).astype(dgate_ref.dtype)
        row = _row_iota(tm)
        lastrow = row == tm - 1
        xg = dy * gl + jnp.where(lastrow, carry[0:1, :], 0.0)
        c = jnp.where(lastrow, 1.0, pltpu.roll(a, tm - 1, 0))
        _, g = _scan_rev(c, xg, tm)
        carry[0:1, :] = a[0:1, :] * g[0:1, :]
        hprev_in = jnp.where(r_idx > 0, hp_ref[SUBLANES - 1:SUBLANES, :], 0.0)
        hprev = jnp.where(row == 0, hprev_in, pltpu.roll(h, 1, 0))
        du = jnp.where(_real_rows(r_idx, tm), g, 0.0)
        da = g * hprev
        dmult = du * ig * x
        dig = du * mult * x
        dxv = du * mult * ig
        e2 = 1.0 - om
        dlog_a = da * a - dmult * e2 / mult
        dr = dlog_a * (-LRU_C) * sp
        dsp = jnp.sum(dlog_a * (-LRU_C) * r, axis=0, keepdims=True)
        dlam_ref[...] += -dsp * _sigmoid(-lam)
        dza = dr * r * (1.0 - r)
        dzx = dig * ig * (1.0 - ig)
        dba_ref[...] += jnp.sum(dza, axis=0, keepdims=True)
        dbx_ref[...] += jnp.sum(dzx, axis=0, keepdims=True)
        dzab = dza.astype(BF16)
        dzxb = dzx.astype(BF16)
        parts = []
        for n in range(LRU_BLOCKS):
            sl = slice(n * LRU_BS, (n + 1) * LRU_BS)
            dwa_ref[n] += _dot_tn(xb[:, sl], dzab[:, sl])
            dwx_ref[n] += _dot_tn(xb[:, sl], dzxb[:, sl])
            parts.append(_dot_nt(dzab[:, sl], wa_ref[n]) + _dot_nt(dzxb[:, sl], wx_ref[n]))
        dx_ref[...] = dxv + jnp.concatenate(parts, axis=1)

    return _rowcall(name, body, lp, tm,
                    rows=[(proj, d, 0), (xrc, d, 0), (hl, d, 0), (dmix, d, 0)],
                    prevs=[(hl, d, 0)], vecs=[wa, wx, ba, bx, lam],
                    outs=[(d, BF16), (d, F32)],
                    accs=[((LRU_BLOCKS, LRU_BS, LRU_BS), F32), ((LRU_BLOCKS, LRU_BS, LRU_BS), F32),
                          ((1, d), F32), ((1, d), F32), ((1, d), F32)],
                    scratch=[pltpu.VMEM((SUBLANES, d), F32)], reverse=True, seq=True, plan=plan)


_SLOPES = [2.0 ** (-8.0 * (h + 1) / N_Q_HEADS) for h in range(N_Q_HEADS)]
_QK_SCALE = HEAD_DIM ** -0.5
_QCOL = 2 * D_MODEL // D_MODEL
_KCOL = (3 * D_MODEL) // LANES
_VCOL = _KCOL + 1


def _attn_masks(n):
    start = pl.multiple_of(jnp.maximum(n - 1, 0) * BLOCK, BLOCK)
    qi = n * BLOCK + lax.broadcasted_iota(jnp.int32, (BLOCK, 2 * BLOCK), 0)
    kj = start + lax.broadcasted_iota(jnp.int32, (BLOCK, 2 * BLOCK), 1)
    dist = qi - kj
    ok = (kj >= BLOCK) & (dist >= 0) & (dist < WINDOW)
    dm = (n * BLOCK - PAD + lax.broadcasted_iota(jnp.int32, (BLOCK, N_META), 0)
          - lax.broadcasted_iota(jnp.int32, (BLOCK, N_META), 1))
    okm = dm >= 0
    return start, ok, dist.astype(F32), okm, jnp.minimum(dm, WINDOW).astype(F32)


def _group_rows(ref, g, scale=None):
    x = jnp.concatenate(
        [ref[:, (g * Q_PER_KV + hh) * HEAD_DIM:(g * Q_PER_KV + hh + 1) * HEAD_DIM] for hh in range(Q_PER_KV)],
        axis=0)
    return (x if scale is None else x * scale).astype(BF16)


def _attn_probs(s, sm, sink_ref, g, ok, distf, okm, dmf):
    slope = jnp.stack([jnp.full((1, 1), _SLOPES[g * Q_PER_KV + hh], F32) for hh in range(Q_PER_KV)])
    sink = jnp.stack([sink_ref[0:1, g * Q_PER_KV + hh:g * Q_PER_KV + hh + 1] for hh in range(Q_PER_KV)])
    s = s.reshape(Q_PER_KV, BLOCK, 2 * BLOCK)
    sm = sm.reshape(Q_PER_KV, BLOCK, N_META)
    s = jnp.where(ok[None], s - slope * distf[None], NEG)
    sm = jnp.where(okm[None], sm - slope * dmf[None], NEG)
    mx = jnp.maximum(jnp.maximum(jnp.max(s, axis=-1, keepdims=True),
                                 jnp.max(sm, axis=-1, keepdims=True)), sink)
    p = jnp.exp(s - mx)
    pm = jnp.exp(sm - mx)
    ps = jnp.exp(sink - mx)
    inv = 1.0 / (jnp.sum(p, axis=-1, keepdims=True) + jnp.sum(pm, axis=-1, keepdims=True) + ps)
    return p, pm, ps, inv


def _attn_fwd(proj, sinks, name, plan=None):
    lp = proj.shape[0]
    nblk = lp // BLOCK
    p_in, p_shapes, p_out, p_scr = _plan_parts(plan)

    def body(*refs):
        q_ref, k_ref, v_ref, sink_ref = refs[:4]
        cins = refs[4:4 + len(p_in)]
        o_ref = refs[4 + len(p_in)]
        couts = refs[5 + len(p_in):5 + len(p_in) + len(p_out)]
        sems = refs[5 + len(p_in) + len(p_out):]
        n = pl.program_id(0)
        if plan is not None:
            @pl.when(n == 0)
            def _():
                plan.start(cins, couts, sems)

        start, ok, distf, okm, dmf = _attn_masks(n)
        kb = k_ref[pl.ds(start, 2 * BLOCK), :].astype(BF16)
        vb = v_ref[pl.ds(start, 2 * BLOCK), :].astype(BF16)
        km = k_ref[PAD:BLOCK, :].astype(BF16)
        vm = v_ref[PAD:BLOCK, :].astype(BF16)
        rows = Q_PER_KV * BLOCK
        gsl = [slice(g * HEAD_DIM, (g + 1) * HEAD_DIM) for g in range(N_KV_HEADS)]
        raw = []
        for g in range(N_KV_HEADS):
            qg = _group_rows(q_ref, g, _QK_SCALE)
            raw.append((_dot_nt(qg, kb[:, gsl[g]]), _dot_nt(qg, km[:, gsl[g]])))
        for g in range(N_KV_HEADS):
            gs = gsl[g]
            p, pm, _, inv = _attn_probs(raw[g][0], raw[g][1], sink_ref, g, ok, distf, okm, dmf)
            o = (_dot(p.astype(BF16).reshape(rows, 2 * BLOCK), vb[:, gs])
                 + _dot(pm.astype(BF16).reshape(rows, N_META), vm[:, gs])) * inv.reshape(rows, 1)
            for hh in range(Q_PER_KV):
                h = g * Q_PER_KV + hh
                o_ref[:, h * HEAD_DIM:(h + 1) * HEAD_DIM] = o[hh * BLOCK:(hh + 1) * BLOCK, :].astype(o_ref.dtype)
        if plan is not None:
            @pl.when(n == nblk - 1)
            def _():
                plan.wait(cins, couts, sems)

    res = pl.pallas_call(
        body, name=name, grid=(nblk,),
        in_specs=[pl.BlockSpec((BLOCK, D_MODEL), lambda n: (n, _QCOL)),
                  pl.BlockSpec((lp, LANES), lambda n: (0, _KCOL)),
                  pl.BlockSpec((lp, LANES), lambda n: (0, _VCOL)),
                  pl.BlockSpec(sinks.shape, lambda n: (0, 0))] + p_in,
        out_specs=[pl.BlockSpec((BLOCK, D_MODEL), lambda n: (n, 0))] + p_out,
        out_shape=[jax.ShapeDtypeStruct((lp, D_MODEL), BF16)] + p_shapes,
        scratch_shapes=p_scr,
        compiler_params=pltpu.CompilerParams(dimension_semantics=("arbitrary",)),
    )(proj, proj, proj, sinks, *(plan.ins if plan is not None else []))
    return res[0], res[1:]


def _attn_bwd(proj, sinks, dmix, name, plan=None):
    lp = proj.shape[0]
    nblk = lp // BLOCK

    p_in, p_shapes, p_out, p_scr = _plan_parts(plan)

    def body(*refs):
        q_ref, k_ref, v_ref, sink_ref, dy_ref = refs[:5]
        cins = refs[5:5 + len(p_in)]
        dq_ref, dk_ref, dv_ref, ds_ref = refs[5 + len(p_in):9 + len(p_in)]
        couts = refs[9 + len(p_in):9 + len(p_in) + len(p_out)]
        sems = refs[9 + len(p_in) + len(p_out):]
        n = pl.program_id(0)

        @pl.when(n == 0)
        def _():
            dk_ref[...] = jnp.zeros_like(dk_ref)
            dv_ref[...] = jnp.zeros_like(dv_ref)
            ds_ref[...] = jnp.zeros_like(ds_ref)
            if plan is not None:
                plan.start(cins, couts, sems)

        start, ok, distf, okm, dmf = _attn_masks(n)
        kb = k_ref[pl.ds(start, 2 * BLOCK), :].astype(BF16)
        vb = v_ref[pl.ds(start, 2 * BLOCK), :].astype(BF16)
        km = k_ref[PAD:BLOCK, :].astype(BF16)
        vm = v_ref[PAD:BLOCK, :].astype(BF16)
        lane16 = lax.broadcasted_iota(jnp.int32, (1, N_Q_HEADS), 1)
        dsink = jnp.zeros((1, N_Q_HEADS), F32)
        rows = Q_PER_KV * BLOCK
        gsl = [slice(g * HEAD_DIM, (g + 1) * HEAD_DIM) for g in range(N_KV_HEADS)]
        pre = []
        for g in range(N_KV_HEADS):
            qg = _group_rows(q_ref, g, _QK_SCALE)
            dog = _group_rows(dy_ref, g)
            pre.append((qg, dog, _dot_nt(qg, kb[:, gsl[g]]), _dot_nt(qg, km[:, gsl[g]]),
                        _dot_nt(dog, vb[:, gsl[g]]), _dot_nt(dog, vm[:, gsl[g]])))
        for g in range(N_KV_HEADS):
            gs = gsl[g]
            qg, dog, s_raw, sm_raw, dp, dpm = pre[g]
            p, pm, ps, inv = _attn_probs(s_raw, sm_raw, sink_ref, g, ok, distf, okm, dmf)
            pn, pmn, psn = p * inv, pm * inv, ps * inv
            dp = dp.reshape(Q_PER_KV, BLOCK, 2 * BLOCK)
            dpm = dpm.reshape(Q_PER_KV, BLOCK, N_META)
            delta = (jnp.sum(pn * dp, axis=-1, keepdims=True)
                     + jnp.sum(pmn * dpm, axis=-1, keepdims=True))
            dsb = (pn * (dp - delta)).astype(BF16).reshape(rows, 2 * BLOCK)
            dsm = (pmn * (dpm - delta)).astype(BF16).reshape(rows, N_META)
            dsk = jnp.sum(psn * delta, axis=1, keepdims=True)
            for hh in range(Q_PER_KV):
                dsink = dsink - jnp.where(lane16 == g * Q_PER_KV + hh, dsk[hh], 0.0)
            dq = (_dot(dsb, kb[:, gs]) + _dot(dsm, km[:, gs])) * _QK_SCALE
            for hh in range(Q_PER_KV):
                h = g * Q_PER_KV + hh
                dq_ref[:, h * HEAD_DIM:(h + 1) * HEAD_DIM] = dq[hh * BLOCK:(hh + 1) * BLOCK, :].astype(dq_ref.dtype)
            pnb = pn.astype(BF16).reshape(rows, 2 * BLOCK)
            pmnb = pmn.astype(BF16).reshape(rows, N_META)
            dk_ref[pl.ds(start, 2 * BLOCK), gs] += _dot_tn(dsb, qg)
            dv_ref[pl.ds(start, 2 * BLOCK), gs] += _dot_tn(pnb, dog)
            dk_ref[PAD:BLOCK, gs] += _dot_tn(dsm, qg)
            dv_ref[PAD:BLOCK, gs] += _dot_tn(pmnb, dog)
        ds_ref[...] += dsink
        if plan is not None:
            @pl.when(n == nblk - 1)
            def _():
                plan.wait(cins, couts, sems)

    res = pl.pallas_call(
        body, name=name, grid=(nblk,),
        in_specs=[pl.BlockSpec((BLOCK, D_MODEL), lambda n: (n, _QCOL)),
                  pl.BlockSpec((lp, LANES), lambda n: (0, _KCOL)),
                  pl.BlockSpec((lp, LANES), lambda n: (0, _VCOL)),
                  pl.BlockSpec(sinks.shape, lambda n: (0, 0)),
                  pl.BlockSpec((BLOCK, D_MODEL), lambda n: (n, 1))] + p_in,
        out_specs=[pl.BlockSpec((BLOCK, D_MODEL), lambda n: (n, 0)),
                   pl.BlockSpec((lp, LANES), lambda n: (0, 0)),
                   pl.BlockSpec((lp, LANES), lambda n: (0, 0)),
                   pl.BlockSpec((1, N_Q_HEADS), lambda n: (0, 0))] + p_out,
        out_shape=[jax.ShapeDtypeStruct((lp, D_MODEL), BF16),
                   jax.ShapeDtypeStruct((lp, LANES), F32),
                   jax.ShapeDtypeStruct((lp, LANES), F32),
                   jax.ShapeDtypeStruct((1, N_Q_HEADS), F32)] + p_shapes,
        scratch_shapes=p_scr,
        compiler_params=pltpu.CompilerParams(dimension_semantics=("arbitrary",)),
    )(proj, proj, proj, sinks, dmix, *(plan.ins if plan is not None else []))
    return res[:4], res[4:]


_ZW = D_SSM
_XBC_W = D_SSM + 2 * SSD_GROUPS * SSD_N
_DT_COL = (_ZW + _XBC_W) // LANES
EVEN_IN = 3 * D_MODEL + 2 * LANES
ODD_IN = _ZW + _XBC_W + SSD_HEADS
ODD_IN_PAD = _ZW + _XBC_W + LANES


def _ssm_convprep_fwd(proj, cw, cb, dt_bias, name):
    lp = proj.shape[0]
    kk = cw.shape[0]
    tm, wc = _conv_tiles(lp, _XBC_W)
    offb = _ZW // wc
    nj = _XBC_W // wc
    hb = tm // SUBLANES

    def body(x_ref, xp_ref, dtr_ref, w_ref, b_ref, bias_ref, xc_ref, act_ref, dt_ref):
        i, j = pl.program_id(0), pl.program_id(1)
        real = _real_rows(i, tm)
        xv = x_ref[...]
        xx = jnp.concatenate([jnp.where(i > 0, xp_ref[...], 0.0), xv], axis=0)
        acc = b_ref[...] + w_ref[kk - 1:kk, :] * xv
        for m in range(1, kk):
            acc = acc + w_ref[kk - 1 - m:kk - m, :] * pltpu.roll(xx, m, 0)[SUBLANES:, :]
        xc_ref[...] = acc
        act, _ = _silu_and_grad(acc)
        act_ref[...] = jnp.where(real, act, 0.0)

        @pl.when(j == 0)
        def _():
            dt_ref[...] = jnp.where(real, _softplus(dtr_ref[...] + bias_ref[...]), 0.0)

    return pl.pallas_call(
        body, name=name, grid=(lp // tm, nj),
        in_specs=[pl.BlockSpec((tm, wc), lambda i, j: (i, offb + j)),
                  pl.BlockSpec((SUBLANES, wc), lambda i, j: (jnp.maximum(i * hb - 1, 0), offb + j)),
                  pl.BlockSpec((tm, LANES), lambda i, j: (i, _DT_COL)),
                  pl.BlockSpec((kk, wc), lambda i, j: (0, j)),
                  pl.BlockSpec((1, wc), lambda i, j: (0, j)),
                  pl.BlockSpec((1, LANES), lambda i, j: (0, 0))],
        out_specs=[pl.BlockSpec((tm, wc), lambda i, j: (i, j)),
                   pl.BlockSpec((tm, wc), lambda i, j: (i, j)),
                   pl.BlockSpec((tm, LANES), lambda i, j: (i, 0))],
        out_shape=[jax.ShapeDtypeStruct((lp, _XBC_W), F32), jax.ShapeDtypeStruct((lp, _XBC_W), F32),
                   jax.ShapeDtypeStruct((lp, LANES), F32)],
        compiler_params=pltpu.CompilerParams(dimension_semantics=("parallel", "arbitrary")),
    )(proj, proj, proj, cw, cb, dt_bias)


def _ssd_common(dt, alog):
    a = -jnp.exp(alog)
    cs = _cumsum_rows(dt * a, BLOCK)
    cst = cs.T
    cl = cs[BLOCK - 1:BLOCK, :]
    tril = (lax.broadcasted_iota(jnp.int32, (BLOCK, BLOCK), 0)
            >= lax.broadcasted_iota(jnp.int32, (BLOCK, BLOCK), 1))
    return a, cs, cst, cl, jnp.exp(cs), jnp.exp(cl - cs), jnp.exp(cl), tril


def _head_cols(ecl, g):
    lane = lax.broadcasted_iota(jnp.int32, (1, SSD_HPG * SSD_P), 1)
    e = [ecl[:, SSD_HPG * g + hh:SSD_HPG * g + hh + 1] for hh in range(SSD_HPG)]
    return jnp.where(lane < SSD_P, e[0], jnp.where(lane < 2 * SSD_P, e[1],
                                                   jnp.where(lane < 3 * SSD_P, e[2], e[3])))


def _ssd_fwd(xbc, dt, alog, name, plan=None):
    lp = xbc.shape[0]
    nc = lp // BLOCK
    gw = SSD_HPG * SSD_P
    p_in, p_shapes, p_out, p_scr = _plan_parts(plan)

    def body(*refs):
        xs_ref, b_ref, c_ref, dt_ref, alog_ref = refs[:5]
        cins = refs[5:5 + len(p_in)]
        y_ref, so_ref = refs[5 + len(p_in):7 + len(p_in)]
        couts = refs[7 + len(p_in):7 + len(p_in) + len(p_out)]
        st, fx = refs[7 + len(p_in) + len(p_out):9 + len(p_in) + len(p_out)]
        sems = refs[9 + len(p_in) + len(p_out):]
        n = pl.program_id(0)

        @pl.when(n == 0)
        def _():
            st[...] = jnp.zeros_like(st)
            if plan is not None:
                plan.start(cins, couts, sems)

        dtv = dt_ref[...]
        _, cs, cst, cl, e, f, ecl, tril = _ssd_common(dtv, alog_ref[...])
        pre = []
        for g in range(SSD_GROUPS):
            bg = b_ref[:, g * SSD_N:(g + 1) * SSD_N].astype(BF16)
            cg = c_ref[:, g * SSD_N:(g + 1) * SSD_N].astype(BF16)
            stg = st[g]
            so_ref[0, g] = stg
            pre.append((bg, stg, _dot_nt(cg, bg), _dot(cg, stg.astype(BF16))))
        for g in range(SSD_GROUPS):
            bg, stg, gm, yoff = pre[g]
            heads = [SSD_HPG * g + hh for hh in range(SSD_HPG)]
            cols = lambda v: jnp.stack([v[:, h:h + 1] for h in heads])
            x4 = jnp.stack([xs_ref[:, h * SSD_P:(h + 1) * SSD_P] for h in heads])
            csr = jnp.stack([cst[h:h + 1, :] for h in heads])
            m = gm[None] * jnp.exp(jnp.where(tril[None], cols(cs) - csr, NEG))
            xdt = x4 * cols(dtv)
            yoff4 = jnp.stack([yoff[:, hh * SSD_P:(hh + 1) * SSD_P] for hh in range(SSD_HPG)])
            y4 = (jnp.einsum("hls,hsp->hlp", m.astype(BF16), xdt.astype(BF16), preferred_element_type=F32)
                  + cols(e) * yoff4)
            fx4 = cols(f) * xdt
            for hh, h in enumerate(heads):
                y_ref[:, h * SSD_P:(h + 1) * SSD_P] = y4[hh]
                fx[:, hh * SSD_P:(hh + 1) * SSD_P] = fx4[hh]
            st[g] = stg * _head_cols(ecl, g) + _dot_tn(bg, fx[...].astype(BF16))
        if plan is not None:
            @pl.when(n == nc - 1)
            def _():
                plan.wait(cins, couts, sems)

    res = pl.pallas_call(
        body, name=name, grid=(nc,),
        in_specs=[pl.BlockSpec((BLOCK, D_SSM), lambda n: (n, 0)),
                  pl.BlockSpec((BLOCK, 1024), lambda n: (n, 2)),
                  pl.BlockSpec((BLOCK, 1024), lambda n: (n, 3)),
                  pl.BlockSpec((BLOCK, LANES), lambda n: (n, 0)),
                  pl.BlockSpec((1, LANES), lambda n: (0, 0))] + p_in,
        out_specs=[pl.BlockSpec((BLOCK, D_SSM), lambda n: (n, 0)),
                   pl.BlockSpec((1, SSD_GROUPS, SSD_N, gw), lambda n: (n, 0, 0, 0))] + p_out,
        out_shape=[jax.ShapeDtypeStruct((lp, D_SSM), F32),
                   jax.ShapeDtypeStruct((nc, SSD_GROUPS, SSD_N, gw), F32)] + p_shapes,
        scratch_shapes=[pltpu.VMEM((SSD_GROUPS, SSD_N, gw), F32), pltpu.VMEM((BLOCK, gw), F32)] + p_scr,
        compiler_params=pltpu.CompilerParams(dimension_semantics=("arbitrary",)),
    )(xbc, xbc, xbc, dt, alog, *(plan.ins if plan is not None else []))
    return res[:2], res[2:]


def _ssd_bwd(xbc, dt, alog, states, dy, xc, proj, dt_bias, dxskip, name, plan=None):
    lp = xbc.shape[0]
    nc = lp // BLOCK
    gw = SSD_HPG * SSD_P
    p_in, p_shapes, p_out, p_scr = _plan_parts(plan)

    def body(*refs):
        xs_ref, b_ref, c_ref, dt_ref, alog_ref, dy_ref, st_ref, xc_ref, dtr_ref, bias_ref, dsk_ref = refs[:11]
        cins = refs[11:11 + len(p_in)]
        dxc_ref, ddtr_ref, dalog_ref, dbias_ref = refs[11 + len(p_in):15 + len(p_in)]
        couts = refs[15 + len(p_in):15 + len(p_in) + len(p_out)]
        dst, edy, fx, gx = refs[15 + len(p_in) + len(p_out):19 + len(p_in) + len(p_out)]
        sems = refs[19 + len(p_in) + len(p_out):]
        i = pl.program_id(0)

        @pl.when(i == 0)
        def _():
            dst[...] = jnp.zeros_like(dst)
            dalog_ref[...] = jnp.zeros_like(dalog_ref)
            dbias_ref[...] = jnp.zeros_like(dbias_ref)
            if plan is not None:
                plan.start(cins, couts, sems)

        dtv = dt_ref[...]
        a, cs, cst, cl, e, f, ecl, tril = _ssd_common(dtv, alog_ref[...])
        lane = lax.broadcasted_iota(jnp.int32, (1, LANES), 1)
        sub = _row_iota(BLOCK)
        triu = (lax.broadcasted_iota(jnp.int32, (BLOCK, BLOCK), 1)
                >= lax.broadcasted_iota(jnp.int32, (BLOCK, BLOCK), 0))
        dcs = jnp.zeros((BLOCK, LANES), F32)
        dcst = jnp.zeros((LANES, BLOCK), F32)
        dcl = jnp.zeros((1, LANES), F32)
        ddtx = jnp.zeros((BLOCK, LANES), F32)
        pre = []
        for g in range(SSD_GROUPS):
            bg = b_ref[:, g * SSD_N:(g + 1) * SSD_N].astype(BF16)
            cg = c_ref[:, g * SSD_N:(g + 1) * SSD_N].astype(BF16)
            stb = st_ref[0, g].astype(BF16)
            dsob = dst[g].astype(BF16)
            pre.append((bg, cg, stb, dsob, _dot_nt(cg, bg), _dot_nt(bg, cg), _dot(cg, stb), _dot(bg, dsob)))
        for g in range(SSD_GROUPS):
            bg, cg, stb, dsob, gm, gmt, yraw, dfx = pre[g]
            dso = dst[g]
            prodsum = jnp.sum(dso * st_ref[0, g], axis=0, keepdims=True)
            heads = [SSD_HPG * g + hh for hh in range(SSD_HPG)]
            cols = lambda v: jnp.stack([v[:, h:h + 1] for h in heads])
            parts = lambda v: jnp.stack([v[:, hh * SSD_P:(hh + 1) * SSD_P] for hh in range(SSD_HPG)])
            x4 = jnp.stack([xs_ref[:, h * SSD_P:(h + 1) * SSD_P] for h in heads])
            dy4 = jnp.stack([dy_ref[:, h * SSD_P:(h + 1) * SSD_P] for h in heads])
            csc, dtc, ec, fc = cols(cs), cols(dtv), cols(e), cols(f)
            csr = jnp.stack([cst[h:h + 1, :] for h in heads])
            seg = csc - csr
            lam = jnp.exp(jnp.where(tril[None], seg, NEG))
            lamt = jnp.exp(jnp.where(triu[None], -seg, NEG))
            mt = gmt[None] * lamt
            xdt = x4 * dtc
            dyb = dy4.astype(BF16)
            dm = jnp.einsum("hlp,hsp->hls", dyb, xdt.astype(BF16), preferred_element_type=F32)
            dfx4 = parts(dfx)
            dxdt = jnp.einsum("hsl,hlp->hsp", mt.astype(BF16), dyb, preferred_element_type=F32) + fc * dfx4
            dml = dm * lam
            w = dml * gm[None]
            dgm = jnp.sum(dml, axis=0)
            dff = jnp.sum(dfx4 * xdt, axis=2, keepdims=True) * fc
            colv = (jnp.sum(w, axis=2, keepdims=True)
                    + jnp.sum(dy4 * parts(yraw), axis=2, keepdims=True) * ec - dff)
            roww = jnp.sum(w, axis=1, keepdims=True)
            ddtc = jnp.sum(dxdt * x4, axis=2, keepdims=True)
            dffs = jnp.sum(dff, axis=1, keepdims=True)
            dxs4 = dxdt * dtc
            edy4 = ec * dy4
            fx4 = fc * xdt
            for hh, h in enumerate(heads):
                ls = slice(hh * SSD_P, (hh + 1) * SSD_P)
                onl = (lane == h).astype(F32)
                dcs = dcs + colv[hh] * onl
                dcst = dcst - (sub == h).astype(F32) * roww[hh]
                dcl = dcl + (dffs[hh] + ecl[:, h:h + 1] * jnp.sum(prodsum[:, ls], axis=1, keepdims=True)) * onl
                ddtx = ddtx + ddtc[hh] * onl
                gx[:, h * SSD_P:(h + 1) * SSD_P] = dxs4[hh]
                edy[:, ls] = edy4[hh]
                fx[:, ls] = fx4[hh]
            edyb = edy[...].astype(BF16)
            fxb = fx[...].astype(BF16)
            dgb = dgm.astype(BF16)
            gx[:, D_SSM + 1024 + g * SSD_N:D_SSM + 1024 + (g + 1) * SSD_N] = _dot_nt(edyb, stb) + _dot(dgb, bg)
            gx[:, D_SSM + g * SSD_N:D_SSM + (g + 1) * SSD_N] = _dot_nt(fxb, dsob) + _dot_tn(dgb, cg)
            dst[g] = dso * _head_cols(ecl, g) + _dot_tn(cg, edyb)
        dcs = dcs + dcst.T + jnp.where(sub == BLOCK - 1, dcl, 0.0)
        dda = _rev_cumsum_rows(dcs, BLOCK)
        dalog_ref[...] += jnp.sum(dda * dtv, axis=0, keepdims=True) * a
        real = _real_rows(nc - 1 - i, BLOCK)
        _, ds = _silu_and_grad(xc_ref[...])
        dxc_ref[:, :D_SSM] = jnp.where(
            real, (gx[:, :D_SSM] + dsk_ref[...].astype(F32)) * ds[:, :D_SSM], 0.0).astype(dxc_ref.dtype)
        dxc_ref[:, D_SSM:] = jnp.where(real, gx[:, D_SSM:] * ds[:, D_SSM:], 0.0).astype(dxc_ref.dtype)
        dd = jnp.where(real, (ddtx + dda * a) * _sigmoid(dtr_ref[...] + bias_ref[...]), 0.0)
        ddtr_ref[...] = dd.astype(ddtr_ref.dtype)
        dbias_ref[...] += jnp.sum(dd, axis=0, keepdims=True)
        if plan is not None:
            @pl.when(i == nc - 1)
            def _():
                plan.wait(cins, couts, sems)

    rev = lambda i: nc - 1 - i
    res = pl.pallas_call(
        body, name=name, grid=(nc,),
        in_specs=[pl.BlockSpec((BLOCK, D_SSM), lambda i: (rev(i), 0)),
                  pl.BlockSpec((BLOCK, 1024), lambda i: (rev(i), 2)),
                  pl.BlockSpec((BLOCK, 1024), lambda i: (rev(i), 3)),
                  pl.BlockSpec((BLOCK, LANES), lambda i: (rev(i), 0)),
                  pl.BlockSpec((1, LANES), lambda i: (0, 0)),
                  pl.BlockSpec((BLOCK, D_SSM), lambda i: (rev(i), 0)),
                  pl.BlockSpec((1, SSD_GROUPS, SSD_N, gw), lambda i: (rev(i), 0, 0, 0)),
                  pl.BlockSpec((BLOCK, _XBC_W), lambda i: (rev(i), 0)),
                  pl.BlockSpec((BLOCK, LANES), lambda i: (rev(i), _DT_COL)),
                  pl.BlockSpec((1, LANES), lambda i: (0, 0)),
                  pl.BlockSpec((BLOCK, D_SSM), lambda i: (rev(i), 0))] + p_in,
        out_specs=[pl.BlockSpec((BLOCK, _XBC_W), lambda i: (rev(i), 0)),
                   pl.BlockSpec((BLOCK, LANES), lambda i: (rev(i), 0)),
                   pl.BlockSpec((1, LANES), lambda i: (0, 0)),
                   pl.BlockSpec((1, LANES), lambda i: (0, 0))] + p_out,
        out_shape=[jax.ShapeDtypeStruct((lp, _XBC_W), BF16),
                   jax.ShapeDtypeStruct((lp, LANES), BF16),
                   jax.ShapeDtypeStruct((1, LANES), F32),
                   jax.ShapeDtypeStruct((1, LANES), F32)] + p_shapes,
        scratch_shapes=[pltpu.VMEM((SSD_GROUPS, SSD_N, gw), F32),
                        pltpu.VMEM((BLOCK, gw), F32), pltpu.VMEM((BLOCK, gw), F32),
                        pltpu.VMEM((BLOCK, _XBC_W), F32)] + p_scr,
        compiler_params=pltpu.CompilerParams(dimension_semantics=("arbitrary",)),
    )(xbc, xbc, xbc, dt, alog, dy, states, xc, proj, dt_bias, dxskip,
      *(plan.ins if plan is not None else []))
    return res[:4], res[4:]


_GN_GROUPS = 8
_GN_W = D_SSM // _GN_GROUPS


def _ssm_gate_out_postnorm(yssd, xbc, proj, dskip, gnorm, w_out, h, g, next_pre, name):
    lp, d = h.shape
    tm = _pick(lp, (320, 256, 128))

    def body(y_ref, x_ref, z_ref, d_ref, gn_ref, w_ref, h_ref, g_ref, np_ref, yn_ref, o_ref, hn_ref, u_ref):
        sz, _ = _silu_and_grad(z_ref[...])
        y2 = (y_ref[...] + d_ref[...] * x_ref[...]) * sz
        for k in range(_GN_GROUPS):
            sl = slice(k * _GN_W, (k + 1) * _GN_W)
            yk = y2[:, sl]
            rs = lax.rsqrt(jnp.mean(yk * yk, axis=-1, keepdims=True) + EPS)
            yn_ref[:, sl] = (yk * rs * gn_ref[:, sl]).astype(yn_ref.dtype)
        acc = _dot(yn_ref[...], w_ref[...])
        o_ref[...] = acc
        rs = lax.rsqrt(jnp.mean(acc * acc, axis=-1, keepdims=True) + EPS)
        hn = jnp.where(_real_rows(pl.program_id(0), tm), h_ref[...] + acc * rs * g_ref[...], 0.0)
        hn_ref[...] = hn
        rs2 = lax.rsqrt(jnp.mean(hn * hn, axis=-1, keepdims=True) + EPS)
        u_ref[...] = (hn * rs2 * np_ref[...]).astype(u_ref.dtype)

    wide = pl.BlockSpec((tm, D_SSM), lambda i: (i, 0))
    row = pl.BlockSpec((tm, d), lambda i: (i, 0))
    vec = lambda n: pl.BlockSpec((1, n), lambda i: (0, 0))
    return pl.pallas_call(
        body, name=name, grid=(lp // tm,),
        in_specs=[wide, wide, wide, vec(D_SSM), vec(D_SSM), pl.BlockSpec((D_SSM, d), lambda i: (0, 0)),
                  row, vec(d), vec(d)],
        out_specs=[wide, row, row, row],
        out_shape=[jax.ShapeDtypeStruct((lp, D_SSM), BF16), jax.ShapeDtypeStruct((lp, d), F32),
                   jax.ShapeDtypeStruct((lp, d), F32), jax.ShapeDtypeStruct((lp, d), BF16)],
        compiler_params=pltpu.CompilerParams(dimension_semantics=("parallel",),
                                             vmem_limit_bytes=_VMEM_LIMIT_WIDE),
    )(yssd, xbc, proj, dskip, gnorm, w_out, h, g, next_pre)


def _ssm_out_gate_bwd(o, gain, dh, w_out, yssd, xbc, proj, dskip, gnorm, name):
    lp, d = o.shape
    tm = _pick(lp, (320, 256, 128))
    tk = D_SSM // 2

    def body(o_ref, gain_ref, dh_ref, w_ref, y_ref, x_ref, z_ref, d_ref, g_ref,
             dy_ref, dx_ref, dz_ref, dd_ref, dg_ref, do_ref, dgain_ref):
        first = pl.program_id(0) == 0
        dob = _postnorm_bwd_tile(o_ref, gain_ref, dh_ref, do_ref, dgain_ref)
        z = z_ref[...]
        sz, dsz = _silu_and_grad(z)
        xs = x_ref[...]
        y1 = y_ref[...] + d_ref[...] * xs
        y2 = y1 * sz
        for k in range(_GN_GROUPS):
            sl = slice(k * _GN_W, (k + 1) * _GN_W)
            if k % (tk // _GN_W) == 0:
                dyn = _dot_nt(dob, w_ref[k * _GN_W:k * _GN_W + tk, :])
            loc = slice((k % (tk // _GN_W)) * _GN_W, (k % (tk // _GN_W) + 1) * _GN_W)
            dx, dgt = _rms_bwd(y2[:, sl], g_ref[:, sl], dyn[:, loc])
            dy1 = dx * sz[:, sl]
            dy_ref[:, sl] = dy1.astype(dy_ref.dtype)
            dx_ref[:, sl] = (dy1 * d_ref[:, sl]).astype(dx_ref.dtype)
            dz_ref[:, sl] = (dx * y1[:, sl] * dsz[:, sl]).astype(dz_ref.dtype)

            @pl.when(first)
            def _():
                dd_ref[:, sl] = jnp.zeros((1, _GN_W), F32)
                dg_ref[:, sl] = jnp.zeros((1, _GN_W), F32)

            dd_ref[:, sl] += jnp.sum(dy1 * xs[:, sl], axis=0, keepdims=True)
            dg_ref[:, sl] += jnp.sum(dgt, axis=0, keepdims=True)

    tile = pl.BlockSpec((tm, D_SSM), lambda i: (i, 0))
    vec = pl.BlockSpec((1, D_SSM), lambda i: (0, 0))
    row = pl.BlockSpec((tm, d), lambda i: (i, 0))
    rvec = pl.BlockSpec((1, d), lambda i: (0, 0))
    return pl.pallas_call(
        body, name=name, grid=(lp // tm,),
        in_specs=[row, rvec, row, pl.BlockSpec((D_SSM, d), lambda i: (0, 0)), tile, tile, tile, vec, vec],
        out_specs=[tile, tile, tile, vec, vec, row, rvec],
        out_shape=[jax.ShapeDtypeStruct((lp, D_SSM), BF16)] * 3 + [jax.ShapeDtypeStruct((1, D_SSM), F32)] * 2
        + [jax.ShapeDtypeStruct((lp, d), BF16), jax.ShapeDtypeStruct((1, d), F32)],
        compiler_params=pltpu.CompilerParams(dimension_semantics=("arbitrary",),
                                             vmem_limit_bytes=_VMEM_LIMIT_WIDE),
    )(o, gain, dh, w_out, yssd, xbc, proj, dskip, gnorm)


def _shape2d(shape):
    n = math.prod(shape)
    if len(shape) == 2:
        return tuple(shape)
    return (n // LANES, LANES) if n % LANES == 0 else (1, n)


def _adamw_many(ws, gs, ms, vs, name):
    n = len(ws)
    c1 = 1.0 / (1.0 - ADAM_B1 ** ADAM_STEP)
    c2 = 1.0 / (1.0 - ADAM_B2 ** ADAM_STEP)

    def body(*refs):
        for i in range(n):
            w_ref, g_ref, m_ref, v_ref = (refs[j * n + i] for j in range(4))
            d_ref, nm_ref, nv_ref = (refs[(4 + j) * n + i] for j in range(3))
            gv = g_ref[...]
            nm = ADAM_B1 * m_ref[...] + (1.0 - ADAM_B1) * gv
            nv = ADAM_B2 * v_ref[...] + (1.0 - ADAM_B2) * (gv * gv)
            nm_ref[...] = nm
            nv_ref[...] = nv
            d_ref[...] = -ADAM_LR * ((nm * c1) / (jnp.sqrt(nv * c2) + ADAM_EPS) + ADAM_WD * w_ref[...])

    vm = pl.BlockSpec(memory_space=pltpu.VMEM)
    return pl.pallas_call(
        body, name=name, in_specs=[vm] * (4 * n), out_specs=[vm] * (3 * n),
        out_shape=[jax.ShapeDtypeStruct(w.shape, F32) for w in ws] * 3,
    )(*ws, *gs, *ms, *vs)


def _place():
    return lax.axis_index("x"), lax.axis_index("y"), lax.axis_index("c")


def _other_chips(x, y):
    return [(1 - x, y), (x, 1 - y), (1 - x, 1 - y)]


_ANY = pl.BlockSpec(memory_space=pl.ANY)


class _Plan:
    def __init__(self, ins, out_shapes, n_remote, n_local, issue):
        self.ins = list(ins)
        self.out_shapes = list(out_shapes)
        self.issue = issue
        self.scratch = [pltpu.SemaphoreType.DMA((max(n_remote, 1),)),
                        pltpu.SemaphoreType.DMA((max(n_remote, 1),)),
                        pltpu.SemaphoreType.DMA((max(n_local, 1),))]

    def start(self, ins, outs, sems):
        sends, _, locs = self.issue(ins, outs, *sems)
        for cp in locs + sends:
            cp.start()

    def wait(self, ins, outs, sems):
        sends, recvs, locs = self.issue(ins, outs, *sems)
        for make in recvs:
            make().wait_recv()
        for cp in sends:
            cp.wait_send()
        for cp in locs:
            cp.wait()


def _plan_parts(plan):
    if plan is None:
        return [], [], [], []
    return ([_ANY] * len(plan.ins), plan.out_shapes, [_ANY] * len(plan.out_shapes), plan.scratch)


def _run_plan(plan, name):
    n_in, n_out = len(plan.ins), len(plan.out_shapes)

    def body(*refs):
        ins, outs, sems = refs[:n_in], refs[n_in:n_in + n_out], refs[n_in + n_out:]
        plan.start(ins, outs, sems)
        plan.wait(ins, outs, sems)

    return pl.pallas_call(
        body, name=name, in_specs=[_ANY] * n_in, out_specs=[_ANY] * n_out,
        out_shape=plan.out_shapes, scratch_shapes=plan.scratch,
    )(*plan.ins)


def _gather_plan(shards):
    n = len(shards)

    def issue(ins, outs, send_sems, recv_sems, local_sems):
        x, y, c = _place()
        me = 2 * x + y
        sends, recvs, locs = [], [], []
        for p in range(n):
            locs.append(pltpu.make_async_copy(ins[p], outs[p].at[me], local_sems.at[p]))
            for k, (px, py) in enumerate(_other_chips(x, y)):
                sems = dict(send_sem=send_sems.at[3 * p + k], recv_sem=recv_sems.at[3 * p + k],
                            device_id=(px, py, c), device_id_type=MESH)
                sends.append(pltpu.make_async_remote_copy(src_ref=ins[p], dst_ref=outs[p].at[me], **sems))
                recvs.append(functools.partial(pltpu.make_async_remote_copy, src_ref=ins[p],
                                               dst_ref=outs[p].at[2 * px + py], **sems))
        return sends, recvs, locs

    return _Plan(shards, [jax.ShapeDtypeStruct((N_CHIPS,) + s.shape, s.dtype) for s in shards], 3 * n, n, issue)


_REL7 = [(fx, fy, fc) for fx in (0, 1) for fy in (0, 1) for fc in (0, 1)][1:]


def _scatter8_plan(gs):
    n = len(gs)

    def issue(ins, outs, send_sems, recv_sems, local_sems):
        x, y, c = _place()
        sends = []
        for p in range(n):
            hr = gs[p].shape[1] // 2
            for k, (fx, fy, fc) in enumerate(_REL7):
                tx, ty, tc = x ^ fx, y ^ fy, c ^ fc
                src = ins[p].at[2 * tx + ty, pl.ds(pl.multiple_of(tc * hr, SUBLANES), hr), :]
                sends.append(pltpu.make_async_remote_copy(
                    src_ref=src, dst_ref=outs[p].at[k],
                    send_sem=send_sems.at[7 * p + k], recv_sem=recv_sems.at[7 * p + k],
                    device_id=(tx, ty, tc), device_id_type=MESH))
        return sends, [functools.partial(lambda cp: cp, cp) for cp in sends], []

    shapes = [jax.ShapeDtypeStruct((7, g.shape[1] // 2, g.shape[2]), g.dtype) for g in gs]
    return _Plan(gs, shapes, 7 * n, 0, issue)


def _sibling_plan(ts):
    n = len(ts)

    def issue(ins, outs, send_sems, recv_sems, local_sems):
        x, y, c = _place()
        sends = [pltpu.make_async_remote_copy(
            src_ref=ins[p], dst_ref=outs[p], send_sem=send_sems.at[p], recv_sem=recv_sems.at[p],
            device_id=(x, y, 1 - c), device_id_type=MESH) for p in range(n)]
        return sends, [functools.partial(lambda cp: cp, cp) for cp in sends], []

    return _Plan(ts, [jax.ShapeDtypeStruct(t.shape, t.dtype) for t in ts], n, 0, issue)


def _add8(g, recv, chip, core, name):
    s, r, n = g.shape
    hr = r // 2
    th = hr // 2 if (hr // 2) % SUBLANES == 0 else hr
    nt = hr // th

    def body(chip_ref, core_ref, g_ref, r_ref, o_ref):
        acc = g_ref[0].astype(F32)
        for k in range(7):
            acc = acc + r_ref[k].astype(F32)
        o_ref[...] = acc

    return pl.pallas_call(
        body, name=name,
        grid_spec=pltpu.PrefetchScalarGridSpec(
            num_scalar_prefetch=2, grid=(nt,),
            in_specs=[pl.BlockSpec((1, th, n), lambda i, ch, co: (ch[0], co[0] * nt + i, 0)),
                      pl.BlockSpec((7, th, n), lambda i, ch, co: (0, i, 0))],
            out_specs=pl.BlockSpec((th, n), lambda i, ch, co: (i, 0))),
        out_shape=jax.ShapeDtypeStruct((hr, n), F32),
        compiler_params=pltpu.CompilerParams(dimension_semantics=("parallel",)),
    )(chip, core, g, recv)


def _adamw_halves(w, own, other, m, v, core, name):
    r, n = w.shape
    hr = r // 2
    th = hr // 2 if (hr // 2) % SUBLANES == 0 else hr
    tph = hr // th
    c1 = 1.0 / (1.0 - ADAM_B1 ** ADAM_STEP)
    c2 = 1.0 / (1.0 - ADAM_B2 ** ADAM_STEP)

    def body(core_ref, w_ref, a_ref, b_ref, m_ref, v_ref, g_ref, d_ref, nm_ref, nv_ref):
        half = pl.program_id(0) // tph
        gv = jnp.where(half == core_ref[0], a_ref[...], b_ref[...])
        nm = ADAM_B1 * m_ref[...] + (1.0 - ADAM_B1) * gv
        nv = ADAM_B2 * v_ref[...] + (1.0 - ADAM_B2) * (gv * gv)
        g_ref[...] = gv
        nm_ref[...] = nm
        nv_ref[...] = nv
        d_ref[...] = -ADAM_LR * ((nm * c1) / (jnp.sqrt(nv * c2) + ADAM_EPS) + ADAM_WD * w_ref[...])

    full = pl.BlockSpec((th, n), lambda i, co: (i, 0))
    part = pl.BlockSpec((th, n), lambda i, co: (i % tph, 0))
    return pl.pallas_call(
        body, name=name,
        grid_spec=pltpu.PrefetchScalarGridSpec(
            num_scalar_prefetch=1, grid=(2 * tph,),
            in_specs=[full, part, part, full, full], out_specs=[full] * 4),
        out_shape=[jax.ShapeDtypeStruct((r, n), F32)] * 4,
        compiler_params=pltpu.CompilerParams(dimension_semantics=("parallel",)),
    )(core, w, own, other, m, v)


def _allreduce_small(pack, name):
    r, l = pack.shape
    hr = r // 2
    assert hr % SUBLANES == 0

    def body(p_ref, o_ref, sib, chips, send_sems, recv_sems):
        x, y, c = _place()
        chip = 2 * x + y
        sibling = dict(device_id=(x, y, 1 - c), device_id_type=MESH)
        mine = pl.ds(pl.multiple_of(c * hr, SUBLANES), hr)
        other = pl.ds(pl.multiple_of((1 - c) * hr, SUBLANES), hr)
        a = pltpu.make_async_remote_copy(src_ref=p_ref.at[other], dst_ref=sib, send_sem=send_sems.at[0],
                                         recv_sem=recv_sems.at[0], **sibling)
        a.start()
        a.wait()
        own, got = p_ref[mine, :], sib[...]
        chips[chip] = jnp.where(c == 0, own, got) + jnp.where(c == 0, got, own)
        sends = []
        for k, (px, py) in enumerate(_other_chips(x, y)):
            cp = pltpu.make_async_remote_copy(
                src_ref=chips.at[chip], dst_ref=chips.at[chip], send_sem=send_sems.at[1 + k],
                recv_sem=recv_sems.at[1 + k], device_id=(px, py, c), device_id_type=MESH)
            cp.start()
            sends.append(cp)
        for k, (px, py) in enumerate(_other_chips(x, y)):
            pltpu.make_async_remote_copy(
                src_ref=chips.at[chip], dst_ref=chips.at[2 * px + py], send_sem=send_sems.at[1 + k],
                recv_sem=recv_sems.at[1 + k], device_id=(px, py, c), device_id_type=MESH).wait_recv()
        for cp in sends:
            cp.wait_send()
        o_ref[mine, :] = ((chips[0] + chips[1]) + chips[2]) + chips[3]
        fin = pltpu.make_async_remote_copy(src_ref=o_ref.at[mine], dst_ref=o_ref.at[mine],
                                           send_sem=send_sems.at[4], recv_sem=recv_sems.at[4], **sibling)
        fin.start()
        pltpu.make_async_remote_copy(src_ref=o_ref.at[mine], dst_ref=o_ref.at[other],
                                     send_sem=send_sems.at[4], recv_sem=recv_sems.at[4], **sibling).wait_recv()
        fin.wait_send()

    vm = pl.BlockSpec(memory_space=pltpu.VMEM)
    return pl.pallas_call(
        body, name=name, in_specs=[vm], out_specs=vm,
        out_shape=jax.ShapeDtypeStruct((r, l), F32),
        scratch_shapes=[pltpu.VMEM((hr, l), F32), pltpu.VMEM((N_CHIPS, hr, l), F32),
                        pltpu.SemaphoreType.DMA((5,)), pltpu.SemaphoreType.DMA((5,))],
    )(pack)


def _flat_rows(a, mult=SUBLANES * LANES):
    f = a.reshape(-1)
    padn = (-f.shape[0]) % mult
    if padn:
        f = jnp.concatenate([f, jnp.zeros((padn,), f.dtype)])
    return f


def _pack(arrs, mult=SUBLANES * LANES, total_mult=None):
    flat = [_flat_rows(a, mult) for a in arrs]
    sizes = [f.shape[0] for f in flat]
    if total_mult is not None:
        padn = (-sum(sizes)) % total_mult
        if padn:
            flat.append(jnp.zeros((padn,), flat[0].dtype))
    return jnp.concatenate(flat).reshape(-1, LANES), sizes


def _unpack(pack, shapes, sizes, lead=()):
    flat = pack.reshape(lead + (-1,))
    out, off = [], 0
    for shp, sz in zip(shapes, sizes):
        n = math.prod(shp)
        out.append(flat[..., off:off + n].reshape(lead + tuple(shp)))
        off += sz
    return out


def _cols_from_shards(g):
    s, k, n = g.shape
    return jnp.transpose(g, (1, 0, 2)).reshape(k, s * n)


def _ffn_fwd(h, u, post, w_up, cw, cb, w_down, tag, next_pre=None, plan=None, loss_tgt=None):
    (hp, act, hg, hu), pouts = _ffn_up_convact(u, w_up, cw, cb, f"{tag}_up_convact", plan)
    if loss_tgt is not None:
        o, hn, un = _mm_postnorm_loss(act, w_down, h, post, loss_tgt, f"{tag}_down_postnorm_loss")
    else:
        res = _mm_postnorm_res(act, w_down, h, post, f"{tag}_down_postnorm", next_pre)
        o, hn, un = res if next_pre is not None else (*res, None)
    return hn, un, (h, u, hp, hg, hu, act, o), pouts


def _ffn_bwd(dh, saved, pre, post, w_up, cw, w_down, tag):
    h, u, hp, hg, hu, act, o = saved
    dhg, dhu, do, dpost = _ffn_down_act_bwd(o, post, dh, w_down, hg, hu, f"{tag}_down_dx_act_bwd")
    dw_down = _mm_tn(act, do, f"{tag}_down_dw")
    dxg, dwg, dbg = _conv_bwd(hp, 0, D_FF, dhg, cw, f"{tag}_conv_bwd_gate")
    dxu, dwu, dbu = _conv_bwd(hp, D_FF, D_FF, dhu, cw, f"{tag}_conv_bwd_up", w_col_off=D_FF)
    dhp = (dxg, dxu)
    dcw = jnp.concatenate([dwg, dwu], axis=1)
    dcb = jnp.concatenate([dbg, dbu], axis=1)
    (dhn, dpre), _ = _mm_nt_sh(dhp, w_up, (h, pre, dh), f"{tag}_up_dx_prenorm_bwd")
    dw_up = _mm_tn_sh(u, dhp, w_up.shape[2], f"{tag}_up_dw")
    return dhn, dict(pre=dpre, post=dpost, w_up=dw_up, conv_w=dcw[:3], conv_b=dcb, w_down=dw_down)


class _Exchange:
    GATHER_IN_LRU = ("l0_w_out", "l0_ffn_w_down")
    GATHER_IN_ATTN = ("l0_ffn_w_up", "l1_w_out")
    GATHER_IN_FFN0 = ("l1_w_in",)
    GATHER_IN_SSD = ("l1_ffn_w_up", "l1_ffn_w_down")
    AFTER_L1_OUT = ("l1_ffn_w_up", "l1_ffn_w_down", "l1_w_out")
    IN_LRU_BWD = ("l1_w_in",)
    AFTER_L0_OUT = ("l0_ffn_w_up", "l0_ffn_w_down", "l0_w_out")
    LAST = ("l0_w_in",)

    def __init__(self, late_shards):
        self.late = dict(late_shards)
        self.slabs = {}
        self.recv = {}

    def gather_plan(self, names):
        return _gather_plan([self.late[n] for n in names])

    def gathered(self, names, outs):
        return {n: (g if n in _BIG_COL else g.reshape(-1, g.shape[-1])) for n, g in zip(names, outs)}

    def scatter_plan(self, grads, names):
        for n in names:
            g = grads[n]
            self.slabs[n] = g if n in _BIG_COL else g.reshape(N_CHIPS, -1, g.shape[-1])
        return _scatter8_plan([self.slabs[n] for n in names])

    def scattered(self, names, outs):
        self.recv.update(zip(names, outs))


def _local_step(x, tgt, meta, P, ex=None):
    seq, d = x.shape
    lp = seq + BLOCK
    h0 = jnp.concatenate([jnp.zeros((PAD, d), F32), meta, x], axis=0)
    tgt_p = jnp.concatenate([jnp.zeros((BLOCK, d), F32), tgt], axis=0)

    u0 = _rmsnorm_fwd(h0, P["l0_mix_pre_norm"], "l0_mix_prenorm")
    proj0, xrc = _mm_nn_sh(u0, P["l0_w_in"], EVEN_IN, "l0_in_lru_conv",
                           (D_MODEL, D_MODEL, P["l0_lru_conv_w"], P["l0_lru_conv_b"]))
    lru_args = (P["l0_lru_w_a"], P["l0_lru_w_x"], P["l0_lru_b_a"], P["l0_lru_b_x"], P["l0_lru_lambda"])
    if ex:
        (ya, hl), outs = _lru_fwd(proj0, xrc, *lru_args, "l0_lru", ex.gather_plan(ex.GATHER_IN_LRU))
        P = {**P, **ex.gathered(ex.GATHER_IN_LRU, outs)}
    else:
        ya, hl = _lru_fwd(proj0, xrc, *lru_args, "l0_lru")
    yb, outs = _attn_fwd(proj0, P["l0_attn_sinks"], "l0_attn",
                         ex.gather_plan(ex.GATHER_IN_ATTN) if ex else None)
    if ex:
        P = {**P, **ex.gathered(ex.GATHER_IN_ATTN, outs)}
    o0, h1, u1 = _mm_postnorm_res((ya, yb), P["l0_w_out"], h0, P["l0_mix_post_norm"], "l0_out_postnorm",
                                  P["l0_ffn_pre_norm"])
    h2, u2, ffn0, outs = _ffn_fwd(h1, u1, P["l0_ffn_post_norm"], P["l0_ffn_w_up"], P["l0_ffn_conv_w"],
                                  P["l0_ffn_conv_b"], P["l0_ffn_w_down"], "l0_ffn", P["l1_mix_pre_norm"],
                                  ex.gather_plan(ex.GATHER_IN_FFN0) if ex else None)
    if ex:
        P = {**P, **ex.gathered(ex.GATHER_IN_FFN0, outs)}
    proj1 = _mm_nn_sh(u2, P["l1_w_in"], ODD_IN_PAD, "l1_in")
    xc1, xbc, dt = _ssm_convprep_fwd(proj1, P["l1_ssm_conv_w"], P["l1_ssm_conv_b"], P["l1_dt_bias"],
                                     "l1_ssm_convprep")
    (yssd, states), outs = _ssd_fwd(xbc, dt, P["l1_a_log"], "l1_ssd",
                                    ex.gather_plan(ex.GATHER_IN_SSD) if ex else None)
    if ex:
        P = {**P, **ex.gathered(ex.GATHER_IN_SSD, outs)}
    yn, o1, h3, u3 = _ssm_gate_out_postnorm(
        yssd, xbc, proj1, P["l1_d_skip"], P["l1_gate_norm"], P["l1_w_out"], h2, P["l1_mix_post_norm"],
        P["l1_ffn_pre_norm"], "l1_gate_out_postnorm")
    dh4, loss_cols, ffn1, _ = _ffn_fwd(h3, u3, P["l1_ffn_post_norm"], P["l1_ffn_w_up"], P["l1_ffn_conv_w"],
                                       P["l1_ffn_conv_b"], P["l1_ffn_w_down"], "l1_ffn", loss_tgt=tgt_p)

    G = {}
    dh3, g = _ffn_bwd(dh4, ffn1, P["l1_ffn_pre_norm"], P["l1_ffn_post_norm"], P["l1_ffn_w_up"],
                      P["l1_ffn_conv_w"], P["l1_ffn_w_down"], "l1_ffn")
    for k, v in g.items():
        G["l1_ffn_" + (k + "_norm" if k in ("pre", "post") else k)] = v
    dyssd, dxskip, dz, dd_cols, G["l1_gate_norm"], do1, G["l1_mix_post_norm"] = _ssm_out_gate_bwd(
        o1, P["l1_mix_post_norm"], dh3, P["l1_w_out"], yssd, xbc, proj1, P["l1_d_skip"], P["l1_gate_norm"],
        "l1_out_dx_gate_bwd")
    G["l1_w_out"] = _mm_tn(yn, do1, "l1_out_dw")
    G["l1_d_skip"] = dd_cols.reshape(SSD_HEADS, SSD_P).sum(axis=1)
    (dxc, ddtr, dalog, dbias), outs = _ssd_bwd(
        xbc, dt, P["l1_a_log"], states, dyssd, xc1, proj1, P["l1_dt_bias"], dxskip, "l1_ssd_bwd",
        ex.scatter_plan(G, ex.AFTER_L1_OUT) if ex else None)
    if ex:
        ex.scattered(ex.AFTER_L1_OUT, outs)
    G["l1_a_log"] = dalog[0, :SSD_HEADS]
    G["l1_dt_bias"] = dbias[0, :SSD_HEADS]
    dxbc, dcw, dcb = _conv_bwd(proj1, _ZW, _XBC_W, dxc, P["l1_ssm_conv_w"], "l1_ssm_conv_bwd")
    G["l1_ssm_conv_w"] = dcw[:4]
    G["l1_ssm_conv_b"] = dcb
    dproj1 = jnp.concatenate([dz, dxbc, ddtr], axis=1)
    (dh2, G["l1_mix_pre_norm"]), _ = _mm_nt_sh(dproj1, P["l1_w_in"], (h2, P["l1_mix_pre_norm"], dh3),
                                               "l1_in_dx_prenorm_bwd")
    G["l1_w_in"] = _mm_tn_sh(u2, dproj1, ODD_IN // N_CHIPS, "l1_in_dw")
    dh1, g = _ffn_bwd(dh2, ffn0, P["l0_ffn_pre_norm"], P["l0_ffn_post_norm"], P["l0_ffn_w_up"],
                      P["l0_ffn_conv_w"], P["l0_ffn_w_down"], "l0_ffn")
    for k, v in g.items():
        G["l0_ffn_" + (k + "_norm" if k in ("pre", "post") else k)] = v
    dmix, do0, G["l0_mix_post_norm"] = _mm_nt_postnorm_bwd(o0, P["l0_mix_post_norm"], dh1, P["l0_w_out"],
                                                           "l0_out_dx")
    G["l0_w_out"] = jnp.concatenate([_mm_tn(ya, do0, "l0_out_dw_lru"), _mm_tn(yb, do0, "l0_out_dw_attn")], axis=0)
    if ex:
        lru_out, outs = _lru_bwd(proj0, xrc, hl, dmix, *lru_args, "l0_lru_bwd",
                                 ex.scatter_plan(G, ex.IN_LRU_BWD))
        ex.scattered(ex.IN_LRU_BWD, outs)
    else:
        lru_out = _lru_bwd(proj0, xrc, hl, dmix, *lru_args, "l0_lru_bwd")
    (dgate, dxrc, G["l0_lru_w_a"], G["l0_lru_w_x"], G["l0_lru_b_a"], G["l0_lru_b_x"],
     G["l0_lru_lambda"]) = lru_out
    dxr, dcw, dcb = _conv_bwd(proj0, D_MODEL, D_MODEL, dxrc, P["l0_lru_conv_w"], "l0_lru_conv_bwd")
    G["l0_lru_conv_w"] = dcw[:4]
    G["l0_lru_conv_b"] = dcb
    (dq, dk, dv, G["l0_attn_sinks"]), outs = _attn_bwd(
        proj0, P["l0_attn_sinks"], dmix, "l0_attn_bwd",
        ex.scatter_plan(G, ex.AFTER_L0_OUT) if ex else None)
    if ex:
        ex.scattered(ex.AFTER_L0_OUT, outs)
    dproj0 = jnp.concatenate([dgate, dxr, dq, dk.astype(BF16), dv.astype(BF16)], axis=1)
    G["l0_w_in"] = _mm_tn_sh(u0, dproj0, EVEN_IN // N_CHIPS, "l0_in_dw")
    (dh0, G["l0_mix_pre_norm"]), outs = _mm_nt_sh(
        dproj0, P["l0_w_in"], (h0, P["l0_mix_pre_norm"], dh1), "l0_in_dx_prenorm_bwd",
        ex.scatter_plan(G, ex.LAST) if ex else None)
    if ex:
        ex.scattered(ex.LAST, outs)
    return loss_cols, dh0[BLOCK:], dh0[PAD:BLOCK], G


_BIG_COL = ("l0_w_in", "l0_ffn_w_up", "l1_w_in", "l1_ffn_w_up")
_BIG = ("l0_w_in", "l0_w_out", "l0_ffn_w_up", "l0_ffn_w_down",
        "l1_w_in", "l1_w_out", "l1_ffn_w_up", "l1_ffn_w_down")
_SMALL_SHARDED = ("meta_tokens", "l0_lru_conv_w", "l0_ffn_conv_w", "l1_ssm_conv_w", "l1_ffn_conv_w")
_WEIGHTS = ("meta_tokens", "l0_mix_pre_norm", "l0_mix_post_norm", "l0_w_in", "l0_lru_conv_w",
            "l0_lru_conv_b", "l0_lru_w_a", "l0_lru_b_a", "l0_lru_w_x", "l0_lru_b_x", "l0_lru_lambda",
            "l0_attn_sinks", "l0_w_out", "l0_ffn_pre_norm", "l0_ffn_post_norm", "l0_ffn_w_up",
            "l0_ffn_conv_w", "l0_ffn_conv_b", "l0_ffn_w_down", "l1_mix_pre_norm", "l1_mix_post_norm",
            "l1_w_in", "l1_ssm_conv_w", "l1_ssm_conv_b", "l1_dt_bias", "l1_a_log", "l1_d_skip",
            "l1_gate_norm", "l1_w_out", "l1_ffn_pre_norm", "l1_ffn_post_norm", "l1_ffn_w_up",
            "l1_ffn_conv_w", "l1_ffn_conv_b", "l1_ffn_w_down")
_REPL = tuple(n for n in _WEIGHTS if n not in _BIG and n not in _SMALL_SHARDED)


def _pad_lanes(v, n=LANES):
    return jnp.concatenate([v, jnp.zeros((n - v.shape[0],), v.dtype)]).reshape(1, n)


def _step(x, tgt, W, M, V):
    cx, cy, cc = _place()
    chip = 2 * cx + cy

    small_pack, small_sizes = _pack([W[n] for n in _SMALL_SHARDED])
    first = _run_plan(_gather_plan([W["l0_w_in"].astype(BF16), small_pack]), "gather_first")
    small_full = _unpack(first[1], [W[n].shape for n in _SMALL_SHARDED], small_sizes, lead=(N_CHIPS,))
    ex = _Exchange({n: W[n].astype(BF16) for n in _BIG if n != "l0_w_in"})

    P = {"l0_w_in": first[0]}
    for n, g in zip(_SMALL_SHARDED, small_full):
        P[n] = _cols_from_shards(g)
    for n in _REPL:
        v = W[n]
        P[n] = v.reshape(1, -1) if v.ndim == 1 else v
    P["l0_lru_w_a"] = W["l0_lru_w_a"].astype(BF16)
    P["l0_lru_w_x"] = W["l0_lru_w_x"].astype(BF16)
    P["l1_dt_bias"] = _pad_lanes(W["l1_dt_bias"])
    P["l1_a_log"] = _pad_lanes(W["l1_a_log"])
    P["l1_d_skip"] = jnp.repeat(W["l1_d_skip"], SSD_P).reshape(1, D_SSM)
    meta = P.pop("meta_tokens")

    loss_cols, grad_x, grad_meta, G = _local_step(x, tgt, meta, P, ex)
    G["meta_tokens"] = grad_meta

    core_idx = cc.astype(jnp.int32).reshape(1)
    chip_idx = chip.astype(jnp.int32).reshape(1)
    own_half = [_add8(ex.slabs[n], ex.recv[n], chip_idx, core_idx, f"grad_sum_{n}") for n in _BIG]
    other_half = _run_plan(_sibling_plan(own_half), "grad_sibling_swap")
    small_names = list(_REPL) + list(_SMALL_SHARDED)
    small_list = [G[n] for n in small_names] + [loss_cols]
    spack, ssizes = _pack(small_list, total_mult=2 * SUBLANES * LANES)
    sred = _allreduce_small(spack, "small_allreduce")
    sfull = _unpack(sred, [a.shape for a in small_list], ssizes)
    loss = 0.5 / D_MODEL * jnp.sum(sfull[-1])
    small_grads = {}
    for n, g in zip(small_names, sfull[:-1]):
        if n in _SMALL_SHARDED:
            wcols = W[n].shape[1]
            g = lax.dynamic_slice_in_dim(g, chip * wcols, wcols, axis=1)
        small_grads[n] = g.reshape(W[n].shape)

    grads, delta, new_m, new_v = {}, {}, {}, {}
    for n, own, other in zip(_BIG, own_half, other_half):
        grads[n], delta[n], new_m[n], new_v[n] = _adamw_halves(
            W[n], own, other, M[n], V[n], core_idx, f"adamw_{n}")
    s_names = [n for n in _WEIGHTS if n not in _BIG]
    as2d = lambda a: a.reshape(_shape2d(a.shape))
    outs = _adamw_many([as2d(W[n]) for n in s_names], [as2d(small_grads[n]) for n in s_names],
                       [as2d(M[n]) for n in s_names], [as2d(V[n]) for n in s_names], "adamw_small")
    k = len(s_names)
    for i, n in enumerate(s_names):
        grads[n] = small_grads[n]
        delta[n], new_m[n], new_v[n] = (outs[j * k + i].reshape(W[n].shape) for j in range(3))
    return loss, grad_x, grads, delta, new_m, new_v


def kernel(x, meta_tokens, l0_mix_pre_norm, l0_mix_post_norm, l0_w_in, l0_lru_conv_w, l0_lru_conv_b, l0_lru_w_a, l0_lru_b_a, l0_lru_w_x, l0_lru_b_x, l0_lru_lambda, l0_attn_sinks, l0_w_out, l0_ffn_pre_norm, l0_ffn_post_norm, l0_ffn_w_up, l0_ffn_conv_w, l0_ffn_conv_b, l0_ffn_w_down, l1_mix_pre_norm, l1_mix_post_norm, l1_w_in, l1_ssm_conv_w, l1_ssm_conv_b, l1_dt_bias, l1_a_log, l1_d_skip, l1_gate_norm, l1_w_out, l1_ffn_pre_norm, l1_ffn_post_norm, l1_ffn_w_up, l1_ffn_conv_w, l1_ffn_conv_b, l1_ffn_w_down, loss_target, m_meta_tokens, m_l0_mix_pre_norm, m_l0_mix_post_norm, m_l0_w_in, m_l0_lru_conv_w, m_l0_lru_conv_b, m_l0_lru_w_a, m_l0_lru_b_a, m_l0_lru_w_x, m_l0_lru_b_x, m_l0_lru_lambda, m_l0_attn_sinks, m_l0_w_out, m_l0_ffn_pre_norm, m_l0_ffn_post_norm, m_l0_ffn_w_up, m_l0_ffn_conv_w, m_l0_ffn_conv_b, m_l0_ffn_w_down, m_l1_mix_pre_norm, m_l1_mix_post_norm, m_l1_w_in, m_l1_ssm_conv_w, m_l1_ssm_conv_b, m_l1_dt_bias, m_l1_a_log, m_l1_d_skip, m_l1_gate_norm, m_l1_w_out, m_l1_ffn_pre_norm, m_l1_ffn_post_norm, m_l1_ffn_w_up, m_l1_ffn_conv_w, m_l1_ffn_conv_b, m_l1_ffn_w_down, v_meta_tokens, v_l0_mix_pre_norm, v_l0_mix_post_norm, v_l0_w_in, v_l0_lru_conv_w, v_l0_lru_conv_b, v_l0_lru_w_a, v_l0_lru_b_a, v_l0_lru_w_x, v_l0_lru_b_x, v_l0_lru_lambda, v_l0_attn_sinks, v_l0_w_out, v_l0_ffn_pre_norm, v_l0_ffn_post_norm, v_l0_ffn_w_up, v_l0_ffn_conv_w, v_l0_ffn_conv_b, v_l0_ffn_w_down, v_l1_mix_pre_norm, v_l1_mix_post_norm, v_l1_w_in, v_l1_ssm_conv_w, v_l1_ssm_conv_b, v_l1_dt_bias, v_l1_a_log, v_l1_d_skip, v_l1_gate_norm, v_l1_w_out, v_l1_ffn_pre_norm, v_l1_ffn_post_norm, v_l1_ffn_w_up, v_l1_ffn_conv_w, v_l1_ffn_conv_b, v_l1_ffn_w_down):
    args = locals()
    W = {n: args[n] for n in _WEIGHTS}
    M = {n: args["m_" + n] for n in _WEIGHTS}
    V = {n: args["v_" + n] for n in _WEIGHTS}
    loss, grad_x, grads, delta, new_m, new_v = _step(x[0], loss_target[0], W, M, V)
    return (loss, grad_x[None], *[grads[n] for n in _WEIGHTS], *[delta[n] for n in _WEIGHTS],
            *[new_m[n] for n in _WEIGHTS], *[new_v[n] for n in _WEIGHTS])
```

```python
import functools
import math

import jax
import jax.numpy as jnp
from jax import lax
from jax.experimental import pallas as pl
from jax.experimental.pallas import tpu as pltpu

F32 = jnp.float32
BF16 = jnp.bfloat16

D_MODEL = 1024
N_META = 16
BLOCK = 128
PAD = BLOCK - N_META
EPS = 1e-6
LRU_BLOCKS = 8
LRU_BS = 128
LRU_C = 8.0
N_Q_HEADS = 16
N_KV_HEADS = 2
HEAD_DIM = 64
Q_PER_KV = 8
WINDOW = 128
D_SSM = 2048
SSD_HEADS = 32
SSD_GROUPS = 8
SSD_HPG = 4
SSD_P = 64
SSD_N = 128
D_FF = 2816
NEG = -1e30
LANES = 128
SUBLANES = 8
_VMEM_LIMIT_WIDE = 62 * 1024 * 1024

ADAM_LR = 0.001
ADAM_B1 = 0.9
ADAM_B2 = 0.999
ADAM_EPS = 1e-08
ADAM_WD = 0.01
ADAM_STEP = 10

MESH = pl.DeviceIdType.MESH
N_CHIPS = 4


def _pick(n, cands):
    for c in cands:
        if n % c == 0:
            return c
    raise ValueError(f"no tile for {n} in {cands}")


def _col_tile(n, limit=1792):
    best = None
    for t in range(LANES, min(n, limit) + 1, LANES):
        if n % t == 0:
            best = t
    if best is None:
        raise ValueError(f"no lane tile for {n}")
    return best


def _sigmoid(x):
    return 0.5 + 0.5 * jnp.tanh(0.5 * x)


def _log1p(e):
    u = 1.0 + e
    return jnp.where(u == 1.0, e, jnp.log(u) * (e / jnp.where(u == 1.0, 1.0, u - 1.0)))


def _softplus(x):
    return jnp.maximum(x, 0.0) + _log1p(jnp.exp(-jnp.abs(x)))


def _neg_expm1(x):
    poly = x * (1.0 + x * (0.5 + x * (1.0 / 6.0 + x * (1.0 / 24.0 + x * (1.0 / 120.0)))))
    return -jnp.where(x > -0.05, poly, jnp.exp(x) - 1.0)


_GELU_C = math.sqrt(2.0 / math.pi)


def _gelu(x):
    u = 0.5 + 0.5 * jnp.tanh(x * (_GELU_C + (_GELU_C * 0.044715) * (x * x)))
    return x * u


def _gelu_and_grad(x):
    x2 = x * x
    u = 0.5 + 0.5 * jnp.tanh(x * (_GELU_C + (_GELU_C * 0.044715) * x2))
    g = x * u
    dg = u * (1.0 + (x - g) * (2.0 * _GELU_C + (6.0 * 0.044715 * _GELU_C) * x2))
    return g, dg


def _silu_and_grad(x):
    s = _sigmoid(x)
    return x * s, s * (1.0 + x * (1.0 - s))


def _dot(a, b):
    return jnp.dot(a, b, preferred_element_type=F32)


def _dot_nt(a, b):
    return lax.dot_general(a, b, (((1,), (1,)), ((), ())), preferred_element_type=F32)


def _dot_tn(a, b):
    return lax.dot_general(a, b, (((0,), (0,)), ((), ())), preferred_element_type=F32)


def _row_iota(t):
    return lax.broadcasted_iota(jnp.int32, (t, 1), 0)


def _scan_fwd(a, u, t):
    row = _row_iota(t)
    d = 1
    while d < t:
        m = row >= d
        u_sh = jnp.where(m, pltpu.roll(u, d, 0), 0.0)
        a_sh = jnp.where(m, pltpu.roll(a, d, 0), 1.0)
        u = u + a * u_sh
        a = a * a_sh
        d *= 2
    return a, u


def _scan_rev(c, x, t):
    row = _row_iota(t)
    d = 1
    while d < t:
        m = row < t - d
        x_sh = jnp.where(m, pltpu.roll(x, t - d, 0), 0.0)
        c_sh = jnp.where(m, pltpu.roll(c, t - d, 0), 1.0)
        x = x + c * x_sh
        c = c * c_sh
        d *= 2
    return c, x


def _cumsum_rows(x, t):
    row = _row_iota(t)
    d = 1
    while d < t:
        x = x + jnp.where(row >= d, pltpu.roll(x, d, 0), 0.0)
        d *= 2
    return x


def _rev_cumsum_rows(x, t):
    row = _row_iota(t)
    d = 1
    while d < t:
        x = x + jnp.where(row < t - d, pltpu.roll(x, t - d, 0), 0.0)
        d *= 2
    return x


def _rms_bwd(x, g, dy):
    rs = lax.rsqrt(jnp.mean(x * x, axis=-1, keepdims=True) + EPS)
    gy = dy * g
    dx = rs * gy - x * (rs * rs * rs) * jnp.mean(x * gy, axis=-1, keepdims=True)
    return dx, dy * x * rs


def _mm_nt_postnorm_bwd(o, gain, dh, w, name):
    m, d = o.shape
    k = w.shape[0]
    tm = _pick(m, (640, 512, 256, 128))

    def body(o_ref, gain_ref, dh_ref, w_ref, y_ref, do_ref, dgain_ref):
        y_ref[...] = _dot_nt(_postnorm_bwd_tile(o_ref, gain_ref, dh_ref, do_ref, dgain_ref), w_ref[...])

    row = pl.BlockSpec((tm, d), lambda i: (i, 0))
    vec = pl.BlockSpec((1, d), lambda i: (0, 0))
    return pl.pallas_call(
        body, name=name, grid=(m // tm,),
        in_specs=[row, vec, row, pl.BlockSpec((k, d), lambda i: (0, 0))],
        out_specs=[pl.BlockSpec((tm, k), lambda i: (i, 0)), row, vec],
        out_shape=[jax.ShapeDtypeStruct((m, k), F32), jax.ShapeDtypeStruct((m, d), BF16),
                   jax.ShapeDtypeStruct((1, d), F32)],
        compiler_params=pltpu.CompilerParams(dimension_semantics=("arbitrary",)),
    )(o, gain, dh, w)


def _mm_tn(a, dy, name):
    m, k = a.shape
    n = dy.shape[1]
    tm = _pick(m, (640, 512, 256, 128))
    tk = _col_tile(k, 1408)
    tn = _col_tile(n, 1664)
    nsteps = m // tm

    def body(a_ref, dy_ref, o_ref, acc):
        @pl.when(pl.program_id(2) == 0)
        def _():
            acc[...] = jnp.zeros_like(acc)

        acc[...] += _dot_tn(a_ref[...].astype(BF16), dy_ref[...].astype(BF16))

        @pl.when(pl.program_id(2) == nsteps - 1)
        def _():
            o_ref[...] = acc[...].astype(o_ref.dtype)

    return pl.pallas_call(
        body, name=name, grid=(k // tk, n // tn, nsteps),
        in_specs=[pl.BlockSpec((tm, tk), lambda kk, j, i: (i, kk)),
                  pl.BlockSpec((tm, tn), lambda kk, j, i: (i, j))],
        out_specs=pl.BlockSpec((tk, tn), lambda kk, j, i: (kk, j)),
        out_shape=jax.ShapeDtypeStruct((k, n), BF16),
        scratch_shapes=[pltpu.VMEM((tk, tn), F32)],
        compiler_params=pltpu.CompilerParams(
            dimension_semantics=("parallel", "parallel", "arbitrary")),
    )(a, dy)


def _mm_nn_sh(a, w4, n_out, name, conv=None):
    m, k = a.shape
    s, _, n = w4.shape
    tm = _pick(m, (320, 256, 128))
    if conv is not None:
        c0, cw_, taps, bias = conv
        kk = taps.shape[0]

    def body(*refs):
        a_ref, w_ref = refs[:2]
        av = a_ref[...].astype(BF16)
        o_ref = refs[4] if conv is not None else refs[2]
        for j in range(s):
            o_ref[:, j * n:(j + 1) * n] = _dot(av, w_ref[j])
        if n_out > s * n:
            o_ref[:, s * n:] = jnp.zeros((tm, n_out - s * n), F32)
        if conv is not None:
            t_ref, b_ref, y_ref, carry = refs[2], refs[3], refs[5], refs[6]

            @pl.when(pl.program_id(0) == 0)
            def _():
                carry[...] = jnp.zeros_like(carry)

            xv = o_ref[:, c0:c0 + cw_]
            xx = jnp.concatenate([carry[...], xv], axis=0)
            acc = b_ref[...] + t_ref[kk - 1:kk, :] * xv
            for mm in range(1, kk):
                acc = acc + t_ref[kk - 1 - mm:kk - mm, :] * pltpu.roll(xx, mm, 0)[SUBLANES:, :]
            y_ref[...] = acc
            carry[...] = xv[tm - SUBLANES:, :]

    row_a = pl.BlockSpec((tm, k), lambda i: (i, 0))
    w_spec = pl.BlockSpec((s, k, n), lambda i: (0, 0, 0))
    o_spec = pl.BlockSpec((tm, n_out), lambda i: (i, 0))
    if conv is None:
        return pl.pallas_call(
            body, name=name, grid=(m // tm,), in_specs=[row_a, w_spec], out_specs=o_spec,
            out_shape=jax.ShapeDtypeStruct((m, n_out), F32),
            compiler_params=pltpu.CompilerParams(dimension_semantics=("parallel",)),
        )(a, w4)
    return pl.pallas_call(
        body, name=name, grid=(m // tm,),
        in_specs=[row_a, w_spec, pl.BlockSpec(taps.shape, lambda i: (0, 0)),
                  pl.BlockSpec(bias.shape, lambda i: (0, 0))],
        out_specs=[o_spec, pl.BlockSpec((tm, cw_), lambda i: (i, 0))],
        out_shape=[jax.ShapeDtypeStruct((m, n_out), F32), jax.ShapeDtypeStruct((m, cw_), F32)],
        scratch_shapes=[pltpu.VMEM((SUBLANES, cw_), F32)],
        compiler_params=pltpu.CompilerParams(dimension_semantics=("arbitrary",)),
    )(a, w4, taps, bias)


def _mm_nt_sh(dy, w4, norm, name, plan=None):
    dys = dy if isinstance(dy, (tuple, list)) else (dy,)
    h, g, dres = norm
    m = dys[0].shape[0]
    s, k, n = w4.shape
    tm = _pick(m, (640, 512, 256, 128))
    where = _shard_columns(dys, s, n)
    p_in, p_shapes, p_out, p_scr = _plan_parts(plan)
    nd, ni = len(dys), m // tm

    def body(*refs):
        w_ref, h_ref, g_ref, dres_ref = refs[nd:nd + 4]
        cins = refs[nd + 4:nd + 4 + len(p_in)]
        dh_ref, dg_ref = refs[nd + 4 + len(p_in):nd + 6 + len(p_in)]
        couts = refs[nd + 6 + len(p_in):nd + 6 + len(p_in) + len(p_out)]
        sems = refs[nd + 6 + len(p_in) + len(p_out):]
        i = pl.program_id(0)
        if plan is not None:
            @pl.when(i == 0)
            def _():
                plan.start(cins, couts, sems)

        du = None
        for j, segs in enumerate(where):
            for p, c0, o0, w in segs:
                wj = w_ref[j] if w == n else w_ref[j, :, o0:o0 + w]
                t = _dot_nt(refs[p][:, c0:c0 + w].astype(BF16), wj)
                du = t if du is None else du + t
        dx, dgt = _rms_bwd(h_ref[...], g_ref[...], du)
        dh_ref[...] = jnp.where(_real_rows(i, tm), dres_ref[...] + dx, 0.0)
        _acc_add(i == 0, dg_ref, jnp.sum(dgt, axis=0, keepdims=True))
        if plan is not None:
            @pl.when(i == ni - 1)
            def _():
                plan.wait(cins, couts, sems)

    row = pl.BlockSpec((tm, k), lambda i: (i, 0))
    vec = pl.BlockSpec((1, k), lambda i: (0, 0))
    res = pl.pallas_call(
        body, name=name, grid=(ni,),
        in_specs=[pl.BlockSpec((tm, d.shape[1]), lambda i: (i, 0)) for d in dys]
        + [pl.BlockSpec((s, k, n), lambda i: (0, 0, 0)), row, vec, row] + p_in,
        out_specs=[row, vec] + p_out,
        out_shape=[jax.ShapeDtypeStruct((m, k), F32), jax.ShapeDtypeStruct((1, k), F32)] + p_shapes,
        scratch_shapes=p_scr,
        compiler_params=pltpu.CompilerParams(dimension_semantics=("arbitrary",),
                                             vmem_limit_bytes=_VMEM_LIMIT_WIDE),
    )(*dys, w4, h, g, dres, *(plan.ins if plan is not None else []))
    return (res[0], res[1]), res[2:]


def _shard_columns(dys, s, n):
    segs = [[] for _ in range(s)]
    g0 = 0
    for p, d in enumerate(dys):
        c = 0
        while c < d.shape[1] and g0 + c < s * n:
            j, o = divmod(g0 + c, n)
            w = min(d.shape[1] - c, n - o)
            segs[j].append((p, c, o, w))
            c += w
        g0 += d.shape[1]
    assert g0 >= s * n
    return segs


def _mm_tn_sh(a, dy, n, name):
    dys = dy if isinstance(dy, (tuple, list)) else (dy,)
    m, k = a.shape
    s = N_CHIPS
    tm = _pick(m, (640, 512, 256, 128))
    tk = _col_tile(k, 512)
    nsteps = m // tm
    where = _shard_columns(dys, s, n)

    def body(*refs):
        a_ref, o_ref, acc = refs[0], refs[len(dys) + 1], refs[len(dys) + 2]

        @pl.when(pl.program_id(1) == 0)
        def _():
            acc[...] = jnp.zeros_like(acc)

        av = a_ref[...].astype(BF16)
        for j, segs in enumerate(where):
            for p, c0, o0, w in segs:
                t = _dot_tn(av, refs[1 + p][:, c0:c0 + w].astype(BF16))
                if w == n:
                    acc[j] += t
                else:
                    acc[j, :, o0:o0 + w] += t

        @pl.when(pl.program_id(1) == nsteps - 1)
        def _():
            o_ref[...] = acc[...].astype(o_ref.dtype)

    return pl.pallas_call(
        body, name=name, grid=(k // tk, nsteps),
        in_specs=[pl.BlockSpec((tm, tk), lambda kk, i: (i, kk))]
        + [pl.BlockSpec((tm, d.shape[1]), lambda kk, i: (i, 0)) for d in dys],
        out_specs=pl.BlockSpec((s, tk, n), lambda kk, i: (0, kk, 0)),
        out_shape=jax.ShapeDtypeStruct((s, k, n), BF16),
        scratch_shapes=[pltpu.VMEM((s, tk, n), F32)],
        compiler_params=pltpu.CompilerParams(dimension_semantics=("parallel", "arbitrary"),
                                             vmem_limit_bytes=_VMEM_LIMIT_WIDE),
    )(a, *dys)


def _rowcall(name, body, lp, tm, rows=(), prevs=(), vecs=(), outs=(), accs=(), scratch=(),
             reverse=False, seq=False, plan=None):
    p_in, p_shapes, p_out, p_scr = _plan_parts(plan)
    nt = lp // tm
    hb = tm // SUBLANES

    def ri(i):
        return nt - 1 - i if reverse else i

    in_specs, args = [], []
    for arr, w, cb in rows:
        in_specs.append(pl.BlockSpec((tm, w), lambda i, cb=cb: (ri(i), cb)))
        args.append(arr)
    for arr, w, cb in prevs:
        in_specs.append(pl.BlockSpec((SUBLANES, w), lambda i, cb=cb: (jnp.maximum(ri(i) * hb - 1, 0), cb)))
        args.append(arr)
    for arr in vecs:
        in_specs.append(pl.BlockSpec(arr.shape, lambda i, nd=arr.ndim: (0,) * nd))
        args.append(arr)
    out_shape, out_specs = [], []
    for w, dt in outs:
        out_shape.append(jax.ShapeDtypeStruct((lp, w), dt))
        out_specs.append(pl.BlockSpec((tm, w), lambda i: (ri(i), 0)))
    for shp, dt in accs:
        out_shape.append(jax.ShapeDtypeStruct(shp, dt))
        out_specs.append(pl.BlockSpec(shp, lambda i, nd=len(shp): (0,) * nd))

    n_in, n_out, n_scr = len(args), len(out_shape), len(scratch)

    def kern(*refs):
        i = pl.program_id(0)
        own = (refs[:n_in] + refs[n_in + len(p_in):n_in + len(p_in) + n_out]
               + refs[n_in + len(p_in) + n_out + len(p_out):n_in + len(p_in) + n_out + len(p_out) + n_scr])
        cins = refs[n_in:n_in + len(p_in)]
        couts = refs[n_in + len(p_in) + n_out:n_in + len(p_in) + n_out + len(p_out)]
        sems = refs[n_in + len(p_in) + n_out + len(p_out) + n_scr:]
        if plan is not None:
            @pl.when(i == 0)
            def _():
                plan.start(cins, couts, sems)

        body(ri(i), i == 0, *own)
        if plan is not None:
            @pl.when(i == nt - 1)
            def _():
                plan.wait(cins, couts, sems)

    sem = ("arbitrary",) if (seq or accs or plan is not None) else ("parallel",)
    res = pl.pallas_call(
        kern, name=name, grid=(nt,), in_specs=in_specs + p_in, out_specs=out_specs + p_out,
        out_shape=out_shape + p_shapes, scratch_shapes=list(scratch) + p_scr,
        compiler_params=pltpu.CompilerParams(dimension_semantics=sem),
    )(*args, *(plan.ins if plan is not None else []))
    return res if plan is None else (res[:n_out], res[n_out:])


def _acc_add(first, ref, val):
    @pl.when(first)
    def _():
        ref[...] = jnp.zeros_like(ref)

    ref[...] += val


def _real_rows(r, tm):
    return (r * tm + _row_iota(tm)) >= PAD


def _rmsnorm_fwd(h, g, name):
    lp, d = h.shape
    tm = _pick(lp, (640, 512, 256, 128))

    def body(r, first, h_ref, g_ref, u_ref):
        x = h_ref[...]
        rs = lax.rsqrt(jnp.mean(x * x, axis=-1, keepdims=True) + EPS)
        u_ref[...] = (x * rs * g_ref[...]).astype(u_ref.dtype)

    return _rowcall(name, body, lp, tm, rows=[(h, d, 0)], vecs=[g], outs=[(d, BF16)])[0]


def _mm_postnorm_res(a, w, h, g, name, next_pre=None):
    parts = a if isinstance(a, (tuple, list)) else (a,)
    lp, d = h.shape
    k = w.shape[0]
    tm = _pick(lp, (640, 512, 256, 128))
    offs = [sum(p.shape[1] for p in parts[:i]) for i in range(len(parts))]
    np_ = len(parts)

    def body(*refs):
        w_ref, h_ref, g_ref = refs[np_], refs[np_ + 1], refs[np_ + 2]
        rest = refs[np_ + 3:]
        acc = None
        for a_ref, p, off in zip(refs, parts, offs):
            t = _dot(a_ref[...].astype(BF16), w_ref[off:off + p.shape[1], :])
            acc = t if acc is None else acc + t
        outs = rest[1:] if next_pre is not None else rest
        outs[0][...] = acc
        rs = lax.rsqrt(jnp.mean(acc * acc, axis=-1, keepdims=True) + EPS)
        hn = jnp.where(_real_rows(pl.program_id(0), tm), h_ref[...] + acc * rs * g_ref[...], 0.0)
        outs[1][...] = hn
        if next_pre is not None:
            rs2 = lax.rsqrt(jnp.mean(hn * hn, axis=-1, keepdims=True) + EPS)
            outs[2][...] = (hn * rs2 * rest[0][...]).astype(BF16)

    row = pl.BlockSpec((tm, d), lambda i: (i, 0))
    vec = pl.BlockSpec((1, d), lambda i: (0, 0))
    n_vec = 2 if next_pre is not None else 1
    return pl.pallas_call(
        body, name=name, grid=(lp // tm,),
        in_specs=[pl.BlockSpec((tm, p.shape[1]), lambda i: (i, 0)) for p in parts]
        + [pl.BlockSpec((k, d), lambda i: (0, 0)), row] + [vec] * n_vec,
        out_specs=[row] * (2 + (next_pre is not None)),
        out_shape=[jax.ShapeDtypeStruct((lp, d), F32)] * 2
        + ([jax.ShapeDtypeStruct((lp, d), BF16)] if next_pre is not None else []),
        compiler_params=pltpu.CompilerParams(dimension_semantics=("parallel",)),
    )(*parts, w, h, g, *([next_pre] if next_pre is not None else []))


def _mm_postnorm_loss(a, w, h, g, tgt, name):
    lp, d = h.shape
    k = w.shape[0]
    tm = _pick(lp, (640, 512, 256, 128))

    def body(a_ref, w_ref, h_ref, g_ref, t_ref, o_ref, dh_ref, ls_ref):
        i = pl.program_id(0)
        acc = _dot(a_ref[...].astype(BF16), w_ref[...])
        o_ref[...] = acc
        rs = lax.rsqrt(jnp.mean(acc * acc, axis=-1, keepdims=True) + EPS)
        tok = (i * tm + _row_iota(tm)) >= BLOCK
        e = jnp.where(tok, h_ref[...] + acc * rs * g_ref[...] - t_ref[...], 0.0)
        dh_ref[...] = e * (1.0 / d)
        _acc_add(i == 0, ls_ref, jnp.sum(e * e, axis=0, keepdims=True))

    row = pl.BlockSpec((tm, d), lambda i: (i, 0))
    vec = pl.BlockSpec((1, d), lambda i: (0, 0))
    return pl.pallas_call(
        body, name=name, grid=(lp // tm,),
        in_specs=[pl.BlockSpec((tm, k), lambda i: (i, 0)), pl.BlockSpec((k, d), lambda i: (0, 0)),
                  row, vec, row],
        out_specs=[row, row, vec],
        out_shape=[jax.ShapeDtypeStruct((lp, d), F32), jax.ShapeDtypeStruct((lp, d), F32),
                   jax.ShapeDtypeStruct((1, d), F32)],
        compiler_params=pltpu.CompilerParams(dimension_semantics=("arbitrary",)),
    )(a, w, h, g, tgt)


def _conv_tiles(lp, width):
    wc = _col_tile(width, 1408)
    tm = _pick(lp, (320, 256, 128))
    return tm, wc


def _conv_bwd(x, col_off, width, dy, w, name, w_col_off=0):
    lp = x.shape[0]
    kk = w.shape[0]
    tm, wc = _conv_tiles(lp, width)
    offb = col_off // wc
    woffb = w_col_off // wc
    assert col_off % wc == 0 and w_col_off % wc == 0
    hrows = SUBLANES * (4 // dy.dtype.itemsize)
    ext = tm + hrows

    def body(x_ref, dy_ref, dn_ref, w_ref, dx_ref, dw_ref, db_ref):
        i = pl.program_id(1)
        last = pl.num_programs(1) - 1
        xv = x_ref[...]
        dyv = dy_ref[...].astype(F32)
        dd = jnp.concatenate([dyv, jnp.where(i < last, dn_ref[...].astype(F32), 0.0)], axis=0)
        dx = w_ref[kk - 1:kk, :] * dyv
        rows = [jnp.sum(dyv * xv, axis=0, keepdims=True)]
        for m in range(1, kk):
            ahead = pltpu.roll(dd, ext - m, 0)[:tm, :]
            dx = dx + w_ref[kk - 1 - m:kk - m, :] * ahead
            rows.append(jnp.sum(ahead * xv, axis=0, keepdims=True))
        dx_ref[...] = dx.astype(dx_ref.dtype)
        dwp = jnp.concatenate(rows[::-1] + [jnp.zeros((SUBLANES - kk, wc), F32)], axis=0)

        @pl.when(i == 0)
        def _():
            dw_ref[...] = jnp.zeros_like(dw_ref)
            db_ref[...] = jnp.zeros_like(db_ref)

        dw_ref[...] += dwp
        db_ref[...] += jnp.sum(dyv, axis=0, keepdims=True)

    return pl.pallas_call(
        body, name=name, grid=(width // wc, lp // tm),
        in_specs=[pl.BlockSpec((tm, wc), lambda j, i: (i, offb + j)),
                  pl.BlockSpec((tm, wc), lambda j, i: (i, j)),
                  pl.BlockSpec((hrows, wc), lambda j, i: (jnp.minimum((i + 1) * (tm // hrows), lp // hrows - 1), j)),
                  pl.BlockSpec((kk, wc), lambda j, i: (0, woffb + j))],
        out_specs=[pl.BlockSpec((tm, wc), lambda j, i: (i, j)),
                   pl.BlockSpec((SUBLANES, wc), lambda j, i: (0, j)),
                   pl.BlockSpec((1, wc), lambda j, i: (0, j))],
        out_shape=[jax.ShapeDtypeStruct((lp, width), BF16),
                   jax.ShapeDtypeStruct((SUBLANES, width), F32),
                   jax.ShapeDtypeStruct((1, width), F32)],
        compiler_params=pltpu.CompilerParams(dimension_semantics=("parallel", "arbitrary")),
    )(x, dy, dy, w)


_FFN_WC = 1408


def _ffn_up_convact(u, w4, cw, cb, name, plan=None):
    lp, k = u.shape
    s, _, n = w4.shape
    tm = _pick(lp, (320, 256, 128))
    ni = lp // tm
    p_in, p_shapes, p_out, p_scr = _plan_parts(plan)

    def body(*refs):
        u_ref, w_ref, cw_ref, cb_ref = refs[:4]
        cins = refs[4:4 + len(p_in)]
        hp_ref, a_ref, hg_ref, hu_ref = refs[4 + len(p_in):8 + len(p_in)]
        couts = refs[8 + len(p_in):8 + len(p_in) + len(p_out)]
        carry = refs[8 + len(p_in) + len(p_out)]
        sems = refs[9 + len(p_in) + len(p_out):]
        i = pl.program_id(0)

        @pl.when(i == 0)
        def _():
            carry[...] = jnp.zeros_like(carry)
            if plan is not None:
                plan.start(cins, couts, sems)

        uv = u_ref[...].astype(BF16)
        for j in range(s):
            hp_ref[:, j * n:(j + 1) * n] = _dot(uv, w_ref[j])

        def conv(cols):
            x_ext = jnp.concatenate([carry[:, cols], hp_ref[:, cols]], axis=0)
            w = cw_ref[:, cols]
            y = (cb_ref[:, cols] + w[2:3, :] * x_ext + w[1:2, :] * pltpu.roll(x_ext, 1, 0)
                 + w[0:1, :] * pltpu.roll(x_ext, 2, 0))
            return y[SUBLANES:, :]

        for c in range(D_FF // n):
            gs = slice(c * n, (c + 1) * n)
            hg = conv(gs)
            hu = conv(slice(D_FF + c * n, D_FF + (c + 1) * n))
            a_ref[:, gs] = (_gelu(hg) * hu).astype(a_ref.dtype)
            hg_ref[:, gs] = hg.astype(hg_ref.dtype)
            hu_ref[:, gs] = hu.astype(hu_ref.dtype)
        carry[...] = hp_ref[tm - SUBLANES:tm, :]
        if plan is not None:
            @pl.when(i == ni - 1)
            def _():
                plan.wait(cins, couts, sems)

    half = pl.BlockSpec((tm, D_FF), lambda i: (i, 0))
    res = pl.pallas_call(
        body, name=name, grid=(ni,),
        in_specs=[pl.BlockSpec((tm, k), lambda i: (i, 0)), pl.BlockSpec((s, k, n), lambda i: (0, 0, 0)),
                  pl.BlockSpec(cw.shape, lambda i: (0, 0)), pl.BlockSpec(cb.shape, lambda i: (0, 0))] + p_in,
        out_specs=[pl.BlockSpec((tm, 2 * D_FF), lambda i: (i, 0)), half, half, half] + p_out,
        out_shape=[jax.ShapeDtypeStruct((lp, 2 * D_FF), F32)] + [jax.ShapeDtypeStruct((lp, D_FF), BF16)] * 3
        + p_shapes,
        scratch_shapes=[pltpu.VMEM((SUBLANES, 2 * D_FF), F32)] + p_scr,
        compiler_params=pltpu.CompilerParams(dimension_semantics=("arbitrary",),
                                             vmem_limit_bytes=_VMEM_LIMIT_WIDE),
    )(u, w4, cw, cb, *(plan.ins if plan is not None else []))
    return res[:4], res[4:]


def _postnorm_bwd_tile(o_ref, gain_ref, dh_ref, do_ref, dgain_ref):
    first = pl.program_id(0) == 0
    dx, dgt = _rms_bwd(o_ref[...], gain_ref[...], dh_ref[...])
    dob = dx.astype(BF16)
    do_ref[...] = dob
    _acc_add(first, dgain_ref, jnp.sum(dgt, axis=0, keepdims=True))
    return dob


def _ffn_down_act_bwd(o, gain, dh, w_down, hg, hu, name):
    lp, d = o.shape
    tm = _pick(lp, (320, 256, 128))
    tk = _FFN_WC

    def body(o_ref, gain_ref, dh_ref, w_ref, g_ref, u_ref, dg_ref, du_ref, do_ref, dgain_ref):
        dob = _postnorm_bwd_tile(o_ref, gain_ref, dh_ref, do_ref, dgain_ref)
        for j in range(D_FF // tk):
            cs = slice(j * tk, (j + 1) * tk)
            da = _dot_nt(dob, w_ref[cs, :])
            gl, dgl = _gelu_and_grad(g_ref[:, cs].astype(F32))
            dg_ref[:, cs] = (da * u_ref[:, cs].astype(F32) * dgl).astype(dg_ref.dtype)
            du_ref[:, cs] = (da * gl).astype(du_ref.dtype)

    wide = pl.BlockSpec((tm, D_FF), lambda i: (i, 0))
    row = pl.BlockSpec((tm, d), lambda i: (i, 0))
    vec = pl.BlockSpec((1, d), lambda i: (0, 0))
    return pl.pallas_call(
        body, name=name, grid=(lp // tm,),
        in_specs=[row, vec, row, pl.BlockSpec((D_FF, d), lambda i: (0, 0)), wide, wide],
        out_specs=[wide, wide, row, vec],
        out_shape=[jax.ShapeDtypeStruct((lp, D_FF), BF16)] * 2
        + [jax.ShapeDtypeStruct((lp, d), BF16), jax.ShapeDtypeStruct((1, d), F32)],
        compiler_params=pltpu.CompilerParams(dimension_semantics=("arbitrary",),
                                             vmem_limit_bytes=_VMEM_LIMIT_WIDE),
    )(o, gain, dh, w_down, hg, hu)


def _lru_gates(x, wa_ref, wx_ref, ba, bx, lam):
    xb = x.astype(BF16)
    za, zx = [], []
    for n in range(LRU_BLOCKS):
        xs = xb[:, n * LRU_BS:(n + 1) * LRU_BS]
        za.append(_dot(xs, wa_ref[n]))
        zx.append(_dot(xs, wx_ref[n]))
    r = _sigmoid(jnp.concatenate(za, axis=1) + ba)
    ig = _sigmoid(jnp.concatenate(zx, axis=1) + bx)
    sp = _softplus(-lam)
    log_a = -LRU_C * r * sp
    a = jnp.exp(log_a)
    om = _neg_expm1(2.0 * log_a)
    mult = jnp.sqrt(om)
    return xb, r, ig, sp, a, om, mult


def _lru_fwd(proj, xrc, wa, wx, ba, bx, lam, name, plan=None):
    lp, d = xrc.shape
    tm = BLOCK

    def body(r_idx, first, gate_ref, x_ref, wa_ref, wx_ref, ba_ref, bx_ref, lam_ref,
             y_ref, h_ref, carry):
        @pl.when(first)
        def _():
            carry[...] = jnp.zeros_like(carry)

        x = x_ref[...]
        _, _, ig, _, a, _, mult = _lru_gates(x, wa_ref, wx_ref, ba_ref[...], bx_ref[...], lam_ref[...])
        u = jnp.where(_real_rows(r_idx, tm), mult * ig * x, 0.0)
        acum, hloc = _scan_fwd(a, u, tm)
        h = hloc + acum * carry[0:1, :]
        h_ref[...] = h
        carry[0:1, :] = h[tm - 1:tm, :]
        y_ref[...] = (_gelu(gate_ref[...]) * h).astype(y_ref.dtype)

    return _rowcall(name, body, lp, tm, rows=[(proj, d, 0), (xrc, d, 0)],
                    vecs=[wa, wx, ba, bx, lam], outs=[(d, BF16), (d, F32)],
                    scratch=[pltpu.VMEM((SUBLANES, d), F32)], seq=True, plan=plan)


def _lru_bwd(proj, xrc, hl, dmix, wa, wx, ba, bx, lam, name, plan=None):
    lp, d = xrc.shape
    tm = BLOCK

    def body(r_idx, first, gate_ref, x_ref, h_ref, dy_ref, hp_ref, wa_ref, wx_ref, ba_ref, bx_ref,
             lam_ref, dgate_ref, dx_ref, dwa_ref, dwx_ref, dba_ref, dbx_ref, dlam_ref, carry):
        @pl.when(first)
        def _():
            carry[...] = jnp.zeros_like(carry)
            dwa_ref[...] = jnp.zeros_like(dwa_ref)
            dwx_ref[...] = jnp.zeros_like(dwx_ref)
            dba_ref[...] = jnp.zeros_like(dba_ref)
            dbx_ref[...] = jnp.zeros_like(dbx_ref)
            dlam_ref[...] = jnp.zeros_like(dlam_ref)

        x = x_ref[...]
        lam = lam_ref[...]
        xb, r, ig, sp, a, om, mult = _lru_gates(x, wa_ref, wx_ref, ba_ref[...], bx_ref[...], lam)
        h = h_ref[...]
        dy = dy_ref[...]
        gl, dgl = _gelu_and_grad(gate_ref[...])
        dgate_ref[...] = (dy * h * dgl).astype(dgate_ref.dtype)
        row = _row_iota(tm)
        lastrow = row == tm - 1
        xg = dy * gl + jnp.where(lastrow, carry[0:1, :], 0.0)
        c = jnp.where(lastrow, 1.0, pltpu.roll(a, tm - 1, 0))
        _, g = _scan_rev(c, xg, tm)
        carry[0:1, :] = a[0:1, :] * g[0:1, :]
        hprev_in = jnp.where(r_idx > 0, hp_ref[SUBLANES - 1:SUBLANES, :], 0.0)
        hprev = jnp.where(row == 0, hprev_in, pltpu.roll(h, 1, 0))
        du = jnp.where(_real_rows(r_idx, tm), g, 0.0)
        da = g * hprev
        dmult = du * ig * x
        dig = du * mult * x
        dxv = du * mult * ig
        e2 = 1.0 - om
        dlog_a = da * a - dmult * e2 / mult
        dr = dlog_a * (-LRU_C) * sp
        dsp = jnp.sum(dlog_a * (-LRU_C) * r, axis=0, keepdims=True)
        dlam_ref[...] += -dsp * _sigmoid(-lam)
        dza = dr * r * (1.0 - r)
        dzx = dig * ig * (1.0 - ig)
        dba_ref[...] += jnp.sum(dza, axis=0, keepdims=True)
        dbx_ref[...] += jnp.sum(dzx, axis=0, keepdims=True)
        dzab = dza.astype(BF16)
        dzxb = dzx.astype(BF16)
        parts = []
        for n in range(LRU_BLOCKS):
            sl = slice(n * LRU_BS, (n + 1) * LRU_BS)
            dwa_ref[n] += _dot_tn(xb[:, sl], dzab[:, sl])
            dwx_ref[n] += _dot_tn(xb[:, sl], dzxb[:, sl])
            parts.append(_dot_nt(dzab[:, sl], wa_ref[n]) + _dot_nt(dzxb[:, sl], wx_ref[n]))
        dx_ref[...] = dxv + jnp.concatenate(parts, axis=1)

    return _rowcall(name, body, lp, tm,
                    rows=[(proj, d, 0), (xrc, d, 0), (hl, d, 0), (dmix, d, 0)],
                    prevs=[(hl, d, 0)], vecs=[wa, wx, ba, bx, lam],
                    outs=[(d, BF16), (d, F32)],
                    accs=[((LRU_BLOCKS, LRU_BS, LRU_BS), F32), ((LRU_BLOCKS, LRU_BS, LRU_BS), F32),
                          ((1, d), F32), ((1, d), F32), ((1, d), F32)],
                    scratch=[pltpu.VMEM((SUBLANES, d), F32)], reverse=True, seq=True, plan=plan)


_SLOPES = [2.0 ** (-8.0 * (h + 1) / N_Q_HEADS) for h in range(N_Q_HEADS)]
_QK_SCALE = HEAD_DIM ** -0.5
_QCOL = 2 * D_MODEL // D_MODEL
_KCOL = (3 * D_MODEL) // LANES
_VCOL = _KCOL + 1


def _attn_masks(n):
    start = pl.multiple_of(jnp.maximum(n - 1, 0) * BLOCK, BLOCK)
    qi = n * BLOCK + lax.broadcasted_iota(jnp.int32, (BLOCK, 2 * BLOCK), 0)
    kj = start + lax.broadcasted_iota(jnp.int32, (BLOCK, 2 * BLOCK), 1)
    dist = qi - kj
    ok = (kj >= BLOCK) & (dist >= 0) & (dist < WINDOW)
    dm = (n * BLOCK - PAD + lax.broadcasted_iota(jnp.int32, (BLOCK, N_META), 0)
          - lax.broadcasted_iota(jnp.int32, (BLOCK, N_META), 1))
    okm = dm >= 0
    return start, ok, dist.astype(F32), okm, jnp.minimum(dm, WINDOW).astype(F32)


def _group_rows(ref, g, scale=None):
    x = jnp.concatenate(
        [ref[:, (g * Q_PER_KV + hh) * HEAD_DIM:(g * Q_PER_KV + hh + 1) * HEAD_DIM] for hh in range(Q_PER_KV)],
        axis=0)
    return (x if scale is None else x * scale).astype(BF16)


def _attn_probs(s, sm, sink_ref, g, ok, distf, okm, dmf):
    slope = jnp.stack([jnp.full((1, 1), _SLOPES[g * Q_PER_KV + hh], F32) for hh in range(Q_PER_KV)])
    sink = jnp.stack([sink_ref[0:1, g * Q_PER_KV + hh:g * Q_PER_KV + hh + 1] for hh in range(Q_PER_KV)])
    s = s.reshape(Q_PER_KV, BLOCK, 2 * BLOCK)
    sm = sm.reshape(Q_PER_KV, BLOCK, N_META)
    s = jnp.where(ok[None], s - slope * distf[None], NEG)
    sm = jnp.where(okm[None], sm - slope * dmf[None], NEG)
    mx = jnp.maximum(jnp.maximum(jnp.max(s, axis=-1, keepdims=True),
                                 jnp.max(sm, axis=-1, keepdims=True)), sink)
    p = jnp.exp(s - mx)
    pm = jnp.exp(sm - mx)
    ps = jnp.exp(sink - mx)
    inv = 1.0 / (jnp.sum(p, axis=-1, keepdims=True) + jnp.sum(pm, axis=-1, keepdims=True) + ps)
    return p, pm, ps, inv


def _attn_fwd(proj, sinks, name, plan=None):
    lp = proj.shape[0]
    nblk = lp // BLOCK
    p_in, p_shapes, p_out, p_scr = _plan_parts(plan)

    def body(*refs):
        q_ref, k_ref, v_ref, sink_ref = refs[:4]
        cins = refs[4:4 + len(p_in)]
        o_ref = refs[4 + len(p_in)]
        couts = refs[5 + len(p_in):5 + len(p_in) + len(p_out)]
        sems = refs[5 + len(p_in) + len(p_out):]
        n = pl.program_id(0)
        if plan is not None:
            @pl.when(n == 0)
            def _():
                plan.start(cins, couts, sems)

        start, ok, distf, okm, dmf = _attn_masks(n)
        kb = k_ref[pl.ds(start, 2 * BLOCK), :].astype(BF16)
        vb = v_ref[pl.ds(start, 2 * BLOCK), :].astype(BF16)
        km = k_ref[PAD:BLOCK, :].astype(BF16)
        vm = v_ref[PAD:BLOCK, :].astype(BF16)
        rows = Q_PER_KV * BLOCK
        gsl = [slice(g * HEAD_DIM, (g + 1) * HEAD_DIM) for g in range(N_KV_HEADS)]
        raw = []
        for g in range(N_KV_HEADS):
            qg = _group_rows(q_ref, g, _QK_SCALE)
            raw.append((_dot_nt(qg, kb[:, gsl[g]]), _dot_nt(qg, km[:, gsl[g]])))
        for g in range(N_KV_HEADS):
            gs = gsl[g]
            p, pm, _, inv = _attn_probs(raw[g][0], raw[g][1], sink_ref, g, ok, distf, okm, dmf)
            o = (_dot(p.astype(BF16).reshape(rows, 2 * BLOCK), vb[:, gs])
                 + _dot(pm.astype(BF16).reshape(rows, N_META), vm[:, gs])) * inv.reshape(rows, 1)
            for hh in range(Q_PER_KV):
                h = g * Q_PER_KV + hh
                o_ref[:, h * HEAD_DIM:(h + 1) * HEAD_DIM] = o[hh * BLOCK:(hh + 1) * BLOCK, :].astype(o_ref.dtype)
        if plan is not None:
            @pl.when(n == nblk - 1)
            def _():
                plan.wait(cins, couts, sems)

    res = pl.pallas_call(
        body, name=name, grid=(nblk,),
        in_specs=[pl.BlockSpec((BLOCK, D_MODEL), lambda n: (n, _QCOL)),
                  pl.BlockSpec((lp, LANES), lambda n: (0, _KCOL)),
                  pl.BlockSpec((lp, LANES), lambda n: (0, _VCOL)),
                  pl.BlockSpec(sinks.shape, lambda n: (0, 0))] + p_in,
        out_specs=[pl.BlockSpec((BLOCK, D_MODEL), lambda n: (n, 0))] + p_out,
        out_shape=[jax.ShapeDtypeStruct((lp, D_MODEL), BF16)] + p_shapes,
        scratch_shapes=p_scr,
        compiler_params=pltpu.CompilerParams(dimension_semantics=("arbitrary",)),
    )(proj, proj, proj, sinks, *(plan.ins if plan is not None else []))
    return res[0], res[1:]


def _attn_bwd(proj, sinks, dmix, name, plan=None):
    lp = proj.shape[0]
    nblk = lp // BLOCK

    p_in, p_shapes, p_out, p_scr = _plan_parts(plan)

    def body(*refs):
        q_ref, k_ref, v_ref, sink_ref, dy_ref = refs[:5]
        cins = refs[5:5 + len(p_in)]
        dq_ref, dk_ref, dv_ref, ds_ref = refs[5 + len(p_in):9 + len(p_in)]
        couts = refs[9 + len(p_in):9 + len(p_in) + len(p_out)]
        sems = refs[9 + len(p_in) + len(p_out):]
        n = pl.program_id(0)

        @pl.when(n == 0)
        def _():
            dk_ref[...] = jnp.zeros_like(dk_ref)
            dv_ref[...] = jnp.zeros_like(dv_ref)
            ds_ref[...] = jnp.zeros_like(ds_ref)
            if plan is not None:
                plan.start(cins, couts, sems)

        start, ok, distf, okm, dmf = _attn_masks(n)
        kb = k_ref[pl.ds(start, 2 * BLOCK), :].astype(BF16)
        vb = v_ref[pl.ds(start, 2 * BLOCK), :].astype(BF16)
        km = k_ref[PAD:BLOCK, :].astype(BF16)
        vm = v_ref[PAD:BLOCK, :].astype(BF16)
        lane16 = lax.broadcasted_iota(jnp.int32, (1, N_Q_HEADS), 1)
        dsink = jnp.zeros((1, N_Q_HEADS), F32)
        rows = Q_PER_KV * BLOCK
        gsl = [slice(g * HEAD_DIM, (g + 1) * HEAD_DIM) for g in range(N_KV_HEADS)]
        pre = []
        for g in range(N_KV_HEADS):
            qg = _group_rows(q_ref, g, _QK_SCALE)
            dog = _group_rows(dy_ref, g)
            pre.append((qg, dog, _dot_nt(qg, kb[:, gsl[g]]), _dot_nt(qg, km[:, gsl[g]]),
                        _dot_nt(dog, vb[:, gsl[g]]), _dot_nt(dog, vm[:, gsl[g]])))
        for g in range(N_KV_HEADS):
            gs = gsl[g]
            qg, dog, s_raw, sm_raw, dp, dpm = pre[g]
            p, pm, ps, inv = _attn_probs(s_raw, sm_raw, sink_ref, g, ok, distf, okm, dmf)
            pn, pmn, psn = p * inv, pm * inv, ps * inv
            dp = dp.reshape(Q_PER_KV, BLOCK, 2 * BLOCK)
            dpm = dpm.reshape(Q_PER_KV, BLOCK, N_META)
            delta = (jnp.sum(pn * dp, axis=-1, keepdims=True)
                     + jnp.sum(pmn * dpm, axis=-1, keepdims=True))
            dsb = (pn * (dp - delta)).astype(BF16).reshape(rows, 2 * BLOCK)
            dsm = (pmn * (dpm - delta)).astype(BF16).reshape(rows, N_META)
            dsk = jnp.sum(psn * delta, axis=1, keepdims=True)
            for hh in range(Q_PER_KV):
                dsink = dsink - jnp.where(lane16 == g * Q_PER_KV + hh, dsk[hh], 0.0)
            dq = (_dot(dsb, kb[:, gs]) + _dot(dsm, km[:, gs])) * _QK_SCALE
            for hh in range(Q_PER_KV):
                h = g * Q_PER_KV + hh
                dq_ref[:, h * HEAD_DIM:(h + 1) * HEAD_DIM] = dq[hh * BLOCK:(hh + 1) * BLOCK, :].astype(dq_ref.dtype)
            pnb = pn.astype(BF16).reshape(rows, 2 * BLOCK)
            pmnb = pmn.astype(BF16).reshape(rows, N_META)
            dk_ref[pl.ds(start, 2 * BLOCK), gs] += _dot_tn(dsb, qg)
            dv_ref[pl.ds(start, 2 * BLOCK), gs] += _dot_tn(pnb, dog)
            dk_ref[PAD:BLOCK, gs] += _dot_tn(dsm, qg)
            dv_ref[PAD:BLOCK, gs] += _dot_tn(pmnb, dog)
        ds_ref[...] += dsink
        if plan is not None:
            @pl.when(n == nblk - 1)
            def _():
                plan.wait(cins, couts, sems)

    res = pl.pallas_call(
        body, name=name, grid=(nblk,),
        in_specs=[pl.BlockSpec((BLOCK, D_MODEL), lambda n: (n, _QCOL)),
                  pl.BlockSpec((lp, LANES), lambda n: (0, _KCOL)),
                  pl.BlockSpec((lp, LANES), lambda n: (0, _VCOL)),
                  pl.BlockSpec(sinks.shape, lambda n: (0, 0)),
                  pl.BlockSpec((BLOCK, D_MODEL), lambda n: (n, 1))] + p_in,
        out_specs=[pl.BlockSpec((BLOCK, D_MODEL), lambda n: (n, 0)),
                   pl.BlockSpec((lp, LANES), lambda n: (0, 0)),
                   pl.BlockSpec((lp, LANES), lambda n: (0, 0)),
                   pl.BlockSpec((1, N_Q_HEADS), lambda n: (0, 0))] + p_out,
        out_shape=[jax.ShapeDtypeStruct((lp, D_MODEL), BF16),
                   jax.ShapeDtypeStruct((lp, LANES), F32),
                   jax.ShapeDtypeStruct((lp, LANES), F32),
                   jax.ShapeDtypeStruct((1, N_Q_HEADS), F32)] + p_shapes,
        scratch_shapes=p_scr,
        compiler_params=pltpu.CompilerParams(dimension_semantics=("arbitrary",)),
    )(proj, proj, proj, sinks, dmix, *(plan.ins if plan is not None else []))
    return res[:4], res[4:]


_ZW = D_SSM
_XBC_W = D_SSM + 2 * SSD_GROUPS * SSD_N
_DT_COL = (_ZW + _XBC_W) // LANES
EVEN_IN = 3 * D_MODEL + 2 * LANES
ODD_IN = _ZW + _XBC_W + SSD_HEADS
ODD_IN_PAD = _ZW + _XBC_W + LANES


def _ssm_convprep_fwd(proj, cw, cb, dt_bias, name):
    lp = proj.shape[0]
    kk = cw.shape[0]
    tm, wc = _conv_tiles(lp, _XBC_W)
    offb = _ZW // wc
    nj = _XBC_W // wc
    hb = tm // SUBLANES

    def body(x_ref, xp_ref, dtr_ref, w_ref, b_ref, bias_ref, xc_ref, act_ref, dt_ref):
        i, j = pl.program_id(0), pl.program_id(1)
        real = _real_rows(i, tm)
        xv = x_ref[...]
        xx = jnp.concatenate([jnp.where(i > 0, xp_ref[...], 0.0), xv], axis=0)
        acc = b_ref[...] + w_ref[kk - 1:kk, :] * xv
        for m in range(1, kk):
            acc = acc + w_ref[kk - 1 - m:kk - m, :] * pltpu.roll(xx, m, 0)[SUBLANES:, :]
        xc_ref[...] = acc
        act, _ = _silu_and_grad(acc)
        act_ref[...] = jnp.where(real, act, 0.0)

        @pl.when(j == 0)
        def _():
            dt_ref[...] = jnp.where(real, _softplus(dtr_ref[...] + bias_ref[...]), 0.0)

    return pl.pallas_call(
        body, name=name, grid=(lp // tm, nj),
        in_specs=[pl.BlockSpec((tm, wc), lambda i, j: (i, offb + j)),
                  pl.BlockSpec((SUBLANES, wc), lambda i, j: (jnp.maximum(i * hb - 1, 0), offb + j)),
                  pl.BlockSpec((tm, LANES), lambda i, j: (i, _DT_COL)),
                  pl.BlockSpec((kk, wc), lambda i, j: (0, j)),
                  pl.BlockSpec((1, wc), lambda i, j: (0, j)),
                  pl.BlockSpec((1, LANES), lambda i, j: (0, 0))],
        out_specs=[pl.BlockSpec((tm, wc), lambda i, j: (i, j)),
                   pl.BlockSpec((tm, wc), lambda i, j: (i, j)),
                   pl.BlockSpec((tm, LANES), lambda i, j: (i, 0))],
        out_shape=[jax.ShapeDtypeStruct((lp, _XBC_W), F32), jax.ShapeDtypeStruct((lp, _XBC_W), F32),
                   jax.ShapeDtypeStruct((lp, LANES), F32)],
        compiler_params=pltpu.CompilerParams(dimension_semantics=("parallel", "arbitrary")),
    )(proj, proj, proj, cw, cb, dt_bias)


def _ssd_common(dt, alog):
    a = -jnp.exp(alog)
    cs = _cumsum_rows(dt * a, BLOCK)
    cst = cs.T
    cl = cs[BLOCK - 1:BLOCK, :]
    tril = (lax.broadcasted_iota(jnp.int32, (BLOCK, BLOCK), 0)
            >= lax.broadcasted_iota(jnp.int32, (BLOCK, BLOCK), 1))
    return a, cs, cst, cl, jnp.exp(cs), jnp.exp(cl - cs), jnp.exp(cl), tril


def _head_cols(ecl, g):
    lane = lax.broadcasted_iota(jnp.int32, (1, SSD_HPG * SSD_P), 1)
    e = [ecl[:, SSD_HPG * g + hh:SSD_HPG * g + hh + 1] for hh in range(SSD_HPG)]
    return jnp.where(lane < SSD_P, e[0], jnp.where(lane < 2 * SSD_P, e[1],
                                                   jnp.where(lane < 3 * SSD_P, e[2], e[3])))


def _ssd_fwd(xbc, dt, alog, name, plan=None):
    lp = xbc.shape[0]
    nc = lp // BLOCK
    gw = SSD_HPG * SSD_P
    p_in, p_shapes, p_out, p_scr = _plan_parts(plan)

    def body(*refs):
        xs_ref, b_ref, c_ref, dt_ref, alog_ref = refs[:5]
        cins = refs[5:5 + len(p_in)]
        y_ref, so_ref = refs[5 + len(p_in):7 + len(p_in)]
        couts = refs[7 + len(p_in):7 + len(p_in) + len(p_out)]
        st, fx = refs[7 + len(p_in) + len(p_out):9 + len(p_in) + len(p_out)]
        sems = refs[9 + len(p_in) + len(p_out):]
        n = pl.program_id(0)

        @pl.when(n == 0)
        def _():
            st[...] = jnp.zeros_like(st)
            if plan is not None:
                plan.start(cins, couts, sems)

        dtv = dt_ref[...]
        _, cs, cst, cl, e, f, ecl, tril = _ssd_common(dtv, alog_ref[...])
        pre = []
        for g in range(SSD_GROUPS):
            bg = b_ref[:, g * SSD_N:(g + 1) * SSD_N].astype(BF16)
            cg = c_ref[:, g * SSD_N:(g + 1) * SSD_N].astype(BF16)
            stg = st[g]
            so_ref[0, g] = stg
            pre.append((bg, stg, _dot_nt(cg, bg), _dot(cg, stg.astype(BF16))))
        for g in range(SSD_GROUPS):
            bg, stg, gm, yoff = pre[g]
            heads = [SSD_HPG * g + hh for hh in range(SSD_HPG)]
            cols = lambda v: jnp.stack([v[:, h:h + 1] for h in heads])
            x4 = jnp.stack([xs_ref[:, h * SSD_P:(h + 1) * SSD_P] for h in heads])
            csr = jnp.stack([cst[h:h + 1, :] for h in heads])
            m = gm[None] * jnp.exp(jnp.where(tril[None], cols(cs) - csr, NEG))
            xdt = x4 * cols(dtv)
            yoff4 = jnp.stack([yoff[:, hh * SSD_P:(hh + 1) * SSD_P] for hh in range(SSD_HPG)])
            y4 = (jnp.einsum("hls,hsp->hlp", m.astype(BF16), xdt.astype(BF16), preferred_element_type=F32)
                  + cols(e) * yoff4)
            fx4 = cols(f) * xdt
            for hh, h in enumerate(heads):
                y_ref[:, h * SSD_P:(h + 1) * SSD_P] = y4[hh]
                fx[:, hh * SSD_P:(hh + 1) * SSD_P] = fx4[hh]
            st[g] = stg * _head_cols(ecl, g) + _dot_tn(bg, fx[...].astype(BF16))
        if plan is not None:
            @pl.when(n == nc - 1)
            def _():
                plan.wait(cins, couts, sems)

    res = pl.pallas_call(
        body, name=name, grid=(nc,),
        in_specs=[pl.BlockSpec((BLOCK, D_SSM), lambda n: (n, 0)),
                  pl.BlockSpec((BLOCK, 1024), lambda n: (n, 2)),
                  pl.BlockSpec((BLOCK, 1024), lambda n: (n, 3)),
                  pl.BlockSpec((BLOCK, LANES), lambda n: (n, 0)),
                  pl.BlockSpec((1, LANES), lambda n: (0, 0))] + p_in,
        out_specs=[pl.BlockSpec((BLOCK, D_SSM), lambda n: (n, 0)),
                   pl.BlockSpec((1, SSD_GROUPS, SSD_N, gw), lambda n: (n, 0, 0, 0))] + p_out,
        out_shape=[jax.ShapeDtypeStruct((lp, D_SSM), F32),
                   jax.ShapeDtypeStruct((nc, SSD_GROUPS, SSD_N, gw), F32)] + p_shapes,
        scratch_shapes=[pltpu.VMEM((SSD_GROUPS, SSD_N, gw), F32), pltpu.VMEM((BLOCK, gw), F32)] + p_scr,
        compiler_params=pltpu.CompilerParams(dimension_semantics=("arbitrary",)),
    )(xbc, xbc, xbc, dt, alog, *(plan.ins if plan is not None else []))
    return res[:2], res[2:]


def _ssd_bwd(xbc, dt, alog, states, dy, xc, proj, dt_bias, dxskip, name, plan=None):
    lp = xbc.shape[0]
    nc = lp // BLOCK
    gw = SSD_HPG * SSD_P
    p_in, p_shapes, p_out, p_scr = _plan_parts(plan)

    def body(*refs):
        xs_ref, b_ref, c_ref, dt_ref, alog_ref, dy_ref, st_ref, xc_ref, dtr_ref, bias_ref, dsk_ref = refs[:11]
        cins = refs[11:11 + len(p_in)]
        dxc_ref, ddtr_ref, dalog_ref, dbias_ref = refs[11 + len(p_in):15 + len(p_in)]
        couts = refs[15 + len(p_in):15 + len(p_in) + len(p_out)]
        dst, edy, fx, gx = refs[15 + len(p_in) + len(p_out):19 + len(p_in) + len(p_out)]
        sems = refs[19 + len(p_in) + len(p_out):]
        i = pl.program_id(0)

        @pl.when(i == 0)
        def _():
            dst[...] = jnp.zeros_like(dst)
            dalog_ref[...] = jnp.zeros_like(dalog_ref)
            dbias_ref[...] = jnp.zeros_like(dbias_ref)
            if plan is not None:
                plan.start(cins, couts, sems)

        dtv = dt_ref[...]
        a, cs, cst, cl, e, f, ecl, tril = _ssd_common(dtv, alog_ref[...])
        lane = lax.broadcasted_iota(jnp.int32, (1, LANES), 1)
        sub = _row_iota(BLOCK)
        triu = (lax.broadcasted_iota(jnp.int32, (BLOCK, BLOCK), 1)
                >= lax.broadcasted_iota(jnp.int32, (BLOCK, BLOCK), 0))
        dcs = jnp.zeros((BLOCK, LANES), F32)
        dcst = jnp.zeros((LANES, BLOCK), F32)
        dcl = jnp.zeros((1, LANES), F32)
        ddtx = jnp.zeros((BLOCK, LANES), F32)
        pre = []
        for g in range(SSD_GROUPS):
            bg = b_ref[:, g * SSD_N:(g + 1) * SSD_N].astype(BF16)
            cg = c_ref[:, g * SSD_N:(g + 1) * SSD_N].astype(BF16)
            stb = st_ref[0, g].astype(BF16)
            dsob = dst[g].astype(BF16)
            pre.append((bg, cg, stb, dsob, _dot_nt(cg, bg), _dot_nt(bg, cg), _dot(cg, stb), _dot(bg, dsob)))
        for g in range(SSD_GROUPS):
            bg, cg, stb, dsob, gm, gmt, yraw, dfx = pre[g]
            dso = dst[g]
            prodsum = jnp.sum(dso * st_ref[0, g], axis=0, keepdims=True)
            heads = [SSD_HPG * g + hh for hh in range(SSD_HPG)]
            cols = lambda v: jnp.stack([v[:, h:h + 1] for h in heads])
            parts = lambda v: jnp.stack([v[:, hh * SSD_P:(hh + 1) * SSD_P] for hh in range(SSD_HPG)])
            x4 = jnp.stack([xs_ref[:, h * SSD_P:(h + 1) * SSD_P] for h in heads])
            dy4 = jnp.stack([dy_ref[:, h * SSD_P:(h + 1) * SSD_P] for h in heads])
            csc, dtc, ec, fc = cols(cs), cols(dtv), cols(e), cols(f)
            csr = jnp.stack([cst[h:h + 1, :] for h in heads])
            seg = csc - csr
            lam = jnp.exp(jnp.where(tril[None], seg, NEG))
            lamt = jnp.exp(jnp.where(triu[None], -seg, NEG))
            mt = gmt[None] * lamt
            xdt = x4 * dtc
            dyb = dy4.astype(BF16)
            dm = jnp.einsum("hlp,hsp->hls", dyb, xdt.astype(BF16), preferred_element_type=F32)
            dfx4 = parts(dfx)
            dxdt = jnp.einsum("hsl,hlp->hsp", mt.astype(BF16), dyb, preferred_element_type=F32) + fc * dfx4
            dml = dm * lam
            w = dml * gm[None]
            dgm = jnp.sum(dml, axis=0)
            dff = jnp.sum(dfx4 * xdt, axis=2, keepdims=True) * fc
            colv = (jnp.sum(w, axis=2, keepdims=True)
                    + jnp.sum(dy4 * parts(yraw), axis=2, keepdims=True) * ec - dff)
            roww = jnp.sum(w, axis=1, keepdims=True)
            ddtc = jnp.sum(dxdt * x4, axis=2, keepdims=True)
            dffs = jnp.sum(dff, axis=1, keepdims=True)
            dxs4 = dxdt * dtc
            edy4 = ec * dy4
            fx4 = fc * xdt
            for hh, h in enumerate(heads):
                ls = slice(hh * SSD_P, (hh + 1) * SSD_P)
                onl = (lane == h).astype(F32)
                dcs = dcs + colv[hh] * onl
                dcst = dcst - (sub == h).astype(F32) * roww[hh]
                dcl = dcl + (dffs[hh] + ecl[:, h:h + 1] * jnp.sum(prodsum[:, ls], axis=1, keepdims=True)) * onl
                ddtx = ddtx + ddtc[hh] * onl
                gx[:, h * SSD_P:(h + 1) * SSD_P] = dxs4[hh]
                edy[:, ls] = edy4[hh]
                fx[:, ls] = fx4[hh]
            edyb = edy[...].astype(BF16)
            fxb = fx[...].astype(BF16)
            dgb = dgm.astype(BF16)
            gx[:, D_SSM + 1024 + g * SSD_N:D_SSM + 1024 + (g + 1) * SSD_N] = _dot_nt(edyb, stb) + _dot(dgb, bg)
            gx[:, D_SSM + g * SSD_N:D_SSM + (g + 1) * SSD_N] = _dot_nt(fxb, dsob) + _dot_tn(dgb, cg)
            dst[g] = dso * _head_cols(ecl, g) + _dot_tn(cg, edyb)
        dcs = dcs + dcst.T + jnp.where(sub == BLOCK - 1, dcl, 0.0)
        dda = _rev_cumsum_rows(dcs, BLOCK)
        dalog_ref[...] += jnp.sum(dda * dtv, axis=0, keepdims=True) * a
        real = _real_rows(nc - 1 - i, BLOCK)
        _, ds = _silu_and_grad(xc_ref[...])
        dxc_ref[:, :D_SSM] = jnp.where(
            real, (gx[:, :D_SSM] + dsk_ref[...].astype(F32)) * ds[:, :D_SSM], 0.0).astype(dxc_ref.dtype)
        dxc_ref[:, D_SSM:] = jnp.where(real, gx[:, D_SSM:] * ds[:, D_SSM:], 0.0).astype(dxc_ref.dtype)
        dd = jnp.where(real, (ddtx + dda * a) * _sigmoid(dtr_ref[...] + bias_ref[...]), 0.0)
        ddtr_ref[...] = dd.astype(ddtr_ref.dtype)
        dbias_ref[...] += jnp.sum(dd, axis=0, keepdims=True)
        if plan is not None:
            @pl.when(i == nc - 1)
            def _():
                plan.wait(cins, couts, sems)

    rev = lambda i: nc - 1 - i
    res = pl.pallas_call(
        body, name=name, grid=(nc,),
        in_specs=[pl.BlockSpec((BLOCK, D_SSM), lambda i: (rev(i), 0)),
                  pl.BlockSpec((BLOCK, 1024), lambda i: (rev(i), 2)),
                  pl.BlockSpec((BLOCK, 1024), lambda i: (rev(i), 3)),
                  pl.BlockSpec((BLOCK, LANES), lambda i: (rev(i), 0)),
                  pl.BlockSpec((1, LANES), lambda i: (0, 0)),
                  pl.BlockSpec((BLOCK, D_SSM), lambda i: (rev(i), 0)),
                  pl.BlockSpec((1, SSD_GROUPS, SSD_N, gw), lambda i: (rev(i), 0, 0, 0)),
                  pl.BlockSpec((BLOCK, _XBC_W), lambda i: (rev(i), 0)),
                  pl.BlockSpec((BLOCK, LANES), lambda i: (rev(i), _DT_COL)),
                  pl.BlockSpec((1, LANES), lambda i: (0, 0)),
                  pl.BlockSpec((BLOCK, D_SSM), lambda i: (rev(i), 0))] + p_in,
        out_specs=[pl.BlockSpec((BLOCK, _XBC_W), lambda i: (rev(i), 0)),
                   pl.BlockSpec((BLOCK, LANES), lambda i: (rev(i), 0)),
                   pl.BlockSpec((1, LANES), lambda i: (0, 0)),
                   pl.BlockSpec((1, LANES), lambda i: (0, 0))] + p_out,
        out_shape=[jax.ShapeDtypeStruct((lp, _XBC_W), BF16),
                   jax.ShapeDtypeStruct((lp, LANES), BF16),
                   jax.ShapeDtypeStruct((1, LANES), F32),
                   jax.ShapeDtypeStruct((1, LANES), F32)] + p_shapes,
        scratch_shapes=[pltpu.VMEM((SSD_GROUPS, SSD_N, gw), F32),
                        pltpu.VMEM((BLOCK, gw), F32), pltpu.VMEM((BLOCK, gw), F32),
                        pltpu.VMEM((BLOCK, _XBC_W), F32)] + p_scr,
        compiler_params=pltpu.CompilerParams(dimension_semantics=("arbitrary",)),
    )(xbc, xbc, xbc, dt, alog, dy, states, xc, proj, dt_bias, dxskip,
      *(plan.ins if plan is not None else []))
    return res[:4], res[4:]


_GN_GROUPS = 8
_GN_W = D_SSM // _GN_GROUPS


def _ssm_gate_out_postnorm(yssd, xbc, proj, dskip, gnorm, w_out, h, g, next_pre, name):
    lp, d = h.shape
    tm = _pick(lp, (320, 256, 128))

    def body(y_ref, x_ref, z_ref, d_ref, gn_ref, w_ref, h_ref, g_ref, np_ref, yn_ref, o_ref, hn_ref, u_ref):
        sz, _ = _silu_and_grad(z_ref[...])
        y2 = (y_ref[...] + d_ref[...] * x_ref[...]) * sz
        for k in range(_GN_GROUPS):
            sl = slice(k * _GN_W, (k + 1) * _GN_W)
            yk = y2[:, sl]
            rs = lax.rsqrt(jnp.mean(yk * yk, axis=-1, keepdims=True) + EPS)
            yn_ref[:, sl] = (yk * rs * gn_ref[:, sl]).astype(yn_ref.dtype)
        acc = _dot(yn_ref[...], w_ref[...])
        o_ref[...] = acc
        rs = lax.rsqrt(jnp.mean(acc * acc, axis=-1, keepdims=True) + EPS)
        hn = jnp.where(_real_rows(pl.program_id(0), tm), h_ref[...] + acc * rs * g_ref[...], 0.0)
        hn_ref[...] = hn
        rs2 = lax.rsqrt(jnp.mean(hn * hn, axis=-1, keepdims=True) + EPS)
        u_ref[...] = (hn * rs2 * np_ref[...]).astype(u_ref.dtype)

    wide = pl.BlockSpec((tm, D_SSM), lambda i: (i, 0))
    row = pl.BlockSpec((tm, d), lambda i: (i, 0))
    vec = lambda n: pl.BlockSpec((1, n), lambda i: (0, 0))
    return pl.pallas_call(
        body, name=name, grid=(lp // tm,),
        in_specs=[wide, wide, wide, vec(D_SSM), vec(D_SSM), pl.BlockSpec((D_SSM, d), lambda i: (0, 0)),
                  row, vec(d), vec(d)],
        out_specs=[wide, row, row, row],
        out_shape=[jax.ShapeDtypeStruct((lp, D_SSM), BF16), jax.ShapeDtypeStruct((lp, d), F32),
                   jax.ShapeDtypeStruct((lp, d), F32), jax.ShapeDtypeStruct((lp, d), BF16)],
        compiler_params=pltpu.CompilerParams(dimension_semantics=("parallel",),
                                             vmem_limit_bytes=_VMEM_LIMIT_WIDE),
    )(yssd, xbc, proj, dskip, gnorm, w_out, h, g, next_pre)


def _ssm_out_gate_bwd(o, gain, dh, w_out, yssd, xbc, proj, dskip, gnorm, name):
    lp, d = o.shape
    tm = _pick(lp, (320, 256, 128))
    tk = D_SSM // 2

    def body(o_ref, gain_ref, dh_ref, w_ref, y_ref, x_ref, z_ref, d_ref, g_ref,
             dy_ref, dx_ref, dz_ref, dd_ref, dg_ref, do_ref, dgain_ref):
        first = pl.program_id(0) == 0
        dob = _postnorm_bwd_tile(o_ref, gain_ref, dh_ref, do_ref, dgain_ref)
        z = z_ref[...]
        sz, dsz = _silu_and_grad(z)
        xs = x_ref[...]
        y1 = y_ref[...] + d_ref[...] * xs
        y2 = y1 * sz
        for k in range(_GN_GROUPS):
            sl = slice(k * _GN_W, (k + 1) * _GN_W)
            if k % (tk // _GN_W) == 0:
                dyn = _dot_nt(dob, w_ref[k * _GN_W:k * _GN_W + tk, :])
            loc = slice((k % (tk // _GN_W)) * _GN_W, (k % (tk // _GN_W) + 1) * _GN_W)
            dx, dgt = _rms_bwd(y2[:, sl], g_ref[:, sl], dyn[:, loc])
            dy1 = dx * sz[:, sl]
            dy_ref[:, sl] = dy1.astype(dy_ref.dtype)
            dx_ref[:, sl] = (dy1 * d_ref[:, sl]).astype(dx_ref.dtype)
            dz_ref[:, sl] = (dx * y1[:, sl] * dsz[:, sl]).astype(dz_ref.dtype)

            @pl.when(first)
            def _():
                dd_ref[:, sl] = jnp.zeros((1, _GN_W), F32)
                dg_ref[:, sl] = jnp.zeros((1, _GN_W), F32)

            dd_ref[:, sl] += jnp.sum(dy1 * xs[:, sl], axis=0, keepdims=True)
            dg_ref[:, sl] += jnp.sum(dgt, axis=0, keepdims=True)

    tile = pl.BlockSpec((tm, D_SSM), lambda i: (i, 0))
    vec = pl.BlockSpec((1, D_SSM), lambda i: (0, 0))
    row = pl.BlockSpec((tm, d), lambda i: (i, 0))
    rvec = pl.BlockSpec((1, d), lambda i: (0, 0))
    return pl.pallas_call(
        body, name=name, grid=(lp // tm,),
        in_specs=[row, rvec, row, pl.BlockSpec((D_SSM, d), lambda i: (0, 0)), tile, tile, tile, vec, vec],
        out_specs=[tile, tile, tile, vec, vec, row, rvec],
        out_shape=[jax.ShapeDtypeStruct((lp, D_SSM), BF16)] * 3 + [jax.ShapeDtypeStruct((1, D_SSM), F32)] * 2
        + [jax.ShapeDtypeStruct((lp, d), BF16), jax.ShapeDtypeStruct((1, d), F32)],
        compiler_params=pltpu.CompilerParams(dimension_semantics=("arbitrary",),
                                             vmem_limit_bytes=_VMEM_LIMIT_WIDE),
    )(o, gain, dh, w_out, yssd, xbc, proj, dskip, gnorm)


def _shape2d(shape):
    n = math.prod(shape)
    if len(shape) == 2:
        return tuple(shape)
    return (n // LANES, LANES) if n % LANES == 0 else (1, n)


def _adamw_many(ws, gs, ms, vs, name):
    n = len(ws)
    c1 = 1.0 / (1.0 - ADAM_B1 ** ADAM_STEP)
    c2 = 1.0 / (1.0 - ADAM_B2 ** ADAM_STEP)

    def body(*refs):
        for i in range(n):
            w_ref, g_ref, m_ref, v_ref = (refs[j * n + i] for j in range(4))
            d_ref, nm_ref, nv_ref = (refs[(4 + j) * n + i] for j in range(3))
            gv = g_ref[...]
            nm = ADAM_B1 * m_ref[...] + (1.0 - ADAM_B1) * gv
            nv = ADAM_B2 * v_ref[...] + (1.0 - ADAM_B2) * (gv * gv)
            nm_ref[...] = nm
            nv_ref[...] = nv
            d_ref[...] = -ADAM_LR * ((nm * c1) / (jnp.sqrt(nv * c2) + ADAM_EPS) + ADAM_WD * w_ref[...])

    vm = pl.BlockSpec(memory_space=pltpu.VMEM)
    return pl.pallas_call(
        body, name=name, in_specs=[vm] * (4 * n), out_specs=[vm] * (3 * n),
        out_shape=[jax.ShapeDtypeStruct(w.shape, F32) for w in ws] * 3,
    )(*ws, *gs, *ms, *vs)


def _place():
    return lax.axis_index("x"), lax.axis_index("y"), lax.axis_index("c")


def _other_chips(x, y):
    return [(1 - x, y), (x, 1 - y), (1 - x, 1 - y)]


_ANY = pl.BlockSpec(memory_space=pl.ANY)


class _Plan:
    def __init__(self, ins, out_shapes, n_remote, n_local, issue):
        self.ins = list(ins)
        self.out_shapes = list(out_shapes)
        self.issue = issue
        self.scratch = [pltpu.SemaphoreType.DMA((max(n_remote, 1),)),
                        pltpu.SemaphoreType.DMA((max(n_remote, 1),)),
                        pltpu.SemaphoreType.DMA((max(n_local, 1),))]

    def start(self, ins, outs, sems):
        sends, _, locs = self.issue(ins, outs, *sems)
        for cp in locs + sends:
            cp.start()

    def wait(self, ins, outs, sems):
        sends, recvs, locs = self.issue(ins, outs, *sems)
        for make in recvs:
            make().wait_recv()
        for cp in sends:
            cp.wait_send()
        for cp in locs:
            cp.wait()


def _plan_parts(plan):
    if plan is None:
        return [], [], [], []
    return ([_ANY] * len(plan.ins), plan.out_shapes, [_ANY] * len(plan.out_shapes), plan.scratch)


def _run_plan(plan, name):
    n_in, n_out = len(plan.ins), len(plan.out_shapes)

    def body(*refs):
        ins, outs, sems = refs[:n_in], refs[n_in:n_in + n_out], refs[n_in + n_out:]
        plan.start(ins, outs, sems)
        plan.wait(ins, outs, sems)

    return pl.pallas_call(
        body, name=name, in_specs=[_ANY] * n_in, out_specs=[_ANY] * n_out,
        out_shape=plan.out_shapes, scratch_shapes=plan.scratch,
    )(*plan.ins)


def _gather_plan(shards):
    n = len(shards)

    def issue(ins, outs, send_sems, recv_sems, local_sems):
        x, y, c = _place()
        me = 2 * x + y
        sends, recvs, locs = [], [], []
        for p in range(n):
            locs.append(pltpu.make_async_copy(ins[p], outs[p].at[me], local_sems.at[p]))
            for k, (px, py) in enumerate(_other_chips(x, y)):
                sems = dict(send_sem=send_sems.at[3 * p + k], recv_sem=recv_sems.at[3 * p + k],
                            device_id=(px, py, c), device_id_type=MESH)
                sends.append(pltpu.make_async_remote_copy(src_ref=ins[p], dst_ref=outs[p].at[me], **sems))
                recvs.append(functools.partial(pltpu.make_async_remote_copy, src_ref=ins[p],
                                               dst_ref=outs[p].at[2 * px + py], **sems))
        return sends, recvs, locs

    return _Plan(shards, [jax.ShapeDtypeStruct((N_CHIPS,) + s.shape, s.dtype) for s in shards], 3 * n, n, issue)


_REL7 = [(fx, fy, fc) for fx in (0, 1) for fy in (0, 1) for fc in (0, 1)][1:]


def _scatter8_plan(gs):
    n = len(gs)

    def issue(ins, outs, send_sems, recv_sems, local_sems):
        x, y, c = _place()
        sends = []
        for p in range(n):
            hr = gs[p].shape[1] // 2
            for k, (fx, fy, fc) in enumerate(_REL7):
                tx, ty, tc = x ^ fx, y ^ fy, c ^ fc
                src = ins[p].at[2 * tx + ty, pl.ds(pl.multiple_of(tc * hr, SUBLANES), hr), :]
                sends.append(pltpu.make_async_remote_copy(
                    src_ref=src, dst_ref=outs[p].at[k],
                    send_sem=send_sems.at[7 * p + k], recv_sem=recv_sems.at[7 * p + k],
                    device_id=(tx, ty, tc), device_id_type=MESH))
        return sends, [functools.partial(lambda cp: cp, cp) for cp in sends], []

    shapes = [jax.ShapeDtypeStruct((7, g.shape[1] // 2, g.shape[2]), g.dtype) for g in gs]
    return _Plan(gs, shapes, 7 * n, 0, issue)


def _sibling_plan(ts):
    n = len(ts)

    def issue(ins, outs, send_sems, recv_sems, local_sems):
        x, y, c = _place()
        sends = [pltpu.make_async_remote_copy(
            src_ref=ins[p], dst_ref=outs[p], send_sem=send_sems.at[p], recv_sem=recv_sems.at[p],
            device_id=(x, y, 1 - c), device_id_type=MESH) for p in range(n)]
        return sends, [functools.partial(lambda cp: cp, cp) for cp in sends], []

    return _Plan(ts, [jax.ShapeDtypeStruct(t.shape, t.dtype) for t in ts], n, 0, issue)


def _add8(g, recv, chip, core, name):
    s, r, n = g.shape
    hr = r // 2
    th = hr // 2 if (hr // 2) % SUBLANES == 0 else hr
    nt = hr // th

    def body(chip_ref, core_ref, g_ref, r_ref, o_ref):
        acc = g_ref[0].astype(F32)
        for k in range(7):
            acc = acc + r_ref[k].astype(F32)
        o_ref[...] = acc

    return pl.pallas_call(
        body, name=name,
        grid_spec=pltpu.PrefetchScalarGridSpec(
            num_scalar_prefetch=2, grid=(nt,),
            in_specs=[pl.BlockSpec((1, th, n), lambda i, ch, co: (ch[0], co[0] * nt + i, 0)),
                      pl.BlockSpec((7, th, n), lambda i, ch, co: (0, i, 0))],
            out_specs=pl.BlockSpec((th, n), lambda i, ch, co: (i, 0))),
        out_shape=jax.ShapeDtypeStruct((hr, n), F32),
        compiler_params=pltpu.CompilerParams(dimension_semantics=("parallel",)),
    )(chip, core, g, recv)


def _adamw_halves(w, own, other, m, v, core, name):
    r, n = w.shape
    hr = r // 2
    th = hr // 2 if (hr // 2) % SUBLANES == 0 else hr
    tph = hr // th
    c1 = 1.0 / (1.0 - ADAM_B1 ** ADAM_STEP)
    c2 = 1.0 / (1.0 - ADAM_B2 ** ADAM_STEP)

    def body(core_ref, w_ref, a_ref, b_ref, m_ref, v_ref, g_ref, d_ref, nm_ref, nv_ref):
        half = pl.program_id(0) // tph
        gv = jnp.where(half == core_ref[0], a_ref[...], b_ref[...])
        nm = ADAM_B1 * m_ref[...] + (1.0 - ADAM_B1) * gv
        nv = ADAM_B2 * v_ref[...] + (1.0 - ADAM_B2) * (gv * gv)
        g_ref[...] = gv
        nm_ref[...] = nm
        nv_ref[...] = nv
        d_ref[...] = -ADAM_LR * ((nm * c1) / (jnp.sqrt(nv * c2) + ADAM_EPS) + ADAM_WD * w_ref[...])

    full = pl.BlockSpec((th, n), lambda i, co: (i, 0))
    part = pl.BlockSpec((th, n), lambda i, co: (i % tph, 0))
    return pl.pallas_call(
        body, name=name,
        grid_spec=pltpu.PrefetchScalarGridSpec(
            num_scalar_prefetch=1, grid=(2 * tph,),
            in_specs=[full, part, part, full, full], out_specs=[full] * 4),
        out_shape=[jax.ShapeDtypeStruct((r, n), F32)] * 4,
        compiler_params=pltpu.CompilerParams(dimension_semantics=("parallel",)),
    )(core, w, own, other, m, v)


def _allreduce_small(pack, name):
    r, l = pack.shape
    hr = r // 2
    assert hr % SUBLANES == 0

    def body(p_ref, o_ref, sib, chips, send_sems, recv_sems):
        x, y, c = _place()
        chip = 2 * x + y
        sibling = dict(device_id=(x, y, 1 - c), device_id_type=MESH)
        mine = pl.ds(pl.multiple_of(c * hr, SUBLANES), hr)
        other = pl.ds(pl.multiple_of((1 - c) * hr, SUBLANES), hr)
        a = pltpu.make_async_remote_copy(src_ref=p_ref.at[other], dst_ref=sib, send_sem=send_sems.at[0],
                                         recv_sem=recv_sems.at[0], **sibling)
        a.start()
        a.wait()
        own, got = p_ref[mine, :], sib[...]
        chips[chip] = jnp.where(c == 0, own, got) + jnp.where(c == 0, got, own)
        sends = []
        for k, (px, py) in enumerate(_other_chips(x, y)):
            cp = pltpu.make_async_remote_copy(
                src_ref=chips.at[chip], dst_ref=chips.at[chip], send_sem=send_sems.at[1 + k],
                recv_sem=recv_sems.at[1 + k], device_id=(px, py, c), device_id_type=MESH)
            cp.start()
            sends.append(cp)
        for k, (px, py) in enumerate(_other_chips(x, y)):
            pltpu.make_async_remote_copy(
                src_ref=chips.at[chip], dst_ref=chips.at[2 * px + py], send_sem=send_sems.at[1 + k],
                recv_sem=recv_sems.at[1 + k], device_id=(px, py, c), device_id_type=MESH).wait_recv()
        for cp in sends:
            cp.wait_send()
        o_ref[mine, :] = ((chips[0] + chips[1]) + chips[2]) + chips[3]
        fin = pltpu.make_async_remote_copy(src_ref=o_ref.at[mine], dst_ref=o_ref.at[mine],
                                           send_sem=send_sems.at[4], recv_sem=recv_sems.at[4], **sibling)
        fin.start()
        pltpu.make_async_remote_copy(src_ref=o_ref.at[mine], dst_ref=o_ref.at[other],
                                     send_sem=send_sems.at[4], recv_sem=recv_sems.at[4], **sibling).wait_recv()
        fin.wait_send()

    vm = pl.BlockSpec(memory_space=pltpu.VMEM)
    return pl.pallas_call(
        body, name=name, in_specs=[vm], out_specs=vm,
        out_shape=jax.ShapeDtypeStruct((r, l), F32),
        scratch_shapes=[pltpu.VMEM((hr, l), F32), pltpu.VMEM((N_CHIPS, hr, l), F32),
                        pltpu.SemaphoreType.DMA((5,)), pltpu.SemaphoreType.DMA((5,))],
    )(pack)


def _flat_rows(a, mult=SUBLANES * LANES):
    f = a.reshape(-1)
    padn = (-f.shape[0]) % mult
    if padn:
        f = jnp.concatenate([f, jnp.zeros((padn,), f.dtype)])
    return f


def _pack(arrs, mult=SUBLANES * LANES, total_mult=None):
    flat = [_flat_rows(a, mult) for a in arrs]
    sizes = [f.shape[0] for f in flat]
    if total_mult is not None:
        padn = (-sum(sizes)) % total_mult
        if padn:
            flat.append(jnp.zeros((padn,), flat[0].dtype))
    return jnp.concatenate(flat).reshape(-1, LANES), sizes


def _unpack(pack, shapes, sizes, lead=()):
    flat = pack.reshape(lead + (-1,))
    out, off = [], 0
    for shp, sz in zip(shapes, sizes):
        n = math.prod(shp)
        out.append(flat[..., off:off + n].reshape(lead + tuple(shp)))
        off += sz
    return out


def _cols_from_shards(g):
    s, k, n = g.shape
    return jnp.transpose(g, (1, 0, 2)).reshape(k, s * n)


def _ffn_fwd(h, u, post, w_up, cw, cb, w_down, tag, next_pre=None, plan=None, loss_tgt=None):
    (hp, act, hg, hu), pouts = _ffn_up_convact(u, w_up, cw, cb, f"{tag}_up_convact", plan)
    if loss_tgt is not None:
        o, hn, un = _mm_postnorm_loss(act, w_down, h, post, loss_tgt, f"{tag}_down_postnorm_loss")
    else:
        res = _mm_postnorm_res(act, w_down, h, post, f"{tag}_down_postnorm", next_pre)
        o, hn, un = res if next_pre is not None else (*res, None)
    return hn, un, (h, u, hp, hg, hu, act, o), pouts


def _ffn_bwd(dh, saved, pre, post, w_up, cw, w_down, tag):
    h, u, hp, hg, hu, act, o = saved
    dhg, dhu, do, dpost = _ffn_down_act_bwd(o, post, dh, w_down, hg, hu, f"{tag}_down_dx_act_bwd")
    dw_down = _mm_tn(act, do, f"{tag}_down_dw")
    dxg, dwg, dbg = _conv_bwd(hp, 0, D_FF, dhg, cw, f"{tag}_conv_bwd_gate")
    dxu, dwu, dbu = _conv_bwd(hp, D_FF, D_FF, dhu, cw, f"{tag}_conv_bwd_up", w_col_off=D_FF)
    dhp = (dxg, dxu)
    dcw = jnp.concatenate([dwg, dwu], axis=1)
    dcb = jnp.concatenate([dbg, dbu], axis=1)
    (dhn, dpre), _ = _mm_nt_sh(dhp, w_up, (h, pre, dh), f"{tag}_up_dx_prenorm_bwd")
    dw_up = _mm_tn_sh(u, dhp, w_up.shape[2], f"{tag}_up_dw")
    return dhn, dict(pre=dpre, post=dpost, w_up=dw_up, conv_w=dcw[:3], conv_b=dcb, w_down=dw_down)


class _Exchange:
    GATHER_IN_LRU = ("l0_w_out", "l0_ffn_w_down")
    GATHER_IN_ATTN = ("l0_ffn_w_up", "l1_w_out")
    GATHER_IN_FFN0 = ("l1_w_in",)
    GATHER_IN_SSD = ("l1_ffn_w_up", "l1_ffn_w_down")
    AFTER_L1_OUT = ("l1_ffn_w_up", "l1_ffn_w_down", "l1_w_out")
    IN_LRU_BWD = ("l1_w_in",)
    AFTER_L0_OUT = ("l0_ffn_w_up", "l0_ffn_w_down", "l0_w_out")
    LAST = ("l0_w_in",)

    def __init__(self, late_shards):
        self.late = dict(late_shards)
        self.slabs = {}
        self.recv = {}

    def gather_plan(self, names):
        return _gather_plan([self.late[n] for n in names])

    def gathered(self, names, outs):
        return {n: (g if n in _BIG_COL else g.reshape(-1, g.shape[-1])) for n, g in zip(names, outs)}

    def scatter_plan(self, grads, names):
        for n in names:
            g = grads[n]
            self.slabs[n] = g if n in _BIG_COL else g.reshape(N_CHIPS, -1, g.shape[-1])
        return _scatter8_plan([self.slabs[n] for n in names])

    def scattered(self, names, outs):
        self.recv.update(zip(names, outs))


def _local_step(x, tgt, meta, P, ex=None):
    seq, d = x.shape
    lp = seq + BLOCK
    h0 = jnp.concatenate([jnp.zeros((PAD, d), F32), meta, x], axis=0)
    tgt_p = jnp.concatenate([jnp.zeros((BLOCK, d), F32), tgt], axis=0)

    u0 = _rmsnorm_fwd(h0, P["l0_mix_pre_norm"], "l0_mix_prenorm")
    proj0, xrc = _mm_nn_sh(u0, P["l0_w_in"], EVEN_IN, "l0_in_lru_conv",
                           (D_MODEL, D_MODEL, P["l0_lru_conv_w"], P["l0_lru_conv_b"]))
    lru_args = (P["l0_lru_w_a"], P["l0_lru_w_x"], P["l0_lru_b_a"], P["l0_lru_b_x"], P["l0_lru_lambda"])
    if ex:
        (ya, hl), outs = _lru_fwd(proj0, xrc, *lru_args, "l0_lru", ex.gather_plan(ex.GATHER_IN_LRU))
        P = {**P, **ex.gathered(ex.GATHER_IN_LRU, outs)}
    else:
        ya, hl = _lru_fwd(proj0, xrc, *lru_args, "l0_lru")
    yb, outs = _attn_fwd(proj0, P["l0_attn_sinks"], "l0_attn",
                         ex.gather_plan(ex.GATHER_IN_ATTN) if ex else None)
    if ex:
        P = {**P, **ex.gathered(ex.GATHER_IN_ATTN, outs)}
    o0, h1, u1 = _mm_postnorm_res((ya, yb), P["l0_w_out"], h0, P["l0_mix_post_norm"], "l0_out_postnorm",
                                  P["l0_ffn_pre_norm"])
    h2, u2, ffn0, outs = _ffn_fwd(h1, u1, P["l0_ffn_post_norm"], P["l0_ffn_w_up"], P["l0_ffn_conv_w"],
                                  P["l0_ffn_conv_b"], P["l0_ffn_w_down"], "l0_ffn", P["l1_mix_pre_norm"],
                                  ex.gather_plan(ex.GATHER_IN_FFN0) if ex else None)
    if ex:
        P = {**P, **ex.gathered(ex.GATHER_IN_FFN0, outs)}
    proj1 = _mm_nn_sh(u2, P["l1_w_in"], ODD_IN_PAD, "l1_in")
    xc1, xbc, dt = _ssm_convprep_fwd(proj1, P["l1_ssm_conv_w"], P["l1_ssm_conv_b"], P["l1_dt_bias"],
                                     "l1_ssm_convprep")
    (yssd, states), outs = _ssd_fwd(xbc, dt, P["l1_a_log"], "l1_ssd",
                                    ex.gather_plan(ex.GATHER_IN_SSD) if ex else None)
    if ex:
        P = {**P, **ex.gathered(ex.GATHER_IN_SSD, outs)}
    yn, o1, h3, u3 = _ssm_gate_out_postnorm(
        yssd, xbc, proj1, P["l1_d_skip"], P["l1_gate_norm"], P["l1_w_out"], h2, P["l1_mix_post_norm"],
        P["l1_ffn_pre_norm"], "l1_gate_out_postnorm")
    dh4, loss_cols, ffn1, _ = _ffn_fwd(h3, u3, P["l1_ffn_post_norm"], P["l1_ffn_w_up"], P["l1_ffn_conv_w"],
                                       P["l1_ffn_conv_b"], P["l1_ffn_w_down"], "l1_ffn", loss_tgt=tgt_p)

    G = {}
    dh3, g = _ffn_bwd(dh4, ffn1, P["l1_ffn_pre_norm"], P["l1_ffn_post_norm"], P["l1_ffn_w_up"],
                      P["l1_ffn_conv_w"], P["l1_ffn_w_down"], "l1_ffn")
    for k, v in g.items():
        G["l1_ffn_" + (k + "_norm" if k in ("pre", "post") else k)] = v
    dyssd, dxskip, dz, dd_cols, G["l1_gate_norm"], do1, G["l1_mix_post_norm"] = _ssm_out_gate_bwd(
        o1, P["l1_mix_post_norm"], dh3, P["l1_w_out"], yssd, xbc, proj1, P["l1_d_skip"], P["l1_gate_norm"],
        "l1_out_dx_gate_bwd")
    G["l1_w_out"] = _mm_tn(yn, do1, "l1_out_dw")
    G["l1_d_skip"] = dd_cols.reshape(SSD_HEADS, SSD_P).sum(axis=1)
    (dxc, ddtr, dalog, dbias), outs = _ssd_bwd(
        xbc, dt, P["l1_a_log"], states, dyssd, xc1, proj1, P["l1_dt_bias"], dxskip, "l1_ssd_bwd",
        ex.scatter_plan(G, ex.AFTER_L1_OUT) if ex else None)
    if ex:
        ex.scattered(ex.AFTER_L1_OUT, outs)
    G["l1_a_log"] = dalog[0, :SSD_HEADS]
    G["l1_dt_bias"] = dbias[0, :SSD_HEADS]
    dxbc, dcw, dcb = _conv_bwd(proj1, _ZW, _XBC_W, dxc, P["l1_ssm_conv_w"], "l1_ssm_conv_bwd")
    G["l1_ssm_conv_w"] = dcw[:4]
    G["l1_ssm_conv_b"] = dcb
    dproj1 = (dz, dxbc, ddtr)
    (dh2, G["l1_mix_pre_norm"]), _ = _mm_nt_sh(dproj1, P["l1_w_in"], (h2, P["l1_mix_pre_norm"], dh3),
                                               "l1_in_dx_prenorm_bwd")
    G["l1_w_in"] = _mm_tn_sh(u2, dproj1, ODD_IN // N_CHIPS, "l1_in_dw")
    dh1, g = _ffn_bwd(dh2, ffn0, P["l0_ffn_pre_norm"], P["l0_ffn_post_norm"], P["l0_ffn_w_up"],
                      P["l0_ffn_conv_w"], P["l0_ffn_w_down"], "l0_ffn")
    for k, v in g.items():
        G["l0_ffn_" + (k + "_norm" if k in ("pre", "post") else k)] = v
    dmix, do0, G["l0_mix_post_norm"] = _mm_nt_postnorm_bwd(o0, P["l0_mix_post_norm"], dh1, P["l0_w_out"],
                                                           "l0_out_dx")
    G["l0_w_out"] = jnp.concatenate([_mm_tn(ya, do0, "l0_out_dw_lru"), _mm_tn(yb, do0, "l0_out_dw_attn")], axis=0)
    if ex:
        lru_out, outs = _lru_bwd(proj0, xrc, hl, dmix, *lru_args, "l0_lru_bwd",
                                 ex.scatter_plan(G, ex.IN_LRU_BWD))
        ex.scattered(ex.IN_LRU_BWD, outs)
    else:
        lru_out = _lru_bwd(proj0, xrc, hl, dmix, *lru_args, "l0_lru_bwd")
    (dgate, dxrc, G["l0_lru_w_a"], G["l0_lru_w_x"], G["l0_lru_b_a"], G["l0_lru_b_x"],
     G["l0_lru_lambda"]) = lru_out
    dxr, dcw, dcb = _conv_bwd(proj0, D_MODEL, D_MODEL, dxrc, P["l0_lru_conv_w"], "l0_lru_conv_bwd")
    G["l0_lru_conv_w"] = dcw[:4]
    G["l0_lru_conv_b"] = dcb
    (dq, dk, dv, G["l0_attn_sinks"]), outs = _attn_bwd(
        proj0, P["l0_attn_sinks"], dmix, "l0_attn_bwd",
        ex.scatter_plan(G, ex.AFTER_L0_OUT) if ex else None)
    if ex:
        ex.scattered(ex.AFTER_L0_OUT, outs)
    dproj0 = (dgate, dxr, dq, dk, dv)
    G["l0_w_in"] = _mm_tn_sh(u0, dproj0, EVEN_IN // N_CHIPS, "l0_in_dw")
    (dh0, G["l0_mix_pre_norm"]), outs = _mm_nt_sh(
        dproj0, P["l0_w_in"], (h0, P["l0_mix_pre_norm"], dh1), "l0_in_dx_prenorm_bwd",
        ex.scatter_plan(G, ex.LAST) if ex else None)
    if ex:
        ex.scattered(ex.LAST, outs)
    return loss_cols, dh0[BLOCK:], dh0[PAD:BLOCK], G


_BIG_COL = ("l0_w_in", "l0_ffn_w_up", "l1_w_in", "l1_ffn_w_up")
_BIG = ("l0_w_in", "l0_w_out", "l0_ffn_w_up", "l0_ffn_w_down",
        "l1_w_in", "l1_w_out", "l1_ffn_w_up", "l1_ffn_w_down")
_SMALL_SHARDED = ("meta_tokens", "l0_lru_conv_w", "l0_ffn_conv_w", "l1_ssm_conv_w", "l1_ffn_conv_w")
_WEIGHTS = ("meta_tokens", "l0_mix_pre_norm", "l0_mix_post_norm", "l0_w_in", "l0_lru_conv_w",
            "l0_lru_conv_b", "l0_lru_w_a", "l0_lru_b_a", "l0_lru_w_x", "l0_lru_b_x", "l0_lru_lambda",
            "l0_attn_sinks", "l0_w_out", "l0_ffn_pre_norm", "l0_ffn_post_norm", "l0_ffn_w_up",
            "l0_ffn_conv_w", "l0_ffn_conv_b", "l0_ffn_w_down", "l1_mix_pre_norm", "l1_mix_post_norm",
            "l1_w_in", "l1_ssm_conv_w", "l1_ssm_conv_b", "l1_dt_bias", "l1_a_log", "l1_d_skip",
            "l1_gate_norm", "l1_w_out", "l1_ffn_pre_norm", "l1_ffn_post_norm", "l1_ffn_w_up",
            "l1_ffn_conv_w", "l1_ffn_conv_b", "l1_ffn_w_down")
_REPL = tuple(n for n in _WEIGHTS if n not in _BIG and n not in _SMALL_SHARDED)


def _pad_lanes(v, n=LANES):
    return jnp.concatenate([v, jnp.zeros((n - v.shape[0],), v.dtype)]).reshape(1, n)


def _step(x, tgt, W, M, V):
    cx, cy, cc = _place()
    chip = 2 * cx + cy

    small_pack, small_sizes = _pack([W[n] for n in _SMALL_SHARDED])
    first = _run_plan(_gather_plan([W["l0_w_in"].astype(BF16), small_pack]), "gather_first")
    small_full = _unpack(first[1], [W[n].shape for n in _SMALL_SHARDED], small_sizes, lead=(N_CHIPS,))
    ex = _Exchange({n: W[n].astype(BF16) for n in _BIG if n != "l0_w_in"})

    P = {"l0_w_in": first[0]}
    for n, g in zip(_SMALL_SHARDED, small_full):
        P[n] = _cols_from_shards(g)
    for n in _REPL:
        v = W[n]
        P[n] = v.reshape(1, -1) if v.ndim == 1 else v
    P["l0_lru_w_a"] = W["l0_lru_w_a"].astype(BF16)
    P["l0_lru_w_x"] = W["l0_lru_w_x"].astype(BF16)
    P["l1_dt_bias"] = _pad_lanes(W["l1_dt_bias"])
    P["l1_a_log"] = _pad_lanes(W["l1_a_log"])
    P["l1_d_skip"] = jnp.repeat(W["l1_d_skip"], SSD_P).reshape(1, D_SSM)
    meta = P.pop("meta_tokens")

    loss_cols, grad_x, grad_meta, G = _local_step(x, tgt, meta, P, ex)
    G["meta_tokens"] = grad_meta

    core_idx = cc.astype(jnp.int32).reshape(1)
    chip_idx = chip.astype(jnp.int32).reshape(1)
    own_half = [_add8(ex.slabs[n], ex.recv[n], chip_idx, core_idx, f"grad_sum_{n}") for n in _BIG]
    other_half = _run_plan(_sibling_plan(own_half), "grad_sibling_swap")
    small_names = list(_REPL) + list(_SMALL_SHARDED)
    small_list = [G[n] for n in small_names] + [loss_cols]
    spack, ssizes = _pack(small_list, total_mult=2 * SUBLANES * LANES)
    sred = _allreduce_small(spack, "small_allreduce")
    sfull = _unpack(sred, [a.shape for a in small_list], ssizes)
    loss = 0.5 / D_MODEL * jnp.sum(sfull[-1])
    small_grads = {}
    for n, g in zip(small_names, sfull[:-1]):
        if n in _SMALL_SHARDED:
            wcols = W[n].shape[1]
            g = lax.dynamic_slice_in_dim(g, chip * wcols, wcols, axis=1)
        small_grads[n] = g.reshape(W[n].shape)

    grads, delta, new_m, new_v = {}, {}, {}, {}
    for n, own, other in zip(_BIG, own_half, other_half):
        grads[n], delta[n], new_m[n], new_v[n] = _adamw_halves(
            W[n], own, other, M[n], V[n], core_idx, f"adamw_{n}")
    s_names = [n for n in _WEIGHTS if n not in _BIG]
    as2d = lambda a: a.reshape(_shape2d(a.shape))
    outs = _adamw_many([as2d(W[n]) for n in s_names], [as2d(small_grads[n]) for n in s_names],
                       [as2d(M[n]) for n in s_names], [as2d(V[n]) for n in s_names], "adamw_small")
    k = len(s_names)
    for i, n in enumerate(s_names):
        grads[n] = small_grads[n]
        delta[n], new_m[n], new_v[n] = (outs[j * k + i].reshape(W[n].shape) for j in range(3))
    return loss, grad_x, grads, delta, new_m, new_v


def kernel(x, meta_tokens, l0_mix_pre_norm, l0_mix_post_norm, l0_w_in, l0_lru_conv_w, l0_lru_conv_b, l0_lru_w_a, l0_lru_b_a, l0_lru_w_x, l0_lru_b_x, l0_lru_lambda, l0_attn_sinks, l0_w_out, l0_ffn_pre_norm, l0_ffn_post_norm, l0_ffn_w_up, l0_ffn_conv_w, l0_ffn_conv_b, l0_ffn_w_down, l1_mix_pre_norm, l1_mix_post_norm, l1_w_in, l1_ssm_conv_w, l1_ssm_conv_b, l1_dt_bias, l1_a_log, l1_d_skip, l1_gate_norm, l1_w_out, l1_ffn_pre_norm, l1_ffn_post_norm, l1_ffn_w_up, l1_ffn_conv_w, l1_ffn_conv_b, l1_ffn_w_down, loss_target, m_meta_tokens, m_l0_mix_pre_norm, m_l0_mix_post_norm, m_l0_w_in, m_l0_lru_conv_w, m_l0_lru_conv_b, m_l0_lru_w_a, m_l0_lru_b_a, m_l0_lru_w_x, m_l0_lru_b_x, m_l0_lru_lambda, m_l0_attn_sinks, m_l0_w_out, m_l0_ffn_pre_norm, m_l0_ffn_post_norm, m_l0_ffn_w_up, m_l0_ffn_conv_w, m_l0_ffn_conv_b, m_l0_ffn_w_down, m_l1_mix_pre_norm, m_l1_mix_post_norm, m_l1_w_in, m_l1_ssm_conv_w, m_l1_ssm_conv_b, m_l1_dt_bias, m_l1_a_log, m_l1_d_skip, m_l1_gate_norm, m_l1_w_out, m_l1_ffn_pre_norm, m_l1_ffn_post_norm, m_l1_ffn_w_up, m_l1_ffn_conv_w, m_l1_ffn_conv_b, m_l1_ffn_w_down, v_meta_tokens, v_l0_mix_pre_norm, v_l0_mix_post_norm, v_l0_w_in, v_l0_lru_conv_w, v_l0_lru_conv_b, v_l0_lru_w_a, v_l0_lru_b_a, v_l0_lru_w_x, v_l0_lru_b_x, v_l0_lru_lambda, v_l0_attn_sinks, v_l0_w_out, v_l0_ffn_pre_norm, v_l0_ffn_post_norm, v_l0_ffn_w_up, v_l0_ffn_conv_w, v_l0_ffn_conv_b, v_l0_ffn_w_down, v_l1_mix_pre_norm, v_l1_mix_post_norm, v_l1_w_in, v_l1_ssm_conv_w, v_l1_ssm_conv_b, v_l1_dt_bias, v_l1_a_log, v_l1_d_skip, v_l1_gate_norm, v_l1_w_out, v_l1_ffn_pre_norm, v_l1_ffn_post_norm, v_l1_ffn_w_up, v_l1_ffn_conv_w, v_l1_ffn_conv_b, v_l1_ffn_w_down):
    args = locals()
    W = {n: args[n] for n in _WEIGHTS}
    M = {n: args["m_" + n] for n in _WEIGHTS}
    V = {n: args["v_" + n] for n in _WEIGHTS}
    loss, grad_x, grads, delta, new_m, new_v = _step(x[0], loss_target[0], W, M, V)
    return (loss, grad_x[None], *[grads[n] for n in _WEIGHTS], *[delta[n] for n in _WEIGHTS],
            *[new_m[n] for n in _WEIGHTS], *[new_v[n] for n in _WEIGHTS])
```
